```python
import math
import jax
import jax.numpy as jnp
from jax import lax
import numpy as np

D_MODEL = 1024
BATCH = 8
SEQ = 2048
DEPTH = 4

N_MIXERS = 3
D_PLE = 256
NORM_EPS = 1e-6
Q_BLOCK = 128
NEG_INF = -1e30

MLA_HEADS = 16
MLA_Q_RANK = 384
MLA_KV_RANK = 256
MLA_NOPE = 64
MLA_ROPE = 32
MLA_V = 64
ROPE_THETA = 10000.0

DIL_PATTERNS = ((128, 1), (512, 4), (2048, 16))
DIL_GROUPS = len(DIL_PATTERNS)
DIL_HEADS = 16
DIL_HEAD_DIM = 64

REL_BUCKETS = 32
REL_MAX_DIST = 2048

FOX_HEADS = 16
FOX_HEAD_DIM = 64

D_FF = -(-8 * D_MODEL // (3 * 256)) * 256

kernel_name = 'hybrid_mla_dilated_fox_trunk'


def rms_norm(x, g):
    xf = x.astype(jnp.float32)
    y = xf * lax.rsqrt(jnp.mean(xf * xf, axis=-1, keepdims=True) + NORM_EPS)
    return (y * g.astype(jnp.float32)).astype(x.dtype)


def apply_rope(x, positions):
    half = x.shape[-1] // 2
    inv = ROPE_THETA ** (-jnp.arange(half, dtype=jnp.float32) / half)
    ang = positions.astype(jnp.float32)[:, :, None, None] * inv
    cos, sin = jnp.cos(ang), jnp.sin(ang)
    xf = x.astype(jnp.float32)
    x1, x2 = xf[..., :half], xf[..., half:]
    return jnp.concatenate([x1 * cos - x2 * sin, x2 * cos + x1 * sin], axis=-1).astype(x.dtype)


def causal_block_attention(q, k, v, scale, log_forget_cumsum=None):
    s_len = q.shape[1]
    outs = []
    for b0 in range(0, s_len, Q_BLOCK):
        b1 = b0 + Q_BLOCK
        qb, kb, vb = q[:, b0:b1], k[:, :b1], v[:, :b1]
        logits = jnp.einsum('bqhd,bkhd->bhqk', qb, kb).astype(jnp.float32) * scale
        if log_forget_cumsum is not None:
            c_q = jnp.transpose(log_forget_cumsum[:, b0:b1], (0, 2, 1))
            c_k = jnp.transpose(log_forget_cumsum[:, :b1], (0, 2, 1))
            logits = logits + (c_q[..., :, None] - c_k[..., None, :])
        q_idx = b0 + jnp.arange(Q_BLOCK)
        k_idx = jnp.arange(b1)
        causal = k_idx[None, :] <= q_idx[:, None]
        logits = jnp.where(causal, logits, NEG_INF)
        probs = jax.nn.softmax(logits, axis=-1).astype(v.dtype)
        outs.append(jnp.einsum('bhqk,bkhd->bqhd', probs, vb))
    return jnp.concatenate(outs, axis=1)


def t5_bucket(dist):
    max_exact = REL_BUCKETS // 2
    n = jnp.maximum(dist.astype(jnp.float32), 1.0)
    large = max_exact + (jnp.log(n / max_exact) / math.log(REL_MAX_DIST / max_exact)
                         * (REL_BUCKETS - max_exact)).astype(jnp.int32)
    large = jnp.minimum(large, REL_BUCKETS - 1)
    return jnp.where(dist < max_exact, dist, large)


def dilated_branch(q, k, v, dilation, span, bias_table):
    b, s_len, h, dh = q.shape
    sub_len = s_len // dilation
    n_blk = -(-sub_len // Q_BLOCK)
    pad_len = n_blk * Q_BLOCK

    def to_blocks(t):
        t = t.reshape(b, sub_len, dilation, h, dh).transpose(0, 2, 1, 3, 4)
        t = t.reshape(b * dilation, sub_len, h, dh)
        t = jnp.pad(t, ((0, 0), (0, pad_len - sub_len), (0, 0), (0, 0)))
        return t.reshape(b * dilation, n_blk, Q_BLOCK, h, dh)

    def with_previous(t):
        prev = jnp.concatenate([jnp.zeros_like(t[:, :1]), t[:, :-1]], axis=1)
        return jnp.concatenate([prev, t], axis=2)

    qs = to_blocks(q)
    kb = with_previous(to_blocks(k))
    vb = with_previous(to_blocks(v))

    logits = jnp.einsum('bnqhd,bnkhd->bnhqk', qs, kb).astype(jnp.float32) * (dh ** -0.5)
    i = jnp.arange(Q_BLOCK)
    j = jnp.arange(2 * Q_BLOCK)
    rel = Q_BLOCK + i[:, None] - j[None, :]
    bucket = t5_bucket(jnp.clip(rel, 0) * dilation)
    bias = jnp.transpose(bias_table[bucket].astype(jnp.float32), (2, 0, 1))
    key_pos = jnp.arange(n_blk)[:, None] * Q_BLOCK - Q_BLOCK + j[None, :]
    valid = ((rel >= 0) & (rel <= span))[None] & (key_pos >= 0)[:, None, :]
    logits = jnp.where(valid[None, :, None], logits + bias[None, None], NEG_INF)
    lse = jax.nn.logsumexp(logits, axis=-1, keepdims=True)
    probs = jnp.exp(logits - lse).astype(v.dtype)
    o = jnp.einsum('bnhqk,bnkhd->bnqhd', probs, vb)

    o = o.reshape(b * dilation, pad_len, h, dh)[:, :sub_len]
    o = o.reshape(b, dilation, sub_len, h, dh).transpose(0, 2, 1, 3, 4).reshape(b, s_len, h, dh)
    lse = jnp.transpose(lse[..., 0], (0, 1, 3, 2)).reshape(b * dilation, pad_len, h)[:, :sub_len]
    lse = lse.reshape(b, dilation, sub_len, h).transpose(0, 2, 1, 3).reshape(b, s_len, h)
    return o, lse


def mla_mixer(h, positions, w_a, q_norm, kv_norm, w_uq, w_ukv, w_o):
    b, s_len, _ = h.shape
    a = h @ w_a
    c_q = rms_norm(a[..., :MLA_Q_RANK], q_norm)
    c_kv = rms_norm(a[..., MLA_Q_RANK:MLA_Q_RANK + MLA_KV_RANK], kv_norm)
    k_rot = apply_rope(a[..., MLA_Q_RANK + MLA_KV_RANK:][:, :, None, :], positions)
    q = (c_q @ w_uq).reshape(b, s_len, MLA_HEADS, MLA_NOPE + MLA_ROPE)
    q = jnp.concatenate([q[..., :MLA_NOPE], apply_rope(q[..., MLA_NOPE:], positions)], axis=-1)
    kv = (c_kv @ w_ukv).reshape(b, s_len, MLA_HEADS, MLA_NOPE + MLA_V)
    k = jnp.concatenate([kv[..., :MLA_NOPE],
                         jnp.broadcast_to(k_rot, (b, s_len, MLA_HEADS, MLA_ROPE))], axis=-1)
    v = kv[..., MLA_NOPE:]
    o = causal_block_attention(q, k, v, (MLA_NOPE + MLA_ROPE) ** -0.5)
    return o.reshape(b, s_len, MLA_HEADS * MLA_V) @ w_o


def dilated_mixer(h, w_qkv, w_o, rel_bias):
    b, s_len, _ = h.shape
    qkv = (h @ w_qkv).reshape(b, s_len, DIL_GROUPS, 3, DIL_HEADS, DIL_HEAD_DIM)
    table = rel_bias.reshape(REL_BUCKETS, DIL_GROUPS, DIL_HEADS)
    outs, lses = [], []
    for g, (window, dilation) in enumerate(DIL_PATTERNS):
        o, lse = dilated_branch(qkv[:, :, g, 0], qkv[:, :, g, 1], qkv[:, :, g, 2],
                                dilation, window // dilation, table[:, g])
        outs.append(o)
        lses.append(lse)
    alpha = jax.nn.softmax(jnp.stack(lses, axis=0), axis=0)
    o = jnp.sum(alpha[..., None] * jnp.stack(outs, axis=0).astype(jnp.float32), axis=0).astype(h.dtype)
    return o.reshape(b, s_len, DIL_HEADS * DIL_HEAD_DIM) @ w_o


def fox_mixer(h, w_qkvf, b_f, w_o):
    b, s_len, _ = h.shape
    inner = FOX_HEADS * FOX_HEAD_DIM
    a = h @ w_qkvf
    qkv = a[..., :3 * inner].reshape(b, s_len, 3, FOX_HEADS, FOX_HEAD_DIM)
    log_f = jax.nn.log_sigmoid((a[..., 3 * inner:] + b_f).astype(jnp.float32))
    cum = jnp.cumsum(log_f, axis=1)
    o = causal_block_attention(qkv[:, :, 0], qkv[:, :, 1], qkv[:, :, 2], FOX_HEAD_DIM ** -0.5, cum)
    return o.reshape(b, s_len, inner) @ w_o


def swiglu(h, w_in, w_out):
    gu = h @ w_in
    return (jax.nn.silu(gu[..., :D_FF]) * gu[..., D_FF:]) @ w_out


def _fwd_setup_inputs(seed: int = 0) -> dict:
    key = jax.random.key(seed)
    ks = jax.random.split(key, 20)
    f32 = jnp.float32
    n_a, n_b, n_c = (len(range(m, DEPTH, N_MIXERS)) for m in range(N_MIXERS))

    def dense(k, shape):
        return jax.random.normal(k, shape, f32) * shape[-2] ** -0.5

    def gain(k, shape):
        return 1.0 + 0.1 * jax.random.normal(k, shape, f32)

    x = jax.random.normal(ks[0], (BATCH, SEQ, D_MODEL), f32)
    p = jax.random.normal(ks[1], (DEPTH, BATCH, SEQ, D_PLE), f32)
    offsets = jax.random.randint(ks[2], (BATCH, 1), 0, 4096, jnp.int32)
    positions = (offsets + jnp.arange(SEQ, dtype=jnp.int32)[None, :]).astype(jnp.int32)
    norm_g = gain(ks[3], (DEPTH, 4, D_MODEL))
    ffn_w_in = dense(ks[4], (DEPTH, D_MODEL, 2 * D_FF))
    ffn_w_out = dense(ks[5], (DEPTH, D_FF, D_MODEL))
    ple_w_proj = dense(ks[6], (DEPTH, D_PLE, D_MODEL))
    ple_w_gate = dense(ks[7], (DEPTH, D_MODEL, D_MODEL))
    rel_bias = 0.5 * jax.random.normal(ks[8], (REL_BUCKETS, DIL_GROUPS * DIL_HEADS), f32)
    mla_w_a = dense(ks[9], (n_a, D_MODEL, MLA_Q_RANK + MLA_KV_RANK + MLA_ROPE))
    mla_q_norm = gain(ks[10], (n_a, MLA_Q_RANK))
    mla_kv_norm = gain(ks[11], (n_a, MLA_KV_RANK))
    mla_w_uq = dense(ks[12], (n_a, MLA_Q_RANK, MLA_HEADS * (MLA_NOPE + MLA_ROPE)))
    mla_w_ukv = dense(ks[13], (n_a, MLA_KV_RANK, MLA_HEADS * (MLA_NOPE + MLA_V)))
    mla_w_o = dense(ks[14], (n_a, MLA_HEADS * MLA_V, D_MODEL))
    dil_w_qkv = dense(ks[15], (n_b, D_MODEL, DIL_GROUPS * 3 * DIL_HEADS * DIL_HEAD_DIM))
    dil_w_o = dense(ks[16], (n_b, DIL_HEADS * DIL_HEAD_DIM, D_MODEL))
    fox_w_qkvf = dense(ks[17], (n_c, D_MODEL, 3 * FOX_HEADS * FOX_HEAD_DIM + FOX_HEADS))
    fox_b_f = jax.random.uniform(ks[18], (n_c, FOX_HEADS), f32, 1.0, 5.0)
    fox_w_o = dense(ks[19], (n_c, FOX_HEADS * FOX_HEAD_DIM, D_MODEL))
    return {'x': x, 'p': p, 'positions': positions, 'norm_g': norm_g,
            'ffn_w_in': ffn_w_in, 'ffn_w_out': ffn_w_out,
            'ple_w_proj': ple_w_proj, 'ple_w_gate': ple_w_gate, 'rel_bias': rel_bias,
            'mla_w_a': mla_w_a, 'mla_q_norm': mla_q_norm, 'mla_kv_norm': mla_kv_norm,
            'mla_w_uq': mla_w_uq, 'mla_w_ukv': mla_w_ukv, 'mla_w_o': mla_w_o,
            'dil_w_qkv': dil_w_qkv, 'dil_w_o': dil_w_o,
            'fox_w_qkvf': fox_w_qkvf, 'fox_b_f': fox_b_f, 'fox_w_o': fox_w_o}


def _fwd_reference(x, p, positions, norm_g, ffn_w_in, ffn_w_out, ple_w_proj, ple_w_gate, rel_bias,
              mla_w_a, mla_q_norm, mla_kv_norm, mla_w_uq, mla_w_ukv, mla_w_o,
              dil_w_qkv, dil_w_o, fox_w_qkvf, fox_b_f, fox_w_o):
    h = x
    for i in range(DEPTH):
        mixer, j = i % N_MIXERS, i // N_MIXERS
        g = norm_g[i]
        hn = rms_norm(h, g[0])
        if mixer == 0:
            y = mla_mixer(hn, positions, mla_w_a[j], mla_q_norm[j], mla_kv_norm[j],
                          mla_w_uq[j], mla_w_ukv[j], mla_w_o[j])
        elif mixer == 1:
            y = dilated_mixer(hn, dil_w_qkv[j], dil_w_o[j], rel_bias)
        else:
            y = fox_mixer(hn, fox_w_qkvf[j], fox_b_f[j], fox_w_o[j])
        h = h + rms_norm(y, g[1])
        h = h + rms_norm(swiglu(rms_norm(h, g[2]), ffn_w_in[i], ffn_w_out[i]), g[3])
        h = h + (p[i] @ ple_w_proj[i]) * jax.nn.sigmoid(h @ ple_w_gate[i])
    return h


import jax as _jax
import jax.numpy as _jnp

TWIN_FORMAT = 'train_step'
FWD_PARAMS = ['x', 'p', 'positions', 'norm_g', 'ffn_w_in', 'ffn_w_out', 'ple_w_proj', 'ple_w_gate', 'rel_bias', 'mla_w_a', 'mla_q_norm', 'mla_kv_norm', 'mla_w_uq', 'mla_w_ukv', 'mla_w_o', 'dil_w_qkv', 'dil_w_o', 'fox_w_qkvf', 'fox_b_f', 'fox_w_o']
TWIN_WEIGHTS = ['norm_g', 'ffn_w_in', 'ffn_w_out', 'ple_w_proj', 'ple_w_gate', 'rel_bias', 'mla_w_a', 'mla_q_norm', 'mla_kv_norm', 'mla_w_uq', 'mla_w_ukv', 'mla_w_o', 'dil_w_qkv', 'dil_w_o', 'fox_w_qkvf', 'fox_b_f', 'fox_w_o']
TWIN_DIFF_INPUT = 'x'
TWIN_INPUTS = ['x', 'p', 'positions', 'norm_g', 'ffn_w_in', 'ffn_w_out', 'ple_w_proj', 'ple_w_gate', 'rel_bias', 'mla_w_a', 'mla_q_norm', 'mla_kv_norm', 'mla_w_uq', 'mla_w_ukv', 'mla_w_o', 'dil_w_qkv', 'dil_w_o', 'fox_w_qkvf', 'fox_b_f', 'fox_w_o', 'loss_target', 'm_norm_g', 'm_ffn_w_in', 'm_ffn_w_out', 'm_ple_w_proj', 'm_ple_w_gate', 'm_rel_bias', 'm_mla_w_a', 'm_mla_q_norm', 'm_mla_kv_norm', 'm_mla_w_uq', 'm_mla_w_ukv', 'm_mla_w_o', 'm_dil_w_qkv', 'm_dil_w_o', 'm_fox_w_qkvf', 'm_fox_b_f', 'm_fox_w_o', 'v_norm_g', 'v_ffn_w_in', 'v_ffn_w_out', 'v_ple_w_proj', 'v_ple_w_gate', 'v_rel_bias', 'v_mla_w_a', 'v_mla_q_norm', 'v_mla_kv_norm', 'v_mla_w_uq', 'v_mla_w_ukv', 'v_mla_w_o', 'v_dil_w_qkv', 'v_dil_w_o', 'v_fox_w_qkvf', 'v_fox_b_f', 'v_fox_w_o']
TWIN_OUTPUTS = ['loss', 'grad_x', 'grad_norm_g', 'grad_ffn_w_in', 'grad_ffn_w_out', 'grad_ple_w_proj', 'grad_ple_w_gate', 'grad_rel_bias', 'grad_mla_w_a', 'grad_mla_q_norm', 'grad_mla_kv_norm', 'grad_mla_w_uq', 'grad_mla_w_ukv', 'grad_mla_w_o', 'grad_dil_w_qkv', 'grad_dil_w_o', 'grad_fox_w_qkvf', 'grad_fox_b_f', 'grad_fox_w_o', 'delta_norm_g', 'delta_ffn_w_in', 'delta_ffn_w_out', 'delta_ple_w_proj', 'delta_ple_w_gate', 'delta_rel_bias', 'delta_mla_w_a', 'delta_mla_q_norm', 'delta_mla_kv_norm', 'delta_mla_w_uq', 'delta_mla_w_ukv', 'delta_mla_w_o', 'delta_dil_w_qkv', 'delta_dil_w_o', 'delta_fox_w_qkvf', 'delta_fox_b_f', 'delta_fox_w_o', 'new_m_norm_g', 'new_m_ffn_w_in', 'new_m_ffn_w_out', 'new_m_ple_w_proj', 'new_m_ple_w_gate', 'new_m_rel_bias', 'new_m_mla_w_a', 'new_m_mla_q_norm', 'new_m_mla_kv_norm', 'new_m_mla_w_uq', 'new_m_mla_w_ukv', 'new_m_mla_w_o', 'new_m_dil_w_qkv', 'new_m_dil_w_o', 'new_m_fox_w_qkvf', 'new_m_fox_b_f', 'new_m_fox_w_o', 'new_v_norm_g', 'new_v_ffn_w_in', 'new_v_ffn_w_out', 'new_v_ple_w_proj', 'new_v_ple_w_gate', 'new_v_rel_bias', 'new_v_mla_w_a', 'new_v_mla_q_norm', 'new_v_mla_kv_norm', 'new_v_mla_w_uq', 'new_v_mla_w_ukv', 'new_v_mla_w_o', 'new_v_dil_w_qkv', 'new_v_dil_w_o', 'new_v_fox_w_qkvf', 'new_v_fox_b_f', 'new_v_fox_w_o']
TWIN_LEAF_KINDS = {'loss': 'loss', 'grad_x': 'grad_x', 'grad_norm_g': 'grad_w', 'grad_ffn_w_in': 'grad_w', 'grad_ffn_w_out': 'grad_w', 'grad_ple_w_proj': 'grad_w', 'grad_ple_w_gate': 'grad_w', 'grad_rel_bias': 'grad_w', 'grad_mla_w_a': 'grad_w', 'grad_mla_q_norm': 'grad_w', 'grad_mla_kv_norm': 'grad_w', 'grad_mla_w_uq': 'grad_w', 'grad_mla_w_ukv': 'grad_w', 'grad_mla_w_o': 'grad_w', 'grad_dil_w_qkv': 'grad_w', 'grad_dil_w_o': 'grad_w', 'grad_fox_w_qkvf': 'grad_w', 'grad_fox_b_f': 'grad_w', 'grad_fox_w_o': 'grad_w', 'delta_norm_g': 'delta_w', 'delta_ffn_w_in': 'delta_w', 'delta_ffn_w_out': 'delta_w', 'delta_ple_w_proj': 'delta_w', 'delta_ple_w_gate': 'delta_w', 'delta_rel_bias': 'delta_w', 'delta_mla_w_a': 'delta_w', 'delta_mla_q_norm': 'delta_w', 'delta_mla_kv_norm': 'delta_w', 'delta_mla_w_uq': 'delta_w', 'delta_mla_w_ukv': 'delta_w', 'delta_mla_w_o': 'delta_w', 'delta_dil_w_qkv': 'delta_w', 'delta_dil_w_o': 'delta_w', 'delta_fox_w_qkvf': 'delta_w', 'delta_fox_b_f': 'delta_w', 'delta_fox_w_o': 'delta_w', 'new_m_norm_g': 'new_m', 'new_m_ffn_w_in': 'new_m', 'new_m_ffn_w_out': 'new_m', 'new_m_ple_w_proj': 'new_m', 'new_m_ple_w_gate': 'new_m', 'new_m_rel_bias': 'new_m', 'new_m_mla_w_a': 'new_m', 'new_m_mla_q_norm': 'new_m', 'new_m_mla_kv_norm': 'new_m', 'new_m_mla_w_uq': 'new_m', 'new_m_mla_w_ukv': 'new_m', 'new_m_mla_w_o': 'new_m', 'new_m_dil_w_qkv': 'new_m', 'new_m_dil_w_o': 'new_m', 'new_m_fox_w_qkvf': 'new_m', 'new_m_fox_b_f': 'new_m', 'new_m_fox_w_o': 'new_m', 'new_v_norm_g': 'new_v', 'new_v_ffn_w_in': 'new_v', 'new_v_ffn_w_out': 'new_v', 'new_v_ple_w_proj': 'new_v', 'new_v_ple_w_gate': 'new_v', 'new_v_rel_bias': 'new_v', 'new_v_mla_w_a': 'new_v', 'new_v_mla_q_norm': 'new_v', 'new_v_mla_kv_norm': 'new_v', 'new_v_mla_w_uq': 'new_v', 'new_v_mla_w_ukv': 'new_v', 'new_v_mla_w_o': 'new_v', 'new_v_dil_w_qkv': 'new_v', 'new_v_dil_w_o': 'new_v', 'new_v_fox_w_qkvf': 'new_v', 'new_v_fox_b_f': 'new_v', 'new_v_fox_w_o': 'new_v'}


def _forward(args):
    return _fwd_reference(*[args[k] for k in FWD_PARAMS])


def _output_shape():
    out = _jax.eval_shape(lambda: _forward(_fwd_setup_inputs(0)))
    return out.shape, out.dtype

N_MICROBATCH = 1
ADAM_LR = 0.001
ADAM_B1 = 0.9
ADAM_B2 = 0.999
ADAM_EPS = 1e-08
ADAM_WD = 0.01
ADAM_STEP = 10
PER_EXAMPLE_BATCH_AXIS = {'x': 0, 'p': 1, 'positions': 0, 'loss_target': 0}
SHARED_INPUTS = []
_WEIGHT_DTYPES = {'norm_g': _jnp.float32, 'ffn_w_in': _jnp.float32, 'ffn_w_out': _jnp.float32, 'ple_w_proj': _jnp.float32, 'ple_w_gate': _jnp.float32, 'rel_bias': _jnp.float32, 'mla_w_a': _jnp.float32, 'mla_q_norm': _jnp.float32, 'mla_kv_norm': _jnp.float32, 'mla_w_uq': _jnp.float32, 'mla_w_ukv': _jnp.float32, 'mla_w_o': _jnp.float32, 'dil_w_qkv': _jnp.float32, 'dil_w_o': _jnp.float32, 'fox_w_qkvf': _jnp.float32, 'fox_b_f': _jnp.float32, 'fox_w_o': _jnp.float32}
MOMENT_SCALE = {'norm_g': 1.426140e+01, 'ffn_w_in': 1.671527e+00, 'ffn_w_out': 3.172821e+00, 'ple_w_proj': 5.419986e-01, 'ple_w_gate': 4.832944e-01, 'rel_bias': 2.518852e+00, 'mla_w_a': 1.404080e+01, 'mla_q_norm': 2.542374e+00, 'mla_kv_norm': 1.952134e+01, 'mla_w_uq': 1.250644e+00, 'mla_w_ukv': 7.521164e+00, 'mla_w_o': 1.089552e+01, 'dil_w_qkv': 4.058192e+00, 'dil_w_o': 1.198480e+01, 'fox_w_qkvf': 5.167976e+00, 'fox_b_f': 5.700702e+00, 'fox_w_o': 9.028444e+00}


def _to_microbatches(a, axis):
    t = _jnp.moveaxis(a, axis, 0)
    t = t.reshape((N_MICROBATCH, t.shape[0] // N_MICROBATCH) + t.shape[1:])
    return _jnp.moveaxis(t, 1, axis + 1)


def setup_inputs(seed: int = 0) -> dict:
    inp = _fwd_setup_inputs(seed)
    key = _jax.random.fold_in(_jax.random.key(seed), 7919)
    shape, _ = _output_shape()
    out = dict(inp)
    out["loss_target"] = _jax.random.normal(_jax.random.fold_in(key, 0), shape, _jnp.float32)
    for i, name in enumerate(TWIN_WEIGHTS):
        w = inp[name].astype(_jnp.float32)
        if MOMENT_SCALE is None:
            s = _jnp.sqrt(_jnp.mean(_jnp.square(w)) + 1e-30)
        else:
            s = MOMENT_SCALE[name]
        km, kv = _jax.random.split(_jax.random.fold_in(key, i + 1))
        out[name] = w
        out["m_" + name] = s * _jax.random.normal(km, w.shape, _jnp.float32)
        out["v_" + name] = (s * s) * _jax.random.uniform(kv, w.shape, _jnp.float32, 0.5, 1.5)
    if N_MICROBATCH > 1:
        for name, axis in PER_EXAMPLE_BATCH_AXIS.items():
            out[name] = _to_microbatches(out[name], axis)
    return {'x': out['x'], 'p': out['p'], 'positions': out['positions'], 'norm_g': out['norm_g'], 'ffn_w_in': out['ffn_w_in'], 'ffn_w_out': out['ffn_w_out'], 'ple_w_proj': out['ple_w_proj'], 'ple_w_gate': out['ple_w_gate'], 'rel_bias': out['rel_bias'], 'mla_w_a': out['mla_w_a'], 'mla_q_norm': out['mla_q_norm'], 'mla_kv_norm': out['mla_kv_norm'], 'mla_w_uq': out['mla_w_uq'], 'mla_w_ukv': out['mla_w_ukv'], 'mla_w_o': out['mla_w_o'], 'dil_w_qkv': out['dil_w_qkv'], 'dil_w_o': out['dil_w_o'], 'fox_w_qkvf': out['fox_w_qkvf'], 'fox_b_f': out['fox_b_f'], 'fox_w_o': out['fox_w_o'], 'loss_target': out['loss_target'], 'm_norm_g': out['m_norm_g'], 'm_ffn_w_in': out['m_ffn_w_in'], 'm_ffn_w_out': out['m_ffn_w_out'], 'm_ple_w_proj': out['m_ple_w_proj'], 'm_ple_w_gate': out['m_ple_w_gate'], 'm_rel_bias': out['m_rel_bias'], 'm_mla_w_a': out['m_mla_w_a'], 'm_mla_q_norm': out['m_mla_q_norm'], 'm_mla_kv_norm': out['m_mla_kv_norm'], 'm_mla_w_uq': out['m_mla_w_uq'], 'm_mla_w_ukv': out['m_mla_w_ukv'], 'm_mla_w_o': out['m_mla_w_o'], 'm_dil_w_qkv': out['m_dil_w_qkv'], 'm_dil_w_o': out['m_dil_w_o'], 'm_fox_w_qkvf': out['m_fox_w_qkvf'], 'm_fox_b_f': out['m_fox_b_f'], 'm_fox_w_o': out['m_fox_w_o'], 'v_norm_g': out['v_norm_g'], 'v_ffn_w_in': out['v_ffn_w_in'], 'v_ffn_w_out': out['v_ffn_w_out'], 'v_ple_w_proj': out['v_ple_w_proj'], 'v_ple_w_gate': out['v_ple_w_gate'], 'v_rel_bias': out['v_rel_bias'], 'v_mla_w_a': out['v_mla_w_a'], 'v_mla_q_norm': out['v_mla_q_norm'], 'v_mla_kv_norm': out['v_mla_kv_norm'], 'v_mla_w_uq': out['v_mla_w_uq'], 'v_mla_w_ukv': out['v_mla_w_ukv'], 'v_mla_w_o': out['v_mla_w_o'], 'v_dil_w_qkv': out['v_dil_w_qkv'], 'v_dil_w_o': out['v_dil_w_o'], 'v_fox_w_qkvf': out['v_fox_w_qkvf'], 'v_fox_b_f': out['v_fox_b_f'], 'v_fox_w_o': out['v_fox_w_o']}


def _loss(weights, diff, rest, loss_target):
    with _jax.named_scope("forward"):
        args = {**rest, TWIN_DIFF_INPUT: diff, **{k: w.astype(_WEIGHT_DTYPES[k]) for k, w in weights.items()}}
        y = _forward(args)
    with _jax.named_scope("loss_head"):
        err = _jnp.square(y.astype(_jnp.float32) - loss_target)
        return 0.5 * _jnp.sum(_jnp.mean(err, axis=-1)) if err.ndim else 0.5 * err


def _adamw(w, g, m, v):
    m = ADAM_B1 * m + (1.0 - ADAM_B1) * g
    v = ADAM_B2 * v + (1.0 - ADAM_B2) * _jnp.square(g)
    m_hat = m / (1.0 - ADAM_B1 ** ADAM_STEP)
    v_hat = v / (1.0 - ADAM_B2 ** ADAM_STEP)
    delta = -ADAM_LR * (m_hat / (_jnp.sqrt(v_hat) + ADAM_EPS) + ADAM_WD * w)
    return delta, m, v


def reference(x, p, positions, norm_g, ffn_w_in, ffn_w_out, ple_w_proj, ple_w_gate, rel_bias, mla_w_a, mla_q_norm, mla_kv_norm, mla_w_uq, mla_w_ukv, mla_w_o, dil_w_qkv, dil_w_o, fox_w_qkvf, fox_b_f, fox_w_o, loss_target, m_norm_g, m_ffn_w_in, m_ffn_w_out, m_ple_w_proj, m_ple_w_gate, m_rel_bias, m_mla_w_a, m_mla_q_norm, m_mla_kv_norm, m_mla_w_uq, m_mla_w_ukv, m_mla_w_o, m_dil_w_qkv, m_dil_w_o, m_fox_w_qkvf, m_fox_b_f, m_fox_w_o, v_norm_g, v_ffn_w_in, v_ffn_w_out, v_ple_w_proj, v_ple_w_gate, v_rel_bias, v_mla_w_a, v_mla_q_norm, v_mla_kv_norm, v_mla_w_uq, v_mla_w_ukv, v_mla_w_o, v_dil_w_qkv, v_dil_w_o, v_fox_w_qkvf, v_fox_b_f, v_fox_w_o):
    given = dict(x=x, p=p, positions=positions, norm_g=norm_g, ffn_w_in=ffn_w_in, ffn_w_out=ffn_w_out, ple_w_proj=ple_w_proj, ple_w_gate=ple_w_gate, rel_bias=rel_bias, mla_w_a=mla_w_a, mla_q_norm=mla_q_norm, mla_kv_norm=mla_kv_norm, mla_w_uq=mla_w_uq, mla_w_ukv=mla_w_ukv, mla_w_o=mla_w_o, dil_w_qkv=dil_w_qkv, dil_w_o=dil_w_o, fox_w_qkvf=fox_w_qkvf, fox_b_f=fox_b_f, fox_w_o=fox_w_o, loss_target=loss_target, m_norm_g=m_norm_g, m_ffn_w_in=m_ffn_w_in, m_ffn_w_out=m_ffn_w_out, m_ple_w_proj=m_ple_w_proj, m_ple_w_gate=m_ple_w_gate, m_rel_bias=m_rel_bias, m_mla_w_a=m_mla_w_a, m_mla_q_norm=m_mla_q_norm, m_mla_kv_norm=m_mla_kv_norm, m_mla_w_uq=m_mla_w_uq, m_mla_w_ukv=m_mla_w_ukv, m_mla_w_o=m_mla_w_o, m_dil_w_qkv=m_dil_w_qkv, m_dil_w_o=m_dil_w_o, m_fox_w_qkvf=m_fox_w_qkvf, m_fox_b_f=m_fox_b_f, m_fox_w_o=m_fox_w_o, v_norm_g=v_norm_g, v_ffn_w_in=v_ffn_w_in, v_ffn_w_out=v_ffn_w_out, v_ple_w_proj=v_ple_w_proj, v_ple_w_gate=v_ple_w_gate, v_rel_bias=v_rel_bias, v_mla_w_a=v_mla_w_a, v_mla_q_norm=v_mla_q_norm, v_mla_kv_norm=v_mla_kv_norm, v_mla_w_uq=v_mla_w_uq, v_mla_w_ukv=v_mla_w_ukv, v_mla_w_o=v_mla_w_o, v_dil_w_qkv=v_dil_w_qkv, v_dil_w_o=v_dil_w_o, v_fox_w_qkvf=v_fox_w_qkvf, v_fox_b_f=v_fox_b_f, v_fox_w_o=v_fox_w_o)
    weights = {n: given[n] for n in TWIN_WEIGHTS}
    shared = {n: given[n] for n in SHARED_INPUTS}
    per_example = {n: given[n] for n in ['x', 'p', 'positions']}
    grad_fn = _jax.value_and_grad(_loss, argnums=(0, 1))

    def one_microbatch(ex, loss_target):
        ex = dict(ex)
        diff = ex.pop(TWIN_DIFF_INPUT)
        return grad_fn(weights, diff, {**shared, **ex}, loss_target)

    if N_MICROBATCH == 1:
        loss, (grad_w, grad_x) = one_microbatch(per_example, given["loss_target"])
    else:
        def body(carry, xs):
            loss_sum, grad_sum = carry
            l_k, (gw_k, gx_k) = one_microbatch(xs[0], xs[1])
            with _jax.named_scope("update"):
                return (loss_sum + l_k, _jax.tree.map(_jnp.add, grad_sum, gw_k)), gx_k

        init = (_jnp.zeros((), _jnp.float32), _jax.tree.map(_jnp.zeros_like, weights))
        (loss, grad_w), grad_x = _jax.lax.scan(body, init, (per_example, given["loss_target"]))
    with _jax.named_scope("update"):
        delta_w, new_m, new_v = {}, {}, {}
        for n in TWIN_WEIGHTS:
            delta_w[n], new_m[n], new_v[n] = _adamw(weights[n], grad_w[n], given["m_" + n], given["v_" + n])
    return (loss, grad_x, *[grad_w[n] for n in TWIN_WEIGHTS], *[delta_w[n] for n in TWIN_WEIGHTS],
            *[new_m[n] for n in TWIN_WEIGHTS], *[new_v[n] for n in TWIN_WEIGHTS])
```

```python
import functools
import math

import jax
import jax.numpy as jnp
from jax import lax
from jax.experimental import pallas as pl
from jax.experimental.pallas import tpu as pltpu

F32 = jnp.float32
BF = jnp.bfloat16
MESH = pl.DeviceIdType.MESH
HBM_SPEC = pl.BlockSpec(memory_space=pltpu.HBM)

D_MODEL = 1024
DEPTH = 4
N_MIXERS = 3
D_FF = 2816
NORM_EPS = 1e-6
NEG_INF = -1e30
LANE = 128
HEADS = 16
HEAD_DIM = 64
MLA_Q_RANK = 384
MLA_KV_RANK = 256
MLA_ROPE = 32
MLA_A_PAD = 768
ROPE_THETA = 10000.0
DIL_PATTERNS = ((128, 1), (512, 4), (2048, 16))
Q_BLOCK = 128
REL_BUCKETS = 32
REL_MAX_DIST = 2048
N_CHIPS = 4
N_DEV = 8

ADAM_LR = 0.001
ADAM_B1 = 0.9
ADAM_B2 = 0.999
ADAM_EPS = 1e-08
ADAM_WD = 0.01
ADAM_STEP = 10

VMEM_LIMIT = 56 * 1024 * 1024
MATMUL_VMEM_BUDGET = 36 * 1024 * 1024
ROW_TILE = 256
ATTN_TILE = 256


def _params(sem=None):
    return pltpu.CompilerParams(dimension_semantics=sem, vmem_limit_bytes=VMEM_LIMIT)


def _divisor_tiles(dim):
    tiles = [t for t in range(LANE, dim + 1, LANE) if dim % t == 0]
    return tiles or [dim]


def _matmul_tiles(m, n, k, a_bytes, b_bytes, out_bytes, has_add):
    best = None
    for tm in _divisor_tiles(m):
        for tn in _divisor_tiles(n):
            for tk in _divisor_tiles(k):
                if max(tm, tn, tk) > 2048:
                    continue
                vmem = 2 * (tm * tk * a_bytes + tk * tn * b_bytes + tm * tn * out_bytes) + tm * tn * 4
                if has_add:
                    vmem += 2 * tm * tn * 4
                if vmem > MATMUL_VMEM_BUDGET:
                    continue
                steps = (m // tm) * (n // tn) * (k // tk)
                traffic = m * k * a_bytes * (n // tn) + k * n * b_bytes * (m // tm) + m * n * out_bytes
                cost = traffic / 3.0e12 + steps * 0.4e-6
                if best is None or cost < best[0]:
                    best = (cost, tm, tn, tk)
    return best[1:]


def _matmul(a, b, *, ta=False, tb=False, add=None, out_dtype=F32, name):
    k, m = a.shape if ta else a.shape[::-1]
    kb, n = b.shape[::-1] if tb else b.shape
    assert k == kb, (a.shape, b.shape, ta, tb)
    tm, tn, tk = _matmul_tiles(m, n, k, a.dtype.itemsize, b.dtype.itemsize, jnp.dtype(out_dtype).itemsize,
                               add is not None)
    nk = k // tk
    dims = (((0 if ta else 1,), (1 if tb else 0,)), ((), ()))

    def body(*refs):
        if add is None:
            a_ref, b_ref, o_ref, acc_ref = refs
            add_ref = None
        else:
            a_ref, b_ref, add_ref, o_ref, acc_ref = refs
        kk = pl.program_id(2)

        @pl.when(kk == 0)
        def _():
            acc_ref[...] = jnp.zeros_like(acc_ref)

        acc_ref[...] += lax.dot_general(a_ref[...].astype(BF), b_ref[...].astype(BF), dims,
                                        preferred_element_type=F32)

        @pl.when(kk == nk - 1)
        def _():
            r = acc_ref[...]
            if add_ref is not None:
                r = r + add_ref[...].astype(F32)
            o_ref[...] = r.astype(out_dtype)

    a_spec = pl.BlockSpec((tk, tm), lambda i, j, q: (q, i)) if ta else pl.BlockSpec((tm, tk), lambda i, j, q: (i, q))
    b_spec = pl.BlockSpec((tn, tk), lambda i, j, q: (j, q)) if tb else pl.BlockSpec((tk, tn), lambda i, j, q: (q, j))
    o_spec = pl.BlockSpec((tm, tn), lambda i, j, q: (i, j))
    in_specs = [a_spec, b_spec]
    args = [a, b]
    if add is not None:
        in_specs.append(o_spec)
        args.append(add)
    return pl.pallas_call(
        body, out_shape=jax.ShapeDtypeStruct((m, n), out_dtype), grid=(m // tm, n // tn, nk),
        in_specs=in_specs, out_specs=o_spec, scratch_shapes=[pltpu.VMEM((tm, tn), F32)], name=name,
        compiler_params=_params(("parallel", "parallel", "arbitrary")))(*args)


def _rowwise(body, name, rows, ins, outs, tr=ROW_TILE):
    def row_spec(cols):
        return pl.BlockSpec((tr, cols), lambda i: (i, 0))

    def full_spec(shape):
        zeros = (0,) * len(shape)
        return pl.BlockSpec(shape, lambda i: zeros)

    in_specs = [row_spec(a.shape[1]) if kind == "row" else full_spec(a.shape) for a, kind in ins]
    out_specs = [row_spec(shape[1]) if kind == "row" else full_spec(shape) for shape, _, kind in outs]
    out_shape = [jax.ShapeDtypeStruct(shape, dtype) for shape, dtype, _ in outs]
    return pl.pallas_call(body, out_shape=out_shape, grid=(rows // tr,), in_specs=in_specs, out_specs=out_specs,
                          name=name, compiler_params=_params(("arbitrary",)))(*[a for a, _ in ins])


def _rstd(x):
    return lax.rsqrt(jnp.mean(x * x, axis=-1, keepdims=True) + NORM_EPS)


def _rms_bwd_math(x, g, dy):
    r = _rstd(x)
    gd = dy * g
    dx = r * gd - x * (r * r * r) * jnp.mean(gd * x, axis=-1, keepdims=True)
    dg = jnp.sum(dy * x * r, axis=0, keepdims=True)
    return dx, dg


def _sigmoid(x):
    return 1.0 / (1.0 + jnp.exp(-x))


def _init_acc(*refs):
    @pl.when(pl.program_id(0) == 0)
    def _():
        for r in refs:
            r[...] = jnp.zeros_like(r)


def _prenorm(h, g):
    rows, cols = h.shape

    def body(h_ref, g_ref, o_ref):
        x = h_ref[...]
        o_ref[...] = (x * _rstd(x) * g_ref[...]).astype(BF)

    return _rowwise(body, "prenorm", rows, [(h, "row"), (g, "full")], [((rows, cols), BF, "row")])[0]


def _post_residual(h, y, g_post, g_pre):
    rows, cols = h.shape
    with_pre = g_pre is not None

    def body(*refs):
        if with_pre:
            h_ref, y_ref, gp_ref, gq_ref, hn_ref, hb_ref = refs
        else:
            h_ref, y_ref, gp_ref, hn_ref, hb_ref = refs
        yv = y_ref[...]
        hn = h_ref[...] + yv * _rstd(yv) * gp_ref[...]
        hn_ref[...] = hn
        hb_ref[...] = (hn * _rstd(hn) * gq_ref[...] if with_pre else hn).astype(BF)

    ins = [(h, "row"), (y, "row"), (g_post, "full")] + ([(g_pre, "full")] if with_pre else [])
    return _rowwise(body, "post_residual_pre" if with_pre else "post_residual", rows, ins,
                    [((rows, cols), F32, "row"), ((rows, cols), BF, "row")])


def _ple_forward(h2, pp, z, g_pre):
    rows, cols = h2.shape

    def body(h_ref, p_ref, z_ref, g_ref, h3_ref, hb_ref):
        h3 = h_ref[...] + p_ref[...] * _sigmoid(z_ref[...])
        h3_ref[...] = h3
        hb_ref[...] = (h3 * _rstd(h3) * g_ref[...]).astype(BF)

    return _rowwise(body, "ple_forward", rows, [(h2, "row"), (pp, "row"), (z, "row"), (g_pre, "full")],
                    [((rows, cols), F32, "row"), ((rows, cols), BF, "row")])


def _ple_loss(h2, pp, z, target):
    rows, cols = h2.shape

    def body(h_ref, p_ref, z_ref, t_ref, dh_ref, sq_ref):
        _init_acc(sq_ref)
        err = h_ref[...] + p_ref[...] * _sigmoid(z_ref[...]) - t_ref[...]
        dh_ref[...] = err * (1.0 / cols)
        sq_ref[...] += jnp.sum(err * err, axis=0, keepdims=True)

    return _rowwise(body, "ple_loss", rows, [(h2, "row"), (pp, "row"), (z, "row"), (target, "row")],
                    [((rows, cols), F32, "row"), ((1, cols), F32, "acc")])


def _ple_backward(dh3, pp, z):
    rows, cols = dh3.shape

    def body(d_ref, p_ref, z_ref, dpp_ref, dz_ref):
        d = d_ref[...]
        s = _sigmoid(z_ref[...])
        dpp_ref[...] = (d * s).astype(BF)
        dz_ref[...] = (d * p_ref[...] * s * (1.0 - s)).astype(BF)

    return _rowwise(body, "ple_backward", rows, [(dh3, "row"), (pp, "row"), (z, "row")],
                    [((rows, cols), BF, "row"), ((rows, cols), BF, "row")])


def _rms_backward(x, g, dy, add, out_dtype):
    rows, cols = x.shape
    with_add = add is not None

    def body(*refs):
        if with_add:
            x_ref, g_ref, dy_ref, add_ref, dx_ref, dg_ref = refs
        else:
            x_ref, g_ref, dy_ref, dx_ref, dg_ref = refs
        _init_acc(dg_ref)
        dx, dg = _rms_bwd_math(x_ref[...], g_ref[...], dy_ref[...].astype(F32))
        if with_add:
            dx = dx + add_ref[...]
        dx_ref[...] = dx.astype(out_dtype)
        dg_ref[...] += dg

    ins = [(x, "row"), (g, "full"), (dy, "row")] + ([(add, "row")] if with_add else [])
    return _rowwise(body, "rms_backward_add" if with_add else "rms_backward", rows, ins,
                    [((rows, cols), out_dtype, "row"), ((1, cols), F32, "acc")])


def _swiglu_forward(gu):
    rows = gu.shape[0]
    tc = D_FF // 2

    def body(g_ref, u_ref, o_ref):
        g = g_ref[...].astype(F32)
        o_ref[...] = (g * _sigmoid(g) * u_ref[...].astype(F32)).astype(BF)

    return pl.pallas_call(
        body, out_shape=jax.ShapeDtypeStruct((rows, D_FF), BF), grid=(rows // ROW_TILE, 2),
        in_specs=[pl.BlockSpec((ROW_TILE, tc), lambda i, j: (i, j)), pl.BlockSpec((ROW_TILE, tc), lambda i, j: (i, j + 2))],
        out_specs=pl.BlockSpec((ROW_TILE, tc), lambda i, j: (i, j)), name="swiglu_forward",
        compiler_params=_params(("parallel", "parallel")))(gu, gu)


def _swiglu_backward(gu, dact):
    rows = gu.shape[0]
    tc = D_FF // 2

    def body(g_ref, u_ref, d_ref, o_ref):
        g = g_ref[...].astype(F32)
        u = u_ref[...].astype(F32)
        d = d_ref[...].astype(F32)
        s = _sigmoid(g)

        @pl.when(pl.program_id(1) < 2)
        def _():
            o_ref[...] = (d * u * s * (1.0 + g * (1.0 - s))).astype(BF)

        @pl.when(pl.program_id(1) >= 2)
        def _():
            o_ref[...] = (d * g * s).astype(BF)

    return pl.pallas_call(
        body, out_shape=jax.ShapeDtypeStruct((rows, 2 * D_FF), BF), grid=(rows // ROW_TILE, 4),
        in_specs=[pl.BlockSpec((ROW_TILE, tc), lambda i, j: (i, j % 2)),
                  pl.BlockSpec((ROW_TILE, tc), lambda i, j: (i, j % 2 + 2)),
                  pl.BlockSpec((ROW_TILE, tc), lambda i, j: (i, j % 2))],
        out_specs=pl.BlockSpec((ROW_TILE, tc), lambda i, j: (i, j)), name="swiglu_backward",
        compiler_params=_params(("parallel", "parallel")))(gu, gu, dact)


def _rope_tables(positions):
    half = MLA_ROPE // 2
    inv = ROPE_THETA ** (-jnp.arange(half, dtype=F32) / half)
    ang = positions.astype(F32)[:, None] * inv
    cos, sin = jnp.cos(ang), jnp.sin(ang)
    rows = positions.shape[0]
    c = jnp.ones((rows, LANE), F32).at[:, 64:80].set(cos).at[:, 80:96].set(cos)
    sa = jnp.zeros((rows, LANE), F32).at[:, 64:80].set(-sin)
    sb = jnp.zeros((rows, LANE), F32).at[:, 80:96].set(sin)
    return c, sa, sb


def _rope_apply(x, c, sa, sb):
    return x * c + pltpu.roll(x, LANE - 16, 1) * sa + pltpu.roll(x, 16, 1) * sb


def _rope_apply_t(dy, c, sa, sb):
    return dy * c + pltpu.roll(dy * sa, 16, 1) + pltpu.roll(dy * sb, LANE - 16, 1)


def _rope_heads(x, tables, transpose, name):
    rows, cols = x.shape

    def body(x_ref, c_ref, sa_ref, sb_ref, o_ref):
        fn = _rope_apply_t if transpose else _rope_apply
        o_ref[...] = fn(x_ref[...].astype(F32), c_ref[...], sa_ref[...], sb_ref[...]).astype(BF)

    blk = pl.BlockSpec((ROW_TILE, LANE), lambda i, j: (i, j))
    tbl = pl.BlockSpec((ROW_TILE, LANE), lambda i, j: (i, 0))
    return pl.pallas_call(body, out_shape=jax.ShapeDtypeStruct((rows, cols), BF), grid=(rows // ROW_TILE, cols // LANE),
                          in_specs=[blk, tbl, tbl, tbl], out_specs=blk, name=name,
                          compiler_params=_params(("parallel", "parallel")))(x, *tables)


def _mla_mid_forward(a, q_norm, kv_norm, tables):
    rows = a.shape[0]
    qr, kvr = MLA_Q_RANK, MLA_KV_RANK

    def body(a_ref, qn_ref, kn_ref, c_ref, sa_ref, sb_ref, cq_ref, ckv_ref, kr_ref):
        aq = a_ref[:, 0:qr]
        akv = a_ref[:, qr:qr + kvr]
        cq_ref[...] = (aq * _rstd(aq) * qn_ref[...]).astype(BF)
        ckv_ref[...] = (akv * _rstd(akv) * kn_ref[...]).astype(BF)
        kr_ref[...] = _rope_apply(a_ref[:, qr + kvr:], c_ref[...], sa_ref[...], sb_ref[...]).astype(BF)

    ins = [(a, "row"), (q_norm, "full"), (kv_norm, "full")] + [(t, "row") for t in tables]
    return _rowwise(body, "mla_mid_forward", rows, ins,
                    [((rows, qr), BF, "row"), ((rows, kvr), BF, "row"), ((rows, LANE), BF, "row")])


def _mla_mid_backward(a, q_norm, kv_norm, tables, dcq, dckv, dkr):
    rows = a.shape[0]
    qr, kvr = MLA_Q_RANK, MLA_KV_RANK

    def body(a_ref, qn_ref, kn_ref, c_ref, sa_ref, sb_ref, dcq_ref, dckv_ref, dkr_ref, da_ref, dqn_ref, dkn_ref):
        _init_acc(dqn_ref, dkn_ref)
        dxq, dgq = _rms_bwd_math(a_ref[:, 0:qr], qn_ref[...], dcq_ref[...])
        dxk, dgk = _rms_bwd_math(a_ref[:, qr:qr + kvr], kn_ref[...], dckv_ref[...])
        da_ref[:, 0:qr] = dxq.astype(BF)
        da_ref[:, qr:qr + kvr] = dxk.astype(BF)
        da_ref[:, qr + kvr:] = _rope_apply_t(dkr_ref[...], c_ref[...], sa_ref[...], sb_ref[...]).astype(BF)
        dqn_ref[...] += dgq
        dkn_ref[...] += dgk

    ins = ([(a, "row"), (q_norm, "full"), (kv_norm, "full")] + [(t, "row") for t in tables]
           + [(dcq, "row"), (dckv, "row"), (dkr, "row")])
    return _rowwise(body, "mla_mid_backward", rows, ins,
                    [((rows, MLA_A_PAD), BF, "row"), ((1, qr), F32, "acc"), ((1, kvr), F32, "acc")])


def _attn_specs(rows, kv_off):
    head = pl.BlockSpec((rows, LANE), lambda h: (0, h))
    kv_head = pl.BlockSpec((rows, LANE), lambda h: (0, h + kv_off))
    shared = pl.BlockSpec((rows, LANE), lambda h: (0, 0))
    col_vec = pl.BlockSpec((1, rows, 1), lambda h: (h, 0, 0))
    row_vec = pl.BlockSpec((1, 1, rows), lambda h: (h, 0, 0))
    return head, kv_head, shared, col_vec, row_vec


def _attn_forward(q, kv, kv_off, kr, cum_col, cum_row, scale, name):
    rows = q.shape[0]
    heads = HEADS
    t = ATTN_TILE
    nb = rows // t
    has_kr = kr is not None
    has_f = cum_col is not None

    def body(*refs):
        it = iter(refs)
        q_ref, kv_ref = next(it), next(it)
        kr_ref = next(it) if has_kr else None
        cc_ref = next(it) if has_f else None
        cr_ref = next(it) if has_f else None
        o_ref, lse_ref = next(it), next(it)
        lo = lax.broadcasted_iota(jnp.int32, (1, LANE), 1) < HEAD_DIM
        causal = (lax.broadcasted_iota(jnp.int32, (t, t), 1) <= lax.broadcasted_iota(jnp.int32, (t, t), 0))

        def q_block(i, _):
            qs = pl.ds(pl.multiple_of(i * t, t), t)
            qb = q_ref[qs, :]
            cq = cc_ref[0, qs, :] if has_f else None

            def step(j, carry, diag):
                m, l, acc = carry
                ks = pl.ds(pl.multiple_of(j * t, t), t)
                kvb = kv_ref[ks, :]
                kk = jnp.where(lo, kvb, kr_ref[ks, :] if has_kr else jnp.zeros_like(kvb))
                s = lax.dot_general(qb, kk, (((1,), (1,)), ((), ())), preferred_element_type=F32) * scale
                if has_f:
                    s = s + (cq - cr_ref[0, :, ks])
                if diag:
                    s = jnp.where(causal, s, NEG_INF)
                mn = jnp.maximum(m, jnp.max(s, axis=1, keepdims=True))
                alpha = jnp.exp(m - mn)
                p = jnp.exp(s - mn)
                l = alpha * l + jnp.sum(p, axis=1, keepdims=True)
                acc = alpha * acc + jnp.dot(p.astype(BF), kvb, preferred_element_type=F32)
                return mn, l, acc

            init = (jnp.full((t, 1), NEG_INF, F32), jnp.zeros((t, 1), F32), jnp.zeros((t, LANE), F32))
            carry = lax.fori_loop(0, i, lambda j, c: step(j, c, False), init)
            m, l, acc = step(i, carry, True)
            o_ref[qs, :] = jnp.where(lo, 0.0, acc / l).astype(BF)
            lse_ref[0, qs, :] = m + jnp.log(l)
            return 0

        lax.fori_loop(0, nb, q_block, 0)

    head, kv_head, shared, col_vec, row_vec = _attn_specs(rows, kv_off)
    in_specs, args = [head, kv_head], [q, kv]
    if has_kr:
        in_specs.append(shared)
        args.append(kr)
    if has_f:
        in_specs += [col_vec, row_vec]
        args += [cum_col, cum_row]
    return pl.pallas_call(
        body, out_shape=[jax.ShapeDtypeStruct((rows, heads * LANE), BF), jax.ShapeDtypeStruct((heads, rows, 1), F32)],
        grid=(heads,), in_specs=in_specs, out_specs=[head, col_vec], name=name,
        compiler_params=_params(("arbitrary",)))(*args)


def _attn_backward(q, kv, kv_off, kr, cum_col, cum_row, o, do, lse, scale, name):
    rows = q.shape[0]
    heads = HEADS
    t = ATTN_TILE
    nb = rows // t
    has_kr = kr is not None
    has_f = cum_col is not None

    def body(*refs):
        it = iter(refs)
        q_ref, kv_ref = next(it), next(it)
        kr_ref = next(it) if has_kr else None
        cc_ref = next(it) if has_f else None
        cr_ref = next(it) if has_f else None
        o_ref, do_ref, lse_ref = next(it), next(it), next(it)
        dq_ref, dkv_ref = next(it), next(it)
        dkr_ref = next(it) if has_kr else None
        dck_ref = next(it) if has_f else None
        dcq_ref = next(it) if has_f else None
        dq_acc = next(it)
        lo = lax.broadcasted_iota(jnp.int32, (1, LANE), 1) < HEAD_DIM
        causal = (lax.broadcasted_iota(jnp.int32, (t, t), 1) <= lax.broadcasted_iota(jnp.int32, (t, t), 0))

        dq_acc[...] = jnp.zeros_like(dq_acc)
        if has_kr:
            _init_acc(dkr_ref)
        if has_f:
            dcq_ref[...] = jnp.zeros_like(dcq_ref)

        def kv_block(j, _):
            ks = pl.ds(pl.multiple_of(j * t, t), t)
            kvb = kv_ref[ks, :]
            kk = jnp.where(lo, kvb, kr_ref[ks, :] if has_kr else jnp.zeros_like(kvb))
            ck = cr_ref[0, :, ks] if has_f else None

            def pair(i, carry, diag):
                dkk, dvv, dcs = carry
                qs = pl.ds(pl.multiple_of(i * t, t), t)
                qb = q_ref[qs, :]
                dob = do_ref[qs, :]
                s = lax.dot_general(qb, kk, (((1,), (1,)), ((), ())), preferred_element_type=F32) * scale
                if has_f:
                    s = s + (cc_ref[0, qs, :] - ck)
                if diag:
                    s = jnp.where(causal, s, NEG_INF)
                p = jnp.exp(s - lse_ref[0, qs, :])
                dp = lax.dot_general(dob, kvb, (((1,), (1,)), ((), ())), preferred_element_type=F32)
                delta = jnp.sum(dob.astype(F32) * o_ref[qs, :].astype(F32), axis=1, keepdims=True)
                ds = p * (dp - delta)
                dsb = ds.astype(BF)
                dvv = dvv + lax.dot_general(p.astype(BF), dob, (((0,), (0,)), ((), ())), preferred_element_type=F32)
                dkk = dkk + lax.dot_general(dsb, qb, (((0,), (0,)), ((), ())), preferred_element_type=F32)
                dq_acc[qs, :] += jnp.dot(dsb, kk, preferred_element_type=F32)
                if has_f:
                    dcs = dcs + jnp.sum(ds, axis=0, keepdims=True)
                    dcq_ref[0, qs, :] += jnp.sum(ds, axis=1, keepdims=True)
                return dkk, dvv, dcs

            init = (jnp.zeros((t, LANE), F32), jnp.zeros((t, LANE), F32), jnp.zeros((1, t), F32))
            carry = pair(j, init, True)
            dkk, dvv, dcs = lax.fori_loop(j + 1, nb, lambda i, c: pair(i, c, False), carry)
            dkk = dkk * scale
            dkv_ref[ks, :] = jnp.where(lo, dkk, dvv).astype(BF)
            if has_kr:
                dkr_ref[ks, :] += jnp.where(lo, 0.0, dkk)
            if has_f:
                dck_ref[0, :, ks] = -dcs
            return 0

        lax.fori_loop(0, nb, kv_block, 0)
        dq_ref[...] = (dq_acc[...] * scale).astype(BF)

    head, kv_head, shared, col_vec, row_vec = _attn_specs(rows, kv_off)
    in_specs, args = [head, kv_head], [q, kv]
    if has_kr:
        in_specs.append(shared)
        args.append(kr)
    if has_f:
        in_specs += [col_vec, row_vec]
        args += [cum_col, cum_row]
    in_specs += [head, head, col_vec]
    args += [o, do, lse]
    out_shape = [jax.ShapeDtypeStruct((rows, heads * LANE), BF), jax.ShapeDtypeStruct((rows, heads * LANE), BF)]
    out_specs = [head, head]
    if has_kr:
        out_shape.append(jax.ShapeDtypeStruct((rows, LANE), F32))
        out_specs.append(shared)
    if has_f:
        out_shape += [jax.ShapeDtypeStruct((heads, 1, rows), F32), jax.ShapeDtypeStruct((heads, rows, 1), F32)]
        out_specs += [row_vec, col_vec]
    return pl.pallas_call(
        body, out_shape=out_shape, grid=(heads,), in_specs=in_specs, out_specs=out_specs,
        scratch_shapes=[pltpu.VMEM((rows, LANE), F32)], name=name, compiler_params=_params(("arbitrary",)))(*args)


def _tri_dot(tri, x):
    return jnp.dot(tri, x, preferred_element_type=F32, precision=lax.Precision.HIGHEST)


def _forget_forward(f_raw, b_f):
    rows = f_raw.shape[0]
    t = ATTN_TILE

    def body(f_ref, b_ref, cum_ref):
        tri = (lax.broadcasted_iota(jnp.int32, (t, t), 1) <= lax.broadcasted_iota(jnp.int32, (t, t), 0)).astype(F32)

        def blk(i, carry):
            sl = pl.ds(pl.multiple_of(i * t, t), t)
            xv = f_ref[sl, :] + b_ref[...]
            log_f = jnp.minimum(xv, 0.0) - jnp.log(1.0 + jnp.exp(-jnp.abs(xv)))
            cum_ref[sl, :] = _tri_dot(tri, log_f) + carry
            return carry + jnp.sum(log_f, axis=0, keepdims=True)

        lax.fori_loop(0, rows // t, blk, jnp.zeros((1, LANE), F32))

    return pl.pallas_call(body, out_shape=jax.ShapeDtypeStruct((rows, LANE), F32), name="forget_forward",
                          compiler_params=_params())(f_raw, b_f)


def _forget_backward(f_raw, b_f, dcum):
    rows = f_raw.shape[0]
    t = ATTN_TILE
    nb = rows // t

    def body(f_ref, b_ref, dc_ref, df_ref, db_ref):
        tri = (lax.broadcasted_iota(jnp.int32, (t, t), 1) >= lax.broadcasted_iota(jnp.int32, (t, t), 0)).astype(F32)

        def blk(i, carry):
            later, db = carry
            sl = pl.ds(pl.multiple_of((nb - 1 - i) * t, t), t)
            dc = dc_ref[sl, :]
            dlog = _tri_dot(tri, dc) + later
            xv = f_ref[sl, :] + b_ref[...]
            df = dlog * _sigmoid(-xv)
            df_ref[sl, :] = df.astype(BF)
            return later + jnp.sum(dc, axis=0, keepdims=True), db + jnp.sum(df, axis=0, keepdims=True)

        _, db = lax.fori_loop(0, nb, blk, (jnp.zeros((1, LANE), F32), jnp.zeros((1, LANE), F32)))
        db_ref[...] = db

    return pl.pallas_call(body, out_shape=[jax.ShapeDtypeStruct((rows, LANE), BF), jax.ShapeDtypeStruct((1, LANE), F32)],
                          name="forget_backward", compiler_params=_params())(f_raw, b_f, dcum)


def _t5_bucket(dist):
    max_exact = REL_BUCKETS // 2
    n = jnp.maximum(dist.astype(F32), 1.0)
    large = max_exact + (jnp.log(n / max_exact) / math.log(REL_MAX_DIST / max_exact)
                         * (REL_BUCKETS - max_exact)).astype(jnp.int32)
    large = jnp.minimum(large, REL_BUCKETS - 1)
    return jnp.where(dist < max_exact, dist, large)


def _dil_buckets(dilation):
    i = jnp.arange(Q_BLOCK)[:, None]
    j = jnp.arange(Q_BLOCK)[None, :]
    cur = _t5_bucket(jnp.clip(i - j, 0) * dilation).astype(jnp.int32)
    prev = _t5_bucket(jnp.clip(Q_BLOCK + i - j, 0) * dilation).astype(jnp.int32)
    return cur, prev


def _dil_bias_tiles(tbl_ref, bc_ref, bp_ref, bias_ref, group, hp):
    for hh in range(2):
        col = group * HEADS + 2 * hp + hh
        acc_c = jnp.zeros((Q_BLOCK, Q_BLOCK), F32)
        acc_p = jnp.zeros((Q_BLOCK, Q_BLOCK), F32)
        for b in range(REL_BUCKETS):
            val = tbl_ref[b, col]
            acc_c = jnp.where(bc_ref[...] == b, val, acc_c)
            acc_p = jnp.where(bp_ref[...] == b, val, acc_p)
        bias_ref[2 * hh] = acc_c
        bias_ref[2 * hh + 1] = acc_p


def _dil_specs(group, dilation, length):
    def col(kind):
        return pl.BlockSpec((length, LANE), lambda hp, r: (0, r * 72 + (group * 3 + kind) * 8 + hp))

    out = pl.BlockSpec((length, LANE), lambda hp, r: (0, r * 8 + hp))
    tile = pl.BlockSpec((Q_BLOCK, Q_BLOCK), lambda hp, r: (0, 0))
    table = pl.BlockSpec(memory_space=pltpu.SMEM)
    return col, out, tile, table


def _dil_forward(qkv, group, dilation, table, buckets):
    rows = qkv.shape[0]
    length = rows // dilation
    nb = length // Q_BLOCK
    scale = HEAD_DIM ** -0.5
    qb = Q_BLOCK

    def body(tbl_ref, bc_ref, bp_ref, q_ref, k_ref, v_ref, o_ref, lse_ref, bias_ref):
        hp = pl.program_id(0)

        @pl.when(pl.program_id(1) == 0)
        def _():
            _dil_bias_tiles(tbl_ref, bc_ref, bp_ref, bias_ref, group, hp)

        lo = lax.broadcasted_iota(jnp.int32, (1, LANE), 1) < HEAD_DIM
        ii = lax.broadcasted_iota(jnp.int32, (qb, qb), 0)
        jj = lax.broadcasted_iota(jnp.int32, (qb, qb), 1)

        def blk(n, _):
            cur = pl.ds(pl.multiple_of(n * qb, qb), qb)
            prev = pl.ds(pl.multiple_of(jnp.maximum(n - 1, 0) * qb, qb), qb)
            qn = q_ref[cur, :]
            kc, kp, vc, vp = k_ref[cur, :], k_ref[prev, :], v_ref[cur, :], v_ref[prev, :]
            ok_c = jj <= ii
            ok_p = (jj >= ii) & (n > 0)
            outs, lses = [], []
            for hh in range(2):
                qm = jnp.where(lo if hh == 0 else ~lo, qn, jnp.zeros_like(qn))
                s_c = lax.dot_general(qm, kc, (((1,), (1,)), ((), ())), preferred_element_type=F32) * scale
                s_p = lax.dot_general(qm, kp, (((1,), (1,)), ((), ())), preferred_element_type=F32) * scale
                s_c = jnp.where(ok_c, s_c + bias_ref[2 * hh], NEG_INF)
                s_p = jnp.where(ok_p, s_p + bias_ref[2 * hh + 1], NEG_INF)
                m = jnp.maximum(jnp.max(s_c, axis=1, keepdims=True), jnp.max(s_p, axis=1, keepdims=True))
                e_c = jnp.exp(s_c - m)
                e_p = jnp.exp(s_p - m)
                l = jnp.sum(e_c, axis=1, keepdims=True) + jnp.sum(e_p, axis=1, keepdims=True)
                acc = (jnp.dot(e_c.astype(BF), vc, preferred_element_type=F32)
                       + jnp.dot(e_p.astype(BF), vp, preferred_element_type=F32))
                outs.append(acc / l)
                lses.append(m + jnp.log(l))
            o_ref[cur, :] = jnp.where(lo, outs[0], outs[1])
            lse_ref[cur, :] = jnp.where(lo, lses[0], lses[1])
            return 0

        lax.fori_loop(0, nb, blk, 0)

    col, out, tile, tbl = _dil_specs(group, dilation, length)
    bc, bp = buckets
    o, lse = pl.pallas_call(
        body, out_shape=[jax.ShapeDtypeStruct((length, dilation * D_MODEL), F32)] * 2, grid=(8, dilation),
        in_specs=[tbl, tile, tile, col(0), col(1), col(2)], out_specs=[out, out],
        scratch_shapes=[pltpu.VMEM((4, qb, qb), F32)], name=f"dilated_forward_{dilation}",
        compiler_params=_params(("arbitrary", "arbitrary")))(
            table, bc, bp, *([qkv.reshape(length, dilation * qkv.shape[1])] * 3))
    return o.reshape(rows, D_MODEL), lse.reshape(rows, D_MODEL)


def _dil_backward(qkv, group, dilation, table, buckets, do_g, lse, dlt):
    rows = qkv.shape[0]
    length = rows // dilation
    nb = length // Q_BLOCK
    scale = HEAD_DIM ** -0.5
    qb = Q_BLOCK

    def body(tbl_ref, bc_ref, bp_ref, q_ref, k_ref, v_ref, do_ref, lse_ref, dlt_ref,
             dq_ref, dk_ref, dv_ref, db_ref, bias_ref, dk_acc, dv_acc):
        hp = pl.program_id(0)

        @pl.when(pl.program_id(1) == 0)
        def _():
            _dil_bias_tiles(tbl_ref, bc_ref, bp_ref, bias_ref, group, hp)
            db_ref[...] = jnp.zeros_like(db_ref)

        dk_acc[...] = jnp.zeros_like(dk_acc)
        dv_acc[...] = jnp.zeros_like(dv_acc)
        lo = lax.broadcasted_iota(jnp.int32, (1, LANE), 1) < HEAD_DIM
        ii = lax.broadcasted_iota(jnp.int32, (qb, qb), 0)
        jj = lax.broadcasted_iota(jnp.int32, (qb, qb), 1)
        tn = (((0,), (0,)), ((), ()))
        nt = (((1,), (1,)), ((), ()))

        def blk(n, _):
            cur = pl.ds(pl.multiple_of(n * qb, qb), qb)
            prev = pl.ds(pl.multiple_of(jnp.maximum(n - 1, 0) * qb, qb), qb)
            qn = q_ref[cur, :]
            don = do_ref[cur, :]
            kc, kp, vc, vp = k_ref[cur, :], k_ref[prev, :], v_ref[cur, :], v_ref[prev, :]
            lse_n = lse_ref[cur, :]
            dlt_n = dlt_ref[cur, :]
            ok_c = jj <= ii
            ok_p = (jj >= ii) & (n > 0)
            dqs = []
            dkc = jnp.zeros((qb, LANE), F32)
            dkp = jnp.zeros((qb, LANE), F32)
            dvc = jnp.zeros((qb, LANE), F32)
            dvp = jnp.zeros((qb, LANE), F32)
            for hh in range(2):
                mask = lo if hh == 0 else ~lo
                qm = jnp.where(mask, qn, jnp.zeros_like(qn))
                dom = jnp.where(mask, don, jnp.zeros_like(don))
                lse_h = jnp.max(jnp.where(mask, lse_n, -3e38), axis=1, keepdims=True)
                dlt_h = jnp.max(jnp.where(mask, dlt_n, -3e38), axis=1, keepdims=True)
                s_c = lax.dot_general(qm, kc, nt, preferred_element_type=F32) * scale
                s_p = lax.dot_general(qm, kp, nt, preferred_element_type=F32) * scale
                p_c = jnp.exp(jnp.where(ok_c, s_c + bias_ref[2 * hh], NEG_INF) - lse_h)
                p_p = jnp.exp(jnp.where(ok_p, s_p + bias_ref[2 * hh + 1], NEG_INF) - lse_h)
                ds_c = p_c * (lax.dot_general(dom, vc, nt, preferred_element_type=F32) - dlt_h)
                ds_p = p_p * (lax.dot_general(dom, vp, nt, preferred_element_type=F32) - dlt_h)
                db_ref[0, 2 * hh] += ds_c
                db_ref[0, 2 * hh + 1] += ds_p
                dsc_b, dsp_b = ds_c.astype(BF), ds_p.astype(BF)
                dqs.append(jnp.dot(dsc_b, kc, preferred_element_type=F32)
                           + jnp.dot(dsp_b, kp, preferred_element_type=F32))
                dkc = dkc + lax.dot_general(dsc_b, qm, tn, preferred_element_type=F32)
                dkp = dkp + lax.dot_general(dsp_b, qm, tn, preferred_element_type=F32)
                dvc = dvc + lax.dot_general(p_c.astype(BF), dom, tn, preferred_element_type=F32)
                dvp = dvp + lax.dot_general(p_p.astype(BF), dom, tn, preferred_element_type=F32)
            dq_ref[cur, :] = (jnp.where(lo, dqs[0], dqs[1]) * scale).astype(BF)
            dk_acc[cur, :] += dkc
            dk_acc[prev, :] += dkp
            dv_acc[cur, :] += dvc
            dv_acc[prev, :] += dvp
            return 0

        lax.fori_loop(0, nb, blk, 0)
        dk_ref[...] = (dk_acc[...] * scale).astype(BF)
        dv_ref[...] = dv_acc[...].astype(BF)

    col, out, tile, tbl = _dil_specs(group, dilation, length)
    bc, bp = buckets
    wide = (length, dilation * D_MODEL)
    dq, dk, dv, db = pl.pallas_call(
        body, out_shape=[jax.ShapeDtypeStruct(wide, BF)] * 3 + [jax.ShapeDtypeStruct((8, 4, qb, qb), F32)],
        grid=(8, dilation), in_specs=[tbl, tile, tile, col(0), col(1), col(2), out, out, out],
        out_specs=[out, out, out, pl.BlockSpec((1, 4, qb, qb), lambda hp, r: (hp, 0, 0, 0))],
        scratch_shapes=[pltpu.VMEM((4, qb, qb), F32), pltpu.VMEM((length, LANE), F32), pltpu.VMEM((length, LANE), F32)],
        name=f"dilated_backward_{dilation}", compiler_params=_params(("arbitrary", "arbitrary")))(
            table, bc, bp, *([qkv.reshape(length, dilation * qkv.shape[1])] * 3),
            do_g.reshape(wide), lse.reshape(wide), dlt.reshape(wide))
    return dq.reshape(rows, D_MODEL), dk.reshape(rows, D_MODEL), dv.reshape(rows, D_MODEL), db


def _head_sums(x, lo):
    s0 = jnp.sum(jnp.where(lo, x, 0.0), axis=1, keepdims=True)
    s1 = jnp.sum(jnp.where(lo, 0.0, x), axis=1, keepdims=True)
    return jnp.where(lo, s0, s1)


def _dil_merge_forward(outs, lses):
    rows = outs[0].shape[0]

    def body(o0, o1, o2, l0, l1, l2, o_ref):
        ls = [l0[...], l1[...], l2[...]]
        m = jnp.maximum(jnp.maximum(ls[0], ls[1]), ls[2])
        es = [jnp.exp(v - m) for v in ls]
        tot = es[0] + es[1] + es[2]
        o_ref[...] = ((es[0] * o0[...] + es[1] * o1[...] + es[2] * o2[...]) / tot).astype(BF)

    blk = pl.BlockSpec((ROW_TILE, LANE), lambda i, j: (i, j))
    return pl.pallas_call(body, out_shape=jax.ShapeDtypeStruct((rows, D_MODEL), BF), grid=(rows // ROW_TILE, 8),
                          in_specs=[blk] * 6, out_specs=blk, name="dilated_merge_forward",
                          compiler_params=_params(("parallel", "parallel")))(*outs, *lses)


def _dil_merge_backward(outs, lses, do):
    rows = outs[0].shape[0]

    def body(o0, o1, o2, l0, l1, l2, do_ref, d0, d1, d2, t0, t1, t2):
        lo = lax.broadcasted_iota(jnp.int32, (1, LANE), 1) < HEAD_DIM
        ls = [l0[...], l1[...], l2[...]]
        os_ = [o0[...], o1[...], o2[...]]
        m = jnp.maximum(jnp.maximum(ls[0], ls[1]), ls[2])
        es = [jnp.exp(v - m) for v in ls]
        tot = es[0] + es[1] + es[2]
        alphas = [e / tot for e in es]
        dov = do_ref[...]
        merged = alphas[0] * os_[0] + alphas[1] * os_[1] + alphas[2] * os_[2]
        dot = _head_sums(dov * merged, lo)
        for a, d_ref, t_ref in zip(alphas, (d0, d1, d2), (t0, t1, t2)):
            d_ref[...] = (a * dov).astype(BF)
            t_ref[...] = a * dot

    blk = pl.BlockSpec((ROW_TILE, LANE), lambda i, j: (i, j))
    res = pl.pallas_call(
        body, out_shape=[jax.ShapeDtypeStruct((rows, D_MODEL), BF)] * 3 + [jax.ShapeDtypeStruct((rows, D_MODEL), F32)] * 3,
        grid=(rows // ROW_TILE, 8), in_specs=[blk] * 7, out_specs=[blk] * 6, name="dilated_merge_backward",
        compiler_params=_params(("parallel", "parallel")))(*outs, *lses, do)
    return res[:3], res[3:]


def _rel_bias_grad(dbs, buckets):
    def body(db_ref, bc_ref, bp_ref, o_ref):
        g = pl.program_id(0)
        hp = pl.program_id(1)

        @pl.when((g == 0) & (hp == 0))
        def _():
            o_ref[...] = jnp.zeros_like(o_ref)

        rr = lax.broadcasted_iota(jnp.int32, (REL_BUCKETS, LANE), 0)
        cc = lax.broadcasted_iota(jnp.int32, (REL_BUCKETS, LANE), 1)
        bc = bc_ref[0]
        bp = bp_ref[0]
        acc = jnp.zeros((REL_BUCKETS, LANE), F32)
        for hh in range(2):
            col = g * HEADS + 2 * hp + hh
            d_c = db_ref[0, 0, 2 * hh]
            d_p = db_ref[0, 0, 2 * hh + 1]
            for b in range(REL_BUCKETS):
                val = (jnp.sum(jnp.where(bc == b, d_c, 0.0), keepdims=True)
                       + jnp.sum(jnp.where(bp == b, d_p, 0.0), keepdims=True))
                acc = jnp.where((rr == b) & (cc == col), val, acc)
        o_ref[...] += acc

    db_all = jnp.stack(dbs)
    bc_all = jnp.stack([b[0] for b in buckets])
    bp_all = jnp.stack([b[1] for b in buckets])
    tile = pl.BlockSpec((1, Q_BLOCK, Q_BLOCK), lambda g, hp: (g, 0, 0))
    return pl.pallas_call(
        body, out_shape=jax.ShapeDtypeStruct((REL_BUCKETS, LANE), F32), grid=(3, 8),
        in_specs=[pl.BlockSpec((1, 1, 4, Q_BLOCK, Q_BLOCK), lambda g, hp: (g, hp, 0, 0, 0)), tile, tile],
        out_specs=pl.BlockSpec((REL_BUCKETS, LANE), lambda g, hp: (0, 0)), name="rel_bias_grad",
        compiler_params=_params(("arbitrary", "arbitrary")))(db_all, bc_all, bp_all)


def _mla_forward(hn, w, tables):
    a = _matmul(hn, w["w_a"], name="mla_a")
    cq, ckv, kr = _mla_mid_forward(a, w["q_norm"], w["kv_norm"], tables)
    q_raw = _matmul(cq, w["w_uq"], name="mla_uq")
    q = _rope_heads(q_raw, tables, False, "rope_forward")
    kv = _matmul(ckv, w["w_ukv"], out_dtype=BF, name="mla_ukv")
    scale = (HEAD_DIM + MLA_ROPE) ** -0.5
    o, lse = _attn_forward(q, kv, 0, kr, None, None, scale, "mla_attention_forward")
    y = _matmul(o, w["w_o"], name="attn_out")
    return y, dict(hn=hn, a=a, cq=cq, ckv=ckv, kr=kr, q=q, kv=kv, o=o, lse=lse)


def _mla_backward(dy, w, s, tables):
    scale = (HEAD_DIM + MLA_ROPE) ** -0.5
    g = {}
    g["w_o"] = _matmul(s["o"], dy, ta=True, out_dtype=BF, name="attn_out_dw")
    do = _matmul(dy, w["w_o"], tb=True, out_dtype=BF, name="attn_out_dx")
    dq, dkv, dkr = _attn_backward(s["q"], s["kv"], 0, s["kr"], None, None, s["o"], do, s["lse"], scale,
                                  "mla_attention_backward")
    dq_raw = _rope_heads(dq, tables, True, "rope_backward")
    g["w_uq"] = _matmul(s["cq"], dq_raw, ta=True, out_dtype=BF, name="mla_uq_dw")
    dcq = _matmul(dq_raw, w["w_uq"], tb=True, name="mla_uq_dx")
    g["w_ukv"] = _matmul(s["ckv"], dkv, ta=True, out_dtype=BF, name="mla_ukv_dw")
    dckv = _matmul(dkv, w["w_ukv"], tb=True, name="mla_ukv_dx")
    da, g["q_norm"], g["kv_norm"] = _mla_mid_backward(s["a"], w["q_norm"], w["kv_norm"], tables, dcq, dckv, dkr)
    g["w_a"] = _matmul(s["hn"], da, ta=True, out_dtype=BF, name="mla_a_dw")
    dhn = _matmul(da, w["w_a"], tb=True, name="mla_a_dx")
    return dhn, g


def _fox_forward(hn, w):
    qkv = _matmul(hn, w["w_qkv"], out_dtype=BF, name="fox_qkv")
    f_raw = _matmul(hn, w["w_f"], name="fox_f")
    cum = _forget_forward(f_raw, w["b_f"])
    cum_heads = cum[:, :HEADS].T
    cum_col, cum_row = cum_heads[:, :, None], cum_heads[:, None, :]
    o, lse = _attn_forward(qkv, qkv, HEADS, None, cum_col, cum_row, HEAD_DIM ** -0.5, "fox_attention_forward")
    y = _matmul(o, w["w_o"], name="attn_out")
    return y, dict(hn=hn, qkv=qkv, f_raw=f_raw, cum_col=cum_col, cum_row=cum_row, o=o, lse=lse)


def _fox_backward(dy, w, s):
    g = {}
    g["w_o"] = _matmul(s["o"], dy, ta=True, out_dtype=BF, name="attn_out_dw")
    do = _matmul(dy, w["w_o"], tb=True, out_dtype=BF, name="attn_out_dx")
    dq, dkv, dck, dcq = _attn_backward(s["qkv"], s["qkv"], HEADS, None, s["cum_col"], s["cum_row"], s["o"], do,
                                       s["lse"], HEAD_DIM ** -0.5, "fox_attention_backward")
    dcum = jnp.pad((dck[:, 0, :] + dcq[:, :, 0]).T, ((0, 0), (0, LANE - HEADS)))
    df, g["b_f"] = _forget_backward(s["f_raw"], w["b_f"], dcum)
    dqkv = jnp.concatenate([dq, dkv], axis=1)
    g["w_qkv"] = _matmul(s["hn"], dqkv, ta=True, out_dtype=BF, name="fox_qkv_dw")
    g["w_f"] = _matmul(s["hn"], df, ta=True, out_dtype=BF, name="fox_f_dw")
    dhn = _matmul(dqkv, w["w_qkv"], tb=True, name="fox_qkv_dx")
    dhn = _matmul(df, w["w_f"], tb=True, add=dhn, name="fox_f_dx")
    return dhn, g


def _dil_mixer_forward(hn, w, buckets):
    qkv = _matmul(hn, w["w_qkv"], out_dtype=BF, name="dil_qkv")
    outs, lses = [], []
    for grp, (_, dilation) in enumerate(DIL_PATTERNS):
        o_g, lse_g = _dil_forward(qkv, grp, dilation, w["rel_bias"], buckets[grp])
        outs.append(o_g)
        lses.append(lse_g)
    o = _dil_merge_forward(outs, lses)
    y = _matmul(o, w["w_o"], name="dil_out")
    return y, dict(hn=hn, qkv=qkv, outs=outs, lses=lses, o=o)


def _dil_mixer_backward(dy, w, s, buckets):
    g = {}
    g["w_o"] = _matmul(s["o"], dy, ta=True, out_dtype=BF, name="dil_out_dw")
    do = _matmul(dy, w["w_o"], tb=True, name="dil_out_dx")
    do_gs, dlts = _dil_merge_backward(s["outs"], s["lses"], do)
    parts, dbs = [], []
    for grp, (_, dilation) in enumerate(DIL_PATTERNS):
        dq, dk, dv, db = _dil_backward(s["qkv"], grp, dilation, w["rel_bias"], buckets[grp], do_gs[grp],
                                       s["lses"][grp], dlts[grp])
        parts += [dq, dk, dv]
        dbs.append(db)
    dqkv = jnp.concatenate(parts, axis=1)
    g["rel_bias"] = _rel_bias_grad(dbs, buckets)
    g["w_qkv"] = _matmul(s["hn"], dqkv, ta=True, out_dtype=BF, name="dil_qkv_dw")
    dhn = _matmul(dqkv, w["w_qkv"], tb=True, name="dil_qkv_dx")
    return dhn, g


def _local_step(x, p, positions, target, cw):
    tables = _rope_tables(positions)
    buckets = [_dil_buckets(d) for _, d in DIL_PATTERNS]
    norm_g = cw["norm_g"]

    def gain(i, k):
        return norm_g[i, k][None, :]

    saved = []
    h = x
    hn = _prenorm(h, gain(0, 0))
    sq = dh = None
    for i in range(DEPTH):
        mixer, j = i % N_MIXERS, i // N_MIXERS
        if mixer == 0:
            y, ms = _mla_forward(hn, cw["mla"][j], tables)
        elif mixer == 1:
            y, ms = _dil_mixer_forward(hn, cw["dil"][j], buckets)
        else:
            y, ms = _fox_forward(hn, cw["fox"][j])
        h1, hn2 = _post_residual(h, y, gain(i, 1), gain(i, 2))
        gu = _matmul(hn2, cw["ffn_w_in"][i], out_dtype=BF, name="ffn_in")
        act = _swiglu_forward(gu)
        f = _matmul(act, cw["ffn_w_out"][i], name="ffn_out")
        h2, h2b = _post_residual(h1, f, gain(i, 3), None)
        pp = _matmul(p[i], cw["ple_w_proj"][i], name="ple_proj")
        z = _matmul(h2b, cw["ple_w_gate"][i], name="ple_gate")
        saved.append(dict(h=h, y=y, ms=ms, h1=h1, hn2=hn2, gu=gu, act=act, f=f, h2b=h2b, pp=pp, z=z))
        if i + 1 < DEPTH:
            h, hn = _ple_forward(h2, pp, z, gain(i + 1, 0))
        else:
            dh, sq = _ple_loss(h2, pp, z, target)

    grads = dict(norm_g=[[None] * 4 for _ in range(DEPTH)], ffn_w_in=[None] * DEPTH, ffn_w_out=[None] * DEPTH,
                 ple_w_proj=[None] * DEPTH, ple_w_gate=[None] * DEPTH, mla={}, dil={}, fox={})
    for i in reversed(range(DEPTH)):
        s = saved[i]
        mixer, j = i % N_MIXERS, i // N_MIXERS
        dpp, dz = _ple_backward(dh, s["pp"], s["z"])
        grads["ple_w_proj"][i] = _matmul(p[i], dpp, ta=True, out_dtype=BF, name="ple_proj_dw")
        grads["ple_w_gate"][i] = _matmul(s["h2b"], dz, ta=True, out_dtype=BF, name="ple_gate_dw")
        dh2 = _matmul(dz, cw["ple_w_gate"][i], tb=True, add=dh, name="ple_gate_dx")
        df, grads["norm_g"][i][3] = _rms_backward(s["f"], gain(i, 3), dh2, None, BF)
        grads["ffn_w_out"][i] = _matmul(s["act"], df, ta=True, out_dtype=BF, name="ffn_out_dw")
        dact = _matmul(df, cw["ffn_w_out"][i], tb=True, out_dtype=BF, name="ffn_out_dx")
        dgu = _swiglu_backward(s["gu"], dact)
        grads["ffn_w_in"][i] = _matmul(s["hn2"], dgu, ta=True, out_dtype=BF, name="ffn_in_dw")
        dhn2 = _matmul(dgu, cw["ffn_w_in"][i], tb=True, name="ffn_in_dx")
        dh1, grads["norm_g"][i][2] = _rms_backward(s["h1"], gain(i, 2), dhn2, dh2, F32)
        dy, grads["norm_g"][i][1] = _rms_backward(s["y"], gain(i, 1), dh1, None, BF)
        if mixer == 0:
            dhn, grads["mla"][j] = _mla_backward(dy, cw["mla"][j], s["ms"], tables)
        elif mixer == 1:
            dhn, grads["dil"][j] = _dil_mixer_backward(dy, cw["dil"][j], s["ms"], buckets)
        else:
            dhn, grads["fox"][j] = _fox_backward(dy, cw["fox"][j], s["ms"])
        dh, grads["norm_g"][i][0] = _rms_backward(s["h"], gain(i, 0), dhn, dh1, F32)
    return sq, dh, grads


COL_SHARDED = ("ffn_w_in", "ple_w_proj", "mla_w_uq", "mla_w_ukv", "dil_w_qkv", "fox_w_qkvf")
ROW_SHARDED = ("ffn_w_out", "ple_w_gate", "mla_w_a", "mla_w_o", "dil_w_o", "fox_w_o")
BIG = ("ffn_w_in", "ffn_w_out", "ple_w_proj", "ple_w_gate", "mla_w_a", "mla_w_uq", "mla_w_ukv", "mla_w_o",
       "dil_w_qkv", "dil_w_o", "fox_w_qkvf", "fox_w_o")
SMALL_SHARDED = ("norm_g", "mla_q_norm", "mla_kv_norm")
SMALL_REPLICATED = ("rel_bias", "fox_b_f")
WEIGHTS = ("norm_g", "ffn_w_in", "ffn_w_out", "ple_w_proj", "ple_w_gate", "rel_bias", "mla_w_a", "mla_q_norm",
           "mla_kv_norm", "mla_w_uq", "mla_w_ukv", "mla_w_o", "dil_w_qkv", "dil_w_o", "fox_w_qkvf", "fox_b_f", "fox_w_o")


def _chunks_to_natural(name, chunks):
    n, layers, r, c = chunks.shape
    if name in ROW_SHARDED:
        return chunks.transpose(1, 0, 2, 3).reshape(layers, n * r, c)
    return chunks.transpose(1, 2, 0, 3).reshape(layers, r, n * c)


def _pad_heads_out(wo):
    layers = wo.shape[0]
    w4 = wo.reshape(layers, HEADS, HEAD_DIM, D_MODEL)
    return jnp.pad(w4, ((0, 0), (0, 0), (HEAD_DIM, 0), (0, 0))).reshape(layers, HEADS * LANE, D_MODEL)


def _natural_to_compute(nat):
    cw = {k: nat[k] for k in ("ffn_w_in", "ffn_w_out", "ple_w_proj", "ple_w_gate")}
    wa = nat["mla_w_a"]
    rank = MLA_Q_RANK + MLA_KV_RANK
    wa_p = jnp.concatenate([wa[:, :, :rank], jnp.zeros(wa.shape[:2] + (64,), wa.dtype), wa[:, :, rank:],
                            jnp.zeros(wa.shape[:2] + (32,), wa.dtype)], axis=2)
    wuq = nat["mla_w_uq"]
    layers = wuq.shape[0]
    wuq_p = jnp.pad(wuq.reshape(layers, MLA_Q_RANK, HEADS, HEAD_DIM + MLA_ROPE),
                    ((0, 0), (0, 0), (0, 0), (0, LANE - HEAD_DIM - MLA_ROPE))).reshape(layers, MLA_Q_RANK, HEADS * LANE)
    cw["mla"] = dict(w_a=wa_p, w_uq=wuq_p, w_ukv=nat["mla_w_ukv"], w_o=_pad_heads_out(nat["mla_w_o"]))
    cw["dil"] = dict(w_qkv=nat["dil_w_qkv"], w_o=nat["dil_w_o"])
    wf = nat["fox_w_qkvf"]
    layers = wf.shape[0]
    inner = HEADS * HEAD_DIM
    q4 = wf[:, :, :inner].reshape(layers, D_MODEL, HEADS, HEAD_DIM)
    k4 = wf[:, :, inner:2 * inner].reshape(layers, D_MODEL, HEADS, HEAD_DIM)
    v4 = wf[:, :, 2 * inner:3 * inner].reshape(layers, D_MODEL, HEADS, HEAD_DIM)
    q_p = jnp.pad(q4, ((0, 0), (0, 0), (0, 0), (0, HEAD_DIM))).reshape(layers, D_MODEL, HEADS * LANE)
    kv_p = jnp.concatenate([k4, v4], axis=3).reshape(layers, D_MODEL, HEADS * LANE)
    f_p = jnp.pad(wf[:, :, 3 * inner:], ((0, 0), (0, 0), (0, LANE - HEADS)))
    cw["fox"] = dict(w_qkv=jnp.concatenate([q_p, kv_p], axis=2), w_f=f_p, w_o=_pad_heads_out(nat["fox_w_o"]))
    return cw


def _chunks_to_compute(chunks):
    return _natural_to_compute({k: _chunks_to_natural(k, chunks[k]) for k in BIG})


def _per_layer(cw_big, small):
    n_mla, n_dil, n_fox = (len(range(m, DEPTH, N_MIXERS)) for m in range(N_MIXERS))
    cw = dict(norm_g=small["norm_g"], ffn_w_in=cw_big["ffn_w_in"], ffn_w_out=cw_big["ffn_w_out"],
              ple_w_proj=cw_big["ple_w_proj"], ple_w_gate=cw_big["ple_w_gate"])
    cw["mla"] = [dict(w_a=cw_big["mla"]["w_a"][j], w_uq=cw_big["mla"]["w_uq"][j], w_ukv=cw_big["mla"]["w_ukv"][j],
                      w_o=cw_big["mla"]["w_o"][j], q_norm=small["mla_q_norm"][j][None, :],
                      kv_norm=small["mla_kv_norm"][j][None, :]) for j in range(n_mla)]
    cw["dil"] = [dict(w_qkv=cw_big["dil"]["w_qkv"][j], w_o=cw_big["dil"]["w_o"][j], rel_bias=small["rel_bias"])
                 for j in range(n_dil)]
    cw["fox"] = [dict(w_qkv=cw_big["fox"]["w_qkv"][j], w_f=cw_big["fox"]["w_f"][j], w_o=cw_big["fox"]["w_o"][j],
                      b_f=jnp.pad(small["fox_b_f"][j][None, :], ((0, 0), (0, LANE - HEADS)))) for j in range(n_fox)]
    return cw


def _stack_big_grads(grads):
    def st(group, key):
        return jnp.stack([grads[group][j][key] for j in sorted(grads[group])])

    out = {k: jnp.stack(grads[k]) for k in ("ffn_w_in", "ffn_w_out", "ple_w_proj", "ple_w_gate")}
    out["mla"] = {k: st("mla", k) for k in ("w_a", "w_uq", "w_ukv", "w_o")}
    out["dil"] = {k: st("dil", k) for k in ("w_qkv", "w_o")}
    out["fox"] = {k: st("fox", k) for k in ("w_qkv", "w_f", "w_o")}
    return out


def _small_grads(grads):
    out = {"norm_g": jnp.stack([jnp.concatenate(row, axis=0) for row in grads["norm_g"]])}
    out["mla_q_norm"] = jnp.concatenate([grads["mla"][j]["q_norm"] for j in sorted(grads["mla"])], axis=0)
    out["mla_kv_norm"] = jnp.concatenate([grads["mla"][j]["kv_norm"] for j in sorted(grads["mla"])], axis=0)
    rel = grads["dil"][0]["rel_bias"]
    for j in sorted(grads["dil"])[1:]:
        rel = rel + grads["dil"][j]["rel_bias"]
    out["rel_bias"] = rel[:, :3 * HEADS]
    out["fox_b_f"] = jnp.concatenate([grads["fox"][j]["b_f"][:, :HEADS] for j in sorted(grads["fox"])], axis=0)
    return out


def _chip_peers():
    x, y, c = lax.axis_index("x"), lax.axis_index("y"), lax.axis_index("c")
    peers = [(1 - x, y), (x, 1 - y), (1 - x, 1 - y)]
    return x, y, c, peers


def _all_gather_chips(shards, name):
    n = len(shards)

    def body(*refs):
        ins, outs = refs[:n], refs[n:2 * n]
        send_sems, recv_sems, local_sems = refs[2 * n:]
        x, y, c, peers = _chip_peers()
        me = 2 * x + y
        sends, local = [], []
        for w in range(n):
            cp = pltpu.make_async_copy(ins[w], outs[w].at[me], local_sems.at[w])
            cp.start()
            local.append(cp)
            for k, (px, py) in enumerate(peers):
                rc = pltpu.make_async_remote_copy(src_ref=ins[w], dst_ref=outs[w].at[me], send_sem=send_sems.at[3 * w + k],
                                                  recv_sem=recv_sems.at[3 * w + k], device_id=(px, py, c),
                                                  device_id_type=MESH)
                rc.start()
                sends.append(rc)
        for w in range(n):
            for k, (px, py) in enumerate(peers):
                pltpu.make_async_remote_copy(src_ref=ins[w], dst_ref=outs[w].at[2 * px + py],
                                             send_sem=send_sems.at[3 * w + k], recv_sem=recv_sems.at[3 * w + k],
                                             device_id=(px, py, c), device_id_type=MESH).wait_recv()
        for rc in sends:
            rc.wait_send()
        for cp in local:
            cp.wait()

    return pl.pallas_call(
        body, out_shape=[jax.ShapeDtypeStruct((N_CHIPS,) + s.shape, s.dtype) for s in shards],
        in_specs=[HBM_SPEC] * n, out_specs=[HBM_SPEC] * n,
        scratch_shapes=[pltpu.SemaphoreType.DMA((3 * n,)), pltpu.SemaphoreType.DMA((3 * n,)), pltpu.SemaphoreType.DMA((n,))],
        name=name)(*shards)


def _exchange_chips(parts, name):
    n = len(parts)

    def body(*refs):
        ins, outs = refs[:n], refs[n:2 * n]
        send_sems, recv_sems, local_sems = refs[2 * n:]
        x, y, c, peers = _chip_peers()
        me = 2 * x + y
        sends, local = [], []
        for w in range(n):
            cp = pltpu.make_async_copy(ins[w].at[me], outs[w].at[me], local_sems.at[w])
            cp.start()
            local.append(cp)
            for k, (px, py) in enumerate(peers):
                rc = pltpu.make_async_remote_copy(src_ref=ins[w].at[2 * px + py], dst_ref=outs[w].at[me],
                                                  send_sem=send_sems.at[3 * w + k], recv_sem=recv_sems.at[3 * w + k],
                                                  device_id=(px, py, c), device_id_type=MESH)
                rc.start()
                sends.append(rc)
        for w in range(n):
            for k, (px, py) in enumerate(peers):
                pltpu.make_async_remote_copy(src_ref=ins[w].at[me], dst_ref=outs[w].at[2 * px + py],
                                             send_sem=send_sems.at[3 * w + k], recv_sem=recv_sems.at[3 * w + k],
                                             device_id=(px, py, c), device_id_type=MESH).wait_recv()
        for rc in sends:
            rc.wait_send()
        for cp in local:
            cp.wait()

    return pl.pallas_call(
        body, out_shape=[jax.ShapeDtypeStruct(s.shape, s.dtype) for s in parts],
        in_specs=[HBM_SPEC] * n, out_specs=[HBM_SPEC] * n,
        scratch_shapes=[pltpu.SemaphoreType.DMA((3 * n,)), pltpu.SemaphoreType.DMA((3 * n,)), pltpu.SemaphoreType.DMA((n,))],
        name=name)(*parts)


def _exchange_sibling(arrays, name):
    n = len(arrays)

    def body(*refs):
        ins, outs = refs[:n], refs[n:2 * n]
        send_sems, recv_sems = refs[2 * n:]
        x, y, c = lax.axis_index("x"), lax.axis_index("y"), lax.axis_index("c")
        copies = [pltpu.make_async_remote_copy(src_ref=ins[w], dst_ref=outs[w], send_sem=send_sems.at[w],
                                               recv_sem=recv_sems.at[w], device_id=(x, y, 1 - c), device_id_type=MESH)
                  for w in range(n)]
        for cp in copies:
            cp.start()
        for cp in copies:
            cp.wait_recv()
        for cp in copies:
            cp.wait_send()

    return pl.pallas_call(
        body, out_shape=[jax.ShapeDtypeStruct(s.shape, s.dtype) for s in arrays],
        in_specs=[HBM_SPEC] * n, out_specs=[HBM_SPEC] * n,
        scratch_shapes=[pltpu.SemaphoreType.DMA((n,)), pltpu.SemaphoreType.DMA((n,))], name=name)(*arrays)


def _all_reduce_small(v):
    rows = v.shape[0]

    def body(v_ref, sum_ref, slots, send_sems, recv_sems):
        x, y, c = lax.axis_index("x"), lax.axis_index("y"), lax.axis_index("c")
        me = 4 * x + 2 * y + c
        slots[me] = v_ref[...]
        sends = []
        for k in range(1, N_DEV):
            bx, by, bc = (k >> 2) & 1, (k >> 1) & 1, k & 1
            peer = (x ^ bx, y ^ by, c ^ bc)
            rc = pltpu.make_async_remote_copy(src_ref=v_ref, dst_ref=slots.at[me], send_sem=send_sems.at[k],
                                              recv_sem=recv_sems.at[k], device_id=peer, device_id_type=MESH)
            rc.start()
            sends.append(rc)
        for k in range(1, N_DEV):
            bx, by, bc = (k >> 2) & 1, (k >> 1) & 1, k & 1
            src = 4 * (x ^ bx) + 2 * (y ^ by) + (c ^ bc)
            pltpu.make_async_remote_copy(src_ref=v_ref, dst_ref=slots.at[src], send_sem=send_sems.at[k],
                                         recv_sem=recv_sems.at[k], device_id=(x ^ bx, y ^ by, c ^ bc),
                                         device_id_type=MESH).wait_recv()
        for rc in sends:
            rc.wait_send()
        total = slots[0]
        for k in range(1, N_DEV):
            total = total + slots[k]
        sum_ref[...] = total

    vm = pl.BlockSpec(memory_space=pltpu.VMEM)
    return pl.pallas_call(
        body, out_shape=jax.ShapeDtypeStruct((rows, LANE), F32), in_specs=[vm], out_specs=vm,
        scratch_shapes=[pltpu.VMEM((N_DEV, rows, LANE), F32), pltpu.SemaphoreType.DMA((N_DEV,)),
                        pltpu.SemaphoreType.DMA((N_DEV,))], name="all_reduce_small")(v)


def _as_2d(a):
    return a.reshape(-1, a.shape[-1])


def _row_tile(rows, cols):
    for t in (512, 256, 128, 64, 32, 16):
        if rows % t == 0 and t * cols * 4 <= (1 << 20):
            return t
    return rows


def _sum_chips(parts):
    _, rows, cols = parts.shape
    tr = _row_tile(rows, cols)

    def body(p_ref, o_ref):
        total = p_ref[0].astype(F32)
        for k in range(1, N_CHIPS):
            total = total + p_ref[k].astype(F32)
        o_ref[...] = total

    return pl.pallas_call(body, out_shape=jax.ShapeDtypeStruct((rows, cols), F32), grid=(rows // tr,),
                          in_specs=[pl.BlockSpec((N_CHIPS, tr, cols), lambda i: (0, i, 0))],
                          out_specs=pl.BlockSpec((tr, cols), lambda i: (i, 0)), name="sum_chips",
                          compiler_params=_params(("parallel",)))(parts)


def _adamw_math(w, g, m, v):
    m = ADAM_B1 * m + (1.0 - ADAM_B1) * g
    v = ADAM_B2 * v + (1.0 - ADAM_B2) * (g * g)
    m_hat = m / (1.0 - ADAM_B1 ** ADAM_STEP)
    v_hat = v / (1.0 - ADAM_B2 ** ADAM_STEP)
    delta = -ADAM_LR * (m_hat / (jnp.sqrt(v_hat) + ADAM_EPS) + ADAM_WD * w)
    return delta, m, v


def _adamw(w, m, v, g_mine, g_sibling):
    rows, cols = w.shape
    tr = _row_tile(rows, cols)
    two = g_sibling is not None

    def body(*refs):
        if two:
            w_ref, m_ref, v_ref, ga_ref, gb_ref, g_ref, d_ref, nm_ref, nv_ref = refs
            g = ga_ref[...] + gb_ref[...]
        else:
            w_ref, m_ref, v_ref, ga_ref, g_ref, d_ref, nm_ref, nv_ref = refs
            g = ga_ref[...]
        delta, nm, nv = _adamw_math(w_ref[...], g, m_ref[...], v_ref[...])
        g_ref[...] = g
        d_ref[...] = delta
        nm_ref[...] = nm
        nv_ref[...] = nv

    blk = pl.BlockSpec((tr, cols), lambda i: (i, 0))
    args = [w, m, v, g_mine] + ([g_sibling] if two else [])
    return pl.pallas_call(body, out_shape=[jax.ShapeDtypeStruct((rows, cols), F32)] * 4, grid=(rows // tr,),
                          in_specs=[blk] * len(args), out_specs=[blk] * 4, name="adamw",
                          compiler_params=_params(("parallel",)))(*args)


def _pack_rows(arrays):
    flat = jnp.concatenate([a.reshape(-1) for a in arrays])
    rows = -(-flat.shape[0] // (8 * LANE)) * 8
    return jnp.pad(flat, (0, rows * LANE - flat.shape[0])).reshape(rows, LANE)


def _unpack_rows(packed, shapes):
    flat = packed.reshape(-1)
    out, at = [], 0
    for s in shapes:
        size = math.prod(s)
        out.append(flat[at:at + size].reshape(s))
        at += size
    return out


def kernel(x, p, positions, norm_g, ffn_w_in, ffn_w_out, ple_w_proj, ple_w_gate, rel_bias, mla_w_a, mla_q_norm, mla_kv_norm, mla_w_uq, mla_w_ukv, mla_w_o, dil_w_qkv, dil_w_o, fox_w_qkvf, fox_b_f, fox_w_o, loss_target, m_norm_g, m_ffn_w_in, m_ffn_w_out, m_ple_w_proj, m_ple_w_gate, m_rel_bias, m_mla_w_a, m_mla_q_norm, m_mla_kv_norm, m_mla_w_uq, m_mla_w_ukv, m_mla_w_o, m_dil_w_qkv, m_dil_w_o, m_fox_w_qkvf, m_fox_b_f, m_fox_w_o, v_norm_g, v_ffn_w_in, v_ffn_w_out, v_ple_w_proj, v_ple_w_gate, v_rel_bias, v_mla_w_a, v_mla_q_norm, v_mla_kv_norm, v_mla_w_uq, v_mla_w_ukv, v_mla_w_o, v_dil_w_qkv, v_dil_w_o, v_fox_w_qkvf, v_fox_b_f, v_fox_w_o):
    w = dict(norm_g=norm_g, ffn_w_in=ffn_w_in, ffn_w_out=ffn_w_out, ple_w_proj=ple_w_proj, ple_w_gate=ple_w_gate,
             rel_bias=rel_bias, mla_w_a=mla_w_a, mla_q_norm=mla_q_norm, mla_kv_norm=mla_kv_norm, mla_w_uq=mla_w_uq,
             mla_w_ukv=mla_w_ukv, mla_w_o=mla_w_o, dil_w_qkv=dil_w_qkv, dil_w_o=dil_w_o, fox_w_qkvf=fox_w_qkvf,
             fox_b_f=fox_b_f, fox_w_o=fox_w_o)
    m = dict(norm_g=m_norm_g, ffn_w_in=m_ffn_w_in, ffn_w_out=m_ffn_w_out, ple_w_proj=m_ple_w_proj,
             ple_w_gate=m_ple_w_gate, rel_bias=m_rel_bias, mla_w_a=m_mla_w_a, mla_q_norm=m_mla_q_norm,
             mla_kv_norm=m_mla_kv_norm, mla_w_uq=m_mla_w_uq, mla_w_ukv=m_mla_w_ukv, mla_w_o=m_mla_w_o,
             dil_w_qkv=m_dil_w_qkv, dil_w_o=m_dil_w_o, fox_w_qkvf=m_fox_w_qkvf, fox_b_f=m_fox_b_f, fox_w_o=m_fox_w_o)
    v = dict(norm_g=v_norm_g, ffn_w_in=v_ffn_w_in, ffn_w_out=v_ffn_w_out, ple_w_proj=v_ple_w_proj,
             ple_w_gate=v_ple_w_gate, rel_bias=v_rel_bias, mla_w_a=v_mla_w_a, mla_q_norm=v_mla_q_norm,
             mla_kv_norm=v_mla_kv_norm, mla_w_uq=v_mla_w_uq, mla_w_ukv=v_mla_w_ukv, mla_w_o=v_mla_w_o,
             dil_w_qkv=v_dil_w_qkv, dil_w_o=v_dil_w_o, fox_w_qkvf=v_fox_w_qkvf, fox_b_f=v_fox_b_f, fox_w_o=v_fox_w_o)
    chip = 2 * lax.axis_index("x") + lax.axis_index("y")

    small_shapes = [w[k].shape for k in SMALL_SHARDED]
    small_packed = _pack_rows([w[k] for k in SMALL_SHARDED])
    gathered = _all_gather_chips([w[k].astype(BF) for k in BIG] + [small_packed], "all_gather_weights")
    chunks = dict(zip(BIG, gathered[:-1]))
    small_chunks = [_unpack_rows(gathered[-1][k], small_shapes) for k in range(N_CHIPS)]
    small = {name: jnp.concatenate([small_chunks[k][idx] for k in range(N_CHIPS)], axis=-1)
             for idx, name in enumerate(SMALL_SHARDED)}
    small["rel_bias"] = rel_bias
    small["fox_b_f"] = fox_b_f
    cw = _per_layer(_chunks_to_compute(chunks), small)

    sq, grad_x, grads = _local_step(x[0], p[:, 0], positions[0], loss_target[0], cw)
    loss = lax.psum(0.5 / D_MODEL * jnp.sum(sq), ("x", "y", "c"))

    chunk_spec = {k: jax.ShapeDtypeStruct(chunks[k].shape, BF) for k in BIG}
    (contrib,) = jax.linear_transpose(_chunks_to_compute, chunk_spec)(_stack_big_grads(grads))
    received = _exchange_chips([contrib[k].reshape((N_CHIPS, -1, contrib[k].shape[-1])) for k in BIG],
                               "exchange_gradients")
    mine = [_sum_chips(r) for r in received]
    theirs = _exchange_sibling(mine, "exchange_sibling")
    results = {}
    for k, ga, gb in zip(BIG, mine, theirs):
        outs = _adamw(_as_2d(w[k]), _as_2d(m[k]), _as_2d(v[k]), ga, gb)
        results[k] = [o.reshape(w[k].shape) for o in outs]

    sg = _small_grads(grads)
    small_all = SMALL_SHARDED + SMALL_REPLICATED
    full_shapes = [sg[k].shape for k in small_all]
    reduced = dict(zip(small_all, _unpack_rows(_all_reduce_small(_pack_rows([sg[k] for k in small_all])), full_shapes)))
    local_g = []
    for k in small_all:
        g = reduced[k]
        if k in SMALL_SHARDED:
            width = w[k].shape[-1]
            g = lax.dynamic_slice_in_dim(g, chip * width, width, axis=g.ndim - 1)
        local_g.append(g)
    local_shapes = [w[k].shape for k in small_all]
    outs = _adamw(_pack_rows([w[k] for k in small_all]), _pack_rows([m[k] for k in small_all]),
                  _pack_rows([v[k] for k in small_all]), _pack_rows(local_g), None)
    unpacked = [_unpack_rows(o, local_shapes) for o in outs]
    for idx, k in enumerate(small_all):
        results[k] = [u[idx] for u in unpacked]

    return (loss, grad_x[None], *[results[k][0] for k in WEIGHTS], *[results[k][1] for k in WEIGHTS],
            *[results[k][2] for k in WEIGHTS], *[results[k][3] for k in WEIGHTS])
```

```python
import functools
import math

import jax
import jax.numpy as jnp
from jax import lax
from jax.experimental import pallas as pl
from jax.experimental.pallas import tpu as pltpu

F32 = jnp.float32
BF = jnp.bfloat16
MESH = pl.DeviceIdType.MESH
HBM_SPEC = pl.BlockSpec(memory_space=pltpu.HBM)

D_MODEL = 1024
DEPTH = 4
N_MIXERS = 3
D_FF = 2816
NORM_EPS = 1e-6
NEG_INF = -1e30
LANE = 128
HEADS = 16
HEAD_DIM = 64
MLA_Q_RANK = 384
MLA_KV_RANK = 256
MLA_ROPE = 32
MLA_A_PAD = 768
ROPE_THETA = 10000.0
DIL_PATTERNS = ((128, 1), (512, 4), (2048, 16))
Q_BLOCK = 128
REL_BUCKETS = 32
REL_MAX_DIST = 2048
N_CHIPS = 4
N_DEV = 8

ADAM_LR = 0.001
ADAM_B1 = 0.9
ADAM_B2 = 0.999
ADAM_EPS = 1e-08
ADAM_WD = 0.01
ADAM_STEP = 10

VMEM_LIMIT = 56 * 1024 * 1024
MATMUL_VMEM_BUDGET = 36 * 1024 * 1024
ROW_TILE = 256
ATTN_TILE = 256


def _params(sem=None):
    return pltpu.CompilerParams(dimension_semantics=sem, vmem_limit_bytes=VMEM_LIMIT)


def _divisor_tiles(dim):
    tiles = [t for t in range(LANE, dim + 1, LANE) if dim % t == 0]
    return tiles or [dim]


def _matmul_tiles(m, n, k, a_bytes, b_bytes, out_bytes, has_add, n_unit=None, k_unit=None):
    best = None
    for tm in _divisor_tiles(m):
        for tn in _divisor_tiles(n_unit or n):
            for tk in _divisor_tiles(k_unit or k):
                if max(tm, tn, tk) > 2048:
                    continue
                vmem = 2 * (tm * tk * a_bytes + tk * tn * b_bytes + tm * tn * out_bytes) + tm * tn * 4
                if has_add:
                    vmem += 2 * tm * tn * 4
                if vmem > MATMUL_VMEM_BUDGET:
                    continue
                steps = (m // tm) * (n // tn) * (k // tk)
                traffic = m * k * a_bytes * (n // tn) + k * n * b_bytes * (m // tm) + m * n * out_bytes
                cost = traffic / 3.0e12 + steps * 0.4e-6
                if best is None or cost < best[0]:
                    best = (cost, tm, tn, tk)
    return best[1:]


def _matmul(a, b, *, ta=False, tb=False, b_chunks=False, out_chunks=False, add=None, out_dtype=F32, name):
    k, m = a.shape if ta else a.shape[::-1]
    n_unit = k_unit = None
    if b_chunks:
        chunks, rows_w, c = b.shape
        if tb:
            kb, n, k_unit = chunks * c, rows_w, c
        else:
            kb, n, n_unit = rows_w, chunks * c, c
    else:
        kb, n = b.shape[::-1] if tb else b.shape
    if out_chunks:
        assert n % N_CHIPS == 0 and add is None
        n_unit = n // N_CHIPS
    assert k == kb, (a.shape, b.shape, ta, tb)
    tm, tn, tk = _matmul_tiles(m, n, k, a.dtype.itemsize, b.dtype.itemsize, jnp.dtype(out_dtype).itemsize,
                               add is not None, n_unit, k_unit)
    nk = k // tk
    dims = (((0 if ta else 1,), (1 if tb else 0,)), ((), ()))

    def body(*refs):
        if add is None:
            a_ref, b_ref, o_ref, acc_ref = refs
            add_ref = None
        else:
            a_ref, b_ref, add_ref, o_ref, acc_ref = refs
        kk = pl.program_id(2)

        @pl.when(kk == 0)
        def _():
            acc_ref[...] = jnp.zeros_like(acc_ref)

        acc_ref[...] += lax.dot_general(a_ref[...].astype(BF), b_ref[...].astype(BF), dims,
                                        preferred_element_type=F32)

        @pl.when(kk == nk - 1)
        def _():
            r = acc_ref[...]
            if add_ref is not None:
                r = r + add_ref[...].astype(F32)
            o_ref[...] = r.astype(out_dtype)

    a_spec = pl.BlockSpec((tk, tm), lambda i, j, q: (q, i)) if ta else pl.BlockSpec((tm, tk), lambda i, j, q: (i, q))
    if b_chunks and tb:
        per_k = k_unit // tk
        b_spec = pl.BlockSpec((None, tn, tk), lambda i, j, q: (q // per_k, j, q % per_k))
    elif b_chunks:
        per_n = n_unit // tn
        b_spec = pl.BlockSpec((None, tk, tn), lambda i, j, q: (j // per_n, q, j % per_n))
    elif tb:
        b_spec = pl.BlockSpec((tn, tk), lambda i, j, q: (j, q))
    else:
        b_spec = pl.BlockSpec((tk, tn), lambda i, j, q: (q, j))
    if out_chunks:
        per_o = n_unit // tn
        o_spec = pl.BlockSpec((None, tm, tn), lambda i, j, q: (j // per_o, i, j % per_o))
        out_shape = jax.ShapeDtypeStruct((N_CHIPS, m, n_unit), out_dtype)
    else:
        o_spec = pl.BlockSpec((tm, tn), lambda i, j, q: (i, j))
        out_shape = jax.ShapeDtypeStruct((m, n), out_dtype)
    in_specs = [a_spec, b_spec]
    args = [a, b]
    if add is not None:
        in_specs.append(o_spec)
        args.append(add)
    return pl.pallas_call(
        body, out_shape=out_shape, grid=(m // tm, n // tn, nk),
        in_specs=in_specs, out_specs=o_spec, scratch_shapes=[pltpu.VMEM((tm, tn), F32)], name=name,
        compiler_params=_params(("parallel", "parallel", "arbitrary")))(*args)


def _rowwise(body, name, rows, ins, outs, tr=ROW_TILE):
    def row_spec(cols):
        return pl.BlockSpec((tr, cols), lambda i: (i, 0))

    def full_spec(shape):
        zeros = (0,) * len(shape)
        return pl.BlockSpec(shape, lambda i: zeros)

    in_specs = [row_spec(a.shape[1]) if kind == "row" else full_spec(a.shape) for a, kind in ins]
    out_specs = [row_spec(shape[1]) if kind == "row" else full_spec(shape) for shape, _, kind in outs]
    out_shape = [jax.ShapeDtypeStruct(shape, dtype) for shape, dtype, _ in outs]
    return pl.pallas_call(body, out_shape=out_shape, grid=(rows // tr,), in_specs=in_specs, out_specs=out_specs,
                          name=name, compiler_params=_params(("arbitrary",)))(*[a for a, _ in ins])


def _rstd(x):
    return lax.rsqrt(jnp.mean(x * x, axis=-1, keepdims=True) + NORM_EPS)


def _rms_bwd_math(x, g, dy):
    r = _rstd(x)
    gd = dy * g
    dx = r * gd - x * (r * r * r) * jnp.mean(gd * x, axis=-1, keepdims=True)
    dg = jnp.sum(dy * x * r, axis=0, keepdims=True)
    return dx, dg


def _sigmoid(x):
    return 1.0 / (1.0 + jnp.exp(-x))


def _init_acc(*refs):
    @pl.when(pl.program_id(0) == 0)
    def _():
        for r in refs:
            r[...] = jnp.zeros_like(r)


def _prenorm(h, g):
    rows, cols = h.shape

    def body(h_ref, g_ref, o_ref):
        x = h_ref[...]
        o_ref[...] = (x * _rstd(x) * g_ref[...]).astype(BF)

    return _rowwise(body, "prenorm", rows, [(h, "row"), (g, "full")], [((rows, cols), BF, "row")])[0]


def _post_residual(h, y, g_post, g_pre):
    rows, cols = h.shape
    with_pre = g_pre is not None

    def body(*refs):
        if with_pre:
            h_ref, y_ref, gp_ref, gq_ref, hn_ref, hb_ref = refs
        else:
            h_ref, y_ref, gp_ref, hn_ref, hb_ref = refs
        yv = y_ref[...]
        hn = h_ref[...] + yv * _rstd(yv) * gp_ref[...]
        hn_ref[...] = hn
        hb_ref[...] = (hn * _rstd(hn) * gq_ref[...] if with_pre else hn).astype(BF)

    ins = [(h, "row"), (y, "row"), (g_post, "full")] + ([(g_pre, "full")] if with_pre else [])
    return _rowwise(body, "post_residual_pre" if with_pre else "post_residual", rows, ins,
                    [((rows, cols), F32, "row"), ((rows, cols), BF, "row")])


def _ple_forward(h2, pp, z, g_pre):
    rows, cols = h2.shape

    def body(h_ref, p_ref, z_ref, g_ref, h3_ref, hb_ref):
        h3 = h_ref[...] + p_ref[...] * _sigmoid(z_ref[...])
        h3_ref[...] = h3
        hb_ref[...] = (h3 * _rstd(h3) * g_ref[...]).astype(BF)

    return _rowwise(body, "ple_forward", rows, [(h2, "row"), (pp, "row"), (z, "row"), (g_pre, "full")],
                    [((rows, cols), F32, "row"), ((rows, cols), BF, "row")])


def _ple_loss(h2, pp, z, target):
    rows, cols = h2.shape

    def body(h_ref, p_ref, z_ref, t_ref, dh_ref, sq_ref):
        _init_acc(sq_ref)
        err = h_ref[...] + p_ref[...] * _sigmoid(z_ref[...]) - t_ref[...]
        dh_ref[...] = err * (1.0 / cols)
        sq_ref[...] += jnp.sum(err * err, axis=0, keepdims=True)

    return _rowwise(body, "ple_loss", rows, [(h2, "row"), (pp, "row"), (z, "row"), (target, "row")],
                    [((rows, cols), F32, "row"), ((1, cols), F32, "acc")])


def _ple_backward(dh3, pp, z):
    rows, cols = dh3.shape

    def body(d_ref, p_ref, z_ref, dpp_ref, dz_ref):
        d = d_ref[...]
        s = _sigmoid(z_ref[...])
        dpp_ref[...] = (d * s).astype(BF)
        dz_ref[...] = (d * p_ref[...] * s * (1.0 - s)).astype(BF)

    return _rowwise(body, "ple_backward", rows, [(dh3, "row"), (pp, "row"), (z, "row")],
                    [((rows, cols), BF, "row"), ((rows, cols), BF, "row")])


def _rms_backward(x, g, dy, add, out_dtype):
    rows, cols = x.shape
    with_add = add is not None

    def body(*refs):
        if with_add:
            x_ref, g_ref, dy_ref, add_ref, dx_ref, dg_ref = refs
        else:
            x_ref, g_ref, dy_ref, dx_ref, dg_ref = refs
        _init_acc(dg_ref)
        dx, dg = _rms_bwd_math(x_ref[...], g_ref[...], dy_ref[...].astype(F32))
        if with_add:
            dx = dx + add_ref[...]
        dx_ref[...] = dx.astype(out_dtype)
        dg_ref[...] += dg

    ins = [(x, "row"), (g, "full"), (dy, "row")] + ([(add, "row")] if with_add else [])
    return _rowwise(body, "rms_backward_add" if with_add else "rms_backward", rows, ins,
                    [((rows, cols), out_dtype, "row"), ((1, cols), F32, "acc")])


def _swiglu_forward(gu):
    rows = gu.shape[0]
    tc = D_FF // 2

    def body(g_ref, u_ref, o_ref):
        g = g_ref[...].astype(F32)
        o_ref[...] = (g * _sigmoid(g) * u_ref[...].astype(F32)).astype(BF)

    return pl.pallas_call(
        body, out_shape=jax.ShapeDtypeStruct((rows, D_FF), BF), grid=(rows // ROW_TILE, 2),
        in_specs=[pl.BlockSpec((ROW_TILE, tc), lambda i, j: (i, j)), pl.BlockSpec((ROW_TILE, tc), lambda i, j: (i, j + 2))],
        out_specs=pl.BlockSpec((ROW_TILE, tc), lambda i, j: (i, j)), name="swiglu_forward",
        compiler_params=_params(("parallel", "parallel")))(gu, gu)


def _swiglu_backward(gu, dact):
    rows = gu.shape[0]
    tc = D_FF // 2

    def body(g_ref, u_ref, d_ref, o_ref):
        g = g_ref[...].astype(F32)
        u = u_ref[...].astype(F32)
        d = d_ref[...].astype(F32)
        s = _sigmoid(g)

        @pl.when(pl.program_id(1) < 2)
        def _():
            o_ref[...] = (d * u * s * (1.0 + g * (1.0 - s))).astype(BF)

        @pl.when(pl.program_id(1) >= 2)
        def _():
            o_ref[...] = (d * g * s).astype(BF)

    return pl.pallas_call(
        body, out_shape=jax.ShapeDtypeStruct((rows, 2 * D_FF), BF), grid=(rows // ROW_TILE, 4),
        in_specs=[pl.BlockSpec((ROW_TILE, tc), lambda i, j: (i, j % 2)),
                  pl.BlockSpec((ROW_TILE, tc), lambda i, j: (i, j % 2 + 2)),
                  pl.BlockSpec((ROW_TILE, tc), lambda i, j: (i, j % 2))],
        out_specs=pl.BlockSpec((ROW_TILE, tc), lambda i, j: (i, j)), name="swiglu_backward",
        compiler_params=_params(("parallel", "parallel")))(gu, gu, dact)


def _rope_tables(positions):
    half = MLA_ROPE // 2
    inv = ROPE_THETA ** (-jnp.arange(half, dtype=F32) / half)
    ang = positions.astype(F32)[:, None] * inv
    cos, sin = jnp.cos(ang), jnp.sin(ang)
    rows = positions.shape[0]
    c = jnp.ones((rows, LANE), F32).at[:, 64:80].set(cos).at[:, 80:96].set(cos)
    sa = jnp.zeros((rows, LANE), F32).at[:, 64:80].set(-sin)
    sb = jnp.zeros((rows, LANE), F32).at[:, 80:96].set(sin)
    return c, sa, sb


def _rope_apply(x, c, sa, sb):
    return x * c + pltpu.roll(x, LANE - 16, 1) * sa + pltpu.roll(x, 16, 1) * sb


def _rope_apply_t(dy, c, sa, sb):
    return dy * c + pltpu.roll(dy * sa, 16, 1) + pltpu.roll(dy * sb, LANE - 16, 1)


def _rope_heads(x, tables, transpose, name):
    rows, cols = x.shape

    def body(x_ref, c_ref, sa_ref, sb_ref, o_ref):
        fn = _rope_apply_t if transpose else _rope_apply
        c, sa, sb = c_ref[...], sa_ref[...], sb_ref[...]
        for head in range(cols // LANE):
            lanes = slice(head * LANE, (head + 1) * LANE)
            o_ref[:, lanes] = fn(x_ref[:, lanes].astype(F32), c, sa, sb).astype(BF)

    blk = pl.BlockSpec((ROW_TILE, cols), lambda i: (i, 0))
    tbl = pl.BlockSpec((ROW_TILE, LANE), lambda i: (i, 0))
    return pl.pallas_call(body, out_shape=jax.ShapeDtypeStruct((rows, cols), BF), grid=(rows // ROW_TILE,),
                          in_specs=[blk, tbl, tbl, tbl], out_specs=blk, name=name,
                          compiler_params=_params(("parallel",)))(x, *tables)


def _mla_mid_forward(a, q_norm, kv_norm, tables):
    rows = a.shape[0]
    qr, kvr = MLA_Q_RANK, MLA_KV_RANK

    def body(a_ref, qn_ref, kn_ref, c_ref, sa_ref, sb_ref, cq_ref, ckv_ref, kr_ref):
        aq = a_ref[:, 0:qr]
        akv = a_ref[:, qr:qr + kvr]
        cq_ref[...] = (aq * _rstd(aq) * qn_ref[...]).astype(BF)
        ckv_ref[...] = (akv * _rstd(akv) * kn_ref[...]).astype(BF)
        kr_ref[...] = _rope_apply(a_ref[:, qr + kvr:], c_ref[...], sa_ref[...], sb_ref[...]).astype(BF)

    ins = [(a, "row"), (q_norm, "full"), (kv_norm, "full")] + [(t, "row") for t in tables]
    return _rowwise(body, "mla_mid_forward", rows, ins,
                    [((rows, qr), BF, "row"), ((rows, kvr), BF, "row"), ((rows, LANE), BF, "row")])


def _mla_mid_backward(a, q_norm, kv_norm, tables, dcq, dckv, dkr):
    rows = a.shape[0]
    qr, kvr = MLA_Q_RANK, MLA_KV_RANK

    def body(a_ref, qn_ref, kn_ref, c_ref, sa_ref, sb_ref, dcq_ref, dckv_ref, dkr_ref, da_ref, dqn_ref, dkn_ref):
        _init_acc(dqn_ref, dkn_ref)
        dxq, dgq = _rms_bwd_math(a_ref[:, 0:qr], qn_ref[...], dcq_ref[...])
        dxk, dgk = _rms_bwd_math(a_ref[:, qr:qr + kvr], kn_ref[...], dckv_ref[...])
        da_ref[:, 0:qr] = dxq.astype(BF)
        da_ref[:, qr:qr + kvr] = dxk.astype(BF)
        da_ref[:, qr + kvr:] = _rope_apply_t(dkr_ref[...], c_ref[...], sa_ref[...], sb_ref[...]).astype(BF)
        dqn_ref[...] += dgq
        dkn_ref[...] += dgk

    ins = ([(a, "row"), (q_norm, "full"), (kv_norm, "full")] + [(t, "row") for t in tables]
           + [(dcq, "row"), (dckv, "row"), (dkr, "row")])
    return _rowwise(body, "mla_mid_backward", rows, ins,
                    [((rows, MLA_A_PAD), BF, "row"), ((1, qr), F32, "acc"), ((1, kvr), F32, "acc")])


def _attn_specs(rows, kv_off):
    head = pl.BlockSpec((rows, LANE), lambda h: (0, h))
    kv_head = pl.BlockSpec((rows, LANE), lambda h: (0, h + kv_off))
    shared = pl.BlockSpec((rows, LANE), lambda h: (0, 0))
    col_vec = pl.BlockSpec((1, rows, 1), lambda h: (h, 0, 0))
    row_vec = pl.BlockSpec((1, 1, rows), lambda h: (h, 0, 0))
    return head, kv_head, shared, col_vec, row_vec


def _attn_forward(q, kv, kv_off, kr, cum_col, cum_row, scale, name):
    rows = q.shape[0]
    heads = HEADS
    t = ATTN_TILE
    nb = rows // t
    has_kr = kr is not None
    has_f = cum_col is not None

    def body(*refs):
        it = iter(refs)
        q_ref, kv_ref = next(it), next(it)
        kr_ref = next(it) if has_kr else None
        cc_ref = next(it) if has_f else None
        cr_ref = next(it) if has_f else None
        o_ref, lse_ref = next(it), next(it)
        lo = lax.broadcasted_iota(jnp.int32, (1, LANE), 1) < HEAD_DIM
        causal = (lax.broadcasted_iota(jnp.int32, (t, t), 1) <= lax.broadcasted_iota(jnp.int32, (t, t), 0))

        def q_block(i, _):
            qs = pl.ds(pl.multiple_of(i * t, t), t)
            qb = q_ref[qs, :]
            cq = cc_ref[0, qs, :] if has_f else None

            def step(j, carry, diag):
                m, l, acc = carry
                ks = pl.ds(pl.multiple_of(j * t, t), t)
                kvb = kv_ref[ks, :]
                kk = jnp.where(lo, kvb, kr_ref[ks, :] if has_kr else jnp.zeros_like(kvb))
                s = lax.dot_general(qb, kk, (((1,), (1,)), ((), ())), preferred_element_type=F32) * scale
                if has_f:
                    s = s + (cq - cr_ref[0, :, ks])
                if diag:
                    s = jnp.where(causal, s, NEG_INF)
                mn = jnp.maximum(m, jnp.max(s, axis=1, keepdims=True))
                alpha = jnp.exp(m - mn)
                p = jnp.exp(s - mn)
                l = alpha * l + jnp.sum(p, axis=1, keepdims=True)
                acc = alpha * acc + jnp.dot(p.astype(BF), kvb, preferred_element_type=F32)
                return mn, l, acc

            init = (jnp.full((t, 1), NEG_INF, F32), jnp.zeros((t, 1), F32), jnp.zeros((t, LANE), F32))
            carry = lax.fori_loop(0, i, lambda j, c: step(j, c, False), init)
            m, l, acc = step(i, carry, True)
            o_ref[qs, :] = jnp.where(lo, 0.0, acc / l).astype(BF)
            lse_ref[0, qs, :] = m + jnp.log(l)
            return 0

        lax.fori_loop(0, nb, q_block, 0)

    head, kv_head, shared, col_vec, row_vec = _attn_specs(rows, kv_off)
    in_specs, args = [head, kv_head], [q, kv]
    if has_kr:
        in_specs.append(shared)
        args.append(kr)
    if has_f:
        in_specs += [col_vec, row_vec]
        args += [cum_col, cum_row]
    return pl.pallas_call(
        body, out_shape=[jax.ShapeDtypeStruct((rows, heads * LANE), BF), jax.ShapeDtypeStruct((heads, rows, 1), F32)],
        grid=(heads,), in_specs=in_specs, out_specs=[head, col_vec], name=name,
        compiler_params=_params(("arbitrary",)))(*args)


def _attn_backward(q, kv, kv_off, kr, cum_col, cum_row, o, do, lse, scale, name):
    rows = q.shape[0]
    heads = HEADS
    t = ATTN_TILE
    nb = rows // t
    has_kr = kr is not None
    has_f = cum_col is not None

    def body(*refs):
        it = iter(refs)
        q_ref, kv_ref = next(it), next(it)
        kr_ref = next(it) if has_kr else None
        cc_ref = next(it) if has_f else None
        cr_ref = next(it) if has_f else None
        o_ref, do_ref, lse_ref = next(it), next(it), next(it)
        dq_ref, dkv_ref = next(it), next(it)
        dkr_ref = next(it) if has_kr else None
        dck_ref = next(it) if has_f else None
        dcq_ref = next(it) if has_f else None
        dq_acc = next(it)
        lo = lax.broadcasted_iota(jnp.int32, (1, LANE), 1) < HEAD_DIM
        causal = (lax.broadcasted_iota(jnp.int32, (t, t), 1) <= lax.broadcasted_iota(jnp.int32, (t, t), 0))

        dq_acc[...] = jnp.zeros_like(dq_acc)
        if has_kr:
            _init_acc(dkr_ref)
        if has_f:
            dcq_ref[...] = jnp.zeros_like(dcq_ref)

        def kv_block(j, _):
            ks = pl.ds(pl.multiple_of(j * t, t), t)
            kvb = kv_ref[ks, :]
            kk = jnp.where(lo, kvb, kr_ref[ks, :] if has_kr else jnp.zeros_like(kvb))
            ck = cr_ref[0, :, ks] if has_f else None

            def pair(i, carry, diag):
                dkk, dvv, dcs = carry
                qs = pl.ds(pl.multiple_of(i * t, t), t)
                qb = q_ref[qs, :]
                dob = do_ref[qs, :]
                s = lax.dot_general(qb, kk, (((1,), (1,)), ((), ())), preferred_element_type=F32) * scale
                if has_f:
                    s = s + (cc_ref[0, qs, :] - ck)
                if diag:
                    s = jnp.where(causal, s, NEG_INF)
                p = jnp.exp(s - lse_ref[0, qs, :])
                dp = lax.dot_general(dob, kvb, (((1,), (1,)), ((), ())), preferred_element_type=F32)
                delta = jnp.sum(dob.astype(F32) * o_ref[qs, :].astype(F32), axis=1, keepdims=True)
                ds = p * (dp - delta)
                dsb = ds.astype(BF)
                dvv = dvv + lax.dot_general(p.astype(BF), dob, (((0,), (0,)), ((), ())), preferred_element_type=F32)
                dkk = dkk + lax.dot_general(dsb, qb, (((0,), (0,)), ((), ())), preferred_element_type=F32)
                dq_acc[qs, :] += jnp.dot(dsb, kk, preferred_element_type=F32)
                if has_f:
                    dcs = dcs + jnp.sum(ds, axis=0, keepdims=True)
                    dcq_ref[0, qs, :] += jnp.sum(ds, axis=1, keepdims=True)
                return dkk, dvv, dcs

            init = (jnp.zeros((t, LANE), F32), jnp.zeros((t, LANE), F32), jnp.zeros((1, t), F32))
            carry = pair(j, init, True)
            dkk, dvv, dcs = lax.fori_loop(j + 1, nb, lambda i, c: pair(i, c, False), carry)
            dkk = dkk * scale
            dkv_ref[ks, :] = jnp.where(lo, dkk, dvv).astype(BF)
            if has_kr:
                dkr_ref[ks, :] += jnp.where(lo, 0.0, dkk)
            if has_f:
                dck_ref[0, :, ks] = -dcs
            return 0

        lax.fori_loop(0, nb, kv_block, 0)
        dq_ref[...] = (dq_acc[...] * scale).astype(BF)

    head, kv_head, shared, col_vec, row_vec = _attn_specs(rows, kv_off)
    in_specs, args = [head, kv_head], [q, kv]
    if has_kr:
        in_specs.append(shared)
        args.append(kr)
    if has_f:
        in_specs += [col_vec, row_vec]
        args += [cum_col, cum_row]
    in_specs += [head, head, col_vec]
    args += [o, do, lse]
    out_shape = [jax.ShapeDtypeStruct((rows, heads * LANE), BF), jax.ShapeDtypeStruct((rows, heads * LANE), BF)]
    out_specs = [head, head]
    if has_kr:
        out_shape.append(jax.ShapeDtypeStruct((rows, LANE), F32))
        out_specs.append(shared)
    if has_f:
        out_shape += [jax.ShapeDtypeStruct((heads, 1, rows), F32), jax.ShapeDtypeStruct((heads, rows, 1), F32)]
        out_specs += [row_vec, col_vec]
    return pl.pallas_call(
        body, out_shape=out_shape, grid=(heads,), in_specs=in_specs, out_specs=out_specs,
        scratch_shapes=[pltpu.VMEM((rows, LANE), F32)], name=name, compiler_params=_params(("arbitrary",)))(*args)


def _tri_dot(tri, x):
    return jnp.dot(tri, x, preferred_element_type=F32, precision=lax.Precision.HIGHEST)


def _forget_forward(f_raw, b_f):
    rows = f_raw.shape[0]
    t = ATTN_TILE

    def body(f_ref, b_ref, cum_ref):
        tri = (lax.broadcasted_iota(jnp.int32, (t, t), 1) <= lax.broadcasted_iota(jnp.int32, (t, t), 0)).astype(F32)

        def blk(i, carry):
            sl = pl.ds(pl.multiple_of(i * t, t), t)
            xv = f_ref[sl, :] + b_ref[...]
            log_f = jnp.minimum(xv, 0.0) - jnp.log(1.0 + jnp.exp(-jnp.abs(xv)))
            cum_ref[sl, :] = _tri_dot(tri, log_f) + carry
            return carry + jnp.sum(log_f, axis=0, keepdims=True)

        lax.fori_loop(0, rows // t, blk, jnp.zeros((1, LANE), F32))

    return pl.pallas_call(body, out_shape=jax.ShapeDtypeStruct((rows, LANE), F32), name="forget_forward",
                          compiler_params=_params())(f_raw, b_f)


def _forget_backward(f_raw, b_f, dcum):
    rows = f_raw.shape[0]
    t = ATTN_TILE
    nb = rows // t

    def body(f_ref, b_ref, dc_ref, df_ref, db_ref):
        tri = (lax.broadcasted_iota(jnp.int32, (t, t), 1) >= lax.broadcasted_iota(jnp.int32, (t, t), 0)).astype(F32)

        def blk(i, carry):
            later, db = carry
            sl = pl.ds(pl.multiple_of((nb - 1 - i) * t, t), t)
            dc = dc_ref[sl, :]
            dlog = _tri_dot(tri, dc) + later
            xv = f_ref[sl, :] + b_ref[...]
            df = dlog * _sigmoid(-xv)
            df_ref[sl, :] = df.astype(BF)
            return later + jnp.sum(dc, axis=0, keepdims=True), db + jnp.sum(df, axis=0, keepdims=True)

        _, db = lax.fori_loop(0, nb, blk, (jnp.zeros((1, LANE), F32), jnp.zeros((1, LANE), F32)))
        db_ref[...] = db

    return pl.pallas_call(body, out_shape=[jax.ShapeDtypeStruct((rows, LANE), BF), jax.ShapeDtypeStruct((1, LANE), F32)],
                          name="forget_backward", compiler_params=_params())(f_raw, b_f, dcum)


def _t5_bucket(dist):
    max_exact = REL_BUCKETS // 2
    n = jnp.maximum(dist.astype(F32), 1.0)
    large = max_exact + (jnp.log(n / max_exact) / math.log(REL_MAX_DIST / max_exact)
                         * (REL_BUCKETS - max_exact)).astype(jnp.int32)
    large = jnp.minimum(large, REL_BUCKETS - 1)
    return jnp.where(dist < max_exact, dist, large)


def _dil_buckets(dilation):
    i = jnp.arange(Q_BLOCK)[:, None]
    j = jnp.arange(Q_BLOCK)[None, :]
    cur = _t5_bucket(jnp.clip(i - j, 0) * dilation).astype(jnp.int32)
    prev = _t5_bucket(jnp.clip(Q_BLOCK + i - j, 0) * dilation).astype(jnp.int32)
    return cur, prev


def _dil_bias_tiles(tbl_ref, bc_ref, bp_ref, bias_ref, group, hp):
    for hh in range(2):
        col = group * HEADS + 2 * hp + hh
        acc_c = jnp.zeros((Q_BLOCK, Q_BLOCK), F32)
        acc_p = jnp.zeros((Q_BLOCK, Q_BLOCK), F32)
        for b in range(REL_BUCKETS):
            val = tbl_ref[b, col]
            acc_c = jnp.where(bc_ref[...] == b, val, acc_c)
            acc_p = jnp.where(bp_ref[...] == b, val, acc_p)
        bias_ref[2 * hh] = acc_c
        bias_ref[2 * hh + 1] = acc_p


def _dil_specs(group, dilation, length):
    def col(kind):
        return pl.BlockSpec((length, LANE), lambda hp, r: (0, r * 72 + (group * 3 + kind) * 8 + hp))

    out = pl.BlockSpec((length, LANE), lambda hp, r: (0, r * 8 + hp))
    tile = pl.BlockSpec((Q_BLOCK, Q_BLOCK), lambda hp, r: (0, 0))
    table = pl.BlockSpec(memory_space=pltpu.SMEM)
    return col, out, tile, table


def _dil_forward(qkv, group, dilation, table, buckets):
    rows = qkv.shape[0]
    length = rows // dilation
    nb = length // Q_BLOCK
    scale = HEAD_DIM ** -0.5
    qb = Q_BLOCK

    def body(tbl_ref, bc_ref, bp_ref, q_ref, k_ref, v_ref, o_ref, lse_ref, bias_ref):
        hp = pl.program_id(0)

        @pl.when(pl.program_id(1) == 0)
        def _():
            _dil_bias_tiles(tbl_ref, bc_ref, bp_ref, bias_ref, group, hp)

        lo = lax.broadcasted_iota(jnp.int32, (1, LANE), 1) < HEAD_DIM
        ii = lax.broadcasted_iota(jnp.int32, (qb, qb), 0)
        jj = lax.broadcasted_iota(jnp.int32, (qb, qb), 1)

        def blk(n, _):
            cur = pl.ds(pl.multiple_of(n * qb, qb), qb)
            prev = pl.ds(pl.multiple_of(jnp.maximum(n - 1, 0) * qb, qb), qb)
            qn = q_ref[cur, :]
            kc, kp, vc, vp = k_ref[cur, :], k_ref[prev, :], v_ref[cur, :], v_ref[prev, :]
            ok_c = jj <= ii
            ok_p = (jj >= ii) & (n > 0)
            outs, lses = [], []
            for hh in range(2):
                qm = jnp.where(lo if hh == 0 else ~lo, qn, jnp.zeros_like(qn))
                s_c = lax.dot_general(qm, kc, (((1,), (1,)), ((), ())), preferred_element_type=F32) * scale
                s_p = lax.dot_general(qm, kp, (((1,), (1,)), ((), ())), preferred_element_type=F32) * scale
                s_c = jnp.where(ok_c, s_c + bias_ref[2 * hh], NEG_INF)
                s_p = jnp.where(ok_p, s_p + bias_ref[2 * hh + 1], NEG_INF)
                m = jnp.maximum(jnp.max(s_c, axis=1, keepdims=True), jnp.max(s_p, axis=1, keepdims=True))
                e_c = jnp.exp(s_c - m)
                e_p = jnp.exp(s_p - m)
                l = jnp.sum(e_c, axis=1, keepdims=True) + jnp.sum(e_p, axis=1, keepdims=True)
                acc = (jnp.dot(e_c.astype(BF), vc, preferred_element_type=F32)
                       + jnp.dot(e_p.astype(BF), vp, preferred_element_type=F32))
                outs.append(acc / l)
                lses.append(m + jnp.log(l))
            o_ref[cur, :] = jnp.where(lo, outs[0], outs[1])
            lse_ref[cur, :] = jnp.where(lo, lses[0], lses[1])
            return 0

        lax.fori_loop(0, nb, blk, 0)

    col, out, tile, tbl = _dil_specs(group, dilation, length)
    bc, bp = buckets
    o, lse = pl.pallas_call(
        body, out_shape=[jax.ShapeDtypeStruct((length, dilation * D_MODEL), F32)] * 2, grid=(8, dilation),
        in_specs=[tbl, tile, tile, col(0), col(1), col(2)], out_specs=[out, out],
        scratch_shapes=[pltpu.VMEM((4, qb, qb), F32)], name=f"dilated_forward_{dilation}",
        compiler_params=_params(("arbitrary", "arbitrary")))(
            table, bc, bp, *([qkv.reshape(length, dilation * qkv.shape[1])] * 3))
    return o.reshape(rows, D_MODEL), lse.reshape(rows, D_MODEL)


def _dil_backward(qkv, group, dilation, table, buckets, do_g, lse, dlt):
    rows = qkv.shape[0]
    length = rows // dilation
    nb = length // Q_BLOCK
    scale = HEAD_DIM ** -0.5
    qb = Q_BLOCK

    def body(tbl_ref, bc_ref, bp_ref, q_ref, k_ref, v_ref, do_ref, lse_ref, dlt_ref,
             dq_ref, dk_ref, dv_ref, db_ref, bias_ref, dk_acc, dv_acc):
        hp = pl.program_id(0)

        @pl.when(pl.program_id(1) == 0)
        def _():
            _dil_bias_tiles(tbl_ref, bc_ref, bp_ref, bias_ref, group, hp)
            db_ref[...] = jnp.zeros_like(db_ref)

        dk_acc[...] = jnp.zeros_like(dk_acc)
        dv_acc[...] = jnp.zeros_like(dv_acc)
        lo = lax.broadcasted_iota(jnp.int32, (1, LANE), 1) < HEAD_DIM
        ii = lax.broadcasted_iota(jnp.int32, (qb, qb), 0)
        jj = lax.broadcasted_iota(jnp.int32, (qb, qb), 1)
        tn = (((0,), (0,)), ((), ()))
        nt = (((1,), (1,)), ((), ()))

        def blk(n, _):
            cur = pl.ds(pl.multiple_of(n * qb, qb), qb)
            prev = pl.ds(pl.multiple_of(jnp.maximum(n - 1, 0) * qb, qb), qb)
            qn = q_ref[cur, :]
            don = do_ref[cur, :]
            kc, kp, vc, vp = k_ref[cur, :], k_ref[prev, :], v_ref[cur, :], v_ref[prev, :]
            lse_n = lse_ref[cur, :]
            dlt_n = dlt_ref[cur, :]
            ok_c = jj <= ii
            ok_p = (jj >= ii) & (n > 0)
            dqs = []
            dkc = jnp.zeros((qb, LANE), F32)
            dkp = jnp.zeros((qb, LANE), F32)
            dvc = jnp.zeros((qb, LANE), F32)
            dvp = jnp.zeros((qb, LANE), F32)
            for hh in range(2):
                mask = lo if hh == 0 else ~lo
                qm = jnp.where(mask, qn, jnp.zeros_like(qn))
                dom = jnp.where(mask, don, jnp.zeros_like(don))
                lse_h = jnp.max(jnp.where(mask, lse_n, -3e38), axis=1, keepdims=True)
                dlt_h = jnp.max(jnp.where(mask, dlt_n, -3e38), axis=1, keepdims=True)
                s_c = lax.dot_general(qm, kc, nt, preferred_element_type=F32) * scale
                s_p = lax.dot_general(qm, kp, nt, preferred_element_type=F32) * scale
                p_c = jnp.exp(jnp.where(ok_c, s_c + bias_ref[2 * hh], NEG_INF) - lse_h)
                p_p = jnp.exp(jnp.where(ok_p, s_p + bias_ref[2 * hh + 1], NEG_INF) - lse_h)
                ds_c = p_c * (lax.dot_general(dom, vc, nt, preferred_element_type=F32) - dlt_h)
                ds_p = p_p * (lax.dot_general(dom, vp, nt, preferred_element_type=F32) - dlt_h)
                db_ref[0, 2 * hh] += ds_c
                db_ref[0, 2 * hh + 1] += ds_p
                dsc_b, dsp_b = ds_c.astype(BF), ds_p.astype(BF)
                dqs.append(jnp.dot(dsc_b, kc, preferred_element_type=F32)
                           + jnp.dot(dsp_b, kp, preferred_element_type=F32))
                dkc = dkc + lax.dot_general(dsc_b, qm, tn, preferred_element_type=F32)
                dkp = dkp + lax.dot_general(dsp_b, qm, tn, preferred_element_type=F32)
                dvc = dvc + lax.dot_general(p_c.astype(BF), dom, tn, preferred_element_type=F32)
                dvp = dvp + lax.dot_general(p_p.astype(BF), dom, tn, preferred_element_type=F32)
            dq_ref[cur, :] = (jnp.where(lo, dqs[0], dqs[1]) * scale).astype(BF)
            dk_acc[cur, :] += dkc
            dk_acc[prev, :] += dkp
            dv_acc[cur, :] += dvc
            dv_acc[prev, :] += dvp
            return 0

        lax.fori_loop(0, nb, blk, 0)
        dk_ref[...] = (dk_acc[...] * scale).astype(BF)
        dv_ref[...] = dv_acc[...].astype(BF)

    col, out, tile, tbl = _dil_specs(group, dilation, length)
    bc, bp = buckets
    wide = (length, dilation * D_MODEL)
    dq, dk, dv, db = pl.pallas_call(
        body, out_shape=[jax.ShapeDtypeStruct(wide, BF)] * 3 + [jax.ShapeDtypeStruct((8, 4, qb, qb), F32)],
        grid=(8, dilation), in_specs=[tbl, tile, tile, col(0), col(1), col(2), out, out, out],
        out_specs=[out, out, out, pl.BlockSpec((1, 4, qb, qb), lambda hp, r: (hp, 0, 0, 0))],
        scratch_shapes=[pltpu.VMEM((4, qb, qb), F32), pltpu.VMEM((length, LANE), F32), pltpu.VMEM((length, LANE), F32)],
        name=f"dilated_backward_{dilation}", compiler_params=_params(("arbitrary", "arbitrary")))(
            table, bc, bp, *([qkv.reshape(length, dilation * qkv.shape[1])] * 3),
            do_g.reshape(wide), lse.reshape(wide), dlt.reshape(wide))
    return dq.reshape(rows, D_MODEL), dk.reshape(rows, D_MODEL), dv.reshape(rows, D_MODEL), db


def _head_sums(x, lo):
    s0 = jnp.sum(jnp.where(lo, x, 0.0), axis=1, keepdims=True)
    s1 = jnp.sum(jnp.where(lo, 0.0, x), axis=1, keepdims=True)
    return jnp.where(lo, s0, s1)


def _dil_merge_forward(outs, lses):
    rows = outs[0].shape[0]

    def body(o0, o1, o2, l0, l1, l2, o_ref):
        ls = [l0[...], l1[...], l2[...]]
        m = jnp.maximum(jnp.maximum(ls[0], ls[1]), ls[2])
        es = [jnp.exp(v - m) for v in ls]
        tot = es[0] + es[1] + es[2]
        o_ref[...] = ((es[0] * o0[...] + es[1] * o1[...] + es[2] * o2[...]) / tot).astype(BF)

    blk = pl.BlockSpec((ROW_TILE, LANE), lambda i, j: (i, j))
    return pl.pallas_call(body, out_shape=jax.ShapeDtypeStruct((rows, D_MODEL), BF), grid=(rows // ROW_TILE, 8),
                          in_specs=[blk] * 6, out_specs=blk, name="dilated_merge_forward",
                          compiler_params=_params(("parallel", "parallel")))(*outs, *lses)


def _dil_merge_backward(outs, lses, do):
    rows = outs[0].shape[0]

    def body(o0, o1, o2, l0, l1, l2, do_ref, d0, d1, d2, t0, t1, t2):
        lo = lax.broadcasted_iota(jnp.int32, (1, LANE), 1) < HEAD_DIM
        ls = [l0[...], l1[...], l2[...]]
        os_ = [o0[...], o1[...], o2[...]]
        m = jnp.maximum(jnp.maximum(ls[0], ls[1]), ls[2])
        es = [jnp.exp(v - m) for v in ls]
        tot = es[0] + es[1] + es[2]
        alphas = [e / tot for e in es]
        dov = do_ref[...]
        merged = alphas[0] * os_[0] + alphas[1] * os_[1] + alphas[2] * os_[2]
        dot = _head_sums(dov * merged, lo)
        for a, d_ref, t_ref in zip(alphas, (d0, d1, d2), (t0, t1, t2)):
            d_ref[...] = (a * dov).astype(BF)
            t_ref[...] = a * dot

    blk = pl.BlockSpec((ROW_TILE, LANE), lambda i, j: (i, j))
    res = pl.pallas_call(
        body, out_shape=[jax.ShapeDtypeStruct((rows, D_MODEL), BF)] * 3 + [jax.ShapeDtypeStruct((rows, D_MODEL), F32)] * 3,
        grid=(rows // ROW_TILE, 8), in_specs=[blk] * 7, out_specs=[blk] * 6, name="dilated_merge_backward",
        compiler_params=_params(("parallel", "parallel")))(*outs, *lses, do)
    return res[:3], res[3:]


def _rel_bias_grad(dbs, buckets):
    def body(db_ref, bc_ref, bp_ref, o_ref):
        g = pl.program_id(0)
        hp = pl.program_id(1)

        @pl.when((g == 0) & (hp == 0))
        def _():
            o_ref[...] = jnp.zeros_like(o_ref)

        rr = lax.broadcasted_iota(jnp.int32, (REL_BUCKETS, LANE), 0)
        cc = lax.broadcasted_iota(jnp.int32, (REL_BUCKETS, LANE), 1)
        bc = bc_ref[0]
        bp = bp_ref[0]
        acc = jnp.zeros((REL_BUCKETS, LANE), F32)
        for hh in range(2):
            col = g * HEADS + 2 * hp + hh
            d_c = db_ref[0, 0, 2 * hh]
            d_p = db_ref[0, 0, 2 * hh + 1]
            for b in range(REL_BUCKETS):
                val = (jnp.sum(jnp.where(bc == b, d_c, 0.0), keepdims=True)
                       + jnp.sum(jnp.where(bp == b, d_p, 0.0), keepdims=True))
                acc = jnp.where((rr == b) & (cc == col), val, acc)
        o_ref[...] += acc

    db_all = jnp.stack(dbs)
    bc_all = jnp.stack([b[0] for b in buckets])
    bp_all = jnp.stack([b[1] for b in buckets])
    tile = pl.BlockSpec((1, Q_BLOCK, Q_BLOCK), lambda g, hp: (g, 0, 0))
    return pl.pallas_call(
        body, out_shape=jax.ShapeDtypeStruct((REL_BUCKETS, LANE), F32), grid=(3, 8),
        in_specs=[pl.BlockSpec((1, 1, 4, Q_BLOCK, Q_BLOCK), lambda g, hp: (g, hp, 0, 0, 0)), tile, tile],
        out_specs=pl.BlockSpec((REL_BUCKETS, LANE), lambda g, hp: (0, 0)), name="rel_bias_grad",
        compiler_params=_params(("arbitrary", "arbitrary")))(db_all, bc_all, bp_all)


def _mla_forward(hn, w, tables):
    a = _matmul(hn, w["w_a"], name="mla_a")
    cq, ckv, kr = _mla_mid_forward(a, w["q_norm"], w["kv_norm"], tables)
    q_raw = _matmul(cq, w["w_uq"], name="mla_uq")
    q = _rope_heads(q_raw, tables, False, "rope_forward")
    kv = _matmul(ckv, w["w_ukv"], b_chunks=True, out_dtype=BF, name="mla_ukv")
    scale = (HEAD_DIM + MLA_ROPE) ** -0.5
    o, lse = _attn_forward(q, kv, 0, kr, None, None, scale, "mla_attention_forward")
    y = _matmul(o, w["w_o"], name="attn_out")
    return y, dict(hn=hn, a=a, cq=cq, ckv=ckv, kr=kr, q=q, kv=kv, o=o, lse=lse)


def _mla_backward(dy, w, s, tables):
    scale = (HEAD_DIM + MLA_ROPE) ** -0.5
    g = {}
    g["w_o"] = _matmul(s["o"], dy, ta=True, out_dtype=BF, name="attn_out_dw")
    do = _matmul(dy, w["w_o"], tb=True, out_dtype=BF, name="attn_out_dx")
    dq, dkv, dkr = _attn_backward(s["q"], s["kv"], 0, s["kr"], None, None, s["o"], do, s["lse"], scale,
                                  "mla_attention_backward")
    dq_raw = _rope_heads(dq, tables, True, "rope_backward")
    g["w_uq"] = _matmul(s["cq"], dq_raw, ta=True, out_dtype=BF, name="mla_uq_dw")
    dcq = _matmul(dq_raw, w["w_uq"], tb=True, name="mla_uq_dx")
    g["w_ukv"] = _matmul(s["ckv"], dkv, ta=True, out_chunks=True, out_dtype=BF, name="mla_ukv_dw")
    dckv = _matmul(dkv, w["w_ukv"], tb=True, b_chunks=True, name="mla_ukv_dx")
    da, g["q_norm"], g["kv_norm"] = _mla_mid_backward(s["a"], w["q_norm"], w["kv_norm"], tables, dcq, dckv, dkr)
    g["w_a"] = _matmul(s["hn"], da, ta=True, out_dtype=BF, name="mla_a_dw")
    dhn = _matmul(da, w["w_a"], tb=True, name="mla_a_dx")
    return dhn, g


def _fox_forward(hn, w):
    qkv = _matmul(hn, w["w_qkv"], out_dtype=BF, name="fox_qkv")
    f_raw = _matmul(hn, w["w_f"], name="fox_f")
    cum = _forget_forward(f_raw, w["b_f"])
    cum_heads = cum[:, :HEADS].T
    cum_col, cum_row = cum_heads[:, :, None], cum_heads[:, None, :]
    o, lse = _attn_forward(qkv, qkv, HEADS, None, cum_col, cum_row, HEAD_DIM ** -0.5, "fox_attention_forward")
    y = _matmul(o, w["w_o"], name="attn_out")
    return y, dict(hn=hn, qkv=qkv, f_raw=f_raw, cum_col=cum_col, cum_row=cum_row, o=o, lse=lse)


def _fox_backward(dy, w, s):
    g = {}
    g["w_o"] = _matmul(s["o"], dy, ta=True, out_dtype=BF, name="attn_out_dw")
    do = _matmul(dy, w["w_o"], tb=True, out_dtype=BF, name="attn_out_dx")
    dq, dkv, dck, dcq = _attn_backward(s["qkv"], s["qkv"], HEADS, None, s["cum_col"], s["cum_row"], s["o"], do,
                                       s["lse"], HEAD_DIM ** -0.5, "fox_attention_backward")
    dcum = jnp.pad((dck[:, 0, :] + dcq[:, :, 0]).T, ((0, 0), (0, LANE - HEADS)))
    df, g["b_f"] = _forget_backward(s["f_raw"], w["b_f"], dcum)
    dqkv = jnp.concatenate([dq, dkv], axis=1)
    g["w_qkv"] = _matmul(s["hn"], dqkv, ta=True, out_dtype=BF, name="fox_qkv_dw")
    g["w_f"] = _matmul(s["hn"], df, ta=True, out_dtype=BF, name="fox_f_dw")
    dhn = _matmul(dqkv, w["w_qkv"], tb=True, name="fox_qkv_dx")
    dhn = _matmul(df, w["w_f"], tb=True, add=dhn, name="fox_f_dx")
    return dhn, g


def _dil_mixer_forward(hn, w, buckets):
    qkv = _matmul(hn, w["w_qkv"], b_chunks=True, out_dtype=BF, name="dil_qkv")
    outs, lses = [], []
    for grp, (_, dilation) in enumerate(DIL_PATTERNS):
        o_g, lse_g = _dil_forward(qkv, grp, dilation, w["rel_bias"], buckets[grp])
        outs.append(o_g)
        lses.append(lse_g)
    o = _dil_merge_forward(outs, lses)
    y = _matmul(o, w["w_o"], name="dil_out")
    return y, dict(hn=hn, qkv=qkv, outs=outs, lses=lses, o=o)


def _dil_mixer_backward(dy, w, s, buckets):
    g = {}
    g["w_o"] = _matmul(s["o"], dy, ta=True, out_dtype=BF, name="dil_out_dw")
    do = _matmul(dy, w["w_o"], tb=True, name="dil_out_dx")
    do_gs, dlts = _dil_merge_backward(s["outs"], s["lses"], do)
    parts, dbs = [], []
    for grp, (_, dilation) in enumerate(DIL_PATTERNS):
        dq, dk, dv, db = _dil_backward(s["qkv"], grp, dilation, w["rel_bias"], buckets[grp], do_gs[grp],
                                       s["lses"][grp], dlts[grp])
        parts += [dq, dk, dv]
        dbs.append(db)
    dqkv = jnp.concatenate(parts, axis=1)
    g["rel_bias"] = _rel_bias_grad(dbs, buckets)
    g["w_qkv"] = _matmul(s["hn"], dqkv, ta=True, out_chunks=True, out_dtype=BF, name="dil_qkv_dw")
    dhn = _matmul(dqkv, w["w_qkv"], tb=True, b_chunks=True, name="dil_qkv_dx")
    return dhn, g


def _mixer_weights(i, lw, small):
    mixer, j = i % N_MIXERS, i // N_MIXERS
    if mixer == 0:
        return dict(lw["mixer"], q_norm=small["mla_q_norm"][j][None, :], kv_norm=small["mla_kv_norm"][j][None, :])
    if mixer == 1:
        return dict(lw["mixer"], rel_bias=small["rel_bias"])
    return dict(lw["mixer"], b_f=jnp.pad(small["fox_b_f"][j][None, :], ((0, 0), (0, LANE - HEADS))))


def _run_layers(x, p, positions, target, get_layer, get_small, put_grads):
    tables = _rope_tables(positions)
    buckets = [_dil_buckets(d) for _, d in DIL_PATTERNS]
    layers, saved = [], []
    h = x
    lw = get_layer(0, positions)
    small = get_small()

    def gain(i, k):
        return small["norm_g"][i, k][None, :]

    hn = _prenorm(h, gain(0, 0))
    sq = dh = None
    for i in range(DEPTH):
        mixer = i % N_MIXERS
        if i > 0:
            lw = get_layer(i, h)
        layers.append(lw)
        mw = _mixer_weights(i, lw, small)
        if mixer == 0:
            y, ms = _mla_forward(hn, mw, tables)
        elif mixer == 1:
            y, ms = _dil_mixer_forward(hn, mw, buckets)
        else:
            y, ms = _fox_forward(hn, mw)
        h1, hn2 = _post_residual(h, y, gain(i, 1), gain(i, 2))
        gu = _matmul(hn2, lw["ffn_w_in"], b_chunks=True, out_dtype=BF, name="ffn_in")
        act = _swiglu_forward(gu)
        f = _matmul(act, lw["ffn_w_out"], name="ffn_out")
        h2, h2b = _post_residual(h1, f, gain(i, 3), None)
        pp = _matmul(p[i], lw["ple_w_proj"], b_chunks=True, name="ple_proj")
        z = _matmul(h2b, lw["ple_w_gate"], name="ple_gate")
        saved.append(dict(h=h, y=y, ms=ms, h1=h1, hn2=hn2, gu=gu, act=act, f=f, h2b=h2b, pp=pp, z=z))
        if i + 1 < DEPTH:
            h, hn = _ple_forward(h2, pp, z, gain(i + 1, 0))
        else:
            dh, sq = _ple_loss(h2, pp, z, target)

    norm_rows = [[None] * 4 for _ in range(DEPTH)]
    sg = dict(mla_q_norm={}, mla_kv_norm={}, rel_bias=None, fox_b_f={})
    for i in reversed(range(DEPTH)):
        s, lw = saved[i], layers[i]
        mixer, j = i % N_MIXERS, i // N_MIXERS
        mw = _mixer_weights(i, lw, small)
        lg = {}
        dpp, dz = _ple_backward(dh, s["pp"], s["z"])
        lg["ple_w_proj"] = _matmul(p[i], dpp, ta=True, out_chunks=True, out_dtype=BF, name="ple_proj_dw")
        lg["ple_w_gate"] = _matmul(s["h2b"], dz, ta=True, out_dtype=BF, name="ple_gate_dw")
        dh2 = _matmul(dz, lw["ple_w_gate"], tb=True, add=dh, name="ple_gate_dx")
        df, norm_rows[i][3] = _rms_backward(s["f"], gain(i, 3), dh2, None, BF)
        lg["ffn_w_out"] = _matmul(s["act"], df, ta=True, out_dtype=BF, name="ffn_out_dw")
        dact = _matmul(df, lw["ffn_w_out"], tb=True, out_dtype=BF, name="ffn_out_dx")
        dgu = _swiglu_backward(s["gu"], dact)
        lg["ffn_w_in"] = _matmul(s["hn2"], dgu, ta=True, out_chunks=True, out_dtype=BF, name="ffn_in_dw")
        dhn2 = _matmul(dgu, lw["ffn_w_in"], tb=True, b_chunks=True, name="ffn_in_dx")
        dh1, norm_rows[i][2] = _rms_backward(s["h1"], gain(i, 2), dhn2, dh2, F32)
        dy, norm_rows[i][1] = _rms_backward(s["y"], gain(i, 1), dh1, None, BF)
        if mixer == 0:
            dhn, mg = _mla_backward(dy, mw, s["ms"], tables)
            sg["mla_q_norm"][j] = mg.pop("q_norm")
            sg["mla_kv_norm"][j] = mg.pop("kv_norm")
        elif mixer == 1:
            dhn, mg = _dil_mixer_backward(dy, mw, s["ms"], buckets)
            rel = mg.pop("rel_bias")[:, :3 * HEADS]
            sg["rel_bias"] = rel if sg["rel_bias"] is None else sg["rel_bias"] + rel
        else:
            dhn, mg = _fox_backward(dy, mw, s["ms"])
            sg["fox_b_f"][j] = mg.pop("b_f")[:, :HEADS]
        lg["mixer"] = mg
        token = put_grads(i, lg)
        dh, norm_rows[i][0] = _rms_backward(s["h"], gain(i, 0) + token[0:1, 0:1], dhn, dh1, F32)
    small_grads = dict(norm_g=jnp.stack([jnp.concatenate(row, axis=0) for row in norm_rows]),
                       rel_bias=sg["rel_bias"])
    for k in ("mla_q_norm", "mla_kv_norm", "fox_b_f"):
        small_grads[k] = jnp.concatenate([sg[k][j] for j in sorted(sg[k])], axis=0)
    return sq, dh, small_grads


COL_SHARDED = ("ffn_w_in", "ple_w_proj", "mla_w_uq", "mla_w_ukv", "dil_w_qkv", "fox_w_qkvf")
ROW_SHARDED = ("ffn_w_out", "ple_w_gate", "mla_w_a", "mla_w_o", "dil_w_o", "fox_w_o")
BIG = ("ffn_w_in", "ffn_w_out", "ple_w_proj", "ple_w_gate", "mla_w_a", "mla_w_uq", "mla_w_ukv", "mla_w_o",
       "dil_w_qkv", "dil_w_o", "fox_w_qkvf", "fox_w_o")
SMALL_SHARDED = ("norm_g", "mla_q_norm", "mla_kv_norm")
SMALL_REPLICATED = ("rel_bias", "fox_b_f")
WEIGHTS = ("norm_g", "ffn_w_in", "ffn_w_out", "ple_w_proj", "ple_w_gate", "rel_bias", "mla_w_a", "mla_q_norm",
           "mla_kv_norm", "mla_w_uq", "mla_w_ukv", "mla_w_o", "dil_w_qkv", "dil_w_o", "fox_w_qkvf", "fox_b_f", "fox_w_o")


LAYER_COMMON = ("ffn_w_in", "ffn_w_out", "ple_w_proj", "ple_w_gate")
MIXER_WEIGHTS = (("mla_w_a", "mla_w_uq", "mla_w_ukv", "mla_w_o"), ("dil_w_qkv", "dil_w_o"), ("fox_w_qkvf", "fox_w_o"))


def _layer_names(i):
    return LAYER_COMMON + MIXER_WEIGHTS[i % N_MIXERS]


def _layer_slot(name, i):
    return i if name in LAYER_COMMON else i // N_MIXERS


def _merge_rows(chunks):
    n, r, c = chunks.shape
    return chunks.reshape(n * r, c)


def _merge_cols(chunks):
    n, r, c = chunks.shape
    return chunks.transpose(1, 0, 2).reshape(r, n * c)


def _pad_heads_out(wo):
    w3 = wo.reshape(HEADS, HEAD_DIM, D_MODEL)
    return jnp.pad(w3, ((0, 0), (HEAD_DIM, 0), (0, 0))).reshape(HEADS * LANE, D_MODEL)


def _layer_to_compute(i, ch):
    lw = dict(ffn_w_in=ch["ffn_w_in"], ffn_w_out=_merge_rows(ch["ffn_w_out"]), ple_w_proj=ch["ple_w_proj"],
              ple_w_gate=_merge_rows(ch["ple_w_gate"]))
    mixer = i % N_MIXERS
    if mixer == 0:
        wa = _merge_rows(ch["mla_w_a"])
        rank = MLA_Q_RANK + MLA_KV_RANK
        wa_p = jnp.concatenate([wa[:, :rank], jnp.zeros((wa.shape[0], 64), wa.dtype), wa[:, rank:],
                                jnp.zeros((wa.shape[0], 32), wa.dtype)], axis=1)
        wuq = _merge_cols(ch["mla_w_uq"]).reshape(MLA_Q_RANK, HEADS, HEAD_DIM + MLA_ROPE)
        wuq_p = jnp.pad(wuq, ((0, 0), (0, 0), (0, LANE - HEAD_DIM - MLA_ROPE))).reshape(MLA_Q_RANK, HEADS * LANE)
        lw["mixer"] = dict(w_a=wa_p, w_uq=wuq_p, w_ukv=ch["mla_w_ukv"], w_o=_pad_heads_out(_merge_rows(ch["mla_w_o"])))
    elif mixer == 1:
        lw["mixer"] = dict(w_qkv=ch["dil_w_qkv"], w_o=_merge_rows(ch["dil_w_o"]))
    else:
        wf = _merge_cols(ch["fox_w_qkvf"])
        inner = HEADS * HEAD_DIM
        q3 = wf[:, :inner].reshape(D_MODEL, HEADS, HEAD_DIM)
        k3 = wf[:, inner:2 * inner].reshape(D_MODEL, HEADS, HEAD_DIM)
        v3 = wf[:, 2 * inner:3 * inner].reshape(D_MODEL, HEADS, HEAD_DIM)
        q_p = jnp.pad(q3, ((0, 0), (0, 0), (0, HEAD_DIM))).reshape(D_MODEL, HEADS * LANE)
        kv_p = jnp.concatenate([k3, v3], axis=2).reshape(D_MODEL, HEADS * LANE)
        f_p = jnp.pad(wf[:, 3 * inner:], ((0, 0), (0, LANE - HEADS)))
        lw["mixer"] = dict(w_qkv=jnp.concatenate([q_p, kv_p], axis=1), w_f=f_p,
                           w_o=_pad_heads_out(_merge_rows(ch["fox_w_o"])))
    return lw


def _layer_contributions(i, lg, chunk_shapes):
    spec = {k: jax.ShapeDtypeStruct(s, BF) for k, s in chunk_shapes.items()}
    (contrib,) = jax.linear_transpose(functools.partial(_layer_to_compute, i), spec)(lg)
    return contrib


def _chip_peers():
    x, y, c = lax.axis_index("x"), lax.axis_index("y"), lax.axis_index("c")
    peers = [(1 - x, y), (x, 1 - y), (1 - x, 1 - y)]
    return x, y, c, peers


SEM_SPEC = pl.BlockSpec(memory_space=pltpu.SEMAPHORE)
ANY_SPEC = pl.BlockSpec(memory_space=pl.ANY)
SPLIT_EFFECT = pltpu.SideEffectType.DATAFLOW_SIDE_EFFECTING


def _spread_copy(src, land, exchange, k, peer, c, send_sems, recv_sems, index, slot):
    px, py = peer
    return pltpu.make_async_remote_copy(
        src_ref=src.at[2 * px + py] if exchange else src, dst_ref=land.at[slot],
        send_sem=send_sems.at[3 * index + k], recv_sem=recv_sems.at[3 * index + k],
        device_id=(px, py, c), device_id_type=MESH)


def _spread_start(srcs, exchange, after, name):
    n = len(srcs)
    lands = [lax.empty(s.shape if exchange else (N_CHIPS,) + s.shape, s.dtype) for s in srcs]

    def body(*refs):
        src, land = refs[:n], refs[n:2 * n]
        send_sems, recv_sems = refs[2 * n + 1], refs[2 * n + 2]
        token = refs[-1]
        x, y, c, peers = _chip_peers()
        me = 2 * x + y
        for w in range(n):
            for k, peer in enumerate(peers):
                _spread_copy(src[w], land[w], exchange, k, peer, c, send_sems, recv_sems, w, me).start()
        token[...] = jnp.zeros_like(token)

    hbm = [pltpu.with_memory_space_constraint(a, pltpu.HBM) for a in list(srcs) + lands]
    out = pl.pallas_call(
        body, name=name,
        out_shape=(pltpu.SemaphoreType.DMA((3 * n,)), pltpu.SemaphoreType.DMA((3 * n,)),
                   *[pltpu.HBM(a.shape, a.dtype) for a in hbm], jax.ShapeDtypeStruct((8, LANE), F32)),
        in_specs=[HBM_SPEC] * (2 * n) + [ANY_SPEC],
        out_specs=(SEM_SPEC, SEM_SPEC, *[HBM_SPEC] * (2 * n), pl.BlockSpec(memory_space=pltpu.VMEM)),
        input_output_aliases={w: 2 + w for w in range(2 * n)},
        compiler_params=pltpu.CompilerParams(has_side_effects=SPLIT_EFFECT))(*hbm, after)
    return dict(send=out[0], recv=out[1], srcs=out[2:2 + n], lands=out[2 + n:2 + 2 * n], token=out[-1],
                exchange=exchange)


def _spread_wait(handle, after, name):
    n = len(handle["srcs"])
    exchange = handle["exchange"]

    def body(*refs):
        src, land = refs[:n], refs[n:2 * n]
        send_sems, recv_sems = refs[2 * n], refs[2 * n + 1]
        land_out = refs[2 * n + 3 + n:2 * n + 3 + 2 * n]
        local_sems = refs[-1]
        x, y, c, peers = _chip_peers()
        me = 2 * x + y
        for w in range(n):
            for k, peer in enumerate(peers):
                cp = _spread_copy(src[w], land[w], exchange, k, peer, c, send_sems, recv_sems, w, 2 * peer[0] + peer[1])
                cp.wait_send()
                cp.wait_recv()
        own = [pltpu.make_async_copy(src[w].at[me] if exchange else src[w], land_out[w].at[me], local_sems.at[w])
               for w in range(n)]
        for cp in own:
            cp.start()
        for cp in own:
            cp.wait()

    arrays = list(handle["srcs"]) + list(handle["lands"])
    out = pl.pallas_call(
        body, name=name, out_shape=tuple(pltpu.HBM(a.shape, a.dtype) for a in arrays),
        in_specs=[HBM_SPEC] * (2 * n) + [SEM_SPEC, SEM_SPEC, ANY_SPEC], out_specs=tuple([HBM_SPEC] * (2 * n)),
        input_output_aliases={w: w for w in range(2 * n)}, scratch_shapes=[pltpu.SemaphoreType.DMA((n,))],
        compiler_params=pltpu.CompilerParams(has_side_effects=SPLIT_EFFECT))(*arrays, handle["send"], handle["recv"], after)
    return list(out[n:])


def _exchange_sibling(arrays, name):
    n = len(arrays)

    def body(*refs):
        ins, outs = refs[:n], refs[n:2 * n]
        send_sems, recv_sems = refs[2 * n:]
        x, y, c = lax.axis_index("x"), lax.axis_index("y"), lax.axis_index("c")
        copies = [pltpu.make_async_remote_copy(src_ref=ins[w], dst_ref=outs[w], send_sem=send_sems.at[w],
                                               recv_sem=recv_sems.at[w], device_id=(x, y, 1 - c), device_id_type=MESH)
                  for w in range(n)]
        for cp in copies:
            cp.start()
        for cp in copies:
            cp.wait_recv()
        for cp in copies:
            cp.wait_send()

    return pl.pallas_call(
        body, out_shape=[jax.ShapeDtypeStruct(s.shape, s.dtype) for s in arrays],
        in_specs=[HBM_SPEC] * n, out_specs=[HBM_SPEC] * n,
        scratch_shapes=[pltpu.SemaphoreType.DMA((n,)), pltpu.SemaphoreType.DMA((n,))], name=name)(*arrays)


def _all_reduce_small(v):
    rows = v.shape[0]

    def body(v_ref, sum_ref, slots, send_sems, recv_sems):
        x, y, c = lax.axis_index("x"), lax.axis_index("y"), lax.axis_index("c")
        me = 4 * x + 2 * y + c
        slots[me] = v_ref[...]
        sends = []
        for k in range(1, N_DEV):
            bx, by, bc = (k >> 2) & 1, (k >> 1) & 1, k & 1
            peer = (x ^ bx, y ^ by, c ^ bc)
            rc = pltpu.make_async_remote_copy(src_ref=v_ref, dst_ref=slots.at[me], send_sem=send_sems.at[k],
                                              recv_sem=recv_sems.at[k], device_id=peer, device_id_type=MESH)
            rc.start()
            sends.append(rc)
        for k in range(1, N_DEV):
            bx, by, bc = (k >> 2) & 1, (k >> 1) & 1, k & 1
            src = 4 * (x ^ bx) + 2 * (y ^ by) + (c ^ bc)
            pltpu.make_async_remote_copy(src_ref=v_ref, dst_ref=slots.at[src], send_sem=send_sems.at[k],
                                         recv_sem=recv_sems.at[k], device_id=(x ^ bx, y ^ by, c ^ bc),
                                         device_id_type=MESH).wait_recv()
        for rc in sends:
            rc.wait_send()
        total = slots[0]
        for k in range(1, N_DEV):
            total = total + slots[k]
        sum_ref[...] = total

    vm = pl.BlockSpec(memory_space=pltpu.VMEM)
    return pl.pallas_call(
        body, out_shape=jax.ShapeDtypeStruct((rows, LANE), F32), in_specs=[vm], out_specs=vm,
        scratch_shapes=[pltpu.VMEM((N_DEV, rows, LANE), F32), pltpu.SemaphoreType.DMA((N_DEV,)),
                        pltpu.SemaphoreType.DMA((N_DEV,))], name="all_reduce_small")(v)


def _as_2d(a):
    return a.reshape(-1, a.shape[-1])


def _row_tile(rows, cols):
    for t in (512, 256, 128, 64, 32, 16):
        if rows % t == 0 and t * cols * 4 <= (1 << 20):
            return t
    return rows


def _sum_chips_into(parts, stacked, slot):
    _, rows, cols = parts.shape
    tr = _row_tile(rows, cols)
    first = slot * (rows // tr)

    def body(p_ref, _, o_ref):
        total = p_ref[0].astype(F32)
        for k in range(1, N_CHIPS):
            total = total + p_ref[k].astype(F32)
        o_ref[...] = total

    return pl.pallas_call(body, out_shape=jax.ShapeDtypeStruct(stacked.shape, F32), grid=(rows // tr,),
                          in_specs=[pl.BlockSpec((N_CHIPS, tr, cols), lambda i: (0, i, 0)), ANY_SPEC],
                          out_specs=pl.BlockSpec((tr, cols), lambda i: (first + i, 0)), input_output_aliases={1: 0},
                          name="sum_chips", compiler_params=_params(("parallel",)))(parts, stacked)


def _adamw_math(w, g, m, v):
    m = ADAM_B1 * m + (1.0 - ADAM_B1) * g
    v = ADAM_B2 * v + (1.0 - ADAM_B2) * (g * g)
    m_hat = m / (1.0 - ADAM_B1 ** ADAM_STEP)
    v_hat = v / (1.0 - ADAM_B2 ** ADAM_STEP)
    delta = -ADAM_LR * (m_hat / (jnp.sqrt(v_hat) + ADAM_EPS) + ADAM_WD * w)
    return delta, m, v


def _adamw(w, m, v, g_mine, g_sibling):
    rows, cols = w.shape
    tr = _row_tile(rows, cols)
    two = g_sibling is not None

    def body(*refs):
        if two:
            w_ref, m_ref, v_ref, ga_ref, gb_ref, g_ref, d_ref, nm_ref, nv_ref = refs
            g = ga_ref[...] + gb_ref[...]
        else:
            w_ref, m_ref, v_ref, ga_ref, g_ref, d_ref, nm_ref, nv_ref = refs
            g = ga_ref[...]
        delta, nm, nv = _adamw_math(w_ref[...], g, m_ref[...], v_ref[...])
        g_ref[...] = g
        d_ref[...] = delta
        nm_ref[...] = nm
        nv_ref[...] = nv

    blk = pl.BlockSpec((tr, cols), lambda i: (i, 0))
    args = [w, m, v, g_mine] + ([g_sibling] if two else [])
    return pl.pallas_call(body, out_shape=[jax.ShapeDtypeStruct((rows, cols), F32)] * 4, grid=(rows // tr,),
                          in_specs=[blk] * len(args), out_specs=[blk] * 4, name="adamw",
                          compiler_params=_params(("parallel",)))(*args)


def _pack_rows(arrays):
    flat = jnp.concatenate([a.reshape(-1) for a in arrays])
    rows = -(-flat.shape[0] // (8 * LANE)) * 8
    return jnp.pad(flat, (0, rows * LANE - flat.shape[0])).reshape(rows, LANE)


def _unpack_rows(packed, shapes):
    flat = packed.reshape(-1)
    out, at = [], 0
    for s in shapes:
        size = math.prod(s)
        out.append(flat[at:at + size].reshape(s))
        at += size
    return out


def kernel(x, p, positions, norm_g, ffn_w_in, ffn_w_out, ple_w_proj, ple_w_gate, rel_bias, mla_w_a, mla_q_norm, mla_kv_norm, mla_w_uq, mla_w_ukv, mla_w_o, dil_w_qkv, dil_w_o, fox_w_qkvf, fox_b_f, fox_w_o, loss_target, m_norm_g, m_ffn_w_in, m_ffn_w_out, m_ple_w_proj, m_ple_w_gate, m_rel_bias, m_mla_w_a, m_mla_q_norm, m_mla_kv_norm, m_mla_w_uq, m_mla_w_ukv, m_mla_w_o, m_dil_w_qkv, m_dil_w_o, m_fox_w_qkvf, m_fox_b_f, m_fox_w_o, v_norm_g, v_ffn_w_in, v_ffn_w_out, v_ple_w_proj, v_ple_w_gate, v_rel_bias, v_mla_w_a, v_mla_q_norm, v_mla_kv_norm, v_mla_w_uq, v_mla_w_ukv, v_mla_w_o, v_dil_w_qkv, v_dil_w_o, v_fox_w_qkvf, v_fox_b_f, v_fox_w_o):
    w = dict(norm_g=norm_g, ffn_w_in=ffn_w_in, ffn_w_out=ffn_w_out, ple_w_proj=ple_w_proj, ple_w_gate=ple_w_gate,
             rel_bias=rel_bias, mla_w_a=mla_w_a, mla_q_norm=mla_q_norm, mla_kv_norm=mla_kv_norm, mla_w_uq=mla_w_uq,
             mla_w_ukv=mla_w_ukv, mla_w_o=mla_w_o, dil_w_qkv=dil_w_qkv, dil_w_o=dil_w_o, fox_w_qkvf=fox_w_qkvf,
             fox_b_f=fox_b_f, fox_w_o=fox_w_o)
    m = dict(norm_g=m_norm_g, ffn_w_in=m_ffn_w_in, ffn_w_out=m_ffn_w_out, ple_w_proj=m_ple_w_proj,
             ple_w_gate=m_ple_w_gate, rel_bias=m_rel_bias, mla_w_a=m_mla_w_a, mla_q_norm=m_mla_q_norm,
             mla_kv_norm=m_mla_kv_norm, mla_w_uq=m_mla_w_uq, mla_w_ukv=m_mla_w_ukv, mla_w_o=m_mla_w_o,
             dil_w_qkv=m_dil_w_qkv, dil_w_o=m_dil_w_o, fox_w_qkvf=m_fox_w_qkvf, fox_b_f=m_fox_b_f, fox_w_o=m_fox_w_o)
    v = dict(norm_g=v_norm_g, ffn_w_in=v_ffn_w_in, ffn_w_out=v_ffn_w_out, ple_w_proj=v_ple_w_proj,
             ple_w_gate=v_ple_w_gate, rel_bias=v_rel_bias, mla_w_a=v_mla_w_a, mla_q_norm=v_mla_q_norm,
             mla_kv_norm=v_mla_kv_norm, mla_w_uq=v_mla_w_uq, mla_w_ukv=v_mla_w_ukv, mla_w_o=v_mla_w_o,
             dil_w_qkv=v_dil_w_qkv, dil_w_o=v_dil_w_o, fox_w_qkvf=v_fox_w_qkvf, fox_b_f=v_fox_b_f, fox_w_o=v_fox_w_o)
    chip = 2 * lax.axis_index("x") + lax.axis_index("y")

    small_shapes = [w[k].shape for k in SMALL_SHARDED]
    gathers = []
    after = positions
    for i in range(DEPTH):
        srcs = [w[k][_layer_slot(k, i)].astype(BF) for k in _layer_names(i)]
        if i == 0:
            srcs.append(_pack_rows([w[k] for k in SMALL_SHARDED]))
        gathers.append(_spread_start(srcs, False, after, f"gather_start_{i}"))
        after = gathers[-1]["token"]
    all_started = after
    state = {}

    def get_layer(i, after_array):
        lands = _spread_wait(gathers[i], all_started if i == 0 else after_array, f"gather_wait_{i}")
        if i == 0:
            pieces = [_unpack_rows(lands[-1][k], small_shapes) for k in range(N_CHIPS)]
            small = {name: jnp.concatenate([pieces[k][idx] for k in range(N_CHIPS)], axis=-1)
                     for idx, name in enumerate(SMALL_SHARDED)}
            state["small"] = dict(small, rel_bias=rel_bias, fox_b_f=fox_b_f)
        chunks = dict(zip(_layer_names(i), lands))
        state[i] = {k: a.shape for k, a in chunks.items()}
        return _layer_to_compute(i, chunks)

    exchanges = {}

    def put_grads(i, lg):
        contrib = _layer_contributions(i, lg, state[i])
        exchanges[i] = _spread_start([contrib[k] for k in _layer_names(i)], True, positions, f"exchange_start_{i}")
        return exchanges[i]["token"]

    sq, grad_x, sg = _run_layers(x[0], p[:, 0], positions[0], loss_target[0], get_layer, lambda: state["small"],
                                 put_grads)
    loss = lax.psum(0.5 / D_MODEL * jnp.sum(sq), ("x", "y", "c"))

    mine = {k: lax.empty(_as_2d(w[k]).shape, F32) for k in BIG}
    for i in reversed(range(DEPTH)):
        received = _spread_wait(exchanges[i], grad_x, f"exchange_wait_{i}")
        for k, r in zip(_layer_names(i), received):
            mine[k] = _sum_chips_into(r, mine[k], _layer_slot(k, i))
    theirs = _exchange_sibling([mine[k] for k in BIG], "exchange_sibling")
    results = {}
    for k, gb in zip(BIG, theirs):
        outs = _adamw(_as_2d(w[k]), _as_2d(m[k]), _as_2d(v[k]), mine[k], gb)
        results[k] = [o.reshape(w[k].shape) for o in outs]

    small_all = SMALL_SHARDED + SMALL_REPLICATED
    full_shapes = [sg[k].shape for k in small_all]
    reduced = dict(zip(small_all, _unpack_rows(_all_reduce_small(_pack_rows([sg[k] for k in small_all])), full_shapes)))
    local_g = []
    for k in small_all:
        g = reduced[k]
        if k in SMALL_SHARDED:
            width = w[k].shape[-1]
            g = lax.dynamic_slice_in_dim(g, chip * width, width, axis=g.ndim - 1)
        local_g.append(g)
    local_shapes = [w[k].shape for k in small_all]
    outs = _adamw(_pack_rows([w[k] for k in small_all]), _pack_rows([m[k] for k in small_all]),
                  _pack_rows([v[k] for k in small_all]), _pack_rows(local_g), None)
    unpacked = [_unpack_rows(o, local_shapes) for o in outs]
    for idx, k in enumerate(small_all):
        results[k] = [u[idx] for u in unpacked]

    return (loss, grad_x[None], *[results[k][0] for k in WEIGHTS], *[results[k][1] for k in WEIGHTS],
            *[results[k][2] for k in WEIGHTS], *[results[k][3] for k in WEIGHTS])
```

```python
import functools
import math

import jax
import jax.numpy as jnp
from jax import lax
from jax.experimental import pallas as pl
from jax.experimental.pallas import tpu as pltpu

F32 = jnp.float32
BF = jnp.bfloat16
MESH = pl.DeviceIdType.MESH
HBM_SPEC = pl.BlockSpec(memory_space=pltpu.HBM)

D_MODEL = 1024
DEPTH = 4
N_MIXERS = 3
D_FF = 2816
NORM_EPS = 1e-6
NEG_INF = -1e30
LANE = 128
HEADS = 16
HEAD_DIM = 64
MLA_Q_RANK = 384
MLA_KV_RANK = 256
MLA_ROPE = 32
MLA_A_PAD = 768
ROPE_THETA = 10000.0
DIL_PATTERNS = ((128, 1), (512, 4), (2048, 16))
Q_BLOCK = 128
REL_BUCKETS = 32
REL_MAX_DIST = 2048
N_CHIPS = 4
N_DEV = 8

ADAM_LR = 0.001
ADAM_B1 = 0.9
ADAM_B2 = 0.999
ADAM_EPS = 1e-08
ADAM_WD = 0.01
ADAM_STEP = 10

VMEM_LIMIT = 56 * 1024 * 1024
MATMUL_VMEM_BUDGET = 36 * 1024 * 1024
ROW_TILE = 256
ATTN_TILE = 256


def _params(sem=None):
    return pltpu.CompilerParams(dimension_semantics=sem, vmem_limit_bytes=VMEM_LIMIT)


def _divisor_tiles(dim):
    tiles = [t for t in range(LANE, dim + 1, LANE) if dim % t == 0]
    return tiles or [dim]


def _matmul_tiles(m, n, k, a_bytes, b_bytes, out_bytes, has_add, n_unit=None, k_unit=None):
    best = None
    for tm in _divisor_tiles(m):
        for tn in _divisor_tiles(n_unit or n):
            for tk in _divisor_tiles(k_unit or k):
                if max(tm, tn, tk) > 2048:
                    continue
                vmem = 2 * (tm * tk * a_bytes + tk * tn * b_bytes + tm * tn * out_bytes) + tm * tn * 4
                if has_add:
                    vmem += 2 * tm * tn * 4
                if vmem > MATMUL_VMEM_BUDGET:
                    continue
                steps = (m // tm) * (n // tn) * (k // tk)
                traffic = m * k * a_bytes * (n // tn) + k * n * b_bytes * (m // tm) + m * n * out_bytes
                cost = traffic / 3.0e12 + steps * 0.4e-6
                if best is None or cost < best[0]:
                    best = (cost, tm, tn, tk)
    return best[1:]


def _matmul(a, b, *, ta=False, tb=False, b_chunks=False, out_chunks=False, add=None, out_dtype=F32, name):
    k, m = a.shape if ta else a.shape[::-1]
    n_unit = k_unit = None
    if b_chunks:
        chunks, rows_w, c = b.shape
        if tb:
            kb, n, k_unit = chunks * c, rows_w, c
        else:
            kb, n, n_unit = rows_w, chunks * c, c
    else:
        kb, n = b.shape[::-1] if tb else b.shape
    if out_chunks:
        assert n % N_CHIPS == 0 and add is None
        n_unit = n // N_CHIPS
    assert k == kb, (a.shape, b.shape, ta, tb)
    tm, tn, tk = _matmul_tiles(m, n, k, a.dtype.itemsize, b.dtype.itemsize, jnp.dtype(out_dtype).itemsize,
                               add is not None, n_unit, k_unit)
    nk = k // tk
    dims = (((0 if ta else 1,), (1 if tb else 0,)), ((), ()))

    def body(*refs):
        if add is None:
            a_ref, b_ref, o_ref, acc_ref = refs
            add_ref = None
        else:
            a_ref, b_ref, add_ref, o_ref, acc_ref = refs
        kk = pl.program_id(2)

        @pl.when(kk == 0)
        def _():
            acc_ref[...] = jnp.zeros_like(acc_ref)

        acc_ref[...] += lax.dot_general(a_ref[...].astype(BF), b_ref[...].astype(BF), dims,
                                        preferred_element_type=F32)

        @pl.when(kk == nk - 1)
        def _():
            r = acc_ref[...]
            if add_ref is not None:
                r = r + add_ref[...].astype(F32)
            o_ref[...] = r.astype(out_dtype)

    a_spec = pl.BlockSpec((tk, tm), lambda i, j, q: (q, i)) if ta else pl.BlockSpec((tm, tk), lambda i, j, q: (i, q))
    if b_chunks and tb:
        per_k = k_unit // tk
        b_spec = pl.BlockSpec((None, tn, tk), lambda i, j, q: (q // per_k, j, q % per_k))
    elif b_chunks:
        per_n = n_unit // tn
        b_spec = pl.BlockSpec((None, tk, tn), lambda i, j, q: (j // per_n, q, j % per_n))
    elif tb:
        b_spec = pl.BlockSpec((tn, tk), lambda i, j, q: (j, q))
    else:
        b_spec = pl.BlockSpec((tk, tn), lambda i, j, q: (q, j))
    if out_chunks:
        per_o = n_unit // tn
        o_spec = pl.BlockSpec((None, tm, tn), lambda i, j, q: (j // per_o, i, j % per_o))
        out_shape = jax.ShapeDtypeStruct((N_CHIPS, m, n_unit), out_dtype)
    else:
        o_spec = pl.BlockSpec((tm, tn), lambda i, j, q: (i, j))
        out_shape = jax.ShapeDtypeStruct((m, n), out_dtype)
    in_specs = [a_spec, b_spec]
    args = [a, b]
    if add is not None:
        in_specs.append(o_spec)
        args.append(add)
    return pl.pallas_call(
        body, out_shape=out_shape, grid=(m // tm, n // tn, nk),
        in_specs=in_specs, out_specs=o_spec, scratch_shapes=[pltpu.VMEM((tm, tn), F32)], name=name,
        compiler_params=_params(("parallel", "parallel", "arbitrary")))(*args)


def _rowwise(body, name, rows, ins, outs, tr=ROW_TILE):
    def row_spec(cols):
        return pl.BlockSpec((tr, cols), lambda i: (i, 0))

    def full_spec(shape):
        zeros = (0,) * len(shape)
        return pl.BlockSpec(shape, lambda i: zeros)

    in_specs = [row_spec(a.shape[1]) if kind == "row" else full_spec(a.shape) for a, kind in ins]
    out_specs = [row_spec(shape[1]) if kind == "row" else full_spec(shape) for shape, _, kind in outs]
    out_shape = [jax.ShapeDtypeStruct(shape, dtype) for shape, dtype, _ in outs]
    return pl.pallas_call(body, out_shape=out_shape, grid=(rows // tr,), in_specs=in_specs, out_specs=out_specs,
                          name=name, compiler_params=_params(("arbitrary",)))(*[a for a, _ in ins])


def _rstd(x):
    return lax.rsqrt(jnp.mean(x * x, axis=-1, keepdims=True) + NORM_EPS)


def _rms_bwd_math(x, g, dy):
    r = _rstd(x)
    gd = dy * g
    dx = r * gd - x * (r * r * r) * jnp.mean(gd * x, axis=-1, keepdims=True)
    dg = jnp.sum(dy * x * r, axis=0, keepdims=True)
    return dx, dg


def _sigmoid(x):
    return 1.0 / (1.0 + jnp.exp(-x))


def _init_acc(*refs):
    @pl.when(pl.program_id(0) == 0)
    def _():
        for r in refs:
            r[...] = jnp.zeros_like(r)


def _prenorm(h, g):
    rows, cols = h.shape

    def body(h_ref, g_ref, o_ref):
        x = h_ref[...]
        o_ref[...] = (x * _rstd(x) * g_ref[...]).astype(BF)

    return _rowwise(body, "prenorm", rows, [(h, "row"), (g, "full")], [((rows, cols), BF, "row")])[0]


def _post_residual(h, y, g_post, g_pre):
    rows, cols = h.shape
    with_pre = g_pre is not None

    def body(*refs):
        if with_pre:
            h_ref, y_ref, gp_ref, gq_ref, hn_ref, hb_ref = refs
        else:
            h_ref, y_ref, gp_ref, hn_ref, hb_ref = refs
        yv = y_ref[...]
        hn = h_ref[...] + yv * _rstd(yv) * gp_ref[...]
        hn_ref[...] = hn
        hb_ref[...] = (hn * _rstd(hn) * gq_ref[...] if with_pre else hn).astype(BF)

    ins = [(h, "row"), (y, "row"), (g_post, "full")] + ([(g_pre, "full")] if with_pre else [])
    return _rowwise(body, "post_residual_pre" if with_pre else "post_residual", rows, ins,
                    [((rows, cols), F32, "row"), ((rows, cols), BF, "row")])


def _ple_forward(h2, pp, z, g_pre):
    rows, cols = h2.shape

    def body(h_ref, p_ref, z_ref, g_ref, h3_ref, hb_ref):
        h3 = h_ref[...] + p_ref[...] * _sigmoid(z_ref[...])
        h3_ref[...] = h3
        hb_ref[...] = (h3 * _rstd(h3) * g_ref[...]).astype(BF)

    return _rowwise(body, "ple_forward", rows, [(h2, "row"), (pp, "row"), (z, "row"), (g_pre, "full")],
                    [((rows, cols), F32, "row"), ((rows, cols), BF, "row")])


def _ple_loss(h2, pp, z, target):
    rows, cols = h2.shape

    def body(h_ref, p_ref, z_ref, t_ref, dh_ref, sq_ref):
        _init_acc(sq_ref)
        err = h_ref[...] + p_ref[...] * _sigmoid(z_ref[...]) - t_ref[...]
        dh_ref[...] = err * (1.0 / cols)
        sq_ref[...] += jnp.sum(err * err, axis=0, keepdims=True)

    return _rowwise(body, "ple_loss", rows, [(h2, "row"), (pp, "row"), (z, "row"), (target, "row")],
                    [((rows, cols), F32, "row"), ((1, cols), F32, "acc")])


def _ple_backward(dh3, pp, z):
    rows, cols = dh3.shape

    def body(d_ref, p_ref, z_ref, dpp_ref, dz_ref):
        d = d_ref[...]
        s = _sigmoid(z_ref[...])
        dpp_ref[...] = (d * s).astype(BF)
        dz_ref[...] = (d * p_ref[...] * s * (1.0 - s)).astype(BF)

    return _rowwise(body, "ple_backward", rows, [(dh3, "row"), (pp, "row"), (z, "row")],
                    [((rows, cols), BF, "row"), ((rows, cols), BF, "row")])


def _rms_backward(x, g, dy, add, out_dtype):
    rows, cols = x.shape
    with_add = add is not None

    def body(*refs):
        if with_add:
            x_ref, g_ref, dy_ref, add_ref, dx_ref, dg_ref = refs
        else:
            x_ref, g_ref, dy_ref, dx_ref, dg_ref = refs
        _init_acc(dg_ref)
        dx, dg = _rms_bwd_math(x_ref[...], g_ref[...], dy_ref[...].astype(F32))
        if with_add:
            dx = dx + add_ref[...]
        dx_ref[...] = dx.astype(out_dtype)
        dg_ref[...] += dg

    ins = [(x, "row"), (g, "full"), (dy, "row")] + ([(add, "row")] if with_add else [])
    return _rowwise(body, "rms_backward_add" if with_add else "rms_backward", rows, ins,
                    [((rows, cols), out_dtype, "row"), ((1, cols), F32, "acc")])


def _swiglu_forward(gu):
    rows = gu.shape[0]
    tc = D_FF // 2

    def body(g_ref, u_ref, o_ref):
        g = g_ref[...].astype(F32)
        o_ref[...] = (g * _sigmoid(g) * u_ref[...].astype(F32)).astype(BF)

    return pl.pallas_call(
        body, out_shape=jax.ShapeDtypeStruct((rows, D_FF), BF), grid=(rows // ROW_TILE, 2),
        in_specs=[pl.BlockSpec((ROW_TILE, tc), lambda i, j: (i, j)), pl.BlockSpec((ROW_TILE, tc), lambda i, j: (i, j + 2))],
        out_specs=pl.BlockSpec((ROW_TILE, tc), lambda i, j: (i, j)), name="swiglu_forward",
        compiler_params=_params(("parallel", "parallel")))(gu, gu)


def _swiglu_backward(gu, dact):
    rows = gu.shape[0]
    tc = D_FF // 2

    def body(g_ref, u_ref, d_ref, o_ref):
        g = g_ref[...].astype(F32)
        u = u_ref[...].astype(F32)
        d = d_ref[...].astype(F32)
        s = _sigmoid(g)

        @pl.when(pl.program_id(1) < 2)
        def _():
            o_ref[...] = (d * u * s * (1.0 + g * (1.0 - s))).astype(BF)

        @pl.when(pl.program_id(1) >= 2)
        def _():
            o_ref[...] = (d * g * s).astype(BF)

    return pl.pallas_call(
        body, out_shape=jax.ShapeDtypeStruct((rows, 2 * D_FF), BF), grid=(rows // ROW_TILE, 4),
        in_specs=[pl.BlockSpec((ROW_TILE, tc), lambda i, j: (i, j % 2)),
                  pl.BlockSpec((ROW_TILE, tc), lambda i, j: (i, j % 2 + 2)),
                  pl.BlockSpec((ROW_TILE, tc), lambda i, j: (i, j % 2))],
        out_specs=pl.BlockSpec((ROW_TILE, tc), lambda i, j: (i, j)), name="swiglu_backward",
        compiler_params=_params(("parallel", "parallel")))(gu, gu, dact)


def _rope_tables(positions):
    half = MLA_ROPE // 2
    inv = ROPE_THETA ** (-jnp.arange(half, dtype=F32) / half)
    ang = positions.astype(F32)[:, None] * inv
    cos, sin = jnp.cos(ang), jnp.sin(ang)
    rows = positions.shape[0]
    c = jnp.ones((rows, LANE), F32).at[:, 64:80].set(cos).at[:, 80:96].set(cos)
    sa = jnp.zeros((rows, LANE), F32).at[:, 64:80].set(-sin)
    sb = jnp.zeros((rows, LANE), F32).at[:, 80:96].set(sin)
    return c, sa, sb


def _rope_apply(x, c, sa, sb):
    return x * c + pltpu.roll(x, LANE - 16, 1) * sa + pltpu.roll(x, 16, 1) * sb


def _rope_apply_t(dy, c, sa, sb):
    return dy * c + pltpu.roll(dy * sa, 16, 1) + pltpu.roll(dy * sb, LANE - 16, 1)


def _rope_heads(x, tables, transpose, name):
    rows, cols = x.shape

    def body(x_ref, c_ref, sa_ref, sb_ref, o_ref):
        fn = _rope_apply_t if transpose else _rope_apply
        c, sa, sb = c_ref[...], sa_ref[...], sb_ref[...]
        for head in range(cols // LANE):
            lanes = slice(head * LANE, (head + 1) * LANE)
            o_ref[:, lanes] = fn(x_ref[:, lanes].astype(F32), c, sa, sb).astype(BF)

    blk = pl.BlockSpec((ROW_TILE, cols), lambda i: (i, 0))
    tbl = pl.BlockSpec((ROW_TILE, LANE), lambda i: (i, 0))
    return pl.pallas_call(body, out_shape=jax.ShapeDtypeStruct((rows, cols), BF), grid=(rows // ROW_TILE,),
                          in_specs=[blk, tbl, tbl, tbl], out_specs=blk, name=name,
                          compiler_params=_params(("parallel",)))(x, *tables)


def _mla_mid_forward(a, q_norm, kv_norm, tables):
    rows = a.shape[0]
    qr, kvr = MLA_Q_RANK, MLA_KV_RANK

    def body(a_ref, qn_ref, kn_ref, c_ref, sa_ref, sb_ref, cq_ref, ckv_ref, kr_ref):
        aq = a_ref[:, 0:qr]
        akv = a_ref[:, qr:qr + kvr]
        cq_ref[...] = (aq * _rstd(aq) * qn_ref[...]).astype(BF)
        ckv_ref[...] = (akv * _rstd(akv) * kn_ref[...]).astype(BF)
        kr_ref[...] = _rope_apply(a_ref[:, qr + kvr:], c_ref[...], sa_ref[...], sb_ref[...]).astype(BF)

    ins = [(a, "row"), (q_norm, "full"), (kv_norm, "full")] + [(t, "row") for t in tables]
    return _rowwise(body, "mla_mid_forward", rows, ins,
                    [((rows, qr), BF, "row"), ((rows, kvr), BF, "row"), ((rows, LANE), BF, "row")])


def _mla_mid_backward(a, q_norm, kv_norm, tables, dcq, dckv, dkr):
    rows = a.shape[0]
    qr, kvr = MLA_Q_RANK, MLA_KV_RANK

    def body(a_ref, qn_ref, kn_ref, c_ref, sa_ref, sb_ref, dcq_ref, dckv_ref, dkr_ref, da_ref, dqn_ref, dkn_ref):
        _init_acc(dqn_ref, dkn_ref)
        dxq, dgq = _rms_bwd_math(a_ref[:, 0:qr], qn_ref[...], dcq_ref[...])
        dxk, dgk = _rms_bwd_math(a_ref[:, qr:qr + kvr], kn_ref[...], dckv_ref[...])
        da_ref[:, 0:qr] = dxq.astype(BF)
        da_ref[:, qr:qr + kvr] = dxk.astype(BF)
        da_ref[:, qr + kvr:] = _rope_apply_t(dkr_ref[...], c_ref[...], sa_ref[...], sb_ref[...]).astype(BF)
        dqn_ref[...] += dgq
        dkn_ref[...] += dgk

    ins = ([(a, "row"), (q_norm, "full"), (kv_norm, "full")] + [(t, "row") for t in tables]
           + [(dcq, "row"), (dckv, "row"), (dkr, "row")])
    return _rowwise(body, "mla_mid_backward", rows, ins,
                    [((rows, MLA_A_PAD), BF, "row"), ((1, qr), F32, "acc"), ((1, kvr), F32, "acc")])


def _attn_specs(rows, kv_off):
    head = pl.BlockSpec((rows, LANE), lambda h: (0, h))
    kv_head = pl.BlockSpec((rows, LANE), lambda h: (0, h + kv_off))
    shared = pl.BlockSpec((rows, LANE), lambda h: (0, 0))
    col_vec = pl.BlockSpec((1, rows, 1), lambda h: (h, 0, 0))
    row_vec = pl.BlockSpec((1, 1, rows), lambda h: (h, 0, 0))
    return head, kv_head, shared, col_vec, row_vec


def _attn_forward(q, kv, kv_off, kr, cum_col, cum_row, scale, name):
    rows = q.shape[0]
    heads = HEADS
    t = ATTN_TILE
    nb = rows // t
    has_kr = kr is not None
    has_f = cum_col is not None

    def body(*refs):
        it = iter(refs)
        q_ref, kv_ref = next(it), next(it)
        kr_ref = next(it) if has_kr else None
        cc_ref = next(it) if has_f else None
        cr_ref = next(it) if has_f else None
        o_ref, lse_ref = next(it), next(it)
        lo = lax.broadcasted_iota(jnp.int32, (1, LANE), 1) < HEAD_DIM
        causal = (lax.broadcasted_iota(jnp.int32, (t, t), 1) <= lax.broadcasted_iota(jnp.int32, (t, t), 0))

        def q_block(i, _):
            qs = pl.ds(pl.multiple_of(i * t, t), t)
            qb = q_ref[qs, :]
            cq = cc_ref[0, qs, :] if has_f else None

            def step(j, carry, diag):
                m, l, acc = carry
                ks = pl.ds(pl.multiple_of(j * t, t), t)
                kvb = kv_ref[ks, :]
                kk = jnp.where(lo, kvb, kr_ref[ks, :] if has_kr else jnp.zeros_like(kvb))
                s = lax.dot_general(qb, kk, (((1,), (1,)), ((), ())), preferred_element_type=F32) * scale
                if has_f:
                    s = s + (cq - cr_ref[0, :, ks])
                if diag:
                    s = jnp.where(causal, s, NEG_INF)
                mn = jnp.maximum(m, jnp.max(s, axis=1, keepdims=True))
                alpha = jnp.exp(m - mn)
                p = jnp.exp(s - mn)
                l = alpha * l + jnp.sum(p, axis=1, keepdims=True)
                acc = alpha * acc + jnp.dot(p.astype(BF), kvb, preferred_element_type=F32)
                return mn, l, acc

            init = (jnp.full((t, 1), NEG_INF, F32), jnp.zeros((t, 1), F32), jnp.zeros((t, LANE), F32))
            carry = lax.fori_loop(0, i, lambda j, c: step(j, c, False), init)
            m, l, acc = step(i, carry, True)
            o_ref[qs, :] = jnp.where(lo, 0.0, acc / l).astype(BF)
            lse_ref[0, qs, :] = m + jnp.log(l)
            return 0

        lax.fori_loop(0, nb, q_block, 0)

    head, kv_head, shared, col_vec, row_vec = _attn_specs(rows, kv_off)
    in_specs, args = [head, kv_head], [q, kv]
    if has_kr:
        in_specs.append(shared)
        args.append(kr)
    if has_f:
        in_specs += [col_vec, row_vec]
        args += [cum_col, cum_row]
    return pl.pallas_call(
        body, out_shape=[jax.ShapeDtypeStruct((rows, heads * LANE), BF), jax.ShapeDtypeStruct((heads, rows, 1), F32)],
        grid=(heads,), in_specs=in_specs, out_specs=[head, col_vec], name=name,
        compiler_params=_params(("arbitrary",)))(*args)


def _attn_backward(q, kv, kv_off, kr, cum_col, cum_row, o, do, lse, scale, name):
    rows = q.shape[0]
    heads = HEADS
    t = ATTN_TILE
    nb = rows // t
    has_kr = kr is not None
    has_f = cum_col is not None

    def body(*refs):
        it = iter(refs)
        q_ref, kv_ref = next(it), next(it)
        kr_ref = next(it) if has_kr else None
        cc_ref = next(it) if has_f else None
        cr_ref = next(it) if has_f else None
        o_ref, do_ref, lse_ref = next(it), next(it), next(it)
        dq_ref, dkv_ref = next(it), next(it)
        dkr_ref = next(it) if has_kr else None
        dck_ref = next(it) if has_f else None
        dcq_ref = next(it) if has_f else None
        dq_acc = next(it)
        lo = lax.broadcasted_iota(jnp.int32, (1, LANE), 1) < HEAD_DIM
        causal = (lax.broadcasted_iota(jnp.int32, (t, t), 1) <= lax.broadcasted_iota(jnp.int32, (t, t), 0))

        dq_acc[...] = jnp.zeros_like(dq_acc)
        if has_kr:
            _init_acc(dkr_ref)
        if has_f:
            dcq_ref[...] = jnp.zeros_like(dcq_ref)

        def kv_block(j, _):
            ks = pl.ds(pl.multiple_of(j * t, t), t)
            kvb = kv_ref[ks, :]
            kk = jnp.where(lo, kvb, kr_ref[ks, :] if has_kr else jnp.zeros_like(kvb))
            ck = cr_ref[0, :, ks] if has_f else None

            def pair(i, carry, diag):
                dkk, dvv, dcs = carry
                qs = pl.ds(pl.multiple_of(i * t, t), t)
                qb = q_ref[qs, :]
                dob = do_ref[qs, :]
                s = lax.dot_general(qb, kk, (((1,), (1,)), ((), ())), preferred_element_type=F32) * scale
                if has_f:
                    s = s + (cc_ref[0, qs, :] - ck)
                if diag:
                    s = jnp.where(causal, s, NEG_INF)
                p = jnp.exp(s - lse_ref[0, qs, :])
                dp = lax.dot_general(dob, kvb, (((1,), (1,)), ((), ())), preferred_element_type=F32)
                delta = jnp.sum(dob.astype(F32) * o_ref[qs, :].astype(F32), axis=1, keepdims=True)
                ds = p * (dp - delta)
                dsb = ds.astype(BF)
                dvv = dvv + lax.dot_general(p.astype(BF), dob, (((0,), (0,)), ((), ())), preferred_element_type=F32)
                dkk = dkk + lax.dot_general(dsb, qb, (((0,), (0,)), ((), ())), preferred_element_type=F32)
                dq_acc[qs, :] += jnp.dot(dsb, kk, preferred_element_type=F32)
                if has_f:
                    dcs = dcs + jnp.sum(ds, axis=0, keepdims=True)
                    dcq_ref[0, qs, :] += jnp.sum(ds, axis=1, keepdims=True)
                return dkk, dvv, dcs

            init = (jnp.zeros((t, LANE), F32), jnp.zeros((t, LANE), F32), jnp.zeros((1, t), F32))
            carry = pair(j, init, True)
            dkk, dvv, dcs = lax.fori_loop(j + 1, nb, lambda i, c: pair(i, c, False), carry)
            dkk = dkk * scale
            dkv_ref[ks, :] = jnp.where(lo, dkk, dvv).astype(BF)
            if has_kr:
                dkr_ref[ks, :] += jnp.where(lo, 0.0, dkk)
            if has_f:
                dck_ref[0, :, ks] = -dcs
            return 0

        lax.fori_loop(0, nb, kv_block, 0)
        dq_ref[...] = (dq_acc[...] * scale).astype(BF)

    head, kv_head, shared, col_vec, row_vec = _attn_specs(rows, kv_off)
    in_specs, args = [head, kv_head], [q, kv]
    if has_kr:
        in_specs.append(shared)
        args.append(kr)
    if has_f:
        in_specs += [col_vec, row_vec]
        args += [cum_col, cum_row]
    in_specs += [head, head, col_vec]
    args += [o, do, lse]
    out_shape = [jax.ShapeDtypeStruct((rows, heads * LANE), BF), jax.ShapeDtypeStruct((rows, heads * LANE), BF)]
    out_specs = [head, head]
    if has_kr:
        out_shape.append(jax.ShapeDtypeStruct((rows, LANE), F32))
        out_specs.append(shared)
    if has_f:
        out_shape += [jax.ShapeDtypeStruct((heads, 1, rows), F32), jax.ShapeDtypeStruct((heads, rows, 1), F32)]
        out_specs += [row_vec, col_vec]
    return pl.pallas_call(
        body, out_shape=out_shape, grid=(heads,), in_specs=in_specs, out_specs=out_specs,
        scratch_shapes=[pltpu.VMEM((rows, LANE), F32)], name=name, compiler_params=_params(("arbitrary",)))(*args)


def _tri_dot(tri, x):
    return jnp.dot(tri, x, preferred_element_type=F32, precision=lax.Precision.HIGHEST)


def _forget_forward(f_raw, b_f):
    rows = f_raw.shape[0]
    t = ATTN_TILE

    def body(f_ref, b_ref, cum_ref):
        tri = (lax.broadcasted_iota(jnp.int32, (t, t), 1) <= lax.broadcasted_iota(jnp.int32, (t, t), 0)).astype(F32)

        def blk(i, carry):
            sl = pl.ds(pl.multiple_of(i * t, t), t)
            xv = f_ref[sl, :] + b_ref[...]
            log_f = jnp.minimum(xv, 0.0) - jnp.log(1.0 + jnp.exp(-jnp.abs(xv)))
            cum_ref[sl, :] = _tri_dot(tri, log_f) + carry
            return carry + jnp.sum(log_f, axis=0, keepdims=True)

        lax.fori_loop(0, rows // t, blk, jnp.zeros((1, LANE), F32))

    return pl.pallas_call(body, out_shape=jax.ShapeDtypeStruct((rows, LANE), F32), name="forget_forward",
                          compiler_params=_params())(f_raw, b_f)


def _forget_backward(f_raw, b_f, dcum):
    rows = f_raw.shape[0]
    t = ATTN_TILE
    nb = rows // t

    def body(f_ref, b_ref, dc_ref, df_ref, db_ref):
        tri = (lax.broadcasted_iota(jnp.int32, (t, t), 1) >= lax.broadcasted_iota(jnp.int32, (t, t), 0)).astype(F32)

        def blk(i, carry):
            later, db = carry
            sl = pl.ds(pl.multiple_of((nb - 1 - i) * t, t), t)
            dc = dc_ref[sl, :]
            dlog = _tri_dot(tri, dc) + later
            xv = f_ref[sl, :] + b_ref[...]
            df = dlog * _sigmoid(-xv)
            df_ref[sl, :] = df.astype(BF)
            return later + jnp.sum(dc, axis=0, keepdims=True), db + jnp.sum(df, axis=0, keepdims=True)

        _, db = lax.fori_loop(0, nb, blk, (jnp.zeros((1, LANE), F32), jnp.zeros((1, LANE), F32)))
        db_ref[...] = db

    return pl.pallas_call(body, out_shape=[jax.ShapeDtypeStruct((rows, LANE), BF), jax.ShapeDtypeStruct((1, LANE), F32)],
                          name="forget_backward", compiler_params=_params())(f_raw, b_f, dcum)


def _t5_bucket(dist):
    max_exact = REL_BUCKETS // 2
    n = jnp.maximum(dist.astype(F32), 1.0)
    large = max_exact + (jnp.log(n / max_exact) / math.log(REL_MAX_DIST / max_exact)
                         * (REL_BUCKETS - max_exact)).astype(jnp.int32)
    large = jnp.minimum(large, REL_BUCKETS - 1)
    return jnp.where(dist < max_exact, dist, large)


def _dil_buckets(dilation):
    i = jnp.arange(Q_BLOCK)[:, None]
    j = jnp.arange(Q_BLOCK)[None, :]
    cur = _t5_bucket(jnp.clip(i - j, 0) * dilation).astype(jnp.int32)
    prev = _t5_bucket(jnp.clip(Q_BLOCK + i - j, 0) * dilation).astype(jnp.int32)
    return cur, prev


def _dil_bias_tiles(tbl_ref, bc_ref, bp_ref, bias_ref, group, hp):
    for hh in range(2):
        col = group * HEADS + 2 * hp + hh
        acc_c = jnp.zeros((Q_BLOCK, Q_BLOCK), F32)
        acc_p = jnp.zeros((Q_BLOCK, Q_BLOCK), F32)
        for b in range(REL_BUCKETS):
            val = tbl_ref[b, col]
            acc_c = jnp.where(bc_ref[...] == b, val, acc_c)
            acc_p = jnp.where(bp_ref[...] == b, val, acc_p)
        bias_ref[2 * hh] = acc_c
        bias_ref[2 * hh + 1] = acc_p


def _dil_specs(group, dilation, length):
    def col(kind):
        return pl.BlockSpec((length, LANE), lambda hp, r: (0, r * 72 + (group * 3 + kind) * 8 + hp))

    out = pl.BlockSpec((length, LANE), lambda hp, r: (0, r * 8 + hp))
    tile = pl.BlockSpec((Q_BLOCK, Q_BLOCK), lambda hp, r: (0, 0))
    table = pl.BlockSpec(memory_space=pltpu.SMEM)
    return col, out, tile, table


def _dil_forward(qkv, group, dilation, table, buckets):
    rows = qkv.shape[0]
    length = rows // dilation
    nb = length // Q_BLOCK
    scale = HEAD_DIM ** -0.5
    qb = Q_BLOCK

    def body(tbl_ref, bc_ref, bp_ref, q_ref, k_ref, v_ref, o_ref, lse_ref, bias_ref):
        hp = pl.program_id(0)

        @pl.when(pl.program_id(1) == 0)
        def _():
            _dil_bias_tiles(tbl_ref, bc_ref, bp_ref, bias_ref, group, hp)

        lo = lax.broadcasted_iota(jnp.int32, (1, LANE), 1) < HEAD_DIM
        ii = lax.broadcasted_iota(jnp.int32, (qb, qb), 0)
        jj = lax.broadcasted_iota(jnp.int32, (qb, qb), 1)

        def blk(n, _):
            cur = pl.ds(pl.multiple_of(n * qb, qb), qb)
            prev = pl.ds(pl.multiple_of(jnp.maximum(n - 1, 0) * qb, qb), qb)
            qn = q_ref[cur, :]
            kc, kp, vc, vp = k_ref[cur, :], k_ref[prev, :], v_ref[cur, :], v_ref[prev, :]
            ok_c = jj <= ii
            ok_p = (jj >= ii) & (n > 0)
            outs, lses = [], []
            for hh in range(2):
                qm = jnp.where(lo if hh == 0 else ~lo, qn, jnp.zeros_like(qn))
                s_c = lax.dot_general(qm, kc, (((1,), (1,)), ((), ())), preferred_element_type=F32) * scale
                s_p = lax.dot_general(qm, kp, (((1,), (1,)), ((), ())), preferred_element_type=F32) * scale
                s_c = jnp.where(ok_c, s_c + bias_ref[2 * hh], NEG_INF)
                s_p = jnp.where(ok_p, s_p + bias_ref[2 * hh + 1], NEG_INF)
                m = jnp.maximum(jnp.max(s_c, axis=1, keepdims=True), jnp.max(s_p, axis=1, keepdims=True))
                e_c = jnp.exp(s_c - m)
                e_p = jnp.exp(s_p - m)
                l = jnp.sum(e_c, axis=1, keepdims=True) + jnp.sum(e_p, axis=1, keepdims=True)
                acc = (jnp.dot(e_c.astype(BF), vc, preferred_element_type=F32)
                       + jnp.dot(e_p.astype(BF), vp, preferred_element_type=F32))
                outs.append(acc / l)
                lses.append(m + jnp.log(l))
            o_ref[cur, :] = jnp.where(lo, outs[0], outs[1])
            lse_ref[cur, :] = jnp.where(lo, lses[0], lses[1])
            return 0

        lax.fori_loop(0, nb, blk, 0)

    col, out, tile, tbl = _dil_specs(group, dilation, length)
    bc, bp = buckets
    o, lse = pl.pallas_call(
        body, out_shape=[jax.ShapeDtypeStruct((length, dilation * D_MODEL), F32)] * 2, grid=(8, dilation),
        in_specs=[tbl, tile, tile, col(0), col(1), col(2)], out_specs=[out, out],
        scratch_shapes=[pltpu.VMEM((4, qb, qb), F32)], name=f"dilated_forward_{dilation}",
        compiler_params=_params(("arbitrary", "arbitrary")))(
            table, bc, bp, *([qkv.reshape(length, dilation * qkv.shape[1])] * 3))
    return o.reshape(rows, D_MODEL), lse.reshape(rows, D_MODEL)


def _dil_backward(qkv, group, dilation, table, buckets, do_g, lse, dlt):
    rows = qkv.shape[0]
    length = rows // dilation
    nb = length // Q_BLOCK
    scale = HEAD_DIM ** -0.5
    qb = Q_BLOCK

    def body(tbl_ref, bc_ref, bp_ref, q_ref, k_ref, v_ref, do_ref, lse_ref, dlt_ref,
             dq_ref, dk_ref, dv_ref, db_ref, bias_ref, dk_acc, dv_acc):
        hp = pl.program_id(0)

        @pl.when(pl.program_id(1) == 0)
        def _():
            _dil_bias_tiles(tbl_ref, bc_ref, bp_ref, bias_ref, group, hp)
            db_ref[...] = jnp.zeros_like(db_ref)

        dk_acc[...] = jnp.zeros_like(dk_acc)
        dv_acc[...] = jnp.zeros_like(dv_acc)
        lo = lax.broadcasted_iota(jnp.int32, (1, LANE), 1) < HEAD_DIM
        ii = lax.broadcasted_iota(jnp.int32, (qb, qb), 0)
        jj = lax.broadcasted_iota(jnp.int32, (qb, qb), 1)
        tn = (((0,), (0,)), ((), ()))
        nt = (((1,), (1,)), ((), ()))

        def blk(n, _):
            cur = pl.ds(pl.multiple_of(n * qb, qb), qb)
            prev = pl.ds(pl.multiple_of(jnp.maximum(n - 1, 0) * qb, qb), qb)
            qn = q_ref[cur, :]
            don = do_ref[cur, :]
            kc, kp, vc, vp = k_ref[cur, :], k_ref[prev, :], v_ref[cur, :], v_ref[prev, :]
            lse_n = lse_ref[cur, :]
            dlt_n = dlt_ref[cur, :]
            ok_c = jj <= ii
            ok_p = (jj >= ii) & (n > 0)
            dqs = []
            dkc = jnp.zeros((qb, LANE), F32)
            dkp = jnp.zeros((qb, LANE), F32)
            dvc = jnp.zeros((qb, LANE), F32)
            dvp = jnp.zeros((qb, LANE), F32)
            for hh in range(2):
                mask = lo if hh == 0 else ~lo
                qm = jnp.where(mask, qn, jnp.zeros_like(qn))
                dom = jnp.where(mask, don, jnp.zeros_like(don))
                lse_h = jnp.max(jnp.where(mask, lse_n, -3e38), axis=1, keepdims=True)
                dlt_h = jnp.max(jnp.where(mask, dlt_n, -3e38), axis=1, keepdims=True)
                s_c = lax.dot_general(qm, kc, nt, preferred_element_type=F32) * scale
                s_p = lax.dot_general(qm, kp, nt, preferred_element_type=F32) * scale
                p_c = jnp.exp(jnp.where(ok_c, s_c + bias_ref[2 * hh], NEG_INF) - lse_h)
                p_p = jnp.exp(jnp.where(ok_p, s_p + bias_ref[2 * hh + 1], NEG_INF) - lse_h)
                ds_c = p_c * (lax.dot_general(dom, vc, nt, preferred_element_type=F32) - dlt_h)
                ds_p = p_p * (lax.dot_general(dom, vp, nt, preferred_element_type=F32) - dlt_h)
                db_ref[0, 2 * hh] += ds_c
                db_ref[0, 2 * hh + 1] += ds_p
                dsc_b, dsp_b = ds_c.astype(BF), ds_p.astype(BF)
                dqs.append(jnp.dot(dsc_b, kc, preferred_element_type=F32)
                           + jnp.dot(dsp_b, kp, preferred_element_type=F32))
                dkc = dkc + lax.dot_general(dsc_b, qm, tn, preferred_element_type=F32)
                dkp = dkp + lax.dot_general(dsp_b, qm, tn, preferred_element_type=F32)
                dvc = dvc + lax.dot_general(p_c.astype(BF), dom, tn, preferred_element_type=F32)
                dvp = dvp + lax.dot_general(p_p.astype(BF), dom, tn, preferred_element_type=F32)
            dq_ref[cur, :] = (jnp.where(lo, dqs[0], dqs[1]) * scale).astype(BF)
            dk_acc[cur, :] += dkc
            dk_acc[prev, :] += dkp
            dv_acc[cur, :] += dvc
            dv_acc[prev, :] += dvp
            return 0

        lax.fori_loop(0, nb, blk, 0)
        dk_ref[...] = (dk_acc[...] * scale).astype(BF)
        dv_ref[...] = dv_acc[...].astype(BF)

    col, out, tile, tbl = _dil_specs(group, dilation, length)
    bc, bp = buckets
    wide = (length, dilation * D_MODEL)
    dq, dk, dv, db = pl.pallas_call(
        body, out_shape=[jax.ShapeDtypeStruct(wide, BF)] * 3 + [jax.ShapeDtypeStruct((8, 4, qb, qb), F32)],
        grid=(8, dilation), in_specs=[tbl, tile, tile, col(0), col(1), col(2), out, out, out],
        out_specs=[out, out, out, pl.BlockSpec((1, 4, qb, qb), lambda hp, r: (hp, 0, 0, 0))],
        scratch_shapes=[pltpu.VMEM((4, qb, qb), F32), pltpu.VMEM((length, LANE), F32), pltpu.VMEM((length, LANE), F32)],
        name=f"dilated_backward_{dilation}", compiler_params=_params(("arbitrary", "arbitrary")))(
            table, bc, bp, *([qkv.reshape(length, dilation * qkv.shape[1])] * 3),
            do_g.reshape(wide), lse.reshape(wide), dlt.reshape(wide))
    return dq.reshape(rows, D_MODEL), dk.reshape(rows, D_MODEL), dv.reshape(rows, D_MODEL), db


def _head_sums(x, lo):
    s0 = jnp.sum(jnp.where(lo, x, 0.0), axis=1, keepdims=True)
    s1 = jnp.sum(jnp.where(lo, 0.0, x), axis=1, keepdims=True)
    return jnp.where(lo, s0, s1)


def _dil_merge_forward(outs, lses):
    rows = outs[0].shape[0]

    def body(o0, o1, o2, l0, l1, l2, o_ref):
        ls = [l0[...], l1[...], l2[...]]
        m = jnp.maximum(jnp.maximum(ls[0], ls[1]), ls[2])
        es = [jnp.exp(v - m) for v in ls]
        tot = es[0] + es[1] + es[2]
        o_ref[...] = ((es[0] * o0[...] + es[1] * o1[...] + es[2] * o2[...]) / tot).astype(BF)

    blk = pl.BlockSpec((ROW_TILE, LANE), lambda i, j: (i, j))
    return pl.pallas_call(body, out_shape=jax.ShapeDtypeStruct((rows, D_MODEL), BF), grid=(rows // ROW_TILE, 8),
                          in_specs=[blk] * 6, out_specs=blk, name="dilated_merge_forward",
                          compiler_params=_params(("parallel", "parallel")))(*outs, *lses)


def _dil_merge_backward(outs, lses, do):
    rows = outs[0].shape[0]

    def body(o0, o1, o2, l0, l1, l2, do_ref, d0, d1, d2, t0, t1, t2):
        lo = lax.broadcasted_iota(jnp.int32, (1, LANE), 1) < HEAD_DIM
        ls = [l0[...], l1[...], l2[...]]
        os_ = [o0[...], o1[...], o2[...]]
        m = jnp.maximum(jnp.maximum(ls[0], ls[1]), ls[2])
        es = [jnp.exp(v - m) for v in ls]
        tot = es[0] + es[1] + es[2]
        alphas = [e / tot for e in es]
        dov = do_ref[...]
        merged = alphas[0] * os_[0] + alphas[1] * os_[1] + alphas[2] * os_[2]
        dot = _head_sums(dov * merged, lo)
        for a, d_ref, t_ref in zip(alphas, (d0, d1, d2), (t0, t1, t2)):
            d_ref[...] = (a * dov).astype(BF)
            t_ref[...] = a * dot

    blk = pl.BlockSpec((ROW_TILE, LANE), lambda i, j: (i, j))
    res = pl.pallas_call(
        body, out_shape=[jax.ShapeDtypeStruct((rows, D_MODEL), BF)] * 3 + [jax.ShapeDtypeStruct((rows, D_MODEL), F32)] * 3,
        grid=(rows // ROW_TILE, 8), in_specs=[blk] * 7, out_specs=[blk] * 6, name="dilated_merge_backward",
        compiler_params=_params(("parallel", "parallel")))(*outs, *lses, do)
    return res[:3], res[3:]


def _rel_bias_grad(dbs, buckets):
    def body(db_ref, bc_ref, bp_ref, o_ref):
        g = pl.program_id(0)
        hp = pl.program_id(1)

        @pl.when((g == 0) & (hp == 0))
        def _():
            o_ref[...] = jnp.zeros_like(o_ref)

        rr = lax.broadcasted_iota(jnp.int32, (REL_BUCKETS, LANE), 0)
        cc = lax.broadcasted_iota(jnp.int32, (REL_BUCKETS, LANE), 1)
        bc = bc_ref[0]
        bp = bp_ref[0]
        acc = jnp.zeros((REL_BUCKETS, LANE), F32)
        for hh in range(2):
            col = g * HEADS + 2 * hp + hh
            d_c = db_ref[0, 0, 2 * hh]
            d_p = db_ref[0, 0, 2 * hh + 1]
            for b in range(REL_BUCKETS):
                val = (jnp.sum(jnp.where(bc == b, d_c, 0.0), keepdims=True)
                       + jnp.sum(jnp.where(bp == b, d_p, 0.0), keepdims=True))
                acc = jnp.where((rr == b) & (cc == col), val, acc)
        o_ref[...] += acc

    db_all = jnp.stack(dbs)
    bc_all = jnp.stack([b[0] for b in buckets])
    bp_all = jnp.stack([b[1] for b in buckets])
    tile = pl.BlockSpec((1, Q_BLOCK, Q_BLOCK), lambda g, hp: (g, 0, 0))
    return pl.pallas_call(
        body, out_shape=jax.ShapeDtypeStruct((REL_BUCKETS, LANE), F32), grid=(3, 8),
        in_specs=[pl.BlockSpec((1, 1, 4, Q_BLOCK, Q_BLOCK), lambda g, hp: (g, hp, 0, 0, 0)), tile, tile],
        out_specs=pl.BlockSpec((REL_BUCKETS, LANE), lambda g, hp: (0, 0)), name="rel_bias_grad",
        compiler_params=_params(("arbitrary", "arbitrary")))(db_all, bc_all, bp_all)


def _mla_forward(hn, w, tables):
    a = _matmul(hn, w["w_a"], name="mla_a")
    cq, ckv, kr = _mla_mid_forward(a, w["q_norm"], w["kv_norm"], tables)
    q_raw = _matmul(cq, w["w_uq"], name="mla_uq")
    q = _rope_heads(q_raw, tables, False, "rope_forward")
    kv = _matmul(ckv, w["w_ukv"], b_chunks=True, out_dtype=BF, name="mla_ukv")
    scale = (HEAD_DIM + MLA_ROPE) ** -0.5
    o, lse = _attn_forward(q, kv, 0, kr, None, None, scale, "mla_attention_forward")
    y = _matmul(o, w["w_o"], name="attn_out")
    return y, dict(hn=hn, a=a, cq=cq, ckv=ckv, kr=kr, q=q, kv=kv, o=o, lse=lse)


def _mla_backward(dy, w, s, tables):
    scale = (HEAD_DIM + MLA_ROPE) ** -0.5
    g = {}
    g["w_o"] = _matmul(s["o"], dy, ta=True, out_dtype=BF, name="attn_out_dw")
    do = _matmul(dy, w["w_o"], tb=True, out_dtype=BF, name="attn_out_dx")
    dq, dkv, dkr = _attn_backward(s["q"], s["kv"], 0, s["kr"], None, None, s["o"], do, s["lse"], scale,
                                  "mla_attention_backward")
    dq_raw = _rope_heads(dq, tables, True, "rope_backward")
    g["w_uq"] = _matmul(s["cq"], dq_raw, ta=True, out_dtype=BF, name="mla_uq_dw")
    dcq = _matmul(dq_raw, w["w_uq"], tb=True, name="mla_uq_dx")
    g["w_ukv"] = _matmul(s["ckv"], dkv, ta=True, out_chunks=True, out_dtype=BF, name="mla_ukv_dw")
    dckv = _matmul(dkv, w["w_ukv"], tb=True, b_chunks=True, name="mla_ukv_dx")
    da, g["q_norm"], g["kv_norm"] = _mla_mid_backward(s["a"], w["q_norm"], w["kv_norm"], tables, dcq, dckv, dkr)
    g["w_a"] = _matmul(s["hn"], da, ta=True, out_dtype=BF, name="mla_a_dw")
    dhn = _matmul(da, w["w_a"], tb=True, name="mla_a_dx")
    return dhn, g


def _fox_forward(hn, w):
    qkv = _matmul(hn, w["w_qkv"], out_dtype=BF, name="fox_qkv")
    f_raw = _matmul(hn, w["w_f"], name="fox_f")
    cum = _forget_forward(f_raw, w["b_f"])
    cum_heads = cum[:, :HEADS].T
    cum_col, cum_row = cum_heads[:, :, None], cum_heads[:, None, :]
    o, lse = _attn_forward(qkv, qkv, HEADS, None, cum_col, cum_row, HEAD_DIM ** -0.5, "fox_attention_forward")
    y = _matmul(o, w["w_o"], name="attn_out")
    return y, dict(hn=hn, qkv=qkv, f_raw=f_raw, cum_col=cum_col, cum_row=cum_row, o=o, lse=lse)


def _fox_backward(dy, w, s):
    g = {}
    g["w_o"] = _matmul(s["o"], dy, ta=True, out_dtype=BF, name="attn_out_dw")
    do = _matmul(dy, w["w_o"], tb=True, out_dtype=BF, name="attn_out_dx")
    dq, dkv, dck, dcq = _attn_backward(s["qkv"], s["qkv"], HEADS, None, s["cum_col"], s["cum_row"], s["o"], do,
                                       s["lse"], HEAD_DIM ** -0.5, "fox_attention_backward")
    dcum = jnp.pad((dck[:, 0, :] + dcq[:, :, 0]).T, ((0, 0), (0, LANE - HEADS)))
    df, g["b_f"] = _forget_backward(s["f_raw"], w["b_f"], dcum)
    dqkv = jnp.concatenate([dq, dkv], axis=1)
    g["w_qkv"] = _matmul(s["hn"], dqkv, ta=True, out_dtype=BF, name="fox_qkv_dw")
    g["w_f"] = _matmul(s["hn"], df, ta=True, out_dtype=BF, name="fox_f_dw")
    dhn = _matmul(dqkv, w["w_qkv"], tb=True, name="fox_qkv_dx")
    dhn = _matmul(df, w["w_f"], tb=True, add=dhn, name="fox_f_dx")
    return dhn, g


def _dil_mixer_forward(hn, w, buckets):
    qkv = _matmul(hn, w["w_qkv"], b_chunks=True, out_dtype=BF, name="dil_qkv")
    outs, lses = [], []
    for grp, (_, dilation) in enumerate(DIL_PATTERNS):
        o_g, lse_g = _dil_forward(qkv, grp, dilation, w["rel_bias"], buckets[grp])
        outs.append(o_g)
        lses.append(lse_g)
    o = _dil_merge_forward(outs, lses)
    y = _matmul(o, w["w_o"], name="dil_out")
    return y, dict(hn=hn, qkv=qkv, outs=outs, lses=lses, o=o)


def _dil_mixer_backward(dy, w, s, buckets):
    g = {}
    g["w_o"] = _matmul(s["o"], dy, ta=True, out_dtype=BF, name="dil_out_dw")
    do = _matmul(dy, w["w_o"], tb=True, name="dil_out_dx")
    do_gs, dlts = _dil_merge_backward(s["outs"], s["lses"], do)
    parts, dbs = [], []
    for grp, (_, dilation) in enumerate(DIL_PATTERNS):
        dq, dk, dv, db = _dil_backward(s["qkv"], grp, dilation, w["rel_bias"], buckets[grp], do_gs[grp],
                                       s["lses"][grp], dlts[grp])
        parts += [dq, dk, dv]
        dbs.append(db)
    dqkv = jnp.concatenate(parts, axis=1)
    g["rel_bias"] = _rel_bias_grad(dbs, buckets)
    g["w_qkv"] = _matmul(s["hn"], dqkv, ta=True, out_chunks=True, out_dtype=BF, name="dil_qkv_dw")
    dhn = _matmul(dqkv, w["w_qkv"], tb=True, b_chunks=True, name="dil_qkv_dx")
    return dhn, g


def _mixer_weights(i, lw, small):
    mixer, j = i % N_MIXERS, i // N_MIXERS
    if mixer == 0:
        return dict(lw["mixer"], q_norm=small["mla_q_norm"][j][None, :], kv_norm=small["mla_kv_norm"][j][None, :])
    if mixer == 1:
        return dict(lw["mixer"], rel_bias=small["rel_bias"])
    return dict(lw["mixer"], b_f=jnp.pad(small["fox_b_f"][j][None, :], ((0, 0), (0, LANE - HEADS))))


MIXER_PART, COMMON_PART = 0, 1


def _run_layers(x, p, positions, target, get_part, get_small, put_part):
    tables = _rope_tables(positions)
    buckets = [_dil_buckets(d) for _, d in DIL_PATTERNS]
    layers, saved = [], []
    h = x
    first = get_part(0, MIXER_PART, positions)
    small = get_small()

    def gain(i, k):
        return small["norm_g"][i, k][None, :]

    hn = _prenorm(h, gain(0, 0))
    sq = dh = None
    for i in range(DEPTH):
        mixer = i % N_MIXERS
        lw = dict(mixer=first if i == 0 else get_part(i, MIXER_PART, h))
        mw = _mixer_weights(i, lw, small)
        if mixer == 0:
            y, ms = _mla_forward(hn, mw, tables)
        elif mixer == 1:
            y, ms = _dil_mixer_forward(hn, mw, buckets)
        else:
            y, ms = _fox_forward(hn, mw)
        lw.update(get_part(i, COMMON_PART, y))
        layers.append(lw)
        h1, hn2 = _post_residual(h, y, gain(i, 1), gain(i, 2))
        gu = _matmul(hn2, lw["ffn_w_in"], b_chunks=True, out_dtype=BF, name="ffn_in")
        act = _swiglu_forward(gu)
        f = _matmul(act, lw["ffn_w_out"], name="ffn_out")
        h2, h2b = _post_residual(h1, f, gain(i, 3), None)
        pp = _matmul(p[i], lw["ple_w_proj"], b_chunks=True, name="ple_proj")
        z = _matmul(h2b, lw["ple_w_gate"], name="ple_gate")
        saved.append(dict(h=h, y=y, ms=ms, h1=h1, hn2=hn2, gu=gu, act=act, f=f, h2b=h2b, pp=pp, z=z))
        if i + 1 < DEPTH:
            h, hn = _ple_forward(h2, pp, z, gain(i + 1, 0))
        else:
            dh, sq = _ple_loss(h2, pp, z, target)

    norm_rows = [[None] * 4 for _ in range(DEPTH)]
    sg = dict(mla_q_norm={}, mla_kv_norm={}, rel_bias=None, fox_b_f={})
    for i in reversed(range(DEPTH)):
        s, lw = saved[i], layers[i]
        mixer, j = i % N_MIXERS, i // N_MIXERS
        mw = _mixer_weights(i, lw, small)
        lg = {}
        dpp, dz = _ple_backward(dh, s["pp"], s["z"])
        lg["ple_w_proj"] = _matmul(p[i], dpp, ta=True, out_chunks=True, out_dtype=BF, name="ple_proj_dw")
        lg["ple_w_gate"] = _matmul(s["h2b"], dz, ta=True, out_dtype=BF, name="ple_gate_dw")
        dh2 = _matmul(dz, lw["ple_w_gate"], tb=True, add=dh, name="ple_gate_dx")
        df, norm_rows[i][3] = _rms_backward(s["f"], gain(i, 3), dh2, None, BF)
        lg["ffn_w_out"] = _matmul(s["act"], df, ta=True, out_dtype=BF, name="ffn_out_dw")
        dact = _matmul(df, lw["ffn_w_out"], tb=True, out_dtype=BF, name="ffn_out_dx")
        dgu = _swiglu_backward(s["gu"], dact)
        lg["ffn_w_in"] = _matmul(s["hn2"], dgu, ta=True, out_chunks=True, out_dtype=BF, name="ffn_in_dw")
        token = put_part(i, COMMON_PART, lg)
        dhn2 = _matmul(dgu, lw["ffn_w_in"], tb=True, b_chunks=True, name="ffn_in_dx")
        dh1, norm_rows[i][2] = _rms_backward(s["h1"], gain(i, 2), dhn2, dh2, F32)
        dy, norm_rows[i][1] = _rms_backward(s["y"], gain(i, 1) + token[0:1, 0:1], dh1, None, BF)
        if mixer == 0:
            dhn, mg = _mla_backward(dy, mw, s["ms"], tables)
            sg["mla_q_norm"][j] = mg.pop("q_norm")
            sg["mla_kv_norm"][j] = mg.pop("kv_norm")
        elif mixer == 1:
            dhn, mg = _dil_mixer_backward(dy, mw, s["ms"], buckets)
            rel = mg.pop("rel_bias")[:, :3 * HEADS]
            sg["rel_bias"] = rel if sg["rel_bias"] is None else sg["rel_bias"] + rel
        else:
            dhn, mg = _fox_backward(dy, mw, s["ms"])
            sg["fox_b_f"][j] = mg.pop("b_f")[:, :HEADS]
        token = put_part(i, MIXER_PART, mg)
        dh, norm_rows[i][0] = _rms_backward(s["h"], gain(i, 0) + token[0:1, 0:1], dhn, dh1, F32)
    small_grads = dict(norm_g=jnp.stack([jnp.concatenate(row, axis=0) for row in norm_rows]),
                       rel_bias=sg["rel_bias"])
    for k in ("mla_q_norm", "mla_kv_norm", "fox_b_f"):
        small_grads[k] = jnp.concatenate([sg[k][j] for j in sorted(sg[k])], axis=0)
    return sq, dh, small_grads


COL_SHARDED = ("ffn_w_in", "ple_w_proj", "mla_w_uq", "mla_w_ukv", "dil_w_qkv", "fox_w_qkvf")
ROW_SHARDED = ("ffn_w_out", "ple_w_gate", "mla_w_a", "mla_w_o", "dil_w_o", "fox_w_o")
BIG = ("ffn_w_in", "ffn_w_out", "ple_w_proj", "ple_w_gate", "mla_w_a", "mla_w_uq", "mla_w_ukv", "mla_w_o",
       "dil_w_qkv", "dil_w_o", "fox_w_qkvf", "fox_w_o")
SMALL_SHARDED = ("norm_g", "mla_q_norm", "mla_kv_norm")
SMALL_REPLICATED = ("rel_bias", "fox_b_f")
WEIGHTS = ("norm_g", "ffn_w_in", "ffn_w_out", "ple_w_proj", "ple_w_gate", "rel_bias", "mla_w_a", "mla_q_norm",
           "mla_kv_norm", "mla_w_uq", "mla_w_ukv", "mla_w_o", "dil_w_qkv", "dil_w_o", "fox_w_qkvf", "fox_b_f", "fox_w_o")


LAYER_COMMON = ("ffn_w_in", "ffn_w_out", "ple_w_proj", "ple_w_gate")
MIXER_WEIGHTS = (("mla_w_a", "mla_w_uq", "mla_w_ukv", "mla_w_o"), ("dil_w_qkv", "dil_w_o"), ("fox_w_qkvf", "fox_w_o"))


def _part_names(i, part):
    return MIXER_WEIGHTS[i % N_MIXERS] if part == MIXER_PART else LAYER_COMMON


def _layer_slot(name, i):
    return i if name in LAYER_COMMON else i // N_MIXERS


def _merge_rows(chunks):
    n, r, c = chunks.shape
    return chunks.reshape(n * r, c)


def _merge_cols(chunks):
    n, r, c = chunks.shape
    return chunks.transpose(1, 0, 2).reshape(r, n * c)


def _pad_heads_out(wo):
    w3 = wo.reshape(HEADS, HEAD_DIM, D_MODEL)
    return jnp.pad(w3, ((0, 0), (HEAD_DIM, 0), (0, 0))).reshape(HEADS * LANE, D_MODEL)


def _part_to_compute(i, part, ch):
    if part == COMMON_PART:
        return dict(ffn_w_in=ch["ffn_w_in"], ffn_w_out=_merge_rows(ch["ffn_w_out"]), ple_w_proj=ch["ple_w_proj"],
                    ple_w_gate=_merge_rows(ch["ple_w_gate"]))
    lw = {}
    mixer = i % N_MIXERS
    if mixer == 0:
        wa = _merge_rows(ch["mla_w_a"])
        rank = MLA_Q_RANK + MLA_KV_RANK
        wa_p = jnp.concatenate([wa[:, :rank], jnp.zeros((wa.shape[0], 64), wa.dtype), wa[:, rank:],
                                jnp.zeros((wa.shape[0], 32), wa.dtype)], axis=1)
        wuq = _merge_cols(ch["mla_w_uq"]).reshape(MLA_Q_RANK, HEADS, HEAD_DIM + MLA_ROPE)
        wuq_p = jnp.pad(wuq, ((0, 0), (0, 0), (0, LANE - HEAD_DIM - MLA_ROPE))).reshape(MLA_Q_RANK, HEADS * LANE)
        lw["mixer"] = dict(w_a=wa_p, w_uq=wuq_p, w_ukv=ch["mla_w_ukv"], w_o=_pad_heads_out(_merge_rows(ch["mla_w_o"])))
    elif mixer == 1:
        lw["mixer"] = dict(w_qkv=ch["dil_w_qkv"], w_o=_merge_rows(ch["dil_w_o"]))
    else:
        wf = _merge_cols(ch["fox_w_qkvf"])
        inner = HEADS * HEAD_DIM
        q3 = wf[:, :inner].reshape(D_MODEL, HEADS, HEAD_DIM)
        k3 = wf[:, inner:2 * inner].reshape(D_MODEL, HEADS, HEAD_DIM)
        v3 = wf[:, 2 * inner:3 * inner].reshape(D_MODEL, HEADS, HEAD_DIM)
        q_p = jnp.pad(q3, ((0, 0), (0, 0), (0, HEAD_DIM))).reshape(D_MODEL, HEADS * LANE)
        kv_p = jnp.concatenate([k3, v3], axis=2).reshape(D_MODEL, HEADS * LANE)
        f_p = jnp.pad(wf[:, 3 * inner:], ((0, 0), (0, LANE - HEADS)))
        lw["mixer"] = dict(w_qkv=jnp.concatenate([q_p, kv_p], axis=1), w_f=f_p,
                           w_o=_pad_heads_out(_merge_rows(ch["fox_w_o"])))
    return lw["mixer"]


def _part_contributions(i, part, lg, chunk_shapes):
    spec = {k: jax.ShapeDtypeStruct(s, BF) for k, s in chunk_shapes.items()}
    (contrib,) = jax.linear_transpose(functools.partial(_part_to_compute, i, part), spec)(lg)
    return contrib


def _chip_peers():
    x, y, c = lax.axis_index("x"), lax.axis_index("y"), lax.axis_index("c")
    peers = [(1 - x, y), (x, 1 - y), (1 - x, 1 - y)]
    return x, y, c, peers


SEM_SPEC = pl.BlockSpec(memory_space=pltpu.SEMAPHORE)
ANY_SPEC = pl.BlockSpec(memory_space=pl.ANY)
SPLIT_EFFECT = pltpu.SideEffectType.DATAFLOW_SIDE_EFFECTING


def _spread_copy(src, land, exchange, k, peer, c, send_sems, recv_sems, index, slot):
    px, py = peer
    return pltpu.make_async_remote_copy(
        src_ref=src.at[2 * px + py] if exchange else src, dst_ref=land.at[slot],
        send_sem=send_sems.at[3 * index + k], recv_sem=recv_sems.at[3 * index + k],
        device_id=(px, py, c), device_id_type=MESH)


def _spread_start(srcs, exchange, after, name):
    n = len(srcs)
    lands = [lax.empty(s.shape if exchange else (N_CHIPS,) + s.shape, s.dtype) for s in srcs]

    def body(*refs):
        src, land = refs[:n], refs[n:2 * n]
        send_sems, recv_sems = refs[2 * n + 1], refs[2 * n + 2]
        token = refs[-1]
        x, y, c, peers = _chip_peers()
        me = 2 * x + y
        for w in range(n):
            for k, peer in enumerate(peers):
                _spread_copy(src[w], land[w], exchange, k, peer, c, send_sems, recv_sems, w, me).start()
        token[...] = jnp.zeros_like(token)

    hbm = [pltpu.with_memory_space_constraint(a, pltpu.HBM) for a in list(srcs) + lands]
    out = pl.pallas_call(
        body, name=name,
        out_shape=(pltpu.SemaphoreType.DMA((3 * n,)), pltpu.SemaphoreType.DMA((3 * n,)),
                   *[pltpu.HBM(a.shape, a.dtype) for a in hbm], jax.ShapeDtypeStruct((8, LANE), F32)),
        in_specs=[HBM_SPEC] * (2 * n) + [ANY_SPEC],
        out_specs=(SEM_SPEC, SEM_SPEC, *[HBM_SPEC] * (2 * n), pl.BlockSpec(memory_space=pltpu.VMEM)),
        input_output_aliases={w: 2 + w for w in range(2 * n)},
        compiler_params=pltpu.CompilerParams(has_side_effects=SPLIT_EFFECT))(*hbm, after)
    return dict(send=out[0], recv=out[1], srcs=out[2:2 + n], lands=out[2 + n:2 + 2 * n], token=out[-1],
                exchange=exchange)


def _spread_wait(handle, after, name):
    n = len(handle["srcs"])
    exchange = handle["exchange"]

    def body(*refs):
        src, land = refs[:n], refs[n:2 * n]
        send_sems, recv_sems = refs[2 * n], refs[2 * n + 1]
        _, _, c, peers = _chip_peers()
        for w in range(n):
            for k, peer in enumerate(peers):
                cp = _spread_copy(src[w], land[w], exchange, k, peer, c, send_sems, recv_sems, w, 2 * peer[0] + peer[1])
                cp.wait_send()
                cp.wait_recv()

    arrays = list(handle["srcs"]) + list(handle["lands"])
    out = pl.pallas_call(
        body, name=name, out_shape=tuple(pltpu.HBM(a.shape, a.dtype) for a in arrays),
        in_specs=[HBM_SPEC] * (2 * n) + [SEM_SPEC, SEM_SPEC, ANY_SPEC], out_specs=tuple([HBM_SPEC] * (2 * n)),
        input_output_aliases={w: w for w in range(2 * n)},
        compiler_params=pltpu.CompilerParams(has_side_effects=SPLIT_EFFECT))(*arrays, handle["send"], handle["recv"], after)
    me = 2 * lax.axis_index("x") + lax.axis_index("y")
    filled = []
    for src, land in zip(out[:n], out[n:]):
        own = lax.dynamic_index_in_dim(src, me, 0, keepdims=True) if exchange else src[None]
        filled.append(lax.dynamic_update_index_in_dim(land, own, me, 0))
    return filled


def _exchange_sibling(arrays, name):
    n = len(arrays)

    def body(*refs):
        ins, outs = refs[:n], refs[n:2 * n]
        send_sems, recv_sems = refs[2 * n:]
        x, y, c = lax.axis_index("x"), lax.axis_index("y"), lax.axis_index("c")
        copies = [pltpu.make_async_remote_copy(src_ref=ins[w], dst_ref=outs[w], send_sem=send_sems.at[w],
                                               recv_sem=recv_sems.at[w], device_id=(x, y, 1 - c), device_id_type=MESH)
                  for w in range(n)]
        for cp in copies:
            cp.start()
        for cp in copies:
            cp.wait_recv()
        for cp in copies:
            cp.wait_send()

    return pl.pallas_call(
        body, out_shape=[jax.ShapeDtypeStruct(s.shape, s.dtype) for s in arrays],
        in_specs=[HBM_SPEC] * n, out_specs=[HBM_SPEC] * n,
        scratch_shapes=[pltpu.SemaphoreType.DMA((n,)), pltpu.SemaphoreType.DMA((n,))], name=name)(*arrays)


def _all_reduce_small(v):
    rows = v.shape[0]

    def body(v_ref, sum_ref, slots, send_sems, recv_sems):
        x, y, c = lax.axis_index("x"), lax.axis_index("y"), lax.axis_index("c")
        me = 4 * x + 2 * y + c
        slots[me] = v_ref[...]
        sends = []
        for k in range(1, N_DEV):
            bx, by, bc = (k >> 2) & 1, (k >> 1) & 1, k & 1
            peer = (x ^ bx, y ^ by, c ^ bc)
            rc = pltpu.make_async_remote_copy(src_ref=v_ref, dst_ref=slots.at[me], send_sem=send_sems.at[k],
                                              recv_sem=recv_sems.at[k], device_id=peer, device_id_type=MESH)
            rc.start()
            sends.append(rc)
        for k in range(1, N_DEV):
            bx, by, bc = (k >> 2) & 1, (k >> 1) & 1, k & 1
            src = 4 * (x ^ bx) + 2 * (y ^ by) + (c ^ bc)
            pltpu.make_async_remote_copy(src_ref=v_ref, dst_ref=slots.at[src], send_sem=send_sems.at[k],
                                         recv_sem=recv_sems.at[k], device_id=(x ^ bx, y ^ by, c ^ bc),
                                         device_id_type=MESH).wait_recv()
        for rc in sends:
            rc.wait_send()
        total = slots[0]
        for k in range(1, N_DEV):
            total = total + slots[k]
        sum_ref[...] = total

    vm = pl.BlockSpec(memory_space=pltpu.VMEM)
    return pl.pallas_call(
        body, out_shape=jax.ShapeDtypeStruct((rows, LANE), F32), in_specs=[vm], out_specs=vm,
        scratch_shapes=[pltpu.VMEM((N_DEV, rows, LANE), F32), pltpu.SemaphoreType.DMA((N_DEV,)),
                        pltpu.SemaphoreType.DMA((N_DEV,))], name="all_reduce_small")(v)


def _as_2d(a):
    return a.reshape(-1, a.shape[-1])


def _row_tile(rows, cols):
    for t in (512, 256, 128, 64, 32, 16):
        if rows % t == 0 and t * cols * 4 <= (1 << 20):
            return t
    return rows


def _sum_chips_into(parts, stacked, slot):
    _, rows, cols = parts.shape
    tr = _row_tile(rows, cols)
    first = slot * (rows // tr)

    def body(p_ref, _, o_ref):
        total = p_ref[0].astype(F32)
        for k in range(1, N_CHIPS):
            total = total + p_ref[k].astype(F32)
        o_ref[...] = total

    return pl.pallas_call(body, out_shape=jax.ShapeDtypeStruct(stacked.shape, F32), grid=(rows // tr,),
                          in_specs=[pl.BlockSpec((N_CHIPS, tr, cols), lambda i: (0, i, 0)), ANY_SPEC],
                          out_specs=pl.BlockSpec((tr, cols), lambda i: (first + i, 0)), input_output_aliases={1: 0},
                          name="sum_chips", compiler_params=_params(("parallel",)))(parts, stacked)


def _adamw_math(w, g, m, v):
    m = ADAM_B1 * m + (1.0 - ADAM_B1) * g
    v = ADAM_B2 * v + (1.0 - ADAM_B2) * (g * g)
    m_hat = m / (1.0 - ADAM_B1 ** ADAM_STEP)
    v_hat = v / (1.0 - ADAM_B2 ** ADAM_STEP)
    delta = -ADAM_LR * (m_hat / (jnp.sqrt(v_hat) + ADAM_EPS) + ADAM_WD * w)
    return delta, m, v


def _adamw(w, m, v, g_mine, g_sibling):
    rows, cols = w.shape
    tr = _row_tile(rows, cols)
    two = g_sibling is not None

    def body(*refs):
        if two:
            w_ref, m_ref, v_ref, ga_ref, gb_ref, g_ref, d_ref, nm_ref, nv_ref = refs
            g = ga_ref[...] + gb_ref[...]
        else:
            w_ref, m_ref, v_ref, ga_ref, g_ref, d_ref, nm_ref, nv_ref = refs
            g = ga_ref[...]
        delta, nm, nv = _adamw_math(w_ref[...], g, m_ref[...], v_ref[...])
        g_ref[...] = g
        d_ref[...] = delta
        nm_ref[...] = nm
        nv_ref[...] = nv

    blk = pl.BlockSpec((tr, cols), lambda i: (i, 0))
    args = [w, m, v, g_mine] + ([g_sibling] if two else [])
    return pl.pallas_call(body, out_shape=[jax.ShapeDtypeStruct((rows, cols), F32)] * 4, grid=(rows // tr,),
                          in_specs=[blk] * len(args), out_specs=[blk] * 4, name="adamw",
                          compiler_params=_params(("parallel",)))(*args)


def _pack_rows(arrays):
    flat = jnp.concatenate([a.reshape(-1) for a in arrays])
    rows = -(-flat.shape[0] // (8 * LANE)) * 8
    return jnp.pad(flat, (0, rows * LANE - flat.shape[0])).reshape(rows, LANE)


def _unpack_rows(packed, shapes):
    flat = packed.reshape(-1)
    out, at = [], 0
    for s in shapes:
        size = math.prod(s)
        out.append(flat[at:at + size].reshape(s))
        at += size
    return out


def kernel(x, p, positions, norm_g, ffn_w_in, ffn_w_out, ple_w_proj, ple_w_gate, rel_bias, mla_w_a, mla_q_norm, mla_kv_norm, mla_w_uq, mla_w_ukv, mla_w_o, dil_w_qkv, dil_w_o, fox_w_qkvf, fox_b_f, fox_w_o, loss_target, m_norm_g, m_ffn_w_in, m_ffn_w_out, m_ple_w_proj, m_ple_w_gate, m_rel_bias, m_mla_w_a, m_mla_q_norm, m_mla_kv_norm, m_mla_w_uq, m_mla_w_ukv, m_mla_w_o, m_dil_w_qkv, m_dil_w_o, m_fox_w_qkvf, m_fox_b_f, m_fox_w_o, v_norm_g, v_ffn_w_in, v_ffn_w_out, v_ple_w_proj, v_ple_w_gate, v_rel_bias, v_mla_w_a, v_mla_q_norm, v_mla_kv_norm, v_mla_w_uq, v_mla_w_ukv, v_mla_w_o, v_dil_w_qkv, v_dil_w_o, v_fox_w_qkvf, v_fox_b_f, v_fox_w_o):
    w = dict(norm_g=norm_g, ffn_w_in=ffn_w_in, ffn_w_out=ffn_w_out, ple_w_proj=ple_w_proj, ple_w_gate=ple_w_gate,
             rel_bias=rel_bias, mla_w_a=mla_w_a, mla_q_norm=mla_q_norm, mla_kv_norm=mla_kv_norm, mla_w_uq=mla_w_uq,
             mla_w_ukv=mla_w_ukv, mla_w_o=mla_w_o, dil_w_qkv=dil_w_qkv, dil_w_o=dil_w_o, fox_w_qkvf=fox_w_qkvf,
             fox_b_f=fox_b_f, fox_w_o=fox_w_o)
    m = dict(norm_g=m_norm_g, ffn_w_in=m_ffn_w_in, ffn_w_out=m_ffn_w_out, ple_w_proj=m_ple_w_proj,
             ple_w_gate=m_ple_w_gate, rel_bias=m_rel_bias, mla_w_a=m_mla_w_a, mla_q_norm=m_mla_q_norm,
             mla_kv_norm=m_mla_kv_norm, mla_w_uq=m_mla_w_uq, mla_w_ukv=m_mla_w_ukv, mla_w_o=m_mla_w_o,
             dil_w_qkv=m_dil_w_qkv, dil_w_o=m_dil_w_o, fox_w_qkvf=m_fox_w_qkvf, fox_b_f=m_fox_b_f, fox_w_o=m_fox_w_o)
    v = dict(norm_g=v_norm_g, ffn_w_in=v_ffn_w_in, ffn_w_out=v_ffn_w_out, ple_w_proj=v_ple_w_proj,
             ple_w_gate=v_ple_w_gate, rel_bias=v_rel_bias, mla_w_a=v_mla_w_a, mla_q_norm=v_mla_q_norm,
             mla_kv_norm=v_mla_kv_norm, mla_w_uq=v_mla_w_uq, mla_w_ukv=v_mla_w_ukv, mla_w_o=v_mla_w_o,
             dil_w_qkv=v_dil_w_qkv, dil_w_o=v_dil_w_o, fox_w_qkvf=v_fox_w_qkvf, fox_b_f=v_fox_b_f, fox_w_o=v_fox_w_o)
    chip = 2 * lax.axis_index("x") + lax.axis_index("y")

    small_shapes = [w[k].shape for k in SMALL_SHARDED]
    order = [(i, part) for i in range(DEPTH) for part in (MIXER_PART, COMMON_PART)]
    gathers = {}
    after = positions
    for i, part in order:
        srcs = [w[k][_layer_slot(k, i)].astype(BF) for k in _part_names(i, part)]
        if (i, part) == order[0]:
            srcs.append(_pack_rows([w[k] for k in SMALL_SHARDED]))
        gathers[i, part] = _spread_start(srcs, False, after, f"gather_start_{i}_{part}")
        after = gathers[i, part]["token"]
    all_started = after
    state = {}

    def get_part(i, part, after_array):
        is_first = (i, part) == order[0]
        lands = _spread_wait(gathers[i, part], all_started if is_first else after_array, f"gather_wait_{i}_{part}")
        if is_first:
            pieces = [_unpack_rows(lands[-1][k], small_shapes) for k in range(N_CHIPS)]
            small = {name: jnp.concatenate([pieces[k][idx] for k in range(N_CHIPS)], axis=-1)
                     for idx, name in enumerate(SMALL_SHARDED)}
            state["small"] = dict(small, rel_bias=rel_bias, fox_b_f=fox_b_f)
        chunks = dict(zip(_part_names(i, part), lands))
        state[i, part] = {k: a.shape for k, a in chunks.items()}
        return _part_to_compute(i, part, chunks)

    exchanges = {}

    def put_part(i, part, lg):
        contrib = _part_contributions(i, part, lg, state[i, part])
        exchanges[i, part] = _spread_start([contrib[k] for k in _part_names(i, part)], True, positions,
                                           f"exchange_start_{i}_{part}")
        return exchanges[i, part]["token"]

    sq, grad_x, sg = _run_layers(x[0], p[:, 0], positions[0], loss_target[0], get_part, lambda: state["small"],
                                 put_part)
    loss = lax.psum(0.5 / D_MODEL * jnp.sum(sq), ("x", "y", "c"))

    mine = {k: lax.empty(_as_2d(w[k]).shape, F32) for k in BIG}
    for i, part in [(i, part) for i in reversed(range(DEPTH)) for part in (COMMON_PART, MIXER_PART)]:
        received = _spread_wait(exchanges[i, part], grad_x, f"exchange_wait_{i}_{part}")
        for k, r in zip(_part_names(i, part), received):
            mine[k] = _sum_chips_into(r, mine[k], _layer_slot(k, i))
    theirs = _exchange_sibling([mine[k] for k in BIG], "exchange_sibling")
    results = {}
    for k, gb in zip(BIG, theirs):
        outs = _adamw(_as_2d(w[k]), _as_2d(m[k]), _as_2d(v[k]), mine[k], gb)
        results[k] = [o.reshape(w[k].shape) for o in outs]

    small_all = SMALL_SHARDED + SMALL_REPLICATED
    full_shapes = [sg[k].shape for k in small_all]
    reduced = dict(zip(small_all, _unpack_rows(_all_reduce_small(_pack_rows([sg[k] for k in small_all])), full_shapes)))
    local_g = []
    for k in small_all:
        g = reduced[k]
        if k in SMALL_SHARDED:
            width = w[k].shape[-1]
            g = lax.dynamic_slice_in_dim(g, chip * width, width, axis=g.ndim - 1)
        local_g.append(g)
    local_shapes = [w[k].shape for k in small_all]
    outs = _adamw(_pack_rows([w[k] for k in small_all]), _pack_rows([m[k] for k in small_all]),
                  _pack_rows([v[k] for k in small_all]), _pack_rows(local_g), None)
    unpacked = [_unpack_rows(o, local_shapes) for o in outs]
    for idx, k in enumerate(small_all):
        results[k] = [u[idx] for u in unpacked]

    return (loss, grad_x[None], *[results[k][0] for k in WEIGHTS], *[results[k][1] for k in WEIGHTS],
            *[results[k][2] for k in WEIGHTS], *[results[k][3] for k in WEIGHTS])
```

```python
import functools
import math

import jax
import jax.numpy as jnp
from jax import lax
from jax.experimental import pallas as pl
from jax.experimental.pallas import tpu as pltpu

F32 = jnp.float32
BF = jnp.bfloat16
MESH = pl.DeviceIdType.MESH
HBM_SPEC = pl.BlockSpec(memory_space=pltpu.HBM)

D_MODEL = 1024
DEPTH = 4
N_MIXERS = 3
D_FF = 2816
NORM_EPS = 1e-6
NEG_INF = -1e30
LANE = 128
HEADS = 16
HEAD_DIM = 64
MLA_Q_RANK = 384
MLA_KV_RANK = 256
MLA_ROPE = 32
MLA_A_PAD = 768
ROPE_THETA = 10000.0
DIL_PATTERNS = ((128, 1), (512, 4), (2048, 16))
Q_BLOCK = 128
REL_BUCKETS = 32
REL_MAX_DIST = 2048
N_CHIPS = 4
N_DEV = 8

ADAM_LR = 0.001
ADAM_B1 = 0.9
ADAM_B2 = 0.999
ADAM_EPS = 1e-08
ADAM_WD = 0.01
ADAM_STEP = 10

VMEM_LIMIT = 56 * 1024 * 1024
MATMUL_VMEM_BUDGET = 36 * 1024 * 1024
ROW_TILE = 256
ATTN_TILE = 256
MLA_GROUP = 4
FOX_GROUP = 2


def _params(sem=None):
    return pltpu.CompilerParams(dimension_semantics=sem, vmem_limit_bytes=VMEM_LIMIT)


def _divisor_tiles(dim):
    tiles = [t for t in range(LANE, dim + 1, LANE) if dim % t == 0]
    return tiles or [dim]


def _matmul_tiles(m, n, k, a_bytes, b_bytes, out_bytes, has_add, n_unit=None, k_unit=None):
    best = None
    for tm in _divisor_tiles(m):
        for tn in _divisor_tiles(n_unit or n):
            for tk in _divisor_tiles(k_unit or k):
                if max(tm, tn, tk) > 2048:
                    continue
                vmem = 2 * (tm * tk * a_bytes + tk * tn * b_bytes + tm * tn * out_bytes) + tm * tn * 4
                if has_add:
                    vmem += 2 * tm * tn * 4
                if vmem > MATMUL_VMEM_BUDGET:
                    continue
                steps = (m // tm) * (n // tn) * (k // tk)
                traffic = m * k * a_bytes * (n // tn) + k * n * b_bytes * (m // tm) + m * n * out_bytes
                cost = traffic / 3.0e12 + steps * 0.4e-6
                if best is None or cost < best[0]:
                    best = (cost, tm, tn, tk)
    return best[1:]


def _matmul(a, b, *, ta=False, tb=False, b_chunks=False, out_chunks=False, add=None, out_dtype=F32, name):
    k, m = a.shape if ta else a.shape[::-1]
    n_unit = k_unit = None
    if b_chunks:
        chunks, rows_w, c = b.shape
        if tb:
            kb, n, k_unit = chunks * c, rows_w, c
        else:
            kb, n, n_unit = rows_w, chunks * c, c
    else:
        kb, n = b.shape[::-1] if tb else b.shape
    if out_chunks:
        assert n % N_CHIPS == 0 and add is None
        n_unit = n // N_CHIPS
    assert k == kb, (a.shape, b.shape, ta, tb)
    tm, tn, tk = _matmul_tiles(m, n, k, a.dtype.itemsize, b.dtype.itemsize, jnp.dtype(out_dtype).itemsize,
                               add is not None, n_unit, k_unit)
    nk = k // tk
    dims = (((0 if ta else 1,), (1 if tb else 0,)), ((), ()))

    def body(*refs):
        if add is None:
            a_ref, b_ref, o_ref, acc_ref = refs
            add_ref = None
        else:
            a_ref, b_ref, add_ref, o_ref, acc_ref = refs
        kk = pl.program_id(2)

        @pl.when(kk == 0)
        def _():
            acc_ref[...] = jnp.zeros_like(acc_ref)

        acc_ref[...] += lax.dot_general(a_ref[...].astype(BF), b_ref[...].astype(BF), dims,
                                        preferred_element_type=F32)

        @pl.when(kk == nk - 1)
        def _():
            r = acc_ref[...]
            if add_ref is not None:
                r = r + add_ref[...].astype(F32)
            o_ref[...] = r.astype(out_dtype)

    a_spec = pl.BlockSpec((tk, tm), lambda i, j, q: (q, i)) if ta else pl.BlockSpec((tm, tk), lambda i, j, q: (i, q))
    if b_chunks and tb:
        per_k = k_unit // tk
        b_spec = pl.BlockSpec((None, tn, tk), lambda i, j, q: (q // per_k, j, q % per_k))
    elif b_chunks:
        per_n = n_unit // tn
        b_spec = pl.BlockSpec((None, tk, tn), lambda i, j, q: (j // per_n, q, j % per_n))
    elif tb:
        b_spec = pl.BlockSpec((tn, tk), lambda i, j, q: (j, q))
    else:
        b_spec = pl.BlockSpec((tk, tn), lambda i, j, q: (q, j))
    if out_chunks:
        per_o = n_unit // tn
        o_spec = pl.BlockSpec((None, tm, tn), lambda i, j, q: (j // per_o, i, j % per_o))
        out_shape = jax.ShapeDtypeStruct((N_CHIPS, m, n_unit), out_dtype)
    else:
        o_spec = pl.BlockSpec((tm, tn), lambda i, j, q: (i, j))
        out_shape = jax.ShapeDtypeStruct((m, n), out_dtype)
    in_specs = [a_spec, b_spec]
    args = [a, b]
    if add is not None:
        in_specs.append(o_spec)
        args.append(add)
    return pl.pallas_call(
        body, out_shape=out_shape, grid=(m // tm, n // tn, nk),
        in_specs=in_specs, out_specs=o_spec, scratch_shapes=[pltpu.VMEM((tm, tn), F32)], name=name,
        compiler_params=_params(("parallel", "parallel", "arbitrary")))(*args)


def _rowwise(body, name, rows, ins, outs, tr=ROW_TILE):
    def row_spec(cols):
        return pl.BlockSpec((tr, cols), lambda i: (i, 0))

    def full_spec(shape):
        zeros = (0,) * len(shape)
        return pl.BlockSpec(shape, lambda i: zeros)

    in_specs = [row_spec(a.shape[1]) if kind == "row" else full_spec(a.shape) for a, kind in ins]
    out_specs = [row_spec(shape[1]) if kind == "row" else full_spec(shape) for shape, _, kind in outs]
    out_shape = [jax.ShapeDtypeStruct(shape, dtype) for shape, dtype, _ in outs]
    return pl.pallas_call(body, out_shape=out_shape, grid=(rows // tr,), in_specs=in_specs, out_specs=out_specs,
                          name=name, compiler_params=_params(("arbitrary",)))(*[a for a, _ in ins])


def _rstd(x):
    return lax.rsqrt(jnp.mean(x * x, axis=-1, keepdims=True) + NORM_EPS)


def _rms_bwd_math(x, g, dy):
    r = _rstd(x)
    gd = dy * g
    dx = r * gd - x * (r * r * r) * jnp.mean(gd * x, axis=-1, keepdims=True)
    dg = jnp.sum(dy * x * r, axis=0, keepdims=True)
    return dx, dg


def _sigmoid(x):
    return 0.5 * jnp.tanh(0.5 * x) + 0.5


def _init_acc(*refs):
    @pl.when(pl.program_id(0) == 0)
    def _():
        for r in refs:
            r[...] = jnp.zeros_like(r)


def _prenorm(h, g):
    rows, cols = h.shape

    def body(h_ref, g_ref, o_ref):
        x = h_ref[...]
        o_ref[...] = (x * _rstd(x) * g_ref[...]).astype(BF)

    return _rowwise(body, "prenorm", rows, [(h, "row"), (g, "full")], [((rows, cols), BF, "row")])[0]


def _post_residual(h, y, g_post, g_pre):
    rows, cols = h.shape
    with_pre = g_pre is not None

    def body(*refs):
        if with_pre:
            h_ref, y_ref, gp_ref, gq_ref, hn_ref, hb_ref = refs
        else:
            h_ref, y_ref, gp_ref, hn_ref, hb_ref = refs
        yv = y_ref[...]
        hn = h_ref[...] + yv * _rstd(yv) * gp_ref[...]
        hn_ref[...] = hn
        hb_ref[...] = (hn * _rstd(hn) * gq_ref[...] if with_pre else hn).astype(BF)

    ins = [(h, "row"), (y, "row"), (g_post, "full")] + ([(g_pre, "full")] if with_pre else [])
    return _rowwise(body, "post_residual_pre" if with_pre else "post_residual", rows, ins,
                    [((rows, cols), F32, "row"), ((rows, cols), BF, "row")])


def _ple_forward(h2, pp, z, g_pre):
    rows, cols = h2.shape

    def body(h_ref, p_ref, z_ref, g_ref, h3_ref, hb_ref):
        h3 = h_ref[...] + p_ref[...] * _sigmoid(z_ref[...])
        h3_ref[...] = h3
        hb_ref[...] = (h3 * _rstd(h3) * g_ref[...]).astype(BF)

    return _rowwise(body, "ple_forward", rows, [(h2, "row"), (pp, "row"), (z, "row"), (g_pre, "full")],
                    [((rows, cols), F32, "row"), ((rows, cols), BF, "row")])


def _ple_loss(h2, pp, z, target):
    rows, cols = h2.shape

    def body(h_ref, p_ref, z_ref, t_ref, dh_ref, sq_ref):
        _init_acc(sq_ref)
        err = h_ref[...] + p_ref[...] * _sigmoid(z_ref[...]) - t_ref[...]
        dh_ref[...] = err * (1.0 / cols)
        sq_ref[...] += jnp.sum(err * err, axis=0, keepdims=True)

    return _rowwise(body, "ple_loss", rows, [(h2, "row"), (pp, "row"), (z, "row"), (target, "row")],
                    [((rows, cols), F32, "row"), ((1, cols), F32, "acc")])


def _ple_backward(dh3, pp, z):
    rows, cols = dh3.shape

    def body(d_ref, p_ref, z_ref, dpp_ref, dz_ref):
        d = d_ref[...]
        s = _sigmoid(z_ref[...])
        dpp_ref[...] = (d * s).astype(BF)
        dz_ref[...] = (d * p_ref[...] * s * (1.0 - s)).astype(BF)

    return _rowwise(body, "ple_backward", rows, [(dh3, "row"), (pp, "row"), (z, "row")],
                    [((rows, cols), BF, "row"), ((rows, cols), BF, "row")])


def _rms_backward(x, g, dy, add, out_dtype):
    rows, cols = x.shape
    with_add = add is not None

    def body(*refs):
        if with_add:
            x_ref, g_ref, dy_ref, add_ref, dx_ref, dg_ref = refs
        else:
            x_ref, g_ref, dy_ref, dx_ref, dg_ref = refs
        _init_acc(dg_ref)
        dx, dg = _rms_bwd_math(x_ref[...], g_ref[...], dy_ref[...].astype(F32))
        if with_add:
            dx = dx + add_ref[...]
        dx_ref[...] = dx.astype(out_dtype)
        dg_ref[...] += dg

    ins = [(x, "row"), (g, "full"), (dy, "row")] + ([(add, "row")] if with_add else [])
    return _rowwise(body, "rms_backward_add" if with_add else "rms_backward", rows, ins,
                    [((rows, cols), out_dtype, "row"), ((1, cols), F32, "acc")])


def _swiglu_forward(gu):
    rows = gu.shape[0]
    tc = D_FF // 2

    def body(g_ref, u_ref, o_ref):
        g = g_ref[...].astype(F32)
        o_ref[...] = (g * _sigmoid(g) * u_ref[...].astype(F32)).astype(BF)

    return pl.pallas_call(
        body, out_shape=jax.ShapeDtypeStruct((rows, D_FF), BF), grid=(rows // ROW_TILE, 2),
        in_specs=[pl.BlockSpec((ROW_TILE, tc), lambda i, j: (i, j)), pl.BlockSpec((ROW_TILE, tc), lambda i, j: (i, j + 2))],
        out_specs=pl.BlockSpec((ROW_TILE, tc), lambda i, j: (i, j)), name="swiglu_forward",
        compiler_params=_params(("parallel", "parallel")))(gu, gu)


def _swiglu_backward(gu, dact):
    rows = gu.shape[0]
    tc = D_FF // 2

    def body(g_ref, u_ref, d_ref, o_ref):
        g = g_ref[...].astype(F32)
        u = u_ref[...].astype(F32)
        d = d_ref[...].astype(F32)
        s = _sigmoid(g)

        @pl.when(pl.program_id(1) < 2)
        def _():
            o_ref[...] = (d * u * s * (1.0 + g * (1.0 - s))).astype(BF)

        @pl.when(pl.program_id(1) >= 2)
        def _():
            o_ref[...] = (d * g * s).astype(BF)

    return pl.pallas_call(
        body, out_shape=jax.ShapeDtypeStruct((rows, 2 * D_FF), BF), grid=(rows // ROW_TILE, 4),
        in_specs=[pl.BlockSpec((ROW_TILE, tc), lambda i, j: (i, j % 2)),
                  pl.BlockSpec((ROW_TILE, tc), lambda i, j: (i, j % 2 + 2)),
                  pl.BlockSpec((ROW_TILE, tc), lambda i, j: (i, j % 2))],
        out_specs=pl.BlockSpec((ROW_TILE, tc), lambda i, j: (i, j)), name="swiglu_backward",
        compiler_params=_params(("parallel", "parallel")))(gu, gu, dact)


def _rope_tables(positions):
    half = MLA_ROPE // 2
    inv = ROPE_THETA ** (-jnp.arange(half, dtype=F32) / half)
    ang = positions.astype(F32)[:, None] * inv
    cos, sin = jnp.cos(ang), jnp.sin(ang)
    rows = positions.shape[0]
    c = jnp.ones((rows, LANE), F32).at[:, 64:80].set(cos).at[:, 80:96].set(cos)
    sa = jnp.zeros((rows, LANE), F32).at[:, 64:80].set(-sin)
    sb = jnp.zeros((rows, LANE), F32).at[:, 80:96].set(sin)
    return c, sa, sb


def _rope_apply(x, c, sa, sb):
    return x * c + pltpu.roll(x, LANE - 16, 1) * sa + pltpu.roll(x, 16, 1) * sb


def _rope_apply_t(dy, c, sa, sb):
    return dy * c + pltpu.roll(dy * sa, 16, 1) + pltpu.roll(dy * sb, LANE - 16, 1)


def _rope_heads(x, tables, transpose, name):
    rows, cols = x.shape

    def body(x_ref, c_ref, sa_ref, sb_ref, o_ref):
        fn = _rope_apply_t if transpose else _rope_apply
        c, sa, sb = c_ref[...], sa_ref[...], sb_ref[...]
        for head in range(cols // LANE):
            lanes = slice(head * LANE, (head + 1) * LANE)
            o_ref[:, lanes] = fn(x_ref[:, lanes].astype(F32), c, sa, sb).astype(BF)

    blk = pl.BlockSpec((ROW_TILE, cols), lambda i: (i, 0))
    tbl = pl.BlockSpec((ROW_TILE, LANE), lambda i: (i, 0))
    return pl.pallas_call(body, out_shape=jax.ShapeDtypeStruct((rows, cols), BF), grid=(rows // ROW_TILE,),
                          in_specs=[blk, tbl, tbl, tbl], out_specs=blk, name=name,
                          compiler_params=_params(("parallel",)))(x, *tables)


def _mla_mid_forward(a, q_norm, kv_norm, tables):
    rows = a.shape[0]
    qr, kvr = MLA_Q_RANK, MLA_KV_RANK

    def body(a_ref, qn_ref, kn_ref, c_ref, sa_ref, sb_ref, cq_ref, ckv_ref, kr_ref):
        aq = a_ref[:, 0:qr]
        akv = a_ref[:, qr:qr + kvr]
        cq_ref[...] = (aq * _rstd(aq) * qn_ref[...]).astype(BF)
        ckv_ref[...] = (akv * _rstd(akv) * kn_ref[...]).astype(BF)
        kr_ref[...] = _rope_apply(a_ref[:, qr + kvr:], c_ref[...], sa_ref[...], sb_ref[...]).astype(BF)

    ins = [(a, "row"), (q_norm, "full"), (kv_norm, "full")] + [(t, "row") for t in tables]
    return _rowwise(body, "mla_mid_forward", rows, ins,
                    [((rows, qr), BF, "row"), ((rows, kvr), BF, "row"), ((rows, LANE), BF, "row")])


def _mla_mid_backward(a, q_norm, kv_norm, tables, dcq, dckv, dkr):
    rows = a.shape[0]
    qr, kvr = MLA_Q_RANK, MLA_KV_RANK

    def body(a_ref, qn_ref, kn_ref, c_ref, sa_ref, sb_ref, dcq_ref, dckv_ref, dkr_ref, da_ref, dqn_ref, dkn_ref):
        _init_acc(dqn_ref, dkn_ref)
        dxq, dgq = _rms_bwd_math(a_ref[:, 0:qr], qn_ref[...], dcq_ref[...])
        dxk, dgk = _rms_bwd_math(a_ref[:, qr:qr + kvr], kn_ref[...], dckv_ref[...])
        da_ref[:, 0:qr] = dxq.astype(BF)
        da_ref[:, qr:qr + kvr] = dxk.astype(BF)
        da_ref[:, qr + kvr:] = _rope_apply_t(dkr_ref[...], c_ref[...], sa_ref[...], sb_ref[...]).astype(BF)
        dqn_ref[...] += dgq
        dkn_ref[...] += dgk

    ins = ([(a, "row"), (q_norm, "full"), (kv_norm, "full")] + [(t, "row") for t in tables]
           + [(dcq, "row"), (dckv, "row"), (dkr, "row")])
    return _rowwise(body, "mla_mid_backward", rows, ins,
                    [((rows, MLA_A_PAD), BF, "row"), ((1, qr), F32, "acc"), ((1, kvr), F32, "acc")])


def _attn_specs(rows, kv_off, g):
    head =pl.BlockSpec((rows, g * LANE), lambda h: (0, h))
    kv_head = pl.BlockSpec((rows, g * LANE), lambda h: (0, h + kv_off // g))
    shared = pl.BlockSpec((rows, LANE), lambda h: (0, 0))
    col_vec = pl.BlockSpec((g, rows, 1), lambda h: (h, 0, 0))
    row_vec = pl.BlockSpec((g, 1, rows), lambda h: (h, 0, 0))
    return head, kv_head, shared, col_vec, row_vec


def _attn_forward(q, kv, kv_off, kr, cum_col, cum_row, scale, group_size, name):
    rows = q.shape[0]
    heads = HEADS
    t = ATTN_TILE
    nb = rows // t
    has_kr = kr is not None
    has_f = cum_col is not None
    group = range(group_size)

    def body(*refs):
        it = iter(refs)
        q_ref, kv_ref = next(it), next(it)
        kr_ref = next(it) if has_kr else None
        cc_ref = next(it) if has_f else None
        cr_ref = next(it) if has_f else None
        o_ref, lse_ref = next(it), next(it)
        lo = lax.broadcasted_iota(jnp.int32, (1, LANE), 1) < HEAD_DIM
        causal = (lax.broadcasted_iota(jnp.int32, (t, t), 1) <= lax.broadcasted_iota(jnp.int32, (t, t), 0))
        lanes = [slice(g * LANE, (g + 1) * LANE) for g in group]

        def q_block(i, _):
            qs = pl.ds(pl.multiple_of(i * t, t), t)
            qbs = [q_ref[qs, lanes[g]] for g in group]
            cqs = [cc_ref[g, qs, :] if has_f else None for g in group]

            def step(j, carry, diag):
                ks = pl.ds(pl.multiple_of(j * t, t), t)
                other = kr_ref[ks, :] if has_kr else jnp.zeros((t, LANE), BF)
                out = []
                for g in group:
                    m, l, acc = carry[g]
                    kvb = kv_ref[ks, lanes[g]]
                    kk = jnp.where(lo, kvb, other)
                    s = lax.dot_general(qbs[g], kk, (((1,), (1,)), ((), ())), preferred_element_type=F32) * scale
                    if has_f:
                        s = s + (cqs[g] - cr_ref[g, :, ks])
                    if diag:
                        s = jnp.where(causal, s, NEG_INF)
                    mn = jnp.maximum(m, jnp.max(s, axis=1, keepdims=True))
                    alpha = jnp.exp(m - mn)
                    p = jnp.exp(s - mn)
                    l = alpha * l + jnp.sum(p, axis=1, keepdims=True)
                    acc = alpha * acc + jnp.dot(p.astype(BF), kvb, preferred_element_type=F32)
                    out.append((mn, l, acc))
                return tuple(out)

            init = tuple((jnp.full((t, 1), NEG_INF, F32), jnp.zeros((t, 1), F32), jnp.zeros((t, LANE), F32))
                         for _ in group)
            carry = lax.fori_loop(0, i, lambda j, c: step(j, c, False), init)
            for g, (m, l, acc) in enumerate(step(i, carry, True)):
                o_ref[qs, lanes[g]] = jnp.where(lo, 0.0, acc / l).astype(BF)
                lse_ref[g, qs, :] = m + jnp.log(l)
            return 0

        lax.fori_loop(0, nb, q_block, 0)

    head, kv_head, shared, col_vec, row_vec = _attn_specs(rows, kv_off, group_size)
    in_specs, args = [head, kv_head], [q, kv]
    if has_kr:
        in_specs.append(shared)
        args.append(kr)
    if has_f:
        in_specs += [col_vec, row_vec]
        args += [cum_col, cum_row]
    return pl.pallas_call(
        body, out_shape=[jax.ShapeDtypeStruct((rows, heads * LANE), BF), jax.ShapeDtypeStruct((heads, rows, 1), F32)],
        grid=(heads // group_size,), in_specs=in_specs, out_specs=[head, col_vec], name=name,
        compiler_params=_params(("arbitrary",)))(*args)


def _attn_backward(q, kv, kv_off, kr, cum_col, cum_row, o, do, lse, scale, group_size, name):
    rows = q.shape[0]
    heads = HEADS
    t = ATTN_TILE
    nb = rows // t
    has_kr = kr is not None
    has_f = cum_col is not None
    group = range(group_size)

    def body(*refs):
        it = iter(refs)
        q_ref, kv_ref = next(it), next(it)
        kr_ref = next(it) if has_kr else None
        cc_ref = next(it) if has_f else None
        cr_ref = next(it) if has_f else None
        o_ref, do_ref, lse_ref = next(it), next(it), next(it)
        dq_ref, dkv_ref = next(it), next(it)
        dkr_ref = next(it) if has_kr else None
        dck_ref = next(it) if has_f else None
        dcq_ref = next(it) if has_f else None
        dq_acc = next(it)
        lo = lax.broadcasted_iota(jnp.int32, (1, LANE), 1) < HEAD_DIM
        causal = (lax.broadcasted_iota(jnp.int32, (t, t), 1) <= lax.broadcasted_iota(jnp.int32, (t, t), 0))
        lanes = [slice(g * LANE, (g + 1) * LANE) for g in group]

        dq_acc[...] = jnp.zeros_like(dq_acc)
        if has_kr:
            _init_acc(dkr_ref)
        if has_f:
            dcq_ref[...] = jnp.zeros_like(dcq_ref)

        def kv_block(j, _):
            ks = pl.ds(pl.multiple_of(j * t, t), t)
            other = kr_ref[ks, :] if has_kr else jnp.zeros((t, LANE), BF)
            kvbs = [kv_ref[ks, lanes[g]] for g in group]
            kks = [jnp.where(lo, kvbs[g], other) for g in group]
            cks = [cr_ref[g, :, ks] if has_f else None for g in group]

            def pair(i, carry, diag):
                qs = pl.ds(pl.multiple_of(i * t, t), t)
                out = []
                for g in group:
                    dkk, dvv, dcs = carry[g]
                    qb = q_ref[qs, lanes[g]]
                    dob = do_ref[qs, lanes[g]]
                    s = lax.dot_general(qb, kks[g], (((1,), (1,)), ((), ())), preferred_element_type=F32) * scale
                    if has_f:
                        s = s + (cc_ref[g, qs, :] - cks[g])
                    if diag:
                        s = jnp.where(causal, s, NEG_INF)
                    p = jnp.exp(s - lse_ref[g, qs, :])
                    dp = lax.dot_general(dob, kvbs[g], (((1,), (1,)), ((), ())), preferred_element_type=F32)
                    delta = jnp.sum(dob.astype(F32) * o_ref[qs, lanes[g]].astype(F32), axis=1, keepdims=True)
                    ds = p * (dp - delta)
                    dsb = ds.astype(BF)
                    dvv = dvv + lax.dot_general(p.astype(BF), dob, (((0,), (0,)), ((), ())), preferred_element_type=F32)
                    dkk = dkk + lax.dot_general(dsb, qb, (((0,), (0,)), ((), ())), preferred_element_type=F32)
                    dq_acc[qs, lanes[g]] += jnp.dot(dsb, kks[g], preferred_element_type=F32)
                    if has_f:
                        dcs = dcs + jnp.sum(ds, axis=0, keepdims=True)
                        dcq_ref[g, qs, :] += jnp.sum(ds, axis=1, keepdims=True)
                    out.append((dkk, dvv, dcs))
                return tuple(out)

            init = tuple((jnp.zeros((t, LANE), F32), jnp.zeros((t, LANE), F32), jnp.zeros((1, t), F32)) for _ in group)
            carry = pair(j, init, True)
            carry = lax.fori_loop(j + 1, nb, lambda i, c: pair(i, c, False), carry)
            for g, (dkk, dvv, dcs) in enumerate(carry):
                dkk = dkk * scale
                dkv_ref[ks, lanes[g]] = jnp.where(lo, dkk, dvv).astype(BF)
                if has_kr:
                    dkr_ref[ks, :] += jnp.where(lo, 0.0, dkk)
                if has_f:
                    dck_ref[g, :, ks] = -dcs
            return 0

        lax.fori_loop(0, nb, kv_block, 0)
        dq_ref[...] = (dq_acc[...] * scale).astype(BF)

    head, kv_head, shared, col_vec, row_vec = _attn_specs(rows, kv_off, group_size)
    in_specs, args = [head, kv_head], [q, kv]
    if has_kr:
        in_specs.append(shared)
        args.append(kr)
    if has_f:
        in_specs += [col_vec, row_vec]
        args += [cum_col, cum_row]
    in_specs += [head, head, col_vec]
    args += [o, do, lse]
    out_shape = [jax.ShapeDtypeStruct((rows, heads * LANE), BF), jax.ShapeDtypeStruct((rows, heads * LANE), BF)]
    out_specs = [head, head]
    if has_kr:
        out_shape.append(jax.ShapeDtypeStruct((rows, LANE), F32))
        out_specs.append(shared)
    if has_f:
        out_shape += [jax.ShapeDtypeStruct((heads, 1, rows), F32), jax.ShapeDtypeStruct((heads, rows, 1), F32)]
        out_specs += [row_vec, col_vec]
    return pl.pallas_call(
        body, out_shape=out_shape, grid=(heads // group_size,), in_specs=in_specs, out_specs=out_specs,
        scratch_shapes=[pltpu.VMEM((rows, group_size * LANE), F32)], name=name,
        compiler_params=_params(("arbitrary",)))(*args)


def _tri_dot(tri, x):
    return jnp.dot(tri, x, preferred_element_type=F32, precision=lax.Precision.HIGHEST)


def _forget_forward(f_raw, b_f):
    rows = f_raw.shape[0]
    t = ATTN_TILE

    def body(f_ref, b_ref, cum_ref):
        tri = (lax.broadcasted_iota(jnp.int32, (t, t), 1) <= lax.broadcasted_iota(jnp.int32, (t, t), 0)).astype(F32)

        def blk(i, carry):
            sl = pl.ds(pl.multiple_of(i * t, t), t)
            xv = f_ref[sl, :] + b_ref[...]
            log_f = jnp.minimum(xv, 0.0) - jnp.log(1.0 + jnp.exp(-jnp.abs(xv)))
            cum_ref[sl, :] = _tri_dot(tri, log_f) + carry
            return carry + jnp.sum(log_f, axis=0, keepdims=True)

        lax.fori_loop(0, rows // t, blk, jnp.zeros((1, LANE), F32))

    return pl.pallas_call(body, out_shape=jax.ShapeDtypeStruct((rows, LANE), F32), name="forget_forward",
                          compiler_params=_params())(f_raw, b_f)


def _forget_backward(f_raw, b_f, dcum):
    rows = f_raw.shape[0]
    t = ATTN_TILE
    nb = rows // t

    def body(f_ref, b_ref, dc_ref, df_ref, db_ref):
        tri = (lax.broadcasted_iota(jnp.int32, (t, t), 1) >= lax.broadcasted_iota(jnp.int32, (t, t), 0)).astype(F32)

        def blk(i, carry):
            later, db = carry
            sl = pl.ds(pl.multiple_of((nb - 1 - i) * t, t), t)
            dc = dc_ref[sl, :]
            dlog = _tri_dot(tri, dc) + later
            xv = f_ref[sl, :] + b_ref[...]
            df = dlog / (1.0 + jnp.exp(xv))
            df_ref[sl, :] = df.astype(BF)
            return later + jnp.sum(dc, axis=0, keepdims=True), db + jnp.sum(df, axis=0, keepdims=True)

        _, db = lax.fori_loop(0, nb, blk, (jnp.zeros((1, LANE), F32), jnp.zeros((1, LANE), F32)))
        db_ref[...] = db

    return pl.pallas_call(body, out_shape=[jax.ShapeDtypeStruct((rows, LANE), BF), jax.ShapeDtypeStruct((1, LANE), F32)],
                          name="forget_backward", compiler_params=_params())(f_raw, b_f, dcum)


def _t5_bucket(dist):
    max_exact = REL_BUCKETS // 2
    n = jnp.maximum(dist.astype(F32), 1.0)
    large = max_exact + (jnp.log(n / max_exact) / math.log(REL_MAX_DIST / max_exact)
                         * (REL_BUCKETS - max_exact)).astype(jnp.int32)
    large = jnp.minimum(large, REL_BUCKETS - 1)
    return jnp.where(dist < max_exact, dist, large)


def _dil_buckets(dilation):
    i = jnp.arange(Q_BLOCK)[:, None]
    j = jnp.arange(Q_BLOCK)[None, :]
    cur = _t5_bucket(jnp.clip(i - j, 0) * dilation).astype(jnp.int32)
    prev = _t5_bucket(jnp.clip(Q_BLOCK + i - j, 0) * dilation).astype(jnp.int32)
    return cur, prev


def _dil_bias_tiles(tbl_ref, bc_ref, bp_ref, bias_ref, group, hp):
    for hh in range(2):
        col = group * HEADS + 2 * hp + hh
        acc_c = jnp.zeros((Q_BLOCK, Q_BLOCK), F32)
        acc_p = jnp.zeros((Q_BLOCK, Q_BLOCK), F32)
        for b in range(REL_BUCKETS):
            val = tbl_ref[b, col]
            acc_c = jnp.where(bc_ref[...] == b, val, acc_c)
            acc_p = jnp.where(bp_ref[...] == b, val, acc_p)
        bias_ref[2 * hh] = acc_c
        bias_ref[2 * hh + 1] = acc_p


def _dil_view(qkv, group, dilation):
    if dilation == 1:
        return qkv
    width = 3 * HEADS * HEAD_DIM
    return qkv[:, group * width:(group + 1) * width].reshape(qkv.shape[0] // dilation, dilation * width)


def _dil_specs(group, dilation, length):
    def col(kind):
        if dilation == 1:
            return pl.BlockSpec((length, LANE), lambda hp, r: (0, (group * 3 + kind) * 8 + hp))
        return pl.BlockSpec((length, LANE), lambda hp, r: (0, r * 24 + kind * 8 + hp))

    out = pl.BlockSpec((length, LANE), lambda hp, r: (0, r * 8 + hp))
    tile = pl.BlockSpec((Q_BLOCK, Q_BLOCK), lambda hp, r: (0, 0))
    table = pl.BlockSpec(memory_space=pltpu.SMEM)
    return col, out, tile, table


def _dil_forward(view, group, dilation, table, buckets):
    length = view.shape[0]
    rows = length * dilation
    nb = length // Q_BLOCK
    scale = HEAD_DIM ** -0.5
    qb = Q_BLOCK

    def body(tbl_ref, bc_ref, bp_ref, q_ref, k_ref, v_ref, o_ref, lse_ref, bias_ref):
        hp = pl.program_id(0)

        @pl.when(pl.program_id(1) == 0)
        def _():
            _dil_bias_tiles(tbl_ref, bc_ref, bp_ref, bias_ref, group, hp)

        lo = lax.broadcasted_iota(jnp.int32, (1, LANE), 1) < HEAD_DIM
        ii = lax.broadcasted_iota(jnp.int32, (qb, qb), 0)
        jj = lax.broadcasted_iota(jnp.int32, (qb, qb), 1)

        def blk(n, _):
            cur = pl.ds(pl.multiple_of(n * qb, qb), qb)
            prev = pl.ds(pl.multiple_of(jnp.maximum(n - 1, 0) * qb, qb), qb)
            qn = q_ref[cur, :]
            kc, kp, vc, vp = k_ref[cur, :], k_ref[prev, :], v_ref[cur, :], v_ref[prev, :]
            ok_c = jj <= ii
            ok_p = (jj >= ii) & (n > 0)
            outs, lses = [], []
            for hh in range(2):
                qm = jnp.where(lo if hh == 0 else ~lo, qn, jnp.zeros_like(qn))
                s_c = lax.dot_general(qm, kc, (((1,), (1,)), ((), ())), preferred_element_type=F32) * scale
                s_p = lax.dot_general(qm, kp, (((1,), (1,)), ((), ())), preferred_element_type=F32) * scale
                s_c = jnp.where(ok_c, s_c + bias_ref[2 * hh], NEG_INF)
                s_p = jnp.where(ok_p, s_p + bias_ref[2 * hh + 1], NEG_INF)
                m = jnp.maximum(jnp.max(s_c, axis=1, keepdims=True), jnp.max(s_p, axis=1, keepdims=True))
                e_c = jnp.exp(s_c - m)
                e_p = jnp.exp(s_p - m)
                l = jnp.sum(e_c, axis=1, keepdims=True) + jnp.sum(e_p, axis=1, keepdims=True)
                acc = (jnp.dot(e_c.astype(BF), vc, preferred_element_type=F32)
                       + jnp.dot(e_p.astype(BF), vp, preferred_element_type=F32))
                outs.append(acc / l)
                lses.append(m + jnp.log(l))
            o_ref[cur, :] = jnp.where(lo, outs[0], outs[1])
            lse_ref[cur, :] = jnp.where(lo, lses[0], lses[1])
            return 0

        lax.fori_loop(0, nb, blk, 0)

    col, out, tile, tbl = _dil_specs(group, dilation, length)
    bc, bp = buckets
    o, lse = pl.pallas_call(
        body, out_shape=[jax.ShapeDtypeStruct((length, dilation * D_MODEL), F32)] * 2, grid=(8, dilation),
        in_specs=[tbl, tile, tile, col(0), col(1), col(2)], out_specs=[out, out],
        scratch_shapes=[pltpu.VMEM((4, qb, qb), F32)], name=f"dilated_forward_{dilation}",
        compiler_params=_params(("arbitrary", "arbitrary")))(
            table, bc, bp, view, view, view)
    return o.reshape(rows, D_MODEL), lse.reshape(rows, D_MODEL)


def _dil_backward(view, group, dilation, table, buckets, do_g, lse, dlt):
    length = view.shape[0]
    rows = length * dilation
    nb = length // Q_BLOCK
    scale = HEAD_DIM ** -0.5
    qb = Q_BLOCK

    def body(tbl_ref, bc_ref, bp_ref, q_ref, k_ref, v_ref, do_ref, lse_ref, dlt_ref,
             dq_ref, dk_ref, dv_ref, db_ref, bias_ref, dk_acc, dv_acc):
        hp = pl.program_id(0)

        @pl.when(pl.program_id(1) == 0)
        def _():
            _dil_bias_tiles(tbl_ref, bc_ref, bp_ref, bias_ref, group, hp)
            db_ref[...] = jnp.zeros_like(db_ref)

        dk_acc[...] = jnp.zeros_like(dk_acc)
        dv_acc[...] = jnp.zeros_like(dv_acc)
        lo = lax.broadcasted_iota(jnp.int32, (1, LANE), 1) < HEAD_DIM
        ii = lax.broadcasted_iota(jnp.int32, (qb, qb), 0)
        jj = lax.broadcasted_iota(jnp.int32, (qb, qb), 1)
        tn = (((0,), (0,)), ((), ()))
        nt = (((1,), (1,)), ((), ()))

        def blk(n, _):
            cur = pl.ds(pl.multiple_of(n * qb, qb), qb)
            prev = pl.ds(pl.multiple_of(jnp.maximum(n - 1, 0) * qb, qb), qb)
            qn = q_ref[cur, :]
            don = do_ref[cur, :]
            kc, kp, vc, vp = k_ref[cur, :], k_ref[prev, :], v_ref[cur, :], v_ref[prev, :]
            lse_n = lse_ref[cur, :]
            dlt_n = dlt_ref[cur, :]
            ok_c = jj <= ii
            ok_p = (jj >= ii) & (n > 0)
            dqs = []
            dkc = jnp.zeros((qb, LANE), F32)
            dkp = jnp.zeros((qb, LANE), F32)
            dvc = jnp.zeros((qb, LANE), F32)
            dvp = jnp.zeros((qb, LANE), F32)
            for hh in range(2):
                mask = lo if hh == 0 else ~lo
                qm = jnp.where(mask, qn, jnp.zeros_like(qn))
                dom = jnp.where(mask, don, jnp.zeros_like(don))
                lse_h = jnp.max(jnp.where(mask, lse_n, -3e38), axis=1, keepdims=True)
                dlt_h = jnp.max(jnp.where(mask, dlt_n, -3e38), axis=1, keepdims=True)
                s_c = lax.dot_general(qm, kc, nt, preferred_element_type=F32) * scale
                s_p = lax.dot_general(qm, kp, nt, preferred_element_type=F32) * scale
                p_c = jnp.exp(jnp.where(ok_c, s_c + bias_ref[2 * hh], NEG_INF) - lse_h)
                p_p = jnp.exp(jnp.where(ok_p, s_p + bias_ref[2 * hh + 1], NEG_INF) - lse_h)
                ds_c = p_c * (lax.dot_general(dom, vc, nt, preferred_element_type=F32) - dlt_h)
                ds_p = p_p * (lax.dot_general(dom, vp, nt, preferred_element_type=F32) - dlt_h)
                db_ref[0, 2 * hh] += ds_c
                db_ref[0, 2 * hh + 1] += ds_p
                dsc_b, dsp_b = ds_c.astype(BF), ds_p.astype(BF)
                dqs.append(jnp.dot(dsc_b, kc, preferred_element_type=F32)
                           + jnp.dot(dsp_b, kp, preferred_element_type=F32))
                dkc = dkc + lax.dot_general(dsc_b, qm, tn, preferred_element_type=F32)
                dkp = dkp + lax.dot_general(dsp_b, qm, tn, preferred_element_type=F32)
                dvc = dvc + lax.dot_general(p_c.astype(BF), dom, tn, preferred_element_type=F32)
                dvp = dvp + lax.dot_general(p_p.astype(BF), dom, tn, preferred_element_type=F32)
            dq_ref[cur, :] = (jnp.where(lo, dqs[0], dqs[1]) * scale).astype(BF)
            dk_acc[cur, :] += dkc
            dk_acc[prev, :] += dkp
            dv_acc[cur, :] += dvc
            dv_acc[prev, :] += dvp
            return 0

        lax.fori_loop(0, nb, blk, 0)
        dk_ref[...] = (dk_acc[...] * scale).astype(BF)
        dv_ref[...] = dv_acc[...].astype(BF)

    col, out, tile, tbl = _dil_specs(group, dilation, length)
    bc, bp = buckets
    wide = (length, dilation * D_MODEL)
    dq, dk, dv, db = pl.pallas_call(
        body, out_shape=[jax.ShapeDtypeStruct(wide, BF)] * 3 + [jax.ShapeDtypeStruct((8, 4, qb, qb), F32)],
        grid=(8, dilation), in_specs=[tbl, tile, tile, col(0), col(1), col(2), out, out, out],
        out_specs=[out, out, out, pl.BlockSpec((1, 4, qb, qb), lambda hp, r: (hp, 0, 0, 0))],
        scratch_shapes=[pltpu.VMEM((4, qb, qb), F32), pltpu.VMEM((length, LANE), F32), pltpu.VMEM((length, LANE), F32)],
        name=f"dilated_backward_{dilation}", compiler_params=_params(("arbitrary", "arbitrary")))(
            table, bc, bp, view, view, view,
            do_g.reshape(wide), lse.reshape(wide), dlt.reshape(wide))
    return dq.reshape(rows, D_MODEL), dk.reshape(rows, D_MODEL), dv.reshape(rows, D_MODEL), db


def _head_sums(x, lo):
    s0 = jnp.sum(jnp.where(lo, x, 0.0), axis=1, keepdims=True)
    s1 = jnp.sum(jnp.where(lo, 0.0, x), axis=1, keepdims=True)
    return jnp.where(lo, s0, s1)


def _dil_merge_forward(outs, lses):
    rows = outs[0].shape[0]

    def body(o0, o1, o2, l0, l1, l2, o_ref):
        ls = [l0[...], l1[...], l2[...]]
        m = jnp.maximum(jnp.maximum(ls[0], ls[1]), ls[2])
        es = [jnp.exp(v - m) for v in ls]
        tot = es[0] + es[1] + es[2]
        o_ref[...] = ((es[0] * o0[...] + es[1] * o1[...] + es[2] * o2[...]) / tot).astype(BF)

    blk = pl.BlockSpec((ROW_TILE, LANE), lambda i, j: (i, j))
    return pl.pallas_call(body, out_shape=jax.ShapeDtypeStruct((rows, D_MODEL), BF), grid=(rows // ROW_TILE, 8),
                          in_specs=[blk] * 6, out_specs=blk, name="dilated_merge_forward",
                          compiler_params=_params(("parallel", "parallel")))(*outs, *lses)


def _dil_merge_backward(outs, lses, do):
    rows = outs[0].shape[0]

    def body(o0, o1, o2, l0, l1, l2, do_ref, d0, d1, d2, t0, t1, t2):
        lo = lax.broadcasted_iota(jnp.int32, (1, LANE), 1) < HEAD_DIM
        ls = [l0[...], l1[...], l2[...]]
        os_ = [o0[...], o1[...], o2[...]]
        m = jnp.maximum(jnp.maximum(ls[0], ls[1]), ls[2])
        es = [jnp.exp(v - m) for v in ls]
        tot = es[0] + es[1] + es[2]
        alphas = [e / tot for e in es]
        dov = do_ref[...]
        merged = alphas[0] * os_[0] + alphas[1] * os_[1] + alphas[2] * os_[2]
        dot = _head_sums(dov * merged, lo)
        for a, d_ref, t_ref in zip(alphas, (d0, d1, d2), (t0, t1, t2)):
            d_ref[...] = (a * dov).astype(BF)
            t_ref[...] = a * dot

    blk = pl.BlockSpec((ROW_TILE, LANE), lambda i, j: (i, j))
    res = pl.pallas_call(
        body, out_shape=[jax.ShapeDtypeStruct((rows, D_MODEL), BF)] * 3 + [jax.ShapeDtypeStruct((rows, D_MODEL), F32)] * 3,
        grid=(rows // ROW_TILE, 8), in_specs=[blk] * 7, out_specs=[blk] * 6, name="dilated_merge_backward",
        compiler_params=_params(("parallel", "parallel")))(*outs, *lses, do)
    return res[:3], res[3:]


def _rel_bias_grad(dbs, buckets):
    def body(db_ref, bc_ref, bp_ref, o_ref):
        g = pl.program_id(0)
        hp = pl.program_id(1)

        @pl.when((g == 0) & (hp == 0))
        def _():
            o_ref[...] = jnp.zeros_like(o_ref)

        rr = lax.broadcasted_iota(jnp.int32, (REL_BUCKETS, LANE), 0)
        cc = lax.broadcasted_iota(jnp.int32, (REL_BUCKETS, LANE), 1)
        bc = bc_ref[0]
        bp = bp_ref[0]
        acc = jnp.zeros((REL_BUCKETS, LANE), F32)
        for hh in range(2):
            col = g * HEADS + 2 * hp + hh
            d_c = db_ref[0, 0, 2 * hh]
            d_p = db_ref[0, 0, 2 * hh + 1]
            for b in range(REL_BUCKETS):
                val = (jnp.sum(jnp.where(bc == b, d_c, 0.0), keepdims=True)
                       + jnp.sum(jnp.where(bp == b, d_p, 0.0), keepdims=True))
                acc = jnp.where((rr == b) & (cc == col), val, acc)
        o_ref[...] += acc

    db_all = jnp.stack(dbs)
    bc_all = jnp.stack([b[0] for b in buckets])
    bp_all = jnp.stack([b[1] for b in buckets])
    tile = pl.BlockSpec((1, Q_BLOCK, Q_BLOCK), lambda g, hp: (g, 0, 0))
    return pl.pallas_call(
        body, out_shape=jax.ShapeDtypeStruct((REL_BUCKETS, LANE), F32), grid=(3, 8),
        in_specs=[pl.BlockSpec((1, 1, 4, Q_BLOCK, Q_BLOCK), lambda g, hp: (g, hp, 0, 0, 0)), tile, tile],
        out_specs=pl.BlockSpec((REL_BUCKETS, LANE), lambda g, hp: (0, 0)), name="rel_bias_grad",
        compiler_params=_params(("arbitrary", "arbitrary")))(db_all, bc_all, bp_all)


def _mla_forward(hn, w, tables):
    a = _matmul(hn, w["w_a"], name="mla_a")
    cq, ckv, kr = _mla_mid_forward(a, w["q_norm"], w["kv_norm"], tables)
    q_raw = _matmul(cq, w["w_uq"], name="mla_uq")
    q = _rope_heads(q_raw, tables, False, "rope_forward")
    kv = _matmul(ckv, w["w_ukv"], b_chunks=True, out_dtype=BF, name="mla_ukv")
    scale = (HEAD_DIM + MLA_ROPE) ** -0.5
    o, lse = _attn_forward(q, kv, 0, kr, None, None, scale, MLA_GROUP, "mla_attention_forward")
    y = _matmul(o, w["w_o"], name="attn_out")
    return y, dict(hn=hn, a=a, cq=cq, ckv=ckv, kr=kr, q=q, kv=kv, o=o, lse=lse)


def _mla_backward(dy, w, s, tables):
    scale = (HEAD_DIM + MLA_ROPE) ** -0.5
    g = {}
    g["w_o"] = _matmul(s["o"], dy, ta=True, out_dtype=BF, name="attn_out_dw")
    do = _matmul(dy, w["w_o"], tb=True, out_dtype=BF, name="attn_out_dx")
    dq, dkv, dkr = _attn_backward(s["q"], s["kv"], 0, s["kr"], None, None, s["o"], do, s["lse"], scale,
                                  MLA_GROUP, "mla_attention_backward")
    dq_raw = _rope_heads(dq, tables, True, "rope_backward")
    g["w_uq"] = _matmul(s["cq"], dq_raw, ta=True, out_dtype=BF, name="mla_uq_dw")
    dcq = _matmul(dq_raw, w["w_uq"], tb=True, name="mla_uq_dx")
    g["w_ukv"] = _matmul(s["ckv"], dkv, ta=True, out_chunks=True, out_dtype=BF, name="mla_ukv_dw")
    dckv = _matmul(dkv, w["w_ukv"], tb=True, b_chunks=True, name="mla_ukv_dx")
    da, g["q_norm"], g["kv_norm"] = _mla_mid_backward(s["a"], w["q_norm"], w["kv_norm"], tables, dcq, dckv, dkr)
    g["w_a"] = _matmul(s["hn"], da, ta=True, out_dtype=BF, name="mla_a_dw")
    dhn = _matmul(da, w["w_a"], tb=True, name="mla_a_dx")
    return dhn, g


def _fox_forward(hn, w):
    qkv = _matmul(hn, w["w_qkv"], out_dtype=BF, name="fox_qkv")
    f_raw = _matmul(hn, w["w_f"], name="fox_f")
    cum = _forget_forward(f_raw, w["b_f"])
    cum_heads = cum[:, :HEADS].T
    cum_col, cum_row = cum_heads[:, :, None], cum_heads[:, None, :]
    o, lse = _attn_forward(qkv, qkv, HEADS, None, cum_col, cum_row, HEAD_DIM ** -0.5, FOX_GROUP,
                           "fox_attention_forward")
    y = _matmul(o, w["w_o"], name="attn_out")
    return y, dict(hn=hn, qkv=qkv, f_raw=f_raw, cum_col=cum_col, cum_row=cum_row, o=o, lse=lse)


def _fox_backward(dy, w, s):
    g = {}
    g["w_o"] = _matmul(s["o"], dy, ta=True, out_dtype=BF, name="attn_out_dw")
    do = _matmul(dy, w["w_o"], tb=True, out_dtype=BF, name="attn_out_dx")
    dq, dkv, dck, dcq = _attn_backward(s["qkv"], s["qkv"], HEADS, None, s["cum_col"], s["cum_row"], s["o"], do,
                                       s["lse"], HEAD_DIM ** -0.5, FOX_GROUP, "fox_attention_backward")
    dcum = jnp.pad((dck[:, 0, :] + dcq[:, :, 0]).T, ((0, 0), (0, LANE - HEADS)))
    df, g["b_f"] = _forget_backward(s["f_raw"], w["b_f"], dcum)
    dqkv = jnp.concatenate([dq, dkv], axis=1)
    g["w_qkv"] = _matmul(s["hn"], dqkv, ta=True, out_dtype=BF, name="fox_qkv_dw")
    g["w_f"] = _matmul(s["hn"], df, ta=True, out_dtype=BF, name="fox_f_dw")
    dhn = _matmul(dqkv, w["w_qkv"], tb=True, name="fox_qkv_dx")
    dhn = _matmul(df, w["w_f"], tb=True, add=dhn, name="fox_f_dx")
    return dhn, g


def _dil_mixer_forward(hn, w, buckets):
    qkv = _matmul(hn, w["w_qkv"], b_chunks=True, out_dtype=BF, name="dil_qkv")
    views = [_dil_view(qkv, grp, dilation) for grp, (_, dilation) in enumerate(DIL_PATTERNS)]
    outs, lses = [], []
    for grp, (_, dilation) in enumerate(DIL_PATTERNS):
        o_g, lse_g = _dil_forward(views[grp], grp, dilation, w["rel_bias"], buckets[grp])
        outs.append(o_g)
        lses.append(lse_g)
    o = _dil_merge_forward(outs, lses)
    y = _matmul(o, w["w_o"], name="dil_out")
    return y, dict(hn=hn, views=views, outs=outs, lses=lses, o=o)


def _dil_mixer_backward(dy, w, s, buckets):
    g = {}
    g["w_o"] = _matmul(s["o"], dy, ta=True, out_dtype=BF, name="dil_out_dw")
    do = _matmul(dy, w["w_o"], tb=True, name="dil_out_dx")
    do_gs, dlts = _dil_merge_backward(s["outs"], s["lses"], do)
    parts, dbs = [], []
    for grp, (_, dilation) in enumerate(DIL_PATTERNS):
        dq, dk, dv, db = _dil_backward(s["views"][grp], grp, dilation, w["rel_bias"], buckets[grp], do_gs[grp],
                                       s["lses"][grp], dlts[grp])
        parts += [dq, dk, dv]
        dbs.append(db)
    dqkv = jnp.concatenate(parts, axis=1)
    g["rel_bias"] = _rel_bias_grad(dbs, buckets)
    g["w_qkv"] = _matmul(s["hn"], dqkv, ta=True, out_chunks=True, out_dtype=BF, name="dil_qkv_dw")
    dhn = _matmul(dqkv, w["w_qkv"], tb=True, b_chunks=True, name="dil_qkv_dx")
    return dhn, g


def _mixer_weights(i, lw, small):
    mixer, j = i % N_MIXERS, i // N_MIXERS
    if mixer == 0:
        return dict(lw["mixer"], q_norm=small["mla_q_norm"][j][None, :], kv_norm=small["mla_kv_norm"][j][None, :])
    if mixer == 1:
        return dict(lw["mixer"], rel_bias=small["rel_bias"])
    return dict(lw["mixer"], b_f=jnp.pad(small["fox_b_f"][j][None, :], ((0, 0), (0, LANE - HEADS))))


MIXER_PART, COMMON_PART = 0, 1


def _run_layers(x, p, positions, target, get_part, get_small, put_part):
    tables = _rope_tables(positions)
    buckets = [_dil_buckets(d) for _, d in DIL_PATTERNS]
    layers, saved = [], []
    h = x
    first = get_part(0, MIXER_PART, positions)
    small = get_small()

    def gain(i, k):
        return small["norm_g"][i, k][None, :]

    hn = _prenorm(h, gain(0, 0))
    sq = dh = None
    for i in range(DEPTH):
        mixer = i % N_MIXERS
        lw = dict(mixer=first if i == 0 else get_part(i, MIXER_PART, h))
        mw = _mixer_weights(i, lw, small)
        if mixer == 0:
            y, ms = _mla_forward(hn, mw, tables)
        elif mixer == 1:
            y, ms = _dil_mixer_forward(hn, mw, buckets)
        else:
            y, ms = _fox_forward(hn, mw)
        lw.update(get_part(i, COMMON_PART, y))
        layers.append(lw)
        h1, hn2 = _post_residual(h, y, gain(i, 1), gain(i, 2))
        gu = _matmul(hn2, lw["ffn_w_in"], b_chunks=True, out_dtype=BF, name="ffn_in")
        act = _swiglu_forward(gu)
        f = _matmul(act, lw["ffn_w_out"], name="ffn_out")
        h2, h2b = _post_residual(h1, f, gain(i, 3), None)
        pp = _matmul(p[i], lw["ple_w_proj"], b_chunks=True, name="ple_proj")
        z = _matmul(h2b, lw["ple_w_gate"], name="ple_gate")
        saved.append(dict(h=h, y=y, ms=ms, h1=h1, hn2=hn2, gu=gu, act=act, f=f, h2b=h2b, pp=pp, z=z))
        if i + 1 < DEPTH:
            h, hn = _ple_forward(h2, pp, z, gain(i + 1, 0))
        else:
            dh, sq = _ple_loss(h2, pp, z, target)

    norm_rows = [[None] * 4 for _ in range(DEPTH)]
    sg = dict(mla_q_norm={}, mla_kv_norm={}, rel_bias=None, fox_b_f={})
    for i in reversed(range(DEPTH)):
        s, lw = saved[i], layers[i]
        mixer, j = i % N_MIXERS, i // N_MIXERS
        mw = _mixer_weights(i, lw, small)
        lg = {}
        dpp, dz = _ple_backward(dh, s["pp"], s["z"])
        lg["ple_w_proj"] = _matmul(p[i], dpp, ta=True, out_chunks=True, out_dtype=BF, name="ple_proj_dw")
        lg["ple_w_gate"] = _matmul(s["h2b"], dz, ta=True, out_dtype=BF, name="ple_gate_dw")
        dh2 = _matmul(dz, lw["ple_w_gate"], tb=True, add=dh, name="ple_gate_dx")
        df, norm_rows[i][3] = _rms_backward(s["f"], gain(i, 3), dh2, None, BF)
        lg["ffn_w_out"] = _matmul(s["act"], df, ta=True, out_dtype=BF, name="ffn_out_dw")
        dact = _matmul(df, lw["ffn_w_out"], tb=True, out_dtype=BF, name="ffn_out_dx")
        dgu = _swiglu_backward(s["gu"], dact)
        lg["ffn_w_in"] = _matmul(s["hn2"], dgu, ta=True, out_chunks=True, out_dtype=BF, name="ffn_in_dw")
        token = put_part(i, COMMON_PART, lg)
        dhn2 = _matmul(dgu, lw["ffn_w_in"], tb=True, b_chunks=True, name="ffn_in_dx")
        dh1, norm_rows[i][2] = _rms_backward(s["h1"], gain(i, 2), dhn2, dh2, F32)
        dy, norm_rows[i][1] = _rms_backward(s["y"], gain(i, 1) + token[0:1, 0:1], dh1, None, BF)
        if mixer == 0:
            dhn, mg = _mla_backward(dy, mw, s["ms"], tables)
            sg["mla_q_norm"][j] = mg.pop("q_norm")
            sg["mla_kv_norm"][j] = mg.pop("kv_norm")
        elif mixer == 1:
            dhn, mg = _dil_mixer_backward(dy, mw, s["ms"], buckets)
            rel = mg.pop("rel_bias")[:, :3 * HEADS]
            sg["rel_bias"] = rel if sg["rel_bias"] is None else sg["rel_bias"] + rel
        else:
            dhn, mg = _fox_backward(dy, mw, s["ms"])
            sg["fox_b_f"][j] = mg.pop("b_f")[:, :HEADS]
        token = put_part(i, MIXER_PART, mg)
        dh, norm_rows[i][0] = _rms_backward(s["h"], gain(i, 0) + token[0:1, 0:1], dhn, dh1, F32)
    small_grads = dict(norm_g=jnp.stack([jnp.concatenate(row, axis=0) for row in norm_rows]),
                       rel_bias=sg["rel_bias"])
    for k in ("mla_q_norm", "mla_kv_norm", "fox_b_f"):
        small_grads[k] = jnp.concatenate([sg[k][j] for j in sorted(sg[k])], axis=0)
    return sq, dh, small_grads


COL_SHARDED = ("ffn_w_in", "ple_w_proj", "mla_w_uq", "mla_w_ukv", "dil_w_qkv", "fox_w_qkvf")
ROW_SHARDED = ("ffn_w_out", "ple_w_gate", "mla_w_a", "mla_w_o", "dil_w_o", "fox_w_o")
BIG = ("ffn_w_in", "ffn_w_out", "ple_w_proj", "ple_w_gate", "mla_w_a", "mla_w_uq", "mla_w_ukv", "mla_w_o",
       "dil_w_qkv", "dil_w_o", "fox_w_qkvf", "fox_w_o")
SMALL_SHARDED = ("norm_g", "mla_q_norm", "mla_kv_norm")
SMALL_REPLICATED = ("rel_bias", "fox_b_f")
WEIGHTS = ("norm_g", "ffn_w_in", "ffn_w_out", "ple_w_proj", "ple_w_gate", "rel_bias", "mla_w_a", "mla_q_norm",
           "mla_kv_norm", "mla_w_uq", "mla_w_ukv", "mla_w_o", "dil_w_qkv", "dil_w_o", "fox_w_qkvf", "fox_b_f", "fox_w_o")


LAYER_COMMON = ("ffn_w_in", "ffn_w_out", "ple_w_proj", "ple_w_gate")
MIXER_WEIGHTS = (("mla_w_a", "mla_w_uq", "mla_w_ukv", "mla_w_o"), ("dil_w_qkv", "dil_w_o"), ("fox_w_qkvf", "fox_w_o"))


def _part_names(i, part):
    return MIXER_WEIGHTS[i % N_MIXERS] if part == MIXER_PART else LAYER_COMMON


def _layer_slot(name, i):
    return i if name in LAYER_COMMON else i // N_MIXERS


def _merge_rows(chunks):
    n, r, c = chunks.shape
    return chunks.reshape(n * r, c)


def _merge_cols(chunks):
    n, r, c = chunks.shape
    return chunks.transpose(1, 0, 2).reshape(r, n * c)


def _pad_heads_out(wo):
    w3 = wo.reshape(HEADS, HEAD_DIM, D_MODEL)
    return jnp.pad(w3, ((0, 0), (HEAD_DIM, 0), (0, 0))).reshape(HEADS * LANE, D_MODEL)


def _part_to_compute(i, part, ch):
    if part == COMMON_PART:
        return dict(ffn_w_in=ch["ffn_w_in"], ffn_w_out=_merge_rows(ch["ffn_w_out"]), ple_w_proj=ch["ple_w_proj"],
                    ple_w_gate=_merge_rows(ch["ple_w_gate"]))
    lw = {}
    mixer = i % N_MIXERS
    if mixer == 0:
        wa = _merge_rows(ch["mla_w_a"])
        rank = MLA_Q_RANK + MLA_KV_RANK
        wa_p = jnp.concatenate([wa[:, :rank], jnp.zeros((wa.shape[0], 64), wa.dtype), wa[:, rank:],
                                jnp.zeros((wa.shape[0], 32), wa.dtype)], axis=1)
        wuq = _merge_cols(ch["mla_w_uq"]).reshape(MLA_Q_RANK, HEADS, HEAD_DIM + MLA_ROPE)
        wuq_p = jnp.pad(wuq, ((0, 0), (0, 0), (0, LANE - HEAD_DIM - MLA_ROPE))).reshape(MLA_Q_RANK, HEADS * LANE)
        lw["mixer"] = dict(w_a=wa_p, w_uq=wuq_p, w_ukv=ch["mla_w_ukv"], w_o=_pad_heads_out(_merge_rows(ch["mla_w_o"])))
    elif mixer == 1:
        lw["mixer"] = dict(w_qkv=ch["dil_w_qkv"], w_o=_merge_rows(ch["dil_w_o"]))
    else:
        wf = _merge_cols(ch["fox_w_qkvf"])
        inner = HEADS * HEAD_DIM
        q3 = wf[:, :inner].reshape(D_MODEL, HEADS, HEAD_DIM)
        k3 = wf[:, inner:2 * inner].reshape(D_MODEL, HEADS, HEAD_DIM)
        v3 = wf[:, 2 * inner:3 * inner].reshape(D_MODEL, HEADS, HEAD_DIM)
        q_p = jnp.pad(q3, ((0, 0), (0, 0), (0, HEAD_DIM))).reshape(D_MODEL, HEADS * LANE)
        kv_p = jnp.concatenate([k3, v3], axis=2).reshape(D_MODEL, HEADS * LANE)
        f_p = jnp.pad(wf[:, 3 * inner:], ((0, 0), (0, LANE - HEADS)))
        lw["mixer"] = dict(w_qkv=jnp.concatenate([q_p, kv_p], axis=1), w_f=f_p,
                           w_o=_pad_heads_out(_merge_rows(ch["fox_w_o"])))
    return lw["mixer"]


def _part_contributions(i, part, lg, chunk_shapes):
    spec = {k: jax.ShapeDtypeStruct(s, BF) for k, s in chunk_shapes.items()}
    (contrib,) = jax.linear_transpose(functools.partial(_part_to_compute, i, part), spec)(lg)
    return contrib


def _chip_peers():
    x, y, c = lax.axis_index("x"), lax.axis_index("y"), lax.axis_index("c")
    peers = [(1 - x, y), (x, 1 - y), (1 - x, 1 - y)]
    return x, y, c, peers


SEM_SPEC = pl.BlockSpec(memory_space=pltpu.SEMAPHORE)
ANY_SPEC = pl.BlockSpec(memory_space=pl.ANY)
SPLIT_EFFECT = pltpu.SideEffectType.DATAFLOW_SIDE_EFFECTING


def _spread_copy(src, land, exchange, k, peer, c, send_sems, recv_sems, index, slot):
    px, py = peer
    return pltpu.make_async_remote_copy(
        src_ref=src.at[2 * px + py] if exchange else src, dst_ref=land.at[slot],
        send_sem=send_sems.at[3 * index + k], recv_sem=recv_sems.at[3 * index + k],
        device_id=(px, py, c), device_id_type=MESH)


def _spread_start(srcs, exchange, after, name):
    n = len(srcs)
    lands = [lax.empty(s.shape if exchange else (N_CHIPS,) + s.shape, s.dtype) for s in srcs]

    def body(*refs):
        src, land = refs[:n], refs[n:2 * n]
        send_sems, recv_sems = refs[2 * n + 1], refs[2 * n + 2]
        token = refs[-1]
        x, y, c, peers = _chip_peers()
        me = 2 * x + y
        for w in range(n):
            for k, peer in enumerate(peers):
                _spread_copy(src[w], land[w], exchange, k, peer, c, send_sems, recv_sems, w, me).start()
        token[...] = jnp.zeros_like(token)

    hbm = [pltpu.with_memory_space_constraint(a, pltpu.HBM) for a in list(srcs) + lands]
    out = pl.pallas_call(
        body, name=name,
        out_shape=(pltpu.SemaphoreType.DMA((3 * n,)), pltpu.SemaphoreType.DMA((3 * n,)),
                   *[pltpu.HBM(a.shape, a.dtype) for a in hbm], jax.ShapeDtypeStruct((8, LANE), F32)),
        in_specs=[HBM_SPEC] * (2 * n) + [ANY_SPEC],
        out_specs=(SEM_SPEC, SEM_SPEC, *[HBM_SPEC] * (2 * n), pl.BlockSpec(memory_space=pltpu.VMEM)),
        input_output_aliases={w: 2 + w for w in range(2 * n)},
        compiler_params=pltpu.CompilerParams(has_side_effects=SPLIT_EFFECT))(*hbm, after)
    return dict(send=out[0], recv=out[1], srcs=out[2:2 + n], lands=out[2 + n:2 + 2 * n], token=out[-1],
                exchange=exchange)


def _spread_wait(handle, after, name):
    n = len(handle["srcs"])
    exchange = handle["exchange"]

    def body(*refs):
        src, land = refs[:n], refs[n:2 * n]
        send_sems, recv_sems = refs[2 * n], refs[2 * n + 1]
        _, _, c, peers = _chip_peers()
        for w in range(n):
            for k, peer in enumerate(peers):
                cp = _spread_copy(src[w], land[w], exchange, k, peer, c, send_sems, recv_sems, w, 2 * peer[0] + peer[1])
                cp.wait_send()
                cp.wait_recv()

    arrays = list(handle["srcs"]) + list(handle["lands"])
    out = pl.pallas_call(
        body, name=name, out_shape=tuple(pltpu.HBM(a.shape, a.dtype) for a in arrays),
        in_specs=[HBM_SPEC] * (2 * n) + [SEM_SPEC, SEM_SPEC, ANY_SPEC], out_specs=tuple([HBM_SPEC] * (2 * n)),
        input_output_aliases={w: w for w in range(2 * n)},
        compiler_params=pltpu.CompilerParams(has_side_effects=SPLIT_EFFECT))(*arrays, handle["send"], handle["recv"], after)
    me = 2 * lax.axis_index("x") + lax.axis_index("y")
    filled = []
    for src, land in zip(out[:n], out[n:]):
        own = lax.dynamic_index_in_dim(src, me, 0, keepdims=True) if exchange else src[None]
        filled.append(lax.dynamic_update_index_in_dim(land, own, me, 0))
    return filled


def _exchange_sibling(arrays, name):
    n = len(arrays)

    def body(*refs):
        ins, outs = refs[:n], refs[n:2 * n]
        send_sems, recv_sems = refs[2 * n:]
        x, y, c = lax.axis_index("x"), lax.axis_index("y"), lax.axis_index("c")
        copies = [pltpu.make_async_remote_copy(src_ref=ins[w], dst_ref=outs[w], send_sem=send_sems.at[w],
                                               recv_sem=recv_sems.at[w], device_id=(x, y, 1 - c), device_id_type=MESH)
                  for w in range(n)]
        for cp in copies:
            cp.start()
        for cp in copies:
            cp.wait_recv()
        for cp in copies:
            cp.wait_send()

    return pl.pallas_call(
        body, out_shape=[jax.ShapeDtypeStruct(s.shape, s.dtype) for s in arrays],
        in_specs=[HBM_SPEC] * n, out_specs=[HBM_SPEC] * n,
        scratch_shapes=[pltpu.SemaphoreType.DMA((n,)), pltpu.SemaphoreType.DMA((n,))], name=name)(*arrays)


def _all_reduce_small(v):
    rows = v.shape[0]

    def body(v_ref, sum_ref, slots, send_sems, recv_sems):
        x, y, c = lax.axis_index("x"), lax.axis_index("y"), lax.axis_index("c")
        me = 4 * x + 2 * y + c
        slots[me] = v_ref[...]
        sends = []
        for k in range(1, N_DEV):
            bx, by, bc = (k >> 2) & 1, (k >> 1) & 1, k & 1
            peer = (x ^ bx, y ^ by, c ^ bc)
            rc = pltpu.make_async_remote_copy(src_ref=v_ref, dst_ref=slots.at[me], send_sem=send_sems.at[k],
                                              recv_sem=recv_sems.at[k], device_id=peer, device_id_type=MESH)
            rc.start()
            sends.append(rc)
        for k in range(1, N_DEV):
            bx, by, bc = (k >> 2) & 1, (k >> 1) & 1, k & 1
            src = 4 * (x ^ bx) + 2 * (y ^ by) + (c ^ bc)
            pltpu.make_async_remote_copy(src_ref=v_ref, dst_ref=slots.at[src], send_sem=send_sems.at[k],
                                         recv_sem=recv_sems.at[k], device_id=(x ^ bx, y ^ by, c ^ bc),
                                         device_id_type=MESH).wait_recv()
        for rc in sends:
            rc.wait_send()
        total = slots[0]
        for k in range(1, N_DEV):
            total = total + slots[k]
        sum_ref[...] = total

    vm = pl.BlockSpec(memory_space=pltpu.VMEM)
    return pl.pallas_call(
        body, out_shape=jax.ShapeDtypeStruct((rows, LANE), F32), in_specs=[vm], out_specs=vm,
        scratch_shapes=[pltpu.VMEM((N_DEV, rows, LANE), F32), pltpu.SemaphoreType.DMA((N_DEV,)),
                        pltpu.SemaphoreType.DMA((N_DEV,))], name="all_reduce_small")(v)


def _as_2d(a):
    return a.reshape(-1, a.shape[-1])


def _row_tile(rows, cols):
    for t in (512, 256, 128, 64, 32, 16):
        if rows % t == 0 and t * cols * 4 <= (1 << 20):
            return t
    return rows


def _sum_chips_into(parts, stacked, slot):
    _, rows, cols = parts.shape
    tr = _row_tile(rows, cols)
    first = slot * (rows // tr)

    def body(p_ref, _, o_ref):
        total = p_ref[0].astype(F32)
        for k in range(1, N_CHIPS):
            total = total + p_ref[k].astype(F32)
        o_ref[...] = total

    return pl.pallas_call(body, out_shape=jax.ShapeDtypeStruct(stacked.shape, F32), grid=(rows // tr,),
                          in_specs=[pl.BlockSpec((N_CHIPS, tr, cols), lambda i: (0, i, 0)), ANY_SPEC],
                          out_specs=pl.BlockSpec((tr, cols), lambda i: (first + i, 0)), input_output_aliases={1: 0},
                          name="sum_chips", compiler_params=_params(("parallel",)))(parts, stacked)


def _adamw_math(w, g, m, v):
    m = ADAM_B1 * m + (1.0 - ADAM_B1) * g
    v = ADAM_B2 * v + (1.0 - ADAM_B2) * (g * g)
    m_hat = m / (1.0 - ADAM_B1 ** ADAM_STEP)
    v_hat = v / (1.0 - ADAM_B2 ** ADAM_STEP)
    delta = -ADAM_LR * (m_hat / (jnp.sqrt(v_hat) + ADAM_EPS) + ADAM_WD * w)
    return delta, m, v


def _adamw(w, m, v, g_mine, g_sibling):
    rows, cols = w.shape
    tr = _row_tile(rows, cols)
    two = g_sibling is not None

    def body(*refs):
        if two:
            w_ref, m_ref, v_ref, ga_ref, gb_ref, g_ref, d_ref, nm_ref, nv_ref = refs
            g = ga_ref[...] + gb_ref[...]
        else:
            w_ref, m_ref, v_ref, ga_ref, g_ref, d_ref, nm_ref, nv_ref = refs
            g = ga_ref[...]
        delta, nm, nv = _adamw_math(w_ref[...], g, m_ref[...], v_ref[...])
        g_ref[...] = g
        d_ref[...] = delta
        nm_ref[...] = nm
        nv_ref[...] = nv

    blk = pl.BlockSpec((tr, cols), lambda i: (i, 0))
    args = [w, m, v, g_mine] + ([g_sibling] if two else [])
    return pl.pallas_call(body, out_shape=[jax.ShapeDtypeStruct((rows, cols), F32)] * 4, grid=(rows // tr,),
                          in_specs=[blk] * len(args), out_specs=[blk] * 4, name="adamw",
                          compiler_params=_params(("parallel",)))(*args)


def _pack_rows(arrays):
    flat = jnp.concatenate([a.reshape(-1) for a in arrays])
    rows = -(-flat.shape[0] // (8 * LANE)) * 8
    return jnp.pad(flat, (0, rows * LANE - flat.shape[0])).reshape(rows, LANE)


def _unpack_rows(packed, shapes):
    flat = packed.reshape(-1)
    out, at = [], 0
    for s in shapes:
        size = math.prod(s)
        out.append(flat[at:at + size].reshape(s))
        at += size
    return out


def kernel(x, p, positions, norm_g, ffn_w_in, ffn_w_out, ple_w_proj, ple_w_gate, rel_bias, mla_w_a, mla_q_norm, mla_kv_norm, mla_w_uq, mla_w_ukv, mla_w_o, dil_w_qkv, dil_w_o, fox_w_qkvf, fox_b_f, fox_w_o, loss_target, m_norm_g, m_ffn_w_in, m_ffn_w_out, m_ple_w_proj, m_ple_w_gate, m_rel_bias, m_mla_w_a, m_mla_q_norm, m_mla_kv_norm, m_mla_w_uq, m_mla_w_ukv, m_mla_w_o, m_dil_w_qkv, m_dil_w_o, m_fox_w_qkvf, m_fox_b_f, m_fox_w_o, v_norm_g, v_ffn_w_in, v_ffn_w_out, v_ple_w_proj, v_ple_w_gate, v_rel_bias, v_mla_w_a, v_mla_q_norm, v_mla_kv_norm, v_mla_w_uq, v_mla_w_ukv, v_mla_w_o, v_dil_w_qkv, v_dil_w_o, v_fox_w_qkvf, v_fox_b_f, v_fox_w_o):
    w = dict(norm_g=norm_g, ffn_w_in=ffn_w_in, ffn_w_out=ffn_w_out, ple_w_proj=ple_w_proj, ple_w_gate=ple_w_gate,
             rel_bias=rel_bias, mla_w_a=mla_w_a, mla_q_norm=mla_q_norm, mla_kv_norm=mla_kv_norm, mla_w_uq=mla_w_uq,
             mla_w_ukv=mla_w_ukv, mla_w_o=mla_w_o, dil_w_qkv=dil_w_qkv, dil_w_o=dil_w_o, fox_w_qkvf=fox_w_qkvf,
             fox_b_f=fox_b_f, fox_w_o=fox_w_o)
    m = dict(norm_g=m_norm_g, ffn_w_in=m_ffn_w_in, ffn_w_out=m_ffn_w_out, ple_w_proj=m_ple_w_proj,
             ple_w_gate=m_ple_w_gate, rel_bias=m_rel_bias, mla_w_a=m_mla_w_a, mla_q_norm=m_mla_q_norm,
             mla_kv_norm=m_mla_kv_norm, mla_w_uq=m_mla_w_uq, mla_w_ukv=m_mla_w_ukv, mla_w_o=m_mla_w_o,
             dil_w_qkv=m_dil_w_qkv, dil_w_o=m_dil_w_o, fox_w_qkvf=m_fox_w_qkvf, fox_b_f=m_fox_b_f, fox_w_o=m_fox_w_o)
    v = dict(norm_g=v_norm_g, ffn_w_in=v_ffn_w_in, ffn_w_out=v_ffn_w_out, ple_w_proj=v_ple_w_proj,
             ple_w_gate=v_ple_w_gate, rel_bias=v_rel_bias, mla_w_a=v_mla_w_a, mla_q_norm=v_mla_q_norm,
             mla_kv_norm=v_mla_kv_norm, mla_w_uq=v_mla_w_uq, mla_w_ukv=v_mla_w_ukv, mla_w_o=v_mla_w_o,
             dil_w_qkv=v_dil_w_qkv, dil_w_o=v_dil_w_o, fox_w_qkvf=v_fox_w_qkvf, fox_b_f=v_fox_b_f, fox_w_o=v_fox_w_o)
    chip = 2 * lax.axis_index("x") + lax.axis_index("y")

    small_shapes = [w[k].shape for k in SMALL_SHARDED]
    order = [(i, part) for i in range(DEPTH) for part in (MIXER_PART, COMMON_PART)]
    gathers = {}
    after = positions
    for i, part in order:
        srcs = [w[k][_layer_slot(k, i)].astype(BF) for k in _part_names(i, part)]
        if (i, part) == order[0]:
            srcs.append(_pack_rows([w[k] for k in SMALL_SHARDED]))
        gathers[i, part] = _spread_start(srcs, False, after, f"gather_start_{i}_{part}")
        after = gathers[i, part]["token"]
    all_started = after
    state = {}

    def get_part(i, part, after_array):
        is_first = (i, part) == order[0]
        lands = _spread_wait(gathers[i, part], all_started if is_first else after_array, f"gather_wait_{i}_{part}")
        if is_first:
            pieces = [_unpack_rows(lands[-1][k], small_shapes) for k in range(N_CHIPS)]
            small = {name: jnp.concatenate([pieces[k][idx] for k in range(N_CHIPS)], axis=-1)
                     for idx, name in enumerate(SMALL_SHARDED)}
            state["small"] = dict(small, rel_bias=rel_bias, fox_b_f=fox_b_f)
        chunks = dict(zip(_part_names(i, part), lands))
        state[i, part] = {k: a.shape for k, a in chunks.items()}
        return _part_to_compute(i, part, chunks)

    exchanges = {}

    def put_part(i, part, lg):
        contrib = _part_contributions(i, part, lg, state[i, part])
        exchanges[i, part] = _spread_start([contrib[k] for k in _part_names(i, part)], True, positions,
                                           f"exchange_start_{i}_{part}")
        return exchanges[i, part]["token"]

    sq, grad_x, sg = _run_layers(x[0], p[:, 0], positions[0], loss_target[0], get_part, lambda: state["small"],
                                 put_part)
    loss = lax.psum(0.5 / D_MODEL * jnp.sum(sq), ("x", "y", "c"))

    mine = {k: lax.empty(_as_2d(w[k]).shape, F32) for k in BIG}
    for i, part in [(i, part) for i in reversed(range(DEPTH)) for part in (COMMON_PART, MIXER_PART)]:
        received = _spread_wait(exchanges[i, part], grad_x, f"exchange_wait_{i}_{part}")
        for k, r in zip(_part_names(i, part), received):
            mine[k] = _sum_chips_into(r, mine[k], _layer_slot(k, i))
    theirs = _exchange_sibling([mine[k] for k in BIG], "exchange_sibling")
    results = {}
    for k, gb in zip(BIG, theirs):
        outs = _adamw(_as_2d(w[k]), _as_2d(m[k]), _as_2d(v[k]), mine[k], gb)
        results[k] = [o.reshape(w[k].shape) for o in outs]

    small_all = SMALL_SHARDED + SMALL_REPLICATED
    full_shapes = [sg[k].shape for k in small_all]
    reduced = dict(zip(small_all, _unpack_rows(_all_reduce_small(_pack_rows([sg[k] for k in small_all])), full_shapes)))
    local_g = []
    for k in small_all:
        g = reduced[k]
        if k in SMALL_SHARDED:
            width = w[k].shape[-1]
            g = lax.dynamic_slice_in_dim(g, chip * width, width, axis=g.ndim - 1)
        local_g.append(g)
    local_shapes = [w[k].shape for k in small_all]
    outs = _adamw(_pack_rows([w[k] for k in small_all]), _pack_rows([m[k] for k in small_all]),
                  _pack_rows([v[k] for k in small_all]), _pack_rows(local_g), None)
    unpacked = [_unpack_rows(o, local_shapes) for o in outs]
    for idx, k in enumerate(small_all):
        results[k] = [u[idx] for u in unpacked]

    return (loss, grad_x[None], *[results[k][0] for k in WEIGHTS], *[results[k][1] for k in WEIGHTS],
            *[results[k][2] for k in WEIGHTS], *[results[k][3] for k in WEIGHTS])
```

```python
import functools
import math

import jax
import jax.numpy as jnp
from jax import lax
from jax.experimental import pallas as pl
from jax.experimental.pallas import tpu as pltpu

F32 = jnp.float32
BF = jnp.bfloat16
MESH = pl.DeviceIdType.MESH
HBM_SPEC = pl.BlockSpec(memory_space=pltpu.HBM)

D_MODEL = 1024
DEPTH = 4
N_MIXERS = 3
D_FF = 2816
NORM_EPS = 1e-6
NEG_INF = -1e30
LANE = 128
HEADS = 16
HEAD_DIM = 64
MLA_Q_RANK = 384
MLA_KV_RANK = 256
MLA_ROPE = 32
MLA_A_PAD = 768
ROPE_THETA = 10000.0
DIL_PATTERNS = ((128, 1), (512, 4), (2048, 16))
Q_BLOCK = 128
REL_BUCKETS = 32
REL_MAX_DIST = 2048
N_CHIPS = 4
N_DEV = 8

ADAM_LR = 0.001
ADAM_B1 = 0.9
ADAM_B2 = 0.999
ADAM_EPS = 1e-08
ADAM_WD = 0.01
ADAM_STEP = 10

VMEM_LIMIT = 56 * 1024 * 1024
MATMUL_VMEM_BUDGET = 36 * 1024 * 1024
ROW_TILE = 256
ATTN_TILE = 256
MLA_GROUP = 4
FOX_GROUP = 4


def _params(sem=None):
    return pltpu.CompilerParams(dimension_semantics=sem, vmem_limit_bytes=VMEM_LIMIT)


def _divisor_tiles(dim):
    tiles = [t for t in range(LANE, dim + 1, LANE) if dim % t == 0]
    return tiles or [dim]


def _matmul_tiles(m, n, k, a_bytes, b_bytes, out_bytes, has_add, n_unit=None, k_unit=None):
    best = None
    for tm in _divisor_tiles(m):
        for tn in _divisor_tiles(n_unit or n):
            for tk in _divisor_tiles(k_unit or k):
                if max(tm, tn, tk) > 2048:
                    continue
                vmem = 2 * (tm * tk * a_bytes + tk * tn * b_bytes + tm * tn * out_bytes) + tm * tn * 4
                if has_add:
                    vmem += 2 * tm * tn * 4
                if vmem > MATMUL_VMEM_BUDGET:
                    continue
                steps = (m // tm) * (n // tn) * (k // tk)
                traffic = m * k * a_bytes * (n // tn) + k * n * b_bytes * (m // tm) + m * n * out_bytes
                cost = traffic / 3.0e12 + steps * 0.4e-6
                if best is None or cost < best[0]:
                    best = (cost, tm, tn, tk)
    return best[1:]


def _matmul(a, b, *, ta=False, tb=False, b_chunks=False, out_chunks=False, add=None, out_dtype=F32, name):
    k, m = a.shape if ta else a.shape[::-1]
    n_unit = k_unit = None
    if b_chunks:
        chunks, rows_w, c = b.shape
        if tb:
            kb, n, k_unit = chunks * c, rows_w, c
        else:
            kb, n, n_unit = rows_w, chunks * c, c
    else:
        kb, n = b.shape[::-1] if tb else b.shape
    if out_chunks:
        assert n % N_CHIPS == 0 and add is None
        n_unit = n // N_CHIPS
    assert k == kb, (a.shape, b.shape, ta, tb)
    tm, tn, tk = _matmul_tiles(m, n, k, a.dtype.itemsize, b.dtype.itemsize, jnp.dtype(out_dtype).itemsize,
                               add is not None, n_unit, k_unit)
    nk = k // tk
    dims = (((0 if ta else 1,), (1 if tb else 0,)), ((), ()))

    def body(*refs):
        if add is None:
            a_ref, b_ref, o_ref, acc_ref = refs
            add_ref = None
        else:
            a_ref, b_ref, add_ref, o_ref, acc_ref = refs
        kk = pl.program_id(2)

        @pl.when(kk == 0)
        def _():
            acc_ref[...] = jnp.zeros_like(acc_ref)

        acc_ref[...] += lax.dot_general(a_ref[...].astype(BF), b_ref[...].astype(BF), dims,
                                        preferred_element_type=F32)

        @pl.when(kk == nk - 1)
        def _():
            r = acc_ref[...]
            if add_ref is not None:
                r = r + add_ref[...].astype(F32)
            o_ref[...] = r.astype(out_dtype)

    a_spec = pl.BlockSpec((tk, tm), lambda i, j, q: (q, i)) if ta else pl.BlockSpec((tm, tk), lambda i, j, q: (i, q))
    if b_chunks and tb:
        per_k = k_unit // tk
        b_spec = pl.BlockSpec((None, tn, tk), lambda i, j, q: (q // per_k, j, q % per_k))
    elif b_chunks:
        per_n = n_unit // tn
        b_spec = pl.BlockSpec((None, tk, tn), lambda i, j, q: (j // per_n, q, j % per_n))
    elif tb:
        b_spec = pl.BlockSpec((tn, tk), lambda i, j, q: (j, q))
    else:
        b_spec = pl.BlockSpec((tk, tn), lambda i, j, q: (q, j))
    if out_chunks:
        per_o = n_unit // tn
        o_spec = pl.BlockSpec((None, tm, tn), lambda i, j, q: (j // per_o, i, j % per_o))
        out_shape = jax.ShapeDtypeStruct((N_CHIPS, m, n_unit), out_dtype)
    else:
        o_spec = pl.BlockSpec((tm, tn), lambda i, j, q: (i, j))
        out_shape = jax.ShapeDtypeStruct((m, n), out_dtype)
    in_specs = [a_spec, b_spec]
    args = [a, b]
    if add is not None:
        in_specs.append(o_spec)
        args.append(add)
    return pl.pallas_call(
        body, out_shape=out_shape, grid=(m // tm, n // tn, nk),
        in_specs=in_specs, out_specs=o_spec, scratch_shapes=[pltpu.VMEM((tm, tn), F32)], name=name,
        compiler_params=_params(("parallel", "parallel", "arbitrary")))(*args)


def _rowwise(body, name, rows, ins, outs, tr=ROW_TILE):
    def row_spec(cols):
        return pl.BlockSpec((tr, cols), lambda i: (i, 0))

    def full_spec(shape):
        zeros = (0,) * len(shape)
        return pl.BlockSpec(shape, lambda i: zeros)

    in_specs = [row_spec(a.shape[1]) if kind == "row" else full_spec(a.shape) for a, kind in ins]
    out_specs = [row_spec(shape[1]) if kind == "row" else full_spec(shape) for shape, _, kind in outs]
    out_shape = [jax.ShapeDtypeStruct(shape, dtype) for shape, dtype, _ in outs]
    return pl.pallas_call(body, out_shape=out_shape, grid=(rows // tr,), in_specs=in_specs, out_specs=out_specs,
                          name=name, compiler_params=_params(("arbitrary",)))(*[a for a, _ in ins])


def _rstd(x):
    return lax.rsqrt(jnp.mean(x * x, axis=-1, keepdims=True) + NORM_EPS)


def _rms_bwd_math(x, g, dy):
    r = _rstd(x)
    gd = dy * g
    dx = r * gd - x * (r * r * r) * jnp.mean(gd * x, axis=-1, keepdims=True)
    dg = jnp.sum(dy * x * r, axis=0, keepdims=True)
    return dx, dg


def _sigmoid(x):
    return 0.5 * jnp.tanh(0.5 * x) + 0.5


def _init_acc(*refs):
    @pl.when(pl.program_id(0) == 0)
    def _():
        for r in refs:
            r[...] = jnp.zeros_like(r)


def _prenorm(h, g):
    rows, cols = h.shape

    def body(h_ref, g_ref, o_ref):
        x = h_ref[...]
        o_ref[...] = (x * _rstd(x) * g_ref[...]).astype(BF)

    return _rowwise(body, "prenorm", rows, [(h, "row"), (g, "full")], [((rows, cols), BF, "row")])[0]


def _post_residual(h, y, g_post, g_pre):
    rows, cols = h.shape
    with_pre = g_pre is not None

    def body(*refs):
        if with_pre:
            h_ref, y_ref, gp_ref, gq_ref, hn_ref, hb_ref = refs
        else:
            h_ref, y_ref, gp_ref, hn_ref, hb_ref = refs
        yv = y_ref[...]
        hn = h_ref[...] + yv * _rstd(yv) * gp_ref[...]
        hn_ref[...] = hn
        hb_ref[...] = (hn * _rstd(hn) * gq_ref[...] if with_pre else hn).astype(BF)

    ins = [(h, "row"), (y, "row"), (g_post, "full")] + ([(g_pre, "full")] if with_pre else [])
    return _rowwise(body, "post_residual_pre" if with_pre else "post_residual", rows, ins,
                    [((rows, cols), F32, "row"), ((rows, cols), BF, "row")])


def _ple_forward(h2, pp, z, g_pre):
    rows, cols = h2.shape

    def body(h_ref, p_ref, z_ref, g_ref, h3_ref, hb_ref):
        h3 = h_ref[...] + p_ref[...] * _sigmoid(z_ref[...])
        h3_ref[...] = h3
        hb_ref[...] = (h3 * _rstd(h3) * g_ref[...]).astype(BF)

    return _rowwise(body, "ple_forward", rows, [(h2, "row"), (pp, "row"), (z, "row"), (g_pre, "full")],
                    [((rows, cols), F32, "row"), ((rows, cols), BF, "row")])


def _ple_loss(h2, pp, z, target):
    rows, cols = h2.shape

    def body(h_ref, p_ref, z_ref, t_ref, dh_ref, sq_ref):
        _init_acc(sq_ref)
        err = h_ref[...] + p_ref[...] * _sigmoid(z_ref[...]) - t_ref[...]
        dh_ref[...] = err * (1.0 / cols)
        sq_ref[...] += jnp.sum(err * err, axis=0, keepdims=True)

    return _rowwise(body, "ple_loss", rows, [(h2, "row"), (pp, "row"), (z, "row"), (target, "row")],
                    [((rows, cols), F32, "row"), ((1, cols), F32, "acc")])


def _ple_backward(dh3, pp, z):
    rows, cols = dh3.shape

    def body(d_ref, p_ref, z_ref, dpp_ref, dz_ref):
        d = d_ref[...]
        s = _sigmoid(z_ref[...])
        dpp_ref[...] = (d * s).astype(BF)
        dz_ref[...] = (d * p_ref[...] * s * (1.0 - s)).astype(BF)

    return _rowwise(body, "ple_backward", rows, [(dh3, "row"), (pp, "row"), (z, "row")],
                    [((rows, cols), BF, "row"), ((rows, cols), BF, "row")])


def _rms_backward(x, g, dy, add, out_dtype):
    rows, cols = x.shape
    with_add = add is not None

    def body(*refs):
        if with_add:
            x_ref, g_ref, dy_ref, add_ref, dx_ref, dg_ref = refs
        else:
            x_ref, g_ref, dy_ref, dx_ref, dg_ref = refs
        _init_acc(dg_ref)
        dx, dg = _rms_bwd_math(x_ref[...], g_ref[...], dy_ref[...].astype(F32))
        if with_add:
            dx = dx + add_ref[...]
        dx_ref[...] = dx.astype(out_dtype)
        dg_ref[...] += dg

    ins = [(x, "row"), (g, "full"), (dy, "row")] + ([(add, "row")] if with_add else [])
    return _rowwise(body, "rms_backward_add" if with_add else "rms_backward", rows, ins,
                    [((rows, cols), out_dtype, "row"), ((1, cols), F32, "acc")])


def _swiglu_forward(gu):
    rows = gu.shape[0]

    def body(gu_ref, o_ref):
        g = gu_ref[:, :D_FF].astype(F32)
        o_ref[...] = (g * _sigmoid(g) * gu_ref[:, D_FF:].astype(F32)).astype(BF)

    return _rowwise(body, "swiglu_forward", rows, [(gu, "row")], [((rows, D_FF), BF, "row")])[0]


def _swiglu_backward(gu, dact):
    rows = gu.shape[0]

    def body(gu_ref, d_ref, o_ref):
        g = gu_ref[:, :D_FF].astype(F32)
        u = gu_ref[:, D_FF:].astype(F32)
        d = d_ref[...].astype(F32)
        s = _sigmoid(g)
        gs = g * s
        o_ref[:, :D_FF] = (d * u * (s + gs * (1.0 - s))).astype(BF)
        o_ref[:, D_FF:] = (d * gs).astype(BF)

    return _rowwise(body, "swiglu_backward", rows, [(gu, "row"), (dact, "row")], [((rows, 2 * D_FF), BF, "row")])[0]


def _rope_tables(positions):
    half = MLA_ROPE // 2
    inv = ROPE_THETA ** (-jnp.arange(half, dtype=F32) / half)
    ang = positions.astype(F32)[:, None] * inv
    cos, sin = jnp.cos(ang), jnp.sin(ang)
    rows = positions.shape[0]
    c = jnp.ones((rows, LANE), F32).at[:, 64:80].set(cos).at[:, 80:96].set(cos)
    sa = jnp.zeros((rows, LANE), F32).at[:, 64:80].set(-sin)
    sb = jnp.zeros((rows, LANE), F32).at[:, 80:96].set(sin)
    return c, sa, sb


def _rope_apply(x, c, sa, sb):
    return x * c + pltpu.roll(x, LANE - 16, 1) * sa + pltpu.roll(x, 16, 1) * sb


def _rope_apply_t(dy, c, sa, sb):
    return dy * c + pltpu.roll(dy * sa, 16, 1) + pltpu.roll(dy * sb, LANE - 16, 1)


def _rope_heads(x, tables, transpose, name):
    rows, cols = x.shape

    def body(x_ref, c_ref, sa_ref, sb_ref, o_ref):
        fn = _rope_apply_t if transpose else _rope_apply
        c, sa, sb = c_ref[...], sa_ref[...], sb_ref[...]
        for head in range(cols // LANE):
            lanes = slice(head * LANE, (head + 1) * LANE)
            o_ref[:, lanes] = fn(x_ref[:, lanes].astype(F32), c, sa, sb).astype(BF)

    blk = pl.BlockSpec((ROW_TILE, cols), lambda i: (i, 0))
    tbl = pl.BlockSpec((ROW_TILE, LANE), lambda i: (i, 0))
    return pl.pallas_call(body, out_shape=jax.ShapeDtypeStruct((rows, cols), BF), grid=(rows // ROW_TILE,),
                          in_specs=[blk, tbl, tbl, tbl], out_specs=blk, name=name,
                          compiler_params=_params(("parallel",)))(x, *tables)


def _mla_mid_forward(a, q_norm, kv_norm, tables):
    rows = a.shape[0]
    qr, kvr = MLA_Q_RANK, MLA_KV_RANK

    def body(a_ref, qn_ref, kn_ref, c_ref, sa_ref, sb_ref, cq_ref, ckv_ref, kr_ref):
        aq = a_ref[:, 0:qr]
        akv = a_ref[:, qr:qr + kvr]
        cq_ref[...] = (aq * _rstd(aq) * qn_ref[...]).astype(BF)
        ckv_ref[...] = (akv * _rstd(akv) * kn_ref[...]).astype(BF)
        kr_ref[...] = _rope_apply(a_ref[:, qr + kvr:], c_ref[...], sa_ref[...], sb_ref[...]).astype(BF)

    ins = [(a, "row"), (q_norm, "full"), (kv_norm, "full")] + [(t, "row") for t in tables]
    return _rowwise(body, "mla_mid_forward", rows, ins,
                    [((rows, qr), BF, "row"), ((rows, kvr), BF, "row"), ((rows, LANE), BF, "row")])


def _mla_mid_backward(a, q_norm, kv_norm, tables, dcq, dckv, dkr):
    rows = a.shape[0]
    qr, kvr = MLA_Q_RANK, MLA_KV_RANK

    def body(a_ref, qn_ref, kn_ref, c_ref, sa_ref, sb_ref, dcq_ref, dckv_ref, dkr_ref, da_ref, dqn_ref, dkn_ref):
        _init_acc(dqn_ref, dkn_ref)
        dxq, dgq = _rms_bwd_math(a_ref[:, 0:qr], qn_ref[...], dcq_ref[...])
        dxk, dgk = _rms_bwd_math(a_ref[:, qr:qr + kvr], kn_ref[...], dckv_ref[...])
        da_ref[:, 0:qr] = dxq.astype(BF)
        da_ref[:, qr:qr + kvr] = dxk.astype(BF)
        da_ref[:, qr + kvr:] = _rope_apply_t(dkr_ref[...], c_ref[...], sa_ref[...], sb_ref[...]).astype(BF)
        dqn_ref[...] += dgq
        dkn_ref[...] += dgk

    ins = ([(a, "row"), (q_norm, "full"), (kv_norm, "full")] + [(t, "row") for t in tables]
           + [(dcq, "row"), (dckv, "row"), (dkr, "row")])
    return _rowwise(body, "mla_mid_backward", rows, ins,
                    [((rows, MLA_A_PAD), BF, "row"), ((1, qr), F32, "acc"), ((1, kvr), F32, "acc")])


def _attn_specs(rows, kv_off, g):
    head =pl.BlockSpec((rows, g * LANE), lambda h: (0, h))
    kv_head = pl.BlockSpec((rows, g * LANE), lambda h: (0, h + kv_off // g))
    shared = pl.BlockSpec((rows, LANE), lambda h: (0, 0))
    col_vec = pl.BlockSpec((g, rows, 1), lambda h: (h, 0, 0), pipeline_mode=pl.Buffered(1))
    row_vec = pl.BlockSpec((g, 1, rows), lambda h: (h, 0, 0))
    return head, kv_head, shared, col_vec, row_vec


def _attn_forward(q, kv, kv_off, kr, cum_col, cum_row, scale, group_size, name):
    rows = q.shape[0]
    heads = HEADS
    t = ATTN_TILE
    nb = rows // t
    has_kr = kr is not None
    has_f = cum_col is not None
    group = range(group_size)

    def body(*refs):
        it = iter(refs)
        q_ref, kv_ref = next(it), next(it)
        kr_ref = next(it) if has_kr else None
        cc_ref = next(it) if has_f else None
        cr_ref = next(it) if has_f else None
        o_ref, lse_ref = next(it), next(it)
        lo = lax.broadcasted_iota(jnp.int32, (1, LANE), 1) < HEAD_DIM
        causal = (lax.broadcasted_iota(jnp.int32, (t, t), 1) <= lax.broadcasted_iota(jnp.int32, (t, t), 0))
        lanes = [slice(g * LANE, (g + 1) * LANE) for g in group]

        def q_block(i, _):
            qs = pl.ds(pl.multiple_of(i * t, t), t)
            qbs = [q_ref[qs, lanes[g]] for g in group]
            cqs = [cc_ref[g, qs, :] if has_f else None for g in group]

            def step(j, carry, diag):
                ks = pl.ds(pl.multiple_of(j * t, t), t)
                other = kr_ref[ks, :] if has_kr else jnp.zeros((t, LANE), BF)
                out = []
                for g in group:
                    m, l, acc = carry[g]
                    kvb = kv_ref[ks, lanes[g]]
                    kk = jnp.where(lo, kvb, other)
                    s = lax.dot_general(qbs[g], kk, (((1,), (1,)), ((), ())), preferred_element_type=F32) * scale
                    if has_f:
                        s = s + (cqs[g] - cr_ref[g, :, ks])
                    if diag:
                        s = jnp.where(causal, s, NEG_INF)
                    mn = jnp.maximum(m, jnp.max(s, axis=1, keepdims=True))
                    alpha = jnp.exp(m - mn)
                    p = jnp.exp(s - mn)
                    l = alpha * l + jnp.sum(p, axis=1, keepdims=True)
                    acc = alpha * acc + jnp.dot(p.astype(BF), kvb, preferred_element_type=F32)
                    out.append((mn, l, acc))
                return tuple(out)

            init = tuple((jnp.full((t, 1), NEG_INF, F32), jnp.zeros((t, 1), F32), jnp.zeros((t, LANE), F32))
                         for _ in group)
            carry = lax.fori_loop(0, i, lambda j, c: step(j, c, False), init)
            for g, (m, l, acc) in enumerate(step(i, carry, True)):
                o_ref[qs, lanes[g]] = jnp.where(lo, 0.0, acc / l).astype(BF)
                lse_ref[g, qs, :] = m + jnp.log(l)
            return 0

        lax.fori_loop(0, nb, q_block, 0)

    head, kv_head, shared, col_vec, row_vec = _attn_specs(rows, kv_off, group_size)
    in_specs, args = [head, kv_head], [q, kv]
    if has_kr:
        in_specs.append(shared)
        args.append(kr)
    if has_f:
        in_specs += [col_vec, row_vec]
        args += [cum_col, cum_row]
    return pl.pallas_call(
        body, out_shape=[jax.ShapeDtypeStruct((rows, heads * LANE), BF), jax.ShapeDtypeStruct((heads, rows, 1), F32)],
        grid=(heads // group_size,), in_specs=in_specs, out_specs=[head, col_vec], name=name,
        compiler_params=_params(("arbitrary",)))(*args)


def _attn_backward(q, kv, kv_off, kr, cum_col, cum_row, o, do, lse, scale, group_size, name):
    rows = q.shape[0]
    heads = HEADS
    t = ATTN_TILE
    nb = rows // t
    has_kr = kr is not None
    has_f = cum_col is not None
    group = range(group_size)

    def body(*refs):
        it = iter(refs)
        q_ref, kv_ref = next(it), next(it)
        kr_ref = next(it) if has_kr else None
        cc_ref = next(it) if has_f else None
        cr_ref = next(it) if has_f else None
        o_ref, do_ref, lse_ref = next(it), next(it), next(it)
        dq_ref, dkv_ref = next(it), next(it)
        dkr_ref = next(it) if has_kr else None
        dck_ref = next(it) if has_f else None
        dcq_ref = next(it) if has_f else None
        dq_acc = next(it)
        lo = lax.broadcasted_iota(jnp.int32, (1, LANE), 1) < HEAD_DIM
        causal = (lax.broadcasted_iota(jnp.int32, (t, t), 1) <= lax.broadcasted_iota(jnp.int32, (t, t), 0))
        lanes = [slice(g * LANE, (g + 1) * LANE) for g in group]

        dq_acc[...] = jnp.zeros_like(dq_acc)
        if has_kr:
            _init_acc(dkr_ref)
        if has_f:
            dcq_ref[...] = jnp.zeros_like(dcq_ref)

        def kv_block(j, _):
            ks = pl.ds(pl.multiple_of(j * t, t), t)
            other = kr_ref[ks, :] if has_kr else jnp.zeros((t, LANE), BF)
            kvbs = [kv_ref[ks, lanes[g]] for g in group]
            kks = [jnp.where(lo, kvbs[g], other) for g in group]
            cks = [cr_ref[g, :, ks] if has_f else None for g in group]

            def pair(i, carry, diag):
                qs = pl.ds(pl.multiple_of(i * t, t), t)
                out = []
                for g in group:
                    dkk, dvv, dcs = carry[g]
                    qb = q_ref[qs, lanes[g]]
                    dob = do_ref[qs, lanes[g]]
                    s = lax.dot_general(qb, kks[g], (((1,), (1,)), ((), ())), preferred_element_type=F32) * scale
                    if has_f:
                        s = s + (cc_ref[g, qs, :] - cks[g])
                    if diag:
                        s = jnp.where(causal, s, NEG_INF)
                    p = jnp.exp(s - lse_ref[g, qs, :])
                    dp = lax.dot_general(dob, kvbs[g], (((1,), (1,)), ((), ())), preferred_element_type=F32)
                    delta = jnp.sum(dob.astype(F32) * o_ref[qs, lanes[g]].astype(F32), axis=1, keepdims=True)
                    ds = p * (dp - delta)
                    dsb = ds.astype(BF)
                    dvv = dvv + lax.dot_general(p.astype(BF), dob, (((0,), (0,)), ((), ())), preferred_element_type=F32)
                    dkk = dkk + lax.dot_general(dsb, qb, (((0,), (0,)), ((), ())), preferred_element_type=F32)
                    dq_acc[qs, lanes[g]] += jnp.dot(dsb, kks[g], preferred_element_type=F32)
                    if has_f:
                        dcs = dcs + jnp.sum(ds, axis=0, keepdims=True)
                        dcq_ref[g, qs, :] += jnp.sum(ds, axis=1, keepdims=True)
                    out.append((dkk, dvv, dcs))
                return tuple(out)

            init = tuple((jnp.zeros((t, LANE), F32), jnp.zeros((t, LANE), F32), jnp.zeros((1, t), F32)) for _ in group)
            carry = pair(j, init, True)
            carry = lax.fori_loop(j + 1, nb, lambda i, c: pair(i, c, False), carry)
            for g, (dkk, dvv, dcs) in enumerate(carry):
                dkk = dkk * scale
                dkv_ref[ks, lanes[g]] = jnp.where(lo, dkk, dvv).astype(BF)
                if has_kr:
                    dkr_ref[ks, :] += jnp.where(lo, 0.0, dkk)
                if has_f:
                    dck_ref[g, :, ks] = -dcs
            return 0

        lax.fori_loop(0, nb, kv_block, 0)
        dq_ref[...] = (dq_acc[...] * scale).astype(BF)

    head, kv_head, shared, col_vec, row_vec = _attn_specs(rows, kv_off, group_size)
    in_specs, args = [head, kv_head], [q, kv]
    if has_kr:
        in_specs.append(shared)
        args.append(kr)
    if has_f:
        in_specs += [col_vec, row_vec]
        args += [cum_col, cum_row]
    in_specs += [head, head, col_vec]
    args += [o, do, lse]
    out_shape = [jax.ShapeDtypeStruct((rows, heads * LANE), BF), jax.ShapeDtypeStruct((rows, heads * LANE), BF)]
    out_specs = [head, head]
    if has_kr:
        out_shape.append(jax.ShapeDtypeStruct((rows, LANE), F32))
        out_specs.append(shared)
    if has_f:
        out_shape += [jax.ShapeDtypeStruct((heads, 1, rows), F32), jax.ShapeDtypeStruct((heads, rows, 1), F32)]
        out_specs += [row_vec, col_vec]
    return pl.pallas_call(
        body, out_shape=out_shape, grid=(heads // group_size,), in_specs=in_specs, out_specs=out_specs,
        scratch_shapes=[pltpu.VMEM((rows, group_size * LANE), F32)], name=name,
        compiler_params=_params(("arbitrary",)))(*args)


def _tri_dot(tri, x):
    return jnp.dot(tri, x, preferred_element_type=F32, precision=lax.Precision.HIGHEST)


def _forget_forward(f_raw, b_f):
    rows = f_raw.shape[0]
    t = ATTN_TILE

    def body(f_ref, b_ref, cum_ref):
        tri = (lax.broadcasted_iota(jnp.int32, (t, t), 1) <= lax.broadcasted_iota(jnp.int32, (t, t), 0)).astype(F32)

        def blk(i, carry):
            sl = pl.ds(pl.multiple_of(i * t, t), t)
            xv = f_ref[sl, :] + b_ref[...]
            log_f = jnp.minimum(xv, 0.0) - jnp.log(1.0 + jnp.exp(-jnp.abs(xv)))
            cum_ref[sl, :] = _tri_dot(tri, log_f) + carry
            return carry + jnp.sum(log_f, axis=0, keepdims=True)

        lax.fori_loop(0, rows // t, blk, jnp.zeros((1, LANE), F32))

    return pl.pallas_call(body, out_shape=jax.ShapeDtypeStruct((rows, LANE), F32), name="forget_forward",
                          compiler_params=_params())(f_raw, b_f)


def _forget_backward(f_raw, b_f, dcum):
    rows = f_raw.shape[0]
    t = ATTN_TILE
    nb = rows // t

    def body(f_ref, b_ref, dc_ref, df_ref, db_ref):
        tri = (lax.broadcasted_iota(jnp.int32, (t, t), 1) >= lax.broadcasted_iota(jnp.int32, (t, t), 0)).astype(F32)

        def blk(i, carry):
            later, db = carry
            sl = pl.ds(pl.multiple_of((nb - 1 - i) * t, t), t)
            dc = dc_ref[sl, :]
            dlog = _tri_dot(tri, dc) + later
            xv = f_ref[sl, :] + b_ref[...]
            df = dlog / (1.0 + jnp.exp(xv))
            df_ref[sl, :] = df.astype(BF)
            return later + jnp.sum(dc, axis=0, keepdims=True), db + jnp.sum(df, axis=0, keepdims=True)

        _, db = lax.fori_loop(0, nb, blk, (jnp.zeros((1, LANE), F32), jnp.zeros((1, LANE), F32)))
        db_ref[...] = db

    return pl.pallas_call(body, out_shape=[jax.ShapeDtypeStruct((rows, LANE), BF), jax.ShapeDtypeStruct((1, LANE), F32)],
                          name="forget_backward", compiler_params=_params())(f_raw, b_f, dcum)


def _t5_bucket(dist):
    max_exact = REL_BUCKETS // 2
    n = jnp.maximum(dist.astype(F32), 1.0)
    large = max_exact + (jnp.log(n / max_exact) / math.log(REL_MAX_DIST / max_exact)
                         * (REL_BUCKETS - max_exact)).astype(jnp.int32)
    large = jnp.minimum(large, REL_BUCKETS - 1)
    return jnp.where(dist < max_exact, dist, large)


def _dil_buckets(dilation):
    i = jnp.arange(Q_BLOCK)[:, None]
    j = jnp.arange(Q_BLOCK)[None, :]
    cur = _t5_bucket(jnp.clip(i - j, 0) * dilation).astype(jnp.int32)
    prev = _t5_bucket(jnp.clip(Q_BLOCK + i - j, 0) * dilation).astype(jnp.int32)
    return cur, prev


def _dil_bias_tiles(tbl_ref, bc_ref, bp_ref, bias_ref, group, hp):
    for hh in range(2):
        col = group * HEADS + 2 * hp + hh
        acc_c = jnp.zeros((Q_BLOCK, Q_BLOCK), F32)
        acc_p = jnp.zeros((Q_BLOCK, Q_BLOCK), F32)
        for b in range(REL_BUCKETS):
            val = tbl_ref[b, col]
            acc_c = jnp.where(bc_ref[...] == b, val, acc_c)
            acc_p = jnp.where(bp_ref[...] == b, val, acc_p)
        bias_ref[2 * hh] = acc_c
        bias_ref[2 * hh + 1] = acc_p


def _dil_view(qkv, group, dilation):
    if dilation == 1:
        return qkv
    width = 3 * HEADS * HEAD_DIM
    return qkv[:, group * width:(group + 1) * width].reshape(qkv.shape[0] // dilation, dilation * width)


def _dil_specs(group, dilation, length):
    def col(kind):
        if dilation == 1:
            return pl.BlockSpec((length, LANE), lambda hp, r: (0, (group * 3 + kind) * 8 + hp))
        return pl.BlockSpec((length, LANE), lambda hp, r: (0, r * 24 + kind * 8 + hp))

    out = pl.BlockSpec((length, LANE), lambda hp, r: (0, r * 8 + hp))
    tile = pl.BlockSpec((Q_BLOCK, Q_BLOCK), lambda hp, r: (0, 0))
    table = pl.BlockSpec(memory_space=pltpu.SMEM)
    return col, out, tile, table


def _dil_forward(view, group, dilation, table, buckets):
    length = view.shape[0]
    rows = length * dilation
    nb = length // Q_BLOCK
    scale = HEAD_DIM ** -0.5
    qb = Q_BLOCK

    def body(tbl_ref, bc_ref, bp_ref, q_ref, k_ref, v_ref, o_ref, lse_ref, bias_ref):
        hp = pl.program_id(0)

        @pl.when(pl.program_id(1) == 0)
        def _():
            _dil_bias_tiles(tbl_ref, bc_ref, bp_ref, bias_ref, group, hp)

        lo = lax.broadcasted_iota(jnp.int32, (1, LANE), 1) < HEAD_DIM
        ii = lax.broadcasted_iota(jnp.int32, (qb, qb), 0)
        jj = lax.broadcasted_iota(jnp.int32, (qb, qb), 1)

        def blk(n, _):
            cur = pl.ds(pl.multiple_of(n * qb, qb), qb)
            prev = pl.ds(pl.multiple_of(jnp.maximum(n - 1, 0) * qb, qb), qb)
            qn = q_ref[cur, :]
            kc, kp, vc, vp = k_ref[cur, :], k_ref[prev, :], v_ref[cur, :], v_ref[prev, :]
            ok_c = jj <= ii
            ok_p = (jj >= ii) & (n > 0)
            outs, lses = [], []
            for hh in range(2):
                qm = jnp.where(lo if hh == 0 else ~lo, qn, jnp.zeros_like(qn))
                s_c = lax.dot_general(qm, kc, (((1,), (1,)), ((), ())), preferred_element_type=F32) * scale
                s_p = lax.dot_general(qm, kp, (((1,), (1,)), ((), ())), preferred_element_type=F32) * scale
                s_c = jnp.where(ok_c, s_c + bias_ref[2 * hh], NEG_INF)
                s_p = jnp.where(ok_p, s_p + bias_ref[2 * hh + 1], NEG_INF)
                m = jnp.maximum(jnp.max(s_c, axis=1, keepdims=True), jnp.max(s_p, axis=1, keepdims=True))
                e_c = jnp.exp(s_c - m)
                e_p = jnp.exp(s_p - m)
                l = jnp.sum(e_c, axis=1, keepdims=True) + jnp.sum(e_p, axis=1, keepdims=True)
                acc = (jnp.dot(e_c.astype(BF), vc, preferred_element_type=F32)
                       + jnp.dot(e_p.astype(BF), vp, preferred_element_type=F32))
                outs.append(acc / l)
                lses.append(m + jnp.log(l))
            o_ref[cur, :] = jnp.where(lo, outs[0], outs[1])
            lse_ref[cur, :] = jnp.where(lo, lses[0], lses[1])
            return 0

        lax.fori_loop(0, nb, blk, 0)

    col, out, tile, tbl = _dil_specs(group, dilation, length)
    bc, bp = buckets
    o, lse = pl.pallas_call(
        body, out_shape=[jax.ShapeDtypeStruct((length, dilation * D_MODEL), F32)] * 2, grid=(8, dilation),
        in_specs=[tbl, tile, tile, col(0), col(1), col(2)], out_specs=[out, out],
        scratch_shapes=[pltpu.VMEM((4, qb, qb), F32)], name=f"dilated_forward_{dilation}",
        compiler_params=_params(("arbitrary", "arbitrary")))(
            table, bc, bp, view, view, view)
    return o.reshape(rows, D_MODEL), lse.reshape(rows, D_MODEL)


def _dil_backward(view, group, dilation, table, buckets, do_g, lse, dlt):
    length = view.shape[0]
    rows = length * dilation
    nb = length // Q_BLOCK
    scale = HEAD_DIM ** -0.5
    qb = Q_BLOCK

    def body(tbl_ref, bc_ref, bp_ref, q_ref, k_ref, v_ref, do_ref, lse_ref, dlt_ref,
             dq_ref, dk_ref, dv_ref, db_ref, bias_ref, dk_acc, dv_acc):
        hp = pl.program_id(0)

        @pl.when(pl.program_id(1) == 0)
        def _():
            _dil_bias_tiles(tbl_ref, bc_ref, bp_ref, bias_ref, group, hp)
            db_ref[...] = jnp.zeros_like(db_ref)

        dk_acc[...] = jnp.zeros_like(dk_acc)
        dv_acc[...] = jnp.zeros_like(dv_acc)
        lo = lax.broadcasted_iota(jnp.int32, (1, LANE), 1) < HEAD_DIM
        ii = lax.broadcasted_iota(jnp.int32, (qb, qb), 0)
        jj = lax.broadcasted_iota(jnp.int32, (qb, qb), 1)
        tn = (((0,), (0,)), ((), ()))
        nt = (((1,), (1,)), ((), ()))

        def blk(n, _):
            cur = pl.ds(pl.multiple_of(n * qb, qb), qb)
            prev = pl.ds(pl.multiple_of(jnp.maximum(n - 1, 0) * qb, qb), qb)
            qn = q_ref[cur, :]
            don = do_ref[cur, :]
            kc, kp, vc, vp = k_ref[cur, :], k_ref[prev, :], v_ref[cur, :], v_ref[prev, :]
            lse_n = lse_ref[cur, :]
            dlt_n = dlt_ref[cur, :]
            ok_c = jj <= ii
            ok_p = (jj >= ii) & (n > 0)
            dqs = []
            dkc = jnp.zeros((qb, LANE), F32)
            dkp = jnp.zeros((qb, LANE), F32)
            dvc = jnp.zeros((qb, LANE), F32)
            dvp = jnp.zeros((qb, LANE), F32)
            for hh in range(2):
                mask = lo if hh == 0 else ~lo
                qm = jnp.where(mask, qn, jnp.zeros_like(qn))
                dom = jnp.where(mask, don, jnp.zeros_like(don))
                lse_h = jnp.max(jnp.where(mask, lse_n, -3e38), axis=1, keepdims=True)
                dlt_h = jnp.max(jnp.where(mask, dlt_n, -3e38), axis=1, keepdims=True)
                s_c = lax.dot_general(qm, kc, nt, preferred_element_type=F32) * scale
                s_p = lax.dot_general(qm, kp, nt, preferred_element_type=F32) * scale
                p_c = jnp.exp(jnp.where(ok_c, s_c + bias_ref[2 * hh], NEG_INF) - lse_h)
                p_p = jnp.exp(jnp.where(ok_p, s_p + bias_ref[2 * hh + 1], NEG_INF) - lse_h)
                ds_c = p_c * (lax.dot_general(dom, vc, nt, preferred_element_type=F32) - dlt_h)
                ds_p = p_p * (lax.dot_general(dom, vp, nt, preferred_element_type=F32) - dlt_h)
                db_ref[0, 2 * hh] += ds_c
                db_ref[0, 2 * hh + 1] += ds_p
                dsc_b, dsp_b = ds_c.astype(BF), ds_p.astype(BF)
                dqs.append(jnp.dot(dsc_b, kc, preferred_element_type=F32)
                           + jnp.dot(dsp_b, kp, preferred_element_type=F32))
                dkc = dkc + lax.dot_general(dsc_b, qm, tn, preferred_element_type=F32)
                dkp = dkp + lax.dot_general(dsp_b, qm, tn, preferred_element_type=F32)
                dvc = dvc + lax.dot_general(p_c.astype(BF), dom, tn, preferred_element_type=F32)
                dvp = dvp + lax.dot_general(p_p.astype(BF), dom, tn, preferred_element_type=F32)
            dq_ref[cur, :] = (jnp.where(lo, dqs[0], dqs[1]) * scale).astype(BF)
            dk_acc[cur, :] += dkc
            dk_acc[prev, :] += dkp
            dv_acc[cur, :] += dvc
            dv_acc[prev, :] += dvp
            return 0

        lax.fori_loop(0, nb, blk, 0)
        dk_ref[...] = (dk_acc[...] * scale).astype(BF)
        dv_ref[...] = dv_acc[...].astype(BF)

    col, out, tile, tbl = _dil_specs(group, dilation, length)
    bc, bp = buckets
    wide = (length, dilation * D_MODEL)
    dq, dk, dv, db = pl.pallas_call(
        body, out_shape=[jax.ShapeDtypeStruct(wide, BF)] * 3 + [jax.ShapeDtypeStruct((8, 4, qb, qb), F32)],
        grid=(8, dilation), in_specs=[tbl, tile, tile, col(0), col(1), col(2), out, out, out],
        out_specs=[out, out, out, pl.BlockSpec((1, 4, qb, qb), lambda hp, r: (hp, 0, 0, 0))],
        scratch_shapes=[pltpu.VMEM((4, qb, qb), F32), pltpu.VMEM((length, LANE), F32), pltpu.VMEM((length, LANE), F32)],
        name=f"dilated_backward_{dilation}", compiler_params=_params(("arbitrary", "arbitrary")))(
            table, bc, bp, view, view, view,
            do_g.reshape(wide), lse.reshape(wide), dlt.reshape(wide))
    return dq.reshape(rows, D_MODEL), dk.reshape(rows, D_MODEL), dv.reshape(rows, D_MODEL), db


def _head_sums(x, lo):
    s0 = jnp.sum(jnp.where(lo, x, 0.0), axis=1, keepdims=True)
    s1 = jnp.sum(jnp.where(lo, 0.0, x), axis=1, keepdims=True)
    return jnp.where(lo, s0, s1)


def _dil_merge_forward(outs, lses):
    rows = outs[0].shape[0]

    def body(o0, o1, o2, l0, l1, l2, o_ref):
        ls = [l0[...], l1[...], l2[...]]
        m = jnp.maximum(jnp.maximum(ls[0], ls[1]), ls[2])
        es = [jnp.exp(v - m) for v in ls]
        tot = es[0] + es[1] + es[2]
        o_ref[...] = ((es[0] * o0[...] + es[1] * o1[...] + es[2] * o2[...]) / tot).astype(BF)

    blk = pl.BlockSpec((ROW_TILE, LANE), lambda i, j: (i, j))
    return pl.pallas_call(body, out_shape=jax.ShapeDtypeStruct((rows, D_MODEL), BF), grid=(rows // ROW_TILE, 8),
                          in_specs=[blk] * 6, out_specs=blk, name="dilated_merge_forward",
                          compiler_params=_params(("parallel", "parallel")))(*outs, *lses)


def _dil_merge_backward(outs, lses, do):
    rows = outs[0].shape[0]

    def body(o0, o1, o2, l0, l1, l2, do_ref, d0, d1, d2, t0, t1, t2):
        lo = lax.broadcasted_iota(jnp.int32, (1, LANE), 1) < HEAD_DIM
        ls = [l0[...], l1[...], l2[...]]
        os_ = [o0[...], o1[...], o2[...]]
        m = jnp.maximum(jnp.maximum(ls[0], ls[1]), ls[2])
        es = [jnp.exp(v - m) for v in ls]
        tot = es[0] + es[1] + es[2]
        alphas = [e / tot for e in es]
        dov = do_ref[...]
        merged = alphas[0] * os_[0] + alphas[1] * os_[1] + alphas[2] * os_[2]
        dot = _head_sums(dov * merged, lo)
        for a, d_ref, t_ref in zip(alphas, (d0, d1, d2), (t0, t1, t2)):
            d_ref[...] = (a * dov).astype(BF)
            t_ref[...] = a * dot

    blk = pl.BlockSpec((ROW_TILE, LANE), lambda i, j: (i, j))
    res = pl.pallas_call(
        body, out_shape=[jax.ShapeDtypeStruct((rows, D_MODEL), BF)] * 3 + [jax.ShapeDtypeStruct((rows, D_MODEL), F32)] * 3,
        grid=(rows // ROW_TILE, 8), in_specs=[blk] * 7, out_specs=[blk] * 6, name="dilated_merge_backward",
        compiler_params=_params(("parallel", "parallel")))(*outs, *lses, do)
    return res[:3], res[3:]


def _rel_bias_grad(dbs, buckets):
    def body(db_ref, bc_ref, bp_ref, o_ref):
        g = pl.program_id(0)
        hp = pl.program_id(1)

        @pl.when((g == 0) & (hp == 0))
        def _():
            o_ref[...] = jnp.zeros_like(o_ref)

        rr = lax.broadcasted_iota(jnp.int32, (REL_BUCKETS, LANE), 0)
        cc = lax.broadcasted_iota(jnp.int32, (REL_BUCKETS, LANE), 1)
        bc = bc_ref[0]
        bp = bp_ref[0]
        acc = jnp.zeros((REL_BUCKETS, LANE), F32)
        for hh in range(2):
            col = g * HEADS + 2 * hp + hh
            d_c = db_ref[0, 0, 2 * hh]
            d_p = db_ref[0, 0, 2 * hh + 1]
            for b in range(REL_BUCKETS):
                val = (jnp.sum(jnp.where(bc == b, d_c, 0.0), keepdims=True)
                       + jnp.sum(jnp.where(bp == b, d_p, 0.0), keepdims=True))
                acc = jnp.where((rr == b) & (cc == col), val, acc)
        o_ref[...] += acc

    db_all = jnp.stack(dbs)
    bc_all = jnp.stack([b[0] for b in buckets])
    bp_all = jnp.stack([b[1] for b in buckets])
    tile = pl.BlockSpec((1, Q_BLOCK, Q_BLOCK), lambda g, hp: (g, 0, 0))
    return pl.pallas_call(
        body, out_shape=jax.ShapeDtypeStruct((REL_BUCKETS, LANE), F32), grid=(3, 8),
        in_specs=[pl.BlockSpec((1, 1, 4, Q_BLOCK, Q_BLOCK), lambda g, hp: (g, hp, 0, 0, 0)), tile, tile],
        out_specs=pl.BlockSpec((REL_BUCKETS, LANE), lambda g, hp: (0, 0)), name="rel_bias_grad",
        compiler_params=_params(("arbitrary", "arbitrary")))(db_all, bc_all, bp_all)


def _mla_forward(hn, w, tables):
    a = _matmul(hn, w["w_a"], name="mla_a")
    cq, ckv, kr = _mla_mid_forward(a, w["q_norm"], w["kv_norm"], tables)
    q_raw = _matmul(cq, w["w_uq"], name="mla_uq")
    q = _rope_heads(q_raw, tables, False, "rope_forward")
    kv = _matmul(ckv, w["w_ukv"], b_chunks=True, out_dtype=BF, name="mla_ukv")
    scale = (HEAD_DIM + MLA_ROPE) ** -0.5
    o, lse = _attn_forward(q, kv, 0, kr, None, None, scale, MLA_GROUP, "mla_attention_forward")
    y = _matmul(o, w["w_o"], name="attn_out")
    return y, dict(hn=hn, a=a, cq=cq, ckv=ckv, kr=kr, q=q, kv=kv, o=o, lse=lse)


def _mla_backward(dy, w, s, tables):
    scale = (HEAD_DIM + MLA_ROPE) ** -0.5
    g = {}
    g["w_o"] = _matmul(s["o"], dy, ta=True, out_dtype=BF, name="attn_out_dw")
    do = _matmul(dy, w["w_o"], tb=True, out_dtype=BF, name="attn_out_dx")
    dq, dkv, dkr = _attn_backward(s["q"], s["kv"], 0, s["kr"], None, None, s["o"], do, s["lse"], scale,
                                  MLA_GROUP, "mla_attention_backward")
    dq_raw = _rope_heads(dq, tables, True, "rope_backward")
    g["w_uq"] = _matmul(s["cq"], dq_raw, ta=True, out_dtype=BF, name="mla_uq_dw")
    dcq = _matmul(dq_raw, w["w_uq"], tb=True, name="mla_uq_dx")
    g["w_ukv"] = _matmul(s["ckv"], dkv, ta=True, out_chunks=True, out_dtype=BF, name="mla_ukv_dw")
    dckv = _matmul(dkv, w["w_ukv"], tb=True, b_chunks=True, name="mla_ukv_dx")
    da, g["q_norm"], g["kv_norm"] = _mla_mid_backward(s["a"], w["q_norm"], w["kv_norm"], tables, dcq, dckv, dkr)
    g["w_a"] = _matmul(s["hn"], da, ta=True, out_dtype=BF, name="mla_a_dw")
    dhn = _matmul(da, w["w_a"], tb=True, name="mla_a_dx")
    return dhn, g


def _fox_forward(hn, w):
    qkv = _matmul(hn, w["w_qkv"], out_dtype=BF, name="fox_qkv")
    f_raw = _matmul(hn, w["w_f"], name="fox_f")
    cum = _forget_forward(f_raw, w["b_f"])
    cum_heads = cum[:, :HEADS].T
    cum_col, cum_row = cum_heads[:, :, None], cum_heads[:, None, :]
    o, lse = _attn_forward(qkv, qkv, HEADS, None, cum_col, cum_row, HEAD_DIM ** -0.5, FOX_GROUP,
                           "fox_attention_forward")
    y = _matmul(o, w["w_o"], name="attn_out")
    return y, dict(hn=hn, qkv=qkv, f_raw=f_raw, cum_col=cum_col, cum_row=cum_row, o=o, lse=lse)


def _fox_backward(dy, w, s):
    g = {}
    g["w_o"] = _matmul(s["o"], dy, ta=True, out_dtype=BF, name="attn_out_dw")
    do = _matmul(dy, w["w_o"], tb=True, out_dtype=BF, name="attn_out_dx")
    dq, dkv, dck, dcq = _attn_backward(s["qkv"], s["qkv"], HEADS, None, s["cum_col"], s["cum_row"], s["o"], do,
                                       s["lse"], HEAD_DIM ** -0.5, FOX_GROUP, "fox_attention_backward")
    dcum = jnp.pad((dck[:, 0, :] + dcq[:, :, 0]).T, ((0, 0), (0, LANE - HEADS)))
    df, g["b_f"] = _forget_backward(s["f_raw"], w["b_f"], dcum)
    dqkv = jnp.concatenate([dq, dkv], axis=1)
    g["w_qkv"] = _matmul(s["hn"], dqkv, ta=True, out_dtype=BF, name="fox_qkv_dw")
    g["w_f"] = _matmul(s["hn"], df, ta=True, out_dtype=BF, name="fox_f_dw")
    dhn = _matmul(dqkv, w["w_qkv"], tb=True, name="fox_qkv_dx")
    dhn = _matmul(df, w["w_f"], tb=True, add=dhn, name="fox_f_dx")
    return dhn, g


def _dil_mixer_forward(hn, w, buckets):
    qkv = _matmul(hn, w["w_qkv"], b_chunks=True, out_dtype=BF, name="dil_qkv")
    views = [_dil_view(qkv, grp, dilation) for grp, (_, dilation) in enumerate(DIL_PATTERNS)]
    outs, lses = [], []
    for grp, (_, dilation) in enumerate(DIL_PATTERNS):
        o_g, lse_g = _dil_forward(views[grp], grp, dilation, w["rel_bias"], buckets[grp])
        outs.append(o_g)
        lses.append(lse_g)
    o = _dil_merge_forward(outs, lses)
    y = _matmul(o, w["w_o"], name="dil_out")
    return y, dict(hn=hn, views=views, outs=outs, lses=lses, o=o)


def _dil_mixer_backward(dy, w, s, buckets):
    g = {}
    g["w_o"] = _matmul(s["o"], dy, ta=True, out_dtype=BF, name="dil_out_dw")
    do = _matmul(dy, w["w_o"], tb=True, name="dil_out_dx")
    do_gs, dlts = _dil_merge_backward(s["outs"], s["lses"], do)
    parts, dbs = [], []
    for grp, (_, dilation) in enumerate(DIL_PATTERNS):
        dq, dk, dv, db = _dil_backward(s["views"][grp], grp, dilation, w["rel_bias"], buckets[grp], do_gs[grp],
                                       s["lses"][grp], dlts[grp])
        parts += [dq, dk, dv]
        dbs.append(db)
    dqkv = jnp.concatenate(parts, axis=1)
    g["rel_bias"] = _rel_bias_grad(dbs, buckets)
    g["w_qkv"] = _matmul(s["hn"], dqkv, ta=True, out_chunks=True, out_dtype=BF, name="dil_qkv_dw")
    dhn = _matmul(dqkv, w["w_qkv"], tb=True, b_chunks=True, name="dil_qkv_dx")
    return dhn, g


def _mixer_weights(i, lw, small):
    mixer, j = i % N_MIXERS, i // N_MIXERS
    if mixer == 0:
        return dict(lw["mixer"], q_norm=small["mla_q_norm"][j][None, :], kv_norm=small["mla_kv_norm"][j][None, :])
    if mixer == 1:
        return dict(lw["mixer"], rel_bias=small["rel_bias"])
    return dict(lw["mixer"], b_f=jnp.pad(small["fox_b_f"][j][None, :], ((0, 0), (0, LANE - HEADS))))


MIXER_PART, COMMON_PART = 0, 1


def _run_layers(x, p, positions, target, get_part, get_small, put_part):
    tables = _rope_tables(positions)
    buckets = [_dil_buckets(d) for _, d in DIL_PATTERNS]
    layers, saved = [], []
    h = x
    first = get_part(0, MIXER_PART, positions)
    small = get_small()

    def gain(i, k):
        return small["norm_g"][i, k][None, :]

    hn = _prenorm(h, gain(0, 0))
    sq = dh = None
    for i in range(DEPTH):
        mixer = i % N_MIXERS
        lw = dict(mixer=first if i == 0 else get_part(i, MIXER_PART, h))
        mw = _mixer_weights(i, lw, small)
        if mixer == 0:
            y, ms = _mla_forward(hn, mw, tables)
        elif mixer == 1:
            y, ms = _dil_mixer_forward(hn, mw, buckets)
        else:
            y, ms = _fox_forward(hn, mw)
        lw.update(get_part(i, COMMON_PART, y))
        layers.append(lw)
        h1, hn2 = _post_residual(h, y, gain(i, 1), gain(i, 2))
        gu = _matmul(hn2, lw["ffn_w_in"], b_chunks=True, out_dtype=BF, name="ffn_in")
        act = _swiglu_forward(gu)
        f = _matmul(act, lw["ffn_w_out"], name="ffn_out")
        h2, h2b = _post_residual(h1, f, gain(i, 3), None)
        pp = _matmul(p[i], lw["ple_w_proj"], b_chunks=True, name="ple_proj")
        z = _matmul(h2b, lw["ple_w_gate"], name="ple_gate")
        saved.append(dict(h=h, y=y, ms=ms, h1=h1, hn2=hn2, gu=gu, act=act, f=f, h2b=h2b, pp=pp, z=z))
        if i + 1 < DEPTH:
            h, hn = _ple_forward(h2, pp, z, gain(i + 1, 0))
        else:
            dh, sq = _ple_loss(h2, pp, z, target)

    norm_rows = [[None] * 4 for _ in range(DEPTH)]
    sg = dict(mla_q_norm={}, mla_kv_norm={}, rel_bias=None, fox_b_f={})
    for i in reversed(range(DEPTH)):
        s, lw = saved[i], layers[i]
        mixer, j = i % N_MIXERS, i // N_MIXERS
        mw = _mixer_weights(i, lw, small)
        lg = {}
        dpp, dz = _ple_backward(dh, s["pp"], s["z"])
        lg["ple_w_proj"] = _matmul(p[i], dpp, ta=True, out_chunks=True, out_dtype=BF, name="ple_proj_dw")
        lg["ple_w_gate"] = _matmul(s["h2b"], dz, ta=True, out_dtype=BF, name="ple_gate_dw")
        dh2 = _matmul(dz, lw["ple_w_gate"], tb=True, add=dh, name="ple_gate_dx")
        df, norm_rows[i][3] = _rms_backward(s["f"], gain(i, 3), dh2, None, BF)
        lg["ffn_w_out"] = _matmul(s["act"], df, ta=True, out_dtype=BF, name="ffn_out_dw")
        dact = _matmul(df, lw["ffn_w_out"], tb=True, out_dtype=BF, name="ffn_out_dx")
        dgu = _swiglu_backward(s["gu"], dact)
        lg["ffn_w_in"] = _matmul(s["hn2"], dgu, ta=True, out_chunks=True, out_dtype=BF, name="ffn_in_dw")
        token = put_part(i, COMMON_PART, lg)
        dhn2 = _matmul(dgu, lw["ffn_w_in"], tb=True, b_chunks=True, name="ffn_in_dx")
        dh1, norm_rows[i][2] = _rms_backward(s["h1"], gain(i, 2), dhn2, dh2, F32)
        dy, norm_rows[i][1] = _rms_backward(s["y"], gain(i, 1) + token[0:1, 0:1], dh1, None, BF)
        if mixer == 0:
            dhn, mg = _mla_backward(dy, mw, s["ms"], tables)
            sg["mla_q_norm"][j] = mg.pop("q_norm")
            sg["mla_kv_norm"][j] = mg.pop("kv_norm")
        elif mixer == 1:
            dhn, mg = _dil_mixer_backward(dy, mw, s["ms"], buckets)
            rel = mg.pop("rel_bias")[:, :3 * HEADS]
            sg["rel_bias"] = rel if sg["rel_bias"] is None else sg["rel_bias"] + rel
        else:
            dhn, mg = _fox_backward(dy, mw, s["ms"])
            sg["fox_b_f"][j] = mg.pop("b_f")[:, :HEADS]
        token = put_part(i, MIXER_PART, mg)
        dh, norm_rows[i][0] = _rms_backward(s["h"], gain(i, 0) + token[0:1, 0:1], dhn, dh1, F32)
    small_grads = dict(norm_g=jnp.stack([jnp.concatenate(row, axis=0) for row in norm_rows]),
                       rel_bias=sg["rel_bias"])
    for k in ("mla_q_norm", "mla_kv_norm", "fox_b_f"):
        small_grads[k] = jnp.concatenate([sg[k][j] for j in sorted(sg[k])], axis=0)
    return sq, dh, small_grads


COL_SHARDED = ("ffn_w_in", "ple_w_proj", "mla_w_uq", "mla_w_ukv", "dil_w_qkv", "fox_w_qkvf")
ROW_SHARDED = ("ffn_w_out", "ple_w_gate", "mla_w_a", "mla_w_o", "dil_w_o", "fox_w_o")
BIG = ("ffn_w_in", "ffn_w_out", "ple_w_proj", "ple_w_gate", "mla_w_a", "mla_w_uq", "mla_w_ukv", "mla_w_o",
       "dil_w_qkv", "dil_w_o", "fox_w_qkvf", "fox_w_o")
SMALL_SHARDED = ("norm_g", "mla_q_norm", "mla_kv_norm")
SMALL_REPLICATED = ("rel_bias", "fox_b_f")
WEIGHTS = ("norm_g", "ffn_w_in", "ffn_w_out", "ple_w_proj", "ple_w_gate", "rel_bias", "mla_w_a", "mla_q_norm",
           "mla_kv_norm", "mla_w_uq", "mla_w_ukv", "mla_w_o", "dil_w_qkv", "dil_w_o", "fox_w_qkvf", "fox_b_f", "fox_w_o")


LAYER_COMMON = ("ffn_w_in", "ffn_w_out", "ple_w_proj", "ple_w_gate")
MIXER_WEIGHTS = (("mla_w_a", "mla_w_uq", "mla_w_ukv", "mla_w_o"), ("dil_w_qkv", "dil_w_o"), ("fox_w_qkvf", "fox_w_o"))


def _part_names(i, part):
    return MIXER_WEIGHTS[i % N_MIXERS] if part == MIXER_PART else LAYER_COMMON


def _layer_slot(name, i):
    return i if name in LAYER_COMMON else i // N_MIXERS


def _merge_rows(chunks):
    n, r, c = chunks.shape
    return chunks.reshape(n * r, c)


def _merge_cols(chunks):
    n, r, c = chunks.shape
    return chunks.transpose(1, 0, 2).reshape(r, n * c)


def _pad_heads_out(wo):
    w3 = wo.reshape(HEADS, HEAD_DIM, D_MODEL)
    return jnp.pad(w3, ((0, 0), (HEAD_DIM, 0), (0, 0))).reshape(HEADS * LANE, D_MODEL)


def _part_to_compute(i, part, ch):
    if part == COMMON_PART:
        return dict(ffn_w_in=ch["ffn_w_in"], ffn_w_out=_merge_rows(ch["ffn_w_out"]), ple_w_proj=ch["ple_w_proj"],
                    ple_w_gate=_merge_rows(ch["ple_w_gate"]))
    lw = {}
    mixer = i % N_MIXERS
    if mixer == 0:
        wa = _merge_rows(ch["mla_w_a"])
        rank = MLA_Q_RANK + MLA_KV_RANK
        wa_p = jnp.concatenate([wa[:, :rank], jnp.zeros((wa.shape[0], 64), wa.dtype), wa[:, rank:],
                                jnp.zeros((wa.shape[0], 32), wa.dtype)], axis=1)
        wuq = _merge_cols(ch["mla_w_uq"]).reshape(MLA_Q_RANK, HEADS, HEAD_DIM + MLA_ROPE)
        wuq_p = jnp.pad(wuq, ((0, 0), (0, 0), (0, LANE - HEAD_DIM - MLA_ROPE))).reshape(MLA_Q_RANK, HEADS * LANE)
        lw["mixer"] = dict(w_a=wa_p, w_uq=wuq_p, w_ukv=ch["mla_w_ukv"], w_o=_pad_heads_out(_merge_rows(ch["mla_w_o"])))
    elif mixer == 1:
        lw["mixer"] = dict(w_qkv=ch["dil_w_qkv"], w_o=_merge_rows(ch["dil_w_o"]))
    else:
        wf = _merge_cols(ch["fox_w_qkvf"])
        inner = HEADS * HEAD_DIM
        q3 = wf[:, :inner].reshape(D_MODEL, HEADS, HEAD_DIM)
        k3 = wf[:, inner:2 * inner].reshape(D_MODEL, HEADS, HEAD_DIM)
        v3 = wf[:, 2 * inner:3 * inner].reshape(D_MODEL, HEADS, HEAD_DIM)
        q_p = jnp.pad(q3, ((0, 0), (0, 0), (0, HEAD_DIM))).reshape(D_MODEL, HEADS * LANE)
        kv_p = jnp.concatenate([k3, v3], axis=2).reshape(D_MODEL, HEADS * LANE)
        f_p = jnp.pad(wf[:, 3 * inner:], ((0, 0), (0, LANE - HEADS)))
        lw["mixer"] = dict(w_qkv=jnp.concatenate([q_p, kv_p], axis=1), w_f=f_p,
                           w_o=_pad_heads_out(_merge_rows(ch["fox_w_o"])))
    return lw["mixer"]


def _part_contributions(i, part, lg, chunk_shapes):
    spec = {k: jax.ShapeDtypeStruct(s, BF) for k, s in chunk_shapes.items()}
    (contrib,) = jax.linear_transpose(functools.partial(_part_to_compute, i, part), spec)(lg)
    return contrib


def _chip_peers():
    x, y, c = lax.axis_index("x"), lax.axis_index("y"), lax.axis_index("c")
    peers = [(1 - x, y), (x, 1 - y), (1 - x, 1 - y)]
    return x, y, c, peers


SEM_SPEC = pl.BlockSpec(memory_space=pltpu.SEMAPHORE)
ANY_SPEC = pl.BlockSpec(memory_space=pl.ANY)
SPLIT_EFFECT = pltpu.SideEffectType.DATAFLOW_SIDE_EFFECTING


def _own_slot(shard):
    me = 2 * lax.axis_index("x") + lax.axis_index("y")
    return lax.dynamic_update_index_in_dim(lax.empty((N_CHIPS,) + shard.shape, shard.dtype), shard[None], me, 0)


def _spread_copy(src, land, k, peer, c, send_sems, recv_sems, index, src_slot, slot):
    px, py = peer
    return pltpu.make_async_remote_copy(
        src_ref=src.at[src_slot], dst_ref=land.at[slot],
        send_sem=send_sems.at[3 * index + k], recv_sem=recv_sems.at[3 * index + k],
        device_id=(px, py, c), device_id_type=MESH)


def _spread_start(bufs, srcs, after, name):
    n = len(bufs)
    exchange = srcs is not None
    arrays = (list(srcs) if exchange else []) + list(bufs)
    na = len(arrays)

    def body(*refs):
        src, land = refs[:n], refs[na - n:na]
        send_sems, recv_sems = refs[na + 1], refs[na + 2]
        token = refs[-1]
        x, y, c, peers = _chip_peers()
        me = 2 * x + y
        for w in range(n):
            for k, peer in enumerate(peers):
                src_slot = 2 * peer[0] + peer[1] if exchange else me
                _spread_copy(src[w], land[w], k, peer, c, send_sems, recv_sems, w, src_slot, me).start()
        token[...] = jnp.zeros_like(token)

    hbm = [pltpu.with_memory_space_constraint(a, pltpu.HBM) for a in arrays]
    out = pl.pallas_call(
        body, name=name,
        out_shape=(pltpu.SemaphoreType.DMA((3 * n,)), pltpu.SemaphoreType.DMA((3 * n,)),
                   *[pltpu.HBM(a.shape, a.dtype) for a in hbm], jax.ShapeDtypeStruct((8, LANE), F32)),
        in_specs=[HBM_SPEC] * na + [ANY_SPEC],
        out_specs=(SEM_SPEC, SEM_SPEC, *[HBM_SPEC] * na, pl.BlockSpec(memory_space=pltpu.VMEM)),
        input_output_aliases={w: 2 + w for w in range(na)},
        compiler_params=pltpu.CompilerParams(has_side_effects=SPLIT_EFFECT))(*hbm, after)
    return dict(send=out[0], recv=out[1], arrays=out[2:2 + na], n=n, token=out[-1], exchange=exchange)


def _spread_wait(handle, after, name):
    n, exchange = handle["n"], handle["exchange"]
    arrays = list(handle["arrays"])
    na = len(arrays)

    def body(*refs):
        src, land = refs[:n], refs[na - n:na]
        send_sems, recv_sems = refs[na], refs[na + 1]
        x, y, c, peers = _chip_peers()
        me = 2 * x + y
        for w in range(n):
            for k, peer in enumerate(peers):
                there = 2 * peer[0] + peer[1]
                cp = _spread_copy(src[w], land[w], k, peer, c, send_sems, recv_sems, w, there if exchange else me, there)
                cp.wait_send()
                cp.wait_recv()

    out = pl.pallas_call(
        body, name=name, out_shape=tuple(pltpu.HBM(a.shape, a.dtype) for a in arrays),
        in_specs=[HBM_SPEC] * na + [SEM_SPEC, SEM_SPEC, ANY_SPEC], out_specs=tuple([HBM_SPEC] * na),
        input_output_aliases={w: w for w in range(na)},
        compiler_params=pltpu.CompilerParams(has_side_effects=SPLIT_EFFECT))(*arrays, handle["send"], handle["recv"], after)
    return (list(out[n:]), list(out[:n])) if exchange else list(out)


def _exchange_sibling(arrays, name):
    n = len(arrays)

    def body(*refs):
        ins, outs = refs[:n], refs[n:2 * n]
        send_sems, recv_sems = refs[2 * n:]
        x, y, c = lax.axis_index("x"), lax.axis_index("y"), lax.axis_index("c")
        copies = [pltpu.make_async_remote_copy(src_ref=ins[w], dst_ref=outs[w], send_sem=send_sems.at[w],
                                               recv_sem=recv_sems.at[w], device_id=(x, y, 1 - c), device_id_type=MESH)
                  for w in range(n)]
        for cp in copies:
            cp.start()
        for cp in copies:
            cp.wait_recv()
        for cp in copies:
            cp.wait_send()

    return pl.pallas_call(
        body, out_shape=[jax.ShapeDtypeStruct(s.shape, s.dtype) for s in arrays],
        in_specs=[HBM_SPEC] * n, out_specs=[HBM_SPEC] * n,
        scratch_shapes=[pltpu.SemaphoreType.DMA((n,)), pltpu.SemaphoreType.DMA((n,))], name=name)(*arrays)


def _all_reduce_small(v):
    rows = v.shape[0]

    def body(v_ref, sum_ref, slots, send_sems, recv_sems):
        x, y, c = lax.axis_index("x"), lax.axis_index("y"), lax.axis_index("c")
        me = 4 * x + 2 * y + c
        slots[me] = v_ref[...]
        sends = []
        for k in range(1, N_DEV):
            bx, by, bc = (k >> 2) & 1, (k >> 1) & 1, k & 1
            peer = (x ^ bx, y ^ by, c ^ bc)
            rc = pltpu.make_async_remote_copy(src_ref=v_ref, dst_ref=slots.at[me], send_sem=send_sems.at[k],
                                              recv_sem=recv_sems.at[k], device_id=peer, device_id_type=MESH)
            rc.start()
            sends.append(rc)
        for k in range(1, N_DEV):
            bx, by, bc = (k >> 2) & 1, (k >> 1) & 1, k & 1
            src = 4 * (x ^ bx) + 2 * (y ^ by) + (c ^ bc)
            pltpu.make_async_remote_copy(src_ref=v_ref, dst_ref=slots.at[src], send_sem=send_sems.at[k],
                                         recv_sem=recv_sems.at[k], device_id=(x ^ bx, y ^ by, c ^ bc),
                                         device_id_type=MESH).wait_recv()
        for rc in sends:
            rc.wait_send()
        total = slots[0]
        for k in range(1, N_DEV):
            total = total + slots[k]
        sum_ref[...] = total

    vm = pl.BlockSpec(memory_space=pltpu.VMEM)
    return pl.pallas_call(
        body, out_shape=jax.ShapeDtypeStruct((rows, LANE), F32), in_specs=[vm], out_specs=vm,
        scratch_shapes=[pltpu.VMEM((N_DEV, rows, LANE), F32), pltpu.SemaphoreType.DMA((N_DEV,)),
                        pltpu.SemaphoreType.DMA((N_DEV,))], name="all_reduce_small")(v)


def _as_2d(a):
    return a.reshape(-1, a.shape[-1])


def _row_tile(rows, cols):
    for t in (512, 256, 128, 64, 32, 16):
        if rows % t == 0 and t * cols * 4 <= (1 << 20):
            return t
    return rows


def _sum_chips_into(received, sent, stacked, slot):
    _, rows, cols = received.shape
    tr = _row_tile(rows, cols)
    first = slot * (rows // tr)
    me = (2 * lax.axis_index("x") + lax.axis_index("y")).astype(jnp.int32).reshape(1)

    def body(me_ref, r_ref, own_ref, _, o_ref):
        total = None
        for k in range(N_CHIPS):
            part = jnp.where(me_ref[0] == k, own_ref[...], r_ref[k]).astype(F32)
            total = part if total is None else total + part
        o_ref[...] = total

    grid_spec = pltpu.PrefetchScalarGridSpec(
        num_scalar_prefetch=1, grid=(rows // tr,),
        in_specs=[pl.BlockSpec((N_CHIPS, tr, cols), lambda i, me_ref: (0, i, 0)),
                  pl.BlockSpec((None, tr, cols), lambda i, me_ref: (me_ref[0], i, 0)), ANY_SPEC],
        out_specs=pl.BlockSpec((tr, cols), lambda i, me_ref: (first + i, 0)))
    return pl.pallas_call(body, out_shape=jax.ShapeDtypeStruct(stacked.shape, F32), grid_spec=grid_spec,
                          input_output_aliases={3: 0}, name="sum_chips",
                          compiler_params=_params(("parallel",)))(me, received, sent, stacked)


def _adamw_math(w, g, m, v):
    m = ADAM_B1 * m + (1.0 - ADAM_B1) * g
    v = ADAM_B2 * v + (1.0 - ADAM_B2) * (g * g)
    m_hat = m / (1.0 - ADAM_B1 ** ADAM_STEP)
    v_hat = v / (1.0 - ADAM_B2 ** ADAM_STEP)
    delta = -ADAM_LR * (m_hat / (jnp.sqrt(v_hat) + ADAM_EPS) + ADAM_WD * w)
    return delta, m, v


def _adamw(w, m, v, g_mine, g_sibling):
    rows, cols = w.shape
    tr = _row_tile(rows, cols)
    two = g_sibling is not None

    def body(*refs):
        if two:
            w_ref, m_ref, v_ref, ga_ref, gb_ref, g_ref, d_ref, nm_ref, nv_ref = refs
            g = ga_ref[...] + gb_ref[...]
        else:
            w_ref, m_ref, v_ref, ga_ref, g_ref, d_ref, nm_ref, nv_ref = refs
            g = ga_ref[...]
        delta, nm, nv = _adamw_math(w_ref[...], g, m_ref[...], v_ref[...])
        g_ref[...] = g
        d_ref[...] = delta
        nm_ref[...] = nm
        nv_ref[...] = nv

    blk = pl.BlockSpec((tr, cols), lambda i: (i, 0))
    args = [w, m, v, g_mine] + ([g_sibling] if two else [])
    return pl.pallas_call(body, out_shape=[jax.ShapeDtypeStruct((rows, cols), F32)] * 4, grid=(rows // tr,),
                          in_specs=[blk] * len(args), out_specs=[blk] * 4, name="adamw",
                          compiler_params=_params(("parallel",)))(*args)


def _pack_rows(arrays):
    flat = jnp.concatenate([a.reshape(-1) for a in arrays])
    rows = -(-flat.shape[0] // (8 * LANE)) * 8
    return jnp.pad(flat, (0, rows * LANE - flat.shape[0])).reshape(rows, LANE)


def _unpack_rows(packed, shapes):
    flat = packed.reshape(-1)
    out, at = [], 0
    for s in shapes:
        size = math.prod(s)
        out.append(flat[at:at + size].reshape(s))
        at += size
    return out


def kernel(x, p, positions, norm_g, ffn_w_in, ffn_w_out, ple_w_proj, ple_w_gate, rel_bias, mla_w_a, mla_q_norm, mla_kv_norm, mla_w_uq, mla_w_ukv, mla_w_o, dil_w_qkv, dil_w_o, fox_w_qkvf, fox_b_f, fox_w_o, loss_target, m_norm_g, m_ffn_w_in, m_ffn_w_out, m_ple_w_proj, m_ple_w_gate, m_rel_bias, m_mla_w_a, m_mla_q_norm, m_mla_kv_norm, m_mla_w_uq, m_mla_w_ukv, m_mla_w_o, m_dil_w_qkv, m_dil_w_o, m_fox_w_qkvf, m_fox_b_f, m_fox_w_o, v_norm_g, v_ffn_w_in, v_ffn_w_out, v_ple_w_proj, v_ple_w_gate, v_rel_bias, v_mla_w_a, v_mla_q_norm, v_mla_kv_norm, v_mla_w_uq, v_mla_w_ukv, v_mla_w_o, v_dil_w_qkv, v_dil_w_o, v_fox_w_qkvf, v_fox_b_f, v_fox_w_o):
    w = dict(norm_g=norm_g, ffn_w_in=ffn_w_in, ffn_w_out=ffn_w_out, ple_w_proj=ple_w_proj, ple_w_gate=ple_w_gate,
             rel_bias=rel_bias, mla_w_a=mla_w_a, mla_q_norm=mla_q_norm, mla_kv_norm=mla_kv_norm, mla_w_uq=mla_w_uq,
             mla_w_ukv=mla_w_ukv, mla_w_o=mla_w_o, dil_w_qkv=dil_w_qkv, dil_w_o=dil_w_o, fox_w_qkvf=fox_w_qkvf,
             fox_b_f=fox_b_f, fox_w_o=fox_w_o)
    m = dict(norm_g=m_norm_g, ffn_w_in=m_ffn_w_in, ffn_w_out=m_ffn_w_out, ple_w_proj=m_ple_w_proj,
             ple_w_gate=m_ple_w_gate, rel_bias=m_rel_bias, mla_w_a=m_mla_w_a, mla_q_norm=m_mla_q_norm,
             mla_kv_norm=m_mla_kv_norm, mla_w_uq=m_mla_w_uq, mla_w_ukv=m_mla_w_ukv, mla_w_o=m_mla_w_o,
             dil_w_qkv=m_dil_w_qkv, dil_w_o=m_dil_w_o, fox_w_qkvf=m_fox_w_qkvf, fox_b_f=m_fox_b_f, fox_w_o=m_fox_w_o)
    v = dict(norm_g=v_norm_g, ffn_w_in=v_ffn_w_in, ffn_w_out=v_ffn_w_out, ple_w_proj=v_ple_w_proj,
             ple_w_gate=v_ple_w_gate, rel_bias=v_rel_bias, mla_w_a=v_mla_w_a, mla_q_norm=v_mla_q_norm,
             mla_kv_norm=v_mla_kv_norm, mla_w_uq=v_mla_w_uq, mla_w_ukv=v_mla_w_ukv, mla_w_o=v_mla_w_o,
             dil_w_qkv=v_dil_w_qkv, dil_w_o=v_dil_w_o, fox_w_qkvf=v_fox_w_qkvf, fox_b_f=v_fox_b_f, fox_w_o=v_fox_w_o)
    chip = 2 * lax.axis_index("x") + lax.axis_index("y")

    small_shapes = [w[k].shape for k in SMALL_SHARDED]
    order = [(i, part) for i in range(DEPTH) for part in (MIXER_PART, COMMON_PART)]
    gathers = {}
    after = positions
    for i, part in order:
        bufs = [_own_slot(w[k][_layer_slot(k, i)].astype(BF)) for k in _part_names(i, part)]
        if (i, part) == order[0]:
            bufs.append(_own_slot(_pack_rows([w[k] for k in SMALL_SHARDED])))
        gathers[i, part] = _spread_start(bufs, None, after, f"gather_start_{i}_{part}")
        after = gathers[i, part]["token"]
    all_started = after
    state = {}

    def get_part(i, part, after_array):
        is_first = (i, part) == order[0]
        lands = _spread_wait(gathers[i, part], all_started if is_first else after_array, f"gather_wait_{i}_{part}")
        if is_first:
            pieces = [_unpack_rows(lands[-1][k], small_shapes) for k in range(N_CHIPS)]
            small = {name: jnp.concatenate([pieces[k][idx] for k in range(N_CHIPS)], axis=-1)
                     for idx, name in enumerate(SMALL_SHARDED)}
            state["small"] = dict(small, rel_bias=rel_bias, fox_b_f=fox_b_f)
        chunks = dict(zip(_part_names(i, part), lands))
        state[i, part] = {k: a.shape for k, a in chunks.items()}
        return _part_to_compute(i, part, chunks)

    exchanges = {}

    def put_part(i, part, lg):
        contrib = _part_contributions(i, part, lg, state[i, part])
        srcs = [contrib[k] for k in _part_names(i, part)]
        exchanges[i, part] = _spread_start([lax.empty(s.shape, s.dtype) for s in srcs], srcs, positions,
                                           f"exchange_start_{i}_{part}")
        return exchanges[i, part]["token"]

    sq, grad_x, sg = _run_layers(x[0], p[:, 0], positions[0], loss_target[0], get_part, lambda: state["small"],
                                 put_part)
    loss = lax.psum(0.5 / D_MODEL * jnp.sum(sq), ("x", "y", "c"))

    mine = {k: lax.empty(_as_2d(w[k]).shape, F32) for k in BIG}
    for i, part in [(i, part) for i in reversed(range(DEPTH)) for part in (COMMON_PART, MIXER_PART)]:
        received, sent = _spread_wait(exchanges[i, part], grad_x, f"exchange_wait_{i}_{part}")
        for k, r, s in zip(_part_names(i, part), received, sent):
            mine[k] = _sum_chips_into(r, s, mine[k], _layer_slot(k, i))
    theirs = _exchange_sibling([mine[k] for k in BIG], "exchange_sibling")
    results = {}
    for k, gb in zip(BIG, theirs):
        outs = _adamw(_as_2d(w[k]), _as_2d(m[k]), _as_2d(v[k]), mine[k], gb)
        results[k] = [o.reshape(w[k].shape) for o in outs]

    small_all = SMALL_SHARDED + SMALL_REPLICATED
    full_shapes = [sg[k].shape for k in small_all]
    reduced = dict(zip(small_all, _unpack_rows(_all_reduce_small(_pack_rows([sg[k] for k in small_all])), full_shapes)))
    local_g = []
    for k in small_all:
        g = reduced[k]
        if k in SMALL_SHARDED:
            width = w[k].shape[-1]
            g = lax.dynamic_slice_in_dim(g, chip * width, width, axis=g.ndim - 1)
        local_g.append(g)
    local_shapes = [w[k].shape for k in small_all]
    outs = _adamw(_pack_rows([w[k] for k in small_all]), _pack_rows([m[k] for k in small_all]),
                  _pack_rows([v[k] for k in small_all]), _pack_rows(local_g), None)
    unpacked = [_unpack_rows(o, local_shapes) for o in outs]
    for idx, k in enumerate(small_all):
        results[k] = [u[idx] for u in unpacked]

    return (loss, grad_x[None], *[results[k][0] for k in WEIGHTS], *[results[k][1] for k in WEIGHTS],
            *[results[k][2] for k in WEIGHTS], *[results[k][3] for k in WEIGHTS])
```

```python
import functools
import math

import jax
import jax.numpy as jnp
from jax import lax
from jax.experimental import pallas as pl
from jax.experimental.pallas import tpu as pltpu

F32 = jnp.float32
BF = jnp.bfloat16
MESH = pl.DeviceIdType.MESH
HBM_SPEC = pl.BlockSpec(memory_space=pltpu.HBM)

D_MODEL = 1024
DEPTH = 4
N_MIXERS = 3
D_FF = 2816
NORM_EPS = 1e-6
NEG_INF = -1e30
LANE = 128
HEADS = 16
HEAD_DIM = 64
MLA_Q_RANK = 384
MLA_KV_RANK = 256
MLA_ROPE = 32
MLA_A_PAD = 768
ROPE_THETA = 10000.0
DIL_PATTERNS = ((128, 1), (512, 4), (2048, 16))
Q_BLOCK = 128
DIL_PAIRS = 2
REL_BUCKETS = 32
REL_MAX_DIST = 2048
N_CHIPS = 4
N_DEV = 8

ADAM_LR = 0.001
ADAM_B1 = 0.9
ADAM_B2 = 0.999
ADAM_EPS = 1e-08
ADAM_WD = 0.01
ADAM_STEP = 10

VMEM_LIMIT = 56 * 1024 * 1024
MATMUL_VMEM_BUDGET = 36 * 1024 * 1024
ROW_TILE = 256
ATTN_TILE = 256
MLA_GROUP = 4
FOX_GROUP = 4


def _params(sem=None):
    return pltpu.CompilerParams(dimension_semantics=sem, vmem_limit_bytes=VMEM_LIMIT)


def _divisor_tiles(dim):
    tiles = [t for t in range(LANE, dim + 1, LANE) if dim % t == 0]
    return tiles or [dim]


def _matmul_tiles(m, n, k, a_bytes, b_bytes, out_bytes, has_add, n_unit=None, k_unit=None):
    best = None
    for tm in _divisor_tiles(m):
        for tn in _divisor_tiles(n_unit or n):
            for tk in _divisor_tiles(k_unit or k):
                if max(tm, tn, tk) > 2048:
                    continue
                vmem = 2 * (tm * tk * a_bytes + tk * tn * b_bytes + tm * tn * out_bytes) + tm * tn * 4
                if has_add:
                    vmem += 2 * tm * tn * 4
                if vmem > MATMUL_VMEM_BUDGET:
                    continue
                steps = (m // tm) * (n // tn) * (k // tk)
                traffic = m * k * a_bytes * (n // tn) + k * n * b_bytes * (m // tm) + m * n * out_bytes
                cost = traffic / 3.0e12 + steps * 0.4e-6
                if best is None or cost < best[0]:
                    best = (cost, tm, tn, tk)
    return best[1:]


def _matmul(a, b, *, ta=False, tb=False, b_chunks=False, out_chunks=False, add=None, out_dtype=F32, name):
    k, m = a.shape if ta else a.shape[::-1]
    n_unit = k_unit = None
    if b_chunks:
        chunks, rows_w, c = b.shape
        if tb:
            kb, n, k_unit = chunks * c, rows_w, c
        else:
            kb, n, n_unit = rows_w, chunks * c, c
    else:
        kb, n = b.shape[::-1] if tb else b.shape
    if out_chunks:
        assert n % N_CHIPS == 0 and add is None
        n_unit = n // N_CHIPS
    assert k == kb, (a.shape, b.shape, ta, tb)
    tm, tn, tk = _matmul_tiles(m, n, k, a.dtype.itemsize, b.dtype.itemsize, jnp.dtype(out_dtype).itemsize,
                               add is not None, n_unit, k_unit)
    nk = k // tk
    dims = (((0 if ta else 1,), (1 if tb else 0,)), ((), ()))

    def body(*refs):
        if add is None:
            a_ref, b_ref, o_ref, acc_ref = refs
            add_ref = None
        else:
            a_ref, b_ref, add_ref, o_ref, acc_ref = refs
        kk = pl.program_id(2)

        @pl.when(kk == 0)
        def _():
            acc_ref[...] = jnp.zeros_like(acc_ref)

        acc_ref[...] += lax.dot_general(a_ref[...].astype(BF), b_ref[...].astype(BF), dims,
                                        preferred_element_type=F32)

        @pl.when(kk == nk - 1)
        def _():
            r = acc_ref[...]
            if add_ref is not None:
                r = r + add_ref[...].astype(F32)
            o_ref[...] = r.astype(out_dtype)

    a_spec = pl.BlockSpec((tk, tm), lambda i, j, q: (q, i)) if ta else pl.BlockSpec((tm, tk), lambda i, j, q: (i, q))
    if b_chunks and tb:
        per_k = k_unit // tk
        b_spec = pl.BlockSpec((None, tn, tk), lambda i, j, q: (q // per_k, j, q % per_k))
    elif b_chunks:
        per_n = n_unit // tn
        b_spec = pl.BlockSpec((None, tk, tn), lambda i, j, q: (j // per_n, q, j % per_n))
    elif tb:
        b_spec = pl.BlockSpec((tn, tk), lambda i, j, q: (j, q))
    else:
        b_spec = pl.BlockSpec((tk, tn), lambda i, j, q: (q, j))
    if out_chunks:
        per_o = n_unit // tn
        o_spec = pl.BlockSpec((None, tm, tn), lambda i, j, q: (j // per_o, i, j % per_o))
        out_shape = jax.ShapeDtypeStruct((N_CHIPS, m, n_unit), out_dtype)
    else:
        o_spec = pl.BlockSpec((tm, tn), lambda i, j, q: (i, j))
        out_shape = jax.ShapeDtypeStruct((m, n), out_dtype)
    in_specs = [a_spec, b_spec]
    args = [a, b]
    if add is not None:
        in_specs.append(o_spec)
        args.append(add)
    return pl.pallas_call(
        body, out_shape=out_shape, grid=(m // tm, n // tn, nk),
        in_specs=in_specs, out_specs=o_spec, scratch_shapes=[pltpu.VMEM((tm, tn), F32)], name=name,
        compiler_params=_params(("parallel", "parallel", "arbitrary")))(*args)


def _rowwise(body, name, rows, ins, outs, tr=ROW_TILE):
    def row_spec(cols):
        return pl.BlockSpec((tr, cols), lambda i: (i, 0))

    def full_spec(shape):
        zeros = (0,) * len(shape)
        return pl.BlockSpec(shape, lambda i: zeros)

    in_specs = [row_spec(a.shape[1]) if kind == "row" else full_spec(a.shape) for a, kind in ins]
    out_specs = [row_spec(shape[1]) if kind == "row" else full_spec(shape) for shape, _, kind in outs]
    out_shape = [jax.ShapeDtypeStruct(shape, dtype) for shape, dtype, _ in outs]
    return pl.pallas_call(body, out_shape=out_shape, grid=(rows // tr,), in_specs=in_specs, out_specs=out_specs,
                          name=name, compiler_params=_params(("arbitrary",)))(*[a for a, _ in ins])


def _rstd(x):
    return lax.rsqrt(jnp.mean(x * x, axis=-1, keepdims=True) + NORM_EPS)


def _rms_bwd_math(x, g, dy):
    r = _rstd(x)
    gd = dy * g
    dx = r * gd - x * (r * r * r) * jnp.mean(gd * x, axis=-1, keepdims=True)
    dg = jnp.sum(dy * x * r, axis=0, keepdims=True)
    return dx, dg


def _sigmoid(x):
    return 0.5 * jnp.tanh(0.5 * x) + 0.5


def _init_acc(*refs):
    @pl.when(pl.program_id(0) == 0)
    def _():
        for r in refs:
            r[...] = jnp.zeros_like(r)


def _prenorm(h, g):
    rows, cols = h.shape

    def body(h_ref, g_ref, o_ref):
        x = h_ref[...]
        o_ref[...] = (x * _rstd(x) * g_ref[...]).astype(BF)

    return _rowwise(body, "prenorm", rows, [(h, "row"), (g, "full")], [((rows, cols), BF, "row")])[0]


def _post_residual(h, y, g_post, g_pre):
    rows, cols = h.shape
    with_pre = g_pre is not None

    def body(*refs):
        if with_pre:
            h_ref, y_ref, gp_ref, gq_ref, hn_ref, hb_ref = refs
        else:
            h_ref, y_ref, gp_ref, hn_ref, hb_ref = refs
        yv = y_ref[...]
        hn = h_ref[...] + yv * _rstd(yv) * gp_ref[...]
        hn_ref[...] = hn
        hb_ref[...] = (hn * _rstd(hn) * gq_ref[...] if with_pre else hn).astype(BF)

    ins = [(h, "row"), (y, "row"), (g_post, "full")] + ([(g_pre, "full")] if with_pre else [])
    return _rowwise(body, "post_residual_pre" if with_pre else "post_residual", rows, ins,
                    [((rows, cols), F32, "row"), ((rows, cols), BF, "row")])


def _ple_forward(h2, pp, z, g_pre):
    rows, cols = h2.shape

    def body(h_ref, p_ref, z_ref, g_ref, h3_ref, hb_ref):
        h3 = h_ref[...] + p_ref[...] * _sigmoid(z_ref[...])
        h3_ref[...] = h3
        hb_ref[...] = (h3 * _rstd(h3) * g_ref[...]).astype(BF)

    return _rowwise(body, "ple_forward", rows, [(h2, "row"), (pp, "row"), (z, "row"), (g_pre, "full")],
                    [((rows, cols), F32, "row"), ((rows, cols), BF, "row")])


def _ple_loss(h2, pp, z, target):
    rows, cols = h2.shape

    def body(h_ref, p_ref, z_ref, t_ref, dh_ref, sq_ref):
        _init_acc(sq_ref)
        err = h_ref[...] + p_ref[...] * _sigmoid(z_ref[...]) - t_ref[...]
        dh_ref[...] = err * (1.0 / cols)
        sq_ref[...] += jnp.sum(err * err, axis=0, keepdims=True)

    return _rowwise(body, "ple_loss", rows, [(h2, "row"), (pp, "row"), (z, "row"), (target, "row")],
                    [((rows, cols), F32, "row"), ((1, cols), F32, "acc")])


def _ple_backward(dh3, pp, z):
    rows, cols = dh3.shape

    def body(d_ref, p_ref, z_ref, dpp_ref, dz_ref):
        d = d_ref[...]
        s = _sigmoid(z_ref[...])
        dpp_ref[...] = (d * s).astype(BF)
        dz_ref[...] = (d * p_ref[...] * s * (1.0 - s)).astype(BF)

    return _rowwise(body, "ple_backward", rows, [(dh3, "row"), (pp, "row"), (z, "row")],
                    [((rows, cols), BF, "row"), ((rows, cols), BF, "row")])


def _rms_backward(x, g, dy, add, out_dtype):
    rows, cols = x.shape
    with_add = add is not None

    def body(*refs):
        if with_add:
            x_ref, g_ref, dy_ref, add_ref, dx_ref, dg_ref = refs
        else:
            x_ref, g_ref, dy_ref, dx_ref, dg_ref = refs
        _init_acc(dg_ref)
        dx, dg = _rms_bwd_math(x_ref[...], g_ref[...], dy_ref[...].astype(F32))
        if with_add:
            dx = dx + add_ref[...]
        dx_ref[...] = dx.astype(out_dtype)
        dg_ref[...] += dg

    ins = [(x, "row"), (g, "full"), (dy, "row")] + ([(add, "row")] if with_add else [])
    return _rowwise(body, "rms_backward_add" if with_add else "rms_backward", rows, ins,
                    [((rows, cols), out_dtype, "row"), ((1, cols), F32, "acc")])


def _swiglu_forward(gu):
    rows = gu.shape[0]

    def body(gu_ref, o_ref):
        g = gu_ref[:, :D_FF].astype(F32)
        o_ref[...] = (g * _sigmoid(g) * gu_ref[:, D_FF:].astype(F32)).astype(BF)

    return _rowwise(body, "swiglu_forward", rows, [(gu, "row")], [((rows, D_FF), BF, "row")])[0]


def _swiglu_backward(gu, dact):
    rows = gu.shape[0]

    def body(gu_ref, d_ref, o_ref):
        g = gu_ref[:, :D_FF].astype(F32)
        u = gu_ref[:, D_FF:].astype(F32)
        d = d_ref[...].astype(F32)
        s = _sigmoid(g)
        gs = g * s
        o_ref[:, :D_FF] = (d * u * (s + gs * (1.0 - s))).astype(BF)
        o_ref[:, D_FF:] = (d * gs).astype(BF)

    return _rowwise(body, "swiglu_backward", rows, [(gu, "row"), (dact, "row")], [((rows, 2 * D_FF), BF, "row")])[0]


def _rope_tables(positions):
    half = MLA_ROPE // 2
    inv = ROPE_THETA ** (-jnp.arange(half, dtype=F32) / half)
    ang = positions.astype(F32)[:, None] * inv
    cos, sin = jnp.cos(ang), jnp.sin(ang)
    rows = positions.shape[0]
    c = jnp.ones((rows, LANE), F32).at[:, 64:80].set(cos).at[:, 80:96].set(cos)
    sa = jnp.zeros((rows, LANE), F32).at[:, 64:80].set(-sin)
    sb = jnp.zeros((rows, LANE), F32).at[:, 80:96].set(sin)
    return c, sa, sb


def _rope_apply(x, c, sa, sb):
    return x * c + pltpu.roll(x, LANE - 16, 1) * sa + pltpu.roll(x, 16, 1) * sb


def _rope_apply_t(dy, c, sa, sb):
    return dy * c + pltpu.roll(dy * sa, 16, 1) + pltpu.roll(dy * sb, LANE - 16, 1)


def _rope_heads(x, tables, transpose, name):
    rows, cols = x.shape

    def body(x_ref, c_ref, sa_ref, sb_ref, o_ref):
        fn = _rope_apply_t if transpose else _rope_apply
        c, sa, sb = c_ref[...], sa_ref[...], sb_ref[...]
        for head in range(cols // LANE):
            lanes = slice(head * LANE, (head + 1) * LANE)
            o_ref[:, lanes] = fn(x_ref[:, lanes].astype(F32), c, sa, sb).astype(BF)

    blk = pl.BlockSpec((ROW_TILE, cols), lambda i: (i, 0))
    tbl = pl.BlockSpec((ROW_TILE, LANE), lambda i: (i, 0))
    return pl.pallas_call(body, out_shape=jax.ShapeDtypeStruct((rows, cols), BF), grid=(rows // ROW_TILE,),
                          in_specs=[blk, tbl, tbl, tbl], out_specs=blk, name=name,
                          compiler_params=_params(("parallel",)))(x, *tables)


def _mla_mid_forward(a, q_norm, kv_norm, tables):
    rows = a.shape[0]
    qr, kvr = MLA_Q_RANK, MLA_KV_RANK

    def body(a_ref, qn_ref, kn_ref, c_ref, sa_ref, sb_ref, cq_ref, ckv_ref, kr_ref):
        aq = a_ref[:, 0:qr]
        akv = a_ref[:, qr:qr + kvr]
        cq_ref[...] = (aq * _rstd(aq) * qn_ref[...]).astype(BF)
        ckv_ref[...] = (akv * _rstd(akv) * kn_ref[...]).astype(BF)
        kr_ref[...] = _rope_apply(a_ref[:, qr + kvr:], c_ref[...], sa_ref[...], sb_ref[...]).astype(BF)

    ins = [(a, "row"), (q_norm, "full"), (kv_norm, "full")] + [(t, "row") for t in tables]
    return _rowwise(body, "mla_mid_forward", rows, ins,
                    [((rows, qr), BF, "row"), ((rows, kvr), BF, "row"), ((rows, LANE), BF, "row")])


def _mla_mid_backward(a, q_norm, kv_norm, tables, dcq, dckv, dkr):
    rows = a.shape[0]
    qr, kvr = MLA_Q_RANK, MLA_KV_RANK

    def body(a_ref, qn_ref, kn_ref, c_ref, sa_ref, sb_ref, dcq_ref, dckv_ref, dkr_ref, da_ref, dqn_ref, dkn_ref):
        _init_acc(dqn_ref, dkn_ref)
        dxq, dgq = _rms_bwd_math(a_ref[:, 0:qr], qn_ref[...], dcq_ref[...])
        dxk, dgk = _rms_bwd_math(a_ref[:, qr:qr + kvr], kn_ref[...], dckv_ref[...])
        da_ref[:, 0:qr] = dxq.astype(BF)
        da_ref[:, qr:qr + kvr] = dxk.astype(BF)
        da_ref[:, qr + kvr:] = _rope_apply_t(dkr_ref[...], c_ref[...], sa_ref[...], sb_ref[...]).astype(BF)
        dqn_ref[...] += dgq
        dkn_ref[...] += dgk

    ins = ([(a, "row"), (q_norm, "full"), (kv_norm, "full")] + [(t, "row") for t in tables]
           + [(dcq, "row"), (dckv, "row"), (dkr, "row")])
    return _rowwise(body, "mla_mid_backward", rows, ins,
                    [((rows, MLA_A_PAD), BF, "row"), ((1, qr), F32, "acc"), ((1, kvr), F32, "acc")])


def _attn_specs(rows, kv_off, g, many_row_vectors):
    head =pl.BlockSpec((rows, g * LANE), lambda h: (0, h))
    kv_head = pl.BlockSpec((rows, g * LANE), lambda h: (0, h + kv_off // g))
    shared = pl.BlockSpec((rows, LANE), lambda h: (0, 0))
    col_vec = pl.BlockSpec((g, rows, 1), lambda h: (h, 0, 0),
                           pipeline_mode=pl.Buffered(1 if many_row_vectors else 2))
    row_vec = pl.BlockSpec((g, 1, rows), lambda h: (h, 0, 0))
    return head, kv_head, shared, col_vec, row_vec


def _attn_forward(q, kv, kv_off, kr, cum_col, cum_row, scale, group_size, name):
    rows = q.shape[0]
    heads = HEADS
    t = ATTN_TILE
    nb = rows // t
    has_kr = kr is not None
    has_f = cum_col is not None
    group = range(group_size)

    def body(*refs):
        it = iter(refs)
        q_ref, kv_ref = next(it), next(it)
        kr_ref = next(it) if has_kr else None
        cc_ref = next(it) if has_f else None
        cr_ref = next(it) if has_f else None
        o_ref, lse_ref = next(it), next(it)
        lo = lax.broadcasted_iota(jnp.int32, (1, LANE), 1) < HEAD_DIM
        causal = (lax.broadcasted_iota(jnp.int32, (t, t), 1) <= lax.broadcasted_iota(jnp.int32, (t, t), 0))
        lanes = [slice(g * LANE, (g + 1) * LANE) for g in group]

        def q_block(i, _):
            qs = pl.ds(pl.multiple_of(i * t, t), t)
            qbs = [q_ref[qs, lanes[g]] for g in group]
            cqs = [cc_ref[g, qs, :] if has_f else None for g in group]

            def step(j, carry, diag):
                ks = pl.ds(pl.multiple_of(j * t, t), t)
                other = kr_ref[ks, :] if has_kr else jnp.zeros((t, LANE), BF)
                out = []
                for g in group:
                    m, l, acc = carry[g]
                    kvb = kv_ref[ks, lanes[g]]
                    kk = jnp.where(lo, kvb, other)
                    s = lax.dot_general(qbs[g], kk, (((1,), (1,)), ((), ())), preferred_element_type=F32) * scale
                    if has_f:
                        s = s + (cqs[g] - cr_ref[g, :, ks])
                    if diag:
                        s = jnp.where(causal, s, NEG_INF)
                    mn = jnp.maximum(m, jnp.max(s, axis=1, keepdims=True))
                    alpha = jnp.exp(m - mn)
                    p = jnp.exp(s - mn)
                    l = alpha * l + jnp.sum(p, axis=1, keepdims=True)
                    acc = alpha * acc + jnp.dot(p.astype(BF), kvb, preferred_element_type=F32)
                    out.append((mn, l, acc))
                return tuple(out)

            init = tuple((jnp.full((t, 1), NEG_INF, F32), jnp.zeros((t, 1), F32), jnp.zeros((t, LANE), F32))
                         for _ in group)
            carry = lax.fori_loop(0, i, lambda j, c: step(j, c, False), init)
            for g, (m, l, acc) in enumerate(step(i, carry, True)):
                o_ref[qs, lanes[g]] = jnp.where(lo, 0.0, acc * (1.0 / l)).astype(BF)
                lse_ref[g, qs, :] = m + jnp.log(l)
            return 0

        lax.fori_loop(0, nb, q_block, 0)

    head, kv_head, shared, col_vec, row_vec = _attn_specs(rows, kv_off, group_size, has_f)
    in_specs, args = [head, kv_head], [q, kv]
    if has_kr:
        in_specs.append(shared)
        args.append(kr)
    if has_f:
        in_specs += [col_vec, row_vec]
        args += [cum_col, cum_row]
    return pl.pallas_call(
        body, out_shape=[jax.ShapeDtypeStruct((rows, heads * LANE), BF), jax.ShapeDtypeStruct((heads, rows, 1), F32)],
        grid=(heads // group_size,), in_specs=in_specs, out_specs=[head, col_vec], name=name,
        compiler_params=_params(("arbitrary",)))(*args)


def _attn_backward(q, kv, kv_off, kr, cum_col, cum_row, o, do, lse, scale, group_size, name):
    rows = q.shape[0]
    heads = HEADS
    t = ATTN_TILE
    nb = rows // t
    has_kr = kr is not None
    has_f = cum_col is not None
    group = range(group_size)

    def body(*refs):
        it = iter(refs)
        q_ref, kv_ref = next(it), next(it)
        kr_ref = next(it) if has_kr else None
        cc_ref = next(it) if has_f else None
        cr_ref = next(it) if has_f else None
        o_ref, do_ref, lse_ref = next(it), next(it), next(it)
        dq_ref, dkv_ref = next(it), next(it)
        dkr_ref = next(it) if has_kr else None
        dck_ref = next(it) if has_f else None
        dcq_ref = next(it) if has_f else None
        dq_acc = next(it)
        lo = lax.broadcasted_iota(jnp.int32, (1, LANE), 1) < HEAD_DIM
        causal = (lax.broadcasted_iota(jnp.int32, (t, t), 1) <= lax.broadcasted_iota(jnp.int32, (t, t), 0))
        lanes = [slice(g * LANE, (g + 1) * LANE) for g in group]

        dq_acc[...] = jnp.zeros_like(dq_acc)
        if has_kr:
            _init_acc(dkr_ref)
        if has_f:
            dcq_ref[...] = jnp.zeros_like(dcq_ref)

        def kv_block(j, _):
            ks = pl.ds(pl.multiple_of(j * t, t), t)
            other = kr_ref[ks, :] if has_kr else jnp.zeros((t, LANE), BF)
            kvbs = [kv_ref[ks, lanes[g]] for g in group]
            kks = [jnp.where(lo, kvbs[g], other) for g in group]
            cks = [cr_ref[g, :, ks] if has_f else None for g in group]

            def pair(i, carry, diag):
                qs = pl.ds(pl.multiple_of(i * t, t), t)
                out = []
                for g in group:
                    dkk, dvv, dcs = carry[g]
                    qb = q_ref[qs, lanes[g]]
                    dob = do_ref[qs, lanes[g]]
                    s = lax.dot_general(qb, kks[g], (((1,), (1,)), ((), ())), preferred_element_type=F32) * scale
                    if has_f:
                        s = s + (cc_ref[g, qs, :] - cks[g])
                    if diag:
                        s = jnp.where(causal, s, NEG_INF)
                    p = jnp.exp(s - lse_ref[g, qs, :])
                    dp = lax.dot_general(dob, kvbs[g], (((1,), (1,)), ((), ())), preferred_element_type=F32)
                    delta = jnp.sum(dob.astype(F32) * o_ref[qs, lanes[g]].astype(F32), axis=1, keepdims=True)
                    ds = p * (dp - delta)
                    dsb = ds.astype(BF)
                    dvv = dvv + lax.dot_general(p.astype(BF), dob, (((0,), (0,)), ((), ())), preferred_element_type=F32)
                    dkk = dkk + lax.dot_general(dsb, qb, (((0,), (0,)), ((), ())), preferred_element_type=F32)
                    dq_acc[qs, lanes[g]] += jnp.dot(dsb, kks[g], preferred_element_type=F32)
                    if has_f:
                        dcs = dcs + jnp.sum(ds, axis=0, keepdims=True)
                        dcq_ref[g, qs, :] += jnp.sum(ds, axis=1, keepdims=True)
                    out.append((dkk, dvv, dcs))
                return tuple(out)

            init = tuple((jnp.zeros((t, LANE), F32), jnp.zeros((t, LANE), F32), jnp.zeros((1, t), F32)) for _ in group)
            carry = pair(j, init, True)
            carry = lax.fori_loop(j + 1, nb, lambda i, c: pair(i, c, False), carry)
            for g, (dkk, dvv, dcs) in enumerate(carry):
                dkk = dkk * scale
                dkv_ref[ks, lanes[g]] = jnp.where(lo, dkk, dvv).astype(BF)
                if has_kr:
                    dkr_ref[ks, :] += jnp.where(lo, 0.0, dkk)
                if has_f:
                    dck_ref[g, :, ks] = -dcs
            return 0

        lax.fori_loop(0, nb, kv_block, 0)
        dq_ref[...] = (dq_acc[...] * scale).astype(BF)

    head, kv_head, shared, col_vec, row_vec = _attn_specs(rows, kv_off, group_size, has_f)
    in_specs, args = [head, kv_head], [q, kv]
    if has_kr:
        in_specs.append(shared)
        args.append(kr)
    if has_f:
        in_specs += [col_vec, row_vec]
        args += [cum_col, cum_row]
    in_specs += [head, head, col_vec]
    args += [o, do, lse]
    out_shape = [jax.ShapeDtypeStruct((rows, heads * LANE), BF), jax.ShapeDtypeStruct((rows, heads * LANE), BF)]
    out_specs = [head, head]
    if has_kr:
        out_shape.append(jax.ShapeDtypeStruct((rows, LANE), F32))
        out_specs.append(shared)
    if has_f:
        out_shape += [jax.ShapeDtypeStruct((heads, 1, rows), F32), jax.ShapeDtypeStruct((heads, rows, 1), F32)]
        out_specs += [row_vec, col_vec]
    return pl.pallas_call(
        body, out_shape=out_shape, grid=(heads // group_size,), in_specs=in_specs, out_specs=out_specs,
        scratch_shapes=[pltpu.VMEM((rows, group_size * LANE), F32)], name=name,
        compiler_params=_params(("arbitrary",)))(*args)


def _tri_dot(tri, x):
    return jnp.dot(tri, x, preferred_element_type=F32, precision=lax.Precision.HIGHEST)


def _forget_forward(f_raw, b_f):
    rows = f_raw.shape[0]
    t = ATTN_TILE

    def body(f_ref, b_ref, cum_ref):
        tri = (lax.broadcasted_iota(jnp.int32, (t, t), 1) <= lax.broadcasted_iota(jnp.int32, (t, t), 0)).astype(F32)

        def blk(i, carry):
            sl = pl.ds(pl.multiple_of(i * t, t), t)
            xv = f_ref[sl, :] + b_ref[...]
            log_f = jnp.minimum(xv, 0.0) - jnp.log(1.0 + jnp.exp(-jnp.abs(xv)))
            cum_ref[sl, :] = _tri_dot(tri, log_f) + carry
            return carry + jnp.sum(log_f, axis=0, keepdims=True)

        lax.fori_loop(0, rows // t, blk, jnp.zeros((1, LANE), F32))

    return pl.pallas_call(body, out_shape=jax.ShapeDtypeStruct((rows, LANE), F32), name="forget_forward",
                          compiler_params=_params())(f_raw, b_f)


def _forget_backward(f_raw, b_f, dcum):
    rows = f_raw.shape[0]
    t = ATTN_TILE
    nb = rows // t

    def body(f_ref, b_ref, dc_ref, df_ref, db_ref):
        tri = (lax.broadcasted_iota(jnp.int32, (t, t), 1) >= lax.broadcasted_iota(jnp.int32, (t, t), 0)).astype(F32)

        def blk(i, carry):
            later, db = carry
            sl = pl.ds(pl.multiple_of((nb - 1 - i) * t, t), t)
            dc = dc_ref[sl, :]
            dlog = _tri_dot(tri, dc) + later
            xv = f_ref[sl, :] + b_ref[...]
            df = dlog / (1.0 + jnp.exp(xv))
            df_ref[sl, :] = df.astype(BF)
            return later + jnp.sum(dc, axis=0, keepdims=True), db + jnp.sum(df, axis=0, keepdims=True)

        _, db = lax.fori_loop(0, nb, blk, (jnp.zeros((1, LANE), F32), jnp.zeros((1, LANE), F32)))
        db_ref[...] = db

    return pl.pallas_call(body, out_shape=[jax.ShapeDtypeStruct((rows, LANE), BF), jax.ShapeDtypeStruct((1, LANE), F32)],
                          name="forget_backward", compiler_params=_params())(f_raw, b_f, dcum)


def _t5_bucket(dist):
    max_exact = REL_BUCKETS // 2
    n = jnp.maximum(dist.astype(F32), 1.0)
    large = max_exact + (jnp.log(n / max_exact) / math.log(REL_MAX_DIST / max_exact)
                         * (REL_BUCKETS - max_exact)).astype(jnp.int32)
    large = jnp.minimum(large, REL_BUCKETS - 1)
    return jnp.where(dist < max_exact, dist, large)


def _dil_buckets(dilation):
    i = jnp.arange(Q_BLOCK)[:, None]
    j = jnp.arange(Q_BLOCK)[None, :]
    cur = _t5_bucket(jnp.clip(i - j, 0) * dilation).astype(jnp.int32)
    prev = _t5_bucket(jnp.clip(Q_BLOCK + i - j, 0) * dilation).astype(jnp.int32)
    return cur, prev


def _dil_bias_tiles(tbl_ref, bc_ref, bp_ref, bias_ref, group, hp):
    ii = lax.broadcasted_iota(jnp.int32, (Q_BLOCK, Q_BLOCK), 0)
    jj = lax.broadcasted_iota(jnp.int32, (Q_BLOCK, Q_BLOCK), 1)
    for hh in range(2 * DIL_PAIRS):
        col = group * HEADS + 2 * DIL_PAIRS * hp + hh
        acc_c = jnp.zeros((Q_BLOCK, Q_BLOCK), F32)
        acc_p = jnp.zeros((Q_BLOCK, Q_BLOCK), F32)
        for b in range(REL_BUCKETS):
            val = tbl_ref[b, col]
            acc_c = jnp.where(bc_ref[...] == b, val, acc_c)
            acc_p = jnp.where(bp_ref[...] == b, val, acc_p)
        bias_ref[2 * hh] = jnp.where(jj <= ii, acc_c, NEG_INF)
        bias_ref[2 * hh + 1] = jnp.where(jj >= ii, acc_p, NEG_INF)


def _dil_view(qkv, group, dilation):
    if dilation == 1:
        return qkv
    width = 3 * HEADS * HEAD_DIM
    return qkv[:, group * width:(group + 1) * width].reshape(qkv.shape[0] // dilation, dilation * width)


def _dil_specs(group, dilation, length):
    width = DIL_PAIRS * LANE
    per = 8 // DIL_PAIRS

    def col(kind):
        if dilation == 1:
            return pl.BlockSpec((length, width), lambda hp, r: (0, (group * 3 + kind) * per + hp))
        return pl.BlockSpec((length, width), lambda hp, r: (0, (r * 3 + kind) * per + hp))

    out = pl.BlockSpec((length, width), lambda hp, r: (0, r * per + hp))
    tile = pl.BlockSpec((Q_BLOCK, Q_BLOCK), lambda hp, r: (0, 0))
    table = pl.BlockSpec(memory_space=pltpu.SMEM)
    return col, out, tile, table


def _dil_forward(view, group, dilation, table, buckets):
    length = view.shape[0]
    rows = length * dilation
    nb = length // Q_BLOCK
    scale = HEAD_DIM ** -0.5
    qb = Q_BLOCK

    def body(tbl_ref, bc_ref, bp_ref, q_ref, k_ref, v_ref, o_ref, lse_ref, bias_ref):
        hp = pl.program_id(0)

        @pl.when(pl.program_id(1) == 0)
        def _():
            _dil_bias_tiles(tbl_ref, bc_ref, bp_ref, bias_ref, group, hp)

        lo = lax.broadcasted_iota(jnp.int32, (1, LANE), 1) < HEAD_DIM
        nt = (((1,), (1,)), ((), ()))

        def blk(n, first):
            cur = pl.ds(0, qb) if first else pl.ds(pl.multiple_of(n * qb, qb), qb)
            prev = None if first else pl.ds(pl.multiple_of((n - 1) * qb, qb), qb)
            for pair in range(DIL_PAIRS):
                lanes = slice(pair * LANE, (pair + 1) * LANE)
                qn = q_ref[cur, lanes] * scale
                kc, vc = k_ref[cur, lanes], v_ref[cur, lanes]
                if not first:
                    kp, vp = k_ref[prev, lanes], v_ref[prev, lanes]
                outs, lses = [], []
                for hh in range(2):
                    bias = 4 * pair + 2 * hh
                    qm = jnp.where(lo if hh == 0 else ~lo, qn, jnp.zeros_like(qn))
                    s_c = lax.dot_general(qm, kc, nt, preferred_element_type=F32) + bias_ref[bias]
                    m = jnp.max(s_c, axis=1, keepdims=True)
                    if not first:
                        s_p = lax.dot_general(qm, kp, nt, preferred_element_type=F32) + bias_ref[bias + 1]
                        m = jnp.maximum(m, jnp.max(s_p, axis=1, keepdims=True))
                    e_c = jnp.exp(s_c - m)
                    l = jnp.sum(e_c, axis=1, keepdims=True)
                    acc = jnp.dot(e_c.astype(BF), vc, preferred_element_type=F32)
                    if not first:
                        e_p = jnp.exp(s_p - m)
                        l = l + jnp.sum(e_p, axis=1, keepdims=True)
                        acc = acc + jnp.dot(e_p.astype(BF), vp, preferred_element_type=F32)
                    outs.append(acc * (1.0 / l))
                    lses.append(m + jnp.log(l))
                o_ref[cur, lanes] = jnp.where(lo, outs[0], outs[1])
                lse_ref[cur, lanes] = jnp.where(lo, lses[0], lses[1])
            return 0

        blk(0, True)
        if nb > 1:
            lax.fori_loop(1, nb, lambda n, _: blk(n, False), 0)

    col, out, tile, tbl = _dil_specs(group, dilation, length)
    bc, bp = buckets
    o, lse = pl.pallas_call(
        body, out_shape=[jax.ShapeDtypeStruct((length, dilation * D_MODEL), F32)] * 2,
        grid=(8 // DIL_PAIRS, dilation), in_specs=[tbl, tile, tile, col(0), col(1), col(2)], out_specs=[out, out],
        scratch_shapes=[pltpu.VMEM((4 * DIL_PAIRS, qb, qb), F32)], name=f"dilated_forward_{dilation}",
        compiler_params=_params(("arbitrary", "arbitrary")))(
            table, bc, bp, view, view, view)
    return o.reshape(rows, D_MODEL), lse.reshape(rows, D_MODEL)


def _dil_backward(view, group, dilation, table, buckets, do_g, lse, dlt):
    length = view.shape[0]
    rows = length * dilation
    nb = length // Q_BLOCK
    scale = HEAD_DIM ** -0.5
    qb = Q_BLOCK

    def body(tbl_ref, bc_ref, bp_ref, q_ref, k_ref, v_ref, do_ref, lse_ref, dlt_ref,
             dq_ref, dk_ref, dv_ref, db_ref, bias_ref, dk_acc, dv_acc):
        hp = pl.program_id(0)

        @pl.when(pl.program_id(1) == 0)
        def _():
            _dil_bias_tiles(tbl_ref, bc_ref, bp_ref, bias_ref, group, hp)
            db_ref[...] = jnp.zeros_like(db_ref)

        dk_acc[...] = jnp.zeros_like(dk_acc)
        dv_acc[...] = jnp.zeros_like(dv_acc)
        lo = lax.broadcasted_iota(jnp.int32, (1, LANE), 1) < HEAD_DIM
        tn = (((0,), (0,)), ((), ()))
        nt = (((1,), (1,)), ((), ()))

        def blk(n, first):
            cur = pl.ds(0, qb) if first else pl.ds(pl.multiple_of(n * qb, qb), qb)
            prev = None if first else pl.ds(pl.multiple_of((n - 1) * qb, qb), qb)
            for pair in range(DIL_PAIRS):
                lanes = slice(pair * LANE, (pair + 1) * LANE)
                qn = q_ref[cur, lanes] * scale
                don = do_ref[cur, lanes]
                kc, vc = k_ref[cur, lanes], v_ref[cur, lanes]
                if not first:
                    kp, vp = k_ref[prev, lanes], v_ref[prev, lanes]
                lse_n = lse_ref[cur, lanes]
                dlt_n = dlt_ref[cur, lanes]
                dqs = []
                dkc = jnp.zeros((qb, LANE), F32)
                dkp = jnp.zeros((qb, LANE), F32)
                dvc = jnp.zeros((qb, LANE), F32)
                dvp = jnp.zeros((qb, LANE), F32)
                for hh in range(2):
                    bias = 4 * pair + 2 * hh
                    mask = lo if hh == 0 else ~lo
                    qm = jnp.where(mask, qn, jnp.zeros_like(qn))
                    dom = jnp.where(mask, don, jnp.zeros_like(don))
                    lse_h = jnp.max(jnp.where(mask, lse_n, -3e38), axis=1, keepdims=True)
                    dlt_h = jnp.max(jnp.where(mask, dlt_n, -3e38), axis=1, keepdims=True)
                    p_c = jnp.exp(lax.dot_general(qm, kc, nt, preferred_element_type=F32) + bias_ref[bias] - lse_h)
                    ds_c = p_c * (lax.dot_general(dom, vc, nt, preferred_element_type=F32) - dlt_h)
                    db_ref[pair, 2 * hh] += ds_c
                    dsc_b = ds_c.astype(BF)
                    dq = jnp.dot(dsc_b, kc, preferred_element_type=F32)
                    dkc = dkc + lax.dot_general(dsc_b, qm, tn, preferred_element_type=F32)
                    dvc = dvc + lax.dot_general(p_c.astype(BF), dom, tn, preferred_element_type=F32)
                    if not first:
                        p_p = jnp.exp(lax.dot_general(qm, kp, nt, preferred_element_type=F32) + bias_ref[bias + 1] - lse_h)
                        ds_p = p_p * (lax.dot_general(dom, vp, nt, preferred_element_type=F32) - dlt_h)
                        db_ref[pair, 2 * hh + 1] += ds_p
                        dsp_b = ds_p.astype(BF)
                        dq = dq + jnp.dot(dsp_b, kp, preferred_element_type=F32)
                        dkp = dkp + lax.dot_general(dsp_b, qm, tn, preferred_element_type=F32)
                        dvp = dvp + lax.dot_general(p_p.astype(BF), dom, tn, preferred_element_type=F32)
                    dqs.append(dq)
                dq_ref[cur, lanes] = (jnp.where(lo, dqs[0], dqs[1]) * scale).astype(BF)
                dk_acc[cur, lanes] += dkc
                dv_acc[cur, lanes] += dvc
                if not first:
                    dk_acc[prev, lanes] += dkp
                    dv_acc[prev, lanes] += dvp
            return 0

        blk(0, True)
        if nb > 1:
            lax.fori_loop(1, nb, lambda n, _: blk(n, False), 0)
        dk_ref[...] = dk_acc[...].astype(BF)
        dv_ref[...] = dv_acc[...].astype(BF)

    col, out, tile, tbl = _dil_specs(group, dilation, length)
    bc, bp = buckets
    wide = (length, dilation * D_MODEL)
    dq, dk, dv, db = pl.pallas_call(
        body, out_shape=[jax.ShapeDtypeStruct(wide, BF)] * 3 + [jax.ShapeDtypeStruct((8, 4, qb, qb), F32)],
        grid=(8 // DIL_PAIRS, dilation), in_specs=[tbl, tile, tile, col(0), col(1), col(2), out, out, out],
        out_specs=[out, out, out, pl.BlockSpec((DIL_PAIRS, 4, qb, qb), lambda hp, r: (hp, 0, 0, 0))],
        scratch_shapes=[pltpu.VMEM((4 * DIL_PAIRS, qb, qb), F32), pltpu.VMEM((length, DIL_PAIRS * LANE), F32),
                        pltpu.VMEM((length, DIL_PAIRS * LANE), F32)],
        name=f"dilated_backward_{dilation}", compiler_params=_params(("arbitrary", "arbitrary")))(
            table, bc, bp, view, view, view,
            do_g.reshape(wide), lse.reshape(wide), dlt.reshape(wide))
    return dq.reshape(rows, D_MODEL), dk.reshape(rows, D_MODEL), dv.reshape(rows, D_MODEL), db


def _head_sums(x, lo):
    s0 = jnp.sum(jnp.where(lo, x, 0.0), axis=1, keepdims=True)
    s1 = jnp.sum(jnp.where(lo, 0.0, x), axis=1, keepdims=True)
    return jnp.where(lo, s0, s1)


def _dil_merge_forward(outs, lses):
    rows = outs[0].shape[0]

    def body(o0, o1, o2, l0, l1, l2, o_ref):
        ls = [l0[...], l1[...], l2[...]]
        m = jnp.maximum(jnp.maximum(ls[0], ls[1]), ls[2])
        es = [jnp.exp(v - m) for v in ls]
        tot = es[0] + es[1] + es[2]
        o_ref[...] = ((es[0] * o0[...] + es[1] * o1[...] + es[2] * o2[...]) / tot).astype(BF)

    blk = pl.BlockSpec((ROW_TILE, LANE), lambda i, j: (i, j))
    return pl.pallas_call(body, out_shape=jax.ShapeDtypeStruct((rows, D_MODEL), BF), grid=(rows // ROW_TILE, 8),
                          in_specs=[blk] * 6, out_specs=blk, name="dilated_merge_forward",
                          compiler_params=_params(("parallel", "parallel")))(*outs, *lses)


def _dil_merge_backward(outs, lses, do):
    rows = outs[0].shape[0]

    def body(o0, o1, o2, l0, l1, l2, do_ref, d0, d1, d2, t0, t1, t2):
        lo = lax.broadcasted_iota(jnp.int32, (1, LANE), 1) < HEAD_DIM
        ls = [l0[...], l1[...], l2[...]]
        os_ = [o0[...], o1[...], o2[...]]
        m = jnp.maximum(jnp.maximum(ls[0], ls[1]), ls[2])
        es = [jnp.exp(v - m) for v in ls]
        inv = 1.0 / (es[0] + es[1] + es[2])
        alphas = [e * inv for e in es]
        dov = do_ref[...]
        merged = alphas[0] * os_[0] + alphas[1] * os_[1] + alphas[2] * os_[2]
        dot = _head_sums(dov * merged, lo)
        for a, d_ref, t_ref in zip(alphas, (d0, d1, d2), (t0, t1, t2)):
            d_ref[...] = (a * dov).astype(BF)
            t_ref[...] = a * dot

    blk = pl.BlockSpec((ROW_TILE, LANE), lambda i, j: (i, j))
    res = pl.pallas_call(
        body, out_shape=[jax.ShapeDtypeStruct((rows, D_MODEL), BF)] * 3 + [jax.ShapeDtypeStruct((rows, D_MODEL), F32)] * 3,
        grid=(rows // ROW_TILE, 8), in_specs=[blk] * 7, out_specs=[blk] * 6, name="dilated_merge_backward",
        compiler_params=_params(("parallel", "parallel")))(*outs, *lses, do)
    return res[:3], res[3:]


def _rel_bias_grad(dbs, buckets):
    def body(db_ref, bc_ref, bp_ref, o_ref):
        g = pl.program_id(0)
        hp = pl.program_id(1)

        @pl.when((g == 0) & (hp == 0))
        def _():
            o_ref[...] = jnp.zeros_like(o_ref)

        rr = lax.broadcasted_iota(jnp.int32, (REL_BUCKETS, LANE), 0)
        cc = lax.broadcasted_iota(jnp.int32, (REL_BUCKETS, LANE), 1)
        bc = bc_ref[0]
        bp = bp_ref[0]
        acc = jnp.zeros((REL_BUCKETS, LANE), F32)
        for hh in range(2):
            col = g * HEADS + 2 * hp + hh
            d_c = db_ref[0, 0, 2 * hh]
            d_p = db_ref[0, 0, 2 * hh + 1]
            for b in range(REL_BUCKETS):
                val = (jnp.sum(jnp.where(bc == b, d_c, 0.0), keepdims=True)
                       + jnp.sum(jnp.where(bp == b, d_p, 0.0), keepdims=True))
                acc = jnp.where((rr == b) & (cc == col), val, acc)
        o_ref[...] += acc

    db_all = jnp.stack(dbs)
    bc_all = jnp.stack([b[0] for b in buckets])
    bp_all = jnp.stack([b[1] for b in buckets])
    tile = pl.BlockSpec((1, Q_BLOCK, Q_BLOCK), lambda g, hp: (g, 0, 0))
    return pl.pallas_call(
        body, out_shape=jax.ShapeDtypeStruct((REL_BUCKETS, LANE), F32), grid=(3, 8),
        in_specs=[pl.BlockSpec((1, 1, 4, Q_BLOCK, Q_BLOCK), lambda g, hp: (g, hp, 0, 0, 0)), tile, tile],
        out_specs=pl.BlockSpec((REL_BUCKETS, LANE), lambda g, hp: (0, 0)), name="rel_bias_grad",
        compiler_params=_params(("arbitrary", "arbitrary")))(db_all, bc_all, bp_all)


def _mla_forward(hn, w, tables):
    a = _matmul(hn, w["w_a"], name="mla_a")
    cq, ckv, kr = _mla_mid_forward(a, w["q_norm"], w["kv_norm"], tables)
    q_raw = _matmul(cq, w["w_uq"], name="mla_uq")
    q = _rope_heads(q_raw, tables, False, "rope_forward")
    kv = _matmul(ckv, w["w_ukv"], b_chunks=True, out_dtype=BF, name="mla_ukv")
    scale = (HEAD_DIM + MLA_ROPE) ** -0.5
    o, lse = _attn_forward(q, kv, 0, kr, None, None, scale, MLA_GROUP, "mla_attention_forward")
    y = _matmul(o, w["w_o"], name="attn_out")
    return y, dict(hn=hn, a=a, cq=cq, ckv=ckv, kr=kr, q=q, kv=kv, o=o, lse=lse)


def _mla_backward(dy, w, s, tables):
    scale = (HEAD_DIM + MLA_ROPE) ** -0.5
    g = {}
    g["w_o"] = _matmul(s["o"], dy, ta=True, out_dtype=BF, name="attn_out_dw")
    do = _matmul(dy, w["w_o"], tb=True, out_dtype=BF, name="attn_out_dx")
    dq, dkv, dkr = _attn_backward(s["q"], s["kv"], 0, s["kr"], None, None, s["o"], do, s["lse"], scale,
                                  MLA_GROUP, "mla_attention_backward")
    dq_raw = _rope_heads(dq, tables, True, "rope_backward")
    g["w_uq"] = _matmul(s["cq"], dq_raw, ta=True, out_dtype=BF, name="mla_uq_dw")
    dcq = _matmul(dq_raw, w["w_uq"], tb=True, name="mla_uq_dx")
    g["w_ukv"] = _matmul(s["ckv"], dkv, ta=True, out_chunks=True, out_dtype=BF, name="mla_ukv_dw")
    dckv = _matmul(dkv, w["w_ukv"], tb=True, b_chunks=True, name="mla_ukv_dx")
    da, g["q_norm"], g["kv_norm"] = _mla_mid_backward(s["a"], w["q_norm"], w["kv_norm"], tables, dcq, dckv, dkr)
    g["w_a"] = _matmul(s["hn"], da, ta=True, out_dtype=BF, name="mla_a_dw")
    dhn = _matmul(da, w["w_a"], tb=True, name="mla_a_dx")
    return dhn, g


def _fox_forward(hn, w):
    qkv = _matmul(hn, w["w_qkv"], out_dtype=BF, name="fox_qkv")
    f_raw = _matmul(hn, w["w_f"], name="fox_f")
    cum = _forget_forward(f_raw, w["b_f"])
    cum_heads = cum[:, :HEADS].T
    cum_col, cum_row = cum_heads[:, :, None], cum_heads[:, None, :]
    o, lse = _attn_forward(qkv, qkv, HEADS, None, cum_col, cum_row, HEAD_DIM ** -0.5, FOX_GROUP,
                           "fox_attention_forward")
    y = _matmul(o, w["w_o"], name="attn_out")
    return y, dict(hn=hn, qkv=qkv, f_raw=f_raw, cum_col=cum_col, cum_row=cum_row, o=o, lse=lse)


def _fox_backward(dy, w, s):
    g = {}
    g["w_o"] = _matmul(s["o"], dy, ta=True, out_dtype=BF, name="attn_out_dw")
    do = _matmul(dy, w["w_o"], tb=True, out_dtype=BF, name="attn_out_dx")
    dq, dkv, dck, dcq = _attn_backward(s["qkv"], s["qkv"], HEADS, None, s["cum_col"], s["cum_row"], s["o"], do,
                                       s["lse"], HEAD_DIM ** -0.5, FOX_GROUP, "fox_attention_backward")
    dcum = jnp.pad((dck[:, 0, :] + dcq[:, :, 0]).T, ((0, 0), (0, LANE - HEADS)))
    df, g["b_f"] = _forget_backward(s["f_raw"], w["b_f"], dcum)
    dqkv = jnp.concatenate([dq, dkv], axis=1)
    g["w_qkv"] = _matmul(s["hn"], dqkv, ta=True, out_dtype=BF, name="fox_qkv_dw")
    g["w_f"] = _matmul(s["hn"], df, ta=True, out_dtype=BF, name="fox_f_dw")
    dhn = _matmul(dqkv, w["w_qkv"], tb=True, name="fox_qkv_dx")
    dhn = _matmul(df, w["w_f"], tb=True, add=dhn, name="fox_f_dx")
    return dhn, g


def _dil_mixer_forward(hn, w, buckets):
    qkv = _matmul(hn, w["w_qkv"], b_chunks=True, out_dtype=BF, name="dil_qkv")
    views = [_dil_view(qkv, grp, dilation) for grp, (_, dilation) in enumerate(DIL_PATTERNS)]
    outs, lses = [], []
    for grp, (_, dilation) in enumerate(DIL_PATTERNS):
        o_g, lse_g = _dil_forward(views[grp], grp, dilation, w["rel_bias"], buckets[grp])
        outs.append(o_g)
        lses.append(lse_g)
    o = _dil_merge_forward(outs, lses)
    y = _matmul(o, w["w_o"], name="dil_out")
    return y, dict(hn=hn, views=views, outs=outs, lses=lses, o=o)


def _dil_mixer_backward(dy, w, s, buckets):
    g = {}
    g["w_o"] = _matmul(s["o"], dy, ta=True, out_dtype=BF, name="dil_out_dw")
    do = _matmul(dy, w["w_o"], tb=True, name="dil_out_dx")
    do_gs, dlts = _dil_merge_backward(s["outs"], s["lses"], do)
    parts, dbs = [], []
    for grp, (_, dilation) in enumerate(DIL_PATTERNS):
        dq, dk, dv, db = _dil_backward(s["views"][grp], grp, dilation, w["rel_bias"], buckets[grp], do_gs[grp],
                                       s["lses"][grp], dlts[grp])
        parts += [dq, dk, dv]
        dbs.append(db)
    dqkv = jnp.concatenate(parts, axis=1)
    g["rel_bias"] = _rel_bias_grad(dbs, buckets)
    g["w_qkv"] = _matmul(s["hn"], dqkv, ta=True, out_chunks=True, out_dtype=BF, name="dil_qkv_dw")
    dhn = _matmul(dqkv, w["w_qkv"], tb=True, b_chunks=True, name="dil_qkv_dx")
    return dhn, g


def _mixer_weights(i, lw, small):
    mixer, j = i % N_MIXERS, i // N_MIXERS
    if mixer == 0:
        return dict(lw["mixer"], q_norm=small["mla_q_norm"][j][None, :], kv_norm=small["mla_kv_norm"][j][None, :])
    if mixer == 1:
        return dict(lw["mixer"], rel_bias=small["rel_bias"])
    return dict(lw["mixer"], b_f=jnp.pad(small["fox_b_f"][j][None, :], ((0, 0), (0, LANE - HEADS))))


MIXER_PART, COMMON_PART = 0, 1


def _run_layers(x, p, positions, target, get_part, get_small, put_part):
    tables = _rope_tables(positions)
    buckets = [_dil_buckets(d) for _, d in DIL_PATTERNS]
    layers, saved = [], []
    h = x
    first = get_part(0, MIXER_PART, positions)
    small = get_small()

    def gain(i, k):
        return small["norm_g"][i, k][None, :]

    hn = _prenorm(h, gain(0, 0))
    sq = dh = None
    for i in range(DEPTH):
        mixer = i % N_MIXERS
        lw = dict(mixer=first if i == 0 else get_part(i, MIXER_PART, h))
        mw = _mixer_weights(i, lw, small)
        if mixer == 0:
            y, ms = _mla_forward(hn, mw, tables)
        elif mixer == 1:
            y, ms = _dil_mixer_forward(hn, mw, buckets)
        else:
            y, ms = _fox_forward(hn, mw)
        lw.update(get_part(i, COMMON_PART, y))
        layers.append(lw)
        h1, hn2 = _post_residual(h, y, gain(i, 1), gain(i, 2))
        gu = _matmul(hn2, lw["ffn_w_in"], b_chunks=True, out_dtype=BF, name="ffn_in")
        act = _swiglu_forward(gu)
        f = _matmul(act, lw["ffn_w_out"], name="ffn_out")
        h2, h2b = _post_residual(h1, f, gain(i, 3), None)
        pp = _matmul(p[i], lw["ple_w_proj"], b_chunks=True, name="ple_proj")
        z = _matmul(h2b, lw["ple_w_gate"], name="ple_gate")
        saved.append(dict(h=h, y=y, ms=ms, h1=h1, hn2=hn2, gu=gu, act=act, f=f, h2b=h2b, pp=pp, z=z))
        if i + 1 < DEPTH:
            h, hn = _ple_forward(h2, pp, z, gain(i + 1, 0))
        else:
            dh, sq = _ple_loss(h2, pp, z, target)

    norm_rows = [[None] * 4 for _ in range(DEPTH)]
    sg = dict(mla_q_norm={}, mla_kv_norm={}, rel_bias=None, fox_b_f={})
    for i in reversed(range(DEPTH)):
        s, lw = saved[i], layers[i]
        mixer, j = i % N_MIXERS, i // N_MIXERS
        mw = _mixer_weights(i, lw, small)
        lg = {}
        dpp, dz = _ple_backward(dh, s["pp"], s["z"])
        lg["ple_w_proj"] = _matmul(p[i], dpp, ta=True, out_chunks=True, out_dtype=BF, name="ple_proj_dw")
        lg["ple_w_gate"] = _matmul(s["h2b"], dz, ta=True, out_dtype=BF, name="ple_gate_dw")
        dh2 = _matmul(dz, lw["ple_w_gate"], tb=True, add=dh, name="ple_gate_dx")
        df, norm_rows[i][3] = _rms_backward(s["f"], gain(i, 3), dh2, None, BF)
        lg["ffn_w_out"] = _matmul(s["act"], df, ta=True, out_dtype=BF, name="ffn_out_dw")
        dact = _matmul(df, lw["ffn_w_out"], tb=True, out_dtype=BF, name="ffn_out_dx")
        dgu = _swiglu_backward(s["gu"], dact)
        lg["ffn_w_in"] = _matmul(s["hn2"], dgu, ta=True, out_chunks=True, out_dtype=BF, name="ffn_in_dw")
        token = put_part(i, COMMON_PART, lg)
        dhn2 = _matmul(dgu, lw["ffn_w_in"], tb=True, b_chunks=True, name="ffn_in_dx")
        dh1, norm_rows[i][2] = _rms_backward(s["h1"], gain(i, 2), dhn2, dh2, F32)
        dy, norm_rows[i][1] = _rms_backward(s["y"], gain(i, 1) + token[0:1, 0:1], dh1, None, BF)
        if mixer == 0:
            dhn, mg = _mla_backward(dy, mw, s["ms"], tables)
            sg["mla_q_norm"][j] = mg.pop("q_norm")
            sg["mla_kv_norm"][j] = mg.pop("kv_norm")
        elif mixer == 1:
            dhn, mg = _dil_mixer_backward(dy, mw, s["ms"], buckets)
            rel = mg.pop("rel_bias")[:, :3 * HEADS]
            sg["rel_bias"] = rel if sg["rel_bias"] is None else sg["rel_bias"] + rel
        else:
            dhn, mg = _fox_backward(dy, mw, s["ms"])
            sg["fox_b_f"][j] = mg.pop("b_f")[:, :HEADS]
        token = put_part(i, MIXER_PART, mg)
        dh, norm_rows[i][0] = _rms_backward(s["h"], gain(i, 0) + token[0:1, 0:1], dhn, dh1, F32)
    small_grads = dict(norm_g=jnp.stack([jnp.concatenate(row, axis=0) for row in norm_rows]),
                       rel_bias=sg["rel_bias"])
    for k in ("mla_q_norm", "mla_kv_norm", "fox_b_f"):
        small_grads[k] = jnp.concatenate([sg[k][j] for j in sorted(sg[k])], axis=0)
    return sq, dh, small_grads


COL_SHARDED = ("ffn_w_in", "ple_w_proj", "mla_w_uq", "mla_w_ukv", "dil_w_qkv", "fox_w_qkvf")
ROW_SHARDED = ("ffn_w_out", "ple_w_gate", "mla_w_a", "mla_w_o", "dil_w_o", "fox_w_o")
BIG = ("ffn_w_in", "ffn_w_out", "ple_w_proj", "ple_w_gate", "mla_w_a", "mla_w_uq", "mla_w_ukv", "mla_w_o",
       "dil_w_qkv", "dil_w_o", "fox_w_qkvf", "fox_w_o")
SMALL_SHARDED = ("norm_g", "mla_q_norm", "mla_kv_norm")
SMALL_REPLICATED = ("rel_bias", "fox_b_f")
WEIGHTS = ("norm_g", "ffn_w_in", "ffn_w_out", "ple_w_proj", "ple_w_gate", "rel_bias", "mla_w_a", "mla_q_norm",
           "mla_kv_norm", "mla_w_uq", "mla_w_ukv", "mla_w_o", "dil_w_qkv", "dil_w_o", "fox_w_qkvf", "fox_b_f", "fox_w_o")


LAYER_COMMON = ("ffn_w_in", "ffn_w_out", "ple_w_proj", "ple_w_gate")
MIXER_WEIGHTS = (("mla_w_a", "mla_w_uq", "mla_w_ukv", "mla_w_o"), ("dil_w_qkv", "dil_w_o"), ("fox_w_qkvf", "fox_w_o"))


def _part_names(i, part):
    return MIXER_WEIGHTS[i % N_MIXERS] if part == MIXER_PART else LAYER_COMMON


def _layer_slot(name, i):
    return i if name in LAYER_COMMON else i // N_MIXERS


def _merge_rows(chunks):
    n, r, c = chunks.shape
    return chunks.reshape(n * r, c)


def _merge_cols(chunks):
    n, r, c = chunks.shape
    return chunks.transpose(1, 0, 2).reshape(r, n * c)


def _pad_heads_out(wo):
    w3 = wo.reshape(HEADS, HEAD_DIM, D_MODEL)
    return jnp.pad(w3, ((0, 0), (HEAD_DIM, 0), (0, 0))).reshape(HEADS * LANE, D_MODEL)


def _part_to_compute(i, part, ch):
    if part == COMMON_PART:
        return dict(ffn_w_in=ch["ffn_w_in"], ffn_w_out=_merge_rows(ch["ffn_w_out"]), ple_w_proj=ch["ple_w_proj"],
                    ple_w_gate=_merge_rows(ch["ple_w_gate"]))
    lw = {}
    mixer = i % N_MIXERS
    if mixer == 0:
        wa = _merge_rows(ch["mla_w_a"])
        rank = MLA_Q_RANK + MLA_KV_RANK
        wa_p = jnp.concatenate([wa[:, :rank], jnp.zeros((wa.shape[0], 64), wa.dtype), wa[:, rank:],
                                jnp.zeros((wa.shape[0], 32), wa.dtype)], axis=1)
        wuq = _merge_cols(ch["mla_w_uq"]).reshape(MLA_Q_RANK, HEADS, HEAD_DIM + MLA_ROPE)
        wuq_p = jnp.pad(wuq, ((0, 0), (0, 0), (0, LANE - HEAD_DIM - MLA_ROPE))).reshape(MLA_Q_RANK, HEADS * LANE)
        lw["mixer"] = dict(w_a=wa_p, w_uq=wuq_p, w_ukv=ch["mla_w_ukv"], w_o=_pad_heads_out(_merge_rows(ch["mla_w_o"])))
    elif mixer == 1:
        lw["mixer"] = dict(w_qkv=ch["dil_w_qkv"], w_o=_merge_rows(ch["dil_w_o"]))
    else:
        wf = _merge_cols(ch["fox_w_qkvf"])
        inner = HEADS * HEAD_DIM
        q3 = wf[:, :inner].reshape(D_MODEL, HEADS, HEAD_DIM)
        k3 = wf[:, inner:2 * inner].reshape(D_MODEL, HEADS, HEAD_DIM)
        v3 = wf[:, 2 * inner:3 * inner].reshape(D_MODEL, HEADS, HEAD_DIM)
        q_p = jnp.pad(q3, ((0, 0), (0, 0), (0, HEAD_DIM))).reshape(D_MODEL, HEADS * LANE)
        kv_p = jnp.concatenate([k3, v3], axis=2).reshape(D_MODEL, HEADS * LANE)
        f_p = jnp.pad(wf[:, 3 * inner:], ((0, 0), (0, LANE - HEADS)))
        lw["mixer"] = dict(w_qkv=jnp.concatenate([q_p, kv_p], axis=1), w_f=f_p,
                           w_o=_pad_heads_out(_merge_rows(ch["fox_w_o"])))
    return lw["mixer"]


def _part_contributions(i, part, lg, chunk_shapes):
    spec = {k: jax.ShapeDtypeStruct(s, BF) for k, s in chunk_shapes.items()}
    (contrib,) = jax.linear_transpose(functools.partial(_part_to_compute, i, part), spec)(lg)
    return contrib


def _chip_peers():
    x, y, c = lax.axis_index("x"), lax.axis_index("y"), lax.axis_index("c")
    peers = [(1 - x, y), (x, 1 - y), (1 - x, 1 - y)]
    return x, y, c, peers


SEM_SPEC = pl.BlockSpec(memory_space=pltpu.SEMAPHORE)
ANY_SPEC = pl.BlockSpec(memory_space=pl.ANY)
SPLIT_EFFECT = pltpu.SideEffectType.DATAFLOW_SIDE_EFFECTING


def _own_slot(shard):
    me = 2 * lax.axis_index("x") + lax.axis_index("y")
    return lax.dynamic_update_index_in_dim(lax.empty((N_CHIPS,) + shard.shape, shard.dtype), shard[None], me, 0)


def _spread_copy(src, land, k, peer, c, send_sems, recv_sems, index, src_slot, slot):
    px, py = peer
    return pltpu.make_async_remote_copy(
        src_ref=src.at[src_slot], dst_ref=land.at[slot],
        send_sem=send_sems.at[3 * index + k], recv_sem=recv_sems.at[3 * index + k],
        device_id=(px, py, c), device_id_type=MESH)


def _spread_start(bufs, srcs, after, name):
    n = len(bufs)
    exchange = srcs is not None
    arrays = (list(srcs) if exchange else []) + list(bufs)
    na = len(arrays)

    def body(*refs):
        src, land = refs[:n], refs[na - n:na]
        send_sems, recv_sems = refs[na + 1], refs[na + 2]
        token = refs[-1]
        x, y, c, peers = _chip_peers()
        me = 2 * x + y
        for w in range(n):
            for k, peer in enumerate(peers):
                src_slot = 2 * peer[0] + peer[1] if exchange else me
                _spread_copy(src[w], land[w], k, peer, c, send_sems, recv_sems, w, src_slot, me).start()
        token[...] = jnp.zeros_like(token)

    hbm = [pltpu.with_memory_space_constraint(a, pltpu.HBM) for a in arrays]
    out = pl.pallas_call(
        body, name=name,
        out_shape=(pltpu.SemaphoreType.DMA((3 * n,)), pltpu.SemaphoreType.DMA((3 * n,)),
                   *[pltpu.HBM(a.shape, a.dtype) for a in hbm], jax.ShapeDtypeStruct((8, LANE), F32)),
        in_specs=[HBM_SPEC] * na + [ANY_SPEC],
        out_specs=(SEM_SPEC, SEM_SPEC, *[HBM_SPEC] * na, pl.BlockSpec(memory_space=pltpu.VMEM)),
        input_output_aliases={w: 2 + w for w in range(na)},
        compiler_params=pltpu.CompilerParams(has_side_effects=SPLIT_EFFECT))(*hbm, after)
    return dict(send=out[0], recv=out[1], arrays=out[2:2 + na], n=n, token=out[-1], exchange=exchange)


def _spread_wait(handle, after, name):
    n, exchange = handle["n"], handle["exchange"]
    arrays = list(handle["arrays"])
    na = len(arrays)

    def body(*refs):
        src, land = refs[:n], refs[na - n:na]
        send_sems, recv_sems = refs[na], refs[na + 1]
        x, y, c, peers = _chip_peers()
        me = 2 * x + y
        for w in range(n):
            for k, peer in enumerate(peers):
                there = 2 * peer[0] + peer[1]
                cp = _spread_copy(src[w], land[w], k, peer, c, send_sems, recv_sems, w, there if exchange else me, there)
                cp.wait_send()
                cp.wait_recv()

    out = pl.pallas_call(
        body, name=name, out_shape=tuple(pltpu.HBM(a.shape, a.dtype) for a in arrays),
        in_specs=[HBM_SPEC] * na + [SEM_SPEC, SEM_SPEC, ANY_SPEC], out_specs=tuple([HBM_SPEC] * na),
        input_output_aliases={w: w for w in range(na)},
        compiler_params=pltpu.CompilerParams(has_side_effects=SPLIT_EFFECT))(*arrays, handle["send"], handle["recv"], after)
    return (list(out[n:]), list(out[:n])) if exchange else list(out)


def _exchange_sibling(arrays, name):
    n = len(arrays)

    def body(*refs):
        ins, outs = refs[:n], refs[n:2 * n]
        send_sems, recv_sems = refs[2 * n:]
        x, y, c = lax.axis_index("x"), lax.axis_index("y"), lax.axis_index("c")
        copies = [pltpu.make_async_remote_copy(src_ref=ins[w], dst_ref=outs[w], send_sem=send_sems.at[w],
                                               recv_sem=recv_sems.at[w], device_id=(x, y, 1 - c), device_id_type=MESH)
                  for w in range(n)]
        for cp in copies:
            cp.start()
        for cp in copies:
            cp.wait_recv()
        for cp in copies:
            cp.wait_send()

    return pl.pallas_call(
        body, out_shape=[jax.ShapeDtypeStruct(s.shape, s.dtype) for s in arrays],
        in_specs=[HBM_SPEC] * n, out_specs=[HBM_SPEC] * n,
        scratch_shapes=[pltpu.SemaphoreType.DMA((n,)), pltpu.SemaphoreType.DMA((n,))], name=name)(*arrays)


def _all_reduce_small(v):
    rows = v.shape[0]

    def body(v_ref, sum_ref, slots, send_sems, recv_sems):
        x, y, c = lax.axis_index("x"), lax.axis_index("y"), lax.axis_index("c")
        me = 4 * x + 2 * y + c
        slots[me] = v_ref[...]
        sends = []
        for k in range(1, N_DEV):
            bx, by, bc = (k >> 2) & 1, (k >> 1) & 1, k & 1
            peer = (x ^ bx, y ^ by, c ^ bc)
            rc = pltpu.make_async_remote_copy(src_ref=v_ref, dst_ref=slots.at[me], send_sem=send_sems.at[k],
                                              recv_sem=recv_sems.at[k], device_id=peer, device_id_type=MESH)
            rc.start()
            sends.append(rc)
        for k in range(1, N_DEV):
            bx, by, bc = (k >> 2) & 1, (k >> 1) & 1, k & 1
            src = 4 * (x ^ bx) + 2 * (y ^ by) + (c ^ bc)
            pltpu.make_async_remote_copy(src_ref=v_ref, dst_ref=slots.at[src], send_sem=send_sems.at[k],
                                         recv_sem=recv_sems.at[k], device_id=(x ^ bx, y ^ by, c ^ bc),
                                         device_id_type=MESH).wait_recv()
        for rc in sends:
            rc.wait_send()
        total = slots[0]
        for k in range(1, N_DEV):
            total = total + slots[k]
        sum_ref[...] = total

    vm = pl.BlockSpec(memory_space=pltpu.VMEM)
    return pl.pallas_call(
        body, out_shape=jax.ShapeDtypeStruct((rows, LANE), F32), in_specs=[vm], out_specs=vm,
        scratch_shapes=[pltpu.VMEM((N_DEV, rows, LANE), F32), pltpu.SemaphoreType.DMA((N_DEV,)),
                        pltpu.SemaphoreType.DMA((N_DEV,))], name="all_reduce_small")(v)


def _as_2d(a):
    return a.reshape(-1, a.shape[-1])


def _row_tile(rows, cols):
    for t in (512, 256, 128, 64, 32, 16):
        if rows % t == 0 and t * cols * 4 <= (1 << 20):
            return t
    return rows


def _sum_chips_into(received, sent, stacked, slot):
    _, rows, cols = received.shape
    tr = _row_tile(rows, cols)
    first = slot * (rows // tr)
    me = (2 * lax.axis_index("x") + lax.axis_index("y")).astype(jnp.int32).reshape(1)

    def body(me_ref, r_ref, own_ref, _, o_ref):
        total = None
        for k in range(N_CHIPS):
            part = jnp.where(me_ref[0] == k, own_ref[...], r_ref[k]).astype(F32)
            total = part if total is None else total + part
        o_ref[...] = total

    grid_spec = pltpu.PrefetchScalarGridSpec(
        num_scalar_prefetch=1, grid=(rows // tr,),
        in_specs=[pl.BlockSpec((N_CHIPS, tr, cols), lambda i, me_ref: (0, i, 0)),
                  pl.BlockSpec((None, tr, cols), lambda i, me_ref: (me_ref[0], i, 0)), ANY_SPEC],
        out_specs=pl.BlockSpec((tr, cols), lambda i, me_ref: (first + i, 0)))
    return pl.pallas_call(body, out_shape=jax.ShapeDtypeStruct(stacked.shape, F32), grid_spec=grid_spec,
                          input_output_aliases={3: 0}, name="sum_chips",
                          compiler_params=_params(("parallel",)))(me, received, sent, stacked)


def _adamw_math(w, g, m, v):
    m = ADAM_B1 * m + (1.0 - ADAM_B1) * g
    v = ADAM_B2 * v + (1.0 - ADAM_B2) * (g * g)
    m_hat = m / (1.0 - ADAM_B1 ** ADAM_STEP)
    v_hat = v / (1.0 - ADAM_B2 ** ADAM_STEP)
    delta = -ADAM_LR * (m_hat / (jnp.sqrt(v_hat) + ADAM_EPS) + ADAM_WD * w)
    return delta, m, v


def _adamw(w, m, v, g_mine, g_sibling):
    rows, cols = w.shape
    tr = _row_tile(rows, cols)
    two = g_sibling is not None

    def body(*refs):
        if two:
            w_ref, m_ref, v_ref, ga_ref, gb_ref, g_ref, d_ref, nm_ref, nv_ref = refs
            g = ga_ref[...] + gb_ref[...]
        else:
            w_ref, m_ref, v_ref, ga_ref, g_ref, d_ref, nm_ref, nv_ref = refs
            g = ga_ref[...]
        delta, nm, nv = _adamw_math(w_ref[...], g, m_ref[...], v_ref[...])
        g_ref[...] = g
        d_ref[...] = delta
        nm_ref[...] = nm
        nv_ref[...] = nv

    blk = pl.BlockSpec((tr, cols), lambda i: (i, 0))
    args = [w, m, v, g_mine] + ([g_sibling] if two else [])
    return pl.pallas_call(body, out_shape=[jax.ShapeDtypeStruct((rows, cols), F32)] * 4, grid=(rows // tr,),
                          in_specs=[blk] * len(args), out_specs=[blk] * 4, name="adamw",
                          compiler_params=_params(("parallel",)))(*args)


def _pack_rows(arrays):
    flat = jnp.concatenate([a.reshape(-1) for a in arrays])
    rows = -(-flat.shape[0] // (8 * LANE)) * 8
    return jnp.pad(flat, (0, rows * LANE - flat.shape[0])).reshape(rows, LANE)


def _unpack_rows(packed, shapes):
    flat = packed.reshape(-1)
    out, at = [], 0
    for s in shapes:
        size = math.prod(s)
        out.append(flat[at:at + size].reshape(s))
        at += size
    return out


def kernel(x, p, positions, norm_g, ffn_w_in, ffn_w_out, ple_w_proj, ple_w_gate, rel_bias, mla_w_a, mla_q_norm, mla_kv_norm, mla_w_uq, mla_w_ukv, mla_w_o, dil_w_qkv, dil_w_o, fox_w_qkvf, fox_b_f, fox_w_o, loss_target, m_norm_g, m_ffn_w_in, m_ffn_w_out, m_ple_w_proj, m_ple_w_gate, m_rel_bias, m_mla_w_a, m_mla_q_norm, m_mla_kv_norm, m_mla_w_uq, m_mla_w_ukv, m_mla_w_o, m_dil_w_qkv, m_dil_w_o, m_fox_w_qkvf, m_fox_b_f, m_fox_w_o, v_norm_g, v_ffn_w_in, v_ffn_w_out, v_ple_w_proj, v_ple_w_gate, v_rel_bias, v_mla_w_a, v_mla_q_norm, v_mla_kv_norm, v_mla_w_uq, v_mla_w_ukv, v_mla_w_o, v_dil_w_qkv, v_dil_w_o, v_fox_w_qkvf, v_fox_b_f, v_fox_w_o):
    w = dict(norm_g=norm_g, ffn_w_in=ffn_w_in, ffn_w_out=ffn_w_out, ple_w_proj=ple_w_proj, ple_w_gate=ple_w_gate,
             rel_bias=rel_bias, mla_w_a=mla_w_a, mla_q_norm=mla_q_norm, mla_kv_norm=mla_kv_norm, mla_w_uq=mla_w_uq,
             mla_w_ukv=mla_w_ukv, mla_w_o=mla_w_o, dil_w_qkv=dil_w_qkv, dil_w_o=dil_w_o, fox_w_qkvf=fox_w_qkvf,
             fox_b_f=fox_b_f, fox_w_o=fox_w_o)
    m = dict(norm_g=m_norm_g, ffn_w_in=m_ffn_w_in, ffn_w_out=m_ffn_w_out, ple_w_proj=m_ple_w_proj,
             ple_w_gate=m_ple_w_gate, rel_bias=m_rel_bias, mla_w_a=m_mla_w_a, mla_q_norm=m_mla_q_norm,
             mla_kv_norm=m_mla_kv_norm, mla_w_uq=m_mla_w_uq, mla_w_ukv=m_mla_w_ukv, mla_w_o=m_mla_w_o,
             dil_w_qkv=m_dil_w_qkv, dil_w_o=m_dil_w_o, fox_w_qkvf=m_fox_w_qkvf, fox_b_f=m_fox_b_f, fox_w_o=m_fox_w_o)
    v = dict(norm_g=v_norm_g, ffn_w_in=v_ffn_w_in, ffn_w_out=v_ffn_w_out, ple_w_proj=v_ple_w_proj,
             ple_w_gate=v_ple_w_gate, rel_bias=v_rel_bias, mla_w_a=v_mla_w_a, mla_q_norm=v_mla_q_norm,
             mla_kv_norm=v_mla_kv_norm, mla_w_uq=v_mla_w_uq, mla_w_ukv=v_mla_w_ukv, mla_w_o=v_mla_w_o,
             dil_w_qkv=v_dil_w_qkv, dil_w_o=v_dil_w_o, fox_w_qkvf=v_fox_w_qkvf, fox_b_f=v_fox_b_f, fox_w_o=v_fox_w_o)
    chip = 2 * lax.axis_index("x") + lax.axis_index("y")

    small_shapes = [w[k].shape for k in SMALL_SHARDED]
    order = [(i, part) for i in range(DEPTH) for part in (MIXER_PART, COMMON_PART)]
    gathers = {}
    after = positions
    for i, part in order:
        bufs = [_own_slot(w[k][_layer_slot(k, i)].astype(BF)) for k in _part_names(i, part)]
        if (i, part) == order[0]:
            bufs.append(_own_slot(_pack_rows([w[k] for k in SMALL_SHARDED])))
        gathers[i, part] = _spread_start(bufs, None, after, f"gather_start_{i}_{part}")
        after = gathers[i, part]["token"]
    all_started = after
    state = {}

    def get_part(i, part, after_array):
        is_first = (i, part) == order[0]
        lands = _spread_wait(gathers[i, part], all_started if is_first else after_array, f"gather_wait_{i}_{part}")
        if is_first:
            pieces = [_unpack_rows(lands[-1][k], small_shapes) for k in range(N_CHIPS)]
            small = {name: jnp.concatenate([pieces[k][idx] for k in range(N_CHIPS)], axis=-1)
                     for idx, name in enumerate(SMALL_SHARDED)}
            state["small"] = dict(small, rel_bias=rel_bias, fox_b_f=fox_b_f)
        chunks = dict(zip(_part_names(i, part), lands))
        state[i, part] = {k: a.shape for k, a in chunks.items()}
        return _part_to_compute(i, part, chunks)

    exchanges = {}

    def put_part(i, part, lg):
        contrib = _part_contributions(i, part, lg, state[i, part])
        srcs = [contrib[k] for k in _part_names(i, part)]
        exchanges[i, part] = _spread_start([lax.empty(s.shape, s.dtype) for s in srcs], srcs, positions,
                                           f"exchange_start_{i}_{part}")
        return exchanges[i, part]["token"]

    sq, grad_x, sg = _run_layers(x[0], p[:, 0], positions[0], loss_target[0], get_part, lambda: state["small"],
                                 put_part)
    loss = lax.psum(0.5 / D_MODEL * jnp.sum(sq), ("x", "y", "c"))

    mine = {k: lax.empty(_as_2d(w[k]).shape, F32) for k in BIG}
    for i, part in [(i, part) for i in reversed(range(DEPTH)) for part in (COMMON_PART, MIXER_PART)]:
        received, sent = _spread_wait(exchanges[i, part], grad_x, f"exchange_wait_{i}_{part}")
        for k, r, s in zip(_part_names(i, part), received, sent):
            mine[k] = _sum_chips_into(r, s, mine[k], _layer_slot(k, i))
    theirs = _exchange_sibling([mine[k] for k in BIG], "exchange_sibling")
    results = {}
    for k, gb in zip(BIG, theirs):
        outs = _adamw(_as_2d(w[k]), _as_2d(m[k]), _as_2d(v[k]), mine[k], gb)
        results[k] = [o.reshape(w[k].shape) for o in outs]

    small_all = SMALL_SHARDED + SMALL_REPLICATED
    full_shapes = [sg[k].shape for k in small_all]
    reduced = dict(zip(small_all, _unpack_rows(_all_reduce_small(_pack_rows([sg[k] for k in small_all])), full_shapes)))
    local_g = []
    for k in small_all:
        g = reduced[k]
        if k in SMALL_SHARDED:
            width = w[k].shape[-1]
            g = lax.dynamic_slice_in_dim(g, chip * width, width, axis=g.ndim - 1)
        local_g.append(g)
    local_shapes = [w[k].shape for k in small_all]
    outs = _adamw(_pack_rows([w[k] for k in small_all]), _pack_rows([m[k] for k in small_all]),
                  _pack_rows([v[k] for k in small_all]), _pack_rows(local_g), None)
    unpacked = [_unpack_rows(o, local_shapes) for o in outs]
    for idx, k in enumerate(small_all):
        results[k] = [u[idx] for u in unpacked]

    return (loss, grad_x[None], *[results[k][0] for k in WEIGHTS], *[results[k][1] for k in WEIGHTS],
            *[results[k][2] for k in WEIGHTS], *[results[k][3] for k in WEIGHTS])
```

```python
import functools
import math

import jax
import jax.numpy as jnp
from jax import lax
from jax.experimental import pallas as pl
from jax.experimental.pallas import tpu as pltpu

F32 = jnp.float32
BF = jnp.bfloat16
MESH = pl.DeviceIdType.MESH
HBM_SPEC = pl.BlockSpec(memory_space=pltpu.HBM)

D_MODEL = 1024
DEPTH = 4
N_MIXERS = 3
D_FF = 2816
NORM_EPS = 1e-6
NEG_INF = -1e30
LANE = 128
HEADS = 16
HEAD_DIM = 64
MLA_Q_RANK = 384
MLA_KV_RANK = 256
MLA_ROPE = 32
MLA_A_PAD = 768
ROPE_THETA = 10000.0
DIL_PATTERNS = ((128, 1), (512, 4), (2048, 16))
Q_BLOCK = 128
DIL_PAIRS = 2
REL_BUCKETS = 32
REL_MAX_DIST = 2048
N_CHIPS = 4
N_DEV = 8

ADAM_LR = 0.001
ADAM_B1 = 0.9
ADAM_B2 = 0.999
ADAM_EPS = 1e-08
ADAM_WD = 0.01
ADAM_STEP = 10

VMEM_LIMIT = 56 * 1024 * 1024
MATMUL_VMEM_BUDGET = 36 * 1024 * 1024
ROW_TILE = 256
ATTN_TILE = 256
ATTN_Q_TILE = 512
MLA_GROUP = 4
FOX_GROUP = 4


def _params(sem=None):
    return pltpu.CompilerParams(dimension_semantics=sem, vmem_limit_bytes=VMEM_LIMIT)


def _divisor_tiles(dim):
    tiles = [t for t in range(LANE, dim + 1, LANE) if dim % t == 0]
    return tiles or [dim]


def _matmul_tiles(m, n, k, a_bytes, b_bytes, out_bytes, has_add, n_unit=None, k_unit=None):
    best = None
    for tm in _divisor_tiles(m):
        for tn in _divisor_tiles(n_unit or n):
            for tk in _divisor_tiles(k_unit or k):
                if max(tm, tn, tk) > 2048:
                    continue
                vmem = 2 * (tm * tk * a_bytes + tk * tn * b_bytes + tm * tn * out_bytes) + tm * tn * 4
                if has_add:
                    vmem += 2 * tm * tn * 4
                if vmem > MATMUL_VMEM_BUDGET:
                    continue
                steps = (m // tm) * (n // tn) * (k // tk)
                traffic = m * k * a_bytes * (n // tn) + k * n * b_bytes * (m // tm) + m * n * out_bytes
                cost = traffic / 3.0e12 + steps * 0.4e-6
                if best is None or cost < best[0]:
                    best = (cost, tm, tn, tk)
    return best[1:]


def _matmul(a, b, *, ta=False, tb=False, b_chunks=False, out_chunks=False, add=None, out_dtype=F32, name):
    k, m = a.shape if ta else a.shape[::-1]
    n_unit = k_unit = None
    if b_chunks:
        chunks, rows_w, c = b.shape
        if tb:
            kb, n, k_unit = chunks * c, rows_w, c
        else:
            kb, n, n_unit = rows_w, chunks * c, c
    else:
        kb, n = b.shape[::-1] if tb else b.shape
    if out_chunks:
        assert n % N_CHIPS == 0 and add is None
        n_unit = n // N_CHIPS
    assert k == kb, (a.shape, b.shape, ta, tb)
    tm, tn, tk = _matmul_tiles(m, n, k, a.dtype.itemsize, b.dtype.itemsize, jnp.dtype(out_dtype).itemsize,
                               add is not None, n_unit, k_unit)
    nk = k // tk
    dims = (((0 if ta else 1,), (1 if tb else 0,)), ((), ()))

    def body(*refs):
        if add is None:
            a_ref, b_ref, o_ref, acc_ref = refs
            add_ref = None
        else:
            a_ref, b_ref, add_ref, o_ref, acc_ref = refs
        kk = pl.program_id(2)

        @pl.when(kk == 0)
        def _():
            acc_ref[...] = jnp.zeros_like(acc_ref)

        acc_ref[...] += lax.dot_general(a_ref[...].astype(BF), b_ref[...].astype(BF), dims,
                                        preferred_element_type=F32)

        @pl.when(kk == nk - 1)
        def _():
            r = acc_ref[...]
            if add_ref is not None:
                r = r + add_ref[...].astype(F32)
            o_ref[...] = r.astype(out_dtype)

    a_spec = pl.BlockSpec((tk, tm), lambda i, j, q: (q, i)) if ta else pl.BlockSpec((tm, tk), lambda i, j, q: (i, q))
    if b_chunks and tb:
        per_k = k_unit // tk
        b_spec = pl.BlockSpec((None, tn, tk), lambda i, j, q: (q // per_k, j, q % per_k))
    elif b_chunks:
        per_n = n_unit // tn
        b_spec = pl.BlockSpec((None, tk, tn), lambda i, j, q: (j // per_n, q, j % per_n))
    elif tb:
        b_spec = pl.BlockSpec((tn, tk), lambda i, j, q: (j, q))
    else:
        b_spec = pl.BlockSpec((tk, tn), lambda i, j, q: (q, j))
    if out_chunks:
        per_o = n_unit // tn
        o_spec = pl.BlockSpec((None, tm, tn), lambda i, j, q: (j // per_o, i, j % per_o))
        out_shape = jax.ShapeDtypeStruct((N_CHIPS, m, n_unit), out_dtype)
    else:
        o_spec = pl.BlockSpec((tm, tn), lambda i, j, q: (i, j))
        out_shape = jax.ShapeDtypeStruct((m, n), out_dtype)
    in_specs = [a_spec, b_spec]
    args = [a, b]
    if add is not None:
        in_specs.append(o_spec)
        args.append(add)
    return pl.pallas_call(
        body, out_shape=out_shape, grid=(m // tm, n // tn, nk),
        in_specs=in_specs, out_specs=o_spec, scratch_shapes=[pltpu.VMEM((tm, tn), F32)], name=name,
        compiler_params=_params(("parallel", "parallel", "arbitrary")))(*args)


def _rowwise(body, name, rows, ins, outs, tr=ROW_TILE):
    def row_spec(cols):
        return pl.BlockSpec((tr, cols), lambda i: (i, 0))

    def full_spec(shape):
        zeros = (0,) * len(shape)
        return pl.BlockSpec(shape, lambda i: zeros)

    in_specs = [row_spec(a.shape[1]) if kind == "row" else full_spec(a.shape) for a, kind in ins]
    out_specs = [row_spec(shape[1]) if kind == "row" else full_spec(shape) for shape, _, kind in outs]
    out_shape = [jax.ShapeDtypeStruct(shape, dtype) for shape, dtype, _ in outs]
    return pl.pallas_call(body, out_shape=out_shape, grid=(rows // tr,), in_specs=in_specs, out_specs=out_specs,
                          name=name, compiler_params=_params(("arbitrary",)))(*[a for a, _ in ins])


def _rstd(x):
    return lax.rsqrt(jnp.mean(x * x, axis=-1, keepdims=True) + NORM_EPS)


def _rms_bwd_math(x, g, dy):
    r = _rstd(x)
    gd = dy * g
    dx = r * gd - x * (r * r * r) * jnp.mean(gd * x, axis=-1, keepdims=True)
    dg = jnp.sum(dy * x * r, axis=0, keepdims=True)
    return dx, dg


def _sigmoid(x):
    return 0.5 * jnp.tanh(0.5 * x) + 0.5


def _init_acc(*refs):
    @pl.when(pl.program_id(0) == 0)
    def _():
        for r in refs:
            r[...] = jnp.zeros_like(r)


def _prenorm(h, g):
    rows, cols = h.shape

    def body(h_ref, g_ref, o_ref):
        x = h_ref[...]
        o_ref[...] = (x * _rstd(x) * g_ref[...]).astype(BF)

    return _rowwise(body, "prenorm", rows, [(h, "row"), (g, "full")], [((rows, cols), BF, "row")])[0]


def _post_residual(h, y, g_post, g_pre):
    rows, cols = h.shape
    with_pre = g_pre is not None

    def body(*refs):
        if with_pre:
            h_ref, y_ref, gp_ref, gq_ref, hn_ref, hb_ref = refs
        else:
            h_ref, y_ref, gp_ref, hn_ref, hb_ref = refs
        yv = y_ref[...]
        hn = h_ref[...] + yv * _rstd(yv) * gp_ref[...]
        hn_ref[...] = hn
        hb_ref[...] = (hn * _rstd(hn) * gq_ref[...] if with_pre else hn).astype(BF)

    ins = [(h, "row"), (y, "row"), (g_post, "full")] + ([(g_pre, "full")] if with_pre else [])
    return _rowwise(body, "post_residual_pre" if with_pre else "post_residual", rows, ins,
                    [((rows, cols), F32, "row"), ((rows, cols), BF, "row")])


def _ple_forward(h2, pp, z, g_pre):
    rows, cols = h2.shape

    def body(h_ref, p_ref, z_ref, g_ref, h3_ref, hb_ref):
        h3 = h_ref[...] + p_ref[...] * _sigmoid(z_ref[...])
        h3_ref[...] = h3
        hb_ref[...] = (h3 * _rstd(h3) * g_ref[...]).astype(BF)

    return _rowwise(body, "ple_forward", rows, [(h2, "row"), (pp, "row"), (z, "row"), (g_pre, "full")],
                    [((rows, cols), F32, "row"), ((rows, cols), BF, "row")])


def _ple_loss(h2, pp, z, target):
    rows, cols = h2.shape

    def body(h_ref, p_ref, z_ref, t_ref, dh_ref, sq_ref):
        _init_acc(sq_ref)
        err = h_ref[...] + p_ref[...] * _sigmoid(z_ref[...]) - t_ref[...]
        dh_ref[...] = err * (1.0 / cols)
        sq_ref[...] += jnp.sum(err * err, axis=0, keepdims=True)

    return _rowwise(body, "ple_loss", rows, [(h2, "row"), (pp, "row"), (z, "row"), (target, "row")],
                    [((rows, cols), F32, "row"), ((1, cols), F32, "acc")])


def _ple_backward(dh3, pp, z):
    rows, cols = dh3.shape

    def body(d_ref, p_ref, z_ref, dpp_ref, dz_ref):
        d = d_ref[...]
        s = _sigmoid(z_ref[...])
        dpp_ref[...] = (d * s).astype(BF)
        dz_ref[...] = (d * p_ref[...] * s * (1.0 - s)).astype(BF)

    return _rowwise(body, "ple_backward", rows, [(dh3, "row"), (pp, "row"), (z, "row")],
                    [((rows, cols), BF, "row"), ((rows, cols), BF, "row")])


def _rms_backward(x, g, dy, add, out_dtype):
    rows, cols = x.shape
    with_add = add is not None

    def body(*refs):
        if with_add:
            x_ref, g_ref, dy_ref, add_ref, dx_ref, dg_ref = refs
        else:
            x_ref, g_ref, dy_ref, dx_ref, dg_ref = refs
        _init_acc(dg_ref)
        dx, dg = _rms_bwd_math(x_ref[...], g_ref[...], dy_ref[...].astype(F32))
        if with_add:
            dx = dx + add_ref[...]
        dx_ref[...] = dx.astype(out_dtype)
        dg_ref[...] += dg

    ins = [(x, "row"), (g, "full"), (dy, "row")] + ([(add, "row")] if with_add else [])
    return _rowwise(body, "rms_backward_add" if with_add else "rms_backward", rows, ins,
                    [((rows, cols), out_dtype, "row"), ((1, cols), F32, "acc")])


def _swiglu_forward(gu):
    rows = gu.shape[0]

    def body(gu_ref, o_ref):
        g = gu_ref[:, :D_FF].astype(F32)
        o_ref[...] = (g * _sigmoid(g) * gu_ref[:, D_FF:].astype(F32)).astype(BF)

    return _rowwise(body, "swiglu_forward", rows, [(gu, "row")], [((rows, D_FF), BF, "row")])[0]


def _swiglu_backward(gu, dact):
    rows = gu.shape[0]

    def body(gu_ref, d_ref, o_ref):
        g = gu_ref[:, :D_FF].astype(F32)
        u = gu_ref[:, D_FF:].astype(F32)
        d = d_ref[...].astype(F32)
        s = _sigmoid(g)
        gs = g * s
        o_ref[:, :D_FF] = (d * u * (s + gs * (1.0 - s))).astype(BF)
        o_ref[:, D_FF:] = (d * gs).astype(BF)

    return _rowwise(body, "swiglu_backward", rows, [(gu, "row"), (dact, "row")], [((rows, 2 * D_FF), BF, "row")])[0]


def _rope_tables(positions):
    half = MLA_ROPE // 2
    inv = ROPE_THETA ** (-jnp.arange(half, dtype=F32) / half)
    ang = positions.astype(F32)[:, None] * inv
    cos, sin = jnp.cos(ang), jnp.sin(ang)
    rows = positions.shape[0]
    c = jnp.ones((rows, LANE), F32).at[:, 64:80].set(cos).at[:, 80:96].set(cos)
    sa = jnp.zeros((rows, LANE), F32).at[:, 64:80].set(-sin)
    sb = jnp.zeros((rows, LANE), F32).at[:, 80:96].set(sin)
    return c, sa, sb


def _rope_apply(x, c, sa, sb):
    return x * c + pltpu.roll(x, LANE - 16, 1) * sa + pltpu.roll(x, 16, 1) * sb


def _rope_apply_t(dy, c, sa, sb):
    return dy * c + pltpu.roll(dy * sa, 16, 1) + pltpu.roll(dy * sb, LANE - 16, 1)


def _rope_heads(x, tables, transpose, name):
    rows, cols = x.shape

    def body(x_ref, c_ref, sa_ref, sb_ref, o_ref):
        fn = _rope_apply_t if transpose else _rope_apply
        c, sa, sb = c_ref[...], sa_ref[...], sb_ref[...]
        for head in range(cols // LANE):
            lanes = slice(head * LANE, (head + 1) * LANE)
            o_ref[:, lanes] = fn(x_ref[:, lanes].astype(F32), c, sa, sb).astype(BF)

    blk = pl.BlockSpec((ROW_TILE, cols), lambda i: (i, 0))
    tbl = pl.BlockSpec((ROW_TILE, LANE), lambda i: (i, 0))
    return pl.pallas_call(body, out_shape=jax.ShapeDtypeStruct((rows, cols), BF), grid=(rows // ROW_TILE,),
                          in_specs=[blk, tbl, tbl, tbl], out_specs=blk, name=name,
                          compiler_params=_params(("parallel",)))(x, *tables)


def _mla_mid_forward(a, q_norm, kv_norm, tables):
    rows = a.shape[0]
    qr, kvr = MLA_Q_RANK, MLA_KV_RANK

    def body(a_ref, qn_ref, kn_ref, c_ref, sa_ref, sb_ref, cq_ref, ckv_ref, kr_ref):
        aq = a_ref[:, 0:qr]
        akv = a_ref[:, qr:qr + kvr]
        cq_ref[...] = (aq * _rstd(aq) * qn_ref[...]).astype(BF)
        ckv_ref[...] = (akv * _rstd(akv) * kn_ref[...]).astype(BF)
        kr_ref[...] = _rope_apply(a_ref[:, qr + kvr:], c_ref[...], sa_ref[...], sb_ref[...]).astype(BF)

    ins = [(a, "row"), (q_norm, "full"), (kv_norm, "full")] + [(t, "row") for t in tables]
    return _rowwise(body, "mla_mid_forward", rows, ins,
                    [((rows, qr), BF, "row"), ((rows, kvr), BF, "row"), ((rows, LANE), BF, "row")])


def _mla_mid_backward(a, q_norm, kv_norm, tables, dcq, dckv, dkr):
    rows = a.shape[0]
    qr, kvr = MLA_Q_RANK, MLA_KV_RANK

    def body(a_ref, qn_ref, kn_ref, c_ref, sa_ref, sb_ref, dcq_ref, dckv_ref, dkr_ref, da_ref, dqn_ref, dkn_ref):
        _init_acc(dqn_ref, dkn_ref)
        dxq, dgq = _rms_bwd_math(a_ref[:, 0:qr], qn_ref[...], dcq_ref[...])
        dxk, dgk = _rms_bwd_math(a_ref[:, qr:qr + kvr], kn_ref[...], dckv_ref[...])
        da_ref[:, 0:qr] = dxq.astype(BF)
        da_ref[:, qr:qr + kvr] = dxk.astype(BF)
        da_ref[:, qr + kvr:] = _rope_apply_t(dkr_ref[...], c_ref[...], sa_ref[...], sb_ref[...]).astype(BF)
        dqn_ref[...] += dgq
        dkn_ref[...] += dgk

    ins = ([(a, "row"), (q_norm, "full"), (kv_norm, "full")] + [(t, "row") for t in tables]
           + [(dcq, "row"), (dckv, "row"), (dkr, "row")])
    return _rowwise(body, "mla_mid_backward", rows, ins,
                    [((rows, MLA_A_PAD), BF, "row"), ((1, qr), F32, "acc"), ((1, kvr), F32, "acc")])


def _attn_specs(rows, kv_off, g, many_row_vectors):
    head =pl.BlockSpec((rows, g * LANE), lambda h: (0, h))
    kv_head = pl.BlockSpec((rows, g * LANE), lambda h: (0, h + kv_off // g))
    shared = pl.BlockSpec((rows, LANE), lambda h: (0, 0))
    col_vec = pl.BlockSpec((g, rows, 1), lambda h: (h, 0, 0),
                           pipeline_mode=pl.Buffered(1 if many_row_vectors else 2))
    row_vec = pl.BlockSpec((g, 1, rows), lambda h: (h, 0, 0))
    return head, kv_head, shared, col_vec, row_vec


def _attn_forward(q, kv, kv_off, kr, cum_col, cum_row, scale, group_size, name):
    rows = q.shape[0]
    heads = HEADS
    t = ATTN_TILE
    tq = ATTN_Q_TILE
    per = tq // t
    has_kr = kr is not None
    has_f = cum_col is not None
    group = range(group_size)

    def body(*refs):
        it = iter(refs)
        q_ref, kv_ref = next(it), next(it)
        kr_ref = next(it) if has_kr else None
        cc_ref = next(it) if has_f else None
        cr_ref = next(it) if has_f else None
        o_ref, lse_ref = next(it), next(it)
        lo = lax.broadcasted_iota(jnp.int32, (1, LANE), 1) < HEAD_DIM
        row = lax.broadcasted_iota(jnp.int32, (tq, t), 0)
        col = lax.broadcasted_iota(jnp.int32, (tq, t), 1)
        lanes = [slice(g * LANE, (g + 1) * LANE) for g in group]

        def q_block(i, _):
            qs = pl.ds(pl.multiple_of(i * tq, tq), tq)
            qbs = [q_ref[qs, lanes[g]] for g in group]
            cqs = [cc_ref[g, qs, :] if has_f else None for g in group]

            def step(j, carry, diag):
                ks = pl.ds(pl.multiple_of(j * t, t), t)
                other = kr_ref[ks, :] if has_kr else jnp.zeros((t, LANE), BF)
                out = []
                for g in group:
                    m, l, acc = carry[g]
                    kvb = kv_ref[ks, lanes[g]]
                    kk = jnp.where(lo, kvb, other)
                    s = lax.dot_general(qbs[g], kk, (((1,), (1,)), ((), ())), preferred_element_type=F32) * scale
                    if has_f:
                        s = s + (cqs[g] - cr_ref[g, :, ks])
                    if diag is not None:
                        s = jnp.where(col + diag * t <= row, s, NEG_INF)
                    mn = jnp.maximum(m, jnp.max(s, axis=1, keepdims=True))
                    alpha = jnp.exp(m - mn)
                    p = jnp.exp(s - mn)
                    l = alpha * l + jnp.sum(p, axis=1, keepdims=True)
                    acc = alpha * acc + jnp.dot(p.astype(BF), kvb, preferred_element_type=F32)
                    out.append((mn, l, acc))
                return tuple(out)

            init = tuple((jnp.full((tq, 1), NEG_INF, F32), jnp.zeros((tq, 1), F32), jnp.zeros((tq, LANE), F32))
                         for _ in group)
            carry = lax.fori_loop(0, i * per, lambda j, c: step(j, c, None), init)
            for d in range(per):
                carry = step(i * per + d, carry, d)
            for g, (m, l, acc) in enumerate(carry):
                o_ref[qs, lanes[g]] = jnp.where(lo, 0.0, acc * (1.0 / l)).astype(BF)
                lse_ref[g, qs, :] = m + jnp.log(l)
            return 0

        lax.fori_loop(0, rows // tq, q_block, 0)

    head, kv_head, shared, col_vec, row_vec = _attn_specs(rows, kv_off, group_size, has_f)
    in_specs, args = [head, kv_head], [q, kv]
    if has_kr:
        in_specs.append(shared)
        args.append(kr)
    if has_f:
        in_specs += [col_vec, row_vec]
        args += [cum_col, cum_row]
    return pl.pallas_call(
        body, out_shape=[jax.ShapeDtypeStruct((rows, heads * LANE), BF), jax.ShapeDtypeStruct((heads, rows, 1), F32)],
        grid=(heads // group_size,), in_specs=in_specs, out_specs=[head, col_vec], name=name,
        compiler_params=_params(("arbitrary",)))(*args)


def _attn_backward(q, kv, kv_off, kr, cum_col, cum_row, o, do, lse, scale, group_size, name):
    rows = q.shape[0]
    heads = HEADS
    t = ATTN_TILE
    nb = rows // t
    has_kr = kr is not None
    has_f = cum_col is not None
    group = range(group_size)

    def body(*refs):
        it = iter(refs)
        q_ref, kv_ref = next(it), next(it)
        kr_ref = next(it) if has_kr else None
        cc_ref = next(it) if has_f else None
        cr_ref = next(it) if has_f else None
        o_ref, do_ref, lse_ref = next(it), next(it), next(it)
        dq_ref, dkv_ref = next(it), next(it)
        dkr_ref = next(it) if has_kr else None
        dck_ref = next(it) if has_f else None
        dcq_ref = next(it) if has_f else None
        dq_acc = next(it)
        lo = lax.broadcasted_iota(jnp.int32, (1, LANE), 1) < HEAD_DIM
        causal = (lax.broadcasted_iota(jnp.int32, (t, t), 1) <= lax.broadcasted_iota(jnp.int32, (t, t), 0))
        lanes = [slice(g * LANE, (g + 1) * LANE) for g in group]

        dq_acc[...] = jnp.zeros_like(dq_acc)
        if has_kr:
            _init_acc(dkr_ref)
        if has_f:
            dcq_ref[...] = jnp.zeros_like(dcq_ref)

        def kv_block(j, _):
            ks = pl.ds(pl.multiple_of(j * t, t), t)
            other = kr_ref[ks, :] if has_kr else jnp.zeros((t, LANE), BF)
            kvbs = [kv_ref[ks, lanes[g]] for g in group]
            kks = [jnp.where(lo, kvbs[g], other) for g in group]
            cks = [cr_ref[g, :, ks] if has_f else None for g in group]

            def pair(i, carry, diag):
                qs = pl.ds(pl.multiple_of(i * t, t), t)
                out = []
                for g in group:
                    dkk, dvv, dcs = carry[g]
                    qb = q_ref[qs, lanes[g]]
                    dob = do_ref[qs, lanes[g]]
                    s = lax.dot_general(qb, kks[g], (((1,), (1,)), ((), ())), preferred_element_type=F32) * scale
                    if has_f:
                        s = s + (cc_ref[g, qs, :] - cks[g])
                    if diag:
                        s = jnp.where(causal, s, NEG_INF)
                    p = jnp.exp(s - lse_ref[g, qs, :])
                    dp = lax.dot_general(dob, kvbs[g], (((1,), (1,)), ((), ())), preferred_element_type=F32)
                    delta = jnp.sum(dob.astype(F32) * o_ref[qs, lanes[g]].astype(F32), axis=1, keepdims=True)
                    ds = p * (dp - delta)
                    dsb = ds.astype(BF)
                    dvv = dvv + lax.dot_general(p.astype(BF), dob, (((0,), (0,)), ((), ())), preferred_element_type=F32)
                    dkk = dkk + lax.dot_general(dsb, qb, (((0,), (0,)), ((), ())), preferred_element_type=F32)
                    dq_acc[qs, lanes[g]] += jnp.dot(dsb, kks[g], preferred_element_type=F32)
                    if has_f:
                        dcs = dcs + jnp.sum(ds, axis=0, keepdims=True)
                        dcq_ref[g, qs, :] += jnp.sum(ds, axis=1, keepdims=True)
                    out.append((dkk, dvv, dcs))
                return tuple(out)

            init = tuple((jnp.zeros((t, LANE), F32), jnp.zeros((t, LANE), F32), jnp.zeros((1, t), F32)) for _ in group)
            carry = pair(j, init, True)
            carry = lax.fori_loop(j + 1, nb, lambda i, c: pair(i, c, False), carry)
            for g, (dkk, dvv, dcs) in enumerate(carry):
                dkk = dkk * scale
                dkv_ref[ks, lanes[g]] = jnp.where(lo, dkk, dvv).astype(BF)
                if has_kr:
                    dkr_ref[ks, :] += jnp.where(lo, 0.0, dkk)
                if has_f:
                    dck_ref[g, :, ks] = -dcs
            return 0

        lax.fori_loop(0, nb, kv_block, 0)
        dq_ref[...] = (dq_acc[...] * scale).astype(BF)

    head, kv_head, shared, col_vec, row_vec = _attn_specs(rows, kv_off, group_size, has_f)
    in_specs, args = [head, kv_head], [q, kv]
    if has_kr:
        in_specs.append(shared)
        args.append(kr)
    if has_f:
        in_specs += [col_vec, row_vec]
        args += [cum_col, cum_row]
    in_specs += [head, head, col_vec]
    args += [o, do, lse]
    out_shape = [jax.ShapeDtypeStruct((rows, heads * LANE), BF), jax.ShapeDtypeStruct((rows, heads * LANE), BF)]
    out_specs = [head, head]
    if has_kr:
        out_shape.append(jax.ShapeDtypeStruct((rows, LANE), F32))
        out_specs.append(shared)
    if has_f:
        out_shape += [jax.ShapeDtypeStruct((heads, 1, rows), F32), jax.ShapeDtypeStruct((heads, rows, 1), F32)]
        out_specs += [row_vec, col_vec]
    return pl.pallas_call(
        body, out_shape=out_shape, grid=(heads // group_size,), in_specs=in_specs, out_specs=out_specs,
        scratch_shapes=[pltpu.VMEM((rows, group_size * LANE), F32)], name=name,
        compiler_params=_params(("arbitrary",)))(*args)


def _tri_dot(tri, x):
    return jnp.dot(tri, x, preferred_element_type=F32, precision=lax.Precision.HIGHEST)


def _forget_forward(f_raw, b_f):
    rows = f_raw.shape[0]
    t = ATTN_TILE

    def body(f_ref, b_ref, cum_ref):
        tri = (lax.broadcasted_iota(jnp.int32, (t, t), 1) <= lax.broadcasted_iota(jnp.int32, (t, t), 0)).astype(F32)

        def blk(i, carry):
            sl = pl.ds(pl.multiple_of(i * t, t), t)
            xv = f_ref[sl, :] + b_ref[...]
            log_f = jnp.minimum(xv, 0.0) - jnp.log(1.0 + jnp.exp(-jnp.abs(xv)))
            cum_ref[sl, :] = _tri_dot(tri, log_f) + carry
            return carry + jnp.sum(log_f, axis=0, keepdims=True)

        lax.fori_loop(0, rows // t, blk, jnp.zeros((1, LANE), F32))

    return pl.pallas_call(body, out_shape=jax.ShapeDtypeStruct((rows, LANE), F32), name="forget_forward",
                          compiler_params=_params())(f_raw, b_f)


def _forget_backward(f_raw, b_f, dcum):
    rows = f_raw.shape[0]
    t = ATTN_TILE
    nb = rows // t

    def body(f_ref, b_ref, dc_ref, df_ref, db_ref):
        tri = (lax.broadcasted_iota(jnp.int32, (t, t), 1) >= lax.broadcasted_iota(jnp.int32, (t, t), 0)).astype(F32)

        def blk(i, carry):
            later, db = carry
            sl = pl.ds(pl.multiple_of((nb - 1 - i) * t, t), t)
            dc = dc_ref[sl, :]
            dlog = _tri_dot(tri, dc) + later
            xv = f_ref[sl, :] + b_ref[...]
            df = dlog / (1.0 + jnp.exp(xv))
            df_ref[sl, :] = df.astype(BF)
            return later + jnp.sum(dc, axis=0, keepdims=True), db + jnp.sum(df, axis=0, keepdims=True)

        _, db = lax.fori_loop(0, nb, blk, (jnp.zeros((1, LANE), F32), jnp.zeros((1, LANE), F32)))
        db_ref[...] = db

    return pl.pallas_call(body, out_shape=[jax.ShapeDtypeStruct((rows, LANE), BF), jax.ShapeDtypeStruct((1, LANE), F32)],
                          name="forget_backward", compiler_params=_params())(f_raw, b_f, dcum)


def _t5_bucket(dist):
    max_exact = REL_BUCKETS // 2
    n = jnp.maximum(dist.astype(F32), 1.0)
    large = max_exact + (jnp.log(n / max_exact) / math.log(REL_MAX_DIST / max_exact)
                         * (REL_BUCKETS - max_exact)).astype(jnp.int32)
    large = jnp.minimum(large, REL_BUCKETS - 1)
    return jnp.where(dist < max_exact, dist, large)


def _dil_buckets(dilation):
    i = jnp.arange(Q_BLOCK)[:, None]
    j = jnp.arange(Q_BLOCK)[None, :]
    cur = _t5_bucket(jnp.clip(i - j, 0) * dilation).astype(jnp.int32)
    prev = _t5_bucket(jnp.clip(Q_BLOCK + i - j, 0) * dilation).astype(jnp.int32)
    return cur, prev


def _dil_bias_tiles(tbl_ref, bc_ref, bp_ref, bias_ref, group, hp):
    ii = lax.broadcasted_iota(jnp.int32, (Q_BLOCK, Q_BLOCK), 0)
    jj = lax.broadcasted_iota(jnp.int32, (Q_BLOCK, Q_BLOCK), 1)
    for hh in range(2 * DIL_PAIRS):
        col = group * HEADS + 2 * DIL_PAIRS * hp + hh
        acc_c = jnp.zeros((Q_BLOCK, Q_BLOCK), F32)
        acc_p = jnp.zeros((Q_BLOCK, Q_BLOCK), F32)
        for b in range(REL_BUCKETS):
            val = tbl_ref[b, col]
            acc_c = jnp.where(bc_ref[...] == b, val, acc_c)
            acc_p = jnp.where(bp_ref[...] == b, val, acc_p)
        bias_ref[2 * hh] = jnp.where(jj <= ii, acc_c, NEG_INF)
        bias_ref[2 * hh + 1] = jnp.where(jj >= ii, acc_p, NEG_INF)


def _dil_view(qkv, group, dilation):
    if dilation == 1:
        return qkv
    width = 3 * HEADS * HEAD_DIM
    return qkv[:, group * width:(group + 1) * width].reshape(qkv.shape[0] // dilation, dilation * width)


def _dil_specs(group, dilation, length):
    width = DIL_PAIRS * LANE
    per = 8 // DIL_PAIRS

    def col(kind):
        if dilation == 1:
            return pl.BlockSpec((length, width), lambda hp, r: (0, (group * 3 + kind) * per + hp))
        return pl.BlockSpec((length, width), lambda hp, r: (0, (r * 3 + kind) * per + hp))

    out = pl.BlockSpec((length, width), lambda hp, r: (0, r * per + hp))
    tile = pl.BlockSpec((Q_BLOCK, Q_BLOCK), lambda hp, r: (0, 0))
    table = pl.BlockSpec(memory_space=pltpu.SMEM)
    return col, out, tile, table


def _dil_forward(view, group, dilation, table, buckets):
    length = view.shape[0]
    rows = length * dilation
    nb = length // Q_BLOCK
    scale = HEAD_DIM ** -0.5
    qb = Q_BLOCK

    def body(tbl_ref, bc_ref, bp_ref, q_ref, k_ref, v_ref, o_ref, lse_ref, bias_ref):
        hp = pl.program_id(0)

        @pl.when(pl.program_id(1) == 0)
        def _():
            _dil_bias_tiles(tbl_ref, bc_ref, bp_ref, bias_ref, group, hp)

        lo = lax.broadcasted_iota(jnp.int32, (1, LANE), 1) < HEAD_DIM
        nt = (((1,), (1,)), ((), ()))

        def blk(n, first):
            cur = pl.ds(0, qb) if first else pl.ds(pl.multiple_of(n * qb, qb), qb)
            prev = None if first else pl.ds(pl.multiple_of((n - 1) * qb, qb), qb)
            for pair in range(DIL_PAIRS):
                lanes = slice(pair * LANE, (pair + 1) * LANE)
                qn = q_ref[cur, lanes] * scale
                kc, vc = k_ref[cur, lanes], v_ref[cur, lanes]
                if not first:
                    kp, vp = k_ref[prev, lanes], v_ref[prev, lanes]
                outs, lses = [], []
                for hh in range(2):
                    bias = 4 * pair + 2 * hh
                    qm = jnp.where(lo if hh == 0 else ~lo, qn, jnp.zeros_like(qn))
                    s_c = lax.dot_general(qm, kc, nt, preferred_element_type=F32) + bias_ref[bias]
                    m = jnp.max(s_c, axis=1, keepdims=True)
                    if not first:
                        s_p = lax.dot_general(qm, kp, nt, preferred_element_type=F32) + bias_ref[bias + 1]
                        m = jnp.maximum(m, jnp.max(s_p, axis=1, keepdims=True))
                    e_c = jnp.exp(s_c - m)
                    l = jnp.sum(e_c, axis=1, keepdims=True)
                    acc = jnp.dot(e_c.astype(BF), vc, preferred_element_type=F32)
                    if not first:
                        e_p = jnp.exp(s_p - m)
                        l = l + jnp.sum(e_p, axis=1, keepdims=True)
                        acc = acc + jnp.dot(e_p.astype(BF), vp, preferred_element_type=F32)
                    outs.append(acc * (1.0 / l))
                    lses.append(m + jnp.log(l))
                o_ref[cur, lanes] = jnp.where(lo, outs[0], outs[1])
                lse_ref[cur, lanes] = jnp.where(lo, lses[0], lses[1])
            return 0

        blk(0, True)
        if nb > 1:
            lax.fori_loop(1, nb, lambda n, _: blk(n, False), 0)

    col, out, tile, tbl = _dil_specs(group, dilation, length)
    bc, bp = buckets
    o, lse = pl.pallas_call(
        body, out_shape=[jax.ShapeDtypeStruct((length, dilation * D_MODEL), F32)] * 2,
        grid=(8 // DIL_PAIRS, dilation), in_specs=[tbl, tile, tile, col(0), col(1), col(2)], out_specs=[out, out],
        scratch_shapes=[pltpu.VMEM((4 * DIL_PAIRS, qb, qb), F32)], name=f"dilated_forward_{dilation}",
        compiler_params=_params(("arbitrary", "arbitrary")))(
            table, bc, bp, view, view, view)
    return o.reshape(rows, D_MODEL), lse.reshape(rows, D_MODEL)


def _dil_backward(view, group, dilation, table, buckets, do_g, lse, dlt):
    length = view.shape[0]
    rows = length * dilation
    nb = length // Q_BLOCK
    scale = HEAD_DIM ** -0.5
    qb = Q_BLOCK

    def body(tbl_ref, bc_ref, bp_ref, q_ref, k_ref, v_ref, do_ref, lse_ref, dlt_ref,
             dq_ref, dk_ref, dv_ref, db_ref, bias_ref, dk_acc, dv_acc):
        hp = pl.program_id(0)

        @pl.when(pl.program_id(1) == 0)
        def _():
            _dil_bias_tiles(tbl_ref, bc_ref, bp_ref, bias_ref, group, hp)
            db_ref[...] = jnp.zeros_like(db_ref)

        dk_acc[...] = jnp.zeros_like(dk_acc)
        dv_acc[...] = jnp.zeros_like(dv_acc)
        lo = lax.broadcasted_iota(jnp.int32, (1, LANE), 1) < HEAD_DIM
        tn = (((0,), (0,)), ((), ()))
        nt = (((1,), (1,)), ((), ()))

        def blk(n, first):
            cur = pl.ds(0, qb) if first else pl.ds(pl.multiple_of(n * qb, qb), qb)
            prev = None if first else pl.ds(pl.multiple_of((n - 1) * qb, qb), qb)
            for pair in range(DIL_PAIRS):
                lanes = slice(pair * LANE, (pair + 1) * LANE)
                qn = q_ref[cur, lanes] * scale
                don = do_ref[cur, lanes]
                kc, vc = k_ref[cur, lanes], v_ref[cur, lanes]
                if not first:
                    kp, vp = k_ref[prev, lanes], v_ref[prev, lanes]
                lse_n = lse_ref[cur, lanes]
                dlt_n = dlt_ref[cur, lanes]
                dqs = []
                dkc = jnp.zeros((qb, LANE), F32)
                dkp = jnp.zeros((qb, LANE), F32)
                dvc = jnp.zeros((qb, LANE), F32)
                dvp = jnp.zeros((qb, LANE), F32)
                for hh in range(2):
                    bias = 4 * pair + 2 * hh
                    mask = lo if hh == 0 else ~lo
                    qm = jnp.where(mask, qn, jnp.zeros_like(qn))
                    dom = jnp.where(mask, don, jnp.zeros_like(don))
                    lse_h = jnp.max(jnp.where(mask, lse_n, -3e38), axis=1, keepdims=True)
                    dlt_h = jnp.max(jnp.where(mask, dlt_n, -3e38), axis=1, keepdims=True)
                    p_c = jnp.exp(lax.dot_general(qm, kc, nt, preferred_element_type=F32) + bias_ref[bias] - lse_h)
                    ds_c = p_c * (lax.dot_general(dom, vc, nt, preferred_element_type=F32) - dlt_h)
                    db_ref[pair, 2 * hh] += ds_c
                    dsc_b = ds_c.astype(BF)
                    dq = jnp.dot(dsc_b, kc, preferred_element_type=F32)
                    dkc = dkc + lax.dot_general(dsc_b, qm, tn, preferred_element_type=F32)
                    dvc = dvc + lax.dot_general(p_c.astype(BF), dom, tn, preferred_element_type=F32)
                    if not first:
                        p_p = jnp.exp(lax.dot_general(qm, kp, nt, preferred_element_type=F32) + bias_ref[bias + 1] - lse_h)
                        ds_p = p_p * (lax.dot_general(dom, vp, nt, preferred_element_type=F32) - dlt_h)
                        db_ref[pair, 2 * hh + 1] += ds_p
                        dsp_b = ds_p.astype(BF)
                        dq = dq + jnp.dot(dsp_b, kp, preferred_element_type=F32)
                        dkp = dkp + lax.dot_general(dsp_b, qm, tn, preferred_element_type=F32)
                        dvp = dvp + lax.dot_general(p_p.astype(BF), dom, tn, preferred_element_type=F32)
                    dqs.append(dq)
                dq_ref[cur, lanes] = (jnp.where(lo, dqs[0], dqs[1]) * scale).astype(BF)
                dk_acc[cur, lanes] += dkc
                dv_acc[cur, lanes] += dvc
                if not first:
                    dk_acc[prev, lanes] += dkp
                    dv_acc[prev, lanes] += dvp
            return 0

        blk(0, True)
        if nb > 1:
            lax.fori_loop(1, nb, lambda n, _: blk(n, False), 0)
        dk_ref[...] = dk_acc[...].astype(BF)
        dv_ref[...] = dv_acc[...].astype(BF)

    col, out, tile, tbl = _dil_specs(group, dilation, length)
    bc, bp = buckets
    wide = (length, dilation * D_MODEL)
    dq, dk, dv, db = pl.pallas_call(
        body, out_shape=[jax.ShapeDtypeStruct(wide, BF)] * 3 + [jax.ShapeDtypeStruct((8, 4, qb, qb), F32)],
        grid=(8 // DIL_PAIRS, dilation), in_specs=[tbl, tile, tile, col(0), col(1), col(2), out, out, out],
        out_specs=[out, out, out, pl.BlockSpec((DIL_PAIRS, 4, qb, qb), lambda hp, r: (hp, 0, 0, 0))],
        scratch_shapes=[pltpu.VMEM((4 * DIL_PAIRS, qb, qb), F32), pltpu.VMEM((length, DIL_PAIRS * LANE), F32),
                        pltpu.VMEM((length, DIL_PAIRS * LANE), F32)],
        name=f"dilated_backward_{dilation}", compiler_params=_params(("arbitrary", "arbitrary")))(
            table, bc, bp, view, view, view,
            do_g.reshape(wide), lse.reshape(wide), dlt.reshape(wide))
    return dq.reshape(rows, D_MODEL), dk.reshape(rows, D_MODEL), dv.reshape(rows, D_MODEL), db


def _head_sums(x, lo):
    s0 = jnp.sum(jnp.where(lo, x, 0.0), axis=1, keepdims=True)
    s1 = jnp.sum(jnp.where(lo, 0.0, x), axis=1, keepdims=True)
    return jnp.where(lo, s0, s1)


def _dil_merge_forward(outs, lses):
    rows = outs[0].shape[0]

    def body(o0, o1, o2, l0, l1, l2, o_ref):
        ls = [l0[...], l1[...], l2[...]]
        m = jnp.maximum(jnp.maximum(ls[0], ls[1]), ls[2])
        es = [jnp.exp(v - m) for v in ls]
        tot = es[0] + es[1] + es[2]
        o_ref[...] = ((es[0] * o0[...] + es[1] * o1[...] + es[2] * o2[...]) / tot).astype(BF)

    blk = pl.BlockSpec((ROW_TILE, LANE), lambda i, j: (i, j))
    return pl.pallas_call(body, out_shape=jax.ShapeDtypeStruct((rows, D_MODEL), BF), grid=(rows // ROW_TILE, 8),
                          in_specs=[blk] * 6, out_specs=blk, name="dilated_merge_forward",
                          compiler_params=_params(("parallel", "parallel")))(*outs, *lses)


def _dil_merge_backward(outs, lses, do):
    rows = outs[0].shape[0]

    def body(o0, o1, o2, l0, l1, l2, do_ref, d0, d1, d2, t0, t1, t2):
        lo = lax.broadcasted_iota(jnp.int32, (1, LANE), 1) < HEAD_DIM
        ls = [l0[...], l1[...], l2[...]]
        os_ = [o0[...], o1[...], o2[...]]
        m = jnp.maximum(jnp.maximum(ls[0], ls[1]), ls[2])
        es = [jnp.exp(v - m) for v in ls]
        inv = 1.0 / (es[0] + es[1] + es[2])
        alphas = [e * inv for e in es]
        dov = do_ref[...]
        merged = alphas[0] * os_[0] + alphas[1] * os_[1] + alphas[2] * os_[2]
        dot = _head_sums(dov * merged, lo)
        for a, d_ref, t_ref in zip(alphas, (d0, d1, d2), (t0, t1, t2)):
            d_ref[...] = (a * dov).astype(BF)
            t_ref[...] = a * dot

    blk = pl.BlockSpec((ROW_TILE, LANE), lambda i, j: (i, j))
    res = pl.pallas_call(
        body, out_shape=[jax.ShapeDtypeStruct((rows, D_MODEL), BF)] * 3 + [jax.ShapeDtypeStruct((rows, D_MODEL), F32)] * 3,
        grid=(rows // ROW_TILE, 8), in_specs=[blk] * 7, out_specs=[blk] * 6, name="dilated_merge_backward",
        compiler_params=_params(("parallel", "parallel")))(*outs, *lses, do)
    return res[:3], res[3:]


def _rel_bias_grad(dbs, buckets):
    def body(db_ref, bc_ref, bp_ref, o_ref):
        g = pl.program_id(0)
        hp = pl.program_id(1)

        @pl.when((g == 0) & (hp == 0))
        def _():
            o_ref[...] = jnp.zeros_like(o_ref)

        rr = lax.broadcasted_iota(jnp.int32, (REL_BUCKETS, LANE), 0)
        cc = lax.broadcasted_iota(jnp.int32, (REL_BUCKETS, LANE), 1)
        bc = bc_ref[0]
        bp = bp_ref[0]
        acc = jnp.zeros((REL_BUCKETS, LANE), F32)
        for hh in range(2):
            col = g * HEADS + 2 * hp + hh
            d_c = db_ref[0, 0, 2 * hh]
            d_p = db_ref[0, 0, 2 * hh + 1]
            for b in range(REL_BUCKETS):
                val = (jnp.sum(jnp.where(bc == b, d_c, 0.0), keepdims=True)
                       + jnp.sum(jnp.where(bp == b, d_p, 0.0), keepdims=True))
                acc = jnp.where((rr == b) & (cc == col), val, acc)
        o_ref[...] += acc

    db_all = jnp.stack(dbs)
    bc_all = jnp.stack([b[0] for b in buckets])
    bp_all = jnp.stack([b[1] for b in buckets])
    tile = pl.BlockSpec((1, Q_BLOCK, Q_BLOCK), lambda g, hp: (g, 0, 0))
    return pl.pallas_call(
        body, out_shape=jax.ShapeDtypeStruct((REL_BUCKETS, LANE), F32), grid=(3, 8),
        in_specs=[pl.BlockSpec((1, 1, 4, Q_BLOCK, Q_BLOCK), lambda g, hp: (g, hp, 0, 0, 0)), tile, tile],
        out_specs=pl.BlockSpec((REL_BUCKETS, LANE), lambda g, hp: (0, 0)), name="rel_bias_grad",
        compiler_params=_params(("arbitrary", "arbitrary")))(db_all, bc_all, bp_all)


def _mla_forward(hn, w, tables):
    a = _matmul(hn, w["w_a"], name="mla_a")
    cq, ckv, kr = _mla_mid_forward(a, w["q_norm"], w["kv_norm"], tables)
    q_raw = _matmul(cq, w["w_uq"], name="mla_uq")
    q = _rope_heads(q_raw, tables, False, "rope_forward")
    kv = _matmul(ckv, w["w_ukv"], b_chunks=True, out_dtype=BF, name="mla_ukv")
    scale = (HEAD_DIM + MLA_ROPE) ** -0.5
    o, lse = _attn_forward(q, kv, 0, kr, None, None, scale, MLA_GROUP, "mla_attention_forward")
    y = _matmul(o, w["w_o"], name="attn_out")
    return y, dict(hn=hn, a=a, cq=cq, ckv=ckv, kr=kr, q=q, kv=kv, o=o, lse=lse)


def _mla_backward(dy, w, s, tables):
    scale = (HEAD_DIM + MLA_ROPE) ** -0.5
    g = {}
    g["w_o"] = _matmul(s["o"], dy, ta=True, out_dtype=BF, name="attn_out_dw")
    do = _matmul(dy, w["w_o"], tb=True, out_dtype=BF, name="attn_out_dx")
    dq, dkv, dkr = _attn_backward(s["q"], s["kv"], 0, s["kr"], None, None, s["o"], do, s["lse"], scale,
                                  MLA_GROUP, "mla_attention_backward")
    dq_raw = _rope_heads(dq, tables, True, "rope_backward")
    g["w_uq"] = _matmul(s["cq"], dq_raw, ta=True, out_dtype=BF, name="mla_uq_dw")
    dcq = _matmul(dq_raw, w["w_uq"], tb=True, name="mla_uq_dx")
    g["w_ukv"] = _matmul(s["ckv"], dkv, ta=True, out_chunks=True, out_dtype=BF, name="mla_ukv_dw")
    dckv = _matmul(dkv, w["w_ukv"], tb=True, b_chunks=True, name="mla_ukv_dx")
    da, g["q_norm"], g["kv_norm"] = _mla_mid_backward(s["a"], w["q_norm"], w["kv_norm"], tables, dcq, dckv, dkr)
    g["w_a"] = _matmul(s["hn"], da, ta=True, out_dtype=BF, name="mla_a_dw")
    dhn = _matmul(da, w["w_a"], tb=True, name="mla_a_dx")
    return dhn, g


def _fox_forward(hn, w):
    qkv = _matmul(hn, w["w_qkv"], out_dtype=BF, name="fox_qkv")
    f_raw = _matmul(hn, w["w_f"], name="fox_f")
    cum = _forget_forward(f_raw, w["b_f"])
    cum_heads = cum[:, :HEADS].T
    cum_col, cum_row = cum_heads[:, :, None], cum_heads[:, None, :]
    o, lse = _attn_forward(qkv, qkv, HEADS, None, cum_col, cum_row, HEAD_DIM ** -0.5, FOX_GROUP,
                           "fox_attention_forward")
    y = _matmul(o, w["w_o"], name="attn_out")
    return y, dict(hn=hn, qkv=qkv, f_raw=f_raw, cum_col=cum_col, cum_row=cum_row, o=o, lse=lse)


def _fox_backward(dy, w, s):
    g = {}
    g["w_o"] = _matmul(s["o"], dy, ta=True, out_dtype=BF, name="attn_out_dw")
    do = _matmul(dy, w["w_o"], tb=True, out_dtype=BF, name="attn_out_dx")
    dq, dkv, dck, dcq = _attn_backward(s["qkv"], s["qkv"], HEADS, None, s["cum_col"], s["cum_row"], s["o"], do,
                                       s["lse"], HEAD_DIM ** -0.5, FOX_GROUP, "fox_attention_backward")
    dcum = jnp.pad((dck[:, 0, :] + dcq[:, :, 0]).T, ((0, 0), (0, LANE - HEADS)))
    df, g["b_f"] = _forget_backward(s["f_raw"], w["b_f"], dcum)
    dqkv = jnp.concatenate([dq, dkv], axis=1)
    g["w_qkv"] = _matmul(s["hn"], dqkv, ta=True, out_dtype=BF, name="fox_qkv_dw")
    g["w_f"] = _matmul(s["hn"], df, ta=True, out_dtype=BF, name="fox_f_dw")
    dhn = _matmul(dqkv, w["w_qkv"], tb=True, name="fox_qkv_dx")
    dhn = _matmul(df, w["w_f"], tb=True, add=dhn, name="fox_f_dx")
    return dhn, g


def _dil_mixer_forward(hn, w, buckets):
    qkv = _matmul(hn, w["w_qkv"], b_chunks=True, out_dtype=BF, name="dil_qkv")
    views = [_dil_view(qkv, grp, dilation) for grp, (_, dilation) in enumerate(DIL_PATTERNS)]
    outs, lses = [], []
    for grp, (_, dilation) in enumerate(DIL_PATTERNS):
        o_g, lse_g = _dil_forward(views[grp], grp, dilation, w["rel_bias"], buckets[grp])
        outs.append(o_g)
        lses.append(lse_g)
    o = _dil_merge_forward(outs, lses)
    y = _matmul(o, w["w_o"], name="dil_out")
    return y, dict(hn=hn, views=views, outs=outs, lses=lses, o=o)


def _dil_mixer_backward(dy, w, s, buckets):
    g = {}
    g["w_o"] = _matmul(s["o"], dy, ta=True, out_dtype=BF, name="dil_out_dw")
    do = _matmul(dy, w["w_o"], tb=True, name="dil_out_dx")
    do_gs, dlts = _dil_merge_backward(s["outs"], s["lses"], do)
    parts, dbs = [], []
    for grp, (_, dilation) in enumerate(DIL_PATTERNS):
        dq, dk, dv, db = _dil_backward(s["views"][grp], grp, dilation, w["rel_bias"], buckets[grp], do_gs[grp],
                                       s["lses"][grp], dlts[grp])
        parts += [dq, dk, dv]
        dbs.append(db)
    dqkv = jnp.concatenate(parts, axis=1)
    g["rel_bias"] = _rel_bias_grad(dbs, buckets)
    g["w_qkv"] = _matmul(s["hn"], dqkv, ta=True, out_chunks=True, out_dtype=BF, name="dil_qkv_dw")
    dhn = _matmul(dqkv, w["w_qkv"], tb=True, b_chunks=True, name="dil_qkv_dx")
    return dhn, g


def _mixer_weights(i, lw, small):
    mixer, j = i % N_MIXERS, i // N_MIXERS
    if mixer == 0:
        return dict(lw["mixer"], q_norm=small["mla_q_norm"][j][None, :], kv_norm=small["mla_kv_norm"][j][None, :])
    if mixer == 1:
        return dict(lw["mixer"], rel_bias=small["rel_bias"])
    return dict(lw["mixer"], b_f=jnp.pad(small["fox_b_f"][j][None, :], ((0, 0), (0, LANE - HEADS))))


MIXER_PART, COMMON_PART = 0, 1


def _run_layers(x, p, positions, target, get_part, get_small, put_part):
    tables = _rope_tables(positions)
    buckets = [_dil_buckets(d) for _, d in DIL_PATTERNS]
    layers, saved = [], []
    h = x
    first = get_part(0, MIXER_PART, positions)
    small = get_small()

    def gain(i, k):
        return small["norm_g"][i, k][None, :]

    hn = _prenorm(h, gain(0, 0))
    sq = dh = None
    for i in range(DEPTH):
        mixer = i % N_MIXERS
        lw = dict(mixer=first if i == 0 else get_part(i, MIXER_PART, h))
        mw = _mixer_weights(i, lw, small)
        if mixer == 0:
            y, ms = _mla_forward(hn, mw, tables)
        elif mixer == 1:
            y, ms = _dil_mixer_forward(hn, mw, buckets)
        else:
            y, ms = _fox_forward(hn, mw)
        lw.update(get_part(i, COMMON_PART, y))
        layers.append(lw)
        h1, hn2 = _post_residual(h, y, gain(i, 1), gain(i, 2))
        gu = _matmul(hn2, lw["ffn_w_in"], b_chunks=True, out_dtype=BF, name="ffn_in")
        act = _swiglu_forward(gu)
        f = _matmul(act, lw["ffn_w_out"], name="ffn_out")
        h2, h2b = _post_residual(h1, f, gain(i, 3), None)
        pp = _matmul(p[i], lw["ple_w_proj"], b_chunks=True, name="ple_proj")
        z = _matmul(h2b, lw["ple_w_gate"], name="ple_gate")
        saved.append(dict(h=h, y=y, ms=ms, h1=h1, hn2=hn2, gu=gu, act=act, f=f, h2b=h2b, pp=pp, z=z))
        if i + 1 < DEPTH:
            h, hn = _ple_forward(h2, pp, z, gain(i + 1, 0))
        else:
            dh, sq = _ple_loss(h2, pp, z, target)

    norm_rows = [[None] * 4 for _ in range(DEPTH)]
    sg = dict(mla_q_norm={}, mla_kv_norm={}, rel_bias=None, fox_b_f={})
    for i in reversed(range(DEPTH)):
        s, lw = saved[i], layers[i]
        mixer, j = i % N_MIXERS, i // N_MIXERS
        mw = _mixer_weights(i, lw, small)
        lg = {}
        dpp, dz = _ple_backward(dh, s["pp"], s["z"])
        lg["ple_w_proj"] = _matmul(p[i], dpp, ta=True, out_chunks=True, out_dtype=BF, name="ple_proj_dw")
        lg["ple_w_gate"] = _matmul(s["h2b"], dz, ta=True, out_dtype=BF, name="ple_gate_dw")
        dh2 = _matmul(dz, lw["ple_w_gate"], tb=True, add=dh, name="ple_gate_dx")
        df, norm_rows[i][3] = _rms_backward(s["f"], gain(i, 3), dh2, None, BF)
        lg["ffn_w_out"] = _matmul(s["act"], df, ta=True, out_dtype=BF, name="ffn_out_dw")
        dact = _matmul(df, lw["ffn_w_out"], tb=True, out_dtype=BF, name="ffn_out_dx")
        dgu = _swiglu_backward(s["gu"], dact)
        lg["ffn_w_in"] = _matmul(s["hn2"], dgu, ta=True, out_chunks=True, out_dtype=BF, name="ffn_in_dw")
        token = put_part(i, COMMON_PART, lg)
        dhn2 = _matmul(dgu, lw["ffn_w_in"], tb=True, b_chunks=True, name="ffn_in_dx")
        dh1, norm_rows[i][2] = _rms_backward(s["h1"], gain(i, 2), dhn2, dh2, F32)
        dy, norm_rows[i][1] = _rms_backward(s["y"], gain(i, 1) + token[0:1, 0:1], dh1, None, BF)
        if mixer == 0:
            dhn, mg = _mla_backward(dy, mw, s["ms"], tables)
            sg["mla_q_norm"][j] = mg.pop("q_norm")
            sg["mla_kv_norm"][j] = mg.pop("kv_norm")
        elif mixer == 1:
            dhn, mg = _dil_mixer_backward(dy, mw, s["ms"], buckets)
            rel = mg.pop("rel_bias")[:, :3 * HEADS]
            sg["rel_bias"] = rel if sg["rel_bias"] is None else sg["rel_bias"] + rel
        else:
            dhn, mg = _fox_backward(dy, mw, s["ms"])
            sg["fox_b_f"][j] = mg.pop("b_f")[:, :HEADS]
        token = put_part(i, MIXER_PART, mg)
        dh, norm_rows[i][0] = _rms_backward(s["h"], gain(i, 0) + token[0:1, 0:1], dhn, dh1, F32)
    small_grads = dict(norm_g=jnp.stack([jnp.concatenate(row, axis=0) for row in norm_rows]),
                       rel_bias=sg["rel_bias"])
    for k in ("mla_q_norm", "mla_kv_norm", "fox_b_f"):
        small_grads[k] = jnp.concatenate([sg[k][j] for j in sorted(sg[k])], axis=0)
    return sq, dh, small_grads


COL_SHARDED = ("ffn_w_in", "ple_w_proj", "mla_w_uq", "mla_w_ukv", "dil_w_qkv", "fox_w_qkvf")
ROW_SHARDED = ("ffn_w_out", "ple_w_gate", "mla_w_a", "mla_w_o", "dil_w_o", "fox_w_o")
BIG = ("ffn_w_in", "ffn_w_out", "ple_w_proj", "ple_w_gate", "mla_w_a", "mla_w_uq", "mla_w_ukv", "mla_w_o",
       "dil_w_qkv", "dil_w_o", "fox_w_qkvf", "fox_w_o")
SMALL_SHARDED = ("norm_g", "mla_q_norm", "mla_kv_norm")
SMALL_REPLICATED = ("rel_bias", "fox_b_f")
WEIGHTS = ("norm_g", "ffn_w_in", "ffn_w_out", "ple_w_proj", "ple_w_gate", "rel_bias", "mla_w_a", "mla_q_norm",
           "mla_kv_norm", "mla_w_uq", "mla_w_ukv", "mla_w_o", "dil_w_qkv", "dil_w_o", "fox_w_qkvf", "fox_b_f", "fox_w_o")


LAYER_COMMON = ("ffn_w_in", "ffn_w_out", "ple_w_proj", "ple_w_gate")
MIXER_WEIGHTS = (("mla_w_a", "mla_w_uq", "mla_w_ukv", "mla_w_o"), ("dil_w_qkv", "dil_w_o"), ("fox_w_qkvf", "fox_w_o"))


def _part_names(i, part):
    return MIXER_WEIGHTS[i % N_MIXERS] if part == MIXER_PART else LAYER_COMMON


def _layer_slot(name, i):
    return i if name in LAYER_COMMON else i // N_MIXERS


def _merge_rows(chunks):
    n, r, c = chunks.shape
    return chunks.reshape(n * r, c)


def _merge_cols(chunks):
    n, r, c = chunks.shape
    return chunks.transpose(1, 0, 2).reshape(r, n * c)


def _pad_heads_out(wo):
    w3 = wo.reshape(HEADS, HEAD_DIM, D_MODEL)
    return jnp.pad(w3, ((0, 0), (HEAD_DIM, 0), (0, 0))).reshape(HEADS * LANE, D_MODEL)


def _part_to_compute(i, part, ch):
    if part == COMMON_PART:
        return dict(ffn_w_in=ch["ffn_w_in"], ffn_w_out=_merge_rows(ch["ffn_w_out"]), ple_w_proj=ch["ple_w_proj"],
                    ple_w_gate=_merge_rows(ch["ple_w_gate"]))
    lw = {}
    mixer = i % N_MIXERS
    if mixer == 0:
        wa = _merge_rows(ch["mla_w_a"])
        rank = MLA_Q_RANK + MLA_KV_RANK
        wa_p = jnp.concatenate([wa[:, :rank], jnp.zeros((wa.shape[0], 64), wa.dtype), wa[:, rank:],
                                jnp.zeros((wa.shape[0], 32), wa.dtype)], axis=1)
        wuq = _merge_cols(ch["mla_w_uq"]).reshape(MLA_Q_RANK, HEADS, HEAD_DIM + MLA_ROPE)
        wuq_p = jnp.pad(wuq, ((0, 0), (0, 0), (0, LANE - HEAD_DIM - MLA_ROPE))).reshape(MLA_Q_RANK, HEADS * LANE)
        lw["mixer"] = dict(w_a=wa_p, w_uq=wuq_p, w_ukv=ch["mla_w_ukv"], w_o=_pad_heads_out(_merge_rows(ch["mla_w_o"])))
    elif mixer == 1:
        lw["mixer"] = dict(w_qkv=ch["dil_w_qkv"], w_o=_merge_rows(ch["dil_w_o"]))
    else:
        wf = _merge_cols(ch["fox_w_qkvf"])
        inner = HEADS * HEAD_DIM
        q3 = wf[:, :inner].reshape(D_MODEL, HEADS, HEAD_DIM)
        k3 = wf[:, inner:2 * inner].reshape(D_MODEL, HEADS, HEAD_DIM)
        v3 = wf[:, 2 * inner:3 * inner].reshape(D_MODEL, HEADS, HEAD_DIM)
        q_p = jnp.pad(q3, ((0, 0), (0, 0), (0, HEAD_DIM))).reshape(D_MODEL, HEADS * LANE)
        kv_p = jnp.concatenate([k3, v3], axis=2).reshape(D_MODEL, HEADS * LANE)
        f_p = jnp.pad(wf[:, 3 * inner:], ((0, 0), (0, LANE - HEADS)))
        lw["mixer"] = dict(w_qkv=jnp.concatenate([q_p, kv_p], axis=1), w_f=f_p,
                           w_o=_pad_heads_out(_merge_rows(ch["fox_w_o"])))
    return lw["mixer"]


def _part_contributions(i, part, lg, chunk_shapes):
    spec = {k: jax.ShapeDtypeStruct(s, BF) for k, s in chunk_shapes.items()}
    (contrib,) = jax.linear_transpose(functools.partial(_part_to_compute, i, part), spec)(lg)
    return contrib


def _chip_peers():
    x, y, c = lax.axis_index("x"), lax.axis_index("y"), lax.axis_index("c")
    peers = [(1 - x, y), (x, 1 - y), (1 - x, 1 - y)]
    return x, y, c, peers


SEM_SPEC = pl.BlockSpec(memory_space=pltpu.SEMAPHORE)
ANY_SPEC = pl.BlockSpec(memory_space=pl.ANY)
SPLIT_EFFECT = pltpu.SideEffectType.DATAFLOW_SIDE_EFFECTING


def _own_slot(shard):
    me = 2 * lax.axis_index("x") + lax.axis_index("y")
    return lax.dynamic_update_index_in_dim(lax.empty((N_CHIPS,) + shard.shape, shard.dtype), shard[None], me, 0)


def _spread_copy(src, land, k, peer, c, send_sems, recv_sems, index, src_slot, slot):
    px, py = peer
    return pltpu.make_async_remote_copy(
        src_ref=src.at[src_slot], dst_ref=land.at[slot],
        send_sem=send_sems.at[3 * index + k], recv_sem=recv_sems.at[3 * index + k],
        device_id=(px, py, c), device_id_type=MESH)


def _spread_start(bufs, srcs, after, name):
    n = len(bufs)
    exchange = srcs is not None
    arrays = (list(srcs) if exchange else []) + list(bufs)
    na = len(arrays)

    def body(*refs):
        src, land = refs[:n], refs[na - n:na]
        send_sems, recv_sems = refs[na + 1], refs[na + 2]
        token = refs[-1]
        x, y, c, peers = _chip_peers()
        me = 2 * x + y
        for w in range(n):
            for k, peer in enumerate(peers):
                src_slot = 2 * peer[0] + peer[1] if exchange else me
                _spread_copy(src[w], land[w], k, peer, c, send_sems, recv_sems, w, src_slot, me).start()
        token[...] = jnp.zeros_like(token)

    hbm = [pltpu.with_memory_space_constraint(a, pltpu.HBM) for a in arrays]
    out = pl.pallas_call(
        body, name=name,
        out_shape=(pltpu.SemaphoreType.DMA((3 * n,)), pltpu.SemaphoreType.DMA((3 * n,)),
                   *[pltpu.HBM(a.shape, a.dtype) for a in hbm], jax.ShapeDtypeStruct((8, LANE), F32)),
        in_specs=[HBM_SPEC] * na + [ANY_SPEC],
        out_specs=(SEM_SPEC, SEM_SPEC, *[HBM_SPEC] * na, pl.BlockSpec(memory_space=pltpu.VMEM)),
        input_output_aliases={w: 2 + w for w in range(na)},
        compiler_params=pltpu.CompilerParams(has_side_effects=SPLIT_EFFECT))(*hbm, after)
    return dict(send=out[0], recv=out[1], arrays=out[2:2 + na], n=n, token=out[-1], exchange=exchange)


def _spread_wait(handle, after, name):
    n, exchange = handle["n"], handle["exchange"]
    arrays = list(handle["arrays"])
    na = len(arrays)

    def body(*refs):
        src, land = refs[:n], refs[na - n:na]
        send_sems, recv_sems = refs[na], refs[na + 1]
        x, y, c, peers = _chip_peers()
        me = 2 * x + y
        for w in range(n):
            for k, peer in enumerate(peers):
                there = 2 * peer[0] + peer[1]
                cp = _spread_copy(src[w], land[w], k, peer, c, send_sems, recv_sems, w, there if exchange else me, there)
                cp.wait_send()
                cp.wait_recv()

    out = pl.pallas_call(
        body, name=name, out_shape=tuple(pltpu.HBM(a.shape, a.dtype) for a in arrays),
        in_specs=[HBM_SPEC] * na + [SEM_SPEC, SEM_SPEC, ANY_SPEC], out_specs=tuple([HBM_SPEC] * na),
        input_output_aliases={w: w for w in range(na)},
        compiler_params=pltpu.CompilerParams(has_side_effects=SPLIT_EFFECT))(*arrays, handle["send"], handle["recv"], after)
    return (list(out[n:]), list(out[:n])) if exchange else list(out)


def _sibling_copy(received, sent, land, k, me, peers, sibling, send_sems, recv_sems, index):
    slot = me if k == 3 else 2 * peers[k][0] + peers[k][1]
    src = sent if k == 3 else received
    return pltpu.make_async_remote_copy(
        src_ref=src.at[slot], dst_ref=land.at[slot], send_sem=send_sems.at[4 * index + k],
        recv_sem=recv_sems.at[4 * index + k], device_id=sibling, device_id_type=MESH)


def _sibling_start(received, sent, after, name):
    n = len(received)
    lands = [lax.empty(a.shape, a.dtype) for a in received]
    arrays = list(received) + list(sent) + lands

    def body(*refs):
        rec, snt, land = refs[:n], refs[n:2 * n], refs[2 * n:3 * n]
        send_sems, recv_sems = refs[3 * n + 1], refs[3 * n + 2]
        token = refs[-1]
        x, y, c, peers = _chip_peers()
        for w in range(n):
            for k in range(4):
                _sibling_copy(rec[w], snt[w], land[w], k, 2 * x + y, peers, (x, y, 1 - c), send_sems, recv_sems, w).start()
        token[...] = jnp.zeros_like(token)

    hbm = [pltpu.with_memory_space_constraint(a, pltpu.HBM) for a in arrays]
    out = pl.pallas_call(
        body, name=name,
        out_shape=(pltpu.SemaphoreType.DMA((4 * n,)), pltpu.SemaphoreType.DMA((4 * n,)),
                   *[pltpu.HBM(a.shape, a.dtype) for a in hbm], jax.ShapeDtypeStruct((8, LANE), F32)),
        in_specs=[HBM_SPEC] * (3 * n) + [ANY_SPEC],
        out_specs=(SEM_SPEC, SEM_SPEC, *[HBM_SPEC] * (3 * n), pl.BlockSpec(memory_space=pltpu.VMEM)),
        input_output_aliases={w: 2 + w for w in range(3 * n)},
        compiler_params=pltpu.CompilerParams(has_side_effects=SPLIT_EFFECT))(*hbm, after)
    return dict(send=out[0], recv=out[1], arrays=out[2:2 + 3 * n], n=n, token=out[-1])


def _sibling_wait(handle, after, name):
    n = handle["n"]
    arrays = list(handle["arrays"])

    def body(*refs):
        rec, snt, land = refs[:n], refs[n:2 * n], refs[2 * n:3 * n]
        send_sems, recv_sems = refs[3 * n], refs[3 * n + 1]
        x, y, c, peers = _chip_peers()
        for w in range(n):
            for k in range(4):
                cp = _sibling_copy(rec[w], snt[w], land[w], k, 2 * x + y, peers, (x, y, 1 - c), send_sems, recv_sems, w)
                cp.wait_send()
                cp.wait_recv()

    out = pl.pallas_call(
        body, name=name, out_shape=tuple(pltpu.HBM(a.shape, a.dtype) for a in arrays),
        in_specs=[HBM_SPEC] * (3 * n) + [SEM_SPEC, SEM_SPEC, ANY_SPEC], out_specs=tuple([HBM_SPEC] * (3 * n)),
        input_output_aliases={w: w for w in range(3 * n)},
        compiler_params=pltpu.CompilerParams(has_side_effects=SPLIT_EFFECT))(*arrays, handle["send"], handle["recv"], after)
    return list(out[:n]), list(out[n:2 * n]), list(out[2 * n:])


def _all_reduce_small(v):
    rows = v.shape[0]

    def body(v_ref, sum_ref, slots, send_sems, recv_sems):
        x, y, c = lax.axis_index("x"), lax.axis_index("y"), lax.axis_index("c")
        me = 4 * x + 2 * y + c
        slots[me] = v_ref[...]
        sends = []
        for k in range(1, N_DEV):
            bx, by, bc = (k >> 2) & 1, (k >> 1) & 1, k & 1
            peer = (x ^ bx, y ^ by, c ^ bc)
            rc = pltpu.make_async_remote_copy(src_ref=v_ref, dst_ref=slots.at[me], send_sem=send_sems.at[k],
                                              recv_sem=recv_sems.at[k], device_id=peer, device_id_type=MESH)
            rc.start()
            sends.append(rc)
        for k in range(1, N_DEV):
            bx, by, bc = (k >> 2) & 1, (k >> 1) & 1, k & 1
            src = 4 * (x ^ bx) + 2 * (y ^ by) + (c ^ bc)
            pltpu.make_async_remote_copy(src_ref=v_ref, dst_ref=slots.at[src], send_sem=send_sems.at[k],
                                         recv_sem=recv_sems.at[k], device_id=(x ^ bx, y ^ by, c ^ bc),
                                         device_id_type=MESH).wait_recv()
        for rc in sends:
            rc.wait_send()
        total = slots[0]
        for k in range(1, N_DEV):
            total = total + slots[k]
        sum_ref[...] = total

    vm = pl.BlockSpec(memory_space=pltpu.VMEM)
    return pl.pallas_call(
        body, out_shape=jax.ShapeDtypeStruct((rows, LANE), F32), in_specs=[vm], out_specs=vm,
        scratch_shapes=[pltpu.VMEM((N_DEV, rows, LANE), F32), pltpu.SemaphoreType.DMA((N_DEV,)),
                        pltpu.SemaphoreType.DMA((N_DEV,))], name="all_reduce_small")(v)


def _as_2d(a):
    return a.reshape(-1, a.shape[-1])


def _row_tile(rows, cols):
    for t in (512, 256, 128, 64, 32, 16):
        if rows % t == 0 and t * cols * 4 <= (1 << 20):
            return t
    return rows


def _adamw_layer(w, m, v, received, sent, sibling, outs, slot):
    _, rows, cols = received.shape
    tr = _row_tile(rows, cols)
    first = slot * (rows // tr)
    where = jnp.stack([2 * lax.axis_index("x") + lax.axis_index("y"), lax.axis_index("c")]).astype(jnp.int32)

    def body(where_ref, w_ref, m_ref, v_ref, r_ref, own_ref, s_ref, *rest):
        g_ref, d_ref, nm_ref, nv_ref = rest[4:]
        me, core = where_ref[0], where_ref[1]
        mine = [jnp.where(me == k, own_ref[...], r_ref[k]).astype(F32) for k in range(N_CHIPS)]
        theirs = [s_ref[k].astype(F32) for k in range(N_CHIPS)]
        g = None
        for k in range(N_CHIPS):
            part = jnp.where(core == 0, mine[k], theirs[k])
            g = part if g is None else g + part
        for k in range(N_CHIPS):
            g = g + jnp.where(core == 0, theirs[k], mine[k])
        delta, nm, nv = _adamw_math(w_ref[...], g, m_ref[...], v_ref[...])
        g_ref[...] = g
        d_ref[...] = delta
        nm_ref[...] = nm
        nv_ref[...] = nv

    stacked = pl.BlockSpec((tr, cols), lambda i, where_ref: (first + i, 0))
    four = pl.BlockSpec((N_CHIPS, tr, cols), lambda i, where_ref: (0, i, 0))
    own = pl.BlockSpec((None, tr, cols), lambda i, where_ref: (where_ref[0], i, 0))
    grid_spec = pltpu.PrefetchScalarGridSpec(
        num_scalar_prefetch=1, grid=(rows // tr,),
        in_specs=[stacked, stacked, stacked, four, own, four] + [ANY_SPEC] * 4, out_specs=[stacked] * 4)
    return pl.pallas_call(body, out_shape=[jax.ShapeDtypeStruct(w.shape, F32)] * 4, grid_spec=grid_spec,
                          input_output_aliases={7 + k: k for k in range(4)}, name="adamw_layer",
                          compiler_params=_params(("parallel",)))(where, w, m, v, received, sent, sibling, *outs)


def _adamw_math(w, g, m, v):
    m = ADAM_B1 * m + (1.0 - ADAM_B1) * g
    v = ADAM_B2 * v + (1.0 - ADAM_B2) * (g * g)
    m_hat = m / (1.0 - ADAM_B1 ** ADAM_STEP)
    v_hat = v / (1.0 - ADAM_B2 ** ADAM_STEP)
    delta = -ADAM_LR * (m_hat / (jnp.sqrt(v_hat) + ADAM_EPS) + ADAM_WD * w)
    return delta, m, v


def _adamw(w, m, v, g_mine, g_sibling):
    rows, cols = w.shape
    tr = _row_tile(rows, cols)
    two = g_sibling is not None

    def body(*refs):
        if two:
            w_ref, m_ref, v_ref, ga_ref, gb_ref, g_ref, d_ref, nm_ref, nv_ref = refs
            g = ga_ref[...] + gb_ref[...]
        else:
            w_ref, m_ref, v_ref, ga_ref, g_ref, d_ref, nm_ref, nv_ref = refs
            g = ga_ref[...]
        delta, nm, nv = _adamw_math(w_ref[...], g, m_ref[...], v_ref[...])
        g_ref[...] = g
        d_ref[...] = delta
        nm_ref[...] = nm
        nv_ref[...] = nv

    blk = pl.BlockSpec((tr, cols), lambda i: (i, 0))
    args = [w, m, v, g_mine] + ([g_sibling] if two else [])
    return pl.pallas_call(body, out_shape=[jax.ShapeDtypeStruct((rows, cols), F32)] * 4, grid=(rows // tr,),
                          in_specs=[blk] * len(args), out_specs=[blk] * 4, name="adamw",
                          compiler_params=_params(("parallel",)))(*args)


def _pack_rows(arrays):
    flat = jnp.concatenate([a.reshape(-1) for a in arrays])
    rows = -(-flat.shape[0] // (8 * LANE)) * 8
    return jnp.pad(flat, (0, rows * LANE - flat.shape[0])).reshape(rows, LANE)


def _unpack_rows(packed, shapes):
    flat = packed.reshape(-1)
    out, at = [], 0
    for s in shapes:
        size = math.prod(s)
        out.append(flat[at:at + size].reshape(s))
        at += size
    return out


def kernel(x, p, positions, norm_g, ffn_w_in, ffn_w_out, ple_w_proj, ple_w_gate, rel_bias, mla_w_a, mla_q_norm, mla_kv_norm, mla_w_uq, mla_w_ukv, mla_w_o, dil_w_qkv, dil_w_o, fox_w_qkvf, fox_b_f, fox_w_o, loss_target, m_norm_g, m_ffn_w_in, m_ffn_w_out, m_ple_w_proj, m_ple_w_gate, m_rel_bias, m_mla_w_a, m_mla_q_norm, m_mla_kv_norm, m_mla_w_uq, m_mla_w_ukv, m_mla_w_o, m_dil_w_qkv, m_dil_w_o, m_fox_w_qkvf, m_fox_b_f, m_fox_w_o, v_norm_g, v_ffn_w_in, v_ffn_w_out, v_ple_w_proj, v_ple_w_gate, v_rel_bias, v_mla_w_a, v_mla_q_norm, v_mla_kv_norm, v_mla_w_uq, v_mla_w_ukv, v_mla_w_o, v_dil_w_qkv, v_dil_w_o, v_fox_w_qkvf, v_fox_b_f, v_fox_w_o):
    w = dict(norm_g=norm_g, ffn_w_in=ffn_w_in, ffn_w_out=ffn_w_out, ple_w_proj=ple_w_proj, ple_w_gate=ple_w_gate,
             rel_bias=rel_bias, mla_w_a=mla_w_a, mla_q_norm=mla_q_norm, mla_kv_norm=mla_kv_norm, mla_w_uq=mla_w_uq,
             mla_w_ukv=mla_w_ukv, mla_w_o=mla_w_o, dil_w_qkv=dil_w_qkv, dil_w_o=dil_w_o, fox_w_qkvf=fox_w_qkvf,
             fox_b_f=fox_b_f, fox_w_o=fox_w_o)
    m = dict(norm_g=m_norm_g, ffn_w_in=m_ffn_w_in, ffn_w_out=m_ffn_w_out, ple_w_proj=m_ple_w_proj,
             ple_w_gate=m_ple_w_gate, rel_bias=m_rel_bias, mla_w_a=m_mla_w_a, mla_q_norm=m_mla_q_norm,
             mla_kv_norm=m_mla_kv_norm, mla_w_uq=m_mla_w_uq, mla_w_ukv=m_mla_w_ukv, mla_w_o=m_mla_w_o,
             dil_w_qkv=m_dil_w_qkv, dil_w_o=m_dil_w_o, fox_w_qkvf=m_fox_w_qkvf, fox_b_f=m_fox_b_f, fox_w_o=m_fox_w_o)
    v = dict(norm_g=v_norm_g, ffn_w_in=v_ffn_w_in, ffn_w_out=v_ffn_w_out, ple_w_proj=v_ple_w_proj,
             ple_w_gate=v_ple_w_gate, rel_bias=v_rel_bias, mla_w_a=v_mla_w_a, mla_q_norm=v_mla_q_norm,
             mla_kv_norm=v_mla_kv_norm, mla_w_uq=v_mla_w_uq, mla_w_ukv=v_mla_w_ukv, mla_w_o=v_mla_w_o,
             dil_w_qkv=v_dil_w_qkv, dil_w_o=v_dil_w_o, fox_w_qkvf=v_fox_w_qkvf, fox_b_f=v_fox_b_f, fox_w_o=v_fox_w_o)
    chip = 2 * lax.axis_index("x") + lax.axis_index("y")

    small_shapes = [w[k].shape for k in SMALL_SHARDED]
    order = [(i, part) for i in range(DEPTH) for part in (MIXER_PART, COMMON_PART)]
    gathers = {}
    after = positions
    for i, part in order:
        bufs = [_own_slot(w[k][_layer_slot(k, i)].astype(BF)) for k in _part_names(i, part)]
        if (i, part) == order[0]:
            bufs.append(_own_slot(_pack_rows([w[k] for k in SMALL_SHARDED])))
        gathers[i, part] = _spread_start(bufs, None, after, f"gather_start_{i}_{part}")
        after = gathers[i, part]["token"]
    all_started = after
    state = {}

    def get_part(i, part, after_array):
        is_first = (i, part) == order[0]
        lands = _spread_wait(gathers[i, part], all_started if is_first else after_array, f"gather_wait_{i}_{part}")
        if is_first:
            pieces = [_unpack_rows(lands[-1][k], small_shapes) for k in range(N_CHIPS)]
            small = {name: jnp.concatenate([pieces[k][idx] for k in range(N_CHIPS)], axis=-1)
                     for idx, name in enumerate(SMALL_SHARDED)}
            state["small"] = dict(small, rel_bias=rel_bias, fox_b_f=fox_b_f)
        chunks = dict(zip(_part_names(i, part), lands))
        state[i, part] = {k: a.shape for k, a in chunks.items()}
        return _part_to_compute(i, part, chunks)

    started, forwards = [], {}

    def forward_oldest(after_array):
        i, part, handle = started.pop(0)
        received, sent = _spread_wait(handle, after_array, f"exchange_wait_{i}_{part}")
        forwards[i, part] = _sibling_start(received, sent, after_array, f"sibling_start_{i}_{part}")
        return forwards[i, part]["token"]

    def put_part(i, part, lg):
        contrib = _part_contributions(i, part, lg, state[i, part])
        srcs = [contrib[k] for k in _part_names(i, part)]
        handle = _spread_start([lax.empty(s.shape, s.dtype) for s in srcs], srcs, positions,
                               f"exchange_start_{i}_{part}")
        token = handle["token"]
        if started:
            token = token + forward_oldest(token)
        started.append((i, part, handle))
        return token

    sq, grad_x, sg = _run_layers(x[0], p[:, 0], positions[0], loss_target[0], get_part, lambda: state["small"],
                                 put_part)
    loss = lax.psum(0.5 / D_MODEL * jnp.sum(sq), ("x", "y", "c"))
    forward_oldest(grad_x)

    outs = {k: [lax.empty(_as_2d(w[k]).shape, F32) for _ in range(4)] for k in BIG}
    for i, part in [(i, part) for i in reversed(range(DEPTH)) for part in (COMMON_PART, MIXER_PART)]:
        received, sent, sibling = _sibling_wait(forwards[i, part], grad_x, f"sibling_wait_{i}_{part}")
        for k, r, s, t in zip(_part_names(i, part), received, sent, sibling):
            outs[k] = _adamw_layer(_as_2d(w[k]), _as_2d(m[k]), _as_2d(v[k]), r, s, t, outs[k], _layer_slot(k, i))
    results = {k: [o.reshape(w[k].shape) for o in outs[k]] for k in BIG}

    small_all = SMALL_SHARDED + SMALL_REPLICATED
    full_shapes = [sg[k].shape for k in small_all]
    reduced = dict(zip(small_all, _unpack_rows(_all_reduce_small(_pack_rows([sg[k] for k in small_all])), full_shapes)))
    local_g = []
    for k in small_all:
        g = reduced[k]
        if k in SMALL_SHARDED:
            width = w[k].shape[-1]
            g = lax.dynamic_slice_in_dim(g, chip * width, width, axis=g.ndim - 1)
        local_g.append(g)
    local_shapes = [w[k].shape for k in small_all]
    outs = _adamw(_pack_rows([w[k] for k in small_all]), _pack_rows([m[k] for k in small_all]),
                  _pack_rows([v[k] for k in small_all]), _pack_rows(local_g), None)
    unpacked = [_unpack_rows(o, local_shapes) for o in outs]
    for idx, k in enumerate(small_all):
        results[k] = [u[idx] for u in unpacked]

    return (loss, grad_x[None], *[results[k][0] for k in WEIGHTS], *[results[k][1] for k in WEIGHTS],
            *[results[k][2] for k in WEIGHTS], *[results[k][3] for k in WEIGHTS])
```

```python
import functools
import math

import jax
import jax.numpy as jnp
from jax import lax
from jax.experimental import pallas as pl
from jax.experimental.pallas import tpu as pltpu

F32 = jnp.float32
BF = jnp.bfloat16
MESH = pl.DeviceIdType.MESH
HBM_SPEC = pl.BlockSpec(memory_space=pltpu.HBM)

D_MODEL = 1024
DEPTH = 4
N_MIXERS = 3
D_FF = 2816
NORM_EPS = 1e-6
NEG_INF = -1e30
LANE = 128
HEADS = 16
HEAD_DIM = 64
MLA_Q_RANK = 384
MLA_KV_RANK = 256
MLA_ROPE = 32
MLA_A_PAD = 768
ROPE_THETA = 10000.0
DIL_PATTERNS = ((128, 1), (512, 4), (2048, 16))
Q_BLOCK = 128
DIL_PAIRS = 2
REL_BUCKETS = 32
REL_MAX_DIST = 2048
N_CHIPS = 4
N_DEV = 8

ADAM_LR = 0.001
ADAM_B1 = 0.9
ADAM_B2 = 0.999
ADAM_EPS = 1e-08
ADAM_WD = 0.01
ADAM_STEP = 10

VMEM_LIMIT = 56 * 1024 * 1024
MATMUL_VMEM_BUDGET = 36 * 1024 * 1024
ROW_TILE = 256
ATTN_TILE = 256
ATTN_Q_TILE = 512
MLA_GROUP = 4
FOX_GROUP = 4


def _params(sem=None):
    return pltpu.CompilerParams(dimension_semantics=sem, vmem_limit_bytes=VMEM_LIMIT)


def _divisor_tiles(dim):
    tiles = [t for t in range(LANE, dim + 1, LANE) if dim % t == 0]
    return tiles or [dim]


def _matmul_tiles(m, n, k, a_bytes, b_bytes, out_bytes, has_add, n_unit=None, k_unit=None):
    best = None
    for tm in _divisor_tiles(m):
        for tn in _divisor_tiles(n_unit or n):
            for tk in _divisor_tiles(k_unit or k):
                if max(tm, tn, tk) > 2048:
                    continue
                vmem = 2 * (tm * tk * a_bytes + tk * tn * b_bytes + tm * tn * out_bytes) + tm * tn * 4
                if has_add:
                    vmem += 2 * tm * tn * 4
                if vmem > MATMUL_VMEM_BUDGET:
                    continue
                steps = (m // tm) * (n // tn) * (k // tk)
                traffic = m * k * a_bytes * (n // tn) + k * n * b_bytes * (m // tm) + m * n * out_bytes
                cost = traffic / 3.0e12 + steps * 0.4e-6
                if best is None or cost < best[0]:
                    best = (cost, tm, tn, tk)
    return best[1:]


def _matmul(a, b, *, ta=False, tb=False, b_chunks=False, out_chunks=False, add=None, out_dtype=F32, name):
    k, m = a.shape if ta else a.shape[::-1]
    n_unit = k_unit = None
    if b_chunks:
        chunks, rows_w, c = b.shape
        if tb:
            kb, n, k_unit = chunks * c, rows_w, c
        else:
            kb, n, n_unit = rows_w, chunks * c, c
    else:
        kb, n = b.shape[::-1] if tb else b.shape
    if out_chunks:
        assert n % N_CHIPS == 0 and add is None
        n_unit = n // N_CHIPS
    assert k == kb, (a.shape, b.shape, ta, tb)
    tm, tn, tk = _matmul_tiles(m, n, k, a.dtype.itemsize, b.dtype.itemsize, jnp.dtype(out_dtype).itemsize,
                               add is not None, n_unit, k_unit)
    nk = k // tk
    dims = (((0 if ta else 1,), (1 if tb else 0,)), ((), ()))

    def body(*refs):
        if add is None:
            a_ref, b_ref, o_ref, acc_ref = refs
            add_ref = None
        else:
            a_ref, b_ref, add_ref, o_ref, acc_ref = refs
        kk = pl.program_id(2)

        @pl.when(kk == 0)
        def _():
            acc_ref[...] = jnp.zeros_like(acc_ref)

        acc_ref[...] += lax.dot_general(a_ref[...].astype(BF), b_ref[...].astype(BF), dims,
                                        preferred_element_type=F32)

        @pl.when(kk == nk - 1)
        def _():
            r = acc_ref[...]
            if add_ref is not None:
                r = r + add_ref[...].astype(F32)
            o_ref[...] = r.astype(out_dtype)

    a_spec = pl.BlockSpec((tk, tm), lambda i, j, q: (q, i)) if ta else pl.BlockSpec((tm, tk), lambda i, j, q: (i, q))
    if b_chunks and tb:
        per_k = k_unit // tk
        b_spec = pl.BlockSpec((None, tn, tk), lambda i, j, q: (q // per_k, j, q % per_k))
    elif b_chunks:
        per_n = n_unit // tn
        b_spec = pl.BlockSpec((None, tk, tn), lambda i, j, q: (j // per_n, q, j % per_n))
    elif tb:
        b_spec = pl.BlockSpec((tn, tk), lambda i, j, q: (j, q))
    else:
        b_spec = pl.BlockSpec((tk, tn), lambda i, j, q: (q, j))
    if out_chunks:
        per_o = n_unit // tn
        o_spec = pl.BlockSpec((None, tm, tn), lambda i, j, q: (j // per_o, i, j % per_o))
        out_shape = jax.ShapeDtypeStruct((N_CHIPS, m, n_unit), out_dtype)
    else:
        o_spec = pl.BlockSpec((tm, tn), lambda i, j, q: (i, j))
        out_shape = jax.ShapeDtypeStruct((m, n), out_dtype)
    in_specs = [a_spec, b_spec]
    args = [a, b]
    if add is not None:
        in_specs.append(o_spec)
        args.append(add)
    return pl.pallas_call(
        body, out_shape=out_shape, grid=(m // tm, n // tn, nk),
        in_specs=in_specs, out_specs=o_spec, scratch_shapes=[pltpu.VMEM((tm, tn), F32)], name=name,
        compiler_params=_params(("parallel", "parallel", "arbitrary")))(*args)


def _rowwise(body, name, rows, ins, outs, tr=ROW_TILE):
    def row_spec(cols):
        return pl.BlockSpec((tr, cols), lambda i: (i, 0))

    def full_spec(shape):
        zeros = (0,) * len(shape)
        return pl.BlockSpec(shape, lambda i: zeros)

    in_specs = [row_spec(a.shape[1]) if kind == "row" else full_spec(a.shape) for a, kind in ins]
    out_specs = [row_spec(shape[1]) if kind == "row" else full_spec(shape) for shape, _, kind in outs]
    out_shape = [jax.ShapeDtypeStruct(shape, dtype) for shape, dtype, _ in outs]
    return pl.pallas_call(body, out_shape=out_shape, grid=(rows // tr,), in_specs=in_specs, out_specs=out_specs,
                          name=name, compiler_params=_params(("arbitrary",)))(*[a for a, _ in ins])


def _rstd(x):
    return lax.rsqrt(jnp.mean(x * x, axis=-1, keepdims=True) + NORM_EPS)


def _rms_bwd_math(x, g, dy):
    r = _rstd(x)
    gd = dy * g
    dx = r * gd - x * (r * r * r) * jnp.mean(gd * x, axis=-1, keepdims=True)
    dg = jnp.sum(dy * x * r, axis=0, keepdims=True)
    return dx, dg


def _sigmoid(x):
    return 0.5 * jnp.tanh(0.5 * x) + 0.5


def _init_acc(*refs):
    @pl.when(pl.program_id(0) == 0)
    def _():
        for r in refs:
            r[...] = jnp.zeros_like(r)


def _prenorm(h, g):
    rows, cols = h.shape

    def body(h_ref, g_ref, o_ref):
        x = h_ref[...]
        o_ref[...] = (x * _rstd(x) * g_ref[...]).astype(BF)

    return _rowwise(body, "prenorm", rows, [(h, "row"), (g, "full")], [((rows, cols), BF, "row")])[0]


def _post_residual(h, y, g_post, g_pre):
    rows, cols = h.shape
    with_pre = g_pre is not None

    def body(*refs):
        if with_pre:
            h_ref, y_ref, gp_ref, gq_ref, hn_ref, hb_ref = refs
        else:
            h_ref, y_ref, gp_ref, hn_ref, hb_ref = refs
        yv = y_ref[...]
        hn = h_ref[...] + yv * _rstd(yv) * gp_ref[...]
        hn_ref[...] = hn
        hb_ref[...] = (hn * _rstd(hn) * gq_ref[...] if with_pre else hn).astype(BF)

    ins = [(h, "row"), (y, "row"), (g_post, "full")] + ([(g_pre, "full")] if with_pre else [])
    return _rowwise(body, "post_residual_pre" if with_pre else "post_residual", rows, ins,
                    [((rows, cols), F32, "row"), ((rows, cols), BF, "row")])


def _ple_forward(h2, pp, z, g_pre):
    rows, cols = h2.shape

    def body(h_ref, p_ref, z_ref, g_ref, h3_ref, hb_ref):
        h3 = h_ref[...] + p_ref[...] * _sigmoid(z_ref[...])
        h3_ref[...] = h3
        hb_ref[...] = (h3 * _rstd(h3) * g_ref[...]).astype(BF)

    return _rowwise(body, "ple_forward", rows, [(h2, "row"), (pp, "row"), (z, "row"), (g_pre, "full")],
                    [((rows, cols), F32, "row"), ((rows, cols), BF, "row")])


def _ple_loss(h2, pp, z, target):
    rows, cols = h2.shape

    def body(h_ref, p_ref, z_ref, t_ref, dh_ref, sq_ref):
        _init_acc(sq_ref)
        err = h_ref[...] + p_ref[...] * _sigmoid(z_ref[...]) - t_ref[...]
        dh_ref[...] = err * (1.0 / cols)
        sq_ref[...] += jnp.sum(err * err, axis=0, keepdims=True)

    return _rowwise(body, "ple_loss", rows, [(h2, "row"), (pp, "row"), (z, "row"), (target, "row")],
                    [((rows, cols), F32, "row"), ((1, cols), F32, "acc")])


def _ple_backward(dh3, pp, z):
    rows, cols = dh3.shape

    def body(d_ref, p_ref, z_ref, dpp_ref, dz_ref):
        d = d_ref[...]
        s = _sigmoid(z_ref[...])
        dpp_ref[...] = (d * s).astype(BF)
        dz_ref[...] = (d * p_ref[...] * s * (1.0 - s)).astype(BF)

    return _rowwise(body, "ple_backward", rows, [(dh3, "row"), (pp, "row"), (z, "row")],
                    [((rows, cols), BF, "row"), ((rows, cols), BF, "row")])


def _rms_backward(x, g, dy, add, out_dtype):
    rows, cols = x.shape
    with_add = add is not None

    def body(*refs):
        if with_add:
            x_ref, g_ref, dy_ref, add_ref, dx_ref, dg_ref = refs
        else:
            x_ref, g_ref, dy_ref, dx_ref, dg_ref = refs
        _init_acc(dg_ref)
        dx, dg = _rms_bwd_math(x_ref[...], g_ref[...], dy_ref[...].astype(F32))
        if with_add:
            dx = dx + add_ref[...]
        dx_ref[...] = dx.astype(out_dtype)
        dg_ref[...] += dg

    ins = [(x, "row"), (g, "full"), (dy, "row")] + ([(add, "row")] if with_add else [])
    return _rowwise(body, "rms_backward_add" if with_add else "rms_backward", rows, ins,
                    [((rows, cols), out_dtype, "row"), ((1, cols), F32, "acc")])


def _swiglu_forward(gu):
    rows = gu.shape[0]

    def body(gu_ref, o_ref):
        g = gu_ref[:, :D_FF].astype(F32)
        o_ref[...] = (g * _sigmoid(g) * gu_ref[:, D_FF:].astype(F32)).astype(BF)

    return _rowwise(body, "swiglu_forward", rows, [(gu, "row")], [((rows, D_FF), BF, "row")])[0]


def _swiglu_backward(gu, dact):
    rows = gu.shape[0]

    def body(gu_ref, d_ref, o_ref):
        g = gu_ref[:, :D_FF].astype(F32)
        u = gu_ref[:, D_FF:].astype(F32)
        d = d_ref[...].astype(F32)
        s = _sigmoid(g)
        gs = g * s
        o_ref[:, :D_FF] = (d * u * (s + gs * (1.0 - s))).astype(BF)
        o_ref[:, D_FF:] = (d * gs).astype(BF)

    return _rowwise(body, "swiglu_backward", rows, [(gu, "row"), (dact, "row")], [((rows, 2 * D_FF), BF, "row")])[0]


def _rope_tables(positions):
    half = MLA_ROPE // 2
    inv = ROPE_THETA ** (-jnp.arange(half, dtype=F32) / half)
    ang = positions.astype(F32)[:, None] * inv
    cos, sin = jnp.cos(ang), jnp.sin(ang)
    rows = positions.shape[0]
    c = jnp.ones((rows, LANE), F32).at[:, 64:80].set(cos).at[:, 80:96].set(cos)
    sa = jnp.zeros((rows, LANE), F32).at[:, 64:80].set(-sin)
    sb = jnp.zeros((rows, LANE), F32).at[:, 80:96].set(sin)
    return c, sa, sb


def _rope_apply(x, c, sa, sb):
    return x * c + pltpu.roll(x, LANE - 16, 1) * sa + pltpu.roll(x, 16, 1) * sb


def _rope_apply_t(dy, c, sa, sb):
    return dy * c + pltpu.roll(dy * sa, 16, 1) + pltpu.roll(dy * sb, LANE - 16, 1)


def _rope_heads(x, tables, transpose, name):
    rows, cols = x.shape

    def body(x_ref, c_ref, sa_ref, sb_ref, o_ref):
        fn = _rope_apply_t if transpose else _rope_apply
        c, sa, sb = c_ref[...], sa_ref[...], sb_ref[...]
        for head in range(cols // LANE):
            lanes = slice(head * LANE, (head + 1) * LANE)
            o_ref[:, lanes] = fn(x_ref[:, lanes].astype(F32), c, sa, sb).astype(BF)

    blk = pl.BlockSpec((ROW_TILE, cols), lambda i: (i, 0))
    tbl = pl.BlockSpec((ROW_TILE, LANE), lambda i: (i, 0))
    return pl.pallas_call(body, out_shape=jax.ShapeDtypeStruct((rows, cols), BF), grid=(rows // ROW_TILE,),
                          in_specs=[blk, tbl, tbl, tbl], out_specs=blk, name=name,
                          compiler_params=_params(("parallel",)))(x, *tables)


def _mla_mid_forward(a, q_norm, kv_norm, tables):
    rows = a.shape[0]
    qr, kvr = MLA_Q_RANK, MLA_KV_RANK

    def body(a_ref, qn_ref, kn_ref, c_ref, sa_ref, sb_ref, cq_ref, ckv_ref, kr_ref):
        aq = a_ref[:, 0:qr]
        akv = a_ref[:, qr:qr + kvr]
        cq_ref[...] = (aq * _rstd(aq) * qn_ref[...]).astype(BF)
        ckv_ref[...] = (akv * _rstd(akv) * kn_ref[...]).astype(BF)
        kr_ref[...] = _rope_apply(a_ref[:, qr + kvr:], c_ref[...], sa_ref[...], sb_ref[...]).astype(BF)

    ins = [(a, "row"), (q_norm, "full"), (kv_norm, "full")] + [(t, "row") for t in tables]
    return _rowwise(body, "mla_mid_forward", rows, ins,
                    [((rows, qr), BF, "row"), ((rows, kvr), BF, "row"), ((rows, LANE), BF, "row")])


def _mla_mid_backward(a, q_norm, kv_norm, tables, dcq, dckv, dkr):
    rows = a.shape[0]
    qr, kvr = MLA_Q_RANK, MLA_KV_RANK

    def body(a_ref, qn_ref, kn_ref, c_ref, sa_ref, sb_ref, dcq_ref, dckv_ref, dkr_ref, da_ref, dqn_ref, dkn_ref):
        _init_acc(dqn_ref, dkn_ref)
        dxq, dgq = _rms_bwd_math(a_ref[:, 0:qr], qn_ref[...], dcq_ref[...])
        dxk, dgk = _rms_bwd_math(a_ref[:, qr:qr + kvr], kn_ref[...], dckv_ref[...])
        da_ref[:, 0:qr] = dxq.astype(BF)
        da_ref[:, qr:qr + kvr] = dxk.astype(BF)
        da_ref[:, qr + kvr:] = _rope_apply_t(dkr_ref[...], c_ref[...], sa_ref[...], sb_ref[...]).astype(BF)
        dqn_ref[...] += dgq
        dkn_ref[...] += dgk

    ins = ([(a, "row"), (q_norm, "full"), (kv_norm, "full")] + [(t, "row") for t in tables]
           + [(dcq, "row"), (dckv, "row"), (dkr, "row")])
    return _rowwise(body, "mla_mid_backward", rows, ins,
                    [((rows, MLA_A_PAD), BF, "row"), ((1, qr), F32, "acc"), ((1, kvr), F32, "acc")])


def _attn_specs(rows, kv_off, g, many_row_vectors):
    head =pl.BlockSpec((rows, g * LANE), lambda h: (0, h))
    kv_head = pl.BlockSpec((rows, g * LANE), lambda h: (0, h + kv_off // g))
    shared = pl.BlockSpec((rows, LANE), lambda h: (0, 0))
    col_vec = pl.BlockSpec((g, rows, 1), lambda h: (h, 0, 0),
                           pipeline_mode=pl.Buffered(1 if many_row_vectors else 2))
    row_vec = pl.BlockSpec((g, 1, rows), lambda h: (h, 0, 0))
    return head, kv_head, shared, col_vec, row_vec


def _attn_forward(q, kv, kv_off, kr, cum_col, cum_row, scale, group_size, name):
    rows = q.shape[0]
    heads = HEADS
    t = ATTN_TILE
    tq = ATTN_Q_TILE
    per = tq // t
    has_kr = kr is not None
    has_f = cum_col is not None
    group = range(group_size)

    def body(*refs):
        it = iter(refs)
        q_ref, kv_ref = next(it), next(it)
        kr_ref = next(it) if has_kr else None
        cc_ref = next(it) if has_f else None
        cr_ref = next(it) if has_f else None
        o_ref, lse_ref = next(it), next(it)
        lo = lax.broadcasted_iota(jnp.int32, (1, LANE), 1) < HEAD_DIM
        row = lax.broadcasted_iota(jnp.int32, (tq, t), 0)
        col = lax.broadcasted_iota(jnp.int32, (tq, t), 1)
        lanes = [slice(g * LANE, (g + 1) * LANE) for g in group]

        def q_block(i, _):
            qs = pl.ds(pl.multiple_of(i * tq, tq), tq)
            qbs = [q_ref[qs, lanes[g]] for g in group]
            cqs = [cc_ref[g, qs, :] if has_f else None for g in group]

            def step(j, carry, diag):
                ks = pl.ds(pl.multiple_of(j * t, t), t)
                other = kr_ref[ks, :] if has_kr else jnp.zeros((t, LANE), BF)
                out = []
                for g in group:
                    m, l, acc = carry[g]
                    kvb = kv_ref[ks, lanes[g]]
                    kk = jnp.where(lo, kvb, other)
                    s = lax.dot_general(qbs[g], kk, (((1,), (1,)), ((), ())), preferred_element_type=F32) * scale
                    if has_f:
                        s = s + (cqs[g] - cr_ref[g, :, ks])
                    if diag is not None:
                        s = jnp.where(col + diag * t <= row, s, NEG_INF)
                    mn = jnp.maximum(m, jnp.max(s, axis=1, keepdims=True))
                    alpha = jnp.exp(m - mn)
                    p = jnp.exp(s - mn)
                    l = alpha * l + jnp.sum(p, axis=1, keepdims=True)
                    acc = alpha * acc + jnp.dot(p.astype(BF), kvb, preferred_element_type=F32)
                    out.append((mn, l, acc))
                return tuple(out)

            init = tuple((jnp.full((tq, 1), NEG_INF, F32), jnp.zeros((tq, 1), F32), jnp.zeros((tq, LANE), F32))
                         for _ in group)
            carry = lax.fori_loop(0, i * per, lambda j, c: step(j, c, None), init)
            for d in range(per):
                carry = step(i * per + d, carry, d)
            for g, (m, l, acc) in enumerate(carry):
                o_ref[qs, lanes[g]] = jnp.where(lo, 0.0, acc * (1.0 / l)).astype(BF)
                lse_ref[g, qs, :] = m + jnp.log(l)
            return 0

        lax.fori_loop(0, rows // tq, q_block, 0)

    head, kv_head, shared, col_vec, row_vec = _attn_specs(rows, kv_off, group_size, has_f)
    in_specs, args = [head, kv_head], [q, kv]
    if has_kr:
        in_specs.append(shared)
        args.append(kr)
    if has_f:
        in_specs += [col_vec, row_vec]
        args += [cum_col, cum_row]
    return pl.pallas_call(
        body, out_shape=[jax.ShapeDtypeStruct((rows, heads * LANE), BF), jax.ShapeDtypeStruct((heads, rows, 1), F32)],
        grid=(heads // group_size,), in_specs=in_specs, out_specs=[head, col_vec], name=name,
        compiler_params=_params(("arbitrary",)))(*args)


def _attn_backward(q, kv, kv_off, kr, cum_col, cum_row, o, do, lse, scale, group_size, name):
    rows = q.shape[0]
    heads = HEADS
    t = ATTN_TILE
    nb = rows // t
    has_kr = kr is not None
    has_f = cum_col is not None
    group = range(group_size)

    def body(*refs):
        it = iter(refs)
        q_ref, kv_ref = next(it), next(it)
        kr_ref = next(it) if has_kr else None
        cc_ref = next(it) if has_f else None
        cr_ref = next(it) if has_f else None
        o_ref, do_ref, lse_ref = next(it), next(it), next(it)
        dq_ref, dkv_ref = next(it), next(it)
        dkr_ref = next(it) if has_kr else None
        dck_ref = next(it) if has_f else None
        dcq_ref = next(it) if has_f else None
        dq_acc = next(it)
        lo = lax.broadcasted_iota(jnp.int32, (1, LANE), 1) < HEAD_DIM
        causal = (lax.broadcasted_iota(jnp.int32, (t, t), 1) <= lax.broadcasted_iota(jnp.int32, (t, t), 0))
        lanes = [slice(g * LANE, (g + 1) * LANE) for g in group]

        dq_acc[...] = jnp.zeros_like(dq_acc)
        if has_kr:
            _init_acc(dkr_ref)
        if has_f:
            dcq_ref[...] = jnp.zeros_like(dcq_ref)

        def kv_block(j, _):
            ks = pl.ds(pl.multiple_of(j * t, t), t)
            other = kr_ref[ks, :] if has_kr else jnp.zeros((t, LANE), BF)
            kvbs = [kv_ref[ks, lanes[g]] for g in group]
            kks = [jnp.where(lo, kvbs[g], other) for g in group]
            cks = [cr_ref[g, :, ks] if has_f else None for g in group]

            def pair(i, carry, diag):
                qs = pl.ds(pl.multiple_of(i * t, t), t)
                out = []
                for g in group:
                    dkk, dvv, dcs = carry[g]
                    qb = q_ref[qs, lanes[g]]
                    dob = do_ref[qs, lanes[g]]
                    s = lax.dot_general(qb, kks[g], (((1,), (1,)), ((), ())), preferred_element_type=F32) * scale
                    if has_f:
                        s = s + (cc_ref[g, qs, :] - cks[g])
                    if diag:
                        s = jnp.where(causal, s, NEG_INF)
                    p = jnp.exp(s - lse_ref[g, qs, :])
                    dp = lax.dot_general(dob, kvbs[g], (((1,), (1,)), ((), ())), preferred_element_type=F32)
                    delta = jnp.sum(dob.astype(F32) * o_ref[qs, lanes[g]].astype(F32), axis=1, keepdims=True)
                    ds = p * (dp - delta)
                    dsb = ds.astype(BF)
                    dvv = dvv + lax.dot_general(p.astype(BF), dob, (((0,), (0,)), ((), ())), preferred_element_type=F32)
                    dkk = dkk + lax.dot_general(dsb, qb, (((0,), (0,)), ((), ())), preferred_element_type=F32)
                    dq_acc[qs, lanes[g]] += jnp.dot(dsb, kks[g], preferred_element_type=F32)
                    if has_f:
                        dcs = dcs + jnp.sum(ds, axis=0, keepdims=True)
                        dcq_ref[g, qs, :] += jnp.sum(ds, axis=1, keepdims=True)
                    out.append((dkk, dvv, dcs))
                return tuple(out)

            init = tuple((jnp.zeros((t, LANE), F32), jnp.zeros((t, LANE), F32), jnp.zeros((1, t), F32)) for _ in group)
            carry = pair(j, init, True)
            carry = lax.fori_loop(j + 1, nb, lambda i, c: pair(i, c, False), carry)
            for g, (dkk, dvv, dcs) in enumerate(carry):
                dkk = dkk * scale
                dkv_ref[ks, lanes[g]] = jnp.where(lo, dkk, dvv).astype(BF)
                if has_kr:
                    dkr_ref[ks, :] += jnp.where(lo, 0.0, dkk)
                if has_f:
                    dck_ref[g, :, ks] = -dcs
            return 0

        lax.fori_loop(0, nb, kv_block, 0)
        dq_ref[...] = (dq_acc[...] * scale).astype(BF)

    head, kv_head, shared, col_vec, row_vec = _attn_specs(rows, kv_off, group_size, has_f)
    in_specs, args = [head, kv_head], [q, kv]
    if has_kr:
        in_specs.append(shared)
        args.append(kr)
    if has_f:
        in_specs += [col_vec, row_vec]
        args += [cum_col, cum_row]
    in_specs += [head, head, col_vec]
    args += [o, do, lse]
    out_shape = [jax.ShapeDtypeStruct((rows, heads * LANE), BF), jax.ShapeDtypeStruct((rows, heads * LANE), BF)]
    out_specs = [head, head]
    if has_kr:
        out_shape.append(jax.ShapeDtypeStruct((rows, LANE), F32))
        out_specs.append(shared)
    if has_f:
        out_shape += [jax.ShapeDtypeStruct((heads, 1, rows), F32), jax.ShapeDtypeStruct((heads, rows, 1), F32)]
        out_specs += [row_vec, col_vec]
    return pl.pallas_call(
        body, out_shape=out_shape, grid=(heads // group_size,), in_specs=in_specs, out_specs=out_specs,
        scratch_shapes=[pltpu.VMEM((rows, group_size * LANE), F32)], name=name,
        compiler_params=_params(("arbitrary",)))(*args)


def _tri_dot(tri, x):
    return jnp.dot(tri, x, preferred_element_type=F32, precision=lax.Precision.HIGHEST)


def _forget_forward(f_raw, b_f):
    rows = f_raw.shape[0]
    t = ATTN_TILE

    def body(f_ref, b_ref, cum_ref):
        tri = (lax.broadcasted_iota(jnp.int32, (t, t), 1) <= lax.broadcasted_iota(jnp.int32, (t, t), 0)).astype(F32)

        def blk(i, carry):
            sl = pl.ds(pl.multiple_of(i * t, t), t)
            xv = f_ref[sl, :] + b_ref[...]
            log_f = jnp.minimum(xv, 0.0) - jnp.log(1.0 + jnp.exp(-jnp.abs(xv)))
            cum_ref[sl, :] = _tri_dot(tri, log_f) + carry
            return carry + jnp.sum(log_f, axis=0, keepdims=True)

        lax.fori_loop(0, rows // t, blk, jnp.zeros((1, LANE), F32))

    return pl.pallas_call(body, out_shape=jax.ShapeDtypeStruct((rows, LANE), F32), name="forget_forward",
                          compiler_params=_params())(f_raw, b_f)


def _forget_backward(f_raw, b_f, dcum):
    rows = f_raw.shape[0]
    t = ATTN_TILE
    nb = rows // t

    def body(f_ref, b_ref, dc_ref, df_ref, db_ref):
        tri = (lax.broadcasted_iota(jnp.int32, (t, t), 1) >= lax.broadcasted_iota(jnp.int32, (t, t), 0)).astype(F32)

        def blk(i, carry):
            later, db = carry
            sl = pl.ds(pl.multiple_of((nb - 1 - i) * t, t), t)
            dc = dc_ref[sl, :]
            dlog = _tri_dot(tri, dc) + later
            xv = f_ref[sl, :] + b_ref[...]
            df = dlog / (1.0 + jnp.exp(xv))
            df_ref[sl, :] = df.astype(BF)
            return later + jnp.sum(dc, axis=0, keepdims=True), db + jnp.sum(df, axis=0, keepdims=True)

        _, db = lax.fori_loop(0, nb, blk, (jnp.zeros((1, LANE), F32), jnp.zeros((1, LANE), F32)))
        db_ref[...] = db

    return pl.pallas_call(body, out_shape=[jax.ShapeDtypeStruct((rows, LANE), BF), jax.ShapeDtypeStruct((1, LANE), F32)],
                          name="forget_backward", compiler_params=_params())(f_raw, b_f, dcum)


def _t5_bucket(dist):
    max_exact = REL_BUCKETS // 2
    n = jnp.maximum(dist.astype(F32), 1.0)
    large = max_exact + (jnp.log(n / max_exact) / math.log(REL_MAX_DIST / max_exact)
                         * (REL_BUCKETS - max_exact)).astype(jnp.int32)
    large = jnp.minimum(large, REL_BUCKETS - 1)
    return jnp.where(dist < max_exact, dist, large)


def _dil_buckets(dilation):
    i = jnp.arange(Q_BLOCK)[:, None]
    j = jnp.arange(Q_BLOCK)[None, :]
    cur = _t5_bucket(jnp.clip(i - j, 0) * dilation).astype(jnp.int32)
    prev = _t5_bucket(jnp.clip(Q_BLOCK + i - j, 0) * dilation).astype(jnp.int32)
    return cur, prev


def _dil_bias_tiles(tbl_ref, bc_ref, bp_ref, bias_ref, group, hp):
    ii = lax.broadcasted_iota(jnp.int32, (Q_BLOCK, Q_BLOCK), 0)
    jj = lax.broadcasted_iota(jnp.int32, (Q_BLOCK, Q_BLOCK), 1)
    for hh in range(2 * DIL_PAIRS):
        col = group * HEADS + 2 * DIL_PAIRS * hp + hh
        acc_c = jnp.zeros((Q_BLOCK, Q_BLOCK), F32)
        acc_p = jnp.zeros((Q_BLOCK, Q_BLOCK), F32)
        for b in range(REL_BUCKETS):
            val = tbl_ref[b, col]
            acc_c = jnp.where(bc_ref[...] == b, val, acc_c)
            acc_p = jnp.where(bp_ref[...] == b, val, acc_p)
        bias_ref[2 * hh] = jnp.where(jj <= ii, acc_c, NEG_INF)
        bias_ref[2 * hh + 1] = jnp.where(jj >= ii, acc_p, NEG_INF)


def _dil_view(qkv, group, dilation):
    if dilation == 1:
        return qkv
    width = 3 * HEADS * HEAD_DIM
    return qkv[:, group * width:(group + 1) * width].reshape(qkv.shape[0] // dilation, dilation * width)


def _dil_specs(group, dilation, length):
    width = DIL_PAIRS * LANE
    per = 8 // DIL_PAIRS

    def col(kind):
        if dilation == 1:
            return pl.BlockSpec((length, width), lambda hp, r: (0, (group * 3 + kind) * per + hp))
        return pl.BlockSpec((length, width), lambda hp, r: (0, (r * 3 + kind) * per + hp))

    out = pl.BlockSpec((length, width), lambda hp, r: (0, r * per + hp))
    tile = pl.BlockSpec((Q_BLOCK, Q_BLOCK), lambda hp, r: (0, 0))
    table = pl.BlockSpec(memory_space=pltpu.SMEM)
    return col, out, tile, table


def _dil_forward(view, group, dilation, table, buckets):
    length = view.shape[0]
    rows = length * dilation
    nb = length // Q_BLOCK
    scale = HEAD_DIM ** -0.5
    qb = Q_BLOCK

    def body(tbl_ref, bc_ref, bp_ref, q_ref, k_ref, v_ref, o_ref, lse_ref, bias_ref):
        hp = pl.program_id(0)

        @pl.when(pl.program_id(1) == 0)
        def _():
            _dil_bias_tiles(tbl_ref, bc_ref, bp_ref, bias_ref, group, hp)

        lo = lax.broadcasted_iota(jnp.int32, (1, LANE), 1) < HEAD_DIM
        nt = (((1,), (1,)), ((), ()))

        def blk(n, first):
            cur = pl.ds(0, qb) if first else pl.ds(pl.multiple_of(n * qb, qb), qb)
            prev = None if first else pl.ds(pl.multiple_of((n - 1) * qb, qb), qb)
            for pair in range(DIL_PAIRS):
                lanes = slice(pair * LANE, (pair + 1) * LANE)
                qn = q_ref[cur, lanes] * scale
                kc, vc = k_ref[cur, lanes], v_ref[cur, lanes]
                if not first:
                    kp, vp = k_ref[prev, lanes], v_ref[prev, lanes]
                outs, lses = [], []
                for hh in range(2):
                    bias = 4 * pair + 2 * hh
                    qm = jnp.where(lo if hh == 0 else ~lo, qn, jnp.zeros_like(qn))
                    s_c = lax.dot_general(qm, kc, nt, preferred_element_type=F32) + bias_ref[bias]
                    m = jnp.max(s_c, axis=1, keepdims=True)
                    if not first:
                        s_p = lax.dot_general(qm, kp, nt, preferred_element_type=F32) + bias_ref[bias + 1]
                        m = jnp.maximum(m, jnp.max(s_p, axis=1, keepdims=True))
                    e_c = jnp.exp(s_c - m)
                    l = jnp.sum(e_c, axis=1, keepdims=True)
                    acc = jnp.dot(e_c.astype(BF), vc, preferred_element_type=F32)
                    if not first:
                        e_p = jnp.exp(s_p - m)
                        l = l + jnp.sum(e_p, axis=1, keepdims=True)
                        acc = acc + jnp.dot(e_p.astype(BF), vp, preferred_element_type=F32)
                    outs.append(acc * (1.0 / l))
                    lses.append(m + jnp.log(l))
                o_ref[cur, lanes] = jnp.where(lo, outs[0], outs[1])
                lse_ref[cur, lanes] = jnp.where(lo, lses[0], lses[1])
            return 0

        blk(0, True)
        if nb > 1:
            lax.fori_loop(1, nb, lambda n, _: blk(n, False), 0)

    col, out, tile, tbl = _dil_specs(group, dilation, length)
    bc, bp = buckets
    o, lse = pl.pallas_call(
        body, out_shape=[jax.ShapeDtypeStruct((length, dilation * D_MODEL), F32)] * 2,
        grid=(8 // DIL_PAIRS, dilation), in_specs=[tbl, tile, tile, col(0), col(1), col(2)], out_specs=[out, out],
        scratch_shapes=[pltpu.VMEM((4 * DIL_PAIRS, qb, qb), F32)], name=f"dilated_forward_{dilation}",
        compiler_params=_params(("arbitrary", "arbitrary")))(
            table, bc, bp, view, view, view)
    return o.reshape(rows, D_MODEL), lse.reshape(rows, D_MODEL)


def _dil_backward(view, group, dilation, table, buckets, do_g, lse, dlt):
    length = view.shape[0]
    rows = length * dilation
    nb = length // Q_BLOCK
    scale = HEAD_DIM ** -0.5
    qb = Q_BLOCK

    def body(tbl_ref, bc_ref, bp_ref, q_ref, k_ref, v_ref, do_ref, lse_ref, dlt_ref,
             dq_ref, dk_ref, dv_ref, db_ref, bias_ref, dk_acc, dv_acc):
        hp = pl.program_id(0)

        @pl.when(pl.program_id(1) == 0)
        def _():
            _dil_bias_tiles(tbl_ref, bc_ref, bp_ref, bias_ref, group, hp)
            db_ref[...] = jnp.zeros_like(db_ref)

        dk_acc[...] = jnp.zeros_like(dk_acc)
        dv_acc[...] = jnp.zeros_like(dv_acc)
        lo = lax.broadcasted_iota(jnp.int32, (1, LANE), 1) < HEAD_DIM
        tn = (((0,), (0,)), ((), ()))
        nt = (((1,), (1,)), ((), ()))

        def blk(n, first):
            cur = pl.ds(0, qb) if first else pl.ds(pl.multiple_of(n * qb, qb), qb)
            prev = None if first else pl.ds(pl.multiple_of((n - 1) * qb, qb), qb)
            for pair in range(DIL_PAIRS):
                lanes = slice(pair * LANE, (pair + 1) * LANE)
                qn = q_ref[cur, lanes] * scale
                don = do_ref[cur, lanes]
                kc, vc = k_ref[cur, lanes], v_ref[cur, lanes]
                if not first:
                    kp, vp = k_ref[prev, lanes], v_ref[prev, lanes]
                lse_n = lse_ref[cur, lanes]
                dlt_n = dlt_ref[cur, lanes]
                dqs = []
                dkc = jnp.zeros((qb, LANE), F32)
                dkp = jnp.zeros((qb, LANE), F32)
                dvc = jnp.zeros((qb, LANE), F32)
                dvp = jnp.zeros((qb, LANE), F32)
                for hh in range(2):
                    bias = 4 * pair + 2 * hh
                    mask = lo if hh == 0 else ~lo
                    qm = jnp.where(mask, qn, jnp.zeros_like(qn))
                    dom = jnp.where(mask, don, jnp.zeros_like(don))
                    lse_h = jnp.max(jnp.where(mask, lse_n, -3e38), axis=1, keepdims=True)
                    dlt_h = jnp.max(jnp.where(mask, dlt_n, -3e38), axis=1, keepdims=True)
                    p_c = jnp.exp(lax.dot_general(qm, kc, nt, preferred_element_type=F32) + bias_ref[bias] - lse_h)
                    ds_c = p_c * (lax.dot_general(dom, vc, nt, preferred_element_type=F32) - dlt_h)
                    db_ref[pair, 2 * hh] += ds_c
                    dsc_b = ds_c.astype(BF)
                    dq = jnp.dot(dsc_b, kc, preferred_element_type=F32)
                    dkc = dkc + lax.dot_general(dsc_b, qm, tn, preferred_element_type=F32)
                    dvc = dvc + lax.dot_general(p_c.astype(BF), dom, tn, preferred_element_type=F32)
                    if not first:
                        p_p = jnp.exp(lax.dot_general(qm, kp, nt, preferred_element_type=F32) + bias_ref[bias + 1] - lse_h)
                        ds_p = p_p * (lax.dot_general(dom, vp, nt, preferred_element_type=F32) - dlt_h)
                        db_ref[pair, 2 * hh + 1] += ds_p
                        dsp_b = ds_p.astype(BF)
                        dq = dq + jnp.dot(dsp_b, kp, preferred_element_type=F32)
                        dkp = dkp + lax.dot_general(dsp_b, qm, tn, preferred_element_type=F32)
                        dvp = dvp + lax.dot_general(p_p.astype(BF), dom, tn, preferred_element_type=F32)
                    dqs.append(dq)
                dq_ref[cur, lanes] = (jnp.where(lo, dqs[0], dqs[1]) * scale).astype(BF)
                dk_acc[cur, lanes] += dkc
                dv_acc[cur, lanes] += dvc
                if not first:
                    dk_acc[prev, lanes] += dkp
                    dv_acc[prev, lanes] += dvp
            return 0

        blk(0, True)
        if nb > 1:
            lax.fori_loop(1, nb, lambda n, _: blk(n, False), 0)
        dk_ref[...] = dk_acc[...].astype(BF)
        dv_ref[...] = dv_acc[...].astype(BF)

    col, out, tile, tbl = _dil_specs(group, dilation, length)
    bc, bp = buckets
    wide = (length, dilation * D_MODEL)
    dq, dk, dv, db = pl.pallas_call(
        body, out_shape=[jax.ShapeDtypeStruct(wide, BF)] * 3 + [jax.ShapeDtypeStruct((8, 4, qb, qb), F32)],
        grid=(8 // DIL_PAIRS, dilation), in_specs=[tbl, tile, tile, col(0), col(1), col(2), out, out, out],
        out_specs=[out, out, out, pl.BlockSpec((DIL_PAIRS, 4, qb, qb), lambda hp, r: (hp, 0, 0, 0))],
        scratch_shapes=[pltpu.VMEM((4 * DIL_PAIRS, qb, qb), F32), pltpu.VMEM((length, DIL_PAIRS * LANE), F32),
                        pltpu.VMEM((length, DIL_PAIRS * LANE), F32)],
        name=f"dilated_backward_{dilation}", compiler_params=_params(("arbitrary", "arbitrary")))(
            table, bc, bp, view, view, view,
            do_g.reshape(wide), lse.reshape(wide), dlt.reshape(wide))
    return dq.reshape(rows, D_MODEL), dk.reshape(rows, D_MODEL), dv.reshape(rows, D_MODEL), db


def _head_sums(x, lo):
    s0 = jnp.sum(jnp.where(lo, x, 0.0), axis=1, keepdims=True)
    s1 = jnp.sum(jnp.where(lo, 0.0, x), axis=1, keepdims=True)
    return jnp.where(lo, s0, s1)


def _dil_merge_forward(outs, lses):
    rows = outs[0].shape[0]

    def body(o0, o1, o2, l0, l1, l2, o_ref):
        ls = [l0[...], l1[...], l2[...]]
        m = jnp.maximum(jnp.maximum(ls[0], ls[1]), ls[2])
        es = [jnp.exp(v - m) for v in ls]
        tot = es[0] + es[1] + es[2]
        o_ref[...] = ((es[0] * o0[...] + es[1] * o1[...] + es[2] * o2[...]) / tot).astype(BF)

    blk = pl.BlockSpec((ROW_TILE, LANE), lambda i, j: (i, j))
    return pl.pallas_call(body, out_shape=jax.ShapeDtypeStruct((rows, D_MODEL), BF), grid=(rows // ROW_TILE, 8),
                          in_specs=[blk] * 6, out_specs=blk, name="dilated_merge_forward",
                          compiler_params=_params(("parallel", "parallel")))(*outs, *lses)


def _dil_merge_backward(outs, lses, do):
    rows = outs[0].shape[0]

    def body(o0, o1, o2, l0, l1, l2, do_ref, d0, d1, d2, t0, t1, t2):
        lo = lax.broadcasted_iota(jnp.int32, (1, LANE), 1) < HEAD_DIM
        ls = [l0[...], l1[...], l2[...]]
        os_ = [o0[...], o1[...], o2[...]]
        m = jnp.maximum(jnp.maximum(ls[0], ls[1]), ls[2])
        es = [jnp.exp(v - m) for v in ls]
        inv = 1.0 / (es[0] + es[1] + es[2])
        alphas = [e * inv for e in es]
        dov = do_ref[...]
        merged = alphas[0] * os_[0] + alphas[1] * os_[1] + alphas[2] * os_[2]
        dot = _head_sums(dov * merged, lo)
        for a, d_ref, t_ref in zip(alphas, (d0, d1, d2), (t0, t1, t2)):
            d_ref[...] = (a * dov).astype(BF)
            t_ref[...] = a * dot

    blk = pl.BlockSpec((ROW_TILE, LANE), lambda i, j: (i, j))
    res = pl.pallas_call(
        body, out_shape=[jax.ShapeDtypeStruct((rows, D_MODEL), BF)] * 3 + [jax.ShapeDtypeStruct((rows, D_MODEL), F32)] * 3,
        grid=(rows // ROW_TILE, 8), in_specs=[blk] * 7, out_specs=[blk] * 6, name="dilated_merge_backward",
        compiler_params=_params(("parallel", "parallel")))(*outs, *lses, do)
    return res[:3], res[3:]


def _rel_bias_grad(dbs, buckets):
    def body(db_ref, bc_ref, bp_ref, o_ref):
        g = pl.program_id(0)
        hp = pl.program_id(1)

        @pl.when((g == 0) & (hp == 0))
        def _():
            o_ref[...] = jnp.zeros_like(o_ref)

        rr = lax.broadcasted_iota(jnp.int32, (REL_BUCKETS, LANE), 0)
        cc = lax.broadcasted_iota(jnp.int32, (REL_BUCKETS, LANE), 1)
        bc = bc_ref[0]
        bp = bp_ref[0]
        acc = jnp.zeros((REL_BUCKETS, LANE), F32)
        for hh in range(2):
            col = g * HEADS + 2 * hp + hh
            d_c = db_ref[0, 0, 2 * hh]
            d_p = db_ref[0, 0, 2 * hh + 1]
            for b in range(REL_BUCKETS):
                val = (jnp.sum(jnp.where(bc == b, d_c, 0.0), keepdims=True)
                       + jnp.sum(jnp.where(bp == b, d_p, 0.0), keepdims=True))
                acc = jnp.where((rr == b) & (cc == col), val, acc)
        o_ref[...] += acc

    db_all = jnp.stack(dbs)
    bc_all = jnp.stack([b[0] for b in buckets])
    bp_all = jnp.stack([b[1] for b in buckets])
    tile = pl.BlockSpec((1, Q_BLOCK, Q_BLOCK), lambda g, hp: (g, 0, 0))
    return pl.pallas_call(
        body, out_shape=jax.ShapeDtypeStruct((REL_BUCKETS, LANE), F32), grid=(3, 8),
        in_specs=[pl.BlockSpec((1, 1, 4, Q_BLOCK, Q_BLOCK), lambda g, hp: (g, hp, 0, 0, 0)), tile, tile],
        out_specs=pl.BlockSpec((REL_BUCKETS, LANE), lambda g, hp: (0, 0)), name="rel_bias_grad",
        compiler_params=_params(("arbitrary", "arbitrary")))(db_all, bc_all, bp_all)


def _mla_forward(hn, w, tables):
    a = _matmul(hn, w["w_a"], name="mla_a")
    cq, ckv, kr = _mla_mid_forward(a, w["q_norm"], w["kv_norm"], tables)
    q_raw = _matmul(cq, w["w_uq"], name="mla_uq")
    q = _rope_heads(q_raw, tables, False, "rope_forward")
    kv = _matmul(ckv, w["w_ukv"], b_chunks=True, out_dtype=BF, name="mla_ukv")
    scale = (HEAD_DIM + MLA_ROPE) ** -0.5
    o, lse = _attn_forward(q, kv, 0, kr, None, None, scale, MLA_GROUP, "mla_attention_forward")
    y = _matmul(o, w["w_o"], name="attn_out")
    return y, dict(hn=hn, a=a, cq=cq, ckv=ckv, kr=kr, q=q, kv=kv, o=o, lse=lse)


def _mla_backward(dy, w, s, tables):
    scale = (HEAD_DIM + MLA_ROPE) ** -0.5
    g = {}
    g["w_o"] = _matmul(s["o"], dy, ta=True, out_dtype=BF, name="attn_out_dw")
    do = _matmul(dy, w["w_o"], tb=True, out_dtype=BF, name="attn_out_dx")
    dq, dkv, dkr = _attn_backward(s["q"], s["kv"], 0, s["kr"], None, None, s["o"], do, s["lse"], scale,
                                  MLA_GROUP, "mla_attention_backward")
    dq_raw = _rope_heads(dq, tables, True, "rope_backward")
    g["w_uq"] = _matmul(s["cq"], dq_raw, ta=True, out_dtype=BF, name="mla_uq_dw")
    dcq = _matmul(dq_raw, w["w_uq"], tb=True, name="mla_uq_dx")
    g["w_ukv"] = _matmul(s["ckv"], dkv, ta=True, out_chunks=True, out_dtype=BF, name="mla_ukv_dw")
    dckv = _matmul(dkv, w["w_ukv"], tb=True, b_chunks=True, name="mla_ukv_dx")
    da, g["q_norm"], g["kv_norm"] = _mla_mid_backward(s["a"], w["q_norm"], w["kv_norm"], tables, dcq, dckv, dkr)
    g["w_a"] = _matmul(s["hn"], da, ta=True, out_dtype=BF, name="mla_a_dw")
    dhn = _matmul(da, w["w_a"], tb=True, name="mla_a_dx")
    return dhn, g


def _fox_forward(hn, w):
    qkv = _matmul(hn, w["w_qkv"], out_dtype=BF, name="fox_qkv")
    f_raw = _matmul(hn, w["w_f"], name="fox_f")
    cum = _forget_forward(f_raw, w["b_f"])
    cum_heads = cum[:, :HEADS].T
    cum_col, cum_row = cum_heads[:, :, None], cum_heads[:, None, :]
    o, lse = _attn_forward(qkv, qkv, HEADS, None, cum_col, cum_row, HEAD_DIM ** -0.5, FOX_GROUP,
                           "fox_attention_forward")
    y = _matmul(o, w["w_o"], name="attn_out")
    return y, dict(hn=hn, qkv=qkv, f_raw=f_raw, cum_col=cum_col, cum_row=cum_row, o=o, lse=lse)


def _fox_backward(dy, w, s):
    g = {}
    g["w_o"] = _matmul(s["o"], dy, ta=True, out_dtype=BF, name="attn_out_dw")
    do = _matmul(dy, w["w_o"], tb=True, out_dtype=BF, name="attn_out_dx")
    dq, dkv, dck, dcq = _attn_backward(s["qkv"], s["qkv"], HEADS, None, s["cum_col"], s["cum_row"], s["o"], do,
                                       s["lse"], HEAD_DIM ** -0.5, FOX_GROUP, "fox_attention_backward")
    dcum = jnp.pad((dck[:, 0, :] + dcq[:, :, 0]).T, ((0, 0), (0, LANE - HEADS)))
    df, g["b_f"] = _forget_backward(s["f_raw"], w["b_f"], dcum)
    dqkv = jnp.concatenate([dq, dkv], axis=1)
    g["w_qkv"] = _matmul(s["hn"], dqkv, ta=True, out_dtype=BF, name="fox_qkv_dw")
    g["w_f"] = _matmul(s["hn"], df, ta=True, out_dtype=BF, name="fox_f_dw")
    dhn = _matmul(dqkv, w["w_qkv"], tb=True, name="fox_qkv_dx")
    dhn = _matmul(df, w["w_f"], tb=True, add=dhn, name="fox_f_dx")
    return dhn, g


def _dil_mixer_forward(hn, w, buckets):
    qkv = _matmul(hn, w["w_qkv"], b_chunks=True, out_dtype=BF, name="dil_qkv")
    views = [_dil_view(qkv, grp, dilation) for grp, (_, dilation) in enumerate(DIL_PATTERNS)]
    outs, lses = [], []
    for grp, (_, dilation) in enumerate(DIL_PATTERNS):
        o_g, lse_g = _dil_forward(views[grp], grp, dilation, w["rel_bias"], buckets[grp])
        outs.append(o_g)
        lses.append(lse_g)
    o = _dil_merge_forward(outs, lses)
    y = _matmul(o, w["w_o"], name="dil_out")
    return y, dict(hn=hn, views=views, outs=outs, lses=lses, o=o)


def _dil_mixer_backward(dy, w, s, buckets):
    g = {}
    g["w_o"] = _matmul(s["o"], dy, ta=True, out_dtype=BF, name="dil_out_dw")
    do = _matmul(dy, w["w_o"], tb=True, name="dil_out_dx")
    do_gs, dlts = _dil_merge_backward(s["outs"], s["lses"], do)
    parts, dbs = [], []
    for grp, (_, dilation) in enumerate(DIL_PATTERNS):
        dq, dk, dv, db = _dil_backward(s["views"][grp], grp, dilation, w["rel_bias"], buckets[grp], do_gs[grp],
                                       s["lses"][grp], dlts[grp])
        parts += [dq, dk, dv]
        dbs.append(db)
    dqkv = jnp.concatenate(parts, axis=1)
    g["rel_bias"] = _rel_bias_grad(dbs, buckets)
    g["w_qkv"] = _matmul(s["hn"], dqkv, ta=True, out_chunks=True, out_dtype=BF, name="dil_qkv_dw")
    dhn = _matmul(dqkv, w["w_qkv"], tb=True, b_chunks=True, name="dil_qkv_dx")
    return dhn, g


def _mixer_weights(i, lw, small):
    mixer, j = i % N_MIXERS, i // N_MIXERS
    if mixer == 0:
        return dict(lw["mixer"], q_norm=small["mla_q_norm"][j][None, :], kv_norm=small["mla_kv_norm"][j][None, :])
    if mixer == 1:
        return dict(lw["mixer"], rel_bias=small["rel_bias"])
    return dict(lw["mixer"], b_f=jnp.pad(small["fox_b_f"][j][None, :], ((0, 0), (0, LANE - HEADS))))


MIXER_PART, COMMON_PART = 0, 1


def _run_layers(x, p, positions, target, get_part, get_small, put_part):
    tables = _rope_tables(positions)
    buckets = [_dil_buckets(d) for _, d in DIL_PATTERNS]
    layers, saved = [], []
    h = x
    first = get_part(0, MIXER_PART, positions)
    small = get_small()

    def gain(i, k):
        return small["norm_g"][i, k][None, :]

    hn = _prenorm(h, gain(0, 0))
    sq = dh = None
    for i in range(DEPTH):
        mixer = i % N_MIXERS
        lw = dict(mixer=first if i == 0 else get_part(i, MIXER_PART, h))
        mw = _mixer_weights(i, lw, small)
        if mixer == 0:
            y, ms = _mla_forward(hn, mw, tables)
        elif mixer == 1:
            y, ms = _dil_mixer_forward(hn, mw, buckets)
        else:
            y, ms = _fox_forward(hn, mw)
        lw.update(get_part(i, COMMON_PART, y))
        layers.append(lw)
        h1, hn2 = _post_residual(h, y, gain(i, 1), gain(i, 2))
        gu = _matmul(hn2, lw["ffn_w_in"], b_chunks=True, out_dtype=BF, name="ffn_in")
        act = _swiglu_forward(gu)
        f = _matmul(act, lw["ffn_w_out"], name="ffn_out")
        h2, h2b = _post_residual(h1, f, gain(i, 3), None)
        pp = _matmul(p[i], lw["ple_w_proj"], b_chunks=True, name="ple_proj")
        z = _matmul(h2b, lw["ple_w_gate"], name="ple_gate")
        saved.append(dict(h=h, y=y, ms=ms, h1=h1, hn2=hn2, gu=gu, act=act, f=f, h2b=h2b, pp=pp, z=z))
        if i + 1 < DEPTH:
            h, hn = _ple_forward(h2, pp, z, gain(i + 1, 0))
        else:
            dh, sq = _ple_loss(h2, pp, z, target)

    norm_rows = [[None] * 4 for _ in range(DEPTH)]
    sg = dict(mla_q_norm={}, mla_kv_norm={}, rel_bias=None, fox_b_f={})
    for i in reversed(range(DEPTH)):
        s, lw = saved[i], layers[i]
        mixer, j = i % N_MIXERS, i // N_MIXERS
        mw = _mixer_weights(i, lw, small)
        lg = {}
        dpp, dz = _ple_backward(dh, s["pp"], s["z"])
        lg["ple_w_proj"] = _matmul(p[i], dpp, ta=True, out_chunks=True, out_dtype=BF, name="ple_proj_dw")
        lg["ple_w_gate"] = _matmul(s["h2b"], dz, ta=True, out_dtype=BF, name="ple_gate_dw")
        dh2 = _matmul(dz, lw["ple_w_gate"], tb=True, add=dh, name="ple_gate_dx")
        df, norm_rows[i][3] = _rms_backward(s["f"], gain(i, 3), dh2, None, BF)
        lg["ffn_w_out"] = _matmul(s["act"], df, ta=True, out_dtype=BF, name="ffn_out_dw")
        dact = _matmul(df, lw["ffn_w_out"], tb=True, out_dtype=BF, name="ffn_out_dx")
        dgu = _swiglu_backward(s["gu"], dact)
        lg["ffn_w_in"] = _matmul(s["hn2"], dgu, ta=True, out_chunks=True, out_dtype=BF, name="ffn_in_dw")
        token = put_part(i, COMMON_PART, lg)
        dhn2 = _matmul(dgu, lw["ffn_w_in"], tb=True, b_chunks=True, name="ffn_in_dx")
        dh1, norm_rows[i][2] = _rms_backward(s["h1"], gain(i, 2), dhn2, dh2, F32)
        dy, norm_rows[i][1] = _rms_backward(s["y"], gain(i, 1) + token[0:1, 0:1], dh1, None, BF)
        if mixer == 0:
            dhn, mg = _mla_backward(dy, mw, s["ms"], tables)
            sg["mla_q_norm"][j] = mg.pop("q_norm")
            sg["mla_kv_norm"][j] = mg.pop("kv_norm")
        elif mixer == 1:
            dhn, mg = _dil_mixer_backward(dy, mw, s["ms"], buckets)
            rel = mg.pop("rel_bias")[:, :3 * HEADS]
            sg["rel_bias"] = rel if sg["rel_bias"] is None else sg["rel_bias"] + rel
        else:
            dhn, mg = _fox_backward(dy, mw, s["ms"])
            sg["fox_b_f"][j] = mg.pop("b_f")[:, :HEADS]
        token = put_part(i, MIXER_PART, mg)
        dh, norm_rows[i][0] = _rms_backward(s["h"], gain(i, 0) + token[0:1, 0:1], dhn, dh1, F32)
    small_grads = dict(norm_g=jnp.stack([jnp.concatenate(row, axis=0) for row in norm_rows]),
                       rel_bias=sg["rel_bias"])
    for k in ("mla_q_norm", "mla_kv_norm", "fox_b_f"):
        small_grads[k] = jnp.concatenate([sg[k][j] for j in sorted(sg[k])], axis=0)
    return sq, dh, small_grads


COL_SHARDED = ("ffn_w_in", "ple_w_proj", "mla_w_uq", "mla_w_ukv", "dil_w_qkv", "fox_w_qkvf")
ROW_SHARDED = ("ffn_w_out", "ple_w_gate", "mla_w_a", "mla_w_o", "dil_w_o", "fox_w_o")
BIG = ("ffn_w_in", "ffn_w_out", "ple_w_proj", "ple_w_gate", "mla_w_a", "mla_w_uq", "mla_w_ukv", "mla_w_o",
       "dil_w_qkv", "dil_w_o", "fox_w_qkvf", "fox_w_o")
SMALL_SHARDED = ("norm_g", "mla_q_norm", "mla_kv_norm")
SMALL_REPLICATED = ("rel_bias", "fox_b_f")
WEIGHTS = ("norm_g", "ffn_w_in", "ffn_w_out", "ple_w_proj", "ple_w_gate", "rel_bias", "mla_w_a", "mla_q_norm",
           "mla_kv_norm", "mla_w_uq", "mla_w_ukv", "mla_w_o", "dil_w_qkv", "dil_w_o", "fox_w_qkvf", "fox_b_f", "fox_w_o")


TRANSPOSED = "fox_w_qkvf"
LAYER_COMMON = ("ffn_w_in", "ffn_w_out", "ple_w_proj", "ple_w_gate")
MIXER_WEIGHTS = (("mla_w_a", "mla_w_uq", "mla_w_ukv", "mla_w_o"), ("dil_w_qkv", "dil_w_o"), ("fox_w_qkvf", "fox_w_o"))


def _part_names(i, part):
    return MIXER_WEIGHTS[i % N_MIXERS] if part == MIXER_PART else LAYER_COMMON


def _layer_slot(name, i):
    return i if name in LAYER_COMMON else i // N_MIXERS


def _merge_rows(chunks):
    n, r, c = chunks.shape
    return chunks.reshape(n * r, c)


def _merge_cols(chunks):
    n, r, c = chunks.shape
    return chunks.transpose(1, 0, 2).reshape(r, n * c)


def _pad_heads_out(wo):
    w3 = wo.reshape(HEADS, HEAD_DIM, D_MODEL)
    return jnp.pad(w3, ((0, 0), (HEAD_DIM, 0), (0, 0))).reshape(HEADS * LANE, D_MODEL)


def _part_to_compute(i, part, ch):
    if part == COMMON_PART:
        return dict(ffn_w_in=ch["ffn_w_in"], ffn_w_out=_merge_rows(ch["ffn_w_out"]), ple_w_proj=ch["ple_w_proj"],
                    ple_w_gate=_merge_rows(ch["ple_w_gate"]))
    lw = {}
    mixer = i % N_MIXERS
    if mixer == 0:
        wa = _merge_rows(ch["mla_w_a"])
        rank = MLA_Q_RANK + MLA_KV_RANK
        wa_p = jnp.concatenate([wa[:, :rank], jnp.zeros((wa.shape[0], 64), wa.dtype), wa[:, rank:],
                                jnp.zeros((wa.shape[0], 32), wa.dtype)], axis=1)
        wuq = _merge_cols(ch["mla_w_uq"]).reshape(MLA_Q_RANK, HEADS, HEAD_DIM + MLA_ROPE)
        wuq_p = jnp.pad(wuq, ((0, 0), (0, 0), (0, LANE - HEAD_DIM - MLA_ROPE))).reshape(MLA_Q_RANK, HEADS * LANE)
        lw["mixer"] = dict(w_a=wa_p, w_uq=wuq_p, w_ukv=ch["mla_w_ukv"], w_o=_pad_heads_out(_merge_rows(ch["mla_w_o"])))
    elif mixer == 1:
        lw["mixer"] = dict(w_qkv=ch["dil_w_qkv"], w_o=_merge_rows(ch["dil_w_o"]))
    else:
        wf = _merge_rows(ch["fox_w_qkvf"]).T
        inner = HEADS * HEAD_DIM
        q3 = wf[:, :inner].reshape(D_MODEL, HEADS, HEAD_DIM)
        k3 = wf[:, inner:2 * inner].reshape(D_MODEL, HEADS, HEAD_DIM)
        v3 = wf[:, 2 * inner:3 * inner].reshape(D_MODEL, HEADS, HEAD_DIM)
        q_p = jnp.pad(q3, ((0, 0), (0, 0), (0, HEAD_DIM))).reshape(D_MODEL, HEADS * LANE)
        kv_p = jnp.concatenate([k3, v3], axis=2).reshape(D_MODEL, HEADS * LANE)
        f_p = jnp.pad(wf[:, 3 * inner:], ((0, 0), (0, LANE - HEADS)))
        lw["mixer"] = dict(w_qkv=jnp.concatenate([q_p, kv_p], axis=1), w_f=f_p,
                           w_o=_pad_heads_out(_merge_rows(ch["fox_w_o"])))
    return lw["mixer"]


def _part_contributions(i, part, lg, chunk_shapes):
    spec = {k: jax.ShapeDtypeStruct(s, BF) for k, s in chunk_shapes.items()}
    (contrib,) = jax.linear_transpose(functools.partial(_part_to_compute, i, part), spec)(lg)
    return contrib


def _chip_peers():
    x, y, c = lax.axis_index("x"), lax.axis_index("y"), lax.axis_index("c")
    peers = [(1 - x, y), (x, 1 - y), (1 - x, 1 - y)]
    return x, y, c, peers


SEM_SPEC = pl.BlockSpec(memory_space=pltpu.SEMAPHORE)
ANY_SPEC = pl.BlockSpec(memory_space=pl.ANY)
SPLIT_EFFECT = pltpu.SideEffectType.DATAFLOW_SIDE_EFFECTING


def _own_slot(shard):
    me = 2 * lax.axis_index("x") + lax.axis_index("y")
    return lax.dynamic_update_index_in_dim(lax.empty((N_CHIPS,) + shard.shape, shard.dtype), shard[None], me, 0)


def _spread_copy(src, land, k, peer, c, send_sems, recv_sems, index, src_slot, slot):
    px, py = peer
    return pltpu.make_async_remote_copy(
        src_ref=src.at[src_slot], dst_ref=land.at[slot],
        send_sem=send_sems.at[3 * index + k], recv_sem=recv_sems.at[3 * index + k],
        device_id=(px, py, c), device_id_type=MESH)


def _spread_start(bufs, srcs, after, name):
    n = len(bufs)
    exchange = srcs is not None
    arrays = (list(srcs) if exchange else []) + list(bufs)
    na = len(arrays)

    def body(*refs):
        src, land = refs[:n], refs[na - n:na]
        send_sems, recv_sems = refs[na + 1], refs[na + 2]
        token = refs[-1]
        x, y, c, peers = _chip_peers()
        me = 2 * x + y
        for w in range(n):
            for k, peer in enumerate(peers):
                src_slot = 2 * peer[0] + peer[1] if exchange else me
                _spread_copy(src[w], land[w], k, peer, c, send_sems, recv_sems, w, src_slot, me).start()
        token[...] = jnp.zeros_like(token)

    hbm = [pltpu.with_memory_space_constraint(a, pltpu.HBM) for a in arrays]
    out = pl.pallas_call(
        body, name=name,
        out_shape=(pltpu.SemaphoreType.DMA((3 * n,)), pltpu.SemaphoreType.DMA((3 * n,)),
                   *[pltpu.HBM(a.shape, a.dtype) for a in hbm], jax.ShapeDtypeStruct((8, LANE), F32)),
        in_specs=[HBM_SPEC] * na + [ANY_SPEC],
        out_specs=(SEM_SPEC, SEM_SPEC, *[HBM_SPEC] * na, pl.BlockSpec(memory_space=pltpu.VMEM)),
        input_output_aliases={w: 2 + w for w in range(na)},
        compiler_params=pltpu.CompilerParams(has_side_effects=SPLIT_EFFECT))(*hbm, after)
    return dict(send=out[0], recv=out[1], arrays=out[2:2 + na], n=n, token=out[-1], exchange=exchange)


def _spread_wait(handle, after, name):
    n, exchange = handle["n"], handle["exchange"]
    arrays = list(handle["arrays"])
    na = len(arrays)

    def body(*refs):
        src, land = refs[:n], refs[na - n:na]
        send_sems, recv_sems = refs[na], refs[na + 1]
        x, y, c, peers = _chip_peers()
        me = 2 * x + y
        for w in range(n):
            for k, peer in enumerate(peers):
                there = 2 * peer[0] + peer[1]
                cp = _spread_copy(src[w], land[w], k, peer, c, send_sems, recv_sems, w, there if exchange else me, there)
                cp.wait_send()
                cp.wait_recv()

    out = pl.pallas_call(
        body, name=name, out_shape=tuple(pltpu.HBM(a.shape, a.dtype) for a in arrays),
        in_specs=[HBM_SPEC] * na + [SEM_SPEC, SEM_SPEC, ANY_SPEC], out_specs=tuple([HBM_SPEC] * na),
        input_output_aliases={w: w for w in range(na)},
        compiler_params=pltpu.CompilerParams(has_side_effects=SPLIT_EFFECT))(*arrays, handle["send"], handle["recv"], after)
    return (list(out[n:]), list(out[:n])) if exchange else list(out)


def _sibling_copy(received, sent, land, k, me, peers, sibling, send_sems, recv_sems, index):
    slot = me if k == 3 else 2 * peers[k][0] + peers[k][1]
    src = sent if k == 3 else received
    return pltpu.make_async_remote_copy(
        src_ref=src.at[slot], dst_ref=land.at[slot], send_sem=send_sems.at[4 * index + k],
        recv_sem=recv_sems.at[4 * index + k], device_id=sibling, device_id_type=MESH)


def _sibling_start(received, sent, after, name):
    n = len(received)
    lands = [lax.empty(a.shape, a.dtype) for a in received]
    arrays = list(received) + list(sent) + lands

    def body(*refs):
        rec, snt, land = refs[:n], refs[n:2 * n], refs[2 * n:3 * n]
        send_sems, recv_sems = refs[3 * n + 1], refs[3 * n + 2]
        token = refs[-1]
        x, y, c, peers = _chip_peers()
        for w in range(n):
            for k in range(4):
                _sibling_copy(rec[w], snt[w], land[w], k, 2 * x + y, peers, (x, y, 1 - c), send_sems, recv_sems, w).start()
        token[...] = jnp.zeros_like(token)

    hbm = [pltpu.with_memory_space_constraint(a, pltpu.HBM) for a in arrays]
    out = pl.pallas_call(
        body, name=name,
        out_shape=(pltpu.SemaphoreType.DMA((4 * n,)), pltpu.SemaphoreType.DMA((4 * n,)),
                   *[pltpu.HBM(a.shape, a.dtype) for a in hbm], jax.ShapeDtypeStruct((8, LANE), F32)),
        in_specs=[HBM_SPEC] * (3 * n) + [ANY_SPEC],
        out_specs=(SEM_SPEC, SEM_SPEC, *[HBM_SPEC] * (3 * n), pl.BlockSpec(memory_space=pltpu.VMEM)),
        input_output_aliases={w: 2 + w for w in range(3 * n)},
        compiler_params=pltpu.CompilerParams(has_side_effects=SPLIT_EFFECT))(*hbm, after)
    return dict(send=out[0], recv=out[1], arrays=out[2:2 + 3 * n], n=n, token=out[-1])


def _sibling_wait(handle, after, name):
    n = handle["n"]
    arrays = list(handle["arrays"])

    def body(*refs):
        rec, snt, land = refs[:n], refs[n:2 * n], refs[2 * n:3 * n]
        send_sems, recv_sems = refs[3 * n], refs[3 * n + 1]
        x, y, c, peers = _chip_peers()
        for w in range(n):
            for k in range(4):
                cp = _sibling_copy(rec[w], snt[w], land[w], k, 2 * x + y, peers, (x, y, 1 - c), send_sems, recv_sems, w)
                cp.wait_send()
                cp.wait_recv()

    out = pl.pallas_call(
        body, name=name, out_shape=tuple(pltpu.HBM(a.shape, a.dtype) for a in arrays),
        in_specs=[HBM_SPEC] * (3 * n) + [SEM_SPEC, SEM_SPEC, ANY_SPEC], out_specs=tuple([HBM_SPEC] * (3 * n)),
        input_output_aliases={w: w for w in range(3 * n)},
        compiler_params=pltpu.CompilerParams(has_side_effects=SPLIT_EFFECT))(*arrays, handle["send"], handle["recv"], after)
    return list(out[:n]), list(out[n:2 * n]), list(out[2 * n:])


def _all_reduce_small(v):
    rows = v.shape[0]

    def body(v_ref, sum_ref, slots, send_sems, recv_sems):
        x, y, c = lax.axis_index("x"), lax.axis_index("y"), lax.axis_index("c")
        me = 4 * x + 2 * y + c
        slots[me] = v_ref[...]
        sends = []
        for k in range(1, N_DEV):
            bx, by, bc = (k >> 2) & 1, (k >> 1) & 1, k & 1
            peer = (x ^ bx, y ^ by, c ^ bc)
            rc = pltpu.make_async_remote_copy(src_ref=v_ref, dst_ref=slots.at[me], send_sem=send_sems.at[k],
                                              recv_sem=recv_sems.at[k], device_id=peer, device_id_type=MESH)
            rc.start()
            sends.append(rc)
        for k in range(1, N_DEV):
            bx, by, bc = (k >> 2) & 1, (k >> 1) & 1, k & 1
            src = 4 * (x ^ bx) + 2 * (y ^ by) + (c ^ bc)
            pltpu.make_async_remote_copy(src_ref=v_ref, dst_ref=slots.at[src], send_sem=send_sems.at[k],
                                         recv_sem=recv_sems.at[k], device_id=(x ^ bx, y ^ by, c ^ bc),
                                         device_id_type=MESH).wait_recv()
        for rc in sends:
            rc.wait_send()
        total = slots[0]
        for k in range(1, N_DEV):
            total = total + slots[k]
        sum_ref[...] = total

    vm = pl.BlockSpec(memory_space=pltpu.VMEM)
    return pl.pallas_call(
        body, out_shape=jax.ShapeDtypeStruct((rows, LANE), F32), in_specs=[vm], out_specs=vm,
        scratch_shapes=[pltpu.VMEM((N_DEV, rows, LANE), F32), pltpu.SemaphoreType.DMA((N_DEV,)),
                        pltpu.SemaphoreType.DMA((N_DEV,))], name="all_reduce_small")(v)


def _as_2d(a):
    return a.reshape(-1, a.shape[-1])


def _row_tile(rows, cols):
    for t in (512, 256, 128, 64, 32, 16):
        if rows % t == 0 and t * cols * 4 <= (1 << 20):
            return t
    return rows


def _adamw_layer(w, m, v, received, sent, sibling, outs, slot):
    _, rows, cols = received.shape
    tr = _row_tile(rows, cols)
    by_columns = rows % tr != 0 or tr == rows and rows * cols * 4 > (2 << 20)
    if by_columns:
        assert w.shape[0] == rows and cols % (2 * LANE) == 0, (w.shape, received.shape)
        tr, tc, steps = rows, 2 * LANE, cols // (2 * LANE)
        index = lambda i: (0, i)
    else:
        tc, steps, first = cols, rows // tr, slot * (rows // tr)
        index = lambda i: (i, 0)
    where = (2 * lax.axis_index("x") + lax.axis_index("y")).astype(jnp.int32).reshape(1)

    def body(where_ref, w_ref, m_ref, v_ref, r_ref, own_ref, s_ref, *rest):
        g_ref, d_ref, nm_ref, nv_ref = rest[4:]
        me = where_ref[0]
        mine = theirs = None
        for k in range(N_CHIPS):
            a = jnp.where(me == k, own_ref[...], r_ref[k]).astype(F32)
            b = s_ref[k].astype(F32)
            mine = a if mine is None else mine + a
            theirs = b if theirs is None else theirs + b
        g = mine + theirs
        delta, nm, nv = _adamw_math(w_ref[...], g, m_ref[...], v_ref[...])
        g_ref[...] = g
        d_ref[...] = delta
        nm_ref[...] = nm
        nv_ref[...] = nv

    if by_columns:
        stacked = pl.BlockSpec((tr, tc), lambda i, where_ref: index(i))
    else:
        stacked = pl.BlockSpec((tr, tc), lambda i, where_ref: (first + i, 0))
    four = pl.BlockSpec((N_CHIPS, tr, tc), lambda i, where_ref: (0,) + index(i))
    own = pl.BlockSpec((None, tr, tc), lambda i, where_ref: (where_ref[0],) + index(i))
    grid_spec = pltpu.PrefetchScalarGridSpec(
        num_scalar_prefetch=1, grid=(steps,),
        in_specs=[stacked, stacked, stacked, four, own, four] + [ANY_SPEC] * 4, out_specs=[stacked] * 4)
    return pl.pallas_call(body, out_shape=[jax.ShapeDtypeStruct(w.shape, F32)] * 4, grid_spec=grid_spec,
                          input_output_aliases={7 + k: k for k in range(4)}, name="adamw_layer",
                          compiler_params=_params(("parallel",)))(where, w, m, v, received, sent, sibling, *outs)


def _adamw_math(w, g, m, v):
    m = ADAM_B1 * m + (1.0 - ADAM_B1) * g
    v = ADAM_B2 * v + (1.0 - ADAM_B2) * (g * g)
    m_hat = m * (1.0 / (1.0 - ADAM_B1 ** ADAM_STEP))
    v_hat = v * (1.0 / (1.0 - ADAM_B2 ** ADAM_STEP))
    delta = -ADAM_LR * (m_hat / (jnp.sqrt(v_hat) + ADAM_EPS) + ADAM_WD * w)
    return delta, m, v


def _adamw(w, m, v, g_mine, g_sibling):
    rows, cols = w.shape
    tr = _row_tile(rows, cols)
    two = g_sibling is not None

    def body(*refs):
        if two:
            w_ref, m_ref, v_ref, ga_ref, gb_ref, g_ref, d_ref, nm_ref, nv_ref = refs
            g = ga_ref[...] + gb_ref[...]
        else:
            w_ref, m_ref, v_ref, ga_ref, g_ref, d_ref, nm_ref, nv_ref = refs
            g = ga_ref[...]
        delta, nm, nv = _adamw_math(w_ref[...], g, m_ref[...], v_ref[...])
        g_ref[...] = g
        d_ref[...] = delta
        nm_ref[...] = nm
        nv_ref[...] = nv

    blk = pl.BlockSpec((tr, cols), lambda i: (i, 0))
    args = [w, m, v, g_mine] + ([g_sibling] if two else [])
    return pl.pallas_call(body, out_shape=[jax.ShapeDtypeStruct((rows, cols), F32)] * 4, grid=(rows // tr,),
                          in_specs=[blk] * len(args), out_specs=[blk] * 4, name="adamw",
                          compiler_params=_params(("parallel",)))(*args)


def _pack_rows(arrays):
    flat = jnp.concatenate([a.reshape(-1) for a in arrays])
    rows = -(-flat.shape[0] // (8 * LANE)) * 8
    return jnp.pad(flat, (0, rows * LANE - flat.shape[0])).reshape(rows, LANE)


def _unpack_rows(packed, shapes):
    flat = packed.reshape(-1)
    out, at = [], 0
    for s in shapes:
        size = math.prod(s)
        out.append(flat[at:at + size].reshape(s))
        at += size
    return out


def kernel(x, p, positions, norm_g, ffn_w_in, ffn_w_out, ple_w_proj, ple_w_gate, rel_bias, mla_w_a, mla_q_norm, mla_kv_norm, mla_w_uq, mla_w_ukv, mla_w_o, dil_w_qkv, dil_w_o, fox_w_qkvf, fox_b_f, fox_w_o, loss_target, m_norm_g, m_ffn_w_in, m_ffn_w_out, m_ple_w_proj, m_ple_w_gate, m_rel_bias, m_mla_w_a, m_mla_q_norm, m_mla_kv_norm, m_mla_w_uq, m_mla_w_ukv, m_mla_w_o, m_dil_w_qkv, m_dil_w_o, m_fox_w_qkvf, m_fox_b_f, m_fox_w_o, v_norm_g, v_ffn_w_in, v_ffn_w_out, v_ple_w_proj, v_ple_w_gate, v_rel_bias, v_mla_w_a, v_mla_q_norm, v_mla_kv_norm, v_mla_w_uq, v_mla_w_ukv, v_mla_w_o, v_dil_w_qkv, v_dil_w_o, v_fox_w_qkvf, v_fox_b_f, v_fox_w_o):
    w = dict(norm_g=norm_g, ffn_w_in=ffn_w_in, ffn_w_out=ffn_w_out, ple_w_proj=ple_w_proj, ple_w_gate=ple_w_gate,
             rel_bias=rel_bias, mla_w_a=mla_w_a, mla_q_norm=mla_q_norm, mla_kv_norm=mla_kv_norm, mla_w_uq=mla_w_uq,
             mla_w_ukv=mla_w_ukv, mla_w_o=mla_w_o, dil_w_qkv=dil_w_qkv, dil_w_o=dil_w_o, fox_w_qkvf=fox_w_qkvf,
             fox_b_f=fox_b_f, fox_w_o=fox_w_o)
    m = dict(norm_g=m_norm_g, ffn_w_in=m_ffn_w_in, ffn_w_out=m_ffn_w_out, ple_w_proj=m_ple_w_proj,
             ple_w_gate=m_ple_w_gate, rel_bias=m_rel_bias, mla_w_a=m_mla_w_a, mla_q_norm=m_mla_q_norm,
             mla_kv_norm=m_mla_kv_norm, mla_w_uq=m_mla_w_uq, mla_w_ukv=m_mla_w_ukv, mla_w_o=m_mla_w_o,
             dil_w_qkv=m_dil_w_qkv, dil_w_o=m_dil_w_o, fox_w_qkvf=m_fox_w_qkvf, fox_b_f=m_fox_b_f, fox_w_o=m_fox_w_o)
    v = dict(norm_g=v_norm_g, ffn_w_in=v_ffn_w_in, ffn_w_out=v_ffn_w_out, ple_w_proj=v_ple_w_proj,
             ple_w_gate=v_ple_w_gate, rel_bias=v_rel_bias, mla_w_a=v_mla_w_a, mla_q_norm=v_mla_q_norm,
             mla_kv_norm=v_mla_kv_norm, mla_w_uq=v_mla_w_uq, mla_w_ukv=v_mla_w_ukv, mla_w_o=v_mla_w_o,
             dil_w_qkv=v_dil_w_qkv, dil_w_o=v_dil_w_o, fox_w_qkvf=v_fox_w_qkvf, fox_b_f=v_fox_b_f, fox_w_o=v_fox_w_o)
    chip = 2 * lax.axis_index("x") + lax.axis_index("y")
    for tree in (w, m, v):
        tree[TRANSPOSED] = jnp.swapaxes(tree[TRANSPOSED], 1, 2)

    small_shapes = [w[k].shape for k in SMALL_SHARDED]
    order = [(i, part) for i in range(DEPTH) for part in (MIXER_PART, COMMON_PART)]
    gathers = {}
    after = positions
    for i, part in order:
        bufs = [_own_slot(w[k][_layer_slot(k, i)].astype(BF)) for k in _part_names(i, part)]
        if (i, part) == order[0]:
            bufs.append(_own_slot(_pack_rows([w[k] for k in SMALL_SHARDED])))
        gathers[i, part] = _spread_start(bufs, None, after, f"gather_start_{i}_{part}")
        after = gathers[i, part]["token"]
    all_started = after
    state = {}

    def get_part(i, part, after_array):
        is_first = (i, part) == order[0]
        lands = _spread_wait(gathers[i, part], all_started if is_first else after_array, f"gather_wait_{i}_{part}")
        if is_first:
            pieces = [_unpack_rows(lands[-1][k], small_shapes) for k in range(N_CHIPS)]
            small = {name: jnp.concatenate([pieces[k][idx] for k in range(N_CHIPS)], axis=-1)
                     for idx, name in enumerate(SMALL_SHARDED)}
            state["small"] = dict(small, rel_bias=rel_bias, fox_b_f=fox_b_f)
        chunks = dict(zip(_part_names(i, part), lands))
        state[i, part] = {k: a.shape for k, a in chunks.items()}
        return _part_to_compute(i, part, chunks)

    started, forwards = [], {}

    def forward_oldest(after_array):
        i, part, handle = started.pop(0)
        received, sent = _spread_wait(handle, after_array, f"exchange_wait_{i}_{part}")
        forwards[i, part] = _sibling_start(received, sent, after_array, f"sibling_start_{i}_{part}")
        return forwards[i, part]["token"]

    def put_part(i, part, lg):
        contrib = _part_contributions(i, part, lg, state[i, part])
        srcs = [contrib[k] for k in _part_names(i, part)]
        handle = _spread_start([lax.empty(s.shape, s.dtype) for s in srcs], srcs, positions,
                               f"exchange_start_{i}_{part}")
        token = handle["token"]
        if started:
            token = token + forward_oldest(token)
        started.append((i, part, handle))
        return token

    sq, grad_x, sg = _run_layers(x[0], p[:, 0], positions[0], loss_target[0], get_part, lambda: state["small"],
                                 put_part)
    loss = lax.psum(0.5 / D_MODEL * jnp.sum(sq), ("x", "y", "c"))
    forward_oldest(grad_x)

    outs = {k: [lax.empty(_as_2d(w[k]).shape, F32) for _ in range(4)] for k in BIG}
    for i, part in [(i, part) for i in reversed(range(DEPTH)) for part in (COMMON_PART, MIXER_PART)]:
        received, sent, sibling = _sibling_wait(forwards[i, part], grad_x, f"sibling_wait_{i}_{part}")
        for k, r, s, t in zip(_part_names(i, part), received, sent, sibling):
            outs[k] = _adamw_layer(_as_2d(w[k]), _as_2d(m[k]), _as_2d(v[k]), r, s, t, outs[k], _layer_slot(k, i))
    results = {k: [o.reshape(w[k].shape) for o in outs[k]] for k in BIG}
    results[TRANSPOSED] = [jnp.swapaxes(o, 1, 2) for o in results[TRANSPOSED]]

    small_all = SMALL_SHARDED + SMALL_REPLICATED
    full_shapes = [sg[k].shape for k in small_all]
    reduced = dict(zip(small_all, _unpack_rows(_all_reduce_small(_pack_rows([sg[k] for k in small_all])), full_shapes)))
    local_g = []
    for k in small_all:
        g = reduced[k]
        if k in SMALL_SHARDED:
            width = w[k].shape[-1]
            g = lax.dynamic_slice_in_dim(g, chip * width, width, axis=g.ndim - 1)
        local_g.append(g)
    local_shapes = [w[k].shape for k in small_all]
    outs = _adamw(_pack_rows([w[k] for k in small_all]), _pack_rows([m[k] for k in small_all]),
                  _pack_rows([v[k] for k in small_all]), _pack_rows(local_g), None)
    unpacked = [_unpack_rows(o, local_shapes) for o in outs]
    for idx, k in enumerate(small_all):
        results[k] = [u[idx] for u in unpacked]

    return (loss, grad_x[None], *[results[k][0] for k in WEIGHTS], *[results[k][1] for k in WEIGHTS],
            *[results[k][2] for k in WEIGHTS], *[results[k][3] for k in WEIGHTS])
```

```python
import functools
import math

import jax
import jax.numpy as jnp
from jax import lax
from jax.experimental import pallas as pl
from jax.experimental.pallas import tpu as pltpu

F32 = jnp.float32
BF = jnp.bfloat16
MESH = pl.DeviceIdType.MESH
HBM_SPEC = pl.BlockSpec(memory_space=pltpu.HBM)

D_MODEL = 1024
DEPTH = 4
N_MIXERS = 3
D_FF = 2816
NORM_EPS = 1e-6
NEG_INF = -1e30
LANE = 128
HEADS = 16
HEAD_DIM = 64
MLA_Q_RANK = 384
MLA_KV_RANK = 256
MLA_ROPE = 32
MLA_A_PAD = 768
ROPE_THETA = 10000.0
DIL_PATTERNS = ((128, 1), (512, 4), (2048, 16))
Q_BLOCK = 128
DIL_PAIRS = 2
REL_BUCKETS = 32
REL_MAX_DIST = 2048
N_CHIPS = 4
N_DEV = 8

ADAM_LR = 0.001
ADAM_B1 = 0.9
ADAM_B2 = 0.999
ADAM_EPS = 1e-08
ADAM_WD = 0.01
ADAM_STEP = 10

VMEM_LIMIT = 56 * 1024 * 1024
MATMUL_VMEM_BUDGET = 36 * 1024 * 1024
ROW_TILE = 512
ATTN_TILE = 256
ATTN_Q_TILE = 512
MLA_GROUP = 4
FOX_GROUP = 4


def _params(sem=None):
    return pltpu.CompilerParams(dimension_semantics=sem, vmem_limit_bytes=VMEM_LIMIT)


def _divisor_tiles(dim):
    tiles = [t for t in range(LANE, dim + 1, LANE) if dim % t == 0]
    return tiles or [dim]


def _matmul_tiles(m, n, k, a_bytes, b_bytes, out_bytes, has_add, n_unit=None, k_unit=None):
    best = None
    for tm in _divisor_tiles(m):
        for tn in _divisor_tiles(n_unit or n):
            for tk in _divisor_tiles(k_unit or k):
                if max(tm, tn, tk) > 2048:
                    continue
                vmem = 2 * (tm * tk * a_bytes + tk * tn * b_bytes + tm * tn * out_bytes) + tm * tn * 4
                if has_add:
                    vmem += 2 * tm * tn * 4
                if vmem > MATMUL_VMEM_BUDGET:
                    continue
                steps = (m // tm) * (n // tn) * (k // tk)
                traffic = m * k * a_bytes * (n // tn) + k * n * b_bytes * (m // tm) + m * n * out_bytes
                cost = traffic / 3.0e12 + steps * 0.4e-6
                if best is None or cost < best[0]:
                    best = (cost, tm, tn, tk)
    return best[1:]


def _matmul(a, b, *, ta=False, tb=False, b_chunks=False, out_chunks=False, add=None, out_dtype=F32, name):
    k, m = a.shape if ta else a.shape[::-1]
    n_unit = k_unit = None
    if b_chunks:
        chunks, rows_w, c = b.shape
        if tb:
            kb, n, k_unit = chunks * c, rows_w, c
        else:
            kb, n, n_unit = rows_w, chunks * c, c
    else:
        kb, n = b.shape[::-1] if tb else b.shape
    if out_chunks:
        assert n % N_CHIPS == 0 and add is None
        n_unit = n // N_CHIPS
    assert k == kb, (a.shape, b.shape, ta, tb)
    tm, tn, tk = _matmul_tiles(m, n, k, a.dtype.itemsize, b.dtype.itemsize, jnp.dtype(out_dtype).itemsize,
                               add is not None, n_unit, k_unit)
    nk = k // tk
    dims = (((0 if ta else 1,), (1 if tb else 0,)), ((), ()))

    def body(*refs):
        if add is None:
            a_ref, b_ref, o_ref, acc_ref = refs
            add_ref = None
        else:
            a_ref, b_ref, add_ref, o_ref, acc_ref = refs
        kk = pl.program_id(2)

        @pl.when(kk == 0)
        def _():
            acc_ref[...] = jnp.zeros_like(acc_ref)

        acc_ref[...] += lax.dot_general(a_ref[...].astype(BF), b_ref[...].astype(BF), dims,
                                        preferred_element_type=F32)

        @pl.when(kk == nk - 1)
        def _():
            r = acc_ref[...]
            if add_ref is not None:
                r = r + add_ref[...].astype(F32)
            o_ref[...] = r.astype(out_dtype)

    a_spec = pl.BlockSpec((tk, tm), lambda i, j, q: (q, i)) if ta else pl.BlockSpec((tm, tk), lambda i, j, q: (i, q))
    if b_chunks and tb:
        per_k = k_unit // tk
        b_spec = pl.BlockSpec((None, tn, tk), lambda i, j, q: (q // per_k, j, q % per_k))
    elif b_chunks:
        per_n = n_unit // tn
        b_spec = pl.BlockSpec((None, tk, tn), lambda i, j, q: (j // per_n, q, j % per_n))
    elif tb:
        b_spec = pl.BlockSpec((tn, tk), lambda i, j, q: (j, q))
    else:
        b_spec = pl.BlockSpec((tk, tn), lambda i, j, q: (q, j))
    if out_chunks:
        per_o = n_unit // tn
        o_spec = pl.BlockSpec((None, tm, tn), lambda i, j, q: (j // per_o, i, j % per_o))
        out_shape = jax.ShapeDtypeStruct((N_CHIPS, m, n_unit), out_dtype)
    else:
        o_spec = pl.BlockSpec((tm, tn), lambda i, j, q: (i, j))
        out_shape = jax.ShapeDtypeStruct((m, n), out_dtype)
    in_specs = [a_spec, b_spec]
    args = [a, b]
    if add is not None:
        in_specs.append(o_spec)
        args.append(add)
    return pl.pallas_call(
        body, out_shape=out_shape, grid=(m // tm, n // tn, nk),
        in_specs=in_specs, out_specs=o_spec, scratch_shapes=[pltpu.VMEM((tm, tn), F32)], name=name,
        compiler_params=_params(("parallel", "parallel", "arbitrary")))(*args)


def _rowwise(body, name, rows, ins, outs, tr=ROW_TILE):
    def row_spec(cols):
        return pl.BlockSpec((tr, cols), lambda i: (i, 0))

    def full_spec(shape):
        zeros = (0,) * len(shape)
        return pl.BlockSpec(shape, lambda i: zeros)

    in_specs = [row_spec(a.shape[1]) if kind == "row" else full_spec(a.shape) for a, kind in ins]
    out_specs = [row_spec(shape[1]) if kind == "row" else full_spec(shape) for shape, _, kind in outs]
    out_shape = [jax.ShapeDtypeStruct(shape, dtype) for shape, dtype, _ in outs]
    return pl.pallas_call(body, out_shape=out_shape, grid=(rows // tr,), in_specs=in_specs, out_specs=out_specs,
                          name=name, compiler_params=_params(("arbitrary",)))(*[a for a, _ in ins])


def _rstd(x):
    return lax.rsqrt(jnp.mean(x * x, axis=-1, keepdims=True) + NORM_EPS)


def _rms_bwd_math(x, g, dy):
    r = _rstd(x)
    gd = dy * g
    dx = r * gd - x * (r * r * r) * jnp.mean(gd * x, axis=-1, keepdims=True)
    dg = jnp.sum(dy * x * r, axis=0, keepdims=True)
    return dx, dg


def _sigmoid(x):
    return 0.5 * jnp.tanh(0.5 * x) + 0.5


def _init_acc(*refs):
    @pl.when(pl.program_id(0) == 0)
    def _():
        for r in refs:
            r[...] = jnp.zeros_like(r)


def _prenorm(h, g):
    rows, cols = h.shape

    def body(h_ref, g_ref, o_ref):
        x = h_ref[...]
        o_ref[...] = (x * _rstd(x) * g_ref[...]).astype(BF)

    return _rowwise(body, "prenorm", rows, [(h, "row"), (g, "full")], [((rows, cols), BF, "row")])[0]


def _post_residual(h, y, g_post, g_pre):
    rows, cols = h.shape
    with_pre = g_pre is not None

    def body(*refs):
        if with_pre:
            h_ref, y_ref, gp_ref, gq_ref, hn_ref, hb_ref = refs
        else:
            h_ref, y_ref, gp_ref, hn_ref, hb_ref = refs
        yv = y_ref[...]
        hn = h_ref[...] + yv * _rstd(yv) * gp_ref[...]
        hn_ref[...] = hn
        hb_ref[...] = (hn * _rstd(hn) * gq_ref[...] if with_pre else hn).astype(BF)

    ins = [(h, "row"), (y, "row"), (g_post, "full")] + ([(g_pre, "full")] if with_pre else [])
    return _rowwise(body, "post_residual_pre" if with_pre else "post_residual", rows, ins,
                    [((rows, cols), F32, "row"), ((rows, cols), BF, "row")])


def _ple_forward(h2, pp, z, g_pre):
    rows, cols = h2.shape

    def body(h_ref, p_ref, z_ref, g_ref, h3_ref, hb_ref):
        h3 = h_ref[...] + p_ref[...] * _sigmoid(z_ref[...])
        h3_ref[...] = h3
        hb_ref[...] = (h3 * _rstd(h3) * g_ref[...]).astype(BF)

    return _rowwise(body, "ple_forward", rows, [(h2, "row"), (pp, "row"), (z, "row"), (g_pre, "full")],
                    [((rows, cols), F32, "row"), ((rows, cols), BF, "row")])


def _ple_loss(h2, pp, z, target):
    rows, cols = h2.shape

    def body(h_ref, p_ref, z_ref, t_ref, dh_ref, sq_ref):
        _init_acc(sq_ref)
        err = h_ref[...] + p_ref[...] * _sigmoid(z_ref[...]) - t_ref[...]
        dh_ref[...] = err * (1.0 / cols)
        sq_ref[...] += jnp.sum(err * err, axis=0, keepdims=True)

    return _rowwise(body, "ple_loss", rows, [(h2, "row"), (pp, "row"), (z, "row"), (target, "row")],
                    [((rows, cols), F32, "row"), ((1, cols), F32, "acc")])


def _ple_backward(dh3, pp, z):
    rows, cols = dh3.shape

    def body(d_ref, p_ref, z_ref, dpp_ref, dz_ref):
        d = d_ref[...]
        s = _sigmoid(z_ref[...])
        dpp_ref[...] = (d * s).astype(BF)
        dz_ref[...] = (d * p_ref[...] * s * (1.0 - s)).astype(BF)

    return _rowwise(body, "ple_backward", rows, [(dh3, "row"), (pp, "row"), (z, "row")],
                    [((rows, cols), BF, "row"), ((rows, cols), BF, "row")])


def _rms_backward(x, g, dy, add, out_dtype):
    rows, cols = x.shape
    with_add = add is not None

    def body(*refs):
        if with_add:
            x_ref, g_ref, dy_ref, add_ref, dx_ref, dg_ref = refs
        else:
            x_ref, g_ref, dy_ref, dx_ref, dg_ref = refs
        _init_acc(dg_ref)
        dx, dg = _rms_bwd_math(x_ref[...], g_ref[...], dy_ref[...].astype(F32))
        if with_add:
            dx = dx + add_ref[...]
        dx_ref[...] = dx.astype(out_dtype)
        dg_ref[...] += dg

    ins = [(x, "row"), (g, "full"), (dy, "row")] + ([(add, "row")] if with_add else [])
    return _rowwise(body, "rms_backward_add" if with_add else "rms_backward", rows, ins,
                    [((rows, cols), out_dtype, "row"), ((1, cols), F32, "acc")])


def _swiglu_forward(gu):
    rows = gu.shape[0]

    def body(gu_ref, o_ref):
        g = gu_ref[:, :D_FF].astype(F32)
        o_ref[...] = (g * _sigmoid(g) * gu_ref[:, D_FF:].astype(F32)).astype(BF)

    return _rowwise(body, "swiglu_forward", rows, [(gu, "row")], [((rows, D_FF), BF, "row")])[0]


def _swiglu_backward(gu, dact):
    rows = gu.shape[0]

    def body(gu_ref, d_ref, o_ref):
        g = gu_ref[:, :D_FF].astype(F32)
        u = gu_ref[:, D_FF:].astype(F32)
        d = d_ref[...].astype(F32)
        s = _sigmoid(g)
        gs = g * s
        o_ref[:, :D_FF] = (d * u * (s + gs * (1.0 - s))).astype(BF)
        o_ref[:, D_FF:] = (d * gs).astype(BF)

    return _rowwise(body, "swiglu_backward", rows, [(gu, "row"), (dact, "row")], [((rows, 2 * D_FF), BF, "row")])[0]


def _rope_tables(positions):
    half = MLA_ROPE // 2
    inv = ROPE_THETA ** (-jnp.arange(half, dtype=F32) / half)
    ang = positions.astype(F32)[:, None] * inv
    cos, sin = jnp.cos(ang), jnp.sin(ang)
    rows = positions.shape[0]
    c = jnp.ones((rows, LANE), F32).at[:, 64:80].set(cos).at[:, 80:96].set(cos)
    sa = jnp.zeros((rows, LANE), F32).at[:, 64:80].set(-sin)
    sb = jnp.zeros((rows, LANE), F32).at[:, 80:96].set(sin)
    return c, sa, sb


def _rope_apply(x, c, sa, sb):
    return x * c + pltpu.roll(x, LANE - 16, 1) * sa + pltpu.roll(x, 16, 1) * sb


def _rope_apply_t(dy, c, sa, sb):
    return dy * c + pltpu.roll(dy * sa, 16, 1) + pltpu.roll(dy * sb, LANE - 16, 1)


def _rope_heads(x, tables, transpose, name):
    rows, cols = x.shape

    def body(x_ref, c_ref, sa_ref, sb_ref, o_ref):
        fn = _rope_apply_t if transpose else _rope_apply
        c, sa, sb = c_ref[...], sa_ref[...], sb_ref[...]
        for head in range(cols // LANE):
            lanes = slice(head * LANE, (head + 1) * LANE)
            o_ref[:, lanes] = fn(x_ref[:, lanes].astype(F32), c, sa, sb).astype(BF)

    blk = pl.BlockSpec((ROW_TILE, cols), lambda i: (i, 0))
    tbl = pl.BlockSpec((ROW_TILE, LANE), lambda i: (i, 0))
    return pl.pallas_call(body, out_shape=jax.ShapeDtypeStruct((rows, cols), BF), grid=(rows // ROW_TILE,),
                          in_specs=[blk, tbl, tbl, tbl], out_specs=blk, name=name,
                          compiler_params=_params(("parallel",)))(x, *tables)


def _mla_mid_forward(a, q_norm, kv_norm, tables):
    rows = a.shape[0]
    qr, kvr = MLA_Q_RANK, MLA_KV_RANK

    def body(a_ref, qn_ref, kn_ref, c_ref, sa_ref, sb_ref, cq_ref, ckv_ref, kr_ref):
        aq = a_ref[:, 0:qr]
        akv = a_ref[:, qr:qr + kvr]
        cq_ref[...] = (aq * _rstd(aq) * qn_ref[...]).astype(BF)
        ckv_ref[...] = (akv * _rstd(akv) * kn_ref[...]).astype(BF)
        kr_ref[...] = _rope_apply(a_ref[:, qr + kvr:], c_ref[...], sa_ref[...], sb_ref[...]).astype(BF)

    ins = [(a, "row"), (q_norm, "full"), (kv_norm, "full")] + [(t, "row") for t in tables]
    return _rowwise(body, "mla_mid_forward", rows, ins,
                    [((rows, qr), BF, "row"), ((rows, kvr), BF, "row"), ((rows, LANE), BF, "row")])


def _mla_mid_backward(a, q_norm, kv_norm, tables, dcq, dckv, dkr):
    rows = a.shape[0]
    qr, kvr = MLA_Q_RANK, MLA_KV_RANK

    def body(a_ref, qn_ref, kn_ref, c_ref, sa_ref, sb_ref, dcq_ref, dckv_ref, dkr_ref, da_ref, dqn_ref, dkn_ref):
        _init_acc(dqn_ref, dkn_ref)
        dxq, dgq = _rms_bwd_math(a_ref[:, 0:qr], qn_ref[...], dcq_ref[...])
        dxk, dgk = _rms_bwd_math(a_ref[:, qr:qr + kvr], kn_ref[...], dckv_ref[...])
        da_ref[:, 0:qr] = dxq.astype(BF)
        da_ref[:, qr:qr + kvr] = dxk.astype(BF)
        da_ref[:, qr + kvr:] = _rope_apply_t(dkr_ref[...], c_ref[...], sa_ref[...], sb_ref[...]).astype(BF)
        dqn_ref[...] += dgq
        dkn_ref[...] += dgk

    ins = ([(a, "row"), (q_norm, "full"), (kv_norm, "full")] + [(t, "row") for t in tables]
           + [(dcq, "row"), (dckv, "row"), (dkr, "row")])
    return _rowwise(body, "mla_mid_backward", rows, ins,
                    [((rows, MLA_A_PAD), BF, "row"), ((1, qr), F32, "acc"), ((1, kvr), F32, "acc")])


def _attn_specs(rows, kv_off, g, many_row_vectors):
    head =pl.BlockSpec((rows, g * LANE), lambda h: (0, h))
    kv_head = pl.BlockSpec((rows, g * LANE), lambda h: (0, h + kv_off // g))
    shared = pl.BlockSpec((rows, LANE), lambda h: (0, 0))
    col_vec = pl.BlockSpec((g, rows, 1), lambda h: (h, 0, 0),
                           pipeline_mode=pl.Buffered(1 if many_row_vectors else 2))
    row_vec = pl.BlockSpec((g, 1, rows), lambda h: (h, 0, 0))
    return head, kv_head, shared, col_vec, row_vec


def _attn_forward(q, kv, kv_off, kr, cum_col, cum_row, scale, group_size, name):
    rows = q.shape[0]
    heads = HEADS
    t = ATTN_TILE
    tq = ATTN_Q_TILE
    per = tq // t
    has_kr = kr is not None
    has_f = cum_col is not None
    group = range(group_size)

    def body(*refs):
        it = iter(refs)
        q_ref, kv_ref = next(it), next(it)
        kr_ref = next(it) if has_kr else None
        cc_ref = next(it) if has_f else None
        cr_ref = next(it) if has_f else None
        o_ref, lse_ref = next(it), next(it)
        lo = lax.broadcasted_iota(jnp.int32, (1, LANE), 1) < HEAD_DIM
        row = lax.broadcasted_iota(jnp.int32, (tq, t), 0)
        col = lax.broadcasted_iota(jnp.int32, (tq, t), 1)
        lanes = [slice(g * LANE, (g + 1) * LANE) for g in group]

        def q_block(i, _):
            qs = pl.ds(pl.multiple_of(i * tq, tq), tq)
            qbs = [q_ref[qs, lanes[g]] for g in group]
            cqs = [cc_ref[g, qs, :] if has_f else None for g in group]

            def step(j, carry, diag):
                ks = pl.ds(pl.multiple_of(j * t, t), t)
                other = kr_ref[ks, :] if has_kr else jnp.zeros((t, LANE), BF)
                kvbs = [kv_ref[ks, lanes[g]] for g in group]
                logits = [lax.dot_general(qbs[g], jnp.where(lo, kvbs[g], other), (((1,), (1,)), ((), ())),
                                          preferred_element_type=F32) for g in group]
                out = []
                for g in group:
                    m, l, acc = carry[g]
                    s = logits[g] * scale
                    if has_f:
                        s = s + (cqs[g] - cr_ref[g, :, ks])
                    if diag is not None:
                        s = jnp.where(col + diag * t <= row, s, NEG_INF)
                    mn = jnp.maximum(m, jnp.max(s, axis=1, keepdims=True))
                    alpha = jnp.exp(m - mn)
                    p = jnp.exp(s - mn)
                    l = alpha * l + jnp.sum(p, axis=1, keepdims=True)
                    acc = alpha * acc + jnp.dot(p.astype(BF), kvbs[g], preferred_element_type=F32)
                    out.append((mn, l, acc))
                return tuple(out)

            init = tuple((jnp.full((tq, 1), NEG_INF, F32), jnp.zeros((tq, 1), F32), jnp.zeros((tq, LANE), F32))
                         for _ in group)
            carry = lax.fori_loop(0, i * per, lambda j, c: step(j, c, None), init)
            for d in range(per):
                carry = step(i * per + d, carry, d)
            for g, (m, l, acc) in enumerate(carry):
                o_ref[qs, lanes[g]] = jnp.where(lo, 0.0, acc * (1.0 / l)).astype(BF)
                lse_ref[g, qs, :] = m + jnp.log(l)
            return 0

        lax.fori_loop(0, rows // tq, q_block, 0)

    head, kv_head, shared, col_vec, row_vec = _attn_specs(rows, kv_off, group_size, has_f)
    in_specs, args = [head, kv_head], [q, kv]
    if has_kr:
        in_specs.append(shared)
        args.append(kr)
    if has_f:
        in_specs += [col_vec, row_vec]
        args += [cum_col, cum_row]
    return pl.pallas_call(
        body, out_shape=[jax.ShapeDtypeStruct((rows, heads * LANE), BF), jax.ShapeDtypeStruct((heads, rows, 1), F32)],
        grid=(heads // group_size,), in_specs=in_specs, out_specs=[head, col_vec], name=name,
        compiler_params=_params(("arbitrary",)))(*args)


def _attn_backward(q, kv, kv_off, kr, cum_col, cum_row, o, do, lse, scale, group_size, name):
    rows = q.shape[0]
    heads = HEADS
    t = ATTN_TILE
    nb = rows // t
    has_kr = kr is not None
    has_f = cum_col is not None
    group = range(group_size)

    def body(*refs):
        it = iter(refs)
        q_ref, kv_ref = next(it), next(it)
        kr_ref = next(it) if has_kr else None
        cc_ref = next(it) if has_f else None
        cr_ref = next(it) if has_f else None
        o_ref, do_ref, lse_ref = next(it), next(it), next(it)
        dq_ref, dkv_ref = next(it), next(it)
        dkr_ref = next(it) if has_kr else None
        dck_ref = next(it) if has_f else None
        dcq_ref = next(it) if has_f else None
        dq_acc = next(it)
        lo = lax.broadcasted_iota(jnp.int32, (1, LANE), 1) < HEAD_DIM
        causal = (lax.broadcasted_iota(jnp.int32, (t, t), 1) <= lax.broadcasted_iota(jnp.int32, (t, t), 0))
        lanes = [slice(g * LANE, (g + 1) * LANE) for g in group]

        dq_acc[...] = jnp.zeros_like(dq_acc)
        if has_kr:
            _init_acc(dkr_ref)
        if has_f:
            dcq_ref[...] = jnp.zeros_like(dcq_ref)

        def kv_block(j, _):
            ks = pl.ds(pl.multiple_of(j * t, t), t)
            other = kr_ref[ks, :] if has_kr else jnp.zeros((t, LANE), BF)
            kvbs = [kv_ref[ks, lanes[g]] for g in group]
            kks = [jnp.where(lo, kvbs[g], other) for g in group]
            cks = [cr_ref[g, :, ks] if has_f else None for g in group]

            def pair(i, carry, diag):
                qs = pl.ds(pl.multiple_of(i * t, t), t)
                out = []
                for g in group:
                    dkk, dvv, dcs = carry[g]
                    qb = q_ref[qs, lanes[g]]
                    dob = do_ref[qs, lanes[g]]
                    s = lax.dot_general(qb, kks[g], (((1,), (1,)), ((), ())), preferred_element_type=F32) * scale
                    if has_f:
                        s = s + (cc_ref[g, qs, :] - cks[g])
                    if diag:
                        s = jnp.where(causal, s, NEG_INF)
                    p = jnp.exp(s - lse_ref[g, qs, :])
                    dp = lax.dot_general(dob, kvbs[g], (((1,), (1,)), ((), ())), preferred_element_type=F32)
                    delta = jnp.sum(dob.astype(F32) * o_ref[qs, lanes[g]].astype(F32), axis=1, keepdims=True)
                    ds = p * (dp - delta)
                    dsb = ds.astype(BF)
                    dvv = dvv + lax.dot_general(p.astype(BF), dob, (((0,), (0,)), ((), ())), preferred_element_type=F32)
                    dkk = dkk + lax.dot_general(dsb, qb, (((0,), (0,)), ((), ())), preferred_element_type=F32)
                    dq_acc[qs, lanes[g]] += jnp.dot(dsb, kks[g], preferred_element_type=F32)
                    if has_f:
                        dcs = dcs + jnp.sum(ds, axis=0, keepdims=True)
                        dcq_ref[g, qs, :] += jnp.sum(ds, axis=1, keepdims=True)
                    out.append((dkk, dvv, dcs))
                return tuple(out)

            init = tuple((jnp.zeros((t, LANE), F32), jnp.zeros((t, LANE), F32), jnp.zeros((1, t), F32)) for _ in group)
            carry = pair(j, init, True)
            carry = lax.fori_loop(j + 1, nb, lambda i, c: pair(i, c, False), carry)
            for g, (dkk, dvv, dcs) in enumerate(carry):
                dkk = dkk * scale
                dkv_ref[ks, lanes[g]] = jnp.where(lo, dkk, dvv).astype(BF)
                if has_kr:
                    dkr_ref[ks, :] += jnp.where(lo, 0.0, dkk)
                if has_f:
                    dck_ref[g, :, ks] = -dcs
            return 0

        lax.fori_loop(0, nb, kv_block, 0)
        dq_ref[...] = (dq_acc[...] * scale).astype(BF)

    head, kv_head, shared, col_vec, row_vec = _attn_specs(rows, kv_off, group_size, has_f)
    in_specs, args = [head, kv_head], [q, kv]
    if has_kr:
        in_specs.append(shared)
        args.append(kr)
    if has_f:
        in_specs += [col_vec, row_vec]
        args += [cum_col, cum_row]
    in_specs += [head, head, col_vec]
    args += [o, do, lse]
    out_shape = [jax.ShapeDtypeStruct((rows, heads * LANE), BF), jax.ShapeDtypeStruct((rows, heads * LANE), BF)]
    out_specs = [head, head]
    if has_kr:
        out_shape.append(jax.ShapeDtypeStruct((rows, LANE), F32))
        out_specs.append(shared)
    if has_f:
        out_shape += [jax.ShapeDtypeStruct((heads, 1, rows), F32), jax.ShapeDtypeStruct((heads, rows, 1), F32)]
        out_specs += [row_vec, col_vec]
    return pl.pallas_call(
        body, out_shape=out_shape, grid=(heads // group_size,), in_specs=in_specs, out_specs=out_specs,
        scratch_shapes=[pltpu.VMEM((rows, group_size * LANE), F32)], name=name,
        compiler_params=_params(("arbitrary",)))(*args)


def _tri_dot(tri, x):
    return jnp.dot(tri, x, preferred_element_type=F32, precision=lax.Precision.HIGHEST)


def _forget_forward(f_raw, b_f):
    rows = f_raw.shape[0]
    t = ATTN_TILE

    def body(f_ref, b_ref, cum_ref):
        tri = (lax.broadcasted_iota(jnp.int32, (t, t), 1) <= lax.broadcasted_iota(jnp.int32, (t, t), 0)).astype(F32)

        def blk(i, carry):
            sl = pl.ds(pl.multiple_of(i * t, t), t)
            xv = f_ref[sl, :] + b_ref[...]
            log_f = jnp.minimum(xv, 0.0) - jnp.log(1.0 + jnp.exp(-jnp.abs(xv)))
            cum_ref[sl, :] = _tri_dot(tri, log_f) + carry
            return carry + jnp.sum(log_f, axis=0, keepdims=True)

        lax.fori_loop(0, rows // t, blk, jnp.zeros((1, LANE), F32))

    return pl.pallas_call(body, out_shape=jax.ShapeDtypeStruct((rows, LANE), F32), name="forget_forward",
                          compiler_params=_params())(f_raw, b_f)


def _forget_backward(f_raw, b_f, dcum):
    rows = f_raw.shape[0]
    t = ATTN_TILE
    nb = rows // t

    def body(f_ref, b_ref, dc_ref, df_ref, db_ref):
        tri = (lax.broadcasted_iota(jnp.int32, (t, t), 1) >= lax.broadcasted_iota(jnp.int32, (t, t), 0)).astype(F32)

        def blk(i, carry):
            later, db = carry
            sl = pl.ds(pl.multiple_of((nb - 1 - i) * t, t), t)
            dc = dc_ref[sl, :]
            dlog = _tri_dot(tri, dc) + later
            xv = f_ref[sl, :] + b_ref[...]
            df = dlog / (1.0 + jnp.exp(xv))
            df_ref[sl, :] = df.astype(BF)
            return later + jnp.sum(dc, axis=0, keepdims=True), db + jnp.sum(df, axis=0, keepdims=True)

        _, db = lax.fori_loop(0, nb, blk, (jnp.zeros((1, LANE), F32), jnp.zeros((1, LANE), F32)))
        db_ref[...] = db

    return pl.pallas_call(body, out_shape=[jax.ShapeDtypeStruct((rows, LANE), BF), jax.ShapeDtypeStruct((1, LANE), F32)],
                          name="forget_backward", compiler_params=_params())(f_raw, b_f, dcum)


def _t5_bucket(dist):
    max_exact = REL_BUCKETS // 2
    n = jnp.maximum(dist.astype(F32), 1.0)
    large = max_exact + (jnp.log(n / max_exact) / math.log(REL_MAX_DIST / max_exact)
                         * (REL_BUCKETS - max_exact)).astype(jnp.int32)
    large = jnp.minimum(large, REL_BUCKETS - 1)
    return jnp.where(dist < max_exact, dist, large)


def _dil_buckets(dilation):
    i = jnp.arange(Q_BLOCK)[:, None]
    j = jnp.arange(Q_BLOCK)[None, :]
    cur = _t5_bucket(jnp.clip(i - j, 0) * dilation).astype(jnp.int32)
    prev = _t5_bucket(jnp.clip(Q_BLOCK + i - j, 0) * dilation).astype(jnp.int32)
    return cur, prev


def _dil_bias_tiles(tbl_ref, bc_ref, bp_ref, bias_ref, group, hp):
    ii = lax.broadcasted_iota(jnp.int32, (Q_BLOCK, Q_BLOCK), 0)
    jj = lax.broadcasted_iota(jnp.int32, (Q_BLOCK, Q_BLOCK), 1)
    for hh in range(2 * DIL_PAIRS):
        col = group * HEADS + 2 * DIL_PAIRS * hp + hh
        acc_c = jnp.zeros((Q_BLOCK, Q_BLOCK), F32)
        acc_p = jnp.zeros((Q_BLOCK, Q_BLOCK), F32)
        for b in range(REL_BUCKETS):
            val = tbl_ref[b, col]
            acc_c = jnp.where(bc_ref[...] == b, val, acc_c)
            acc_p = jnp.where(bp_ref[...] == b, val, acc_p)
        bias_ref[2 * hh] = jnp.where(jj <= ii, acc_c, NEG_INF)
        bias_ref[2 * hh + 1] = jnp.where(jj >= ii, acc_p, NEG_INF)


def _dil_view(qkv, group, dilation):
    if dilation == 1:
        return qkv
    width = 3 * HEADS * HEAD_DIM
    return qkv[:, group * width:(group + 1) * width].reshape(qkv.shape[0] // dilation, dilation * width)


def _dil_specs(group, dilation, length):
    width = DIL_PAIRS * LANE
    per = 8 // DIL_PAIRS

    def col(kind):
        if dilation == 1:
            return pl.BlockSpec((length, width), lambda hp, r: (0, (group * 3 + kind) * per + hp))
        return pl.BlockSpec((length, width), lambda hp, r: (0, (r * 3 + kind) * per + hp))

    out = pl.BlockSpec((length, width), lambda hp, r: (0, r * per + hp))
    tile = pl.BlockSpec((Q_BLOCK, Q_BLOCK), lambda hp, r: (0, 0))
    table = pl.BlockSpec(memory_space=pltpu.SMEM)
    return col, out, tile, table


def _dil_forward(view, group, dilation, table, buckets):
    length = view.shape[0]
    rows = length * dilation
    nb = length // Q_BLOCK
    scale = HEAD_DIM ** -0.5
    qb = Q_BLOCK

    def body(tbl_ref, bc_ref, bp_ref, q_ref, k_ref, v_ref, o_ref, lse_ref, bias_ref):
        hp = pl.program_id(0)

        @pl.when(pl.program_id(1) == 0)
        def _():
            _dil_bias_tiles(tbl_ref, bc_ref, bp_ref, bias_ref, group, hp)

        lo = lax.broadcasted_iota(jnp.int32, (1, LANE), 1) < HEAD_DIM
        nt = (((1,), (1,)), ((), ()))

        def blk(n, first):
            cur = pl.ds(0, qb) if first else pl.ds(pl.multiple_of(n * qb, qb), qb)
            prev = None if first else pl.ds(pl.multiple_of((n - 1) * qb, qb), qb)
            for pair in range(DIL_PAIRS):
                lanes = slice(pair * LANE, (pair + 1) * LANE)
                qn = q_ref[cur, lanes] * scale
                kc, vc = k_ref[cur, lanes], v_ref[cur, lanes]
                if not first:
                    kp, vp = k_ref[prev, lanes], v_ref[prev, lanes]
                outs, lses = [], []
                for hh in range(2):
                    bias = 4 * pair + 2 * hh
                    qm = jnp.where(lo if hh == 0 else ~lo, qn, jnp.zeros_like(qn))
                    s_c = lax.dot_general(qm, kc, nt, preferred_element_type=F32) + bias_ref[bias]
                    m = jnp.max(s_c, axis=1, keepdims=True)
                    if not first:
                        s_p = lax.dot_general(qm, kp, nt, preferred_element_type=F32) + bias_ref[bias + 1]
                        m = jnp.maximum(m, jnp.max(s_p, axis=1, keepdims=True))
                    e_c = jnp.exp(s_c - m)
                    l = jnp.sum(e_c, axis=1, keepdims=True)
                    acc = jnp.dot(e_c.astype(BF), vc, preferred_element_type=F32)
                    if not first:
                        e_p = jnp.exp(s_p - m)
                        l = l + jnp.sum(e_p, axis=1, keepdims=True)
                        acc = acc + jnp.dot(e_p.astype(BF), vp, preferred_element_type=F32)
                    outs.append(acc * (1.0 / l))
                    lses.append(m + jnp.log(l))
                o_ref[cur, lanes] = jnp.where(lo, outs[0], outs[1])
                lse_ref[cur, lanes] = jnp.where(lo, lses[0], lses[1])
            return 0

        blk(0, True)
        if nb > 1:
            lax.fori_loop(1, nb, lambda n, _: blk(n, False), 0)

    col, out, tile, tbl = _dil_specs(group, dilation, length)
    bc, bp = buckets
    o, lse = pl.pallas_call(
        body, out_shape=[jax.ShapeDtypeStruct((length, dilation * D_MODEL), F32)] * 2,
        grid=(8 // DIL_PAIRS, dilation), in_specs=[tbl, tile, tile, col(0), col(1), col(2)], out_specs=[out, out],
        scratch_shapes=[pltpu.VMEM((4 * DIL_PAIRS, qb, qb), F32)], name=f"dilated_forward_{dilation}",
        compiler_params=_params(("arbitrary", "arbitrary")))(
            table, bc, bp, view, view, view)
    return o.reshape(rows, D_MODEL), lse.reshape(rows, D_MODEL)


def _dil_backward(view, group, dilation, table, buckets, do_g, lse, dlt):
    length = view.shape[0]
    rows = length * dilation
    nb = length // Q_BLOCK
    scale = HEAD_DIM ** -0.5
    qb = Q_BLOCK

    def body(tbl_ref, bc_ref, bp_ref, q_ref, k_ref, v_ref, do_ref, lse_ref, dlt_ref,
             dq_ref, dk_ref, dv_ref, db_ref, bias_ref, dk_acc, dv_acc):
        hp = pl.program_id(0)

        @pl.when(pl.program_id(1) == 0)
        def _():
            _dil_bias_tiles(tbl_ref, bc_ref, bp_ref, bias_ref, group, hp)
            db_ref[...] = jnp.zeros_like(db_ref)

        dk_acc[...] = jnp.zeros_like(dk_acc)
        dv_acc[...] = jnp.zeros_like(dv_acc)
        lo = lax.broadcasted_iota(jnp.int32, (1, LANE), 1) < HEAD_DIM
        tn = (((0,), (0,)), ((), ()))
        nt = (((1,), (1,)), ((), ()))

        def blk(n, first):
            cur = pl.ds(0, qb) if first else pl.ds(pl.multiple_of(n * qb, qb), qb)
            prev = None if first else pl.ds(pl.multiple_of((n - 1) * qb, qb), qb)
            for pair in range(DIL_PAIRS):
                lanes = slice(pair * LANE, (pair + 1) * LANE)
                qn = q_ref[cur, lanes] * scale
                don = do_ref[cur, lanes]
                kc, vc = k_ref[cur, lanes], v_ref[cur, lanes]
                if not first:
                    kp, vp = k_ref[prev, lanes], v_ref[prev, lanes]
                lse_n = lse_ref[cur, lanes]
                dlt_n = dlt_ref[cur, lanes]
                dqs = []
                dkc = jnp.zeros((qb, LANE), F32)
                dkp = jnp.zeros((qb, LANE), F32)
                dvc = jnp.zeros((qb, LANE), F32)
                dvp = jnp.zeros((qb, LANE), F32)
                for hh in range(2):
                    bias = 4 * pair + 2 * hh
                    mask = lo if hh == 0 else ~lo
                    qm = jnp.where(mask, qn, jnp.zeros_like(qn))
                    dom = jnp.where(mask, don, jnp.zeros_like(don))
                    lse_h = jnp.max(jnp.where(mask, lse_n, -3e38), axis=1, keepdims=True)
                    dlt_h = jnp.max(jnp.where(mask, dlt_n, -3e38), axis=1, keepdims=True)
                    p_c = jnp.exp(lax.dot_general(qm, kc, nt, preferred_element_type=F32) + bias_ref[bias] - lse_h)
                    ds_c = p_c * (lax.dot_general(dom, vc, nt, preferred_element_type=F32) - dlt_h)
                    db_ref[pair, 2 * hh] += ds_c
                    dsc_b = ds_c.astype(BF)
                    dq = jnp.dot(dsc_b, kc, preferred_element_type=F32)
                    dkc = dkc + lax.dot_general(dsc_b, qm, tn, preferred_element_type=F32)
                    dvc = dvc + lax.dot_general(p_c.astype(BF), dom, tn, preferred_element_type=F32)
                    if not first:
                        p_p = jnp.exp(lax.dot_general(qm, kp, nt, preferred_element_type=F32) + bias_ref[bias + 1] - lse_h)
                        ds_p = p_p * (lax.dot_general(dom, vp, nt, preferred_element_type=F32) - dlt_h)
                        db_ref[pair, 2 * hh + 1] += ds_p
                        dsp_b = ds_p.astype(BF)
                        dq = dq + jnp.dot(dsp_b, kp, preferred_element_type=F32)
                        dkp = dkp + lax.dot_general(dsp_b, qm, tn, preferred_element_type=F32)
                        dvp = dvp + lax.dot_general(p_p.astype(BF), dom, tn, preferred_element_type=F32)
                    dqs.append(dq)
                dq_ref[cur, lanes] = (jnp.where(lo, dqs[0], dqs[1]) * scale).astype(BF)
                dk_acc[cur, lanes] += dkc
                dv_acc[cur, lanes] += dvc
                if not first:
                    dk_acc[prev, lanes] += dkp
                    dv_acc[prev, lanes] += dvp
            return 0

        blk(0, True)
        if nb > 1:
            lax.fori_loop(1, nb, lambda n, _: blk(n, False), 0)
        dk_ref[...] = dk_acc[...].astype(BF)
        dv_ref[...] = dv_acc[...].astype(BF)

    col, out, tile, tbl = _dil_specs(group, dilation, length)
    bc, bp = buckets
    wide = (length, dilation * D_MODEL)
    dq, dk, dv, db = pl.pallas_call(
        body, out_shape=[jax.ShapeDtypeStruct(wide, BF)] * 3 + [jax.ShapeDtypeStruct((8, 4, qb, qb), F32)],
        grid=(8 // DIL_PAIRS, dilation), in_specs=[tbl, tile, tile, col(0), col(1), col(2), out, out, out],
        out_specs=[out, out, out, pl.BlockSpec((DIL_PAIRS, 4, qb, qb), lambda hp, r: (hp, 0, 0, 0))],
        scratch_shapes=[pltpu.VMEM((4 * DIL_PAIRS, qb, qb), F32), pltpu.VMEM((length, DIL_PAIRS * LANE), F32),
                        pltpu.VMEM((length, DIL_PAIRS * LANE), F32)],
        name=f"dilated_backward_{dilation}", compiler_params=_params(("arbitrary", "arbitrary")))(
            table, bc, bp, view, view, view,
            do_g.reshape(wide), lse.reshape(wide), dlt.reshape(wide))
    return dq.reshape(rows, D_MODEL), dk.reshape(rows, D_MODEL), dv.reshape(rows, D_MODEL), db


def _head_sums(x, lo):
    s0 = jnp.sum(jnp.where(lo, x, 0.0), axis=1, keepdims=True)
    s1 = jnp.sum(jnp.where(lo, 0.0, x), axis=1, keepdims=True)
    return jnp.where(lo, s0, s1)


def _dil_merge_forward(outs, lses):
    rows = outs[0].shape[0]

    def body(o0, o1, o2, l0, l1, l2, o_ref):
        ls = [l0[...], l1[...], l2[...]]
        m = jnp.maximum(jnp.maximum(ls[0], ls[1]), ls[2])
        es = [jnp.exp(v - m) for v in ls]
        tot = es[0] + es[1] + es[2]
        o_ref[...] = ((es[0] * o0[...] + es[1] * o1[...] + es[2] * o2[...]) / tot).astype(BF)

    blk = pl.BlockSpec((ROW_TILE, LANE), lambda i, j: (i, j))
    return pl.pallas_call(body, out_shape=jax.ShapeDtypeStruct((rows, D_MODEL), BF), grid=(rows // ROW_TILE, 8),
                          in_specs=[blk] * 6, out_specs=blk, name="dilated_merge_forward",
                          compiler_params=_params(("parallel", "parallel")))(*outs, *lses)


def _dil_merge_backward(outs, lses, do):
    rows = outs[0].shape[0]

    def body(o0, o1, o2, l0, l1, l2, do_ref, d0, d1, d2, t0, t1, t2):
        lo = lax.broadcasted_iota(jnp.int32, (1, LANE), 1) < HEAD_DIM
        ls = [l0[...], l1[...], l2[...]]
        os_ = [o0[...], o1[...], o2[...]]
        m = jnp.maximum(jnp.maximum(ls[0], ls[1]), ls[2])
        es = [jnp.exp(v - m) for v in ls]
        inv = 1.0 / (es[0] + es[1] + es[2])
        alphas = [e * inv for e in es]
        dov = do_ref[...]
        merged = alphas[0] * os_[0] + alphas[1] * os_[1] + alphas[2] * os_[2]
        dot = _head_sums(dov * merged, lo)
        for a, d_ref, t_ref in zip(alphas, (d0, d1, d2), (t0, t1, t2)):
            d_ref[...] = (a * dov).astype(BF)
            t_ref[...] = a * dot

    blk = pl.BlockSpec((ROW_TILE, LANE), lambda i, j: (i, j))
    res = pl.pallas_call(
        body, out_shape=[jax.ShapeDtypeStruct((rows, D_MODEL), BF)] * 3 + [jax.ShapeDtypeStruct((rows, D_MODEL), F32)] * 3,
        grid=(rows // ROW_TILE, 8), in_specs=[blk] * 7, out_specs=[blk] * 6, name="dilated_merge_backward",
        compiler_params=_params(("parallel", "parallel")))(*outs, *lses, do)
    return res[:3], res[3:]


def _rel_bias_grad(dbs, buckets):
    def body(db_ref, bc_ref, bp_ref, o_ref):
        g = pl.program_id(0)
        hp = pl.program_id(1)

        @pl.when((g == 0) & (hp == 0))
        def _():
            o_ref[...] = jnp.zeros_like(o_ref)

        rr = lax.broadcasted_iota(jnp.int32, (REL_BUCKETS, LANE), 0)
        cc = lax.broadcasted_iota(jnp.int32, (REL_BUCKETS, LANE), 1)
        bc = bc_ref[0]
        bp = bp_ref[0]
        acc = jnp.zeros((REL_BUCKETS, LANE), F32)
        for hh in range(2):
            col = g * HEADS + 2 * hp + hh
            d_c = db_ref[0, 0, 2 * hh]
            d_p = db_ref[0, 0, 2 * hh + 1]
            for b in range(REL_BUCKETS):
                val = (jnp.sum(jnp.where(bc == b, d_c, 0.0), keepdims=True)
                       + jnp.sum(jnp.where(bp == b, d_p, 0.0), keepdims=True))
                acc = jnp.where((rr == b) & (cc == col), val, acc)
        o_ref[...] += acc

    db_all = jnp.stack(dbs)
    bc_all = jnp.stack([b[0] for b in buckets])
    bp_all = jnp.stack([b[1] for b in buckets])
    tile = pl.BlockSpec((1, Q_BLOCK, Q_BLOCK), lambda g, hp: (g, 0, 0))
    return pl.pallas_call(
        body, out_shape=jax.ShapeDtypeStruct((REL_BUCKETS, LANE), F32), grid=(3, 8),
        in_specs=[pl.BlockSpec((1, 1, 4, Q_BLOCK, Q_BLOCK), lambda g, hp: (g, hp, 0, 0, 0)), tile, tile],
        out_specs=pl.BlockSpec((REL_BUCKETS, LANE), lambda g, hp: (0, 0)), name="rel_bias_grad",
        compiler_params=_params(("arbitrary", "arbitrary")))(db_all, bc_all, bp_all)


def _mla_forward(hn, w, tables):
    a = _matmul(hn, w["w_a"], name="mla_a")
    cq, ckv, kr = _mla_mid_forward(a, w["q_norm"], w["kv_norm"], tables)
    q_raw = _matmul(cq, w["w_uq"], name="mla_uq")
    q = _rope_heads(q_raw, tables, False, "rope_forward")
    kv = _matmul(ckv, w["w_ukv"], b_chunks=True, out_dtype=BF, name="mla_ukv")
    scale = (HEAD_DIM + MLA_ROPE) ** -0.5
    o, lse = _attn_forward(q, kv, 0, kr, None, None, scale, MLA_GROUP, "mla_attention_forward")
    y = _matmul(o, w["w_o"], name="attn_out")
    return y, dict(hn=hn, a=a, cq=cq, ckv=ckv, kr=kr, q=q, kv=kv, o=o, lse=lse)


def _mla_backward(dy, w, s, tables):
    scale = (HEAD_DIM + MLA_ROPE) ** -0.5
    g = {}
    g["w_o"] = _matmul(s["o"], dy, ta=True, out_dtype=BF, name="attn_out_dw")
    do = _matmul(dy, w["w_o"], tb=True, out_dtype=BF, name="attn_out_dx")
    dq, dkv, dkr = _attn_backward(s["q"], s["kv"], 0, s["kr"], None, None, s["o"], do, s["lse"], scale,
                                  MLA_GROUP, "mla_attention_backward")
    dq_raw = _rope_heads(dq, tables, True, "rope_backward")
    g["w_uq"] = _matmul(s["cq"], dq_raw, ta=True, out_dtype=BF, name="mla_uq_dw")
    dcq = _matmul(dq_raw, w["w_uq"], tb=True, name="mla_uq_dx")
    g["w_ukv"] = _matmul(s["ckv"], dkv, ta=True, out_chunks=True, out_dtype=BF, name="mla_ukv_dw")
    dckv = _matmul(dkv, w["w_ukv"], tb=True, b_chunks=True, name="mla_ukv_dx")
    da, g["q_norm"], g["kv_norm"] = _mla_mid_backward(s["a"], w["q_norm"], w["kv_norm"], tables, dcq, dckv, dkr)
    g["w_a"] = _matmul(s["hn"], da, ta=True, out_dtype=BF, name="mla_a_dw")
    dhn = _matmul(da, w["w_a"], tb=True, name="mla_a_dx")
    return dhn, g


def _fox_forward(hn, w):
    qkv = _matmul(hn, w["w_qkv"], out_dtype=BF, name="fox_qkv")
    f_raw = _matmul(hn, w["w_f"], name="fox_f")
    cum = _forget_forward(f_raw, w["b_f"])
    cum_heads = cum[:, :HEADS].T
    cum_col, cum_row = cum_heads[:, :, None], cum_heads[:, None, :]
    o, lse = _attn_forward(qkv, qkv, HEADS, None, cum_col, cum_row, HEAD_DIM ** -0.5, FOX_GROUP,
                           "fox_attention_forward")
    y = _matmul(o, w["w_o"], name="attn_out")
    return y, dict(hn=hn, qkv=qkv, f_raw=f_raw, cum_col=cum_col, cum_row=cum_row, o=o, lse=lse)


def _fox_backward(dy, w, s):
    g = {}
    g["w_o"] = _matmul(s["o"], dy, ta=True, out_dtype=BF, name="attn_out_dw")
    do = _matmul(dy, w["w_o"], tb=True, out_dtype=BF, name="attn_out_dx")
    dq, dkv, dck, dcq = _attn_backward(s["qkv"], s["qkv"], HEADS, None, s["cum_col"], s["cum_row"], s["o"], do,
                                       s["lse"], HEAD_DIM ** -0.5, FOX_GROUP, "fox_attention_backward")
    dcum = jnp.pad((dck[:, 0, :] + dcq[:, :, 0]).T, ((0, 0), (0, LANE - HEADS)))
    df, g["b_f"] = _forget_backward(s["f_raw"], w["b_f"], dcum)
    dqkv = jnp.concatenate([dq, dkv], axis=1)
    g["w_qkv"] = _matmul(s["hn"], dqkv, ta=True, out_dtype=BF, name="fox_qkv_dw")
    g["w_f"] = _matmul(s["hn"], df, ta=True, out_dtype=BF, name="fox_f_dw")
    dhn = _matmul(dqkv, w["w_qkv"], tb=True, name="fox_qkv_dx")
    dhn = _matmul(df, w["w_f"], tb=True, add=dhn, name="fox_f_dx")
    return dhn, g


def _dil_mixer_forward(hn, w, buckets):
    qkv = _matmul(hn, w["w_qkv"], b_chunks=True, out_dtype=BF, name="dil_qkv")
    views = [_dil_view(qkv, grp, dilation) for grp, (_, dilation) in enumerate(DIL_PATTERNS)]
    outs, lses = [], []
    for grp, (_, dilation) in enumerate(DIL_PATTERNS):
        o_g, lse_g = _dil_forward(views[grp], grp, dilation, w["rel_bias"], buckets[grp])
        outs.append(o_g)
        lses.append(lse_g)
    o = _dil_merge_forward(outs, lses)
    y = _matmul(o, w["w_o"], name="dil_out")
    return y, dict(hn=hn, views=views, outs=outs, lses=lses, o=o)


def _dil_mixer_backward(dy, w, s, buckets):
    g = {}
    g["w_o"] = _matmul(s["o"], dy, ta=True, out_dtype=BF, name="dil_out_dw")
    do = _matmul(dy, w["w_o"], tb=True, name="dil_out_dx")
    do_gs, dlts = _dil_merge_backward(s["outs"], s["lses"], do)
    parts, dbs = [], []
    for grp, (_, dilation) in enumerate(DIL_PATTERNS):
        dq, dk, dv, db = _dil_backward(s["views"][grp], grp, dilation, w["rel_bias"], buckets[grp], do_gs[grp],
                                       s["lses"][grp], dlts[grp])
        parts += [dq, dk, dv]
        dbs.append(db)
    dqkv = jnp.concatenate(parts, axis=1)
    g["rel_bias"] = _rel_bias_grad(dbs, buckets)
    g["w_qkv"] = _matmul(s["hn"], dqkv, ta=True, out_chunks=True, out_dtype=BF, name="dil_qkv_dw")
    dhn = _matmul(dqkv, w["w_qkv"], tb=True, b_chunks=True, name="dil_qkv_dx")
    return dhn, g


def _mixer_weights(i, lw, small):
    mixer, j = i % N_MIXERS, i // N_MIXERS
    if mixer == 0:
        return dict(lw["mixer"], q_norm=small["mla_q_norm"][j][None, :], kv_norm=small["mla_kv_norm"][j][None, :])
    if mixer == 1:
        return dict(lw["mixer"], rel_bias=small["rel_bias"])
    return dict(lw["mixer"], b_f=jnp.pad(small["fox_b_f"][j][None, :], ((0, 0), (0, LANE - HEADS))))


MIXER_PART, COMMON_PART = 0, 1


def _run_layers(x, p, positions, target, get_part, get_small, put_part):
    tables = _rope_tables(positions)
    buckets = [_dil_buckets(d) for _, d in DIL_PATTERNS]
    layers, saved = [], []
    h = x
    first = get_part(0, MIXER_PART, positions)
    small = get_small()

    def gain(i, k):
        return small["norm_g"][i, k][None, :]

    hn = _prenorm(h, gain(0, 0))
    sq = dh = None
    for i in range(DEPTH):
        mixer = i % N_MIXERS
        lw = dict(mixer=first if i == 0 else get_part(i, MIXER_PART, h))
        mw = _mixer_weights(i, lw, small)
        if mixer == 0:
            y, ms = _mla_forward(hn, mw, tables)
        elif mixer == 1:
            y, ms = _dil_mixer_forward(hn, mw, buckets)
        else:
            y, ms = _fox_forward(hn, mw)
        lw.update(get_part(i, COMMON_PART, y))
        layers.append(lw)
        h1, hn2 = _post_residual(h, y, gain(i, 1), gain(i, 2))
        gu = _matmul(hn2, lw["ffn_w_in"], b_chunks=True, out_dtype=BF, name="ffn_in")
        act = _swiglu_forward(gu)
        f = _matmul(act, lw["ffn_w_out"], name="ffn_out")
        h2, h2b = _post_residual(h1, f, gain(i, 3), None)
        pp = _matmul(p[i], lw["ple_w_proj"], b_chunks=True, name="ple_proj")
        z = _matmul(h2b, lw["ple_w_gate"], name="ple_gate")
        saved.append(dict(h=h, y=y, ms=ms, h1=h1, hn2=hn2, gu=gu, act=act, f=f, h2b=h2b, pp=pp, z=z))
        if i + 1 < DEPTH:
            h, hn = _ple_forward(h2, pp, z, gain(i + 1, 0))
        else:
            dh, sq = _ple_loss(h2, pp, z, target)

    norm_rows = [[None] * 4 for _ in range(DEPTH)]
    sg = dict(mla_q_norm={}, mla_kv_norm={}, rel_bias=None, fox_b_f={})
    for i in reversed(range(DEPTH)):
        s, lw = saved[i], layers[i]
        mixer, j = i % N_MIXERS, i // N_MIXERS
        mw = _mixer_weights(i, lw, small)
        lg = {}
        dpp, dz = _ple_backward(dh, s["pp"], s["z"])
        lg["ple_w_proj"] = _matmul(p[i], dpp, ta=True, out_chunks=True, out_dtype=BF, name="ple_proj_dw")
        lg["ple_w_gate"] = _matmul(s["h2b"], dz, ta=True, out_dtype=BF, name="ple_gate_dw")
        dh2 = _matmul(dz, lw["ple_w_gate"], tb=True, add=dh, name="ple_gate_dx")
        df, norm_rows[i][3] = _rms_backward(s["f"], gain(i, 3), dh2, None, BF)
        lg["ffn_w_out"] = _matmul(s["act"], df, ta=True, out_dtype=BF, name="ffn_out_dw")
        dact = _matmul(df, lw["ffn_w_out"], tb=True, out_dtype=BF, name="ffn_out_dx")
        dgu = _swiglu_backward(s["gu"], dact)
        lg["ffn_w_in"] = _matmul(s["hn2"], dgu, ta=True, out_chunks=True, out_dtype=BF, name="ffn_in_dw")
        token = put_part(i, COMMON_PART, lg)
        dhn2 = _matmul(dgu, lw["ffn_w_in"], tb=True, b_chunks=True, name="ffn_in_dx")
        dh1, norm_rows[i][2] = _rms_backward(s["h1"], gain(i, 2), dhn2, dh2, F32)
        dy, norm_rows[i][1] = _rms_backward(s["y"], gain(i, 1) + token[0:1, 0:1], dh1, None, BF)
        if mixer == 0:
            dhn, mg = _mla_backward(dy, mw, s["ms"], tables)
            sg["mla_q_norm"][j] = mg.pop("q_norm")
            sg["mla_kv_norm"][j] = mg.pop("kv_norm")
        elif mixer == 1:
            dhn, mg = _dil_mixer_backward(dy, mw, s["ms"], buckets)
            rel = mg.pop("rel_bias")[:, :3 * HEADS]
            sg["rel_bias"] = rel if sg["rel_bias"] is None else sg["rel_bias"] + rel
        else:
            dhn, mg = _fox_backward(dy, mw, s["ms"])
            sg["fox_b_f"][j] = mg.pop("b_f")[:, :HEADS]
        token = put_part(i, MIXER_PART, mg)
        dh, norm_rows[i][0] = _rms_backward(s["h"], gain(i, 0) + token[0:1, 0:1], dhn, dh1, F32)
    small_grads = dict(norm_g=jnp.stack([jnp.concatenate(row, axis=0) for row in norm_rows]),
                       rel_bias=sg["rel_bias"])
    for k in ("mla_q_norm", "mla_kv_norm", "fox_b_f"):
        small_grads[k] = jnp.concatenate([sg[k][j] for j in sorted(sg[k])], axis=0)
    return sq, dh, small_grads


COL_SHARDED = ("ffn_w_in", "ple_w_proj", "mla_w_uq", "mla_w_ukv", "dil_w_qkv", "fox_w_qkvf")
ROW_SHARDED = ("ffn_w_out", "ple_w_gate", "mla_w_a", "mla_w_o", "dil_w_o", "fox_w_o")
BIG = ("ffn_w_in", "ffn_w_out", "ple_w_proj", "ple_w_gate", "mla_w_a", "mla_w_uq", "mla_w_ukv", "mla_w_o",
       "dil_w_qkv", "dil_w_o", "fox_w_qkvf", "fox_w_o")
SMALL_SHARDED = ("norm_g", "mla_q_norm", "mla_kv_norm")
SMALL_REPLICATED = ("rel_bias", "fox_b_f")
WEIGHTS = ("norm_g", "ffn_w_in", "ffn_w_out", "ple_w_proj", "ple_w_gate", "rel_bias", "mla_w_a", "mla_q_norm",
           "mla_kv_norm", "mla_w_uq", "mla_w_ukv", "mla_w_o", "dil_w_qkv", "dil_w_o", "fox_w_qkvf", "fox_b_f", "fox_w_o")


TRANSPOSED = "fox_w_qkvf"
LAYER_COMMON = ("ffn_w_in", "ffn_w_out", "ple_w_proj", "ple_w_gate")
MIXER_WEIGHTS = (("mla_w_a", "mla_w_uq", "mla_w_ukv", "mla_w_o"), ("dil_w_qkv", "dil_w_o"), ("fox_w_qkvf", "fox_w_o"))


def _part_names(i, part):
    return MIXER_WEIGHTS[i % N_MIXERS] if part == MIXER_PART else LAYER_COMMON


def _layer_slot(name, i):
    return i if name in LAYER_COMMON else i // N_MIXERS


def _merge_rows(chunks):
    n, r, c = chunks.shape
    return chunks.reshape(n * r, c)


def _merge_cols(chunks):
    n, r, c = chunks.shape
    return chunks.transpose(1, 0, 2).reshape(r, n * c)


def _pad_heads_out(wo):
    w3 = wo.reshape(HEADS, HEAD_DIM, D_MODEL)
    return jnp.pad(w3, ((0, 0), (HEAD_DIM, 0), (0, 0))).reshape(HEADS * LANE, D_MODEL)


def _part_to_compute(i, part, ch):
    if part == COMMON_PART:
        return dict(ffn_w_in=ch["ffn_w_in"], ffn_w_out=_merge_rows(ch["ffn_w_out"]), ple_w_proj=ch["ple_w_proj"],
                    ple_w_gate=_merge_rows(ch["ple_w_gate"]))
    lw = {}
    mixer = i % N_MIXERS
    if mixer == 0:
        wa = _merge_rows(ch["mla_w_a"])
        rank = MLA_Q_RANK + MLA_KV_RANK
        wa_p = jnp.concatenate([wa[:, :rank], jnp.zeros((wa.shape[0], 64), wa.dtype), wa[:, rank:],
                                jnp.zeros((wa.shape[0], 32), wa.dtype)], axis=1)
        wuq = _merge_cols(ch["mla_w_uq"]).reshape(MLA_Q_RANK, HEADS, HEAD_DIM + MLA_ROPE)
        wuq_p = jnp.pad(wuq, ((0, 0), (0, 0), (0, LANE - HEAD_DIM - MLA_ROPE))).reshape(MLA_Q_RANK, HEADS * LANE)
        lw["mixer"] = dict(w_a=wa_p, w_uq=wuq_p, w_ukv=ch["mla_w_ukv"], w_o=_pad_heads_out(_merge_rows(ch["mla_w_o"])))
    elif mixer == 1:
        lw["mixer"] = dict(w_qkv=ch["dil_w_qkv"], w_o=_merge_rows(ch["dil_w_o"]))
    else:
        wf = _merge_rows(ch["fox_w_qkvf"]).T
        inner = HEADS * HEAD_DIM
        q3 = wf[:, :inner].reshape(D_MODEL, HEADS, HEAD_DIM)
        k3 = wf[:, inner:2 * inner].reshape(D_MODEL, HEADS, HEAD_DIM)
        v3 = wf[:, 2 * inner:3 * inner].reshape(D_MODEL, HEADS, HEAD_DIM)
        q_p = jnp.pad(q3, ((0, 0), (0, 0), (0, HEAD_DIM))).reshape(D_MODEL, HEADS * LANE)
        kv_p = jnp.concatenate([k3, v3], axis=2).reshape(D_MODEL, HEADS * LANE)
        f_p = jnp.pad(wf[:, 3 * inner:], ((0, 0), (0, LANE - HEADS)))
        lw["mixer"] = dict(w_qkv=jnp.concatenate([q_p, kv_p], axis=1), w_f=f_p,
                           w_o=_pad_heads_out(_merge_rows(ch["fox_w_o"])))
    return lw["mixer"]


def _part_contributions(i, part, lg, chunk_shapes):
    spec = {k: jax.ShapeDtypeStruct(s, BF) for k, s in chunk_shapes.items()}
    (contrib,) = jax.linear_transpose(functools.partial(_part_to_compute, i, part), spec)(lg)
    return contrib


def _chip_peers():
    x, y, c = lax.axis_index("x"), lax.axis_index("y"), lax.axis_index("c")
    peers = [(1 - x, y), (x, 1 - y), (1 - x, 1 - y)]
    return x, y, c, peers


SEM_SPEC = pl.BlockSpec(memory_space=pltpu.SEMAPHORE)
ANY_SPEC = pl.BlockSpec(memory_space=pl.ANY)
SPLIT_EFFECT = pltpu.SideEffectType.DATAFLOW_SIDE_EFFECTING


def _own_slot(shard):
    me = 2 * lax.axis_index("x") + lax.axis_index("y")
    return lax.dynamic_update_index_in_dim(lax.empty((N_CHIPS,) + shard.shape, shard.dtype), shard[None], me, 0)


def _spread_copy(src, land, k, peer, c, send_sems, recv_sems, index, src_slot, slot):
    px, py = peer
    return pltpu.make_async_remote_copy(
        src_ref=src.at[src_slot], dst_ref=land.at[slot],
        send_sem=send_sems.at[3 * index + k], recv_sem=recv_sems.at[3 * index + k],
        device_id=(px, py, c), device_id_type=MESH)


def _spread_start(bufs, srcs, after, name):
    n = len(bufs)
    exchange = srcs is not None
    arrays = (list(srcs) if exchange else []) + list(bufs)
    na = len(arrays)

    def body(*refs):
        src, land = refs[:n], refs[na - n:na]
        send_sems, recv_sems = refs[na + 1], refs[na + 2]
        token = refs[-1]
        x, y, c, peers = _chip_peers()
        me = 2 * x + y
        for w in range(n):
            for k, peer in enumerate(peers):
                src_slot = 2 * peer[0] + peer[1] if exchange else me
                _spread_copy(src[w], land[w], k, peer, c, send_sems, recv_sems, w, src_slot, me).start()
        token[...] = jnp.zeros_like(token)

    hbm = [pltpu.with_memory_space_constraint(a, pltpu.HBM) for a in arrays]
    out = pl.pallas_call(
        body, name=name,
        out_shape=(pltpu.SemaphoreType.DMA((3 * n,)), pltpu.SemaphoreType.DMA((3 * n,)),
                   *[pltpu.HBM(a.shape, a.dtype) for a in hbm], jax.ShapeDtypeStruct((8, LANE), F32)),
        in_specs=[HBM_SPEC] * na + [ANY_SPEC],
        out_specs=(SEM_SPEC, SEM_SPEC, *[HBM_SPEC] * na, pl.BlockSpec(memory_space=pltpu.VMEM)),
        input_output_aliases={w: 2 + w for w in range(na)},
        compiler_params=pltpu.CompilerParams(has_side_effects=SPLIT_EFFECT))(*hbm, after)
    return dict(send=out[0], recv=out[1], arrays=out[2:2 + na], n=n, token=out[-1], exchange=exchange)


def _spread_wait(handle, after, name):
    n, exchange = handle["n"], handle["exchange"]
    arrays = list(handle["arrays"])
    na = len(arrays)

    def body(*refs):
        src, land = refs[:n], refs[na - n:na]
        send_sems, recv_sems = refs[na], refs[na + 1]
        x, y, c, peers = _chip_peers()
        me = 2 * x + y
        for w in range(n):
            for k, peer in enumerate(peers):
                there = 2 * peer[0] + peer[1]
                cp = _spread_copy(src[w], land[w], k, peer, c, send_sems, recv_sems, w, there if exchange else me, there)
                cp.wait_send()
                cp.wait_recv()

    out = pl.pallas_call(
        body, name=name, out_shape=tuple(pltpu.HBM(a.shape, a.dtype) for a in arrays),
        in_specs=[HBM_SPEC] * na + [SEM_SPEC, SEM_SPEC, ANY_SPEC], out_specs=tuple([HBM_SPEC] * na),
        input_output_aliases={w: w for w in range(na)},
        compiler_params=pltpu.CompilerParams(has_side_effects=SPLIT_EFFECT))(*arrays, handle["send"], handle["recv"], after)
    return (list(out[n:]), list(out[:n])) if exchange else list(out)


def _sibling_copy(received, sent, land, k, me, peers, sibling, send_sems, recv_sems, index):
    slot = me if k == 3 else 2 * peers[k][0] + peers[k][1]
    src = sent if k == 3 else received
    return pltpu.make_async_remote_copy(
        src_ref=src.at[slot], dst_ref=land.at[slot], send_sem=send_sems.at[4 * index + k],
        recv_sem=recv_sems.at[4 * index + k], device_id=sibling, device_id_type=MESH)


def _sibling_start(received, sent, after, name):
    n = len(received)
    lands = [lax.empty(a.shape, a.dtype) for a in received]
    arrays = list(received) + list(sent) + lands

    def body(*refs):
        rec, snt, land = refs[:n], refs[n:2 * n], refs[2 * n:3 * n]
        send_sems, recv_sems = refs[3 * n + 1], refs[3 * n + 2]
        token = refs[-1]
        x, y, c, peers = _chip_peers()
        for w in range(n):
            for k in range(4):
                _sibling_copy(rec[w], snt[w], land[w], k, 2 * x + y, peers, (x, y, 1 - c), send_sems, recv_sems, w).start()
        token[...] = jnp.zeros_like(token)

    hbm = [pltpu.with_memory_space_constraint(a, pltpu.HBM) for a in arrays]
    out = pl.pallas_call(
        body, name=name,
        out_shape=(pltpu.SemaphoreType.DMA((4 * n,)), pltpu.SemaphoreType.DMA((4 * n,)),
                   *[pltpu.HBM(a.shape, a.dtype) for a in hbm], jax.ShapeDtypeStruct((8, LANE), F32)),
        in_specs=[HBM_SPEC] * (3 * n) + [ANY_SPEC],
        out_specs=(SEM_SPEC, SEM_SPEC, *[HBM_SPEC] * (3 * n), pl.BlockSpec(memory_space=pltpu.VMEM)),
        input_output_aliases={w: 2 + w for w in range(3 * n)},
        compiler_params=pltpu.CompilerParams(has_side_effects=SPLIT_EFFECT))(*hbm, after)
    return dict(send=out[0], recv=out[1], arrays=out[2:2 + 3 * n], n=n, token=out[-1])


def _sibling_wait(handle, after, name):
    n = handle["n"]
    arrays = list(handle["arrays"])

    def body(*refs):
        rec, snt, land = refs[:n], refs[n:2 * n], refs[2 * n:3 * n]
        send_sems, recv_sems = refs[3 * n], refs[3 * n + 1]
        x, y, c, peers = _chip_peers()
        for w in range(n):
            for k in range(4):
                cp = _sibling_copy(rec[w], snt[w], land[w], k, 2 * x + y, peers, (x, y, 1 - c), send_sems, recv_sems, w)
                cp.wait_send()
                cp.wait_recv()

    out = pl.pallas_call(
        body, name=name, out_shape=tuple(pltpu.HBM(a.shape, a.dtype) for a in arrays),
        in_specs=[HBM_SPEC] * (3 * n) + [SEM_SPEC, SEM_SPEC, ANY_SPEC], out_specs=tuple([HBM_SPEC] * (3 * n)),
        input_output_aliases={w: w for w in range(3 * n)},
        compiler_params=pltpu.CompilerParams(has_side_effects=SPLIT_EFFECT))(*arrays, handle["send"], handle["recv"], after)
    return list(out[:n]), list(out[n:2 * n]), list(out[2 * n:])


def _all_reduce_small(v):
    rows = v.shape[0]

    def body(v_ref, sum_ref, slots, send_sems, recv_sems):
        x, y, c = lax.axis_index("x"), lax.axis_index("y"), lax.axis_index("c")
        me = 4 * x + 2 * y + c
        slots[me] = v_ref[...]
        sends = []
        for k in range(1, N_DEV):
            bx, by, bc = (k >> 2) & 1, (k >> 1) & 1, k & 1
            peer = (x ^ bx, y ^ by, c ^ bc)
            rc = pltpu.make_async_remote_copy(src_ref=v_ref, dst_ref=slots.at[me], send_sem=send_sems.at[k],
                                              recv_sem=recv_sems.at[k], device_id=peer, device_id_type=MESH)
            rc.start()
            sends.append(rc)
        for k in range(1, N_DEV):
            bx, by, bc = (k >> 2) & 1, (k >> 1) & 1, k & 1
            src = 4 * (x ^ bx) + 2 * (y ^ by) + (c ^ bc)
            pltpu.make_async_remote_copy(src_ref=v_ref, dst_ref=slots.at[src], send_sem=send_sems.at[k],
                                         recv_sem=recv_sems.at[k], device_id=(x ^ bx, y ^ by, c ^ bc),
                                         device_id_type=MESH).wait_recv()
        for rc in sends:
            rc.wait_send()
        total = slots[0]
        for k in range(1, N_DEV):
            total = total + slots[k]
        sum_ref[...] = total

    vm = pl.BlockSpec(memory_space=pltpu.VMEM)
    return pl.pallas_call(
        body, out_shape=jax.ShapeDtypeStruct((rows, LANE), F32), in_specs=[vm], out_specs=vm,
        scratch_shapes=[pltpu.VMEM((N_DEV, rows, LANE), F32), pltpu.SemaphoreType.DMA((N_DEV,)),
                        pltpu.SemaphoreType.DMA((N_DEV,))], name="all_reduce_small")(v)


def _as_2d(a):
    return a.reshape(-1, a.shape[-1])


def _row_tile(rows, cols):
    for t in (512, 256, 128, 64, 32, 16):
        if rows % t == 0 and t * cols * 4 <= (1 << 20):
            return t
    return rows


def _adamw_layer(w, m, v, received, sent, sibling, outs, slot):
    _, rows, cols = received.shape
    tr = _row_tile(rows, cols)
    by_columns = rows % tr != 0 or tr == rows and rows * cols * 4 > (2 << 20)
    if by_columns:
        assert w.shape[0] == rows and cols % (2 * LANE) == 0, (w.shape, received.shape)
        tr, tc, steps = rows, 2 * LANE, cols // (2 * LANE)
        index = lambda i: (0, i)
    else:
        tc, steps, first = cols, rows // tr, slot * (rows // tr)
        index = lambda i: (i, 0)
    where = (2 * lax.axis_index("x") + lax.axis_index("y")).astype(jnp.int32).reshape(1)

    def body(where_ref, w_ref, m_ref, v_ref, r_ref, own_ref, s_ref, *rest):
        g_ref, d_ref, nm_ref, nv_ref = rest[4:]
        me = where_ref[0]
        mine = theirs = None
        for k in range(N_CHIPS):
            a = jnp.where(me == k, own_ref[...], r_ref[k]).astype(F32)
            b = s_ref[k].astype(F32)
            mine = a if mine is None else mine + a
            theirs = b if theirs is None else theirs + b
        g = mine + theirs
        delta, nm, nv = _adamw_math(w_ref[...], g, m_ref[...], v_ref[...])
        g_ref[...] = g
        d_ref[...] = delta
        nm_ref[...] = nm
        nv_ref[...] = nv

    if by_columns:
        stacked = pl.BlockSpec((tr, tc), lambda i, where_ref: index(i))
    else:
        stacked = pl.BlockSpec((tr, tc), lambda i, where_ref: (first + i, 0))
    four = pl.BlockSpec((N_CHIPS, tr, tc), lambda i, where_ref: (0,) + index(i))
    own = pl.BlockSpec((None, tr, tc), lambda i, where_ref: (where_ref[0],) + index(i))
    grid_spec = pltpu.PrefetchScalarGridSpec(
        num_scalar_prefetch=1, grid=(steps,),
        in_specs=[stacked, stacked, stacked, four, own, four] + [ANY_SPEC] * 4, out_specs=[stacked] * 4)
    return pl.pallas_call(body, out_shape=[jax.ShapeDtypeStruct(w.shape, F32)] * 4, grid_spec=grid_spec,
                          input_output_aliases={7 + k: k for k in range(4)}, name="adamw_layer",
                          compiler_params=_params(("parallel",)))(where, w, m, v, received, sent, sibling, *outs)


def _adamw_math(w, g, m, v):
    m = ADAM_B1 * m + (1.0 - ADAM_B1) * g
    v = ADAM_B2 * v + (1.0 - ADAM_B2) * (g * g)
    m_hat = m * (1.0 / (1.0 - ADAM_B1 ** ADAM_STEP))
    v_hat = v * (1.0 / (1.0 - ADAM_B2 ** ADAM_STEP))
    delta = -ADAM_LR * (m_hat / (jnp.sqrt(v_hat) + ADAM_EPS) + ADAM_WD * w)
    return delta, m, v


def _adamw(w, m, v, g_mine, g_sibling):
    rows, cols = w.shape
    tr = _row_tile(rows, cols)
    two = g_sibling is not None

    def body(*refs):
        if two:
            w_ref, m_ref, v_ref, ga_ref, gb_ref, g_ref, d_ref, nm_ref, nv_ref = refs
            g = ga_ref[...] + gb_ref[...]
        else:
            w_ref, m_ref, v_ref, ga_ref, g_ref, d_ref, nm_ref, nv_ref = refs
            g = ga_ref[...]
        delta, nm, nv = _adamw_math(w_ref[...], g, m_ref[...], v_ref[...])
        g_ref[...] = g
        d_ref[...] = delta
        nm_ref[...] = nm
        nv_ref[...] = nv

    blk = pl.BlockSpec((tr, cols), lambda i: (i, 0))
    args = [w, m, v, g_mine] + ([g_sibling] if two else [])
    return pl.pallas_call(body, out_shape=[jax.ShapeDtypeStruct((rows, cols), F32)] * 4, grid=(rows // tr,),
                          in_specs=[blk] * len(args), out_specs=[blk] * 4, name="adamw",
                          compiler_params=_params(("parallel",)))(*args)


def _pack_rows(arrays):
    flat = jnp.concatenate([a.reshape(-1) for a in arrays])
    rows = -(-flat.shape[0] // (8 * LANE)) * 8
    return jnp.pad(flat, (0, rows * LANE - flat.shape[0])).reshape(rows, LANE)


def _unpack_rows(packed, shapes):
    flat = packed.reshape(-1)
    out, at = [], 0
    for s in shapes:
        size = math.prod(s)
        out.append(flat[at:at + size].reshape(s))
        at += size
    return out


def kernel(x, p, positions, norm_g, ffn_w_in, ffn_w_out, ple_w_proj, ple_w_gate, rel_bias, mla_w_a, mla_q_norm, mla_kv_norm, mla_w_uq, mla_w_ukv, mla_w_o, dil_w_qkv, dil_w_o, fox_w_qkvf, fox_b_f, fox_w_o, loss_target, m_norm_g, m_ffn_w_in, m_ffn_w_out, m_ple_w_proj, m_ple_w_gate, m_rel_bias, m_mla_w_a, m_mla_q_norm, m_mla_kv_norm, m_mla_w_uq, m_mla_w_ukv, m_mla_w_o, m_dil_w_qkv, m_dil_w_o, m_fox_w_qkvf, m_fox_b_f, m_fox_w_o, v_norm_g, v_ffn_w_in, v_ffn_w_out, v_ple_w_proj, v_ple_w_gate, v_rel_bias, v_mla_w_a, v_mla_q_norm, v_mla_kv_norm, v_mla_w_uq, v_mla_w_ukv, v_mla_w_o, v_dil_w_qkv, v_dil_w_o, v_fox_w_qkvf, v_fox_b_f, v_fox_w_o):
    w = dict(norm_g=norm_g, ffn_w_in=ffn_w_in, ffn_w_out=ffn_w_out, ple_w_proj=ple_w_proj, ple_w_gate=ple_w_gate,
             rel_bias=rel_bias, mla_w_a=mla_w_a, mla_q_norm=mla_q_norm, mla_kv_norm=mla_kv_norm, mla_w_uq=mla_w_uq,
             mla_w_ukv=mla_w_ukv, mla_w_o=mla_w_o, dil_w_qkv=dil_w_qkv, dil_w_o=dil_w_o, fox_w_qkvf=fox_w_qkvf,
             fox_b_f=fox_b_f, fox_w_o=fox_w_o)
    m = dict(norm_g=m_norm_g, ffn_w_in=m_ffn_w_in, ffn_w_out=m_ffn_w_out, ple_w_proj=m_ple_w_proj,
             ple_w_gate=m_ple_w_gate, rel_bias=m_rel_bias, mla_w_a=m_mla_w_a, mla_q_norm=m_mla_q_norm,
             mla_kv_norm=m_mla_kv_norm, mla_w_uq=m_mla_w_uq, mla_w_ukv=m_mla_w_ukv, mla_w_o=m_mla_w_o,
             dil_w_qkv=m_dil_w_qkv, dil_w_o=m_dil_w_o, fox_w_qkvf=m_fox_w_qkvf, fox_b_f=m_fox_b_f, fox_w_o=m_fox_w_o)
    v = dict(norm_g=v_norm_g, ffn_w_in=v_ffn_w_in, ffn_w_out=v_ffn_w_out, ple_w_proj=v_ple_w_proj,
             ple_w_gate=v_ple_w_gate, rel_bias=v_rel_bias, mla_w_a=v_mla_w_a, mla_q_norm=v_mla_q_norm,
             mla_kv_norm=v_mla_kv_norm, mla_w_uq=v_mla_w_uq, mla_w_ukv=v_mla_w_ukv, mla_w_o=v_mla_w_o,
             dil_w_qkv=v_dil_w_qkv, dil_w_o=v_dil_w_o, fox_w_qkvf=v_fox_w_qkvf, fox_b_f=v_fox_b_f, fox_w_o=v_fox_w_o)
    chip = 2 * lax.axis_index("x") + lax.axis_index("y")
    for tree in (w, m, v):
        tree[TRANSPOSED] = jnp.swapaxes(tree[TRANSPOSED], 1, 2)

    small_shapes = [w[k].shape for k in SMALL_SHARDED]
    order = [(i, part) for i in range(DEPTH) for part in (MIXER_PART, COMMON_PART)]
    gathers = {}
    after = positions
    for i, part in order:
        bufs = [_own_slot(w[k][_layer_slot(k, i)].astype(BF)) for k in _part_names(i, part)]
        if (i, part) == order[0]:
            bufs.append(_own_slot(_pack_rows([w[k] for k in SMALL_SHARDED])))
        gathers[i, part] = _spread_start(bufs, None, after, f"gather_start_{i}_{part}")
        after = gathers[i, part]["token"]
    all_started = after
    state = {}

    def get_part(i, part, after_array):
        is_first = (i, part) == order[0]
        lands = _spread_wait(gathers[i, part], all_started if is_first else after_array, f"gather_wait_{i}_{part}")
        if is_first:
            pieces = [_unpack_rows(lands[-1][k], small_shapes) for k in range(N_CHIPS)]
            small = {name: jnp.concatenate([pieces[k][idx] for k in range(N_CHIPS)], axis=-1)
                     for idx, name in enumerate(SMALL_SHARDED)}
            state["small"] = dict(small, rel_bias=rel_bias, fox_b_f=fox_b_f)
        chunks = dict(zip(_part_names(i, part), lands))
        state[i, part] = {k: a.shape for k, a in chunks.items()}
        return _part_to_compute(i, part, chunks)

    started, forwards = [], {}

    def forward_oldest(after_array):
        i, part, handle = started.pop(0)
        received, sent = _spread_wait(handle, after_array, f"exchange_wait_{i}_{part}")
        forwards[i, part] = _sibling_start(received, sent, after_array, f"sibling_start_{i}_{part}")
        return forwards[i, part]["token"]

    def put_part(i, part, lg):
        contrib = _part_contributions(i, part, lg, state[i, part])
        srcs = [contrib[k] for k in _part_names(i, part)]
        handle = _spread_start([lax.empty(s.shape, s.dtype) for s in srcs], srcs, positions,
                               f"exchange_start_{i}_{part}")
        token = handle["token"]
        if started:
            token = token + forward_oldest(token)
        started.append((i, part, handle))
        return token

    sq, grad_x, sg = _run_layers(x[0], p[:, 0], positions[0], loss_target[0], get_part, lambda: state["small"],
                                 put_part)
    loss = lax.psum(0.5 / D_MODEL * jnp.sum(sq), ("x", "y", "c"))
    forward_oldest(grad_x)

    outs = {k: [lax.empty(_as_2d(w[k]).shape, F32) for _ in range(4)] for k in BIG}
    for i, part in [(i, part) for i in reversed(range(DEPTH)) for part in (COMMON_PART, MIXER_PART)]:
        received, sent, sibling = _sibling_wait(forwards[i, part], grad_x, f"sibling_wait_{i}_{part}")
        for k, r, s, t in zip(_part_names(i, part), received, sent, sibling):
            outs[k] = _adamw_layer(_as_2d(w[k]), _as_2d(m[k]), _as_2d(v[k]), r, s, t, outs[k], _layer_slot(k, i))
    results = {k: [o.reshape(w[k].shape) for o in outs[k]] for k in BIG}
    results[TRANSPOSED] = [jnp.swapaxes(o, 1, 2) for o in results[TRANSPOSED]]

    small_all = SMALL_SHARDED + SMALL_REPLICATED
    full_shapes = [sg[k].shape for k in small_all]
    reduced = dict(zip(small_all, _unpack_rows(_all_reduce_small(_pack_rows([sg[k] for k in small_all])), full_shapes)))
    local_g = []
    for k in small_all:
        g = reduced[k]
        if k in SMALL_SHARDED:
            width = w[k].shape[-1]
            g = lax.dynamic_slice_in_dim(g, chip * width, width, axis=g.ndim - 1)
        local_g.append(g)
    local_shapes = [w[k].shape for k in small_all]
    outs = _adamw(_pack_rows([w[k] for k in small_all]), _pack_rows([m[k] for k in small_all]),
                  _pack_rows([v[k] for k in small_all]), _pack_rows(local_g), None)
    unpacked = [_unpack_rows(o, local_shapes) for o in outs]
    for idx, k in enumerate(small_all):
        results[k] = [u[idx] for u in unpacked]

    return (loss, grad_x[None], *[results[k][0] for k in WEIGHTS], *[results[k][1] for k in WEIGHTS],
            *[results[k][2] for k in WEIGHTS], *[results[k][3] for k in WEIGHTS])
```

```python
import functools
import math

import jax
import jax.numpy as jnp
from jax import lax
from jax.experimental import pallas as pl
from jax.experimental.pallas import tpu as pltpu

F32 = jnp.float32
BF = jnp.bfloat16
MESH = pl.DeviceIdType.MESH
HBM_SPEC = pl.BlockSpec(memory_space=pltpu.HBM)

D_MODEL = 1024
DEPTH = 4
N_MIXERS = 3
D_FF = 2816
NORM_EPS = 1e-6
NEG_INF = -1e30
LANE = 128
HEADS = 16
HEAD_DIM = 64
MLA_Q_RANK = 384
MLA_KV_RANK = 256
MLA_ROPE = 32
MLA_A_PAD = 768
ROPE_THETA = 10000.0
DIL_PATTERNS = ((128, 1), (512, 4), (2048, 16))
Q_BLOCK = 128
DIL_PAIRS = 2
REL_BUCKETS = 32
REL_MAX_DIST = 2048
N_CHIPS = 4
N_DEV = 8

ADAM_LR = 0.001
ADAM_B1 = 0.9
ADAM_B2 = 0.999
ADAM_EPS = 1e-08
ADAM_WD = 0.01
ADAM_STEP = 10

VMEM_LIMIT = 56 * 1024 * 1024
MATMUL_VMEM_BUDGET = 36 * 1024 * 1024
ROW_TILE = 512
ATTN_TILE = 256
ATTN_Q_TILE = 512
MLA_GROUP = 4
FOX_GROUP = 4


def _params(sem=None):
    return pltpu.CompilerParams(dimension_semantics=sem, vmem_limit_bytes=VMEM_LIMIT)


def _divisor_tiles(dim):
    tiles = [t for t in range(LANE, dim + 1, LANE) if dim % t == 0]
    return tiles or [dim]


def _matmul_tiles(m, n, k, a_bytes, b_bytes, out_bytes, has_add, n_unit=None, k_unit=None):
    best = None
    for tm in _divisor_tiles(m):
        for tn in _divisor_tiles(n_unit or n):
            for tk in _divisor_tiles(k_unit or k):
                if max(tm, tn, tk) > 2048:
                    continue
                vmem = 2 * (tm * tk * a_bytes + tk * tn * b_bytes + tm * tn * out_bytes) + tm * tn * 4
                if has_add:
                    vmem += 2 * tm * tn * 4
                if vmem > MATMUL_VMEM_BUDGET:
                    continue
                steps = (m // tm) * (n // tn) * (k // tk)
                traffic = m * k * a_bytes * (n // tn) + k * n * b_bytes * (m // tm) + m * n * out_bytes
                cost = traffic / 3.0e12 + steps * 0.4e-6
                if best is None or cost < best[0]:
                    best = (cost, tm, tn, tk)
    return best[1:]


def _matmul(a, b, *, ta=False, tb=False, b_chunks=False, out_chunks=False, add=None, out_dtype=F32, name):
    k, m = a.shape if ta else a.shape[::-1]
    n_unit = k_unit = None
    if b_chunks:
        chunks, rows_w, c = b.shape
        if tb:
            kb, n, k_unit = chunks * c, rows_w, c
        else:
            kb, n, n_unit = rows_w, chunks * c, c
    else:
        kb, n = b.shape[::-1] if tb else b.shape
    if out_chunks:
        assert n % N_CHIPS == 0 and add is None
        n_unit = n // N_CHIPS
    assert k == kb, (a.shape, b.shape, ta, tb)
    tm, tn, tk = _matmul_tiles(m, n, k, a.dtype.itemsize, b.dtype.itemsize, jnp.dtype(out_dtype).itemsize,
                               add is not None, n_unit, k_unit)
    nk = k // tk
    dims = (((0 if ta else 1,), (1 if tb else 0,)), ((), ()))

    def body(*refs):
        if add is None:
            a_ref, b_ref, o_ref, acc_ref = refs
            add_ref = None
        else:
            a_ref, b_ref, add_ref, o_ref, acc_ref = refs
        kk = pl.program_id(2)

        @pl.when(kk == 0)
        def _():
            acc_ref[...] = jnp.zeros_like(acc_ref)

        acc_ref[...] += lax.dot_general(a_ref[...].astype(BF), b_ref[...].astype(BF), dims,
                                        preferred_element_type=F32)

        @pl.when(kk == nk - 1)
        def _():
            r = acc_ref[...]
            if add_ref is not None:
                r = r + add_ref[...].astype(F32)
            o_ref[...] = r.astype(out_dtype)

    a_spec = pl.BlockSpec((tk, tm), lambda i, j, q: (q, i)) if ta else pl.BlockSpec((tm, tk), lambda i, j, q: (i, q))
    if b_chunks and tb:
        per_k = k_unit // tk
        b_spec = pl.BlockSpec((None, tn, tk), lambda i, j, q: (q // per_k, j, q % per_k))
    elif b_chunks:
        per_n = n_unit // tn
        b_spec = pl.BlockSpec((None, tk, tn), lambda i, j, q: (j // per_n, q, j % per_n))
    elif tb:
        b_spec = pl.BlockSpec((tn, tk), lambda i, j, q: (j, q))
    else:
        b_spec = pl.BlockSpec((tk, tn), lambda i, j, q: (q, j))
    if out_chunks:
        per_o = n_unit // tn
        o_spec = pl.BlockSpec((None, tm, tn), lambda i, j, q: (j // per_o, i, j % per_o))
        out_shape = jax.ShapeDtypeStruct((N_CHIPS, m, n_unit), out_dtype)
    else:
        o_spec = pl.BlockSpec((tm, tn), lambda i, j, q: (i, j))
        out_shape = jax.ShapeDtypeStruct((m, n), out_dtype)
    in_specs = [a_spec, b_spec]
    args = [a, b]
    if add is not None:
        in_specs.append(o_spec)
        args.append(add)
    return pl.pallas_call(
        body, out_shape=out_shape, grid=(m // tm, n // tn, nk),
        in_specs=in_specs, out_specs=o_spec, scratch_shapes=[pltpu.VMEM((tm, tn), F32)], name=name,
        compiler_params=_params(("parallel", "parallel", "arbitrary")))(*args)


def _rowwise(body, name, rows, ins, outs, tr=ROW_TILE):
    def row_spec(cols):
        return pl.BlockSpec((tr, cols), lambda i: (i, 0))

    def full_spec(shape):
        zeros = (0,) * len(shape)
        return pl.BlockSpec(shape, lambda i: zeros)

    in_specs = [row_spec(a.shape[1]) if kind == "row" else full_spec(a.shape) for a, kind in ins]
    out_specs = [row_spec(shape[1]) if kind == "row" else full_spec(shape) for shape, _, kind in outs]
    out_shape = [jax.ShapeDtypeStruct(shape, dtype) for shape, dtype, _ in outs]
    return pl.pallas_call(body, out_shape=out_shape, grid=(rows // tr,), in_specs=in_specs, out_specs=out_specs,
                          name=name, compiler_params=_params(("arbitrary",)))(*[a for a, _ in ins])


def _rstd(x):
    return lax.rsqrt(jnp.mean(x * x, axis=-1, keepdims=True) + NORM_EPS)


def _rms_bwd_math(x, g, dy):
    r = _rstd(x)
    gd = dy * g
    dx = r * gd - x * (r * r * r) * jnp.mean(gd * x, axis=-1, keepdims=True)
    dg = jnp.sum(dy * x * r, axis=0, keepdims=True)
    return dx, dg


def _sigmoid(x):
    return 0.5 * jnp.tanh(0.5 * x) + 0.5


def _init_acc(*refs):
    @pl.when(pl.program_id(0) == 0)
    def _():
        for r in refs:
            r[...] = jnp.zeros_like(r)


def _prenorm(h, g):
    rows, cols = h.shape

    def body(h_ref, g_ref, o_ref):
        x = h_ref[...]
        o_ref[...] = (x * _rstd(x) * g_ref[...]).astype(BF)

    return _rowwise(body, "prenorm", rows, [(h, "row"), (g, "full")], [((rows, cols), BF, "row")])[0]


def _post_residual(h, y, g_post, g_pre):
    rows, cols = h.shape
    with_pre = g_pre is not None

    def body(*refs):
        if with_pre:
            h_ref, y_ref, gp_ref, gq_ref, hn_ref, hb_ref = refs
        else:
            h_ref, y_ref, gp_ref, hn_ref, hb_ref = refs
        yv = y_ref[...]
        hn = h_ref[...] + yv * _rstd(yv) * gp_ref[...]
        hn_ref[...] = hn
        hb_ref[...] = (hn * _rstd(hn) * gq_ref[...] if with_pre else hn).astype(BF)

    ins = [(h, "row"), (y, "row"), (g_post, "full")] + ([(g_pre, "full")] if with_pre else [])
    return _rowwise(body, "post_residual_pre" if with_pre else "post_residual", rows, ins,
                    [((rows, cols), F32, "row"), ((rows, cols), BF, "row")])


def _ple_forward(h2, pp, z, g_pre):
    rows, cols = h2.shape

    def body(h_ref, p_ref, z_ref, g_ref, h3_ref, hb_ref):
        h3 = h_ref[...] + p_ref[...] * _sigmoid(z_ref[...])
        h3_ref[...] = h3
        hb_ref[...] = (h3 * _rstd(h3) * g_ref[...]).astype(BF)

    return _rowwise(body, "ple_forward", rows, [(h2, "row"), (pp, "row"), (z, "row"), (g_pre, "full")],
                    [((rows, cols), F32, "row"), ((rows, cols), BF, "row")])


def _ple_loss(h2, pp, z, target):
    rows, cols = h2.shape

    def body(h_ref, p_ref, z_ref, t_ref, dh_ref, sq_ref):
        _init_acc(sq_ref)
        err = h_ref[...] + p_ref[...] * _sigmoid(z_ref[...]) - t_ref[...]
        dh_ref[...] = err * (1.0 / cols)
        sq_ref[...] += jnp.sum(err * err, axis=0, keepdims=True)

    return _rowwise(body, "ple_loss", rows, [(h2, "row"), (pp, "row"), (z, "row"), (target, "row")],
                    [((rows, cols), F32, "row"), ((1, cols), F32, "acc")])


def _ple_backward(dh3, pp, z):
    rows, cols = dh3.shape

    def body(d_ref, p_ref, z_ref, dpp_ref, dz_ref):
        d = d_ref[...]
        s = _sigmoid(z_ref[...])
        dpp_ref[...] = (d * s).astype(BF)
        dz_ref[...] = (d * p_ref[...] * s * (1.0 - s)).astype(BF)

    return _rowwise(body, "ple_backward", rows, [(dh3, "row"), (pp, "row"), (z, "row")],
                    [((rows, cols), BF, "row"), ((rows, cols), BF, "row")])


def _rms_backward(x, g, dy, add, out_dtype):
    rows, cols = x.shape
    with_add = add is not None

    def body(*refs):
        if with_add:
            x_ref, g_ref, dy_ref, add_ref, dx_ref, dg_ref = refs
        else:
            x_ref, g_ref, dy_ref, dx_ref, dg_ref = refs
        _init_acc(dg_ref)
        dx, dg = _rms_bwd_math(x_ref[...], g_ref[...], dy_ref[...].astype(F32))
        if with_add:
            dx = dx + add_ref[...]
        dx_ref[...] = dx.astype(out_dtype)
        dg_ref[...] += dg

    ins = [(x, "row"), (g, "full"), (dy, "row")] + ([(add, "row")] if with_add else [])
    return _rowwise(body, "rms_backward_add" if with_add else "rms_backward", rows, ins,
                    [((rows, cols), out_dtype, "row"), ((1, cols), F32, "acc")])


def _swiglu_forward(gu):
    rows = gu.shape[0]

    def body(gu_ref, o_ref):
        g = gu_ref[:, :D_FF].astype(F32)
        o_ref[...] = (g * _sigmoid(g) * gu_ref[:, D_FF:].astype(F32)).astype(BF)

    return _rowwise(body, "swiglu_forward", rows, [(gu, "row")], [((rows, D_FF), BF, "row")])[0]


def _swiglu_backward(gu, dact):
    rows = gu.shape[0]

    def body(gu_ref, d_ref, o_ref):
        g = gu_ref[:, :D_FF].astype(F32)
        u = gu_ref[:, D_FF:].astype(F32)
        d = d_ref[...].astype(F32)
        s = _sigmoid(g)
        gs = g * s
        o_ref[:, :D_FF] = (d * u * (s + gs * (1.0 - s))).astype(BF)
        o_ref[:, D_FF:] = (d * gs).astype(BF)

    return _rowwise(body, "swiglu_backward", rows, [(gu, "row"), (dact, "row")], [((rows, 2 * D_FF), BF, "row")])[0]


def _rope_tables(positions):
    half = MLA_ROPE // 2
    inv = ROPE_THETA ** (-jnp.arange(half, dtype=F32) / half)
    ang = positions.astype(F32)[:, None] * inv
    cos, sin = jnp.cos(ang), jnp.sin(ang)
    rows = positions.shape[0]
    c = jnp.ones((rows, LANE), F32).at[:, 64:80].set(cos).at[:, 80:96].set(cos)
    sa = jnp.zeros((rows, LANE), F32).at[:, 64:80].set(-sin)
    sb = jnp.zeros((rows, LANE), F32).at[:, 80:96].set(sin)
    return c, sa, sb


def _rope_apply(x, c, sa, sb):
    return x * c + pltpu.roll(x, LANE - 16, 1) * sa + pltpu.roll(x, 16, 1) * sb


def _rope_apply_t(dy, c, sa, sb):
    return dy * c + pltpu.roll(dy * sa, 16, 1) + pltpu.roll(dy * sb, LANE - 16, 1)


def _rope_heads(x, tables, transpose, name):
    rows, cols = x.shape

    def body(x_ref, c_ref, sa_ref, sb_ref, o_ref):
        fn = _rope_apply_t if transpose else _rope_apply
        c, sa, sb = c_ref[...], sa_ref[...], sb_ref[...]
        for head in range(cols // LANE):
            lanes = slice(head * LANE, (head + 1) * LANE)
            o_ref[:, lanes] = fn(x_ref[:, lanes].astype(F32), c, sa, sb).astype(BF)

    blk = pl.BlockSpec((ROW_TILE, cols), lambda i: (i, 0))
    tbl = pl.BlockSpec((ROW_TILE, LANE), lambda i: (i, 0))
    return pl.pallas_call(body, out_shape=jax.ShapeDtypeStruct((rows, cols), BF), grid=(rows // ROW_TILE,),
                          in_specs=[blk, tbl, tbl, tbl], out_specs=blk, name=name,
                          compiler_params=_params(("parallel",)))(x, *tables)


def _mla_mid_forward(a, q_norm, kv_norm, tables):
    rows = a.shape[0]
    qr, kvr = MLA_Q_RANK, MLA_KV_RANK

    def body(a_ref, qn_ref, kn_ref, c_ref, sa_ref, sb_ref, cq_ref, ckv_ref, kr_ref):
        aq = a_ref[:, 0:qr]
        akv = a_ref[:, qr:qr + kvr]
        cq_ref[...] = (aq * _rstd(aq) * qn_ref[...]).astype(BF)
        ckv_ref[...] = (akv * _rstd(akv) * kn_ref[...]).astype(BF)
        kr_ref[...] = _rope_apply(a_ref[:, qr + kvr:], c_ref[...], sa_ref[...], sb_ref[...]).astype(BF)

    ins = [(a, "row"), (q_norm, "full"), (kv_norm, "full")] + [(t, "row") for t in tables]
    return _rowwise(body, "mla_mid_forward", rows, ins,
                    [((rows, qr), BF, "row"), ((rows, kvr), BF, "row"), ((rows, LANE), BF, "row")])


def _mla_mid_backward(a, q_norm, kv_norm, tables, dcq, dckv, dkr):
    rows = a.shape[0]
    qr, kvr = MLA_Q_RANK, MLA_KV_RANK

    def body(a_ref, qn_ref, kn_ref, c_ref, sa_ref, sb_ref, dcq_ref, dckv_ref, dkr_ref, da_ref, dqn_ref, dkn_ref):
        _init_acc(dqn_ref, dkn_ref)
        dxq, dgq = _rms_bwd_math(a_ref[:, 0:qr], qn_ref[...], dcq_ref[...])
        dxk, dgk = _rms_bwd_math(a_ref[:, qr:qr + kvr], kn_ref[...], dckv_ref[...])
        da_ref[:, 0:qr] = dxq.astype(BF)
        da_ref[:, qr:qr + kvr] = dxk.astype(BF)
        da_ref[:, qr + kvr:] = _rope_apply_t(dkr_ref[...], c_ref[...], sa_ref[...], sb_ref[...]).astype(BF)
        dqn_ref[...] += dgq
        dkn_ref[...] += dgk

    ins = ([(a, "row"), (q_norm, "full"), (kv_norm, "full")] + [(t, "row") for t in tables]
           + [(dcq, "row"), (dckv, "row"), (dkr, "row")])
    return _rowwise(body, "mla_mid_backward", rows, ins,
                    [((rows, MLA_A_PAD), BF, "row"), ((1, qr), F32, "acc"), ((1, kvr), F32, "acc")])


def _attn_specs(rows, kv_off, g, many_row_vectors):
    head =pl.BlockSpec((rows, g * LANE), lambda h: (0, h))
    kv_head = pl.BlockSpec((rows, g * LANE), lambda h: (0, h + kv_off // g))
    shared = pl.BlockSpec((rows, LANE), lambda h: (0, 0))
    col_vec = pl.BlockSpec((g, rows, 1), lambda h: (h, 0, 0),
                           pipeline_mode=pl.Buffered(1 if many_row_vectors else 2))
    row_vec = pl.BlockSpec((g, 1, rows), lambda h: (h, 0, 0))
    return head, kv_head, shared, col_vec, row_vec


def _attn_forward(q, kv, kv_off, kr, cum_col, cum_row, scale, group_size, name):
    rows = q.shape[0]
    heads = HEADS
    t = ATTN_TILE
    tq = ATTN_Q_TILE
    per = tq // t
    has_kr = kr is not None
    has_f = cum_col is not None
    group = range(group_size)

    def body(*refs):
        it = iter(refs)
        q_ref, kv_ref = next(it), next(it)
        kr_ref = next(it) if has_kr else None
        cc_ref = next(it) if has_f else None
        cr_ref = next(it) if has_f else None
        o_ref, lse_ref = next(it), next(it)
        lo = lax.broadcasted_iota(jnp.int32, (1, LANE), 1) < HEAD_DIM
        row = lax.broadcasted_iota(jnp.int32, (tq, t), 0)
        col = lax.broadcasted_iota(jnp.int32, (tq, t), 1)
        lanes = [slice(g * LANE, (g + 1) * LANE) for g in group]

        def q_block(i, _):
            qs = pl.ds(pl.multiple_of(i * tq, tq), tq)
            qbs = [q_ref[qs, lanes[g]] for g in group]
            cqs = [cc_ref[g, qs, :] if has_f else None for g in group]

            def step(j, carry, diag):
                ks = pl.ds(pl.multiple_of(j * t, t), t)
                other = kr_ref[ks, :] if has_kr else jnp.zeros((t, LANE), BF)
                kvbs = [kv_ref[ks, lanes[g]] for g in group]
                logits = [lax.dot_general(qbs[g], jnp.where(lo, kvbs[g], other), (((1,), (1,)), ((), ())),
                                          preferred_element_type=F32) for g in group]
                out = []
                for g in group:
                    m, l, acc = carry[g]
                    s = logits[g] * scale
                    if has_f:
                        s = s + (cqs[g] - cr_ref[g, :, ks])
                    if diag is not None:
                        s = jnp.where(col + diag * t <= row, s, NEG_INF)
                    mn = jnp.maximum(m, jnp.max(s, axis=1, keepdims=True))
                    alpha = jnp.exp(m - mn)
                    p = jnp.exp(s - mn)
                    l = alpha * l + jnp.sum(p, axis=1, keepdims=True)
                    acc = alpha * acc + jnp.dot(p.astype(BF), kvbs[g], preferred_element_type=F32)
                    out.append((mn, l, acc))
                return tuple(out)

            init = tuple((jnp.full((tq, 1), NEG_INF, F32), jnp.zeros((tq, 1), F32), jnp.zeros((tq, LANE), F32))
                         for _ in group)
            carry = lax.fori_loop(0, i * per, lambda j, c: step(j, c, None), init)
            for d in range(per):
                carry = step(i * per + d, carry, d)
            for g, (m, l, acc) in enumerate(carry):
                o_ref[qs, lanes[g]] = jnp.where(lo, 0.0, acc * (1.0 / l)).astype(BF)
                lse_ref[g, qs, :] = m + jnp.log(l)
            return 0

        lax.fori_loop(0, rows // tq, q_block, 0)

    head, kv_head, shared, col_vec, row_vec = _attn_specs(rows, kv_off, group_size, has_f)
    in_specs, args = [head, kv_head], [q, kv]
    if has_kr:
        in_specs.append(shared)
        args.append(kr)
    if has_f:
        in_specs += [col_vec, row_vec]
        args += [cum_col, cum_row]
    return pl.pallas_call(
        body, out_shape=[jax.ShapeDtypeStruct((rows, heads * LANE), BF), jax.ShapeDtypeStruct((heads, rows, 1), F32)],
        grid=(heads // group_size,), in_specs=in_specs, out_specs=[head, col_vec], name=name,
        compiler_params=_params(("arbitrary",)))(*args)


def _attn_backward(q, kv, kv_off, kr, cum_col, cum_row, o, do, lse, scale, group_size, name):
    rows = q.shape[0]
    heads = HEADS
    t = ATTN_TILE
    nb = rows // t
    has_kr = kr is not None
    has_f = cum_col is not None
    group = range(group_size)

    def body(*refs):
        it = iter(refs)
        q_ref, kv_ref = next(it), next(it)
        kr_ref = next(it) if has_kr else None
        cc_ref = next(it) if has_f else None
        cr_ref = next(it) if has_f else None
        o_ref, do_ref, lse_ref = next(it), next(it), next(it)
        dq_ref, dkv_ref = next(it), next(it)
        dkr_ref = next(it) if has_kr else None
        dck_ref = next(it) if has_f else None
        dcq_ref = next(it) if has_f else None
        dq_acc = next(it)
        lo = lax.broadcasted_iota(jnp.int32, (1, LANE), 1) < HEAD_DIM
        causal = (lax.broadcasted_iota(jnp.int32, (t, t), 1) <= lax.broadcasted_iota(jnp.int32, (t, t), 0))
        lanes = [slice(g * LANE, (g + 1) * LANE) for g in group]

        dq_acc[...] = jnp.zeros_like(dq_acc)
        if has_kr:
            _init_acc(dkr_ref)
        if has_f:
            dcq_ref[...] = jnp.zeros_like(dcq_ref)

        def kv_block(j, _):
            ks = pl.ds(pl.multiple_of(j * t, t), t)
            other = kr_ref[ks, :] if has_kr else jnp.zeros((t, LANE), BF)
            kvbs = [kv_ref[ks, lanes[g]] for g in group]
            kks = [jnp.where(lo, kvbs[g], other) for g in group]
            cks = [cr_ref[g, :, ks] if has_f else None for g in group]

            def pair(i, carry, diag):
                qs = pl.ds(pl.multiple_of(i * t, t), t)
                out = []
                for g in group:
                    dkk, dvv, dcs = carry[g]
                    qb = q_ref[qs, lanes[g]]
                    dob = do_ref[qs, lanes[g]]
                    s = lax.dot_general(qb, kks[g], (((1,), (1,)), ((), ())), preferred_element_type=F32) * scale
                    if has_f:
                        s = s + (cc_ref[g, qs, :] - cks[g])
                    if diag:
                        s = jnp.where(causal, s, NEG_INF)
                    p = jnp.exp(s - lse_ref[g, qs, :])
                    dp = lax.dot_general(dob, kvbs[g], (((1,), (1,)), ((), ())), preferred_element_type=F32)
                    delta = jnp.sum(dob.astype(F32) * o_ref[qs, lanes[g]].astype(F32), axis=1, keepdims=True)
                    ds = p * (dp - delta)
                    dsb = ds.astype(BF)
                    dvv = dvv + lax.dot_general(p.astype(BF), dob, (((0,), (0,)), ((), ())), preferred_element_type=F32)
                    dkk = dkk + lax.dot_general(dsb, qb, (((0,), (0,)), ((), ())), preferred_element_type=F32)
                    dq_acc[qs, lanes[g]] += jnp.dot(dsb, kks[g], preferred_element_type=F32)
                    if has_f:
                        dcs = dcs + jnp.sum(ds, axis=0, keepdims=True)
                        dcq_ref[g, qs, :] += jnp.sum(ds, axis=1, keepdims=True)
                    out.append((dkk, dvv, dcs))
                return tuple(out)

            init = tuple((jnp.zeros((t, LANE), F32), jnp.zeros((t, LANE), F32), jnp.zeros((1, t), F32)) for _ in group)
            carry = pair(j, init, True)
            carry = lax.fori_loop(j + 1, nb, lambda i, c: pair(i, c, False), carry)
            for g, (dkk, dvv, dcs) in enumerate(carry):
                dkk = dkk * scale
                dkv_ref[ks, lanes[g]] = jnp.where(lo, dkk, dvv).astype(BF)
                if has_kr:
                    dkr_ref[ks, :] += jnp.where(lo, 0.0, dkk)
                if has_f:
                    dck_ref[g, :, ks] = -dcs
            return 0

        lax.fori_loop(0, nb, kv_block, 0)
        dq_ref[...] = (dq_acc[...] * scale).astype(BF)

    head, kv_head, shared, col_vec, row_vec = _attn_specs(rows, kv_off, group_size, has_f)
    in_specs, args = [head, kv_head], [q, kv]
    if has_kr:
        in_specs.append(shared)
        args.append(kr)
    if has_f:
        in_specs += [col_vec, row_vec]
        args += [cum_col, cum_row]
    in_specs += [head, head, col_vec]
    args += [o, do, lse]
    out_shape = [jax.ShapeDtypeStruct((rows, heads * LANE), BF), jax.ShapeDtypeStruct((rows, heads * LANE), BF)]
    out_specs = [head, head]
    if has_kr:
        out_shape.append(jax.ShapeDtypeStruct((rows, LANE), F32))
        out_specs.append(shared)
    if has_f:
        out_shape += [jax.ShapeDtypeStruct((heads, 1, rows), F32), jax.ShapeDtypeStruct((heads, rows, 1), F32)]
        out_specs += [row_vec, col_vec]
    return pl.pallas_call(
        body, out_shape=out_shape, grid=(heads // group_size,), in_specs=in_specs, out_specs=out_specs,
        scratch_shapes=[pltpu.VMEM((rows, group_size * LANE), F32)], name=name,
        compiler_params=_params(("arbitrary",)))(*args)


def _tri_dot(tri, x):
    return jnp.dot(tri, x, preferred_element_type=F32, precision=lax.Precision.HIGHEST)


def _forget_forward(f_raw, b_f):
    rows = f_raw.shape[0]
    t = ATTN_TILE

    def body(f_ref, b_ref, cum_ref):
        tri = (lax.broadcasted_iota(jnp.int32, (t, t), 1) <= lax.broadcasted_iota(jnp.int32, (t, t), 0)).astype(F32)

        def blk(i, carry):
            sl = pl.ds(pl.multiple_of(i * t, t), t)
            xv = f_ref[sl, :] + b_ref[...]
            log_f = jnp.minimum(xv, 0.0) - jnp.log(1.0 + jnp.exp(-jnp.abs(xv)))
            cum_ref[sl, :] = _tri_dot(tri, log_f) + carry
            return carry + jnp.sum(log_f, axis=0, keepdims=True)

        lax.fori_loop(0, rows // t, blk, jnp.zeros((1, LANE), F32))

    return pl.pallas_call(body, out_shape=jax.ShapeDtypeStruct((rows, LANE), F32), name="forget_forward",
                          compiler_params=_params())(f_raw, b_f)


def _forget_backward(f_raw, b_f, dcum):
    rows = f_raw.shape[0]
    t = ATTN_TILE
    nb = rows // t

    def body(f_ref, b_ref, dc_ref, df_ref, db_ref):
        tri = (lax.broadcasted_iota(jnp.int32, (t, t), 1) >= lax.broadcasted_iota(jnp.int32, (t, t), 0)).astype(F32)

        def blk(i, carry):
            later, db = carry
            sl = pl.ds(pl.multiple_of((nb - 1 - i) * t, t), t)
            dc = dc_ref[sl, :]
            dlog = _tri_dot(tri, dc) + later
            xv = f_ref[sl, :] + b_ref[...]
            df = dlog / (1.0 + jnp.exp(xv))
            df_ref[sl, :] = df.astype(BF)
            return later + jnp.sum(dc, axis=0, keepdims=True), db + jnp.sum(df, axis=0, keepdims=True)

        _, db = lax.fori_loop(0, nb, blk, (jnp.zeros((1, LANE), F32), jnp.zeros((1, LANE), F32)))
        db_ref[...] = db

    return pl.pallas_call(body, out_shape=[jax.ShapeDtypeStruct((rows, LANE), BF), jax.ShapeDtypeStruct((1, LANE), F32)],
                          name="forget_backward", compiler_params=_params())(f_raw, b_f, dcum)


def _t5_bucket(dist):
    max_exact = REL_BUCKETS // 2
    n = jnp.maximum(dist.astype(F32), 1.0)
    large = max_exact + (jnp.log(n / max_exact) / math.log(REL_MAX_DIST / max_exact)
                         * (REL_BUCKETS - max_exact)).astype(jnp.int32)
    large = jnp.minimum(large, REL_BUCKETS - 1)
    return jnp.where(dist < max_exact, dist, large)


def _dil_buckets(dilation):
    i = jnp.arange(Q_BLOCK)[:, None]
    j = jnp.arange(Q_BLOCK)[None, :]
    cur = _t5_bucket(jnp.clip(i - j, 0) * dilation).astype(jnp.int32)
    prev = _t5_bucket(jnp.clip(Q_BLOCK + i - j, 0) * dilation).astype(jnp.int32)
    return cur, prev


def _dil_bias_tiles(tbl_ref, bc_ref, bp_ref, bias_ref, group, hp):
    ii = lax.broadcasted_iota(jnp.int32, (Q_BLOCK, Q_BLOCK), 0)
    jj = lax.broadcasted_iota(jnp.int32, (Q_BLOCK, Q_BLOCK), 1)
    for hh in range(2 * DIL_PAIRS):
        col = group * HEADS + 2 * DIL_PAIRS * hp + hh
        acc_c = jnp.zeros((Q_BLOCK, Q_BLOCK), F32)
        acc_p = jnp.zeros((Q_BLOCK, Q_BLOCK), F32)
        for b in range(REL_BUCKETS):
            val = tbl_ref[b, col]
            acc_c = jnp.where(bc_ref[...] == b, val, acc_c)
            acc_p = jnp.where(bp_ref[...] == b, val, acc_p)
        bias_ref[2 * hh] = jnp.where(jj <= ii, acc_c, NEG_INF)
        bias_ref[2 * hh + 1] = jnp.where(jj >= ii, acc_p, NEG_INF)


def _dil_view(qkv, group, dilation):
    if dilation == 1:
        return qkv
    width = 3 * HEADS * HEAD_DIM
    return qkv[:, group * width:(group + 1) * width].reshape(qkv.shape[0] // dilation, dilation * width)


def _dil_specs(group, dilation, length):
    width = DIL_PAIRS * LANE
    per = 8 // DIL_PAIRS

    def col(kind):
        if dilation == 1:
            return pl.BlockSpec((length, width), lambda hp, r: (0, (group * 3 + kind) * per + hp))
        return pl.BlockSpec((length, width), lambda hp, r: (0, (r * 3 + kind) * per + hp))

    out = pl.BlockSpec((length, width), lambda hp, r: (0, r * per + hp))
    tile = pl.BlockSpec((Q_BLOCK, Q_BLOCK), lambda hp, r: (0, 0))
    table = pl.BlockSpec(memory_space=pltpu.SMEM)
    return col, out, tile, table


def _dil_forward(view, group, dilation, table, buckets):
    length = view.shape[0]
    rows = length * dilation
    nb = length // Q_BLOCK
    scale = HEAD_DIM ** -0.5
    qb = Q_BLOCK

    def body(tbl_ref, bc_ref, bp_ref, q_ref, k_ref, v_ref, o_ref, lse_ref, bias_ref):
        hp = pl.program_id(0)

        @pl.when(pl.program_id(1) == 0)
        def _():
            _dil_bias_tiles(tbl_ref, bc_ref, bp_ref, bias_ref, group, hp)

        lo = lax.broadcasted_iota(jnp.int32, (1, LANE), 1) < HEAD_DIM
        nt = (((1,), (1,)), ((), ()))

        def blk(n, first):
            cur = pl.ds(0, qb) if first else pl.ds(pl.multiple_of(n * qb, qb), qb)
            prev = None if first else pl.ds(pl.multiple_of((n - 1) * qb, qb), qb)
            for pair in range(DIL_PAIRS):
                lanes = slice(pair * LANE, (pair + 1) * LANE)
                qn = q_ref[cur, lanes] * scale
                kc, vc = k_ref[cur, lanes], v_ref[cur, lanes]
                if not first:
                    kp, vp = k_ref[prev, lanes], v_ref[prev, lanes]
                outs, lses = [], []
                for hh in range(2):
                    bias = 4 * pair + 2 * hh
                    qm = jnp.where(lo if hh == 0 else ~lo, qn, jnp.zeros_like(qn))
                    s_c = lax.dot_general(qm, kc, nt, preferred_element_type=F32) + bias_ref[bias]
                    m = jnp.max(s_c, axis=1, keepdims=True)
                    if not first:
                        s_p = lax.dot_general(qm, kp, nt, preferred_element_type=F32) + bias_ref[bias + 1]
                        m = jnp.maximum(m, jnp.max(s_p, axis=1, keepdims=True))
                    e_c = jnp.exp(s_c - m)
                    l = jnp.sum(e_c, axis=1, keepdims=True)
                    acc = jnp.dot(e_c.astype(BF), vc, preferred_element_type=F32)
                    if not first:
                        e_p = jnp.exp(s_p - m)
                        l = l + jnp.sum(e_p, axis=1, keepdims=True)
                        acc = acc + jnp.dot(e_p.astype(BF), vp, preferred_element_type=F32)
                    outs.append(acc * (1.0 / l))
                    lses.append(m + jnp.log(l))
                o_ref[cur, lanes] = jnp.where(lo, outs[0], outs[1])
                lse_ref[cur, lanes] = jnp.where(lo, lses[0], lses[1])
            return 0

        blk(0, True)
        if nb > 1:
            lax.fori_loop(1, nb, lambda n, _: blk(n, False), 0)

    col, out, tile, tbl = _dil_specs(group, dilation, length)
    bc, bp = buckets
    o, lse = pl.pallas_call(
        body, out_shape=[jax.ShapeDtypeStruct((length, dilation * D_MODEL), F32)] * 2,
        grid=(8 // DIL_PAIRS, dilation), in_specs=[tbl, tile, tile, col(0), col(1), col(2)], out_specs=[out, out],
        scratch_shapes=[pltpu.VMEM((4 * DIL_PAIRS, qb, qb), F32)], name=f"dilated_forward_{dilation}",
        compiler_params=_params(("arbitrary", "arbitrary")))(
            table, bc, bp, view, view, view)
    return o.reshape(rows, D_MODEL), lse.reshape(rows, D_MODEL)


def _dil_backward(view, group, dilation, table, buckets, do_g, lse, dlt):
    length = view.shape[0]
    rows = length * dilation
    nb = length // Q_BLOCK
    scale = HEAD_DIM ** -0.5
    qb = Q_BLOCK

    def body(tbl_ref, bc_ref, bp_ref, q_ref, k_ref, v_ref, do_ref, lse_ref, dlt_ref,
             dq_ref, dk_ref, dv_ref, db_ref, bias_ref, dk_acc, dv_acc):
        hp = pl.program_id(0)

        @pl.when(pl.program_id(1) == 0)
        def _():
            _dil_bias_tiles(tbl_ref, bc_ref, bp_ref, bias_ref, group, hp)
            db_ref[...] = jnp.zeros_like(db_ref)

        dk_acc[...] = jnp.zeros_like(dk_acc)
        dv_acc[...] = jnp.zeros_like(dv_acc)
        lo = lax.broadcasted_iota(jnp.int32, (1, LANE), 1) < HEAD_DIM
        tn = (((0,), (0,)), ((), ()))
        nt = (((1,), (1,)), ((), ()))

        def blk(n, first):
            cur = pl.ds(0, qb) if first else pl.ds(pl.multiple_of(n * qb, qb), qb)
            prev = None if first else pl.ds(pl.multiple_of((n - 1) * qb, qb), qb)
            for pair in range(DIL_PAIRS):
                lanes = slice(pair * LANE, (pair + 1) * LANE)
                qn = q_ref[cur, lanes] * scale
                don = do_ref[cur, lanes]
                kc, vc = k_ref[cur, lanes], v_ref[cur, lanes]
                if not first:
                    kp, vp = k_ref[prev, lanes], v_ref[prev, lanes]
                lse_n = lse_ref[cur, lanes]
                dlt_n = dlt_ref[cur, lanes]
                dqs = []
                dkc = jnp.zeros((qb, LANE), F32)
                dkp = jnp.zeros((qb, LANE), F32)
                dvc = jnp.zeros((qb, LANE), F32)
                dvp = jnp.zeros((qb, LANE), F32)
                for hh in range(2):
                    bias = 4 * pair + 2 * hh
                    mask = lo if hh == 0 else ~lo
                    qm = jnp.where(mask, qn, jnp.zeros_like(qn))
                    dom = jnp.where(mask, don, jnp.zeros_like(don))
                    lse_h = jnp.max(jnp.where(mask, lse_n, -3e38), axis=1, keepdims=True)
                    dlt_h = jnp.max(jnp.where(mask, dlt_n, -3e38), axis=1, keepdims=True)
                    p_c = jnp.exp(lax.dot_general(qm, kc, nt, preferred_element_type=F32) + bias_ref[bias] - lse_h)
                    ds_c = p_c * (lax.dot_general(dom, vc, nt, preferred_element_type=F32) - dlt_h)
                    db_ref[pair, 2 * hh] += ds_c
                    dsc_b = ds_c.astype(BF)
                    dq = jnp.dot(dsc_b, kc, preferred_element_type=F32)
                    dkc = dkc + lax.dot_general(dsc_b, qm, tn, preferred_element_type=F32)
                    dvc = dvc + lax.dot_general(p_c.astype(BF), dom, tn, preferred_element_type=F32)
                    if not first:
                        p_p = jnp.exp(lax.dot_general(qm, kp, nt, preferred_element_type=F32) + bias_ref[bias + 1] - lse_h)
                        ds_p = p_p * (lax.dot_general(dom, vp, nt, preferred_element_type=F32) - dlt_h)
                        db_ref[pair, 2 * hh + 1] += ds_p
                        dsp_b = ds_p.astype(BF)
                        dq = dq + jnp.dot(dsp_b, kp, preferred_element_type=F32)
                        dkp = dkp + lax.dot_general(dsp_b, qm, tn, preferred_element_type=F32)
                        dvp = dvp + lax.dot_general(p_p.astype(BF), dom, tn, preferred_element_type=F32)
                    dqs.append(dq)
                dq_ref[cur, lanes] = (jnp.where(lo, dqs[0], dqs[1]) * scale).astype(BF)
                dk_acc[cur, lanes] += dkc
                dv_acc[cur, lanes] += dvc
                if not first:
                    dk_acc[prev, lanes] += dkp
                    dv_acc[prev, lanes] += dvp
            return 0

        blk(0, True)
        if nb > 1:
            lax.fori_loop(1, nb, lambda n, _: blk(n, False), 0)
        dk_ref[...] = dk_acc[...].astype(BF)
        dv_ref[...] = dv_acc[...].astype(BF)

    col, out, tile, tbl = _dil_specs(group, dilation, length)
    bc, bp = buckets
    wide = (length, dilation * D_MODEL)
    dq, dk, dv, db = pl.pallas_call(
        body, out_shape=[jax.ShapeDtypeStruct(wide, BF)] * 3 + [jax.ShapeDtypeStruct((8, 4, qb, qb), F32)],
        grid=(8 // DIL_PAIRS, dilation), in_specs=[tbl, tile, tile, col(0), col(1), col(2), out, out, out],
        out_specs=[out, out, out, pl.BlockSpec((DIL_PAIRS, 4, qb, qb), lambda hp, r: (hp, 0, 0, 0))],
        scratch_shapes=[pltpu.VMEM((4 * DIL_PAIRS, qb, qb), F32), pltpu.VMEM((length, DIL_PAIRS * LANE), F32),
                        pltpu.VMEM((length, DIL_PAIRS * LANE), F32)],
        name=f"dilated_backward_{dilation}", compiler_params=_params(("arbitrary", "arbitrary")))(
            table, bc, bp, view, view, view,
            do_g.reshape(wide), lse.reshape(wide), dlt.reshape(wide))
    return dq.reshape(rows, D_MODEL), dk.reshape(rows, D_MODEL), dv.reshape(rows, D_MODEL), db


def _head_sums(x, lo):
    s0 = jnp.sum(jnp.where(lo, x, 0.0), axis=1, keepdims=True)
    s1 = jnp.sum(jnp.where(lo, 0.0, x), axis=1, keepdims=True)
    return jnp.where(lo, s0, s1)


def _dil_merge_forward(outs, lses):
    rows = outs[0].shape[0]

    def body(o0, o1, o2, l0, l1, l2, o_ref):
        ls = [l0[...], l1[...], l2[...]]
        m = jnp.maximum(jnp.maximum(ls[0], ls[1]), ls[2])
        es = [jnp.exp(v - m) for v in ls]
        tot = es[0] + es[1] + es[2]
        o_ref[...] = ((es[0] * o0[...] + es[1] * o1[...] + es[2] * o2[...]) / tot).astype(BF)

    blk = pl.BlockSpec((ROW_TILE, LANE), lambda i, j: (i, j))
    return pl.pallas_call(body, out_shape=jax.ShapeDtypeStruct((rows, D_MODEL), BF), grid=(rows // ROW_TILE, 8),
                          in_specs=[blk] * 6, out_specs=blk, name="dilated_merge_forward",
                          compiler_params=_params(("parallel", "parallel")))(*outs, *lses)


def _dil_merge_backward(outs, lses, do):
    rows = outs[0].shape[0]

    def body(o0, o1, o2, l0, l1, l2, do_ref, d0, d1, d2, t0, t1, t2):
        lo = lax.broadcasted_iota(jnp.int32, (1, LANE), 1) < HEAD_DIM
        ls = [l0[...], l1[...], l2[...]]
        os_ = [o0[...], o1[...], o2[...]]
        m = jnp.maximum(jnp.maximum(ls[0], ls[1]), ls[2])
        es = [jnp.exp(v - m) for v in ls]
        inv = 1.0 / (es[0] + es[1] + es[2])
        alphas = [e * inv for e in es]
        dov = do_ref[...]
        merged = alphas[0] * os_[0] + alphas[1] * os_[1] + alphas[2] * os_[2]
        dot = _head_sums(dov * merged, lo)
        for a, d_ref, t_ref in zip(alphas, (d0, d1, d2), (t0, t1, t2)):
            d_ref[...] = (a * dov).astype(BF)
            t_ref[...] = a * dot

    blk = pl.BlockSpec((ROW_TILE, LANE), lambda i, j: (i, j))
    res = pl.pallas_call(
        body, out_shape=[jax.ShapeDtypeStruct((rows, D_MODEL), BF)] * 3 + [jax.ShapeDtypeStruct((rows, D_MODEL), F32)] * 3,
        grid=(rows // ROW_TILE, 8), in_specs=[blk] * 7, out_specs=[blk] * 6, name="dilated_merge_backward",
        compiler_params=_params(("parallel", "parallel")))(*outs, *lses, do)
    return res[:3], res[3:]


def _rel_bias_grad(dbs, buckets):
    def body(db_ref, bc_ref, bp_ref, o_ref):
        g = pl.program_id(0)
        hp = pl.program_id(1)

        @pl.when((g == 0) & (hp == 0))
        def _():
            o_ref[...] = jnp.zeros_like(o_ref)

        rr = lax.broadcasted_iota(jnp.int32, (REL_BUCKETS, LANE), 0)
        cc = lax.broadcasted_iota(jnp.int32, (REL_BUCKETS, LANE), 1)
        bc = bc_ref[0]
        bp = bp_ref[0]
        acc = jnp.zeros((REL_BUCKETS, LANE), F32)
        for hh in range(2):
            col = g * HEADS + 2 * hp + hh
            d_c = db_ref[0, 0, 2 * hh]
            d_p = db_ref[0, 0, 2 * hh + 1]
            for b in range(REL_BUCKETS):
                val = (jnp.sum(jnp.where(bc == b, d_c, 0.0), keepdims=True)
                       + jnp.sum(jnp.where(bp == b, d_p, 0.0), keepdims=True))
                acc = jnp.where((rr == b) & (cc == col), val, acc)
        o_ref[...] += acc

    db_all = jnp.stack(dbs)
    bc_all = jnp.stack([b[0] for b in buckets])
    bp_all = jnp.stack([b[1] for b in buckets])
    tile = pl.BlockSpec((1, Q_BLOCK, Q_BLOCK), lambda g, hp: (g, 0, 0))
    return pl.pallas_call(
        body, out_shape=jax.ShapeDtypeStruct((REL_BUCKETS, LANE), F32), grid=(3, 8),
        in_specs=[pl.BlockSpec((1, 1, 4, Q_BLOCK, Q_BLOCK), lambda g, hp: (g, hp, 0, 0, 0)), tile, tile],
        out_specs=pl.BlockSpec((REL_BUCKETS, LANE), lambda g, hp: (0, 0)), name="rel_bias_grad",
        compiler_params=_params(("arbitrary", "arbitrary")))(db_all, bc_all, bp_all)


def _mla_forward(hn, w, tables):
    a = _matmul(hn, w["w_a"], name="mla_a")
    cq, ckv, kr = _mla_mid_forward(a, w["q_norm"], w["kv_norm"], tables)
    q_raw = _matmul(cq, w["w_uq"], name="mla_uq")
    q = _rope_heads(q_raw, tables, False, "rope_forward")
    kv = _matmul(ckv, w["w_ukv"], b_chunks=True, out_dtype=BF, name="mla_ukv")
    scale = (HEAD_DIM + MLA_ROPE) ** -0.5
    o, lse = _attn_forward(q, kv, 0, kr, None, None, scale, MLA_GROUP, "mla_attention_forward")
    y = _matmul(o, w["w_o"], name="attn_out")
    return y, dict(hn=hn, a=a, cq=cq, ckv=ckv, kr=kr, q=q, kv=kv, o=o, lse=lse)


def _mla_backward(dy, w, s, tables):
    scale = (HEAD_DIM + MLA_ROPE) ** -0.5
    g = {}
    g["w_o"] = _matmul(s["o"], dy, ta=True, out_dtype=BF, name="attn_out_dw")
    do = _matmul(dy, w["w_o"], tb=True, out_dtype=BF, name="attn_out_dx")
    dq, dkv, dkr = _attn_backward(s["q"], s["kv"], 0, s["kr"], None, None, s["o"], do, s["lse"], scale,
                                  MLA_GROUP, "mla_attention_backward")
    dq_raw = _rope_heads(dq, tables, True, "rope_backward")
    g["w_uq"] = _matmul(s["cq"], dq_raw, ta=True, out_dtype=BF, name="mla_uq_dw")
    dcq = _matmul(dq_raw, w["w_uq"], tb=True, name="mla_uq_dx")
    g["w_ukv"] = _matmul(s["ckv"], dkv, ta=True, out_chunks=True, out_dtype=BF, name="mla_ukv_dw")
    dckv = _matmul(dkv, w["w_ukv"], tb=True, b_chunks=True, name="mla_ukv_dx")
    da, g["q_norm"], g["kv_norm"] = _mla_mid_backward(s["a"], w["q_norm"], w["kv_norm"], tables, dcq, dckv, dkr)
    g["w_a"] = _matmul(s["hn"], da, ta=True, out_dtype=BF, name="mla_a_dw")
    dhn = _matmul(da, w["w_a"], tb=True, name="mla_a_dx")
    return dhn, g


def _fox_forward(hn, w):
    qkv = _matmul(hn, w["w_qkv"], out_dtype=BF, name="fox_qkv")
    f_raw = _matmul(hn, w["w_f"], name="fox_f")
    cum = _forget_forward(f_raw, w["b_f"])
    cum_heads = cum[:, :HEADS].T
    cum_col, cum_row = cum_heads[:, :, None], cum_heads[:, None, :]
    o, lse = _attn_forward(qkv, qkv, HEADS, None, cum_col, cum_row, HEAD_DIM ** -0.5, FOX_GROUP,
                           "fox_attention_forward")
    y = _matmul(o, w["w_o"], name="attn_out")
    return y, dict(hn=hn, qkv=qkv, f_raw=f_raw, cum_col=cum_col, cum_row=cum_row, o=o, lse=lse)


def _fox_backward(dy, w, s):
    g = {}
    g["w_o"] = _matmul(s["o"], dy, ta=True, out_dtype=BF, name="attn_out_dw")
    do = _matmul(dy, w["w_o"], tb=True, out_dtype=BF, name="attn_out_dx")
    dq, dkv, dck, dcq = _attn_backward(s["qkv"], s["qkv"], HEADS, None, s["cum_col"], s["cum_row"], s["o"], do,
                                       s["lse"], HEAD_DIM ** -0.5, FOX_GROUP, "fox_attention_backward")
    dcum = jnp.pad((dck[:, 0, :] + dcq[:, :, 0]).T, ((0, 0), (0, LANE - HEADS)))
    df, g["b_f"] = _forget_backward(s["f_raw"], w["b_f"], dcum)
    dqkv = jnp.concatenate([dq, dkv], axis=1)
    g["w_qkv"] = _matmul(s["hn"], dqkv, ta=True, out_dtype=BF, name="fox_qkv_dw")
    g["w_f"] = _matmul(s["hn"], df, ta=True, out_dtype=BF, name="fox_f_dw")
    dhn = _matmul(dqkv, w["w_qkv"], tb=True, name="fox_qkv_dx")
    dhn = _matmul(df, w["w_f"], tb=True, add=dhn, name="fox_f_dx")
    return dhn, g


def _dil_mixer_forward(hn, w, buckets):
    qkv = _matmul(hn, w["w_qkv"], b_chunks=True, out_dtype=BF, name="dil_qkv")
    views = [_dil_view(qkv, grp, dilation) for grp, (_, dilation) in enumerate(DIL_PATTERNS)]
    outs, lses = [], []
    for grp, (_, dilation) in enumerate(DIL_PATTERNS):
        o_g, lse_g = _dil_forward(views[grp], grp, dilation, w["rel_bias"], buckets[grp])
        outs.append(o_g)
        lses.append(lse_g)
    o = _dil_merge_forward(outs, lses)
    y = _matmul(o, w["w_o"], name="dil_out")
    return y, dict(hn=hn, views=views, outs=outs, lses=lses, o=o)


def _dil_mixer_backward(dy, w, s, buckets):
    g = {}
    g["w_o"] = _matmul(s["o"], dy, ta=True, out_dtype=BF, name="dil_out_dw")
    do = _matmul(dy, w["w_o"], tb=True, name="dil_out_dx")
    do_gs, dlts = _dil_merge_backward(s["outs"], s["lses"], do)
    parts, dbs = [], []
    for grp, (_, dilation) in enumerate(DIL_PATTERNS):
        dq, dk, dv, db = _dil_backward(s["views"][grp], grp, dilation, w["rel_bias"], buckets[grp], do_gs[grp],
                                       s["lses"][grp], dlts[grp])
        parts += [dq, dk, dv]
        dbs.append(db)
    dqkv = jnp.concatenate(parts, axis=1)
    g["rel_bias"] = _rel_bias_grad(dbs, buckets)
    g["w_qkv"] = _matmul(s["hn"], dqkv, ta=True, out_chunks=True, out_dtype=BF, name="dil_qkv_dw")
    dhn = _matmul(dqkv, w["w_qkv"], tb=True, b_chunks=True, name="dil_qkv_dx")
    return dhn, g


def _mixer_weights(i, lw, small):
    mixer, j = i % N_MIXERS, i // N_MIXERS
    if mixer == 0:
        return dict(lw["mixer"], q_norm=small["mla_q_norm"][j][None, :], kv_norm=small["mla_kv_norm"][j][None, :])
    if mixer == 1:
        return dict(lw["mixer"], rel_bias=small["rel_bias"])
    return dict(lw["mixer"], b_f=jnp.pad(small["fox_b_f"][j][None, :], ((0, 0), (0, LANE - HEADS))))


MIXER_PART, COMMON_PART = 0, 1


def _run_layers(x, p, positions, target, get_part, get_small, put_part):
    tables = _rope_tables(positions)
    buckets = [_dil_buckets(d) for _, d in DIL_PATTERNS]
    layers, saved = [], []
    h = x
    first = get_part(0, MIXER_PART, positions)
    small = get_small()

    def gain(i, k):
        return small["norm_g"][i, k][None, :]

    hn = _prenorm(h, gain(0, 0))
    sq = dh = None
    for i in range(DEPTH):
        mixer = i % N_MIXERS
        lw = dict(mixer=first if i == 0 else get_part(i, MIXER_PART, h))
        mw = _mixer_weights(i, lw, small)
        if mixer == 0:
            y, ms = _mla_forward(hn, mw, tables)
        elif mixer == 1:
            y, ms = _dil_mixer_forward(hn, mw, buckets)
        else:
            y, ms = _fox_forward(hn, mw)
        lw.update(get_part(i, COMMON_PART, y))
        layers.append(lw)
        h1, hn2 = _post_residual(h, y, gain(i, 1), gain(i, 2))
        gu = _matmul(hn2, lw["ffn_w_in"], b_chunks=True, out_dtype=BF, name="ffn_in")
        act = _swiglu_forward(gu)
        f = _matmul(act, lw["ffn_w_out"], name="ffn_out")
        h2, h2b = _post_residual(h1, f, gain(i, 3), None)
        pp = _matmul(p[i], lw["ple_w_proj"], b_chunks=True, name="ple_proj")
        z = _matmul(h2b, lw["ple_w_gate"], name="ple_gate")
        saved.append(dict(h=h, y=y, ms=ms, h1=h1, hn2=hn2, gu=gu, act=act, f=f, h2b=h2b, pp=pp, z=z))
        if i + 1 < DEPTH:
            h, hn = _ple_forward(h2, pp, z, gain(i + 1, 0))
        else:
            dh, sq = _ple_loss(h2, pp, z, target)

    norm_rows = [[None] * 4 for _ in range(DEPTH)]
    sg = dict(mla_q_norm={}, mla_kv_norm={}, rel_bias=None, fox_b_f={})
    for i in reversed(range(DEPTH)):
        s, lw = saved[i], layers[i]
        mixer, j = i % N_MIXERS, i // N_MIXERS
        mw = _mixer_weights(i, lw, small)
        lg = {}
        dpp, dz = _ple_backward(dh, s["pp"], s["z"])
        lg["ple_w_proj"] = _matmul(p[i], dpp, ta=True, out_chunks=True, out_dtype=BF, name="ple_proj_dw")
        lg["ple_w_gate"] = _matmul(s["h2b"], dz, ta=True, out_dtype=BF, name="ple_gate_dw")
        dh2 = _matmul(dz, lw["ple_w_gate"], tb=True, add=dh, name="ple_gate_dx")
        df, norm_rows[i][3] = _rms_backward(s["f"], gain(i, 3), dh2, None, BF)
        lg["ffn_w_out"] = _matmul(s["act"], df, ta=True, out_dtype=BF, name="ffn_out_dw")
        dact = _matmul(df, lw["ffn_w_out"], tb=True, out_dtype=BF, name="ffn_out_dx")
        dgu = _swiglu_backward(s["gu"], dact)
        lg["ffn_w_in"] = _matmul(s["hn2"], dgu, ta=True, out_chunks=True, out_dtype=BF, name="ffn_in_dw")
        token = put_part(i, COMMON_PART, lg)
        dhn2 = _matmul(dgu, lw["ffn_w_in"], tb=True, b_chunks=True, name="ffn_in_dx")
        dh1, norm_rows[i][2] = _rms_backward(s["h1"], gain(i, 2), dhn2, dh2, F32)
        dy, norm_rows[i][1] = _rms_backward(s["y"], gain(i, 1) + token[0:1, 0:1], dh1, None, BF)
        if mixer == 0:
            dhn, mg = _mla_backward(dy, mw, s["ms"], tables)
            sg["mla_q_norm"][j] = mg.pop("q_norm")
            sg["mla_kv_norm"][j] = mg.pop("kv_norm")
        elif mixer == 1:
            dhn, mg = _dil_mixer_backward(dy, mw, s["ms"], buckets)
            rel = mg.pop("rel_bias")[:, :3 * HEADS]
            sg["rel_bias"] = rel if sg["rel_bias"] is None else sg["rel_bias"] + rel
        else:
            dhn, mg = _fox_backward(dy, mw, s["ms"])
            sg["fox_b_f"][j] = mg.pop("b_f")[:, :HEADS]
        token = put_part(i, MIXER_PART, mg)
        dh, norm_rows[i][0] = _rms_backward(s["h"], gain(i, 0) + token[0:1, 0:1], dhn, dh1, F32)
    small_grads = dict(norm_g=jnp.stack([jnp.concatenate(row, axis=0) for row in norm_rows]),
                       rel_bias=sg["rel_bias"])
    for k in ("mla_q_norm", "mla_kv_norm", "fox_b_f"):
        small_grads[k] = jnp.concatenate([sg[k][j] for j in sorted(sg[k])], axis=0)
    return sq, dh, small_grads


BIG = ("ffn_w_in", "ffn_w_out", "ple_w_proj", "ple_w_gate", "mla_w_a", "mla_w_uq", "mla_w_ukv", "mla_w_o",
       "dil_w_qkv", "dil_w_o", "fox_w_qkvf", "fox_w_o")
SMALL_SHARDED = ("norm_g", "mla_q_norm", "mla_kv_norm")
SMALL_REPLICATED = ("rel_bias", "fox_b_f")
WEIGHTS = ("norm_g", "ffn_w_in", "ffn_w_out", "ple_w_proj", "ple_w_gate", "rel_bias", "mla_w_a", "mla_q_norm",
           "mla_kv_norm", "mla_w_uq", "mla_w_ukv", "mla_w_o", "dil_w_qkv", "dil_w_o", "fox_w_qkvf", "fox_b_f", "fox_w_o")


TRANSPOSED = "fox_w_qkvf"
LAYER_COMMON = ("ffn_w_in", "ffn_w_out", "ple_w_proj", "ple_w_gate")
MIXER_WEIGHTS = (("mla_w_a", "mla_w_uq", "mla_w_ukv", "mla_w_o"), ("dil_w_qkv", "dil_w_o"), ("fox_w_qkvf", "fox_w_o"))


def _part_names(i, part):
    return MIXER_WEIGHTS[i % N_MIXERS] if part == MIXER_PART else LAYER_COMMON


def _layer_slot(name, i):
    return i if name in LAYER_COMMON else i // N_MIXERS


def _merge_rows(chunks):
    n, r, c = chunks.shape
    return chunks.reshape(n * r, c)


def _merge_cols(chunks):
    n, r, c = chunks.shape
    return chunks.transpose(1, 0, 2).reshape(r, n * c)


def _pad_heads_out(wo):
    w3 = wo.reshape(HEADS, HEAD_DIM, D_MODEL)
    return jnp.pad(w3, ((0, 0), (HEAD_DIM, 0), (0, 0))).reshape(HEADS * LANE, D_MODEL)


def _part_to_compute(i, part, ch):
    if part == COMMON_PART:
        return dict(ffn_w_in=ch["ffn_w_in"], ffn_w_out=_merge_rows(ch["ffn_w_out"]), ple_w_proj=ch["ple_w_proj"],
                    ple_w_gate=_merge_rows(ch["ple_w_gate"]))
    lw = {}
    mixer = i % N_MIXERS
    if mixer == 0:
        wa = _merge_rows(ch["mla_w_a"])
        rank = MLA_Q_RANK + MLA_KV_RANK
        wa_p = jnp.concatenate([wa[:, :rank], jnp.zeros((wa.shape[0], 64), wa.dtype), wa[:, rank:],
                                jnp.zeros((wa.shape[0], 32), wa.dtype)], axis=1)
        wuq = _merge_cols(ch["mla_w_uq"]).reshape(MLA_Q_RANK, HEADS, HEAD_DIM + MLA_ROPE)
        wuq_p = jnp.pad(wuq, ((0, 0), (0, 0), (0, LANE - HEAD_DIM - MLA_ROPE))).reshape(MLA_Q_RANK, HEADS * LANE)
        lw["mixer"] = dict(w_a=wa_p, w_uq=wuq_p, w_ukv=ch["mla_w_ukv"], w_o=_pad_heads_out(_merge_rows(ch["mla_w_o"])))
    elif mixer == 1:
        lw["mixer"] = dict(w_qkv=ch["dil_w_qkv"], w_o=_merge_rows(ch["dil_w_o"]))
    else:
        wf = _merge_rows(ch["fox_w_qkvf"]).T
        inner = HEADS * HEAD_DIM
        q3 = wf[:, :inner].reshape(D_MODEL, HEADS, HEAD_DIM)
        k3 = wf[:, inner:2 * inner].reshape(D_MODEL, HEADS, HEAD_DIM)
        v3 = wf[:, 2 * inner:3 * inner].reshape(D_MODEL, HEADS, HEAD_DIM)
        q_p = jnp.pad(q3, ((0, 0), (0, 0), (0, HEAD_DIM))).reshape(D_MODEL, HEADS * LANE)
        kv_p = jnp.concatenate([k3, v3], axis=2).reshape(D_MODEL, HEADS * LANE)
        f_p = jnp.pad(wf[:, 3 * inner:], ((0, 0), (0, LANE - HEADS)))
        lw["mixer"] = dict(w_qkv=jnp.concatenate([q_p, kv_p], axis=1), w_f=f_p,
                           w_o=_pad_heads_out(_merge_rows(ch["fox_w_o"])))
    return lw["mixer"]


def _part_contributions(i, part, lg, chunk_shapes):
    spec = {k: jax.ShapeDtypeStruct(s, BF) for k, s in chunk_shapes.items()}
    (contrib,) = jax.linear_transpose(functools.partial(_part_to_compute, i, part), spec)(lg)
    return contrib


def _chip_peers():
    x, y, c = lax.axis_index("x"), lax.axis_index("y"), lax.axis_index("c")
    peers = [(1 - x, y), (x, 1 - y), (1 - x, 1 - y)]
    return x, y, c, peers


SEM_SPEC = pl.BlockSpec(memory_space=pltpu.SEMAPHORE)
ANY_SPEC = pl.BlockSpec(memory_space=pl.ANY)
SPLIT_EFFECT = pltpu.SideEffectType.DATAFLOW_SIDE_EFFECTING


def _own_slot(shard):
    me = 2 * lax.axis_index("x") + lax.axis_index("y")
    return lax.dynamic_update_index_in_dim(lax.empty((N_CHIPS,) + shard.shape, shard.dtype), shard[None], me, 0)


def _spread_copy(src, land, k, peer, c, send_sems, recv_sems, index, src_slot, slot):
    px, py = peer
    return pltpu.make_async_remote_copy(
        src_ref=src.at[src_slot], dst_ref=land.at[slot],
        send_sem=send_sems.at[3 * index + k], recv_sem=recv_sems.at[3 * index + k],
        device_id=(px, py, c), device_id_type=MESH)


def _spread_start(bufs, srcs, after, name):
    n = len(bufs)
    exchange = srcs is not None
    arrays = (list(srcs) if exchange else []) + list(bufs)
    na = len(arrays)

    def body(*refs):
        src, land = refs[:n], refs[na - n:na]
        send_sems, recv_sems = refs[na + 1], refs[na + 2]
        token = refs[-1]
        x, y, c, peers = _chip_peers()
        me = 2 * x + y
        for w in range(n):
            for k, peer in enumerate(peers):
                src_slot = 2 * peer[0] + peer[1] if exchange else me
                _spread_copy(src[w], land[w], k, peer, c, send_sems, recv_sems, w, src_slot, me).start()
        token[...] = jnp.zeros_like(token)

    hbm = [pltpu.with_memory_space_constraint(a, pltpu.HBM) for a in arrays]
    out = pl.pallas_call(
        body, name=name,
        out_shape=(pltpu.SemaphoreType.DMA((3 * n,)), pltpu.SemaphoreType.DMA((3 * n,)),
                   *[pltpu.HBM(a.shape, a.dtype) for a in hbm], jax.ShapeDtypeStruct((8, LANE), F32)),
        in_specs=[HBM_SPEC] * na + [ANY_SPEC],
        out_specs=(SEM_SPEC, SEM_SPEC, *[HBM_SPEC] * na, pl.BlockSpec(memory_space=pltpu.VMEM)),
        input_output_aliases={w: 2 + w for w in range(na)},
        compiler_params=pltpu.CompilerParams(has_side_effects=SPLIT_EFFECT))(*hbm, after)
    return dict(send=out[0], recv=out[1], arrays=out[2:2 + na], n=n, token=out[-1], exchange=exchange)


def _spread_wait(handle, after, name):
    n, exchange = handle["n"], handle["exchange"]
    arrays = list(handle["arrays"])
    na = len(arrays)

    def body(*refs):
        src, land = refs[:n], refs[na - n:na]
        send_sems, recv_sems = refs[na], refs[na + 1]
        x, y, c, peers = _chip_peers()
        me = 2 * x + y
        for w in range(n):
            for k, peer in enumerate(peers):
                there = 2 * peer[0] + peer[1]
                cp = _spread_copy(src[w], land[w], k, peer, c, send_sems, recv_sems, w, there if exchange else me, there)
                cp.wait_send()
                cp.wait_recv()

    out = pl.pallas_call(
        body, name=name, out_shape=tuple(pltpu.HBM(a.shape, a.dtype) for a in arrays),
        in_specs=[HBM_SPEC] * na + [SEM_SPEC, SEM_SPEC, ANY_SPEC], out_specs=tuple([HBM_SPEC] * na),
        input_output_aliases={w: w for w in range(na)},
        compiler_params=pltpu.CompilerParams(has_side_effects=SPLIT_EFFECT))(*arrays, handle["send"], handle["recv"], after)
    return (list(out[n:]), list(out[:n])) if exchange else list(out)


def _sibling_copy(received, sent, land, k, me, peers, sibling, send_sems, recv_sems, index):
    slot = me if k == 3 else 2 * peers[k][0] + peers[k][1]
    src = sent if k == 3 else received
    return pltpu.make_async_remote_copy(
        src_ref=src.at[slot], dst_ref=land.at[slot], send_sem=send_sems.at[4 * index + k],
        recv_sem=recv_sems.at[4 * index + k], device_id=sibling, device_id_type=MESH)


def _sibling_start(received, sent, after, name):
    n = len(received)
    lands = [lax.empty(a.shape, a.dtype) for a in received]
    arrays = list(received) + list(sent) + lands

    def body(*refs):
        rec, snt, land = refs[:n], refs[n:2 * n], refs[2 * n:3 * n]
        send_sems, recv_sems = refs[3 * n + 1], refs[3 * n + 2]
        token = refs[-1]
        x, y, c, peers = _chip_peers()
        for w in range(n):
            for k in range(4):
                _sibling_copy(rec[w], snt[w], land[w], k, 2 * x + y, peers, (x, y, 1 - c), send_sems, recv_sems, w).start()
        token[...] = jnp.zeros_like(token)

    hbm = [pltpu.with_memory_space_constraint(a, pltpu.HBM) for a in arrays]
    out = pl.pallas_call(
        body, name=name,
        out_shape=(pltpu.SemaphoreType.DMA((4 * n,)), pltpu.SemaphoreType.DMA((4 * n,)),
                   *[pltpu.HBM(a.shape, a.dtype) for a in hbm], jax.ShapeDtypeStruct((8, LANE), F32)),
        in_specs=[HBM_SPEC] * (3 * n) + [ANY_SPEC],
        out_specs=(SEM_SPEC, SEM_SPEC, *[HBM_SPEC] * (3 * n), pl.BlockSpec(memory_space=pltpu.VMEM)),
        input_output_aliases={w: 2 + w for w in range(3 * n)},
        compiler_params=pltpu.CompilerParams(has_side_effects=SPLIT_EFFECT))(*hbm, after)
    return dict(send=out[0], recv=out[1], arrays=out[2:2 + 3 * n], n=n, token=out[-1])


def _sibling_wait(handle, after, name):
    n = handle["n"]
    arrays = list(handle["arrays"])

    def body(*refs):
        rec, snt, land = refs[:n], refs[n:2 * n], refs[2 * n:3 * n]
        send_sems, recv_sems = refs[3 * n], refs[3 * n + 1]
        x, y, c, peers = _chip_peers()
        for w in range(n):
            for k in range(4):
                cp = _sibling_copy(rec[w], snt[w], land[w], k, 2 * x + y, peers, (x, y, 1 - c), send_sems, recv_sems, w)
                cp.wait_send()
                cp.wait_recv()

    out = pl.pallas_call(
        body, name=name, out_shape=tuple(pltpu.HBM(a.shape, a.dtype) for a in arrays),
        in_specs=[HBM_SPEC] * (3 * n) + [SEM_SPEC, SEM_SPEC, ANY_SPEC], out_specs=tuple([HBM_SPEC] * (3 * n)),
        input_output_aliases={w: w for w in range(3 * n)},
        compiler_params=pltpu.CompilerParams(has_side_effects=SPLIT_EFFECT))(*arrays, handle["send"], handle["recv"], after)
    return list(out[:n]), list(out[n:2 * n]), list(out[2 * n:])


def _all_reduce_small(v):
    rows = v.shape[0]

    def body(v_ref, sum_ref, slots, send_sems, recv_sems):
        x, y, c = lax.axis_index("x"), lax.axis_index("y"), lax.axis_index("c")
        me = 4 * x + 2 * y + c
        slots[me] = v_ref[...]
        sends = []
        for k in range(1, N_DEV):
            bx, by, bc = (k >> 2) & 1, (k >> 1) & 1, k & 1
            peer = (x ^ bx, y ^ by, c ^ bc)
            rc = pltpu.make_async_remote_copy(src_ref=v_ref, dst_ref=slots.at[me], send_sem=send_sems.at[k],
                                              recv_sem=recv_sems.at[k], device_id=peer, device_id_type=MESH)
            rc.start()
            sends.append(rc)
        for k in range(1, N_DEV):
            bx, by, bc = (k >> 2) & 1, (k >> 1) & 1, k & 1
            src = 4 * (x ^ bx) + 2 * (y ^ by) + (c ^ bc)
            pltpu.make_async_remote_copy(src_ref=v_ref, dst_ref=slots.at[src], send_sem=send_sems.at[k],
                                         recv_sem=recv_sems.at[k], device_id=(x ^ bx, y ^ by, c ^ bc),
                                         device_id_type=MESH).wait_recv()
        for rc in sends:
            rc.wait_send()
        total = slots[0]
        for k in range(1, N_DEV):
            total = total + slots[k]
        sum_ref[...] = total

    vm = pl.BlockSpec(memory_space=pltpu.VMEM)
    return pl.pallas_call(
        body, out_shape=jax.ShapeDtypeStruct((rows, LANE), F32), in_specs=[vm], out_specs=vm,
        scratch_shapes=[pltpu.VMEM((N_DEV, rows, LANE), F32), pltpu.SemaphoreType.DMA((N_DEV,)),
                        pltpu.SemaphoreType.DMA((N_DEV,))], name="all_reduce_small")(v)


def _as_2d(a):
    return a.reshape(-1, a.shape[-1])


def _row_tile(rows, cols):
    for t in (512, 256, 128, 64, 32, 16):
        if rows % t == 0 and t * cols * 4 <= (1 << 20):
            return t
    return rows


def _adamw_layer(w, m, v, received, sent, sibling, outs, slot):
    _, rows, cols = received.shape
    tr = _row_tile(rows, cols)
    by_columns = rows % tr != 0 or tr == rows and rows * cols * 4 > (2 << 20)
    if by_columns:
        assert w.shape[0] == rows and cols % (2 * LANE) == 0, (w.shape, received.shape)
        tr, tc, steps = rows, cols // 2, 2
        index = lambda i: (0, i)
    else:
        tc, steps, first = cols, rows // tr, slot * (rows // tr)
        index = lambda i: (i, 0)
    where = (2 * lax.axis_index("x") + lax.axis_index("y")).astype(jnp.int32).reshape(1)

    def body(where_ref, w_ref, m_ref, v_ref, r_ref, own_ref, s_ref, *rest):
        g_ref, d_ref, nm_ref, nv_ref = rest[4:]
        me = where_ref[0]
        mine = theirs = None
        for k in range(N_CHIPS):
            a = jnp.where(me == k, own_ref[...], r_ref[k]).astype(F32)
            b = s_ref[k].astype(F32)
            mine = a if mine is None else mine + a
            theirs = b if theirs is None else theirs + b
        g = mine + theirs
        delta, nm, nv = _adamw_math(w_ref[...], g, m_ref[...], v_ref[...])
        g_ref[...] = g
        d_ref[...] = delta
        nm_ref[...] = nm
        nv_ref[...] = nv

    if by_columns:
        stacked = pl.BlockSpec((tr, tc), lambda i, where_ref: index(i))
    else:
        stacked = pl.BlockSpec((tr, tc), lambda i, where_ref: (first + i, 0))
    four = pl.BlockSpec((N_CHIPS, tr, tc), lambda i, where_ref: (0,) + index(i))
    own = pl.BlockSpec((None, tr, tc), lambda i, where_ref: (where_ref[0],) + index(i))
    grid_spec = pltpu.PrefetchScalarGridSpec(
        num_scalar_prefetch=1, grid=(steps,),
        in_specs=[stacked, stacked, stacked, four, own, four] + [ANY_SPEC] * 4, out_specs=[stacked] * 4)
    return pl.pallas_call(body, out_shape=[jax.ShapeDtypeStruct(w.shape, F32)] * 4, grid_spec=grid_spec,
                          input_output_aliases={7 + k: k for k in range(4)}, name="adamw_layer",
                          compiler_params=_params(("parallel",)))(where, w, m, v, received, sent, sibling, *outs)


def _adamw_math(w, g, m, v):
    m = ADAM_B1 * m + (1.0 - ADAM_B1) * g
    v = ADAM_B2 * v + (1.0 - ADAM_B2) * (g * g)
    m_hat = m * (1.0 / (1.0 - ADAM_B1 ** ADAM_STEP))
    v_hat = v * (1.0 / (1.0 - ADAM_B2 ** ADAM_STEP))
    denom = jnp.sqrt(v_hat) + ADAM_EPS
    inv = pl.reciprocal(denom, approx=True)
    inv = inv * (2.0 - denom * inv)
    delta = -ADAM_LR * (m_hat * inv + ADAM_WD * w)
    return delta, m, v


def _adamw(w, m, v, g_mine, g_sibling):
    rows, cols = w.shape
    tr = _row_tile(rows, cols)
    two = g_sibling is not None

    def body(*refs):
        if two:
            w_ref, m_ref, v_ref, ga_ref, gb_ref, g_ref, d_ref, nm_ref, nv_ref = refs
            g = ga_ref[...] + gb_ref[...]
        else:
            w_ref, m_ref, v_ref, ga_ref, g_ref, d_ref, nm_ref, nv_ref = refs
            g = ga_ref[...]
        delta, nm, nv = _adamw_math(w_ref[...], g, m_ref[...], v_ref[...])
        g_ref[...] = g
        d_ref[...] = delta
        nm_ref[...] = nm
        nv_ref[...] = nv

    blk = pl.BlockSpec((tr, cols), lambda i: (i, 0))
    args = [w, m, v, g_mine] + ([g_sibling] if two else [])
    return pl.pallas_call(body, out_shape=[jax.ShapeDtypeStruct((rows, cols), F32)] * 4, grid=(rows // tr,),
                          in_specs=[blk] * len(args), out_specs=[blk] * 4, name="adamw",
                          compiler_params=_params(("parallel",)))(*args)


def _pack_rows(arrays):
    flat = jnp.concatenate([a.reshape(-1) for a in arrays])
    rows = -(-flat.shape[0] // (8 * LANE)) * 8
    return jnp.pad(flat, (0, rows * LANE - flat.shape[0])).reshape(rows, LANE)


def _unpack_rows(packed, shapes):
    flat = packed.reshape(-1)
    out, at = [], 0
    for s in shapes:
        size = math.prod(s)
        out.append(flat[at:at + size].reshape(s))
        at += size
    return out


def kernel(x, p, positions, norm_g, ffn_w_in, ffn_w_out, ple_w_proj, ple_w_gate, rel_bias, mla_w_a, mla_q_norm, mla_kv_norm, mla_w_uq, mla_w_ukv, mla_w_o, dil_w_qkv, dil_w_o, fox_w_qkvf, fox_b_f, fox_w_o, loss_target, m_norm_g, m_ffn_w_in, m_ffn_w_out, m_ple_w_proj, m_ple_w_gate, m_rel_bias, m_mla_w_a, m_mla_q_norm, m_mla_kv_norm, m_mla_w_uq, m_mla_w_ukv, m_mla_w_o, m_dil_w_qkv, m_dil_w_o, m_fox_w_qkvf, m_fox_b_f, m_fox_w_o, v_norm_g, v_ffn_w_in, v_ffn_w_out, v_ple_w_proj, v_ple_w_gate, v_rel_bias, v_mla_w_a, v_mla_q_norm, v_mla_kv_norm, v_mla_w_uq, v_mla_w_ukv, v_mla_w_o, v_dil_w_qkv, v_dil_w_o, v_fox_w_qkvf, v_fox_b_f, v_fox_w_o):
    w = dict(norm_g=norm_g, ffn_w_in=ffn_w_in, ffn_w_out=ffn_w_out, ple_w_proj=ple_w_proj, ple_w_gate=ple_w_gate,
             rel_bias=rel_bias, mla_w_a=mla_w_a, mla_q_norm=mla_q_norm, mla_kv_norm=mla_kv_norm, mla_w_uq=mla_w_uq,
             mla_w_ukv=mla_w_ukv, mla_w_o=mla_w_o, dil_w_qkv=dil_w_qkv, dil_w_o=dil_w_o, fox_w_qkvf=fox_w_qkvf,
             fox_b_f=fox_b_f, fox_w_o=fox_w_o)
    m = dict(norm_g=m_norm_g, ffn_w_in=m_ffn_w_in, ffn_w_out=m_ffn_w_out, ple_w_proj=m_ple_w_proj,
             ple_w_gate=m_ple_w_gate, rel_bias=m_rel_bias, mla_w_a=m_mla_w_a, mla_q_norm=m_mla_q_norm,
             mla_kv_norm=m_mla_kv_norm, mla_w_uq=m_mla_w_uq, mla_w_ukv=m_mla_w_ukv, mla_w_o=m_mla_w_o,
             dil_w_qkv=m_dil_w_qkv, dil_w_o=m_dil_w_o, fox_w_qkvf=m_fox_w_qkvf, fox_b_f=m_fox_b_f, fox_w_o=m_fox_w_o)
    v = dict(norm_g=v_norm_g, ffn_w_in=v_ffn_w_in, ffn_w_out=v_ffn_w_out, ple_w_proj=v_ple_w_proj,
             ple_w_gate=v_ple_w_gate, rel_bias=v_rel_bias, mla_w_a=v_mla_w_a, mla_q_norm=v_mla_q_norm,
             mla_kv_norm=v_mla_kv_norm, mla_w_uq=v_mla_w_uq, mla_w_ukv=v_mla_w_ukv, mla_w_o=v_mla_w_o,
             dil_w_qkv=v_dil_w_qkv, dil_w_o=v_dil_w_o, fox_w_qkvf=v_fox_w_qkvf, fox_b_f=v_fox_b_f, fox_w_o=v_fox_w_o)
    chip = 2 * lax.axis_index("x") + lax.axis_index("y")
    for tree in (w, m, v):
        tree[TRANSPOSED] = jnp.swapaxes(tree[TRANSPOSED], 1, 2)

    small_shapes = [w[k].shape for k in SMALL_SHARDED]
    order = [(i, part) for i in range(DEPTH) for part in (MIXER_PART, COMMON_PART)]
    gathers = {}
    after = positions
    zero = 0.0
    for i, part in order:
        bufs = [_own_slot((w[k][_layer_slot(k, i)] + zero).astype(BF)) for k in _part_names(i, part)]
        if (i, part) == order[0]:
            bufs.append(_own_slot(_pack_rows([w[k] for k in SMALL_SHARDED])))
        gathers[i, part] = _spread_start(bufs, None, after, f"gather_start_{i}_{part}")
        after = gathers[i, part]["token"]
        if (i, part) == order[0]:
            zero = after[0, 0]
    all_started = after
    state = {}

    def get_part(i, part, after_array):
        is_first = (i, part) == order[0]
        lands = _spread_wait(gathers[i, part], all_started if is_first else after_array, f"gather_wait_{i}_{part}")
        if is_first:
            pieces = [_unpack_rows(lands[-1][k], small_shapes) for k in range(N_CHIPS)]
            small = {name: jnp.concatenate([pieces[k][idx] for k in range(N_CHIPS)], axis=-1)
                     for idx, name in enumerate(SMALL_SHARDED)}
            state["small"] = dict(small, rel_bias=rel_bias, fox_b_f=fox_b_f)
        chunks = dict(zip(_part_names(i, part), lands))
        state[i, part] = {k: a.shape for k, a in chunks.items()}
        return _part_to_compute(i, part, chunks)

    started, forwards = [], {}

    def forward_oldest(after_array):
        i, part, handle = started.pop(0)
        received, sent = _spread_wait(handle, after_array, f"exchange_wait_{i}_{part}")
        forwards[i, part] = _sibling_start(received, sent, after_array, f"sibling_start_{i}_{part}")
        return forwards[i, part]["token"]

    def put_part(i, part, lg):
        contrib = _part_contributions(i, part, lg, state[i, part])
        srcs = [contrib[k] for k in _part_names(i, part)]
        handle = _spread_start([lax.empty(s.shape, s.dtype) for s in srcs], srcs, positions,
                               f"exchange_start_{i}_{part}")
        token = handle["token"]
        if started:
            token = token + forward_oldest(token)
        started.append((i, part, handle))
        return token

    sq, grad_x, sg = _run_layers(x[0], p[:, 0], positions[0], loss_target[0], get_part, lambda: state["small"],
                                 put_part)
    loss = lax.psum(0.5 / D_MODEL * jnp.sum(sq), ("x", "y", "c"))
    forward_oldest(grad_x)

    outs = {k: [lax.empty(_as_2d(w[k]).shape, F32) for _ in range(4)] for k in BIG}
    for i, part in [(i, part) for i in reversed(range(DEPTH)) for part in (COMMON_PART, MIXER_PART)]:
        received, sent, sibling = _sibling_wait(forwards[i, part], grad_x, f"sibling_wait_{i}_{part}")
        for k, r, s, t in zip(_part_names(i, part), received, sent, sibling):
            outs[k] = _adamw_layer(_as_2d(w[k]), _as_2d(m[k]), _as_2d(v[k]), r, s, t, outs[k], _layer_slot(k, i))
    results = {k: [o.reshape(w[k].shape) for o in outs[k]] for k in BIG}
    results[TRANSPOSED] = [jnp.swapaxes(o, 1, 2) for o in results[TRANSPOSED]]

    small_all = SMALL_SHARDED + SMALL_REPLICATED
    full_shapes = [sg[k].shape for k in small_all]
    reduced = dict(zip(small_all, _unpack_rows(_all_reduce_small(_pack_rows([sg[k] for k in small_all])), full_shapes)))
    local_g = []
    for k in small_all:
        g = reduced[k]
        if k in SMALL_SHARDED:
            width = w[k].shape[-1]
            g = lax.dynamic_slice_in_dim(g, chip * width, width, axis=g.ndim - 1)
        local_g.append(g)
    local_shapes = [w[k].shape for k in small_all]
    outs = _adamw(_pack_rows([w[k] for k in small_all]), _pack_rows([m[k] for k in small_all]),
                  _pack_rows([v[k] for k in small_all]), _pack_rows(local_g), None)
    unpacked = [_unpack_rows(o, local_shapes) for o in outs]
    for idx, k in enumerate(small_all):
        results[k] = [u[idx] for u in unpacked]

    return (loss, grad_x[None], *[results[k][0] for k in WEIGHTS], *[results[k][1] for k in WEIGHTS],
            *[results[k][2] for k in WEIGHTS], *[results[k][3] for k in WEIGHTS])
```

```python
import functools
import math

import jax
import jax.numpy as jnp
from jax import lax
from jax.experimental import pallas as pl
from jax.experimental.pallas import tpu as pltpu

F32 = jnp.float32
BF = jnp.bfloat16
MESH = pl.DeviceIdType.MESH
HBM_SPEC = pl.BlockSpec(memory_space=pltpu.HBM)

D_MODEL = 1024
DEPTH = 4
N_MIXERS = 3
D_FF = 2816
NORM_EPS = 1e-6
NEG_INF = -1e30
LANE = 128
HEADS = 16
HEAD_DIM = 64
MLA_Q_RANK = 384
MLA_KV_RANK = 256
MLA_ROPE = 32
MLA_A_PAD = 768
ROPE_THETA = 10000.0
DIL_PATTERNS = ((128, 1), (512, 4), (2048, 16))
Q_BLOCK = 128
DIL_PAIRS = 2
REL_BUCKETS = 32
REL_MAX_DIST = 2048
N_CHIPS = 4
N_DEV = 8

ADAM_LR = 0.001
ADAM_B1 = 0.9
ADAM_B2 = 0.999
ADAM_EPS = 1e-08
ADAM_WD = 0.01
ADAM_STEP = 10

VMEM_LIMIT = 56 * 1024 * 1024
MATMUL_VMEM_BUDGET = 36 * 1024 * 1024
ROW_TILE = 512
ATTN_TILE = 256
ATTN_Q_TILE = 512
MLA_GROUP = 4
FOX_GROUP = 4


def _params(sem=None):
    return pltpu.CompilerParams(dimension_semantics=sem, vmem_limit_bytes=VMEM_LIMIT)


def _divisor_tiles(dim):
    tiles = [t for t in range(LANE, dim + 1, LANE) if dim % t == 0]
    return tiles or [dim]


def _matmul_tiles(m, n, k, a_bytes, b_bytes, out_bytes, has_add, n_unit=None, k_unit=None):
    best = None
    for tm in _divisor_tiles(m):
        for tn in _divisor_tiles(n_unit or n):
            for tk in _divisor_tiles(k_unit or k):
                if max(tm, tn, tk) > 2048:
                    continue
                vmem = 2 * (tm * tk * a_bytes + tk * tn * b_bytes + tm * tn * out_bytes) + tm * tn * 4
                if has_add:
                    vmem += 2 * tm * tn * 4
                if vmem > MATMUL_VMEM_BUDGET:
                    continue
                steps = (m // tm) * (n // tn) * (k // tk)
                traffic = m * k * a_bytes * (n // tn) + k * n * b_bytes * (m // tm) + m * n * out_bytes
                cost = traffic / 3.0e12 + steps * 0.4e-6
                if best is None or cost < best[0]:
                    best = (cost, tm, tn, tk)
    return best[1:]


def _matmul(a, b, *, ta=False, tb=False, b_chunks=False, out_chunks=False, add=None, out_dtype=F32, name):
    k, m = a.shape if ta else a.shape[::-1]
    n_unit = k_unit = None
    if b_chunks:
        chunks, rows_w, c = b.shape
        if tb:
            kb, n, k_unit = chunks * c, rows_w, c
        else:
            kb, n, n_unit = rows_w, chunks * c, c
    else:
        kb, n = b.shape[::-1] if tb else b.shape
    if out_chunks:
        assert n % N_CHIPS == 0 and add is None
        n_unit = n // N_CHIPS
    assert k == kb, (a.shape, b.shape, ta, tb)
    tm, tn, tk = _matmul_tiles(m, n, k, a.dtype.itemsize, b.dtype.itemsize, jnp.dtype(out_dtype).itemsize,
                               add is not None, n_unit, k_unit)
    nk = k // tk
    dims = (((0 if ta else 1,), (1 if tb else 0,)), ((), ()))

    def body(*refs):
        if add is None:
            a_ref, b_ref, o_ref, acc_ref = refs
            add_ref = None
        else:
            a_ref, b_ref, add_ref, o_ref, acc_ref = refs
        kk = pl.program_id(2)

        @pl.when(kk == 0)
        def _():
            acc_ref[...] = jnp.zeros_like(acc_ref)

        acc_ref[...] += lax.dot_general(a_ref[...].astype(BF), b_ref[...].astype(BF), dims,
                                        preferred_element_type=F32)

        @pl.when(kk == nk - 1)
        def _():
            r = acc_ref[...]
            if add_ref is not None:
                r = r + add_ref[...].astype(F32)
            o_ref[...] = r.astype(out_dtype)

    a_spec = pl.BlockSpec((tk, tm), lambda i, j, q: (q, i)) if ta else pl.BlockSpec((tm, tk), lambda i, j, q: (i, q))
    if b_chunks and tb:
        per_k = k_unit // tk
        b_spec = pl.BlockSpec((None, tn, tk), lambda i, j, q: (q // per_k, j, q % per_k))
    elif b_chunks:
        per_n = n_unit // tn
        b_spec = pl.BlockSpec((None, tk, tn), lambda i, j, q: (j // per_n, q, j % per_n))
    elif tb:
        b_spec = pl.BlockSpec((tn, tk), lambda i, j, q: (j, q))
    else:
        b_spec = pl.BlockSpec((tk, tn), lambda i, j, q: (q, j))
    if out_chunks:
        per_o = n_unit // tn
        o_spec = pl.BlockSpec((None, tm, tn), lambda i, j, q: (j // per_o, i, j % per_o))
        out_shape = jax.ShapeDtypeStruct((N_CHIPS, m, n_unit), out_dtype)
    else:
        o_spec = pl.BlockSpec((tm, tn), lambda i, j, q: (i, j))
        out_shape = jax.ShapeDtypeStruct((m, n), out_dtype)
    in_specs = [a_spec, b_spec]
    args = [a, b]
    if add is not None:
        in_specs.append(o_spec)
        args.append(add)
    return pl.pallas_call(
        body, out_shape=out_shape, grid=(m // tm, n // tn, nk),
        in_specs=in_specs, out_specs=o_spec, scratch_shapes=[pltpu.VMEM((tm, tn), F32)], name=name,
        compiler_params=_params(("parallel", "parallel", "arbitrary")))(*args)


def _rowwise(body, name, rows, ins, outs, tr=ROW_TILE):
    def row_spec(cols):
        return pl.BlockSpec((tr, cols), lambda i: (i, 0))

    def full_spec(shape):
        zeros = (0,) * len(shape)
        return pl.BlockSpec(shape, lambda i: zeros)

    in_specs = [row_spec(a.shape[1]) if kind == "row" else full_spec(a.shape) for a, kind in ins]
    out_specs = [row_spec(shape[1]) if kind == "row" else full_spec(shape) for shape, _, kind in outs]
    out_shape = [jax.ShapeDtypeStruct(shape, dtype) for shape, dtype, _ in outs]
    return pl.pallas_call(body, out_shape=out_shape, grid=(rows // tr,), in_specs=in_specs, out_specs=out_specs,
                          name=name, compiler_params=_params(("arbitrary",)))(*[a for a, _ in ins])


def _rstd(x):
    return lax.rsqrt(jnp.mean(x * x, axis=-1, keepdims=True) + NORM_EPS)


def _rms_bwd_math(x, g, dy):
    r = _rstd(x)
    gd = dy * g
    dx = r * gd - x * (r * r * r) * jnp.mean(gd * x, axis=-1, keepdims=True)
    dg = jnp.sum(dy * x * r, axis=0, keepdims=True)
    return dx, dg


def _sigmoid(x):
    return 0.5 * jnp.tanh(0.5 * x) + 0.5


def _init_acc(*refs):
    @pl.when(pl.program_id(0) == 0)
    def _():
        for r in refs:
            r[...] = jnp.zeros_like(r)


def _prenorm(h, g):
    rows, cols = h.shape

    def body(h_ref, g_ref, o_ref):
        x = h_ref[...]
        o_ref[...] = (x * _rstd(x) * g_ref[...]).astype(BF)

    return _rowwise(body, "prenorm", rows, [(h, "row"), (g, "full")], [((rows, cols), BF, "row")])[0]


def _post_residual(h, y, g_post, g_pre):
    rows, cols = h.shape
    with_pre = g_pre is not None

    def body(*refs):
        if with_pre:
            h_ref, y_ref, gp_ref, gq_ref, hn_ref, hb_ref = refs
        else:
            h_ref, y_ref, gp_ref, hn_ref, hb_ref = refs
        yv = y_ref[...]
        hn = h_ref[...] + yv * _rstd(yv) * gp_ref[...]
        hn_ref[...] = hn
        hb_ref[...] = (hn * _rstd(hn) * gq_ref[...] if with_pre else hn).astype(BF)

    ins = [(h, "row"), (y, "row"), (g_post, "full")] + ([(g_pre, "full")] if with_pre else [])
    return _rowwise(body, "post_residual_pre" if with_pre else "post_residual", rows, ins,
                    [((rows, cols), F32, "row"), ((rows, cols), BF, "row")])


def _ple_forward(h2, pp, z, g_pre):
    rows, cols = h2.shape

    def body(h_ref, p_ref, z_ref, g_ref, h3_ref, hb_ref):
        h3 = h_ref[...] + p_ref[...] * _sigmoid(z_ref[...])
        h3_ref[...] = h3
        hb_ref[...] = (h3 * _rstd(h3) * g_ref[...]).astype(BF)

    return _rowwise(body, "ple_forward", rows, [(h2, "row"), (pp, "row"), (z, "row"), (g_pre, "full")],
                    [((rows, cols), F32, "row"), ((rows, cols), BF, "row")])


def _ple_loss(h2, pp, z, target):
    rows, cols = h2.shape

    def body(h_ref, p_ref, z_ref, t_ref, dh_ref, sq_ref):
        _init_acc(sq_ref)
        err = h_ref[...] + p_ref[...] * _sigmoid(z_ref[...]) - t_ref[...]
        dh_ref[...] = err * (1.0 / cols)
        sq_ref[...] += jnp.sum(err * err, axis=0, keepdims=True)

    return _rowwise(body, "ple_loss", rows, [(h2, "row"), (pp, "row"), (z, "row"), (target, "row")],
                    [((rows, cols), F32, "row"), ((1, cols), F32, "acc")])


def _ple_backward(dh3, pp, z):
    rows, cols = dh3.shape

    def body(d_ref, p_ref, z_ref, dpp_ref, dz_ref):
        d = d_ref[...]
        s = _sigmoid(z_ref[...])
        dpp_ref[...] = (d * s).astype(BF)
        dz_ref[...] = (d * p_ref[...] * s * (1.0 - s)).astype(BF)

    return _rowwise(body, "ple_backward", rows, [(dh3, "row"), (pp, "row"), (z, "row")],
                    [((rows, cols), BF, "row"), ((rows, cols), BF, "row")])


def _rms_backward(x, g, dy, add, out_dtype):
    rows, cols = x.shape
    with_add = add is not None

    def body(*refs):
        if with_add:
            x_ref, g_ref, dy_ref, add_ref, dx_ref, dg_ref = refs
        else:
            x_ref, g_ref, dy_ref, dx_ref, dg_ref = refs
        _init_acc(dg_ref)
        dx, dg = _rms_bwd_math(x_ref[...], g_ref[...], dy_ref[...].astype(F32))
        if with_add:
            dx = dx + add_ref[...]
        dx_ref[...] = dx.astype(out_dtype)
        dg_ref[...] += dg

    ins = [(x, "row"), (g, "full"), (dy, "row")] + ([(add, "row")] if with_add else [])
    return _rowwise(body, "rms_backward_add" if with_add else "rms_backward", rows, ins,
                    [((rows, cols), out_dtype, "row"), ((1, cols), F32, "acc")])


def _swiglu_forward(gu):
    rows = gu.shape[0]

    def body(gu_ref, o_ref):
        g = gu_ref[:, :D_FF].astype(F32)
        o_ref[...] = (g * _sigmoid(g) * gu_ref[:, D_FF:].astype(F32)).astype(BF)

    return _rowwise(body, "swiglu_forward", rows, [(gu, "row")], [((rows, D_FF), BF, "row")])[0]


def _swiglu_backward(gu, dact):
    rows = gu.shape[0]

    def body(gu_ref, d_ref, o_ref):
        g = gu_ref[:, :D_FF].astype(F32)
        u = gu_ref[:, D_FF:].astype(F32)
        d = d_ref[...].astype(F32)
        s = _sigmoid(g)
        gs = g * s
        o_ref[:, :D_FF] = (d * u * (s + gs * (1.0 - s))).astype(BF)
        o_ref[:, D_FF:] = (d * gs).astype(BF)

    return _rowwise(body, "swiglu_backward", rows, [(gu, "row"), (dact, "row")], [((rows, 2 * D_FF), BF, "row")])[0]


def _rope_tables(positions):
    half = MLA_ROPE // 2
    inv = ROPE_THETA ** (-jnp.arange(half, dtype=F32) / half)
    ang = positions.astype(F32)[:, None] * inv
    cos, sin = jnp.cos(ang), jnp.sin(ang)
    rows = positions.shape[0]
    c = jnp.ones((rows, LANE), F32).at[:, 64:80].set(cos).at[:, 80:96].set(cos)
    sa = jnp.zeros((rows, LANE), F32).at[:, 64:80].set(-sin)
    sb = jnp.zeros((rows, LANE), F32).at[:, 80:96].set(sin)
    return c, sa, sb


def _rope_apply(x, c, sa, sb):
    return x * c + pltpu.roll(x, LANE - 16, 1) * sa + pltpu.roll(x, 16, 1) * sb


def _rope_apply_t(dy, c, sa, sb):
    return dy * c + pltpu.roll(dy * sa, 16, 1) + pltpu.roll(dy * sb, LANE - 16, 1)


def _rope_heads(x, tables, transpose, name):
    rows, cols = x.shape

    def body(x_ref, c_ref, sa_ref, sb_ref, o_ref):
        fn = _rope_apply_t if transpose else _rope_apply
        c, sa, sb = c_ref[...], sa_ref[...], sb_ref[...]
        for head in range(cols // LANE):
            lanes = slice(head * LANE, (head + 1) * LANE)
            o_ref[:, lanes] = fn(x_ref[:, lanes].astype(F32), c, sa, sb).astype(BF)

    blk = pl.BlockSpec((ROW_TILE, cols), lambda i: (i, 0))
    tbl = pl.BlockSpec((ROW_TILE, LANE), lambda i: (i, 0))
    return pl.pallas_call(body, out_shape=jax.ShapeDtypeStruct((rows, cols), BF), grid=(rows // ROW_TILE,),
                          in_specs=[blk, tbl, tbl, tbl], out_specs=blk, name=name,
                          compiler_params=_params(("parallel",)))(x, *tables)


def _mla_mid_forward(a, q_norm, kv_norm, tables):
    rows = a.shape[0]
    qr, kvr = MLA_Q_RANK, MLA_KV_RANK

    def body(a_ref, qn_ref, kn_ref, c_ref, sa_ref, sb_ref, cq_ref, ckv_ref, kr_ref):
        aq = a_ref[:, 0:qr]
        akv = a_ref[:, qr:qr + kvr]
        cq_ref[...] = (aq * _rstd(aq) * qn_ref[...]).astype(BF)
        ckv_ref[...] = (akv * _rstd(akv) * kn_ref[...]).astype(BF)
        kr_ref[...] = _rope_apply(a_ref[:, qr + kvr:], c_ref[...], sa_ref[...], sb_ref[...]).astype(BF)

    ins = [(a, "row"), (q_norm, "full"), (kv_norm, "full")] + [(t, "row") for t in tables]
    return _rowwise(body, "mla_mid_forward", rows, ins,
                    [((rows, qr), BF, "row"), ((rows, kvr), BF, "row"), ((rows, LANE), BF, "row")])


def _mla_mid_backward(a, q_norm, kv_norm, tables, dcq, dckv, dkr):
    rows = a.shape[0]
    qr, kvr = MLA_Q_RANK, MLA_KV_RANK

    def body(a_ref, qn_ref, kn_ref, c_ref, sa_ref, sb_ref, dcq_ref, dckv_ref, dkr_ref, da_ref, dqn_ref, dkn_ref):
        _init_acc(dqn_ref, dkn_ref)
        dxq, dgq = _rms_bwd_math(a_ref[:, 0:qr], qn_ref[...], dcq_ref[...])
        dxk, dgk = _rms_bwd_math(a_ref[:, qr:qr + kvr], kn_ref[...], dckv_ref[...])
        da_ref[:, 0:qr] = dxq.astype(BF)
        da_ref[:, qr:qr + kvr] = dxk.astype(BF)
        da_ref[:, qr + kvr:] = _rope_apply_t(dkr_ref[...], c_ref[...], sa_ref[...], sb_ref[...]).astype(BF)
        dqn_ref[...] += dgq
        dkn_ref[...] += dgk

    ins = ([(a, "row"), (q_norm, "full"), (kv_norm, "full")] + [(t, "row") for t in tables]
           + [(dcq, "row"), (dckv, "row"), (dkr, "row")])
    return _rowwise(body, "mla_mid_backward", rows, ins,
                    [((rows, MLA_A_PAD), BF, "row"), ((1, qr), F32, "acc"), ((1, kvr), F32, "acc")])


def _attn_specs(rows, kv_off, g, many_row_vectors):
    head =pl.BlockSpec((rows, g * LANE), lambda h: (0, h))
    kv_head = pl.BlockSpec((rows, g * LANE), lambda h: (0, h + kv_off // g))
    shared = pl.BlockSpec((rows, LANE), lambda h: (0, 0))
    col_vec = pl.BlockSpec((g, rows, 1), lambda h: (h, 0, 0),
                           pipeline_mode=pl.Buffered(1 if many_row_vectors else 2))
    row_vec = pl.BlockSpec((g, 1, rows), lambda h: (h, 0, 0))
    return head, kv_head, shared, col_vec, row_vec


def _attn_forward(q, kv, kv_off, kr, cum_col, cum_row, scale, group_size, name):
    rows = q.shape[0]
    heads = HEADS
    t = ATTN_TILE
    tq = ATTN_Q_TILE
    per = tq // t
    has_kr = kr is not None
    has_f = cum_col is not None
    group = range(group_size)

    def body(*refs):
        it = iter(refs)
        q_ref, kv_ref = next(it), next(it)
        kr_ref = next(it) if has_kr else None
        cc_ref = next(it) if has_f else None
        cr_ref = next(it) if has_f else None
        o_ref, lse_ref = next(it), next(it)
        lo = lax.broadcasted_iota(jnp.int32, (1, LANE), 1) < HEAD_DIM
        row = lax.broadcasted_iota(jnp.int32, (tq, t), 0)
        col = lax.broadcasted_iota(jnp.int32, (tq, t), 1)
        lanes = [slice(g * LANE, (g + 1) * LANE) for g in group]

        def q_block(i, _):
            qs = pl.ds(pl.multiple_of(i * tq, tq), tq)
            qbs = [q_ref[qs, lanes[g]] for g in group]
            cqs = [cc_ref[g, qs, :] if has_f else None for g in group]

            def step(j, carry, diag):
                ks = pl.ds(pl.multiple_of(j * t, t), t)
                other = kr_ref[ks, :] if has_kr else jnp.zeros((t, LANE), BF)
                kvbs = [kv_ref[ks, lanes[g]] for g in group]
                logits = [lax.dot_general(qbs[g], jnp.where(lo, kvbs[g], other), (((1,), (1,)), ((), ())),
                                          preferred_element_type=F32) for g in group]
                out = []
                for g in group:
                    m, l, acc = carry[g]
                    s = logits[g] * scale
                    if has_f:
                        s = s + (cqs[g] - cr_ref[g, :, ks])
                    if diag is not None:
                        s = jnp.where(col + diag * t <= row, s, NEG_INF)
                    mn = jnp.maximum(m, jnp.max(s, axis=1, keepdims=True))
                    alpha = jnp.exp(m - mn)
                    p = jnp.exp(s - mn)
                    l = alpha * l + jnp.sum(p, axis=1, keepdims=True)
                    acc = alpha * acc + jnp.dot(p.astype(BF), kvbs[g], preferred_element_type=F32)
                    out.append((mn, l, acc))
                return tuple(out)

            init = tuple((jnp.full((tq, 1), NEG_INF, F32), jnp.zeros((tq, 1), F32), jnp.zeros((tq, LANE), F32))
                         for _ in group)
            carry = lax.fori_loop(0, i * per, lambda j, c: step(j, c, None), init)
            for d in range(per):
                carry = step(i * per + d, carry, d)
            for g, (m, l, acc) in enumerate(carry):
                o_ref[qs, lanes[g]] = jnp.where(lo, 0.0, acc * (1.0 / l)).astype(BF)
                lse_ref[g, qs, :] = m + jnp.log(l)
            return 0

        lax.fori_loop(0, rows // tq, q_block, 0)

    head, kv_head, shared, col_vec, row_vec = _attn_specs(rows, kv_off, group_size, has_f)
    in_specs, args = [head, kv_head], [q, kv]
    if has_kr:
        in_specs.append(shared)
        args.append(kr)
    if has_f:
        in_specs += [col_vec, row_vec]
        args += [cum_col, cum_row]
    return pl.pallas_call(
        body, out_shape=[jax.ShapeDtypeStruct((rows, heads * LANE), BF), jax.ShapeDtypeStruct((heads, rows, 1), F32)],
        grid=(heads // group_size,), in_specs=in_specs, out_specs=[head, col_vec], name=name,
        compiler_params=_params(("arbitrary",)))(*args)


def _attn_backward(q, kv, kv_off, kr, cum_col, cum_row, o, do, lse, scale, group_size, name):
    rows = q.shape[0]
    heads = HEADS
    t = ATTN_TILE
    nb = rows // t
    has_kr = kr is not None
    has_f = cum_col is not None
    group = range(group_size)

    def body(*refs):
        it = iter(refs)
        q_ref, kv_ref = next(it), next(it)
        kr_ref = next(it) if has_kr else None
        cc_ref = next(it) if has_f else None
        cr_ref = next(it) if has_f else None
        o_ref, do_ref, lse_ref = next(it), next(it), next(it)
        dq_ref, dkv_ref = next(it), next(it)
        dkr_ref = next(it) if has_kr else None
        dck_ref = next(it) if has_f else None
        dcq_ref = next(it) if has_f else None
        dq_acc = next(it)
        lo = lax.broadcasted_iota(jnp.int32, (1, LANE), 1) < HEAD_DIM
        causal = (lax.broadcasted_iota(jnp.int32, (t, t), 1) <= lax.broadcasted_iota(jnp.int32, (t, t), 0))
        lanes = [slice(g * LANE, (g + 1) * LANE) for g in group]

        dq_acc[...] = jnp.zeros_like(dq_acc)
        if has_kr:
            _init_acc(dkr_ref)
        if has_f:
            dcq_ref[...] = jnp.zeros_like(dcq_ref)

        def kv_block(j, _):
            ks = pl.ds(pl.multiple_of(j * t, t), t)
            other = kr_ref[ks, :] if has_kr else jnp.zeros((t, LANE), BF)
            kvbs = [kv_ref[ks, lanes[g]] for g in group]
            kks = [jnp.where(lo, kvbs[g], other) for g in group]
            cks = [cr_ref[g, :, ks] if has_f else None for g in group]

            def pair(i, carry, diag):
                qs = pl.ds(pl.multiple_of(i * t, t), t)
                out = []
                for g in group:
                    dkk, dvv, dcs = carry[g]
                    qb = q_ref[qs, lanes[g]]
                    dob = do_ref[qs, lanes[g]]
                    s = lax.dot_general(qb, kks[g], (((1,), (1,)), ((), ())), preferred_element_type=F32) * scale
                    if has_f:
                        s = s + (cc_ref[g, qs, :] - cks[g])
                    if diag:
                        s = jnp.where(causal, s, NEG_INF)
                    p = jnp.exp(s - lse_ref[g, qs, :])
                    dp = lax.dot_general(dob, kvbs[g], (((1,), (1,)), ((), ())), preferred_element_type=F32)
                    delta = jnp.sum(dob.astype(F32) * o_ref[qs, lanes[g]].astype(F32), axis=1, keepdims=True)
                    ds = p * (dp - delta)
                    dsb = ds.astype(BF)
                    dvv = dvv + lax.dot_general(p.astype(BF), dob, (((0,), (0,)), ((), ())), preferred_element_type=F32)
                    dkk = dkk + lax.dot_general(dsb, qb, (((0,), (0,)), ((), ())), preferred_element_type=F32)
                    dq_acc[qs, lanes[g]] += jnp.dot(dsb, kks[g], preferred_element_type=F32)
                    if has_f:
                        dcs = dcs + jnp.sum(ds, axis=0, keepdims=True)
                        dcq_ref[g, qs, :] += jnp.sum(ds, axis=1, keepdims=True)
                    out.append((dkk, dvv, dcs))
                return tuple(out)

            init = tuple((jnp.zeros((t, LANE), F32), jnp.zeros((t, LANE), F32), jnp.zeros((1, t), F32)) for _ in group)
            carry = pair(j, init, True)
            carry = lax.fori_loop(j + 1, nb, lambda i, c: pair(i, c, False), carry)
            for g, (dkk, dvv, dcs) in enumerate(carry):
                dkk = dkk * scale
                dkv_ref[ks, lanes[g]] = jnp.where(lo, dkk, dvv).astype(BF)
                if has_kr:
                    dkr_ref[ks, :] += jnp.where(lo, 0.0, dkk)
                if has_f:
                    dck_ref[g, :, ks] = -dcs
            return 0

        lax.fori_loop(0, nb, kv_block, 0)
        dq_ref[...] = (dq_acc[...] * scale).astype(BF)

    head, kv_head, shared, col_vec, row_vec = _attn_specs(rows, kv_off, group_size, has_f)
    in_specs, args = [head, kv_head], [q, kv]
    if has_kr:
        in_specs.append(shared)
        args.append(kr)
    if has_f:
        in_specs += [col_vec, row_vec]
        args += [cum_col, cum_row]
    in_specs += [head, head, col_vec]
    args += [o, do, lse]
    out_shape = [jax.ShapeDtypeStruct((rows, heads * LANE), BF), jax.ShapeDtypeStruct((rows, heads * LANE), BF)]
    out_specs = [head, head]
    if has_kr:
        out_shape.append(jax.ShapeDtypeStruct((rows, LANE), F32))
        out_specs.append(shared)
    if has_f:
        out_shape += [jax.ShapeDtypeStruct((heads, 1, rows), F32), jax.ShapeDtypeStruct((heads, rows, 1), F32)]
        out_specs += [row_vec, col_vec]
    return pl.pallas_call(
        body, out_shape=out_shape, grid=(heads // group_size,), in_specs=in_specs, out_specs=out_specs,
        scratch_shapes=[pltpu.VMEM((rows, group_size * LANE), F32)], name=name,
        compiler_params=_params(("arbitrary",)))(*args)


def _tri_dot(tri, x):
    return jnp.dot(tri, x, preferred_element_type=F32, precision=lax.Precision.HIGHEST)


def _forget_forward(f_raw, b_f):
    rows = f_raw.shape[0]
    t = ATTN_TILE

    def body(f_ref, b_ref, cum_ref):
        tri = (lax.broadcasted_iota(jnp.int32, (t, t), 1) <= lax.broadcasted_iota(jnp.int32, (t, t), 0)).astype(F32)

        def blk(i, carry):
            sl = pl.ds(pl.multiple_of(i * t, t), t)
            xv = f_ref[sl, :] + b_ref[...]
            log_f = jnp.minimum(xv, 0.0) - jnp.log(1.0 + jnp.exp(-jnp.abs(xv)))
            cum_ref[sl, :] = _tri_dot(tri, log_f) + carry
            return carry + jnp.sum(log_f, axis=0, keepdims=True)

        lax.fori_loop(0, rows // t, blk, jnp.zeros((1, LANE), F32))

    return pl.pallas_call(body, out_shape=jax.ShapeDtypeStruct((rows, LANE), F32), name="forget_forward",
                          compiler_params=_params())(f_raw, b_f)


def _forget_backward(f_raw, b_f, dcum):
    rows = f_raw.shape[0]
    t = ATTN_TILE
    nb = rows // t

    def body(f_ref, b_ref, dc_ref, df_ref, db_ref):
        tri = (lax.broadcasted_iota(jnp.int32, (t, t), 1) >= lax.broadcasted_iota(jnp.int32, (t, t), 0)).astype(F32)

        def blk(i, carry):
            later, db = carry
            sl = pl.ds(pl.multiple_of((nb - 1 - i) * t, t), t)
            dc = dc_ref[sl, :]
            dlog = _tri_dot(tri, dc) + later
            xv = f_ref[sl, :] + b_ref[...]
            df = dlog / (1.0 + jnp.exp(xv))
            df_ref[sl, :] = df.astype(BF)
            return later + jnp.sum(dc, axis=0, keepdims=True), db + jnp.sum(df, axis=0, keepdims=True)

        _, db = lax.fori_loop(0, nb, blk, (jnp.zeros((1, LANE), F32), jnp.zeros((1, LANE), F32)))
        db_ref[...] = db

    return pl.pallas_call(body, out_shape=[jax.ShapeDtypeStruct((rows, LANE), BF), jax.ShapeDtypeStruct((1, LANE), F32)],
                          name="forget_backward", compiler_params=_params())(f_raw, b_f, dcum)


def _t5_bucket(dist):
    max_exact = REL_BUCKETS // 2
    n = jnp.maximum(dist.astype(F32), 1.0)
    large = max_exact + (jnp.log(n / max_exact) / math.log(REL_MAX_DIST / max_exact)
                         * (REL_BUCKETS - max_exact)).astype(jnp.int32)
    large = jnp.minimum(large, REL_BUCKETS - 1)
    return jnp.where(dist < max_exact, dist, large)


def _dil_buckets(dilation):
    i = jnp.arange(Q_BLOCK)[:, None]
    j = jnp.arange(Q_BLOCK)[None, :]
    cur = _t5_bucket(jnp.clip(i - j, 0) * dilation).astype(jnp.int32)
    prev = _t5_bucket(jnp.clip(Q_BLOCK + i - j, 0) * dilation).astype(jnp.int32)
    return cur, prev


def _dil_bias_tiles(tbl_ref, bc_ref, bp_ref, bias_ref, group, hp):
    ii = lax.broadcasted_iota(jnp.int32, (Q_BLOCK, Q_BLOCK), 0)
    jj = lax.broadcasted_iota(jnp.int32, (Q_BLOCK, Q_BLOCK), 1)
    for hh in range(2 * DIL_PAIRS):
        col = group * HEADS + 2 * DIL_PAIRS * hp + hh
        acc_c = jnp.zeros((Q_BLOCK, Q_BLOCK), F32)
        acc_p = jnp.zeros((Q_BLOCK, Q_BLOCK), F32)
        for b in range(REL_BUCKETS):
            val = tbl_ref[b, col]
            acc_c = jnp.where(bc_ref[...] == b, val, acc_c)
            acc_p = jnp.where(bp_ref[...] == b, val, acc_p)
        bias_ref[2 * hh] = jnp.where(jj <= ii, acc_c, NEG_INF)
        bias_ref[2 * hh + 1] = jnp.where(jj >= ii, acc_p, NEG_INF)


def _dil_view(qkv, group, dilation):
    if dilation == 1:
        return qkv
    width = 3 * HEADS * HEAD_DIM
    return qkv[:, group * width:(group + 1) * width].reshape(qkv.shape[0] // dilation, dilation * width)


def _dil_specs(group, dilation, length):
    width = DIL_PAIRS * LANE
    per = 8 // DIL_PAIRS

    def col(kind):
        if dilation == 1:
            return pl.BlockSpec((length, width), lambda hp, r: (0, (group * 3 + kind) * per + hp))
        return pl.BlockSpec((length, width), lambda hp, r: (0, (r * 3 + kind) * per + hp))

    out = pl.BlockSpec((length, width), lambda hp, r: (0, r * per + hp))
    tile = pl.BlockSpec((Q_BLOCK, Q_BLOCK), lambda hp, r: (0, 0))
    table = pl.BlockSpec(memory_space=pltpu.SMEM)
    return col, out, tile, table


def _dil_forward(view, group, dilation, table, buckets):
    length = view.shape[0]
    rows = length * dilation
    nb = length // Q_BLOCK
    scale = HEAD_DIM ** -0.5
    qb = Q_BLOCK

    def body(tbl_ref, bc_ref, bp_ref, q_ref, k_ref, v_ref, o_ref, lse_ref, bias_ref):
        hp = pl.program_id(0)

        @pl.when(pl.program_id(1) == 0)
        def _():
            _dil_bias_tiles(tbl_ref, bc_ref, bp_ref, bias_ref, group, hp)

        lo = lax.broadcasted_iota(jnp.int32, (1, LANE), 1) < HEAD_DIM
        nt = (((1,), (1,)), ((), ()))

        def blk(n, first):
            cur = pl.ds(0, qb) if first else pl.ds(pl.multiple_of(n * qb, qb), qb)
            prev = None if first else pl.ds(pl.multiple_of((n - 1) * qb, qb), qb)
            for pair in range(DIL_PAIRS):
                lanes = slice(pair * LANE, (pair + 1) * LANE)
                qn = q_ref[cur, lanes] * scale
                kc, vc = k_ref[cur, lanes], v_ref[cur, lanes]
                if not first:
                    kp, vp = k_ref[prev, lanes], v_ref[prev, lanes]
                outs, lses = [], []
                for hh in range(2):
                    bias = 4 * pair + 2 * hh
                    qm = jnp.where(lo if hh == 0 else ~lo, qn, jnp.zeros_like(qn))
                    s_c = lax.dot_general(qm, kc, nt, preferred_element_type=F32) + bias_ref[bias]
                    m = jnp.max(s_c, axis=1, keepdims=True)
                    if not first:
                        s_p = lax.dot_general(qm, kp, nt, preferred_element_type=F32) + bias_ref[bias + 1]
                        m = jnp.maximum(m, jnp.max(s_p, axis=1, keepdims=True))
                    e_c = jnp.exp(s_c - m)
                    l = jnp.sum(e_c, axis=1, keepdims=True)
                    acc = jnp.dot(e_c.astype(BF), vc, preferred_element_type=F32)
                    if not first:
                        e_p = jnp.exp(s_p - m)
                        l = l + jnp.sum(e_p, axis=1, keepdims=True)
                        acc = acc + jnp.dot(e_p.astype(BF), vp, preferred_element_type=F32)
                    outs.append(acc * (1.0 / l))
                    lses.append(m + jnp.log(l))
                o_ref[cur, lanes] = jnp.where(lo, outs[0], outs[1])
                lse_ref[cur, lanes] = jnp.where(lo, lses[0], lses[1])
            return 0

        blk(0, True)
        if nb > 1:
            lax.fori_loop(1, nb, lambda n, _: blk(n, False), 0)

    col, out, tile, tbl = _dil_specs(group, dilation, length)
    bc, bp = buckets
    o, lse = pl.pallas_call(
        body, out_shape=[jax.ShapeDtypeStruct((length, dilation * D_MODEL), F32)] * 2,
        grid=(8 // DIL_PAIRS, dilation), in_specs=[tbl, tile, tile, col(0), col(1), col(2)], out_specs=[out, out],
        scratch_shapes=[pltpu.VMEM((4 * DIL_PAIRS, qb, qb), F32)], name=f"dilated_forward_{dilation}",
        compiler_params=_params(("arbitrary", "arbitrary")))(
            table, bc, bp, view, view, view)
    return o.reshape(rows, D_MODEL), lse.reshape(rows, D_MODEL)


def _dil_backward(view, group, dilation, table, buckets, do_g, lse, dlt):
    length = view.shape[0]
    rows = length * dilation
    nb = length // Q_BLOCK
    scale = HEAD_DIM ** -0.5
    qb = Q_BLOCK

    def body(tbl_ref, bc_ref, bp_ref, q_ref, k_ref, v_ref, do_ref, lse_ref, dlt_ref,
             dq_ref, dk_ref, dv_ref, db_ref, bias_ref, dk_acc, dv_acc):
        hp = pl.program_id(0)

        @pl.when(pl.program_id(1) == 0)
        def _():
            _dil_bias_tiles(tbl_ref, bc_ref, bp_ref, bias_ref, group, hp)
            db_ref[...] = jnp.zeros_like(db_ref)

        dk_acc[...] = jnp.zeros_like(dk_acc)
        dv_acc[...] = jnp.zeros_like(dv_acc)
        lo = lax.broadcasted_iota(jnp.int32, (1, LANE), 1) < HEAD_DIM
        tn = (((0,), (0,)), ((), ()))
        nt = (((1,), (1,)), ((), ()))

        def blk(n, first):
            cur = pl.ds(0, qb) if first else pl.ds(pl.multiple_of(n * qb, qb), qb)
            prev = None if first else pl.ds(pl.multiple_of((n - 1) * qb, qb), qb)
            for pair in range(DIL_PAIRS):
                lanes = slice(pair * LANE, (pair + 1) * LANE)
                qn = q_ref[cur, lanes] * scale
                don = do_ref[cur, lanes]
                kc, vc = k_ref[cur, lanes], v_ref[cur, lanes]
                if not first:
                    kp, vp = k_ref[prev, lanes], v_ref[prev, lanes]
                lse_n = lse_ref[cur, lanes]
                dlt_n = dlt_ref[cur, lanes]
                dqs = []
                dkc = jnp.zeros((qb, LANE), F32)
                dkp = jnp.zeros((qb, LANE), F32)
                dvc = jnp.zeros((qb, LANE), F32)
                dvp = jnp.zeros((qb, LANE), F32)
                for hh in range(2):
                    bias = 4 * pair + 2 * hh
                    mask = lo if hh == 0 else ~lo
                    qm = jnp.where(mask, qn, jnp.zeros_like(qn))
                    dom = jnp.where(mask, don, jnp.zeros_like(don))
                    lse_h = jnp.max(jnp.where(mask, lse_n, -3e38), axis=1, keepdims=True)
                    dlt_h = jnp.max(jnp.where(mask, dlt_n, -3e38), axis=1, keepdims=True)
                    p_c = jnp.exp(lax.dot_general(qm, kc, nt, preferred_element_type=F32) + bias_ref[bias] - lse_h)
                    ds_c = p_c * (lax.dot_general(dom, vc, nt, preferred_element_type=F32) - dlt_h)
                    db_ref[pair, 2 * hh] += ds_c
                    dsc_b = ds_c.astype(BF)
                    dq = jnp.dot(dsc_b, kc, preferred_element_type=F32)
                    dkc = dkc + lax.dot_general(dsc_b, qm, tn, preferred_element_type=F32)
                    dvc = dvc + lax.dot_general(p_c.astype(BF), dom, tn, preferred_element_type=F32)
                    if not first:
                        p_p = jnp.exp(lax.dot_general(qm, kp, nt, preferred_element_type=F32) + bias_ref[bias + 1] - lse_h)
                        ds_p = p_p * (lax.dot_general(dom, vp, nt, preferred_element_type=F32) - dlt_h)
                        db_ref[pair, 2 * hh + 1] += ds_p
                        dsp_b = ds_p.astype(BF)
                        dq = dq + jnp.dot(dsp_b, kp, preferred_element_type=F32)
                        dkp = dkp + lax.dot_general(dsp_b, qm, tn, preferred_element_type=F32)
                        dvp = dvp + lax.dot_general(p_p.astype(BF), dom, tn, preferred_element_type=F32)
                    dqs.append(dq)
                dq_ref[cur, lanes] = (jnp.where(lo, dqs[0], dqs[1]) * scale).astype(BF)
                dk_acc[cur, lanes] += dkc
                dv_acc[cur, lanes] += dvc
                if not first:
                    dk_acc[prev, lanes] += dkp
                    dv_acc[prev, lanes] += dvp
            return 0

        blk(0, True)
        if nb > 1:
            lax.fori_loop(1, nb, lambda n, _: blk(n, False), 0)
        dk_ref[...] = dk_acc[...].astype(BF)
        dv_ref[...] = dv_acc[...].astype(BF)

    col, out, tile, tbl = _dil_specs(group, dilation, length)
    bc, bp = buckets
    wide = (length, dilation * D_MODEL)
    dq, dk, dv, db = pl.pallas_call(
        body, out_shape=[jax.ShapeDtypeStruct(wide, BF)] * 3 + [jax.ShapeDtypeStruct((8, 4, qb, qb), F32)],
        grid=(8 // DIL_PAIRS, dilation), in_specs=[tbl, tile, tile, col(0), col(1), col(2), out, out, out],
        out_specs=[out, out, out, pl.BlockSpec((DIL_PAIRS, 4, qb, qb), lambda hp, r: (hp, 0, 0, 0))],
        scratch_shapes=[pltpu.VMEM((4 * DIL_PAIRS, qb, qb), F32), pltpu.VMEM((length, DIL_PAIRS * LANE), F32),
                        pltpu.VMEM((length, DIL_PAIRS * LANE), F32)],
        name=f"dilated_backward_{dilation}", compiler_params=_params(("arbitrary", "arbitrary")))(
            table, bc, bp, view, view, view,
            do_g.reshape(wide), lse.reshape(wide), dlt.reshape(wide))
    return dq.reshape(rows, D_MODEL), dk.reshape(rows, D_MODEL), dv.reshape(rows, D_MODEL), db


def _head_sums(x, lo):
    s0 = jnp.sum(jnp.where(lo, x, 0.0), axis=1, keepdims=True)
    s1 = jnp.sum(jnp.where(lo, 0.0, x), axis=1, keepdims=True)
    return jnp.where(lo, s0, s1)


def _dil_merge_forward(outs, lses):
    rows = outs[0].shape[0]

    def body(o0, o1, o2, l0, l1, l2, o_ref):
        ls = [l0[...], l1[...], l2[...]]
        m = jnp.maximum(jnp.maximum(ls[0], ls[1]), ls[2])
        es = [jnp.exp(v - m) for v in ls]
        tot = es[0] + es[1] + es[2]
        o_ref[...] = ((es[0] * o0[...] + es[1] * o1[...] + es[2] * o2[...]) / tot).astype(BF)

    blk = pl.BlockSpec((ROW_TILE, LANE), lambda i, j: (i, j))
    return pl.pallas_call(body, out_shape=jax.ShapeDtypeStruct((rows, D_MODEL), BF), grid=(rows // ROW_TILE, 8),
                          in_specs=[blk] * 6, out_specs=blk, name="dilated_merge_forward",
                          compiler_params=_params(("parallel", "parallel")))(*outs, *lses)


def _dil_merge_backward(outs, lses, do):
    rows = outs[0].shape[0]

    def body(o0, o1, o2, l0, l1, l2, do_ref, d0, d1, d2, t0, t1, t2):
        lo = lax.broadcasted_iota(jnp.int32, (1, LANE), 1) < HEAD_DIM
        ls = [l0[...], l1[...], l2[...]]
        os_ = [o0[...], o1[...], o2[...]]
        m = jnp.maximum(jnp.maximum(ls[0], ls[1]), ls[2])
        es = [jnp.exp(v - m) for v in ls]
        inv = 1.0 / (es[0] + es[1] + es[2])
        alphas = [e * inv for e in es]
        dov = do_ref[...]
        merged = alphas[0] * os_[0] + alphas[1] * os_[1] + alphas[2] * os_[2]
        dot = _head_sums(dov * merged, lo)
        for a, d_ref, t_ref in zip(alphas, (d0, d1, d2), (t0, t1, t2)):
            d_ref[...] = (a * dov).astype(BF)
            t_ref[...] = a * dot

    blk = pl.BlockSpec((ROW_TILE, LANE), lambda i, j: (i, j))
    res = pl.pallas_call(
        body, out_shape=[jax.ShapeDtypeStruct((rows, D_MODEL), BF)] * 3 + [jax.ShapeDtypeStruct((rows, D_MODEL), F32)] * 3,
        grid=(rows // ROW_TILE, 8), in_specs=[blk] * 7, out_specs=[blk] * 6, name="dilated_merge_backward",
        compiler_params=_params(("parallel", "parallel")))(*outs, *lses, do)
    return res[:3], res[3:]


def _rel_bias_grad(dbs, buckets):
    def body(db_ref, bc_ref, bp_ref, o_ref):
        g = pl.program_id(0)
        hp = pl.program_id(1)

        @pl.when((g == 0) & (hp == 0))
        def _():
            o_ref[...] = jnp.zeros_like(o_ref)

        rr = lax.broadcasted_iota(jnp.int32, (REL_BUCKETS, LANE), 0)
        cc = lax.broadcasted_iota(jnp.int32, (REL_BUCKETS, LANE), 1)
        bc = bc_ref[0]
        bp = bp_ref[0]
        acc = jnp.zeros((REL_BUCKETS, LANE), F32)
        for hh in range(2):
            col = g * HEADS + 2 * hp + hh
            d_c = db_ref[0, 0, 2 * hh]
            d_p = db_ref[0, 0, 2 * hh + 1]
            for b in range(REL_BUCKETS):
                val = (jnp.sum(jnp.where(bc == b, d_c, 0.0), keepdims=True)
                       + jnp.sum(jnp.where(bp == b, d_p, 0.0), keepdims=True))
                acc = jnp.where((rr == b) & (cc == col), val, acc)
        o_ref[...] += acc

    db_all = jnp.stack(dbs)
    bc_all = jnp.stack([b[0] for b in buckets])
    bp_all = jnp.stack([b[1] for b in buckets])
    tile = pl.BlockSpec((1, Q_BLOCK, Q_BLOCK), lambda g, hp: (g, 0, 0))
    return pl.pallas_call(
        body, out_shape=jax.ShapeDtypeStruct((REL_BUCKETS, LANE), F32), grid=(3, 8),
        in_specs=[pl.BlockSpec((1, 1, 4, Q_BLOCK, Q_BLOCK), lambda g, hp: (g, hp, 0, 0, 0)), tile, tile],
        out_specs=pl.BlockSpec((REL_BUCKETS, LANE), lambda g, hp: (0, 0)), name="rel_bias_grad",
        compiler_params=_params(("arbitrary", "arbitrary")))(db_all, bc_all, bp_all)


def _mla_forward(hn, w, tables):
    a = _matmul(hn, w["w_a"], name="mla_a")
    cq, ckv, kr = _mla_mid_forward(a, w["q_norm"], w["kv_norm"], tables)
    q_raw = _matmul(cq, w["w_uq"], name="mla_uq")
    q = _rope_heads(q_raw, tables, False, "rope_forward")
    kv = _matmul(ckv, w["w_ukv"], b_chunks=True, out_dtype=BF, name="mla_ukv")
    scale = (HEAD_DIM + MLA_ROPE) ** -0.5
    o, lse = _attn_forward(q, kv, 0, kr, None, None, scale, MLA_GROUP, "mla_attention_forward")
    y = _matmul(o, w["w_o"], name="attn_out")
    return y, dict(hn=hn, a=a, cq=cq, ckv=ckv, kr=kr, q=q, kv=kv, o=o, lse=lse)


def _mla_backward(dy, w, s, tables):
    scale = (HEAD_DIM + MLA_ROPE) ** -0.5
    g = {}
    g["w_o"] = _matmul(s["o"], dy, ta=True, out_dtype=BF, name="attn_out_dw")
    do = _matmul(dy, w["w_o"], tb=True, out_dtype=BF, name="attn_out_dx")
    dq, dkv, dkr = _attn_backward(s["q"], s["kv"], 0, s["kr"], None, None, s["o"], do, s["lse"], scale,
                                  MLA_GROUP, "mla_attention_backward")
    dq_raw = _rope_heads(dq, tables, True, "rope_backward")
    g["w_uq"] = _matmul(s["cq"], dq_raw, ta=True, out_dtype=BF, name="mla_uq_dw")
    dcq = _matmul(dq_raw, w["w_uq"], tb=True, name="mla_uq_dx")
    g["w_ukv"] = _matmul(s["ckv"], dkv, ta=True, out_chunks=True, out_dtype=BF, name="mla_ukv_dw")
    dckv = _matmul(dkv, w["w_ukv"], tb=True, b_chunks=True, name="mla_ukv_dx")
    da, g["q_norm"], g["kv_norm"] = _mla_mid_backward(s["a"], w["q_norm"], w["kv_norm"], tables, dcq, dckv, dkr)
    g["w_a"] = _matmul(s["hn"], da, ta=True, out_dtype=BF, name="mla_a_dw")
    dhn = _matmul(da, w["w_a"], tb=True, name="mla_a_dx")
    return dhn, g


def _fox_forward(hn, w):
    qkv = _matmul(hn, w["w_qkv"], out_dtype=BF, name="fox_qkv")
    f_raw = _matmul(hn, w["w_f"], name="fox_f")
    cum = _forget_forward(f_raw, w["b_f"])
    cum_heads = cum[:, :HEADS].T
    cum_col, cum_row = cum_heads[:, :, None], cum_heads[:, None, :]
    o, lse = _attn_forward(qkv, qkv, HEADS, None, cum_col, cum_row, HEAD_DIM ** -0.5, FOX_GROUP,
                           "fox_attention_forward")
    y = _matmul(o, w["w_o"], name="attn_out")
    return y, dict(hn=hn, qkv=qkv, f_raw=f_raw, cum_col=cum_col, cum_row=cum_row, o=o, lse=lse)


def _fox_backward(dy, w, s):
    g = {}
    g["w_o"] = _matmul(s["o"], dy, ta=True, out_dtype=BF, name="attn_out_dw")
    do = _matmul(dy, w["w_o"], tb=True, out_dtype=BF, name="attn_out_dx")
    dq, dkv, dck, dcq = _attn_backward(s["qkv"], s["qkv"], HEADS, None, s["cum_col"], s["cum_row"], s["o"], do,
                                       s["lse"], HEAD_DIM ** -0.5, FOX_GROUP, "fox_attention_backward")
    dcum = jnp.pad((dck[:, 0, :] + dcq[:, :, 0]).T, ((0, 0), (0, LANE - HEADS)))
    df, g["b_f"] = _forget_backward(s["f_raw"], w["b_f"], dcum)
    dqkv = jnp.concatenate([dq, dkv], axis=1)
    g["w_qkv"] = _matmul(s["hn"], dqkv, ta=True, out_dtype=BF, name="fox_qkv_dw")
    g["w_f"] = _matmul(s["hn"], df, ta=True, out_dtype=BF, name="fox_f_dw")
    dhn = _matmul(dqkv, w["w_qkv"], tb=True, name="fox_qkv_dx")
    dhn = _matmul(df, w["w_f"], tb=True, add=dhn, name="fox_f_dx")
    return dhn, g


def _dil_mixer_forward(hn, w, buckets):
    qkv = _matmul(hn, w["w_qkv"], b_chunks=True, out_dtype=BF, name="dil_qkv")
    views = [_dil_view(qkv, grp, dilation) for grp, (_, dilation) in enumerate(DIL_PATTERNS)]
    outs, lses = [], []
    for grp, (_, dilation) in enumerate(DIL_PATTERNS):
        o_g, lse_g = _dil_forward(views[grp], grp, dilation, w["rel_bias"], buckets[grp])
        outs.append(o_g)
        lses.append(lse_g)
    o = _dil_merge_forward(outs, lses)
    y = _matmul(o, w["w_o"], name="dil_out")
    return y, dict(hn=hn, views=views, outs=outs, lses=lses, o=o)


def _dil_mixer_backward(dy, w, s, buckets):
    g = {}
    g["w_o"] = _matmul(s["o"], dy, ta=True, out_dtype=BF, name="dil_out_dw")
    do = _matmul(dy, w["w_o"], tb=True, name="dil_out_dx")
    do_gs, dlts = _dil_merge_backward(s["outs"], s["lses"], do)
    parts, dbs = [], []
    for grp, (_, dilation) in enumerate(DIL_PATTERNS):
        dq, dk, dv, db = _dil_backward(s["views"][grp], grp, dilation, w["rel_bias"], buckets[grp], do_gs[grp],
                                       s["lses"][grp], dlts[grp])
        parts += [dq, dk, dv]
        dbs.append(db)
    dqkv = jnp.concatenate(parts, axis=1)
    g["rel_bias"] = _rel_bias_grad(dbs, buckets)
    g["w_qkv"] = _matmul(s["hn"], dqkv, ta=True, out_chunks=True, out_dtype=BF, name="dil_qkv_dw")
    dhn = _matmul(dqkv, w["w_qkv"], tb=True, b_chunks=True, name="dil_qkv_dx")
    return dhn, g


def _mixer_weights(i, lw, small):
    mixer, j = i % N_MIXERS, i // N_MIXERS
    if mixer == 0:
        return dict(lw["mixer"], q_norm=small["mla_q_norm"][j][None, :], kv_norm=small["mla_kv_norm"][j][None, :])
    if mixer == 1:
        return dict(lw["mixer"], rel_bias=small["rel_bias"])
    return dict(lw["mixer"], b_f=jnp.pad(small["fox_b_f"][j][None, :], ((0, 0), (0, LANE - HEADS))))


MIXER_PART, COMMON_PART = 0, 1


def _run_layers(x, p, positions, target, get_part, get_small, put_part):
    tables = _rope_tables(positions)
    buckets = [_dil_buckets(d) for _, d in DIL_PATTERNS]
    layers, saved = [], []
    h = x
    first = get_part(0, MIXER_PART, positions)
    small = get_small()

    def gain(i, k):
        return small["norm_g"][i, k][None, :]

    hn = _prenorm(h, gain(0, 0))
    sq = dh = None
    for i in range(DEPTH):
        mixer = i % N_MIXERS
        lw = dict(mixer=first if i == 0 else get_part(i, MIXER_PART, h))
        mw = _mixer_weights(i, lw, small)
        if mixer == 0:
            y, ms = _mla_forward(hn, mw, tables)
        elif mixer == 1:
            y, ms = _dil_mixer_forward(hn, mw, buckets)
        else:
            y, ms = _fox_forward(hn, mw)
        lw.update(get_part(i, COMMON_PART, y))
        layers.append(lw)
        h1, hn2 = _post_residual(h, y, gain(i, 1), gain(i, 2))
        gu = _matmul(hn2, lw["ffn_w_in"], b_chunks=True, out_dtype=BF, name="ffn_in")
        act = _swiglu_forward(gu)
        f = _matmul(act, lw["ffn_w_out"], name="ffn_out")
        h2, h2b = _post_residual(h1, f, gain(i, 3), None)
        pp = _matmul(p[i], lw["ple_w_proj"], b_chunks=True, name="ple_proj")
        z = _matmul(h2b, lw["ple_w_gate"], name="ple_gate")
        saved.append(dict(h=h, y=y, ms=ms, h1=h1, hn2=hn2, gu=gu, act=act, f=f, h2b=h2b, pp=pp, z=z))
        if i + 1 < DEPTH:
            h, hn = _ple_forward(h2, pp, z, gain(i + 1, 0))
        else:
            dh, sq = _ple_loss(h2, pp, z, target)

    norm_rows = [[None] * 4 for _ in range(DEPTH)]
    sg = dict(mla_q_norm={}, mla_kv_norm={}, rel_bias=None, fox_b_f={})
    for i in reversed(range(DEPTH)):
        s, lw = saved[i], layers[i]
        mixer, j = i % N_MIXERS, i // N_MIXERS
        mw = _mixer_weights(i, lw, small)
        lg = {}
        dpp, dz = _ple_backward(dh, s["pp"], s["z"])
        lg["ple_w_proj"] = _matmul(p[i], dpp, ta=True, out_chunks=True, out_dtype=BF, name="ple_proj_dw")
        lg["ple_w_gate"] = _matmul(s["h2b"], dz, ta=True, out_dtype=BF, name="ple_gate_dw")
        dh2 = _matmul(dz, lw["ple_w_gate"], tb=True, add=dh, name="ple_gate_dx")
        df, norm_rows[i][3] = _rms_backward(s["f"], gain(i, 3), dh2, None, BF)
        lg["ffn_w_out"] = _matmul(s["act"], df, ta=True, out_dtype=BF, name="ffn_out_dw")
        dact = _matmul(df, lw["ffn_w_out"], tb=True, out_dtype=BF, name="ffn_out_dx")
        dgu = _swiglu_backward(s["gu"], dact)
        lg["ffn_w_in"] = _matmul(s["hn2"], dgu, ta=True, out_chunks=True, out_dtype=BF, name="ffn_in_dw")
        token = put_part(i, COMMON_PART, lg)
        dhn2 = _matmul(dgu, lw["ffn_w_in"], tb=True, b_chunks=True, name="ffn_in_dx")
        dh1, norm_rows[i][2] = _rms_backward(s["h1"], gain(i, 2), dhn2, dh2, F32)
        dy, norm_rows[i][1] = _rms_backward(s["y"], gain(i, 1) + token[0:1, 0:1], dh1, None, BF)
        if mixer == 0:
            dhn, mg = _mla_backward(dy, mw, s["ms"], tables)
            sg["mla_q_norm"][j] = mg.pop("q_norm")
            sg["mla_kv_norm"][j] = mg.pop("kv_norm")
        elif mixer == 1:
            dhn, mg = _dil_mixer_backward(dy, mw, s["ms"], buckets)
            rel = mg.pop("rel_bias")[:, :3 * HEADS]
            sg["rel_bias"] = rel if sg["rel_bias"] is None else sg["rel_bias"] + rel
        else:
            dhn, mg = _fox_backward(dy, mw, s["ms"])
            sg["fox_b_f"][j] = mg.pop("b_f")[:, :HEADS]
        token = put_part(i, MIXER_PART, mg)
        dh, norm_rows[i][0] = _rms_backward(s["h"], gain(i, 0) + token[0:1, 0:1], dhn, dh1, F32)
    small_grads = dict(norm_g=jnp.stack([jnp.concatenate(row, axis=0) for row in norm_rows]),
                       rel_bias=sg["rel_bias"])
    for k in ("mla_q_norm", "mla_kv_norm", "fox_b_f"):
        small_grads[k] = jnp.concatenate([sg[k][j] for j in sorted(sg[k])], axis=0)
    return sq, dh, small_grads


BIG = ("ffn_w_in", "ffn_w_out", "ple_w_proj", "ple_w_gate", "mla_w_a", "mla_w_uq", "mla_w_ukv", "mla_w_o",
       "dil_w_qkv", "dil_w_o", "fox_w_qkvf", "fox_w_o")
SMALL_SHARDED = ("norm_g", "mla_q_norm", "mla_kv_norm")
SMALL_REPLICATED = ("rel_bias", "fox_b_f")
WEIGHTS = ("norm_g", "ffn_w_in", "ffn_w_out", "ple_w_proj", "ple_w_gate", "rel_bias", "mla_w_a", "mla_q_norm",
           "mla_kv_norm", "mla_w_uq", "mla_w_ukv", "mla_w_o", "dil_w_qkv", "dil_w_o", "fox_w_qkvf", "fox_b_f", "fox_w_o")


TRANSPOSED = "fox_w_qkvf"
LAYER_COMMON = ("ffn_w_in", "ffn_w_out", "ple_w_proj", "ple_w_gate")
MIXER_WEIGHTS = (("mla_w_a", "mla_w_uq", "mla_w_ukv", "mla_w_o"), ("dil_w_qkv", "dil_w_o"), ("fox_w_qkvf", "fox_w_o"))


def _part_names(i, part):
    return MIXER_WEIGHTS[i % N_MIXERS] if part == MIXER_PART else LAYER_COMMON


def _layer_slot(name, i):
    return i if name in LAYER_COMMON else i // N_MIXERS


def _merge_rows(chunks):
    n, r, c = chunks.shape
    return chunks.reshape(n * r, c)


def _merge_cols(chunks):
    n, r, c = chunks.shape
    return chunks.transpose(1, 0, 2).reshape(r, n * c)


def _pad_heads_out(wo):
    w3 = wo.reshape(HEADS, HEAD_DIM, D_MODEL)
    return jnp.pad(w3, ((0, 0), (HEAD_DIM, 0), (0, 0))).reshape(HEADS * LANE, D_MODEL)


def _part_to_compute(i, part, ch):
    if part == COMMON_PART:
        return dict(ffn_w_in=ch["ffn_w_in"], ffn_w_out=_merge_rows(ch["ffn_w_out"]), ple_w_proj=ch["ple_w_proj"],
                    ple_w_gate=_merge_rows(ch["ple_w_gate"]))
    lw = {}
    mixer = i % N_MIXERS
    if mixer == 0:
        wa = _merge_rows(ch["mla_w_a"])
        rank = MLA_Q_RANK + MLA_KV_RANK
        wa_p = jnp.concatenate([wa[:, :rank], jnp.zeros((wa.shape[0], 64), wa.dtype), wa[:, rank:],
                                jnp.zeros((wa.shape[0], 32), wa.dtype)], axis=1)
        wuq = _merge_cols(ch["mla_w_uq"]).reshape(MLA_Q_RANK, HEADS, HEAD_DIM + MLA_ROPE)
        wuq_p = jnp.pad(wuq, ((0, 0), (0, 0), (0, LANE - HEAD_DIM - MLA_ROPE))).reshape(MLA_Q_RANK, HEADS * LANE)
        lw["mixer"] = dict(w_a=wa_p, w_uq=wuq_p, w_ukv=ch["mla_w_ukv"], w_o=_pad_heads_out(_merge_rows(ch["mla_w_o"])))
    elif mixer == 1:
        lw["mixer"] = dict(w_qkv=ch["dil_w_qkv"], w_o=_merge_rows(ch["dil_w_o"]))
    else:
        wf = _merge_rows(ch["fox_w_qkvf"]).T
        inner = HEADS * HEAD_DIM
        q3 = wf[:, :inner].reshape(D_MODEL, HEADS, HEAD_DIM)
        k3 = wf[:, inner:2 * inner].reshape(D_MODEL, HEADS, HEAD_DIM)
        v3 = wf[:, 2 * inner:3 * inner].reshape(D_MODEL, HEADS, HEAD_DIM)
        q_p = jnp.pad(q3, ((0, 0), (0, 0), (0, HEAD_DIM))).reshape(D_MODEL, HEADS * LANE)
        kv_p = jnp.concatenate([k3, v3], axis=2).reshape(D_MODEL, HEADS * LANE)
        f_p = jnp.pad(wf[:, 3 * inner:], ((0, 0), (0, LANE - HEADS)))
        lw["mixer"] = dict(w_qkv=jnp.concatenate([q_p, kv_p], axis=1), w_f=f_p,
                           w_o=_pad_heads_out(_merge_rows(ch["fox_w_o"])))
    return lw["mixer"]


def _part_contributions(i, part, lg, chunk_shapes):
    spec = {k: jax.ShapeDtypeStruct(s, BF) for k, s in chunk_shapes.items()}
    (contrib,) = jax.linear_transpose(functools.partial(_part_to_compute, i, part), spec)(lg)
    return contrib


def _chip_peers():
    x, y, c = lax.axis_index("x"), lax.axis_index("y"), lax.axis_index("c")
    peers = [(1 - x, y), (x, 1 - y), (1 - x, 1 - y)]
    return x, y, c, peers


SEM_SPEC = pl.BlockSpec(memory_space=pltpu.SEMAPHORE)
ANY_SPEC = pl.BlockSpec(memory_space=pl.ANY)
SPLIT_EFFECT = pltpu.SideEffectType.DATAFLOW_SIDE_EFFECTING


def _own_slot(shard):
    me = 2 * lax.axis_index("x") + lax.axis_index("y")
    return lax.dynamic_update_index_in_dim(lax.empty((N_CHIPS,) + shard.shape, shard.dtype), shard[None], me, 0)


def _spread_copy(src, land, k, peer, c, send_sems, recv_sems, index, src_slot, slot):
    px, py = peer
    return pltpu.make_async_remote_copy(
        src_ref=src.at[src_slot], dst_ref=land.at[slot],
        send_sem=send_sems.at[3 * index + k], recv_sem=recv_sems.at[3 * index + k],
        device_id=(px, py, c), device_id_type=MESH)


def _spread_start(bufs, srcs, after, name):
    n = len(bufs)
    exchange = srcs is not None
    arrays = (list(srcs) if exchange else []) + list(bufs)
    na = len(arrays)

    def body(*refs):
        src, land = refs[:n], refs[na - n:na]
        send_sems, recv_sems = refs[na + 1], refs[na + 2]
        token = refs[-1]
        x, y, c, peers = _chip_peers()
        me = 2 * x + y
        for w in range(n):
            for k, peer in enumerate(peers):
                src_slot = 2 * peer[0] + peer[1] if exchange else me
                _spread_copy(src[w], land[w], k, peer, c, send_sems, recv_sems, w, src_slot, me).start()
        token[...] = jnp.zeros_like(token)

    hbm = [pltpu.with_memory_space_constraint(a, pltpu.HBM) for a in arrays]
    out = pl.pallas_call(
        body, name=name,
        out_shape=(pltpu.SemaphoreType.DMA((3 * n,)), pltpu.SemaphoreType.DMA((3 * n,)),
                   *[pltpu.HBM(a.shape, a.dtype) for a in hbm], jax.ShapeDtypeStruct((8, LANE), F32)),
        in_specs=[HBM_SPEC] * na + [ANY_SPEC],
        out_specs=(SEM_SPEC, SEM_SPEC, *[HBM_SPEC] * na, pl.BlockSpec(memory_space=pltpu.VMEM)),
        input_output_aliases={w: 2 + w for w in range(na)},
        compiler_params=pltpu.CompilerParams(has_side_effects=SPLIT_EFFECT))(*hbm, after)
    return dict(send=out[0], recv=out[1], arrays=out[2:2 + na], n=n, token=out[-1], exchange=exchange)


def _spread_wait(handle, after, name):
    n, exchange = handle["n"], handle["exchange"]
    arrays = list(handle["arrays"])
    na = len(arrays)

    def body(*refs):
        src, land = refs[:n], refs[na - n:na]
        send_sems, recv_sems = refs[na], refs[na + 1]
        x, y, c, peers = _chip_peers()
        me = 2 * x + y
        for w in range(n):
            for k, peer in enumerate(peers):
                there = 2 * peer[0] + peer[1]
                cp = _spread_copy(src[w], land[w], k, peer, c, send_sems, recv_sems, w, there if exchange else me, there)
                cp.wait_send()
                cp.wait_recv()

    out = pl.pallas_call(
        body, name=name, out_shape=tuple(pltpu.HBM(a.shape, a.dtype) for a in arrays),
        in_specs=[HBM_SPEC] * na + [SEM_SPEC, SEM_SPEC, ANY_SPEC], out_specs=tuple([HBM_SPEC] * na),
        input_output_aliases={w: w for w in range(na)},
        compiler_params=pltpu.CompilerParams(has_side_effects=SPLIT_EFFECT))(*arrays, handle["send"], handle["recv"], after)
    return (list(out[n:]), list(out[:n])) if exchange else list(out)


def _sibling_copy(received, sent, land, k, me, peers, sibling, send_sems, recv_sems, index):
    slot = me if k == 3 else 2 * peers[k][0] + peers[k][1]
    src = sent if k == 3 else received
    return pltpu.make_async_remote_copy(
        src_ref=src.at[slot], dst_ref=land.at[slot], send_sem=send_sems.at[4 * index + k],
        recv_sem=recv_sems.at[4 * index + k], device_id=sibling, device_id_type=MESH)


def _sibling_start(received, sent, after, name):
    n = len(received)
    lands = [lax.empty(a.shape, a.dtype) for a in received]
    arrays = list(received) + list(sent) + lands

    def body(*refs):
        rec, snt, land = refs[:n], refs[n:2 * n], refs[2 * n:3 * n]
        send_sems, recv_sems = refs[3 * n + 1], refs[3 * n + 2]
        token = refs[-1]
        x, y, c, peers = _chip_peers()
        for w in range(n):
            for k in range(4):
                _sibling_copy(rec[w], snt[w], land[w], k, 2 * x + y, peers, (x, y, 1 - c), send_sems, recv_sems, w).start()
        token[...] = jnp.zeros_like(token)

    hbm = [pltpu.with_memory_space_constraint(a, pltpu.HBM) for a in arrays]
    out = pl.pallas_call(
        body, name=name,
        out_shape=(pltpu.SemaphoreType.DMA((4 * n,)), pltpu.SemaphoreType.DMA((4 * n,)),
                   *[pltpu.HBM(a.shape, a.dtype) for a in hbm], jax.ShapeDtypeStruct((8, LANE), F32)),
        in_specs=[HBM_SPEC] * (3 * n) + [ANY_SPEC],
        out_specs=(SEM_SPEC, SEM_SPEC, *[HBM_SPEC] * (3 * n), pl.BlockSpec(memory_space=pltpu.VMEM)),
        input_output_aliases={w: 2 + w for w in range(3 * n)},
        compiler_params=pltpu.CompilerParams(has_side_effects=SPLIT_EFFECT))(*hbm, after)
    return dict(send=out[0], recv=out[1], arrays=out[2:2 + 3 * n], n=n, token=out[-1])


def _sibling_wait(handle, after, name):
    n = handle["n"]
    arrays = list(handle["arrays"])

    def body(*refs):
        rec, snt, land = refs[:n], refs[n:2 * n], refs[2 * n:3 * n]
        send_sems, recv_sems = refs[3 * n], refs[3 * n + 1]
        x, y, c, peers = _chip_peers()
        for w in range(n):
            for k in range(4):
                cp = _sibling_copy(rec[w], snt[w], land[w], k, 2 * x + y, peers, (x, y, 1 - c), send_sems, recv_sems, w)
                cp.wait_send()
                cp.wait_recv()

    out = pl.pallas_call(
        body, name=name, out_shape=tuple(pltpu.HBM(a.shape, a.dtype) for a in arrays),
        in_specs=[HBM_SPEC] * (3 * n) + [SEM_SPEC, SEM_SPEC, ANY_SPEC], out_specs=tuple([HBM_SPEC] * (3 * n)),
        input_output_aliases={w: w for w in range(3 * n)},
        compiler_params=pltpu.CompilerParams(has_side_effects=SPLIT_EFFECT))(*arrays, handle["send"], handle["recv"], after)
    return list(out[:n]), list(out[n:2 * n]), list(out[2 * n:])


def _all_reduce_small(v):
    rows = v.shape[0]

    def body(v_ref, sum_ref, slots, send_sems, recv_sems):
        x, y, c = lax.axis_index("x"), lax.axis_index("y"), lax.axis_index("c")
        me = 4 * x + 2 * y + c
        slots[me] = v_ref[...]
        sends = []
        for k in range(1, N_DEV):
            bx, by, bc = (k >> 2) & 1, (k >> 1) & 1, k & 1
            peer = (x ^ bx, y ^ by, c ^ bc)
            rc = pltpu.make_async_remote_copy(src_ref=v_ref, dst_ref=slots.at[me], send_sem=send_sems.at[k],
                                              recv_sem=recv_sems.at[k], device_id=peer, device_id_type=MESH)
            rc.start()
            sends.append(rc)
        for k in range(1, N_DEV):
            bx, by, bc = (k >> 2) & 1, (k >> 1) & 1, k & 1
            src = 4 * (x ^ bx) + 2 * (y ^ by) + (c ^ bc)
            pltpu.make_async_remote_copy(src_ref=v_ref, dst_ref=slots.at[src], send_sem=send_sems.at[k],
                                         recv_sem=recv_sems.at[k], device_id=(x ^ bx, y ^ by, c ^ bc),
                                         device_id_type=MESH).wait_recv()
        for rc in sends:
            rc.wait_send()
        total = slots[0]
        for k in range(1, N_DEV):
            total = total + slots[k]
        sum_ref[...] = total

    vm = pl.BlockSpec(memory_space=pltpu.VMEM)
    return pl.pallas_call(
        body, out_shape=jax.ShapeDtypeStruct((rows, LANE), F32), in_specs=[vm], out_specs=vm,
        scratch_shapes=[pltpu.VMEM((N_DEV, rows, LANE), F32), pltpu.SemaphoreType.DMA((N_DEV,)),
                        pltpu.SemaphoreType.DMA((N_DEV,))], name="all_reduce_small")(v)


def _as_2d(a):
    return a.reshape(-1, a.shape[-1])


def _row_tile(rows, cols):
    for t in (512, 256, 128, 64, 32, 16):
        if rows % t == 0 and t * cols * 4 <= (1 << 20):
            return t
    return rows


def _adamw_weight(w, m, v, received, sent, sibling):
    layers = len(received)
    _, rows, cols = received[0].shape
    tr = _row_tile(rows, cols)
    by_columns = rows % tr != 0 or tr == rows and rows * cols * 4 > (2 << 20)
    if by_columns:
        assert layers == 1 and cols % (2 * LANE) == 0, (w.shape, received[0].shape)
        tr, tc, steps = rows, cols // 2, 2
        index = lambda i: (0, i)
    else:
        tc, steps = cols, rows // tr
        index = lambda i: (i, 0)
    where = (2 * lax.axis_index("x") + lax.axis_index("y")).astype(jnp.int32).reshape(1)

    def body(where_ref, w_ref, m_ref, v_ref, *rest):
        per_layer, (g_ref, d_ref, nm_ref, nv_ref) = rest[:3 * layers], rest[3 * layers:]
        me = where_ref[0]
        for layer in range(layers):
            r_ref, own_ref, s_ref = per_layer[3 * layer:3 * layer + 3]

            @pl.when(pl.program_id(0) == layer)
            def _():
                mine = theirs = None
                for k in range(N_CHIPS):
                    a = jnp.where(me == k, own_ref[...], r_ref[k]).astype(F32)
                    b = s_ref[k].astype(F32)
                    mine = a if mine is None else mine + a
                    theirs = b if theirs is None else theirs + b
                g = mine + theirs
                delta, nm, nv = _adamw_math(w_ref[...], g, m_ref[...], v_ref[...])
                g_ref[...] = g
                d_ref[...] = delta
                nm_ref[...] = nm
                nv_ref[...] = nv

    def held(layer, now, i):
        return jnp.where(now < layer, 0, jnp.where(now > layer, steps - 1, i))

    if by_columns:
        stacked = pl.BlockSpec((tr, tc), lambda now, i, where_ref: index(i))
    else:
        stacked = pl.BlockSpec((tr, tc), lambda now, i, where_ref: (now * steps + i, 0))
    in_specs = [stacked, stacked, stacked]
    args = [where, w, m, v]
    for layer in range(layers):
        four = pl.BlockSpec((N_CHIPS, tr, tc), lambda now, i, where_ref, layer=layer: (0,) + index(held(layer, now, i)))
        own = pl.BlockSpec((None, tr, tc),
                           lambda now, i, where_ref, layer=layer: (where_ref[0],) + index(held(layer, now, i)))
        in_specs += [four, own, four]
        args += [received[layer], sent[layer], sibling[layer]]
    grid_spec = pltpu.PrefetchScalarGridSpec(num_scalar_prefetch=1, grid=(layers, steps), in_specs=in_specs,
                                             out_specs=[stacked] * 4)
    return pl.pallas_call(body, out_shape=[jax.ShapeDtypeStruct(w.shape, F32)] * 4, grid_spec=grid_spec,
                          name="adamw_weight", compiler_params=_params(("arbitrary", "arbitrary")))(*args)


def _adamw_math(w, g, m, v):
    m = ADAM_B1 * m + (1.0 - ADAM_B1) * g
    v = ADAM_B2 * v + (1.0 - ADAM_B2) * (g * g)
    m_hat = m * (1.0 / (1.0 - ADAM_B1 ** ADAM_STEP))
    v_hat = v * (1.0 / (1.0 - ADAM_B2 ** ADAM_STEP))
    denom = jnp.sqrt(v_hat) + ADAM_EPS
    inv = pl.reciprocal(denom, approx=True)
    inv = inv * (2.0 - denom * inv)
    delta = -ADAM_LR * (m_hat * inv + ADAM_WD * w)
    return delta, m, v


def _adamw(w, m, v, g_mine, g_sibling):
    rows, cols = w.shape
    tr = _row_tile(rows, cols)
    two = g_sibling is not None

    def body(*refs):
        if two:
            w_ref, m_ref, v_ref, ga_ref, gb_ref, g_ref, d_ref, nm_ref, nv_ref = refs
            g = ga_ref[...] + gb_ref[...]
        else:
            w_ref, m_ref, v_ref, ga_ref, g_ref, d_ref, nm_ref, nv_ref = refs
            g = ga_ref[...]
        delta, nm, nv = _adamw_math(w_ref[...], g, m_ref[...], v_ref[...])
        g_ref[...] = g
        d_ref[...] = delta
        nm_ref[...] = nm
        nv_ref[...] = nv

    blk = pl.BlockSpec((tr, cols), lambda i: (i, 0))
    args = [w, m, v, g_mine] + ([g_sibling] if two else [])
    return pl.pallas_call(body, out_shape=[jax.ShapeDtypeStruct((rows, cols), F32)] * 4, grid=(rows // tr,),
                          in_specs=[blk] * len(args), out_specs=[blk] * 4, name="adamw",
                          compiler_params=_params(("parallel",)))(*args)


def _pack_rows(arrays):
    flat = jnp.concatenate([a.reshape(-1) for a in arrays])
    rows = -(-flat.shape[0] // (8 * LANE)) * 8
    return jnp.pad(flat, (0, rows * LANE - flat.shape[0])).reshape(rows, LANE)


def _unpack_rows(packed, shapes):
    flat = packed.reshape(-1)
    out, at = [], 0
    for s in shapes:
        size = math.prod(s)
        out.append(flat[at:at + size].reshape(s))
        at += size
    return out


def kernel(x, p, positions, norm_g, ffn_w_in, ffn_w_out, ple_w_proj, ple_w_gate, rel_bias, mla_w_a, mla_q_norm, mla_kv_norm, mla_w_uq, mla_w_ukv, mla_w_o, dil_w_qkv, dil_w_o, fox_w_qkvf, fox_b_f, fox_w_o, loss_target, m_norm_g, m_ffn_w_in, m_ffn_w_out, m_ple_w_proj, m_ple_w_gate, m_rel_bias, m_mla_w_a, m_mla_q_norm, m_mla_kv_norm, m_mla_w_uq, m_mla_w_ukv, m_mla_w_o, m_dil_w_qkv, m_dil_w_o, m_fox_w_qkvf, m_fox_b_f, m_fox_w_o, v_norm_g, v_ffn_w_in, v_ffn_w_out, v_ple_w_proj, v_ple_w_gate, v_rel_bias, v_mla_w_a, v_mla_q_norm, v_mla_kv_norm, v_mla_w_uq, v_mla_w_ukv, v_mla_w_o, v_dil_w_qkv, v_dil_w_o, v_fox_w_qkvf, v_fox_b_f, v_fox_w_o):
    w = dict(norm_g=norm_g, ffn_w_in=ffn_w_in, ffn_w_out=ffn_w_out, ple_w_proj=ple_w_proj, ple_w_gate=ple_w_gate,
             rel_bias=rel_bias, mla_w_a=mla_w_a, mla_q_norm=mla_q_norm, mla_kv_norm=mla_kv_norm, mla_w_uq=mla_w_uq,
             mla_w_ukv=mla_w_ukv, mla_w_o=mla_w_o, dil_w_qkv=dil_w_qkv, dil_w_o=dil_w_o, fox_w_qkvf=fox_w_qkvf,
             fox_b_f=fox_b_f, fox_w_o=fox_w_o)
    m = dict(norm_g=m_norm_g, ffn_w_in=m_ffn_w_in, ffn_w_out=m_ffn_w_out, ple_w_proj=m_ple_w_proj,
             ple_w_gate=m_ple_w_gate, rel_bias=m_rel_bias, mla_w_a=m_mla_w_a, mla_q_norm=m_mla_q_norm,
             mla_kv_norm=m_mla_kv_norm, mla_w_uq=m_mla_w_uq, mla_w_ukv=m_mla_w_ukv, mla_w_o=m_mla_w_o,
             dil_w_qkv=m_dil_w_qkv, dil_w_o=m_dil_w_o, fox_w_qkvf=m_fox_w_qkvf, fox_b_f=m_fox_b_f, fox_w_o=m_fox_w_o)
    v = dict(norm_g=v_norm_g, ffn_w_in=v_ffn_w_in, ffn_w_out=v_ffn_w_out, ple_w_proj=v_ple_w_proj,
             ple_w_gate=v_ple_w_gate, rel_bias=v_rel_bias, mla_w_a=v_mla_w_a, mla_q_norm=v_mla_q_norm,
             mla_kv_norm=v_mla_kv_norm, mla_w_uq=v_mla_w_uq, mla_w_ukv=v_mla_w_ukv, mla_w_o=v_mla_w_o,
             dil_w_qkv=v_dil_w_qkv, dil_w_o=v_dil_w_o, fox_w_qkvf=v_fox_w_qkvf, fox_b_f=v_fox_b_f, fox_w_o=v_fox_w_o)
    chip = 2 * lax.axis_index("x") + lax.axis_index("y")
    for tree in (w, m, v):
        tree[TRANSPOSED] = jnp.swapaxes(tree[TRANSPOSED], 1, 2)

    small_shapes = [w[k].shape for k in SMALL_SHARDED]
    order = [(i, part) for i in range(DEPTH) for part in (MIXER_PART, COMMON_PART)]
    gathers = {}
    after = positions
    zero = 0.0
    for i, part in order:
        bufs = [_own_slot((w[k][_layer_slot(k, i)] + zero).astype(BF)) for k in _part_names(i, part)]
        if (i, part) == order[0]:
            bufs.append(_own_slot(_pack_rows([w[k] for k in SMALL_SHARDED])))
        gathers[i, part] = _spread_start(bufs, None, after, f"gather_start_{i}_{part}")
        after = gathers[i, part]["token"]
        if (i, part) == order[0]:
            zero = after[0, 0]
    all_started = after
    state = {}

    def get_part(i, part, after_array):
        is_first = (i, part) == order[0]
        lands = _spread_wait(gathers[i, part], all_started if is_first else after_array, f"gather_wait_{i}_{part}")
        if is_first:
            pieces = [_unpack_rows(lands[-1][k], small_shapes) for k in range(N_CHIPS)]
            small = {name: jnp.concatenate([pieces[k][idx] for k in range(N_CHIPS)], axis=-1)
                     for idx, name in enumerate(SMALL_SHARDED)}
            state["small"] = dict(small, rel_bias=rel_bias, fox_b_f=fox_b_f)
        chunks = dict(zip(_part_names(i, part), lands))
        state[i, part] = {k: a.shape for k, a in chunks.items()}
        return _part_to_compute(i, part, chunks)

    started, forwards = [], {}

    def forward_oldest(after_array):
        i, part, handle = started.pop(0)
        received, sent = _spread_wait(handle, after_array, f"exchange_wait_{i}_{part}")
        forwards[i, part] = _sibling_start(received, sent, after_array, f"sibling_start_{i}_{part}")
        return forwards[i, part]["token"]

    def put_part(i, part, lg):
        contrib = _part_contributions(i, part, lg, state[i, part])
        srcs = [contrib[k] for k in _part_names(i, part)]
        handle = _spread_start([lax.empty(s.shape, s.dtype) for s in srcs], srcs, positions,
                               f"exchange_start_{i}_{part}")
        token = handle["token"]
        if started:
            token = token + forward_oldest(token)
        started.append((i, part, handle))
        return token

    sq, grad_x, sg = _run_layers(x[0], p[:, 0], positions[0], loss_target[0], get_part, lambda: state["small"],
                                 put_part)
    loss = lax.psum(0.5 / D_MODEL * jnp.sum(sq), ("x", "y", "c"))
    forward_oldest(grad_x)

    held = {k: {} for k in BIG}
    for i, part in [(i, part) for i in reversed(range(DEPTH)) for part in (COMMON_PART, MIXER_PART)]:
        received, sent, sibling = _sibling_wait(forwards[i, part], grad_x, f"sibling_wait_{i}_{part}")
        for k, r, s, t in zip(_part_names(i, part), received, sent, sibling):
            held[k][_layer_slot(k, i)] = (r, s, t)
    results = {}
    for k in BIG:
        per_layer = [held[k][slot] for slot in sorted(held[k])]
        outs = _adamw_weight(_as_2d(w[k]), _as_2d(m[k]), _as_2d(v[k]), *[list(col) for col in zip(*per_layer)])
        results[k] = [o.reshape(w[k].shape) for o in outs]
    results[TRANSPOSED] = [jnp.swapaxes(o, 1, 2) for o in results[TRANSPOSED]]

    small_all = SMALL_SHARDED + SMALL_REPLICATED
    full_shapes = [sg[k].shape for k in small_all]
    reduced = dict(zip(small_all, _unpack_rows(_all_reduce_small(_pack_rows([sg[k] for k in small_all])), full_shapes)))
    local_g = []
    for k in small_all:
        g = reduced[k]
        if k in SMALL_SHARDED:
            width = w[k].shape[-1]
            g = lax.dynamic_slice_in_dim(g, chip * width, width, axis=g.ndim - 1)
        local_g.append(g)
    local_shapes = [w[k].shape for k in small_all]
    outs = _adamw(_pack_rows([w[k] for k in small_all]), _pack_rows([m[k] for k in small_all]),
                  _pack_rows([v[k] for k in small_all]), _pack_rows(local_g), None)
    unpacked = [_unpack_rows(o, local_shapes) for o in outs]
    for idx, k in enumerate(small_all):
        results[k] = [u[idx] for u in unpacked]

    return (loss, grad_x[None], *[results[k][0] for k in WEIGHTS], *[results[k][1] for k in WEIGHTS],
            *[results[k][2] for k in WEIGHTS], *[results[k][3] for k in WEIGHTS])
```

```python
import functools
import math

import jax
import jax.numpy as jnp
from jax import lax
from jax.experimental import pallas as pl
from jax.experimental.pallas import tpu as pltpu

F32 = jnp.float32
BF = jnp.bfloat16
MESH = pl.DeviceIdType.MESH
HBM_SPEC = pl.BlockSpec(memory_space=pltpu.HBM)

D_MODEL = 1024
DEPTH = 4
N_MIXERS = 3
D_FF = 2816
NORM_EPS = 1e-6
NEG_INF = -1e30
LANE = 128
HEADS = 16
HEAD_DIM = 64
MLA_Q_RANK = 384
MLA_KV_RANK = 256
MLA_ROPE = 32
MLA_A_PAD = 768
ROPE_THETA = 10000.0
DIL_PATTERNS = ((128, 1), (512, 4), (2048, 16))
Q_BLOCK = 128
DIL_PAIRS = 2
REL_BUCKETS = 32
REL_MAX_DIST = 2048
N_CHIPS = 4
N_DEV = 8

ADAM_LR = 0.001
ADAM_B1 = 0.9
ADAM_B2 = 0.999
ADAM_EPS = 1e-08
ADAM_WD = 0.01
ADAM_STEP = 10

VMEM_LIMIT = 56 * 1024 * 1024
MATMUL_VMEM_BUDGET = 36 * 1024 * 1024
ROW_TILE = 512
ATTN_TILE = 256
ATTN_Q_TILE = 512
MLA_GROUP = 4
FOX_GROUP = 4


def _params(sem=None):
    return pltpu.CompilerParams(dimension_semantics=sem, vmem_limit_bytes=VMEM_LIMIT)


def _divisor_tiles(dim):
    tiles = [t for t in range(LANE, dim + 1, LANE) if dim % t == 0]
    return tiles or [dim]


def _matmul_tiles(m, n, k, a_bytes, b_bytes, out_bytes, has_add, n_unit=None, k_unit=None):
    best = None
    for tm in _divisor_tiles(m):
        for tn in _divisor_tiles(n_unit or n):
            for tk in _divisor_tiles(k_unit or k):
                if max(tm, tn, tk) > 2048:
                    continue
                vmem = 2 * (tm * tk * a_bytes + tk * tn * b_bytes + tm * tn * out_bytes) + tm * tn * 4
                if has_add:
                    vmem += 2 * tm * tn * 4
                if vmem > MATMUL_VMEM_BUDGET:
                    continue
                steps = (m // tm) * (n // tn) * (k // tk)
                traffic = m * k * a_bytes * (n // tn) + k * n * b_bytes * (m // tm) + m * n * out_bytes
                cost = traffic / 3.0e12 + steps * 0.4e-6
                if best is None or cost < best[0]:
                    best = (cost, tm, tn, tk)
    return best[1:]


def _matmul(a, b, *, ta=False, tb=False, b_chunks=False, out_chunks=False, add=None, out_dtype=F32, name):
    k, m = a.shape if ta else a.shape[::-1]
    n_unit = k_unit = None
    if b_chunks:
        chunks, rows_w, c = b.shape
        if tb:
            kb, n, k_unit = chunks * c, rows_w, c
        else:
            kb, n, n_unit = rows_w, chunks * c, c
    else:
        kb, n = b.shape[::-1] if tb else b.shape
    if out_chunks:
        assert n % N_CHIPS == 0 and add is None
        n_unit = n // N_CHIPS
    assert k == kb, (a.shape, b.shape, ta, tb)
    tm, tn, tk = _matmul_tiles(m, n, k, a.dtype.itemsize, b.dtype.itemsize, jnp.dtype(out_dtype).itemsize,
                               add is not None, n_unit, k_unit)
    nk = k // tk
    dims = (((0 if ta else 1,), (1 if tb else 0,)), ((), ()))

    def body(*refs):
        if add is None:
            a_ref, b_ref, o_ref, acc_ref = refs
            add_ref = None
        else:
            a_ref, b_ref, add_ref, o_ref, acc_ref = refs
        kk = pl.program_id(2)

        @pl.when(kk == 0)
        def _():
            acc_ref[...] = jnp.zeros_like(acc_ref)

        acc_ref[...] += lax.dot_general(a_ref[...].astype(BF), b_ref[...].astype(BF), dims,
                                        preferred_element_type=F32)

        @pl.when(kk == nk - 1)
        def _():
            r = acc_ref[...]
            if add_ref is not None:
                r = r + add_ref[...].astype(F32)
            o_ref[...] = r.astype(out_dtype)

    a_spec = pl.BlockSpec((tk, tm), lambda i, j, q: (q, i)) if ta else pl.BlockSpec((tm, tk), lambda i, j, q: (i, q))
    if b_chunks and tb:
        per_k = k_unit // tk
        b_spec = pl.BlockSpec((None, tn, tk), lambda i, j, q: (q // per_k, j, q % per_k))
    elif b_chunks:
        per_n = n_unit // tn
        b_spec = pl.BlockSpec((None, tk, tn), lambda i, j, q: (j // per_n, q, j % per_n))
    elif tb:
        b_spec = pl.BlockSpec((tn, tk), lambda i, j, q: (j, q))
    else:
        b_spec = pl.BlockSpec((tk, tn), lambda i, j, q: (q, j))
    if out_chunks:
        per_o = n_unit // tn
        o_spec = pl.BlockSpec((None, tm, tn), lambda i, j, q: (j // per_o, i, j % per_o))
        out_shape = jax.ShapeDtypeStruct((N_CHIPS, m, n_unit), out_dtype)
    else:
        o_spec = pl.BlockSpec((tm, tn), lambda i, j, q: (i, j))
        out_shape = jax.ShapeDtypeStruct((m, n), out_dtype)
    in_specs = [a_spec, b_spec]
    args = [a, b]
    if add is not None:
        in_specs.append(o_spec)
        args.append(add)
    return pl.pallas_call(
        body, out_shape=out_shape, grid=(m // tm, n // tn, nk),
        in_specs=in_specs, out_specs=o_spec, scratch_shapes=[pltpu.VMEM((tm, tn), F32)], name=name,
        compiler_params=_params(("parallel", "parallel", "arbitrary")))(*args)


def _rowwise(body, name, rows, ins, outs, tr=ROW_TILE):
    def row_spec(cols):
        return pl.BlockSpec((tr, cols), lambda i: (i, 0))

    def full_spec(shape):
        zeros = (0,) * len(shape)
        return pl.BlockSpec(shape, lambda i: zeros)

    in_specs = [row_spec(a.shape[1]) if kind == "row" else full_spec(a.shape) for a, kind in ins]
    out_specs = [row_spec(shape[1]) if kind == "row" else full_spec(shape) for shape, _, kind in outs]
    out_shape = [jax.ShapeDtypeStruct(shape, dtype) for shape, dtype, _ in outs]
    return pl.pallas_call(body, out_shape=out_shape, grid=(rows // tr,), in_specs=in_specs, out_specs=out_specs,
                          name=name, compiler_params=_params(("arbitrary",)))(*[a for a, _ in ins])


def _rstd(x):
    return lax.rsqrt(jnp.mean(x * x, axis=-1, keepdims=True) + NORM_EPS)


def _rms_bwd_math(x, g, dy):
    r = _rstd(x)
    gd = dy * g
    dx = r * gd - x * (r * r * r) * jnp.mean(gd * x, axis=-1, keepdims=True)
    dg = jnp.sum(dy * x * r, axis=0, keepdims=True)
    return dx, dg


def _sigmoid(x):
    return 0.5 * jnp.tanh(0.5 * x) + 0.5


def _init_acc(*refs):
    @pl.when(pl.program_id(0) == 0)
    def _():
        for r in refs:
            r[...] = jnp.zeros_like(r)


def _prenorm(h, g):
    rows, cols = h.shape

    def body(h_ref, g_ref, o_ref):
        x = h_ref[...]
        o_ref[...] = (x * _rstd(x) * g_ref[...]).astype(BF)

    return _rowwise(body, "prenorm", rows, [(h, "row"), (g, "full")], [((rows, cols), BF, "row")])[0]


def _post_residual(h, y, g_post, g_pre):
    rows, cols = h.shape
    with_pre = g_pre is not None

    def body(*refs):
        if with_pre:
            h_ref, y_ref, gp_ref, gq_ref, hn_ref, hb_ref = refs
        else:
            h_ref, y_ref, gp_ref, hn_ref, hb_ref = refs
        yv = y_ref[...]
        hn = h_ref[...] + yv * _rstd(yv) * gp_ref[...]
        hn_ref[...] = hn
        hb_ref[...] = (hn * _rstd(hn) * gq_ref[...] if with_pre else hn).astype(BF)

    ins = [(h, "row"), (y, "row"), (g_post, "full")] + ([(g_pre, "full")] if with_pre else [])
    return _rowwise(body, "post_residual_pre" if with_pre else "post_residual", rows, ins,
                    [((rows, cols), F32, "row"), ((rows, cols), BF, "row")])


def _ple_forward(h2, pp, z, g_pre):
    rows, cols = h2.shape

    def body(h_ref, p_ref, z_ref, g_ref, h3_ref, hb_ref):
        h3 = h_ref[...] + p_ref[...] * _sigmoid(z_ref[...])
        h3_ref[...] = h3
        hb_ref[...] = (h3 * _rstd(h3) * g_ref[...]).astype(BF)

    return _rowwise(body, "ple_forward", rows, [(h2, "row"), (pp, "row"), (z, "row"), (g_pre, "full")],
                    [((rows, cols), F32, "row"), ((rows, cols), BF, "row")])


def _ple_loss(h2, pp, z, target):
    rows, cols = h2.shape

    def body(h_ref, p_ref, z_ref, t_ref, dh_ref, sq_ref):
        _init_acc(sq_ref)
        err = h_ref[...] + p_ref[...] * _sigmoid(z_ref[...]) - t_ref[...]
        dh_ref[...] = err * (1.0 / cols)
        sq_ref[...] += jnp.sum(err * err, axis=0, keepdims=True)

    return _rowwise(body, "ple_loss", rows, [(h2, "row"), (pp, "row"), (z, "row"), (target, "row")],
                    [((rows, cols), F32, "row"), ((1, cols), F32, "acc")])


def _ple_backward(dh3, pp, z):
    rows, cols = dh3.shape

    def body(d_ref, p_ref, z_ref, dpp_ref, dz_ref):
        d = d_ref[...]
        s = _sigmoid(z_ref[...])
        dpp_ref[...] = (d * s).astype(BF)
        dz_ref[...] = (d * p_ref[...] * s * (1.0 - s)).astype(BF)

    return _rowwise(body, "ple_backward", rows, [(dh3, "row"), (pp, "row"), (z, "row")],
                    [((rows, cols), BF, "row"), ((rows, cols), BF, "row")])


def _rms_backward(x, g, dy, add, out_dtype):
    rows, cols = x.shape
    with_add = add is not None

    def body(*refs):
        if with_add:
            x_ref, g_ref, dy_ref, add_ref, dx_ref, dg_ref = refs
        else:
            x_ref, g_ref, dy_ref, dx_ref, dg_ref = refs
        _init_acc(dg_ref)
        dx, dg = _rms_bwd_math(x_ref[...], g_ref[...], dy_ref[...].astype(F32))
        if with_add:
            dx = dx + add_ref[...]
        dx_ref[...] = dx.astype(out_dtype)
        dg_ref[...] += dg

    ins = [(x, "row"), (g, "full"), (dy, "row")] + ([(add, "row")] if with_add else [])
    return _rowwise(body, "rms_backward_add" if with_add else "rms_backward", rows, ins,
                    [((rows, cols), out_dtype, "row"), ((1, cols), F32, "acc")])


def _swiglu_forward(gu):
    rows = gu.shape[0]

    def body(gu_ref, o_ref):
        g = gu_ref[:, :D_FF].astype(F32)
        o_ref[...] = (g * _sigmoid(g) * gu_ref[:, D_FF:].astype(F32)).astype(BF)

    return _rowwise(body, "swiglu_forward", rows, [(gu, "row")], [((rows, D_FF), BF, "row")])[0]


def _swiglu_backward(gu, dact):
    rows = gu.shape[0]

    def body(gu_ref, d_ref, o_ref):
        g = gu_ref[:, :D_FF].astype(F32)
        u = gu_ref[:, D_FF:].astype(F32)
        d = d_ref[...].astype(F32)
        s = _sigmoid(g)
        gs = g * s
        o_ref[:, :D_FF] = (d * u * (s + gs * (1.0 - s))).astype(BF)
        o_ref[:, D_FF:] = (d * gs).astype(BF)

    return _rowwise(body, "swiglu_backward", rows, [(gu, "row"), (dact, "row")], [((rows, 2 * D_FF), BF, "row")])[0]


def _rope_tables(positions):
    half = MLA_ROPE // 2
    inv = ROPE_THETA ** (-jnp.arange(half, dtype=F32) / half)
    ang = positions.astype(F32)[:, None] * inv
    cos, sin = jnp.cos(ang), jnp.sin(ang)
    rows = positions.shape[0]
    c = jnp.ones((rows, LANE), F32).at[:, 64:80].set(cos).at[:, 80:96].set(cos)
    sa = jnp.zeros((rows, LANE), F32).at[:, 64:80].set(-sin)
    sb = jnp.zeros((rows, LANE), F32).at[:, 80:96].set(sin)
    return c, sa, sb


def _rope_apply(x, c, sa, sb):
    return x * c + pltpu.roll(x, LANE - 16, 1) * sa + pltpu.roll(x, 16, 1) * sb


def _rope_apply_t(dy, c, sa, sb):
    return dy * c + pltpu.roll(dy * sa, 16, 1) + pltpu.roll(dy * sb, LANE - 16, 1)


def _rope_heads(x, tables, transpose, name):
    rows, cols = x.shape

    def body(x_ref, c_ref, sa_ref, sb_ref, o_ref):
        fn = _rope_apply_t if transpose else _rope_apply
        c, sa, sb = c_ref[...], sa_ref[...], sb_ref[...]
        for head in range(cols // LANE):
            lanes = slice(head * LANE, (head + 1) * LANE)
            o_ref[:, lanes] = fn(x_ref[:, lanes].astype(F32), c, sa, sb).astype(BF)

    blk = pl.BlockSpec((ROW_TILE, cols), lambda i: (i, 0))
    tbl = pl.BlockSpec((ROW_TILE, LANE), lambda i: (i, 0))
    return pl.pallas_call(body, out_shape=jax.ShapeDtypeStruct((rows, cols), BF), grid=(rows // ROW_TILE,),
                          in_specs=[blk, tbl, tbl, tbl], out_specs=blk, name=name,
                          compiler_params=_params(("parallel",)))(x, *tables)


def _mla_mid_forward(a, q_norm, kv_norm, tables):
    rows = a.shape[0]
    qr, kvr = MLA_Q_RANK, MLA_KV_RANK

    def body(a_ref, qn_ref, kn_ref, c_ref, sa_ref, sb_ref, cq_ref, ckv_ref, kr_ref):
        aq = a_ref[:, 0:qr]
        akv = a_ref[:, qr:qr + kvr]
        cq_ref[...] = (aq * _rstd(aq) * qn_ref[...]).astype(BF)
        ckv_ref[...] = (akv * _rstd(akv) * kn_ref[...]).astype(BF)
        kr_ref[...] = _rope_apply(a_ref[:, qr + kvr:], c_ref[...], sa_ref[...], sb_ref[...]).astype(BF)

    ins = [(a, "row"), (q_norm, "full"), (kv_norm, "full")] + [(t, "row") for t in tables]
    return _rowwise(body, "mla_mid_forward", rows, ins,
                    [((rows, qr), BF, "row"), ((rows, kvr), BF, "row"), ((rows, LANE), BF, "row")])


def _mla_mid_backward(a, q_norm, kv_norm, tables, dcq, dckv, dkr):
    rows = a.shape[0]
    qr, kvr = MLA_Q_RANK, MLA_KV_RANK

    def body(a_ref, qn_ref, kn_ref, c_ref, sa_ref, sb_ref, dcq_ref, dckv_ref, dkr_ref, da_ref, dqn_ref, dkn_ref):
        _init_acc(dqn_ref, dkn_ref)
        dxq, dgq = _rms_bwd_math(a_ref[:, 0:qr], qn_ref[...], dcq_ref[...])
        dxk, dgk = _rms_bwd_math(a_ref[:, qr:qr + kvr], kn_ref[...], dckv_ref[...])
        da_ref[:, 0:qr] = dxq.astype(BF)
        da_ref[:, qr:qr + kvr] = dxk.astype(BF)
        da_ref[:, qr + kvr:] = _rope_apply_t(dkr_ref[...], c_ref[...], sa_ref[...], sb_ref[...]).astype(BF)
        dqn_ref[...] += dgq
        dkn_ref[...] += dgk

    ins = ([(a, "row"), (q_norm, "full"), (kv_norm, "full")] + [(t, "row") for t in tables]
           + [(dcq, "row"), (dckv, "row"), (dkr, "row")])
    return _rowwise(body, "mla_mid_backward", rows, ins,
                    [((rows, MLA_A_PAD), BF, "row"), ((1, qr), F32, "acc"), ((1, kvr), F32, "acc")])


def _attn_specs(rows, kv_off, g, many_row_vectors):
    head =pl.BlockSpec((rows, g * LANE), lambda h: (0, h))
    kv_head = pl.BlockSpec((rows, g * LANE), lambda h: (0, h + kv_off // g))
    shared = pl.BlockSpec((rows, LANE), lambda h: (0, 0))
    col_vec = pl.BlockSpec((g, rows, 1), lambda h: (h, 0, 0),
                           pipeline_mode=pl.Buffered(1 if many_row_vectors else 2))
    row_vec = pl.BlockSpec((g, 1, rows), lambda h: (h, 0, 0))
    return head, kv_head, shared, col_vec, row_vec


def _attn_forward(q, kv, kv_off, kr, cum_col, cum_row, scale, group_size, name):
    rows = q.shape[0]
    heads = HEADS
    t = ATTN_TILE
    tq = ATTN_Q_TILE
    per = tq // t
    has_kr = kr is not None
    has_f = cum_col is not None
    group = range(group_size)

    def body(*refs):
        it = iter(refs)
        q_ref, kv_ref = next(it), next(it)
        kr_ref = next(it) if has_kr else None
        cc_ref = next(it) if has_f else None
        cr_ref = next(it) if has_f else None
        o_ref, lse_ref = next(it), next(it)
        lo = lax.broadcasted_iota(jnp.int32, (1, LANE), 1) < HEAD_DIM
        row = lax.broadcasted_iota(jnp.int32, (tq, t), 0)
        col = lax.broadcasted_iota(jnp.int32, (tq, t), 1)
        lanes = [slice(g * LANE, (g + 1) * LANE) for g in group]

        def q_block(i, _):
            qs = pl.ds(pl.multiple_of(i * tq, tq), tq)
            qbs = [q_ref[qs, lanes[g]] for g in group]
            cqs = [cc_ref[g, qs, :] if has_f else None for g in group]

            def step(j, carry, diag):
                ks = pl.ds(pl.multiple_of(j * t, t), t)
                other = kr_ref[ks, :] if has_kr else jnp.zeros((t, LANE), BF)
                kvbs = [kv_ref[ks, lanes[g]] for g in group]

                def logit(g):
                    return lax.dot_general(qbs[g], jnp.where(lo, kvbs[g], other), (((1,), (1,)), ((), ())),
                                           preferred_element_type=F32)

                logits = {g: logit(g) for g in (group if has_f else group[:1])}
                out = []
                for g in group:
                    m, l, acc = carry[g]
                    if not has_f and g + 1 < len(group):
                        logits[g + 1] = logit(g + 1)
                    s = logits[g] * scale
                    if has_f:
                        s = s + (cqs[g] - cr_ref[g, :, ks])
                    if diag is not None:
                        s = jnp.where(col + diag * t <= row, s, NEG_INF)
                    mn = jnp.maximum(m, jnp.max(s, axis=1, keepdims=True))
                    alpha = jnp.exp(m - mn)
                    p = jnp.exp(s - mn)
                    l = alpha * l + jnp.sum(p, axis=1, keepdims=True)
                    acc = alpha * acc + jnp.dot(p.astype(BF), kvbs[g], preferred_element_type=F32)
                    out.append((mn, l, acc))
                return tuple(out)

            init = tuple((jnp.full((tq, 1), NEG_INF, F32), jnp.zeros((tq, 1), F32), jnp.zeros((tq, LANE), F32))
                         for _ in group)
            carry = lax.fori_loop(0, i * per, lambda j, c: step(j, c, None), init)
            for d in range(per):
                carry = step(i * per + d, carry, d)
            for g, (m, l, acc) in enumerate(carry):
                o_ref[qs, lanes[g]] = jnp.where(lo, 0.0, acc * (1.0 / l)).astype(BF)
                lse_ref[g, qs, :] = m + jnp.log(l)
            return 0

        lax.fori_loop(0, rows // tq, q_block, 0)

    head, kv_head, shared, col_vec, row_vec = _attn_specs(rows, kv_off, group_size, has_f)
    in_specs, args = [head, kv_head], [q, kv]
    if has_kr:
        in_specs.append(shared)
        args.append(kr)
    if has_f:
        in_specs += [col_vec, row_vec]
        args += [cum_col, cum_row]
    return pl.pallas_call(
        body, out_shape=[jax.ShapeDtypeStruct((rows, heads * LANE), BF), jax.ShapeDtypeStruct((heads, rows, 1), F32)],
        grid=(heads // group_size,), in_specs=in_specs, out_specs=[head, col_vec], name=name,
        compiler_params=_params(("arbitrary",)))(*args)


def _attn_backward(q, kv, kv_off, kr, cum_col, cum_row, o, do, lse, scale, group_size, name):
    rows = q.shape[0]
    heads = HEADS
    t = ATTN_TILE
    nb = rows // t
    has_kr = kr is not None
    has_f = cum_col is not None
    group = range(group_size)

    def body(*refs):
        it = iter(refs)
        q_ref, kv_ref = next(it), next(it)
        kr_ref = next(it) if has_kr else None
        cc_ref = next(it) if has_f else None
        cr_ref = next(it) if has_f else None
        o_ref, do_ref, lse_ref = next(it), next(it), next(it)
        dq_ref, dkv_ref = next(it), next(it)
        dkr_ref = next(it) if has_kr else None
        dck_ref = next(it) if has_f else None
        dcq_ref = next(it) if has_f else None
        dq_acc = next(it)
        lo = lax.broadcasted_iota(jnp.int32, (1, LANE), 1) < HEAD_DIM
        causal = (lax.broadcasted_iota(jnp.int32, (t, t), 1) <= lax.broadcasted_iota(jnp.int32, (t, t), 0))
        lanes = [slice(g * LANE, (g + 1) * LANE) for g in group]

        dq_acc[...] = jnp.zeros_like(dq_acc)
        if has_kr:
            _init_acc(dkr_ref)
        if has_f:
            dcq_ref[...] = jnp.zeros_like(dcq_ref)

        def kv_block(j, _):
            ks = pl.ds(pl.multiple_of(j * t, t), t)
            other = kr_ref[ks, :] if has_kr else jnp.zeros((t, LANE), BF)
            kvbs = [kv_ref[ks, lanes[g]] for g in group]
            kks = [jnp.where(lo, kvbs[g], other) for g in group]
            cks = [cr_ref[g, :, ks] if has_f else None for g in group]

            def pair(i, carry, diag):
                qs = pl.ds(pl.multiple_of(i * t, t), t)
                nt = (((1,), (1,)), ((), ()))

                def first_stage(g):
                    qb = q_ref[qs, lanes[g]]
                    dob = do_ref[qs, lanes[g]]
                    return (qb, dob, lax.dot_general(qb, kks[g], nt, preferred_element_type=F32),
                            lax.dot_general(dob, kvbs[g], nt, preferred_element_type=F32))

                first = {g: first_stage(g) for g in (group[:1] if has_f else group)}
                out = []
                for g in group:
                    dkk, dvv, dcs = carry[g]
                    qb, dob, logit, dp = first[g]
                    if has_f and g + 1 < len(group):
                        first[g + 1] = first_stage(g + 1)
                    s = logit * scale
                    if has_f:
                        s = s + (cc_ref[g, qs, :] - cks[g])
                    if diag:
                        s = jnp.where(causal, s, NEG_INF)
                    p = jnp.exp(s - lse_ref[g, qs, :])
                    delta = jnp.sum(dob.astype(F32) * o_ref[qs, lanes[g]].astype(F32), axis=1, keepdims=True)
                    ds = p * (dp - delta)
                    dsb = ds.astype(BF)
                    dvv = dvv + lax.dot_general(p.astype(BF), dob, (((0,), (0,)), ((), ())), preferred_element_type=F32)
                    dkk = dkk + lax.dot_general(dsb, qb, (((0,), (0,)), ((), ())), preferred_element_type=F32)
                    dq_acc[qs, lanes[g]] += jnp.dot(dsb, kks[g], preferred_element_type=F32)
                    if has_f:
                        dcs = dcs + jnp.sum(ds, axis=0, keepdims=True)
                        dcq_ref[g, qs, :] += jnp.sum(ds, axis=1, keepdims=True)
                    out.append((dkk, dvv, dcs))
                return tuple(out)

            init = tuple((jnp.zeros((t, LANE), F32), jnp.zeros((t, LANE), F32), jnp.zeros((1, t), F32)) for _ in group)
            carry = pair(j, init, True)
            carry = lax.fori_loop(j + 1, nb, lambda i, c: pair(i, c, False), carry)
            for g, (dkk, dvv, dcs) in enumerate(carry):
                dkk = dkk * scale
                dkv_ref[ks, lanes[g]] = jnp.where(lo, dkk, dvv).astype(BF)
                if has_kr:
                    dkr_ref[ks, :] += jnp.where(lo, 0.0, dkk)
                if has_f:
                    dck_ref[g, :, ks] = -dcs
            return 0

        lax.fori_loop(0, nb, kv_block, 0)
        dq_ref[...] = (dq_acc[...] * scale).astype(BF)

    head, kv_head, shared, col_vec, row_vec = _attn_specs(rows, kv_off, group_size, has_f)
    in_specs, args = [head, kv_head], [q, kv]
    if has_kr:
        in_specs.append(shared)
        args.append(kr)
    if has_f:
        in_specs += [col_vec, row_vec]
        args += [cum_col, cum_row]
    in_specs += [head, head, col_vec]
    args += [o, do, lse]
    out_shape = [jax.ShapeDtypeStruct((rows, heads * LANE), BF), jax.ShapeDtypeStruct((rows, heads * LANE), BF)]
    out_specs = [head, head]
    if has_kr:
        out_shape.append(jax.ShapeDtypeStruct((rows, LANE), F32))
        out_specs.append(shared)
    if has_f:
        out_shape += [jax.ShapeDtypeStruct((heads, 1, rows), F32), jax.ShapeDtypeStruct((heads, rows, 1), F32)]
        out_specs += [row_vec, col_vec]
    return pl.pallas_call(
        body, out_shape=out_shape, grid=(heads // group_size,), in_specs=in_specs, out_specs=out_specs,
        scratch_shapes=[pltpu.VMEM((rows, group_size * LANE), F32)], name=name,
        compiler_params=_params(("arbitrary",)))(*args)


def _tri_dot(tri, x):
    return jnp.dot(tri, x, preferred_element_type=F32, precision=lax.Precision.HIGHEST)


def _forget_forward(f_raw, b_f):
    rows = f_raw.shape[0]
    t = ATTN_TILE

    def body(f_ref, b_ref, cum_ref):
        tri = (lax.broadcasted_iota(jnp.int32, (t, t), 1) <= lax.broadcasted_iota(jnp.int32, (t, t), 0)).astype(F32)

        def blk(i, carry):
            sl = pl.ds(pl.multiple_of(i * t, t), t)
            xv = f_ref[sl, :] + b_ref[...]
            log_f = jnp.minimum(xv, 0.0) - jnp.log(1.0 + jnp.exp(-jnp.abs(xv)))
            cum_ref[sl, :] = _tri_dot(tri, log_f) + carry
            return carry + jnp.sum(log_f, axis=0, keepdims=True)

        lax.fori_loop(0, rows // t, blk, jnp.zeros((1, LANE), F32))

    return pl.pallas_call(body, out_shape=jax.ShapeDtypeStruct((rows, LANE), F32), name="forget_forward",
                          compiler_params=_params())(f_raw, b_f)


def _forget_backward(f_raw, b_f, dcum):
    rows = f_raw.shape[0]
    t = ATTN_TILE
    nb = rows // t

    def body(f_ref, b_ref, dc_ref, df_ref, db_ref):
        tri = (lax.broadcasted_iota(jnp.int32, (t, t), 1) >= lax.broadcasted_iota(jnp.int32, (t, t), 0)).astype(F32)

        def blk(i, carry):
            later, db = carry
            sl = pl.ds(pl.multiple_of((nb - 1 - i) * t, t), t)
            dc = dc_ref[sl, :]
            dlog = _tri_dot(tri, dc) + later
            xv = f_ref[sl, :] + b_ref[...]
            df = dlog / (1.0 + jnp.exp(xv))
            df_ref[sl, :] = df.astype(BF)
            return later + jnp.sum(dc, axis=0, keepdims=True), db + jnp.sum(df, axis=0, keepdims=True)

        _, db = lax.fori_loop(0, nb, blk, (jnp.zeros((1, LANE), F32), jnp.zeros((1, LANE), F32)))
        db_ref[...] = db

    return pl.pallas_call(body, out_shape=[jax.ShapeDtypeStruct((rows, LANE), BF), jax.ShapeDtypeStruct((1, LANE), F32)],
                          name="forget_backward", compiler_params=_params())(f_raw, b_f, dcum)


def _t5_bucket(dist):
    max_exact = REL_BUCKETS // 2
    n = jnp.maximum(dist.astype(F32), 1.0)
    large = max_exact + (jnp.log(n / max_exact) / math.log(REL_MAX_DIST / max_exact)
                         * (REL_BUCKETS - max_exact)).astype(jnp.int32)
    large = jnp.minimum(large, REL_BUCKETS - 1)
    return jnp.where(dist < max_exact, dist, large)


def _dil_buckets(dilation):
    i = jnp.arange(Q_BLOCK)[:, None]
    j = jnp.arange(Q_BLOCK)[None, :]
    cur = _t5_bucket(jnp.clip(i - j, 0) * dilation).astype(jnp.int32)
    prev = _t5_bucket(jnp.clip(Q_BLOCK + i - j, 0) * dilation).astype(jnp.int32)
    return cur, prev


def _dil_bias_tiles(tbl_ref, bc_ref, bp_ref, bias_ref, group, hp):
    ii = lax.broadcasted_iota(jnp.int32, (Q_BLOCK, Q_BLOCK), 0)
    jj = lax.broadcasted_iota(jnp.int32, (Q_BLOCK, Q_BLOCK), 1)
    for hh in range(2 * DIL_PAIRS):
        col = group * HEADS + 2 * DIL_PAIRS * hp + hh
        acc_c = jnp.zeros((Q_BLOCK, Q_BLOCK), F32)
        acc_p = jnp.zeros((Q_BLOCK, Q_BLOCK), F32)
        for b in range(REL_BUCKETS):
            val = tbl_ref[b, col]
            acc_c = jnp.where(bc_ref[...] == b, val, acc_c)
            acc_p = jnp.where(bp_ref[...] == b, val, acc_p)
        bias_ref[2 * hh] = jnp.where(jj <= ii, acc_c, NEG_INF)
        bias_ref[2 * hh + 1] = jnp.where(jj >= ii, acc_p, NEG_INF)


def _dil_view(qkv, group, dilation):
    if dilation == 1:
        return qkv
    width = 3 * HEADS * HEAD_DIM
    return qkv[:, group * width:(group + 1) * width].reshape(qkv.shape[0] // dilation, dilation * width)


def _dil_specs(group, dilation, length):
    width = DIL_PAIRS * LANE
    per = 8 // DIL_PAIRS

    def col(kind):
        if dilation == 1:
            return pl.BlockSpec((length, width), lambda hp, r: (0, (group * 3 + kind) * per + hp))
        return pl.BlockSpec((length, width), lambda hp, r: (0, (r * 3 + kind) * per + hp))

    out = pl.BlockSpec((length, width), lambda hp, r: (0, r * per + hp))
    tile = pl.BlockSpec((Q_BLOCK, Q_BLOCK), lambda hp, r: (0, 0))
    table = pl.BlockSpec(memory_space=pltpu.SMEM)
    return col, out, tile, table


def _dil_forward(view, group, dilation, table, buckets):
    length = view.shape[0]
    rows = length * dilation
    nb = length // Q_BLOCK
    scale = HEAD_DIM ** -0.5
    qb = Q_BLOCK

    def body(tbl_ref, bc_ref, bp_ref, q_ref, k_ref, v_ref, o_ref, lse_ref, bias_ref):
        hp = pl.program_id(0)

        @pl.when(pl.program_id(1) == 0)
        def _():
            _dil_bias_tiles(tbl_ref, bc_ref, bp_ref, bias_ref, group, hp)

        lo = lax.broadcasted_iota(jnp.int32, (1, LANE), 1) < HEAD_DIM
        nt = (((1,), (1,)), ((), ()))

        def blk(n, first):
            cur = pl.ds(0, qb) if first else pl.ds(pl.multiple_of(n * qb, qb), qb)
            prev = None if first else pl.ds(pl.multiple_of((n - 1) * qb, qb), qb)
            logits = []
            for pair in range(DIL_PAIRS):
                lanes = slice(pair * LANE, (pair + 1) * LANE)
                qn = q_ref[cur, lanes] * scale
                for hh in range(2):
                    qm = jnp.where(lo if hh == 0 else ~lo, qn, jnp.zeros_like(qn))
                    s_c = lax.dot_general(qm, k_ref[cur, lanes], nt, preferred_element_type=F32)
                    s_p = None if first else lax.dot_general(qm, k_ref[prev, lanes], nt, preferred_element_type=F32)
                    logits.append((s_c, s_p))
            for pair in range(DIL_PAIRS):
                lanes = slice(pair * LANE, (pair + 1) * LANE)
                outs, lses = [], []
                for hh in range(2):
                    bias = 4 * pair + 2 * hh
                    s_c, s_p = logits[2 * pair + hh]
                    s_c = s_c + bias_ref[bias]
                    m = jnp.max(s_c, axis=1, keepdims=True)
                    if not first:
                        s_p = s_p + bias_ref[bias + 1]
                        m = jnp.maximum(m, jnp.max(s_p, axis=1, keepdims=True))
                    e_c = jnp.exp(s_c - m)
                    l = jnp.sum(e_c, axis=1, keepdims=True)
                    acc = jnp.dot(e_c.astype(BF), v_ref[cur, lanes], preferred_element_type=F32)
                    if not first:
                        e_p = jnp.exp(s_p - m)
                        l = l + jnp.sum(e_p, axis=1, keepdims=True)
                        acc = acc + jnp.dot(e_p.astype(BF), v_ref[prev, lanes], preferred_element_type=F32)
                    outs.append(acc * (1.0 / l))
                    lses.append(m + jnp.log(l))
                o_ref[cur, lanes] = jnp.where(lo, outs[0], outs[1])
                lse_ref[cur, lanes] = jnp.where(lo, lses[0], lses[1])
            return 0

        blk(0, True)
        if nb > 1:
            lax.fori_loop(1, nb, lambda n, _: blk(n, False), 0)

    col, out, tile, tbl = _dil_specs(group, dilation, length)
    bc, bp = buckets
    o, lse = pl.pallas_call(
        body, out_shape=[jax.ShapeDtypeStruct((length, dilation * D_MODEL), F32)] * 2,
        grid=(8 // DIL_PAIRS, dilation), in_specs=[tbl, tile, tile, col(0), col(1), col(2)], out_specs=[out, out],
        scratch_shapes=[pltpu.VMEM((4 * DIL_PAIRS, qb, qb), F32)], name=f"dilated_forward_{dilation}",
        compiler_params=_params(("arbitrary", "arbitrary")))(
            table, bc, bp, view, view, view)
    return o.reshape(rows, D_MODEL), lse.reshape(rows, D_MODEL)


def _dil_backward(view, group, dilation, table, buckets, do_g, lse, dlt):
    length = view.shape[0]
    rows = length * dilation
    nb = length // Q_BLOCK
    scale = HEAD_DIM ** -0.5
    qb = Q_BLOCK

    def body(tbl_ref, bc_ref, bp_ref, q_ref, k_ref, v_ref, do_ref, lse_ref, dlt_ref,
             dq_ref, dk_ref, dv_ref, db_ref, bias_ref, dk_acc, dv_acc):
        hp = pl.program_id(0)

        @pl.when(pl.program_id(1) == 0)
        def _():
            _dil_bias_tiles(tbl_ref, bc_ref, bp_ref, bias_ref, group, hp)
            db_ref[...] = jnp.zeros_like(db_ref)

        dk_acc[...] = jnp.zeros_like(dk_acc)
        dv_acc[...] = jnp.zeros_like(dv_acc)
        lo = lax.broadcasted_iota(jnp.int32, (1, LANE), 1) < HEAD_DIM
        tn = (((0,), (0,)), ((), ()))
        nt = (((1,), (1,)), ((), ()))

        def blk(n, first):
            cur = pl.ds(0, qb) if first else pl.ds(pl.multiple_of(n * qb, qb), qb)
            prev = None if first else pl.ds(pl.multiple_of((n - 1) * qb, qb), qb)
            inputs = []
            for pair in range(DIL_PAIRS):
                lanes = slice(pair * LANE, (pair + 1) * LANE)
                qn = q_ref[cur, lanes] * scale
                don = do_ref[cur, lanes]
                for hh in range(2):
                    mask = lo if hh == 0 else ~lo
                    qm = jnp.where(mask, qn, jnp.zeros_like(qn))
                    dom = jnp.where(mask, don, jnp.zeros_like(don))
                    stage = [qm, dom, lax.dot_general(qm, k_ref[cur, lanes], nt, preferred_element_type=F32),
                             lax.dot_general(dom, v_ref[cur, lanes], nt, preferred_element_type=F32)]
                    if not first:
                        stage += [lax.dot_general(qm, k_ref[prev, lanes], nt, preferred_element_type=F32),
                                  lax.dot_general(dom, v_ref[prev, lanes], nt, preferred_element_type=F32)]
                    inputs.append(stage)
            for pair in range(DIL_PAIRS):
                lanes = slice(pair * LANE, (pair + 1) * LANE)
                kc = k_ref[cur, lanes]
                if not first:
                    kp = k_ref[prev, lanes]
                lse_n = lse_ref[cur, lanes]
                dlt_n = dlt_ref[cur, lanes]
                dqs = []
                dkc = jnp.zeros((qb, LANE), F32)
                dkp = jnp.zeros((qb, LANE), F32)
                dvc = jnp.zeros((qb, LANE), F32)
                dvp = jnp.zeros((qb, LANE), F32)
                for hh in range(2):
                    bias = 4 * pair + 2 * hh
                    mask = lo if hh == 0 else ~lo
                    qm, dom, s_c, dp_c = inputs[2 * pair + hh][:4]
                    lse_h = jnp.max(jnp.where(mask, lse_n, -3e38), axis=1, keepdims=True)
                    dlt_h = jnp.max(jnp.where(mask, dlt_n, -3e38), axis=1, keepdims=True)
                    p_c = jnp.exp(s_c + bias_ref[bias] - lse_h)
                    ds_c = p_c * (dp_c - dlt_h)
                    db_ref[pair, 2 * hh] += ds_c
                    dsc_b = ds_c.astype(BF)
                    dq = jnp.dot(dsc_b, kc, preferred_element_type=F32)
                    dkc = dkc + lax.dot_general(dsc_b, qm, tn, preferred_element_type=F32)
                    dvc = dvc + lax.dot_general(p_c.astype(BF), dom, tn, preferred_element_type=F32)
                    if not first:
                        s_p, dp_p = inputs[2 * pair + hh][4:]
                        p_p = jnp.exp(s_p + bias_ref[bias + 1] - lse_h)
                        ds_p = p_p * (dp_p - dlt_h)
                        db_ref[pair, 2 * hh + 1] += ds_p
                        dsp_b = ds_p.astype(BF)
                        dq = dq + jnp.dot(dsp_b, kp, preferred_element_type=F32)
                        dkp = dkp + lax.dot_general(dsp_b, qm, tn, preferred_element_type=F32)
                        dvp = dvp + lax.dot_general(p_p.astype(BF), dom, tn, preferred_element_type=F32)
                    dqs.append(dq)
                dq_ref[cur, lanes] = (jnp.where(lo, dqs[0], dqs[1]) * scale).astype(BF)
                dk_acc[cur, lanes] += dkc
                dv_acc[cur, lanes] += dvc
                if not first:
                    dk_acc[prev, lanes] += dkp
                    dv_acc[prev, lanes] += dvp
            return 0

        blk(0, True)
        if nb > 1:
            lax.fori_loop(1, nb, lambda n, _: blk(n, False), 0)
        dk_ref[...] = dk_acc[...].astype(BF)
        dv_ref[...] = dv_acc[...].astype(BF)

    col, out, tile, tbl = _dil_specs(group, dilation, length)
    bc, bp = buckets
    wide = (length, dilation * D_MODEL)
    dq, dk, dv, db = pl.pallas_call(
        body, out_shape=[jax.ShapeDtypeStruct(wide, BF)] * 3 + [jax.ShapeDtypeStruct((8, 4, qb, qb), F32)],
        grid=(8 // DIL_PAIRS, dilation), in_specs=[tbl, tile, tile, col(0), col(1), col(2), out, out, out],
        out_specs=[out, out, out, pl.BlockSpec((DIL_PAIRS, 4, qb, qb), lambda hp, r: (hp, 0, 0, 0))],
        scratch_shapes=[pltpu.VMEM((4 * DIL_PAIRS, qb, qb), F32), pltpu.VMEM((length, DIL_PAIRS * LANE), F32),
                        pltpu.VMEM((length, DIL_PAIRS * LANE), F32)],
        name=f"dilated_backward_{dilation}", compiler_params=_params(("arbitrary", "arbitrary")))(
            table, bc, bp, view, view, view,
            do_g.reshape(wide), lse.reshape(wide), dlt.reshape(wide))
    return dq.reshape(rows, D_MODEL), dk.reshape(rows, D_MODEL), dv.reshape(rows, D_MODEL), db


def _head_sums(x, lo):
    s0 = jnp.sum(jnp.where(lo, x, 0.0), axis=1, keepdims=True)
    s1 = jnp.sum(jnp.where(lo, 0.0, x), axis=1, keepdims=True)
    return jnp.where(lo, s0, s1)


def _dil_merge_forward(outs, lses):
    rows = outs[0].shape[0]

    def body(o0, o1, o2, l0, l1, l2, o_ref):
        ls = [l0[...], l1[...], l2[...]]
        m = jnp.maximum(jnp.maximum(ls[0], ls[1]), ls[2])
        es = [jnp.exp(v - m) for v in ls]
        tot = es[0] + es[1] + es[2]
        o_ref[...] = ((es[0] * o0[...] + es[1] * o1[...] + es[2] * o2[...]) / tot).astype(BF)

    blk = pl.BlockSpec((ROW_TILE, LANE), lambda i, j: (i, j))
    return pl.pallas_call(body, out_shape=jax.ShapeDtypeStruct((rows, D_MODEL), BF), grid=(rows // ROW_TILE, 8),
                          in_specs=[blk] * 6, out_specs=blk, name="dilated_merge_forward",
                          compiler_params=_params(("parallel", "parallel")))(*outs, *lses)


def _dil_merge_backward(outs, lses, do):
    rows = outs[0].shape[0]

    def body(o0, o1, o2, l0, l1, l2, do_ref, d0, d1, d2, t0, t1, t2):
        lo = lax.broadcasted_iota(jnp.int32, (1, LANE), 1) < HEAD_DIM
        ls = [l0[...], l1[...], l2[...]]
        os_ = [o0[...], o1[...], o2[...]]
        m = jnp.maximum(jnp.maximum(ls[0], ls[1]), ls[2])
        es = [jnp.exp(v - m) for v in ls]
        inv = 1.0 / (es[0] + es[1] + es[2])
        alphas = [e * inv for e in es]
        dov = do_ref[...]
        merged = alphas[0] * os_[0] + alphas[1] * os_[1] + alphas[2] * os_[2]
        dot = _head_sums(dov * merged, lo)
        for a, d_ref, t_ref in zip(alphas, (d0, d1, d2), (t0, t1, t2)):
            d_ref[...] = (a * dov).astype(BF)
            t_ref[...] = a * dot

    blk = pl.BlockSpec((ROW_TILE, LANE), lambda i, j: (i, j))
    res = pl.pallas_call(
        body, out_shape=[jax.ShapeDtypeStruct((rows, D_MODEL), BF)] * 3 + [jax.ShapeDtypeStruct((rows, D_MODEL), F32)] * 3,
        grid=(rows // ROW_TILE, 8), in_specs=[blk] * 7, out_specs=[blk] * 6, name="dilated_merge_backward",
        compiler_params=_params(("parallel", "parallel")))(*outs, *lses, do)
    return res[:3], res[3:]


def _rel_bias_grad(dbs, buckets):
    def body(db_ref, bc_ref, bp_ref, o_ref):
        g = pl.program_id(0)
        hp = pl.program_id(1)

        @pl.when((g == 0) & (hp == 0))
        def _():
            o_ref[...] = jnp.zeros_like(o_ref)

        rr = lax.broadcasted_iota(jnp.int32, (REL_BUCKETS, LANE), 0)
        cc = lax.broadcasted_iota(jnp.int32, (REL_BUCKETS, LANE), 1)
        bc = bc_ref[0]
        bp = bp_ref[0]
        acc = jnp.zeros((REL_BUCKETS, LANE), F32)
        for hh in range(2):
            col = g * HEADS + 2 * hp + hh
            d_c = db_ref[0, 0, 2 * hh]
            d_p = db_ref[0, 0, 2 * hh + 1]
            for b in range(REL_BUCKETS):
                val = (jnp.sum(jnp.where(bc == b, d_c, 0.0), keepdims=True)
                       + jnp.sum(jnp.where(bp == b, d_p, 0.0), keepdims=True))
                acc = jnp.where((rr == b) & (cc == col), val, acc)
        o_ref[...] += acc

    db_all = jnp.stack(dbs)
    bc_all = jnp.stack([b[0] for b in buckets])
    bp_all = jnp.stack([b[1] for b in buckets])
    tile = pl.BlockSpec((1, Q_BLOCK, Q_BLOCK), lambda g, hp: (g, 0, 0))
    return pl.pallas_call(
        body, out_shape=jax.ShapeDtypeStruct((REL_BUCKETS, LANE), F32), grid=(3, 8),
        in_specs=[pl.BlockSpec((1, 1, 4, Q_BLOCK, Q_BLOCK), lambda g, hp: (g, hp, 0, 0, 0)), tile, tile],
        out_specs=pl.BlockSpec((REL_BUCKETS, LANE), lambda g, hp: (0, 0)), name="rel_bias_grad",
        compiler_params=_params(("arbitrary", "arbitrary")))(db_all, bc_all, bp_all)


def _mla_forward(hn, w, tables):
    a = _matmul(hn, w["w_a"], name="mla_a")
    cq, ckv, kr = _mla_mid_forward(a, w["q_norm"], w["kv_norm"], tables)
    q_raw = _matmul(cq, w["w_uq"], name="mla_uq")
    q = _rope_heads(q_raw, tables, False, "rope_forward")
    kv = _matmul(ckv, w["w_ukv"], b_chunks=True, out_dtype=BF, name="mla_ukv")
    scale = (HEAD_DIM + MLA_ROPE) ** -0.5
    o, lse = _attn_forward(q, kv, 0, kr, None, None, scale, MLA_GROUP, "mla_attention_forward")
    y = _matmul(o, w["w_o"], name="attn_out")
    return y, dict(hn=hn, a=a, cq=cq, ckv=ckv, kr=kr, q=q, kv=kv, o=o, lse=lse)


def _mla_backward(dy, w, s, tables):
    scale = (HEAD_DIM + MLA_ROPE) ** -0.5
    g = {}
    g["w_o"] = _matmul(s["o"], dy, ta=True, out_dtype=BF, name="attn_out_dw")
    do = _matmul(dy, w["w_o"], tb=True, out_dtype=BF, name="attn_out_dx")
    dq, dkv, dkr = _attn_backward(s["q"], s["kv"], 0, s["kr"], None, None, s["o"], do, s["lse"], scale,
                                  MLA_GROUP, "mla_attention_backward")
    dq_raw = _rope_heads(dq, tables, True, "rope_backward")
    g["w_uq"] = _matmul(s["cq"], dq_raw, ta=True, out_dtype=BF, name="mla_uq_dw")
    dcq = _matmul(dq_raw, w["w_uq"], tb=True, name="mla_uq_dx")
    g["w_ukv"] = _matmul(s["ckv"], dkv, ta=True, out_chunks=True, out_dtype=BF, name="mla_ukv_dw")
    dckv = _matmul(dkv, w["w_ukv"], tb=True, b_chunks=True, name="mla_ukv_dx")
    da, g["q_norm"], g["kv_norm"] = _mla_mid_backward(s["a"], w["q_norm"], w["kv_norm"], tables, dcq, dckv, dkr)
    g["w_a"] = _matmul(s["hn"], da, ta=True, out_dtype=BF, name="mla_a_dw")
    dhn = _matmul(da, w["w_a"], tb=True, name="mla_a_dx")
    return dhn, g


def _fox_forward(hn, w):
    qkv = _matmul(hn, w["w_qkv"], out_dtype=BF, name="fox_qkv")
    f_raw = _matmul(hn, w["w_f"], name="fox_f")
    cum = _forget_forward(f_raw, w["b_f"])
    cum_heads = cum[:, :HEADS].T
    cum_col, cum_row = cum_heads[:, :, None], cum_heads[:, None, :]
    o, lse = _attn_forward(qkv, qkv, HEADS, None, cum_col, cum_row, HEAD_DIM ** -0.5, FOX_GROUP,
                           "fox_attention_forward")
    y = _matmul(o, w["w_o"], name="attn_out")
    return y, dict(hn=hn, qkv=qkv, f_raw=f_raw, cum_col=cum_col, cum_row=cum_row, o=o, lse=lse)


def _fox_backward(dy, w, s):
    g = {}
    g["w_o"] = _matmul(s["o"], dy, ta=True, out_dtype=BF, name="attn_out_dw")
    do = _matmul(dy, w["w_o"], tb=True, out_dtype=BF, name="attn_out_dx")
    dq, dkv, dck, dcq = _attn_backward(s["qkv"], s["qkv"], HEADS, None, s["cum_col"], s["cum_row"], s["o"], do,
                                       s["lse"], HEAD_DIM ** -0.5, FOX_GROUP, "fox_attention_backward")
    dcum = jnp.pad((dck[:, 0, :] + dcq[:, :, 0]).T, ((0, 0), (0, LANE - HEADS)))
    df, g["b_f"] = _forget_backward(s["f_raw"], w["b_f"], dcum)
    dqkv = jnp.concatenate([dq, dkv], axis=1)
    g["w_qkv"] = _matmul(s["hn"], dqkv, ta=True, out_dtype=BF, name="fox_qkv_dw")
    g["w_f"] = _matmul(s["hn"], df, ta=True, out_dtype=BF, name="fox_f_dw")
    dhn = _matmul(dqkv, w["w_qkv"], tb=True, name="fox_qkv_dx")
    dhn = _matmul(df, w["w_f"], tb=True, add=dhn, name="fox_f_dx")
    return dhn, g


def _dil_mixer_forward(hn, w, buckets):
    qkv = _matmul(hn, w["w_qkv"], b_chunks=True, out_dtype=BF, name="dil_qkv")
    views = [_dil_view(qkv, grp, dilation) for grp, (_, dilation) in enumerate(DIL_PATTERNS)]
    outs, lses = [], []
    for grp, (_, dilation) in enumerate(DIL_PATTERNS):
        o_g, lse_g = _dil_forward(views[grp], grp, dilation, w["rel_bias"], buckets[grp])
        outs.append(o_g)
        lses.append(lse_g)
    o = _dil_merge_forward(outs, lses)
    y = _matmul(o, w["w_o"], name="dil_out")
    return y, dict(hn=hn, views=views, outs=outs, lses=lses, o=o)


def _dil_mixer_backward(dy, w, s, buckets):
    g = {}
    g["w_o"] = _matmul(s["o"], dy, ta=True, out_dtype=BF, name="dil_out_dw")
    do = _matmul(dy, w["w_o"], tb=True, name="dil_out_dx")
    do_gs, dlts = _dil_merge_backward(s["outs"], s["lses"], do)
    parts, dbs = [], []
    for grp, (_, dilation) in enumerate(DIL_PATTERNS):
        dq, dk, dv, db = _dil_backward(s["views"][grp], grp, dilation, w["rel_bias"], buckets[grp], do_gs[grp],
                                       s["lses"][grp], dlts[grp])
        parts += [dq, dk, dv]
        dbs.append(db)
    dqkv = jnp.concatenate(parts, axis=1)
    g["rel_bias"] = _rel_bias_grad(dbs, buckets)
    g["w_qkv"] = _matmul(s["hn"], dqkv, ta=True, out_chunks=True, out_dtype=BF, name="dil_qkv_dw")
    dhn = _matmul(dqkv, w["w_qkv"], tb=True, b_chunks=True, name="dil_qkv_dx")
    return dhn, g


def _mixer_weights(i, lw, small):
    mixer, j = i % N_MIXERS, i // N_MIXERS
    if mixer == 0:
        return dict(lw["mixer"], q_norm=small["mla_q_norm"][j][None, :], kv_norm=small["mla_kv_norm"][j][None, :])
    if mixer == 1:
        return dict(lw["mixer"], rel_bias=small["rel_bias"])
    return dict(lw["mixer"], b_f=jnp.pad(small["fox_b_f"][j][None, :], ((0, 0), (0, LANE - HEADS))))


MIXER_PART, COMMON_PART = 0, 1


def _run_layers(x, p, positions, target, get_part, get_small, put_part):
    tables = _rope_tables(positions)
    buckets = [_dil_buckets(d) for _, d in DIL_PATTERNS]
    layers, saved = [], []
    h = x
    first = get_part(0, MIXER_PART, positions)
    small = get_small()

    def gain(i, k):
        return small["norm_g"][i, k][None, :]

    hn = _prenorm(h, gain(0, 0))
    sq = dh = None
    for i in range(DEPTH):
        mixer = i % N_MIXERS
        lw = dict(mixer=first if i == 0 else get_part(i, MIXER_PART, h))
        mw = _mixer_weights(i, lw, small)
        if mixer == 0:
            y, ms = _mla_forward(hn, mw, tables)
        elif mixer == 1:
            y, ms = _dil_mixer_forward(hn, mw, buckets)
        else:
            y, ms = _fox_forward(hn, mw)
        lw.update(get_part(i, COMMON_PART, y))
        layers.append(lw)
        h1, hn2 = _post_residual(h, y, gain(i, 1), gain(i, 2))
        gu = _matmul(hn2, lw["ffn_w_in"], b_chunks=True, out_dtype=BF, name="ffn_in")
        act = _swiglu_forward(gu)
        f = _matmul(act, lw["ffn_w_out"], name="ffn_out")
        h2, h2b = _post_residual(h1, f, gain(i, 3), None)
        pp = _matmul(p[i], lw["ple_w_proj"], b_chunks=True, name="ple_proj")
        z = _matmul(h2b, lw["ple_w_gate"], name="ple_gate")
        saved.append(dict(h=h, y=y, ms=ms, h1=h1, hn2=hn2, gu=gu, act=act, f=f, h2b=h2b, pp=pp, z=z))
        if i + 1 < DEPTH:
            h, hn = _ple_forward(h2, pp, z, gain(i + 1, 0))
        else:
            dh, sq = _ple_loss(h2, pp, z, target)

    norm_rows = [[None] * 4 for _ in range(DEPTH)]
    sg = dict(mla_q_norm={}, mla_kv_norm={}, rel_bias=None, fox_b_f={})
    for i in reversed(range(DEPTH)):
        s, lw = saved[i], layers[i]
        mixer, j = i % N_MIXERS, i // N_MIXERS
        mw = _mixer_weights(i, lw, small)
        lg = {}
        dpp, dz = _ple_backward(dh, s["pp"], s["z"])
        lg["ple_w_proj"] = _matmul(p[i], dpp, ta=True, out_chunks=True, out_dtype=BF, name="ple_proj_dw")
        lg["ple_w_gate"] = _matmul(s["h2b"], dz, ta=True, out_dtype=BF, name="ple_gate_dw")
        dh2 = _matmul(dz, lw["ple_w_gate"], tb=True, add=dh, name="ple_gate_dx")
        df, norm_rows[i][3] = _rms_backward(s["f"], gain(i, 3), dh2, None, BF)
        lg["ffn_w_out"] = _matmul(s["act"], df, ta=True, out_dtype=BF, name="ffn_out_dw")
        dact = _matmul(df, lw["ffn_w_out"], tb=True, out_dtype=BF, name="ffn_out_dx")
        dgu = _swiglu_backward(s["gu"], dact)
        lg["ffn_w_in"] = _matmul(s["hn2"], dgu, ta=True, out_chunks=True, out_dtype=BF, name="ffn_in_dw")
        token = put_part(i, COMMON_PART, lg)
        dhn2 = _matmul(dgu, lw["ffn_w_in"], tb=True, b_chunks=True, name="ffn_in_dx")
        dh1, norm_rows[i][2] = _rms_backward(s["h1"], gain(i, 2), dhn2, dh2, F32)
        dy, norm_rows[i][1] = _rms_backward(s["y"], gain(i, 1) + token[0:1, 0:1], dh1, None, BF)
        if mixer == 0:
            dhn, mg = _mla_backward(dy, mw, s["ms"], tables)
            sg["mla_q_norm"][j] = mg.pop("q_norm")
            sg["mla_kv_norm"][j] = mg.pop("kv_norm")
        elif mixer == 1:
            dhn, mg = _dil_mixer_backward(dy, mw, s["ms"], buckets)
            rel = mg.pop("rel_bias")[:, :3 * HEADS]
            sg["rel_bias"] = rel if sg["rel_bias"] is None else sg["rel_bias"] + rel
        else:
            dhn, mg = _fox_backward(dy, mw, s["ms"])
            sg["fox_b_f"][j] = mg.pop("b_f")[:, :HEADS]
        token = put_part(i, MIXER_PART, mg)
        dh, norm_rows[i][0] = _rms_backward(s["h"], gain(i, 0) + token[0:1, 0:1], dhn, dh1, F32)
    small_grads = dict(norm_g=jnp.stack([jnp.concatenate(row, axis=0) for row in norm_rows]),
                       rel_bias=sg["rel_bias"])
    for k in ("mla_q_norm", "mla_kv_norm", "fox_b_f"):
        small_grads[k] = jnp.concatenate([sg[k][j] for j in sorted(sg[k])], axis=0)
    return sq, dh, small_grads


BIG = ("ffn_w_in", "ffn_w_out", "ple_w_proj", "ple_w_gate", "mla_w_a", "mla_w_uq", "mla_w_ukv", "mla_w_o",
       "dil_w_qkv", "dil_w_o", "fox_w_qkvf", "fox_w_o")
SMALL_SHARDED = ("norm_g", "mla_q_norm", "mla_kv_norm")
SMALL_REPLICATED = ("rel_bias", "fox_b_f")
WEIGHTS = ("norm_g", "ffn_w_in", "ffn_w_out", "ple_w_proj", "ple_w_gate", "rel_bias", "mla_w_a", "mla_q_norm",
           "mla_kv_norm", "mla_w_uq", "mla_w_ukv", "mla_w_o", "dil_w_qkv", "dil_w_o", "fox_w_qkvf", "fox_b_f", "fox_w_o")


TRANSPOSED = "fox_w_qkvf"
LAYER_COMMON = ("ffn_w_in", "ffn_w_out", "ple_w_proj", "ple_w_gate")
MIXER_WEIGHTS = (("mla_w_a", "mla_w_uq", "mla_w_ukv", "mla_w_o"), ("dil_w_qkv", "dil_w_o"), ("fox_w_qkvf", "fox_w_o"))


def _part_names(i, part):
    return MIXER_WEIGHTS[i % N_MIXERS] if part == MIXER_PART else LAYER_COMMON


def _layer_slot(name, i):
    return i if name in LAYER_COMMON else i // N_MIXERS


def _merge_rows(chunks):
    n, r, c = chunks.shape
    return chunks.reshape(n * r, c)


def _merge_cols(chunks):
    n, r, c = chunks.shape
    return chunks.transpose(1, 0, 2).reshape(r, n * c)


def _pad_heads_out(wo):
    w3 = wo.reshape(HEADS, HEAD_DIM, D_MODEL)
    return jnp.pad(w3, ((0, 0), (HEAD_DIM, 0), (0, 0))).reshape(HEADS * LANE, D_MODEL)


def _part_to_compute(i, part, ch):
    if part == COMMON_PART:
        return dict(ffn_w_in=ch["ffn_w_in"], ffn_w_out=_merge_rows(ch["ffn_w_out"]), ple_w_proj=ch["ple_w_proj"],
                    ple_w_gate=_merge_rows(ch["ple_w_gate"]))
    lw = {}
    mixer = i % N_MIXERS
    if mixer == 0:
        wa = _merge_rows(ch["mla_w_a"])
        rank = MLA_Q_RANK + MLA_KV_RANK
        wa_p = jnp.concatenate([wa[:, :rank], jnp.zeros((wa.shape[0], 64), wa.dtype), wa[:, rank:],
                                jnp.zeros((wa.shape[0], 32), wa.dtype)], axis=1)
        wuq = _merge_cols(ch["mla_w_uq"]).reshape(MLA_Q_RANK, HEADS, HEAD_DIM + MLA_ROPE)
        wuq_p = jnp.pad(wuq, ((0, 0), (0, 0), (0, LANE - HEAD_DIM - MLA_ROPE))).reshape(MLA_Q_RANK, HEADS * LANE)
        lw["mixer"] = dict(w_a=wa_p, w_uq=wuq_p, w_ukv=ch["mla_w_ukv"], w_o=_pad_heads_out(_merge_rows(ch["mla_w_o"])))
    elif mixer == 1:
        lw["mixer"] = dict(w_qkv=ch["dil_w_qkv"], w_o=_merge_rows(ch["dil_w_o"]))
    else:
        wf = _merge_rows(ch["fox_w_qkvf"]).T
        inner = HEADS * HEAD_DIM
        q3 = wf[:, :inner].reshape(D_MODEL, HEADS, HEAD_DIM)
        k3 = wf[:, inner:2 * inner].reshape(D_MODEL, HEADS, HEAD_DIM)
        v3 = wf[:, 2 * inner:3 * inner].reshape(D_MODEL, HEADS, HEAD_DIM)
        q_p = jnp.pad(q3, ((0, 0), (0, 0), (0, HEAD_DIM))).reshape(D_MODEL, HEADS * LANE)
        kv_p = jnp.concatenate([k3, v3], axis=2).reshape(D_MODEL, HEADS * LANE)
        f_p = jnp.pad(wf[:, 3 * inner:], ((0, 0), (0, LANE - HEADS)))
        lw["mixer"] = dict(w_qkv=jnp.concatenate([q_p, kv_p], axis=1), w_f=f_p,
                           w_o=_pad_heads_out(_merge_rows(ch["fox_w_o"])))
    return lw["mixer"]


def _part_contributions(i, part, lg, chunk_shapes):
    spec = {k: jax.ShapeDtypeStruct(s, BF) for k, s in chunk_shapes.items()}
    (contrib,) = jax.linear_transpose(functools.partial(_part_to_compute, i, part), spec)(lg)
    return contrib


def _chip_peers():
    x, y, c = lax.axis_index("x"), lax.axis_index("y"), lax.axis_index("c")
    peers = [(1 - x, y), (x, 1 - y), (1 - x, 1 - y)]
    return x, y, c, peers


SEM_SPEC = pl.BlockSpec(memory_space=pltpu.SEMAPHORE)
ANY_SPEC = pl.BlockSpec(memory_space=pl.ANY)
SPLIT_EFFECT = pltpu.SideEffectType.DATAFLOW_SIDE_EFFECTING


def _own_slot(shard):
    me = 2 * lax.axis_index("x") + lax.axis_index("y")
    return lax.dynamic_update_index_in_dim(lax.empty((N_CHIPS,) + shard.shape, shard.dtype), shard[None], me, 0)


def _spread_copy(src, land, k, peer, c, send_sems, recv_sems, index, src_slot, slot):
    px, py = peer
    return pltpu.make_async_remote_copy(
        src_ref=src.at[src_slot], dst_ref=land.at[slot],
        send_sem=send_sems.at[3 * index + k], recv_sem=recv_sems.at[3 * index + k],
        device_id=(px, py, c), device_id_type=MESH)


def _spread_start(bufs, srcs, after, name):
    n = len(bufs)
    exchange = srcs is not None
    arrays = (list(srcs) if exchange else []) + list(bufs)
    na = len(arrays)

    def body(*refs):
        src, land = refs[:n], refs[na - n:na]
        send_sems, recv_sems = refs[na + 1], refs[na + 2]
        token = refs[-1]
        x, y, c, peers = _chip_peers()
        me = 2 * x + y
        for w in range(n):
            for k, peer in enumerate(peers):
                src_slot = 2 * peer[0] + peer[1] if exchange else me
                _spread_copy(src[w], land[w], k, peer, c, send_sems, recv_sems, w, src_slot, me).start()
        token[...] = jnp.zeros_like(token)

    hbm = [pltpu.with_memory_space_constraint(a, pltpu.HBM) for a in arrays]
    out = pl.pallas_call(
        body, name=name,
        out_shape=(pltpu.SemaphoreType.DMA((3 * n,)), pltpu.SemaphoreType.DMA((3 * n,)),
                   *[pltpu.HBM(a.shape, a.dtype) for a in hbm], jax.ShapeDtypeStruct((8, LANE), F32)),
        in_specs=[HBM_SPEC] * na + [ANY_SPEC],
        out_specs=(SEM_SPEC, SEM_SPEC, *[HBM_SPEC] * na, pl.BlockSpec(memory_space=pltpu.VMEM)),
        input_output_aliases={w: 2 + w for w in range(na)},
        compiler_params=pltpu.CompilerParams(has_side_effects=SPLIT_EFFECT))(*hbm, after)
    return dict(send=out[0], recv=out[1], arrays=out[2:2 + na], n=n, token=out[-1], exchange=exchange)


def _spread_wait(handle, after, name):
    n, exchange = handle["n"], handle["exchange"]
    arrays = list(handle["arrays"])
    na = len(arrays)

    def body(*refs):
        src, land = refs[:n], refs[na - n:na]
        send_sems, recv_sems = refs[na], refs[na + 1]
        x, y, c, peers = _chip_peers()
        me = 2 * x + y
        for w in range(n):
            for k, peer in enumerate(peers):
                there = 2 * peer[0] + peer[1]
                cp = _spread_copy(src[w], land[w], k, peer, c, send_sems, recv_sems, w, there if exchange else me, there)
                cp.wait_send()
                cp.wait_recv()

    out = pl.pallas_call(
        body, name=name, out_shape=tuple(pltpu.HBM(a.shape, a.dtype) for a in arrays),
        in_specs=[HBM_SPEC] * na + [SEM_SPEC, SEM_SPEC, ANY_SPEC], out_specs=tuple([HBM_SPEC] * na),
        input_output_aliases={w: w for w in range(na)},
        compiler_params=pltpu.CompilerParams(has_side_effects=SPLIT_EFFECT))(*arrays, handle["send"], handle["recv"], after)
    return (list(out[n:]), list(out[:n])) if exchange else list(out)


def _sibling_copy(received, sent, land, k, me, peers, sibling, send_sems, recv_sems, index):
    slot = me if k == 3 else 2 * peers[k][0] + peers[k][1]
    src = sent if k == 3 else received
    return pltpu.make_async_remote_copy(
        src_ref=src.at[slot], dst_ref=land.at[slot], send_sem=send_sems.at[4 * index + k],
        recv_sem=recv_sems.at[4 * index + k], device_id=sibling, device_id_type=MESH)


def _sibling_start(received, sent, after, name):
    n = len(received)
    lands = [lax.empty(a.shape, a.dtype) for a in received]
    arrays = list(received) + list(sent) + lands

    def body(*refs):
        rec, snt, land = refs[:n], refs[n:2 * n], refs[2 * n:3 * n]
        send_sems, recv_sems = refs[3 * n + 1], refs[3 * n + 2]
        token = refs[-1]
        x, y, c, peers = _chip_peers()
        for w in range(n):
            for k in range(4):
                _sibling_copy(rec[w], snt[w], land[w], k, 2 * x + y, peers, (x, y, 1 - c), send_sems, recv_sems, w).start()
        token[...] = jnp.zeros_like(token)

    hbm = [pltpu.with_memory_space_constraint(a, pltpu.HBM) for a in arrays]
    out = pl.pallas_call(
        body, name=name,
        out_shape=(pltpu.SemaphoreType.DMA((4 * n,)), pltpu.SemaphoreType.DMA((4 * n,)),
                   *[pltpu.HBM(a.shape, a.dtype) for a in hbm], jax.ShapeDtypeStruct((8, LANE), F32)),
        in_specs=[HBM_SPEC] * (3 * n) + [ANY_SPEC],
        out_specs=(SEM_SPEC, SEM_SPEC, *[HBM_SPEC] * (3 * n), pl.BlockSpec(memory_space=pltpu.VMEM)),
        input_output_aliases={w: 2 + w for w in range(3 * n)},
        compiler_params=pltpu.CompilerParams(has_side_effects=SPLIT_EFFECT))(*hbm, after)
    return dict(send=out[0], recv=out[1], arrays=out[2:2 + 3 * n], n=n, token=out[-1])


def _sibling_wait(handle, after, name):
    n = handle["n"]
    arrays = list(handle["arrays"])

    def body(*refs):
        rec, snt, land = refs[:n], refs[n:2 * n], refs[2 * n:3 * n]
        send_sems, recv_sems = refs[3 * n], refs[3 * n + 1]
        x, y, c, peers = _chip_peers()
        for w in range(n):
            for k in range(4):
                cp = _sibling_copy(rec[w], snt[w], land[w], k, 2 * x + y, peers, (x, y, 1 - c), send_sems, recv_sems, w)
                cp.wait_send()
                cp.wait_recv()

    out = pl.pallas_call(
        body, name=name, out_shape=tuple(pltpu.HBM(a.shape, a.dtype) for a in arrays),
        in_specs=[HBM_SPEC] * (3 * n) + [SEM_SPEC, SEM_SPEC, ANY_SPEC], out_specs=tuple([HBM_SPEC] * (3 * n)),
        input_output_aliases={w: w for w in range(3 * n)},
        compiler_params=pltpu.CompilerParams(has_side_effects=SPLIT_EFFECT))(*arrays, handle["send"], handle["recv"], after)
    return list(out[:n]), list(out[n:2 * n]), list(out[2 * n:])


def _all_reduce_small(v):
    rows = v.shape[0]

    def body(v_ref, sum_ref, slots, send_sems, recv_sems):
        x, y, c = lax.axis_index("x"), lax.axis_index("y"), lax.axis_index("c")
        me = 4 * x + 2 * y + c
        slots[me] = v_ref[...]
        sends = []
        for k in range(1, N_DEV):
            bx, by, bc = (k >> 2) & 1, (k >> 1) & 1, k & 1
            peer = (x ^ bx, y ^ by, c ^ bc)
            rc = pltpu.make_async_remote_copy(src_ref=v_ref, dst_ref=slots.at[me], send_sem=send_sems.at[k],
                                              recv_sem=recv_sems.at[k], device_id=peer, device_id_type=MESH)
            rc.start()
            sends.append(rc)
        for k in range(1, N_DEV):
            bx, by, bc = (k >> 2) & 1, (k >> 1) & 1, k & 1
            src = 4 * (x ^ bx) + 2 * (y ^ by) + (c ^ bc)
            pltpu.make_async_remote_copy(src_ref=v_ref, dst_ref=slots.at[src], send_sem=send_sems.at[k],
                                         recv_sem=recv_sems.at[k], device_id=(x ^ bx, y ^ by, c ^ bc),
                                         device_id_type=MESH).wait_recv()
        for rc in sends:
            rc.wait_send()
        total = slots[0]
        for k in range(1, N_DEV):
            total = total + slots[k]
        sum_ref[...] = total

    vm = pl.BlockSpec(memory_space=pltpu.VMEM)
    return pl.pallas_call(
        body, out_shape=jax.ShapeDtypeStruct((rows, LANE), F32), in_specs=[vm], out_specs=vm,
        scratch_shapes=[pltpu.VMEM((N_DEV, rows, LANE), F32), pltpu.SemaphoreType.DMA((N_DEV,)),
                        pltpu.SemaphoreType.DMA((N_DEV,))], name="all_reduce_small")(v)


def _as_2d(a):
    return a.reshape(-1, a.shape[-1])


def _row_tile(rows, cols):
    for t in (512, 256, 128, 64, 32, 16):
        if rows % t == 0 and t * cols * 4 <= (1 << 20):
            return t
    return rows


def _adamw_weight(w, m, v, received, sent, sibling):
    layers = len(received)
    _, rows, cols = received[0].shape
    tr = _row_tile(rows, cols)
    by_columns = rows % tr != 0 or tr == rows and rows * cols * 4 > (2 << 20)
    if by_columns:
        assert layers == 1 and cols % (2 * LANE) == 0, (w.shape, received[0].shape)
        tr, tc, steps = rows, cols // 2, 2
        index = lambda i: (0, i)
    else:
        tc, steps = cols, rows // tr
        index = lambda i: (i, 0)
    where = (2 * lax.axis_index("x") + lax.axis_index("y")).astype(jnp.int32).reshape(1)

    def body(where_ref, w_ref, m_ref, v_ref, *rest):
        per_layer, (g_ref, d_ref, nm_ref, nv_ref) = rest[:3 * layers], rest[3 * layers:]
        me = where_ref[0]
        for layer in range(layers):
            r_ref, own_ref, s_ref = per_layer[3 * layer:3 * layer + 3]

            @pl.when(pl.program_id(0) == layer)
            def _():
                mine = theirs = None
                for k in range(N_CHIPS):
                    a = jnp.where(me == k, own_ref[...], r_ref[k]).astype(F32)
                    b = s_ref[k].astype(F32)
                    mine = a if mine is None else mine + a
                    theirs = b if theirs is None else theirs + b
                g = mine + theirs
                delta, nm, nv = _adamw_math(w_ref[...], g, m_ref[...], v_ref[...])
                g_ref[...] = g
                d_ref[...] = delta
                nm_ref[...] = nm
                nv_ref[...] = nv

    def held(layer, now, i):
        return jnp.where(now < layer, 0, jnp.where(now > layer, steps - 1, i))

    if by_columns:
        stacked = pl.BlockSpec((tr, tc), lambda now, i, where_ref: index(i))
    else:
        stacked = pl.BlockSpec((tr, tc), lambda now, i, where_ref: (now * steps + i, 0))
    in_specs = [stacked, stacked, stacked]
    args = [where, w, m, v]
    for layer in range(layers):
        four = pl.BlockSpec((N_CHIPS, tr, tc), lambda now, i, where_ref, layer=layer: (0,) + index(held(layer, now, i)))
        own = pl.BlockSpec((None, tr, tc),
                           lambda now, i, where_ref, layer=layer: (where_ref[0],) + index(held(layer, now, i)))
        in_specs += [four, own, four]
        args += [received[layer], sent[layer], sibling[layer]]
    grid_spec = pltpu.PrefetchScalarGridSpec(num_scalar_prefetch=1, grid=(layers, steps), in_specs=in_specs,
                                             out_specs=[stacked] * 4)
    return pl.pallas_call(body, out_shape=[jax.ShapeDtypeStruct(w.shape, F32)] * 4, grid_spec=grid_spec,
                          name="adamw_weight", compiler_params=_params(("arbitrary", "arbitrary")))(*args)


def _adamw_math(w, g, m, v):
    m = ADAM_B1 * m + (1.0 - ADAM_B1) * g
    v = ADAM_B2 * v + (1.0 - ADAM_B2) * (g * g)
    m_hat = m * (1.0 / (1.0 - ADAM_B1 ** ADAM_STEP))
    v_hat = v * (1.0 / (1.0 - ADAM_B2 ** ADAM_STEP))
    denom = jnp.sqrt(v_hat) + ADAM_EPS
    inv = pl.reciprocal(denom, approx=True)
    inv = inv * (2.0 - denom * inv)
    delta = -ADAM_LR * (m_hat * inv + ADAM_WD * w)
    return delta, m, v


def _adamw(w, m, v, g_mine, g_sibling):
    rows, cols = w.shape
    tr = _row_tile(rows, cols)
    two = g_sibling is not None

    def body(*refs):
        if two:
            w_ref, m_ref, v_ref, ga_ref, gb_ref, g_ref, d_ref, nm_ref, nv_ref = refs
            g = ga_ref[...] + gb_ref[...]
        else:
            w_ref, m_ref, v_ref, ga_ref, g_ref, d_ref, nm_ref, nv_ref = refs
            g = ga_ref[...]
        delta, nm, nv = _adamw_math(w_ref[...], g, m_ref[...], v_ref[...])
        g_ref[...] = g
        d_ref[...] = delta
        nm_ref[...] = nm
        nv_ref[...] = nv

    blk = pl.BlockSpec((tr, cols), lambda i: (i, 0))
    args = [w, m, v, g_mine] + ([g_sibling] if two else [])
    return pl.pallas_call(body, out_shape=[jax.ShapeDtypeStruct((rows, cols), F32)] * 4, grid=(rows // tr,),
                          in_specs=[blk] * len(args), out_specs=[blk] * 4, name="adamw",
                          compiler_params=_params(("parallel",)))(*args)


def _pack_rows(arrays):
    flat = jnp.concatenate([a.reshape(-1) for a in arrays])
    rows = -(-flat.shape[0] // (8 * LANE)) * 8
    return jnp.pad(flat, (0, rows * LANE - flat.shape[0])).reshape(rows, LANE)


def _unpack_rows(packed, shapes):
    flat = packed.reshape(-1)
    out, at = [], 0
    for s in shapes:
        size = math.prod(s)
        out.append(flat[at:at + size].reshape(s))
        at += size
    return out


def kernel(x, p, positions, norm_g, ffn_w_in, ffn_w_out, ple_w_proj, ple_w_gate, rel_bias, mla_w_a, mla_q_norm, mla_kv_norm, mla_w_uq, mla_w_ukv, mla_w_o, dil_w_qkv, dil_w_o, fox_w_qkvf, fox_b_f, fox_w_o, loss_target, m_norm_g, m_ffn_w_in, m_ffn_w_out, m_ple_w_proj, m_ple_w_gate, m_rel_bias, m_mla_w_a, m_mla_q_norm, m_mla_kv_norm, m_mla_w_uq, m_mla_w_ukv, m_mla_w_o, m_dil_w_qkv, m_dil_w_o, m_fox_w_qkvf, m_fox_b_f, m_fox_w_o, v_norm_g, v_ffn_w_in, v_ffn_w_out, v_ple_w_proj, v_ple_w_gate, v_rel_bias, v_mla_w_a, v_mla_q_norm, v_mla_kv_norm, v_mla_w_uq, v_mla_w_ukv, v_mla_w_o, v_dil_w_qkv, v_dil_w_o, v_fox_w_qkvf, v_fox_b_f, v_fox_w_o):
    w = dict(norm_g=norm_g, ffn_w_in=ffn_w_in, ffn_w_out=ffn_w_out, ple_w_proj=ple_w_proj, ple_w_gate=ple_w_gate,
             rel_bias=rel_bias, mla_w_a=mla_w_a, mla_q_norm=mla_q_norm, mla_kv_norm=mla_kv_norm, mla_w_uq=mla_w_uq,
             mla_w_ukv=mla_w_ukv, mla_w_o=mla_w_o, dil_w_qkv=dil_w_qkv, dil_w_o=dil_w_o, fox_w_qkvf=fox_w_qkvf,
             fox_b_f=fox_b_f, fox_w_o=fox_w_o)
    m = dict(norm_g=m_norm_g, ffn_w_in=m_ffn_w_in, ffn_w_out=m_ffn_w_out, ple_w_proj=m_ple_w_proj,
             ple_w_gate=m_ple_w_gate, rel_bias=m_rel_bias, mla_w_a=m_mla_w_a, mla_q_norm=m_mla_q_norm,
             mla_kv_norm=m_mla_kv_norm, mla_w_uq=m_mla_w_uq, mla_w_ukv=m_mla_w_ukv, mla_w_o=m_mla_w_o,
             dil_w_qkv=m_dil_w_qkv, dil_w_o=m_dil_w_o, fox_w_qkvf=m_fox_w_qkvf, fox_b_f=m_fox_b_f, fox_w_o=m_fox_w_o)
    v = dict(norm_g=v_norm_g, ffn_w_in=v_ffn_w_in, ffn_w_out=v_ffn_w_out, ple_w_proj=v_ple_w_proj,
             ple_w_gate=v_ple_w_gate, rel_bias=v_rel_bias, mla_w_a=v_mla_w_a, mla_q_norm=v_mla_q_norm,
             mla_kv_norm=v_mla_kv_norm, mla_w_uq=v_mla_w_uq, mla_w_ukv=v_mla_w_ukv, mla_w_o=v_mla_w_o,
             dil_w_qkv=v_dil_w_qkv, dil_w_o=v_dil_w_o, fox_w_qkvf=v_fox_w_qkvf, fox_b_f=v_fox_b_f, fox_w_o=v_fox_w_o)
    chip = 2 * lax.axis_index("x") + lax.axis_index("y")
    for tree in (w, m, v):
        tree[TRANSPOSED] = jnp.swapaxes(tree[TRANSPOSED], 1, 2)

    small_shapes = [w[k].shape for k in SMALL_SHARDED]
    order = [(i, part) for i in range(DEPTH) for part in (MIXER_PART, COMMON_PART)]
    gathers = {}
    after = positions
    zero = 0.0
    for i, part in order:
        bufs = [_own_slot((w[k][_layer_slot(k, i)] + zero).astype(BF)) for k in _part_names(i, part)]
        if (i, part) == order[0]:
            bufs.append(_own_slot(_pack_rows([w[k] for k in SMALL_SHARDED])))
        gathers[i, part] = _spread_start(bufs, None, after, f"gather_start_{i}_{part}")
        after = gathers[i, part]["token"]
        if (i, part) == order[0]:
            zero = after[0, 0]
    all_started = after
    state = {}

    def get_part(i, part, after_array):
        is_first = (i, part) == order[0]
        lands = _spread_wait(gathers[i, part], all_started if is_first else after_array, f"gather_wait_{i}_{part}")
        if is_first:
            pieces = [_unpack_rows(lands[-1][k], small_shapes) for k in range(N_CHIPS)]
            small = {name: jnp.concatenate([pieces[k][idx] for k in range(N_CHIPS)], axis=-1)
                     for idx, name in enumerate(SMALL_SHARDED)}
            state["small"] = dict(small, rel_bias=rel_bias, fox_b_f=fox_b_f)
        chunks = dict(zip(_part_names(i, part), lands))
        state[i, part] = {k: a.shape for k, a in chunks.items()}
        return _part_to_compute(i, part, chunks)

    started, forwards = [], {}

    def forward_oldest(after_array):
        i, part, handle = started.pop(0)
        received, sent = _spread_wait(handle, after_array, f"exchange_wait_{i}_{part}")
        forwards[i, part] = _sibling_start(received, sent, after_array, f"sibling_start_{i}_{part}")
        return forwards[i, part]["token"]

    def put_part(i, part, lg):
        contrib = _part_contributions(i, part, lg, state[i, part])
        srcs = [contrib[k] for k in _part_names(i, part)]
        handle = _spread_start([lax.empty(s.shape, s.dtype) for s in srcs], srcs, positions,
                               f"exchange_start_{i}_{part}")
        token = handle["token"]
        if started:
            token = token + forward_oldest(token)
        started.append((i, part, handle))
        return token

    sq, grad_x, sg = _run_layers(x[0], p[:, 0], positions[0], loss_target[0], get_part, lambda: state["small"],
                                 put_part)
    loss = lax.psum(0.5 / D_MODEL * jnp.sum(sq), ("x", "y", "c"))
    forward_oldest(grad_x)

    held = {k: {} for k in BIG}
    for i, part in [(i, part) for i in reversed(range(DEPTH)) for part in (COMMON_PART, MIXER_PART)]:
        received, sent, sibling = _sibling_wait(forwards[i, part], grad_x, f"sibling_wait_{i}_{part}")
        for k, r, s, t in zip(_part_names(i, part), received, sent, sibling):
            held[k][_layer_slot(k, i)] = (r, s, t)
    results = {}
    for k in BIG:
        per_layer = [held[k][slot] for slot in sorted(held[k])]
        outs = _adamw_weight(_as_2d(w[k]), _as_2d(m[k]), _as_2d(v[k]), *[list(col) for col in zip(*per_layer)])
        results[k] = [o.reshape(w[k].shape) for o in outs]
    results[TRANSPOSED] = [jnp.swapaxes(o, 1, 2) for o in results[TRANSPOSED]]

    small_all = SMALL_SHARDED + SMALL_REPLICATED
    full_shapes = [sg[k].shape for k in small_all]
    reduced = dict(zip(small_all, _unpack_rows(_all_reduce_small(_pack_rows([sg[k] for k in small_all])), full_shapes)))
    local_g = []
    for k in small_all:
        g = reduced[k]
        if k in SMALL_SHARDED:
            width = w[k].shape[-1]
            g = lax.dynamic_slice_in_dim(g, chip * width, width, axis=g.ndim - 1)
        local_g.append(g)
    local_shapes = [w[k].shape for k in small_all]
    outs = _adamw(_pack_rows([w[k] for k in small_all]), _pack_rows([m[k] for k in small_all]),
                  _pack_rows([v[k] for k in small_all]), _pack_rows(local_g), None)
    unpacked = [_unpack_rows(o, local_shapes) for o in outs]
    for idx, k in enumerate(small_all):
        results[k] = [u[idx] for u in unpacked]

    return (loss, grad_x[None], *[results[k][0] for k in WEIGHTS], *[results[k][1] for k in WEIGHTS],
            *[results[k][2] for k in WEIGHTS], *[results[k][3] for k in WEIGHTS])
```

```python
import functools
import math

import jax
import jax.numpy as jnp
from jax import lax
from jax.experimental import pallas as pl
from jax.experimental.pallas import tpu as pltpu

F32 = jnp.float32
BF = jnp.bfloat16
MESH = pl.DeviceIdType.MESH
HBM_SPEC = pl.BlockSpec(memory_space=pltpu.HBM)

D_MODEL = 1024
DEPTH = 4
N_MIXERS = 3
D_FF = 2816
NORM_EPS = 1e-6
NEG_INF = -1e30
LANE = 128
HEADS = 16
HEAD_DIM = 64
MLA_Q_RANK = 384
MLA_KV_RANK = 256
MLA_ROPE = 32
MLA_A_PAD = 768
ROPE_THETA = 10000.0
DIL_PATTERNS = ((128, 1), (512, 4), (2048, 16))
Q_BLOCK = 128
DIL_PAIRS = 2
REL_BUCKETS = 32
REL_MAX_DIST = 2048
N_CHIPS = 4
N_DEV = 8

ADAM_LR = 0.001
ADAM_B1 = 0.9
ADAM_B2 = 0.999
ADAM_EPS = 1e-08
ADAM_WD = 0.01
ADAM_STEP = 10

VMEM_LIMIT = 56 * 1024 * 1024
MATMUL_VMEM_BUDGET = 36 * 1024 * 1024
ROW_TILE = 512
ATTN_TILE = 256
ATTN_Q_TILE = 512
MLA_GROUP = 4
FOX_GROUP = 4
FORWARD_GROUP = 4


def _params(sem=None):
    return pltpu.CompilerParams(dimension_semantics=sem, vmem_limit_bytes=VMEM_LIMIT)


def _divisor_tiles(dim):
    tiles = [t for t in range(LANE, dim + 1, LANE) if dim % t == 0]
    return tiles or [dim]


def _matmul_tiles(m, n, k, a_bytes, b_bytes, out_bytes, has_add, n_unit=None, k_unit=None):
    best = None
    for tm in _divisor_tiles(m):
        for tn in _divisor_tiles(n_unit or n):
            for tk in _divisor_tiles(k_unit or k):
                if max(tm, tn, tk) > 2048:
                    continue
                vmem = 2 * (tm * tk * a_bytes + tk * tn * b_bytes + tm * tn * out_bytes) + tm * tn * 4
                if has_add:
                    vmem += 2 * tm * tn * 4
                if vmem > MATMUL_VMEM_BUDGET:
                    continue
                steps = (m // tm) * (n // tn) * (k // tk)
                traffic = m * k * a_bytes * (n // tn) + k * n * b_bytes * (m // tm) + m * n * out_bytes
                cost = traffic / 3.0e12 + steps * 0.4e-6
                if best is None or cost < best[0]:
                    best = (cost, tm, tn, tk)
    return best[1:]


def _matmul(a, b, *, ta=False, tb=False, b_chunks=False, out_chunks=False, add=None, out_dtype=F32, name):
    k, m = a.shape if ta else a.shape[::-1]
    n_unit = k_unit = None
    if b_chunks:
        chunks, rows_w, c = b.shape
        if tb:
            kb, n, k_unit = chunks * c, rows_w, c
        else:
            kb, n, n_unit = rows_w, chunks * c, c
    else:
        kb, n = b.shape[::-1] if tb else b.shape
    if out_chunks:
        assert n % N_CHIPS == 0 and add is None
        n_unit = n // N_CHIPS
    assert k == kb, (a.shape, b.shape, ta, tb)
    tm, tn, tk = _matmul_tiles(m, n, k, a.dtype.itemsize, b.dtype.itemsize, jnp.dtype(out_dtype).itemsize,
                               add is not None, n_unit, k_unit)
    nk = k // tk
    dims = (((0 if ta else 1,), (1 if tb else 0,)), ((), ()))

    def body(*refs):
        if add is None:
            a_ref, b_ref, o_ref, acc_ref = refs
            add_ref = None
        else:
            a_ref, b_ref, add_ref, o_ref, acc_ref = refs
        kk = pl.program_id(2)

        @pl.when(kk == 0)
        def _():
            acc_ref[...] = jnp.zeros_like(acc_ref)

        acc_ref[...] += lax.dot_general(a_ref[...].astype(BF), b_ref[...].astype(BF), dims,
                                        preferred_element_type=F32)

        @pl.when(kk == nk - 1)
        def _():
            r = acc_ref[...]
            if add_ref is not None:
                r = r + add_ref[...].astype(F32)
            o_ref[...] = r.astype(out_dtype)

    a_spec = pl.BlockSpec((tk, tm), lambda i, j, q: (q, i)) if ta else pl.BlockSpec((tm, tk), lambda i, j, q: (i, q))
    if b_chunks and tb:
        per_k = k_unit // tk
        b_spec = pl.BlockSpec((None, tn, tk), lambda i, j, q: (q // per_k, j, q % per_k))
    elif b_chunks:
        per_n = n_unit // tn
        b_spec = pl.BlockSpec((None, tk, tn), lambda i, j, q: (j // per_n, q, j % per_n))
    elif tb:
        b_spec = pl.BlockSpec((tn, tk), lambda i, j, q: (j, q))
    else:
        b_spec = pl.BlockSpec((tk, tn), lambda i, j, q: (q, j))
    if out_chunks:
        per_o = n_unit // tn
        o_spec = pl.BlockSpec((None, tm, tn), lambda i, j, q: (j // per_o, i, j % per_o))
        out_shape = jax.ShapeDtypeStruct((N_CHIPS, m, n_unit), out_dtype)
    else:
        o_spec = pl.BlockSpec((tm, tn), lambda i, j, q: (i, j))
        out_shape = jax.ShapeDtypeStruct((m, n), out_dtype)
    in_specs = [a_spec, b_spec]
    args = [a, b]
    if add is not None:
        in_specs.append(o_spec)
        args.append(add)
    return pl.pallas_call(
        body, out_shape=out_shape, grid=(m // tm, n // tn, nk),
        in_specs=in_specs, out_specs=o_spec, scratch_shapes=[pltpu.VMEM((tm, tn), F32)], name=name,
        compiler_params=_params(("parallel", "parallel", "arbitrary")))(*args)


def _rowwise(body, name, rows, ins, outs, tr=ROW_TILE):
    def row_spec(cols):
        return pl.BlockSpec((tr, cols), lambda i: (i, 0))

    def full_spec(shape):
        zeros = (0,) * len(shape)
        return pl.BlockSpec(shape, lambda i: zeros)

    in_specs = [row_spec(a.shape[1]) if kind == "row" else full_spec(a.shape) for a, kind in ins]
    out_specs = [row_spec(shape[1]) if kind == "row" else full_spec(shape) for shape, _, kind in outs]
    out_shape = [jax.ShapeDtypeStruct(shape, dtype) for shape, dtype, _ in outs]
    return pl.pallas_call(body, out_shape=out_shape, grid=(rows // tr,), in_specs=in_specs, out_specs=out_specs,
                          name=name, compiler_params=_params(("arbitrary",)))(*[a for a, _ in ins])


def _rstd(x):
    return lax.rsqrt(jnp.mean(x * x, axis=-1, keepdims=True) + NORM_EPS)


def _rms_bwd_math(x, g, dy):
    r = _rstd(x)
    gd = dy * g
    dx = r * gd - x * (r * r * r) * jnp.mean(gd * x, axis=-1, keepdims=True)
    dg = jnp.sum(dy * x * r, axis=0, keepdims=True)
    return dx, dg


def _sigmoid(x):
    return 0.5 * jnp.tanh(0.5 * x) + 0.5


def _init_acc(*refs):
    @pl.when(pl.program_id(0) == 0)
    def _():
        for r in refs:
            r[...] = jnp.zeros_like(r)


def _prenorm(h, g):
    rows, cols = h.shape

    def body(h_ref, g_ref, o_ref):
        x = h_ref[...]
        o_ref[...] = (x * _rstd(x) * g_ref[...]).astype(BF)

    return _rowwise(body, "prenorm", rows, [(h, "row"), (g, "full")], [((rows, cols), BF, "row")])[0]


def _post_residual(h, y, g_post, g_pre):
    rows, cols = h.shape
    with_pre = g_pre is not None

    def body(*refs):
        if with_pre:
            h_ref, y_ref, gp_ref, gq_ref, hn_ref, hb_ref = refs
        else:
            h_ref, y_ref, gp_ref, hn_ref, hb_ref = refs
        yv = y_ref[...]
        hn = h_ref[...] + yv * _rstd(yv) * gp_ref[...]
        hn_ref[...] = hn
        hb_ref[...] = (hn * _rstd(hn) * gq_ref[...] if with_pre else hn).astype(BF)

    ins = [(h, "row"), (y, "row"), (g_post, "full")] + ([(g_pre, "full")] if with_pre else [])
    return _rowwise(body, "post_residual_pre" if with_pre else "post_residual", rows, ins,
                    [((rows, cols), F32, "row"), ((rows, cols), BF, "row")])


def _ple_forward(h2, pp, z, g_pre):
    rows, cols = h2.shape

    def body(h_ref, p_ref, z_ref, g_ref, h3_ref, hb_ref):
        h3 = h_ref[...] + p_ref[...] * _sigmoid(z_ref[...])
        h3_ref[...] = h3
        hb_ref[...] = (h3 * _rstd(h3) * g_ref[...]).astype(BF)

    return _rowwise(body, "ple_forward", rows, [(h2, "row"), (pp, "row"), (z, "row"), (g_pre, "full")],
                    [((rows, cols), F32, "row"), ((rows, cols), BF, "row")])


def _ple_loss(h2, pp, z, target):
    rows, cols = h2.shape

    def body(h_ref, p_ref, z_ref, t_ref, dh_ref, sq_ref):
        _init_acc(sq_ref)
        err = h_ref[...] + p_ref[...] * _sigmoid(z_ref[...]) - t_ref[...]
        dh_ref[...] = err * (1.0 / cols)
        sq_ref[...] += jnp.sum(err * err, axis=0, keepdims=True)

    return _rowwise(body, "ple_loss", rows, [(h2, "row"), (pp, "row"), (z, "row"), (target, "row")],
                    [((rows, cols), F32, "row"), ((1, cols), F32, "acc")])


def _ple_backward(dh3, pp, z):
    rows, cols = dh3.shape

    def body(d_ref, p_ref, z_ref, dpp_ref, dz_ref):
        d = d_ref[...]
        s = _sigmoid(z_ref[...])
        dpp_ref[...] = (d * s).astype(BF)
        dz_ref[...] = (d * p_ref[...] * s * (1.0 - s)).astype(BF)

    return _rowwise(body, "ple_backward", rows, [(dh3, "row"), (pp, "row"), (z, "row")],
                    [((rows, cols), BF, "row"), ((rows, cols), BF, "row")])


def _rms_backward(x, g, dy, add, out_dtype):
    rows, cols = x.shape
    with_add = add is not None

    def body(*refs):
        if with_add:
            x_ref, g_ref, dy_ref, add_ref, dx_ref, dg_ref = refs
        else:
            x_ref, g_ref, dy_ref, dx_ref, dg_ref = refs
        _init_acc(dg_ref)
        dx, dg = _rms_bwd_math(x_ref[...], g_ref[...], dy_ref[...].astype(F32))
        if with_add:
            dx = dx + add_ref[...]
        dx_ref[...] = dx.astype(out_dtype)
        dg_ref[...] += dg

    ins = [(x, "row"), (g, "full"), (dy, "row")] + ([(add, "row")] if with_add else [])
    return _rowwise(body, "rms_backward_add" if with_add else "rms_backward", rows, ins,
                    [((rows, cols), out_dtype, "row"), ((1, cols), F32, "acc")])


def _swiglu_forward(gu):
    rows = gu.shape[0]

    def body(gu_ref, o_ref):
        g = gu_ref[:, :D_FF].astype(F32)
        o_ref[...] = (g * _sigmoid(g) * gu_ref[:, D_FF:].astype(F32)).astype(BF)

    return _rowwise(body, "swiglu_forward", rows, [(gu, "row")], [((rows, D_FF), BF, "row")])[0]


def _swiglu_backward(gu, dact):
    rows = gu.shape[0]

    def body(gu_ref, d_ref, o_ref):
        g = gu_ref[:, :D_FF].astype(F32)
        u = gu_ref[:, D_FF:].astype(F32)
        d = d_ref[...].astype(F32)
        s = _sigmoid(g)
        gs = g * s
        o_ref[:, :D_FF] = (d * u * (s + gs * (1.0 - s))).astype(BF)
        o_ref[:, D_FF:] = (d * gs).astype(BF)

    return _rowwise(body, "swiglu_backward", rows, [(gu, "row"), (dact, "row")], [((rows, 2 * D_FF), BF, "row")])[0]


def _rope_tables(positions):
    half = MLA_ROPE // 2
    inv = ROPE_THETA ** (-jnp.arange(half, dtype=F32) / half)
    ang = positions.astype(F32)[:, None] * inv
    cos, sin = jnp.cos(ang), jnp.sin(ang)
    rows = positions.shape[0]
    c = jnp.ones((rows, LANE), F32).at[:, 64:80].set(cos).at[:, 80:96].set(cos)
    sa = jnp.zeros((rows, LANE), F32).at[:, 64:80].set(-sin)
    sb = jnp.zeros((rows, LANE), F32).at[:, 80:96].set(sin)
    return c, sa, sb


def _rope_apply(x, c, sa, sb):
    return x * c + pltpu.roll(x, LANE - 16, 1) * sa + pltpu.roll(x, 16, 1) * sb


def _rope_apply_t(dy, c, sa, sb):
    return dy * c + pltpu.roll(dy * sa, 16, 1) + pltpu.roll(dy * sb, LANE - 16, 1)


def _rope_heads(x, tables, transpose, name):
    rows, cols = x.shape

    def body(x_ref, c_ref, sa_ref, sb_ref, o_ref):
        fn = _rope_apply_t if transpose else _rope_apply
        c, sa, sb = c_ref[...], sa_ref[...], sb_ref[...]
        for head in range(cols // LANE):
            lanes = slice(head * LANE, (head + 1) * LANE)
            o_ref[:, lanes] = fn(x_ref[:, lanes].astype(F32), c, sa, sb).astype(BF)

    blk = pl.BlockSpec((ROW_TILE, cols), lambda i: (i, 0))
    tbl = pl.BlockSpec((ROW_TILE, LANE), lambda i: (i, 0))
    return pl.pallas_call(body, out_shape=jax.ShapeDtypeStruct((rows, cols), BF), grid=(rows // ROW_TILE,),
                          in_specs=[blk, tbl, tbl, tbl], out_specs=blk, name=name,
                          compiler_params=_params(("parallel",)))(x, *tables)


def _mla_mid_forward(a, q_norm, kv_norm, tables):
    rows = a.shape[0]
    qr, kvr = MLA_Q_RANK, MLA_KV_RANK

    def body(a_ref, qn_ref, kn_ref, c_ref, sa_ref, sb_ref, cq_ref, ckv_ref, kr_ref):
        aq = a_ref[:, 0:qr]
        akv = a_ref[:, qr:qr + kvr]
        cq_ref[...] = (aq * _rstd(aq) * qn_ref[...]).astype(BF)
        ckv_ref[...] = (akv * _rstd(akv) * kn_ref[...]).astype(BF)
        kr_ref[...] = _rope_apply(a_ref[:, qr + kvr:], c_ref[...], sa_ref[...], sb_ref[...]).astype(BF)

    ins = [(a, "row"), (q_norm, "full"), (kv_norm, "full")] + [(t, "row") for t in tables]
    return _rowwise(body, "mla_mid_forward", rows, ins,
                    [((rows, qr), BF, "row"), ((rows, kvr), BF, "row"), ((rows, LANE), BF, "row")])


def _mla_mid_backward(a, q_norm, kv_norm, tables, dcq, dckv, dkr):
    rows = a.shape[0]
    qr, kvr = MLA_Q_RANK, MLA_KV_RANK

    def body(a_ref, qn_ref, kn_ref, c_ref, sa_ref, sb_ref, dcq_ref, dckv_ref, dkr_ref, da_ref, dqn_ref, dkn_ref):
        _init_acc(dqn_ref, dkn_ref)
        dxq, dgq = _rms_bwd_math(a_ref[:, 0:qr], qn_ref[...], dcq_ref[...])
        dxk, dgk = _rms_bwd_math(a_ref[:, qr:qr + kvr], kn_ref[...], dckv_ref[...])
        da_ref[:, 0:qr] = dxq.astype(BF)
        da_ref[:, qr:qr + kvr] = dxk.astype(BF)
        da_ref[:, qr + kvr:] = _rope_apply_t(dkr_ref[...], c_ref[...], sa_ref[...], sb_ref[...]).astype(BF)
        dqn_ref[...] += dgq
        dkn_ref[...] += dgk

    ins = ([(a, "row"), (q_norm, "full"), (kv_norm, "full")] + [(t, "row") for t in tables]
           + [(dcq, "row"), (dckv, "row"), (dkr, "row")])
    return _rowwise(body, "mla_mid_backward", rows, ins,
                    [((rows, MLA_A_PAD), BF, "row"), ((1, qr), F32, "acc"), ((1, kvr), F32, "acc")])


def _attn_specs(rows, kv_off, g, many_row_vectors):
    head =pl.BlockSpec((rows, g * LANE), lambda h: (0, h))
    kv_head = pl.BlockSpec((rows, g * LANE), lambda h: (0, h + kv_off // g))
    shared = pl.BlockSpec((rows, LANE), lambda h: (0, 0))
    col_vec = pl.BlockSpec((g, rows, 1), lambda h: (h, 0, 0),
                           pipeline_mode=pl.Buffered(1 if many_row_vectors else 2))
    row_vec = pl.BlockSpec((g, 1, rows), lambda h: (h, 0, 0))
    return head, kv_head, shared, col_vec, row_vec


def _attn_forward(q, kv, kv_off, kr, cum_col, cum_row, scale, group_size, name):
    rows = q.shape[0]
    heads = HEADS
    t = ATTN_TILE
    tq = ATTN_Q_TILE
    per = tq // t
    has_kr = kr is not None
    has_f = cum_col is not None
    group = range(group_size)

    def body(*refs):
        it = iter(refs)
        q_ref, kv_ref = next(it), next(it)
        kr_ref = next(it) if has_kr else None
        cc_ref = next(it) if has_f else None
        cr_ref = next(it) if has_f else None
        o_ref, lse_ref = next(it), next(it)
        lo = lax.broadcasted_iota(jnp.int32, (1, LANE), 1) < HEAD_DIM
        row = lax.broadcasted_iota(jnp.int32, (tq, t), 0)
        col = lax.broadcasted_iota(jnp.int32, (tq, t), 1)
        lanes = [slice(g * LANE, (g + 1) * LANE) for g in group]

        def q_block(i, _):
            qs = pl.ds(pl.multiple_of(i * tq, tq), tq)
            qbs = [q_ref[qs, lanes[g]] for g in group]
            cqs = [cc_ref[g, qs, :] if has_f else None for g in group]

            def step(j, carry, diag):
                ks = pl.ds(pl.multiple_of(j * t, t), t)
                other = kr_ref[ks, :] if has_kr else jnp.zeros((t, LANE), BF)
                kvbs = [kv_ref[ks, lanes[g]] for g in group]

                def logit(g):
                    return lax.dot_general(qbs[g], jnp.where(lo, kvbs[g], other), (((1,), (1,)), ((), ())),
                                           preferred_element_type=F32)

                logits = {g: logit(g) for g in (group if has_f else group[:1])}
                out = []
                for g in group:
                    m, l, acc = carry[g]
                    if not has_f and g + 1 < len(group):
                        logits[g + 1] = logit(g + 1)
                    s = logits[g] * scale
                    if has_f:
                        s = s + (cqs[g] - cr_ref[g, :, ks])
                    if diag is not None:
                        s = jnp.where(col + diag * t <= row, s, NEG_INF)
                    mn = jnp.maximum(m, jnp.max(s, axis=1, keepdims=True))
                    alpha = jnp.exp(m - mn)
                    p = jnp.exp(s - mn)
                    l = alpha * l + jnp.sum(p, axis=1, keepdims=True)
                    acc = alpha * acc + jnp.dot(p.astype(BF), kvbs[g], preferred_element_type=F32)
                    out.append((mn, l, acc))
                return tuple(out)

            init = tuple((jnp.full((tq, 1), NEG_INF, F32), jnp.zeros((tq, 1), F32), jnp.zeros((tq, LANE), F32))
                         for _ in group)
            carry = lax.fori_loop(0, i * per, lambda j, c: step(j, c, None), init)
            for d in range(per):
                carry = step(i * per + d, carry, d)
            for g, (m, l, acc) in enumerate(carry):
                o_ref[qs, lanes[g]] = jnp.where(lo, 0.0, acc * (1.0 / l)).astype(BF)
                lse_ref[g, qs, :] = m + jnp.log(l)
            return 0

        lax.fori_loop(0, rows // tq, q_block, 0)

    head, kv_head, shared, col_vec, row_vec = _attn_specs(rows, kv_off, group_size, has_f)
    in_specs, args = [head, kv_head], [q, kv]
    if has_kr:
        in_specs.append(shared)
        args.append(kr)
    if has_f:
        in_specs += [col_vec, row_vec]
        args += [cum_col, cum_row]
    return pl.pallas_call(
        body, out_shape=[jax.ShapeDtypeStruct((rows, heads * LANE), BF), jax.ShapeDtypeStruct((heads, rows, 1), F32)],
        grid=(heads // group_size,), in_specs=in_specs, out_specs=[head, col_vec], name=name,
        compiler_params=_params(("arbitrary",)))(*args)


def _attn_backward(q, kv, kv_off, kr, cum_col, cum_row, o, do, lse, scale, group_size, name):
    rows = q.shape[0]
    heads = HEADS
    t = ATTN_TILE
    nb = rows // t
    has_kr = kr is not None
    has_f = cum_col is not None
    group = range(group_size)

    def body(*refs):
        it = iter(refs)
        q_ref, kv_ref = next(it), next(it)
        kr_ref = next(it) if has_kr else None
        cc_ref = next(it) if has_f else None
        cr_ref = next(it) if has_f else None
        o_ref, do_ref, lse_ref = next(it), next(it), next(it)
        dq_ref, dkv_ref = next(it), next(it)
        dkr_ref = next(it) if has_kr else None
        dck_ref = next(it) if has_f else None
        dcq_ref = next(it) if has_f else None
        dq_acc = next(it)
        lo = lax.broadcasted_iota(jnp.int32, (1, LANE), 1) < HEAD_DIM
        causal = (lax.broadcasted_iota(jnp.int32, (t, t), 1) <= lax.broadcasted_iota(jnp.int32, (t, t), 0))
        lanes = [slice(g * LANE, (g + 1) * LANE) for g in group]

        dq_acc[...] = jnp.zeros_like(dq_acc)
        if has_kr:
            _init_acc(dkr_ref)
        if has_f:
            dcq_ref[...] = jnp.zeros_like(dcq_ref)

        def kv_block(j, _):
            ks = pl.ds(pl.multiple_of(j * t, t), t)
            other = kr_ref[ks, :] if has_kr else jnp.zeros((t, LANE), BF)
            kvbs = [kv_ref[ks, lanes[g]] for g in group]
            kks = [jnp.where(lo, kvbs[g], other) for g in group]
            cks = [cr_ref[g, :, ks] if has_f else None for g in group]

            def pair(i, carry, diag):
                qs = pl.ds(pl.multiple_of(i * t, t), t)
                nt = (((1,), (1,)), ((), ()))

                def first_stage(g):
                    qb = q_ref[qs, lanes[g]]
                    dob = do_ref[qs, lanes[g]]
                    return (qb, dob, lax.dot_general(qb, kks[g], nt, preferred_element_type=F32),
                            lax.dot_general(dob, kvbs[g], nt, preferred_element_type=F32))

                first = {g: first_stage(g) for g in (group[:1] if has_f else group)}
                out = []
                for g in group:
                    dkk, dvv, dcs = carry[g]
                    qb, dob, logit, dp = first[g]
                    if has_f and g + 1 < len(group):
                        first[g + 1] = first_stage(g + 1)
                    s = logit * scale
                    if has_f:
                        s = s + (cc_ref[g, qs, :] - cks[g])
                    if diag:
                        s = jnp.where(causal, s, NEG_INF)
                    p = jnp.exp(s - lse_ref[g, qs, :])
                    delta = jnp.sum(dob.astype(F32) * o_ref[qs, lanes[g]].astype(F32), axis=1, keepdims=True)
                    ds = p * (dp - delta)
                    dsb = ds.astype(BF)
                    dvv = dvv + lax.dot_general(p.astype(BF), dob, (((0,), (0,)), ((), ())), preferred_element_type=F32)
                    dkk = dkk + lax.dot_general(dsb, qb, (((0,), (0,)), ((), ())), preferred_element_type=F32)
                    dq_acc[qs, lanes[g]] += jnp.dot(dsb, kks[g], preferred_element_type=F32)
                    if has_f:
                        dcs = dcs + jnp.sum(ds, axis=0, keepdims=True)
                        dcq_ref[g, qs, :] += jnp.sum(ds, axis=1, keepdims=True)
                    out.append((dkk, dvv, dcs))
                return tuple(out)

            init = tuple((jnp.zeros((t, LANE), F32), jnp.zeros((t, LANE), F32), jnp.zeros((1, t), F32)) for _ in group)
            carry = pair(j, init, True)
            carry = lax.fori_loop(j + 1, nb, lambda i, c: pair(i, c, False), carry)
            for g, (dkk, dvv, dcs) in enumerate(carry):
                dkk = dkk * scale
                dkv_ref[ks, lanes[g]] = jnp.where(lo, dkk, dvv).astype(BF)
                if has_kr:
                    dkr_ref[ks, :] += jnp.where(lo, 0.0, dkk)
                if has_f:
                    dck_ref[g, :, ks] = -dcs
            return 0

        lax.fori_loop(0, nb, kv_block, 0)
        dq_ref[...] = (dq_acc[...] * scale).astype(BF)

    head, kv_head, shared, col_vec, row_vec = _attn_specs(rows, kv_off, group_size, has_f)
    in_specs, args = [head, kv_head], [q, kv]
    if has_kr:
        in_specs.append(shared)
        args.append(kr)
    if has_f:
        in_specs += [col_vec, row_vec]
        args += [cum_col, cum_row]
    in_specs += [head, head, col_vec]
    args += [o, do, lse]
    out_shape = [jax.ShapeDtypeStruct((rows, heads * LANE), BF), jax.ShapeDtypeStruct((rows, heads * LANE), BF)]
    out_specs = [head, head]
    if has_kr:
        out_shape.append(jax.ShapeDtypeStruct((rows, LANE), F32))
        out_specs.append(shared)
    if has_f:
        out_shape += [jax.ShapeDtypeStruct((heads, 1, rows), F32), jax.ShapeDtypeStruct((heads, rows, 1), F32)]
        out_specs += [row_vec, col_vec]
    return pl.pallas_call(
        body, out_shape=out_shape, grid=(heads // group_size,), in_specs=in_specs, out_specs=out_specs,
        scratch_shapes=[pltpu.VMEM((rows, group_size * LANE), F32)], name=name,
        compiler_params=_params(("arbitrary",)))(*args)


def _tri_dot(tri, x):
    return jnp.dot(tri, x, preferred_element_type=F32, precision=lax.Precision.HIGHEST)


def _forget_forward(f_raw, b_f):
    rows = f_raw.shape[0]
    t = ATTN_TILE

    def body(f_ref, b_ref, cum_ref):
        tri = (lax.broadcasted_iota(jnp.int32, (t, t), 1) <= lax.broadcasted_iota(jnp.int32, (t, t), 0)).astype(F32)

        def blk(i, carry):
            sl = pl.ds(pl.multiple_of(i * t, t), t)
            xv = f_ref[sl, :] + b_ref[...]
            log_f = jnp.minimum(xv, 0.0) - jnp.log(1.0 + jnp.exp(-jnp.abs(xv)))
            cum_ref[sl, :] = _tri_dot(tri, log_f) + carry
            return carry + jnp.sum(log_f, axis=0, keepdims=True)

        lax.fori_loop(0, rows // t, blk, jnp.zeros((1, LANE), F32))

    return pl.pallas_call(body, out_shape=jax.ShapeDtypeStruct((rows, LANE), F32), name="forget_forward",
                          compiler_params=_params())(f_raw, b_f)


def _forget_backward(f_raw, b_f, dcum):
    rows = f_raw.shape[0]
    t = ATTN_TILE
    nb = rows // t

    def body(f_ref, b_ref, dc_ref, df_ref, db_ref):
        tri = (lax.broadcasted_iota(jnp.int32, (t, t), 1) >= lax.broadcasted_iota(jnp.int32, (t, t), 0)).astype(F32)

        def blk(i, carry):
            later, db = carry
            sl = pl.ds(pl.multiple_of((nb - 1 - i) * t, t), t)
            dc = dc_ref[sl, :]
            dlog = _tri_dot(tri, dc) + later
            xv = f_ref[sl, :] + b_ref[...]
            df = dlog / (1.0 + jnp.exp(xv))
            df_ref[sl, :] = df.astype(BF)
            return later + jnp.sum(dc, axis=0, keepdims=True), db + jnp.sum(df, axis=0, keepdims=True)

        _, db = lax.fori_loop(0, nb, blk, (jnp.zeros((1, LANE), F32), jnp.zeros((1, LANE), F32)))
        db_ref[...] = db

    return pl.pallas_call(body, out_shape=[jax.ShapeDtypeStruct((rows, LANE), BF), jax.ShapeDtypeStruct((1, LANE), F32)],
                          name="forget_backward", compiler_params=_params())(f_raw, b_f, dcum)


def _t5_bucket(dist):
    max_exact = REL_BUCKETS // 2
    n = jnp.maximum(dist.astype(F32), 1.0)
    large = max_exact + (jnp.log(n / max_exact) / math.log(REL_MAX_DIST / max_exact)
                         * (REL_BUCKETS - max_exact)).astype(jnp.int32)
    large = jnp.minimum(large, REL_BUCKETS - 1)
    return jnp.where(dist < max_exact, dist, large)


def _dil_buckets(dilation):
    i = jnp.arange(Q_BLOCK)[:, None]
    j = jnp.arange(Q_BLOCK)[None, :]
    cur = _t5_bucket(jnp.clip(i - j, 0) * dilation).astype(jnp.int32)
    prev = _t5_bucket(jnp.clip(Q_BLOCK + i - j, 0) * dilation).astype(jnp.int32)
    return cur, prev


def _dil_bias_tiles(tbl_ref, bc_ref, bp_ref, bias_ref, group, hp):
    ii = lax.broadcasted_iota(jnp.int32, (Q_BLOCK, Q_BLOCK), 0)
    jj = lax.broadcasted_iota(jnp.int32, (Q_BLOCK, Q_BLOCK), 1)
    for hh in range(2 * DIL_PAIRS):
        col = group * HEADS + 2 * DIL_PAIRS * hp + hh
        acc_c = jnp.zeros((Q_BLOCK, Q_BLOCK), F32)
        acc_p = jnp.zeros((Q_BLOCK, Q_BLOCK), F32)
        for b in range(REL_BUCKETS):
            val = tbl_ref[b, col]
            acc_c = jnp.where(bc_ref[...] == b, val, acc_c)
            acc_p = jnp.where(bp_ref[...] == b, val, acc_p)
        bias_ref[2 * hh] = jnp.where(jj <= ii, acc_c, NEG_INF)
        bias_ref[2 * hh + 1] = jnp.where(jj >= ii, acc_p, NEG_INF)


def _dil_view(qkv, group, dilation):
    if dilation == 1:
        return qkv
    width = 3 * HEADS * HEAD_DIM
    return qkv[:, group * width:(group + 1) * width].reshape(qkv.shape[0] // dilation, dilation * width)


def _dil_specs(group, dilation, length):
    width = DIL_PAIRS * LANE
    per = 8 // DIL_PAIRS

    def col(kind):
        if dilation == 1:
            return pl.BlockSpec((length, width), lambda hp, r: (0, (group * 3 + kind) * per + hp))
        return pl.BlockSpec((length, width), lambda hp, r: (0, (r * 3 + kind) * per + hp))

    out = pl.BlockSpec((length, width), lambda hp, r: (0, r * per + hp))
    tile = pl.BlockSpec((Q_BLOCK, Q_BLOCK), lambda hp, r: (0, 0))
    table = pl.BlockSpec(memory_space=pltpu.SMEM)
    return col, out, tile, table


def _dil_forward(view, group, dilation, table, buckets):
    length = view.shape[0]
    rows = length * dilation
    nb = length // Q_BLOCK
    scale = HEAD_DIM ** -0.5
    qb = Q_BLOCK

    def body(tbl_ref, bc_ref, bp_ref, q_ref, k_ref, v_ref, o_ref, lse_ref, bias_ref):
        hp = pl.program_id(0)

        @pl.when(pl.program_id(1) == 0)
        def _():
            _dil_bias_tiles(tbl_ref, bc_ref, bp_ref, bias_ref, group, hp)

        lo = lax.broadcasted_iota(jnp.int32, (1, LANE), 1) < HEAD_DIM
        nt = (((1,), (1,)), ((), ()))

        def blk(n, first):
            cur = pl.ds(0, qb) if first else pl.ds(pl.multiple_of(n * qb, qb), qb)
            prev = None if first else pl.ds(pl.multiple_of((n - 1) * qb, qb), qb)
            logits = []
            for pair in range(DIL_PAIRS):
                lanes = slice(pair * LANE, (pair + 1) * LANE)
                qn = q_ref[cur, lanes] * scale
                for hh in range(2):
                    qm = jnp.where(lo if hh == 0 else ~lo, qn, jnp.zeros_like(qn))
                    s_c = lax.dot_general(qm, k_ref[cur, lanes], nt, preferred_element_type=F32)
                    s_p = None if first else lax.dot_general(qm, k_ref[prev, lanes], nt, preferred_element_type=F32)
                    logits.append((s_c, s_p))
            for pair in range(DIL_PAIRS):
                lanes = slice(pair * LANE, (pair + 1) * LANE)
                outs, lses = [], []
                for hh in range(2):
                    bias = 4 * pair + 2 * hh
                    s_c, s_p = logits[2 * pair + hh]
                    s_c = s_c + bias_ref[bias]
                    m = jnp.max(s_c, axis=1, keepdims=True)
                    if not first:
                        s_p = s_p + bias_ref[bias + 1]
                        m = jnp.maximum(m, jnp.max(s_p, axis=1, keepdims=True))
                    e_c = jnp.exp(s_c - m)
                    l = jnp.sum(e_c, axis=1, keepdims=True)
                    acc = jnp.dot(e_c.astype(BF), v_ref[cur, lanes], preferred_element_type=F32)
                    if not first:
                        e_p = jnp.exp(s_p - m)
                        l = l + jnp.sum(e_p, axis=1, keepdims=True)
                        acc = acc + jnp.dot(e_p.astype(BF), v_ref[prev, lanes], preferred_element_type=F32)
                    outs.append(acc * (1.0 / l))
                    lses.append(m + jnp.log(l))
                o_ref[cur, lanes] = jnp.where(lo, outs[0], outs[1])
                lse_ref[cur, lanes] = jnp.where(lo, lses[0], lses[1])
            return 0

        blk(0, True)
        if nb > 1:
            lax.fori_loop(1, nb, lambda n, _: blk(n, False), 0)

    col, out, tile, tbl = _dil_specs(group, dilation, length)
    bc, bp = buckets
    o, lse = pl.pallas_call(
        body, out_shape=[jax.ShapeDtypeStruct((length, dilation * D_MODEL), F32)] * 2,
        grid=(8 // DIL_PAIRS, dilation), in_specs=[tbl, tile, tile, col(0), col(1), col(2)], out_specs=[out, out],
        scratch_shapes=[pltpu.VMEM((4 * DIL_PAIRS, qb, qb), F32)], name=f"dilated_forward_{dilation}",
        compiler_params=_params(("arbitrary", "arbitrary")))(
            table, bc, bp, view, view, view)
    return o.reshape(rows, D_MODEL), lse.reshape(rows, D_MODEL)


def _dil_backward(view, group, dilation, table, buckets, do_g, lse, dlt):
    length = view.shape[0]
    rows = length * dilation
    nb = length // Q_BLOCK
    scale = HEAD_DIM ** -0.5
    qb = Q_BLOCK

    def body(tbl_ref, bc_ref, bp_ref, q_ref, k_ref, v_ref, do_ref, lse_ref, dlt_ref,
             dq_ref, dk_ref, dv_ref, db_ref, bias_ref, dk_acc, dv_acc):
        hp = pl.program_id(0)

        @pl.when(pl.program_id(1) == 0)
        def _():
            _dil_bias_tiles(tbl_ref, bc_ref, bp_ref, bias_ref, group, hp)
            db_ref[...] = jnp.zeros_like(db_ref)

        dk_acc[...] = jnp.zeros_like(dk_acc)
        dv_acc[...] = jnp.zeros_like(dv_acc)
        lo = lax.broadcasted_iota(jnp.int32, (1, LANE), 1) < HEAD_DIM
        tn = (((0,), (0,)), ((), ()))
        nt = (((1,), (1,)), ((), ()))

        def blk(n, first):
            cur = pl.ds(0, qb) if first else pl.ds(pl.multiple_of(n * qb, qb), qb)
            prev = None if first else pl.ds(pl.multiple_of((n - 1) * qb, qb), qb)
            inputs = []
            for pair in range(DIL_PAIRS):
                lanes = slice(pair * LANE, (pair + 1) * LANE)
                qn = q_ref[cur, lanes] * scale
                don = do_ref[cur, lanes]
                for hh in range(2):
                    mask = lo if hh == 0 else ~lo
                    qm = jnp.where(mask, qn, jnp.zeros_like(qn))
                    dom = jnp.where(mask, don, jnp.zeros_like(don))
                    stage = [qm, dom, lax.dot_general(qm, k_ref[cur, lanes], nt, preferred_element_type=F32),
                             lax.dot_general(dom, v_ref[cur, lanes], nt, preferred_element_type=F32)]
                    if not first:
                        stage += [lax.dot_general(qm, k_ref[prev, lanes], nt, preferred_element_type=F32),
                                  lax.dot_general(dom, v_ref[prev, lanes], nt, preferred_element_type=F32)]
                    inputs.append(stage)
            for pair in range(DIL_PAIRS):
                lanes = slice(pair * LANE, (pair + 1) * LANE)
                kc = k_ref[cur, lanes]
                if not first:
                    kp = k_ref[prev, lanes]
                lse_n = lse_ref[cur, lanes]
                dlt_n = dlt_ref[cur, lanes]
                dqs = []
                dkc = jnp.zeros((qb, LANE), F32)
                dkp = jnp.zeros((qb, LANE), F32)
                dvc = jnp.zeros((qb, LANE), F32)
                dvp = jnp.zeros((qb, LANE), F32)
                for hh in range(2):
                    bias = 4 * pair + 2 * hh
                    mask = lo if hh == 0 else ~lo
                    qm, dom, s_c, dp_c = inputs[2 * pair + hh][:4]
                    lse_h = jnp.max(jnp.where(mask, lse_n, -3e38), axis=1, keepdims=True)
                    dlt_h = jnp.max(jnp.where(mask, dlt_n, -3e38), axis=1, keepdims=True)
                    p_c = jnp.exp(s_c + bias_ref[bias] - lse_h)
                    ds_c = p_c * (dp_c - dlt_h)
                    db_ref[pair, 2 * hh] += ds_c
                    dsc_b = ds_c.astype(BF)
                    dq = jnp.dot(dsc_b, kc, preferred_element_type=F32)
                    dkc = dkc + lax.dot_general(dsc_b, qm, tn, preferred_element_type=F32)
                    dvc = dvc + lax.dot_general(p_c.astype(BF), dom, tn, preferred_element_type=F32)
                    if not first:
                        s_p, dp_p = inputs[2 * pair + hh][4:]
                        p_p = jnp.exp(s_p + bias_ref[bias + 1] - lse_h)
                        ds_p = p_p * (dp_p - dlt_h)
                        db_ref[pair, 2 * hh + 1] += ds_p
                        dsp_b = ds_p.astype(BF)
                        dq = dq + jnp.dot(dsp_b, kp, preferred_element_type=F32)
                        dkp = dkp + lax.dot_general(dsp_b, qm, tn, preferred_element_type=F32)
                        dvp = dvp + lax.dot_general(p_p.astype(BF), dom, tn, preferred_element_type=F32)
                    dqs.append(dq)
                dq_ref[cur, lanes] = (jnp.where(lo, dqs[0], dqs[1]) * scale).astype(BF)
                dk_acc[cur, lanes] += dkc
                dv_acc[cur, lanes] += dvc
                if not first:
                    dk_acc[prev, lanes] += dkp
                    dv_acc[prev, lanes] += dvp
            return 0

        blk(0, True)
        if nb > 1:
            lax.fori_loop(1, nb, lambda n, _: blk(n, False), 0)
        dk_ref[...] = dk_acc[...].astype(BF)
        dv_ref[...] = dv_acc[...].astype(BF)

    col, out, tile, tbl = _dil_specs(group, dilation, length)
    bc, bp = buckets
    wide = (length, dilation * D_MODEL)
    dq, dk, dv, db = pl.pallas_call(
        body, out_shape=[jax.ShapeDtypeStruct(wide, BF)] * 3 + [jax.ShapeDtypeStruct((8, 4, qb, qb), F32)],
        grid=(8 // DIL_PAIRS, dilation), in_specs=[tbl, tile, tile, col(0), col(1), col(2), out, out, out],
        out_specs=[out, out, out, pl.BlockSpec((DIL_PAIRS, 4, qb, qb), lambda hp, r: (hp, 0, 0, 0))],
        scratch_shapes=[pltpu.VMEM((4 * DIL_PAIRS, qb, qb), F32), pltpu.VMEM((length, DIL_PAIRS * LANE), F32),
                        pltpu.VMEM((length, DIL_PAIRS * LANE), F32)],
        name=f"dilated_backward_{dilation}", compiler_params=_params(("arbitrary", "arbitrary")))(
            table, bc, bp, view, view, view,
            do_g.reshape(wide), lse.reshape(wide), dlt.reshape(wide))
    return dq.reshape(rows, D_MODEL), dk.reshape(rows, D_MODEL), dv.reshape(rows, D_MODEL), db


def _head_sums(x, lo):
    s0 = jnp.sum(jnp.where(lo, x, 0.0), axis=1, keepdims=True)
    s1 = jnp.sum(jnp.where(lo, 0.0, x), axis=1, keepdims=True)
    return jnp.where(lo, s0, s1)


def _dil_merge_forward(outs, lses):
    rows = outs[0].shape[0]

    def body(o0, o1, o2, l0, l1, l2, o_ref):
        ls = [l0[...], l1[...], l2[...]]
        m = jnp.maximum(jnp.maximum(ls[0], ls[1]), ls[2])
        es = [jnp.exp(v - m) for v in ls]
        tot = es[0] + es[1] + es[2]
        o_ref[...] = ((es[0] * o0[...] + es[1] * o1[...] + es[2] * o2[...]) / tot).astype(BF)

    blk = pl.BlockSpec((ROW_TILE, LANE), lambda i, j: (i, j))
    return pl.pallas_call(body, out_shape=jax.ShapeDtypeStruct((rows, D_MODEL), BF), grid=(rows // ROW_TILE, 8),
                          in_specs=[blk] * 6, out_specs=blk, name="dilated_merge_forward",
                          compiler_params=_params(("parallel", "parallel")))(*outs, *lses)


def _dil_merge_backward(outs, lses, do):
    rows = outs[0].shape[0]

    def body(o0, o1, o2, l0, l1, l2, do_ref, d0, d1, d2, t0, t1, t2):
        lo = lax.broadcasted_iota(jnp.int32, (1, LANE), 1) < HEAD_DIM
        ls = [l0[...], l1[...], l2[...]]
        os_ = [o0[...], o1[...], o2[...]]
        m = jnp.maximum(jnp.maximum(ls[0], ls[1]), ls[2])
        es = [jnp.exp(v - m) for v in ls]
        inv = 1.0 / (es[0] + es[1] + es[2])
        alphas = [e * inv for e in es]
        dov = do_ref[...]
        merged = alphas[0] * os_[0] + alphas[1] * os_[1] + alphas[2] * os_[2]
        dot = _head_sums(dov * merged, lo)
        for a, d_ref, t_ref in zip(alphas, (d0, d1, d2), (t0, t1, t2)):
            d_ref[...] = (a * dov).astype(BF)
            t_ref[...] = a * dot

    blk = pl.BlockSpec((ROW_TILE, LANE), lambda i, j: (i, j))
    res = pl.pallas_call(
        body, out_shape=[jax.ShapeDtypeStruct((rows, D_MODEL), BF)] * 3 + [jax.ShapeDtypeStruct((rows, D_MODEL), F32)] * 3,
        grid=(rows // ROW_TILE, 8), in_specs=[blk] * 7, out_specs=[blk] * 6, name="dilated_merge_backward",
        compiler_params=_params(("parallel", "parallel")))(*outs, *lses, do)
    return res[:3], res[3:]


def _rel_bias_grad(dbs, buckets):
    def body(db_ref, bc_ref, bp_ref, o_ref):
        g = pl.program_id(0)
        hp = pl.program_id(1)

        @pl.when((g == 0) & (hp == 0))
        def _():
            o_ref[...] = jnp.zeros_like(o_ref)

        rr = lax.broadcasted_iota(jnp.int32, (REL_BUCKETS, LANE), 0)
        cc = lax.broadcasted_iota(jnp.int32, (REL_BUCKETS, LANE), 1)
        bc = bc_ref[0]
        bp = bp_ref[0]
        acc = jnp.zeros((REL_BUCKETS, LANE), F32)
        for hh in range(2):
            col = g * HEADS + 2 * hp + hh
            d_c = db_ref[0, 0, 2 * hh]
            d_p = db_ref[0, 0, 2 * hh + 1]
            for b in range(REL_BUCKETS):
                val = (jnp.sum(jnp.where(bc == b, d_c, 0.0), keepdims=True)
                       + jnp.sum(jnp.where(bp == b, d_p, 0.0), keepdims=True))
                acc = jnp.where((rr == b) & (cc == col), val, acc)
        o_ref[...] += acc

    db_all = jnp.stack(dbs)
    bc_all = jnp.stack([b[0] for b in buckets])
    bp_all = jnp.stack([b[1] for b in buckets])
    tile = pl.BlockSpec((1, Q_BLOCK, Q_BLOCK), lambda g, hp: (g, 0, 0))
    return pl.pallas_call(
        body, out_shape=jax.ShapeDtypeStruct((REL_BUCKETS, LANE), F32), grid=(3, 8),
        in_specs=[pl.BlockSpec((1, 1, 4, Q_BLOCK, Q_BLOCK), lambda g, hp: (g, hp, 0, 0, 0)), tile, tile],
        out_specs=pl.BlockSpec((REL_BUCKETS, LANE), lambda g, hp: (0, 0)), name="rel_bias_grad",
        compiler_params=_params(("arbitrary", "arbitrary")))(db_all, bc_all, bp_all)


def _mla_forward(hn, w, tables):
    a = _matmul(hn, w["w_a"], name="mla_a")
    cq, ckv, kr = _mla_mid_forward(a, w["q_norm"], w["kv_norm"], tables)
    q_raw = _matmul(cq, w["w_uq"], name="mla_uq")
    q = _rope_heads(q_raw, tables, False, "rope_forward")
    kv = _matmul(ckv, w["w_ukv"], b_chunks=True, out_dtype=BF, name="mla_ukv")
    scale = (HEAD_DIM + MLA_ROPE) ** -0.5
    o, lse = _attn_forward(q, kv, 0, kr, None, None, scale, FORWARD_GROUP, "mla_attention_forward")
    y = _matmul(o, w["w_o"], name="attn_out")
    return y, dict(hn=hn, a=a, cq=cq, ckv=ckv, kr=kr, q=q, kv=kv, o=o, lse=lse)


def _mla_backward(dy, w, s, tables):
    scale = (HEAD_DIM + MLA_ROPE) ** -0.5
    g = {}
    g["w_o"] = _matmul(s["o"], dy, ta=True, out_dtype=BF, name="attn_out_dw")
    do = _matmul(dy, w["w_o"], tb=True, out_dtype=BF, name="attn_out_dx")
    dq, dkv, dkr = _attn_backward(s["q"], s["kv"], 0, s["kr"], None, None, s["o"], do, s["lse"], scale,
                                  MLA_GROUP, "mla_attention_backward")
    dq_raw = _rope_heads(dq, tables, True, "rope_backward")
    g["w_uq"] = _matmul(s["cq"], dq_raw, ta=True, out_dtype=BF, name="mla_uq_dw")
    dcq = _matmul(dq_raw, w["w_uq"], tb=True, name="mla_uq_dx")
    g["w_ukv"] = _matmul(s["ckv"], dkv, ta=True, out_chunks=True, out_dtype=BF, name="mla_ukv_dw")
    dckv = _matmul(dkv, w["w_ukv"], tb=True, b_chunks=True, name="mla_ukv_dx")
    da, g["q_norm"], g["kv_norm"] = _mla_mid_backward(s["a"], w["q_norm"], w["kv_norm"], tables, dcq, dckv, dkr)
    g["w_a"] = _matmul(s["hn"], da, ta=True, out_dtype=BF, name="mla_a_dw")
    dhn = _matmul(da, w["w_a"], tb=True, name="mla_a_dx")
    return dhn, g


def _fox_forward(hn, w):
    qkv = _matmul(hn, w["w_qkv"], out_dtype=BF, name="fox_qkv")
    f_raw = _matmul(hn, w["w_f"], name="fox_f")
    cum = _forget_forward(f_raw, w["b_f"])
    cum_heads = cum[:, :HEADS].T
    cum_col, cum_row = cum_heads[:, :, None], cum_heads[:, None, :]
    o, lse = _attn_forward(qkv, qkv, HEADS, None, cum_col, cum_row, HEAD_DIM ** -0.5, FORWARD_GROUP,
                           "fox_attention_forward")
    y = _matmul(o, w["w_o"], name="attn_out")
    return y, dict(hn=hn, qkv=qkv, f_raw=f_raw, cum_col=cum_col, cum_row=cum_row, o=o, lse=lse)


def _fox_backward(dy, w, s):
    g = {}
    g["w_o"] = _matmul(s["o"], dy, ta=True, out_dtype=BF, name="attn_out_dw")
    do = _matmul(dy, w["w_o"], tb=True, out_dtype=BF, name="attn_out_dx")
    dq, dkv, dck, dcq = _attn_backward(s["qkv"], s["qkv"], HEADS, None, s["cum_col"], s["cum_row"], s["o"], do,
                                       s["lse"], HEAD_DIM ** -0.5, FOX_GROUP, "fox_attention_backward")
    dcum = jnp.pad((dck[:, 0, :] + dcq[:, :, 0]).T, ((0, 0), (0, LANE - HEADS)))
    df, g["b_f"] = _forget_backward(s["f_raw"], w["b_f"], dcum)
    dqkv = jnp.concatenate([dq, dkv], axis=1)
    g["w_qkv"] = _matmul(s["hn"], dqkv, ta=True, out_dtype=BF, name="fox_qkv_dw")
    g["w_f"] = _matmul(s["hn"], df, ta=True, out_dtype=BF, name="fox_f_dw")
    dhn = _matmul(dqkv, w["w_qkv"], tb=True, name="fox_qkv_dx")
    dhn = _matmul(df, w["w_f"], tb=True, add=dhn, name="fox_f_dx")
    return dhn, g


def _dil_mixer_forward(hn, w, buckets):
    qkv = _matmul(hn, w["w_qkv"], b_chunks=True, out_dtype=BF, name="dil_qkv")
    views = [_dil_view(qkv, grp, dilation) for grp, (_, dilation) in enumerate(DIL_PATTERNS)]
    outs, lses = [], []
    for grp, (_, dilation) in enumerate(DIL_PATTERNS):
        o_g, lse_g = _dil_forward(views[grp], grp, dilation, w["rel_bias"], buckets[grp])
        outs.append(o_g)
        lses.append(lse_g)
    o = _dil_merge_forward(outs, lses)
    y = _matmul(o, w["w_o"], name="dil_out")
    return y, dict(hn=hn, views=views, outs=outs, lses=lses, o=o)


def _dil_mixer_backward(dy, w, s, buckets):
    g = {}
    g["w_o"] = _matmul(s["o"], dy, ta=True, out_dtype=BF, name="dil_out_dw")
    do = _matmul(dy, w["w_o"], tb=True, name="dil_out_dx")
    do_gs, dlts = _dil_merge_backward(s["outs"], s["lses"], do)
    parts, dbs = [], []
    for grp, (_, dilation) in enumerate(DIL_PATTERNS):
        dq, dk, dv, db = _dil_backward(s["views"][grp], grp, dilation, w["rel_bias"], buckets[grp], do_gs[grp],
                                       s["lses"][grp], dlts[grp])
        parts += [dq, dk, dv]
        dbs.append(db)
    dqkv = jnp.concatenate(parts, axis=1)
    g["rel_bias"] = _rel_bias_grad(dbs, buckets)
    g["w_qkv"] = _matmul(s["hn"], dqkv, ta=True, out_chunks=True, out_dtype=BF, name="dil_qkv_dw")
    dhn = _matmul(dqkv, w["w_qkv"], tb=True, b_chunks=True, name="dil_qkv_dx")
    return dhn, g


def _mixer_weights(i, lw, small):
    mixer, j = i % N_MIXERS, i // N_MIXERS
    if mixer == 0:
        return dict(lw["mixer"], q_norm=small["mla_q_norm"][j][None, :], kv_norm=small["mla_kv_norm"][j][None, :])
    if mixer == 1:
        return dict(lw["mixer"], rel_bias=small["rel_bias"])
    return dict(lw["mixer"], b_f=jnp.pad(small["fox_b_f"][j][None, :], ((0, 0), (0, LANE - HEADS))))


MIXER_PART, COMMON_PART = 0, 1


def _run_layers(x, p, positions, target, get_part, get_small, put_part):
    tables = _rope_tables(positions)
    buckets = [_dil_buckets(d) for _, d in DIL_PATTERNS]
    layers, saved = [], []
    h = x
    first = get_part(0, MIXER_PART, positions)
    small = get_small()

    def gain(i, k):
        return small["norm_g"][i, k][None, :]

    hn = _prenorm(h, gain(0, 0))
    sq = dh = None
    for i in range(DEPTH):
        mixer = i % N_MIXERS
        lw = dict(first if i == 0 else get_part(i, MIXER_PART, h))
        mw = _mixer_weights(i, lw, small)
        if mixer == 0:
            y, ms = _mla_forward(hn, mw, tables)
        elif mixer == 1:
            y, ms = _dil_mixer_forward(hn, mw, buckets)
        else:
            y, ms = _fox_forward(hn, mw)
        if "ffn_w_in" not in lw:
            lw.update(get_part(i, COMMON_PART, y))
        layers.append(lw)
        h1, hn2 = _post_residual(h, y, gain(i, 1), gain(i, 2))
        gu = _matmul(hn2, lw["ffn_w_in"], b_chunks=True, out_dtype=BF, name="ffn_in")
        act = _swiglu_forward(gu)
        f = _matmul(act, lw["ffn_w_out"], name="ffn_out")
        h2, h2b = _post_residual(h1, f, gain(i, 3), None)
        pp = _matmul(p[i], lw["ple_w_proj"], b_chunks=True, name="ple_proj")
        z = _matmul(h2b, lw["ple_w_gate"], name="ple_gate")
        saved.append(dict(h=h, y=y, ms=ms, h1=h1, hn2=hn2, gu=gu, act=act, f=f, h2b=h2b, pp=pp, z=z))
        if i + 1 < DEPTH:
            h, hn = _ple_forward(h2, pp, z, gain(i + 1, 0))
        else:
            dh, sq = _ple_loss(h2, pp, z, target)

    norm_rows = [[None] * 4 for _ in range(DEPTH)]
    sg = dict(mla_q_norm={}, mla_kv_norm={}, rel_bias=None, fox_b_f={})
    for i in reversed(range(DEPTH)):
        s, lw = saved[i], layers[i]
        mixer, j = i % N_MIXERS, i // N_MIXERS
        mw = _mixer_weights(i, lw, small)
        lg = {}
        dpp, dz = _ple_backward(dh, s["pp"], s["z"])
        lg["ple_w_proj"] = _matmul(p[i], dpp, ta=True, out_chunks=True, out_dtype=BF, name="ple_proj_dw")
        lg["ple_w_gate"] = _matmul(s["h2b"], dz, ta=True, out_dtype=BF, name="ple_gate_dw")
        dh2 = _matmul(dz, lw["ple_w_gate"], tb=True, add=dh, name="ple_gate_dx")
        df, norm_rows[i][3] = _rms_backward(s["f"], gain(i, 3), dh2, None, BF)
        lg["ffn_w_out"] = _matmul(s["act"], df, ta=True, out_dtype=BF, name="ffn_out_dw")
        dact = _matmul(df, lw["ffn_w_out"], tb=True, out_dtype=BF, name="ffn_out_dx")
        dgu = _swiglu_backward(s["gu"], dact)
        lg["ffn_w_in"] = _matmul(s["hn2"], dgu, ta=True, out_chunks=True, out_dtype=BF, name="ffn_in_dw")
        split = i in SPLIT_LAYERS
        zero = put_part(i, COMMON_PART, lg)[0:1, 0:1] if split else 0.0
        dhn2 = _matmul(dgu, lw["ffn_w_in"], tb=True, b_chunks=True, name="ffn_in_dx")
        dh1, norm_rows[i][2] = _rms_backward(s["h1"], gain(i, 2), dhn2, dh2, F32)
        dy, norm_rows[i][1] = _rms_backward(s["y"], gain(i, 1) + zero, dh1, None, BF)
        if mixer == 0:
            dhn, mg = _mla_backward(dy, mw, s["ms"], tables)
            sg["mla_q_norm"][j] = mg.pop("q_norm")
            sg["mla_kv_norm"][j] = mg.pop("kv_norm")
        elif mixer == 1:
            dhn, mg = _dil_mixer_backward(dy, mw, s["ms"], buckets)
            rel = mg.pop("rel_bias")[:, :3 * HEADS]
            sg["rel_bias"] = rel if sg["rel_bias"] is None else sg["rel_bias"] + rel
        else:
            dhn, mg = _fox_backward(dy, mw, s["ms"])
            sg["fox_b_f"][j] = mg.pop("b_f")[:, :HEADS]
        token = put_part(i, MIXER_PART, dict(mixer=mg) if split else dict(lg, mixer=mg))
        dh, norm_rows[i][0] = _rms_backward(s["h"], gain(i, 0) + token[0:1, 0:1], dhn, dh1, F32)
    small_grads = dict(norm_g=jnp.stack([jnp.concatenate(row, axis=0) for row in norm_rows]),
                       rel_bias=sg["rel_bias"])
    for k in ("mla_q_norm", "mla_kv_norm", "fox_b_f"):
        small_grads[k] = jnp.concatenate([sg[k][j] for j in sorted(sg[k])], axis=0)
    return sq, dh, small_grads


BIG = ("ffn_w_in", "ffn_w_out", "ple_w_proj", "ple_w_gate", "mla_w_a", "mla_w_uq", "mla_w_ukv", "mla_w_o",
       "dil_w_qkv", "dil_w_o", "fox_w_qkvf", "fox_w_o")
SMALL_SHARDED = ("norm_g", "mla_q_norm", "mla_kv_norm")
SMALL_REPLICATED = ("rel_bias", "fox_b_f")
WEIGHTS = ("norm_g", "ffn_w_in", "ffn_w_out", "ple_w_proj", "ple_w_gate", "rel_bias", "mla_w_a", "mla_q_norm",
           "mla_kv_norm", "mla_w_uq", "mla_w_ukv", "mla_w_o", "dil_w_qkv", "dil_w_o", "fox_w_qkvf", "fox_b_f", "fox_w_o")


TRANSPOSED = "fox_w_qkvf"
SPLIT_LAYERS = (0,)
LAYER_COMMON = ("ffn_w_in", "ffn_w_out", "ple_w_proj", "ple_w_gate")
MIXER_WEIGHTS = (("mla_w_a", "mla_w_uq", "mla_w_ukv", "mla_w_o"), ("dil_w_qkv", "dil_w_o"), ("fox_w_qkvf", "fox_w_o"))


def _part_names(i, part):
    if i in SPLIT_LAYERS:
        return MIXER_WEIGHTS[i % N_MIXERS] if part == MIXER_PART else LAYER_COMMON
    return MIXER_WEIGHTS[i % N_MIXERS] + LAYER_COMMON if part == MIXER_PART else ()


def _layer_slot(name, i):
    return i if name in LAYER_COMMON else i // N_MIXERS


def _merge_rows(chunks):
    n, r, c = chunks.shape
    return chunks.reshape(n * r, c)


def _merge_cols(chunks):
    n, r, c = chunks.shape
    return chunks.transpose(1, 0, 2).reshape(r, n * c)


def _pad_heads_out(wo):
    w3 = wo.reshape(HEADS, HEAD_DIM, D_MODEL)
    return jnp.pad(w3, ((0, 0), (HEAD_DIM, 0), (0, 0))).reshape(HEADS * LANE, D_MODEL)


def _part_to_compute(i, part, ch):
    lw = {}
    if "ffn_w_in" in ch:
        lw.update(ffn_w_in=ch["ffn_w_in"], ffn_w_out=_merge_rows(ch["ffn_w_out"]), ple_w_proj=ch["ple_w_proj"],
                  ple_w_gate=_merge_rows(ch["ple_w_gate"]))
    if part == COMMON_PART:
        return lw
    mixer = i % N_MIXERS
    if mixer == 0:
        wa = _merge_rows(ch["mla_w_a"])
        rank = MLA_Q_RANK + MLA_KV_RANK
        wa_p = jnp.concatenate([wa[:, :rank], jnp.zeros((wa.shape[0], 64), wa.dtype), wa[:, rank:],
                                jnp.zeros((wa.shape[0], 32), wa.dtype)], axis=1)
        wuq = _merge_cols(ch["mla_w_uq"]).reshape(MLA_Q_RANK, HEADS, HEAD_DIM + MLA_ROPE)
        wuq_p = jnp.pad(wuq, ((0, 0), (0, 0), (0, LANE - HEAD_DIM - MLA_ROPE))).reshape(MLA_Q_RANK, HEADS * LANE)
        lw["mixer"] = dict(w_a=wa_p, w_uq=wuq_p, w_ukv=ch["mla_w_ukv"], w_o=_pad_heads_out(_merge_rows(ch["mla_w_o"])))
    elif mixer == 1:
        lw["mixer"] = dict(w_qkv=ch["dil_w_qkv"], w_o=_merge_rows(ch["dil_w_o"]))
    else:
        wf = _merge_rows(ch["fox_w_qkvf"]).T
        inner = HEADS * HEAD_DIM
        q3 = wf[:, :inner].reshape(D_MODEL, HEADS, HEAD_DIM)
        k3 = wf[:, inner:2 * inner].reshape(D_MODEL, HEADS, HEAD_DIM)
        v3 = wf[:, 2 * inner:3 * inner].reshape(D_MODEL, HEADS, HEAD_DIM)
        q_p = jnp.pad(q3, ((0, 0), (0, 0), (0, HEAD_DIM))).reshape(D_MODEL, HEADS * LANE)
        kv_p = jnp.concatenate([k3, v3], axis=2).reshape(D_MODEL, HEADS * LANE)
        f_p = jnp.pad(wf[:, 3 * inner:], ((0, 0), (0, LANE - HEADS)))
        lw["mixer"] = dict(w_qkv=jnp.concatenate([q_p, kv_p], axis=1), w_f=f_p,
                           w_o=_pad_heads_out(_merge_rows(ch["fox_w_o"])))
    return lw


def _part_contributions(i, part, lg, chunk_shapes):
    spec = {k: jax.ShapeDtypeStruct(s, BF) for k, s in chunk_shapes.items()}
    (contrib,) = jax.linear_transpose(functools.partial(_part_to_compute, i, part), spec)(lg)
    return contrib


def _chip_peers():
    x, y, c = lax.axis_index("x"), lax.axis_index("y"), lax.axis_index("c")
    peers = [(1 - x, y), (x, 1 - y), (1 - x, 1 - y)]
    return x, y, c, peers


SEM_SPEC = pl.BlockSpec(memory_space=pltpu.SEMAPHORE)
ANY_SPEC = pl.BlockSpec(memory_space=pl.ANY)
SPLIT_EFFECT = pltpu.SideEffectType.DATAFLOW_SIDE_EFFECTING


def _own_slot(shard):
    me = 2 * lax.axis_index("x") + lax.axis_index("y")
    return lax.dynamic_update_index_in_dim(lax.empty((N_CHIPS,) + shard.shape, shard.dtype), shard[None], me, 0)


def _spread_copy(src, land, k, peer, c, send_sems, recv_sems, index, src_slot, slot):
    px, py = peer
    return pltpu.make_async_remote_copy(
        src_ref=src.at[src_slot], dst_ref=land.at[slot],
        send_sem=send_sems.at[3 * index + k], recv_sem=recv_sems.at[3 * index + k],
        device_id=(px, py, c), device_id_type=MESH)


def _spread_start(bufs, srcs, after, name):
    n = len(bufs)
    exchange = srcs is not None
    arrays = (list(srcs) if exchange else []) + list(bufs)
    na = len(arrays)

    def body(*refs):
        src, land = refs[:n], refs[na - n:na]
        send_sems, recv_sems = refs[na + 1], refs[na + 2]
        token = refs[-1]
        x, y, c, peers = _chip_peers()
        me = 2 * x + y
        for w in range(n):
            for k, peer in enumerate(peers):
                src_slot = 2 * peer[0] + peer[1] if exchange else me
                _spread_copy(src[w], land[w], k, peer, c, send_sems, recv_sems, w, src_slot, me).start()
        token[...] = jnp.zeros_like(token)

    hbm = [pltpu.with_memory_space_constraint(a, pltpu.HBM) for a in arrays]
    out = pl.pallas_call(
        body, name=name,
        out_shape=(pltpu.SemaphoreType.DMA((3 * n,)), pltpu.SemaphoreType.DMA((3 * n,)),
                   *[pltpu.HBM(a.shape, a.dtype) for a in hbm], jax.ShapeDtypeStruct((8, LANE), F32)),
        in_specs=[HBM_SPEC] * na + [ANY_SPEC],
        out_specs=(SEM_SPEC, SEM_SPEC, *[HBM_SPEC] * na, pl.BlockSpec(memory_space=pltpu.VMEM)),
        input_output_aliases={w: 2 + w for w in range(na)},
        compiler_params=pltpu.CompilerParams(has_side_effects=SPLIT_EFFECT))(*hbm, after)
    return dict(send=out[0], recv=out[1], arrays=out[2:2 + na], n=n, token=out[-1], exchange=exchange)


def _spread_wait(handle, after, name):
    n, exchange = handle["n"], handle["exchange"]
    arrays = list(handle["arrays"])
    na = len(arrays)

    def body(*refs):
        src, land = refs[:n], refs[na - n:na]
        send_sems, recv_sems = refs[na], refs[na + 1]
        x, y, c, peers = _chip_peers()
        me = 2 * x + y
        for w in range(n):
            for k, peer in enumerate(peers):
                there = 2 * peer[0] + peer[1]
                cp = _spread_copy(src[w], land[w], k, peer, c, send_sems, recv_sems, w, there if exchange else me, there)
                cp.wait_send()
                cp.wait_recv()

    out = pl.pallas_call(
        body, name=name, out_shape=tuple(pltpu.HBM(a.shape, a.dtype) for a in arrays),
        in_specs=[HBM_SPEC] * na + [SEM_SPEC, SEM_SPEC, ANY_SPEC], out_specs=tuple([HBM_SPEC] * na),
        input_output_aliases={w: w for w in range(na)},
        compiler_params=pltpu.CompilerParams(has_side_effects=SPLIT_EFFECT))(*arrays, handle["send"], handle["recv"], after)
    return (list(out[n:]), list(out[:n])) if exchange else list(out)


def _sibling_copy(received, sent, land, k, me, peers, sibling, send_sems, recv_sems, index):
    slot = me if k == 3 else 2 * peers[k][0] + peers[k][1]
    src = sent if k == 3 else received
    return pltpu.make_async_remote_copy(
        src_ref=src.at[slot], dst_ref=land.at[slot], send_sem=send_sems.at[4 * index + k],
        recv_sem=recv_sems.at[4 * index + k], device_id=sibling, device_id_type=MESH)


def _sibling_start(received, sent, after, name):
    n = len(received)
    lands = [lax.empty(a.shape, a.dtype) for a in received]
    arrays = list(received) + list(sent) + lands

    def body(*refs):
        rec, snt, land = refs[:n], refs[n:2 * n], refs[2 * n:3 * n]
        send_sems, recv_sems = refs[3 * n + 1], refs[3 * n + 2]
        token = refs[-1]
        x, y, c, peers = _chip_peers()
        for w in range(n):
            for k in range(4):
                _sibling_copy(rec[w], snt[w], land[w], k, 2 * x + y, peers, (x, y, 1 - c), send_sems, recv_sems, w).start()
        token[...] = jnp.zeros_like(token)

    hbm = [pltpu.with_memory_space_constraint(a, pltpu.HBM) for a in arrays]
    out = pl.pallas_call(
        body, name=name,
        out_shape=(pltpu.SemaphoreType.DMA((4 * n,)), pltpu.SemaphoreType.DMA((4 * n,)),
                   *[pltpu.HBM(a.shape, a.dtype) for a in hbm], jax.ShapeDtypeStruct((8, LANE), F32)),
        in_specs=[HBM_SPEC] * (3 * n) + [ANY_SPEC],
        out_specs=(SEM_SPEC, SEM_SPEC, *[HBM_SPEC] * (3 * n), pl.BlockSpec(memory_space=pltpu.VMEM)),
        input_output_aliases={w: 2 + w for w in range(3 * n)},
        compiler_params=pltpu.CompilerParams(has_side_effects=SPLIT_EFFECT))(*hbm, after)
    return dict(send=out[0], recv=out[1], arrays=out[2:2 + 3 * n], n=n, token=out[-1])


def _sibling_wait(handle, after, name):
    n = handle["n"]
    arrays = list(handle["arrays"])

    def body(*refs):
        rec, snt, land = refs[:n], refs[n:2 * n], refs[2 * n:3 * n]
        send_sems, recv_sems = refs[3 * n], refs[3 * n + 1]
        x, y, c, peers = _chip_peers()
        for w in range(n):
            for k in range(4):
                cp = _sibling_copy(rec[w], snt[w], land[w], k, 2 * x + y, peers, (x, y, 1 - c), send_sems, recv_sems, w)
                cp.wait_send()
                cp.wait_recv()

    out = pl.pallas_call(
        body, name=name, out_shape=tuple(pltpu.HBM(a.shape, a.dtype) for a in arrays),
        in_specs=[HBM_SPEC] * (3 * n) + [SEM_SPEC, SEM_SPEC, ANY_SPEC], out_specs=tuple([HBM_SPEC] * (3 * n)),
        input_output_aliases={w: w for w in range(3 * n)},
        compiler_params=pltpu.CompilerParams(has_side_effects=SPLIT_EFFECT))(*arrays, handle["send"], handle["recv"], after)
    return list(out[:n]), list(out[n:2 * n]), list(out[2 * n:])


def _all_reduce_small(v):
    rows = v.shape[0]

    def body(v_ref, sum_ref, slots, send_sems, recv_sems):
        x, y, c = lax.axis_index("x"), lax.axis_index("y"), lax.axis_index("c")
        me = 4 * x + 2 * y + c
        slots[me] = v_ref[...]
        sends = []
        for k in range(1, N_DEV):
            bx, by, bc = (k >> 2) & 1, (k >> 1) & 1, k & 1
            peer = (x ^ bx, y ^ by, c ^ bc)
            rc = pltpu.make_async_remote_copy(src_ref=v_ref, dst_ref=slots.at[me], send_sem=send_sems.at[k],
                                              recv_sem=recv_sems.at[k], device_id=peer, device_id_type=MESH)
            rc.start()
            sends.append(rc)
        for k in range(1, N_DEV):
            bx, by, bc = (k >> 2) & 1, (k >> 1) & 1, k & 1
            src = 4 * (x ^ bx) + 2 * (y ^ by) + (c ^ bc)
            pltpu.make_async_remote_copy(src_ref=v_ref, dst_ref=slots.at[src], send_sem=send_sems.at[k],
                                         recv_sem=recv_sems.at[k], device_id=(x ^ bx, y ^ by, c ^ bc),
                                         device_id_type=MESH).wait_recv()
        for rc in sends:
            rc.wait_send()
        total = slots[0]
        for k in range(1, N_DEV):
            total = total + slots[k]
        sum_ref[...] = total

    vm = pl.BlockSpec(memory_space=pltpu.VMEM)
    return pl.pallas_call(
        body, out_shape=jax.ShapeDtypeStruct((rows, LANE), F32), in_specs=[vm], out_specs=vm,
        scratch_shapes=[pltpu.VMEM((N_DEV, rows, LANE), F32), pltpu.SemaphoreType.DMA((N_DEV,)),
                        pltpu.SemaphoreType.DMA((N_DEV,))], name="all_reduce_small")(v)


def _as_2d(a):
    return a.reshape(-1, a.shape[-1])


def _row_tile(rows, cols):
    for t in (512, 256, 128, 64, 32, 16):
        if rows % t == 0 and t * cols * 4 <= (1 << 20):
            return t
    return rows


def _adamw_weight(w, m, v, received, sent, sibling):
    layers = len(received)
    _, rows, cols = received[0].shape
    tr = _row_tile(rows, cols)
    by_columns = rows % tr != 0 or tr == rows and rows * cols * 4 > (2 << 20)
    if by_columns:
        assert layers == 1 and cols % (2 * LANE) == 0, (w.shape, received[0].shape)
        tr, tc, steps = rows, cols // 2, 2
        index = lambda i: (0, i)
    else:
        tc, steps = cols, rows // tr
        index = lambda i: (i, 0)
    where = (2 * lax.axis_index("x") + lax.axis_index("y")).astype(jnp.int32).reshape(1)

    def body(where_ref, w_ref, m_ref, v_ref, *rest):
        per_layer, (g_ref, d_ref, nm_ref, nv_ref) = rest[:3 * layers], rest[3 * layers:]
        me = where_ref[0]
        for layer in range(layers):
            r_ref, own_ref, s_ref = per_layer[3 * layer:3 * layer + 3]

            @pl.when(pl.program_id(0) == layer)
            def _():
                mine = theirs = None
                for k in range(N_CHIPS):
                    a = jnp.where(me == k, own_ref[...], r_ref[k]).astype(F32)
                    b = s_ref[k].astype(F32)
                    mine = a if mine is None else mine + a
                    theirs = b if theirs is None else theirs + b
                g = mine + theirs
                delta, nm, nv = _adamw_math(w_ref[...], g, m_ref[...], v_ref[...])
                g_ref[...] = g
                d_ref[...] = delta
                nm_ref[...] = nm
                nv_ref[...] = nv

    def held(layer, now, i):
        return jnp.where(now < layer, 0, jnp.where(now > layer, steps - 1, i))

    if by_columns:
        stacked = pl.BlockSpec((tr, tc), lambda now, i, where_ref: index(i))
    else:
        stacked = pl.BlockSpec((tr, tc), lambda now, i, where_ref: (now * steps + i, 0))
    in_specs = [stacked, stacked, stacked]
    args = [where, w, m, v]
    for layer in range(layers):
        four = pl.BlockSpec((N_CHIPS, tr, tc), lambda now, i, where_ref, layer=layer: (0,) + index(held(layer, now, i)))
        own = pl.BlockSpec((None, tr, tc),
                           lambda now, i, where_ref, layer=layer: (where_ref[0],) + index(held(layer, now, i)))
        in_specs += [four, own, four]
        args += [received[layer], sent[layer], sibling[layer]]
    grid_spec = pltpu.PrefetchScalarGridSpec(num_scalar_prefetch=1, grid=(layers, steps), in_specs=in_specs,
                                             out_specs=[stacked] * 4)
    return pl.pallas_call(body, out_shape=[jax.ShapeDtypeStruct(w.shape, F32)] * 4, grid_spec=grid_spec,
                          name="adamw_weight", compiler_params=_params(("arbitrary", "arbitrary")))(*args)


def _adamw_math(w, g, m, v):
    m = ADAM_B1 * m + (1.0 - ADAM_B1) * g
    v = ADAM_B2 * v + (1.0 - ADAM_B2) * (g * g)
    m_hat = m * (1.0 / (1.0 - ADAM_B1 ** ADAM_STEP))
    v_hat = v * (1.0 / (1.0 - ADAM_B2 ** ADAM_STEP))
    denom = jnp.sqrt(v_hat) + ADAM_EPS
    inv = pl.reciprocal(denom, approx=True)
    inv = inv * (2.0 - denom * inv)
    delta = -ADAM_LR * (m_hat * inv + ADAM_WD * w)
    return delta, m, v


def _adamw(w, m, v, g_mine, g_sibling):
    rows, cols = w.shape
    tr = _row_tile(rows, cols)
    two = g_sibling is not None

    def body(*refs):
        if two:
            w_ref, m_ref, v_ref, ga_ref, gb_ref, g_ref, d_ref, nm_ref, nv_ref = refs
            g = ga_ref[...] + gb_ref[...]
        else:
            w_ref, m_ref, v_ref, ga_ref, g_ref, d_ref, nm_ref, nv_ref = refs
            g = ga_ref[...]
        delta, nm, nv = _adamw_math(w_ref[...], g, m_ref[...], v_ref[...])
        g_ref[...] = g
        d_ref[...] = delta
        nm_ref[...] = nm
        nv_ref[...] = nv

    blk = pl.BlockSpec((tr, cols), lambda i: (i, 0))
    args = [w, m, v, g_mine] + ([g_sibling] if two else [])
    return pl.pallas_call(body, out_shape=[jax.ShapeDtypeStruct((rows, cols), F32)] * 4, grid=(rows // tr,),
                          in_specs=[blk] * len(args), out_specs=[blk] * 4, name="adamw",
                          compiler_params=_params(("parallel",)))(*args)


def _pack_rows(arrays):
    flat = jnp.concatenate([a.reshape(-1) for a in arrays])
    rows = -(-flat.shape[0] // (8 * LANE)) * 8
    return jnp.pad(flat, (0, rows * LANE - flat.shape[0])).reshape(rows, LANE)


def _unpack_rows(packed, shapes):
    flat = packed.reshape(-1)
    out, at = [], 0
    for s in shapes:
        size = math.prod(s)
        out.append(flat[at:at + size].reshape(s))
        at += size
    return out


def kernel(x, p, positions, norm_g, ffn_w_in, ffn_w_out, ple_w_proj, ple_w_gate, rel_bias, mla_w_a, mla_q_norm, mla_kv_norm, mla_w_uq, mla_w_ukv, mla_w_o, dil_w_qkv, dil_w_o, fox_w_qkvf, fox_b_f, fox_w_o, loss_target, m_norm_g, m_ffn_w_in, m_ffn_w_out, m_ple_w_proj, m_ple_w_gate, m_rel_bias, m_mla_w_a, m_mla_q_norm, m_mla_kv_norm, m_mla_w_uq, m_mla_w_ukv, m_mla_w_o, m_dil_w_qkv, m_dil_w_o, m_fox_w_qkvf, m_fox_b_f, m_fox_w_o, v_norm_g, v_ffn_w_in, v_ffn_w_out, v_ple_w_proj, v_ple_w_gate, v_rel_bias, v_mla_w_a, v_mla_q_norm, v_mla_kv_norm, v_mla_w_uq, v_mla_w_ukv, v_mla_w_o, v_dil_w_qkv, v_dil_w_o, v_fox_w_qkvf, v_fox_b_f, v_fox_w_o):
    w = dict(norm_g=norm_g, ffn_w_in=ffn_w_in, ffn_w_out=ffn_w_out, ple_w_proj=ple_w_proj, ple_w_gate=ple_w_gate,
             rel_bias=rel_bias, mla_w_a=mla_w_a, mla_q_norm=mla_q_norm, mla_kv_norm=mla_kv_norm, mla_w_uq=mla_w_uq,
             mla_w_ukv=mla_w_ukv, mla_w_o=mla_w_o, dil_w_qkv=dil_w_qkv, dil_w_o=dil_w_o, fox_w_qkvf=fox_w_qkvf,
             fox_b_f=fox_b_f, fox_w_o=fox_w_o)
    m = dict(norm_g=m_norm_g, ffn_w_in=m_ffn_w_in, ffn_w_out=m_ffn_w_out, ple_w_proj=m_ple_w_proj,
             ple_w_gate=m_ple_w_gate, rel_bias=m_rel_bias, mla_w_a=m_mla_w_a, mla_q_norm=m_mla_q_norm,
             mla_kv_norm=m_mla_kv_norm, mla_w_uq=m_mla_w_uq, mla_w_ukv=m_mla_w_ukv, mla_w_o=m_mla_w_o,
             dil_w_qkv=m_dil_w_qkv, dil_w_o=m_dil_w_o, fox_w_qkvf=m_fox_w_qkvf, fox_b_f=m_fox_b_f, fox_w_o=m_fox_w_o)
    v = dict(norm_g=v_norm_g, ffn_w_in=v_ffn_w_in, ffn_w_out=v_ffn_w_out, ple_w_proj=v_ple_w_proj,
             ple_w_gate=v_ple_w_gate, rel_bias=v_rel_bias, mla_w_a=v_mla_w_a, mla_q_norm=v_mla_q_norm,
             mla_kv_norm=v_mla_kv_norm, mla_w_uq=v_mla_w_uq, mla_w_ukv=v_mla_w_ukv, mla_w_o=v_mla_w_o,
             dil_w_qkv=v_dil_w_qkv, dil_w_o=v_dil_w_o, fox_w_qkvf=v_fox_w_qkvf, fox_b_f=v_fox_b_f, fox_w_o=v_fox_w_o)
    chip = 2 * lax.axis_index("x") + lax.axis_index("y")
    for tree in (w, m, v):
        tree[TRANSPOSED] = jnp.swapaxes(tree[TRANSPOSED], 1, 2)

    small_shapes = [w[k].shape for k in SMALL_SHARDED]
    order = [(i, part) for i in range(DEPTH) for part in (MIXER_PART, COMMON_PART) if _part_names(i, part)]
    gathers = {}
    after = positions
    zero = 0.0
    for i, part in order:
        bufs = [_own_slot((w[k][_layer_slot(k, i)] + zero).astype(BF)) for k in _part_names(i, part)]
        if (i, part) == order[0]:
            bufs.append(_own_slot(_pack_rows([w[k] for k in SMALL_SHARDED])))
        gathers[i, part] = _spread_start(bufs, None, after, f"gather_start_{i}_{part}")
        after = gathers[i, part]["token"]
        if (i, part) == order[0]:
            zero = after[0, 0]
    all_started = after
    state = {}

    def get_part(i, part, after_array):
        is_first = (i, part) == order[0]
        lands = _spread_wait(gathers[i, part], all_started if is_first else after_array, f"gather_wait_{i}_{part}")
        if is_first:
            pieces = [_unpack_rows(lands[-1][k], small_shapes) for k in range(N_CHIPS)]
            small = {name: jnp.concatenate([pieces[k][idx] for k in range(N_CHIPS)], axis=-1)
                     for idx, name in enumerate(SMALL_SHARDED)}
            state["small"] = dict(small, rel_bias=rel_bias, fox_b_f=fox_b_f)
        chunks = dict(zip(_part_names(i, part), lands))
        state[i, part] = {k: a.shape for k, a in chunks.items()}
        return _part_to_compute(i, part, chunks)

    started, forwards = [], {}

    def forward_oldest(after_array):
        i, part, handle = started.pop(0)
        received, sent = _spread_wait(handle, after_array, f"exchange_wait_{i}_{part}")
        forwards[i, part] = _sibling_start(received, sent, after_array, f"sibling_start_{i}_{part}")
        return forwards[i, part]["token"]

    def put_part(i, part, lg):
        contrib = _part_contributions(i, part, lg, state[i, part])
        srcs = [contrib[k] for k in _part_names(i, part)]
        handle = _spread_start([lax.empty(s.shape, s.dtype) for s in srcs], srcs, positions,
                               f"exchange_start_{i}_{part}")
        token = handle["token"]
        if started:
            token = token + forward_oldest(token)
        started.append((i, part, handle))
        return token

    sq, grad_x, sg = _run_layers(x[0], p[:, 0], positions[0], loss_target[0], get_part, lambda: state["small"],
                                 put_part)
    loss = lax.psum(0.5 / D_MODEL * jnp.sum(sq), ("x", "y", "c"))
    forward_oldest(grad_x)

    held = {k: {} for k in BIG}
    for i, part in sorted(forwards, reverse=True):
        received, sent, sibling = _sibling_wait(forwards[i, part], grad_x, f"sibling_wait_{i}_{part}")
        for k, r, s, t in zip(_part_names(i, part), received, sent, sibling):
            held[k][_layer_slot(k, i)] = (r, s, t)
    results = {}
    for k in BIG:
        per_layer = [held[k][slot] for slot in sorted(held[k])]
        outs = _adamw_weight(_as_2d(w[k]), _as_2d(m[k]), _as_2d(v[k]), *[list(col) for col in zip(*per_layer)])
        results[k] = [o.reshape(w[k].shape) for o in outs]
    results[TRANSPOSED] = [jnp.swapaxes(o, 1, 2) for o in results[TRANSPOSED]]

    small_all = SMALL_SHARDED + SMALL_REPLICATED
    full_shapes = [sg[k].shape for k in small_all]
    reduced = dict(zip(small_all, _unpack_rows(_all_reduce_small(_pack_rows([sg[k] for k in small_all])), full_shapes)))
    local_g = []
    for k in small_all:
        g = reduced[k]
        if k in SMALL_SHARDED:
            width = w[k].shape[-1]
            g = lax.dynamic_slice_in_dim(g, chip * width, width, axis=g.ndim - 1)
        local_g.append(g)
    local_shapes = [w[k].shape for k in small_all]
    outs = _adamw(_pack_rows([w[k] for k in small_all]), _pack_rows([m[k] for k in small_all]),
                  _pack_rows([v[k] for k in small_all]), _pack_rows(local_g), None)
    unpacked = [_unpack_rows(o, local_shapes) for o in outs]
    for idx, k in enumerate(small_all):
        results[k] = [u[idx] for u in unpacked]

    return (loss, grad_x[None], *[results[k][0] for k in WEIGHTS], *[results[k][1] for k in WEIGHTS],
            *[results[k][2] for k in WEIGHTS], *[results[k][3] for k in WEIGHTS])
```

```python
import functools
import math

import jax
import jax.numpy as jnp
from jax import lax
from jax.experimental import pallas as pl
from jax.experimental.pallas import tpu as pltpu

F32 = jnp.float32
BF = jnp.bfloat16
MESH = pl.DeviceIdType.MESH
HBM_SPEC = pl.BlockSpec(memory_space=pltpu.HBM)

D_MODEL = 1024
DEPTH = 4
N_MIXERS = 3
D_FF = 2816
NORM_EPS = 1e-6
NEG_INF = -1e30
LANE = 128
HEADS = 16
HEAD_DIM = 64
MLA_Q_RANK = 384
MLA_KV_RANK = 256
MLA_ROPE = 32
MLA_A_PAD = 768
ROPE_THETA = 10000.0
DIL_PATTERNS = ((128, 1), (512, 4), (2048, 16))
Q_BLOCK = 128
DIL_PAIRS = {1: 2, 4: 4, 16: 4}
REL_BUCKETS = 32
REL_MAX_DIST = 2048
N_CHIPS = 4
N_DEV = 8

ADAM_LR = 0.001
ADAM_B1 = 0.9
ADAM_B2 = 0.999
ADAM_EPS = 1e-08
ADAM_WD = 0.01
ADAM_STEP = 10

VMEM_LIMIT = 56 * 1024 * 1024
MATMUL_VMEM_BUDGET = 36 * 1024 * 1024
ROW_TILE = 512
ATTN_TILE = 256
ATTN_Q_TILE = 512
MLA_GROUP = 4
FOX_GROUP = 4
FORWARD_GROUP = 4


def _params(sem=None):
    return pltpu.CompilerParams(dimension_semantics=sem, vmem_limit_bytes=VMEM_LIMIT)


def _divisor_tiles(dim):
    tiles = [t for t in range(LANE, dim + 1, LANE) if dim % t == 0]
    return tiles or [dim]


def _matmul_tiles(m, n, k, a_bytes, b_bytes, out_bytes, has_add, n_unit=None, k_unit=None):
    best = None
    for tm in _divisor_tiles(m):
        for tn in _divisor_tiles(n_unit or n):
            for tk in _divisor_tiles(k_unit or k):
                if max(tm, tn, tk) > 2048:
                    continue
                vmem = 2 * (tm * tk * a_bytes + tk * tn * b_bytes + tm * tn * out_bytes) + tm * tn * 4
                if has_add:
                    vmem += 2 * tm * tn * 4
                if vmem > MATMUL_VMEM_BUDGET:
                    continue
                steps = (m // tm) * (n // tn) * (k // tk)
                traffic = m * k * a_bytes * (n // tn) + k * n * b_bytes * (m // tm) + m * n * out_bytes
                cost = traffic / 3.0e12 + steps * 0.4e-6
                if best is None or cost < best[0]:
                    best = (cost, tm, tn, tk)
    return best[1:]


def _matmul(a, b, *, ta=False, tb=False, b_chunks=False, out_chunks=False, add=None, out_dtype=F32, name):
    k, m = a.shape if ta else a.shape[::-1]
    n_unit = k_unit = None
    if b_chunks:
        chunks, rows_w, c = b.shape
        if tb:
            kb, n, k_unit = chunks * c, rows_w, c
        else:
            kb, n, n_unit = rows_w, chunks * c, c
    else:
        kb, n = b.shape[::-1] if tb else b.shape
    if out_chunks:
        assert n % N_CHIPS == 0 and add is None
        n_unit = n // N_CHIPS
    assert k == kb, (a.shape, b.shape, ta, tb)
    tm, tn, tk = _matmul_tiles(m, n, k, a.dtype.itemsize, b.dtype.itemsize, jnp.dtype(out_dtype).itemsize,
                               add is not None, n_unit, k_unit)
    nk = k // tk
    dims = (((0 if ta else 1,), (1 if tb else 0,)), ((), ()))

    def body(*refs):
        if add is None:
            a_ref, b_ref, o_ref, acc_ref = refs
            add_ref = None
        else:
            a_ref, b_ref, add_ref, o_ref, acc_ref = refs
        kk = pl.program_id(2)

        @pl.when(kk == 0)
        def _():
            acc_ref[...] = jnp.zeros_like(acc_ref)

        acc_ref[...] += lax.dot_general(a_ref[...].astype(BF), b_ref[...].astype(BF), dims,
                                        preferred_element_type=F32)

        @pl.when(kk == nk - 1)
        def _():
            r = acc_ref[...]
            if add_ref is not None:
                r = r + add_ref[...].astype(F32)
            o_ref[...] = r.astype(out_dtype)

    a_spec = pl.BlockSpec((tk, tm), lambda i, j, q: (q, i)) if ta else pl.BlockSpec((tm, tk), lambda i, j, q: (i, q))
    if b_chunks and tb:
        per_k = k_unit // tk
        b_spec = pl.BlockSpec((None, tn, tk), lambda i, j, q: (q // per_k, j, q % per_k))
    elif b_chunks:
        per_n = n_unit // tn
        b_spec = pl.BlockSpec((None, tk, tn), lambda i, j, q: (j // per_n, q, j % per_n))
    elif tb:
        b_spec = pl.BlockSpec((tn, tk), lambda i, j, q: (j, q))
    else:
        b_spec = pl.BlockSpec((tk, tn), lambda i, j, q: (q, j))
    if out_chunks:
        per_o = n_unit // tn
        o_spec = pl.BlockSpec((None, tm, tn), lambda i, j, q: (j // per_o, i, j % per_o))
        out_shape = jax.ShapeDtypeStruct((N_CHIPS, m, n_unit), out_dtype)
    else:
        o_spec = pl.BlockSpec((tm, tn), lambda i, j, q: (i, j))
        out_shape = jax.ShapeDtypeStruct((m, n), out_dtype)
    in_specs = [a_spec, b_spec]
    args = [a, b]
    if add is not None:
        in_specs.append(o_spec)
        args.append(add)
    return pl.pallas_call(
        body, out_shape=out_shape, grid=(m // tm, n // tn, nk),
        in_specs=in_specs, out_specs=o_spec, scratch_shapes=[pltpu.VMEM((tm, tn), F32)], name=name,
        compiler_params=_params(("parallel", "parallel", "arbitrary")))(*args)


def _rowwise(body, name, rows, ins, outs, tr=ROW_TILE):
    def row_spec(cols):
        return pl.BlockSpec((tr, cols), lambda i: (i, 0))

    def full_spec(shape):
        zeros = (0,) * len(shape)
        return pl.BlockSpec(shape, lambda i: zeros)

    in_specs = [row_spec(a.shape[1]) if kind == "row" else full_spec(a.shape) for a, kind in ins]
    out_specs = [row_spec(shape[1]) if kind == "row" else full_spec(shape) for shape, _, kind in outs]
    out_shape = [jax.ShapeDtypeStruct(shape, dtype) for shape, dtype, _ in outs]
    return pl.pallas_call(body, out_shape=out_shape, grid=(rows // tr,), in_specs=in_specs, out_specs=out_specs,
                          name=name, compiler_params=_params(("arbitrary",)))(*[a for a, _ in ins])


def _rstd(x):
    return lax.rsqrt(jnp.mean(x * x, axis=-1, keepdims=True) + NORM_EPS)


def _rms_bwd_math(x, g, dy):
    r = _rstd(x)
    gd = dy * g
    dx = r * gd - x * (r * r * r) * jnp.mean(gd * x, axis=-1, keepdims=True)
    dg = jnp.sum(dy * x * r, axis=0, keepdims=True)
    return dx, dg


def _sigmoid(x):
    return 0.5 * jnp.tanh(0.5 * x) + 0.5


def _init_acc(*refs):
    @pl.when(pl.program_id(0) == 0)
    def _():
        for r in refs:
            r[...] = jnp.zeros_like(r)


def _prenorm(h, g):
    rows, cols = h.shape

    def body(h_ref, g_ref, o_ref):
        x = h_ref[...]
        o_ref[...] = (x * _rstd(x) * g_ref[...]).astype(BF)

    return _rowwise(body, "prenorm", rows, [(h, "row"), (g, "full")], [((rows, cols), BF, "row")])[0]


def _post_residual(h, y, g_post, g_pre):
    rows, cols = h.shape
    with_pre = g_pre is not None

    def body(*refs):
        if with_pre:
            h_ref, y_ref, gp_ref, gq_ref, hn_ref, hb_ref = refs
        else:
            h_ref, y_ref, gp_ref, hn_ref, hb_ref = refs
        yv = y_ref[...]
        hn = h_ref[...] + yv * _rstd(yv) * gp_ref[...]
        hn_ref[...] = hn
        hb_ref[...] = (hn * _rstd(hn) * gq_ref[...] if with_pre else hn).astype(BF)

    ins = [(h, "row"), (y, "row"), (g_post, "full")] + ([(g_pre, "full")] if with_pre else [])
    return _rowwise(body, "post_residual_pre" if with_pre else "post_residual", rows, ins,
                    [((rows, cols), F32, "row"), ((rows, cols), BF, "row")])


def _ple_forward(h2, pp, z, g_pre):
    rows, cols = h2.shape

    def body(h_ref, p_ref, z_ref, g_ref, h3_ref, hb_ref):
        h3 = h_ref[...] + p_ref[...] * _sigmoid(z_ref[...])
        h3_ref[...] = h3
        hb_ref[...] = (h3 * _rstd(h3) * g_ref[...]).astype(BF)

    return _rowwise(body, "ple_forward", rows, [(h2, "row"), (pp, "row"), (z, "row"), (g_pre, "full")],
                    [((rows, cols), F32, "row"), ((rows, cols), BF, "row")])


def _ple_loss(h2, pp, z, target):
    rows, cols = h2.shape

    def body(h_ref, p_ref, z_ref, t_ref, dh_ref, sq_ref):
        _init_acc(sq_ref)
        err = h_ref[...] + p_ref[...] * _sigmoid(z_ref[...]) - t_ref[...]
        dh_ref[...] = err * (1.0 / cols)
        sq_ref[...] += jnp.sum(err * err, axis=0, keepdims=True)

    return _rowwise(body, "ple_loss", rows, [(h2, "row"), (pp, "row"), (z, "row"), (target, "row")],
                    [((rows, cols), F32, "row"), ((1, cols), F32, "acc")])


def _ple_backward(dh3, pp, z):
    rows, cols = dh3.shape

    def body(d_ref, p_ref, z_ref, dpp_ref, dz_ref):
        d = d_ref[...]
        s = _sigmoid(z_ref[...])
        dpp_ref[...] = (d * s).astype(BF)
        dz_ref[...] = (d * p_ref[...] * s * (1.0 - s)).astype(BF)

    return _rowwise(body, "ple_backward", rows, [(dh3, "row"), (pp, "row"), (z, "row")],
                    [((rows, cols), BF, "row"), ((rows, cols), BF, "row")])


def _rms_backward(x, g, dy, add, out_dtype):
    rows, cols = x.shape
    with_add = add is not None

    def body(*refs):
        if with_add:
            x_ref, g_ref, dy_ref, add_ref, dx_ref, dg_ref = refs
        else:
            x_ref, g_ref, dy_ref, dx_ref, dg_ref = refs
        _init_acc(dg_ref)
        dx, dg = _rms_bwd_math(x_ref[...], g_ref[...], dy_ref[...].astype(F32))
        if with_add:
            dx = dx + add_ref[...]
        dx_ref[...] = dx.astype(out_dtype)
        dg_ref[...] += dg

    ins = [(x, "row"), (g, "full"), (dy, "row")] + ([(add, "row")] if with_add else [])
    return _rowwise(body, "rms_backward_add" if with_add else "rms_backward", rows, ins,
                    [((rows, cols), out_dtype, "row"), ((1, cols), F32, "acc")])


def _swiglu_forward(gu):
    rows = gu.shape[0]

    def body(gu_ref, o_ref):
        g = gu_ref[:, :D_FF].astype(F32)
        o_ref[...] = (g * _sigmoid(g) * gu_ref[:, D_FF:].astype(F32)).astype(BF)

    return _rowwise(body, "swiglu_forward", rows, [(gu, "row")], [((rows, D_FF), BF, "row")])[0]


def _swiglu_backward(gu, dact):
    rows = gu.shape[0]

    def body(gu_ref, d_ref, o_ref):
        g = gu_ref[:, :D_FF].astype(F32)
        u = gu_ref[:, D_FF:].astype(F32)
        d = d_ref[...].astype(F32)
        s = _sigmoid(g)
        gs = g * s
        o_ref[:, :D_FF] = (d * u * (s + gs * (1.0 - s))).astype(BF)
        o_ref[:, D_FF:] = (d * gs).astype(BF)

    return _rowwise(body, "swiglu_backward", rows, [(gu, "row"), (dact, "row")], [((rows, 2 * D_FF), BF, "row")])[0]


def _rope_tables(positions):
    half = MLA_ROPE // 2
    inv = ROPE_THETA ** (-jnp.arange(half, dtype=F32) / half)
    ang = positions.astype(F32)[:, None] * inv
    cos, sin = jnp.cos(ang), jnp.sin(ang)
    rows = positions.shape[0]
    c = jnp.ones((rows, LANE), F32).at[:, 64:80].set(cos).at[:, 80:96].set(cos)
    sa = jnp.zeros((rows, LANE), F32).at[:, 64:80].set(-sin)
    sb = jnp.zeros((rows, LANE), F32).at[:, 80:96].set(sin)
    return c, sa, sb


def _rope_apply(x, c, sa, sb):
    return x * c + pltpu.roll(x, LANE - 16, 1) * sa + pltpu.roll(x, 16, 1) * sb


def _rope_apply_t(dy, c, sa, sb):
    return dy * c + pltpu.roll(dy * sa, 16, 1) + pltpu.roll(dy * sb, LANE - 16, 1)


def _rope_heads(x, tables, transpose, name):
    rows, cols = x.shape

    def body(x_ref, c_ref, sa_ref, sb_ref, o_ref):
        fn = _rope_apply_t if transpose else _rope_apply
        c, sa, sb = c_ref[...], sa_ref[...], sb_ref[...]
        for head in range(cols // LANE):
            lanes = slice(head * LANE, (head + 1) * LANE)
            o_ref[:, lanes] = fn(x_ref[:, lanes].astype(F32), c, sa, sb).astype(BF)

    blk = pl.BlockSpec((ROW_TILE, cols), lambda i: (i, 0))
    tbl = pl.BlockSpec((ROW_TILE, LANE), lambda i: (i, 0))
    return pl.pallas_call(body, out_shape=jax.ShapeDtypeStruct((rows, cols), BF), grid=(rows // ROW_TILE,),
                          in_specs=[blk, tbl, tbl, tbl], out_specs=blk, name=name,
                          compiler_params=_params(("parallel",)))(x, *tables)


def _mla_mid_forward(a, q_norm, kv_norm, tables):
    rows = a.shape[0]
    qr, kvr = MLA_Q_RANK, MLA_KV_RANK

    def body(a_ref, qn_ref, kn_ref, c_ref, sa_ref, sb_ref, cq_ref, ckv_ref, kr_ref):
        aq = a_ref[:, 0:qr]
        akv = a_ref[:, qr:qr + kvr]
        cq_ref[...] = (aq * _rstd(aq) * qn_ref[...]).astype(BF)
        ckv_ref[...] = (akv * _rstd(akv) * kn_ref[...]).astype(BF)
        kr_ref[...] = _rope_apply(a_ref[:, qr + kvr:], c_ref[...], sa_ref[...], sb_ref[...]).astype(BF)

    ins = [(a, "row"), (q_norm, "full"), (kv_norm, "full")] + [(t, "row") for t in tables]
    return _rowwise(body, "mla_mid_forward", rows, ins,
                    [((rows, qr), BF, "row"), ((rows, kvr), BF, "row"), ((rows, LANE), BF, "row")])


def _mla_mid_backward(a, q_norm, kv_norm, tables, dcq, dckv, dkr):
    rows = a.shape[0]
    qr, kvr = MLA_Q_RANK, MLA_KV_RANK

    def body(a_ref, qn_ref, kn_ref, c_ref, sa_ref, sb_ref, dcq_ref, dckv_ref, dkr_ref, da_ref, dqn_ref, dkn_ref):
        _init_acc(dqn_ref, dkn_ref)
        dxq, dgq = _rms_bwd_math(a_ref[:, 0:qr], qn_ref[...], dcq_ref[...])
        dxk, dgk = _rms_bwd_math(a_ref[:, qr:qr + kvr], kn_ref[...], dckv_ref[...])
        da_ref[:, 0:qr] = dxq.astype(BF)
        da_ref[:, qr:qr + kvr] = dxk.astype(BF)
        da_ref[:, qr + kvr:] = _rope_apply_t(dkr_ref[...], c_ref[...], sa_ref[...], sb_ref[...]).astype(BF)
        dqn_ref[...] += dgq
        dkn_ref[...] += dgk

    ins = ([(a, "row"), (q_norm, "full"), (kv_norm, "full")] + [(t, "row") for t in tables]
           + [(dcq, "row"), (dckv, "row"), (dkr, "row")])
    return _rowwise(body, "mla_mid_backward", rows, ins,
                    [((rows, MLA_A_PAD), BF, "row"), ((1, qr), F32, "acc"), ((1, kvr), F32, "acc")])


def _attn_specs(rows, kv_off, g, many_row_vectors):
    head =pl.BlockSpec((rows, g * LANE), lambda h: (0, h))
    kv_head = pl.BlockSpec((rows, g * LANE), lambda h: (0, h + kv_off // g))
    shared = pl.BlockSpec((rows, LANE), lambda h: (0, 0))
    col_vec = pl.BlockSpec((g, rows, 1), lambda h: (h, 0, 0),
                           pipeline_mode=pl.Buffered(1 if many_row_vectors else 2))
    row_vec = pl.BlockSpec((g, 1, rows), lambda h: (h, 0, 0))
    return head, kv_head, shared, col_vec, row_vec


def _attn_forward(q, kv, kv_off, kr, cum_col, cum_row, scale, group_size, name):
    rows = q.shape[0]
    heads = HEADS
    t = ATTN_TILE
    tq = ATTN_Q_TILE
    per = tq // t
    has_kr = kr is not None
    has_f = cum_col is not None
    group = range(group_size)

    def body(*refs):
        it = iter(refs)
        q_ref, kv_ref = next(it), next(it)
        kr_ref = next(it) if has_kr else None
        cc_ref = next(it) if has_f else None
        cr_ref = next(it) if has_f else None
        o_ref, lse_ref = next(it), next(it)
        lo = lax.broadcasted_iota(jnp.int32, (1, LANE), 1) < HEAD_DIM
        row = lax.broadcasted_iota(jnp.int32, (tq, t), 0)
        col = lax.broadcasted_iota(jnp.int32, (tq, t), 1)
        lanes = [slice(g * LANE, (g + 1) * LANE) for g in group]

        def q_block(i, _):
            qs = pl.ds(pl.multiple_of(i * tq, tq), tq)
            qbs = [q_ref[qs, lanes[g]] for g in group]
            cqs = [cc_ref[g, qs, :] if has_f else None for g in group]

            def step(j, carry, diag):
                ks = pl.ds(pl.multiple_of(j * t, t), t)
                other = kr_ref[ks, :] if has_kr else jnp.zeros((t, LANE), BF)
                kvbs = [kv_ref[ks, lanes[g]] for g in group]

                def logit(g):
                    return lax.dot_general(qbs[g], jnp.where(lo, kvbs[g], other), (((1,), (1,)), ((), ())),
                                           preferred_element_type=F32)

                logits = {g: logit(g) for g in (group if has_f else group[:1])}
                out = []
                for g in group:
                    m, l, acc = carry[g]
                    if not has_f and g + 1 < len(group):
                        logits[g + 1] = logit(g + 1)
                    s = logits[g] * scale
                    if has_f:
                        s = s + (cqs[g] - cr_ref[g, :, ks])
                    if diag is not None:
                        s = jnp.where(col + diag * t <= row, s, NEG_INF)
                    mn = jnp.maximum(m, jnp.max(s, axis=1, keepdims=True))
                    alpha = jnp.exp(m - mn)
                    p = jnp.exp(s - mn)
                    l = alpha * l + jnp.sum(p, axis=1, keepdims=True)
                    acc = alpha * acc + jnp.dot(p.astype(BF), kvbs[g], preferred_element_type=F32)
                    out.append((mn, l, acc))
                return tuple(out)

            init = tuple((jnp.full((tq, 1), NEG_INF, F32), jnp.zeros((tq, 1), F32), jnp.zeros((tq, LANE), F32))
                         for _ in group)
            carry = lax.fori_loop(0, i * per, lambda j, c: step(j, c, None), init)
            for d in range(per):
                carry = step(i * per + d, carry, d)
            for g, (m, l, acc) in enumerate(carry):
                o_ref[qs, lanes[g]] = jnp.where(lo, 0.0, acc * (1.0 / l)).astype(BF)
                lse_ref[g, qs, :] = m + jnp.log(l)
            return 0

        lax.fori_loop(0, rows // tq, q_block, 0)

    head, kv_head, shared, col_vec, row_vec = _attn_specs(rows, kv_off, group_size, has_f)
    in_specs, args = [head, kv_head], [q, kv]
    if has_kr:
        in_specs.append(shared)
        args.append(kr)
    if has_f:
        in_specs += [col_vec, row_vec]
        args += [cum_col, cum_row]
    return pl.pallas_call(
        body, out_shape=[jax.ShapeDtypeStruct((rows, heads * LANE), BF), jax.ShapeDtypeStruct((heads, rows, 1), F32)],
        grid=(heads // group_size,), in_specs=in_specs, out_specs=[head, col_vec], name=name,
        compiler_params=_params(("arbitrary",)))(*args)


def _attn_backward(q, kv, kv_off, kr, cum_col, cum_row, o, do, lse, scale, group_size, name):
    rows = q.shape[0]
    heads = HEADS
    t = ATTN_TILE
    nb = rows // t
    has_kr = kr is not None
    has_f = cum_col is not None
    group = range(group_size)

    def body(*refs):
        it = iter(refs)
        q_ref, kv_ref = next(it), next(it)
        kr_ref = next(it) if has_kr else None
        cc_ref = next(it) if has_f else None
        cr_ref = next(it) if has_f else None
        o_ref, do_ref, lse_ref = next(it), next(it), next(it)
        dq_ref, dkv_ref = next(it), next(it)
        dkr_ref = next(it) if has_kr else None
        dck_ref = next(it) if has_f else None
        dcq_ref = next(it) if has_f else None
        dq_acc = next(it)
        lo = lax.broadcasted_iota(jnp.int32, (1, LANE), 1) < HEAD_DIM
        causal = (lax.broadcasted_iota(jnp.int32, (t, t), 1) <= lax.broadcasted_iota(jnp.int32, (t, t), 0))
        lanes = [slice(g * LANE, (g + 1) * LANE) for g in group]

        dq_acc[...] = jnp.zeros_like(dq_acc)
        if has_kr:
            _init_acc(dkr_ref)
        if has_f:
            dcq_ref[...] = jnp.zeros_like(dcq_ref)

        def kv_block(j, _):
            ks = pl.ds(pl.multiple_of(j * t, t), t)
            other = kr_ref[ks, :] if has_kr else jnp.zeros((t, LANE), BF)
            kvbs = [kv_ref[ks, lanes[g]] for g in group]
            kks = [jnp.where(lo, kvbs[g], other) for g in group]
            cks = [cr_ref[g, :, ks] if has_f else None for g in group]

            def pair(i, carry, diag):
                qs = pl.ds(pl.multiple_of(i * t, t), t)
                nt = (((1,), (1,)), ((), ()))

                def first_stage(g):
                    qb = q_ref[qs, lanes[g]]
                    dob = do_ref[qs, lanes[g]]
                    return (qb, dob, lax.dot_general(qb, kks[g], nt, preferred_element_type=F32),
                            lax.dot_general(dob, kvbs[g], nt, preferred_element_type=F32))

                first = {g: first_stage(g) for g in (group[:1] if has_f else group)}
                out = []
                for g in group:
                    dkk, dvv, dcs = carry[g]
                    qb, dob, logit, dp = first[g]
                    if has_f and g + 1 < len(group):
                        first[g + 1] = first_stage(g + 1)
                    s = logit * scale
                    if has_f:
                        s = s + (cc_ref[g, qs, :] - cks[g])
                    if diag:
                        s = jnp.where(causal, s, NEG_INF)
                    p = jnp.exp(s - lse_ref[g, qs, :])
                    delta = jnp.sum(dob.astype(F32) * o_ref[qs, lanes[g]].astype(F32), axis=1, keepdims=True)
                    ds = p * (dp - delta)
                    dsb = ds.astype(BF)
                    dvv = dvv + lax.dot_general(p.astype(BF), dob, (((0,), (0,)), ((), ())), preferred_element_type=F32)
                    dkk = dkk + lax.dot_general(dsb, qb, (((0,), (0,)), ((), ())), preferred_element_type=F32)
                    dq_acc[qs, lanes[g]] += jnp.dot(dsb, kks[g], preferred_element_type=F32)
                    if has_f:
                        dcs = dcs + jnp.sum(ds, axis=0, keepdims=True)
                        dcq_ref[g, qs, :] += jnp.sum(ds, axis=1, keepdims=True)
                    out.append((dkk, dvv, dcs))
                return tuple(out)

            init = tuple((jnp.zeros((t, LANE), F32), jnp.zeros((t, LANE), F32), jnp.zeros((1, t), F32)) for _ in group)
            carry = pair(j, init, True)
            carry = lax.fori_loop(j + 1, nb, lambda i, c: pair(i, c, False), carry)
            for g, (dkk, dvv, dcs) in enumerate(carry):
                dkk = dkk * scale
                dkv_ref[ks, lanes[g]] = jnp.where(lo, dkk, dvv).astype(BF)
                if has_kr:
                    dkr_ref[ks, :] += jnp.where(lo, 0.0, dkk)
                if has_f:
                    dck_ref[g, :, ks] = -dcs
            return 0

        lax.fori_loop(0, nb, kv_block, 0)
        dq_ref[...] = (dq_acc[...] * scale).astype(BF)

    head, kv_head, shared, col_vec, row_vec = _attn_specs(rows, kv_off, group_size, has_f)
    in_specs, args = [head, kv_head], [q, kv]
    if has_kr:
        in_specs.append(shared)
        args.append(kr)
    if has_f:
        in_specs += [col_vec, row_vec]
        args += [cum_col, cum_row]
    in_specs += [head, head, col_vec]
    args += [o, do, lse]
    out_shape = [jax.ShapeDtypeStruct((rows, heads * LANE), BF), jax.ShapeDtypeStruct((rows, heads * LANE), BF)]
    out_specs = [head, head]
    if has_kr:
        out_shape.append(jax.ShapeDtypeStruct((rows, LANE), F32))
        out_specs.append(shared)
    if has_f:
        out_shape += [jax.ShapeDtypeStruct((heads, 1, rows), F32), jax.ShapeDtypeStruct((heads, rows, 1), F32)]
        out_specs += [row_vec, col_vec]
    return pl.pallas_call(
        body, out_shape=out_shape, grid=(heads // group_size,), in_specs=in_specs, out_specs=out_specs,
        scratch_shapes=[pltpu.VMEM((rows, group_size * LANE), F32)], name=name,
        compiler_params=_params(("arbitrary",)))(*args)


def _tri_dot(tri, x):
    return jnp.dot(tri, x, preferred_element_type=F32, precision=lax.Precision.HIGHEST)


def _forget_forward(f_raw, b_f):
    rows = f_raw.shape[0]
    t = ATTN_TILE

    def body(f_ref, b_ref, cum_ref):
        tri = (lax.broadcasted_iota(jnp.int32, (t, t), 1) <= lax.broadcasted_iota(jnp.int32, (t, t), 0)).astype(F32)

        def blk(i, carry):
            sl = pl.ds(pl.multiple_of(i * t, t), t)
            xv = f_ref[sl, :] + b_ref[...]
            log_f = jnp.minimum(xv, 0.0) - jnp.log(1.0 + jnp.exp(-jnp.abs(xv)))
            cum_ref[sl, :] = _tri_dot(tri, log_f) + carry
            return carry + jnp.sum(log_f, axis=0, keepdims=True)

        lax.fori_loop(0, rows // t, blk, jnp.zeros((1, LANE), F32))

    return pl.pallas_call(body, out_shape=jax.ShapeDtypeStruct((rows, LANE), F32), name="forget_forward",
                          compiler_params=_params())(f_raw, b_f)


def _forget_backward(f_raw, b_f, dcum):
    rows = f_raw.shape[0]
    t = ATTN_TILE
    nb = rows // t

    def body(f_ref, b_ref, dc_ref, df_ref, db_ref):
        tri = (lax.broadcasted_iota(jnp.int32, (t, t), 1) >= lax.broadcasted_iota(jnp.int32, (t, t), 0)).astype(F32)

        def blk(i, carry):
            later, db = carry
            sl = pl.ds(pl.multiple_of((nb - 1 - i) * t, t), t)
            dc = dc_ref[sl, :]
            dlog = _tri_dot(tri, dc) + later
            xv = f_ref[sl, :] + b_ref[...]
            df = dlog / (1.0 + jnp.exp(xv))
            df_ref[sl, :] = df.astype(BF)
            return later + jnp.sum(dc, axis=0, keepdims=True), db + jnp.sum(df, axis=0, keepdims=True)

        _, db = lax.fori_loop(0, nb, blk, (jnp.zeros((1, LANE), F32), jnp.zeros((1, LANE), F32)))
        db_ref[...] = db

    return pl.pallas_call(body, out_shape=[jax.ShapeDtypeStruct((rows, LANE), BF), jax.ShapeDtypeStruct((1, LANE), F32)],
                          name="forget_backward", compiler_params=_params())(f_raw, b_f, dcum)


def _t5_bucket(dist):
    max_exact = REL_BUCKETS // 2
    n = jnp.maximum(dist.astype(F32), 1.0)
    large = max_exact + (jnp.log(n / max_exact) / math.log(REL_MAX_DIST / max_exact)
                         * (REL_BUCKETS - max_exact)).astype(jnp.int32)
    large = jnp.minimum(large, REL_BUCKETS - 1)
    return jnp.where(dist < max_exact, dist, large)


def _dil_buckets(dilation):
    i = jnp.arange(Q_BLOCK)[:, None]
    j = jnp.arange(Q_BLOCK)[None, :]
    cur = _t5_bucket(jnp.clip(i - j, 0) * dilation).astype(jnp.int32)
    prev = _t5_bucket(jnp.clip(Q_BLOCK + i - j, 0) * dilation).astype(jnp.int32)
    return cur, prev


def _dil_bias_tiles(tbl_ref, bc_ref, bp_ref, bias_ref, group, hp, pairs):
    ii = lax.broadcasted_iota(jnp.int32, (Q_BLOCK, Q_BLOCK), 0)
    jj = lax.broadcasted_iota(jnp.int32, (Q_BLOCK, Q_BLOCK), 1)
    for hh in range(2 * pairs):
        col = group * HEADS + 2 * pairs * hp + hh
        acc_c = jnp.zeros((Q_BLOCK, Q_BLOCK), F32)
        acc_p = jnp.zeros((Q_BLOCK, Q_BLOCK), F32)
        for b in range(REL_BUCKETS):
            val = tbl_ref[b, col]
            acc_c = jnp.where(bc_ref[...] == b, val, acc_c)
            acc_p = jnp.where(bp_ref[...] == b, val, acc_p)
        bias_ref[2 * hh] = jnp.where(jj <= ii, acc_c, NEG_INF)
        bias_ref[2 * hh + 1] = jnp.where(jj >= ii, acc_p, NEG_INF)


def _dil_view(qkv, group, dilation):
    if dilation == 1:
        return qkv
    width = 3 * HEADS * HEAD_DIM
    return qkv[:, group * width:(group + 1) * width].reshape(qkv.shape[0] // dilation, dilation * width)


def _dil_specs(group, dilation, length):
    width = DIL_PAIRS[dilation] * LANE
    per = 8 // DIL_PAIRS[dilation]

    def col(kind):
        if dilation == 1:
            return pl.BlockSpec((length, width), lambda hp, r: (0, (group * 3 + kind) * per + hp))
        return pl.BlockSpec((length, width), lambda hp, r: (0, (r * 3 + kind) * per + hp))

    out = pl.BlockSpec((length, width), lambda hp, r: (0, r * per + hp))
    tile = pl.BlockSpec((Q_BLOCK, Q_BLOCK), lambda hp, r: (0, 0))
    table = pl.BlockSpec(memory_space=pltpu.SMEM)
    return col, out, tile, table


def _dil_forward(view, group, dilation, table, buckets):
    length = view.shape[0]
    rows = length * dilation
    pairs = DIL_PAIRS[dilation]
    nb = length // Q_BLOCK
    scale = HEAD_DIM ** -0.5
    qb = Q_BLOCK

    def body(tbl_ref, bc_ref, bp_ref, q_ref, k_ref, v_ref, o_ref, lse_ref, bias_ref):
        hp = pl.program_id(0)

        @pl.when(pl.program_id(1) == 0)
        def _():
            _dil_bias_tiles(tbl_ref, bc_ref, bp_ref, bias_ref, group, hp, pairs)

        lo = lax.broadcasted_iota(jnp.int32, (1, LANE), 1) < HEAD_DIM
        nt = (((1,), (1,)), ((), ()))

        def blk(n, first):
            cur = pl.ds(0, qb) if first else pl.ds(pl.multiple_of(n * qb, qb), qb)
            prev = None if first else pl.ds(pl.multiple_of((n - 1) * qb, qb), qb)
            logits = []
            for pair in range(pairs):
                lanes = slice(pair * LANE, (pair + 1) * LANE)
                qn = q_ref[cur, lanes] * scale
                for hh in range(2):
                    qm = jnp.where(lo if hh == 0 else ~lo, qn, jnp.zeros_like(qn))
                    s_c = lax.dot_general(qm, k_ref[cur, lanes], nt, preferred_element_type=F32)
                    s_p = None if first else lax.dot_general(qm, k_ref[prev, lanes], nt, preferred_element_type=F32)
                    logits.append((s_c, s_p))
            for pair in range(pairs):
                lanes = slice(pair * LANE, (pair + 1) * LANE)
                outs, lses = [], []
                for hh in range(2):
                    bias = 4 * pair + 2 * hh
                    s_c, s_p = logits[2 * pair + hh]
                    s_c = s_c + bias_ref[bias]
                    m = jnp.max(s_c, axis=1, keepdims=True)
                    if not first:
                        s_p = s_p + bias_ref[bias + 1]
                        m = jnp.maximum(m, jnp.max(s_p, axis=1, keepdims=True))
                    e_c = jnp.exp(s_c - m)
                    l = jnp.sum(e_c, axis=1, keepdims=True)
                    acc = jnp.dot(e_c.astype(BF), v_ref[cur, lanes], preferred_element_type=F32)
                    if not first:
                        e_p = jnp.exp(s_p - m)
                        l = l + jnp.sum(e_p, axis=1, keepdims=True)
                        acc = acc + jnp.dot(e_p.astype(BF), v_ref[prev, lanes], preferred_element_type=F32)
                    outs.append(acc * (1.0 / l))
                    lses.append(m + jnp.log(l))
                o_ref[cur, lanes] = jnp.where(lo, outs[0], outs[1])
                lse_ref[cur, lanes] = jnp.where(lo, lses[0], lses[1])
            return 0

        blk(0, True)
        if nb > 1:
            lax.fori_loop(1, nb, lambda n, _: blk(n, False), 0)

    col, out, tile, tbl = _dil_specs(group, dilation, length)
    bc, bp = buckets
    o, lse = pl.pallas_call(
        body, out_shape=[jax.ShapeDtypeStruct((length, dilation * D_MODEL), F32)] * 2,
        grid=(8 // pairs, dilation), in_specs=[tbl, tile, tile, col(0), col(1), col(2)], out_specs=[out, out],
        scratch_shapes=[pltpu.VMEM((4 * pairs, qb, qb), F32)], name=f"dilated_forward_{dilation}",
        compiler_params=_params(("arbitrary", "arbitrary")))(
            table, bc, bp, view, view, view)
    return o.reshape(rows, D_MODEL), lse.reshape(rows, D_MODEL)


def _dil_backward(view, group, dilation, table, buckets, do_g, lse, dlt):
    length = view.shape[0]
    rows = length * dilation
    pairs = DIL_PAIRS[dilation]
    nb = length // Q_BLOCK
    scale = HEAD_DIM ** -0.5
    qb = Q_BLOCK

    def body(tbl_ref, bc_ref, bp_ref, q_ref, k_ref, v_ref, do_ref, lse_ref, dlt_ref,
             dq_ref, dk_ref, dv_ref, db_ref, bias_ref, dk_acc, dv_acc):
        hp = pl.program_id(0)

        @pl.when(pl.program_id(1) == 0)
        def _():
            _dil_bias_tiles(tbl_ref, bc_ref, bp_ref, bias_ref, group, hp, pairs)
            db_ref[...] = jnp.zeros_like(db_ref)

        dk_acc[...] = jnp.zeros_like(dk_acc)
        dv_acc[...] = jnp.zeros_like(dv_acc)
        lo = lax.broadcasted_iota(jnp.int32, (1, LANE), 1) < HEAD_DIM
        tn = (((0,), (0,)), ((), ()))
        nt = (((1,), (1,)), ((), ()))

        def blk(n, first):
            cur = pl.ds(0, qb) if first else pl.ds(pl.multiple_of(n * qb, qb), qb)
            prev = None if first else pl.ds(pl.multiple_of((n - 1) * qb, qb), qb)
            inputs = []
            for pair in range(pairs):
                lanes = slice(pair * LANE, (pair + 1) * LANE)
                qn = q_ref[cur, lanes] * scale
                don = do_ref[cur, lanes]
                for hh in range(2):
                    mask = lo if hh == 0 else ~lo
                    qm = jnp.where(mask, qn, jnp.zeros_like(qn))
                    dom = jnp.where(mask, don, jnp.zeros_like(don))
                    stage = [qm, dom, lax.dot_general(qm, k_ref[cur, lanes], nt, preferred_element_type=F32),
                             lax.dot_general(dom, v_ref[cur, lanes], nt, preferred_element_type=F32)]
                    if not first:
                        stage += [lax.dot_general(qm, k_ref[prev, lanes], nt, preferred_element_type=F32),
                                  lax.dot_general(dom, v_ref[prev, lanes], nt, preferred_element_type=F32)]
                    inputs.append(stage)
            for pair in range(pairs):
                lanes = slice(pair * LANE, (pair + 1) * LANE)
                kc = k_ref[cur, lanes]
                if not first:
                    kp = k_ref[prev, lanes]
                lse_n = lse_ref[cur, lanes]
                dlt_n = dlt_ref[cur, lanes]
                dqs = []
                dkc = jnp.zeros((qb, LANE), F32)
                dkp = jnp.zeros((qb, LANE), F32)
                dvc = jnp.zeros((qb, LANE), F32)
                dvp = jnp.zeros((qb, LANE), F32)
                for hh in range(2):
                    bias = 4 * pair + 2 * hh
                    mask = lo if hh == 0 else ~lo
                    qm, dom, s_c, dp_c = inputs[2 * pair + hh][:4]
                    lse_h = jnp.max(jnp.where(mask, lse_n, -3e38), axis=1, keepdims=True)
                    dlt_h = jnp.max(jnp.where(mask, dlt_n, -3e38), axis=1, keepdims=True)
                    p_c = jnp.exp(s_c + bias_ref[bias] - lse_h)
                    ds_c = p_c * (dp_c - dlt_h)
                    db_ref[pair, 2 * hh] += ds_c
                    dsc_b = ds_c.astype(BF)
                    dq = jnp.dot(dsc_b, kc, preferred_element_type=F32)
                    dkc = dkc + lax.dot_general(dsc_b, qm, tn, preferred_element_type=F32)
                    dvc = dvc + lax.dot_general(p_c.astype(BF), dom, tn, preferred_element_type=F32)
                    if not first:
                        s_p, dp_p = inputs[2 * pair + hh][4:]
                        p_p = jnp.exp(s_p + bias_ref[bias + 1] - lse_h)
                        ds_p = p_p * (dp_p - dlt_h)
                        db_ref[pair, 2 * hh + 1] += ds_p
                        dsp_b = ds_p.astype(BF)
                        dq = dq + jnp.dot(dsp_b, kp, preferred_element_type=F32)
                        dkp = dkp + lax.dot_general(dsp_b, qm, tn, preferred_element_type=F32)
                        dvp = dvp + lax.dot_general(p_p.astype(BF), dom, tn, preferred_element_type=F32)
                    dqs.append(dq)
                dq_ref[cur, lanes] = (jnp.where(lo, dqs[0], dqs[1]) * scale).astype(BF)
                dk_acc[cur, lanes] += dkc
                dv_acc[cur, lanes] += dvc
                if not first:
                    dk_acc[prev, lanes] += dkp
                    dv_acc[prev, lanes] += dvp
            return 0

        blk(0, True)
        if nb > 1:
            lax.fori_loop(1, nb, lambda n, _: blk(n, False), 0)
        dk_ref[...] = dk_acc[...].astype(BF)
        dv_ref[...] = dv_acc[...].astype(BF)

    col, out, tile, tbl = _dil_specs(group, dilation, length)
    bc, bp = buckets
    wide = (length, dilation * D_MODEL)
    dq, dk, dv, db = pl.pallas_call(
        body, out_shape=[jax.ShapeDtypeStruct(wide, BF)] * 3 + [jax.ShapeDtypeStruct((8, 4, qb, qb), F32)],
        grid=(8 // pairs, dilation), in_specs=[tbl, tile, tile, col(0), col(1), col(2), out, out, out],
        out_specs=[out, out, out, pl.BlockSpec((pairs, 4, qb, qb), lambda hp, r: (hp, 0, 0, 0))],
        scratch_shapes=[pltpu.VMEM((4 * pairs, qb, qb), F32), pltpu.VMEM((length, pairs * LANE), F32),
                        pltpu.VMEM((length, pairs * LANE), F32)],
        name=f"dilated_backward_{dilation}", compiler_params=_params(("arbitrary", "arbitrary")))(
            table, bc, bp, view, view, view,
            do_g.reshape(wide), lse.reshape(wide), dlt.reshape(wide))
    return dq.reshape(rows, D_MODEL), dk.reshape(rows, D_MODEL), dv.reshape(rows, D_MODEL), db


def _head_sums(x, lo):
    s0 = jnp.sum(jnp.where(lo, x, 0.0), axis=1, keepdims=True)
    s1 = jnp.sum(jnp.where(lo, 0.0, x), axis=1, keepdims=True)
    return jnp.where(lo, s0, s1)


def _dil_merge_forward(outs, lses):
    rows = outs[0].shape[0]

    def body(o0, o1, o2, l0, l1, l2, o_ref):
        ls = [l0[...], l1[...], l2[...]]
        m = jnp.maximum(jnp.maximum(ls[0], ls[1]), ls[2])
        es = [jnp.exp(v - m) for v in ls]
        tot = es[0] + es[1] + es[2]
        o_ref[...] = ((es[0] * o0[...] + es[1] * o1[...] + es[2] * o2[...]) / tot).astype(BF)

    blk = pl.BlockSpec((ROW_TILE, LANE), lambda i, j: (i, j))
    return pl.pallas_call(body, out_shape=jax.ShapeDtypeStruct((rows, D_MODEL), BF), grid=(rows // ROW_TILE, 8),
                          in_specs=[blk] * 6, out_specs=blk, name="dilated_merge_forward",
                          compiler_params=_params(("parallel", "parallel")))(*outs, *lses)


def _dil_merge_backward(outs, lses, do):
    rows = outs[0].shape[0]

    def body(o0, o1, o2, l0, l1, l2, do_ref, d0, d1, d2, t0, t1, t2):
        lo = lax.broadcasted_iota(jnp.int32, (1, LANE), 1) < HEAD_DIM
        ls = [l0[...], l1[...], l2[...]]
        os_ = [o0[...], o1[...], o2[...]]
        m = jnp.maximum(jnp.maximum(ls[0], ls[1]), ls[2])
        es = [jnp.exp(v - m) for v in ls]
        inv = 1.0 / (es[0] + es[1] + es[2])
        alphas = [e * inv for e in es]
        dov = do_ref[...]
        merged = alphas[0] * os_[0] + alphas[1] * os_[1] + alphas[2] * os_[2]
        dot = _head_sums(dov * merged, lo)
        for a, d_ref, t_ref in zip(alphas, (d0, d1, d2), (t0, t1, t2)):
            d_ref[...] = (a * dov).astype(BF)
            t_ref[...] = a * dot

    blk = pl.BlockSpec((ROW_TILE, LANE), lambda i, j: (i, j))
    res = pl.pallas_call(
        body, out_shape=[jax.ShapeDtypeStruct((rows, D_MODEL), BF)] * 3 + [jax.ShapeDtypeStruct((rows, D_MODEL), F32)] * 3,
        grid=(rows // ROW_TILE, 8), in_specs=[blk] * 7, out_specs=[blk] * 6, name="dilated_merge_backward",
        compiler_params=_params(("parallel", "parallel")))(*outs, *lses, do)
    return res[:3], res[3:]


def _rel_bias_grad(dbs, buckets):
    def body(db_ref, bc_ref, bp_ref, o_ref):
        g = pl.program_id(0)
        hp = pl.program_id(1)

        @pl.when((g == 0) & (hp == 0))
        def _():
            o_ref[...] = jnp.zeros_like(o_ref)

        rr = lax.broadcasted_iota(jnp.int32, (REL_BUCKETS, LANE), 0)
        cc = lax.broadcasted_iota(jnp.int32, (REL_BUCKETS, LANE), 1)
        bc = bc_ref[0]
        bp = bp_ref[0]
        acc = jnp.zeros((REL_BUCKETS, LANE), F32)
        for hh in range(2):
            col = g * HEADS + 2 * hp + hh
            d_c = db_ref[0, 0, 2 * hh]
            d_p = db_ref[0, 0, 2 * hh + 1]
            for b in range(REL_BUCKETS):
                val = (jnp.sum(jnp.where(bc == b, d_c, 0.0), keepdims=True)
                       + jnp.sum(jnp.where(bp == b, d_p, 0.0), keepdims=True))
                acc = jnp.where((rr == b) & (cc == col), val, acc)
        o_ref[...] += acc

    db_all = jnp.stack(dbs)
    bc_all = jnp.stack([b[0] for b in buckets])
    bp_all = jnp.stack([b[1] for b in buckets])
    tile = pl.BlockSpec((1, Q_BLOCK, Q_BLOCK), lambda g, hp: (g, 0, 0))
    return pl.pallas_call(
        body, out_shape=jax.ShapeDtypeStruct((REL_BUCKETS, LANE), F32), grid=(3, 8),
        in_specs=[pl.BlockSpec((1, 1, 4, Q_BLOCK, Q_BLOCK), lambda g, hp: (g, hp, 0, 0, 0)), tile, tile],
        out_specs=pl.BlockSpec((REL_BUCKETS, LANE), lambda g, hp: (0, 0)), name="rel_bias_grad",
        compiler_params=_params(("arbitrary", "arbitrary")))(db_all, bc_all, bp_all)


def _mla_forward(hn, w, tables):
    a = _matmul(hn, w["w_a"], name="mla_a")
    cq, ckv, kr = _mla_mid_forward(a, w["q_norm"], w["kv_norm"], tables)
    q_raw = _matmul(cq, w["w_uq"], name="mla_uq")
    q = _rope_heads(q_raw, tables, False, "rope_forward")
    kv = _matmul(ckv, w["w_ukv"], b_chunks=True, out_dtype=BF, name="mla_ukv")
    scale = (HEAD_DIM + MLA_ROPE) ** -0.5
    o, lse = _attn_forward(q, kv, 0, kr, None, None, scale, FORWARD_GROUP, "mla_attention_forward")
    y = _matmul(o, w["w_o"], name="attn_out")
    return y, dict(hn=hn, a=a, cq=cq, ckv=ckv, kr=kr, q=q, kv=kv, o=o, lse=lse)


def _mla_backward(dy, w, s, tables):
    scale = (HEAD_DIM + MLA_ROPE) ** -0.5
    g = {}
    g["w_o"] = _matmul(s["o"], dy, ta=True, out_dtype=BF, name="attn_out_dw")
    do = _matmul(dy, w["w_o"], tb=True, out_dtype=BF, name="attn_out_dx")
    dq, dkv, dkr = _attn_backward(s["q"], s["kv"], 0, s["kr"], None, None, s["o"], do, s["lse"], scale,
                                  MLA_GROUP, "mla_attention_backward")
    dq_raw = _rope_heads(dq, tables, True, "rope_backward")
    g["w_uq"] = _matmul(s["cq"], dq_raw, ta=True, out_dtype=BF, name="mla_uq_dw")
    dcq = _matmul(dq_raw, w["w_uq"], tb=True, name="mla_uq_dx")
    g["w_ukv"] = _matmul(s["ckv"], dkv, ta=True, out_chunks=True, out_dtype=BF, name="mla_ukv_dw")
    dckv = _matmul(dkv, w["w_ukv"], tb=True, b_chunks=True, name="mla_ukv_dx")
    da, g["q_norm"], g["kv_norm"] = _mla_mid_backward(s["a"], w["q_norm"], w["kv_norm"], tables, dcq, dckv, dkr)
    g["w_a"] = _matmul(s["hn"], da, ta=True, out_dtype=BF, name="mla_a_dw")
    dhn = _matmul(da, w["w_a"], tb=True, name="mla_a_dx")
    return dhn, g


def _fox_forward(hn, w):
    qkv = _matmul(hn, w["w_qkv"], out_dtype=BF, name="fox_qkv")
    f_raw = _matmul(hn, w["w_f"], name="fox_f")
    cum = _forget_forward(f_raw, w["b_f"])
    cum_heads = cum[:, :HEADS].T
    cum_col, cum_row = cum_heads[:, :, None], cum_heads[:, None, :]
    o, lse = _attn_forward(qkv, qkv, HEADS, None, cum_col, cum_row, HEAD_DIM ** -0.5, FORWARD_GROUP,
                           "fox_attention_forward")
    y = _matmul(o, w["w_o"], name="attn_out")
    return y, dict(hn=hn, qkv=qkv, f_raw=f_raw, cum_col=cum_col, cum_row=cum_row, o=o, lse=lse)


def _fox_backward(dy, w, s):
    g = {}
    g["w_o"] = _matmul(s["o"], dy, ta=True, out_dtype=BF, name="attn_out_dw")
    do = _matmul(dy, w["w_o"], tb=True, out_dtype=BF, name="attn_out_dx")
    dq, dkv, dck, dcq = _attn_backward(s["qkv"], s["qkv"], HEADS, None, s["cum_col"], s["cum_row"], s["o"], do,
                                       s["lse"], HEAD_DIM ** -0.5, FOX_GROUP, "fox_attention_backward")
    dcum = jnp.pad((dck[:, 0, :] + dcq[:, :, 0]).T, ((0, 0), (0, LANE - HEADS)))
    df, g["b_f"] = _forget_backward(s["f_raw"], w["b_f"], dcum)
    dqkv = jnp.concatenate([dq, dkv], axis=1)
    g["w_qkv"] = _matmul(s["hn"], dqkv, ta=True, out_dtype=BF, name="fox_qkv_dw")
    g["w_f"] = _matmul(s["hn"], df, ta=True, out_dtype=BF, name="fox_f_dw")
    dhn = _matmul(dqkv, w["w_qkv"], tb=True, name="fox_qkv_dx")
    dhn = _matmul(df, w["w_f"], tb=True, add=dhn, name="fox_f_dx")
    return dhn, g


def _dil_mixer_forward(hn, w, buckets):
    qkv = _matmul(hn, w["w_qkv"], b_chunks=True, out_dtype=BF, name="dil_qkv")
    views = [_dil_view(qkv, grp, dilation) for grp, (_, dilation) in enumerate(DIL_PATTERNS)]
    outs, lses = [], []
    for grp, (_, dilation) in enumerate(DIL_PATTERNS):
        o_g, lse_g = _dil_forward(views[grp], grp, dilation, w["rel_bias"], buckets[grp])
        outs.append(o_g)
        lses.append(lse_g)
    o = _dil_merge_forward(outs, lses)
    y = _matmul(o, w["w_o"], name="dil_out")
    return y, dict(hn=hn, views=views, outs=outs, lses=lses, o=o)


def _dil_mixer_backward(dy, w, s, buckets):
    g = {}
    g["w_o"] = _matmul(s["o"], dy, ta=True, out_dtype=BF, name="dil_out_dw")
    do = _matmul(dy, w["w_o"], tb=True, name="dil_out_dx")
    do_gs, dlts = _dil_merge_backward(s["outs"], s["lses"], do)
    parts, dbs = [], []
    for grp, (_, dilation) in enumerate(DIL_PATTERNS):
        dq, dk, dv, db = _dil_backward(s["views"][grp], grp, dilation, w["rel_bias"], buckets[grp], do_gs[grp],
                                       s["lses"][grp], dlts[grp])
        parts += [dq, dk, dv]
        dbs.append(db)
    dqkv = jnp.concatenate(parts, axis=1)
    g["rel_bias"] = _rel_bias_grad(dbs, buckets)
    g["w_qkv"] = _matmul(s["hn"], dqkv, ta=True, out_chunks=True, out_dtype=BF, name="dil_qkv_dw")
    dhn = _matmul(dqkv, w["w_qkv"], tb=True, b_chunks=True, name="dil_qkv_dx")
    return dhn, g


def _mixer_weights(i, lw, small):
    mixer, j = i % N_MIXERS, i // N_MIXERS
    if mixer == 0:
        return dict(lw["mixer"], q_norm=small["mla_q_norm"][j][None, :], kv_norm=small["mla_kv_norm"][j][None, :])
    if mixer == 1:
        return dict(lw["mixer"], rel_bias=small["rel_bias"])
    return dict(lw["mixer"], b_f=jnp.pad(small["fox_b_f"][j][None, :], ((0, 0), (0, LANE - HEADS))))


MIXER_PART, COMMON_PART = 0, 1


def _run_layers(x, p, positions, target, get_part, get_small, put_part):
    tables = _rope_tables(positions)
    buckets = [_dil_buckets(d) for _, d in DIL_PATTERNS]
    layers, saved = [], []
    h = x
    first = get_part(0, MIXER_PART, positions)
    small = get_small()

    def gain(i, k):
        return small["norm_g"][i, k][None, :]

    hn = _prenorm(h, gain(0, 0))
    sq = dh = None
    for i in range(DEPTH):
        mixer = i % N_MIXERS
        lw = dict(first if i == 0 else get_part(i, MIXER_PART, h))
        mw = _mixer_weights(i, lw, small)
        if mixer == 0:
            y, ms = _mla_forward(hn, mw, tables)
        elif mixer == 1:
            y, ms = _dil_mixer_forward(hn, mw, buckets)
        else:
            y, ms = _fox_forward(hn, mw)
        if "ffn_w_in" not in lw:
            lw.update(get_part(i, COMMON_PART, y))
        layers.append(lw)
        h1, hn2 = _post_residual(h, y, gain(i, 1), gain(i, 2))
        gu = _matmul(hn2, lw["ffn_w_in"], b_chunks=True, out_dtype=BF, name="ffn_in")
        act = _swiglu_forward(gu)
        f = _matmul(act, lw["ffn_w_out"], name="ffn_out")
        h2, h2b = _post_residual(h1, f, gain(i, 3), None)
        pp = _matmul(p[i], lw["ple_w_proj"], b_chunks=True, name="ple_proj")
        z = _matmul(h2b, lw["ple_w_gate"], name="ple_gate")
        saved.append(dict(h=h, y=y, ms=ms, h1=h1, hn2=hn2, gu=gu, act=act, f=f, h2b=h2b, pp=pp, z=z))
        if i + 1 < DEPTH:
            h, hn = _ple_forward(h2, pp, z, gain(i + 1, 0))
        else:
            dh, sq = _ple_loss(h2, pp, z, target)

    norm_rows = [[None] * 4 for _ in range(DEPTH)]
    sg = dict(mla_q_norm={}, mla_kv_norm={}, rel_bias=None, fox_b_f={})
    for i in reversed(range(DEPTH)):
        s, lw = saved[i], layers[i]
        mixer, j = i % N_MIXERS, i // N_MIXERS
        mw = _mixer_weights(i, lw, small)
        lg = {}
        dpp, dz = _ple_backward(dh, s["pp"], s["z"])
        lg["ple_w_proj"] = _matmul(p[i], dpp, ta=True, out_chunks=True, out_dtype=BF, name="ple_proj_dw")
        lg["ple_w_gate"] = _matmul(s["h2b"], dz, ta=True, out_dtype=BF, name="ple_gate_dw")
        dh2 = _matmul(dz, lw["ple_w_gate"], tb=True, add=dh, name="ple_gate_dx")
        df, norm_rows[i][3] = _rms_backward(s["f"], gain(i, 3), dh2, None, BF)
        lg["ffn_w_out"] = _matmul(s["act"], df, ta=True, out_dtype=BF, name="ffn_out_dw")
        dact = _matmul(df, lw["ffn_w_out"], tb=True, out_dtype=BF, name="ffn_out_dx")
        dgu = _swiglu_backward(s["gu"], dact)
        lg["ffn_w_in"] = _matmul(s["hn2"], dgu, ta=True, out_chunks=True, out_dtype=BF, name="ffn_in_dw")
        split = i in SPLIT_LAYERS
        zero = put_part(i, COMMON_PART, lg)[0:1, 0:1] if split else 0.0
        dhn2 = _matmul(dgu, lw["ffn_w_in"], tb=True, b_chunks=True, name="ffn_in_dx")
        dh1, norm_rows[i][2] = _rms_backward(s["h1"], gain(i, 2), dhn2, dh2, F32)
        dy, norm_rows[i][1] = _rms_backward(s["y"], gain(i, 1) + zero, dh1, None, BF)
        if mixer == 0:
            dhn, mg = _mla_backward(dy, mw, s["ms"], tables)
            sg["mla_q_norm"][j] = mg.pop("q_norm")
            sg["mla_kv_norm"][j] = mg.pop("kv_norm")
        elif mixer == 1:
            dhn, mg = _dil_mixer_backward(dy, mw, s["ms"], buckets)
            rel = mg.pop("rel_bias")[:, :3 * HEADS]
            sg["rel_bias"] = rel if sg["rel_bias"] is None else sg["rel_bias"] + rel
        else:
            dhn, mg = _fox_backward(dy, mw, s["ms"])
            sg["fox_b_f"][j] = mg.pop("b_f")[:, :HEADS]
        token = put_part(i, MIXER_PART, dict(mixer=mg) if split else dict(lg, mixer=mg))
        dh, norm_rows[i][0] = _rms_backward(s["h"], gain(i, 0) + token[0:1, 0:1], dhn, dh1, F32)
    small_grads = dict(norm_g=jnp.stack([jnp.concatenate(row, axis=0) for row in norm_rows]),
                       rel_bias=sg["rel_bias"])
    for k in ("mla_q_norm", "mla_kv_norm", "fox_b_f"):
        small_grads[k] = jnp.concatenate([sg[k][j] for j in sorted(sg[k])], axis=0)
    return sq, dh, small_grads


BIG = ("ffn_w_in", "ffn_w_out", "ple_w_proj", "ple_w_gate", "mla_w_a", "mla_w_uq", "mla_w_ukv", "mla_w_o",
       "dil_w_qkv", "dil_w_o", "fox_w_qkvf", "fox_w_o")
SMALL_SHARDED = ("norm_g", "mla_q_norm", "mla_kv_norm")
SMALL_REPLICATED = ("rel_bias", "fox_b_f")
WEIGHTS = ("norm_g", "ffn_w_in", "ffn_w_out", "ple_w_proj", "ple_w_gate", "rel_bias", "mla_w_a", "mla_q_norm",
           "mla_kv_norm", "mla_w_uq", "mla_w_ukv", "mla_w_o", "dil_w_qkv", "dil_w_o", "fox_w_qkvf", "fox_b_f", "fox_w_o")


TRANSPOSED = "fox_w_qkvf"
SPLIT_LAYERS = (0, 1, 2, 3)
LAYER_COMMON = ("ffn_w_in", "ffn_w_out", "ple_w_proj", "ple_w_gate")
MIXER_WEIGHTS = (("mla_w_a", "mla_w_uq", "mla_w_ukv", "mla_w_o"), ("dil_w_qkv", "dil_w_o"), ("fox_w_qkvf", "fox_w_o"))


def _part_names(i, part):
    if i in SPLIT_LAYERS:
        return MIXER_WEIGHTS[i % N_MIXERS] if part == MIXER_PART else LAYER_COMMON
    return MIXER_WEIGHTS[i % N_MIXERS] + LAYER_COMMON if part == MIXER_PART else ()


def _layer_slot(name, i):
    return i if name in LAYER_COMMON else i // N_MIXERS


def _merge_rows(chunks):
    n, r, c = chunks.shape
    return chunks.reshape(n * r, c)


def _merge_cols(chunks):
    n, r, c = chunks.shape
    return chunks.transpose(1, 0, 2).reshape(r, n * c)


def _pad_heads_out(wo):
    w3 = wo.reshape(HEADS, HEAD_DIM, D_MODEL)
    return jnp.pad(w3, ((0, 0), (HEAD_DIM, 0), (0, 0))).reshape(HEADS * LANE, D_MODEL)


def _part_to_compute(i, part, ch):
    lw = {}
    if "ffn_w_in" in ch:
        lw.update(ffn_w_in=ch["ffn_w_in"], ffn_w_out=_merge_rows(ch["ffn_w_out"]), ple_w_proj=ch["ple_w_proj"],
                  ple_w_gate=_merge_rows(ch["ple_w_gate"]))
    if part == COMMON_PART:
        return lw
    mixer = i % N_MIXERS
    if mixer == 0:
        wa = _merge_rows(ch["mla_w_a"])
        rank = MLA_Q_RANK + MLA_KV_RANK
        wa_p = jnp.concatenate([wa[:, :rank], jnp.zeros((wa.shape[0], 64), wa.dtype), wa[:, rank:],
                                jnp.zeros((wa.shape[0], 32), wa.dtype)], axis=1)
        wuq = _merge_cols(ch["mla_w_uq"]).reshape(MLA_Q_RANK, HEADS, HEAD_DIM + MLA_ROPE)
        wuq_p = jnp.pad(wuq, ((0, 0), (0, 0), (0, LANE - HEAD_DIM - MLA_ROPE))).reshape(MLA_Q_RANK, HEADS * LANE)
        lw["mixer"] = dict(w_a=wa_p, w_uq=wuq_p, w_ukv=ch["mla_w_ukv"], w_o=_pad_heads_out(_merge_rows(ch["mla_w_o"])))
    elif mixer == 1:
        lw["mixer"] = dict(w_qkv=ch["dil_w_qkv"], w_o=_merge_rows(ch["dil_w_o"]))
    else:
        wf = _merge_rows(ch["fox_w_qkvf"]).T
        inner = HEADS * HEAD_DIM
        q3 = wf[:, :inner].reshape(D_MODEL, HEADS, HEAD_DIM)
        k3 = wf[:, inner:2 * inner].reshape(D_MODEL, HEADS, HEAD_DIM)
        v3 = wf[:, 2 * inner:3 * inner].reshape(D_MODEL, HEADS, HEAD_DIM)
        q_p = jnp.pad(q3, ((0, 0), (0, 0), (0, HEAD_DIM))).reshape(D_MODEL, HEADS * LANE)
        kv_p = jnp.concatenate([k3, v3], axis=2).reshape(D_MODEL, HEADS * LANE)
        f_p = jnp.pad(wf[:, 3 * inner:], ((0, 0), (0, LANE - HEADS)))
        lw["mixer"] = dict(w_qkv=jnp.concatenate([q_p, kv_p], axis=1), w_f=f_p,
                           w_o=_pad_heads_out(_merge_rows(ch["fox_w_o"])))
    return lw


def _part_contributions(i, part, lg, chunk_shapes):
    spec = {k: jax.ShapeDtypeStruct(s, BF) for k, s in chunk_shapes.items()}
    (contrib,) = jax.linear_transpose(functools.partial(_part_to_compute, i, part), spec)(lg)
    return contrib


def _chip_peers():
    x, y, c = lax.axis_index("x"), lax.axis_index("y"), lax.axis_index("c")
    peers = [(1 - x, y), (x, 1 - y), (1 - x, 1 - y)]
    return x, y, c, peers


SEM_SPEC = pl.BlockSpec(memory_space=pltpu.SEMAPHORE)
ANY_SPEC = pl.BlockSpec(memory_space=pl.ANY)
SPLIT_EFFECT = pltpu.SideEffectType.DATAFLOW_SIDE_EFFECTING


def _own_slot(shard):
    me = 2 * lax.axis_index("x") + lax.axis_index("y")
    return lax.dynamic_update_index_in_dim(lax.empty((N_CHIPS,) + shard.shape, shard.dtype), shard[None], me, 0)


def _spread_copy(src, land, k, peer, c, send_sems, recv_sems, index, src_slot, slot):
    px, py = peer
    return pltpu.make_async_remote_copy(
        src_ref=src.at[src_slot], dst_ref=land.at[slot],
        send_sem=send_sems.at[3 * index + k], recv_sem=recv_sems.at[3 * index + k],
        device_id=(px, py, c), device_id_type=MESH)


def _spread_start(bufs, srcs, after, name):
    n = len(bufs)
    exchange = srcs is not None
    arrays = (list(srcs) if exchange else []) + list(bufs)
    na = len(arrays)

    def body(*refs):
        src, land = refs[:n], refs[na - n:na]
        send_sems, recv_sems = refs[na + 1], refs[na + 2]
        token = refs[-1]
        x, y, c, peers = _chip_peers()
        me = 2 * x + y
        for w in range(n):
            for k, peer in enumerate(peers):
                src_slot = 2 * peer[0] + peer[1] if exchange else me
                _spread_copy(src[w], land[w], k, peer, c, send_sems, recv_sems, w, src_slot, me).start()
        token[...] = jnp.zeros_like(token)

    hbm = [pltpu.with_memory_space_constraint(a, pltpu.HBM) for a in arrays]
    out = pl.pallas_call(
        body, name=name,
        out_shape=(pltpu.SemaphoreType.DMA((3 * n,)), pltpu.SemaphoreType.DMA((3 * n,)),
                   *[pltpu.HBM(a.shape, a.dtype) for a in hbm], jax.ShapeDtypeStruct((8, LANE), F32)),
        in_specs=[HBM_SPEC] * na + [ANY_SPEC],
        out_specs=(SEM_SPEC, SEM_SPEC, *[HBM_SPEC] * na, pl.BlockSpec(memory_space=pltpu.VMEM)),
        input_output_aliases={w: 2 + w for w in range(na)},
        compiler_params=pltpu.CompilerParams(has_side_effects=SPLIT_EFFECT))(*hbm, after)
    return dict(send=out[0], recv=out[1], arrays=out[2:2 + na], n=n, token=out[-1], exchange=exchange)


def _spread_wait(handle, after, name):
    n, exchange = handle["n"], handle["exchange"]
    arrays = list(handle["arrays"])
    na = len(arrays)

    def body(*refs):
        src, land = refs[:n], refs[na - n:na]
        send_sems, recv_sems = refs[na], refs[na + 1]
        x, y, c, peers = _chip_peers()
        me = 2 * x + y
        for w in range(n):
            for k, peer in enumerate(peers):
                there = 2 * peer[0] + peer[1]
                cp = _spread_copy(src[w], land[w], k, peer, c, send_sems, recv_sems, w, there if exchange else me, there)
                cp.wait_send()
                cp.wait_recv()

    out = pl.pallas_call(
        body, name=name, out_shape=tuple(pltpu.HBM(a.shape, a.dtype) for a in arrays),
        in_specs=[HBM_SPEC] * na + [SEM_SPEC, SEM_SPEC, ANY_SPEC], out_specs=tuple([HBM_SPEC] * na),
        input_output_aliases={w: w for w in range(na)},
        compiler_params=pltpu.CompilerParams(has_side_effects=SPLIT_EFFECT))(*arrays, handle["send"], handle["recv"], after)
    return (list(out[n:]), list(out[:n])) if exchange else list(out)


def _sibling_copy(received, sent, land, k, me, peers, sibling, send_sems, recv_sems, index):
    slot = me if k == 3 else 2 * peers[k][0] + peers[k][1]
    src = sent if k == 3 else received
    return pltpu.make_async_remote_copy(
        src_ref=src.at[slot], dst_ref=land.at[slot], send_sem=send_sems.at[4 * index + k],
        recv_sem=recv_sems.at[4 * index + k], device_id=sibling, device_id_type=MESH)


def _sibling_start(received, sent, after, name):
    n = len(received)
    lands = [lax.empty(a.shape, a.dtype) for a in received]
    arrays = list(received) + list(sent) + lands

    def body(*refs):
        rec, snt, land = refs[:n], refs[n:2 * n], refs[2 * n:3 * n]
        send_sems, recv_sems = refs[3 * n + 1], refs[3 * n + 2]
        token = refs[-1]
        x, y, c, peers = _chip_peers()
        for w in range(n):
            for k in range(4):
                _sibling_copy(rec[w], snt[w], land[w], k, 2 * x + y, peers, (x, y, 1 - c), send_sems, recv_sems, w).start()
        token[...] = jnp.zeros_like(token)

    hbm = [pltpu.with_memory_space_constraint(a, pltpu.HBM) for a in arrays]
    out = pl.pallas_call(
        body, name=name,
        out_shape=(pltpu.SemaphoreType.DMA((4 * n,)), pltpu.SemaphoreType.DMA((4 * n,)),
                   *[pltpu.HBM(a.shape, a.dtype) for a in hbm], jax.ShapeDtypeStruct((8, LANE), F32)),
        in_specs=[HBM_SPEC] * (3 * n) + [ANY_SPEC],
        out_specs=(SEM_SPEC, SEM_SPEC, *[HBM_SPEC] * (3 * n), pl.BlockSpec(memory_space=pltpu.VMEM)),
        input_output_aliases={w: 2 + w for w in range(3 * n)},
        compiler_params=pltpu.CompilerParams(has_side_effects=SPLIT_EFFECT))(*hbm, after)
    return dict(send=out[0], recv=out[1], arrays=out[2:2 + 3 * n], n=n, token=out[-1])


def _sibling_wait(handle, after, name):
    n = handle["n"]
    arrays = list(handle["arrays"])

    def body(*refs):
        rec, snt, land = refs[:n], refs[n:2 * n], refs[2 * n:3 * n]
        send_sems, recv_sems = refs[3 * n], refs[3 * n + 1]
        x, y, c, peers = _chip_peers()
        for w in range(n):
            for k in range(4):
                cp = _sibling_copy(rec[w], snt[w], land[w], k, 2 * x + y, peers, (x, y, 1 - c), send_sems, recv_sems, w)
                cp.wait_send()
                cp.wait_recv()

    out = pl.pallas_call(
        body, name=name, out_shape=tuple(pltpu.HBM(a.shape, a.dtype) for a in arrays),
        in_specs=[HBM_SPEC] * (3 * n) + [SEM_SPEC, SEM_SPEC, ANY_SPEC], out_specs=tuple([HBM_SPEC] * (3 * n)),
        input_output_aliases={w: w for w in range(3 * n)},
        compiler_params=pltpu.CompilerParams(has_side_effects=SPLIT_EFFECT))(*arrays, handle["send"], handle["recv"], after)
    return list(out[:n]), list(out[n:2 * n]), list(out[2 * n:])


def _all_reduce_small(v):
    rows = v.shape[0]

    def body(v_ref, sum_ref, slots, send_sems, recv_sems):
        x, y, c = lax.axis_index("x"), lax.axis_index("y"), lax.axis_index("c")
        me = 4 * x + 2 * y + c
        slots[me] = v_ref[...]
        sends = []
        for k in range(1, N_DEV):
            bx, by, bc = (k >> 2) & 1, (k >> 1) & 1, k & 1
            peer = (x ^ bx, y ^ by, c ^ bc)
            rc = pltpu.make_async_remote_copy(src_ref=v_ref, dst_ref=slots.at[me], send_sem=send_sems.at[k],
                                              recv_sem=recv_sems.at[k], device_id=peer, device_id_type=MESH)
            rc.start()
            sends.append(rc)
        for k in range(1, N_DEV):
            bx, by, bc = (k >> 2) & 1, (k >> 1) & 1, k & 1
            src = 4 * (x ^ bx) + 2 * (y ^ by) + (c ^ bc)
            pltpu.make_async_remote_copy(src_ref=v_ref, dst_ref=slots.at[src], send_sem=send_sems.at[k],
                                         recv_sem=recv_sems.at[k], device_id=(x ^ bx, y ^ by, c ^ bc),
                                         device_id_type=MESH).wait_recv()
        for rc in sends:
            rc.wait_send()
        total = slots[0]
        for k in range(1, N_DEV):
            total = total + slots[k]
        sum_ref[...] = total

    vm = pl.BlockSpec(memory_space=pltpu.VMEM)
    return pl.pallas_call(
        body, out_shape=jax.ShapeDtypeStruct((rows, LANE), F32), in_specs=[vm], out_specs=vm,
        scratch_shapes=[pltpu.VMEM((N_DEV, rows, LANE), F32), pltpu.SemaphoreType.DMA((N_DEV,)),
                        pltpu.SemaphoreType.DMA((N_DEV,))], name="all_reduce_small")(v)


def _as_2d(a):
    return a.reshape(-1, a.shape[-1])


def _row_tile(rows, cols):
    for t in (512, 256, 128, 64, 32, 16):
        if rows % t == 0 and t * cols * 4 <= (1 << 20):
            return t
    return rows


def _adamw_weight(w, m, v, received, sent, sibling):
    layers = len(received)
    _, rows, cols = received[0].shape
    tr = _row_tile(rows, cols)
    by_columns = rows % tr != 0 or tr == rows and rows * cols * 4 > (2 << 20)
    if by_columns:
        assert layers == 1 and cols % (2 * LANE) == 0, (w.shape, received[0].shape)
        tr, tc, steps = rows, cols // 2, 2
        index = lambda i: (0, i)
    else:
        tc, steps = cols, rows // tr
        index = lambda i: (i, 0)
    where = (2 * lax.axis_index("x") + lax.axis_index("y")).astype(jnp.int32).reshape(1)

    def body(where_ref, w_ref, m_ref, v_ref, *rest):
        per_layer, (g_ref, d_ref, nm_ref, nv_ref) = rest[:3 * layers], rest[3 * layers:]
        me = where_ref[0]
        for layer in range(layers):
            r_ref, own_ref, s_ref = per_layer[3 * layer:3 * layer + 3]

            @pl.when(pl.program_id(0) == layer)
            def _():
                mine = theirs = None
                for k in range(N_CHIPS):
                    a = jnp.where(me == k, own_ref[...], r_ref[k]).astype(F32)
                    b = s_ref[k].astype(F32)
                    mine = a if mine is None else mine + a
                    theirs = b if theirs is None else theirs + b
                g = mine + theirs
                delta, nm, nv = _adamw_math(w_ref[...], g, m_ref[...], v_ref[...])
                g_ref[...] = g
                d_ref[...] = delta
                nm_ref[...] = nm
                nv_ref[...] = nv

    def held(layer, now, i):
        return jnp.where(now < layer, 0, jnp.where(now > layer, steps - 1, i))

    if by_columns:
        stacked = pl.BlockSpec((tr, tc), lambda now, i, where_ref: index(i))
    else:
        stacked = pl.BlockSpec((tr, tc), lambda now, i, where_ref: (now * steps + i, 0))
    in_specs = [stacked, stacked, stacked]
    args = [where, w, m, v]
    for layer in range(layers):
        four = pl.BlockSpec((N_CHIPS, tr, tc), lambda now, i, where_ref, layer=layer: (0,) + index(held(layer, now, i)))
        own = pl.BlockSpec((None, tr, tc),
                           lambda now, i, where_ref, layer=layer: (where_ref[0],) + index(held(layer, now, i)))
        in_specs += [four, own, four]
        args += [received[layer], sent[layer], sibling[layer]]
    grid_spec = pltpu.PrefetchScalarGridSpec(num_scalar_prefetch=1, grid=(layers, steps), in_specs=in_specs,
                                             out_specs=[stacked] * 4)
    return pl.pallas_call(body, out_shape=[jax.ShapeDtypeStruct(w.shape, F32)] * 4, grid_spec=grid_spec,
                          name="adamw_weight", compiler_params=_params(("arbitrary", "arbitrary")))(*args)


def _adamw_math(w, g, m, v):
    m = ADAM_B1 * m + (1.0 - ADAM_B1) * g
    v = ADAM_B2 * v + (1.0 - ADAM_B2) * (g * g)
    m_hat = m * (1.0 / (1.0 - ADAM_B1 ** ADAM_STEP))
    v_hat = v * (1.0 / (1.0 - ADAM_B2 ** ADAM_STEP))
    denom = jnp.sqrt(v_hat) + ADAM_EPS
    inv = pl.reciprocal(denom, approx=True)
    inv = inv * (2.0 - denom * inv)
    delta = -ADAM_LR * (m_hat * inv + ADAM_WD * w)
    return delta, m, v


def _adamw(w, m, v, g_mine, g_sibling):
    rows, cols = w.shape
    tr = _row_tile(rows, cols)
    two = g_sibling is not None

    def body(*refs):
        if two:
            w_ref, m_ref, v_ref, ga_ref, gb_ref, g_ref, d_ref, nm_ref, nv_ref = refs
            g = ga_ref[...] + gb_ref[...]
        else:
            w_ref, m_ref, v_ref, ga_ref, g_ref, d_ref, nm_ref, nv_ref = refs
            g = ga_ref[...]
        delta, nm, nv = _adamw_math(w_ref[...], g, m_ref[...], v_ref[...])
        g_ref[...] = g
        d_ref[...] = delta
        nm_ref[...] = nm
        nv_ref[...] = nv

    blk = pl.BlockSpec((tr, cols), lambda i: (i, 0))
    args = [w, m, v, g_mine] + ([g_sibling] if two else [])
    return pl.pallas_call(body, out_shape=[jax.ShapeDtypeStruct((rows, cols), F32)] * 4, grid=(rows // tr,),
                          in_specs=[blk] * len(args), out_specs=[blk] * 4, name="adamw",
                          compiler_params=_params(("parallel",)))(*args)


def _pack_rows(arrays):
    flat = jnp.concatenate([a.reshape(-1) for a in arrays])
    rows = -(-flat.shape[0] // (8 * LANE)) * 8
    return jnp.pad(flat, (0, rows * LANE - flat.shape[0])).reshape(rows, LANE)


def _unpack_rows(packed, shapes):
    flat = packed.reshape(-1)
    out, at = [], 0
    for s in shapes:
        size = math.prod(s)
        out.append(flat[at:at + size].reshape(s))
        at += size
    return out


def kernel(x, p, positions, norm_g, ffn_w_in, ffn_w_out, ple_w_proj, ple_w_gate, rel_bias, mla_w_a, mla_q_norm, mla_kv_norm, mla_w_uq, mla_w_ukv, mla_w_o, dil_w_qkv, dil_w_o, fox_w_qkvf, fox_b_f, fox_w_o, loss_target, m_norm_g, m_ffn_w_in, m_ffn_w_out, m_ple_w_proj, m_ple_w_gate, m_rel_bias, m_mla_w_a, m_mla_q_norm, m_mla_kv_norm, m_mla_w_uq, m_mla_w_ukv, m_mla_w_o, m_dil_w_qkv, m_dil_w_o, m_fox_w_qkvf, m_fox_b_f, m_fox_w_o, v_norm_g, v_ffn_w_in, v_ffn_w_out, v_ple_w_proj, v_ple_w_gate, v_rel_bias, v_mla_w_a, v_mla_q_norm, v_mla_kv_norm, v_mla_w_uq, v_mla_w_ukv, v_mla_w_o, v_dil_w_qkv, v_dil_w_o, v_fox_w_qkvf, v_fox_b_f, v_fox_w_o):
    w = dict(norm_g=norm_g, ffn_w_in=ffn_w_in, ffn_w_out=ffn_w_out, ple_w_proj=ple_w_proj, ple_w_gate=ple_w_gate,
             rel_bias=rel_bias, mla_w_a=mla_w_a, mla_q_norm=mla_q_norm, mla_kv_norm=mla_kv_norm, mla_w_uq=mla_w_uq,
             mla_w_ukv=mla_w_ukv, mla_w_o=mla_w_o, dil_w_qkv=dil_w_qkv, dil_w_o=dil_w_o, fox_w_qkvf=fox_w_qkvf,
             fox_b_f=fox_b_f, fox_w_o=fox_w_o)
    m = dict(norm_g=m_norm_g, ffn_w_in=m_ffn_w_in, ffn_w_out=m_ffn_w_out, ple_w_proj=m_ple_w_proj,
             ple_w_gate=m_ple_w_gate, rel_bias=m_rel_bias, mla_w_a=m_mla_w_a, mla_q_norm=m_mla_q_norm,
             mla_kv_norm=m_mla_kv_norm, mla_w_uq=m_mla_w_uq, mla_w_ukv=m_mla_w_ukv, mla_w_o=m_mla_w_o,
             dil_w_qkv=m_dil_w_qkv, dil_w_o=m_dil_w_o, fox_w_qkvf=m_fox_w_qkvf, fox_b_f=m_fox_b_f, fox_w_o=m_fox_w_o)
    v = dict(norm_g=v_norm_g, ffn_w_in=v_ffn_w_in, ffn_w_out=v_ffn_w_out, ple_w_proj=v_ple_w_proj,
             ple_w_gate=v_ple_w_gate, rel_bias=v_rel_bias, mla_w_a=v_mla_w_a, mla_q_norm=v_mla_q_norm,
             mla_kv_norm=v_mla_kv_norm, mla_w_uq=v_mla_w_uq, mla_w_ukv=v_mla_w_ukv, mla_w_o=v_mla_w_o,
             dil_w_qkv=v_dil_w_qkv, dil_w_o=v_dil_w_o, fox_w_qkvf=v_fox_w_qkvf, fox_b_f=v_fox_b_f, fox_w_o=v_fox_w_o)
    chip = 2 * lax.axis_index("x") + lax.axis_index("y")
    for tree in (w, m, v):
        tree[TRANSPOSED] = jnp.swapaxes(tree[TRANSPOSED], 1, 2)

    small_shapes = [w[k].shape for k in SMALL_SHARDED]
    order = [(i, part) for i in range(DEPTH) for part in (MIXER_PART, COMMON_PART) if _part_names(i, part)]
    gathers = {}
    after = positions
    zero = 0.0
    for i, part in order:
        bufs = [_own_slot((w[k][_layer_slot(k, i)] + zero).astype(BF)) for k in _part_names(i, part)]
        if (i, part) == order[0]:
            bufs.append(_own_slot(_pack_rows([w[k] for k in SMALL_SHARDED])))
        gathers[i, part] = _spread_start(bufs, None, after, f"gather_start_{i}_{part}")
        after = gathers[i, part]["token"]
        if (i, part) == order[0]:
            zero = after[0, 0]
    all_started = after
    state = {}

    def get_part(i, part, after_array):
        is_first = (i, part) == order[0]
        lands = _spread_wait(gathers[i, part], all_started if is_first else after_array, f"gather_wait_{i}_{part}")
        if is_first:
            pieces = [_unpack_rows(lands[-1][k], small_shapes) for k in range(N_CHIPS)]
            small = {name: jnp.concatenate([pieces[k][idx] for k in range(N_CHIPS)], axis=-1)
                     for idx, name in enumerate(SMALL_SHARDED)}
            state["small"] = dict(small, rel_bias=rel_bias, fox_b_f=fox_b_f)
        chunks = dict(zip(_part_names(i, part), lands))
        state[i, part] = {k: a.shape for k, a in chunks.items()}
        return _part_to_compute(i, part, chunks)

    started, forwards = [], {}

    def forward_oldest(after_array):
        i, part, handle = started.pop(0)
        received, sent = _spread_wait(handle, after_array, f"exchange_wait_{i}_{part}")
        forwards[i, part] = _sibling_start(received, sent, after_array, f"sibling_start_{i}_{part}")
        return forwards[i, part]["token"]

    def put_part(i, part, lg):
        contrib = _part_contributions(i, part, lg, state[i, part])
        srcs = [contrib[k] for k in _part_names(i, part)]
        handle = _spread_start([lax.empty(s.shape, s.dtype) for s in srcs], srcs, positions,
                               f"exchange_start_{i}_{part}")
        token = handle["token"]
        if started:
            token = token + forward_oldest(token)
        started.append((i, part, handle))
        return token

    sq, grad_x, sg = _run_layers(x[0], p[:, 0], positions[0], loss_target[0], get_part, lambda: state["small"],
                                 put_part)
    loss = lax.psum(0.5 / D_MODEL * jnp.sum(sq), ("x", "y", "c"))
    forward_oldest(grad_x)

    held = {k: {} for k in BIG}
    for i, part in sorted(forwards, reverse=True):
        received, sent, sibling = _sibling_wait(forwards[i, part], grad_x, f"sibling_wait_{i}_{part}")
        for k, r, s, t in zip(_part_names(i, part), received, sent, sibling):
            held[k][_layer_slot(k, i)] = (r, s, t)
    results = {}
    for k in BIG:
        per_layer = [held[k][slot] for slot in sorted(held[k])]
        outs = _adamw_weight(_as_2d(w[k]), _as_2d(m[k]), _as_2d(v[k]), *[list(col) for col in zip(*per_layer)])
        results[k] = [o.reshape(w[k].shape) for o in outs]
    results[TRANSPOSED] = [jnp.swapaxes(o, 1, 2) for o in results[TRANSPOSED]]

    small_all = SMALL_SHARDED + SMALL_REPLICATED
    full_shapes = [sg[k].shape for k in small_all]
    reduced = dict(zip(small_all, _unpack_rows(_all_reduce_small(_pack_rows([sg[k] for k in small_all])), full_shapes)))
    local_g = []
    for k in small_all:
        g = reduced[k]
        if k in SMALL_SHARDED:
            width = w[k].shape[-1]
            g = lax.dynamic_slice_in_dim(g, chip * width, width, axis=g.ndim - 1)
        local_g.append(g)
    local_shapes = [w[k].shape for k in small_all]
    outs = _adamw(_pack_rows([w[k] for k in small_all]), _pack_rows([m[k] for k in small_all]),
                  _pack_rows([v[k] for k in small_all]), _pack_rows(local_g), None)
    unpacked = [_unpack_rows(o, local_shapes) for o in outs]
    for idx, k in enumerate(small_all):
        results[k] = [u[idx] for u in unpacked]

    return (loss, grad_x[None], *[results[k][0] for k in WEIGHTS], *[results[k][1] for k in WEIGHTS],
            *[results[k][2] for k in WEIGHTS], *[results[k][3] for k in WEIGHTS])
```

```python
import functools
import math

import jax
import jax.numpy as jnp
from jax import lax
from jax.experimental import pallas as pl
from jax.experimental.pallas import tpu as pltpu

F32 = jnp.float32
BF = jnp.bfloat16
MESH = pl.DeviceIdType.MESH
HBM_SPEC = pl.BlockSpec(memory_space=pltpu.HBM)

D_MODEL = 1024
DEPTH = 4
N_MIXERS = 3
D_FF = 2816
NORM_EPS = 1e-6
NEG_INF = -1e30
LANE = 128
HEADS = 16
HEAD_DIM = 64
MLA_Q_RANK = 384
MLA_KV_RANK = 256
MLA_ROPE = 32
MLA_A_PAD = 768
ROPE_THETA = 10000.0
DIL_PATTERNS = ((128, 1), (512, 4), (2048, 16))
Q_BLOCK = 128
DIL_PAIRS = {1: 2, 4: 4, 16: 4}
REL_BUCKETS = 32
REL_MAX_DIST = 2048
N_CHIPS = 4
N_DEV = 8

ADAM_LR = 0.001
ADAM_B1 = 0.9
ADAM_B2 = 0.999
ADAM_EPS = 1e-08
ADAM_WD = 0.01
ADAM_STEP = 10

VMEM_LIMIT = 56 * 1024 * 1024
MATMUL_VMEM_BUDGET = 36 * 1024 * 1024
ROW_TILE = 512
ATTN_TILE = 256
ATTN_Q_TILE = 512
MLA_GROUP = 4
FOX_GROUP = 2
FORWARD_GROUP = 4


def _params(sem=None):
    return pltpu.CompilerParams(dimension_semantics=sem, vmem_limit_bytes=VMEM_LIMIT)


def _divisor_tiles(dim):
    tiles = [t for t in range(LANE, dim + 1, LANE) if dim % t == 0]
    return tiles or [dim]


def _matmul_tiles(m, n, k, a_bytes, b_bytes, out_bytes, has_add, n_unit=None, k_unit=None):
    best = None
    for tm in _divisor_tiles(m):
        for tn in _divisor_tiles(n_unit or n):
            for tk in _divisor_tiles(k_unit or k):
                if max(tm, tn, tk) > 2048:
                    continue
                vmem = 2 * (tm * tk * a_bytes + tk * tn * b_bytes + tm * tn * out_bytes) + tm * tn * 4
                if has_add:
                    vmem += 2 * tm * tn * 4
                if vmem > MATMUL_VMEM_BUDGET:
                    continue
                steps = (m // tm) * (n // tn) * (k // tk)
                traffic = m * k * a_bytes * (n // tn) + k * n * b_bytes * (m // tm) + m * n * out_bytes
                cost = traffic / 3.0e12 + steps * 0.4e-6
                if best is None or cost < best[0]:
                    best = (cost, tm, tn, tk)
    return best[1:]


def _matmul(a, b, *, ta=False, tb=False, b_chunks=False, out_chunks=False, add=None, out_dtype=F32, name):
    k, m = a.shape if ta else a.shape[::-1]
    n_unit = k_unit = None
    if b_chunks:
        chunks, rows_w, c = b.shape
        if tb:
            kb, n, k_unit = chunks * c, rows_w, c
        else:
            kb, n, n_unit = rows_w, chunks * c, c
    else:
        kb, n = b.shape[::-1] if tb else b.shape
    if out_chunks:
        assert n % N_CHIPS == 0 and add is None
        n_unit = n // N_CHIPS
    assert k == kb, (a.shape, b.shape, ta, tb)
    tm, tn, tk = _matmul_tiles(m, n, k, a.dtype.itemsize, b.dtype.itemsize, jnp.dtype(out_dtype).itemsize,
                               add is not None, n_unit, k_unit)
    nk = k // tk
    dims = (((0 if ta else 1,), (1 if tb else 0,)), ((), ()))

    def body(*refs):
        if add is None:
            a_ref, b_ref, o_ref, acc_ref = refs
            add_ref = None
        else:
            a_ref, b_ref, add_ref, o_ref, acc_ref = refs
        kk = pl.program_id(2)

        @pl.when(kk == 0)
        def _():
            acc_ref[...] = jnp.zeros_like(acc_ref)

        acc_ref[...] += lax.dot_general(a_ref[...].astype(BF), b_ref[...].astype(BF), dims,
                                        preferred_element_type=F32)

        @pl.when(kk == nk - 1)
        def _():
            r = acc_ref[...]
            if add_ref is not None:
                r = r + add_ref[...].astype(F32)
            o_ref[...] = r.astype(out_dtype)

    a_spec = pl.BlockSpec((tk, tm), lambda i, j, q: (q, i)) if ta else pl.BlockSpec((tm, tk), lambda i, j, q: (i, q))
    if b_chunks and tb:
        per_k = k_unit // tk
        b_spec = pl.BlockSpec((None, tn, tk), lambda i, j, q: (q // per_k, j, q % per_k))
    elif b_chunks:
        per_n = n_unit // tn
        b_spec = pl.BlockSpec((None, tk, tn), lambda i, j, q: (j // per_n, q, j % per_n))
    elif tb:
        b_spec = pl.BlockSpec((tn, tk), lambda i, j, q: (j, q))
    else:
        b_spec = pl.BlockSpec((tk, tn), lambda i, j, q: (q, j))
    if out_chunks:
        per_o = n_unit // tn
        o_spec = pl.BlockSpec((None, tm, tn), lambda i, j, q: (j // per_o, i, j % per_o))
        out_shape = jax.ShapeDtypeStruct((N_CHIPS, m, n_unit), out_dtype)
    else:
        o_spec = pl.BlockSpec((tm, tn), lambda i, j, q: (i, j))
        out_shape = jax.ShapeDtypeStruct((m, n), out_dtype)
    in_specs = [a_spec, b_spec]
    args = [a, b]
    if add is not None:
        in_specs.append(o_spec)
        args.append(add)
    return pl.pallas_call(
        body, out_shape=out_shape, grid=(m // tm, n // tn, nk),
        in_specs=in_specs, out_specs=o_spec, scratch_shapes=[pltpu.VMEM((tm, tn), F32)], name=name,
        compiler_params=_params(("parallel", "parallel", "arbitrary")))(*args)


def _rowwise(body, name, rows, ins, outs, tr=ROW_TILE):
    def row_spec(cols):
        return pl.BlockSpec((tr, cols), lambda i: (i, 0))

    def full_spec(shape):
        zeros = (0,) * len(shape)
        return pl.BlockSpec(shape, lambda i: zeros)

    in_specs = [row_spec(a.shape[1]) if kind == "row" else full_spec(a.shape) for a, kind in ins]
    out_specs = [row_spec(shape[1]) if kind == "row" else full_spec(shape) for shape, _, kind in outs]
    out_shape = [jax.ShapeDtypeStruct(shape, dtype) for shape, dtype, _ in outs]
    return pl.pallas_call(body, out_shape=out_shape, grid=(rows // tr,), in_specs=in_specs, out_specs=out_specs,
                          name=name, compiler_params=_params(("arbitrary",)))(*[a for a, _ in ins])


def _rstd(x):
    return lax.rsqrt(jnp.mean(x * x, axis=-1, keepdims=True) + NORM_EPS)


def _rms_bwd_math(x, g, dy):
    r = _rstd(x)
    gd = dy * g
    dx = r * gd - x * (r * r * r) * jnp.mean(gd * x, axis=-1, keepdims=True)
    dg = jnp.sum(dy * x * r, axis=0, keepdims=True)
    return dx, dg


def _sigmoid(x):
    return 0.5 * jnp.tanh(0.5 * x) + 0.5


def _init_acc(*refs):
    @pl.when(pl.program_id(0) == 0)
    def _():
        for r in refs:
            r[...] = jnp.zeros_like(r)


def _prenorm(h, g):
    rows, cols = h.shape

    def body(h_ref, g_ref, o_ref):
        x = h_ref[...]
        o_ref[...] = (x * _rstd(x) * g_ref[...]).astype(BF)

    return _rowwise(body, "prenorm", rows, [(h, "row"), (g, "full")], [((rows, cols), BF, "row")])[0]


def _post_residual(h, y, g_post, g_pre):
    rows, cols = h.shape
    with_pre = g_pre is not None

    def body(*refs):
        if with_pre:
            h_ref, y_ref, gp_ref, gq_ref, hn_ref, hb_ref = refs
        else:
            h_ref, y_ref, gp_ref, hn_ref, hb_ref = refs
        yv = y_ref[...]
        hn = h_ref[...] + yv * _rstd(yv) * gp_ref[...]
        hn_ref[...] = hn
        hb_ref[...] = (hn * _rstd(hn) * gq_ref[...] if with_pre else hn).astype(BF)

    ins = [(h, "row"), (y, "row"), (g_post, "full")] + ([(g_pre, "full")] if with_pre else [])
    return _rowwise(body, "post_residual_pre" if with_pre else "post_residual", rows, ins,
                    [((rows, cols), F32, "row"), ((rows, cols), BF, "row")])


def _ple_forward(h2, pp, z, g_pre):
    rows, cols = h2.shape

    def body(h_ref, p_ref, z_ref, g_ref, h3_ref, hb_ref):
        h3 = h_ref[...] + p_ref[...] * _sigmoid(z_ref[...])
        h3_ref[...] = h3
        hb_ref[...] = (h3 * _rstd(h3) * g_ref[...]).astype(BF)

    return _rowwise(body, "ple_forward", rows, [(h2, "row"), (pp, "row"), (z, "row"), (g_pre, "full")],
                    [((rows, cols), F32, "row"), ((rows, cols), BF, "row")])


def _ple_loss(h2, pp, z, target):
    rows, cols = h2.shape

    def body(h_ref, p_ref, z_ref, t_ref, dh_ref, sq_ref):
        _init_acc(sq_ref)
        err = h_ref[...] + p_ref[...] * _sigmoid(z_ref[...]) - t_ref[...]
        dh_ref[...] = err * (1.0 / cols)
        sq_ref[...] += jnp.sum(err * err, axis=0, keepdims=True)

    return _rowwise(body, "ple_loss", rows, [(h2, "row"), (pp, "row"), (z, "row"), (target, "row")],
                    [((rows, cols), F32, "row"), ((1, cols), F32, "acc")])


def _ple_backward(dh3, pp, z):
    rows, cols = dh3.shape

    def body(d_ref, p_ref, z_ref, dpp_ref, dz_ref):
        d = d_ref[...]
        s = _sigmoid(z_ref[...])
        dpp_ref[...] = (d * s).astype(BF)
        dz_ref[...] = (d * p_ref[...] * s * (1.0 - s)).astype(BF)

    return _rowwise(body, "ple_backward", rows, [(dh3, "row"), (pp, "row"), (z, "row")],
                    [((rows, cols), BF, "row"), ((rows, cols), BF, "row")])


def _rms_backward(x, g, dy, add, out_dtype):
    rows, cols = x.shape
    with_add = add is not None

    def body(*refs):
        if with_add:
            x_ref, g_ref, dy_ref, add_ref, dx_ref, dg_ref = refs
        else:
            x_ref, g_ref, dy_ref, dx_ref, dg_ref = refs
        _init_acc(dg_ref)
        dx, dg = _rms_bwd_math(x_ref[...], g_ref[...], dy_ref[...].astype(F32))
        if with_add:
            dx = dx + add_ref[...]
        dx_ref[...] = dx.astype(out_dtype)
        dg_ref[...] += dg

    ins = [(x, "row"), (g, "full"), (dy, "row")] + ([(add, "row")] if with_add else [])
    return _rowwise(body, "rms_backward_add" if with_add else "rms_backward", rows, ins,
                    [((rows, cols), out_dtype, "row"), ((1, cols), F32, "acc")])


def _swiglu_forward(gu):
    rows = gu.shape[0]

    def body(gu_ref, o_ref):
        g = gu_ref[:, :D_FF].astype(F32)
        o_ref[...] = (g * _sigmoid(g) * gu_ref[:, D_FF:].astype(F32)).astype(BF)

    return _rowwise(body, "swiglu_forward", rows, [(gu, "row")], [((rows, D_FF), BF, "row")])[0]


def _swiglu_backward(gu, dact):
    rows = gu.shape[0]

    def body(gu_ref, d_ref, o_ref):
        g = gu_ref[:, :D_FF].astype(F32)
        u = gu_ref[:, D_FF:].astype(F32)
        d = d_ref[...].astype(F32)
        s = _sigmoid(g)
        gs = g * s
        o_ref[:, :D_FF] = (d * u * (s + gs * (1.0 - s))).astype(BF)
        o_ref[:, D_FF:] = (d * gs).astype(BF)

    return _rowwise(body, "swiglu_backward", rows, [(gu, "row"), (dact, "row")], [((rows, 2 * D_FF), BF, "row")])[0]


def _rope_tables(positions):
    half = MLA_ROPE // 2
    inv = ROPE_THETA ** (-jnp.arange(half, dtype=F32) / half)
    ang = positions.astype(F32)[:, None] * inv
    cos, sin = jnp.cos(ang), jnp.sin(ang)
    rows = positions.shape[0]
    c = jnp.ones((rows, LANE), F32).at[:, 64:80].set(cos).at[:, 80:96].set(cos)
    sa = jnp.zeros((rows, LANE), F32).at[:, 64:80].set(-sin)
    sb = jnp.zeros((rows, LANE), F32).at[:, 80:96].set(sin)
    return c, sa, sb


def _rope_apply(x, c, sa, sb):
    return x * c + pltpu.roll(x, LANE - 16, 1) * sa + pltpu.roll(x, 16, 1) * sb


def _rope_apply_t(dy, c, sa, sb):
    return dy * c + pltpu.roll(dy * sa, 16, 1) + pltpu.roll(dy * sb, LANE - 16, 1)


def _rope_heads(x, tables, transpose, name):
    rows, cols = x.shape

    def body(x_ref, c_ref, sa_ref, sb_ref, o_ref):
        fn = _rope_apply_t if transpose else _rope_apply
        c, sa, sb = c_ref[...], sa_ref[...], sb_ref[...]
        for head in range(cols // LANE):
            lanes = slice(head * LANE, (head + 1) * LANE)
            o_ref[:, lanes] = fn(x_ref[:, lanes].astype(F32), c, sa, sb).astype(BF)

    blk = pl.BlockSpec((ROW_TILE, cols), lambda i: (i, 0))
    tbl = pl.BlockSpec((ROW_TILE, LANE), lambda i: (i, 0))
    return pl.pallas_call(body, out_shape=jax.ShapeDtypeStruct((rows, cols), BF), grid=(rows // ROW_TILE,),
                          in_specs=[blk, tbl, tbl, tbl], out_specs=blk, name=name,
                          compiler_params=_params(("parallel",)))(x, *tables)


def _mla_mid_forward(a, q_norm, kv_norm, tables):
    rows = a.shape[0]
    qr, kvr = MLA_Q_RANK, MLA_KV_RANK

    def body(a_ref, qn_ref, kn_ref, c_ref, sa_ref, sb_ref, cq_ref, ckv_ref, kr_ref):
        aq = a_ref[:, 0:qr]
        akv = a_ref[:, qr:qr + kvr]
        cq_ref[...] = (aq * _rstd(aq) * qn_ref[...]).astype(BF)
        ckv_ref[...] = (akv * _rstd(akv) * kn_ref[...]).astype(BF)
        kr_ref[...] = _rope_apply(a_ref[:, qr + kvr:], c_ref[...], sa_ref[...], sb_ref[...]).astype(BF)

    ins = [(a, "row"), (q_norm, "full"), (kv_norm, "full")] + [(t, "row") for t in tables]
    return _rowwise(body, "mla_mid_forward", rows, ins,
                    [((rows, qr), BF, "row"), ((rows, kvr), BF, "row"), ((rows, LANE), BF, "row")])


def _mla_mid_backward(a, q_norm, kv_norm, tables, dcq, dckv, dkr):
    rows = a.shape[0]
    qr, kvr = MLA_Q_RANK, MLA_KV_RANK

    def body(a_ref, qn_ref, kn_ref, c_ref, sa_ref, sb_ref, dcq_ref, dckv_ref, dkr_ref, da_ref, dqn_ref, dkn_ref):
        _init_acc(dqn_ref, dkn_ref)
        dxq, dgq = _rms_bwd_math(a_ref[:, 0:qr], qn_ref[...], dcq_ref[...])
        dxk, dgk = _rms_bwd_math(a_ref[:, qr:qr + kvr], kn_ref[...], dckv_ref[...])
        da_ref[:, 0:qr] = dxq.astype(BF)
        da_ref[:, qr:qr + kvr] = dxk.astype(BF)
        da_ref[:, qr + kvr:] = _rope_apply_t(dkr_ref[...], c_ref[...], sa_ref[...], sb_ref[...]).astype(BF)
        dqn_ref[...] += dgq
        dkn_ref[...] += dgk

    ins = ([(a, "row"), (q_norm, "full"), (kv_norm, "full")] + [(t, "row") for t in tables]
           + [(dcq, "row"), (dckv, "row"), (dkr, "row")])
    return _rowwise(body, "mla_mid_backward", rows, ins,
                    [((rows, MLA_A_PAD), BF, "row"), ((1, qr), F32, "acc"), ((1, kvr), F32, "acc")])


def _attn_specs(rows, kv_off, g, many_row_vectors):
    head =pl.BlockSpec((rows, g * LANE), lambda h: (0, h))
    kv_head = pl.BlockSpec((rows, g * LANE), lambda h: (0, h + kv_off // g))
    shared = pl.BlockSpec((rows, LANE), lambda h: (0, 0))
    col_vec = pl.BlockSpec((g, rows, 1), lambda h: (h, 0, 0),
                           pipeline_mode=pl.Buffered(1 if many_row_vectors and g > 2 else 2))
    row_vec = pl.BlockSpec((g, 1, rows), lambda h: (h, 0, 0))
    return head, kv_head, shared, col_vec, row_vec


def _attn_forward(q, kv, kv_off, kr, cum_col, cum_row, scale, group_size, name):
    rows = q.shape[0]
    heads = HEADS
    t = ATTN_TILE
    tq = ATTN_Q_TILE
    per = tq // t
    has_kr = kr is not None
    has_f = cum_col is not None
    group = range(group_size)

    def body(*refs):
        it = iter(refs)
        q_ref, kv_ref = next(it), next(it)
        kr_ref = next(it) if has_kr else None
        cc_ref = next(it) if has_f else None
        cr_ref = next(it) if has_f else None
        o_ref, lse_ref = next(it), next(it)
        lo = lax.broadcasted_iota(jnp.int32, (1, LANE), 1) < HEAD_DIM
        row = lax.broadcasted_iota(jnp.int32, (tq, t), 0)
        col = lax.broadcasted_iota(jnp.int32, (tq, t), 1)
        lanes = [slice(g * LANE, (g + 1) * LANE) for g in group]

        def q_block(i, _):
            qs = pl.ds(pl.multiple_of(i * tq, tq), tq)
            qbs = [q_ref[qs, lanes[g]] for g in group]
            cqs = [cc_ref[g, qs, :] if has_f else None for g in group]

            def step(j, carry, diag):
                ks = pl.ds(pl.multiple_of(j * t, t), t)
                skip = diag * t if diag else 0
                other = kr_ref[ks, :] if has_kr else jnp.zeros((t, LANE), BF)
                kvbs = [kv_ref[ks, lanes[g]] for g in group]

                def logit(g):
                    return lax.dot_general(qbs[g][skip:], jnp.where(lo, kvbs[g], other), (((1,), (1,)), ((), ())),
                                           preferred_element_type=F32)

                logits = {g: logit(g) for g in (group if has_f else group[:1])}
                out = []
                for g in group:
                    m, l, acc = (a[skip:] for a in carry[g])
                    if not has_f and g + 1 < len(group):
                        logits[g + 1] = logit(g + 1)
                    s = logits[g] * scale
                    if has_f:
                        s = s + (cqs[g][skip:] - cr_ref[g, :, ks])
                    if diag is not None:
                        s = jnp.where(col[skip:] + skip <= row[skip:], s, NEG_INF)
                    mn = jnp.maximum(m, jnp.max(s, axis=1, keepdims=True))
                    alpha = jnp.exp(m - mn)
                    p = jnp.exp(s - mn)
                    l = alpha * l + jnp.sum(p, axis=1, keepdims=True)
                    acc = alpha * acc + jnp.dot(p.astype(BF), kvbs[g], preferred_element_type=F32)
                    new = (mn, l, acc)
                    if skip:
                        new = tuple(jnp.concatenate([old[:skip], a], axis=0) for old, a in zip(carry[g], new))
                    out.append(new)
                return tuple(out)

            init = tuple((jnp.full((tq, 1), NEG_INF, F32), jnp.zeros((tq, 1), F32), jnp.zeros((tq, LANE), F32))
                         for _ in group)
            carry = lax.fori_loop(0, i * per, lambda j, c: step(j, c, None), init)
            for d in range(per):
                carry = step(i * per + d, carry, d)
            for g, (m, l, acc) in enumerate(carry):
                o_ref[qs, lanes[g]] = jnp.where(lo, 0.0, acc * (1.0 / l)).astype(BF)
                lse_ref[g, qs, :] = m + jnp.log(l)
            return 0

        lax.fori_loop(0, rows // tq, q_block, 0)

    head, kv_head, shared, col_vec, row_vec = _attn_specs(rows, kv_off, group_size, has_f)
    in_specs, args = [head, kv_head], [q, kv]
    if has_kr:
        in_specs.append(shared)
        args.append(kr)
    if has_f:
        in_specs += [col_vec, row_vec]
        args += [cum_col, cum_row]
    return pl.pallas_call(
        body, out_shape=[jax.ShapeDtypeStruct((rows, heads * LANE), BF), jax.ShapeDtypeStruct((heads, rows, 1), F32)],
        grid=(heads // group_size,), in_specs=in_specs, out_specs=[head, col_vec], name=name,
        compiler_params=_params(("arbitrary",)))(*args)


def _attn_backward(q, kv, kv_off, kr, cum_col, cum_row, o, do, lse, scale, group_size, name):
    rows = q.shape[0]
    heads = HEADS
    t = ATTN_TILE
    nb = rows // t
    has_kr = kr is not None
    has_f = cum_col is not None
    group = range(group_size)

    def body(*refs):
        it = iter(refs)
        q_ref, kv_ref = next(it), next(it)
        kr_ref = next(it) if has_kr else None
        cc_ref = next(it) if has_f else None
        cr_ref = next(it) if has_f else None
        o_ref, do_ref, lse_ref = next(it), next(it), next(it)
        dq_ref, dkv_ref = next(it), next(it)
        dkr_ref = next(it) if has_kr else None
        dck_ref = next(it) if has_f else None
        dcq_ref = next(it) if has_f else None
        dq_acc = next(it)
        lo = lax.broadcasted_iota(jnp.int32, (1, LANE), 1) < HEAD_DIM
        causal = (lax.broadcasted_iota(jnp.int32, (t, t), 1) <= lax.broadcasted_iota(jnp.int32, (t, t), 0))
        lanes = [slice(g * LANE, (g + 1) * LANE) for g in group]

        dq_acc[...] = jnp.zeros_like(dq_acc)
        if has_kr:
            _init_acc(dkr_ref)
        if has_f:
            dcq_ref[...] = jnp.zeros_like(dcq_ref)

        def kv_block(j, _):
            ks = pl.ds(pl.multiple_of(j * t, t), t)
            other = kr_ref[ks, :] if has_kr else jnp.zeros((t, LANE), BF)
            kvbs = [kv_ref[ks, lanes[g]] for g in group]
            kks = [jnp.where(lo, kvbs[g], other) for g in group]
            cks = [cr_ref[g, :, ks] if has_f else None for g in group]

            def pair(i, carry, diag):
                qs = pl.ds(pl.multiple_of(i * t, t), t)
                nt = (((1,), (1,)), ((), ()))

                def first_stage(g):
                    qb = q_ref[qs, lanes[g]]
                    dob = do_ref[qs, lanes[g]]
                    return (qb, dob, lax.dot_general(qb, kks[g], nt, preferred_element_type=F32),
                            lax.dot_general(dob, kvbs[g], nt, preferred_element_type=F32))

                first = {g: first_stage(g) for g in (group[:1] if has_f else group)}
                out = []
                for g in group:
                    dkk, dvv, dcs = carry[g]
                    qb, dob, logit, dp = first[g]
                    if has_f and g + 1 < len(group):
                        first[g + 1] = first_stage(g + 1)
                    s = logit * scale
                    if has_f:
                        s = s + (cc_ref[g, qs, :] - cks[g])
                    if diag:
                        s = jnp.where(causal, s, NEG_INF)
                    p = jnp.exp(s - lse_ref[g, qs, :])
                    delta = jnp.sum(dob.astype(F32) * o_ref[qs, lanes[g]].astype(F32), axis=1, keepdims=True)
                    ds = p * (dp - delta)
                    dsb = ds.astype(BF)
                    dvv = dvv + lax.dot_general(p.astype(BF), dob, (((0,), (0,)), ((), ())), preferred_element_type=F32)
                    dkk = dkk + lax.dot_general(dsb, qb, (((0,), (0,)), ((), ())), preferred_element_type=F32)
                    dq_acc[qs, lanes[g]] += jnp.dot(dsb, kks[g], preferred_element_type=F32)
                    if has_f:
                        dcs = dcs + jnp.sum(ds, axis=0, keepdims=True)
                        dcq_ref[g, qs, :] += jnp.sum(ds, axis=1, keepdims=True)
                    out.append((dkk, dvv, dcs))
                return tuple(out)

            init = tuple((jnp.zeros((t, LANE), F32), jnp.zeros((t, LANE), F32), jnp.zeros((1, t), F32)) for _ in group)
            carry = pair(j, init, True)
            carry = lax.fori_loop(j + 1, nb, lambda i, c: pair(i, c, False), carry)
            for g, (dkk, dvv, dcs) in enumerate(carry):
                dkk = dkk * scale
                dkv_ref[ks, lanes[g]] = jnp.where(lo, dkk, dvv).astype(BF)
                if has_kr:
                    dkr_ref[ks, :] += jnp.where(lo, 0.0, dkk)
                if has_f:
                    dck_ref[g, :, ks] = -dcs
            return 0

        lax.fori_loop(0, nb, kv_block, 0)
        dq_ref[...] = (dq_acc[...] * scale).astype(BF)

    head, kv_head, shared, col_vec, row_vec = _attn_specs(rows, kv_off, group_size, has_f)
    in_specs, args = [head, kv_head], [q, kv]
    if has_kr:
        in_specs.append(shared)
        args.append(kr)
    if has_f:
        in_specs += [col_vec, row_vec]
        args += [cum_col, cum_row]
    in_specs += [head, head, col_vec]
    args += [o, do, lse]
    out_shape = [jax.ShapeDtypeStruct((rows, heads * LANE), BF), jax.ShapeDtypeStruct((rows, heads * LANE), BF)]
    out_specs = [head, head]
    if has_kr:
        out_shape.append(jax.ShapeDtypeStruct((rows, LANE), F32))
        out_specs.append(shared)
    if has_f:
        out_shape += [jax.ShapeDtypeStruct((heads, 1, rows), F32), jax.ShapeDtypeStruct((heads, rows, 1), F32)]
        out_specs += [row_vec, col_vec]
    return pl.pallas_call(
        body, out_shape=out_shape, grid=(heads // group_size,), in_specs=in_specs, out_specs=out_specs,
        scratch_shapes=[pltpu.VMEM((rows, group_size * LANE), F32)], name=name,
        compiler_params=_params(("arbitrary",)))(*args)


def _tri_dot(tri, x):
    return jnp.dot(tri, x, preferred_element_type=F32, precision=lax.Precision.HIGHEST)


def _forget_forward(f_raw, b_f):
    rows = f_raw.shape[0]
    t = ATTN_TILE

    def body(f_ref, b_ref, cum_ref):
        tri = (lax.broadcasted_iota(jnp.int32, (t, t), 1) <= lax.broadcasted_iota(jnp.int32, (t, t), 0)).astype(F32)

        def blk(i, carry):
            sl = pl.ds(pl.multiple_of(i * t, t), t)
            xv = f_ref[sl, :] + b_ref[...]
            log_f = jnp.minimum(xv, 0.0) - jnp.log(1.0 + jnp.exp(-jnp.abs(xv)))
            cum_ref[sl, :] = _tri_dot(tri, log_f) + carry
            return carry + jnp.sum(log_f, axis=0, keepdims=True)

        lax.fori_loop(0, rows // t, blk, jnp.zeros((1, LANE), F32))

    return pl.pallas_call(body, out_shape=jax.ShapeDtypeStruct((rows, LANE), F32), name="forget_forward",
                          compiler_params=_params())(f_raw, b_f)


def _forget_backward(f_raw, b_f, dcum):
    rows = f_raw.shape[0]
    t = ATTN_TILE
    nb = rows // t

    def body(f_ref, b_ref, dc_ref, df_ref, db_ref):
        tri = (lax.broadcasted_iota(jnp.int32, (t, t), 1) >= lax.broadcasted_iota(jnp.int32, (t, t), 0)).astype(F32)

        def blk(i, carry):
            later, db = carry
            sl = pl.ds(pl.multiple_of((nb - 1 - i) * t, t), t)
            dc = dc_ref[sl, :]
            dlog = _tri_dot(tri, dc) + later
            xv = f_ref[sl, :] + b_ref[...]
            df = dlog / (1.0 + jnp.exp(xv))
            df_ref[sl, :] = df.astype(BF)
            return later + jnp.sum(dc, axis=0, keepdims=True), db + jnp.sum(df, axis=0, keepdims=True)

        _, db = lax.fori_loop(0, nb, blk, (jnp.zeros((1, LANE), F32), jnp.zeros((1, LANE), F32)))
        db_ref[...] = db

    return pl.pallas_call(body, out_shape=[jax.ShapeDtypeStruct((rows, LANE), BF), jax.ShapeDtypeStruct((1, LANE), F32)],
                          name="forget_backward", compiler_params=_params())(f_raw, b_f, dcum)


def _t5_bucket(dist):
    max_exact = REL_BUCKETS // 2
    n = jnp.maximum(dist.astype(F32), 1.0)
    large = max_exact + (jnp.log(n / max_exact) / math.log(REL_MAX_DIST / max_exact)
                         * (REL_BUCKETS - max_exact)).astype(jnp.int32)
    large = jnp.minimum(large, REL_BUCKETS - 1)
    return jnp.where(dist < max_exact, dist, large)


def _dil_buckets(dilation):
    i = jnp.arange(Q_BLOCK)[:, None]
    j = jnp.arange(Q_BLOCK)[None, :]
    cur = _t5_bucket(jnp.clip(i - j, 0) * dilation).astype(jnp.int32)
    prev = _t5_bucket(jnp.clip(Q_BLOCK + i - j, 0) * dilation).astype(jnp.int32)
    return cur, prev


def _dil_bias_tiles(tbl_ref, bc_ref, bp_ref, bias_ref, group, hp, pairs):
    ii = lax.broadcasted_iota(jnp.int32, (Q_BLOCK, Q_BLOCK), 0)
    jj = lax.broadcasted_iota(jnp.int32, (Q_BLOCK, Q_BLOCK), 1)
    for hh in range(2 * pairs):
        col = group * HEADS + 2 * pairs * hp + hh
        acc_c = jnp.zeros((Q_BLOCK, Q_BLOCK), F32)
        acc_p = jnp.zeros((Q_BLOCK, Q_BLOCK), F32)
        for b in range(REL_BUCKETS):
            val = tbl_ref[b, col]
            acc_c = jnp.where(bc_ref[...] == b, val, acc_c)
            acc_p = jnp.where(bp_ref[...] == b, val, acc_p)
        bias_ref[2 * hh] = jnp.where(jj <= ii, acc_c, NEG_INF)
        bias_ref[2 * hh + 1] = jnp.where(jj >= ii, acc_p, NEG_INF)


def _dil_view(qkv, group, dilation):
    if dilation == 1:
        return qkv
    width = 3 * HEADS * HEAD_DIM
    return qkv[:, group * width:(group + 1) * width].reshape(qkv.shape[0] // dilation, dilation * width)


def _dil_specs(group, dilation, length):
    width = DIL_PAIRS[dilation] * LANE
    per = 8 // DIL_PAIRS[dilation]

    def col(kind):
        if dilation == 1:
            return pl.BlockSpec((length, width), lambda hp, r: (0, (group * 3 + kind) * per + hp))
        return pl.BlockSpec((length, width), lambda hp, r: (0, (r * 3 + kind) * per + hp))

    out = pl.BlockSpec((length, width), lambda hp, r: (0, r * per + hp))
    tile = pl.BlockSpec((Q_BLOCK, Q_BLOCK), lambda hp, r: (0, 0))
    table = pl.BlockSpec(memory_space=pltpu.SMEM)
    return col, out, tile, table


def _dil_forward(view, group, dilation, table, buckets):
    length = view.shape[0]
    rows = length * dilation
    pairs = DIL_PAIRS[dilation]
    nb = length // Q_BLOCK
    scale = HEAD_DIM ** -0.5
    qb = Q_BLOCK

    def body(tbl_ref, bc_ref, bp_ref, q_ref, k_ref, v_ref, o_ref, lse_ref, bias_ref):
        hp = pl.program_id(0)

        @pl.when(pl.program_id(1) == 0)
        def _():
            _dil_bias_tiles(tbl_ref, bc_ref, bp_ref, bias_ref, group, hp, pairs)

        lo = lax.broadcasted_iota(jnp.int32, (1, LANE), 1) < HEAD_DIM
        nt = (((1,), (1,)), ((), ()))

        def blk(n, first):
            cur = pl.ds(0, qb) if first else pl.ds(pl.multiple_of(n * qb, qb), qb)
            prev = None if first else pl.ds(pl.multiple_of((n - 1) * qb, qb), qb)
            logits = []
            for pair in range(pairs):
                lanes = slice(pair * LANE, (pair + 1) * LANE)
                qn = q_ref[cur, lanes] * scale
                for hh in range(2):
                    qm = jnp.where(lo if hh == 0 else ~lo, qn, jnp.zeros_like(qn))
                    s_c = lax.dot_general(qm, k_ref[cur, lanes], nt, preferred_element_type=F32)
                    s_p = None if first else lax.dot_general(qm, k_ref[prev, lanes], nt, preferred_element_type=F32)
                    logits.append((s_c, s_p))
            for pair in range(pairs):
                lanes = slice(pair * LANE, (pair + 1) * LANE)
                outs, lses = [], []
                for hh in range(2):
                    bias = 4 * pair + 2 * hh
                    s_c, s_p = logits[2 * pair + hh]
                    s_c = s_c + bias_ref[bias]
                    m = jnp.max(s_c, axis=1, keepdims=True)
                    if not first:
                        s_p = s_p + bias_ref[bias + 1]
                        m = jnp.maximum(m, jnp.max(s_p, axis=1, keepdims=True))
                    e_c = jnp.exp(s_c - m)
                    l = jnp.sum(e_c, axis=1, keepdims=True)
                    acc = jnp.dot(e_c.astype(BF), v_ref[cur, lanes], preferred_element_type=F32)
                    if not first:
                        e_p = jnp.exp(s_p - m)
                        l = l + jnp.sum(e_p, axis=1, keepdims=True)
                        acc = acc + jnp.dot(e_p.astype(BF), v_ref[prev, lanes], preferred_element_type=F32)
                    outs.append(acc * (1.0 / l))
                    lses.append(m + jnp.log(l))
                o_ref[cur, lanes] = jnp.where(lo, outs[0], outs[1])
                lse_ref[cur, lanes] = jnp.where(lo, lses[0], lses[1])
            return 0

        blk(0, True)
        if nb > 1:
            lax.fori_loop(1, nb, lambda n, _: blk(n, False), 0)

    col, out, tile, tbl = _dil_specs(group, dilation, length)
    bc, bp = buckets
    o, lse = pl.pallas_call(
        body, out_shape=[jax.ShapeDtypeStruct((length, dilation * D_MODEL), F32)] * 2,
        grid=(8 // pairs, dilation), in_specs=[tbl, tile, tile, col(0), col(1), col(2)], out_specs=[out, out],
        scratch_shapes=[pltpu.VMEM((4 * pairs, qb, qb), F32)], name=f"dilated_forward_{dilation}",
        compiler_params=_params(("arbitrary", "arbitrary")))(
            table, bc, bp, view, view, view)
    return o.reshape(rows, D_MODEL), lse.reshape(rows, D_MODEL)


def _dil_backward(view, group, dilation, table, buckets, do_g, lse, dlt):
    length = view.shape[0]
    rows = length * dilation
    pairs = DIL_PAIRS[dilation]
    nb = length // Q_BLOCK
    scale = HEAD_DIM ** -0.5
    qb = Q_BLOCK

    def body(tbl_ref, bc_ref, bp_ref, q_ref, k_ref, v_ref, do_ref, lse_ref, dlt_ref,
             dq_ref, dk_ref, dv_ref, db_ref, bias_ref, dk_acc, dv_acc):
        hp = pl.program_id(0)

        @pl.when(pl.program_id(1) == 0)
        def _():
            _dil_bias_tiles(tbl_ref, bc_ref, bp_ref, bias_ref, group, hp, pairs)
            db_ref[...] = jnp.zeros_like(db_ref)

        dk_acc[...] = jnp.zeros_like(dk_acc)
        dv_acc[...] = jnp.zeros_like(dv_acc)
        lo = lax.broadcasted_iota(jnp.int32, (1, LANE), 1) < HEAD_DIM
        tn = (((0,), (0,)), ((), ()))
        nt = (((1,), (1,)), ((), ()))

        def blk(n, first):
            cur = pl.ds(0, qb) if first else pl.ds(pl.multiple_of(n * qb, qb), qb)
            prev = None if first else pl.ds(pl.multiple_of((n - 1) * qb, qb), qb)
            inputs = []
            for pair in range(pairs):
                lanes = slice(pair * LANE, (pair + 1) * LANE)
                qn = q_ref[cur, lanes] * scale
                don = do_ref[cur, lanes]
                for hh in range(2):
                    mask = lo if hh == 0 else ~lo
                    qm = jnp.where(mask, qn, jnp.zeros_like(qn))
                    dom = jnp.where(mask, don, jnp.zeros_like(don))
                    stage = [qm, dom, lax.dot_general(qm, k_ref[cur, lanes], nt, preferred_element_type=F32),
                             lax.dot_general(dom, v_ref[cur, lanes], nt, preferred_element_type=F32)]
                    if not first:
                        stage += [lax.dot_general(qm, k_ref[prev, lanes], nt, preferred_element_type=F32),
                                  lax.dot_general(dom, v_ref[prev, lanes], nt, preferred_element_type=F32)]
                    inputs.append(stage)
            for pair in range(pairs):
                lanes = slice(pair * LANE, (pair + 1) * LANE)
                kc = k_ref[cur, lanes]
                if not first:
                    kp = k_ref[prev, lanes]
                lse_n = lse_ref[cur, lanes]
                dlt_n = dlt_ref[cur, lanes]
                dqs = []
                dkc = jnp.zeros((qb, LANE), F32)
                dkp = jnp.zeros((qb, LANE), F32)
                dvc = jnp.zeros((qb, LANE), F32)
                dvp = jnp.zeros((qb, LANE), F32)
                for hh in range(2):
                    bias = 4 * pair + 2 * hh
                    mask = lo if hh == 0 else ~lo
                    qm, dom, s_c, dp_c = inputs[2 * pair + hh][:4]
                    lse_h = jnp.max(jnp.where(mask, lse_n, -3e38), axis=1, keepdims=True)
                    dlt_h = jnp.max(jnp.where(mask, dlt_n, -3e38), axis=1, keepdims=True)
                    p_c = jnp.exp(s_c + bias_ref[bias] - lse_h)
                    ds_c = p_c * (dp_c - dlt_h)
                    db_ref[pair, 2 * hh] += ds_c
                    dsc_b = ds_c.astype(BF)
                    dq = jnp.dot(dsc_b, kc, preferred_element_type=F32)
                    dkc = dkc + lax.dot_general(dsc_b, qm, tn, preferred_element_type=F32)
                    dvc = dvc + lax.dot_general(p_c.astype(BF), dom, tn, preferred_element_type=F32)
                    if not first:
                        s_p, dp_p = inputs[2 * pair + hh][4:]
                        p_p = jnp.exp(s_p + bias_ref[bias + 1] - lse_h)
                        ds_p = p_p * (dp_p - dlt_h)
                        db_ref[pair, 2 * hh + 1] += ds_p
                        dsp_b = ds_p.astype(BF)
                        dq = dq + jnp.dot(dsp_b, kp, preferred_element_type=F32)
                        dkp = dkp + lax.dot_general(dsp_b, qm, tn, preferred_element_type=F32)
                        dvp = dvp + lax.dot_general(p_p.astype(BF), dom, tn, preferred_element_type=F32)
                    dqs.append(dq)
                dq_ref[cur, lanes] = (jnp.where(lo, dqs[0], dqs[1]) * scale).astype(BF)
                dk_acc[cur, lanes] += dkc
                dv_acc[cur, lanes] += dvc
                if not first:
                    dk_acc[prev, lanes] += dkp
                    dv_acc[prev, lanes] += dvp
            return 0

        blk(0, True)
        if nb > 1:
            lax.fori_loop(1, nb, lambda n, _: blk(n, False), 0)
        dk_ref[...] = dk_acc[...].astype(BF)
        dv_ref[...] = dv_acc[...].astype(BF)

    col, out, tile, tbl = _dil_specs(group, dilation, length)
    bc, bp = buckets
    wide = (length, dilation * D_MODEL)
    dq, dk, dv, db = pl.pallas_call(
        body, out_shape=[jax.ShapeDtypeStruct(wide, BF)] * 3 + [jax.ShapeDtypeStruct((8, 4, qb, qb), F32)],
        grid=(8 // pairs, dilation), in_specs=[tbl, tile, tile, col(0), col(1), col(2), out, out, out],
        out_specs=[out, out, out, pl.BlockSpec((pairs, 4, qb, qb), lambda hp, r: (hp, 0, 0, 0))],
        scratch_shapes=[pltpu.VMEM((4 * pairs, qb, qb), F32), pltpu.VMEM((length, pairs * LANE), F32),
                        pltpu.VMEM((length, pairs * LANE), F32)],
        name=f"dilated_backward_{dilation}", compiler_params=_params(("arbitrary", "arbitrary")))(
            table, bc, bp, view, view, view,
            do_g.reshape(wide), lse.reshape(wide), dlt.reshape(wide))
    return dq.reshape(rows, D_MODEL), dk.reshape(rows, D_MODEL), dv.reshape(rows, D_MODEL), db


def _head_sums(x, lo):
    s0 = jnp.sum(jnp.where(lo, x, 0.0), axis=1, keepdims=True)
    s1 = jnp.sum(jnp.where(lo, 0.0, x), axis=1, keepdims=True)
    return jnp.where(lo, s0, s1)


def _dil_merge_forward(outs, lses):
    rows = outs[0].shape[0]

    def body(o0, o1, o2, l0, l1, l2, o_ref):
        ls = [l0[...], l1[...], l2[...]]
        m = jnp.maximum(jnp.maximum(ls[0], ls[1]), ls[2])
        es = [jnp.exp(v - m) for v in ls]
        tot = es[0] + es[1] + es[2]
        o_ref[...] = ((es[0] * o0[...] + es[1] * o1[...] + es[2] * o2[...]) / tot).astype(BF)

    blk = pl.BlockSpec((ROW_TILE, LANE), lambda i, j: (i, j))
    return pl.pallas_call(body, out_shape=jax.ShapeDtypeStruct((rows, D_MODEL), BF), grid=(rows // ROW_TILE, 8),
                          in_specs=[blk] * 6, out_specs=blk, name="dilated_merge_forward",
                          compiler_params=_params(("parallel", "parallel")))(*outs, *lses)


def _dil_merge_backward(outs, lses, do):
    rows = outs[0].shape[0]

    def body(o0, o1, o2, l0, l1, l2, do_ref, d0, d1, d2, t0, t1, t2):
        lo = lax.broadcasted_iota(jnp.int32, (1, LANE), 1) < HEAD_DIM
        ls = [l0[...], l1[...], l2[...]]
        os_ = [o0[...], o1[...], o2[...]]
        m = jnp.maximum(jnp.maximum(ls[0], ls[1]), ls[2])
        es = [jnp.exp(v - m) for v in ls]
        inv = 1.0 / (es[0] + es[1] + es[2])
        alphas = [e * inv for e in es]
        dov = do_ref[...]
        merged = alphas[0] * os_[0] + alphas[1] * os_[1] + alphas[2] * os_[2]
        dot = _head_sums(dov * merged, lo)
        for a, d_ref, t_ref in zip(alphas, (d0, d1, d2), (t0, t1, t2)):
            d_ref[...] = (a * dov).astype(BF)
            t_ref[...] = a * dot

    blk = pl.BlockSpec((ROW_TILE, LANE), lambda i, j: (i, j))
    res = pl.pallas_call(
        body, out_shape=[jax.ShapeDtypeStruct((rows, D_MODEL), BF)] * 3 + [jax.ShapeDtypeStruct((rows, D_MODEL), F32)] * 3,
        grid=(rows // ROW_TILE, 8), in_specs=[blk] * 7, out_specs=[blk] * 6, name="dilated_merge_backward",
        compiler_params=_params(("parallel", "parallel")))(*outs, *lses, do)
    return res[:3], res[3:]


def _rel_bias_grad(dbs, buckets):
    def body(db_ref, bc_ref, bp_ref, o_ref):
        g = pl.program_id(0)
        hp = pl.program_id(1)

        @pl.when((g == 0) & (hp == 0))
        def _():
            o_ref[...] = jnp.zeros_like(o_ref)

        rr = lax.broadcasted_iota(jnp.int32, (REL_BUCKETS, LANE), 0)
        cc = lax.broadcasted_iota(jnp.int32, (REL_BUCKETS, LANE), 1)
        bc = bc_ref[0]
        bp = bp_ref[0]
        acc = jnp.zeros((REL_BUCKETS, LANE), F32)
        for hh in range(2):
            col = g * HEADS + 2 * hp + hh
            d_c = db_ref[0, 0, 2 * hh]
            d_p = db_ref[0, 0, 2 * hh + 1]
            for b in range(REL_BUCKETS):
                val = (jnp.sum(jnp.where(bc == b, d_c, 0.0), keepdims=True)
                       + jnp.sum(jnp.where(bp == b, d_p, 0.0), keepdims=True))
                acc = jnp.where((rr == b) & (cc == col), val, acc)
        o_ref[...] += acc

    db_all = jnp.stack(dbs)
    bc_all = jnp.stack([b[0] for b in buckets])
    bp_all = jnp.stack([b[1] for b in buckets])
    tile = pl.BlockSpec((1, Q_BLOCK, Q_BLOCK), lambda g, hp: (g, 0, 0))
    return pl.pallas_call(
        body, out_shape=jax.ShapeDtypeStruct((REL_BUCKETS, LANE), F32), grid=(3, 8),
        in_specs=[pl.BlockSpec((1, 1, 4, Q_BLOCK, Q_BLOCK), lambda g, hp: (g, hp, 0, 0, 0)), tile, tile],
        out_specs=pl.BlockSpec((REL_BUCKETS, LANE), lambda g, hp: (0, 0)), name="rel_bias_grad",
        compiler_params=_params(("arbitrary", "arbitrary")))(db_all, bc_all, bp_all)


def _mla_forward(hn, w, tables):
    a = _matmul(hn, w["w_a"], name="mla_a")
    cq, ckv, kr = _mla_mid_forward(a, w["q_norm"], w["kv_norm"], tables)
    q_raw = _matmul(cq, w["w_uq"], name="mla_uq")
    q = _rope_heads(q_raw, tables, False, "rope_forward")
    kv = _matmul(ckv, w["w_ukv"], b_chunks=True, out_dtype=BF, name="mla_ukv")
    scale = (HEAD_DIM + MLA_ROPE) ** -0.5
    o, lse = _attn_forward(q, kv, 0, kr, None, None, scale, FORWARD_GROUP, "mla_attention_forward")
    y = _matmul(o, w["w_o"], name="attn_out")
    return y, dict(hn=hn, a=a, cq=cq, ckv=ckv, kr=kr, q=q, kv=kv, o=o, lse=lse)


def _mla_backward(dy, w, s, tables):
    scale = (HEAD_DIM + MLA_ROPE) ** -0.5
    g = {}
    g["w_o"] = _matmul(s["o"], dy, ta=True, out_dtype=BF, name="attn_out_dw")
    do = _matmul(dy, w["w_o"], tb=True, out_dtype=BF, name="attn_out_dx")
    dq, dkv, dkr = _attn_backward(s["q"], s["kv"], 0, s["kr"], None, None, s["o"], do, s["lse"], scale,
                                  MLA_GROUP, "mla_attention_backward")
    dq_raw = _rope_heads(dq, tables, True, "rope_backward")
    g["w_uq"] = _matmul(s["cq"], dq_raw, ta=True, out_dtype=BF, name="mla_uq_dw")
    dcq = _matmul(dq_raw, w["w_uq"], tb=True, name="mla_uq_dx")
    g["w_ukv"] = _matmul(s["ckv"], dkv, ta=True, out_chunks=True, out_dtype=BF, name="mla_ukv_dw")
    dckv = _matmul(dkv, w["w_ukv"], tb=True, b_chunks=True, name="mla_ukv_dx")
    da, g["q_norm"], g["kv_norm"] = _mla_mid_backward(s["a"], w["q_norm"], w["kv_norm"], tables, dcq, dckv, dkr)
    g["w_a"] = _matmul(s["hn"], da, ta=True, out_dtype=BF, name="mla_a_dw")
    dhn = _matmul(da, w["w_a"], tb=True, name="mla_a_dx")
    return dhn, g


def _fox_forward(hn, w):
    qkv = _matmul(hn, w["w_qkv"], out_dtype=BF, name="fox_qkv")
    f_raw = _matmul(hn, w["w_f"], name="fox_f")
    cum = _forget_forward(f_raw, w["b_f"])
    cum_heads = cum[:, :HEADS].T
    cum_col, cum_row = cum_heads[:, :, None], cum_heads[:, None, :]
    o, lse = _attn_forward(qkv, qkv, HEADS, None, cum_col, cum_row, HEAD_DIM ** -0.5, FORWARD_GROUP,
                           "fox_attention_forward")
    y = _matmul(o, w["w_o"], name="attn_out")
    return y, dict(hn=hn, qkv=qkv, f_raw=f_raw, cum_col=cum_col, cum_row=cum_row, o=o, lse=lse)


def _fox_backward(dy, w, s):
    g = {}
    g["w_o"] = _matmul(s["o"], dy, ta=True, out_dtype=BF, name="attn_out_dw")
    do = _matmul(dy, w["w_o"], tb=True, out_dtype=BF, name="attn_out_dx")
    dq, dkv, dck, dcq = _attn_backward(s["qkv"], s["qkv"], HEADS, None, s["cum_col"], s["cum_row"], s["o"], do,
                                       s["lse"], HEAD_DIM ** -0.5, FOX_GROUP, "fox_attention_backward")
    dcum = jnp.pad((dck[:, 0, :] + dcq[:, :, 0]).T, ((0, 0), (0, LANE - HEADS)))
    df, g["b_f"] = _forget_backward(s["f_raw"], w["b_f"], dcum)
    dqkv = jnp.concatenate([dq, dkv], axis=1)
    g["w_qkv"] = _matmul(s["hn"], dqkv, ta=True, out_dtype=BF, name="fox_qkv_dw")
    g["w_f"] = _matmul(s["hn"], df, ta=True, out_dtype=BF, name="fox_f_dw")
    dhn = _matmul(dqkv, w["w_qkv"], tb=True, name="fox_qkv_dx")
    dhn = _matmul(df, w["w_f"], tb=True, add=dhn, name="fox_f_dx")
    return dhn, g


def _dil_mixer_forward(hn, w, buckets):
    qkv = _matmul(hn, w["w_qkv"], b_chunks=True, out_dtype=BF, name="dil_qkv")
    views = [_dil_view(qkv, grp, dilation) for grp, (_, dilation) in enumerate(DIL_PATTERNS)]
    outs, lses = [], []
    for grp, (_, dilation) in enumerate(DIL_PATTERNS):
        o_g, lse_g = _dil_forward(views[grp], grp, dilation, w["rel_bias"], buckets[grp])
        outs.append(o_g)
        lses.append(lse_g)
    o = _dil_merge_forward(outs, lses)
    y = _matmul(o, w["w_o"], name="dil_out")
    return y, dict(hn=hn, views=views, outs=outs, lses=lses, o=o)


def _dil_mixer_backward(dy, w, s, buckets):
    g = {}
    g["w_o"] = _matmul(s["o"], dy, ta=True, out_dtype=BF, name="dil_out_dw")
    do = _matmul(dy, w["w_o"], tb=True, name="dil_out_dx")
    do_gs, dlts = _dil_merge_backward(s["outs"], s["lses"], do)
    parts, dbs = [], []
    for grp, (_, dilation) in enumerate(DIL_PATTERNS):
        dq, dk, dv, db = _dil_backward(s["views"][grp], grp, dilation, w["rel_bias"], buckets[grp], do_gs[grp],
                                       s["lses"][grp], dlts[grp])
        parts += [dq, dk, dv]
        dbs.append(db)
    dqkv = jnp.concatenate(parts, axis=1)
    g["rel_bias"] = _rel_bias_grad(dbs, buckets)
    g["w_qkv"] = _matmul(s["hn"], dqkv, ta=True, out_chunks=True, out_dtype=BF, name="dil_qkv_dw")
    dhn = _matmul(dqkv, w["w_qkv"], tb=True, b_chunks=True, name="dil_qkv_dx")
    return dhn, g


def _mixer_weights(i, lw, small):
    mixer, j = i % N_MIXERS, i // N_MIXERS
    if mixer == 0:
        return dict(lw["mixer"], q_norm=small["mla_q_norm"][j][None, :], kv_norm=small["mla_kv_norm"][j][None, :])
    if mixer == 1:
        return dict(lw["mixer"], rel_bias=small["rel_bias"])
    return dict(lw["mixer"], b_f=jnp.pad(small["fox_b_f"][j][None, :], ((0, 0), (0, LANE - HEADS))))


MIXER_PART, COMMON_PART = 0, 1


def _run_layers(x, p, positions, target, get_part, get_small, put_part):
    tables = _rope_tables(positions)
    buckets = [_dil_buckets(d) for _, d in DIL_PATTERNS]
    layers, saved = [], []
    h = x
    first = get_part(0, MIXER_PART, positions)
    small = get_small()

    def gain(i, k):
        return small["norm_g"][i, k][None, :]

    hn = _prenorm(h, gain(0, 0))
    sq = dh = None
    for i in range(DEPTH):
        mixer = i % N_MIXERS
        lw = dict(first if i == 0 else get_part(i, MIXER_PART, h))
        mw = _mixer_weights(i, lw, small)
        if mixer == 0:
            y, ms = _mla_forward(hn, mw, tables)
        elif mixer == 1:
            y, ms = _dil_mixer_forward(hn, mw, buckets)
        else:
            y, ms = _fox_forward(hn, mw)
        if "ffn_w_in" not in lw:
            lw.update(get_part(i, COMMON_PART, y))
        layers.append(lw)
        h1, hn2 = _post_residual(h, y, gain(i, 1), gain(i, 2))
        gu = _matmul(hn2, lw["ffn_w_in"], b_chunks=True, out_dtype=BF, name="ffn_in")
        act = _swiglu_forward(gu)
        f = _matmul(act, lw["ffn_w_out"], name="ffn_out")
        h2, h2b = _post_residual(h1, f, gain(i, 3), None)
        pp = _matmul(p[i], lw["ple_w_proj"], b_chunks=True, name="ple_proj")
        z = _matmul(h2b, lw["ple_w_gate"], name="ple_gate")
        saved.append(dict(h=h, y=y, ms=ms, h1=h1, hn2=hn2, gu=gu, act=act, f=f, h2b=h2b, pp=pp, z=z))
        if i + 1 < DEPTH:
            h, hn = _ple_forward(h2, pp, z, gain(i + 1, 0))
        else:
            dh, sq = _ple_loss(h2, pp, z, target)

    norm_rows = [[None] * 4 for _ in range(DEPTH)]
    sg = dict(mla_q_norm={}, mla_kv_norm={}, rel_bias=None, fox_b_f={})
    for i in reversed(range(DEPTH)):
        s, lw = saved[i], layers[i]
        mixer, j = i % N_MIXERS, i // N_MIXERS
        mw = _mixer_weights(i, lw, small)
        lg = {}
        dpp, dz = _ple_backward(dh, s["pp"], s["z"])
        lg["ple_w_proj"] = _matmul(p[i], dpp, ta=True, out_chunks=True, out_dtype=BF, name="ple_proj_dw")
        lg["ple_w_gate"] = _matmul(s["h2b"], dz, ta=True, out_dtype=BF, name="ple_gate_dw")
        dh2 = _matmul(dz, lw["ple_w_gate"], tb=True, add=dh, name="ple_gate_dx")
        df, norm_rows[i][3] = _rms_backward(s["f"], gain(i, 3), dh2, None, BF)
        lg["ffn_w_out"] = _matmul(s["act"], df, ta=True, out_dtype=BF, name="ffn_out_dw")
        dact = _matmul(df, lw["ffn_w_out"], tb=True, out_dtype=BF, name="ffn_out_dx")
        dgu = _swiglu_backward(s["gu"], dact)
        lg["ffn_w_in"] = _matmul(s["hn2"], dgu, ta=True, out_chunks=True, out_dtype=BF, name="ffn_in_dw")
        split = i in SPLIT_LAYERS
        zero = put_part(i, COMMON_PART, lg)[0:1, 0:1] if split else 0.0
        dhn2 = _matmul(dgu, lw["ffn_w_in"], tb=True, b_chunks=True, name="ffn_in_dx")
        dh1, norm_rows[i][2] = _rms_backward(s["h1"], gain(i, 2), dhn2, dh2, F32)
        dy, norm_rows[i][1] = _rms_backward(s["y"], gain(i, 1) + zero, dh1, None, BF)
        if mixer == 0:
            dhn, mg = _mla_backward(dy, mw, s["ms"], tables)
            sg["mla_q_norm"][j] = mg.pop("q_norm")
            sg["mla_kv_norm"][j] = mg.pop("kv_norm")
        elif mixer == 1:
            dhn, mg = _dil_mixer_backward(dy, mw, s["ms"], buckets)
            rel = mg.pop("rel_bias")[:, :3 * HEADS]
            sg["rel_bias"] = rel if sg["rel_bias"] is None else sg["rel_bias"] + rel
        else:
            dhn, mg = _fox_backward(dy, mw, s["ms"])
            sg["fox_b_f"][j] = mg.pop("b_f")[:, :HEADS]
        token = put_part(i, MIXER_PART, dict(mixer=mg) if split else dict(lg, mixer=mg))
        dh, norm_rows[i][0] = _rms_backward(s["h"], gain(i, 0) + token[0:1, 0:1], dhn, dh1, F32)
    small_grads = dict(norm_g=jnp.stack([jnp.concatenate(row, axis=0) for row in norm_rows]),
                       rel_bias=sg["rel_bias"])
    for k in ("mla_q_norm", "mla_kv_norm", "fox_b_f"):
        small_grads[k] = jnp.concatenate([sg[k][j] for j in sorted(sg[k])], axis=0)
    return sq, dh, small_grads


BIG = ("ffn_w_in", "ffn_w_out", "ple_w_proj", "ple_w_gate", "mla_w_a", "mla_w_uq", "mla_w_ukv", "mla_w_o",
       "dil_w_qkv", "dil_w_o", "fox_w_qkvf", "fox_w_o")
SMALL_SHARDED = ("norm_g", "mla_q_norm", "mla_kv_norm")
SMALL_REPLICATED = ("rel_bias", "fox_b_f")
WEIGHTS = ("norm_g", "ffn_w_in", "ffn_w_out", "ple_w_proj", "ple_w_gate", "rel_bias", "mla_w_a", "mla_q_norm",
           "mla_kv_norm", "mla_w_uq", "mla_w_ukv", "mla_w_o", "dil_w_qkv", "dil_w_o", "fox_w_qkvf", "fox_b_f", "fox_w_o")


TRANSPOSED = "fox_w_qkvf"
SPLIT_LAYERS = (0, 1, 2, 3)
LAYER_COMMON = ("ffn_w_in", "ffn_w_out", "ple_w_proj", "ple_w_gate")
MIXER_WEIGHTS = (("mla_w_a", "mla_w_uq", "mla_w_ukv", "mla_w_o"), ("dil_w_qkv", "dil_w_o"), ("fox_w_qkvf", "fox_w_o"))


def _part_names(i, part):
    if i in SPLIT_LAYERS:
        return MIXER_WEIGHTS[i % N_MIXERS] if part == MIXER_PART else LAYER_COMMON
    return MIXER_WEIGHTS[i % N_MIXERS] + LAYER_COMMON if part == MIXER_PART else ()


def _layer_slot(name, i):
    return i if name in LAYER_COMMON else i // N_MIXERS


def _merge_rows(chunks):
    n, r, c = chunks.shape
    return chunks.reshape(n * r, c)


def _merge_cols(chunks):
    n, r, c = chunks.shape
    return chunks.transpose(1, 0, 2).reshape(r, n * c)


def _pad_heads_out(wo):
    w3 = wo.reshape(HEADS, HEAD_DIM, D_MODEL)
    return jnp.pad(w3, ((0, 0), (HEAD_DIM, 0), (0, 0))).reshape(HEADS * LANE, D_MODEL)


def _part_to_compute(i, part, ch):
    lw = {}
    if "ffn_w_in" in ch:
        lw.update(ffn_w_in=ch["ffn_w_in"], ffn_w_out=_merge_rows(ch["ffn_w_out"]), ple_w_proj=ch["ple_w_proj"],
                  ple_w_gate=_merge_rows(ch["ple_w_gate"]))
    if part == COMMON_PART:
        return lw
    mixer = i % N_MIXERS
    if mixer == 0:
        wa = _merge_rows(ch["mla_w_a"])
        rank = MLA_Q_RANK + MLA_KV_RANK
        wa_p = jnp.concatenate([wa[:, :rank], jnp.zeros((wa.shape[0], 64), wa.dtype), wa[:, rank:],
                                jnp.zeros((wa.shape[0], 32), wa.dtype)], axis=1)
        wuq = _merge_cols(ch["mla_w_uq"]).reshape(MLA_Q_RANK, HEADS, HEAD_DIM + MLA_ROPE)
        wuq_p = jnp.pad(wuq, ((0, 0), (0, 0), (0, LANE - HEAD_DIM - MLA_ROPE))).reshape(MLA_Q_RANK, HEADS * LANE)
        lw["mixer"] = dict(w_a=wa_p, w_uq=wuq_p, w_ukv=ch["mla_w_ukv"], w_o=_pad_heads_out(_merge_rows(ch["mla_w_o"])))
    elif mixer == 1:
        lw["mixer"] = dict(w_qkv=ch["dil_w_qkv"], w_o=_merge_rows(ch["dil_w_o"]))
    else:
        wf = _merge_rows(ch["fox_w_qkvf"]).T
        inner = HEADS * HEAD_DIM
        q3 = wf[:, :inner].reshape(D_MODEL, HEADS, HEAD_DIM)
        k3 = wf[:, inner:2 * inner].reshape(D_MODEL, HEADS, HEAD_DIM)
        v3 = wf[:, 2 * inner:3 * inner].reshape(D_MODEL, HEADS, HEAD_DIM)
        q_p = jnp.pad(q3, ((0, 0), (0, 0), (0, HEAD_DIM))).reshape(D_MODEL, HEADS * LANE)
        kv_p = jnp.concatenate([k3, v3], axis=2).reshape(D_MODEL, HEADS * LANE)
        f_p = jnp.pad(wf[:, 3 * inner:], ((0, 0), (0, LANE - HEADS)))
        lw["mixer"] = dict(w_qkv=jnp.concatenate([q_p, kv_p], axis=1), w_f=f_p,
                           w_o=_pad_heads_out(_merge_rows(ch["fox_w_o"])))
    return lw


def _part_contributions(i, part, lg, chunk_shapes):
    spec = {k: jax.ShapeDtypeStruct(s, BF) for k, s in chunk_shapes.items()}
    (contrib,) = jax.linear_transpose(functools.partial(_part_to_compute, i, part), spec)(lg)
    return contrib


def _chip_peers():
    x, y, c = lax.axis_index("x"), lax.axis_index("y"), lax.axis_index("c")
    peers = [(1 - x, y), (x, 1 - y), (1 - x, 1 - y)]
    return x, y, c, peers


SEM_SPEC = pl.BlockSpec(memory_space=pltpu.SEMAPHORE)
ANY_SPEC = pl.BlockSpec(memory_space=pl.ANY)
SPLIT_EFFECT = pltpu.SideEffectType.DATAFLOW_SIDE_EFFECTING


def _own_slot(shard):
    me = 2 * lax.axis_index("x") + lax.axis_index("y")
    return lax.dynamic_update_index_in_dim(lax.empty((N_CHIPS,) + shard.shape, shard.dtype), shard[None], me, 0)


def _spread_copy(src, land, k, peer, c, send_sems, recv_sems, index, src_slot, slot):
    px, py = peer
    return pltpu.make_async_remote_copy(
        src_ref=src.at[src_slot], dst_ref=land.at[slot],
        send_sem=send_sems.at[3 * index + k], recv_sem=recv_sems.at[3 * index + k],
        device_id=(px, py, c), device_id_type=MESH)


def _spread_start(bufs, srcs, after, name):
    n = len(bufs)
    exchange = srcs is not None
    arrays = (list(srcs) if exchange else []) + list(bufs)
    na = len(arrays)

    def body(*refs):
        src, land = refs[:n], refs[na - n:na]
        send_sems, recv_sems = refs[na + 1], refs[na + 2]
        token = refs[-1]
        x, y, c, peers = _chip_peers()
        me = 2 * x + y
        for w in range(n):
            for k, peer in enumerate(peers):
                src_slot = 2 * peer[0] + peer[1] if exchange else me
                _spread_copy(src[w], land[w], k, peer, c, send_sems, recv_sems, w, src_slot, me).start()
        token[...] = jnp.zeros_like(token)

    hbm = [pltpu.with_memory_space_constraint(a, pltpu.HBM) for a in arrays]
    out = pl.pallas_call(
        body, name=name,
        out_shape=(pltpu.SemaphoreType.DMA((3 * n,)), pltpu.SemaphoreType.DMA((3 * n,)),
                   *[pltpu.HBM(a.shape, a.dtype) for a in hbm], jax.ShapeDtypeStruct((8, LANE), F32)),
        in_specs=[HBM_SPEC] * na + [ANY_SPEC],
        out_specs=(SEM_SPEC, SEM_SPEC, *[HBM_SPEC] * na, pl.BlockSpec(memory_space=pltpu.VMEM)),
        input_output_aliases={w: 2 + w for w in range(na)},
        compiler_params=pltpu.CompilerParams(has_side_effects=SPLIT_EFFECT))(*hbm, after)
    return dict(send=out[0], recv=out[1], arrays=out[2:2 + na], n=n, token=out[-1], exchange=exchange)


def _spread_wait(handle, after, name):
    n, exchange = handle["n"], handle["exchange"]
    arrays = list(handle["arrays"])
    na = len(arrays)

    def body(*refs):
        src, land = refs[:n], refs[na - n:na]
        send_sems, recv_sems = refs[na], refs[na + 1]
        x, y, c, peers = _chip_peers()
        me = 2 * x + y
        for w in range(n):
            for k, peer in enumerate(peers):
                there = 2 * peer[0] + peer[1]
                cp = _spread_copy(src[w], land[w], k, peer, c, send_sems, recv_sems, w, there if exchange else me, there)
                cp.wait_send()
                cp.wait_recv()

    out = pl.pallas_call(
        body, name=name, out_shape=tuple(pltpu.HBM(a.shape, a.dtype) for a in arrays),
        in_specs=[HBM_SPEC] * na + [SEM_SPEC, SEM_SPEC, ANY_SPEC], out_specs=tuple([HBM_SPEC] * na),
        input_output_aliases={w: w for w in range(na)},
        compiler_params=pltpu.CompilerParams(has_side_effects=SPLIT_EFFECT))(*arrays, handle["send"], handle["recv"], after)
    return (list(out[n:]), list(out[:n])) if exchange else list(out)


def _sibling_copy(received, sent, land, k, me, peers, sibling, send_sems, recv_sems, index):
    slot = me if k == 3 else 2 * peers[k][0] + peers[k][1]
    src = sent if k == 3 else received
    return pltpu.make_async_remote_copy(
        src_ref=src.at[slot], dst_ref=land.at[slot], send_sem=send_sems.at[4 * index + k],
        recv_sem=recv_sems.at[4 * index + k], device_id=sibling, device_id_type=MESH)


def _sibling_start(received, sent, after, name):
    n = len(received)
    lands = [lax.empty(a.shape, a.dtype) for a in received]
    arrays = list(received) + list(sent) + lands

    def body(*refs):
        rec, snt, land = refs[:n], refs[n:2 * n], refs[2 * n:3 * n]
        send_sems, recv_sems = refs[3 * n + 1], refs[3 * n + 2]
        token = refs[-1]
        x, y, c, peers = _chip_peers()
        for w in range(n):
            for k in range(4):
                _sibling_copy(rec[w], snt[w], land[w], k, 2 * x + y, peers, (x, y, 1 - c), send_sems, recv_sems, w).start()
        token[...] = jnp.zeros_like(token)

    hbm = [pltpu.with_memory_space_constraint(a, pltpu.HBM) for a in arrays]
    out = pl.pallas_call(
        body, name=name,
        out_shape=(pltpu.SemaphoreType.DMA((4 * n,)), pltpu.SemaphoreType.DMA((4 * n,)),
                   *[pltpu.HBM(a.shape, a.dtype) for a in hbm], jax.ShapeDtypeStruct((8, LANE), F32)),
        in_specs=[HBM_SPEC] * (3 * n) + [ANY_SPEC],
        out_specs=(SEM_SPEC, SEM_SPEC, *[HBM_SPEC] * (3 * n), pl.BlockSpec(memory_space=pltpu.VMEM)),
        input_output_aliases={w: 2 + w for w in range(3 * n)},
        compiler_params=pltpu.CompilerParams(has_side_effects=SPLIT_EFFECT))(*hbm, after)
    return dict(send=out[0], recv=out[1], arrays=out[2:2 + 3 * n], n=n, token=out[-1])


def _sibling_wait(handle, after, name):
    n = handle["n"]
    arrays = list(handle["arrays"])

    def body(*refs):
        rec, snt, land = refs[:n], refs[n:2 * n], refs[2 * n:3 * n]
        send_sems, recv_sems = refs[3 * n], refs[3 * n + 1]
        x, y, c, peers = _chip_peers()
        for w in range(n):
            for k in range(4):
                cp = _sibling_copy(rec[w], snt[w], land[w], k, 2 * x + y, peers, (x, y, 1 - c), send_sems, recv_sems, w)
                cp.wait_send()
                cp.wait_recv()

    out = pl.pallas_call(
        body, name=name, out_shape=tuple(pltpu.HBM(a.shape, a.dtype) for a in arrays),
        in_specs=[HBM_SPEC] * (3 * n) + [SEM_SPEC, SEM_SPEC, ANY_SPEC], out_specs=tuple([HBM_SPEC] * (3 * n)),
        input_output_aliases={w: w for w in range(3 * n)},
        compiler_params=pltpu.CompilerParams(has_side_effects=SPLIT_EFFECT))(*arrays, handle["send"], handle["recv"], after)
    return list(out[:n]), list(out[n:2 * n]), list(out[2 * n:])


def _all_reduce_small(v):
    rows = v.shape[0]

    def body(v_ref, sum_ref, slots, send_sems, recv_sems):
        x, y, c = lax.axis_index("x"), lax.axis_index("y"), lax.axis_index("c")
        me = 4 * x + 2 * y + c
        slots[me] = v_ref[...]
        sends = []
        for k in range(1, N_DEV):
            bx, by, bc = (k >> 2) & 1, (k >> 1) & 1, k & 1
            peer = (x ^ bx, y ^ by, c ^ bc)
            rc = pltpu.make_async_remote_copy(src_ref=v_ref, dst_ref=slots.at[me], send_sem=send_sems.at[k],
                                              recv_sem=recv_sems.at[k], device_id=peer, device_id_type=MESH)
            rc.start()
            sends.append(rc)
        for k in range(1, N_DEV):
            bx, by, bc = (k >> 2) & 1, (k >> 1) & 1, k & 1
            src = 4 * (x ^ bx) + 2 * (y ^ by) + (c ^ bc)
            pltpu.make_async_remote_copy(src_ref=v_ref, dst_ref=slots.at[src], send_sem=send_sems.at[k],
                                         recv_sem=recv_sems.at[k], device_id=(x ^ bx, y ^ by, c ^ bc),
                                         device_id_type=MESH).wait_recv()
        for rc in sends:
            rc.wait_send()
        total = slots[0]
        for k in range(1, N_DEV):
            total = total + slots[k]
        sum_ref[...] = total

    vm = pl.BlockSpec(memory_space=pltpu.VMEM)
    return pl.pallas_call(
        body, out_shape=jax.ShapeDtypeStruct((rows, LANE), F32), in_specs=[vm], out_specs=vm,
        scratch_shapes=[pltpu.VMEM((N_DEV, rows, LANE), F32), pltpu.SemaphoreType.DMA((N_DEV,)),
                        pltpu.SemaphoreType.DMA((N_DEV,))], name="all_reduce_small")(v)


def _as_2d(a):
    return a.reshape(-1, a.shape[-1])


def _row_tile(rows, cols):
    for t in (512, 256, 128, 64, 32, 16):
        if rows % t == 0 and t * cols * 4 <= (1 << 20):
            return t
    return rows


def _adamw_weight(w, m, v, received, sent, sibling):
    layers = len(received)
    _, rows, cols = received[0].shape
    tr = _row_tile(rows, cols)
    by_columns = rows % tr != 0 or tr == rows and rows * cols * 4 > (2 << 20)
    if by_columns:
        assert layers == 1 and cols % (2 * LANE) == 0, (w.shape, received[0].shape)
        tr, tc, steps = rows, cols // 2, 2
        index = lambda i: (0, i)
    else:
        tc, steps = cols, rows // tr
        index = lambda i: (i, 0)
    where = (2 * lax.axis_index("x") + lax.axis_index("y")).astype(jnp.int32).reshape(1)

    def body(where_ref, w_ref, m_ref, v_ref, *rest):
        per_layer, (g_ref, d_ref, nm_ref, nv_ref) = rest[:3 * layers], rest[3 * layers:]
        me = where_ref[0]
        for layer in range(layers):
            r_ref, own_ref, s_ref = per_layer[3 * layer:3 * layer + 3]

            @pl.when(pl.program_id(0) == layer)
            def _():
                mine = theirs = None
                for k in range(N_CHIPS):
                    a = jnp.where(me == k, own_ref[...], r_ref[k]).astype(F32)
                    b = s_ref[k].astype(F32)
                    mine = a if mine is None else mine + a
                    theirs = b if theirs is None else theirs + b
                g = mine + theirs
                delta, nm, nv = _adamw_math(w_ref[...], g, m_ref[...], v_ref[...])
                g_ref[...] = g
                d_ref[...] = delta
                nm_ref[...] = nm
                nv_ref[...] = nv

    def held(layer, now, i):
        return jnp.where(now < layer, 0, jnp.where(now > layer, steps - 1, i))

    if by_columns:
        stacked = pl.BlockSpec((tr, tc), lambda now, i, where_ref: index(i))
    else:
        stacked = pl.BlockSpec((tr, tc), lambda now, i, where_ref: (now * steps + i, 0))
    in_specs = [stacked, stacked, stacked]
    args = [where, w, m, v]
    for layer in range(layers):
        four = pl.BlockSpec((N_CHIPS, tr, tc), lambda now, i, where_ref, layer=layer: (0,) + index(held(layer, now, i)))
        own = pl.BlockSpec((None, tr, tc),
                           lambda now, i, where_ref, layer=layer: (where_ref[0],) + index(held(layer, now, i)))
        in_specs += [four, own, four]
        args += [received[layer], sent[layer], sibling[layer]]
    grid_spec = pltpu.PrefetchScalarGridSpec(num_scalar_prefetch=1, grid=(layers, steps), in_specs=in_specs,
                                             out_specs=[stacked] * 4)
    return pl.pallas_call(body, out_shape=[jax.ShapeDtypeStruct(w.shape, F32)] * 4, grid_spec=grid_spec,
                          name="adamw_weight", compiler_params=_params(("arbitrary", "arbitrary")))(*args)


def _adamw_math(w, g, m, v):
    m = ADAM_B1 * m + (1.0 - ADAM_B1) * g
    v = ADAM_B2 * v + (1.0 - ADAM_B2) * (g * g)
    m_hat = m * (1.0 / (1.0 - ADAM_B1 ** ADAM_STEP))
    v_hat = v * (1.0 / (1.0 - ADAM_B2 ** ADAM_STEP))
    denom = jnp.sqrt(v_hat) + ADAM_EPS
    inv = pl.reciprocal(denom, approx=True)
    inv = inv * (2.0 - denom * inv)
    delta = -ADAM_LR * (m_hat * inv + ADAM_WD * w)
    return delta, m, v


def _adamw(w, m, v, g_mine, g_sibling):
    rows, cols = w.shape
    tr = _row_tile(rows, cols)
    two = g_sibling is not None

    def body(*refs):
        if two:
            w_ref, m_ref, v_ref, ga_ref, gb_ref, g_ref, d_ref, nm_ref, nv_ref = refs
            g = ga_ref[...] + gb_ref[...]
        else:
            w_ref, m_ref, v_ref, ga_ref, g_ref, d_ref, nm_ref, nv_ref = refs
            g = ga_ref[...]
        delta, nm, nv = _adamw_math(w_ref[...], g, m_ref[...], v_ref[...])
        g_ref[...] = g
        d_ref[...] = delta
        nm_ref[...] = nm
        nv_ref[...] = nv

    blk = pl.BlockSpec((tr, cols), lambda i: (i, 0))
    args = [w, m, v, g_mine] + ([g_sibling] if two else [])
    return pl.pallas_call(body, out_shape=[jax.ShapeDtypeStruct((rows, cols), F32)] * 4, grid=(rows // tr,),
                          in_specs=[blk] * len(args), out_specs=[blk] * 4, name="adamw",
                          compiler_params=_params(("parallel",)))(*args)


def _pack_rows(arrays):
    flat = jnp.concatenate([a.reshape(-1) for a in arrays])
    rows = -(-flat.shape[0] // (8 * LANE)) * 8
    return jnp.pad(flat, (0, rows * LANE - flat.shape[0])).reshape(rows, LANE)


def _unpack_rows(packed, shapes):
    flat = packed.reshape(-1)
    out, at = [], 0
    for s in shapes:
        size = math.prod(s)
        out.append(flat[at:at + size].reshape(s))
        at += size
    return out


def kernel(x, p, positions, norm_g, ffn_w_in, ffn_w_out, ple_w_proj, ple_w_gate, rel_bias, mla_w_a, mla_q_norm, mla_kv_norm, mla_w_uq, mla_w_ukv, mla_w_o, dil_w_qkv, dil_w_o, fox_w_qkvf, fox_b_f, fox_w_o, loss_target, m_norm_g, m_ffn_w_in, m_ffn_w_out, m_ple_w_proj, m_ple_w_gate, m_rel_bias, m_mla_w_a, m_mla_q_norm, m_mla_kv_norm, m_mla_w_uq, m_mla_w_ukv, m_mla_w_o, m_dil_w_qkv, m_dil_w_o, m_fox_w_qkvf, m_fox_b_f, m_fox_w_o, v_norm_g, v_ffn_w_in, v_ffn_w_out, v_ple_w_proj, v_ple_w_gate, v_rel_bias, v_mla_w_a, v_mla_q_norm, v_mla_kv_norm, v_mla_w_uq, v_mla_w_ukv, v_mla_w_o, v_dil_w_qkv, v_dil_w_o, v_fox_w_qkvf, v_fox_b_f, v_fox_w_o):
    w = dict(norm_g=norm_g, ffn_w_in=ffn_w_in, ffn_w_out=ffn_w_out, ple_w_proj=ple_w_proj, ple_w_gate=ple_w_gate,
             rel_bias=rel_bias, mla_w_a=mla_w_a, mla_q_norm=mla_q_norm, mla_kv_norm=mla_kv_norm, mla_w_uq=mla_w_uq,
             mla_w_ukv=mla_w_ukv, mla_w_o=mla_w_o, dil_w_qkv=dil_w_qkv, dil_w_o=dil_w_o, fox_w_qkvf=fox_w_qkvf,
             fox_b_f=fox_b_f, fox_w_o=fox_w_o)
    m = dict(norm_g=m_norm_g, ffn_w_in=m_ffn_w_in, ffn_w_out=m_ffn_w_out, ple_w_proj=m_ple_w_proj,
             ple_w_gate=m_ple_w_gate, rel_bias=m_rel_bias, mla_w_a=m_mla_w_a, mla_q_norm=m_mla_q_norm,
             mla_kv_norm=m_mla_kv_norm, mla_w_uq=m_mla_w_uq, mla_w_ukv=m_mla_w_ukv, mla_w_o=m_mla_w_o,
             dil_w_qkv=m_dil_w_qkv, dil_w_o=m_dil_w_o, fox_w_qkvf=m_fox_w_qkvf, fox_b_f=m_fox_b_f, fox_w_o=m_fox_w_o)
    v = dict(norm_g=v_norm_g, ffn_w_in=v_ffn_w_in, ffn_w_out=v_ffn_w_out, ple_w_proj=v_ple_w_proj,
             ple_w_gate=v_ple_w_gate, rel_bias=v_rel_bias, mla_w_a=v_mla_w_a, mla_q_norm=v_mla_q_norm,
             mla_kv_norm=v_mla_kv_norm, mla_w_uq=v_mla_w_uq, mla_w_ukv=v_mla_w_ukv, mla_w_o=v_mla_w_o,
             dil_w_qkv=v_dil_w_qkv, dil_w_o=v_dil_w_o, fox_w_qkvf=v_fox_w_qkvf, fox_b_f=v_fox_b_f, fox_w_o=v_fox_w_o)
    chip = 2 * lax.axis_index("x") + lax.axis_index("y")
    for tree in (w, m, v):
        tree[TRANSPOSED] = jnp.swapaxes(tree[TRANSPOSED], 1, 2)

    small_shapes = [w[k].shape for k in SMALL_SHARDED]
    order = [(i, part) for i in range(DEPTH) for part in (MIXER_PART, COMMON_PART) if _part_names(i, part)]
    gathers = {}
    after = positions
    zero = 0.0
    for i, part in order:
        bufs = [_own_slot((w[k][_layer_slot(k, i)] + zero).astype(BF)) for k in _part_names(i, part)]
        if (i, part) == order[0]:
            bufs.append(_own_slot(_pack_rows([w[k] for k in SMALL_SHARDED])))
        gathers[i, part] = _spread_start(bufs, None, after, f"gather_start_{i}_{part}")
        after = gathers[i, part]["token"]
        if (i, part) == order[0]:
            zero = after[0, 0]
    all_started = after
    state = {}

    def get_part(i, part, after_array):
        is_first = (i, part) == order[0]
        lands = _spread_wait(gathers[i, part], all_started if is_first else after_array, f"gather_wait_{i}_{part}")
        if is_first:
            pieces = [_unpack_rows(lands[-1][k], small_shapes) for k in range(N_CHIPS)]
            small = {name: jnp.concatenate([pieces[k][idx] for k in range(N_CHIPS)], axis=-1)
                     for idx, name in enumerate(SMALL_SHARDED)}
            state["small"] = dict(small, rel_bias=rel_bias, fox_b_f=fox_b_f)
        chunks = dict(zip(_part_names(i, part), lands))
        state[i, part] = {k: a.shape for k, a in chunks.items()}
        return _part_to_compute(i, part, chunks)

    started, forwards = [], {}

    def forward_oldest(after_array):
        i, part, handle = started.pop(0)
        received, sent = _spread_wait(handle, after_array, f"exchange_wait_{i}_{part}")
        forwards[i, part] = _sibling_start(received, sent, after_array, f"sibling_start_{i}_{part}")
        return forwards[i, part]["token"]

    def put_part(i, part, lg):
        contrib = _part_contributions(i, part, lg, state[i, part])
        srcs = [contrib[k] for k in _part_names(i, part)]
        handle = _spread_start([lax.empty(s.shape, s.dtype) for s in srcs], srcs, positions,
                               f"exchange_start_{i}_{part}")
        token = handle["token"]
        if started:
            token = token + forward_oldest(token)
        started.append((i, part, handle))
        return token

    sq, grad_x, sg = _run_layers(x[0], p[:, 0], positions[0], loss_target[0], get_part, lambda: state["small"],
                                 put_part)
    loss = lax.psum(0.5 / D_MODEL * jnp.sum(sq), ("x", "y", "c"))
    forward_oldest(grad_x)

    held = {k: {} for k in BIG}
    for i, part in sorted(forwards, reverse=True):
        received, sent, sibling = _sibling_wait(forwards[i, part], grad_x, f"sibling_wait_{i}_{part}")
        for k, r, s, t in zip(_part_names(i, part), received, sent, sibling):
            held[k][_layer_slot(k, i)] = (r, s, t)
    results = {}
    for k in BIG:
        per_layer = [held[k][slot] for slot in sorted(held[k])]
        outs = _adamw_weight(_as_2d(w[k]), _as_2d(m[k]), _as_2d(v[k]), *[list(col) for col in zip(*per_layer)])
        results[k] = [o.reshape(w[k].shape) for o in outs]
    results[TRANSPOSED] = [jnp.swapaxes(o, 1, 2) for o in results[TRANSPOSED]]

    small_all = SMALL_SHARDED + SMALL_REPLICATED
    full_shapes = [sg[k].shape for k in small_all]
    reduced = dict(zip(small_all, _unpack_rows(_all_reduce_small(_pack_rows([sg[k] for k in small_all])), full_shapes)))
    local_g = []
    for k in small_all:
        g = reduced[k]
        if k in SMALL_SHARDED:
            width = w[k].shape[-1]
            g = lax.dynamic_slice_in_dim(g, chip * width, width, axis=g.ndim - 1)
        local_g.append(g)
    local_shapes = [w[k].shape for k in small_all]
    outs = _adamw(_pack_rows([w[k] for k in small_all]), _pack_rows([m[k] for k in small_all]),
                  _pack_rows([v[k] for k in small_all]), _pack_rows(local_g), None)
    unpacked = [_unpack_rows(o, local_shapes) for o in outs]
    for idx, k in enumerate(small_all):
        results[k] = [u[idx] for u in unpacked]

    return (loss, grad_x[None], *[results[k][0] for k in WEIGHTS], *[results[k][1] for k in WEIGHTS],
            *[results[k][2] for k in WEIGHTS], *[results[k][3] for k in WEIGHTS])
```

```python
import functools
import math

import jax
import jax.numpy as jnp
from jax import lax
from jax.experimental import pallas as pl
from jax.experimental.pallas import tpu as pltpu

F32 = jnp.float32
BF = jnp.bfloat16
MESH = pl.DeviceIdType.MESH
HBM_SPEC = pl.BlockSpec(memory_space=pltpu.HBM)

D_MODEL = 1024
DEPTH = 4
N_MIXERS = 3
D_FF = 2816
NORM_EPS = 1e-6
NEG_INF = -1e30
LANE = 128
HEADS = 16
HEAD_DIM = 64
MLA_Q_RANK = 384
MLA_KV_RANK = 256
MLA_ROPE = 32
MLA_A_PAD = 768
ROPE_THETA = 10000.0
DIL_PATTERNS = ((128, 1), (512, 4), (2048, 16))
Q_BLOCK = 128
DIL_PAIRS = {1: 2, 4: 4, 16: 4}
REL_BUCKETS = 32
REL_MAX_DIST = 2048
N_CHIPS = 4
N_DEV = 8

ADAM_LR = 0.001
ADAM_B1 = 0.9
ADAM_B2 = 0.999
ADAM_EPS = 1e-08
ADAM_WD = 0.01
ADAM_STEP = 10

VMEM_LIMIT = 56 * 1024 * 1024
MATMUL_VMEM_BUDGET = 36 * 1024 * 1024
ROW_TILE = 512
ATTN_TILE = 256
ATTN_Q_TILE = 512
MLA_GROUP = 4
FOX_GROUP = 2


def _params(sem=None):
    return pltpu.CompilerParams(dimension_semantics=sem, vmem_limit_bytes=VMEM_LIMIT)


def _divisor_tiles(dim):
    tiles = [t for t in range(LANE, dim + 1, LANE) if dim % t == 0]
    return tiles or [dim]


def _matmul_tiles(m, n, k, a_bytes, b_bytes, out_bytes, has_add, n_unit=None, k_unit=None):
    best = None
    for tm in _divisor_tiles(m):
        for tn in _divisor_tiles(n_unit or n):
            for tk in _divisor_tiles(k_unit or k):
                if max(tm, tn, tk) > 2048:
                    continue
                vmem = 2 * (tm * tk * a_bytes + tk * tn * b_bytes + tm * tn * out_bytes) + tm * tn * 4
                if has_add:
                    vmem += 2 * tm * tn * 4
                if vmem > MATMUL_VMEM_BUDGET:
                    continue
                steps = (m // tm) * (n // tn) * (k // tk)
                traffic = m * k * a_bytes * (n // tn) + k * n * b_bytes * (m // tm) + m * n * out_bytes
                cost = traffic / 3.0e12 + steps * 0.4e-6
                if best is None or cost < best[0]:
                    best = (cost, tm, tn, tk)
    return best[1:]


def _matmul(a, b, *, ta=False, tb=False, b_chunks=False, out_chunks=False, add=None, out_dtype=F32, name):
    k, m = a.shape if ta else a.shape[::-1]
    n_unit = k_unit = None
    if b_chunks:
        chunks, rows_w, c = b.shape
        if tb:
            kb, n, k_unit = chunks * c, rows_w, c
        else:
            kb, n, n_unit = rows_w, chunks * c, c
    else:
        kb, n = b.shape[::-1] if tb else b.shape
    if out_chunks:
        assert n % N_CHIPS == 0 and add is None
        n_unit = n // N_CHIPS
    assert k == kb, (a.shape, b.shape, ta, tb)
    tm, tn, tk = _matmul_tiles(m, n, k, a.dtype.itemsize, b.dtype.itemsize, jnp.dtype(out_dtype).itemsize,
                               add is not None, n_unit, k_unit)
    nk = k // tk
    dims = (((0 if ta else 1,), (1 if tb else 0,)), ((), ()))

    def body(*refs):
        if add is None:
            a_ref, b_ref, o_ref, acc_ref = refs
            add_ref = None
        else:
            a_ref, b_ref, add_ref, o_ref, acc_ref = refs
        kk = pl.program_id(2)

        @pl.when(kk == 0)
        def _():
            acc_ref[...] = jnp.zeros_like(acc_ref)

        acc_ref[...] += lax.dot_general(a_ref[...].astype(BF), b_ref[...].astype(BF), dims,
                                        preferred_element_type=F32)

        @pl.when(kk == nk - 1)
        def _():
            r = acc_ref[...]
            if add_ref is not None:
                r = r + add_ref[...].astype(F32)
            o_ref[...] = r.astype(out_dtype)

    a_spec = pl.BlockSpec((tk, tm), lambda i, j, q: (q, i)) if ta else pl.BlockSpec((tm, tk), lambda i, j, q: (i, q))
    if b_chunks and tb:
        per_k = k_unit // tk
        b_spec = pl.BlockSpec((None, tn, tk), lambda i, j, q: (q // per_k, j, q % per_k))
    elif b_chunks:
        per_n = n_unit // tn
        b_spec = pl.BlockSpec((None, tk, tn), lambda i, j, q: (j // per_n, q, j % per_n))
    elif tb:
        b_spec = pl.BlockSpec((tn, tk), lambda i, j, q: (j, q))
    else:
        b_spec = pl.BlockSpec((tk, tn), lambda i, j, q: (q, j))
    if out_chunks:
        per_o = n_unit // tn
        o_spec = pl.BlockSpec((None, tm, tn), lambda i, j, q: (j // per_o, i, j % per_o))
        out_shape = jax.ShapeDtypeStruct((N_CHIPS, m, n_unit), out_dtype)
    else:
        o_spec = pl.BlockSpec((tm, tn), lambda i, j, q: (i, j))
        out_shape = jax.ShapeDtypeStruct((m, n), out_dtype)
    in_specs = [a_spec, b_spec]
    args = [a, b]
    if add is not None:
        in_specs.append(o_spec)
        args.append(add)
    return pl.pallas_call(
        body, out_shape=out_shape, grid=(m // tm, n // tn, nk),
        in_specs=in_specs, out_specs=o_spec, scratch_shapes=[pltpu.VMEM((tm, tn), F32)], name=name,
        compiler_params=_params(("parallel", "parallel", "arbitrary")))(*args)


def _rowwise(body, name, rows, ins, outs, tr=ROW_TILE):
    def row_spec(cols):
        return pl.BlockSpec((tr, cols), lambda i: (i, 0))

    def full_spec(shape):
        zeros = (0,) * len(shape)
        return pl.BlockSpec(shape, lambda i: zeros)

    in_specs = [row_spec(a.shape[1]) if kind == "row" else full_spec(a.shape) for a, kind in ins]
    out_specs = [row_spec(shape[1]) if kind == "row" else full_spec(shape) for shape, _, kind in outs]
    out_shape = [jax.ShapeDtypeStruct(shape, dtype) for shape, dtype, _ in outs]
    return pl.pallas_call(body, out_shape=out_shape, grid=(rows // tr,), in_specs=in_specs, out_specs=out_specs,
                          name=name, compiler_params=_params(("arbitrary",)))(*[a for a, _ in ins])


def _rstd(x):
    return lax.rsqrt(jnp.mean(x * x, axis=-1, keepdims=True) + NORM_EPS)


def _rms_bwd_math(x, g, dy):
    r = _rstd(x)
    gd = dy * g
    dx = r * gd - x * (r * r * r) * jnp.mean(gd * x, axis=-1, keepdims=True)
    dg = jnp.sum(dy * x * r, axis=0, keepdims=True)
    return dx, dg


def _sigmoid(x):
    return 0.5 * jnp.tanh(0.5 * x) + 0.5


def _init_acc(*refs):
    @pl.when(pl.program_id(0) == 0)
    def _():
        for r in refs:
            r[...] = jnp.zeros_like(r)


def _prenorm(h, g):
    rows, cols = h.shape

    def body(h_ref, g_ref, o_ref):
        x = h_ref[...]
        o_ref[...] = (x * _rstd(x) * g_ref[...]).astype(BF)

    return _rowwise(body, "prenorm", rows, [(h, "row"), (g, "full")], [((rows, cols), BF, "row")])[0]


def _post_residual(h, y, g_post, g_pre):
    rows, cols = h.shape
    with_pre = g_pre is not None

    def body(*refs):
        if with_pre:
            h_ref, y_ref, gp_ref, gq_ref, hn_ref, hb_ref = refs
        else:
            h_ref, y_ref, gp_ref, hn_ref, hb_ref = refs
        yv = y_ref[...]
        hn = h_ref[...] + yv * _rstd(yv) * gp_ref[...]
        hn_ref[...] = hn
        hb_ref[...] = (hn * _rstd(hn) * gq_ref[...] if with_pre else hn).astype(BF)

    ins = [(h, "row"), (y, "row"), (g_post, "full")] + ([(g_pre, "full")] if with_pre else [])
    return _rowwise(body, "post_residual_pre" if with_pre else "post_residual", rows, ins,
                    [((rows, cols), F32, "row"), ((rows, cols), BF, "row")])


def _ple_forward(h2, pp, z, g_pre):
    rows, cols = h2.shape

    def body(h_ref, p_ref, z_ref, g_ref, h3_ref, hb_ref):
        h3 = h_ref[...] + p_ref[...] * _sigmoid(z_ref[...])
        h3_ref[...] = h3
        hb_ref[...] = (h3 * _rstd(h3) * g_ref[...]).astype(BF)

    return _rowwise(body, "ple_forward", rows, [(h2, "row"), (pp, "row"), (z, "row"), (g_pre, "full")],
                    [((rows, cols), F32, "row"), ((rows, cols), BF, "row")])


def _ple_loss(h2, pp, z, target):
    rows, cols = h2.shape

    def body(h_ref, p_ref, z_ref, t_ref, dh_ref, sq_ref):
        _init_acc(sq_ref)
        err = h_ref[...] + p_ref[...] * _sigmoid(z_ref[...]) - t_ref[...]
        dh_ref[...] = err * (1.0 / cols)
        sq_ref[...] += jnp.sum(err * err, axis=0, keepdims=True)

    return _rowwise(body, "ple_loss", rows, [(h2, "row"), (pp, "row"), (z, "row"), (target, "row")],
                    [((rows, cols), F32, "row"), ((1, cols), F32, "acc")])


def _ple_backward(dh3, pp, z):
    rows, cols = dh3.shape

    def body(d_ref, p_ref, z_ref, dpp_ref, dz_ref):
        d = d_ref[...]
        s = _sigmoid(z_ref[...])
        dpp_ref[...] = (d * s).astype(BF)
        dz_ref[...] = (d * p_ref[...] * s * (1.0 - s)).astype(BF)

    return _rowwise(body, "ple_backward", rows, [(dh3, "row"), (pp, "row"), (z, "row")],
                    [((rows, cols), BF, "row"), ((rows, cols), BF, "row")])


def _rms_backward(x, g, dy, add, out_dtype):
    rows, cols = x.shape
    with_add = add is not None

    def body(*refs):
        if with_add:
            x_ref, g_ref, dy_ref, add_ref, dx_ref, dg_ref = refs
        else:
            x_ref, g_ref, dy_ref, dx_ref, dg_ref = refs
        _init_acc(dg_ref)
        dx, dg = _rms_bwd_math(x_ref[...], g_ref[...], dy_ref[...].astype(F32))
        if with_add:
            dx = dx + add_ref[...]
        dx_ref[...] = dx.astype(out_dtype)
        dg_ref[...] += dg

    ins = [(x, "row"), (g, "full"), (dy, "row")] + ([(add, "row")] if with_add else [])
    return _rowwise(body, "rms_backward_add" if with_add else "rms_backward", rows, ins,
                    [((rows, cols), out_dtype, "row"), ((1, cols), F32, "acc")])


def _swiglu_forward(gu):
    rows = gu.shape[0]

    def body(gu_ref, o_ref):
        g = gu_ref[:, :D_FF].astype(F32)
        o_ref[...] = (g * _sigmoid(g) * gu_ref[:, D_FF:].astype(F32)).astype(BF)

    return _rowwise(body, "swiglu_forward", rows, [(gu, "row")], [((rows, D_FF), BF, "row")])[0]


def _swiglu_backward(gu, dact):
    rows = gu.shape[0]

    def body(gu_ref, d_ref, o_ref):
        g = gu_ref[:, :D_FF].astype(F32)
        u = gu_ref[:, D_FF:].astype(F32)
        d = d_ref[...].astype(F32)
        s = _sigmoid(g)
        gs = g * s
        o_ref[:, :D_FF] = (d * u * (s + gs * (1.0 - s))).astype(BF)
        o_ref[:, D_FF:] = (d * gs).astype(BF)

    return _rowwise(body, "swiglu_backward", rows, [(gu, "row"), (dact, "row")], [((rows, 2 * D_FF), BF, "row")])[0]


def _rope_tables(positions):
    half = MLA_ROPE // 2
    inv = ROPE_THETA ** (-jnp.arange(half, dtype=F32) / half)
    ang = positions.astype(F32)[:, None] * inv
    cos, sin = jnp.cos(ang), jnp.sin(ang)
    rows = positions.shape[0]
    c = jnp.ones((rows, LANE), F32).at[:, 64:80].set(cos).at[:, 80:96].set(cos)
    sa = jnp.zeros((rows, LANE), F32).at[:, 64:80].set(-sin)
    sb = jnp.zeros((rows, LANE), F32).at[:, 80:96].set(sin)
    return c, sa, sb


def _rope_apply(x, c, sa, sb):
    return x * c + pltpu.roll(x, LANE - 16, 1) * sa + pltpu.roll(x, 16, 1) * sb


def _rope_apply_t(dy, c, sa, sb):
    return dy * c + pltpu.roll(dy * sa, 16, 1) + pltpu.roll(dy * sb, LANE - 16, 1)


def _rope_heads(x, tables, transpose, name):
    rows, cols = x.shape

    def body(x_ref, c_ref, sa_ref, sb_ref, o_ref):
        fn = _rope_apply_t if transpose else _rope_apply
        c, sa, sb = c_ref[...], sa_ref[...], sb_ref[...]
        for head in range(cols // LANE):
            lanes = slice(head * LANE, (head + 1) * LANE)
            o_ref[:, lanes] = fn(x_ref[:, lanes].astype(F32), c, sa, sb).astype(BF)

    blk = pl.BlockSpec((ROW_TILE, cols), lambda i: (i, 0))
    tbl = pl.BlockSpec((ROW_TILE, LANE), lambda i: (i, 0))
    return pl.pallas_call(body, out_shape=jax.ShapeDtypeStruct((rows, cols), BF), grid=(rows // ROW_TILE,),
                          in_specs=[blk, tbl, tbl, tbl], out_specs=blk, name=name,
                          compiler_params=_params(("parallel",)))(x, *tables)


def _mla_mid_forward(a, q_norm, kv_norm, tables):
    rows = a.shape[0]
    qr, kvr = MLA_Q_RANK, MLA_KV_RANK

    def body(a_ref, qn_ref, kn_ref, c_ref, sa_ref, sb_ref, cq_ref, ckv_ref, kr_ref):
        aq = a_ref[:, 0:qr]
        akv = a_ref[:, qr:qr + kvr]
        cq_ref[...] = (aq * _rstd(aq) * qn_ref[...]).astype(BF)
        ckv_ref[...] = (akv * _rstd(akv) * kn_ref[...]).astype(BF)
        kr_ref[...] = _rope_apply(a_ref[:, qr + kvr:], c_ref[...], sa_ref[...], sb_ref[...]).astype(BF)

    ins = [(a, "row"), (q_norm, "full"), (kv_norm, "full")] + [(t, "row") for t in tables]
    return _rowwise(body, "mla_mid_forward", rows, ins,
                    [((rows, qr), BF, "row"), ((rows, kvr), BF, "row"), ((rows, LANE), BF, "row")])


def _mla_mid_backward(a, q_norm, kv_norm, tables, dcq, dckv, dkr):
    rows = a.shape[0]
    qr, kvr = MLA_Q_RANK, MLA_KV_RANK

    def body(a_ref, qn_ref, kn_ref, c_ref, sa_ref, sb_ref, dcq_ref, dckv_ref, dkr_ref, da_ref, dqn_ref, dkn_ref):
        _init_acc(dqn_ref, dkn_ref)
        dxq, dgq = _rms_bwd_math(a_ref[:, 0:qr], qn_ref[...], dcq_ref[...])
        dxk, dgk = _rms_bwd_math(a_ref[:, qr:qr + kvr], kn_ref[...], dckv_ref[...])
        da_ref[:, 0:qr] = dxq.astype(BF)
        da_ref[:, qr:qr + kvr] = dxk.astype(BF)
        da_ref[:, qr + kvr:] = _rope_apply_t(dkr_ref[...], c_ref[...], sa_ref[...], sb_ref[...]).astype(BF)
        dqn_ref[...] += dgq
        dkn_ref[...] += dgk

    ins = ([(a, "row"), (q_norm, "full"), (kv_norm, "full")] + [(t, "row") for t in tables]
           + [(dcq, "row"), (dckv, "row"), (dkr, "row")])
    return _rowwise(body, "mla_mid_backward", rows, ins,
                    [((rows, MLA_A_PAD), BF, "row"), ((1, qr), F32, "acc"), ((1, kvr), F32, "acc")])


def _attn_specs(rows, kv_off, g):
    head = pl.BlockSpec((rows, g * LANE), lambda h: (0, h))
    kv_head = pl.BlockSpec((rows, g * LANE), lambda h: (0, h + kv_off // g))
    shared = pl.BlockSpec((rows, LANE), lambda h: (0, 0))
    col_vec = pl.BlockSpec((g, rows, 1), lambda h: (h, 0, 0))
    row_vec = pl.BlockSpec((g, 1, rows), lambda h: (h, 0, 0))
    return head, kv_head, shared, col_vec, row_vec


def _attn_forward(q, kv, kv_off, kr, cum_col, cum_row, scale, group_size, name):
    rows = q.shape[0]
    heads = HEADS
    t = ATTN_TILE
    tq = ATTN_Q_TILE
    per = tq // t
    has_kr = kr is not None
    has_f = cum_col is not None
    group = range(group_size)

    def body(*refs):
        it = iter(refs)
        q_ref, kv_ref = next(it), next(it)
        kr_ref = next(it) if has_kr else None
        cc_ref = next(it) if has_f else None
        cr_ref = next(it) if has_f else None
        o_ref, lse_ref = next(it), next(it)
        lo = lax.broadcasted_iota(jnp.int32, (1, LANE), 1) < HEAD_DIM
        row = lax.broadcasted_iota(jnp.int32, (tq, t), 0)
        col = lax.broadcasted_iota(jnp.int32, (tq, t), 1)
        lanes = [slice(g * LANE, (g + 1) * LANE) for g in group]

        def q_block(i, _):
            qs = pl.ds(pl.multiple_of(i * tq, tq), tq)
            qbs = [q_ref[qs, lanes[g]] for g in group]
            cqs = [cc_ref[g, qs, :] if has_f else None for g in group]

            def step(j, carry, diag):
                ks = pl.ds(pl.multiple_of(j * t, t), t)
                skip = diag * t if diag and has_f else 0
                other = kr_ref[ks, :] if has_kr else jnp.zeros((t, LANE), BF)
                kvbs = [kv_ref[ks, lanes[g]] for g in group]

                def logit(g):
                    return lax.dot_general(qbs[g][skip:], jnp.where(lo, kvbs[g], other), (((1,), (1,)), ((), ())),
                                           preferred_element_type=F32)

                logits = {g: logit(g) for g in (group if has_f else group[:1])}
                out = []
                for g in group:
                    m, l, acc = (a[skip:] for a in carry[g])
                    if not has_f and g + 1 < len(group):
                        logits[g + 1] = logit(g + 1)
                    s = logits[g] * scale
                    if has_f:
                        s = s + (cqs[g][skip:] - cr_ref[g, :, ks])
                    if diag is not None:
                        s = jnp.where(col[skip:] + diag * t <= row[skip:], s, NEG_INF)
                    mn = jnp.maximum(m, jnp.max(s, axis=1, keepdims=True))
                    alpha = jnp.exp(m - mn)
                    p = jnp.exp(s - mn)
                    l = alpha * l + jnp.sum(p, axis=1, keepdims=True)
                    acc = alpha * acc + jnp.dot(p.astype(BF), kvbs[g], preferred_element_type=F32)
                    new = (mn, l, acc)
                    if skip:
                        new = tuple(jnp.concatenate([old[:skip], a], axis=0) for old, a in zip(carry[g], new))
                    out.append(new)
                return tuple(out)

            init = tuple((jnp.full((tq, 1), NEG_INF, F32), jnp.zeros((tq, 1), F32), jnp.zeros((tq, LANE), F32))
                         for _ in group)
            carry = lax.fori_loop(0, i * per, lambda j, c: step(j, c, None), init)
            for d in range(per):
                carry = step(i * per + d, carry, d)
            for g, (m, l, acc) in enumerate(carry):
                o_ref[qs, lanes[g]] = jnp.where(lo, 0.0, acc * (1.0 / l)).astype(BF)
                lse_ref[g, qs, :] = m + jnp.log(l)
            return 0

        lax.fori_loop(0, rows // tq, q_block, 0)

    head, kv_head, shared, col_vec, row_vec = _attn_specs(rows, kv_off, group_size)
    in_specs, args = [head, kv_head], [q, kv]
    if has_kr:
        in_specs.append(shared)
        args.append(kr)
    if has_f:
        in_specs += [col_vec, row_vec]
        args += [cum_col, cum_row]
    return pl.pallas_call(
        body, out_shape=[jax.ShapeDtypeStruct((rows, heads * LANE), BF), jax.ShapeDtypeStruct((heads, rows, 1), F32)],
        grid=(heads // group_size,), in_specs=in_specs, out_specs=[head, col_vec], name=name,
        compiler_params=_params(("arbitrary",)))(*args)


def _attn_backward(q, kv, kv_off, kr, cum_col, cum_row, o, do, lse, scale, group_size, name):
    rows = q.shape[0]
    heads = HEADS
    t = ATTN_TILE
    nb = rows // t
    has_kr = kr is not None
    has_f = cum_col is not None
    group = range(group_size)

    def body(*refs):
        it = iter(refs)
        q_ref, kv_ref = next(it), next(it)
        kr_ref = next(it) if has_kr else None
        cc_ref = next(it) if has_f else None
        cr_ref = next(it) if has_f else None
        o_ref, do_ref, lse_ref = next(it), next(it), next(it)
        dq_ref, dkv_ref = next(it), next(it)
        dkr_ref = next(it) if has_kr else None
        dck_ref = next(it) if has_f else None
        dcq_ref = next(it) if has_f else None
        dq_acc = next(it)
        lo = lax.broadcasted_iota(jnp.int32, (1, LANE), 1) < HEAD_DIM
        causal = (lax.broadcasted_iota(jnp.int32, (t, t), 1) <= lax.broadcasted_iota(jnp.int32, (t, t), 0))
        lanes = [slice(g * LANE, (g + 1) * LANE) for g in group]

        dq_acc[...] = jnp.zeros_like(dq_acc)
        if has_kr:
            _init_acc(dkr_ref)
        if has_f:
            dcq_ref[...] = jnp.zeros_like(dcq_ref)

        def kv_block(j, _):
            ks = pl.ds(pl.multiple_of(j * t, t), t)
            other = kr_ref[ks, :] if has_kr else jnp.zeros((t, LANE), BF)
            kvbs = [kv_ref[ks, lanes[g]] for g in group]
            kks = [jnp.where(lo, kvbs[g], other) for g in group]
            cks = [cr_ref[g, :, ks] if has_f else None for g in group]

            def pair(i, carry, diag):
                qs = pl.ds(pl.multiple_of(i * t, t), t)
                nt = (((1,), (1,)), ((), ()))

                def first_stage(g):
                    qb = q_ref[qs, lanes[g]]
                    dob = do_ref[qs, lanes[g]]
                    return (qb, dob, lax.dot_general(qb, kks[g], nt, preferred_element_type=F32),
                            lax.dot_general(dob, kvbs[g], nt, preferred_element_type=F32))

                first = {g: first_stage(g) for g in (group[:1] if has_f else group)}
                out = []
                for g in group:
                    dkk, dvv, dcs = carry[g]
                    qb, dob, logit, dp = first[g]
                    if has_f and g + 1 < len(group):
                        first[g + 1] = first_stage(g + 1)
                    s = logit * scale
                    if has_f:
                        s = s + (cc_ref[g, qs, :] - cks[g])
                    if diag:
                        s = jnp.where(causal, s, NEG_INF)
                    p = jnp.exp(s - lse_ref[g, qs, :])
                    delta = jnp.sum(dob.astype(F32) * o_ref[qs, lanes[g]].astype(F32), axis=1, keepdims=True)
                    ds = p * (dp - delta)
                    dsb = ds.astype(BF)
                    dvv = dvv + lax.dot_general(p.astype(BF), dob, (((0,), (0,)), ((), ())), preferred_element_type=F32)
                    dkk = dkk + lax.dot_general(dsb, qb, (((0,), (0,)), ((), ())), preferred_element_type=F32)
                    dq_acc[qs, lanes[g]] += jnp.dot(dsb, kks[g], preferred_element_type=F32)
                    if has_f:
                        dcs = dcs + jnp.sum(ds, axis=0, keepdims=True)
                        dcq_ref[g, qs, :] += jnp.sum(ds, axis=1, keepdims=True)
                    out.append((dkk, dvv, dcs))
                return tuple(out)

            init = tuple((jnp.zeros((t, LANE), F32), jnp.zeros((t, LANE), F32), jnp.zeros((1, t), F32)) for _ in group)
            carry = pair(j, init, True)
            carry = lax.fori_loop(j + 1, nb, lambda i, c: pair(i, c, False), carry)
            for g, (dkk, dvv, dcs) in enumerate(carry):
                dkk = dkk * scale
                dkv_ref[ks, lanes[g]] = jnp.where(lo, dkk, dvv).astype(BF)
                if has_kr:
                    dkr_ref[ks, :] += jnp.where(lo, 0.0, dkk)
                if has_f:
                    dck_ref[g, :, ks] = -dcs
            return 0

        lax.fori_loop(0, nb, kv_block, 0)
        dq_ref[...] = (dq_acc[...] * scale).astype(BF)

    head, kv_head, shared, col_vec, row_vec = _attn_specs(rows, kv_off, group_size)
    in_specs, args = [head, kv_head], [q, kv]
    if has_kr:
        in_specs.append(shared)
        args.append(kr)
    if has_f:
        in_specs += [col_vec, row_vec]
        args += [cum_col, cum_row]
    in_specs += [head, head, col_vec]
    args += [o, do, lse]
    out_shape = [jax.ShapeDtypeStruct((rows, heads * LANE), BF), jax.ShapeDtypeStruct((rows, heads * LANE), BF)]
    out_specs = [head, head]
    if has_kr:
        out_shape.append(jax.ShapeDtypeStruct((rows, LANE), F32))
        out_specs.append(shared)
    if has_f:
        out_shape += [jax.ShapeDtypeStruct((heads, 1, rows), F32), jax.ShapeDtypeStruct((heads, rows, 1), F32)]
        out_specs += [row_vec, col_vec]
    return pl.pallas_call(
        body, out_shape=out_shape, grid=(heads // group_size,), in_specs=in_specs, out_specs=out_specs,
        scratch_shapes=[pltpu.VMEM((rows, group_size * LANE), F32)], name=name,
        compiler_params=_params(("arbitrary",)))(*args)


def _tri_dot(tri, x):
    return jnp.dot(tri, x, preferred_element_type=F32, precision=lax.Precision.HIGHEST)


def _forget_forward(f_raw, b_f):
    rows = f_raw.shape[0]
    t = ATTN_TILE

    def body(f_ref, b_ref, cum_ref):
        tri = (lax.broadcasted_iota(jnp.int32, (t, t), 1) <= lax.broadcasted_iota(jnp.int32, (t, t), 0)).astype(F32)

        def blk(i, carry):
            sl = pl.ds(pl.multiple_of(i * t, t), t)
            xv = f_ref[sl, :] + b_ref[...]
            log_f = jnp.minimum(xv, 0.0) - jnp.log(1.0 + jnp.exp(-jnp.abs(xv)))
            cum_ref[sl, :] = _tri_dot(tri, log_f) + carry
            return carry + jnp.sum(log_f, axis=0, keepdims=True)

        lax.fori_loop(0, rows // t, blk, jnp.zeros((1, LANE), F32))

    return pl.pallas_call(body, out_shape=jax.ShapeDtypeStruct((rows, LANE), F32), name="forget_forward",
                          compiler_params=_params())(f_raw, b_f)


def _forget_backward(f_raw, b_f, dcum):
    rows = f_raw.shape[0]
    t = ATTN_TILE
    nb = rows // t

    def body(f_ref, b_ref, dc_ref, df_ref, db_ref):
        tri = (lax.broadcasted_iota(jnp.int32, (t, t), 1) >= lax.broadcasted_iota(jnp.int32, (t, t), 0)).astype(F32)

        def blk(i, carry):
            later, db = carry
            sl = pl.ds(pl.multiple_of((nb - 1 - i) * t, t), t)
            dc = dc_ref[sl, :]
            dlog = _tri_dot(tri, dc) + later
            xv = f_ref[sl, :] + b_ref[...]
            df = dlog / (1.0 + jnp.exp(xv))
            df_ref[sl, :] = df.astype(BF)
            return later + jnp.sum(dc, axis=0, keepdims=True), db + jnp.sum(df, axis=0, keepdims=True)

        _, db = lax.fori_loop(0, nb, blk, (jnp.zeros((1, LANE), F32), jnp.zeros((1, LANE), F32)))
        db_ref[...] = db

    return pl.pallas_call(body, out_shape=[jax.ShapeDtypeStruct((rows, LANE), BF), jax.ShapeDtypeStruct((1, LANE), F32)],
                          name="forget_backward", compiler_params=_params())(f_raw, b_f, dcum)


def _t5_bucket(dist):
    max_exact = REL_BUCKETS // 2
    n = jnp.maximum(dist.astype(F32), 1.0)
    large = max_exact + (jnp.log(n / max_exact) / math.log(REL_MAX_DIST / max_exact)
                         * (REL_BUCKETS - max_exact)).astype(jnp.int32)
    large = jnp.minimum(large, REL_BUCKETS - 1)
    return jnp.where(dist < max_exact, dist, large)


def _dil_buckets(dilation):
    i = jnp.arange(Q_BLOCK)[:, None]
    j = jnp.arange(Q_BLOCK)[None, :]
    cur = _t5_bucket(jnp.clip(i - j, 0) * dilation).astype(jnp.int32)
    prev = _t5_bucket(jnp.clip(Q_BLOCK + i - j, 0) * dilation).astype(jnp.int32)
    return cur, prev


def _dil_bias_tiles(tbl_ref, bc_ref, bp_ref, bias_ref, group, hp, pairs):
    ii = lax.broadcasted_iota(jnp.int32, (Q_BLOCK, Q_BLOCK), 0)
    jj = lax.broadcasted_iota(jnp.int32, (Q_BLOCK, Q_BLOCK), 1)
    for hh in range(2 * pairs):
        col = group * HEADS + 2 * pairs * hp + hh
        acc_c = jnp.zeros((Q_BLOCK, Q_BLOCK), F32)
        acc_p = jnp.zeros((Q_BLOCK, Q_BLOCK), F32)
        for b in range(REL_BUCKETS):
            val = tbl_ref[b, col]
            acc_c = jnp.where(bc_ref[...] == b, val, acc_c)
            acc_p = jnp.where(bp_ref[...] == b, val, acc_p)
        bias_ref[2 * hh] = jnp.where(jj <= ii, acc_c, NEG_INF)
        bias_ref[2 * hh + 1] = jnp.where(jj >= ii, acc_p, NEG_INF)


def _dil_view(qkv, group, dilation):
    if dilation == 1:
        return qkv
    width = 3 * HEADS * HEAD_DIM
    return qkv[:, group * width:(group + 1) * width].reshape(qkv.shape[0] // dilation, dilation * width)


def _dil_specs(group, dilation, length):
    width = DIL_PAIRS[dilation] * LANE
    per = 8 // DIL_PAIRS[dilation]

    def col(kind):
        if dilation == 1:
            return pl.BlockSpec((length, width), lambda hp, r: (0, (group * 3 + kind) * per + hp))
        return pl.BlockSpec((length, width), lambda hp, r: (0, (r * 3 + kind) * per + hp))

    out = pl.BlockSpec((length, width), lambda hp, r: (0, r * per + hp))
    tile = pl.BlockSpec((Q_BLOCK, Q_BLOCK), lambda hp, r: (0, 0))
    table = pl.BlockSpec(memory_space=pltpu.SMEM)
    return col, out, tile, table


def _dil_forward(view, group, dilation, table, buckets):
    length = view.shape[0]
    rows = length * dilation
    pairs = DIL_PAIRS[dilation]
    nb = length // Q_BLOCK
    scale = HEAD_DIM ** -0.5
    qb = Q_BLOCK

    def body(tbl_ref, bc_ref, bp_ref, q_ref, k_ref, v_ref, o_ref, lse_ref, bias_ref):
        hp = pl.program_id(0)

        @pl.when(pl.program_id(1) == 0)
        def _():
            _dil_bias_tiles(tbl_ref, bc_ref, bp_ref, bias_ref, group, hp, pairs)

        lo = lax.broadcasted_iota(jnp.int32, (1, LANE), 1) < HEAD_DIM
        nt = (((1,), (1,)), ((), ()))

        def blk(n, first):
            cur = pl.ds(0, qb) if first else pl.ds(pl.multiple_of(n * qb, qb), qb)
            prev = None if first else pl.ds(pl.multiple_of((n - 1) * qb, qb), qb)
            logits = []
            for pair in range(pairs):
                lanes = slice(pair * LANE, (pair + 1) * LANE)
                qn = q_ref[cur, lanes] * scale
                for hh in range(2):
                    qm = jnp.where(lo if hh == 0 else ~lo, qn, jnp.zeros_like(qn))
                    s_c = lax.dot_general(qm, k_ref[cur, lanes], nt, preferred_element_type=F32)
                    s_p = None if first else lax.dot_general(qm, k_ref[prev, lanes], nt, preferred_element_type=F32)
                    logits.append((s_c, s_p))
            for pair in range(pairs):
                lanes = slice(pair * LANE, (pair + 1) * LANE)
                outs, lses = [], []
                for hh in range(2):
                    bias = 4 * pair + 2 * hh
                    s_c, s_p = logits[2 * pair + hh]
                    s_c = s_c + bias_ref[bias]
                    m = jnp.max(s_c, axis=1, keepdims=True)
                    if not first:
                        s_p = s_p + bias_ref[bias + 1]
                        m = jnp.maximum(m, jnp.max(s_p, axis=1, keepdims=True))
                    e_c = jnp.exp(s_c - m)
                    l = jnp.sum(e_c, axis=1, keepdims=True)
                    acc = jnp.dot(e_c.astype(BF), v_ref[cur, lanes], preferred_element_type=F32)
                    if not first:
                        e_p = jnp.exp(s_p - m)
                        l = l + jnp.sum(e_p, axis=1, keepdims=True)
                        acc = acc + jnp.dot(e_p.astype(BF), v_ref[prev, lanes], preferred_element_type=F32)
                    outs.append(acc * (1.0 / l))
                    lses.append(m + jnp.log(l))
                o_ref[cur, lanes] = jnp.where(lo, outs[0], outs[1])
                lse_ref[cur, lanes] = jnp.where(lo, lses[0], lses[1])
            return 0

        blk(0, True)
        if nb > 1:
            lax.fori_loop(1, nb, lambda n, _: blk(n, False), 0)

    col, out, tile, tbl = _dil_specs(group, dilation, length)
    bc, bp = buckets
    o, lse = pl.pallas_call(
        body, out_shape=[jax.ShapeDtypeStruct((length, dilation * D_MODEL), F32)] * 2,
        grid=(8 // pairs, dilation), in_specs=[tbl, tile, tile, col(0), col(1), col(2)], out_specs=[out, out],
        scratch_shapes=[pltpu.VMEM((4 * pairs, qb, qb), F32)], name=f"dilated_forward_{dilation}",
        compiler_params=_params(("arbitrary", "arbitrary")))(
            table, bc, bp, view, view, view)
    return o.reshape(rows, D_MODEL), lse.reshape(rows, D_MODEL)


def _dil_backward(view, group, dilation, table, buckets, do_g, lse, dlt):
    length = view.shape[0]
    rows = length * dilation
    pairs = DIL_PAIRS[dilation]
    nb = length // Q_BLOCK
    scale = HEAD_DIM ** -0.5
    qb = Q_BLOCK

    def body(tbl_ref, bc_ref, bp_ref, q_ref, k_ref, v_ref, do_ref, lse_ref, dlt_ref,
             dq_ref, dk_ref, dv_ref, db_ref, bias_ref, dk_acc, dv_acc):
        hp = pl.program_id(0)

        @pl.when(pl.program_id(1) == 0)
        def _():
            _dil_bias_tiles(tbl_ref, bc_ref, bp_ref, bias_ref, group, hp, pairs)
            db_ref[...] = jnp.zeros_like(db_ref)

        dk_acc[...] = jnp.zeros_like(dk_acc)
        dv_acc[...] = jnp.zeros_like(dv_acc)
        lo = lax.broadcasted_iota(jnp.int32, (1, LANE), 1) < HEAD_DIM
        tn = (((0,), (0,)), ((), ()))
        nt = (((1,), (1,)), ((), ()))

        def blk(n, first):
            cur = pl.ds(0, qb) if first else pl.ds(pl.multiple_of(n * qb, qb), qb)
            prev = None if first else pl.ds(pl.multiple_of((n - 1) * qb, qb), qb)
            inputs = []
            for pair in range(pairs):
                lanes = slice(pair * LANE, (pair + 1) * LANE)
                qn = q_ref[cur, lanes] * scale
                don = do_ref[cur, lanes]
                for hh in range(2):
                    mask = lo if hh == 0 else ~lo
                    qm = jnp.where(mask, qn, jnp.zeros_like(qn))
                    dom = jnp.where(mask, don, jnp.zeros_like(don))
                    stage = [qm, dom, lax.dot_general(qm, k_ref[cur, lanes], nt, preferred_element_type=F32),
                             lax.dot_general(dom, v_ref[cur, lanes], nt, preferred_element_type=F32)]
                    if not first:
                        stage += [lax.dot_general(qm, k_ref[prev, lanes], nt, preferred_element_type=F32),
                                  lax.dot_general(dom, v_ref[prev, lanes], nt, preferred_element_type=F32)]
                    inputs.append(stage)
            for pair in range(pairs):
                lanes = slice(pair * LANE, (pair + 1) * LANE)
                kc = k_ref[cur, lanes]
                if not first:
                    kp = k_ref[prev, lanes]
                lse_n = lse_ref[cur, lanes]
                dlt_n = dlt_ref[cur, lanes]
                dqs = []
                dkc = jnp.zeros((qb, LANE), F32)
                dkp = jnp.zeros((qb, LANE), F32)
                dvc = jnp.zeros((qb, LANE), F32)
                dvp = jnp.zeros((qb, LANE), F32)
                for hh in range(2):
                    bias = 4 * pair + 2 * hh
                    mask = lo if hh == 0 else ~lo
                    qm, dom, s_c, dp_c = inputs[2 * pair + hh][:4]
                    lse_h = jnp.max(jnp.where(mask, lse_n, -3e38), axis=1, keepdims=True)
                    dlt_h = jnp.max(jnp.where(mask, dlt_n, -3e38), axis=1, keepdims=True)
                    p_c = jnp.exp(s_c + bias_ref[bias] - lse_h)
                    ds_c = p_c * (dp_c - dlt_h)
                    db_ref[pair, 2 * hh] += ds_c
                    dsc_b = ds_c.astype(BF)
                    dq = jnp.dot(dsc_b, kc, preferred_element_type=F32)
                    dkc = dkc + lax.dot_general(dsc_b, qm, tn, preferred_element_type=F32)
                    dvc = dvc + lax.dot_general(p_c.astype(BF), dom, tn, preferred_element_type=F32)
                    if not first:
                        s_p, dp_p = inputs[2 * pair + hh][4:]
                        p_p = jnp.exp(s_p + bias_ref[bias + 1] - lse_h)
                        ds_p = p_p * (dp_p - dlt_h)
                        db_ref[pair, 2 * hh + 1] += ds_p
                        dsp_b = ds_p.astype(BF)
                        dq = dq + jnp.dot(dsp_b, kp, preferred_element_type=F32)
                        dkp = dkp + lax.dot_general(dsp_b, qm, tn, preferred_element_type=F32)
                        dvp = dvp + lax.dot_general(p_p.astype(BF), dom, tn, preferred_element_type=F32)
                    dqs.append(dq)
                dq_ref[cur, lanes] = (jnp.where(lo, dqs[0], dqs[1]) * scale).astype(BF)
                dk_acc[cur, lanes] += dkc
                dv_acc[cur, lanes] += dvc
                if not first:
                    dk_acc[prev, lanes] += dkp
                    dv_acc[prev, lanes] += dvp
            return 0

        blk(0, True)
        if nb > 1:
            lax.fori_loop(1, nb, lambda n, _: blk(n, False), 0)
        dk_ref[...] = dk_acc[...].astype(BF)
        dv_ref[...] = dv_acc[...].astype(BF)

    col, out, tile, tbl = _dil_specs(group, dilation, length)
    bc, bp = buckets
    wide = (length, dilation * D_MODEL)
    dq, dk, dv, db = pl.pallas_call(
        body, out_shape=[jax.ShapeDtypeStruct(wide, BF)] * 3 + [jax.ShapeDtypeStruct((8, 4, qb, qb), F32)],
        grid=(8 // pairs, dilation), in_specs=[tbl, tile, tile, col(0), col(1), col(2), out, out, out],
        out_specs=[out, out, out, pl.BlockSpec((pairs, 4, qb, qb), lambda hp, r: (hp, 0, 0, 0))],
        scratch_shapes=[pltpu.VMEM((4 * pairs, qb, qb), F32), pltpu.VMEM((length, pairs * LANE), F32),
                        pltpu.VMEM((length, pairs * LANE), F32)],
        name=f"dilated_backward_{dilation}", compiler_params=_params(("arbitrary", "arbitrary")))(
            table, bc, bp, view, view, view,
            do_g.reshape(wide), lse.reshape(wide), dlt.reshape(wide))
    return dq.reshape(rows, D_MODEL), dk.reshape(rows, D_MODEL), dv.reshape(rows, D_MODEL), db


def _head_sums(x, lo):
    s0 = jnp.sum(jnp.where(lo, x, 0.0), axis=1, keepdims=True)
    s1 = jnp.sum(jnp.where(lo, 0.0, x), axis=1, keepdims=True)
    return jnp.where(lo, s0, s1)


def _dil_merge_forward(outs, lses):
    rows = outs[0].shape[0]

    def body(o0, o1, o2, l0, l1, l2, o_ref):
        ls = [l0[...], l1[...], l2[...]]
        m = jnp.maximum(jnp.maximum(ls[0], ls[1]), ls[2])
        es = [jnp.exp(v - m) for v in ls]
        tot = es[0] + es[1] + es[2]
        o_ref[...] = ((es[0] * o0[...] + es[1] * o1[...] + es[2] * o2[...]) / tot).astype(BF)

    blk = pl.BlockSpec((ROW_TILE, LANE), lambda i, j: (i, j))
    return pl.pallas_call(body, out_shape=jax.ShapeDtypeStruct((rows, D_MODEL), BF), grid=(rows // ROW_TILE, 8),
                          in_specs=[blk] * 6, out_specs=blk, name="dilated_merge_forward",
                          compiler_params=_params(("parallel", "parallel")))(*outs, *lses)


def _dil_merge_backward(outs, lses, do):
    rows = outs[0].shape[0]

    def body(o0, o1, o2, l0, l1, l2, do_ref, d0, d1, d2, t0, t1, t2):
        lo = lax.broadcasted_iota(jnp.int32, (1, LANE), 1) < HEAD_DIM
        ls = [l0[...], l1[...], l2[...]]
        os_ = [o0[...], o1[...], o2[...]]
        m = jnp.maximum(jnp.maximum(ls[0], ls[1]), ls[2])
        es = [jnp.exp(v - m) for v in ls]
        inv = 1.0 / (es[0] + es[1] + es[2])
        alphas = [e * inv for e in es]
        dov = do_ref[...]
        merged = alphas[0] * os_[0] + alphas[1] * os_[1] + alphas[2] * os_[2]
        dot = _head_sums(dov * merged, lo)
        for a, d_ref, t_ref in zip(alphas, (d0, d1, d2), (t0, t1, t2)):
            d_ref[...] = (a * dov).astype(BF)
            t_ref[...] = a * dot

    blk = pl.BlockSpec((ROW_TILE, LANE), lambda i, j: (i, j))
    res = pl.pallas_call(
        body, out_shape=[jax.ShapeDtypeStruct((rows, D_MODEL), BF)] * 3 + [jax.ShapeDtypeStruct((rows, D_MODEL), F32)] * 3,
        grid=(rows // ROW_TILE, 8), in_specs=[blk] * 7, out_specs=[blk] * 6, name="dilated_merge_backward",
        compiler_params=_params(("parallel", "parallel")))(*outs, *lses, do)
    return res[:3], res[3:]


def _rel_bias_grad(dbs, buckets):
    def body(db_ref, bc_ref, bp_ref, o_ref):
        g = pl.program_id(0)
        hp = pl.program_id(1)

        @pl.when((g == 0) & (hp == 0))
        def _():
            o_ref[...] = jnp.zeros_like(o_ref)

        rr = lax.broadcasted_iota(jnp.int32, (REL_BUCKETS, LANE), 0)
        cc = lax.broadcasted_iota(jnp.int32, (REL_BUCKETS, LANE), 1)
        bc = bc_ref[0]
        bp = bp_ref[0]
        acc = jnp.zeros((REL_BUCKETS, LANE), F32)
        for hh in range(2):
            col = g * HEADS + 2 * hp + hh
            d_c = db_ref[0, 0, 2 * hh]
            d_p = db_ref[0, 0, 2 * hh + 1]
            for b in range(REL_BUCKETS):
                val = (jnp.sum(jnp.where(bc == b, d_c, 0.0), keepdims=True)
                       + jnp.sum(jnp.where(bp == b, d_p, 0.0), keepdims=True))
                acc = jnp.where((rr == b) & (cc == col), val, acc)
        o_ref[...] += acc

    db_all = jnp.stack(dbs)
    bc_all = jnp.stack([b[0] for b in buckets])
    bp_all = jnp.stack([b[1] for b in buckets])
    tile = pl.BlockSpec((1, Q_BLOCK, Q_BLOCK), lambda g, hp: (g, 0, 0))
    return pl.pallas_call(
        body, out_shape=jax.ShapeDtypeStruct((REL_BUCKETS, LANE), F32), grid=(3, 8),
        in_specs=[pl.BlockSpec((1, 1, 4, Q_BLOCK, Q_BLOCK), lambda g, hp: (g, hp, 0, 0, 0)), tile, tile],
        out_specs=pl.BlockSpec((REL_BUCKETS, LANE), lambda g, hp: (0, 0)), name="rel_bias_grad",
        compiler_params=_params(("arbitrary", "arbitrary")))(db_all, bc_all, bp_all)


def _mla_forward(hn, w, tables):
    a = _matmul(hn, w["w_a"], name="mla_a")
    cq, ckv, kr = _mla_mid_forward(a, w["q_norm"], w["kv_norm"], tables)
    q_raw = _matmul(cq, w["w_uq"], name="mla_uq")
    q = _rope_heads(q_raw, tables, False, "rope_forward")
    kv = _matmul(ckv, w["w_ukv"], b_chunks=True, out_dtype=BF, name="mla_ukv")
    scale = (HEAD_DIM + MLA_ROPE) ** -0.5
    o, lse = _attn_forward(q, kv, 0, kr, None, None, scale, MLA_GROUP, "mla_attention_forward")
    y = _matmul(o, w["w_o"], name="attn_out")
    return y, dict(hn=hn, a=a, cq=cq, ckv=ckv, kr=kr, q=q, kv=kv, o=o, lse=lse)


def _mla_backward(dy, w, s, tables):
    scale = (HEAD_DIM + MLA_ROPE) ** -0.5
    g = {}
    g["w_o"] = _matmul(s["o"], dy, ta=True, out_dtype=BF, name="attn_out_dw")
    do = _matmul(dy, w["w_o"], tb=True, out_dtype=BF, name="attn_out_dx")
    dq, dkv, dkr = _attn_backward(s["q"], s["kv"], 0, s["kr"], None, None, s["o"], do, s["lse"], scale,
                                  MLA_GROUP, "mla_attention_backward")
    dq_raw = _rope_heads(dq, tables, True, "rope_backward")
    g["w_uq"] = _matmul(s["cq"], dq_raw, ta=True, out_dtype=BF, name="mla_uq_dw")
    dcq = _matmul(dq_raw, w["w_uq"], tb=True, name="mla_uq_dx")
    g["w_ukv"] = _matmul(s["ckv"], dkv, ta=True, out_chunks=True, out_dtype=BF, name="mla_ukv_dw")
    dckv = _matmul(dkv, w["w_ukv"], tb=True, b_chunks=True, name="mla_ukv_dx")
    da, g["q_norm"], g["kv_norm"] = _mla_mid_backward(s["a"], w["q_norm"], w["kv_norm"], tables, dcq, dckv, dkr)
    g["w_a"] = _matmul(s["hn"], da, ta=True, out_dtype=BF, name="mla_a_dw")
    dhn = _matmul(da, w["w_a"], tb=True, name="mla_a_dx")
    return dhn, g


def _fox_forward(hn, w):
    qkv = _matmul(hn, w["w_qkv"], out_dtype=BF, name="fox_qkv")
    f_raw = _matmul(hn, w["w_f"], name="fox_f")
    cum = _forget_forward(f_raw, w["b_f"])
    cum_heads = cum[:, :HEADS].T
    cum_col, cum_row = cum_heads[:, :, None], cum_heads[:, None, :]
    o, lse = _attn_forward(qkv, qkv, HEADS, None, cum_col, cum_row, HEAD_DIM ** -0.5, FOX_GROUP,
                           "fox_attention_forward")
    y = _matmul(o, w["w_o"], name="attn_out")
    return y, dict(hn=hn, qkv=qkv, f_raw=f_raw, cum_col=cum_col, cum_row=cum_row, o=o, lse=lse)


def _fox_backward(dy, w, s):
    g = {}
    g["w_o"] = _matmul(s["o"], dy, ta=True, out_dtype=BF, name="attn_out_dw")
    do = _matmul(dy, w["w_o"], tb=True, out_dtype=BF, name="attn_out_dx")
    dq, dkv, dck, dcq = _attn_backward(s["qkv"], s["qkv"], HEADS, None, s["cum_col"], s["cum_row"], s["o"], do,
                                       s["lse"], HEAD_DIM ** -0.5, FOX_GROUP, "fox_attention_backward")
    dcum = jnp.pad((dck[:, 0, :] + dcq[:, :, 0]).T, ((0, 0), (0, LANE - HEADS)))
    df, g["b_f"] = _forget_backward(s["f_raw"], w["b_f"], dcum)
    dqkv = jnp.concatenate([dq, dkv], axis=1)
    g["w_qkv"] = _matmul(s["hn"], dqkv, ta=True, out_dtype=BF, name="fox_qkv_dw")
    g["w_f"] = _matmul(s["hn"], df, ta=True, out_dtype=BF, name="fox_f_dw")
    dhn = _matmul(dqkv, w["w_qkv"], tb=True, name="fox_qkv_dx")
    dhn = _matmul(df, w["w_f"], tb=True, add=dhn, name="fox_f_dx")
    return dhn, g


def _dil_mixer_forward(hn, w, buckets):
    qkv = _matmul(hn, w["w_qkv"], b_chunks=True, out_dtype=BF, name="dil_qkv")
    views = [_dil_view(qkv, grp, dilation) for grp, (_, dilation) in enumerate(DIL_PATTERNS)]
    outs, lses = [], []
    for grp, (_, dilation) in enumerate(DIL_PATTERNS):
        o_g, lse_g = _dil_forward(views[grp], grp, dilation, w["rel_bias"], buckets[grp])
        outs.append(o_g)
        lses.append(lse_g)
    o = _dil_merge_forward(outs, lses)
    y = _matmul(o, w["w_o"], name="dil_out")
    return y, dict(hn=hn, views=views, outs=outs, lses=lses, o=o)


def _dil_mixer_backward(dy, w, s, buckets):
    g = {}
    g["w_o"] = _matmul(s["o"], dy, ta=True, out_dtype=BF, name="dil_out_dw")
    do = _matmul(dy, w["w_o"], tb=True, name="dil_out_dx")
    do_gs, dlts = _dil_merge_backward(s["outs"], s["lses"], do)
    parts, dbs = [], []
    for grp, (_, dilation) in enumerate(DIL_PATTERNS):
        dq, dk, dv, db = _dil_backward(s["views"][grp], grp, dilation, w["rel_bias"], buckets[grp], do_gs[grp],
                                       s["lses"][grp], dlts[grp])
        parts += [dq, dk, dv]
        dbs.append(db)
    dqkv = jnp.concatenate(parts, axis=1)
    g["rel_bias"] = _rel_bias_grad(dbs, buckets)
    g["w_qkv"] = _matmul(s["hn"], dqkv, ta=True, out_chunks=True, out_dtype=BF, name="dil_qkv_dw")
    dhn = _matmul(dqkv, w["w_qkv"], tb=True, b_chunks=True, name="dil_qkv_dx")
    return dhn, g


def _mixer_weights(i, lw, small):
    mixer, j = i % N_MIXERS, i // N_MIXERS
    if mixer == 0:
        return dict(lw["mixer"], q_norm=small["mla_q_norm"][j][None, :], kv_norm=small["mla_kv_norm"][j][None, :])
    if mixer == 1:
        return dict(lw["mixer"], rel_bias=small["rel_bias"])
    return dict(lw["mixer"], b_f=jnp.pad(small["fox_b_f"][j][None, :], ((0, 0), (0, LANE - HEADS))))


MIXER_PART, COMMON_PART = 0, 1


def _run_layers(x, p, positions, target, get_part, get_small, put_part):
    tables = _rope_tables(positions)
    buckets = [_dil_buckets(d) for _, d in DIL_PATTERNS]
    layers, saved = [], []
    h = x
    first = get_part(0, MIXER_PART, positions)
    small = get_small()

    def gain(i, k):
        return small["norm_g"][i, k][None, :]

    hn = _prenorm(h, gain(0, 0))
    sq = dh = None
    for i in range(DEPTH):
        mixer = i % N_MIXERS
        lw = dict(first if i == 0 else get_part(i, MIXER_PART, h))
        mw = _mixer_weights(i, lw, small)
        if mixer == 0:
            y, ms = _mla_forward(hn, mw, tables)
        elif mixer == 1:
            y, ms = _dil_mixer_forward(hn, mw, buckets)
        else:
            y, ms = _fox_forward(hn, mw)
        if "ffn_w_in" not in lw:
            lw.update(get_part(i, COMMON_PART, y))
        layers.append(lw)
        h1, hn2 = _post_residual(h, y, gain(i, 1), gain(i, 2))
        gu = _matmul(hn2, lw["ffn_w_in"], b_chunks=True, out_dtype=BF, name="ffn_in")
        act = _swiglu_forward(gu)
        f = _matmul(act, lw["ffn_w_out"], name="ffn_out")
        h2, h2b = _post_residual(h1, f, gain(i, 3), None)
        pp = _matmul(p[i], lw["ple_w_proj"], b_chunks=True, name="ple_proj")
        z = _matmul(h2b, lw["ple_w_gate"], name="ple_gate")
        saved.append(dict(h=h, y=y, ms=ms, h1=h1, hn2=hn2, gu=gu, act=act, f=f, h2b=h2b, pp=pp, z=z))
        if i + 1 < DEPTH:
            h, hn = _ple_forward(h2, pp, z, gain(i + 1, 0))
        else:
            dh, sq = _ple_loss(h2, pp, z, target)

    norm_rows = [[None] * 4 for _ in range(DEPTH)]
    sg = dict(mla_q_norm={}, mla_kv_norm={}, rel_bias=None, fox_b_f={})
    for i in reversed(range(DEPTH)):
        s, lw = saved[i], layers[i]
        mixer, j = i % N_MIXERS, i // N_MIXERS
        mw = _mixer_weights(i, lw, small)
        lg = {}
        dpp, dz = _ple_backward(dh, s["pp"], s["z"])
        lg["ple_w_proj"] = _matmul(p[i], dpp, ta=True, out_chunks=True, out_dtype=BF, name="ple_proj_dw")
        lg["ple_w_gate"] = _matmul(s["h2b"], dz, ta=True, out_dtype=BF, name="ple_gate_dw")
        dh2 = _matmul(dz, lw["ple_w_gate"], tb=True, add=dh, name="ple_gate_dx")
        df, norm_rows[i][3] = _rms_backward(s["f"], gain(i, 3), dh2, None, BF)
        lg["ffn_w_out"] = _matmul(s["act"], df, ta=True, out_dtype=BF, name="ffn_out_dw")
        dact = _matmul(df, lw["ffn_w_out"], tb=True, out_dtype=BF, name="ffn_out_dx")
        dgu = _swiglu_backward(s["gu"], dact)
        lg["ffn_w_in"] = _matmul(s["hn2"], dgu, ta=True, out_chunks=True, out_dtype=BF, name="ffn_in_dw")
        split = i in SPLIT_LAYERS
        zero = put_part(i, COMMON_PART, lg)[0:1, 0:1] if split else 0.0
        dhn2 = _matmul(dgu, lw["ffn_w_in"], tb=True, b_chunks=True, name="ffn_in_dx")
        dh1, norm_rows[i][2] = _rms_backward(s["h1"], gain(i, 2), dhn2, dh2, F32)
        dy, norm_rows[i][1] = _rms_backward(s["y"], gain(i, 1) + zero, dh1, None, BF)
        if mixer == 0:
            dhn, mg = _mla_backward(dy, mw, s["ms"], tables)
            sg["mla_q_norm"][j] = mg.pop("q_norm")
            sg["mla_kv_norm"][j] = mg.pop("kv_norm")
        elif mixer == 1:
            dhn, mg = _dil_mixer_backward(dy, mw, s["ms"], buckets)
            rel = mg.pop("rel_bias")[:, :3 * HEADS]
            sg["rel_bias"] = rel if sg["rel_bias"] is None else sg["rel_bias"] + rel
        else:
            dhn, mg = _fox_backward(dy, mw, s["ms"])
            sg["fox_b_f"][j] = mg.pop("b_f")[:, :HEADS]
        token = put_part(i, MIXER_PART, dict(mixer=mg) if split else dict(lg, mixer=mg))
        dh, norm_rows[i][0] = _rms_backward(s["h"], gain(i, 0) + token[0:1, 0:1], dhn, dh1, F32)
    small_grads = dict(norm_g=jnp.stack([jnp.concatenate(row, axis=0) for row in norm_rows]),
                       rel_bias=sg["rel_bias"])
    for k in ("mla_q_norm", "mla_kv_norm", "fox_b_f"):
        small_grads[k] = jnp.concatenate([sg[k][j] for j in sorted(sg[k])], axis=0)
    return sq, dh, small_grads


BIG = ("ffn_w_in", "ffn_w_out", "ple_w_proj", "ple_w_gate", "mla_w_a", "mla_w_uq", "mla_w_ukv", "mla_w_o",
       "dil_w_qkv", "dil_w_o", "fox_w_qkvf", "fox_w_o")
SMALL_SHARDED = ("norm_g", "mla_q_norm", "mla_kv_norm")
SMALL_REPLICATED = ("rel_bias", "fox_b_f")
WEIGHTS = ("norm_g", "ffn_w_in", "ffn_w_out", "ple_w_proj", "ple_w_gate", "rel_bias", "mla_w_a", "mla_q_norm",
           "mla_kv_norm", "mla_w_uq", "mla_w_ukv", "mla_w_o", "dil_w_qkv", "dil_w_o", "fox_w_qkvf", "fox_b_f", "fox_w_o")


TRANSPOSED = "fox_w_qkvf"
SPLIT_LAYERS = (0, 1, 2, 3)
LAYER_COMMON = ("ffn_w_in", "ffn_w_out", "ple_w_proj", "ple_w_gate")
MIXER_WEIGHTS = (("mla_w_a", "mla_w_uq", "mla_w_ukv", "mla_w_o"), ("dil_w_qkv", "dil_w_o"), ("fox_w_qkvf", "fox_w_o"))


def _part_names(i, part):
    if i in SPLIT_LAYERS:
        return MIXER_WEIGHTS[i % N_MIXERS] if part == MIXER_PART else LAYER_COMMON
    return MIXER_WEIGHTS[i % N_MIXERS] + LAYER_COMMON if part == MIXER_PART else ()


def _layer_slot(name, i):
    return i if name in LAYER_COMMON else i // N_MIXERS


def _merge_rows(chunks):
    n, r, c = chunks.shape
    return chunks.reshape(n * r, c)


def _merge_cols(chunks):
    n, r, c = chunks.shape
    return chunks.transpose(1, 0, 2).reshape(r, n * c)


def _pad_heads_out(wo):
    w3 = wo.reshape(HEADS, HEAD_DIM, D_MODEL)
    return jnp.pad(w3, ((0, 0), (HEAD_DIM, 0), (0, 0))).reshape(HEADS * LANE, D_MODEL)


def _part_to_compute(i, part, ch):
    lw = {}
    if "ffn_w_in" in ch:
        lw.update(ffn_w_in=ch["ffn_w_in"], ffn_w_out=_merge_rows(ch["ffn_w_out"]), ple_w_proj=ch["ple_w_proj"],
                  ple_w_gate=_merge_rows(ch["ple_w_gate"]))
    if part == COMMON_PART:
        return lw
    mixer = i % N_MIXERS
    if mixer == 0:
        wa = _merge_rows(ch["mla_w_a"])
        rank = MLA_Q_RANK + MLA_KV_RANK
        wa_p = jnp.concatenate([wa[:, :rank], jnp.zeros((wa.shape[0], 64), wa.dtype), wa[:, rank:],
                                jnp.zeros((wa.shape[0], 32), wa.dtype)], axis=1)
        wuq = _merge_cols(ch["mla_w_uq"]).reshape(MLA_Q_RANK, HEADS, HEAD_DIM + MLA_ROPE)
        wuq_p = jnp.pad(wuq, ((0, 0), (0, 0), (0, LANE - HEAD_DIM - MLA_ROPE))).reshape(MLA_Q_RANK, HEADS * LANE)
        lw["mixer"] = dict(w_a=wa_p, w_uq=wuq_p, w_ukv=ch["mla_w_ukv"], w_o=_pad_heads_out(_merge_rows(ch["mla_w_o"])))
    elif mixer == 1:
        lw["mixer"] = dict(w_qkv=ch["dil_w_qkv"], w_o=_merge_rows(ch["dil_w_o"]))
    else:
        wf = _merge_rows(ch["fox_w_qkvf"]).T
        inner = HEADS * HEAD_DIM
        q3 = wf[:, :inner].reshape(D_MODEL, HEADS, HEAD_DIM)
        k3 = wf[:, inner:2 * inner].reshape(D_MODEL, HEADS, HEAD_DIM)
        v3 = wf[:, 2 * inner:3 * inner].reshape(D_MODEL, HEADS, HEAD_DIM)
        q_p = jnp.pad(q3, ((0, 0), (0, 0), (0, HEAD_DIM))).reshape(D_MODEL, HEADS * LANE)
        kv_p = jnp.concatenate([k3, v3], axis=2).reshape(D_MODEL, HEADS * LANE)
        f_p = jnp.pad(wf[:, 3 * inner:], ((0, 0), (0, LANE - HEADS)))
        lw["mixer"] = dict(w_qkv=jnp.concatenate([q_p, kv_p], axis=1), w_f=f_p,
                           w_o=_pad_heads_out(_merge_rows(ch["fox_w_o"])))
    return lw


def _part_contributions(i, part, lg, chunk_shapes):
    spec = {k: jax.ShapeDtypeStruct(s, BF) for k, s in chunk_shapes.items()}
    (contrib,) = jax.linear_transpose(functools.partial(_part_to_compute, i, part), spec)(lg)
    return contrib


def _chip_peers():
    x, y, c = lax.axis_index("x"), lax.axis_index("y"), lax.axis_index("c")
    peers = [(1 - x, y), (x, 1 - y), (1 - x, 1 - y)]
    return x, y, c, peers


SEM_SPEC = pl.BlockSpec(memory_space=pltpu.SEMAPHORE)
ANY_SPEC = pl.BlockSpec(memory_space=pl.ANY)
SPLIT_EFFECT = pltpu.SideEffectType.DATAFLOW_SIDE_EFFECTING


def _own_slot(shard):
    me = 2 * lax.axis_index("x") + lax.axis_index("y")
    return lax.dynamic_update_index_in_dim(lax.empty((N_CHIPS,) + shard.shape, shard.dtype), shard[None], me, 0)


def _spread_copy(src, land, k, peer, c, send_sems, recv_sems, index, src_slot, slot):
    px, py = peer
    return pltpu.make_async_remote_copy(
        src_ref=src.at[src_slot], dst_ref=land.at[slot],
        send_sem=send_sems.at[3 * index + k], recv_sem=recv_sems.at[3 * index + k],
        device_id=(px, py, c), device_id_type=MESH)


def _spread_start(bufs, srcs, after, name):
    n = len(bufs)
    exchange = srcs is not None
    arrays = (list(srcs) if exchange else []) + list(bufs)
    na = len(arrays)

    def body(*refs):
        src, land = refs[:n], refs[na - n:na]
        send_sems, recv_sems = refs[na + 1], refs[na + 2]
        token = refs[-1]
        x, y, c, peers = _chip_peers()
        me = 2 * x + y
        for w in range(n):
            for k, peer in enumerate(peers):
                src_slot = 2 * peer[0] + peer[1] if exchange else me
                _spread_copy(src[w], land[w], k, peer, c, send_sems, recv_sems, w, src_slot, me).start()
        token[...] = jnp.zeros_like(token)

    hbm = [pltpu.with_memory_space_constraint(a, pltpu.HBM) for a in arrays]
    out = pl.pallas_call(
        body, name=name,
        out_shape=(pltpu.SemaphoreType.DMA((3 * n,)), pltpu.SemaphoreType.DMA((3 * n,)),
                   *[pltpu.HBM(a.shape, a.dtype) for a in hbm], jax.ShapeDtypeStruct((8, LANE), F32)),
        in_specs=[HBM_SPEC] * na + [ANY_SPEC],
        out_specs=(SEM_SPEC, SEM_SPEC, *[HBM_SPEC] * na, pl.BlockSpec(memory_space=pltpu.VMEM)),
        input_output_aliases={w: 2 + w for w in range(na)},
        compiler_params=pltpu.CompilerParams(has_side_effects=SPLIT_EFFECT))(*hbm, after)
    return dict(send=out[0], recv=out[1], arrays=out[2:2 + na], n=n, token=out[-1], exchange=exchange)


def _spread_wait(handle, after, name):
    n, exchange = handle["n"], handle["exchange"]
    arrays = list(handle["arrays"])
    na = len(arrays)

    def body(*refs):
        src, land = refs[:n], refs[na - n:na]
        send_sems, recv_sems = refs[na], refs[na + 1]
        x, y, c, peers = _chip_peers()
        me = 2 * x + y
        for w in range(n):
            for k, peer in enumerate(peers):
                there = 2 * peer[0] + peer[1]
                cp = _spread_copy(src[w], land[w], k, peer, c, send_sems, recv_sems, w, there if exchange else me, there)
                cp.wait_send()
                cp.wait_recv()

    out = pl.pallas_call(
        body, name=name, out_shape=tuple(pltpu.HBM(a.shape, a.dtype) for a in arrays),
        in_specs=[HBM_SPEC] * na + [SEM_SPEC, SEM_SPEC, ANY_SPEC], out_specs=tuple([HBM_SPEC] * na),
        input_output_aliases={w: w for w in range(na)},
        compiler_params=pltpu.CompilerParams(has_side_effects=SPLIT_EFFECT))(*arrays, handle["send"], handle["recv"], after)
    return (list(out[n:]), list(out[:n])) if exchange else list(out)


def _sibling_copy(received, sent, land, k, me, peers, sibling, send_sems, recv_sems, index):
    slot = me if k == 3 else 2 * peers[k][0] + peers[k][1]
    src = sent if k == 3 else received
    return pltpu.make_async_remote_copy(
        src_ref=src.at[slot], dst_ref=land.at[slot], send_sem=send_sems.at[4 * index + k],
        recv_sem=recv_sems.at[4 * index + k], device_id=sibling, device_id_type=MESH)


def _sibling_start(received, sent, after, name):
    n = len(received)
    lands = [lax.empty(a.shape, a.dtype) for a in received]
    arrays = list(received) + list(sent) + lands

    def body(*refs):
        rec, snt, land = refs[:n], refs[n:2 * n], refs[2 * n:3 * n]
        send_sems, recv_sems = refs[3 * n + 1], refs[3 * n + 2]
        token = refs[-1]
        x, y, c, peers = _chip_peers()
        for w in range(n):
            for k in range(4):
                _sibling_copy(rec[w], snt[w], land[w], k, 2 * x + y, peers, (x, y, 1 - c), send_sems, recv_sems, w).start()
        token[...] = jnp.zeros_like(token)

    hbm = [pltpu.with_memory_space_constraint(a, pltpu.HBM) for a in arrays]
    out = pl.pallas_call(
        body, name=name,
        out_shape=(pltpu.SemaphoreType.DMA((4 * n,)), pltpu.SemaphoreType.DMA((4 * n,)),
                   *[pltpu.HBM(a.shape, a.dtype) for a in hbm], jax.ShapeDtypeStruct((8, LANE), F32)),
        in_specs=[HBM_SPEC] * (3 * n) + [ANY_SPEC],
        out_specs=(SEM_SPEC, SEM_SPEC, *[HBM_SPEC] * (3 * n), pl.BlockSpec(memory_space=pltpu.VMEM)),
        input_output_aliases={w: 2 + w for w in range(3 * n)},
        compiler_params=pltpu.CompilerParams(has_side_effects=SPLIT_EFFECT))(*hbm, after)
    return dict(send=out[0], recv=out[1], arrays=out[2:2 + 3 * n], n=n, token=out[-1])


def _sibling_wait(handle, after, name):
    n = handle["n"]
    arrays = list(handle["arrays"])

    def body(*refs):
        rec, snt, land = refs[:n], refs[n:2 * n], refs[2 * n:3 * n]
        send_sems, recv_sems = refs[3 * n], refs[3 * n + 1]
        x, y, c, peers = _chip_peers()
        for w in range(n):
            for k in range(4):
                cp = _sibling_copy(rec[w], snt[w], land[w], k, 2 * x + y, peers, (x, y, 1 - c), send_sems, recv_sems, w)
                cp.wait_send()
                cp.wait_recv()

    out = pl.pallas_call(
        body, name=name, out_shape=tuple(pltpu.HBM(a.shape, a.dtype) for a in arrays),
        in_specs=[HBM_SPEC] * (3 * n) + [SEM_SPEC, SEM_SPEC, ANY_SPEC], out_specs=tuple([HBM_SPEC] * (3 * n)),
        input_output_aliases={w: w for w in range(3 * n)},
        compiler_params=pltpu.CompilerParams(has_side_effects=SPLIT_EFFECT))(*arrays, handle["send"], handle["recv"], after)
    return list(out[:n]), list(out[n:2 * n]), list(out[2 * n:])


def _all_reduce_small(v):
    rows = v.shape[0]

    def body(v_ref, sum_ref, slots, send_sems, recv_sems):
        x, y, c = lax.axis_index("x"), lax.axis_index("y"), lax.axis_index("c")
        me = 4 * x + 2 * y + c
        slots[me] = v_ref[...]
        sends = []
        for k in range(1, N_DEV):
            bx, by, bc = (k >> 2) & 1, (k >> 1) & 1, k & 1
            peer = (x ^ bx, y ^ by, c ^ bc)
            rc = pltpu.make_async_remote_copy(src_ref=v_ref, dst_ref=slots.at[me], send_sem=send_sems.at[k],
                                              recv_sem=recv_sems.at[k], device_id=peer, device_id_type=MESH)
            rc.start()
            sends.append(rc)
        for k in range(1, N_DEV):
            bx, by, bc = (k >> 2) & 1, (k >> 1) & 1, k & 1
            src = 4 * (x ^ bx) + 2 * (y ^ by) + (c ^ bc)
            pltpu.make_async_remote_copy(src_ref=v_ref, dst_ref=slots.at[src], send_sem=send_sems.at[k],
                                         recv_sem=recv_sems.at[k], device_id=(x ^ bx, y ^ by, c ^ bc),
                                         device_id_type=MESH).wait_recv()
        for rc in sends:
            rc.wait_send()
        total = slots[0]
        for k in range(1, N_DEV):
            total = total + slots[k]
        sum_ref[...] = total

    vm = pl.BlockSpec(memory_space=pltpu.VMEM)
    return pl.pallas_call(
        body, out_shape=jax.ShapeDtypeStruct((rows, LANE), F32), in_specs=[vm], out_specs=vm,
        scratch_shapes=[pltpu.VMEM((N_DEV, rows, LANE), F32), pltpu.SemaphoreType.DMA((N_DEV,)),
                        pltpu.SemaphoreType.DMA((N_DEV,))], name="all_reduce_small")(v)


def _as_2d(a):
    return a.reshape(-1, a.shape[-1])


def _row_tile(rows, cols):
    for t in (512, 256, 128, 64, 32, 16):
        if rows % t == 0 and t * cols * 4 <= (1 << 20):
            return t
    return rows


def _adamw_weight(w, m, v, received, sent, sibling):
    layers = len(received)
    _, rows, cols = received[0].shape
    tr = _row_tile(rows, cols)
    by_columns = rows % tr != 0 or tr == rows and rows * cols * 4 > (2 << 20)
    if by_columns:
        assert layers == 1 and cols % (2 * LANE) == 0, (w.shape, received[0].shape)
        tr, tc, steps = rows, cols // 2, 2
        index = lambda i: (0, i)
    else:
        tc, steps = cols, rows // tr
        index = lambda i: (i, 0)
    where = (2 * lax.axis_index("x") + lax.axis_index("y")).astype(jnp.int32).reshape(1)

    def body(where_ref, w_ref, m_ref, v_ref, *rest):
        per_layer, (g_ref, d_ref, nm_ref, nv_ref) = rest[:3 * layers], rest[3 * layers:]
        me = where_ref[0]
        for layer in range(layers):
            r_ref, own_ref, s_ref = per_layer[3 * layer:3 * layer + 3]

            @pl.when(pl.program_id(0) == layer)
            def _():
                mine = theirs = None
                for k in range(N_CHIPS):
                    a = jnp.where(me == k, own_ref[...], r_ref[k]).astype(F32)
                    b = s_ref[k].astype(F32)
                    mine = a if mine is None else mine + a
                    theirs = b if theirs is None else theirs + b
                g = mine + theirs
                delta, nm, nv = _adamw_math(w_ref[...], g, m_ref[...], v_ref[...])
                g_ref[...] = g
                d_ref[...] = delta
                nm_ref[...] = nm
                nv_ref[...] = nv

    def held(layer, now, i):
        return jnp.where(now < layer, 0, jnp.where(now > layer, steps - 1, i))

    if by_columns:
        stacked = pl.BlockSpec((tr, tc), lambda now, i, where_ref: index(i))
    else:
        stacked = pl.BlockSpec((tr, tc), lambda now, i, where_ref: (now * steps + i, 0))
    in_specs = [stacked, stacked, stacked]
    args = [where, w, m, v]
    for layer in range(layers):
        four = pl.BlockSpec((N_CHIPS, tr, tc), lambda now, i, where_ref, layer=layer: (0,) + index(held(layer, now, i)))
        own = pl.BlockSpec((None, tr, tc),
                           lambda now, i, where_ref, layer=layer: (where_ref[0],) + index(held(layer, now, i)))
        in_specs += [four, own, four]
        args += [received[layer], sent[layer], sibling[layer]]
    grid_spec = pltpu.PrefetchScalarGridSpec(num_scalar_prefetch=1, grid=(layers, steps), in_specs=in_specs,
                                             out_specs=[stacked] * 4)
    return pl.pallas_call(body, out_shape=[jax.ShapeDtypeStruct(w.shape, F32)] * 4, grid_spec=grid_spec,
                          name="adamw_weight", compiler_params=_params(("arbitrary", "arbitrary")))(*args)


def _adamw_math(w, g, m, v):
    m = ADAM_B1 * m + (1.0 - ADAM_B1) * g
    v = ADAM_B2 * v + (1.0 - ADAM_B2) * (g * g)
    m_hat = m * (1.0 / (1.0 - ADAM_B1 ** ADAM_STEP))
    v_hat = v * (1.0 / (1.0 - ADAM_B2 ** ADAM_STEP))
    denom = jnp.sqrt(v_hat) + ADAM_EPS
    inv = pl.reciprocal(denom, approx=True)
    inv = inv * (2.0 - denom * inv)
    delta = -ADAM_LR * (m_hat * inv + ADAM_WD * w)
    return delta, m, v


def _adamw(w, m, v, g_mine, g_sibling):
    rows, cols = w.shape
    tr = _row_tile(rows, cols)
    two = g_sibling is not None

    def body(*refs):
        if two:
            w_ref, m_ref, v_ref, ga_ref, gb_ref, g_ref, d_ref, nm_ref, nv_ref = refs
            g = ga_ref[...] + gb_ref[...]
        else:
            w_ref, m_ref, v_ref, ga_ref, g_ref, d_ref, nm_ref, nv_ref = refs
            g = ga_ref[...]
        delta, nm, nv = _adamw_math(w_ref[...], g, m_ref[...], v_ref[...])
        g_ref[...] = g
        d_ref[...] = delta
        nm_ref[...] = nm
        nv_ref[...] = nv

    blk = pl.BlockSpec((tr, cols), lambda i: (i, 0))
    args = [w, m, v, g_mine] + ([g_sibling] if two else [])
    return pl.pallas_call(body, out_shape=[jax.ShapeDtypeStruct((rows, cols), F32)] * 4, grid=(rows // tr,),
                          in_specs=[blk] * len(args), out_specs=[blk] * 4, name="adamw",
                          compiler_params=_params(("parallel",)))(*args)


def _pack_rows(arrays):
    flat = jnp.concatenate([a.reshape(-1) for a in arrays])
    rows = -(-flat.shape[0] // (8 * LANE)) * 8
    return jnp.pad(flat, (0, rows * LANE - flat.shape[0])).reshape(rows, LANE)


def _unpack_rows(packed, shapes):
    flat = packed.reshape(-1)
    out, at = [], 0
    for s in shapes:
        size = math.prod(s)
        out.append(flat[at:at + size].reshape(s))
        at += size
    return out


def kernel(x, p, positions, norm_g, ffn_w_in, ffn_w_out, ple_w_proj, ple_w_gate, rel_bias, mla_w_a, mla_q_norm, mla_kv_norm, mla_w_uq, mla_w_ukv, mla_w_o, dil_w_qkv, dil_w_o, fox_w_qkvf, fox_b_f, fox_w_o, loss_target, m_norm_g, m_ffn_w_in, m_ffn_w_out, m_ple_w_proj, m_ple_w_gate, m_rel_bias, m_mla_w_a, m_mla_q_norm, m_mla_kv_norm, m_mla_w_uq, m_mla_w_ukv, m_mla_w_o, m_dil_w_qkv, m_dil_w_o, m_fox_w_qkvf, m_fox_b_f, m_fox_w_o, v_norm_g, v_ffn_w_in, v_ffn_w_out, v_ple_w_proj, v_ple_w_gate, v_rel_bias, v_mla_w_a, v_mla_q_norm, v_mla_kv_norm, v_mla_w_uq, v_mla_w_ukv, v_mla_w_o, v_dil_w_qkv, v_dil_w_o, v_fox_w_qkvf, v_fox_b_f, v_fox_w_o):
    w = dict(norm_g=norm_g, ffn_w_in=ffn_w_in, ffn_w_out=ffn_w_out, ple_w_proj=ple_w_proj, ple_w_gate=ple_w_gate,
             rel_bias=rel_bias, mla_w_a=mla_w_a, mla_q_norm=mla_q_norm, mla_kv_norm=mla_kv_norm, mla_w_uq=mla_w_uq,
             mla_w_ukv=mla_w_ukv, mla_w_o=mla_w_o, dil_w_qkv=dil_w_qkv, dil_w_o=dil_w_o, fox_w_qkvf=fox_w_qkvf,
             fox_b_f=fox_b_f, fox_w_o=fox_w_o)
    m = dict(norm_g=m_norm_g, ffn_w_in=m_ffn_w_in, ffn_w_out=m_ffn_w_out, ple_w_proj=m_ple_w_proj,
             ple_w_gate=m_ple_w_gate, rel_bias=m_rel_bias, mla_w_a=m_mla_w_a, mla_q_norm=m_mla_q_norm,
             mla_kv_norm=m_mla_kv_norm, mla_w_uq=m_mla_w_uq, mla_w_ukv=m_mla_w_ukv, mla_w_o=m_mla_w_o,
             dil_w_qkv=m_dil_w_qkv, dil_w_o=m_dil_w_o, fox_w_qkvf=m_fox_w_qkvf, fox_b_f=m_fox_b_f, fox_w_o=m_fox_w_o)
    v = dict(norm_g=v_norm_g, ffn_w_in=v_ffn_w_in, ffn_w_out=v_ffn_w_out, ple_w_proj=v_ple_w_proj,
             ple_w_gate=v_ple_w_gate, rel_bias=v_rel_bias, mla_w_a=v_mla_w_a, mla_q_norm=v_mla_q_norm,
             mla_kv_norm=v_mla_kv_norm, mla_w_uq=v_mla_w_uq, mla_w_ukv=v_mla_w_ukv, mla_w_o=v_mla_w_o,
             dil_w_qkv=v_dil_w_qkv, dil_w_o=v_dil_w_o, fox_w_qkvf=v_fox_w_qkvf, fox_b_f=v_fox_b_f, fox_w_o=v_fox_w_o)
    chip = 2 * lax.axis_index("x") + lax.axis_index("y")
    for tree in (w, m, v):
        tree[TRANSPOSED] = jnp.swapaxes(tree[TRANSPOSED], 1, 2)

    small_shapes = [w[k].shape for k in SMALL_SHARDED]
    order = [(i, part) for i in range(DEPTH) for part in (MIXER_PART, COMMON_PART) if _part_names(i, part)]
    gathers = {}
    after = positions
    zero = 0.0
    for i, part in order:
        bufs = [_own_slot((w[k][_layer_slot(k, i)] + zero).astype(BF)) for k in _part_names(i, part)]
        if (i, part) == order[0]:
            bufs.append(_own_slot(_pack_rows([w[k] for k in SMALL_SHARDED])))
        gathers[i, part] = _spread_start(bufs, None, after, f"gather_start_{i}_{part}")
        after = gathers[i, part]["token"]
        if (i, part) == order[0]:
            zero = after[0, 0]
    all_started = after
    state = {}

    def get_part(i, part, after_array):
        is_first = (i, part) == order[0]
        lands = _spread_wait(gathers[i, part], all_started if is_first else after_array, f"gather_wait_{i}_{part}")
        if is_first:
            pieces = [_unpack_rows(lands[-1][k], small_shapes) for k in range(N_CHIPS)]
            small = {name: jnp.concatenate([pieces[k][idx] for k in range(N_CHIPS)], axis=-1)
                     for idx, name in enumerate(SMALL_SHARDED)}
            state["small"] = dict(small, rel_bias=rel_bias, fox_b_f=fox_b_f)
        chunks = dict(zip(_part_names(i, part), lands))
        state[i, part] = {k: a.shape for k, a in chunks.items()}
        return _part_to_compute(i, part, chunks)

    started, forwards = [], {}

    def forward_oldest(after_array):
        i, part, handle = started.pop(0)
        received, sent = _spread_wait(handle, after_array, f"exchange_wait_{i}_{part}")
        forwards[i, part] = _sibling_start(received, sent, after_array, f"sibling_start_{i}_{part}")
        return forwards[i, part]["token"]

    def put_part(i, part, lg):
        contrib = _part_contributions(i, part, lg, state[i, part])
        srcs = [contrib[k] for k in _part_names(i, part)]
        handle = _spread_start([lax.empty(s.shape, s.dtype) for s in srcs], srcs, positions,
                               f"exchange_start_{i}_{part}")
        token = handle["token"]
        if started:
            token = token + forward_oldest(token)
        started.append((i, part, handle))
        return token

    sq, grad_x, sg = _run_layers(x[0], p[:, 0], positions[0], loss_target[0], get_part, lambda: state["small"],
                                 put_part)
    loss = lax.psum(0.5 / D_MODEL * jnp.sum(sq), ("x", "y", "c"))
    forward_oldest(grad_x)

    held = {k: {} for k in BIG}
    for i, part in sorted(forwards, reverse=True):
        received, sent, sibling = _sibling_wait(forwards[i, part], grad_x, f"sibling_wait_{i}_{part}")
        for k, r, s, t in zip(_part_names(i, part), received, sent, sibling):
            held[k][_layer_slot(k, i)] = (r, s, t)
    results = {}
    for k in BIG:
        per_layer = [held[k][slot] for slot in sorted(held[k])]
        outs = _adamw_weight(_as_2d(w[k]), _as_2d(m[k]), _as_2d(v[k]), *[list(col) for col in zip(*per_layer)])
        results[k] = [o.reshape(w[k].shape) for o in outs]
    results[TRANSPOSED] = [jnp.swapaxes(o, 1, 2) for o in results[TRANSPOSED]]

    small_all = SMALL_SHARDED + SMALL_REPLICATED
    full_shapes = [sg[k].shape for k in small_all]
    reduced = dict(zip(small_all, _unpack_rows(_all_reduce_small(_pack_rows([sg[k] for k in small_all])), full_shapes)))
    local_g = []
    for k in small_all:
        g = reduced[k]
        if k in SMALL_SHARDED:
            width = w[k].shape[-1]
            g = lax.dynamic_slice_in_dim(g, chip * width, width, axis=g.ndim - 1)
        local_g.append(g)
    local_shapes = [w[k].shape for k in small_all]
    outs = _adamw(_pack_rows([w[k] for k in small_all]), _pack_rows([m[k] for k in small_all]),
                  _pack_rows([v[k] for k in small_all]), _pack_rows(local_g), None)
    unpacked = [_unpack_rows(o, local_shapes) for o in outs]
    for idx, k in enumerate(small_all):
        results[k] = [u[idx] for u in unpacked]

    return (loss, grad_x[None], *[results[k][0] for k in WEIGHTS], *[results[k][1] for k in WEIGHTS],
            *[results[k][2] for k in WEIGHTS], *[results[k][3] for k in WEIGHTS])
```

```python
import functools
import math

import jax
import jax.numpy as jnp
from jax import lax
from jax.experimental import pallas as pl
from jax.experimental.pallas import tpu as pltpu

F32 = jnp.float32
BF = jnp.bfloat16
MESH = pl.DeviceIdType.MESH
HBM_SPEC = pl.BlockSpec(memory_space=pltpu.HBM)

D_MODEL = 1024
DEPTH = 4
N_MIXERS = 3
D_FF = 2816
NORM_EPS = 1e-6
NEG_INF = -1e30
LANE = 128
HEADS = 16
HEAD_DIM = 64
MLA_Q_RANK = 384
MLA_KV_RANK = 256
MLA_ROPE = 32
MLA_A_PAD = 768
ROPE_THETA = 10000.0
DIL_PATTERNS = ((128, 1), (512, 4), (2048, 16))
Q_BLOCK = 128
DIL_PAIRS = {1: 2, 4: 4, 16: 4}
REL_BUCKETS = 32
REL_MAX_DIST = 2048
N_CHIPS = 4
N_DEV = 8

ADAM_LR = 0.001
ADAM_B1 = 0.9
ADAM_B2 = 0.999
ADAM_EPS = 1e-08
ADAM_WD = 0.01
ADAM_STEP = 10

VMEM_LIMIT = 56 * 1024 * 1024
MATMUL_VMEM_BUDGET = 36 * 1024 * 1024
ROW_TILE = 512
ATTN_TILE = 256
ATTN_Q_TILE = 512
ATTN_FORWARD_KEY_TILE = 512
MLA_GROUP = 4
FOX_GROUP = 2


def _params(sem=None):
    return pltpu.CompilerParams(dimension_semantics=sem, vmem_limit_bytes=VMEM_LIMIT)


def _divisor_tiles(dim):
    tiles = [t for t in range(LANE, dim + 1, LANE) if dim % t == 0]
    return tiles or [dim]


def _matmul_tiles(m, n, k, a_bytes, b_bytes, out_bytes, has_add, n_unit=None, k_unit=None):
    best = None
    for tm in _divisor_tiles(m):
        for tn in _divisor_tiles(n_unit or n):
            for tk in _divisor_tiles(k_unit or k):
                if max(tm, tn, tk) > 2048:
                    continue
                vmem = 2 * (tm * tk * a_bytes + tk * tn * b_bytes + tm * tn * out_bytes) + tm * tn * 4
                if has_add:
                    vmem += 2 * tm * tn * 4
                if vmem > MATMUL_VMEM_BUDGET:
                    continue
                steps = (m // tm) * (n // tn) * (k // tk)
                traffic = m * k * a_bytes * (n // tn) + k * n * b_bytes * (m // tm) + m * n * out_bytes
                cost = traffic / 3.0e12 + steps * 0.4e-6
                if best is None or cost < best[0]:
                    best = (cost, tm, tn, tk)
    return best[1:]


def _matmul(a, b, *, ta=False, tb=False, b_chunks=False, out_chunks=False, add=None, out_dtype=F32, name):
    k, m = a.shape if ta else a.shape[::-1]
    n_unit = k_unit = None
    if b_chunks:
        chunks, rows_w, c = b.shape
        if tb:
            kb, n, k_unit = chunks * c, rows_w, c
        else:
            kb, n, n_unit = rows_w, chunks * c, c
    else:
        kb, n = b.shape[::-1] if tb else b.shape
    if out_chunks:
        assert n % N_CHIPS == 0 and add is None
        n_unit = n // N_CHIPS
    assert k == kb, (a.shape, b.shape, ta, tb)
    tm, tn, tk = _matmul_tiles(m, n, k, a.dtype.itemsize, b.dtype.itemsize, jnp.dtype(out_dtype).itemsize,
                               add is not None, n_unit, k_unit)
    nk = k // tk
    dims = (((0 if ta else 1,), (1 if tb else 0,)), ((), ()))

    def body(*refs):
        if add is None:
            a_ref, b_ref, o_ref, acc_ref = refs
            add_ref = None
        else:
            a_ref, b_ref, add_ref, o_ref, acc_ref = refs
        kk = pl.program_id(2)

        @pl.when(kk == 0)
        def _():
            acc_ref[...] = jnp.zeros_like(acc_ref)

        acc_ref[...] += lax.dot_general(a_ref[...].astype(BF), b_ref[...].astype(BF), dims,
                                        preferred_element_type=F32)

        @pl.when(kk == nk - 1)
        def _():
            r = acc_ref[...]
            if add_ref is not None:
                r = r + add_ref[...].astype(F32)
            o_ref[...] = r.astype(out_dtype)

    a_spec = pl.BlockSpec((tk, tm), lambda i, j, q: (q, i)) if ta else pl.BlockSpec((tm, tk), lambda i, j, q: (i, q))
    if b_chunks and tb:
        per_k = k_unit // tk
        b_spec = pl.BlockSpec((None, tn, tk), lambda i, j, q: (q // per_k, j, q % per_k))
    elif b_chunks:
        per_n = n_unit // tn
        b_spec = pl.BlockSpec((None, tk, tn), lambda i, j, q: (j // per_n, q, j % per_n))
    elif tb:
        b_spec = pl.BlockSpec((tn, tk), lambda i, j, q: (j, q))
    else:
        b_spec = pl.BlockSpec((tk, tn), lambda i, j, q: (q, j))
    if out_chunks:
        per_o = n_unit // tn
        o_spec = pl.BlockSpec((None, tm, tn), lambda i, j, q: (j // per_o, i, j % per_o))
        out_shape = jax.ShapeDtypeStruct((N_CHIPS, m, n_unit), out_dtype)
    else:
        o_spec = pl.BlockSpec((tm, tn), lambda i, j, q: (i, j))
        out_shape = jax.ShapeDtypeStruct((m, n), out_dtype)
    in_specs = [a_spec, b_spec]
    args = [a, b]
    if add is not None:
        in_specs.append(o_spec)
        args.append(add)
    return pl.pallas_call(
        body, out_shape=out_shape, grid=(m // tm, n // tn, nk),
        in_specs=in_specs, out_specs=o_spec, scratch_shapes=[pltpu.VMEM((tm, tn), F32)], name=name,
        compiler_params=_params(("parallel", "parallel", "arbitrary")))(*args)


def _rowwise(body, name, rows, ins, outs, tr=ROW_TILE):
    def row_spec(cols):
        return pl.BlockSpec((tr, cols), lambda i: (i, 0))

    def full_spec(shape):
        zeros = (0,) * len(shape)
        return pl.BlockSpec(shape, lambda i: zeros)

    in_specs = [row_spec(a.shape[1]) if kind == "row" else full_spec(a.shape) for a, kind in ins]
    out_specs = [row_spec(shape[1]) if kind == "row" else full_spec(shape) for shape, _, kind in outs]
    out_shape = [jax.ShapeDtypeStruct(shape, dtype) for shape, dtype, _ in outs]
    return pl.pallas_call(body, out_shape=out_shape, grid=(rows // tr,), in_specs=in_specs, out_specs=out_specs,
                          name=name, compiler_params=_params(("arbitrary",)))(*[a for a, _ in ins])


def _rstd(x):
    return lax.rsqrt(jnp.mean(x * x, axis=-1, keepdims=True) + NORM_EPS)


def _rms_bwd_math(x, g, dy):
    r = _rstd(x)
    gd = dy * g
    dx = r * gd - x * (r * r * r) * jnp.mean(gd * x, axis=-1, keepdims=True)
    dg = jnp.sum(dy * x * r, axis=0, keepdims=True)
    return dx, dg


def _sigmoid(x):
    return 0.5 * jnp.tanh(0.5 * x) + 0.5


def _init_acc(*refs):
    @pl.when(pl.program_id(0) == 0)
    def _():
        for r in refs:
            r[...] = jnp.zeros_like(r)


def _prenorm(h, g):
    rows, cols = h.shape

    def body(h_ref, g_ref, o_ref):
        x = h_ref[...]
        o_ref[...] = (x * _rstd(x) * g_ref[...]).astype(BF)

    return _rowwise(body, "prenorm", rows, [(h, "row"), (g, "full")], [((rows, cols), BF, "row")])[0]


def _post_residual(h, y, g_post, g_pre):
    rows, cols = h.shape
    with_pre = g_pre is not None

    def body(*refs):
        if with_pre:
            h_ref, y_ref, gp_ref, gq_ref, hn_ref, hb_ref = refs
        else:
            h_ref, y_ref, gp_ref, hn_ref, hb_ref = refs
        yv = y_ref[...]
        hn = h_ref[...] + yv * _rstd(yv) * gp_ref[...]
        hn_ref[...] = hn
        hb_ref[...] = (hn * _rstd(hn) * gq_ref[...] if with_pre else hn).astype(BF)

    ins = [(h, "row"), (y, "row"), (g_post, "full")] + ([(g_pre, "full")] if with_pre else [])
    return _rowwise(body, "post_residual_pre" if with_pre else "post_residual", rows, ins,
                    [((rows, cols), F32, "row"), ((rows, cols), BF, "row")])


def _ple_forward(h2, pp, z, g_pre):
    rows, cols = h2.shape

    def body(h_ref, p_ref, z_ref, g_ref, h3_ref, hb_ref):
        h3 = h_ref[...] + p_ref[...] * _sigmoid(z_ref[...])
        h3_ref[...] = h3
        hb_ref[...] = (h3 * _rstd(h3) * g_ref[...]).astype(BF)

    return _rowwise(body, "ple_forward", rows, [(h2, "row"), (pp, "row"), (z, "row"), (g_pre, "full")],
                    [((rows, cols), F32, "row"), ((rows, cols), BF, "row")])


def _ple_loss(h2, pp, z, target):
    rows, cols = h2.shape

    def body(h_ref, p_ref, z_ref, t_ref, dh_ref, sq_ref):
        _init_acc(sq_ref)
        err = h_ref[...] + p_ref[...] * _sigmoid(z_ref[...]) - t_ref[...]
        dh_ref[...] = err * (1.0 / cols)
        sq_ref[...] += jnp.sum(err * err, axis=0, keepdims=True)

    return _rowwise(body, "ple_loss", rows, [(h2, "row"), (pp, "row"), (z, "row"), (target, "row")],
                    [((rows, cols), F32, "row"), ((1, cols), F32, "acc")])


def _ple_backward(dh3, pp, z):
    rows, cols = dh3.shape

    def body(d_ref, p_ref, z_ref, dpp_ref, dz_ref):
        d = d_ref[...]
        s = _sigmoid(z_ref[...])
        dpp_ref[...] = (d * s).astype(BF)
        dz_ref[...] = (d * p_ref[...] * s * (1.0 - s)).astype(BF)

    return _rowwise(body, "ple_backward", rows, [(dh3, "row"), (pp, "row"), (z, "row")],
                    [((rows, cols), BF, "row"), ((rows, cols), BF, "row")])


def _rms_backward(x, g, dy, add, out_dtype):
    rows, cols = x.shape
    with_add = add is not None

    def body(*refs):
        if with_add:
            x_ref, g_ref, dy_ref, add_ref, dx_ref, dg_ref = refs
        else:
            x_ref, g_ref, dy_ref, dx_ref, dg_ref = refs
        _init_acc(dg_ref)
        dx, dg = _rms_bwd_math(x_ref[...], g_ref[...], dy_ref[...].astype(F32))
        if with_add:
            dx = dx + add_ref[...]
        dx_ref[...] = dx.astype(out_dtype)
        dg_ref[...] += dg

    ins = [(x, "row"), (g, "full"), (dy, "row")] + ([(add, "row")] if with_add else [])
    return _rowwise(body, "rms_backward_add" if with_add else "rms_backward", rows, ins,
                    [((rows, cols), out_dtype, "row"), ((1, cols), F32, "acc")])


def _swiglu_forward(gu):
    rows = gu.shape[0]

    def body(gu_ref, o_ref):
        g = gu_ref[:, :D_FF].astype(F32)
        o_ref[...] = (g * _sigmoid(g) * gu_ref[:, D_FF:].astype(F32)).astype(BF)

    return _rowwise(body, "swiglu_forward", rows, [(gu, "row")], [((rows, D_FF), BF, "row")])[0]


def _swiglu_backward(gu, dact):
    rows = gu.shape[0]

    def body(gu_ref, d_ref, o_ref):
        g = gu_ref[:, :D_FF].astype(F32)
        u = gu_ref[:, D_FF:].astype(F32)
        d = d_ref[...].astype(F32)
        s = _sigmoid(g)
        gs = g * s
        o_ref[:, :D_FF] = (d * u * (s + gs * (1.0 - s))).astype(BF)
        o_ref[:, D_FF:] = (d * gs).astype(BF)

    return _rowwise(body, "swiglu_backward", rows, [(gu, "row"), (dact, "row")], [((rows, 2 * D_FF), BF, "row")])[0]


def _rope_tables(positions):
    half = MLA_ROPE // 2
    inv = ROPE_THETA ** (-jnp.arange(half, dtype=F32) / half)
    ang = positions.astype(F32)[:, None] * inv
    cos, sin = jnp.cos(ang), jnp.sin(ang)
    rows = positions.shape[0]
    c = jnp.ones((rows, LANE), F32).at[:, 64:80].set(cos).at[:, 80:96].set(cos)
    sa = jnp.zeros((rows, LANE), F32).at[:, 64:80].set(-sin)
    sb = jnp.zeros((rows, LANE), F32).at[:, 80:96].set(sin)
    return c, sa, sb


def _rope_apply(x, c, sa, sb):
    return x * c + pltpu.roll(x, LANE - 16, 1) * sa + pltpu.roll(x, 16, 1) * sb


def _rope_apply_t(dy, c, sa, sb):
    return dy * c + pltpu.roll(dy * sa, 16, 1) + pltpu.roll(dy * sb, LANE - 16, 1)


def _rope_heads(x, tables, transpose, name):
    rows, cols = x.shape

    def body(x_ref, c_ref, sa_ref, sb_ref, o_ref):
        fn = _rope_apply_t if transpose else _rope_apply
        c, sa, sb = c_ref[...], sa_ref[...], sb_ref[...]
        for head in range(cols // LANE):
            lanes = slice(head * LANE, (head + 1) * LANE)
            o_ref[:, lanes] = fn(x_ref[:, lanes].astype(F32), c, sa, sb).astype(BF)

    blk = pl.BlockSpec((ROW_TILE, cols), lambda i: (i, 0))
    tbl = pl.BlockSpec((ROW_TILE, LANE), lambda i: (i, 0))
    return pl.pallas_call(body, out_shape=jax.ShapeDtypeStruct((rows, cols), BF), grid=(rows // ROW_TILE,),
                          in_specs=[blk, tbl, tbl, tbl], out_specs=blk, name=name,
                          compiler_params=_params(("parallel",)))(x, *tables)


def _mla_mid_forward(a, q_norm, kv_norm, tables):
    rows = a.shape[0]
    qr, kvr = MLA_Q_RANK, MLA_KV_RANK

    def body(a_ref, qn_ref, kn_ref, c_ref, sa_ref, sb_ref, cq_ref, ckv_ref, kr_ref):
        aq = a_ref[:, 0:qr]
        akv = a_ref[:, qr:qr + kvr]
        cq_ref[...] = (aq * _rstd(aq) * qn_ref[...]).astype(BF)
        ckv_ref[...] = (akv * _rstd(akv) * kn_ref[...]).astype(BF)
        kr_ref[...] = _rope_apply(a_ref[:, qr + kvr:], c_ref[...], sa_ref[...], sb_ref[...]).astype(BF)

    ins = [(a, "row"), (q_norm, "full"), (kv_norm, "full")] + [(t, "row") for t in tables]
    return _rowwise(body, "mla_mid_forward", rows, ins,
                    [((rows, qr), BF, "row"), ((rows, kvr), BF, "row"), ((rows, LANE), BF, "row")])


def _mla_mid_backward(a, q_norm, kv_norm, tables, dcq, dckv, dkr):
    rows = a.shape[0]
    qr, kvr = MLA_Q_RANK, MLA_KV_RANK

    def body(a_ref, qn_ref, kn_ref, c_ref, sa_ref, sb_ref, dcq_ref, dckv_ref, dkr_ref, da_ref, dqn_ref, dkn_ref):
        _init_acc(dqn_ref, dkn_ref)
        dxq, dgq = _rms_bwd_math(a_ref[:, 0:qr], qn_ref[...], dcq_ref[...])
        dxk, dgk = _rms_bwd_math(a_ref[:, qr:qr + kvr], kn_ref[...], dckv_ref[...])
        da_ref[:, 0:qr] = dxq.astype(BF)
        da_ref[:, qr:qr + kvr] = dxk.astype(BF)
        da_ref[:, qr + kvr:] = _rope_apply_t(dkr_ref[...], c_ref[...], sa_ref[...], sb_ref[...]).astype(BF)
        dqn_ref[...] += dgq
        dkn_ref[...] += dgk

    ins = ([(a, "row"), (q_norm, "full"), (kv_norm, "full")] + [(t, "row") for t in tables]
           + [(dcq, "row"), (dckv, "row"), (dkr, "row")])
    return _rowwise(body, "mla_mid_backward", rows, ins,
                    [((rows, MLA_A_PAD), BF, "row"), ((1, qr), F32, "acc"), ((1, kvr), F32, "acc")])


def _attn_specs(rows, kv_off, g):
    head = pl.BlockSpec((rows, g * LANE), lambda h: (0, h))
    kv_head = pl.BlockSpec((rows, g * LANE), lambda h: (0, h + kv_off // g))
    shared = pl.BlockSpec((rows, LANE), lambda h: (0, 0))
    col_vec = pl.BlockSpec((g, rows, 1), lambda h: (h, 0, 0))
    row_vec = pl.BlockSpec((g, 1, rows), lambda h: (h, 0, 0))
    return head, kv_head, shared, col_vec, row_vec


def _attn_forward(q, kv, kv_off, kr, cum_col, cum_row, scale, group_size, name):
    rows = q.shape[0]
    heads = HEADS
    t = ATTN_FORWARD_KEY_TILE
    tq = ATTN_Q_TILE
    per = tq // t
    has_kr = kr is not None
    has_f = cum_col is not None
    group = range(group_size)

    def body(*refs):
        it = iter(refs)
        q_ref, kv_ref = next(it), next(it)
        kr_ref = next(it) if has_kr else None
        cc_ref = next(it) if has_f else None
        cr_ref = next(it) if has_f else None
        o_ref, lse_ref = next(it), next(it)
        lo = lax.broadcasted_iota(jnp.int32, (1, LANE), 1) < HEAD_DIM
        row = lax.broadcasted_iota(jnp.int32, (tq, t), 0)
        col = lax.broadcasted_iota(jnp.int32, (tq, t), 1)
        lanes = [slice(g * LANE, (g + 1) * LANE) for g in group]

        def q_block(i, _):
            qs = pl.ds(pl.multiple_of(i * tq, tq), tq)
            qbs = [q_ref[qs, lanes[g]] for g in group]
            cqs = [cc_ref[g, qs, :] if has_f else None for g in group]

            def step(j, carry, diag):
                ks = pl.ds(pl.multiple_of(j * t, t), t)
                skip = diag * t if diag and has_f else 0
                other = kr_ref[ks, :] if has_kr else jnp.zeros((t, LANE), BF)
                kvbs = [kv_ref[ks, lanes[g]] for g in group]

                def logit(g):
                    return lax.dot_general(qbs[g][skip:], jnp.where(lo, kvbs[g], other), (((1,), (1,)), ((), ())),
                                           preferred_element_type=F32)

                logits = {g: logit(g) for g in (group if has_f else group[:1])}
                out = []
                for g in group:
                    m, l, acc = (a[skip:] for a in carry[g])
                    if not has_f and g + 1 < len(group):
                        logits[g + 1] = logit(g + 1)
                    s = logits[g] * scale
                    if has_f:
                        s = s + (cqs[g][skip:] - cr_ref[g, :, ks])
                    if diag is not None:
                        s = jnp.where(col[skip:] + diag * t <= row[skip:], s, NEG_INF)
                    mn = jnp.maximum(m, jnp.max(s, axis=1, keepdims=True))
                    alpha = jnp.exp(m - mn)
                    p = jnp.exp(s - mn)
                    l = alpha * l + jnp.sum(p, axis=1, keepdims=True)
                    acc = alpha * acc + jnp.dot(p.astype(BF), kvbs[g], preferred_element_type=F32)
                    new = (mn, l, acc)
                    if skip:
                        new = tuple(jnp.concatenate([old[:skip], a], axis=0) for old, a in zip(carry[g], new))
                    out.append(new)
                return tuple(out)

            init = tuple((jnp.full((tq, 1), NEG_INF, F32), jnp.zeros((tq, 1), F32), jnp.zeros((tq, LANE), F32))
                         for _ in group)
            carry = lax.fori_loop(0, i * per, lambda j, c: step(j, c, None), init)
            for d in range(per):
                carry = step(i * per + d, carry, d)
            for g, (m, l, acc) in enumerate(carry):
                o_ref[qs, lanes[g]] = jnp.where(lo, 0.0, acc * (1.0 / l)).astype(BF)
                lse_ref[g, qs, :] = m + jnp.log(l)
            return 0

        lax.fori_loop(0, rows // tq, q_block, 0)

    head, kv_head, shared, col_vec, row_vec = _attn_specs(rows, kv_off, group_size)
    in_specs, args = [head, kv_head], [q, kv]
    if has_kr:
        in_specs.append(shared)
        args.append(kr)
    if has_f:
        in_specs += [col_vec, row_vec]
        args += [cum_col, cum_row]
    return pl.pallas_call(
        body, out_shape=[jax.ShapeDtypeStruct((rows, heads * LANE), BF), jax.ShapeDtypeStruct((heads, rows, 1), F32)],
        grid=(heads // group_size,), in_specs=in_specs, out_specs=[head, col_vec], name=name,
        compiler_params=_params(("arbitrary",)))(*args)


def _attn_backward(q, kv, kv_off, kr, cum_col, cum_row, o, do, lse, scale, group_size, name):
    rows = q.shape[0]
    heads = HEADS
    t = ATTN_TILE
    nb = rows // t
    has_kr = kr is not None
    has_f = cum_col is not None
    group = range(group_size)

    def body(*refs):
        it = iter(refs)
        q_ref, kv_ref = next(it), next(it)
        kr_ref = next(it) if has_kr else None
        cc_ref = next(it) if has_f else None
        cr_ref = next(it) if has_f else None
        o_ref, do_ref, lse_ref = next(it), next(it), next(it)
        dq_ref, dkv_ref = next(it), next(it)
        dkr_ref = next(it) if has_kr else None
        dck_ref = next(it) if has_f else None
        dcq_ref = next(it) if has_f else None
        dq_acc = next(it)
        lo = lax.broadcasted_iota(jnp.int32, (1, LANE), 1) < HEAD_DIM
        causal = (lax.broadcasted_iota(jnp.int32, (t, t), 1) <= lax.broadcasted_iota(jnp.int32, (t, t), 0))
        lanes = [slice(g * LANE, (g + 1) * LANE) for g in group]

        dq_acc[...] = jnp.zeros_like(dq_acc)
        if has_kr:
            _init_acc(dkr_ref)
        if has_f:
            dcq_ref[...] = jnp.zeros_like(dcq_ref)

        def kv_block(j, _):
            ks = pl.ds(pl.multiple_of(j * t, t), t)
            other = kr_ref[ks, :] if has_kr else jnp.zeros((t, LANE), BF)
            kvbs = [kv_ref[ks, lanes[g]] for g in group]
            kks = [jnp.where(lo, kvbs[g], other) for g in group]
            cks = [cr_ref[g, :, ks] if has_f else None for g in group]

            def pair(i, carry, diag):
                qs = pl.ds(pl.multiple_of(i * t, t), t)
                nt = (((1,), (1,)), ((), ()))

                def first_stage(g):
                    qb = q_ref[qs, lanes[g]]
                    dob = do_ref[qs, lanes[g]]
                    return (qb, dob, lax.dot_general(qb, kks[g], nt, preferred_element_type=F32),
                            lax.dot_general(dob, kvbs[g], nt, preferred_element_type=F32))

                first = {g: first_stage(g) for g in (group[:1] if has_f else group)}
                out = []
                for g in group:
                    dkk, dvv, dcs = carry[g]
                    qb, dob, logit, dp = first[g]
                    if has_f and g + 1 < len(group):
                        first[g + 1] = first_stage(g + 1)
                    s = logit * scale
                    if has_f:
                        s = s + (cc_ref[g, qs, :] - cks[g])
                    if diag:
                        s = jnp.where(causal, s, NEG_INF)
                    p = jnp.exp(s - lse_ref[g, qs, :])
                    delta = jnp.sum(dob.astype(F32) * o_ref[qs, lanes[g]].astype(F32), axis=1, keepdims=True)
                    ds = p * (dp - delta)
                    dsb = ds.astype(BF)
                    dvv = dvv + lax.dot_general(p.astype(BF), dob, (((0,), (0,)), ((), ())), preferred_element_type=F32)
                    dkk = dkk + lax.dot_general(dsb, qb, (((0,), (0,)), ((), ())), preferred_element_type=F32)
                    dq_acc[qs, lanes[g]] += jnp.dot(dsb, kks[g], preferred_element_type=F32)
                    if has_f:
                        dcs = dcs + jnp.sum(ds, axis=0, keepdims=True)
                        dcq_ref[g, qs, :] += jnp.sum(ds, axis=1, keepdims=True)
                    out.append((dkk, dvv, dcs))
                return tuple(out)

            init = tuple((jnp.zeros((t, LANE), F32), jnp.zeros((t, LANE), F32), jnp.zeros((1, t), F32)) for _ in group)
            carry = pair(j, init, True)
            carry = lax.fori_loop(j + 1, nb, lambda i, c: pair(i, c, False), carry)
            for g, (dkk, dvv, dcs) in enumerate(carry):
                dkk = dkk * scale
                dkv_ref[ks, lanes[g]] = jnp.where(lo, dkk, dvv).astype(BF)
                if has_kr:
                    dkr_ref[ks, :] += jnp.where(lo, 0.0, dkk)
                if has_f:
                    dck_ref[g, :, ks] = -dcs
            return 0

        lax.fori_loop(0, nb, kv_block, 0)
        dq_ref[...] = (dq_acc[...] * scale).astype(BF)

    head, kv_head, shared, col_vec, row_vec = _attn_specs(rows, kv_off, group_size)
    in_specs, args = [head, kv_head], [q, kv]
    if has_kr:
        in_specs.append(shared)
        args.append(kr)
    if has_f:
        in_specs += [col_vec, row_vec]
        args += [cum_col, cum_row]
    in_specs += [head, head, col_vec]
    args += [o, do, lse]
    out_shape = [jax.ShapeDtypeStruct((rows, heads * LANE), BF), jax.ShapeDtypeStruct((rows, heads * LANE), BF)]
    out_specs = [head, head]
    if has_kr:
        out_shape.append(jax.ShapeDtypeStruct((rows, LANE), F32))
        out_specs.append(shared)
    if has_f:
        out_shape += [jax.ShapeDtypeStruct((heads, 1, rows), F32), jax.ShapeDtypeStruct((heads, rows, 1), F32)]
        out_specs += [row_vec, col_vec]
    return pl.pallas_call(
        body, out_shape=out_shape, grid=(heads // group_size,), in_specs=in_specs, out_specs=out_specs,
        scratch_shapes=[pltpu.VMEM((rows, group_size * LANE), F32)], name=name,
        compiler_params=_params(("arbitrary",)))(*args)


def _tri_dot(tri, x):
    return jnp.dot(tri, x, preferred_element_type=F32, precision=lax.Precision.HIGHEST)


def _forget_forward(f_raw, b_f):
    rows = f_raw.shape[0]
    t = ATTN_TILE

    def body(f_ref, b_ref, cum_ref):
        tri = (lax.broadcasted_iota(jnp.int32, (t, t), 1) <= lax.broadcasted_iota(jnp.int32, (t, t), 0)).astype(F32)

        def blk(i, carry):
            sl = pl.ds(pl.multiple_of(i * t, t), t)
            xv = f_ref[sl, :] + b_ref[...]
            log_f = jnp.minimum(xv, 0.0) - jnp.log(1.0 + jnp.exp(-jnp.abs(xv)))
            cum_ref[sl, :] = _tri_dot(tri, log_f) + carry
            return carry + jnp.sum(log_f, axis=0, keepdims=True)

        lax.fori_loop(0, rows // t, blk, jnp.zeros((1, LANE), F32))

    return pl.pallas_call(body, out_shape=jax.ShapeDtypeStruct((rows, LANE), F32), name="forget_forward",
                          compiler_params=_params())(f_raw, b_f)


def _forget_backward(f_raw, b_f, dcum):
    rows = f_raw.shape[0]
    t = ATTN_TILE
    nb = rows // t

    def body(f_ref, b_ref, dc_ref, df_ref, db_ref):
        tri = (lax.broadcasted_iota(jnp.int32, (t, t), 1) >= lax.broadcasted_iota(jnp.int32, (t, t), 0)).astype(F32)

        def blk(i, carry):
            later, db = carry
            sl = pl.ds(pl.multiple_of((nb - 1 - i) * t, t), t)
            dc = dc_ref[sl, :]
            dlog = _tri_dot(tri, dc) + later
            xv = f_ref[sl, :] + b_ref[...]
            df = dlog / (1.0 + jnp.exp(xv))
            df_ref[sl, :] = df.astype(BF)
            return later + jnp.sum(dc, axis=0, keepdims=True), db + jnp.sum(df, axis=0, keepdims=True)

        _, db = lax.fori_loop(0, nb, blk, (jnp.zeros((1, LANE), F32), jnp.zeros((1, LANE), F32)))
        db_ref[...] = db

    return pl.pallas_call(body, out_shape=[jax.ShapeDtypeStruct((rows, LANE), BF), jax.ShapeDtypeStruct((1, LANE), F32)],
                          name="forget_backward", compiler_params=_params())(f_raw, b_f, dcum)


def _t5_bucket(dist):
    max_exact = REL_BUCKETS // 2
    n = jnp.maximum(dist.astype(F32), 1.0)
    large = max_exact + (jnp.log(n / max_exact) / math.log(REL_MAX_DIST / max_exact)
                         * (REL_BUCKETS - max_exact)).astype(jnp.int32)
    large = jnp.minimum(large, REL_BUCKETS - 1)
    return jnp.where(dist < max_exact, dist, large)


def _dil_buckets(dilation):
    i = jnp.arange(Q_BLOCK)[:, None]
    j = jnp.arange(Q_BLOCK)[None, :]
    cur = _t5_bucket(jnp.clip(i - j, 0) * dilation).astype(jnp.int32)
    prev = _t5_bucket(jnp.clip(Q_BLOCK + i - j, 0) * dilation).astype(jnp.int32)
    return cur, prev


def _dil_bias_tiles(tbl_ref, bc_ref, bp_ref, bias_ref, group, hp, pairs):
    ii = lax.broadcasted_iota(jnp.int32, (Q_BLOCK, Q_BLOCK), 0)
    jj = lax.broadcasted_iota(jnp.int32, (Q_BLOCK, Q_BLOCK), 1)
    for hh in range(2 * pairs):
        col = group * HEADS + 2 * pairs * hp + hh
        acc_c = jnp.zeros((Q_BLOCK, Q_BLOCK), F32)
        acc_p = jnp.zeros((Q_BLOCK, Q_BLOCK), F32)
        for b in range(REL_BUCKETS):
            val = tbl_ref[b, col]
            acc_c = jnp.where(bc_ref[...] == b, val, acc_c)
            acc_p = jnp.where(bp_ref[...] == b, val, acc_p)
        bias_ref[2 * hh] = jnp.where(jj <= ii, acc_c, NEG_INF)
        bias_ref[2 * hh + 1] = jnp.where(jj >= ii, acc_p, NEG_INF)


def _dil_view(qkv, group, dilation):
    if dilation == 1:
        return qkv
    width = 3 * HEADS * HEAD_DIM
    return qkv[:, group * width:(group + 1) * width].reshape(qkv.shape[0] // dilation, dilation * width)


def _dil_specs(group, dilation, length):
    width = DIL_PAIRS[dilation] * LANE
    per = 8 // DIL_PAIRS[dilation]

    def col(kind):
        if dilation == 1:
            return pl.BlockSpec((length, width), lambda hp, r: (0, (group * 3 + kind) * per + hp))
        return pl.BlockSpec((length, width), lambda hp, r: (0, (r * 3 + kind) * per + hp))

    out = pl.BlockSpec((length, width), lambda hp, r: (0, r * per + hp))
    tile = pl.BlockSpec((Q_BLOCK, Q_BLOCK), lambda hp, r: (0, 0))
    table = pl.BlockSpec(memory_space=pltpu.SMEM)
    return col, out, tile, table


def _dil_forward(view, group, dilation, table, buckets):
    length = view.shape[0]
    rows = length * dilation
    pairs = DIL_PAIRS[dilation]
    nb = length // Q_BLOCK
    scale = HEAD_DIM ** -0.5
    qb = Q_BLOCK

    def body(tbl_ref, bc_ref, bp_ref, q_ref, k_ref, v_ref, o_ref, lse_ref, bias_ref):
        hp = pl.program_id(0)

        @pl.when(pl.program_id(1) == 0)
        def _():
            _dil_bias_tiles(tbl_ref, bc_ref, bp_ref, bias_ref, group, hp, pairs)

        lo = lax.broadcasted_iota(jnp.int32, (1, LANE), 1) < HEAD_DIM
        nt = (((1,), (1,)), ((), ()))

        def blk(n, first):
            cur = pl.ds(0, qb) if first else pl.ds(pl.multiple_of(n * qb, qb), qb)
            prev = None if first else pl.ds(pl.multiple_of((n - 1) * qb, qb), qb)
            logits = []
            for pair in range(pairs):
                lanes = slice(pair * LANE, (pair + 1) * LANE)
                qn = q_ref[cur, lanes] * scale
                for hh in range(2):
                    qm = jnp.where(lo if hh == 0 else ~lo, qn, jnp.zeros_like(qn))
                    s_c = lax.dot_general(qm, k_ref[cur, lanes], nt, preferred_element_type=F32)
                    s_p = None if first else lax.dot_general(qm, k_ref[prev, lanes], nt, preferred_element_type=F32)
                    logits.append((s_c, s_p))
            for pair in range(pairs):
                lanes = slice(pair * LANE, (pair + 1) * LANE)
                outs, lses = [], []
                for hh in range(2):
                    bias = 4 * pair + 2 * hh
                    s_c, s_p = logits[2 * pair + hh]
                    s_c = s_c + bias_ref[bias]
                    m = jnp.max(s_c, axis=1, keepdims=True)
                    if not first:
                        s_p = s_p + bias_ref[bias + 1]
                        m = jnp.maximum(m, jnp.max(s_p, axis=1, keepdims=True))
                    e_c = jnp.exp(s_c - m)
                    l = jnp.sum(e_c, axis=1, keepdims=True)
                    acc = jnp.dot(e_c.astype(BF), v_ref[cur, lanes], preferred_element_type=F32)
                    if not first:
                        e_p = jnp.exp(s_p - m)
                        l = l + jnp.sum(e_p, axis=1, keepdims=True)
                        acc = acc + jnp.dot(e_p.astype(BF), v_ref[prev, lanes], preferred_element_type=F32)
                    outs.append(acc * (1.0 / l))
                    lses.append(m + jnp.log(l))
                o_ref[cur, lanes] = jnp.where(lo, outs[0], outs[1])
                lse_ref[cur, lanes] = jnp.where(lo, lses[0], lses[1])
            return 0

        blk(0, True)
        if nb > 1:
            lax.fori_loop(1, nb, lambda n, _: blk(n, False), 0)

    col, out, tile, tbl = _dil_specs(group, dilation, length)
    bc, bp = buckets
    o, lse = pl.pallas_call(
        body, out_shape=[jax.ShapeDtypeStruct((length, dilation * D_MODEL), F32)] * 2,
        grid=(8 // pairs, dilation), in_specs=[tbl, tile, tile, col(0), col(1), col(2)], out_specs=[out, out],
        scratch_shapes=[pltpu.VMEM((4 * pairs, qb, qb), F32)], name=f"dilated_forward_{dilation}",
        compiler_params=_params(("arbitrary", "arbitrary")))(
            table, bc, bp, view, view, view)
    return o.reshape(rows, D_MODEL), lse.reshape(rows, D_MODEL)


def _dil_backward(view, group, dilation, table, buckets, do_g, lse, dlt):
    length = view.shape[0]
    rows = length * dilation
    pairs = DIL_PAIRS[dilation]
    nb = length // Q_BLOCK
    scale = HEAD_DIM ** -0.5
    qb = Q_BLOCK

    def body(tbl_ref, bc_ref, bp_ref, q_ref, k_ref, v_ref, do_ref, lse_ref, dlt_ref,
             dq_ref, dk_ref, dv_ref, db_ref, bias_ref, dk_acc, dv_acc):
        hp = pl.program_id(0)

        @pl.when(pl.program_id(1) == 0)
        def _():
            _dil_bias_tiles(tbl_ref, bc_ref, bp_ref, bias_ref, group, hp, pairs)
            db_ref[...] = jnp.zeros_like(db_ref)

        dk_acc[...] = jnp.zeros_like(dk_acc)
        dv_acc[...] = jnp.zeros_like(dv_acc)
        lo = lax.broadcasted_iota(jnp.int32, (1, LANE), 1) < HEAD_DIM
        tn = (((0,), (0,)), ((), ()))
        nt = (((1,), (1,)), ((), ()))

        def blk(n, first):
            cur = pl.ds(0, qb) if first else pl.ds(pl.multiple_of(n * qb, qb), qb)
            prev = None if first else pl.ds(pl.multiple_of((n - 1) * qb, qb), qb)
            inputs = []
            for pair in range(pairs):
                lanes = slice(pair * LANE, (pair + 1) * LANE)
                qn = q_ref[cur, lanes] * scale
                don = do_ref[cur, lanes]
                for hh in range(2):
                    mask = lo if hh == 0 else ~lo
                    qm = jnp.where(mask, qn, jnp.zeros_like(qn))
                    dom = jnp.where(mask, don, jnp.zeros_like(don))
                    stage = [qm, dom, lax.dot_general(qm, k_ref[cur, lanes], nt, preferred_element_type=F32),
                             lax.dot_general(dom, v_ref[cur, lanes], nt, preferred_element_type=F32)]
                    if not first:
                        stage += [lax.dot_general(qm, k_ref[prev, lanes], nt, preferred_element_type=F32),
                                  lax.dot_general(dom, v_ref[prev, lanes], nt, preferred_element_type=F32)]
                    inputs.append(stage)
            for pair in range(pairs):
                lanes = slice(pair * LANE, (pair + 1) * LANE)
                kc = k_ref[cur, lanes]
                if not first:
                    kp = k_ref[prev, lanes]
                lse_n = lse_ref[cur, lanes]
                dlt_n = dlt_ref[cur, lanes]
                dqs = []
                dkc = jnp.zeros((qb, LANE), F32)
                dkp = jnp.zeros((qb, LANE), F32)
                dvc = jnp.zeros((qb, LANE), F32)
                dvp = jnp.zeros((qb, LANE), F32)
                for hh in range(2):
                    bias = 4 * pair + 2 * hh
                    mask = lo if hh == 0 else ~lo
                    qm, dom, s_c, dp_c = inputs[2 * pair + hh][:4]
                    lse_h = jnp.max(jnp.where(mask, lse_n, -3e38), axis=1, keepdims=True)
                    dlt_h = jnp.max(jnp.where(mask, dlt_n, -3e38), axis=1, keepdims=True)
                    p_c = jnp.exp(s_c + bias_ref[bias] - lse_h)
                    ds_c = p_c * (dp_c - dlt_h)
                    db_ref[pair, 2 * hh] += ds_c
                    dsc_b = ds_c.astype(BF)
                    dq = jnp.dot(dsc_b, kc, preferred_element_type=F32)
                    dkc = dkc + lax.dot_general(dsc_b, qm, tn, preferred_element_type=F32)
                    dvc = dvc + lax.dot_general(p_c.astype(BF), dom, tn, preferred_element_type=F32)
                    if not first:
                        s_p, dp_p = inputs[2 * pair + hh][4:]
                        p_p = jnp.exp(s_p + bias_ref[bias + 1] - lse_h)
                        ds_p = p_p * (dp_p - dlt_h)
                        db_ref[pair, 2 * hh + 1] += ds_p
                        dsp_b = ds_p.astype(BF)
                        dq = dq + jnp.dot(dsp_b, kp, preferred_element_type=F32)
                        dkp = dkp + lax.dot_general(dsp_b, qm, tn, preferred_element_type=F32)
                        dvp = dvp + lax.dot_general(p_p.astype(BF), dom, tn, preferred_element_type=F32)
                    dqs.append(dq)
                dq_ref[cur, lanes] = (jnp.where(lo, dqs[0], dqs[1]) * scale).astype(BF)
                dk_acc[cur, lanes] += dkc
                dv_acc[cur, lanes] += dvc
                if not first:
                    dk_acc[prev, lanes] += dkp
                    dv_acc[prev, lanes] += dvp
            return 0

        blk(0, True)
        if nb > 1:
            lax.fori_loop(1, nb, lambda n, _: blk(n, False), 0)
        dk_ref[...] = dk_acc[...].astype(BF)
        dv_ref[...] = dv_acc[...].astype(BF)

    col, out, tile, tbl = _dil_specs(group, dilation, length)
    bc, bp = buckets
    wide = (length, dilation * D_MODEL)
    dq, dk, dv, db = pl.pallas_call(
        body, out_shape=[jax.ShapeDtypeStruct(wide, BF)] * 3 + [jax.ShapeDtypeStruct((8, 4, qb, qb), F32)],
        grid=(8 // pairs, dilation), in_specs=[tbl, tile, tile, col(0), col(1), col(2), out, out, out],
        out_specs=[out, out, out, pl.BlockSpec((pairs, 4, qb, qb), lambda hp, r: (hp, 0, 0, 0))],
        scratch_shapes=[pltpu.VMEM((4 * pairs, qb, qb), F32), pltpu.VMEM((length, pairs * LANE), F32),
                        pltpu.VMEM((length, pairs * LANE), F32)],
        name=f"dilated_backward_{dilation}", compiler_params=_params(("arbitrary", "arbitrary")))(
            table, bc, bp, view, view, view,
            do_g.reshape(wide), lse.reshape(wide), dlt.reshape(wide))
    return dq.reshape(rows, D_MODEL), dk.reshape(rows, D_MODEL), dv.reshape(rows, D_MODEL), db


def _head_sums(x, lo):
    s0 = jnp.sum(jnp.where(lo, x, 0.0), axis=1, keepdims=True)
    s1 = jnp.sum(jnp.where(lo, 0.0, x), axis=1, keepdims=True)
    return jnp.where(lo, s0, s1)


def _dil_merge_forward(outs, lses):
    rows = outs[0].shape[0]

    def body(o0, o1, o2, l0, l1, l2, o_ref):
        ls = [l0[...], l1[...], l2[...]]
        m = jnp.maximum(jnp.maximum(ls[0], ls[1]), ls[2])
        es = [jnp.exp(v - m) for v in ls]
        tot = es[0] + es[1] + es[2]
        o_ref[...] = ((es[0] * o0[...] + es[1] * o1[...] + es[2] * o2[...]) / tot).astype(BF)

    blk = pl.BlockSpec((ROW_TILE, LANE), lambda i, j: (i, j))
    return pl.pallas_call(body, out_shape=jax.ShapeDtypeStruct((rows, D_MODEL), BF), grid=(rows // ROW_TILE, 8),
                          in_specs=[blk] * 6, out_specs=blk, name="dilated_merge_forward",
                          compiler_params=_params(("parallel", "parallel")))(*outs, *lses)


def _dil_merge_backward(outs, lses, do):
    rows = outs[0].shape[0]

    def body(o0, o1, o2, l0, l1, l2, do_ref, d0, d1, d2, t0, t1, t2):
        lo = lax.broadcasted_iota(jnp.int32, (1, LANE), 1) < HEAD_DIM
        ls = [l0[...], l1[...], l2[...]]
        os_ = [o0[...], o1[...], o2[...]]
        m = jnp.maximum(jnp.maximum(ls[0], ls[1]), ls[2])
        es = [jnp.exp(v - m) for v in ls]
        inv = 1.0 / (es[0] + es[1] + es[2])
        alphas = [e * inv for e in es]
        dov = do_ref[...]
        merged = alphas[0] * os_[0] + alphas[1] * os_[1] + alphas[2] * os_[2]
        dot = _head_sums(dov * merged, lo)
        for a, d_ref, t_ref in zip(alphas, (d0, d1, d2), (t0, t1, t2)):
            d_ref[...] = (a * dov).astype(BF)
            t_ref[...] = a * dot

    blk = pl.BlockSpec((ROW_TILE, LANE), lambda i, j: (i, j))
    res = pl.pallas_call(
        body, out_shape=[jax.ShapeDtypeStruct((rows, D_MODEL), BF)] * 3 + [jax.ShapeDtypeStruct((rows, D_MODEL), F32)] * 3,
        grid=(rows // ROW_TILE, 8), in_specs=[blk] * 7, out_specs=[blk] * 6, name="dilated_merge_backward",
        compiler_params=_params(("parallel", "parallel")))(*outs, *lses, do)
    return res[:3], res[3:]


def _rel_bias_grad(dbs, buckets):
    def body(db_ref, bc_ref, bp_ref, o_ref):
        g = pl.program_id(0)
        hp = pl.program_id(1)

        @pl.when((g == 0) & (hp == 0))
        def _():
            o_ref[...] = jnp.zeros_like(o_ref)

        rr = lax.broadcasted_iota(jnp.int32, (REL_BUCKETS, LANE), 0)
        cc = lax.broadcasted_iota(jnp.int32, (REL_BUCKETS, LANE), 1)
        bc = bc_ref[0]
        bp = bp_ref[0]
        acc = jnp.zeros((REL_BUCKETS, LANE), F32)
        for hh in range(2):
            col = g * HEADS + 2 * hp + hh
            d_c = db_ref[0, 0, 2 * hh]
            d_p = db_ref[0, 0, 2 * hh + 1]
            for b in range(REL_BUCKETS):
                val = (jnp.sum(jnp.where(bc == b, d_c, 0.0), keepdims=True)
                       + jnp.sum(jnp.where(bp == b, d_p, 0.0), keepdims=True))
                acc = jnp.where((rr == b) & (cc == col), val, acc)
        o_ref[...] += acc

    db_all = jnp.stack(dbs)
    bc_all = jnp.stack([b[0] for b in buckets])
    bp_all = jnp.stack([b[1] for b in buckets])
    tile = pl.BlockSpec((1, Q_BLOCK, Q_BLOCK), lambda g, hp: (g, 0, 0))
    return pl.pallas_call(
        body, out_shape=jax.ShapeDtypeStruct((REL_BUCKETS, LANE), F32), grid=(3, 8),
        in_specs=[pl.BlockSpec((1, 1, 4, Q_BLOCK, Q_BLOCK), lambda g, hp: (g, hp, 0, 0, 0)), tile, tile],
        out_specs=pl.BlockSpec((REL_BUCKETS, LANE), lambda g, hp: (0, 0)), name="rel_bias_grad",
        compiler_params=_params(("arbitrary", "arbitrary")))(db_all, bc_all, bp_all)


def _mla_forward(hn, w, tables):
    a = _matmul(hn, w["w_a"], name="mla_a")
    cq, ckv, kr = _mla_mid_forward(a, w["q_norm"], w["kv_norm"], tables)
    q_raw = _matmul(cq, w["w_uq"], name="mla_uq")
    q = _rope_heads(q_raw, tables, False, "rope_forward")
    kv = _matmul(ckv, w["w_ukv"], b_chunks=True, out_dtype=BF, name="mla_ukv")
    scale = (HEAD_DIM + MLA_ROPE) ** -0.5
    o, lse = _attn_forward(q, kv, 0, kr, None, None, scale, MLA_GROUP, "mla_attention_forward")
    y = _matmul(o, w["w_o"], name="attn_out")
    return y, dict(hn=hn, a=a, cq=cq, ckv=ckv, kr=kr, q=q, kv=kv, o=o, lse=lse)


def _mla_backward(dy, w, s, tables):
    scale = (HEAD_DIM + MLA_ROPE) ** -0.5
    g = {}
    g["w_o"] = _matmul(s["o"], dy, ta=True, out_dtype=BF, name="attn_out_dw")
    do = _matmul(dy, w["w_o"], tb=True, out_dtype=BF, name="attn_out_dx")
    dq, dkv, dkr = _attn_backward(s["q"], s["kv"], 0, s["kr"], None, None, s["o"], do, s["lse"], scale,
                                  MLA_GROUP, "mla_attention_backward")
    dq_raw = _rope_heads(dq, tables, True, "rope_backward")
    g["w_uq"] = _matmul(s["cq"], dq_raw, ta=True, out_dtype=BF, name="mla_uq_dw")
    dcq = _matmul(dq_raw, w["w_uq"], tb=True, name="mla_uq_dx")
    g["w_ukv"] = _matmul(s["ckv"], dkv, ta=True, out_chunks=True, out_dtype=BF, name="mla_ukv_dw")
    dckv = _matmul(dkv, w["w_ukv"], tb=True, b_chunks=True, name="mla_ukv_dx")
    da, g["q_norm"], g["kv_norm"] = _mla_mid_backward(s["a"], w["q_norm"], w["kv_norm"], tables, dcq, dckv, dkr)
    g["w_a"] = _matmul(s["hn"], da, ta=True, out_dtype=BF, name="mla_a_dw")
    dhn = _matmul(da, w["w_a"], tb=True, name="mla_a_dx")
    return dhn, g


def _fox_forward(hn, w):
    qkv = _matmul(hn, w["w_qkv"], out_dtype=BF, name="fox_qkv")
    f_raw = _matmul(hn, w["w_f"], name="fox_f")
    cum = _forget_forward(f_raw, w["b_f"])
    cum_heads = cum[:, :HEADS].T
    cum_col, cum_row = cum_heads[:, :, None], cum_heads[:, None, :]
    o, lse = _attn_forward(qkv, qkv, HEADS, None, cum_col, cum_row, HEAD_DIM ** -0.5, FOX_GROUP,
                           "fox_attention_forward")
    y = _matmul(o, w["w_o"], name="attn_out")
    return y, dict(hn=hn, qkv=qkv, f_raw=f_raw, cum_col=cum_col, cum_row=cum_row, o=o, lse=lse)


def _fox_backward(dy, w, s):
    g = {}
    g["w_o"] = _matmul(s["o"], dy, ta=True, out_dtype=BF, name="attn_out_dw")
    do = _matmul(dy, w["w_o"], tb=True, out_dtype=BF, name="attn_out_dx")
    dq, dkv, dck, dcq = _attn_backward(s["qkv"], s["qkv"], HEADS, None, s["cum_col"], s["cum_row"], s["o"], do,
                                       s["lse"], HEAD_DIM ** -0.5, FOX_GROUP, "fox_attention_backward")
    dcum = jnp.pad((dck[:, 0, :] + dcq[:, :, 0]).T, ((0, 0), (0, LANE - HEADS)))
    df, g["b_f"] = _forget_backward(s["f_raw"], w["b_f"], dcum)
    dqkv = jnp.concatenate([dq, dkv], axis=1)
    g["w_qkv"] = _matmul(s["hn"], dqkv, ta=True, out_dtype=BF, name="fox_qkv_dw")
    g["w_f"] = _matmul(s["hn"], df, ta=True, out_dtype=BF, name="fox_f_dw")
    dhn = _matmul(dqkv, w["w_qkv"], tb=True, name="fox_qkv_dx")
    dhn = _matmul(df, w["w_f"], tb=True, add=dhn, name="fox_f_dx")
    return dhn, g


def _dil_mixer_forward(hn, w, buckets):
    qkv = _matmul(hn, w["w_qkv"], b_chunks=True, out_dtype=BF, name="dil_qkv")
    views = [_dil_view(qkv, grp, dilation) for grp, (_, dilation) in enumerate(DIL_PATTERNS)]
    outs, lses = [], []
    for grp, (_, dilation) in enumerate(DIL_PATTERNS):
        o_g, lse_g = _dil_forward(views[grp], grp, dilation, w["rel_bias"], buckets[grp])
        outs.append(o_g)
        lses.append(lse_g)
    o = _dil_merge_forward(outs, lses)
    y = _matmul(o, w["w_o"], name="dil_out")
    return y, dict(hn=hn, views=views, outs=outs, lses=lses, o=o)


def _dil_mixer_backward(dy, w, s, buckets):
    g = {}
    g["w_o"] = _matmul(s["o"], dy, ta=True, out_dtype=BF, name="dil_out_dw")
    do = _matmul(dy, w["w_o"], tb=True, name="dil_out_dx")
    do_gs, dlts = _dil_merge_backward(s["outs"], s["lses"], do)
    parts, dbs = [], []
    for grp, (_, dilation) in enumerate(DIL_PATTERNS):
        dq, dk, dv, db = _dil_backward(s["views"][grp], grp, dilation, w["rel_bias"], buckets[grp], do_gs[grp],
                                       s["lses"][grp], dlts[grp])
        parts += [dq, dk, dv]
        dbs.append(db)
    dqkv = jnp.concatenate(parts, axis=1)
    g["rel_bias"] = _rel_bias_grad(dbs, buckets)
    g["w_qkv"] = _matmul(s["hn"], dqkv, ta=True, out_chunks=True, out_dtype=BF, name="dil_qkv_dw")
    dhn = _matmul(dqkv, w["w_qkv"], tb=True, b_chunks=True, name="dil_qkv_dx")
    return dhn, g


def _mixer_weights(i, lw, small):
    mixer, j = i % N_MIXERS, i // N_MIXERS
    if mixer == 0:
        return dict(lw["mixer"], q_norm=small["mla_q_norm"][j][None, :], kv_norm=small["mla_kv_norm"][j][None, :])
    if mixer == 1:
        return dict(lw["mixer"], rel_bias=small["rel_bias"])
    return dict(lw["mixer"], b_f=jnp.pad(small["fox_b_f"][j][None, :], ((0, 0), (0, LANE - HEADS))))


MIXER_PART, COMMON_PART = 0, 1


def _run_layers(x, p, positions, target, get_part, get_small, put_part):
    tables = _rope_tables(positions)
    buckets = [_dil_buckets(d) for _, d in DIL_PATTERNS]
    layers, saved = [], []
    h = x
    first = get_part(0, MIXER_PART, positions)
    small = get_small()

    def gain(i, k):
        return small["norm_g"][i, k][None, :]

    hn = _prenorm(h, gain(0, 0))
    sq = dh = None
    for i in range(DEPTH):
        mixer = i % N_MIXERS
        lw = dict(first if i == 0 else get_part(i, MIXER_PART, h))
        mw = _mixer_weights(i, lw, small)
        if mixer == 0:
            y, ms = _mla_forward(hn, mw, tables)
        elif mixer == 1:
            y, ms = _dil_mixer_forward(hn, mw, buckets)
        else:
            y, ms = _fox_forward(hn, mw)
        if "ffn_w_in" not in lw:
            lw.update(get_part(i, COMMON_PART, y))
        layers.append(lw)
        h1, hn2 = _post_residual(h, y, gain(i, 1), gain(i, 2))
        gu = _matmul(hn2, lw["ffn_w_in"], b_chunks=True, out_dtype=BF, name="ffn_in")
        act = _swiglu_forward(gu)
        f = _matmul(act, lw["ffn_w_out"], name="ffn_out")
        h2, h2b = _post_residual(h1, f, gain(i, 3), None)
        pp = _matmul(p[i], lw["ple_w_proj"], b_chunks=True, name="ple_proj")
        z = _matmul(h2b, lw["ple_w_gate"], name="ple_gate")
        saved.append(dict(h=h, y=y, ms=ms, h1=h1, hn2=hn2, gu=gu, act=act, f=f, h2b=h2b, pp=pp, z=z))
        if i + 1 < DEPTH:
            h, hn = _ple_forward(h2, pp, z, gain(i + 1, 0))
        else:
            dh, sq = _ple_loss(h2, pp, z, target)

    norm_rows = [[None] * 4 for _ in range(DEPTH)]
    sg = dict(mla_q_norm={}, mla_kv_norm={}, rel_bias=None, fox_b_f={})
    for i in reversed(range(DEPTH)):
        s, lw = saved[i], layers[i]
        mixer, j = i % N_MIXERS, i // N_MIXERS
        mw = _mixer_weights(i, lw, small)
        lg = {}
        dpp, dz = _ple_backward(dh, s["pp"], s["z"])
        lg["ple_w_proj"] = _matmul(p[i], dpp, ta=True, out_chunks=True, out_dtype=BF, name="ple_proj_dw")
        lg["ple_w_gate"] = _matmul(s["h2b"], dz, ta=True, out_dtype=BF, name="ple_gate_dw")
        dh2 = _matmul(dz, lw["ple_w_gate"], tb=True, add=dh, name="ple_gate_dx")
        df, norm_rows[i][3] = _rms_backward(s["f"], gain(i, 3), dh2, None, BF)
        lg["ffn_w_out"] = _matmul(s["act"], df, ta=True, out_dtype=BF, name="ffn_out_dw")
        dact = _matmul(df, lw["ffn_w_out"], tb=True, out_dtype=BF, name="ffn_out_dx")
        dgu = _swiglu_backward(s["gu"], dact)
        lg["ffn_w_in"] = _matmul(s["hn2"], dgu, ta=True, out_chunks=True, out_dtype=BF, name="ffn_in_dw")
        split = i in SPLIT_LAYERS
        zero = put_part(i, COMMON_PART, lg)[0:1, 0:1] if split else 0.0
        dhn2 = _matmul(dgu, lw["ffn_w_in"], tb=True, b_chunks=True, name="ffn_in_dx")
        dh1, norm_rows[i][2] = _rms_backward(s["h1"], gain(i, 2), dhn2, dh2, F32)
        dy, norm_rows[i][1] = _rms_backward(s["y"], gain(i, 1) + zero, dh1, None, BF)
        if mixer == 0:
            dhn, mg = _mla_backward(dy, mw, s["ms"], tables)
            sg["mla_q_norm"][j] = mg.pop("q_norm")
            sg["mla_kv_norm"][j] = mg.pop("kv_norm")
        elif mixer == 1:
            dhn, mg = _dil_mixer_backward(dy, mw, s["ms"], buckets)
            rel = mg.pop("rel_bias")[:, :3 * HEADS]
            sg["rel_bias"] = rel if sg["rel_bias"] is None else sg["rel_bias"] + rel
        else:
            dhn, mg = _fox_backward(dy, mw, s["ms"])
            sg["fox_b_f"][j] = mg.pop("b_f")[:, :HEADS]
        token = put_part(i, MIXER_PART, dict(mixer=mg) if split else dict(lg, mixer=mg))
        dh, norm_rows[i][0] = _rms_backward(s["h"], gain(i, 0) + token[0:1, 0:1], dhn, dh1, F32)
    small_grads = dict(norm_g=jnp.stack([jnp.concatenate(row, axis=0) for row in norm_rows]),
                       rel_bias=sg["rel_bias"])
    for k in ("mla_q_norm", "mla_kv_norm", "fox_b_f"):
        small_grads[k] = jnp.concatenate([sg[k][j] for j in sorted(sg[k])], axis=0)
    return sq, dh, small_grads


BIG = ("ffn_w_in", "ffn_w_out", "ple_w_proj", "ple_w_gate", "mla_w_a", "mla_w_uq", "mla_w_ukv", "mla_w_o",
       "dil_w_qkv", "dil_w_o", "fox_w_qkvf", "fox_w_o")
SMALL_SHARDED = ("norm_g", "mla_q_norm", "mla_kv_norm")
SMALL_REPLICATED = ("rel_bias", "fox_b_f")
WEIGHTS = ("norm_g", "ffn_w_in", "ffn_w_out", "ple_w_proj", "ple_w_gate", "rel_bias", "mla_w_a", "mla_q_norm",
           "mla_kv_norm", "mla_w_uq", "mla_w_ukv", "mla_w_o", "dil_w_qkv", "dil_w_o", "fox_w_qkvf", "fox_b_f", "fox_w_o")


TRANSPOSED = "fox_w_qkvf"
SPLIT_LAYERS = (0, 1, 2, 3)
LAYER_COMMON = ("ffn_w_in", "ffn_w_out", "ple_w_proj", "ple_w_gate")
MIXER_WEIGHTS = (("mla_w_a", "mla_w_uq", "mla_w_ukv", "mla_w_o"), ("dil_w_qkv", "dil_w_o"), ("fox_w_qkvf", "fox_w_o"))


def _part_names(i, part):
    if i in SPLIT_LAYERS:
        return MIXER_WEIGHTS[i % N_MIXERS] if part == MIXER_PART else LAYER_COMMON
    return MIXER_WEIGHTS[i % N_MIXERS] + LAYER_COMMON if part == MIXER_PART else ()


def _layer_slot(name, i):
    return i if name in LAYER_COMMON else i // N_MIXERS


def _merge_rows(chunks):
    n, r, c = chunks.shape
    return chunks.reshape(n * r, c)


def _merge_cols(chunks):
    n, r, c = chunks.shape
    return chunks.transpose(1, 0, 2).reshape(r, n * c)


def _pad_heads_out(wo):
    w3 = wo.reshape(HEADS, HEAD_DIM, D_MODEL)
    return jnp.pad(w3, ((0, 0), (HEAD_DIM, 0), (0, 0))).reshape(HEADS * LANE, D_MODEL)


def _part_to_compute(i, part, ch):
    lw = {}
    if "ffn_w_in" in ch:
        lw.update(ffn_w_in=ch["ffn_w_in"], ffn_w_out=_merge_rows(ch["ffn_w_out"]), ple_w_proj=ch["ple_w_proj"],
                  ple_w_gate=_merge_rows(ch["ple_w_gate"]))
    if part == COMMON_PART:
        return lw
    mixer = i % N_MIXERS
    if mixer == 0:
        wa = _merge_rows(ch["mla_w_a"])
        rank = MLA_Q_RANK + MLA_KV_RANK
        wa_p = jnp.concatenate([wa[:, :rank], jnp.zeros((wa.shape[0], 64), wa.dtype), wa[:, rank:],
                                jnp.zeros((wa.shape[0], 32), wa.dtype)], axis=1)
        wuq = _merge_cols(ch["mla_w_uq"]).reshape(MLA_Q_RANK, HEADS, HEAD_DIM + MLA_ROPE)
        wuq_p = jnp.pad(wuq, ((0, 0), (0, 0), (0, LANE - HEAD_DIM - MLA_ROPE))).reshape(MLA_Q_RANK, HEADS * LANE)
        lw["mixer"] = dict(w_a=wa_p, w_uq=wuq_p, w_ukv=ch["mla_w_ukv"], w_o=_pad_heads_out(_merge_rows(ch["mla_w_o"])))
    elif mixer == 1:
        lw["mixer"] = dict(w_qkv=ch["dil_w_qkv"], w_o=_merge_rows(ch["dil_w_o"]))
    else:
        wf = _merge_rows(ch["fox_w_qkvf"]).T
        inner = HEADS * HEAD_DIM
        q3 = wf[:, :inner].reshape(D_MODEL, HEADS, HEAD_DIM)
        k3 = wf[:, inner:2 * inner].reshape(D_MODEL, HEADS, HEAD_DIM)
        v3 = wf[:, 2 * inner:3 * inner].reshape(D_MODEL, HEADS, HEAD_DIM)
        q_p = jnp.pad(q3, ((0, 0), (0, 0), (0, HEAD_DIM))).reshape(D_MODEL, HEADS * LANE)
        kv_p = jnp.concatenate([k3, v3], axis=2).reshape(D_MODEL, HEADS * LANE)
        f_p = jnp.pad(wf[:, 3 * inner:], ((0, 0), (0, LANE - HEADS)))
        lw["mixer"] = dict(w_qkv=jnp.concatenate([q_p, kv_p], axis=1), w_f=f_p,
                           w_o=_pad_heads_out(_merge_rows(ch["fox_w_o"])))
    return lw


def _part_contributions(i, part, lg, chunk_shapes):
    spec = {k: jax.ShapeDtypeStruct(s, BF) for k, s in chunk_shapes.items()}
    (contrib,) = jax.linear_transpose(functools.partial(_part_to_compute, i, part), spec)(lg)
    return contrib


def _chip_peers():
    x, y, c = lax.axis_index("x"), lax.axis_index("y"), lax.axis_index("c")
    peers = [(1 - x, y), (x, 1 - y), (1 - x, 1 - y)]
    return x, y, c, peers


SEM_SPEC = pl.BlockSpec(memory_space=pltpu.SEMAPHORE)
ANY_SPEC = pl.BlockSpec(memory_space=pl.ANY)
SPLIT_EFFECT = pltpu.SideEffectType.DATAFLOW_SIDE_EFFECTING


def _own_slot(shard):
    me = 2 * lax.axis_index("x") + lax.axis_index("y")
    return lax.dynamic_update_index_in_dim(lax.empty((N_CHIPS,) + shard.shape, shard.dtype), shard[None], me, 0)


def _spread_copy(src, land, k, peer, c, send_sems, recv_sems, index, src_slot, slot):
    px, py = peer
    return pltpu.make_async_remote_copy(
        src_ref=src.at[src_slot], dst_ref=land.at[slot],
        send_sem=send_sems.at[3 * index + k], recv_sem=recv_sems.at[3 * index + k],
        device_id=(px, py, c), device_id_type=MESH)


def _spread_start(bufs, srcs, after, name):
    n = len(bufs)
    exchange = srcs is not None
    arrays = (list(srcs) if exchange else []) + list(bufs)
    na = len(arrays)

    def body(*refs):
        src, land = refs[:n], refs[na - n:na]
        send_sems, recv_sems = refs[na + 1], refs[na + 2]
        token = refs[-1]
        x, y, c, peers = _chip_peers()
        me = 2 * x + y
        for w in range(n):
            for k, peer in enumerate(peers):
                src_slot = 2 * peer[0] + peer[1] if exchange else me
                _spread_copy(src[w], land[w], k, peer, c, send_sems, recv_sems, w, src_slot, me).start()
        token[...] = jnp.zeros_like(token)

    hbm = [pltpu.with_memory_space_constraint(a, pltpu.HBM) for a in arrays]
    out = pl.pallas_call(
        body, name=name,
        out_shape=(pltpu.SemaphoreType.DMA((3 * n,)), pltpu.SemaphoreType.DMA((3 * n,)),
                   *[pltpu.HBM(a.shape, a.dtype) for a in hbm], jax.ShapeDtypeStruct((8, LANE), F32)),
        in_specs=[HBM_SPEC] * na + [ANY_SPEC],
        out_specs=(SEM_SPEC, SEM_SPEC, *[HBM_SPEC] * na, pl.BlockSpec(memory_space=pltpu.VMEM)),
        input_output_aliases={w: 2 + w for w in range(na)},
        compiler_params=pltpu.CompilerParams(has_side_effects=SPLIT_EFFECT))(*hbm, after)
    return dict(send=out[0], recv=out[1], arrays=out[2:2 + na], n=n, token=out[-1], exchange=exchange)


def _spread_wait(handle, after, name):
    n, exchange = handle["n"], handle["exchange"]
    arrays = list(handle["arrays"])
    na = len(arrays)

    def body(*refs):
        src, land = refs[:n], refs[na - n:na]
        send_sems, recv_sems = refs[na], refs[na + 1]
        x, y, c, peers = _chip_peers()
        me = 2 * x + y
        for w in range(n):
            for k, peer in enumerate(peers):
                there = 2 * peer[0] + peer[1]
                cp = _spread_copy(src[w], land[w], k, peer, c, send_sems, recv_sems, w, there if exchange else me, there)
                cp.wait_send()
                cp.wait_recv()

    out = pl.pallas_call(
        body, name=name, out_shape=tuple(pltpu.HBM(a.shape, a.dtype) for a in arrays),
        in_specs=[HBM_SPEC] * na + [SEM_SPEC, SEM_SPEC, ANY_SPEC], out_specs=tuple([HBM_SPEC] * na),
        input_output_aliases={w: w for w in range(na)},
        compiler_params=pltpu.CompilerParams(has_side_effects=SPLIT_EFFECT))(*arrays, handle["send"], handle["recv"], after)
    return (list(out[n:]), list(out[:n])) if exchange else list(out)


def _sibling_copy(received, sent, land, k, me, peers, sibling, send_sems, recv_sems, index):
    slot = me if k == 3 else 2 * peers[k][0] + peers[k][1]
    src = sent if k == 3 else received
    return pltpu.make_async_remote_copy(
        src_ref=src.at[slot], dst_ref=land.at[slot], send_sem=send_sems.at[4 * index + k],
        recv_sem=recv_sems.at[4 * index + k], device_id=sibling, device_id_type=MESH)


def _sibling_start(received, sent, after, name):
    n = len(received)
    lands = [lax.empty(a.shape, a.dtype) for a in received]
    arrays = list(received) + list(sent) + lands

    def body(*refs):
        rec, snt, land = refs[:n], refs[n:2 * n], refs[2 * n:3 * n]
        send_sems, recv_sems = refs[3 * n + 1], refs[3 * n + 2]
        token = refs[-1]
        x, y, c, peers = _chip_peers()
        for w in range(n):
            for k in range(4):
                _sibling_copy(rec[w], snt[w], land[w], k, 2 * x + y, peers, (x, y, 1 - c), send_sems, recv_sems, w).start()
        token[...] = jnp.zeros_like(token)

    hbm = [pltpu.with_memory_space_constraint(a, pltpu.HBM) for a in arrays]
    out = pl.pallas_call(
        body, name=name,
        out_shape=(pltpu.SemaphoreType.DMA((4 * n,)), pltpu.SemaphoreType.DMA((4 * n,)),
                   *[pltpu.HBM(a.shape, a.dtype) for a in hbm], jax.ShapeDtypeStruct((8, LANE), F32)),
        in_specs=[HBM_SPEC] * (3 * n) + [ANY_SPEC],
        out_specs=(SEM_SPEC, SEM_SPEC, *[HBM_SPEC] * (3 * n), pl.BlockSpec(memory_space=pltpu.VMEM)),
        input_output_aliases={w: 2 + w for w in range(3 * n)},
        compiler_params=pltpu.CompilerParams(has_side_effects=SPLIT_EFFECT))(*hbm, after)
    return dict(send=out[0], recv=out[1], arrays=out[2:2 + 3 * n], n=n, token=out[-1])


def _sibling_wait(handle, after, name):
    n = handle["n"]
    arrays = list(handle["arrays"])

    def body(*refs):
        rec, snt, land = refs[:n], refs[n:2 * n], refs[2 * n:3 * n]
        send_sems, recv_sems = refs[3 * n], refs[3 * n + 1]
        x, y, c, peers = _chip_peers()
        for w in range(n):
            for k in range(4):
                cp = _sibling_copy(rec[w], snt[w], land[w], k, 2 * x + y, peers, (x, y, 1 - c), send_sems, recv_sems, w)
                cp.wait_send()
                cp.wait_recv()

    out = pl.pallas_call(
        body, name=name, out_shape=tuple(pltpu.HBM(a.shape, a.dtype) for a in arrays),
        in_specs=[HBM_SPEC] * (3 * n) + [SEM_SPEC, SEM_SPEC, ANY_SPEC], out_specs=tuple([HBM_SPEC] * (3 * n)),
        input_output_aliases={w: w for w in range(3 * n)},
        compiler_params=pltpu.CompilerParams(has_side_effects=SPLIT_EFFECT))(*arrays, handle["send"], handle["recv"], after)
    return list(out[:n]), list(out[n:2 * n]), list(out[2 * n:])


def _all_reduce_small(v):
    rows = v.shape[0]

    def body(v_ref, sum_ref, slots, send_sems, recv_sems):
        x, y, c = lax.axis_index("x"), lax.axis_index("y"), lax.axis_index("c")
        me = 4 * x + 2 * y + c
        slots[me] = v_ref[...]
        sends = []
        for k in range(1, N_DEV):
            bx, by, bc = (k >> 2) & 1, (k >> 1) & 1, k & 1
            peer = (x ^ bx, y ^ by, c ^ bc)
            rc = pltpu.make_async_remote_copy(src_ref=v_ref, dst_ref=slots.at[me], send_sem=send_sems.at[k],
                                              recv_sem=recv_sems.at[k], device_id=peer, device_id_type=MESH)
            rc.start()
            sends.append(rc)
        for k in range(1, N_DEV):
            bx, by, bc = (k >> 2) & 1, (k >> 1) & 1, k & 1
            src = 4 * (x ^ bx) + 2 * (y ^ by) + (c ^ bc)
            pltpu.make_async_remote_copy(src_ref=v_ref, dst_ref=slots.at[src], send_sem=send_sems.at[k],
                                         recv_sem=recv_sems.at[k], device_id=(x ^ bx, y ^ by, c ^ bc),
                                         device_id_type=MESH).wait_recv()
        for rc in sends:
            rc.wait_send()
        total = slots[0]
        for k in range(1, N_DEV):
            total = total + slots[k]
        sum_ref[...] = total

    vm = pl.BlockSpec(memory_space=pltpu.VMEM)
    return pl.pallas_call(
        body, out_shape=jax.ShapeDtypeStruct((rows, LANE), F32), in_specs=[vm], out_specs=vm,
        scratch_shapes=[pltpu.VMEM((N_DEV, rows, LANE), F32), pltpu.SemaphoreType.DMA((N_DEV,)),
                        pltpu.SemaphoreType.DMA((N_DEV,))], name="all_reduce_small")(v)


def _as_2d(a):
    return a.reshape(-1, a.shape[-1])


def _row_tile(rows, cols):
    for t in (512, 256, 128, 64, 32, 16):
        if rows % t == 0 and t * cols * 4 <= (1 << 20):
            return t
    return rows


def _adamw_weight(w, m, v, received, sent, sibling):
    layers = len(received)
    _, rows, cols = received[0].shape
    tr = _row_tile(rows, cols)
    by_columns = rows % tr != 0 or tr == rows and rows * cols * 4 > (2 << 20)
    if by_columns:
        assert layers == 1 and cols % (2 * LANE) == 0, (w.shape, received[0].shape)
        tr, tc, steps = rows, cols // 2, 2
        index = lambda i: (0, i)
    else:
        tc, steps = cols, rows // tr
        index = lambda i: (i, 0)
    where = (2 * lax.axis_index("x") + lax.axis_index("y")).astype(jnp.int32).reshape(1)

    def body(where_ref, w_ref, m_ref, v_ref, *rest):
        per_layer, (g_ref, d_ref, nm_ref, nv_ref) = rest[:3 * layers], rest[3 * layers:]
        me = where_ref[0]
        for layer in range(layers):
            r_ref, own_ref, s_ref = per_layer[3 * layer:3 * layer + 3]

            @pl.when(pl.program_id(0) == layer)
            def _():
                mine = theirs = None
                for k in range(N_CHIPS):
                    a = jnp.where(me == k, own_ref[...], r_ref[k]).astype(F32)
                    b = s_ref[k].astype(F32)
                    mine = a if mine is None else mine + a
                    theirs = b if theirs is None else theirs + b
                g = mine + theirs
                delta, nm, nv = _adamw_math(w_ref[...], g, m_ref[...], v_ref[...])
                g_ref[...] = g
                d_ref[...] = delta
                nm_ref[...] = nm
                nv_ref[...] = nv

    def held(layer, now, i):
        return jnp.where(now < layer, 0, jnp.where(now > layer, steps - 1, i))

    if by_columns:
        stacked = pl.BlockSpec((tr, tc), lambda now, i, where_ref: index(i))
    else:
        stacked = pl.BlockSpec((tr, tc), lambda now, i, where_ref: (now * steps + i, 0))
    in_specs = [stacked, stacked, stacked]
    args = [where, w, m, v]
    for layer in range(layers):
        four = pl.BlockSpec((N_CHIPS, tr, tc), lambda now, i, where_ref, layer=layer: (0,) + index(held(layer, now, i)))
        own = pl.BlockSpec((None, tr, tc),
                           lambda now, i, where_ref, layer=layer: (where_ref[0],) + index(held(layer, now, i)))
        in_specs += [four, own, four]
        args += [received[layer], sent[layer], sibling[layer]]
    grid_spec = pltpu.PrefetchScalarGridSpec(num_scalar_prefetch=1, grid=(layers, steps), in_specs=in_specs,
                                             out_specs=[stacked] * 4)
    return pl.pallas_call(body, out_shape=[jax.ShapeDtypeStruct(w.shape, F32)] * 4, grid_spec=grid_spec,
                          name="adamw_weight", compiler_params=_params(("arbitrary", "arbitrary")))(*args)


def _adamw_math(w, g, m, v):
    m = ADAM_B1 * m + (1.0 - ADAM_B1) * g
    v = ADAM_B2 * v + (1.0 - ADAM_B2) * (g * g)
    m_hat = m * (1.0 / (1.0 - ADAM_B1 ** ADAM_STEP))
    v_hat = v * (1.0 / (1.0 - ADAM_B2 ** ADAM_STEP))
    denom = jnp.sqrt(v_hat) + ADAM_EPS
    inv = pl.reciprocal(denom, approx=True)
    inv = inv * (2.0 - denom * inv)
    delta = -ADAM_LR * (m_hat * inv + ADAM_WD * w)
    return delta, m, v


def _adamw(w, m, v, g_mine, g_sibling):
    rows, cols = w.shape
    tr = _row_tile(rows, cols)
    two = g_sibling is not None

    def body(*refs):
        if two:
            w_ref, m_ref, v_ref, ga_ref, gb_ref, g_ref, d_ref, nm_ref, nv_ref = refs
            g = ga_ref[...] + gb_ref[...]
        else:
            w_ref, m_ref, v_ref, ga_ref, g_ref, d_ref, nm_ref, nv_ref = refs
            g = ga_ref[...]
        delta, nm, nv = _adamw_math(w_ref[...], g, m_ref[...], v_ref[...])
        g_ref[...] = g
        d_ref[...] = delta
        nm_ref[...] = nm
        nv_ref[...] = nv

    blk = pl.BlockSpec((tr, cols), lambda i: (i, 0))
    args = [w, m, v, g_mine] + ([g_sibling] if two else [])
    return pl.pallas_call(body, out_shape=[jax.ShapeDtypeStruct((rows, cols), F32)] * 4, grid=(rows // tr,),
                          in_specs=[blk] * len(args), out_specs=[blk] * 4, name="adamw",
                          compiler_params=_params(("parallel",)))(*args)


def _pack_rows(arrays):
    flat = jnp.concatenate([a.reshape(-1) for a in arrays])
    rows = -(-flat.shape[0] // (8 * LANE)) * 8
    return jnp.pad(flat, (0, rows * LANE - flat.shape[0])).reshape(rows, LANE)


def _unpack_rows(packed, shapes):
    flat = packed.reshape(-1)
    out, at = [], 0
    for s in shapes:
        size = math.prod(s)
        out.append(flat[at:at + size].reshape(s))
        at += size
    return out


def kernel(x, p, positions, norm_g, ffn_w_in, ffn_w_out, ple_w_proj, ple_w_gate, rel_bias, mla_w_a, mla_q_norm, mla_kv_norm, mla_w_uq, mla_w_ukv, mla_w_o, dil_w_qkv, dil_w_o, fox_w_qkvf, fox_b_f, fox_w_o, loss_target, m_norm_g, m_ffn_w_in, m_ffn_w_out, m_ple_w_proj, m_ple_w_gate, m_rel_bias, m_mla_w_a, m_mla_q_norm, m_mla_kv_norm, m_mla_w_uq, m_mla_w_ukv, m_mla_w_o, m_dil_w_qkv, m_dil_w_o, m_fox_w_qkvf, m_fox_b_f, m_fox_w_o, v_norm_g, v_ffn_w_in, v_ffn_w_out, v_ple_w_proj, v_ple_w_gate, v_rel_bias, v_mla_w_a, v_mla_q_norm, v_mla_kv_norm, v_mla_w_uq, v_mla_w_ukv, v_mla_w_o, v_dil_w_qkv, v_dil_w_o, v_fox_w_qkvf, v_fox_b_f, v_fox_w_o):
    w = dict(norm_g=norm_g, ffn_w_in=ffn_w_in, ffn_w_out=ffn_w_out, ple_w_proj=ple_w_proj, ple_w_gate=ple_w_gate,
             rel_bias=rel_bias, mla_w_a=mla_w_a, mla_q_norm=mla_q_norm, mla_kv_norm=mla_kv_norm, mla_w_uq=mla_w_uq,
             mla_w_ukv=mla_w_ukv, mla_w_o=mla_w_o, dil_w_qkv=dil_w_qkv, dil_w_o=dil_w_o, fox_w_qkvf=fox_w_qkvf,
             fox_b_f=fox_b_f, fox_w_o=fox_w_o)
    m = dict(norm_g=m_norm_g, ffn_w_in=m_ffn_w_in, ffn_w_out=m_ffn_w_out, ple_w_proj=m_ple_w_proj,
             ple_w_gate=m_ple_w_gate, rel_bias=m_rel_bias, mla_w_a=m_mla_w_a, mla_q_norm=m_mla_q_norm,
             mla_kv_norm=m_mla_kv_norm, mla_w_uq=m_mla_w_uq, mla_w_ukv=m_mla_w_ukv, mla_w_o=m_mla_w_o,
             dil_w_qkv=m_dil_w_qkv, dil_w_o=m_dil_w_o, fox_w_qkvf=m_fox_w_qkvf, fox_b_f=m_fox_b_f, fox_w_o=m_fox_w_o)
    v = dict(norm_g=v_norm_g, ffn_w_in=v_ffn_w_in, ffn_w_out=v_ffn_w_out, ple_w_proj=v_ple_w_proj,
             ple_w_gate=v_ple_w_gate, rel_bias=v_rel_bias, mla_w_a=v_mla_w_a, mla_q_norm=v_mla_q_norm,
             mla_kv_norm=v_mla_kv_norm, mla_w_uq=v_mla_w_uq, mla_w_ukv=v_mla_w_ukv, mla_w_o=v_mla_w_o,
             dil_w_qkv=v_dil_w_qkv, dil_w_o=v_dil_w_o, fox_w_qkvf=v_fox_w_qkvf, fox_b_f=v_fox_b_f, fox_w_o=v_fox_w_o)
    chip = 2 * lax.axis_index("x") + lax.axis_index("y")
    for tree in (w, m, v):
        tree[TRANSPOSED] = jnp.swapaxes(tree[TRANSPOSED], 1, 2)

    small_shapes = [w[k].shape for k in SMALL_SHARDED]
    order = [(i, part) for i in range(DEPTH) for part in (MIXER_PART, COMMON_PART) if _part_names(i, part)]
    gathers = {}
    after = positions
    zero = 0.0
    for i, part in order:
        bufs = [_own_slot((w[k][_layer_slot(k, i)] + zero).astype(BF)) for k in _part_names(i, part)]
        if (i, part) == order[0]:
            bufs.append(_own_slot(_pack_rows([w[k] for k in SMALL_SHARDED])))
        gathers[i, part] = _spread_start(bufs, None, after, f"gather_start_{i}_{part}")
        after = gathers[i, part]["token"]
        if (i, part) == order[0]:
            zero = after[0, 0]
    all_started = after
    state = {}

    def get_part(i, part, after_array):
        is_first = (i, part) == order[0]
        lands = _spread_wait(gathers[i, part], all_started if is_first else after_array, f"gather_wait_{i}_{part}")
        if is_first:
            pieces = [_unpack_rows(lands[-1][k], small_shapes) for k in range(N_CHIPS)]
            small = {name: jnp.concatenate([pieces[k][idx] for k in range(N_CHIPS)], axis=-1)
                     for idx, name in enumerate(SMALL_SHARDED)}
            state["small"] = dict(small, rel_bias=rel_bias, fox_b_f=fox_b_f)
        chunks = dict(zip(_part_names(i, part), lands))
        state[i, part] = {k: a.shape for k, a in chunks.items()}
        return _part_to_compute(i, part, chunks)

    started, forwards = [], {}

    def forward_oldest(after_array):
        i, part, handle = started.pop(0)
        received, sent = _spread_wait(handle, after_array, f"exchange_wait_{i}_{part}")
        forwards[i, part] = _sibling_start(received, sent, after_array, f"sibling_start_{i}_{part}")
        return forwards[i, part]["token"]

    def put_part(i, part, lg):
        contrib = _part_contributions(i, part, lg, state[i, part])
        srcs = [contrib[k] for k in _part_names(i, part)]
        handle = _spread_start([lax.empty(s.shape, s.dtype) for s in srcs], srcs, positions,
                               f"exchange_start_{i}_{part}")
        token = handle["token"]
        if started:
            token = token + forward_oldest(token)
        started.append((i, part, handle))
        return token

    sq, grad_x, sg = _run_layers(x[0], p[:, 0], positions[0], loss_target[0], get_part, lambda: state["small"],
                                 put_part)
    loss = lax.psum(0.5 / D_MODEL * jnp.sum(sq), ("x", "y", "c"))
    forward_oldest(grad_x)

    held = {k: {} for k in BIG}
    for i, part in sorted(forwards, reverse=True):
        received, sent, sibling = _sibling_wait(forwards[i, part], grad_x, f"sibling_wait_{i}_{part}")
        for k, r, s, t in zip(_part_names(i, part), received, sent, sibling):
            held[k][_layer_slot(k, i)] = (r, s, t)
    results = {}
    for k in BIG:
        per_layer = [held[k][slot] for slot in sorted(held[k])]
        outs = _adamw_weight(_as_2d(w[k]), _as_2d(m[k]), _as_2d(v[k]), *[list(col) for col in zip(*per_layer)])
        results[k] = [o.reshape(w[k].shape) for o in outs]
    results[TRANSPOSED] = [jnp.swapaxes(o, 1, 2) for o in results[TRANSPOSED]]

    small_all = SMALL_SHARDED + SMALL_REPLICATED
    full_shapes = [sg[k].shape for k in small_all]
    reduced = dict(zip(small_all, _unpack_rows(_all_reduce_small(_pack_rows([sg[k] for k in small_all])), full_shapes)))
    local_g = []
    for k in small_all:
        g = reduced[k]
        if k in SMALL_SHARDED:
            width = w[k].shape[-1]
            g = lax.dynamic_slice_in_dim(g, chip * width, width, axis=g.ndim - 1)
        local_g.append(g)
    local_shapes = [w[k].shape for k in small_all]
    outs = _adamw(_pack_rows([w[k] for k in small_all]), _pack_rows([m[k] for k in small_all]),
                  _pack_rows([v[k] for k in small_all]), _pack_rows(local_g), None)
    unpacked = [_unpack_rows(o, local_shapes) for o in outs]
    for idx, k in enumerate(small_all):
        results[k] = [u[idx] for u in unpacked]

    return (loss, grad_x[None], *[results[k][0] for k in WEIGHTS], *[results[k][1] for k in WEIGHTS],
            *[results[k][2] for k in WEIGHTS], *[results[k][3] for k in WEIGHTS])
```

```python
import functools
import math

import jax
import jax.numpy as jnp
from jax import lax
from jax.experimental import pallas as pl
from jax.experimental.pallas import tpu as pltpu

F32 = jnp.float32
BF = jnp.bfloat16
MESH = pl.DeviceIdType.MESH
HBM_SPEC = pl.BlockSpec(memory_space=pltpu.HBM)

D_MODEL = 1024
DEPTH = 4
N_MIXERS = 3
D_FF = 2816
NORM_EPS = 1e-6
NEG_INF = -1e30
LANE = 128
HEADS = 16
HEAD_DIM = 64
MLA_Q_RANK = 384
MLA_KV_RANK = 256
MLA_ROPE = 32
MLA_A_PAD = 768
ROPE_THETA = 10000.0
DIL_PATTERNS = ((128, 1), (512, 4), (2048, 16))
Q_BLOCK = 128
DIL_PAIRS = {1: 2, 4: 4, 16: 4}
REL_BUCKETS = 32
REL_MAX_DIST = 2048
N_CHIPS = 4
N_DEV = 8

ADAM_LR = 0.001
ADAM_B1 = 0.9
ADAM_B2 = 0.999
ADAM_EPS = 1e-08
ADAM_WD = 0.01
ADAM_STEP = 10

VMEM_LIMIT = 56 * 1024 * 1024
MATMUL_VMEM_BUDGET = 36 * 1024 * 1024
ROW_TILE = 512
ATTN_TILE = 256
ATTN_Q_TILE = 512
ATTN_FORWARD_KEY_TILE = 512
ATTN_BACKWARD_Q_TILE = 512
MLA_GROUP = 4
FOX_GROUP = 2


def _params(sem=None):
    return pltpu.CompilerParams(dimension_semantics=sem, vmem_limit_bytes=VMEM_LIMIT)


def _divisor_tiles(dim):
    tiles = [t for t in range(LANE, dim + 1, LANE) if dim % t == 0]
    return tiles or [dim]


def _matmul_tiles(m, n, k, a_bytes, b_bytes, out_bytes, has_add, n_unit=None, k_unit=None):
    best = None
    for tm in _divisor_tiles(m):
        for tn in _divisor_tiles(n_unit or n):
            for tk in _divisor_tiles(k_unit or k):
                if max(tm, tn, tk) > 2048:
                    continue
                vmem = 2 * (tm * tk * a_bytes + tk * tn * b_bytes + tm * tn * out_bytes) + tm * tn * 4
                if has_add:
                    vmem += 2 * tm * tn * 4
                if vmem > MATMUL_VMEM_BUDGET:
                    continue
                steps = (m // tm) * (n // tn) * (k // tk)
                traffic = m * k * a_bytes * (n // tn) + k * n * b_bytes * (m // tm) + m * n * out_bytes
                cost = traffic / 3.0e12 + steps * 0.4e-6
                if best is None or cost < best[0]:
                    best = (cost, tm, tn, tk)
    return best[1:]


def _matmul(a, b, *, ta=False, tb=False, b_chunks=False, out_chunks=False, add=None, out_dtype=F32, name):
    k, m = a.shape if ta else a.shape[::-1]
    n_unit = k_unit = None
    if b_chunks:
        chunks, rows_w, c = b.shape
        if tb:
            kb, n, k_unit = chunks * c, rows_w, c
        else:
            kb, n, n_unit = rows_w, chunks * c, c
    else:
        kb, n = b.shape[::-1] if tb else b.shape
    if out_chunks:
        assert n % N_CHIPS == 0 and add is None
        n_unit = n // N_CHIPS
    assert k == kb, (a.shape, b.shape, ta, tb)
    tm, tn, tk = _matmul_tiles(m, n, k, a.dtype.itemsize, b.dtype.itemsize, jnp.dtype(out_dtype).itemsize,
                               add is not None, n_unit, k_unit)
    nk = k // tk
    dims = (((0 if ta else 1,), (1 if tb else 0,)), ((), ()))

    def body(*refs):
        if add is None:
            a_ref, b_ref, o_ref, acc_ref = refs
            add_ref = None
        else:
            a_ref, b_ref, add_ref, o_ref, acc_ref = refs
        kk = pl.program_id(2)

        @pl.when(kk == 0)
        def _():
            acc_ref[...] = jnp.zeros_like(acc_ref)

        acc_ref[...] += lax.dot_general(a_ref[...].astype(BF), b_ref[...].astype(BF), dims,
                                        preferred_element_type=F32)

        @pl.when(kk == nk - 1)
        def _():
            r = acc_ref[...]
            if add_ref is not None:
                r = r + add_ref[...].astype(F32)
            o_ref[...] = r.astype(out_dtype)

    a_spec = pl.BlockSpec((tk, tm), lambda i, j, q: (q, i)) if ta else pl.BlockSpec((tm, tk), lambda i, j, q: (i, q))
    if b_chunks and tb:
        per_k = k_unit // tk
        b_spec = pl.BlockSpec((None, tn, tk), lambda i, j, q: (q // per_k, j, q % per_k))
    elif b_chunks:
        per_n = n_unit // tn
        b_spec = pl.BlockSpec((None, tk, tn), lambda i, j, q: (j // per_n, q, j % per_n))
    elif tb:
        b_spec = pl.BlockSpec((tn, tk), lambda i, j, q: (j, q))
    else:
        b_spec = pl.BlockSpec((tk, tn), lambda i, j, q: (q, j))
    if out_chunks:
        per_o = n_unit // tn
        o_spec = pl.BlockSpec((None, tm, tn), lambda i, j, q: (j // per_o, i, j % per_o))
        out_shape = jax.ShapeDtypeStruct((N_CHIPS, m, n_unit), out_dtype)
    else:
        o_spec = pl.BlockSpec((tm, tn), lambda i, j, q: (i, j))
        out_shape = jax.ShapeDtypeStruct((m, n), out_dtype)
    in_specs = [a_spec, b_spec]
    args = [a, b]
    if add is not None:
        in_specs.append(o_spec)
        args.append(add)
    return pl.pallas_call(
        body, out_shape=out_shape, grid=(m // tm, n // tn, nk),
        in_specs=in_specs, out_specs=o_spec, scratch_shapes=[pltpu.VMEM((tm, tn), F32)], name=name,
        compiler_params=_params(("parallel", "parallel", "arbitrary")))(*args)


def _rowwise(body, name, rows, ins, outs, tr=ROW_TILE):
    def row_spec(cols):
        return pl.BlockSpec((tr, cols), lambda i: (i, 0))

    def full_spec(shape):
        zeros = (0,) * len(shape)
        return pl.BlockSpec(shape, lambda i: zeros)

    in_specs = [row_spec(a.shape[1]) if kind == "row" else full_spec(a.shape) for a, kind in ins]
    out_specs = [row_spec(shape[1]) if kind == "row" else full_spec(shape) for shape, _, kind in outs]
    out_shape = [jax.ShapeDtypeStruct(shape, dtype) for shape, dtype, _ in outs]
    return pl.pallas_call(body, out_shape=out_shape, grid=(rows // tr,), in_specs=in_specs, out_specs=out_specs,
                          name=name, compiler_params=_params(("arbitrary",)))(*[a for a, _ in ins])


def _rstd(x):
    return lax.rsqrt(jnp.mean(x * x, axis=-1, keepdims=True) + NORM_EPS)


def _rms_bwd_math(x, g, dy):
    r = _rstd(x)
    gd = dy * g
    dx = r * gd - x * (r * r * r) * jnp.mean(gd * x, axis=-1, keepdims=True)
    dg = jnp.sum(dy * x * r, axis=0, keepdims=True)
    return dx, dg


def _sigmoid(x):
    return 0.5 * jnp.tanh(0.5 * x) + 0.5


def _init_acc(*refs):
    @pl.when(pl.program_id(0) == 0)
    def _():
        for r in refs:
            r[...] = jnp.zeros_like(r)


def _prenorm(h, g):
    rows, cols = h.shape

    def body(h_ref, g_ref, o_ref):
        x = h_ref[...]
        o_ref[...] = (x * _rstd(x) * g_ref[...]).astype(BF)

    return _rowwise(body, "prenorm", rows, [(h, "row"), (g, "full")], [((rows, cols), BF, "row")])[0]


def _post_residual(h, y, g_post, g_pre):
    rows, cols = h.shape
    with_pre = g_pre is not None

    def body(*refs):
        if with_pre:
            h_ref, y_ref, gp_ref, gq_ref, hn_ref, hb_ref = refs
        else:
            h_ref, y_ref, gp_ref, hn_ref, hb_ref = refs
        yv = y_ref[...]
        hn = h_ref[...] + yv * _rstd(yv) * gp_ref[...]
        hn_ref[...] = hn
        hb_ref[...] = (hn * _rstd(hn) * gq_ref[...] if with_pre else hn).astype(BF)

    ins = [(h, "row"), (y, "row"), (g_post, "full")] + ([(g_pre, "full")] if with_pre else [])
    return _rowwise(body, "post_residual_pre" if with_pre else "post_residual", rows, ins,
                    [((rows, cols), F32, "row"), ((rows, cols), BF, "row")])


def _ple_forward(h2, pp, z, g_pre):
    rows, cols = h2.shape

    def body(h_ref, p_ref, z_ref, g_ref, h3_ref, hb_ref):
        h3 = h_ref[...] + p_ref[...] * _sigmoid(z_ref[...])
        h3_ref[...] = h3
        hb_ref[...] = (h3 * _rstd(h3) * g_ref[...]).astype(BF)

    return _rowwise(body, "ple_forward", rows, [(h2, "row"), (pp, "row"), (z, "row"), (g_pre, "full")],
                    [((rows, cols), F32, "row"), ((rows, cols), BF, "row")])


def _ple_loss(h2, pp, z, target):
    rows, cols = h2.shape

    def body(h_ref, p_ref, z_ref, t_ref, dh_ref, sq_ref):
        _init_acc(sq_ref)
        err = h_ref[...] + p_ref[...] * _sigmoid(z_ref[...]) - t_ref[...]
        dh_ref[...] = err * (1.0 / cols)
        sq_ref[...] += jnp.sum(err * err, axis=0, keepdims=True)

    return _rowwise(body, "ple_loss", rows, [(h2, "row"), (pp, "row"), (z, "row"), (target, "row")],
                    [((rows, cols), F32, "row"), ((1, cols), F32, "acc")])


def _ple_backward(dh3, pp, z):
    rows, cols = dh3.shape

    def body(d_ref, p_ref, z_ref, dpp_ref, dz_ref):
        d = d_ref[...]
        s = _sigmoid(z_ref[...])
        dpp_ref[...] = (d * s).astype(BF)
        dz_ref[...] = (d * p_ref[...] * s * (1.0 - s)).astype(BF)

    return _rowwise(body, "ple_backward", rows, [(dh3, "row"), (pp, "row"), (z, "row")],
                    [((rows, cols), BF, "row"), ((rows, cols), BF, "row")])


def _rms_backward(x, g, dy, add, out_dtype):
    rows, cols = x.shape
    with_add = add is not None

    def body(*refs):
        if with_add:
            x_ref, g_ref, dy_ref, add_ref, dx_ref, dg_ref = refs
        else:
            x_ref, g_ref, dy_ref, dx_ref, dg_ref = refs
        _init_acc(dg_ref)
        dx, dg = _rms_bwd_math(x_ref[...], g_ref[...], dy_ref[...].astype(F32))
        if with_add:
            dx = dx + add_ref[...]
        dx_ref[...] = dx.astype(out_dtype)
        dg_ref[...] += dg

    ins = [(x, "row"), (g, "full"), (dy, "row")] + ([(add, "row")] if with_add else [])
    return _rowwise(body, "rms_backward_add" if with_add else "rms_backward", rows, ins,
                    [((rows, cols), out_dtype, "row"), ((1, cols), F32, "acc")])


def _swiglu_forward(gu):
    rows = gu.shape[0]

    def body(gu_ref, o_ref):
        g = gu_ref[:, :D_FF].astype(F32)
        o_ref[...] = (g * _sigmoid(g) * gu_ref[:, D_FF:].astype(F32)).astype(BF)

    return _rowwise(body, "swiglu_forward", rows, [(gu, "row")], [((rows, D_FF), BF, "row")])[0]


def _swiglu_backward(gu, dact):
    rows = gu.shape[0]

    def body(gu_ref, d_ref, o_ref):
        g = gu_ref[:, :D_FF].astype(F32)
        u = gu_ref[:, D_FF:].astype(F32)
        d = d_ref[...].astype(F32)
        s = _sigmoid(g)
        gs = g * s
        o_ref[:, :D_FF] = (d * u * (s + gs * (1.0 - s))).astype(BF)
        o_ref[:, D_FF:] = (d * gs).astype(BF)

    return _rowwise(body, "swiglu_backward", rows, [(gu, "row"), (dact, "row")], [((rows, 2 * D_FF), BF, "row")])[0]


def _rope_tables(positions):
    half = MLA_ROPE // 2
    inv = ROPE_THETA ** (-jnp.arange(half, dtype=F32) / half)
    ang = positions.astype(F32)[:, None] * inv
    cos, sin = jnp.cos(ang), jnp.sin(ang)
    rows = positions.shape[0]
    c = jnp.ones((rows, LANE), F32).at[:, 64:80].set(cos).at[:, 80:96].set(cos)
    sa = jnp.zeros((rows, LANE), F32).at[:, 64:80].set(-sin)
    sb = jnp.zeros((rows, LANE), F32).at[:, 80:96].set(sin)
    return c, sa, sb


def _rope_apply(x, c, sa, sb):
    return x * c + pltpu.roll(x, LANE - 16, 1) * sa + pltpu.roll(x, 16, 1) * sb


def _rope_apply_t(dy, c, sa, sb):
    return dy * c + pltpu.roll(dy * sa, 16, 1) + pltpu.roll(dy * sb, LANE - 16, 1)


def _rope_heads(x, tables, transpose, name):
    rows, cols = x.shape

    def body(x_ref, c_ref, sa_ref, sb_ref, o_ref):
        fn = _rope_apply_t if transpose else _rope_apply
        c, sa, sb = c_ref[...], sa_ref[...], sb_ref[...]
        for head in range(cols // LANE):
            lanes = slice(head * LANE, (head + 1) * LANE)
            o_ref[:, lanes] = fn(x_ref[:, lanes].astype(F32), c, sa, sb).astype(BF)

    blk = pl.BlockSpec((ROW_TILE, cols), lambda i: (i, 0))
    tbl = pl.BlockSpec((ROW_TILE, LANE), lambda i: (i, 0))
    return pl.pallas_call(body, out_shape=jax.ShapeDtypeStruct((rows, cols), BF), grid=(rows // ROW_TILE,),
                          in_specs=[blk, tbl, tbl, tbl], out_specs=blk, name=name,
                          compiler_params=_params(("parallel",)))(x, *tables)


def _mla_mid_forward(a, q_norm, kv_norm, tables):
    rows = a.shape[0]
    qr, kvr = MLA_Q_RANK, MLA_KV_RANK

    def body(a_ref, qn_ref, kn_ref, c_ref, sa_ref, sb_ref, cq_ref, ckv_ref, kr_ref):
        aq = a_ref[:, 0:qr]
        akv = a_ref[:, qr:qr + kvr]
        cq_ref[...] = (aq * _rstd(aq) * qn_ref[...]).astype(BF)
        ckv_ref[...] = (akv * _rstd(akv) * kn_ref[...]).astype(BF)
        kr_ref[...] = _rope_apply(a_ref[:, qr + kvr:], c_ref[...], sa_ref[...], sb_ref[...]).astype(BF)

    ins = [(a, "row"), (q_norm, "full"), (kv_norm, "full")] + [(t, "row") for t in tables]
    return _rowwise(body, "mla_mid_forward", rows, ins,
                    [((rows, qr), BF, "row"), ((rows, kvr), BF, "row"), ((rows, LANE), BF, "row")])


def _mla_mid_backward(a, q_norm, kv_norm, tables, dcq, dckv, dkr):
    rows = a.shape[0]
    qr, kvr = MLA_Q_RANK, MLA_KV_RANK

    def body(a_ref, qn_ref, kn_ref, c_ref, sa_ref, sb_ref, dcq_ref, dckv_ref, dkr_ref, da_ref, dqn_ref, dkn_ref):
        _init_acc(dqn_ref, dkn_ref)
        dxq, dgq = _rms_bwd_math(a_ref[:, 0:qr], qn_ref[...], dcq_ref[...])
        dxk, dgk = _rms_bwd_math(a_ref[:, qr:qr + kvr], kn_ref[...], dckv_ref[...])
        da_ref[:, 0:qr] = dxq.astype(BF)
        da_ref[:, qr:qr + kvr] = dxk.astype(BF)
        da_ref[:, qr + kvr:] = _rope_apply_t(dkr_ref[...], c_ref[...], sa_ref[...], sb_ref[...]).astype(BF)
        dqn_ref[...] += dgq
        dkn_ref[...] += dgk

    ins = ([(a, "row"), (q_norm, "full"), (kv_norm, "full")] + [(t, "row") for t in tables]
           + [(dcq, "row"), (dckv, "row"), (dkr, "row")])
    return _rowwise(body, "mla_mid_backward", rows, ins,
                    [((rows, MLA_A_PAD), BF, "row"), ((1, qr), F32, "acc"), ((1, kvr), F32, "acc")])


def _attn_specs(rows, kv_off, g):
    head = pl.BlockSpec((rows, g * LANE), lambda h: (0, h))
    kv_head = pl.BlockSpec((rows, g * LANE), lambda h: (0, h + kv_off // g))
    shared = pl.BlockSpec((rows, LANE), lambda h: (0, 0))
    col_vec = pl.BlockSpec((g, rows, 1), lambda h: (h, 0, 0))
    row_vec = pl.BlockSpec((g, 1, rows), lambda h: (h, 0, 0))
    return head, kv_head, shared, col_vec, row_vec


def _attn_forward(q, kv, kv_off, kr, cum_col, cum_row, scale, group_size, name):
    rows = q.shape[0]
    heads = HEADS
    t = ATTN_FORWARD_KEY_TILE
    tq = ATTN_Q_TILE
    per = tq // t
    has_kr = kr is not None
    has_f = cum_col is not None
    group = range(group_size)

    def body(*refs):
        it = iter(refs)
        q_ref, kv_ref = next(it), next(it)
        kr_ref = next(it) if has_kr else None
        cc_ref = next(it) if has_f else None
        cr_ref = next(it) if has_f else None
        o_ref, lse_ref = next(it), next(it)
        lo = lax.broadcasted_iota(jnp.int32, (1, LANE), 1) < HEAD_DIM
        row = lax.broadcasted_iota(jnp.int32, (tq, t), 0)
        col = lax.broadcasted_iota(jnp.int32, (tq, t), 1)
        lanes = [slice(g * LANE, (g + 1) * LANE) for g in group]

        def q_block(i, _):
            qs = pl.ds(pl.multiple_of(i * tq, tq), tq)
            qbs = [q_ref[qs, lanes[g]] for g in group]
            cqs = [cc_ref[g, qs, :] if has_f else None for g in group]

            def step(j, carry, diag):
                ks = pl.ds(pl.multiple_of(j * t, t), t)
                skip = diag * t if diag and has_f else 0
                other = kr_ref[ks, :] if has_kr else jnp.zeros((t, LANE), BF)
                kvbs = [kv_ref[ks, lanes[g]] for g in group]

                def logit(g):
                    return lax.dot_general(qbs[g][skip:], jnp.where(lo, kvbs[g], other), (((1,), (1,)), ((), ())),
                                           preferred_element_type=F32)

                logits = {g: logit(g) for g in (group if has_f else group[:1])}
                out = []
                for g in group:
                    m, l, acc = (a[skip:] for a in carry[g])
                    if not has_f and g + 1 < len(group):
                        logits[g + 1] = logit(g + 1)
                    s = logits[g] * scale
                    if has_f:
                        s = s + (cqs[g][skip:] - cr_ref[g, :, ks])
                    if diag is not None:
                        s = jnp.where(col[skip:] + diag * t <= row[skip:], s, NEG_INF)
                    mn = jnp.maximum(m, jnp.max(s, axis=1, keepdims=True))
                    alpha = jnp.exp(m - mn)
                    p = jnp.exp(s - mn)
                    l = alpha * l + jnp.sum(p, axis=1, keepdims=True)
                    acc = alpha * acc + jnp.dot(p.astype(BF), kvbs[g], preferred_element_type=F32)
                    new = (mn, l, acc)
                    if skip:
                        new = tuple(jnp.concatenate([old[:skip], a], axis=0) for old, a in zip(carry[g], new))
                    out.append(new)
                return tuple(out)

            init = tuple((jnp.full((tq, 1), NEG_INF, F32), jnp.zeros((tq, 1), F32), jnp.zeros((tq, LANE), F32))
                         for _ in group)
            carry = lax.fori_loop(0, i * per, lambda j, c: step(j, c, None), init)
            for d in range(per):
                carry = step(i * per + d, carry, d)
            for g, (m, l, acc) in enumerate(carry):
                o_ref[qs, lanes[g]] = jnp.where(lo, 0.0, acc * (1.0 / l)).astype(BF)
                lse_ref[g, qs, :] = m + jnp.log(l)
            return 0

        lax.fori_loop(0, rows // tq, q_block, 0)

    head, kv_head, shared, col_vec, row_vec = _attn_specs(rows, kv_off, group_size)
    in_specs, args = [head, kv_head], [q, kv]
    if has_kr:
        in_specs.append(shared)
        args.append(kr)
    if has_f:
        in_specs += [col_vec, row_vec]
        args += [cum_col, cum_row]
    return pl.pallas_call(
        body, out_shape=[jax.ShapeDtypeStruct((rows, heads * LANE), BF), jax.ShapeDtypeStruct((heads, rows, 1), F32)],
        grid=(heads // group_size,), in_specs=in_specs, out_specs=[head, col_vec], name=name,
        compiler_params=_params(("arbitrary",)))(*args)


def _attn_backward(q, kv, kv_off, kr, cum_col, cum_row, o, do, lse, scale, group_size, name):
    rows = q.shape[0]
    heads = HEADS
    t = ATTN_TILE
    tq = ATTN_BACKWARD_Q_TILE
    nb = rows // t
    has_kr = kr is not None
    has_f = cum_col is not None
    group = range(group_size)

    def body(*refs):
        it = iter(refs)
        q_ref, kv_ref = next(it), next(it)
        kr_ref = next(it) if has_kr else None
        cc_ref = next(it) if has_f else None
        cr_ref = next(it) if has_f else None
        o_ref, do_ref, lse_ref = next(it), next(it), next(it)
        dq_ref, dkv_ref = next(it), next(it)
        dkr_ref = next(it) if has_kr else None
        dck_ref = next(it) if has_f else None
        dcq_ref = next(it) if has_f else None
        dq_acc = next(it)
        lo = lax.broadcasted_iota(jnp.int32, (1, LANE), 1) < HEAD_DIM
        row = lax.broadcasted_iota(jnp.int32, (tq, t), 0)
        col = lax.broadcasted_iota(jnp.int32, (tq, t), 1)
        lanes = [slice(g * LANE, (g + 1) * LANE) for g in group]

        dq_acc[...] = jnp.zeros_like(dq_acc)
        if has_kr:
            _init_acc(dkr_ref)
        if has_f:
            dcq_ref[...] = jnp.zeros_like(dcq_ref)

        def kv_block(j, _):
            ks = pl.ds(pl.multiple_of(j * t, t), t)
            other = kr_ref[ks, :] if has_kr else jnp.zeros((t, LANE), BF)
            kvbs = [kv_ref[ks, lanes[g]] for g in group]
            kks = [jnp.where(lo, kvbs[g], other) for g in group]
            cks = [cr_ref[g, :, ks] if has_f else None for g in group]
            first_q = (j * t) // tq
            causal = col + (j * t - first_q * tq) <= row

            def pair(i, carry, diag):
                qs = pl.ds(pl.multiple_of(i * tq, tq), tq)
                nt = (((1,), (1,)), ((), ()))

                def first_stage(g):
                    qb = q_ref[qs, lanes[g]]
                    dob = do_ref[qs, lanes[g]]
                    return (qb, dob, lax.dot_general(qb, kks[g], nt, preferred_element_type=F32),
                            lax.dot_general(dob, kvbs[g], nt, preferred_element_type=F32))

                first = {g: first_stage(g) for g in (group[:1] if has_f else group)}
                out = []
                for g in group:
                    dkk, dvv, dcs = carry[g]
                    qb, dob, logit, dp = first[g]
                    if has_f and g + 1 < len(group):
                        first[g + 1] = first_stage(g + 1)
                    s = logit * scale
                    if has_f:
                        s = s + (cc_ref[g, qs, :] - cks[g])
                    if diag:
                        s = jnp.where(causal, s, NEG_INF)
                    p = jnp.exp(s - lse_ref[g, qs, :])
                    delta = jnp.sum(dob.astype(F32) * o_ref[qs, lanes[g]].astype(F32), axis=1, keepdims=True)
                    ds = p * (dp - delta)
                    dsb = ds.astype(BF)
                    dvv = dvv + lax.dot_general(p.astype(BF), dob, (((0,), (0,)), ((), ())), preferred_element_type=F32)
                    dkk = dkk + lax.dot_general(dsb, qb, (((0,), (0,)), ((), ())), preferred_element_type=F32)
                    dq_acc[qs, lanes[g]] += jnp.dot(dsb, kks[g], preferred_element_type=F32)
                    if has_f:
                        dcs = dcs + jnp.sum(ds, axis=0, keepdims=True)
                        dcq_ref[g, qs, :] += jnp.sum(ds, axis=1, keepdims=True)
                    out.append((dkk, dvv, dcs))
                return tuple(out)

            init = tuple((jnp.zeros((t, LANE), F32), jnp.zeros((t, LANE), F32), jnp.zeros((1, t), F32)) for _ in group)
            carry = pair(first_q, init, True)
            carry = lax.fori_loop(first_q + 1, rows // tq, lambda i, c: pair(i, c, False), carry)
            for g, (dkk, dvv, dcs) in enumerate(carry):
                dkk = dkk * scale
                dkv_ref[ks, lanes[g]] = jnp.where(lo, dkk, dvv).astype(BF)
                if has_kr:
                    dkr_ref[ks, :] += jnp.where(lo, 0.0, dkk)
                if has_f:
                    dck_ref[g, :, ks] = -dcs
            return 0

        lax.fori_loop(0, nb, kv_block, 0)
        dq_ref[...] = (dq_acc[...] * scale).astype(BF)

    head, kv_head, shared, col_vec, row_vec = _attn_specs(rows, kv_off, group_size)
    in_specs, args = [head, kv_head], [q, kv]
    if has_kr:
        in_specs.append(shared)
        args.append(kr)
    if has_f:
        in_specs += [col_vec, row_vec]
        args += [cum_col, cum_row]
    in_specs += [head, head, col_vec]
    args += [o, do, lse]
    out_shape = [jax.ShapeDtypeStruct((rows, heads * LANE), BF), jax.ShapeDtypeStruct((rows, heads * LANE), BF)]
    out_specs = [head, head]
    if has_kr:
        out_shape.append(jax.ShapeDtypeStruct((rows, LANE), F32))
        out_specs.append(shared)
    if has_f:
        out_shape += [jax.ShapeDtypeStruct((heads, 1, rows), F32), jax.ShapeDtypeStruct((heads, rows, 1), F32)]
        out_specs += [row_vec, col_vec]
    return pl.pallas_call(
        body, out_shape=out_shape, grid=(heads // group_size,), in_specs=in_specs, out_specs=out_specs,
        scratch_shapes=[pltpu.VMEM((rows, group_size * LANE), F32)], name=name,
        compiler_params=_params(("arbitrary",)))(*args)


def _tri_dot(tri, x):
    return jnp.dot(tri, x, preferred_element_type=F32, precision=lax.Precision.HIGHEST)


def _forget_forward(f_raw, b_f):
    rows = f_raw.shape[0]
    t = ATTN_TILE

    def body(f_ref, b_ref, cum_ref):
        tri = (lax.broadcasted_iota(jnp.int32, (t, t), 1) <= lax.broadcasted_iota(jnp.int32, (t, t), 0)).astype(F32)

        def blk(i, carry):
            sl = pl.ds(pl.multiple_of(i * t, t), t)
            xv = f_ref[sl, :] + b_ref[...]
            log_f = jnp.minimum(xv, 0.0) - jnp.log(1.0 + jnp.exp(-jnp.abs(xv)))
            cum_ref[sl, :] = _tri_dot(tri, log_f) + carry
            return carry + jnp.sum(log_f, axis=0, keepdims=True)

        lax.fori_loop(0, rows // t, blk, jnp.zeros((1, LANE), F32))

    return pl.pallas_call(body, out_shape=jax.ShapeDtypeStruct((rows, LANE), F32), name="forget_forward",
                          compiler_params=_params())(f_raw, b_f)


def _forget_backward(f_raw, b_f, dcum):
    rows = f_raw.shape[0]
    t = ATTN_TILE
    nb = rows // t

    def body(f_ref, b_ref, dc_ref, df_ref, db_ref):
        tri = (lax.broadcasted_iota(jnp.int32, (t, t), 1) >= lax.broadcasted_iota(jnp.int32, (t, t), 0)).astype(F32)

        def blk(i, carry):
            later, db = carry
            sl = pl.ds(pl.multiple_of((nb - 1 - i) * t, t), t)
            dc = dc_ref[sl, :]
            dlog = _tri_dot(tri, dc) + later
            xv = f_ref[sl, :] + b_ref[...]
            df = dlog / (1.0 + jnp.exp(xv))
            df_ref[sl, :] = df.astype(BF)
            return later + jnp.sum(dc, axis=0, keepdims=True), db + jnp.sum(df, axis=0, keepdims=True)

        _, db = lax.fori_loop(0, nb, blk, (jnp.zeros((1, LANE), F32), jnp.zeros((1, LANE), F32)))
        db_ref[...] = db

    return pl.pallas_call(body, out_shape=[jax.ShapeDtypeStruct((rows, LANE), BF), jax.ShapeDtypeStruct((1, LANE), F32)],
                          name="forget_backward", compiler_params=_params())(f_raw, b_f, dcum)


def _t5_bucket(dist):
    max_exact = REL_BUCKETS // 2
    n = jnp.maximum(dist.astype(F32), 1.0)
    large = max_exact + (jnp.log(n / max_exact) / math.log(REL_MAX_DIST / max_exact)
                         * (REL_BUCKETS - max_exact)).astype(jnp.int32)
    large = jnp.minimum(large, REL_BUCKETS - 1)
    return jnp.where(dist < max_exact, dist, large)


def _dil_buckets(dilation):
    i = jnp.arange(Q_BLOCK)[:, None]
    j = jnp.arange(Q_BLOCK)[None, :]
    cur = _t5_bucket(jnp.clip(i - j, 0) * dilation).astype(jnp.int32)
    prev = _t5_bucket(jnp.clip(Q_BLOCK + i - j, 0) * dilation).astype(jnp.int32)
    return cur, prev


def _dil_bias_tiles(tbl_ref, bc_ref, bp_ref, bias_ref, group, hp, pairs):
    ii = lax.broadcasted_iota(jnp.int32, (Q_BLOCK, Q_BLOCK), 0)
    jj = lax.broadcasted_iota(jnp.int32, (Q_BLOCK, Q_BLOCK), 1)
    for hh in range(2 * pairs):
        col = group * HEADS + 2 * pairs * hp + hh
        acc_c = jnp.zeros((Q_BLOCK, Q_BLOCK), F32)
        acc_p = jnp.zeros((Q_BLOCK, Q_BLOCK), F32)
        for b in range(REL_BUCKETS):
            val = tbl_ref[b, col]
            acc_c = jnp.where(bc_ref[...] == b, val, acc_c)
            acc_p = jnp.where(bp_ref[...] == b, val, acc_p)
        bias_ref[2 * hh] = jnp.where(jj <= ii, acc_c, NEG_INF)
        bias_ref[2 * hh + 1] = jnp.where(jj >= ii, acc_p, NEG_INF)


def _dil_view(qkv, group, dilation):
    if dilation == 1:
        return qkv
    width = 3 * HEADS * HEAD_DIM
    return qkv[:, group * width:(group + 1) * width].reshape(qkv.shape[0] // dilation, dilation * width)


def _dil_specs(group, dilation, length):
    width = DIL_PAIRS[dilation] * LANE
    per = 8 // DIL_PAIRS[dilation]

    def col(kind):
        if dilation == 1:
            return pl.BlockSpec((length, width), lambda hp, r: (0, (group * 3 + kind) * per + hp))
        return pl.BlockSpec((length, width), lambda hp, r: (0, (r * 3 + kind) * per + hp))

    out = pl.BlockSpec((length, width), lambda hp, r: (0, r * per + hp))
    tile = pl.BlockSpec((Q_BLOCK, Q_BLOCK), lambda hp, r: (0, 0))
    table = pl.BlockSpec(memory_space=pltpu.SMEM)
    return col, out, tile, table


def _dil_forward(view, group, dilation, table, buckets):
    length = view.shape[0]
    rows = length * dilation
    pairs = DIL_PAIRS[dilation]
    nb = length // Q_BLOCK
    scale = HEAD_DIM ** -0.5
    qb = Q_BLOCK

    def body(tbl_ref, bc_ref, bp_ref, q_ref, k_ref, v_ref, o_ref, lse_ref, bias_ref):
        hp = pl.program_id(0)

        @pl.when(pl.program_id(1) == 0)
        def _():
            _dil_bias_tiles(tbl_ref, bc_ref, bp_ref, bias_ref, group, hp, pairs)

        lo = lax.broadcasted_iota(jnp.int32, (1, LANE), 1) < HEAD_DIM
        nt = (((1,), (1,)), ((), ()))

        def blk(n, first):
            cur = pl.ds(0, qb) if first else pl.ds(pl.multiple_of(n * qb, qb), qb)
            prev = None if first else pl.ds(pl.multiple_of((n - 1) * qb, qb), qb)
            logits = []
            for pair in range(pairs):
                lanes = slice(pair * LANE, (pair + 1) * LANE)
                qn = q_ref[cur, lanes] * scale
                for hh in range(2):
                    qm = jnp.where(lo if hh == 0 else ~lo, qn, jnp.zeros_like(qn))
                    s_c = lax.dot_general(qm, k_ref[cur, lanes], nt, preferred_element_type=F32)
                    s_p = None if first else lax.dot_general(qm, k_ref[prev, lanes], nt, preferred_element_type=F32)
                    logits.append((s_c, s_p))
            for pair in range(pairs):
                lanes = slice(pair * LANE, (pair + 1) * LANE)
                outs, lses = [], []
                for hh in range(2):
                    bias = 4 * pair + 2 * hh
                    s_c, s_p = logits[2 * pair + hh]
                    s_c = s_c + bias_ref[bias]
                    m = jnp.max(s_c, axis=1, keepdims=True)
                    if not first:
                        s_p = s_p + bias_ref[bias + 1]
                        m = jnp.maximum(m, jnp.max(s_p, axis=1, keepdims=True))
                    e_c = jnp.exp(s_c - m)
                    l = jnp.sum(e_c, axis=1, keepdims=True)
                    acc = jnp.dot(e_c.astype(BF), v_ref[cur, lanes], preferred_element_type=F32)
                    if not first:
                        e_p = jnp.exp(s_p - m)
                        l = l + jnp.sum(e_p, axis=1, keepdims=True)
                        acc = acc + jnp.dot(e_p.astype(BF), v_ref[prev, lanes], preferred_element_type=F32)
                    outs.append(acc * (1.0 / l))
                    lses.append(m + jnp.log(l))
                o_ref[cur, lanes] = jnp.where(lo, outs[0], outs[1])
                lse_ref[cur, lanes] = jnp.where(lo, lses[0], lses[1])
            return 0

        blk(0, True)
        if nb > 1:
            lax.fori_loop(1, nb, lambda n, _: blk(n, False), 0)

    col, out, tile, tbl = _dil_specs(group, dilation, length)
    bc, bp = buckets
    o, lse = pl.pallas_call(
        body, out_shape=[jax.ShapeDtypeStruct((length, dilation * D_MODEL), F32)] * 2,
        grid=(8 // pairs, dilation), in_specs=[tbl, tile, tile, col(0), col(1), col(2)], out_specs=[out, out],
        scratch_shapes=[pltpu.VMEM((4 * pairs, qb, qb), F32)], name=f"dilated_forward_{dilation}",
        compiler_params=_params(("arbitrary", "arbitrary")))(
            table, bc, bp, view, view, view)
    return o.reshape(rows, D_MODEL), lse.reshape(rows, D_MODEL)


def _dil_backward(view, group, dilation, table, buckets, do_g, lse, dlt):
    length = view.shape[0]
    rows = length * dilation
    pairs = DIL_PAIRS[dilation]
    nb = length // Q_BLOCK
    scale = HEAD_DIM ** -0.5
    qb = Q_BLOCK

    def body(tbl_ref, bc_ref, bp_ref, q_ref, k_ref, v_ref, do_ref, lse_ref, dlt_ref,
             dq_ref, dk_ref, dv_ref, db_ref, bias_ref, dk_acc, dv_acc):
        hp = pl.program_id(0)

        @pl.when(pl.program_id(1) == 0)
        def _():
            _dil_bias_tiles(tbl_ref, bc_ref, bp_ref, bias_ref, group, hp, pairs)
            db_ref[...] = jnp.zeros_like(db_ref)

        dk_acc[...] = jnp.zeros_like(dk_acc)
        dv_acc[...] = jnp.zeros_like(dv_acc)
        lo = lax.broadcasted_iota(jnp.int32, (1, LANE), 1) < HEAD_DIM
        tn = (((0,), (0,)), ((), ()))
        nt = (((1,), (1,)), ((), ()))

        def blk(n, first):
            cur = pl.ds(0, qb) if first else pl.ds(pl.multiple_of(n * qb, qb), qb)
            prev = None if first else pl.ds(pl.multiple_of((n - 1) * qb, qb), qb)
            inputs = []
            for pair in range(pairs):
                lanes = slice(pair * LANE, (pair + 1) * LANE)
                qn = q_ref[cur, lanes] * scale
                don = do_ref[cur, lanes]
                for hh in range(2):
                    mask = lo if hh == 0 else ~lo
                    qm = jnp.where(mask, qn, jnp.zeros_like(qn))
                    dom = jnp.where(mask, don, jnp.zeros_like(don))
                    stage = [qm, dom, lax.dot_general(qm, k_ref[cur, lanes], nt, preferred_element_type=F32),
                             lax.dot_general(dom, v_ref[cur, lanes], nt, preferred_element_type=F32)]
                    if not first:
                        stage += [lax.dot_general(qm, k_ref[prev, lanes], nt, preferred_element_type=F32),
                                  lax.dot_general(dom, v_ref[prev, lanes], nt, preferred_element_type=F32)]
                    inputs.append(stage)
            for pair in range(pairs):
                lanes = slice(pair * LANE, (pair + 1) * LANE)
                kc = k_ref[cur, lanes]
                if not first:
                    kp = k_ref[prev, lanes]
                lse_n = lse_ref[cur, lanes]
                dlt_n = dlt_ref[cur, lanes]
                dqs = []
                dkc = jnp.zeros((qb, LANE), F32)
                dkp = jnp.zeros((qb, LANE), F32)
                dvc = jnp.zeros((qb, LANE), F32)
                dvp = jnp.zeros((qb, LANE), F32)
                for hh in range(2):
                    bias = 4 * pair + 2 * hh
                    mask = lo if hh == 0 else ~lo
                    qm, dom, s_c, dp_c = inputs[2 * pair + hh][:4]
                    lse_h = jnp.max(jnp.where(mask, lse_n, -3e38), axis=1, keepdims=True)
                    dlt_h = jnp.max(jnp.where(mask, dlt_n, -3e38), axis=1, keepdims=True)
                    p_c = jnp.exp(s_c + bias_ref[bias] - lse_h)
                    ds_c = p_c * (dp_c - dlt_h)
                    db_ref[pair, 2 * hh] += ds_c
                    dsc_b = ds_c.astype(BF)
                    dq = jnp.dot(dsc_b, kc, preferred_element_type=F32)
                    dkc = dkc + lax.dot_general(dsc_b, qm, tn, preferred_element_type=F32)
                    dvc = dvc + lax.dot_general(p_c.astype(BF), dom, tn, preferred_element_type=F32)
                    if not first:
                        s_p, dp_p = inputs[2 * pair + hh][4:]
                        p_p = jnp.exp(s_p + bias_ref[bias + 1] - lse_h)
                        ds_p = p_p * (dp_p - dlt_h)
                        db_ref[pair, 2 * hh + 1] += ds_p
                        dsp_b = ds_p.astype(BF)
                        dq = dq + jnp.dot(dsp_b, kp, preferred_element_type=F32)
                        dkp = dkp + lax.dot_general(dsp_b, qm, tn, preferred_element_type=F32)
                        dvp = dvp + lax.dot_general(p_p.astype(BF), dom, tn, preferred_element_type=F32)
                    dqs.append(dq)
                dq_ref[cur, lanes] = (jnp.where(lo, dqs[0], dqs[1]) * scale).astype(BF)
                dk_acc[cur, lanes] += dkc
                dv_acc[cur, lanes] += dvc
                if not first:
                    dk_acc[prev, lanes] += dkp
                    dv_acc[prev, lanes] += dvp
            return 0

        blk(0, True)
        if nb > 1:
            lax.fori_loop(1, nb, lambda n, _: blk(n, False), 0)
        dk_ref[...] = dk_acc[...].astype(BF)
        dv_ref[...] = dv_acc[...].astype(BF)

    col, out, tile, tbl = _dil_specs(group, dilation, length)
    bc, bp = buckets
    wide = (length, dilation * D_MODEL)
    dq, dk, dv, db = pl.pallas_call(
        body, out_shape=[jax.ShapeDtypeStruct(wide, BF)] * 3 + [jax.ShapeDtypeStruct((8, 4, qb, qb), F32)],
        grid=(8 // pairs, dilation), in_specs=[tbl, tile, tile, col(0), col(1), col(2), out, out, out],
        out_specs=[out, out, out, pl.BlockSpec((pairs, 4, qb, qb), lambda hp, r: (hp, 0, 0, 0))],
        scratch_shapes=[pltpu.VMEM((4 * pairs, qb, qb), F32), pltpu.VMEM((length, pairs * LANE), F32),
                        pltpu.VMEM((length, pairs * LANE), F32)],
        name=f"dilated_backward_{dilation}", compiler_params=_params(("arbitrary", "arbitrary")))(
            table, bc, bp, view, view, view,
            do_g.reshape(wide), lse.reshape(wide), dlt.reshape(wide))
    return dq.reshape(rows, D_MODEL), dk.reshape(rows, D_MODEL), dv.reshape(rows, D_MODEL), db


def _head_sums(x, lo):
    s0 = jnp.sum(jnp.where(lo, x, 0.0), axis=1, keepdims=True)
    s1 = jnp.sum(jnp.where(lo, 0.0, x), axis=1, keepdims=True)
    return jnp.where(lo, s0, s1)


def _dil_merge_forward(outs, lses):
    rows = outs[0].shape[0]

    def body(o0, o1, o2, l0, l1, l2, o_ref):
        ls = [l0[...], l1[...], l2[...]]
        m = jnp.maximum(jnp.maximum(ls[0], ls[1]), ls[2])
        es = [jnp.exp(v - m) for v in ls]
        tot = es[0] + es[1] + es[2]
        o_ref[...] = ((es[0] * o0[...] + es[1] * o1[...] + es[2] * o2[...]) / tot).astype(BF)

    blk = pl.BlockSpec((ROW_TILE, LANE), lambda i, j: (i, j))
    return pl.pallas_call(body, out_shape=jax.ShapeDtypeStruct((rows, D_MODEL), BF), grid=(rows // ROW_TILE, 8),
                          in_specs=[blk] * 6, out_specs=blk, name="dilated_merge_forward",
                          compiler_params=_params(("parallel", "parallel")))(*outs, *lses)


def _dil_merge_backward(outs, lses, do):
    rows = outs[0].shape[0]

    def body(o0, o1, o2, l0, l1, l2, do_ref, d0, d1, d2, t0, t1, t2):
        lo = lax.broadcasted_iota(jnp.int32, (1, LANE), 1) < HEAD_DIM
        ls = [l0[...], l1[...], l2[...]]
        os_ = [o0[...], o1[...], o2[...]]
        m = jnp.maximum(jnp.maximum(ls[0], ls[1]), ls[2])
        es = [jnp.exp(v - m) for v in ls]
        inv = 1.0 / (es[0] + es[1] + es[2])
        alphas = [e * inv for e in es]
        dov = do_ref[...]
        merged = alphas[0] * os_[0] + alphas[1] * os_[1] + alphas[2] * os_[2]
        dot = _head_sums(dov * merged, lo)
        for a, d_ref, t_ref in zip(alphas, (d0, d1, d2), (t0, t1, t2)):
            d_ref[...] = (a * dov).astype(BF)
            t_ref[...] = a * dot

    blk = pl.BlockSpec((ROW_TILE, LANE), lambda i, j: (i, j))
    res = pl.pallas_call(
        body, out_shape=[jax.ShapeDtypeStruct((rows, D_MODEL), BF)] * 3 + [jax.ShapeDtypeStruct((rows, D_MODEL), F32)] * 3,
        grid=(rows // ROW_TILE, 8), in_specs=[blk] * 7, out_specs=[blk] * 6, name="dilated_merge_backward",
        compiler_params=_params(("parallel", "parallel")))(*outs, *lses, do)
    return res[:3], res[3:]


def _rel_bias_grad(dbs, buckets):
    def body(db_ref, bc_ref, bp_ref, o_ref):
        g = pl.program_id(0)
        hp = pl.program_id(1)

        @pl.when((g == 0) & (hp == 0))
        def _():
            o_ref[...] = jnp.zeros_like(o_ref)

        rr = lax.broadcasted_iota(jnp.int32, (REL_BUCKETS, LANE), 0)
        cc = lax.broadcasted_iota(jnp.int32, (REL_BUCKETS, LANE), 1)
        bc = bc_ref[0]
        bp = bp_ref[0]
        acc = jnp.zeros((REL_BUCKETS, LANE), F32)
        for hh in range(2):
            col = g * HEADS + 2 * hp + hh
            d_c = db_ref[0, 0, 2 * hh]
            d_p = db_ref[0, 0, 2 * hh + 1]
            for b in range(REL_BUCKETS):
                val = (jnp.sum(jnp.where(bc == b, d_c, 0.0), keepdims=True)
                       + jnp.sum(jnp.where(bp == b, d_p, 0.0), keepdims=True))
                acc = jnp.where((rr == b) & (cc == col), val, acc)
        o_ref[...] += acc

    db_all = jnp.stack(dbs)
    bc_all = jnp.stack([b[0] for b in buckets])
    bp_all = jnp.stack([b[1] for b in buckets])
    tile = pl.BlockSpec((1, Q_BLOCK, Q_BLOCK), lambda g, hp: (g, 0, 0))
    return pl.pallas_call(
        body, out_shape=jax.ShapeDtypeStruct((REL_BUCKETS, LANE), F32), grid=(3, 8),
        in_specs=[pl.BlockSpec((1, 1, 4, Q_BLOCK, Q_BLOCK), lambda g, hp: (g, hp, 0, 0, 0)), tile, tile],
        out_specs=pl.BlockSpec((REL_BUCKETS, LANE), lambda g, hp: (0, 0)), name="rel_bias_grad",
        compiler_params=_params(("arbitrary", "arbitrary")))(db_all, bc_all, bp_all)


def _mla_forward(hn, w, tables):
    a = _matmul(hn, w["w_a"], name="mla_a")
    cq, ckv, kr = _mla_mid_forward(a, w["q_norm"], w["kv_norm"], tables)
    q_raw = _matmul(cq, w["w_uq"], name="mla_uq")
    q = _rope_heads(q_raw, tables, False, "rope_forward")
    kv = _matmul(ckv, w["w_ukv"], b_chunks=True, out_dtype=BF, name="mla_ukv")
    scale = (HEAD_DIM + MLA_ROPE) ** -0.5
    o, lse = _attn_forward(q, kv, 0, kr, None, None, scale, MLA_GROUP, "mla_attention_forward")
    y = _matmul(o, w["w_o"], name="attn_out")
    return y, dict(hn=hn, a=a, cq=cq, ckv=ckv, kr=kr, q=q, kv=kv, o=o, lse=lse)


def _mla_backward(dy, w, s, tables):
    scale = (HEAD_DIM + MLA_ROPE) ** -0.5
    g = {}
    g["w_o"] = _matmul(s["o"], dy, ta=True, out_dtype=BF, name="attn_out_dw")
    do = _matmul(dy, w["w_o"], tb=True, out_dtype=BF, name="attn_out_dx")
    dq, dkv, dkr = _attn_backward(s["q"], s["kv"], 0, s["kr"], None, None, s["o"], do, s["lse"], scale,
                                  MLA_GROUP, "mla_attention_backward")
    dq_raw = _rope_heads(dq, tables, True, "rope_backward")
    g["w_uq"] = _matmul(s["cq"], dq_raw, ta=True, out_dtype=BF, name="mla_uq_dw")
    dcq = _matmul(dq_raw, w["w_uq"], tb=True, name="mla_uq_dx")
    g["w_ukv"] = _matmul(s["ckv"], dkv, ta=True, out_chunks=True, out_dtype=BF, name="mla_ukv_dw")
    dckv = _matmul(dkv, w["w_ukv"], tb=True, b_chunks=True, name="mla_ukv_dx")
    da, g["q_norm"], g["kv_norm"] = _mla_mid_backward(s["a"], w["q_norm"], w["kv_norm"], tables, dcq, dckv, dkr)
    g["w_a"] = _matmul(s["hn"], da, ta=True, out_dtype=BF, name="mla_a_dw")
    dhn = _matmul(da, w["w_a"], tb=True, name="mla_a_dx")
    return dhn, g


def _fox_forward(hn, w):
    qkv = _matmul(hn, w["w_qkv"], out_dtype=BF, name="fox_qkv")
    f_raw = _matmul(hn, w["w_f"], name="fox_f")
    cum = _forget_forward(f_raw, w["b_f"])
    cum_heads = cum[:, :HEADS].T
    cum_col, cum_row = cum_heads[:, :, None], cum_heads[:, None, :]
    o, lse = _attn_forward(qkv, qkv, HEADS, None, cum_col, cum_row, HEAD_DIM ** -0.5, FOX_GROUP,
                           "fox_attention_forward")
    y = _matmul(o, w["w_o"], name="attn_out")
    return y, dict(hn=hn, qkv=qkv, f_raw=f_raw, cum_col=cum_col, cum_row=cum_row, o=o, lse=lse)


def _fox_backward(dy, w, s):
    g = {}
    g["w_o"] = _matmul(s["o"], dy, ta=True, out_dtype=BF, name="attn_out_dw")
    do = _matmul(dy, w["w_o"], tb=True, out_dtype=BF, name="attn_out_dx")
    dq, dkv, dck, dcq = _attn_backward(s["qkv"], s["qkv"], HEADS, None, s["cum_col"], s["cum_row"], s["o"], do,
                                       s["lse"], HEAD_DIM ** -0.5, FOX_GROUP, "fox_attention_backward")
    dcum = jnp.pad((dck[:, 0, :] + dcq[:, :, 0]).T, ((0, 0), (0, LANE - HEADS)))
    df, g["b_f"] = _forget_backward(s["f_raw"], w["b_f"], dcum)
    dqkv = jnp.concatenate([dq, dkv], axis=1)
    g["w_qkv"] = _matmul(s["hn"], dqkv, ta=True, out_dtype=BF, name="fox_qkv_dw")
    g["w_f"] = _matmul(s["hn"], df, ta=True, out_dtype=BF, name="fox_f_dw")
    dhn = _matmul(dqkv, w["w_qkv"], tb=True, name="fox_qkv_dx")
    dhn = _matmul(df, w["w_f"], tb=True, add=dhn, name="fox_f_dx")
    return dhn, g


def _dil_mixer_forward(hn, w, buckets):
    qkv = _matmul(hn, w["w_qkv"], b_chunks=True, out_dtype=BF, name="dil_qkv")
    views = [_dil_view(qkv, grp, dilation) for grp, (_, dilation) in enumerate(DIL_PATTERNS)]
    outs, lses = [], []
    for grp, (_, dilation) in enumerate(DIL_PATTERNS):
        o_g, lse_g = _dil_forward(views[grp], grp, dilation, w["rel_bias"], buckets[grp])
        outs.append(o_g)
        lses.append(lse_g)
    o = _dil_merge_forward(outs, lses)
    y = _matmul(o, w["w_o"], name="dil_out")
    return y, dict(hn=hn, views=views, outs=outs, lses=lses, o=o)


def _dil_mixer_backward(dy, w, s, buckets):
    g = {}
    g["w_o"] = _matmul(s["o"], dy, ta=True, out_dtype=BF, name="dil_out_dw")
    do = _matmul(dy, w["w_o"], tb=True, name="dil_out_dx")
    do_gs, dlts = _dil_merge_backward(s["outs"], s["lses"], do)
    parts, dbs = [], []
    for grp, (_, dilation) in enumerate(DIL_PATTERNS):
        dq, dk, dv, db = _dil_backward(s["views"][grp], grp, dilation, w["rel_bias"], buckets[grp], do_gs[grp],
                                       s["lses"][grp], dlts[grp])
        parts += [dq, dk, dv]
        dbs.append(db)
    dqkv = jnp.concatenate(parts, axis=1)
    g["rel_bias"] = _rel_bias_grad(dbs, buckets)
    g["w_qkv"] = _matmul(s["hn"], dqkv, ta=True, out_chunks=True, out_dtype=BF, name="dil_qkv_dw")
    dhn = _matmul(dqkv, w["w_qkv"], tb=True, b_chunks=True, name="dil_qkv_dx")
    return dhn, g


def _mixer_weights(i, lw, small):
    mixer, j = i % N_MIXERS, i // N_MIXERS
    if mixer == 0:
        return dict(lw["mixer"], q_norm=small["mla_q_norm"][j][None, :], kv_norm=small["mla_kv_norm"][j][None, :])
    if mixer == 1:
        return dict(lw["mixer"], rel_bias=small["rel_bias"])
    return dict(lw["mixer"], b_f=jnp.pad(small["fox_b_f"][j][None, :], ((0, 0), (0, LANE - HEADS))))


MIXER_PART, COMMON_PART = 0, 1


def _run_layers(x, p, positions, target, get_part, get_small, put_part):
    tables = _rope_tables(positions)
    buckets = [_dil_buckets(d) for _, d in DIL_PATTERNS]
    layers, saved = [], []
    h = x
    first = get_part(0, MIXER_PART, positions)
    small = get_small()

    def gain(i, k):
        return small["norm_g"][i, k][None, :]

    hn = _prenorm(h, gain(0, 0))
    sq = dh = None
    for i in range(DEPTH):
        mixer = i % N_MIXERS
        lw = dict(first if i == 0 else get_part(i, MIXER_PART, h))
        mw = _mixer_weights(i, lw, small)
        if mixer == 0:
            y, ms = _mla_forward(hn, mw, tables)
        elif mixer == 1:
            y, ms = _dil_mixer_forward(hn, mw, buckets)
        else:
            y, ms = _fox_forward(hn, mw)
        if "ffn_w_in" not in lw:
            lw.update(get_part(i, COMMON_PART, y))
        layers.append(lw)
        h1, hn2 = _post_residual(h, y, gain(i, 1), gain(i, 2))
        gu = _matmul(hn2, lw["ffn_w_in"], b_chunks=True, out_dtype=BF, name="ffn_in")
        act = _swiglu_forward(gu)
        f = _matmul(act, lw["ffn_w_out"], name="ffn_out")
        h2, h2b = _post_residual(h1, f, gain(i, 3), None)
        pp = _matmul(p[i], lw["ple_w_proj"], b_chunks=True, name="ple_proj")
        z = _matmul(h2b, lw["ple_w_gate"], name="ple_gate")
        saved.append(dict(h=h, y=y, ms=ms, h1=h1, hn2=hn2, gu=gu, act=act, f=f, h2b=h2b, pp=pp, z=z))
        if i + 1 < DEPTH:
            h, hn = _ple_forward(h2, pp, z, gain(i + 1, 0))
        else:
            dh, sq = _ple_loss(h2, pp, z, target)

    norm_rows = [[None] * 4 for _ in range(DEPTH)]
    sg = dict(mla_q_norm={}, mla_kv_norm={}, rel_bias=None, fox_b_f={})
    for i in reversed(range(DEPTH)):
        s, lw = saved[i], layers[i]
        mixer, j = i % N_MIXERS, i // N_MIXERS
        mw = _mixer_weights(i, lw, small)
        lg = {}
        dpp, dz = _ple_backward(dh, s["pp"], s["z"])
        lg["ple_w_proj"] = _matmul(p[i], dpp, ta=True, out_chunks=True, out_dtype=BF, name="ple_proj_dw")
        lg["ple_w_gate"] = _matmul(s["h2b"], dz, ta=True, out_dtype=BF, name="ple_gate_dw")
        dh2 = _matmul(dz, lw["ple_w_gate"], tb=True, add=dh, name="ple_gate_dx")
        df, norm_rows[i][3] = _rms_backward(s["f"], gain(i, 3), dh2, None, BF)
        lg["ffn_w_out"] = _matmul(s["act"], df, ta=True, out_dtype=BF, name="ffn_out_dw")
        dact = _matmul(df, lw["ffn_w_out"], tb=True, out_dtype=BF, name="ffn_out_dx")
        dgu = _swiglu_backward(s["gu"], dact)
        lg["ffn_w_in"] = _matmul(s["hn2"], dgu, ta=True, out_chunks=True, out_dtype=BF, name="ffn_in_dw")
        split = i in SPLIT_LAYERS
        zero = put_part(i, COMMON_PART, lg)[0:1, 0:1] if split else 0.0
        dhn2 = _matmul(dgu, lw["ffn_w_in"], tb=True, b_chunks=True, name="ffn_in_dx")
        dh1, norm_rows[i][2] = _rms_backward(s["h1"], gain(i, 2), dhn2, dh2, F32)
        dy, norm_rows[i][1] = _rms_backward(s["y"], gain(i, 1) + zero, dh1, None, BF)
        if mixer == 0:
            dhn, mg = _mla_backward(dy, mw, s["ms"], tables)
            sg["mla_q_norm"][j] = mg.pop("q_norm")
            sg["mla_kv_norm"][j] = mg.pop("kv_norm")
        elif mixer == 1:
            dhn, mg = _dil_mixer_backward(dy, mw, s["ms"], buckets)
            rel = mg.pop("rel_bias")[:, :3 * HEADS]
            sg["rel_bias"] = rel if sg["rel_bias"] is None else sg["rel_bias"] + rel
        else:
            dhn, mg = _fox_backward(dy, mw, s["ms"])
            sg["fox_b_f"][j] = mg.pop("b_f")[:, :HEADS]
        token = put_part(i, MIXER_PART, dict(mixer=mg) if split else dict(lg, mixer=mg))
        dh, norm_rows[i][0] = _rms_backward(s["h"], gain(i, 0) + token[0:1, 0:1], dhn, dh1, F32)
    small_grads = dict(norm_g=jnp.stack([jnp.concatenate(row, axis=0) for row in norm_rows]),
                       rel_bias=sg["rel_bias"])
    for k in ("mla_q_norm", "mla_kv_norm", "fox_b_f"):
        small_grads[k] = jnp.concatenate([sg[k][j] for j in sorted(sg[k])], axis=0)
    return sq, dh, small_grads


BIG = ("ffn_w_in", "ffn_w_out", "ple_w_proj", "ple_w_gate", "mla_w_a", "mla_w_uq", "mla_w_ukv", "mla_w_o",
       "dil_w_qkv", "dil_w_o", "fox_w_qkvf", "fox_w_o")
SMALL_SHARDED = ("norm_g", "mla_q_norm", "mla_kv_norm")
SMALL_REPLICATED = ("rel_bias", "fox_b_f")
WEIGHTS = ("norm_g", "ffn_w_in", "ffn_w_out", "ple_w_proj", "ple_w_gate", "rel_bias", "mla_w_a", "mla_q_norm",
           "mla_kv_norm", "mla_w_uq", "mla_w_ukv", "mla_w_o", "dil_w_qkv", "dil_w_o", "fox_w_qkvf", "fox_b_f", "fox_w_o")


TRANSPOSED = "fox_w_qkvf"
SPLIT_LAYERS = (0, 1, 2, 3)
LAYER_COMMON = ("ffn_w_in", "ffn_w_out", "ple_w_proj", "ple_w_gate")
MIXER_WEIGHTS = (("mla_w_a", "mla_w_uq", "mla_w_ukv", "mla_w_o"), ("dil_w_qkv", "dil_w_o"), ("fox_w_qkvf", "fox_w_o"))


def _part_names(i, part):
    if i in SPLIT_LAYERS:
        return MIXER_WEIGHTS[i % N_MIXERS] if part == MIXER_PART else LAYER_COMMON
    return MIXER_WEIGHTS[i % N_MIXERS] + LAYER_COMMON if part == MIXER_PART else ()


def _layer_slot(name, i):
    return i if name in LAYER_COMMON else i // N_MIXERS


def _merge_rows(chunks):
    n, r, c = chunks.shape
    return chunks.reshape(n * r, c)


def _merge_cols(chunks):
    n, r, c = chunks.shape
    return chunks.transpose(1, 0, 2).reshape(r, n * c)


def _pad_heads_out(wo):
    w3 = wo.reshape(HEADS, HEAD_DIM, D_MODEL)
    return jnp.pad(w3, ((0, 0), (HEAD_DIM, 0), (0, 0))).reshape(HEADS * LANE, D_MODEL)


def _part_to_compute(i, part, ch):
    lw = {}
    if "ffn_w_in" in ch:
        lw.update(ffn_w_in=ch["ffn_w_in"], ffn_w_out=_merge_rows(ch["ffn_w_out"]), ple_w_proj=ch["ple_w_proj"],
                  ple_w_gate=_merge_rows(ch["ple_w_gate"]))
    if part == COMMON_PART:
        return lw
    mixer = i % N_MIXERS
    if mixer == 0:
        wa = _merge_rows(ch["mla_w_a"])
        rank = MLA_Q_RANK + MLA_KV_RANK
        wa_p = jnp.concatenate([wa[:, :rank], jnp.zeros((wa.shape[0], 64), wa.dtype), wa[:, rank:],
                                jnp.zeros((wa.shape[0], 32), wa.dtype)], axis=1)
        wuq = _merge_cols(ch["mla_w_uq"]).reshape(MLA_Q_RANK, HEADS, HEAD_DIM + MLA_ROPE)
        wuq_p = jnp.pad(wuq, ((0, 0), (0, 0), (0, LANE - HEAD_DIM - MLA_ROPE))).reshape(MLA_Q_RANK, HEADS * LANE)
        lw["mixer"] = dict(w_a=wa_p, w_uq=wuq_p, w_ukv=ch["mla_w_ukv"], w_o=_pad_heads_out(_merge_rows(ch["mla_w_o"])))
    elif mixer == 1:
        lw["mixer"] = dict(w_qkv=ch["dil_w_qkv"], w_o=_merge_rows(ch["dil_w_o"]))
    else:
        wf = _merge_rows(ch["fox_w_qkvf"]).T
        inner = HEADS * HEAD_DIM
        q3 = wf[:, :inner].reshape(D_MODEL, HEADS, HEAD_DIM)
        k3 = wf[:, inner:2 * inner].reshape(D_MODEL, HEADS, HEAD_DIM)
        v3 = wf[:, 2 * inner:3 * inner].reshape(D_MODEL, HEADS, HEAD_DIM)
        q_p = jnp.pad(q3, ((0, 0), (0, 0), (0, HEAD_DIM))).reshape(D_MODEL, HEADS * LANE)
        kv_p = jnp.concatenate([k3, v3], axis=2).reshape(D_MODEL, HEADS * LANE)
        f_p = jnp.pad(wf[:, 3 * inner:], ((0, 0), (0, LANE - HEADS)))
        lw["mixer"] = dict(w_qkv=jnp.concatenate([q_p, kv_p], axis=1), w_f=f_p,
                           w_o=_pad_heads_out(_merge_rows(ch["fox_w_o"])))
    return lw


def _part_contributions(i, part, lg, chunk_shapes):
    spec = {k: jax.ShapeDtypeStruct(s, BF) for k, s in chunk_shapes.items()}
    (contrib,) = jax.linear_transpose(functools.partial(_part_to_compute, i, part), spec)(lg)
    return contrib


def _chip_peers():
    x, y, c = lax.axis_index("x"), lax.axis_index("y"), lax.axis_index("c")
    peers = [(1 - x, y), (x, 1 - y), (1 - x, 1 - y)]
    return x, y, c, peers


SEM_SPEC = pl.BlockSpec(memory_space=pltpu.SEMAPHORE)
ANY_SPEC = pl.BlockSpec(memory_space=pl.ANY)
SPLIT_EFFECT = pltpu.SideEffectType.DATAFLOW_SIDE_EFFECTING


def _own_slot(shard):
    me = 2 * lax.axis_index("x") + lax.axis_index("y")
    return lax.dynamic_update_index_in_dim(lax.empty((N_CHIPS,) + shard.shape, shard.dtype), shard[None], me, 0)


def _spread_copy(src, land, k, peer, c, send_sems, recv_sems, index, src_slot, slot):
    px, py = peer
    return pltpu.make_async_remote_copy(
        src_ref=src.at[src_slot], dst_ref=land.at[slot],
        send_sem=send_sems.at[3 * index + k], recv_sem=recv_sems.at[3 * index + k],
        device_id=(px, py, c), device_id_type=MESH)


def _spread_start(bufs, srcs, after, name):
    n = len(bufs)
    exchange = srcs is not None
    arrays = (list(srcs) if exchange else []) + list(bufs)
    na = len(arrays)

    def body(*refs):
        src, land = refs[:n], refs[na - n:na]
        send_sems, recv_sems = refs[na + 1], refs[na + 2]
        token = refs[-1]
        x, y, c, peers = _chip_peers()
        me = 2 * x + y
        for w in range(n):
            for k, peer in enumerate(peers):
                src_slot = 2 * peer[0] + peer[1] if exchange else me
                _spread_copy(src[w], land[w], k, peer, c, send_sems, recv_sems, w, src_slot, me).start()
        token[...] = jnp.zeros_like(token)

    hbm = [pltpu.with_memory_space_constraint(a, pltpu.HBM) for a in arrays]
    out = pl.pallas_call(
        body, name=name,
        out_shape=(pltpu.SemaphoreType.DMA((3 * n,)), pltpu.SemaphoreType.DMA((3 * n,)),
                   *[pltpu.HBM(a.shape, a.dtype) for a in hbm], jax.ShapeDtypeStruct((8, LANE), F32)),
        in_specs=[HBM_SPEC] * na + [ANY_SPEC],
        out_specs=(SEM_SPEC, SEM_SPEC, *[HBM_SPEC] * na, pl.BlockSpec(memory_space=pltpu.VMEM)),
        input_output_aliases={w: 2 + w for w in range(na)},
        compiler_params=pltpu.CompilerParams(has_side_effects=SPLIT_EFFECT))(*hbm, after)
    return dict(send=out[0], recv=out[1], arrays=out[2:2 + na], n=n, token=out[-1], exchange=exchange)


def _spread_wait(handle, after, name):
    n, exchange = handle["n"], handle["exchange"]
    arrays = list(handle["arrays"])
    na = len(arrays)

    def body(*refs):
        src, land = refs[:n], refs[na - n:na]
        send_sems, recv_sems = refs[na], refs[na + 1]
        x, y, c, peers = _chip_peers()
        me = 2 * x + y
        for w in range(n):
            for k, peer in enumerate(peers):
                there = 2 * peer[0] + peer[1]
                cp = _spread_copy(src[w], land[w], k, peer, c, send_sems, recv_sems, w, there if exchange else me, there)
                cp.wait_send()
                cp.wait_recv()

    out = pl.pallas_call(
        body, name=name, out_shape=tuple(pltpu.HBM(a.shape, a.dtype) for a in arrays),
        in_specs=[HBM_SPEC] * na + [SEM_SPEC, SEM_SPEC, ANY_SPEC], out_specs=tuple([HBM_SPEC] * na),
        input_output_aliases={w: w for w in range(na)},
        compiler_params=pltpu.CompilerParams(has_side_effects=SPLIT_EFFECT))(*arrays, handle["send"], handle["recv"], after)
    return (list(out[n:]), list(out[:n])) if exchange else list(out)


def _sibling_copy(received, sent, land, k, me, peers, sibling, send_sems, recv_sems, index):
    slot = me if k == 3 else 2 * peers[k][0] + peers[k][1]
    src = sent if k == 3 else received
    return pltpu.make_async_remote_copy(
        src_ref=src.at[slot], dst_ref=land.at[slot], send_sem=send_sems.at[4 * index + k],
        recv_sem=recv_sems.at[4 * index + k], device_id=sibling, device_id_type=MESH)


def _sibling_start(received, sent, after, name):
    n = len(received)
    lands = [lax.empty(a.shape, a.dtype) for a in received]
    arrays = list(received) + list(sent) + lands

    def body(*refs):
        rec, snt, land = refs[:n], refs[n:2 * n], refs[2 * n:3 * n]
        send_sems, recv_sems = refs[3 * n + 1], refs[3 * n + 2]
        token = refs[-1]
        x, y, c, peers = _chip_peers()
        for w in range(n):
            for k in range(4):
                _sibling_copy(rec[w], snt[w], land[w], k, 2 * x + y, peers, (x, y, 1 - c), send_sems, recv_sems, w).start()
        token[...] = jnp.zeros_like(token)

    hbm = [pltpu.with_memory_space_constraint(a, pltpu.HBM) for a in arrays]
    out = pl.pallas_call(
        body, name=name,
        out_shape=(pltpu.SemaphoreType.DMA((4 * n,)), pltpu.SemaphoreType.DMA((4 * n,)),
                   *[pltpu.HBM(a.shape, a.dtype) for a in hbm], jax.ShapeDtypeStruct((8, LANE), F32)),
        in_specs=[HBM_SPEC] * (3 * n) + [ANY_SPEC],
        out_specs=(SEM_SPEC, SEM_SPEC, *[HBM_SPEC] * (3 * n), pl.BlockSpec(memory_space=pltpu.VMEM)),
        input_output_aliases={w: 2 + w for w in range(3 * n)},
        compiler_params=pltpu.CompilerParams(has_side_effects=SPLIT_EFFECT))(*hbm, after)
    return dict(send=out[0], recv=out[1], arrays=out[2:2 + 3 * n], n=n, token=out[-1])


def _sibling_wait(handle, after, name):
    n = handle["n"]
    arrays = list(handle["arrays"])

    def body(*refs):
        rec, snt, land = refs[:n], refs[n:2 * n], refs[2 * n:3 * n]
        send_sems, recv_sems = refs[3 * n], refs[3 * n + 1]
        x, y, c, peers = _chip_peers()
        for w in range(n):
            for k in range(4):
                cp = _sibling_copy(rec[w], snt[w], land[w], k, 2 * x + y, peers, (x, y, 1 - c), send_sems, recv_sems, w)
                cp.wait_send()
                cp.wait_recv()

    out = pl.pallas_call(
        body, name=name, out_shape=tuple(pltpu.HBM(a.shape, a.dtype) for a in arrays),
        in_specs=[HBM_SPEC] * (3 * n) + [SEM_SPEC, SEM_SPEC, ANY_SPEC], out_specs=tuple([HBM_SPEC] * (3 * n)),
        input_output_aliases={w: w for w in range(3 * n)},
        compiler_params=pltpu.CompilerParams(has_side_effects=SPLIT_EFFECT))(*arrays, handle["send"], handle["recv"], after)
    return list(out[:n]), list(out[n:2 * n]), list(out[2 * n:])


def _all_reduce_small(v):
    rows = v.shape[0]

    def body(v_ref, sum_ref, slots, send_sems, recv_sems):
        x, y, c = lax.axis_index("x"), lax.axis_index("y"), lax.axis_index("c")
        me = 4 * x + 2 * y + c
        slots[me] = v_ref[...]
        sends = []
        for k in range(1, N_DEV):
            bx, by, bc = (k >> 2) & 1, (k >> 1) & 1, k & 1
            peer = (x ^ bx, y ^ by, c ^ bc)
            rc = pltpu.make_async_remote_copy(src_ref=v_ref, dst_ref=slots.at[me], send_sem=send_sems.at[k],
                                              recv_sem=recv_sems.at[k], device_id=peer, device_id_type=MESH)
            rc.start()
            sends.append(rc)
        for k in range(1, N_DEV):
            bx, by, bc = (k >> 2) & 1, (k >> 1) & 1, k & 1
            src = 4 * (x ^ bx) + 2 * (y ^ by) + (c ^ bc)
            pltpu.make_async_remote_copy(src_ref=v_ref, dst_ref=slots.at[src], send_sem=send_sems.at[k],
                                         recv_sem=recv_sems.at[k], device_id=(x ^ bx, y ^ by, c ^ bc),
                                         device_id_type=MESH).wait_recv()
        for rc in sends:
            rc.wait_send()
        total = slots[0]
        for k in range(1, N_DEV):
            total = total + slots[k]
        sum_ref[...] = total

    vm = pl.BlockSpec(memory_space=pltpu.VMEM)
    return pl.pallas_call(
        body, out_shape=jax.ShapeDtypeStruct((rows, LANE), F32), in_specs=[vm], out_specs=vm,
        scratch_shapes=[pltpu.VMEM((N_DEV, rows, LANE), F32), pltpu.SemaphoreType.DMA((N_DEV,)),
                        pltpu.SemaphoreType.DMA((N_DEV,))], name="all_reduce_small")(v)


def _as_2d(a):
    return a.reshape(-1, a.shape[-1])


def _row_tile(rows, cols):
    for t in (512, 256, 128, 64, 32, 16):
        if rows % t == 0 and t * cols * 4 <= (1 << 20):
            return t
    return rows


def _adamw_weight(w, m, v, received, sent, sibling):
    layers = len(received)
    _, rows, cols = received[0].shape
    tr = _row_tile(rows, cols)
    by_columns = rows % tr != 0 or tr == rows and rows * cols * 4 > (2 << 20)
    if by_columns:
        assert layers == 1 and cols % (2 * LANE) == 0, (w.shape, received[0].shape)
        tr, tc, steps = rows, cols // 2, 2
        index = lambda i: (0, i)
    else:
        tc, steps = cols, rows // tr
        index = lambda i: (i, 0)
    where = (2 * lax.axis_index("x") + lax.axis_index("y")).astype(jnp.int32).reshape(1)

    def body(where_ref, w_ref, m_ref, v_ref, *rest):
        per_layer, (g_ref, d_ref, nm_ref, nv_ref) = rest[:3 * layers], rest[3 * layers:]
        me = where_ref[0]
        for layer in range(layers):
            r_ref, own_ref, s_ref = per_layer[3 * layer:3 * layer + 3]

            @pl.when(pl.program_id(0) == layer)
            def _():
                mine = theirs = None
                for k in range(N_CHIPS):
                    a = jnp.where(me == k, own_ref[...], r_ref[k]).astype(F32)
                    b = s_ref[k].astype(F32)
                    mine = a if mine is None else mine + a
                    theirs = b if theirs is None else theirs + b
                g = mine + theirs
                delta, nm, nv = _adamw_math(w_ref[...], g, m_ref[...], v_ref[...])
                g_ref[...] = g
                d_ref[...] = delta
                nm_ref[...] = nm
                nv_ref[...] = nv

    def held(layer, now, i):
        return jnp.where(now < layer, 0, jnp.where(now > layer, steps - 1, i))

    if by_columns:
        stacked = pl.BlockSpec((tr, tc), lambda now, i, where_ref: index(i))
    else:
        stacked = pl.BlockSpec((tr, tc), lambda now, i, where_ref: (now * steps + i, 0))
    in_specs = [stacked, stacked, stacked]
    args = [where, w, m, v]
    for layer in range(layers):
        four = pl.BlockSpec((N_CHIPS, tr, tc), lambda now, i, where_ref, layer=layer: (0,) + index(held(layer, now, i)))
        own = pl.BlockSpec((None, tr, tc),
                           lambda now, i, where_ref, layer=layer: (where_ref[0],) + index(held(layer, now, i)))
        in_specs += [four, own, four]
        args += [received[layer], sent[layer], sibling[layer]]
    grid_spec = pltpu.PrefetchScalarGridSpec(num_scalar_prefetch=1, grid=(layers, steps), in_specs=in_specs,
                                             out_specs=[stacked] * 4)
    return pl.pallas_call(body, out_shape=[jax.ShapeDtypeStruct(w.shape, F32)] * 4, grid_spec=grid_spec,
                          name="adamw_weight", compiler_params=_params(("arbitrary", "arbitrary")))(*args)


def _adamw_math(w, g, m, v):
    m = ADAM_B1 * m + (1.0 - ADAM_B1) * g
    v = ADAM_B2 * v + (1.0 - ADAM_B2) * (g * g)
    m_hat = m * (1.0 / (1.0 - ADAM_B1 ** ADAM_STEP))
    v_hat = v * (1.0 / (1.0 - ADAM_B2 ** ADAM_STEP))
    denom = jnp.sqrt(v_hat) + ADAM_EPS
    inv = pl.reciprocal(denom, approx=True)
    inv = inv * (2.0 - denom * inv)
    delta = -ADAM_LR * (m_hat * inv + ADAM_WD * w)
    return delta, m, v


def _adamw(w, m, v, g_mine, g_sibling):
    rows, cols = w.shape
    tr = _row_tile(rows, cols)
    two = g_sibling is not None

    def body(*refs):
        if two:
            w_ref, m_ref, v_ref, ga_ref, gb_ref, g_ref, d_ref, nm_ref, nv_ref = refs
            g = ga_ref[...] + gb_ref[...]
        else:
            w_ref, m_ref, v_ref, ga_ref, g_ref, d_ref, nm_ref, nv_ref = refs
            g = ga_ref[...]
        delta, nm, nv = _adamw_math(w_ref[...], g, m_ref[...], v_ref[...])
        g_ref[...] = g
        d_ref[...] = delta
        nm_ref[...] = nm
        nv_ref[...] = nv

    blk = pl.BlockSpec((tr, cols), lambda i: (i, 0))
    args = [w, m, v, g_mine] + ([g_sibling] if two else [])
    return pl.pallas_call(body, out_shape=[jax.ShapeDtypeStruct((rows, cols), F32)] * 4, grid=(rows // tr,),
                          in_specs=[blk] * len(args), out_specs=[blk] * 4, name="adamw",
                          compiler_params=_params(("parallel",)))(*args)


def _pack_rows(arrays):
    flat = jnp.concatenate([a.reshape(-1) for a in arrays])
    rows = -(-flat.shape[0] // (8 * LANE)) * 8
    return jnp.pad(flat, (0, rows * LANE - flat.shape[0])).reshape(rows, LANE)


def _unpack_rows(packed, shapes):
    flat = packed.reshape(-1)
    out, at = [], 0
    for s in shapes:
        size = math.prod(s)
        out.append(flat[at:at + size].reshape(s))
        at += size
    return out


def kernel(x, p, positions, norm_g, ffn_w_in, ffn_w_out, ple_w_proj, ple_w_gate, rel_bias, mla_w_a, mla_q_norm, mla_kv_norm, mla_w_uq, mla_w_ukv, mla_w_o, dil_w_qkv, dil_w_o, fox_w_qkvf, fox_b_f, fox_w_o, loss_target, m_norm_g, m_ffn_w_in, m_ffn_w_out, m_ple_w_proj, m_ple_w_gate, m_rel_bias, m_mla_w_a, m_mla_q_norm, m_mla_kv_norm, m_mla_w_uq, m_mla_w_ukv, m_mla_w_o, m_dil_w_qkv, m_dil_w_o, m_fox_w_qkvf, m_fox_b_f, m_fox_w_o, v_norm_g, v_ffn_w_in, v_ffn_w_out, v_ple_w_proj, v_ple_w_gate, v_rel_bias, v_mla_w_a, v_mla_q_norm, v_mla_kv_norm, v_mla_w_uq, v_mla_w_ukv, v_mla_w_o, v_dil_w_qkv, v_dil_w_o, v_fox_w_qkvf, v_fox_b_f, v_fox_w_o):
    w = dict(norm_g=norm_g, ffn_w_in=ffn_w_in, ffn_w_out=ffn_w_out, ple_w_proj=ple_w_proj, ple_w_gate=ple_w_gate,
             rel_bias=rel_bias, mla_w_a=mla_w_a, mla_q_norm=mla_q_norm, mla_kv_norm=mla_kv_norm, mla_w_uq=mla_w_uq,
             mla_w_ukv=mla_w_ukv, mla_w_o=mla_w_o, dil_w_qkv=dil_w_qkv, dil_w_o=dil_w_o, fox_w_qkvf=fox_w_qkvf,
             fox_b_f=fox_b_f, fox_w_o=fox_w_o)
    m = dict(norm_g=m_norm_g, ffn_w_in=m_ffn_w_in, ffn_w_out=m_ffn_w_out, ple_w_proj=m_ple_w_proj,
             ple_w_gate=m_ple_w_gate, rel_bias=m_rel_bias, mla_w_a=m_mla_w_a, mla_q_norm=m_mla_q_norm,
             mla_kv_norm=m_mla_kv_norm, mla_w_uq=m_mla_w_uq, mla_w_ukv=m_mla_w_ukv, mla_w_o=m_mla_w_o,
             dil_w_qkv=m_dil_w_qkv, dil_w_o=m_dil_w_o, fox_w_qkvf=m_fox_w_qkvf, fox_b_f=m_fox_b_f, fox_w_o=m_fox_w_o)
    v = dict(norm_g=v_norm_g, ffn_w_in=v_ffn_w_in, ffn_w_out=v_ffn_w_out, ple_w_proj=v_ple_w_proj,
             ple_w_gate=v_ple_w_gate, rel_bias=v_rel_bias, mla_w_a=v_mla_w_a, mla_q_norm=v_mla_q_norm,
             mla_kv_norm=v_mla_kv_norm, mla_w_uq=v_mla_w_uq, mla_w_ukv=v_mla_w_ukv, mla_w_o=v_mla_w_o,
             dil_w_qkv=v_dil_w_qkv, dil_w_o=v_dil_w_o, fox_w_qkvf=v_fox_w_qkvf, fox_b_f=v_fox_b_f, fox_w_o=v_fox_w_o)
    chip = 2 * lax.axis_index("x") + lax.axis_index("y")
    for tree in (w, m, v):
        tree[TRANSPOSED] = jnp.swapaxes(tree[TRANSPOSED], 1, 2)

    small_shapes = [w[k].shape for k in SMALL_SHARDED]
    order = [(i, part) for i in range(DEPTH) for part in (MIXER_PART, COMMON_PART) if _part_names(i, part)]
    gathers = {}
    after = positions
    zero = 0.0
    for i, part in order:
        bufs = [_own_slot((w[k][_layer_slot(k, i)] + zero).astype(BF)) for k in _part_names(i, part)]
        if (i, part) == order[0]:
            bufs.append(_own_slot(_pack_rows([w[k] for k in SMALL_SHARDED])))
        gathers[i, part] = _spread_start(bufs, None, after, f"gather_start_{i}_{part}")
        after = gathers[i, part]["token"]
        if (i, part) == order[0]:
            zero = after[0, 0]
    all_started = after
    state = {}

    def get_part(i, part, after_array):
        is_first = (i, part) == order[0]
        lands = _spread_wait(gathers[i, part], all_started if is_first else after_array, f"gather_wait_{i}_{part}")
        if is_first:
            pieces = [_unpack_rows(lands[-1][k], small_shapes) for k in range(N_CHIPS)]
            small = {name: jnp.concatenate([pieces[k][idx] for k in range(N_CHIPS)], axis=-1)
                     for idx, name in enumerate(SMALL_SHARDED)}
            state["small"] = dict(small, rel_bias=rel_bias, fox_b_f=fox_b_f)
        chunks = dict(zip(_part_names(i, part), lands))
        state[i, part] = {k: a.shape for k, a in chunks.items()}
        return _part_to_compute(i, part, chunks)

    started, forwards = [], {}

    def forward_oldest(after_array):
        i, part, handle = started.pop(0)
        received, sent = _spread_wait(handle, after_array, f"exchange_wait_{i}_{part}")
        forwards[i, part] = _sibling_start(received, sent, after_array, f"sibling_start_{i}_{part}")
        return forwards[i, part]["token"]

    def put_part(i, part, lg):
        contrib = _part_contributions(i, part, lg, state[i, part])
        srcs = [contrib[k] for k in _part_names(i, part)]
        handle = _spread_start([lax.empty(s.shape, s.dtype) for s in srcs], srcs, positions,
                               f"exchange_start_{i}_{part}")
        token = handle["token"]
        if started:
            token = token + forward_oldest(token)
        started.append((i, part, handle))
        return token

    sq, grad_x, sg = _run_layers(x[0], p[:, 0], positions[0], loss_target[0], get_part, lambda: state["small"],
                                 put_part)
    loss = lax.psum(0.5 / D_MODEL * jnp.sum(sq), ("x", "y", "c"))
    forward_oldest(grad_x)

    held = {k: {} for k in BIG}
    for i, part in sorted(forwards, reverse=True):
        received, sent, sibling = _sibling_wait(forwards[i, part], grad_x, f"sibling_wait_{i}_{part}")
        for k, r, s, t in zip(_part_names(i, part), received, sent, sibling):
            held[k][_layer_slot(k, i)] = (r, s, t)
    results = {}
    for k in BIG:
        per_layer = [held[k][slot] for slot in sorted(held[k])]
        outs = _adamw_weight(_as_2d(w[k]), _as_2d(m[k]), _as_2d(v[k]), *[list(col) for col in zip(*per_layer)])
        results[k] = [o.reshape(w[k].shape) for o in outs]
    results[TRANSPOSED] = [jnp.swapaxes(o, 1, 2) for o in results[TRANSPOSED]]

    small_all = SMALL_SHARDED + SMALL_REPLICATED
    full_shapes = [sg[k].shape for k in small_all]
    reduced = dict(zip(small_all, _unpack_rows(_all_reduce_small(_pack_rows([sg[k] for k in small_all])), full_shapes)))
    local_g = []
    for k in small_all:
        g = reduced[k]
        if k in SMALL_SHARDED:
            width = w[k].shape[-1]
            g = lax.dynamic_slice_in_dim(g, chip * width, width, axis=g.ndim - 1)
        local_g.append(g)
    local_shapes = [w[k].shape for k in small_all]
    outs = _adamw(_pack_rows([w[k] for k in small_all]), _pack_rows([m[k] for k in small_all]),
                  _pack_rows([v[k] for k in small_all]), _pack_rows(local_g), None)
    unpacked = [_unpack_rows(o, local_shapes) for o in outs]
    for idx, k in enumerate(small_all):
        results[k] = [u[idx] for u in unpacked]

    return (loss, grad_x[None], *[results[k][0] for k in WEIGHTS], *[results[k][1] for k in WEIGHTS],
            *[results[k][2] for k in WEIGHTS], *[results[k][3] for k in WEIGHTS])
```

```python
import functools
import math

import jax
import jax.numpy as jnp
from jax import lax
from jax.experimental import pallas as pl
from jax.experimental.pallas import tpu as pltpu

F32 = jnp.float32
BF = jnp.bfloat16
MESH = pl.DeviceIdType.MESH
HBM_SPEC = pl.BlockSpec(memory_space=pltpu.HBM)

D_MODEL = 1024
DEPTH = 4
N_MIXERS = 3
D_FF = 2816
NORM_EPS = 1e-6
NEG_INF = -1e30
LANE = 128
HEADS = 16
HEAD_DIM = 64
MLA_Q_RANK = 384
MLA_KV_RANK = 256
MLA_ROPE = 32
MLA_A_PAD = 768
ROPE_THETA = 10000.0
DIL_PATTERNS = ((128, 1), (512, 4), (2048, 16))
Q_BLOCK = 128
DIL_PAIRS = {1: 2, 4: 4, 16: 4}
REL_BUCKETS = 32
REL_MAX_DIST = 2048
N_CHIPS = 4
N_DEV = 8

ADAM_LR = 0.001
ADAM_B1 = 0.9
ADAM_B2 = 0.999
ADAM_EPS = 1e-08
ADAM_WD = 0.01
ADAM_STEP = 10

VMEM_LIMIT = 56 * 1024 * 1024
MATMUL_VMEM_BUDGET = 36 * 1024 * 1024
ROW_TILE = 512
ATTN_TILE = 256
ATTN_Q_TILE = 512
ATTN_FORWARD_KEY_TILE = 512
ATTN_BACKWARD_Q_TILE = 512
MLA_GROUP = 4
FOX_GROUP = 2


def _params(sem=None):
    return pltpu.CompilerParams(dimension_semantics=sem, vmem_limit_bytes=VMEM_LIMIT)


def _divisor_tiles(dim):
    tiles = [t for t in range(LANE, dim + 1, LANE) if dim % t == 0]
    return tiles or [dim]


def _matmul_tiles(m, n, k, a_bytes, b_bytes, out_bytes, has_add, n_unit=None, k_unit=None):
    best = None
    for tm in _divisor_tiles(m):
        for tn in _divisor_tiles(n_unit or n):
            for tk in _divisor_tiles(k_unit or k):
                if max(tm, tn, tk) > 2048:
                    continue
                vmem = 2 * (tm * tk * a_bytes + tk * tn * b_bytes + tm * tn * out_bytes) + tm * tn * 4
                if has_add:
                    vmem += 2 * tm * tn * 4
                if vmem > MATMUL_VMEM_BUDGET:
                    continue
                steps = (m // tm) * (n // tn) * (k // tk)
                traffic = m * k * a_bytes * (n // tn) + k * n * b_bytes * (m // tm) + m * n * out_bytes
                cost = traffic / 3.0e12 + steps * 0.4e-6
                if best is None or cost < best[0]:
                    best = (cost, tm, tn, tk)
    return best[1:]


def _matmul(a, b, *, ta=False, tb=False, b_chunks=False, out_chunks=False, add=None, out_dtype=F32, name):
    k, m = a.shape if ta else a.shape[::-1]
    n_unit = k_unit = None
    if b_chunks:
        chunks, rows_w, c = b.shape
        if tb:
            kb, n, k_unit = chunks * c, rows_w, c
        else:
            kb, n, n_unit = rows_w, chunks * c, c
    else:
        kb, n = b.shape[::-1] if tb else b.shape
    if out_chunks:
        assert n % N_CHIPS == 0 and add is None
        n_unit = n // N_CHIPS
    assert k == kb, (a.shape, b.shape, ta, tb)
    tm, tn, tk = _matmul_tiles(m, n, k, a.dtype.itemsize, b.dtype.itemsize, jnp.dtype(out_dtype).itemsize,
                               add is not None, n_unit, k_unit)
    nk = k // tk
    dims = (((0 if ta else 1,), (1 if tb else 0,)), ((), ()))

    def body(*refs):
        if add is None:
            a_ref, b_ref, o_ref, acc_ref = refs
            add_ref = None
        else:
            a_ref, b_ref, add_ref, o_ref, acc_ref = refs
        kk = pl.program_id(2)

        @pl.when(kk == 0)
        def _():
            acc_ref[...] = jnp.zeros_like(acc_ref)

        acc_ref[...] += lax.dot_general(a_ref[...].astype(BF), b_ref[...].astype(BF), dims,
                                        preferred_element_type=F32)

        @pl.when(kk == nk - 1)
        def _():
            r = acc_ref[...]
            if add_ref is not None:
                r = r + add_ref[...].astype(F32)
            o_ref[...] = r.astype(out_dtype)

    a_spec = pl.BlockSpec((tk, tm), lambda i, j, q: (q, i)) if ta else pl.BlockSpec((tm, tk), lambda i, j, q: (i, q))
    if b_chunks and tb:
        per_k = k_unit // tk
        b_spec = pl.BlockSpec((None, tn, tk), lambda i, j, q: (q // per_k, j, q % per_k))
    elif b_chunks:
        per_n = n_unit // tn
        b_spec = pl.BlockSpec((None, tk, tn), lambda i, j, q: (j // per_n, q, j % per_n))
    elif tb:
        b_spec = pl.BlockSpec((tn, tk), lambda i, j, q: (j, q))
    else:
        b_spec = pl.BlockSpec((tk, tn), lambda i, j, q: (q, j))
    if out_chunks:
        per_o = n_unit // tn
        o_spec = pl.BlockSpec((None, tm, tn), lambda i, j, q: (j // per_o, i, j % per_o))
        out_shape = jax.ShapeDtypeStruct((N_CHIPS, m, n_unit), out_dtype)
    else:
        o_spec = pl.BlockSpec((tm, tn), lambda i, j, q: (i, j))
        out_shape = jax.ShapeDtypeStruct((m, n), out_dtype)
    in_specs = [a_spec, b_spec]
    args = [a, b]
    if add is not None:
        in_specs.append(o_spec)
        args.append(add)
    return pl.pallas_call(
        body, out_shape=out_shape, grid=(m // tm, n // tn, nk),
        in_specs=in_specs, out_specs=o_spec, scratch_shapes=[pltpu.VMEM((tm, tn), F32)], name=name,
        compiler_params=_params(("parallel", "parallel", "arbitrary")))(*args)


def _rowwise(body, name, rows, ins, outs, tr=ROW_TILE):
    def row_spec(cols):
        return pl.BlockSpec((tr, cols), lambda i: (i, 0))

    def full_spec(shape):
        zeros = (0,) * len(shape)
        return pl.BlockSpec(shape, lambda i: zeros)

    in_specs = [row_spec(a.shape[1]) if kind == "row" else full_spec(a.shape) for a, kind in ins]
    out_specs = [row_spec(shape[1]) if kind == "row" else full_spec(shape) for shape, _, kind in outs]
    out_shape = [jax.ShapeDtypeStruct(shape, dtype) for shape, dtype, _ in outs]
    return pl.pallas_call(body, out_shape=out_shape, grid=(rows // tr,), in_specs=in_specs, out_specs=out_specs,
                          name=name, compiler_params=_params(("arbitrary",)))(*[a for a, _ in ins])


def _rstd(x):
    return lax.rsqrt(jnp.mean(x * x, axis=-1, keepdims=True) + NORM_EPS)


def _rms_bwd_math(x, g, dy):
    r = _rstd(x)
    gd = dy * g
    dx = r * gd - x * (r * r * r) * jnp.mean(gd * x, axis=-1, keepdims=True)
    dg = jnp.sum(dy * x * r, axis=0, keepdims=True)
    return dx, dg


def _sigmoid(x):
    return 0.5 * jnp.tanh(0.5 * x) + 0.5


def _init_acc(*refs):
    @pl.when(pl.program_id(0) == 0)
    def _():
        for r in refs:
            r[...] = jnp.zeros_like(r)


def _prenorm(h, g):
    rows, cols = h.shape

    def body(h_ref, g_ref, o_ref):
        x = h_ref[...]
        o_ref[...] = (x * _rstd(x) * g_ref[...]).astype(BF)

    return _rowwise(body, "prenorm", rows, [(h, "row"), (g, "full")], [((rows, cols), BF, "row")])[0]


def _post_residual(h, y, g_post, g_pre):
    rows, cols = h.shape
    with_pre = g_pre is not None

    def body(*refs):
        if with_pre:
            h_ref, y_ref, gp_ref, gq_ref, hn_ref, hb_ref = refs
        else:
            h_ref, y_ref, gp_ref, hn_ref, hb_ref = refs
        yv = y_ref[...]
        hn = h_ref[...] + yv * _rstd(yv) * gp_ref[...]
        hn_ref[...] = hn
        hb_ref[...] = (hn * _rstd(hn) * gq_ref[...] if with_pre else hn).astype(BF)

    ins = [(h, "row"), (y, "row"), (g_post, "full")] + ([(g_pre, "full")] if with_pre else [])
    return _rowwise(body, "post_residual_pre" if with_pre else "post_residual", rows, ins,
                    [((rows, cols), F32, "row"), ((rows, cols), BF, "row")])


def _ple_forward(h2, pp, z, g_pre):
    rows, cols = h2.shape

    def body(h_ref, p_ref, z_ref, g_ref, h3_ref, hb_ref):
        h3 = h_ref[...] + p_ref[...] * _sigmoid(z_ref[...])
        h3_ref[...] = h3
        hb_ref[...] = (h3 * _rstd(h3) * g_ref[...]).astype(BF)

    return _rowwise(body, "ple_forward", rows, [(h2, "row"), (pp, "row"), (z, "row"), (g_pre, "full")],
                    [((rows, cols), F32, "row"), ((rows, cols), BF, "row")])


def _ple_loss(h2, pp, z, target):
    rows, cols = h2.shape

    def body(h_ref, p_ref, z_ref, t_ref, dh_ref, sq_ref):
        _init_acc(sq_ref)
        err = h_ref[...] + p_ref[...] * _sigmoid(z_ref[...]) - t_ref[...]
        dh_ref[...] = err * (1.0 / cols)
        sq_ref[...] += jnp.sum(err * err, axis=0, keepdims=True)

    return _rowwise(body, "ple_loss", rows, [(h2, "row"), (pp, "row"), (z, "row"), (target, "row")],
                    [((rows, cols), F32, "row"), ((1, cols), F32, "acc")])


def _ple_backward(dh3, pp, z):
    rows, cols = dh3.shape

    def body(d_ref, p_ref, z_ref, dpp_ref, dz_ref):
        d = d_ref[...]
        s = _sigmoid(z_ref[...])
        dpp_ref[...] = (d * s).astype(BF)
        dz_ref[...] = (d * p_ref[...] * s * (1.0 - s)).astype(BF)

    return _rowwise(body, "ple_backward", rows, [(dh3, "row"), (pp, "row"), (z, "row")],
                    [((rows, cols), BF, "row"), ((rows, cols), BF, "row")])


def _rms_backward(x, g, dy, add, out_dtype):
    rows, cols = x.shape
    with_add = add is not None

    def body(*refs):
        if with_add:
            x_ref, g_ref, dy_ref, add_ref, dx_ref, dg_ref = refs
        else:
            x_ref, g_ref, dy_ref, dx_ref, dg_ref = refs
        _init_acc(dg_ref)
        dx, dg = _rms_bwd_math(x_ref[...], g_ref[...], dy_ref[...].astype(F32))
        if with_add:
            dx = dx + add_ref[...]
        dx_ref[...] = dx.astype(out_dtype)
        dg_ref[...] += dg

    ins = [(x, "row"), (g, "full"), (dy, "row")] + ([(add, "row")] if with_add else [])
    return _rowwise(body, "rms_backward_add" if with_add else "rms_backward", rows, ins,
                    [((rows, cols), out_dtype, "row"), ((1, cols), F32, "acc")])


def _swiglu_forward(gu):
    rows = gu.shape[0]

    def body(gu_ref, o_ref):
        g = gu_ref[:, :D_FF].astype(F32)
        o_ref[...] = (g * _sigmoid(g) * gu_ref[:, D_FF:].astype(F32)).astype(BF)

    return _rowwise(body, "swiglu_forward", rows, [(gu, "row")], [((rows, D_FF), BF, "row")])[0]


def _swiglu_backward(gu, dact):
    rows = gu.shape[0]

    def body(gu_ref, d_ref, o_ref):
        g = gu_ref[:, :D_FF].astype(F32)
        u = gu_ref[:, D_FF:].astype(F32)
        d = d_ref[...].astype(F32)
        s = _sigmoid(g)
        gs = g * s
        o_ref[:, :D_FF] = (d * u * (s + gs * (1.0 - s))).astype(BF)
        o_ref[:, D_FF:] = (d * gs).astype(BF)

    return _rowwise(body, "swiglu_backward", rows, [(gu, "row"), (dact, "row")], [((rows, 2 * D_FF), BF, "row")])[0]


def _rope_tables(positions):
    half = MLA_ROPE // 2
    inv = ROPE_THETA ** (-jnp.arange(half, dtype=F32) / half)
    ang = positions.astype(F32)[:, None] * inv
    cos, sin = jnp.cos(ang), jnp.sin(ang)
    rows = positions.shape[0]
    c = jnp.ones((rows, LANE), F32).at[:, 64:80].set(cos).at[:, 80:96].set(cos)
    sa = jnp.zeros((rows, LANE), F32).at[:, 64:80].set(-sin)
    sb = jnp.zeros((rows, LANE), F32).at[:, 80:96].set(sin)
    return c, sa, sb


def _rope_apply(x, c, sa, sb):
    return x * c + pltpu.roll(x, LANE - 16, 1) * sa + pltpu.roll(x, 16, 1) * sb


def _rope_apply_t(dy, c, sa, sb):
    return dy * c + pltpu.roll(dy * sa, 16, 1) + pltpu.roll(dy * sb, LANE - 16, 1)


def _rope_heads(x, tables, transpose, name):
    rows, cols = x.shape

    def body(x_ref, c_ref, sa_ref, sb_ref, o_ref):
        fn = _rope_apply_t if transpose else _rope_apply
        c, sa, sb = c_ref[...], sa_ref[...], sb_ref[...]
        for head in range(cols // LANE):
            lanes = slice(head * LANE, (head + 1) * LANE)
            o_ref[:, lanes] = fn(x_ref[:, lanes].astype(F32), c, sa, sb).astype(BF)

    blk = pl.BlockSpec((ROW_TILE, cols), lambda i: (i, 0))
    tbl = pl.BlockSpec((ROW_TILE, LANE), lambda i: (i, 0))
    return pl.pallas_call(body, out_shape=jax.ShapeDtypeStruct((rows, cols), BF), grid=(rows // ROW_TILE,),
                          in_specs=[blk, tbl, tbl, tbl], out_specs=blk, name=name,
                          compiler_params=_params(("parallel",)))(x, *tables)


def _mla_mid_forward(a, q_norm, kv_norm, tables):
    rows = a.shape[0]
    qr, kvr = MLA_Q_RANK, MLA_KV_RANK

    def body(a_ref, qn_ref, kn_ref, c_ref, sa_ref, sb_ref, cq_ref, ckv_ref, kr_ref):
        aq = a_ref[:, 0:qr]
        akv = a_ref[:, qr:qr + kvr]
        cq_ref[...] = (aq * _rstd(aq) * qn_ref[...]).astype(BF)
        ckv_ref[...] = (akv * _rstd(akv) * kn_ref[...]).astype(BF)
        kr_ref[...] = _rope_apply(a_ref[:, qr + kvr:], c_ref[...], sa_ref[...], sb_ref[...]).astype(BF)

    ins = [(a, "row"), (q_norm, "full"), (kv_norm, "full")] + [(t, "row") for t in tables]
    return _rowwise(body, "mla_mid_forward", rows, ins,
                    [((rows, qr), BF, "row"), ((rows, kvr), BF, "row"), ((rows, LANE), BF, "row")])


def _mla_mid_backward(a, q_norm, kv_norm, tables, dcq, dckv, dkr):
    rows = a.shape[0]
    qr, kvr = MLA_Q_RANK, MLA_KV_RANK

    def body(a_ref, qn_ref, kn_ref, c_ref, sa_ref, sb_ref, dcq_ref, dckv_ref, dkr_ref, da_ref, dqn_ref, dkn_ref):
        _init_acc(dqn_ref, dkn_ref)
        dxq, dgq = _rms_bwd_math(a_ref[:, 0:qr], qn_ref[...], dcq_ref[...])
        dxk, dgk = _rms_bwd_math(a_ref[:, qr:qr + kvr], kn_ref[...], dckv_ref[...])
        da_ref[:, 0:qr] = dxq.astype(BF)
        da_ref[:, qr:qr + kvr] = dxk.astype(BF)
        da_ref[:, qr + kvr:] = _rope_apply_t(dkr_ref[...], c_ref[...], sa_ref[...], sb_ref[...]).astype(BF)
        dqn_ref[...] += dgq
        dkn_ref[...] += dgk

    ins = ([(a, "row"), (q_norm, "full"), (kv_norm, "full")] + [(t, "row") for t in tables]
           + [(dcq, "row"), (dckv, "row"), (dkr, "row")])
    return _rowwise(body, "mla_mid_backward", rows, ins,
                    [((rows, MLA_A_PAD), BF, "row"), ((1, qr), F32, "acc"), ((1, kvr), F32, "acc")])


def _attn_specs(rows, kv_off, g):
    head = pl.BlockSpec((rows, g * LANE), lambda h: (0, h))
    kv_head = pl.BlockSpec((rows, g * LANE), lambda h: (0, h + kv_off // g))
    shared = pl.BlockSpec((rows, LANE), lambda h: (0, 0))
    col_vec = pl.BlockSpec((g, rows, 1), lambda h: (h, 0, 0))
    row_vec = pl.BlockSpec((g, 1, rows), lambda h: (h, 0, 0))
    return head, kv_head, shared, col_vec, row_vec


def _attn_forward(q, kv, kv_off, kr, cum_col, cum_row, scale, group_size, name):
    rows = q.shape[0]
    heads = HEADS
    t = ATTN_FORWARD_KEY_TILE
    tq = ATTN_Q_TILE
    per = tq // t
    has_kr = kr is not None
    has_f = cum_col is not None
    group = range(group_size)

    def body(*refs):
        it = iter(refs)
        q_ref, kv_ref = next(it), next(it)
        kr_ref = next(it) if has_kr else None
        cc_ref = next(it) if has_f else None
        cr_ref = next(it) if has_f else None
        o_ref, lse_ref = next(it), next(it)
        lo = lax.broadcasted_iota(jnp.int32, (1, LANE), 1) < HEAD_DIM
        row = lax.broadcasted_iota(jnp.int32, (tq, t), 0)
        col = lax.broadcasted_iota(jnp.int32, (tq, t), 1)
        lanes = [slice(g * LANE, (g + 1) * LANE) for g in group]

        def q_block(i, _):
            qs = pl.ds(pl.multiple_of(i * tq, tq), tq)
            qbs = [q_ref[qs, lanes[g]] for g in group]
            cqs = [cc_ref[g, qs, :] if has_f else None for g in group]

            def step(j, carry, diag):
                ks = pl.ds(pl.multiple_of(j * t, t), t)
                skip = diag * t if diag and has_f else 0
                other = kr_ref[ks, :] if has_kr else jnp.zeros((t, LANE), BF)
                kvbs = [kv_ref[ks, lanes[g]] for g in group]

                def logit(g):
                    return lax.dot_general(qbs[g][skip:], jnp.where(lo, kvbs[g], other), (((1,), (1,)), ((), ())),
                                           preferred_element_type=F32)

                logits = {g: logit(g) for g in (group if has_f else group[:1])}
                out = []
                for g in group:
                    m, l, acc = (a[skip:] for a in carry[g])
                    if not has_f and g + 1 < len(group):
                        logits[g + 1] = logit(g + 1)
                    s = logits[g] * scale
                    if has_f:
                        s = s + (cqs[g][skip:] - cr_ref[g, :, ks])
                    if diag is not None:
                        s = jnp.where(col[skip:] + diag * t <= row[skip:], s, NEG_INF)
                    mn = jnp.maximum(m, jnp.max(s, axis=1, keepdims=True))
                    alpha = jnp.exp(m - mn)
                    p = jnp.exp(s - mn)
                    l = alpha * l + jnp.sum(p, axis=1, keepdims=True)
                    acc = alpha * acc + jnp.dot(p.astype(BF), kvbs[g], preferred_element_type=F32)
                    new = (mn, l, acc)
                    if skip:
                        new = tuple(jnp.concatenate([old[:skip], a], axis=0) for old, a in zip(carry[g], new))
                    out.append(new)
                return tuple(out)

            init = tuple((jnp.full((tq, 1), NEG_INF, F32), jnp.zeros((tq, 1), F32), jnp.zeros((tq, LANE), F32))
                         for _ in group)
            carry = lax.fori_loop(0, i * per, lambda j, c: step(j, c, None), init)
            for d in range(per):
                carry = step(i * per + d, carry, d)
            for g, (m, l, acc) in enumerate(carry):
                o_ref[qs, lanes[g]] = jnp.where(lo, 0.0, acc * (1.0 / l)).astype(BF)
                lse_ref[g, qs, :] = m + jnp.log(l)
            return 0

        lax.fori_loop(0, rows // tq, q_block, 0)

    head, kv_head, shared, col_vec, row_vec = _attn_specs(rows, kv_off, group_size)
    in_specs, args = [head, kv_head], [q, kv]
    if has_kr:
        in_specs.append(shared)
        args.append(kr)
    if has_f:
        in_specs += [col_vec, row_vec]
        args += [cum_col, cum_row]
    return pl.pallas_call(
        body, out_shape=[jax.ShapeDtypeStruct((rows, heads * LANE), BF), jax.ShapeDtypeStruct((heads, rows, 1), F32)],
        grid=(heads // group_size,), in_specs=in_specs, out_specs=[head, col_vec], name=name,
        compiler_params=_params(("arbitrary",)))(*args)


def _attn_backward(q, kv, kv_off, kr, cum_col, cum_row, o, do, lse, scale, group_size, name):
    rows = q.shape[0]
    heads = HEADS
    t = ATTN_TILE
    tq = ATTN_BACKWARD_Q_TILE
    has_kr = kr is not None
    has_f = cum_col is not None
    group = range(group_size)

    def body(*refs):
        it = iter(refs)
        q_ref, kv_ref = next(it), next(it)
        kr_ref = next(it) if has_kr else None
        cc_ref = next(it) if has_f else None
        cr_ref = next(it) if has_f else None
        o_ref, do_ref, lse_ref = next(it), next(it), next(it)
        dq_ref, dkv_ref = next(it), next(it)
        dkr_ref = next(it) if has_kr else None
        dck_ref = next(it) if has_f else None
        dcq_ref = next(it) if has_f else None
        dq_acc = next(it)
        lo = lax.broadcasted_iota(jnp.int32, (1, LANE), 1) < HEAD_DIM
        row = lax.broadcasted_iota(jnp.int32, (tq, t), 0)
        col = lax.broadcasted_iota(jnp.int32, (tq, t), 1)
        lanes = [slice(g * LANE, (g + 1) * LANE) for g in group]

        dq_acc[...] = jnp.zeros_like(dq_acc)
        if has_kr:
            _init_acc(dkr_ref)
        if has_f:
            dcq_ref[...] = jnp.zeros_like(dcq_ref)

        def kv_block(first_q, within):
            j = first_q * (tq // t) + within
            skip = within * t
            ks = pl.ds(pl.multiple_of(j * t, t), t)
            other = kr_ref[ks, :] if has_kr else jnp.zeros((t, LANE), BF)
            kvbs = [kv_ref[ks, lanes[g]] for g in group]
            kks = [jnp.where(lo, kvbs[g], other) for g in group]
            cks = [cr_ref[g, :, ks] if has_f else None for g in group]
            causal = col[:tq - skip] <= row[:tq - skip]

            def pair(i, carry, diag):
                start = pl.multiple_of(i * tq, tq)
                qs = pl.ds(start + skip, tq - skip) if diag else pl.ds(start, tq)
                nt = (((1,), (1,)), ((), ()))

                def first_stage(g):
                    qb = q_ref[qs, lanes[g]]
                    dob = do_ref[qs, lanes[g]]
                    return (qb, dob, lax.dot_general(qb, kks[g], nt, preferred_element_type=F32),
                            lax.dot_general(dob, kvbs[g], nt, preferred_element_type=F32))

                first = {g: first_stage(g) for g in (group[:1] if has_f else group)}
                out = []
                for g in group:
                    dkk, dvv, dcs = carry[g]
                    qb, dob, logit, dp = first[g]
                    if has_f and g + 1 < len(group):
                        first[g + 1] = first_stage(g + 1)
                    s = logit * scale
                    if has_f:
                        s = s + (cc_ref[g, qs, :] - cks[g])
                    if diag:
                        s = jnp.where(causal, s, NEG_INF)
                    p = jnp.exp(s - lse_ref[g, qs, :])
                    delta = jnp.sum(dob.astype(F32) * o_ref[qs, lanes[g]].astype(F32), axis=1, keepdims=True)
                    ds = p * (dp - delta)
                    dsb = ds.astype(BF)
                    dvv = dvv + lax.dot_general(p.astype(BF), dob, (((0,), (0,)), ((), ())), preferred_element_type=F32)
                    dkk = dkk + lax.dot_general(dsb, qb, (((0,), (0,)), ((), ())), preferred_element_type=F32)
                    dq_acc[qs, lanes[g]] += jnp.dot(dsb, kks[g], preferred_element_type=F32)
                    if has_f:
                        dcs = dcs + jnp.sum(ds, axis=0, keepdims=True)
                        dcq_ref[g, qs, :] += jnp.sum(ds, axis=1, keepdims=True)
                    out.append((dkk, dvv, dcs))
                return tuple(out)

            init = tuple((jnp.zeros((t, LANE), F32), jnp.zeros((t, LANE), F32), jnp.zeros((1, t), F32)) for _ in group)
            carry = pair(first_q, init, True)
            carry = lax.fori_loop(first_q + 1, rows // tq, lambda i, c: pair(i, c, False), carry)
            for g, (dkk, dvv, dcs) in enumerate(carry):
                dkk = dkk * scale
                dkv_ref[ks, lanes[g]] = jnp.where(lo, dkk, dvv).astype(BF)
                if has_kr:
                    dkr_ref[ks, :] += jnp.where(lo, 0.0, dkk)
                if has_f:
                    dck_ref[g, :, ks] = -dcs

        def q_diagonal(first_q, _):
            for within in range(tq // t):
                kv_block(first_q, within)
            return 0

        lax.fori_loop(0, rows // tq, q_diagonal, 0)
        dq_ref[...] = (dq_acc[...] * scale).astype(BF)

    head, kv_head, shared, col_vec, row_vec = _attn_specs(rows, kv_off, group_size)
    in_specs, args = [head, kv_head], [q, kv]
    if has_kr:
        in_specs.append(shared)
        args.append(kr)
    if has_f:
        in_specs += [col_vec, row_vec]
        args += [cum_col, cum_row]
    in_specs += [head, head, col_vec]
    args += [o, do, lse]
    out_shape = [jax.ShapeDtypeStruct((rows, heads * LANE), BF), jax.ShapeDtypeStruct((rows, heads * LANE), BF)]
    out_specs = [head, head]
    if has_kr:
        out_shape.append(jax.ShapeDtypeStruct((rows, LANE), F32))
        out_specs.append(shared)
    if has_f:
        out_shape += [jax.ShapeDtypeStruct((heads, 1, rows), F32), jax.ShapeDtypeStruct((heads, rows, 1), F32)]
        out_specs += [row_vec, col_vec]
    return pl.pallas_call(
        body, out_shape=out_shape, grid=(heads // group_size,), in_specs=in_specs, out_specs=out_specs,
        scratch_shapes=[pltpu.VMEM((rows, group_size * LANE), F32)], name=name,
        compiler_params=_params(("arbitrary",)))(*args)


def _tri_dot(tri, x):
    return jnp.dot(tri, x, preferred_element_type=F32, precision=lax.Precision.HIGHEST)


def _forget_forward(f_raw, b_f):
    rows = f_raw.shape[0]
    t = ATTN_TILE

    def body(f_ref, b_ref, cum_ref):
        tri = (lax.broadcasted_iota(jnp.int32, (t, t), 1) <= lax.broadcasted_iota(jnp.int32, (t, t), 0)).astype(F32)

        def blk(i, carry):
            sl = pl.ds(pl.multiple_of(i * t, t), t)
            xv = f_ref[sl, :] + b_ref[...]
            log_f = jnp.minimum(xv, 0.0) - jnp.log(1.0 + jnp.exp(-jnp.abs(xv)))
            cum_ref[sl, :] = _tri_dot(tri, log_f) + carry
            return carry + jnp.sum(log_f, axis=0, keepdims=True)

        lax.fori_loop(0, rows // t, blk, jnp.zeros((1, LANE), F32))

    return pl.pallas_call(body, out_shape=jax.ShapeDtypeStruct((rows, LANE), F32), name="forget_forward",
                          compiler_params=_params())(f_raw, b_f)


def _forget_backward(f_raw, b_f, dcum):
    rows = f_raw.shape[0]
    t = ATTN_TILE
    nb = rows // t

    def body(f_ref, b_ref, dc_ref, df_ref, db_ref):
        tri = (lax.broadcasted_iota(jnp.int32, (t, t), 1) >= lax.broadcasted_iota(jnp.int32, (t, t), 0)).astype(F32)

        def blk(i, carry):
            later, db = carry
            sl = pl.ds(pl.multiple_of((nb - 1 - i) * t, t), t)
            dc = dc_ref[sl, :]
            dlog = _tri_dot(tri, dc) + later
            xv = f_ref[sl, :] + b_ref[...]
            df = dlog / (1.0 + jnp.exp(xv))
            df_ref[sl, :] = df.astype(BF)
            return later + jnp.sum(dc, axis=0, keepdims=True), db + jnp.sum(df, axis=0, keepdims=True)

        _, db = lax.fori_loop(0, nb, blk, (jnp.zeros((1, LANE), F32), jnp.zeros((1, LANE), F32)))
        db_ref[...] = db

    return pl.pallas_call(body, out_shape=[jax.ShapeDtypeStruct((rows, LANE), BF), jax.ShapeDtypeStruct((1, LANE), F32)],
                          name="forget_backward", compiler_params=_params())(f_raw, b_f, dcum)


def _t5_bucket(dist):
    max_exact = REL_BUCKETS // 2
    n = jnp.maximum(dist.astype(F32), 1.0)
    large = max_exact + (jnp.log(n / max_exact) / math.log(REL_MAX_DIST / max_exact)
                         * (REL_BUCKETS - max_exact)).astype(jnp.int32)
    large = jnp.minimum(large, REL_BUCKETS - 1)
    return jnp.where(dist < max_exact, dist, large)


def _dil_buckets(dilation):
    i = jnp.arange(Q_BLOCK)[:, None]
    j = jnp.arange(Q_BLOCK)[None, :]
    cur = _t5_bucket(jnp.clip(i - j, 0) * dilation).astype(jnp.int32)
    prev = _t5_bucket(jnp.clip(Q_BLOCK + i - j, 0) * dilation).astype(jnp.int32)
    return cur, prev


def _dil_bias_tiles(tbl_ref, bc_ref, bp_ref, bias_ref, group, hp, pairs):
    ii = lax.broadcasted_iota(jnp.int32, (Q_BLOCK, Q_BLOCK), 0)
    jj = lax.broadcasted_iota(jnp.int32, (Q_BLOCK, Q_BLOCK), 1)
    for hh in range(2 * pairs):
        col = group * HEADS + 2 * pairs * hp + hh
        acc_c = jnp.zeros((Q_BLOCK, Q_BLOCK), F32)
        acc_p = jnp.zeros((Q_BLOCK, Q_BLOCK), F32)
        for b in range(REL_BUCKETS):
            val = tbl_ref[b, col]
            acc_c = jnp.where(bc_ref[...] == b, val, acc_c)
            acc_p = jnp.where(bp_ref[...] == b, val, acc_p)
        bias_ref[2 * hh] = jnp.where(jj <= ii, acc_c, NEG_INF)
        bias_ref[2 * hh + 1] = jnp.where(jj >= ii, acc_p, NEG_INF)


def _dil_view(qkv, group, dilation):
    if dilation == 1:
        return qkv
    width = 3 * HEADS * HEAD_DIM
    return qkv[:, group * width:(group + 1) * width].reshape(qkv.shape[0] // dilation, dilation * width)


def _dil_specs(group, dilation, length):
    width = DIL_PAIRS[dilation] * LANE
    per = 8 // DIL_PAIRS[dilation]

    def col(kind):
        if dilation == 1:
            return pl.BlockSpec((length, width), lambda hp, r: (0, (group * 3 + kind) * per + hp))
        return pl.BlockSpec((length, width), lambda hp, r: (0, (r * 3 + kind) * per + hp))

    out = pl.BlockSpec((length, width), lambda hp, r: (0, r * per + hp))
    tile = pl.BlockSpec((Q_BLOCK, Q_BLOCK), lambda hp, r: (0, 0))
    table = pl.BlockSpec(memory_space=pltpu.SMEM)
    return col, out, tile, table


def _dil_forward(view, group, dilation, table, buckets):
    length = view.shape[0]
    rows = length * dilation
    pairs = DIL_PAIRS[dilation]
    nb = length // Q_BLOCK
    scale = HEAD_DIM ** -0.5
    qb = Q_BLOCK

    def body(tbl_ref, bc_ref, bp_ref, q_ref, k_ref, v_ref, o_ref, lse_ref, bias_ref):
        hp = pl.program_id(0)

        @pl.when(pl.program_id(1) == 0)
        def _():
            _dil_bias_tiles(tbl_ref, bc_ref, bp_ref, bias_ref, group, hp, pairs)

        lo = lax.broadcasted_iota(jnp.int32, (1, LANE), 1) < HEAD_DIM
        nt = (((1,), (1,)), ((), ()))

        def blk(n, first):
            cur = pl.ds(0, qb) if first else pl.ds(pl.multiple_of(n * qb, qb), qb)
            prev = None if first else pl.ds(pl.multiple_of((n - 1) * qb, qb), qb)
            logits = []
            for pair in range(pairs):
                lanes = slice(pair * LANE, (pair + 1) * LANE)
                qn = q_ref[cur, lanes] * scale
                for hh in range(2):
                    qm = jnp.where(lo if hh == 0 else ~lo, qn, jnp.zeros_like(qn))
                    s_c = lax.dot_general(qm, k_ref[cur, lanes], nt, preferred_element_type=F32)
                    s_p = None if first else lax.dot_general(qm, k_ref[prev, lanes], nt, preferred_element_type=F32)
                    logits.append((s_c, s_p))
            for pair in range(pairs):
                lanes = slice(pair * LANE, (pair + 1) * LANE)
                outs, lses = [], []
                for hh in range(2):
                    bias = 4 * pair + 2 * hh
                    s_c, s_p = logits[2 * pair + hh]
                    s_c = s_c + bias_ref[bias]
                    m = jnp.max(s_c, axis=1, keepdims=True)
                    if not first:
                        s_p = s_p + bias_ref[bias + 1]
                        m = jnp.maximum(m, jnp.max(s_p, axis=1, keepdims=True))
                    e_c = jnp.exp(s_c - m)
                    l = jnp.sum(e_c, axis=1, keepdims=True)
                    acc = jnp.dot(e_c.astype(BF), v_ref[cur, lanes], preferred_element_type=F32)
                    if not first:
                        e_p = jnp.exp(s_p - m)
                        l = l + jnp.sum(e_p, axis=1, keepdims=True)
                        acc = acc + jnp.dot(e_p.astype(BF), v_ref[prev, lanes], preferred_element_type=F32)
                    outs.append(acc * (1.0 / l))
                    lses.append(m + jnp.log(l))
                o_ref[cur, lanes] = jnp.where(lo, outs[0], outs[1])
                lse_ref[cur, lanes] = jnp.where(lo, lses[0], lses[1])
            return 0

        blk(0, True)
        if nb > 1:
            lax.fori_loop(1, nb, lambda n, _: blk(n, False), 0)

    col, out, tile, tbl = _dil_specs(group, dilation, length)
    bc, bp = buckets
    o, lse = pl.pallas_call(
        body, out_shape=[jax.ShapeDtypeStruct((length, dilation * D_MODEL), F32)] * 2,
        grid=(8 // pairs, dilation), in_specs=[tbl, tile, tile, col(0), col(1), col(2)], out_specs=[out, out],
        scratch_shapes=[pltpu.VMEM((4 * pairs, qb, qb), F32)], name=f"dilated_forward_{dilation}",
        compiler_params=_params(("arbitrary", "arbitrary")))(
            table, bc, bp, view, view, view)
    return o.reshape(rows, D_MODEL), lse.reshape(rows, D_MODEL)


def _dil_backward(view, group, dilation, table, buckets, do_g, lse, dlt):
    length = view.shape[0]
    rows = length * dilation
    pairs = DIL_PAIRS[dilation]
    nb = length // Q_BLOCK
    scale = HEAD_DIM ** -0.5
    qb = Q_BLOCK

    def body(tbl_ref, bc_ref, bp_ref, q_ref, k_ref, v_ref, do_ref, lse_ref, dlt_ref,
             dq_ref, dk_ref, dv_ref, db_ref, bias_ref, dk_acc, dv_acc):
        hp = pl.program_id(0)

        @pl.when(pl.program_id(1) == 0)
        def _():
            _dil_bias_tiles(tbl_ref, bc_ref, bp_ref, bias_ref, group, hp, pairs)
            db_ref[...] = jnp.zeros_like(db_ref)

        dk_acc[...] = jnp.zeros_like(dk_acc)
        dv_acc[...] = jnp.zeros_like(dv_acc)
        lo = lax.broadcasted_iota(jnp.int32, (1, LANE), 1) < HEAD_DIM
        tn = (((0,), (0,)), ((), ()))
        nt = (((1,), (1,)), ((), ()))

        def blk(n, first):
            cur = pl.ds(0, qb) if first else pl.ds(pl.multiple_of(n * qb, qb), qb)
            prev = None if first else pl.ds(pl.multiple_of((n - 1) * qb, qb), qb)
            inputs = []
            for pair in range(pairs):
                lanes = slice(pair * LANE, (pair + 1) * LANE)
                qn = q_ref[cur, lanes] * scale
                don = do_ref[cur, lanes]
                for hh in range(2):
                    mask = lo if hh == 0 else ~lo
                    qm = jnp.where(mask, qn, jnp.zeros_like(qn))
                    dom = jnp.where(mask, don, jnp.zeros_like(don))
                    stage = [qm, dom, lax.dot_general(qm, k_ref[cur, lanes], nt, preferred_element_type=F32),
                             lax.dot_general(dom, v_ref[cur, lanes], nt, preferred_element_type=F32)]
                    if not first:
                        stage += [lax.dot_general(qm, k_ref[prev, lanes], nt, preferred_element_type=F32),
                                  lax.dot_general(dom, v_ref[prev, lanes], nt, preferred_element_type=F32)]
                    inputs.append(stage)
            for pair in range(pairs):
                lanes = slice(pair * LANE, (pair + 1) * LANE)
                kc = k_ref[cur, lanes]
                if not first:
                    kp = k_ref[prev, lanes]
                lse_n = lse_ref[cur, lanes]
                dlt_n = dlt_ref[cur, lanes]
                dqs = []
                dkc = jnp.zeros((qb, LANE), F32)
                dkp = jnp.zeros((qb, LANE), F32)
                dvc = jnp.zeros((qb, LANE), F32)
                dvp = jnp.zeros((qb, LANE), F32)
                for hh in range(2):
                    bias = 4 * pair + 2 * hh
                    mask = lo if hh == 0 else ~lo
                    qm, dom, s_c, dp_c = inputs[2 * pair + hh][:4]
                    lse_h = jnp.max(jnp.where(mask, lse_n, -3e38), axis=1, keepdims=True)
                    dlt_h = jnp.max(jnp.where(mask, dlt_n, -3e38), axis=1, keepdims=True)
                    p_c = jnp.exp(s_c + bias_ref[bias] - lse_h)
                    ds_c = p_c * (dp_c - dlt_h)
                    db_ref[pair, 2 * hh] += ds_c
                    dsc_b = ds_c.astype(BF)
                    dq = jnp.dot(dsc_b, kc, preferred_element_type=F32)
                    dkc = dkc + lax.dot_general(dsc_b, qm, tn, preferred_element_type=F32)
                    dvc = dvc + lax.dot_general(p_c.astype(BF), dom, tn, preferred_element_type=F32)
                    if not first:
                        s_p, dp_p = inputs[2 * pair + hh][4:]
                        p_p = jnp.exp(s_p + bias_ref[bias + 1] - lse_h)
                        ds_p = p_p * (dp_p - dlt_h)
                        db_ref[pair, 2 * hh + 1] += ds_p
                        dsp_b = ds_p.astype(BF)
                        dq = dq + jnp.dot(dsp_b, kp, preferred_element_type=F32)
                        dkp = dkp + lax.dot_general(dsp_b, qm, tn, preferred_element_type=F32)
                        dvp = dvp + lax.dot_general(p_p.astype(BF), dom, tn, preferred_element_type=F32)
                    dqs.append(dq)
                dq_ref[cur, lanes] = (jnp.where(lo, dqs[0], dqs[1]) * scale).astype(BF)
                dk_acc[cur, lanes] += dkc
                dv_acc[cur, lanes] += dvc
                if not first:
                    dk_acc[prev, lanes] += dkp
                    dv_acc[prev, lanes] += dvp
            return 0

        blk(0, True)
        if nb > 1:
            lax.fori_loop(1, nb, lambda n, _: blk(n, False), 0)
        dk_ref[...] = dk_acc[...].astype(BF)
        dv_ref[...] = dv_acc[...].astype(BF)

    col, out, tile, tbl = _dil_specs(group, dilation, length)
    bc, bp = buckets
    wide = (length, dilation * D_MODEL)
    dq, dk, dv, db = pl.pallas_call(
        body, out_shape=[jax.ShapeDtypeStruct(wide, BF)] * 3 + [jax.ShapeDtypeStruct((8, 4, qb, qb), F32)],
        grid=(8 // pairs, dilation), in_specs=[tbl, tile, tile, col(0), col(1), col(2), out, out, out],
        out_specs=[out, out, out, pl.BlockSpec((pairs, 4, qb, qb), lambda hp, r: (hp, 0, 0, 0))],
        scratch_shapes=[pltpu.VMEM((4 * pairs, qb, qb), F32), pltpu.VMEM((length, pairs * LANE), F32),
                        pltpu.VMEM((length, pairs * LANE), F32)],
        name=f"dilated_backward_{dilation}", compiler_params=_params(("arbitrary", "arbitrary")))(
            table, bc, bp, view, view, view,
            do_g.reshape(wide), lse.reshape(wide), dlt.reshape(wide))
    return dq.reshape(rows, D_MODEL), dk.reshape(rows, D_MODEL), dv.reshape(rows, D_MODEL), db


def _head_sums(x, lo):
    s0 = jnp.sum(jnp.where(lo, x, 0.0), axis=1, keepdims=True)
    s1 = jnp.sum(jnp.where(lo, 0.0, x), axis=1, keepdims=True)
    return jnp.where(lo, s0, s1)


def _dil_merge_forward(outs, lses):
    rows = outs[0].shape[0]

    def body(o0, o1, o2, l0, l1, l2, o_ref):
        ls = [l0[...], l1[...], l2[...]]
        m = jnp.maximum(jnp.maximum(ls[0], ls[1]), ls[2])
        es = [jnp.exp(v - m) for v in ls]
        tot = es[0] + es[1] + es[2]
        o_ref[...] = ((es[0] * o0[...] + es[1] * o1[...] + es[2] * o2[...]) / tot).astype(BF)

    blk = pl.BlockSpec((ROW_TILE, LANE), lambda i, j: (i, j))
    return pl.pallas_call(body, out_shape=jax.ShapeDtypeStruct((rows, D_MODEL), BF), grid=(rows // ROW_TILE, 8),
                          in_specs=[blk] * 6, out_specs=blk, name="dilated_merge_forward",
                          compiler_params=_params(("parallel", "parallel")))(*outs, *lses)


def _dil_merge_backward(outs, lses, do):
    rows = outs[0].shape[0]

    def body(o0, o1, o2, l0, l1, l2, do_ref, d0, d1, d2, t0, t1, t2):
        lo = lax.broadcasted_iota(jnp.int32, (1, LANE), 1) < HEAD_DIM
        ls = [l0[...], l1[...], l2[...]]
        os_ = [o0[...], o1[...], o2[...]]
        m = jnp.maximum(jnp.maximum(ls[0], ls[1]), ls[2])
        es = [jnp.exp(v - m) for v in ls]
        inv = 1.0 / (es[0] + es[1] + es[2])
        alphas = [e * inv for e in es]
        dov = do_ref[...]
        merged = alphas[0] * os_[0] + alphas[1] * os_[1] + alphas[2] * os_[2]
        dot = _head_sums(dov * merged, lo)
        for a, d_ref, t_ref in zip(alphas, (d0, d1, d2), (t0, t1, t2)):
            d_ref[...] = (a * dov).astype(BF)
            t_ref[...] = a * dot

    blk = pl.BlockSpec((ROW_TILE, LANE), lambda i, j: (i, j))
    res = pl.pallas_call(
        body, out_shape=[jax.ShapeDtypeStruct((rows, D_MODEL), BF)] * 3 + [jax.ShapeDtypeStruct((rows, D_MODEL), F32)] * 3,
        grid=(rows // ROW_TILE, 8), in_specs=[blk] * 7, out_specs=[blk] * 6, name="dilated_merge_backward",
        compiler_params=_params(("parallel", "parallel")))(*outs, *lses, do)
    return res[:3], res[3:]


def _rel_bias_grad(dbs, buckets):
    def body(db_ref, bc_ref, bp_ref, o_ref):
        g = pl.program_id(0)
        hp = pl.program_id(1)

        @pl.when((g == 0) & (hp == 0))
        def _():
            o_ref[...] = jnp.zeros_like(o_ref)

        rr = lax.broadcasted_iota(jnp.int32, (REL_BUCKETS, LANE), 0)
        cc = lax.broadcasted_iota(jnp.int32, (REL_BUCKETS, LANE), 1)
        bc = bc_ref[0]
        bp = bp_ref[0]
        acc = jnp.zeros((REL_BUCKETS, LANE), F32)
        for hh in range(2):
            col = g * HEADS + 2 * hp + hh
            d_c = db_ref[0, 0, 2 * hh]
            d_p = db_ref[0, 0, 2 * hh + 1]
            for b in range(REL_BUCKETS):
                val = (jnp.sum(jnp.where(bc == b, d_c, 0.0), keepdims=True)
                       + jnp.sum(jnp.where(bp == b, d_p, 0.0), keepdims=True))
                acc = jnp.where((rr == b) & (cc == col), val, acc)
        o_ref[...] += acc

    db_all = jnp.stack(dbs)
    bc_all = jnp.stack([b[0] for b in buckets])
    bp_all = jnp.stack([b[1] for b in buckets])
    tile = pl.BlockSpec((1, Q_BLOCK, Q_BLOCK), lambda g, hp: (g, 0, 0))
    return pl.pallas_call(
        body, out_shape=jax.ShapeDtypeStruct((REL_BUCKETS, LANE), F32), grid=(3, 8),
        in_specs=[pl.BlockSpec((1, 1, 4, Q_BLOCK, Q_BLOCK), lambda g, hp: (g, hp, 0, 0, 0)), tile, tile],
        out_specs=pl.BlockSpec((REL_BUCKETS, LANE), lambda g, hp: (0, 0)), name="rel_bias_grad",
        compiler_params=_params(("arbitrary", "arbitrary")))(db_all, bc_all, bp_all)


def _mla_forward(hn, w, tables):
    a = _matmul(hn, w["w_a"], name="mla_a")
    cq, ckv, kr = _mla_mid_forward(a, w["q_norm"], w["kv_norm"], tables)
    q_raw = _matmul(cq, w["w_uq"], name="mla_uq")
    q = _rope_heads(q_raw, tables, False, "rope_forward")
    kv = _matmul(ckv, w["w_ukv"], b_chunks=True, out_dtype=BF, name="mla_ukv")
    scale = (HEAD_DIM + MLA_ROPE) ** -0.5
    o, lse = _attn_forward(q, kv, 0, kr, None, None, scale, MLA_GROUP, "mla_attention_forward")
    y = _matmul(o, w["w_o"], name="attn_out")
    return y, dict(hn=hn, a=a, cq=cq, ckv=ckv, kr=kr, q=q, kv=kv, o=o, lse=lse)


def _mla_backward(dy, w, s, tables):
    scale = (HEAD_DIM + MLA_ROPE) ** -0.5
    g = {}
    g["w_o"] = _matmul(s["o"], dy, ta=True, out_dtype=BF, name="attn_out_dw")
    do = _matmul(dy, w["w_o"], tb=True, out_dtype=BF, name="attn_out_dx")
    dq, dkv, dkr = _attn_backward(s["q"], s["kv"], 0, s["kr"], None, None, s["o"], do, s["lse"], scale,
                                  MLA_GROUP, "mla_attention_backward")
    dq_raw = _rope_heads(dq, tables, True, "rope_backward")
    g["w_uq"] = _matmul(s["cq"], dq_raw, ta=True, out_dtype=BF, name="mla_uq_dw")
    dcq = _matmul(dq_raw, w["w_uq"], tb=True, name="mla_uq_dx")
    g["w_ukv"] = _matmul(s["ckv"], dkv, ta=True, out_chunks=True, out_dtype=BF, name="mla_ukv_dw")
    dckv = _matmul(dkv, w["w_ukv"], tb=True, b_chunks=True, name="mla_ukv_dx")
    da, g["q_norm"], g["kv_norm"] = _mla_mid_backward(s["a"], w["q_norm"], w["kv_norm"], tables, dcq, dckv, dkr)
    g["w_a"] = _matmul(s["hn"], da, ta=True, out_dtype=BF, name="mla_a_dw")
    dhn = _matmul(da, w["w_a"], tb=True, name="mla_a_dx")
    return dhn, g


def _fox_forward(hn, w):
    qkv = _matmul(hn, w["w_qkv"], out_dtype=BF, name="fox_qkv")
    f_raw = _matmul(hn, w["w_f"], name="fox_f")
    cum = _forget_forward(f_raw, w["b_f"])
    cum_heads = cum[:, :HEADS].T
    cum_col, cum_row = cum_heads[:, :, None], cum_heads[:, None, :]
    o, lse = _attn_forward(qkv, qkv, HEADS, None, cum_col, cum_row, HEAD_DIM ** -0.5, FOX_GROUP,
                           "fox_attention_forward")
    y = _matmul(o, w["w_o"], name="attn_out")
    return y, dict(hn=hn, qkv=qkv, f_raw=f_raw, cum_col=cum_col, cum_row=cum_row, o=o, lse=lse)


def _fox_backward(dy, w, s):
    g = {}
    g["w_o"] = _matmul(s["o"], dy, ta=True, out_dtype=BF, name="attn_out_dw")
    do = _matmul(dy, w["w_o"], tb=True, out_dtype=BF, name="attn_out_dx")
    dq, dkv, dck, dcq = _attn_backward(s["qkv"], s["qkv"], HEADS, None, s["cum_col"], s["cum_row"], s["o"], do,
                                       s["lse"], HEAD_DIM ** -0.5, FOX_GROUP, "fox_attention_backward")
    dcum = jnp.pad((dck[:, 0, :] + dcq[:, :, 0]).T, ((0, 0), (0, LANE - HEADS)))
    df, g["b_f"] = _forget_backward(s["f_raw"], w["b_f"], dcum)
    dqkv = jnp.concatenate([dq, dkv], axis=1)
    g["w_qkv"] = _matmul(s["hn"], dqkv, ta=True, out_dtype=BF, name="fox_qkv_dw")
    g["w_f"] = _matmul(s["hn"], df, ta=True, out_dtype=BF, name="fox_f_dw")
    dhn = _matmul(dqkv, w["w_qkv"], tb=True, name="fox_qkv_dx")
    dhn = _matmul(df, w["w_f"], tb=True, add=dhn, name="fox_f_dx")
    return dhn, g


def _dil_mixer_forward(hn, w, buckets):
    qkv = _matmul(hn, w["w_qkv"], b_chunks=True, out_dtype=BF, name="dil_qkv")
    views = [_dil_view(qkv, grp, dilation) for grp, (_, dilation) in enumerate(DIL_PATTERNS)]
    outs, lses = [], []
    for grp, (_, dilation) in enumerate(DIL_PATTERNS):
        o_g, lse_g = _dil_forward(views[grp], grp, dilation, w["rel_bias"], buckets[grp])
        outs.append(o_g)
        lses.append(lse_g)
    o = _dil_merge_forward(outs, lses)
    y = _matmul(o, w["w_o"], name="dil_out")
    return y, dict(hn=hn, views=views, outs=outs, lses=lses, o=o)


def _dil_mixer_backward(dy, w, s, buckets):
    g = {}
    g["w_o"] = _matmul(s["o"], dy, ta=True, out_dtype=BF, name="dil_out_dw")
    do = _matmul(dy, w["w_o"], tb=True, name="dil_out_dx")
    do_gs, dlts = _dil_merge_backward(s["outs"], s["lses"], do)
    parts, dbs = [], []
    for grp, (_, dilation) in enumerate(DIL_PATTERNS):
        dq, dk, dv, db = _dil_backward(s["views"][grp], grp, dilation, w["rel_bias"], buckets[grp], do_gs[grp],
                                       s["lses"][grp], dlts[grp])
        parts += [dq, dk, dv]
        dbs.append(db)
    dqkv = jnp.concatenate(parts, axis=1)
    g["rel_bias"] = _rel_bias_grad(dbs, buckets)
    g["w_qkv"] = _matmul(s["hn"], dqkv, ta=True, out_chunks=True, out_dtype=BF, name="dil_qkv_dw")
    dhn = _matmul(dqkv, w["w_qkv"], tb=True, b_chunks=True, name="dil_qkv_dx")
    return dhn, g


def _mixer_weights(i, lw, small):
    mixer, j = i % N_MIXERS, i // N_MIXERS
    if mixer == 0:
        return dict(lw["mixer"], q_norm=small["mla_q_norm"][j][None, :], kv_norm=small["mla_kv_norm"][j][None, :])
    if mixer == 1:
        return dict(lw["mixer"], rel_bias=small["rel_bias"])
    return dict(lw["mixer"], b_f=jnp.pad(small["fox_b_f"][j][None, :], ((0, 0), (0, LANE - HEADS))))


MIXER_PART, COMMON_PART = 0, 1


def _run_layers(x, p, positions, target, get_part, get_small, put_part):
    tables = _rope_tables(positions)
    buckets = [_dil_buckets(d) for _, d in DIL_PATTERNS]
    layers, saved = [], []
    h = x
    first = get_part(0, MIXER_PART, positions)
    small = get_small()

    def gain(i, k):
        return small["norm_g"][i, k][None, :]

    hn = _prenorm(h, gain(0, 0))
    sq = dh = None
    for i in range(DEPTH):
        mixer = i % N_MIXERS
        lw = dict(first if i == 0 else get_part(i, MIXER_PART, h))
        mw = _mixer_weights(i, lw, small)
        if mixer == 0:
            y, ms = _mla_forward(hn, mw, tables)
        elif mixer == 1:
            y, ms = _dil_mixer_forward(hn, mw, buckets)
        else:
            y, ms = _fox_forward(hn, mw)
        if "ffn_w_in" not in lw:
            lw.update(get_part(i, COMMON_PART, y))
        layers.append(lw)
        h1, hn2 = _post_residual(h, y, gain(i, 1), gain(i, 2))
        gu = _matmul(hn2, lw["ffn_w_in"], b_chunks=True, out_dtype=BF, name="ffn_in")
        act = _swiglu_forward(gu)
        f = _matmul(act, lw["ffn_w_out"], name="ffn_out")
        h2, h2b = _post_residual(h1, f, gain(i, 3), None)
        pp = _matmul(p[i], lw["ple_w_proj"], b_chunks=True, name="ple_proj")
        z = _matmul(h2b, lw["ple_w_gate"], name="ple_gate")
        saved.append(dict(h=h, y=y, ms=ms, h1=h1, hn2=hn2, gu=gu, act=act, f=f, h2b=h2b, pp=pp, z=z))
        if i + 1 < DEPTH:
            h, hn = _ple_forward(h2, pp, z, gain(i + 1, 0))
        else:
            dh, sq = _ple_loss(h2, pp, z, target)

    norm_rows = [[None] * 4 for _ in range(DEPTH)]
    sg = dict(mla_q_norm={}, mla_kv_norm={}, rel_bias=None, fox_b_f={})
    for i in reversed(range(DEPTH)):
        s, lw = saved[i], layers[i]
        mixer, j = i % N_MIXERS, i // N_MIXERS
        mw = _mixer_weights(i, lw, small)
        lg = {}
        dpp, dz = _ple_backward(dh, s["pp"], s["z"])
        lg["ple_w_proj"] = _matmul(p[i], dpp, ta=True, out_chunks=True, out_dtype=BF, name="ple_proj_dw")
        lg["ple_w_gate"] = _matmul(s["h2b"], dz, ta=True, out_dtype=BF, name="ple_gate_dw")
        dh2 = _matmul(dz, lw["ple_w_gate"], tb=True, add=dh, name="ple_gate_dx")
        df, norm_rows[i][3] = _rms_backward(s["f"], gain(i, 3), dh2, None, BF)
        lg["ffn_w_out"] = _matmul(s["act"], df, ta=True, out_dtype=BF, name="ffn_out_dw")
        dact = _matmul(df, lw["ffn_w_out"], tb=True, out_dtype=BF, name="ffn_out_dx")
        dgu = _swiglu_backward(s["gu"], dact)
        lg["ffn_w_in"] = _matmul(s["hn2"], dgu, ta=True, out_chunks=True, out_dtype=BF, name="ffn_in_dw")
        split = i in SPLIT_LAYERS
        zero = put_part(i, COMMON_PART, lg)[0:1, 0:1] if split else 0.0
        dhn2 = _matmul(dgu, lw["ffn_w_in"], tb=True, b_chunks=True, name="ffn_in_dx")
        dh1, norm_rows[i][2] = _rms_backward(s["h1"], gain(i, 2), dhn2, dh2, F32)
        dy, norm_rows[i][1] = _rms_backward(s["y"], gain(i, 1) + zero, dh1, None, BF)
        if mixer == 0:
            dhn, mg = _mla_backward(dy, mw, s["ms"], tables)
            sg["mla_q_norm"][j] = mg.pop("q_norm")
            sg["mla_kv_norm"][j] = mg.pop("kv_norm")
        elif mixer == 1:
            dhn, mg = _dil_mixer_backward(dy, mw, s["ms"], buckets)
            rel = mg.pop("rel_bias")[:, :3 * HEADS]
            sg["rel_bias"] = rel if sg["rel_bias"] is None else sg["rel_bias"] + rel
        else:
            dhn, mg = _fox_backward(dy, mw, s["ms"])
            sg["fox_b_f"][j] = mg.pop("b_f")[:, :HEADS]
        token = put_part(i, MIXER_PART, dict(mixer=mg) if split else dict(lg, mixer=mg))
        dh, norm_rows[i][0] = _rms_backward(s["h"], gain(i, 0) + token[0:1, 0:1], dhn, dh1, F32)
    small_grads = dict(norm_g=jnp.stack([jnp.concatenate(row, axis=0) for row in norm_rows]),
                       rel_bias=sg["rel_bias"])
    for k in ("mla_q_norm", "mla_kv_norm", "fox_b_f"):
        small_grads[k] = jnp.concatenate([sg[k][j] for j in sorted(sg[k])], axis=0)
    return sq, dh, small_grads


BIG = ("ffn_w_in", "ffn_w_out", "ple_w_proj", "ple_w_gate", "mla_w_a", "mla_w_uq", "mla_w_ukv", "mla_w_o",
       "dil_w_qkv", "dil_w_o", "fox_w_qkvf", "fox_w_o")
SMALL_SHARDED = ("norm_g", "mla_q_norm", "mla_kv_norm")
SMALL_REPLICATED = ("rel_bias", "fox_b_f")
WEIGHTS = ("norm_g", "ffn_w_in", "ffn_w_out", "ple_w_proj", "ple_w_gate", "rel_bias", "mla_w_a", "mla_q_norm",
           "mla_kv_norm", "mla_w_uq", "mla_w_ukv", "mla_w_o", "dil_w_qkv", "dil_w_o", "fox_w_qkvf", "fox_b_f", "fox_w_o")


TRANSPOSED = "fox_w_qkvf"
SPLIT_LAYERS = (0, 1, 2, 3)
LAYER_COMMON = ("ffn_w_in", "ffn_w_out", "ple_w_proj", "ple_w_gate")
MIXER_WEIGHTS = (("mla_w_a", "mla_w_uq", "mla_w_ukv", "mla_w_o"), ("dil_w_qkv", "dil_w_o"), ("fox_w_qkvf", "fox_w_o"))


def _part_names(i, part):
    if i in SPLIT_LAYERS:
        return MIXER_WEIGHTS[i % N_MIXERS] if part == MIXER_PART else LAYER_COMMON
    return MIXER_WEIGHTS[i % N_MIXERS] + LAYER_COMMON if part == MIXER_PART else ()


def _layer_slot(name, i):
    return i if name in LAYER_COMMON else i // N_MIXERS


def _merge_rows(chunks):
    n, r, c = chunks.shape
    return chunks.reshape(n * r, c)


def _merge_cols(chunks):
    n, r, c = chunks.shape
    return chunks.transpose(1, 0, 2).reshape(r, n * c)


def _pad_heads_out(wo):
    w3 = wo.reshape(HEADS, HEAD_DIM, D_MODEL)
    return jnp.pad(w3, ((0, 0), (HEAD_DIM, 0), (0, 0))).reshape(HEADS * LANE, D_MODEL)


def _part_to_compute(i, part, ch):
    lw = {}
    if "ffn_w_in" in ch:
        lw.update(ffn_w_in=ch["ffn_w_in"], ffn_w_out=_merge_rows(ch["ffn_w_out"]), ple_w_proj=ch["ple_w_proj"],
                  ple_w_gate=_merge_rows(ch["ple_w_gate"]))
    if part == COMMON_PART:
        return lw
    mixer = i % N_MIXERS
    if mixer == 0:
        wa = _merge_rows(ch["mla_w_a"])
        rank = MLA_Q_RANK + MLA_KV_RANK
        wa_p = jnp.concatenate([wa[:, :rank], jnp.zeros((wa.shape[0], 64), wa.dtype), wa[:, rank:],
                                jnp.zeros((wa.shape[0], 32), wa.dtype)], axis=1)
        wuq = _merge_cols(ch["mla_w_uq"]).reshape(MLA_Q_RANK, HEADS, HEAD_DIM + MLA_ROPE)
        wuq_p = jnp.pad(wuq, ((0, 0), (0, 0), (0, LANE - HEAD_DIM - MLA_ROPE))).reshape(MLA_Q_RANK, HEADS * LANE)
        lw["mixer"] = dict(w_a=wa_p, w_uq=wuq_p, w_ukv=ch["mla_w_ukv"], w_o=_pad_heads_out(_merge_rows(ch["mla_w_o"])))
    elif mixer == 1:
        lw["mixer"] = dict(w_qkv=ch["dil_w_qkv"], w_o=_merge_rows(ch["dil_w_o"]))
    else:
        wf = _merge_rows(ch["fox_w_qkvf"]).T
        inner = HEADS * HEAD_DIM
        q3 = wf[:, :inner].reshape(D_MODEL, HEADS, HEAD_DIM)
        k3 = wf[:, inner:2 * inner].reshape(D_MODEL, HEADS, HEAD_DIM)
        v3 = wf[:, 2 * inner:3 * inner].reshape(D_MODEL, HEADS, HEAD_DIM)
        q_p = jnp.pad(q3, ((0, 0), (0, 0), (0, HEAD_DIM))).reshape(D_MODEL, HEADS * LANE)
        kv_p = jnp.concatenate([k3, v3], axis=2).reshape(D_MODEL, HEADS * LANE)
        f_p = jnp.pad(wf[:, 3 * inner:], ((0, 0), (0, LANE - HEADS)))
        lw["mixer"] = dict(w_qkv=jnp.concatenate([q_p, kv_p], axis=1), w_f=f_p,
                           w_o=_pad_heads_out(_merge_rows(ch["fox_w_o"])))
    return lw


def _part_contributions(i, part, lg, chunk_shapes):
    spec = {k: jax.ShapeDtypeStruct(s, BF) for k, s in chunk_shapes.items()}
    (contrib,) = jax.linear_transpose(functools.partial(_part_to_compute, i, part), spec)(lg)
    return contrib


def _chip_peers():
    x, y, c = lax.axis_index("x"), lax.axis_index("y"), lax.axis_index("c")
    peers = [(1 - x, y), (x, 1 - y), (1 - x, 1 - y)]
    return x, y, c, peers


SEM_SPEC = pl.BlockSpec(memory_space=pltpu.SEMAPHORE)
ANY_SPEC = pl.BlockSpec(memory_space=pl.ANY)
SPLIT_EFFECT = pltpu.SideEffectType.DATAFLOW_SIDE_EFFECTING


def _own_slot(shard):
    me = 2 * lax.axis_index("x") + lax.axis_index("y")
    return lax.dynamic_update_index_in_dim(lax.empty((N_CHIPS,) + shard.shape, shard.dtype), shard[None], me, 0)


def _spread_copy(src, land, k, peer, c, send_sems, recv_sems, index, src_slot, slot):
    px, py = peer
    return pltpu.make_async_remote_copy(
        src_ref=src.at[src_slot], dst_ref=land.at[slot],
        send_sem=send_sems.at[3 * index + k], recv_sem=recv_sems.at[3 * index + k],
        device_id=(px, py, c), device_id_type=MESH)


def _spread_start(bufs, srcs, after, name):
    n = len(bufs)
    exchange = srcs is not None
    arrays = (list(srcs) if exchange else []) + list(bufs)
    na = len(arrays)

    def body(*refs):
        src, land = refs[:n], refs[na - n:na]
        send_sems, recv_sems = refs[na + 1], refs[na + 2]
        token = refs[-1]
        x, y, c, peers = _chip_peers()
        me = 2 * x + y
        for w in range(n):
            for k, peer in enumerate(peers):
                src_slot = 2 * peer[0] + peer[1] if exchange else me
                _spread_copy(src[w], land[w], k, peer, c, send_sems, recv_sems, w, src_slot, me).start()
        token[...] = jnp.zeros_like(token)

    hbm = [pltpu.with_memory_space_constraint(a, pltpu.HBM) for a in arrays]
    out = pl.pallas_call(
        body, name=name,
        out_shape=(pltpu.SemaphoreType.DMA((3 * n,)), pltpu.SemaphoreType.DMA((3 * n,)),
                   *[pltpu.HBM(a.shape, a.dtype) for a in hbm], jax.ShapeDtypeStruct((8, LANE), F32)),
        in_specs=[HBM_SPEC] * na + [ANY_SPEC],
        out_specs=(SEM_SPEC, SEM_SPEC, *[HBM_SPEC] * na, pl.BlockSpec(memory_space=pltpu.VMEM)),
        input_output_aliases={w: 2 + w for w in range(na)},
        compiler_params=pltpu.CompilerParams(has_side_effects=SPLIT_EFFECT))(*hbm, after)
    return dict(send=out[0], recv=out[1], arrays=out[2:2 + na], n=n, token=out[-1], exchange=exchange)


def _spread_wait(handle, after, name):
    n, exchange = handle["n"], handle["exchange"]
    arrays = list(handle["arrays"])
    na = len(arrays)

    def body(*refs):
        src, land = refs[:n], refs[na - n:na]
        send_sems, recv_sems = refs[na], refs[na + 1]
        x, y, c, peers = _chip_peers()
        me = 2 * x + y
        for w in range(n):
            for k, peer in enumerate(peers):
                there = 2 * peer[0] + peer[1]
                cp = _spread_copy(src[w], land[w], k, peer, c, send_sems, recv_sems, w, there if exchange else me, there)
                cp.wait_send()
                cp.wait_recv()

    out = pl.pallas_call(
        body, name=name, out_shape=tuple(pltpu.HBM(a.shape, a.dtype) for a in arrays),
        in_specs=[HBM_SPEC] * na + [SEM_SPEC, SEM_SPEC, ANY_SPEC], out_specs=tuple([HBM_SPEC] * na),
        input_output_aliases={w: w for w in range(na)},
        compiler_params=pltpu.CompilerParams(has_side_effects=SPLIT_EFFECT))(*arrays, handle["send"], handle["recv"], after)
    return (list(out[n:]), list(out[:n])) if exchange else list(out)


def _sibling_copy(received, sent, land, k, me, peers, sibling, send_sems, recv_sems, index):
    slot = me if k == 3 else 2 * peers[k][0] + peers[k][1]
    src = sent if k == 3 else received
    return pltpu.make_async_remote_copy(
        src_ref=src.at[slot], dst_ref=land.at[slot], send_sem=send_sems.at[4 * index + k],
        recv_sem=recv_sems.at[4 * index + k], device_id=sibling, device_id_type=MESH)


def _sibling_start(received, sent, after, name):
    n = len(received)
    lands = [lax.empty(a.shape, a.dtype) for a in received]
    arrays = list(received) + list(sent) + lands

    def body(*refs):
        rec, snt, land = refs[:n], refs[n:2 * n], refs[2 * n:3 * n]
        send_sems, recv_sems = refs[3 * n + 1], refs[3 * n + 2]
        token = refs[-1]
        x, y, c, peers = _chip_peers()
        for w in range(n):
            for k in range(4):
                _sibling_copy(rec[w], snt[w], land[w], k, 2 * x + y, peers, (x, y, 1 - c), send_sems, recv_sems, w).start()
        token[...] = jnp.zeros_like(token)

    hbm = [pltpu.with_memory_space_constraint(a, pltpu.HBM) for a in arrays]
    out = pl.pallas_call(
        body, name=name,
        out_shape=(pltpu.SemaphoreType.DMA((4 * n,)), pltpu.SemaphoreType.DMA((4 * n,)),
                   *[pltpu.HBM(a.shape, a.dtype) for a in hbm], jax.ShapeDtypeStruct((8, LANE), F32)),
        in_specs=[HBM_SPEC] * (3 * n) + [ANY_SPEC],
        out_specs=(SEM_SPEC, SEM_SPEC, *[HBM_SPEC] * (3 * n), pl.BlockSpec(memory_space=pltpu.VMEM)),
        input_output_aliases={w: 2 + w for w in range(3 * n)},
        compiler_params=pltpu.CompilerParams(has_side_effects=SPLIT_EFFECT))(*hbm, after)
    return dict(send=out[0], recv=out[1], arrays=out[2:2 + 3 * n], n=n, token=out[-1])


def _sibling_wait(handle, after, name):
    n = handle["n"]
    arrays = list(handle["arrays"])

    def body(*refs):
        rec, snt, land = refs[:n], refs[n:2 * n], refs[2 * n:3 * n]
        send_sems, recv_sems = refs[3 * n], refs[3 * n + 1]
        x, y, c, peers = _chip_peers()
        for w in range(n):
            for k in range(4):
                cp = _sibling_copy(rec[w], snt[w], land[w], k, 2 * x + y, peers, (x, y, 1 - c), send_sems, recv_sems, w)
                cp.wait_send()
                cp.wait_recv()

    out = pl.pallas_call(
        body, name=name, out_shape=tuple(pltpu.HBM(a.shape, a.dtype) for a in arrays),
        in_specs=[HBM_SPEC] * (3 * n) + [SEM_SPEC, SEM_SPEC, ANY_SPEC], out_specs=tuple([HBM_SPEC] * (3 * n)),
        input_output_aliases={w: w for w in range(3 * n)},
        compiler_params=pltpu.CompilerParams(has_side_effects=SPLIT_EFFECT))(*arrays, handle["send"], handle["recv"], after)
    return list(out[:n]), list(out[n:2 * n]), list(out[2 * n:])


def _all_reduce_small(v):
    rows = v.shape[0]

    def body(v_ref, sum_ref, slots, send_sems, recv_sems):
        x, y, c = lax.axis_index("x"), lax.axis_index("y"), lax.axis_index("c")
        me = 4 * x + 2 * y + c
        slots[me] = v_ref[...]
        sends = []
        for k in range(1, N_DEV):
            bx, by, bc = (k >> 2) & 1, (k >> 1) & 1, k & 1
            peer = (x ^ bx, y ^ by, c ^ bc)
            rc = pltpu.make_async_remote_copy(src_ref=v_ref, dst_ref=slots.at[me], send_sem=send_sems.at[k],
                                              recv_sem=recv_sems.at[k], device_id=peer, device_id_type=MESH)
            rc.start()
            sends.append(rc)
        for k in range(1, N_DEV):
            bx, by, bc = (k >> 2) & 1, (k >> 1) & 1, k & 1
            src = 4 * (x ^ bx) + 2 * (y ^ by) + (c ^ bc)
            pltpu.make_async_remote_copy(src_ref=v_ref, dst_ref=slots.at[src], send_sem=send_sems.at[k],
                                         recv_sem=recv_sems.at[k], device_id=(x ^ bx, y ^ by, c ^ bc),
                                         device_id_type=MESH).wait_recv()
        for rc in sends:
            rc.wait_send()
        total = slots[0]
        for k in range(1, N_DEV):
            total = total + slots[k]
        sum_ref[...] = total

    vm = pl.BlockSpec(memory_space=pltpu.VMEM)
    return pl.pallas_call(
        body, out_shape=jax.ShapeDtypeStruct((rows, LANE), F32), in_specs=[vm], out_specs=vm,
        scratch_shapes=[pltpu.VMEM((N_DEV, rows, LANE), F32), pltpu.SemaphoreType.DMA((N_DEV,)),
                        pltpu.SemaphoreType.DMA((N_DEV,))], name="all_reduce_small")(v)


def _as_2d(a):
    return a.reshape(-1, a.shape[-1])


def _row_tile(rows, cols):
    for t in (512, 256, 128, 64, 32, 16):
        if rows % t == 0 and t * cols * 4 <= (1 << 20):
            return t
    return rows


def _adamw_weight(w, m, v, received, sent, sibling):
    layers = len(received)
    _, rows, cols = received[0].shape
    tr = _row_tile(rows, cols)
    by_columns = rows % tr != 0 or tr == rows and rows * cols * 4 > (2 << 20)
    if by_columns:
        assert layers == 1 and cols % (2 * LANE) == 0, (w.shape, received[0].shape)
        tr, tc, steps = rows, cols // 2, 2
        index = lambda i: (0, i)
    else:
        tc, steps = cols, rows // tr
        index = lambda i: (i, 0)
    where = (2 * lax.axis_index("x") + lax.axis_index("y")).astype(jnp.int32).reshape(1)

    def body(where_ref, w_ref, m_ref, v_ref, *rest):
        per_layer, (g_ref, d_ref, nm_ref, nv_ref) = rest[:3 * layers], rest[3 * layers:]
        me = where_ref[0]
        for layer in range(layers):
            r_ref, own_ref, s_ref = per_layer[3 * layer:3 * layer + 3]

            @pl.when(pl.program_id(0) == layer)
            def _():
                mine = theirs = None
                for k in range(N_CHIPS):
                    a = jnp.where(me == k, own_ref[...], r_ref[k]).astype(F32)
                    b = s_ref[k].astype(F32)
                    mine = a if mine is None else mine + a
                    theirs = b if theirs is None else theirs + b
                g = mine + theirs
                delta, nm, nv = _adamw_math(w_ref[...], g, m_ref[...], v_ref[...])
                g_ref[...] = g
                d_ref[...] = delta
                nm_ref[...] = nm
                nv_ref[...] = nv

    def held(layer, now, i):
        return jnp.where(now < layer, 0, jnp.where(now > layer, steps - 1, i))

    if by_columns:
        stacked = pl.BlockSpec((tr, tc), lambda now, i, where_ref: index(i))
    else:
        stacked = pl.BlockSpec((tr, tc), lambda now, i, where_ref: (now * steps + i, 0))
    in_specs = [stacked, stacked, stacked]
    args = [where, w, m, v]
    for layer in range(layers):
        four = pl.BlockSpec((N_CHIPS, tr, tc), lambda now, i, where_ref, layer=layer: (0,) + index(held(layer, now, i)))
        own = pl.BlockSpec((None, tr, tc),
                           lambda now, i, where_ref, layer=layer: (where_ref[0],) + index(held(layer, now, i)))
        in_specs += [four, own, four]
        args += [received[layer], sent[layer], sibling[layer]]
    grid_spec = pltpu.PrefetchScalarGridSpec(num_scalar_prefetch=1, grid=(layers, steps), in_specs=in_specs,
                                             out_specs=[stacked] * 4)
    return pl.pallas_call(body, out_shape=[jax.ShapeDtypeStruct(w.shape, F32)] * 4, grid_spec=grid_spec,
                          name="adamw_weight", compiler_params=_params(("arbitrary", "arbitrary")))(*args)


def _adamw_math(w, g, m, v):
    m = ADAM_B1 * m + (1.0 - ADAM_B1) * g
    v = ADAM_B2 * v + (1.0 - ADAM_B2) * (g * g)
    m_hat = m * (1.0 / (1.0 - ADAM_B1 ** ADAM_STEP))
    v_hat = v * (1.0 / (1.0 - ADAM_B2 ** ADAM_STEP))
    denom = jnp.sqrt(v_hat) + ADAM_EPS
    inv = pl.reciprocal(denom, approx=True)
    inv = inv * (2.0 - denom * inv)
    delta = -ADAM_LR * (m_hat * inv + ADAM_WD * w)
    return delta, m, v


def _adamw(w, m, v, g_mine, g_sibling):
    rows, cols = w.shape
    tr = _row_tile(rows, cols)
    two = g_sibling is not None

    def body(*refs):
        if two:
            w_ref, m_ref, v_ref, ga_ref, gb_ref, g_ref, d_ref, nm_ref, nv_ref = refs
            g = ga_ref[...] + gb_ref[...]
        else:
            w_ref, m_ref, v_ref, ga_ref, g_ref, d_ref, nm_ref, nv_ref = refs
            g = ga_ref[...]
        delta, nm, nv = _adamw_math(w_ref[...], g, m_ref[...], v_ref[...])
        g_ref[...] = g
        d_ref[...] = delta
        nm_ref[...] = nm
        nv_ref[...] = nv

    blk = pl.BlockSpec((tr, cols), lambda i: (i, 0))
    args = [w, m, v, g_mine] + ([g_sibling] if two else [])
    return pl.pallas_call(body, out_shape=[jax.ShapeDtypeStruct((rows, cols), F32)] * 4, grid=(rows // tr,),
                          in_specs=[blk] * len(args), out_specs=[blk] * 4, name="adamw",
                          compiler_params=_params(("parallel",)))(*args)


def _pack_rows(arrays):
    flat = jnp.concatenate([a.reshape(-1) for a in arrays])
    rows = -(-flat.shape[0] // (8 * LANE)) * 8
    return jnp.pad(flat, (0, rows * LANE - flat.shape[0])).reshape(rows, LANE)


def _unpack_rows(packed, shapes):
    flat = packed.reshape(-1)
    out, at = [], 0
    for s in shapes:
        size = math.prod(s)
        out.append(flat[at:at + size].reshape(s))
        at += size
    return out


def kernel(x, p, positions, norm_g, ffn_w_in, ffn_w_out, ple_w_proj, ple_w_gate, rel_bias, mla_w_a, mla_q_norm, mla_kv_norm, mla_w_uq, mla_w_ukv, mla_w_o, dil_w_qkv, dil_w_o, fox_w_qkvf, fox_b_f, fox_w_o, loss_target, m_norm_g, m_ffn_w_in, m_ffn_w_out, m_ple_w_proj, m_ple_w_gate, m_rel_bias, m_mla_w_a, m_mla_q_norm, m_mla_kv_norm, m_mla_w_uq, m_mla_w_ukv, m_mla_w_o, m_dil_w_qkv, m_dil_w_o, m_fox_w_qkvf, m_fox_b_f, m_fox_w_o, v_norm_g, v_ffn_w_in, v_ffn_w_out, v_ple_w_proj, v_ple_w_gate, v_rel_bias, v_mla_w_a, v_mla_q_norm, v_mla_kv_norm, v_mla_w_uq, v_mla_w_ukv, v_mla_w_o, v_dil_w_qkv, v_dil_w_o, v_fox_w_qkvf, v_fox_b_f, v_fox_w_o):
    w = dict(norm_g=norm_g, ffn_w_in=ffn_w_in, ffn_w_out=ffn_w_out, ple_w_proj=ple_w_proj, ple_w_gate=ple_w_gate,
             rel_bias=rel_bias, mla_w_a=mla_w_a, mla_q_norm=mla_q_norm, mla_kv_norm=mla_kv_norm, mla_w_uq=mla_w_uq,
             mla_w_ukv=mla_w_ukv, mla_w_o=mla_w_o, dil_w_qkv=dil_w_qkv, dil_w_o=dil_w_o, fox_w_qkvf=fox_w_qkvf,
             fox_b_f=fox_b_f, fox_w_o=fox_w_o)
    m = dict(norm_g=m_norm_g, ffn_w_in=m_ffn_w_in, ffn_w_out=m_ffn_w_out, ple_w_proj=m_ple_w_proj,
             ple_w_gate=m_ple_w_gate, rel_bias=m_rel_bias, mla_w_a=m_mla_w_a, mla_q_norm=m_mla_q_norm,
             mla_kv_norm=m_mla_kv_norm, mla_w_uq=m_mla_w_uq, mla_w_ukv=m_mla_w_ukv, mla_w_o=m_mla_w_o,
             dil_w_qkv=m_dil_w_qkv, dil_w_o=m_dil_w_o, fox_w_qkvf=m_fox_w_qkvf, fox_b_f=m_fox_b_f, fox_w_o=m_fox_w_o)
    v = dict(norm_g=v_norm_g, ffn_w_in=v_ffn_w_in, ffn_w_out=v_ffn_w_out, ple_w_proj=v_ple_w_proj,
             ple_w_gate=v_ple_w_gate, rel_bias=v_rel_bias, mla_w_a=v_mla_w_a, mla_q_norm=v_mla_q_norm,
             mla_kv_norm=v_mla_kv_norm, mla_w_uq=v_mla_w_uq, mla_w_ukv=v_mla_w_ukv, mla_w_o=v_mla_w_o,
             dil_w_qkv=v_dil_w_qkv, dil_w_o=v_dil_w_o, fox_w_qkvf=v_fox_w_qkvf, fox_b_f=v_fox_b_f, fox_w_o=v_fox_w_o)
    chip = 2 * lax.axis_index("x") + lax.axis_index("y")
    for tree in (w, m, v):
        tree[TRANSPOSED] = jnp.swapaxes(tree[TRANSPOSED], 1, 2)

    small_shapes = [w[k].shape for k in SMALL_SHARDED]
    order = [(i, part) for i in range(DEPTH) for part in (MIXER_PART, COMMON_PART) if _part_names(i, part)]
    gathers = {}
    after = positions
    zero = 0.0
    for i, part in order:
        bufs = [_own_slot((w[k][_layer_slot(k, i)] + zero).astype(BF)) for k in _part_names(i, part)]
        if (i, part) == order[0]:
            bufs.append(_own_slot(_pack_rows([w[k] for k in SMALL_SHARDED])))
        gathers[i, part] = _spread_start(bufs, None, after, f"gather_start_{i}_{part}")
        after = gathers[i, part]["token"]
        if (i, part) == order[0]:
            zero = after[0, 0]
    all_started = after
    state = {}

    def get_part(i, part, after_array):
        is_first = (i, part) == order[0]
        lands = _spread_wait(gathers[i, part], all_started if is_first else after_array, f"gather_wait_{i}_{part}")
        if is_first:
            pieces = [_unpack_rows(lands[-1][k], small_shapes) for k in range(N_CHIPS)]
            small = {name: jnp.concatenate([pieces[k][idx] for k in range(N_CHIPS)], axis=-1)
                     for idx, name in enumerate(SMALL_SHARDED)}
            state["small"] = dict(small, rel_bias=rel_bias, fox_b_f=fox_b_f)
        chunks = dict(zip(_part_names(i, part), lands))
        state[i, part] = {k: a.shape for k, a in chunks.items()}
        return _part_to_compute(i, part, chunks)

    started, forwards = [], {}

    def forward_oldest(after_array):
        i, part, handle = started.pop(0)
        received, sent = _spread_wait(handle, after_array, f"exchange_wait_{i}_{part}")
        forwards[i, part] = _sibling_start(received, sent, after_array, f"sibling_start_{i}_{part}")
        return forwards[i, part]["token"]

    def put_part(i, part, lg):
        contrib = _part_contributions(i, part, lg, state[i, part])
        srcs = [contrib[k] for k in _part_names(i, part)]
        handle = _spread_start([lax.empty(s.shape, s.dtype) for s in srcs], srcs, positions,
                               f"exchange_start_{i}_{part}")
        token = handle["token"]
        if started:
            token = token + forward_oldest(token)
        started.append((i, part, handle))
        return token

    sq, grad_x, sg = _run_layers(x[0], p[:, 0], positions[0], loss_target[0], get_part, lambda: state["small"],
                                 put_part)
    loss = lax.psum(0.5 / D_MODEL * jnp.sum(sq), ("x", "y", "c"))
    forward_oldest(grad_x)

    held = {k: {} for k in BIG}
    for i, part in sorted(forwards, reverse=True):
        received, sent, sibling = _sibling_wait(forwards[i, part], grad_x, f"sibling_wait_{i}_{part}")
        for k, r, s, t in zip(_part_names(i, part), received, sent, sibling):
            held[k][_layer_slot(k, i)] = (r, s, t)
    results = {}
    for k in BIG:
        per_layer = [held[k][slot] for slot in sorted(held[k])]
        outs = _adamw_weight(_as_2d(w[k]), _as_2d(m[k]), _as_2d(v[k]), *[list(col) for col in zip(*per_layer)])
        results[k] = [o.reshape(w[k].shape) for o in outs]
    results[TRANSPOSED] = [jnp.swapaxes(o, 1, 2) for o in results[TRANSPOSED]]

    small_all = SMALL_SHARDED + SMALL_REPLICATED
    full_shapes = [sg[k].shape for k in small_all]
    reduced = dict(zip(small_all, _unpack_rows(_all_reduce_small(_pack_rows([sg[k] for k in small_all])), full_shapes)))
    local_g = []
    for k in small_all:
        g = reduced[k]
        if k in SMALL_SHARDED:
            width = w[k].shape[-1]
            g = lax.dynamic_slice_in_dim(g, chip * width, width, axis=g.ndim - 1)
        local_g.append(g)
    local_shapes = [w[k].shape for k in small_all]
    outs = _adamw(_pack_rows([w[k] for k in small_all]), _pack_rows([m[k] for k in small_all]),
                  _pack_rows([v[k] for k in small_all]), _pack_rows(local_g), None)
    unpacked = [_unpack_rows(o, local_shapes) for o in outs]
    for idx, k in enumerate(small_all):
        results[k] = [u[idx] for u in unpacked]

    return (loss, grad_x[None], *[results[k][0] for k in WEIGHTS], *[results[k][1] for k in WEIGHTS],
            *[results[k][2] for k in WEIGHTS], *[results[k][3] for k in WEIGHTS])
```

```python
import functools
import math

import jax
import jax.numpy as jnp
from jax import lax
from jax.experimental import pallas as pl
from jax.experimental.pallas import tpu as pltpu

F32 = jnp.float32
BF = jnp.bfloat16
MESH = pl.DeviceIdType.MESH
HBM_SPEC = pl.BlockSpec(memory_space=pltpu.HBM)

D_MODEL = 1024
DEPTH = 4
N_MIXERS = 3
D_FF = 2816
NORM_EPS = 1e-6
NEG_INF = -1e30
LANE = 128
HEADS = 16
HEAD_DIM = 64
MLA_Q_RANK = 384
MLA_KV_RANK = 256
MLA_ROPE = 32
MLA_A_PAD = 768
ROPE_THETA = 10000.0
DIL_PATTERNS = ((128, 1), (512, 4), (2048, 16))
Q_BLOCK = 128
DIL_PAIRS = {1: 2, 4: 4, 16: 4}
REL_BUCKETS = 32
REL_MAX_DIST = 2048
N_CHIPS = 4
N_DEV = 8

ADAM_LR = 0.001
ADAM_B1 = 0.9
ADAM_B2 = 0.999
ADAM_EPS = 1e-08
ADAM_WD = 0.01
ADAM_STEP = 10

VMEM_LIMIT = 56 * 1024 * 1024
MATMUL_VMEM_BUDGET = 36 * 1024 * 1024
ROW_TILE = 512
ATTN_TILE = 256
ATTN_Q_TILE = 512
ATTN_FORWARD_KEY_TILE = 512
ATTN_BACKWARD_Q_TILE = 512
FOX_BACKWARD_KEY_TILE = 512
MLA_GROUP = 4
FOX_GROUP = 2


def _params(sem=None):
    return pltpu.CompilerParams(dimension_semantics=sem, vmem_limit_bytes=VMEM_LIMIT)


def _divisor_tiles(dim):
    tiles = [t for t in range(LANE, dim + 1, LANE) if dim % t == 0]
    return tiles or [dim]


def _matmul_tiles(m, n, k, a_bytes, b_bytes, out_bytes, has_add, n_unit=None, k_unit=None):
    best = None
    for tm in _divisor_tiles(m):
        for tn in _divisor_tiles(n_unit or n):
            for tk in _divisor_tiles(k_unit or k):
                if max(tm, tn, tk) > 2048:
                    continue
                vmem = 2 * (tm * tk * a_bytes + tk * tn * b_bytes + tm * tn * out_bytes) + tm * tn * 4
                if has_add:
                    vmem += 2 * tm * tn * 4
                if vmem > MATMUL_VMEM_BUDGET:
                    continue
                steps = (m // tm) * (n // tn) * (k // tk)
                traffic = m * k * a_bytes * (n // tn) + k * n * b_bytes * (m // tm) + m * n * out_bytes
                cost = traffic / 3.0e12 + steps * 0.4e-6
                if best is None or cost < best[0]:
                    best = (cost, tm, tn, tk)
    return best[1:]


def _matmul(a, b, *, ta=False, tb=False, b_chunks=False, out_chunks=False, add=None, out_dtype=F32, name):
    k, m = a.shape if ta else a.shape[::-1]
    n_unit = k_unit = None
    if b_chunks:
        chunks, rows_w, c = b.shape
        if tb:
            kb, n, k_unit = chunks * c, rows_w, c
        else:
            kb, n, n_unit = rows_w, chunks * c, c
    else:
        kb, n = b.shape[::-1] if tb else b.shape
    if out_chunks:
        assert n % N_CHIPS == 0 and add is None
        n_unit = n // N_CHIPS
    assert k == kb, (a.shape, b.shape, ta, tb)
    tm, tn, tk = _matmul_tiles(m, n, k, a.dtype.itemsize, b.dtype.itemsize, jnp.dtype(out_dtype).itemsize,
                               add is not None, n_unit, k_unit)
    nk = k // tk
    dims = (((0 if ta else 1,), (1 if tb else 0,)), ((), ()))

    def body(*refs):
        if add is None:
            a_ref, b_ref, o_ref, acc_ref = refs
            add_ref = None
        else:
            a_ref, b_ref, add_ref, o_ref, acc_ref = refs
        kk = pl.program_id(2)

        @pl.when(kk == 0)
        def _():
            acc_ref[...] = jnp.zeros_like(acc_ref)

        acc_ref[...] += lax.dot_general(a_ref[...].astype(BF), b_ref[...].astype(BF), dims,
                                        preferred_element_type=F32)

        @pl.when(kk == nk - 1)
        def _():
            r = acc_ref[...]
            if add_ref is not None:
                r = r + add_ref[...].astype(F32)
            o_ref[...] = r.astype(out_dtype)

    a_spec = pl.BlockSpec((tk, tm), lambda i, j, q: (q, i)) if ta else pl.BlockSpec((tm, tk), lambda i, j, q: (i, q))
    if b_chunks and tb:
        per_k = k_unit // tk
        b_spec = pl.BlockSpec((None, tn, tk), lambda i, j, q: (q // per_k, j, q % per_k))
    elif b_chunks:
        per_n = n_unit // tn
        b_spec = pl.BlockSpec((None, tk, tn), lambda i, j, q: (j // per_n, q, j % per_n))
    elif tb:
        b_spec = pl.BlockSpec((tn, tk), lambda i, j, q: (j, q))
    else:
        b_spec = pl.BlockSpec((tk, tn), lambda i, j, q: (q, j))
    if out_chunks:
        per_o = n_unit // tn
        o_spec = pl.BlockSpec((None, tm, tn), lambda i, j, q: (j // per_o, i, j % per_o))
        out_shape = jax.ShapeDtypeStruct((N_CHIPS, m, n_unit), out_dtype)
    else:
        o_spec = pl.BlockSpec((tm, tn), lambda i, j, q: (i, j))
        out_shape = jax.ShapeDtypeStruct((m, n), out_dtype)
    in_specs = [a_spec, b_spec]
    args = [a, b]
    if add is not None:
        in_specs.append(o_spec)
        args.append(add)
    return pl.pallas_call(
        body, out_shape=out_shape, grid=(m // tm, n // tn, nk),
        in_specs=in_specs, out_specs=o_spec, scratch_shapes=[pltpu.VMEM((tm, tn), F32)], name=name,
        compiler_params=_params(("parallel", "parallel", "arbitrary")))(*args)


def _rowwise(body, name, rows, ins, outs, tr=ROW_TILE):
    def row_spec(cols):
        return pl.BlockSpec((tr, cols), lambda i: (i, 0))

    def full_spec(shape):
        zeros = (0,) * len(shape)
        return pl.BlockSpec(shape, lambda i: zeros)

    in_specs = [row_spec(a.shape[1]) if kind == "row" else full_spec(a.shape) for a, kind in ins]
    out_specs = [row_spec(shape[1]) if kind == "row" else full_spec(shape) for shape, _, kind in outs]
    out_shape = [jax.ShapeDtypeStruct(shape, dtype) for shape, dtype, _ in outs]
    return pl.pallas_call(body, out_shape=out_shape, grid=(rows // tr,), in_specs=in_specs, out_specs=out_specs,
                          name=name, compiler_params=_params(("arbitrary",)))(*[a for a, _ in ins])


def _rstd(x):
    return lax.rsqrt(jnp.mean(x * x, axis=-1, keepdims=True) + NORM_EPS)


def _rms_bwd_math(x, g, dy):
    r = _rstd(x)
    gd = dy * g
    dx = r * gd - x * (r * r * r) * jnp.mean(gd * x, axis=-1, keepdims=True)
    dg = jnp.sum(dy * x * r, axis=0, keepdims=True)
    return dx, dg


def _sigmoid(x):
    return 0.5 * jnp.tanh(0.5 * x) + 0.5


def _init_acc(*refs):
    @pl.when(pl.program_id(0) == 0)
    def _():
        for r in refs:
            r[...] = jnp.zeros_like(r)


def _prenorm(h, g):
    rows, cols = h.shape

    def body(h_ref, g_ref, o_ref):
        x = h_ref[...]
        o_ref[...] = (x * _rstd(x) * g_ref[...]).astype(BF)

    return _rowwise(body, "prenorm", rows, [(h, "row"), (g, "full")], [((rows, cols), BF, "row")])[0]


def _post_residual(h, y, g_post, g_pre):
    rows, cols = h.shape
    with_pre = g_pre is not None

    def body(*refs):
        if with_pre:
            h_ref, y_ref, gp_ref, gq_ref, hn_ref, hb_ref = refs
        else:
            h_ref, y_ref, gp_ref, hn_ref, hb_ref = refs
        yv = y_ref[...]
        hn = h_ref[...] + yv * _rstd(yv) * gp_ref[...]
        hn_ref[...] = hn
        hb_ref[...] = (hn * _rstd(hn) * gq_ref[...] if with_pre else hn).astype(BF)

    ins = [(h, "row"), (y, "row"), (g_post, "full")] + ([(g_pre, "full")] if with_pre else [])
    return _rowwise(body, "post_residual_pre" if with_pre else "post_residual", rows, ins,
                    [((rows, cols), F32, "row"), ((rows, cols), BF, "row")])


def _ple_forward(h2, pp, z, g_pre):
    rows, cols = h2.shape

    def body(h_ref, p_ref, z_ref, g_ref, h3_ref, hb_ref):
        h3 = h_ref[...] + p_ref[...] * _sigmoid(z_ref[...])
        h3_ref[...] = h3
        hb_ref[...] = (h3 * _rstd(h3) * g_ref[...]).astype(BF)

    return _rowwise(body, "ple_forward", rows, [(h2, "row"), (pp, "row"), (z, "row"), (g_pre, "full")],
                    [((rows, cols), F32, "row"), ((rows, cols), BF, "row")])


def _ple_loss(h2, pp, z, target):
    rows, cols = h2.shape

    def body(h_ref, p_ref, z_ref, t_ref, dh_ref, sq_ref):
        _init_acc(sq_ref)
        err = h_ref[...] + p_ref[...] * _sigmoid(z_ref[...]) - t_ref[...]
        dh_ref[...] = err * (1.0 / cols)
        sq_ref[...] += jnp.sum(err * err, axis=0, keepdims=True)

    return _rowwise(body, "ple_loss", rows, [(h2, "row"), (pp, "row"), (z, "row"), (target, "row")],
                    [((rows, cols), F32, "row"), ((1, cols), F32, "acc")])


def _ple_backward(dh3, pp, z):
    rows, cols = dh3.shape

    def body(d_ref, p_ref, z_ref, dpp_ref, dz_ref):
        d = d_ref[...]
        s = _sigmoid(z_ref[...])
        dpp_ref[...] = (d * s).astype(BF)
        dz_ref[...] = (d * p_ref[...] * s * (1.0 - s)).astype(BF)

    return _rowwise(body, "ple_backward", rows, [(dh3, "row"), (pp, "row"), (z, "row")],
                    [((rows, cols), BF, "row"), ((rows, cols), BF, "row")])


def _rms_backward(x, g, dy, add, out_dtype):
    rows, cols = x.shape
    with_add = add is not None

    def body(*refs):
        if with_add:
            x_ref, g_ref, dy_ref, add_ref, dx_ref, dg_ref = refs
        else:
            x_ref, g_ref, dy_ref, dx_ref, dg_ref = refs
        _init_acc(dg_ref)
        dx, dg = _rms_bwd_math(x_ref[...], g_ref[...], dy_ref[...].astype(F32))
        if with_add:
            dx = dx + add_ref[...]
        dx_ref[...] = dx.astype(out_dtype)
        dg_ref[...] += dg

    ins = [(x, "row"), (g, "full"), (dy, "row")] + ([(add, "row")] if with_add else [])
    return _rowwise(body, "rms_backward_add" if with_add else "rms_backward", rows, ins,
                    [((rows, cols), out_dtype, "row"), ((1, cols), F32, "acc")])


def _swiglu_forward(gu):
    rows = gu.shape[0]

    def body(gu_ref, o_ref):
        g = gu_ref[:, :D_FF].astype(F32)
        o_ref[...] = (g * _sigmoid(g) * gu_ref[:, D_FF:].astype(F32)).astype(BF)

    return _rowwise(body, "swiglu_forward", rows, [(gu, "row")], [((rows, D_FF), BF, "row")])[0]


def _swiglu_backward(gu, dact):
    rows = gu.shape[0]

    def body(gu_ref, d_ref, o_ref):
        g = gu_ref[:, :D_FF].astype(F32)
        u = gu_ref[:, D_FF:].astype(F32)
        d = d_ref[...].astype(F32)
        s = _sigmoid(g)
        gs = g * s
        o_ref[:, :D_FF] = (d * u * (s + gs * (1.0 - s))).astype(BF)
        o_ref[:, D_FF:] = (d * gs).astype(BF)

    return _rowwise(body, "swiglu_backward", rows, [(gu, "row"), (dact, "row")], [((rows, 2 * D_FF), BF, "row")])[0]


def _rope_tables(positions):
    half = MLA_ROPE // 2
    inv = ROPE_THETA ** (-jnp.arange(half, dtype=F32) / half)
    ang = positions.astype(F32)[:, None] * inv
    cos, sin = jnp.cos(ang), jnp.sin(ang)
    rows = positions.shape[0]
    c = jnp.ones((rows, LANE), F32).at[:, 64:80].set(cos).at[:, 80:96].set(cos)
    sa = jnp.zeros((rows, LANE), F32).at[:, 64:80].set(-sin)
    sb = jnp.zeros((rows, LANE), F32).at[:, 80:96].set(sin)
    return c, sa, sb


def _rope_apply(x, c, sa, sb):
    return x * c + pltpu.roll(x, LANE - 16, 1) * sa + pltpu.roll(x, 16, 1) * sb


def _rope_apply_t(dy, c, sa, sb):
    return dy * c + pltpu.roll(dy * sa, 16, 1) + pltpu.roll(dy * sb, LANE - 16, 1)


def _rope_heads(x, tables, transpose, name):
    rows, cols = x.shape

    def body(x_ref, c_ref, sa_ref, sb_ref, o_ref):
        fn = _rope_apply_t if transpose else _rope_apply
        c, sa, sb = c_ref[...], sa_ref[...], sb_ref[...]
        for head in range(cols // LANE):
            lanes = slice(head * LANE, (head + 1) * LANE)
            o_ref[:, lanes] = fn(x_ref[:, lanes].astype(F32), c, sa, sb).astype(BF)

    blk = pl.BlockSpec((ROW_TILE, cols), lambda i: (i, 0))
    tbl = pl.BlockSpec((ROW_TILE, LANE), lambda i: (i, 0))
    return pl.pallas_call(body, out_shape=jax.ShapeDtypeStruct((rows, cols), BF), grid=(rows // ROW_TILE,),
                          in_specs=[blk, tbl, tbl, tbl], out_specs=blk, name=name,
                          compiler_params=_params(("parallel",)))(x, *tables)


def _mla_mid_forward(a, q_norm, kv_norm, tables):
    rows = a.shape[0]
    qr, kvr = MLA_Q_RANK, MLA_KV_RANK

    def body(a_ref, qn_ref, kn_ref, c_ref, sa_ref, sb_ref, cq_ref, ckv_ref, kr_ref):
        aq = a_ref[:, 0:qr]
        akv = a_ref[:, qr:qr + kvr]
        cq_ref[...] = (aq * _rstd(aq) * qn_ref[...]).astype(BF)
        ckv_ref[...] = (akv * _rstd(akv) * kn_ref[...]).astype(BF)
        kr_ref[...] = _rope_apply(a_ref[:, qr + kvr:], c_ref[...], sa_ref[...], sb_ref[...]).astype(BF)

    ins = [(a, "row"), (q_norm, "full"), (kv_norm, "full")] + [(t, "row") for t in tables]
    return _rowwise(body, "mla_mid_forward", rows, ins,
                    [((rows, qr), BF, "row"), ((rows, kvr), BF, "row"), ((rows, LANE), BF, "row")])


def _mla_mid_backward(a, q_norm, kv_norm, tables, dcq, dckv, dkr):
    rows = a.shape[0]
    qr, kvr = MLA_Q_RANK, MLA_KV_RANK

    def body(a_ref, qn_ref, kn_ref, c_ref, sa_ref, sb_ref, dcq_ref, dckv_ref, dkr_ref, da_ref, dqn_ref, dkn_ref):
        _init_acc(dqn_ref, dkn_ref)
        dxq, dgq = _rms_bwd_math(a_ref[:, 0:qr], qn_ref[...], dcq_ref[...])
        dxk, dgk = _rms_bwd_math(a_ref[:, qr:qr + kvr], kn_ref[...], dckv_ref[...])
        da_ref[:, 0:qr] = dxq.astype(BF)
        da_ref[:, qr:qr + kvr] = dxk.astype(BF)
        da_ref[:, qr + kvr:] = _rope_apply_t(dkr_ref[...], c_ref[...], sa_ref[...], sb_ref[...]).astype(BF)
        dqn_ref[...] += dgq
        dkn_ref[...] += dgk

    ins = ([(a, "row"), (q_norm, "full"), (kv_norm, "full")] + [(t, "row") for t in tables]
           + [(dcq, "row"), (dckv, "row"), (dkr, "row")])
    return _rowwise(body, "mla_mid_backward", rows, ins,
                    [((rows, MLA_A_PAD), BF, "row"), ((1, qr), F32, "acc"), ((1, kvr), F32, "acc")])


def _attn_specs(rows, kv_off, g):
    head = pl.BlockSpec((rows, g * LANE), lambda h: (0, h))
    kv_head = pl.BlockSpec((rows, g * LANE), lambda h: (0, h + kv_off // g))
    shared = pl.BlockSpec((rows, LANE), lambda h: (0, 0))
    col_vec = pl.BlockSpec((g, rows, 1), lambda h: (h, 0, 0))
    row_vec = pl.BlockSpec((g, 1, rows), lambda h: (h, 0, 0))
    return head, kv_head, shared, col_vec, row_vec


def _attn_forward(q, kv, kv_off, kr, cum_col, cum_row, scale, group_size, name):
    rows = q.shape[0]
    heads = HEADS
    t = ATTN_FORWARD_KEY_TILE
    tq = ATTN_Q_TILE
    per = tq // t
    has_kr = kr is not None
    has_f = cum_col is not None
    group = range(group_size)

    def body(*refs):
        it = iter(refs)
        q_ref, kv_ref = next(it), next(it)
        kr_ref = next(it) if has_kr else None
        cc_ref = next(it) if has_f else None
        cr_ref = next(it) if has_f else None
        o_ref, lse_ref = next(it), next(it)
        lo = lax.broadcasted_iota(jnp.int32, (1, LANE), 1) < HEAD_DIM
        row = lax.broadcasted_iota(jnp.int32, (tq, t), 0)
        col = lax.broadcasted_iota(jnp.int32, (tq, t), 1)
        lanes = [slice(g * LANE, (g + 1) * LANE) for g in group]

        def q_block(i, _):
            qs = pl.ds(pl.multiple_of(i * tq, tq), tq)
            qbs = [q_ref[qs, lanes[g]] for g in group]
            cqs = [cc_ref[g, qs, :] if has_f else None for g in group]

            def step(j, carry, diag):
                ks = pl.ds(pl.multiple_of(j * t, t), t)
                skip = diag * t if diag and has_f else 0
                other = kr_ref[ks, :] if has_kr else jnp.zeros((t, LANE), BF)
                kvbs = [kv_ref[ks, lanes[g]] for g in group]

                def logit(g):
                    return lax.dot_general(qbs[g][skip:], jnp.where(lo, kvbs[g], other), (((1,), (1,)), ((), ())),
                                           preferred_element_type=F32)

                logits = {g: logit(g) for g in (group if has_f else group[:1])}
                out = []
                for g in group:
                    m, l, acc = (a[skip:] for a in carry[g])
                    if not has_f and g + 1 < len(group):
                        logits[g + 1] = logit(g + 1)
                    s = logits[g] * scale
                    if has_f:
                        s = s + (cqs[g][skip:] - cr_ref[g, :, ks])
                    if diag is not None:
                        s = jnp.where(col[skip:] + diag * t <= row[skip:], s, NEG_INF)
                    mn = jnp.maximum(m, jnp.max(s, axis=1, keepdims=True))
                    alpha = jnp.exp(m - mn)
                    p = jnp.exp(s - mn)
                    l = alpha * l + jnp.sum(p, axis=1, keepdims=True)
                    acc = alpha * acc + jnp.dot(p.astype(BF), kvbs[g], preferred_element_type=F32)
                    new = (mn, l, acc)
                    if skip:
                        new = tuple(jnp.concatenate([old[:skip], a], axis=0) for old, a in zip(carry[g], new))
                    out.append(new)
                return tuple(out)

            init = tuple((jnp.full((tq, 1), NEG_INF, F32), jnp.zeros((tq, 1), F32), jnp.zeros((tq, LANE), F32))
                         for _ in group)
            carry = lax.fori_loop(0, i * per, lambda j, c: step(j, c, None), init)
            for d in range(per):
                carry = step(i * per + d, carry, d)
            for g, (m, l, acc) in enumerate(carry):
                o_ref[qs, lanes[g]] = jnp.where(lo, 0.0, acc * (1.0 / l)).astype(BF)
                lse_ref[g, qs, :] = m + jnp.log(l)
            return 0

        lax.fori_loop(0, rows // tq, q_block, 0)

    head, kv_head, shared, col_vec, row_vec = _attn_specs(rows, kv_off, group_size)
    in_specs, args = [head, kv_head], [q, kv]
    if has_kr:
        in_specs.append(shared)
        args.append(kr)
    if has_f:
        in_specs += [col_vec, row_vec]
        args += [cum_col, cum_row]
    return pl.pallas_call(
        body, out_shape=[jax.ShapeDtypeStruct((rows, heads * LANE), BF), jax.ShapeDtypeStruct((heads, rows, 1), F32)],
        grid=(heads // group_size,), in_specs=in_specs, out_specs=[head, col_vec], name=name,
        compiler_params=_params(("arbitrary",)))(*args)


def _attn_backward(q, kv, kv_off, kr, cum_col, cum_row, o, do, lse, scale, group_size, name):
    rows = q.shape[0]
    heads = HEADS
    has_kr = kr is not None
    has_f = cum_col is not None
    t = FOX_BACKWARD_KEY_TILE if has_f else ATTN_TILE
    tq = ATTN_BACKWARD_Q_TILE
    group = range(group_size)

    def body(*refs):
        it = iter(refs)
        q_ref, kv_ref = next(it), next(it)
        kr_ref = next(it) if has_kr else None
        cc_ref = next(it) if has_f else None
        cr_ref = next(it) if has_f else None
        o_ref, do_ref, lse_ref = next(it), next(it), next(it)
        dq_ref, dkv_ref = next(it), next(it)
        dkr_ref = next(it) if has_kr else None
        dck_ref = next(it) if has_f else None
        dcq_ref = next(it) if has_f else None
        dq_acc = next(it)
        lo = lax.broadcasted_iota(jnp.int32, (1, LANE), 1) < HEAD_DIM
        row = lax.broadcasted_iota(jnp.int32, (tq, t), 0)
        col = lax.broadcasted_iota(jnp.int32, (tq, t), 1)
        lanes = [slice(g * LANE, (g + 1) * LANE) for g in group]

        dq_acc[...] = jnp.zeros_like(dq_acc)
        if has_kr:
            _init_acc(dkr_ref)
        if has_f:
            dcq_ref[...] = jnp.zeros_like(dcq_ref)

        def kv_block(first_q, within):
            j = first_q * (tq // t) + within
            skip = within * t
            ks = pl.ds(pl.multiple_of(j * t, t), t)
            other = kr_ref[ks, :] if has_kr else jnp.zeros((t, LANE), BF)
            kvbs = [kv_ref[ks, lanes[g]] for g in group]
            kks = [jnp.where(lo, kvbs[g], other) for g in group]
            cks = [cr_ref[g, :, ks] if has_f else None for g in group]
            causal = col[:tq - skip] <= row[:tq - skip]

            def pair(i, carry, diag):
                start = pl.multiple_of(i * tq, tq)
                qs = pl.ds(start + skip, tq - skip) if diag else pl.ds(start, tq)
                nt = (((1,), (1,)), ((), ()))

                def first_stage(g):
                    qb = q_ref[qs, lanes[g]]
                    dob = do_ref[qs, lanes[g]]
                    return (qb, dob, lax.dot_general(qb, kks[g], nt, preferred_element_type=F32),
                            lax.dot_general(dob, kvbs[g], nt, preferred_element_type=F32))

                first = {g: first_stage(g) for g in group}
                out = []
                for g in group:
                    dkk, dvv, dcs = carry[g]
                    qb, dob, logit, dp = first[g]
                    s = logit * scale
                    if has_f:
                        s = s + (cc_ref[g, qs, :] - cks[g])
                    if diag:
                        s = jnp.where(causal, s, NEG_INF)
                    p = jnp.exp(s - lse_ref[g, qs, :])
                    delta = jnp.sum(dob.astype(F32) * o_ref[qs, lanes[g]].astype(F32), axis=1, keepdims=True)
                    ds = p * (dp - delta)
                    dsb = ds.astype(BF)
                    dvv = dvv + lax.dot_general(p.astype(BF), dob, (((0,), (0,)), ((), ())), preferred_element_type=F32)
                    dkk = dkk + lax.dot_general(dsb, qb, (((0,), (0,)), ((), ())), preferred_element_type=F32)
                    dq_acc[qs, lanes[g]] += jnp.dot(dsb, kks[g], preferred_element_type=F32)
                    if has_f:
                        dcs = dcs + jnp.sum(ds, axis=0, keepdims=True)
                        dcq_ref[g, qs, :] += jnp.sum(ds, axis=1, keepdims=True)
                    out.append((dkk, dvv, dcs))
                return tuple(out)

            init = tuple((jnp.zeros((t, LANE), F32), jnp.zeros((t, LANE), F32), jnp.zeros((1, t), F32)) for _ in group)
            carry = pair(first_q, init, True)
            carry = lax.fori_loop(first_q + 1, rows // tq, lambda i, c: pair(i, c, False), carry)
            for g, (dkk, dvv, dcs) in enumerate(carry):
                dkk = dkk * scale
                dkv_ref[ks, lanes[g]] = jnp.where(lo, dkk, dvv).astype(BF)
                if has_kr:
                    dkr_ref[ks, :] += jnp.where(lo, 0.0, dkk)
                if has_f:
                    dck_ref[g, :, ks] = -dcs

        def q_diagonal(first_q, _):
            for within in range(tq // t):
                kv_block(first_q, within)
            return 0

        lax.fori_loop(0, rows // tq, q_diagonal, 0)
        dq_ref[...] = (dq_acc[...] * scale).astype(BF)

    head, kv_head, shared, col_vec, row_vec = _attn_specs(rows, kv_off, group_size)
    in_specs, args = [head, kv_head], [q, kv]
    if has_kr:
        in_specs.append(shared)
        args.append(kr)
    if has_f:
        in_specs += [col_vec, row_vec]
        args += [cum_col, cum_row]
    in_specs += [head, head, col_vec]
    args += [o, do, lse]
    out_shape = [jax.ShapeDtypeStruct((rows, heads * LANE), BF), jax.ShapeDtypeStruct((rows, heads * LANE), BF)]
    out_specs = [head, head]
    if has_kr:
        out_shape.append(jax.ShapeDtypeStruct((rows, LANE), F32))
        out_specs.append(shared)
    if has_f:
        out_shape += [jax.ShapeDtypeStruct((heads, 1, rows), F32), jax.ShapeDtypeStruct((heads, rows, 1), F32)]
        out_specs += [row_vec, col_vec]
    return pl.pallas_call(
        body, out_shape=out_shape, grid=(heads // group_size,), in_specs=in_specs, out_specs=out_specs,
        scratch_shapes=[pltpu.VMEM((rows, group_size * LANE), F32)], name=name,
        compiler_params=_params(("arbitrary",)))(*args)


def _tri_dot(tri, x):
    return jnp.dot(tri, x, preferred_element_type=F32, precision=lax.Precision.HIGHEST)


def _forget_forward(f_raw, b_f):
    rows = f_raw.shape[0]
    t = ATTN_TILE

    def body(f_ref, b_ref, cum_ref):
        tri = (lax.broadcasted_iota(jnp.int32, (t, t), 1) <= lax.broadcasted_iota(jnp.int32, (t, t), 0)).astype(F32)

        def blk(i, carry):
            sl = pl.ds(pl.multiple_of(i * t, t), t)
            xv = f_ref[sl, :] + b_ref[...]
            log_f = jnp.minimum(xv, 0.0) - jnp.log(1.0 + jnp.exp(-jnp.abs(xv)))
            cum_ref[sl, :] = _tri_dot(tri, log_f) + carry
            return carry + jnp.sum(log_f, axis=0, keepdims=True)

        lax.fori_loop(0, rows // t, blk, jnp.zeros((1, LANE), F32))

    return pl.pallas_call(body, out_shape=jax.ShapeDtypeStruct((rows, LANE), F32), name="forget_forward",
                          compiler_params=_params())(f_raw, b_f)


def _forget_backward(f_raw, b_f, dcum):
    rows = f_raw.shape[0]
    t = ATTN_TILE
    nb = rows // t

    def body(f_ref, b_ref, dc_ref, df_ref, db_ref):
        tri = (lax.broadcasted_iota(jnp.int32, (t, t), 1) >= lax.broadcasted_iota(jnp.int32, (t, t), 0)).astype(F32)

        def blk(i, carry):
            later, db = carry
            sl = pl.ds(pl.multiple_of((nb - 1 - i) * t, t), t)
            dc = dc_ref[sl, :]
            dlog = _tri_dot(tri, dc) + later
            xv = f_ref[sl, :] + b_ref[...]
            df = dlog / (1.0 + jnp.exp(xv))
            df_ref[sl, :] = df.astype(BF)
            return later + jnp.sum(dc, axis=0, keepdims=True), db + jnp.sum(df, axis=0, keepdims=True)

        _, db = lax.fori_loop(0, nb, blk, (jnp.zeros((1, LANE), F32), jnp.zeros((1, LANE), F32)))
        db_ref[...] = db

    return pl.pallas_call(body, out_shape=[jax.ShapeDtypeStruct((rows, LANE), BF), jax.ShapeDtypeStruct((1, LANE), F32)],
                          name="forget_backward", compiler_params=_params())(f_raw, b_f, dcum)


def _t5_bucket(dist):
    max_exact = REL_BUCKETS // 2
    n = jnp.maximum(dist.astype(F32), 1.0)
    large = max_exact + (jnp.log(n / max_exact) / math.log(REL_MAX_DIST / max_exact)
                         * (REL_BUCKETS - max_exact)).astype(jnp.int32)
    large = jnp.minimum(large, REL_BUCKETS - 1)
    return jnp.where(dist < max_exact, dist, large)


def _dil_buckets(dilation):
    i = jnp.arange(Q_BLOCK)[:, None]
    j = jnp.arange(Q_BLOCK)[None, :]
    cur = _t5_bucket(jnp.clip(i - j, 0) * dilation).astype(jnp.int32)
    prev = _t5_bucket(jnp.clip(Q_BLOCK + i - j, 0) * dilation).astype(jnp.int32)
    return cur, prev


def _dil_bias_tiles(tbl_ref, bc_ref, bp_ref, bias_ref, group, hp, pairs):
    ii = lax.broadcasted_iota(jnp.int32, (Q_BLOCK, Q_BLOCK), 0)
    jj = lax.broadcasted_iota(jnp.int32, (Q_BLOCK, Q_BLOCK), 1)
    for hh in range(2 * pairs):
        col = group * HEADS + 2 * pairs * hp + hh
        acc_c = jnp.zeros((Q_BLOCK, Q_BLOCK), F32)
        acc_p = jnp.zeros((Q_BLOCK, Q_BLOCK), F32)
        for b in range(REL_BUCKETS):
            val = tbl_ref[b, col]
            acc_c = jnp.where(bc_ref[...] == b, val, acc_c)
            acc_p = jnp.where(bp_ref[...] == b, val, acc_p)
        bias_ref[2 * hh] = jnp.where(jj <= ii, acc_c, NEG_INF)
        bias_ref[2 * hh + 1] = jnp.where(jj >= ii, acc_p, NEG_INF)


def _dil_view(qkv, group, dilation):
    if dilation == 1:
        return qkv
    width = 3 * HEADS * HEAD_DIM
    return qkv[:, group * width:(group + 1) * width].reshape(qkv.shape[0] // dilation, dilation * width)


def _dil_specs(group, dilation, length):
    width = DIL_PAIRS[dilation] * LANE
    per = 8 // DIL_PAIRS[dilation]

    def col(kind):
        if dilation == 1:
            return pl.BlockSpec((length, width), lambda hp, r: (0, (group * 3 + kind) * per + hp))
        return pl.BlockSpec((length, width), lambda hp, r: (0, (r * 3 + kind) * per + hp))

    out = pl.BlockSpec((length, width), lambda hp, r: (0, r * per + hp))
    tile = pl.BlockSpec((Q_BLOCK, Q_BLOCK), lambda hp, r: (0, 0))
    table = pl.BlockSpec(memory_space=pltpu.SMEM)
    return col, out, tile, table


def _dil_forward(view, group, dilation, table, buckets):
    length = view.shape[0]
    rows = length * dilation
    pairs = DIL_PAIRS[dilation]
    nb = length // Q_BLOCK
    scale = HEAD_DIM ** -0.5
    qb = Q_BLOCK

    def body(tbl_ref, bc_ref, bp_ref, q_ref, k_ref, v_ref, o_ref, lse_ref, bias_ref):
        hp = pl.program_id(0)

        @pl.when(pl.program_id(1) == 0)
        def _():
            _dil_bias_tiles(tbl_ref, bc_ref, bp_ref, bias_ref, group, hp, pairs)

        lo = lax.broadcasted_iota(jnp.int32, (1, LANE), 1) < HEAD_DIM
        nt = (((1,), (1,)), ((), ()))

        def blk(n, first):
            cur = pl.ds(0, qb) if first else pl.ds(pl.multiple_of(n * qb, qb), qb)
            prev = None if first else pl.ds(pl.multiple_of((n - 1) * qb, qb), qb)
            logits = []
            for pair in range(pairs):
                lanes = slice(pair * LANE, (pair + 1) * LANE)
                qn = q_ref[cur, lanes] * scale
                for hh in range(2):
                    qm = jnp.where(lo if hh == 0 else ~lo, qn, jnp.zeros_like(qn))
                    s_c = lax.dot_general(qm, k_ref[cur, lanes], nt, preferred_element_type=F32)
                    s_p = None if first else lax.dot_general(qm, k_ref[prev, lanes], nt, preferred_element_type=F32)
                    logits.append((s_c, s_p))
            for pair in range(pairs):
                lanes = slice(pair * LANE, (pair + 1) * LANE)
                outs, lses = [], []
                for hh in range(2):
                    bias = 4 * pair + 2 * hh
                    s_c, s_p = logits[2 * pair + hh]
                    s_c = s_c + bias_ref[bias]
                    m = jnp.max(s_c, axis=1, keepdims=True)
                    if not first:
                        s_p = s_p + bias_ref[bias + 1]
                        m = jnp.maximum(m, jnp.max(s_p, axis=1, keepdims=True))
                    e_c = jnp.exp(s_c - m)
                    l = jnp.sum(e_c, axis=1, keepdims=True)
                    acc = jnp.dot(e_c.astype(BF), v_ref[cur, lanes], preferred_element_type=F32)
                    if not first:
                        e_p = jnp.exp(s_p - m)
                        l = l + jnp.sum(e_p, axis=1, keepdims=True)
                        acc = acc + jnp.dot(e_p.astype(BF), v_ref[prev, lanes], preferred_element_type=F32)
                    outs.append(acc * (1.0 / l))
                    lses.append(m + jnp.log(l))
                o_ref[cur, lanes] = jnp.where(lo, outs[0], outs[1])
                lse_ref[cur, lanes] = jnp.where(lo, lses[0], lses[1])
            return 0

        blk(0, True)
        if nb > 1:
            lax.fori_loop(1, nb, lambda n, _: blk(n, False), 0)

    col, out, tile, tbl = _dil_specs(group, dilation, length)
    bc, bp = buckets
    o, lse = pl.pallas_call(
        body, out_shape=[jax.ShapeDtypeStruct((length, dilation * D_MODEL), F32)] * 2,
        grid=(8 // pairs, dilation), in_specs=[tbl, tile, tile, col(0), col(1), col(2)], out_specs=[out, out],
        scratch_shapes=[pltpu.VMEM((4 * pairs, qb, qb), F32)], name=f"dilated_forward_{dilation}",
        compiler_params=_params(("arbitrary", "arbitrary")))(
            table, bc, bp, view, view, view)
    return o.reshape(rows, D_MODEL), lse.reshape(rows, D_MODEL)


def _dil_backward(view, group, dilation, table, buckets, do_g, lse, dlt):
    length = view.shape[0]
    rows = length * dilation
    pairs = DIL_PAIRS[dilation]
    nb = length // Q_BLOCK
    scale = HEAD_DIM ** -0.5
    qb = Q_BLOCK

    def body(tbl_ref, bc_ref, bp_ref, q_ref, k_ref, v_ref, do_ref, lse_ref, dlt_ref,
             dq_ref, dk_ref, dv_ref, db_ref, bias_ref, dk_acc, dv_acc):
        hp = pl.program_id(0)

        @pl.when(pl.program_id(1) == 0)
        def _():
            _dil_bias_tiles(tbl_ref, bc_ref, bp_ref, bias_ref, group, hp, pairs)
            db_ref[...] = jnp.zeros_like(db_ref)

        dk_acc[...] = jnp.zeros_like(dk_acc)
        dv_acc[...] = jnp.zeros_like(dv_acc)
        lo = lax.broadcasted_iota(jnp.int32, (1, LANE), 1) < HEAD_DIM
        tn = (((0,), (0,)), ((), ()))
        nt = (((1,), (1,)), ((), ()))

        def blk(n, first):
            cur = pl.ds(0, qb) if first else pl.ds(pl.multiple_of(n * qb, qb), qb)
            prev = None if first else pl.ds(pl.multiple_of((n - 1) * qb, qb), qb)
            inputs = []
            for pair in range(pairs):
                lanes = slice(pair * LANE, (pair + 1) * LANE)
                qn = q_ref[cur, lanes] * scale
                don = do_ref[cur, lanes]
                for hh in range(2):
                    mask = lo if hh == 0 else ~lo
                    qm = jnp.where(mask, qn, jnp.zeros_like(qn))
                    dom = jnp.where(mask, don, jnp.zeros_like(don))
                    stage = [qm, dom, lax.dot_general(qm, k_ref[cur, lanes], nt, preferred_element_type=F32),
                             lax.dot_general(dom, v_ref[cur, lanes], nt, preferred_element_type=F32)]
                    if not first:
                        stage += [lax.dot_general(qm, k_ref[prev, lanes], nt, preferred_element_type=F32),
                                  lax.dot_general(dom, v_ref[prev, lanes], nt, preferred_element_type=F32)]
                    inputs.append(stage)
            for pair in range(pairs):
                lanes = slice(pair * LANE, (pair + 1) * LANE)
                kc = k_ref[cur, lanes]
                if not first:
                    kp = k_ref[prev, lanes]
                lse_n = lse_ref[cur, lanes]
                dlt_n = dlt_ref[cur, lanes]
                dqs = []
                dkc = jnp.zeros((qb, LANE), F32)
                dkp = jnp.zeros((qb, LANE), F32)
                dvc = jnp.zeros((qb, LANE), F32)
                dvp = jnp.zeros((qb, LANE), F32)
                for hh in range(2):
                    bias = 4 * pair + 2 * hh
                    mask = lo if hh == 0 else ~lo
                    qm, dom, s_c, dp_c = inputs[2 * pair + hh][:4]
                    lse_h = jnp.max(jnp.where(mask, lse_n, -3e38), axis=1, keepdims=True)
                    dlt_h = jnp.max(jnp.where(mask, dlt_n, -3e38), axis=1, keepdims=True)
                    p_c = jnp.exp(s_c + bias_ref[bias] - lse_h)
                    ds_c = p_c * (dp_c - dlt_h)
                    db_ref[pair, 2 * hh] += ds_c
                    dsc_b = ds_c.astype(BF)
                    dq = jnp.dot(dsc_b, kc, preferred_element_type=F32)
                    dkc = dkc + lax.dot_general(dsc_b, qm, tn, preferred_element_type=F32)
                    dvc = dvc + lax.dot_general(p_c.astype(BF), dom, tn, preferred_element_type=F32)
                    if not first:
                        s_p, dp_p = inputs[2 * pair + hh][4:]
                        p_p = jnp.exp(s_p + bias_ref[bias + 1] - lse_h)
                        ds_p = p_p * (dp_p - dlt_h)
                        db_ref[pair, 2 * hh + 1] += ds_p
                        dsp_b = ds_p.astype(BF)
                        dq = dq + jnp.dot(dsp_b, kp, preferred_element_type=F32)
                        dkp = dkp + lax.dot_general(dsp_b, qm, tn, preferred_element_type=F32)
                        dvp = dvp + lax.dot_general(p_p.astype(BF), dom, tn, preferred_element_type=F32)
                    dqs.append(dq)
                dq_ref[cur, lanes] = (jnp.where(lo, dqs[0], dqs[1]) * scale).astype(BF)
                dk_acc[cur, lanes] += dkc
                dv_acc[cur, lanes] += dvc
                if not first:
                    dk_acc[prev, lanes] += dkp
                    dv_acc[prev, lanes] += dvp
            return 0

        blk(0, True)
        if nb > 1:
            lax.fori_loop(1, nb, lambda n, _: blk(n, False), 0)
        dk_ref[...] = dk_acc[...].astype(BF)
        dv_ref[...] = dv_acc[...].astype(BF)

    col, out, tile, tbl = _dil_specs(group, dilation, length)
    bc, bp = buckets
    wide = (length, dilation * D_MODEL)
    dq, dk, dv, db = pl.pallas_call(
        body, out_shape=[jax.ShapeDtypeStruct(wide, BF)] * 3 + [jax.ShapeDtypeStruct((8, 4, qb, qb), F32)],
        grid=(8 // pairs, dilation), in_specs=[tbl, tile, tile, col(0), col(1), col(2), out, out, out],
        out_specs=[out, out, out, pl.BlockSpec((pairs, 4, qb, qb), lambda hp, r: (hp, 0, 0, 0))],
        scratch_shapes=[pltpu.VMEM((4 * pairs, qb, qb), F32), pltpu.VMEM((length, pairs * LANE), F32),
                        pltpu.VMEM((length, pairs * LANE), F32)],
        name=f"dilated_backward_{dilation}", compiler_params=_params(("arbitrary", "arbitrary")))(
            table, bc, bp, view, view, view,
            do_g.reshape(wide), lse.reshape(wide), dlt.reshape(wide))
    return dq.reshape(rows, D_MODEL), dk.reshape(rows, D_MODEL), dv.reshape(rows, D_MODEL), db


def _head_sums(x, lo):
    s0 = jnp.sum(jnp.where(lo, x, 0.0), axis=1, keepdims=True)
    s1 = jnp.sum(jnp.where(lo, 0.0, x), axis=1, keepdims=True)
    return jnp.where(lo, s0, s1)


def _dil_merge_forward(outs, lses):
    rows = outs[0].shape[0]

    def body(o0, o1, o2, l0, l1, l2, o_ref):
        ls = [l0[...], l1[...], l2[...]]
        m = jnp.maximum(jnp.maximum(ls[0], ls[1]), ls[2])
        es = [jnp.exp(v - m) for v in ls]
        tot = es[0] + es[1] + es[2]
        o_ref[...] = ((es[0] * o0[...] + es[1] * o1[...] + es[2] * o2[...]) / tot).astype(BF)

    blk = pl.BlockSpec((ROW_TILE, LANE), lambda i, j: (i, j))
    return pl.pallas_call(body, out_shape=jax.ShapeDtypeStruct((rows, D_MODEL), BF), grid=(rows // ROW_TILE, 8),
                          in_specs=[blk] * 6, out_specs=blk, name="dilated_merge_forward",
                          compiler_params=_params(("parallel", "parallel")))(*outs, *lses)


def _dil_merge_backward(outs, lses, do):
    rows = outs[0].shape[0]

    def body(o0, o1, o2, l0, l1, l2, do_ref, d0, d1, d2, t0, t1, t2):
        lo = lax.broadcasted_iota(jnp.int32, (1, LANE), 1) < HEAD_DIM
        ls = [l0[...], l1[...], l2[...]]
        os_ = [o0[...], o1[...], o2[...]]
        m = jnp.maximum(jnp.maximum(ls[0], ls[1]), ls[2])
        es = [jnp.exp(v - m) for v in ls]
        inv = 1.0 / (es[0] + es[1] + es[2])
        alphas = [e * inv for e in es]
        dov = do_ref[...]
        merged = alphas[0] * os_[0] + alphas[1] * os_[1] + alphas[2] * os_[2]
        dot = _head_sums(dov * merged, lo)
        for a, d_ref, t_ref in zip(alphas, (d0, d1, d2), (t0, t1, t2)):
            d_ref[...] = (a * dov).astype(BF)
            t_ref[...] = a * dot

    blk = pl.BlockSpec((ROW_TILE, LANE), lambda i, j: (i, j))
    res = pl.pallas_call(
        body, out_shape=[jax.ShapeDtypeStruct((rows, D_MODEL), BF)] * 3 + [jax.ShapeDtypeStruct((rows, D_MODEL), F32)] * 3,
        grid=(rows // ROW_TILE, 8), in_specs=[blk] * 7, out_specs=[blk] * 6, name="dilated_merge_backward",
        compiler_params=_params(("parallel", "parallel")))(*outs, *lses, do)
    return res[:3], res[3:]


def _rel_bias_grad(dbs, buckets):
    def body(db_ref, bc_ref, bp_ref, o_ref):
        g = pl.program_id(0)
        hp = pl.program_id(1)

        @pl.when((g == 0) & (hp == 0))
        def _():
            o_ref[...] = jnp.zeros_like(o_ref)

        rr = lax.broadcasted_iota(jnp.int32, (REL_BUCKETS, LANE), 0)
        cc = lax.broadcasted_iota(jnp.int32, (REL_BUCKETS, LANE), 1)
        bc = bc_ref[0]
        bp = bp_ref[0]
        acc = jnp.zeros((REL_BUCKETS, LANE), F32)
        for hh in range(2):
            col = g * HEADS + 2 * hp + hh
            d_c = db_ref[0, 0, 2 * hh]
            d_p = db_ref[0, 0, 2 * hh + 1]
            for b in range(REL_BUCKETS):
                val = (jnp.sum(jnp.where(bc == b, d_c, 0.0), keepdims=True)
                       + jnp.sum(jnp.where(bp == b, d_p, 0.0), keepdims=True))
                acc = jnp.where((rr == b) & (cc == col), val, acc)
        o_ref[...] += acc

    db_all = jnp.stack(dbs)
    bc_all = jnp.stack([b[0] for b in buckets])
    bp_all = jnp.stack([b[1] for b in buckets])
    tile = pl.BlockSpec((1, Q_BLOCK, Q_BLOCK), lambda g, hp: (g, 0, 0))
    return pl.pallas_call(
        body, out_shape=jax.ShapeDtypeStruct((REL_BUCKETS, LANE), F32), grid=(3, 8),
        in_specs=[pl.BlockSpec((1, 1, 4, Q_BLOCK, Q_BLOCK), lambda g, hp: (g, hp, 0, 0, 0)), tile, tile],
        out_specs=pl.BlockSpec((REL_BUCKETS, LANE), lambda g, hp: (0, 0)), name="rel_bias_grad",
        compiler_params=_params(("arbitrary", "arbitrary")))(db_all, bc_all, bp_all)


def _mla_forward(hn, w, tables):
    a = _matmul(hn, w["w_a"], name="mla_a")
    cq, ckv, kr = _mla_mid_forward(a, w["q_norm"], w["kv_norm"], tables)
    q_raw = _matmul(cq, w["w_uq"], name="mla_uq")
    q = _rope_heads(q_raw, tables, False, "rope_forward")
    kv = _matmul(ckv, w["w_ukv"], b_chunks=True, out_dtype=BF, name="mla_ukv")
    scale = (HEAD_DIM + MLA_ROPE) ** -0.5
    o, lse = _attn_forward(q, kv, 0, kr, None, None, scale, MLA_GROUP, "mla_attention_forward")
    y = _matmul(o, w["w_o"], name="attn_out")
    return y, dict(hn=hn, a=a, cq=cq, ckv=ckv, kr=kr, q=q, kv=kv, o=o, lse=lse)


def _mla_backward(dy, w, s, tables):
    scale = (HEAD_DIM + MLA_ROPE) ** -0.5
    g = {}
    g["w_o"] = _matmul(s["o"], dy, ta=True, out_dtype=BF, name="attn_out_dw")
    do = _matmul(dy, w["w_o"], tb=True, out_dtype=BF, name="attn_out_dx")
    dq, dkv, dkr = _attn_backward(s["q"], s["kv"], 0, s["kr"], None, None, s["o"], do, s["lse"], scale,
                                  MLA_GROUP, "mla_attention_backward")
    dq_raw = _rope_heads(dq, tables, True, "rope_backward")
    g["w_uq"] = _matmul(s["cq"], dq_raw, ta=True, out_dtype=BF, name="mla_uq_dw")
    dcq = _matmul(dq_raw, w["w_uq"], tb=True, name="mla_uq_dx")
    g["w_ukv"] = _matmul(s["ckv"], dkv, ta=True, out_chunks=True, out_dtype=BF, name="mla_ukv_dw")
    dckv = _matmul(dkv, w["w_ukv"], tb=True, b_chunks=True, name="mla_ukv_dx")
    da, g["q_norm"], g["kv_norm"] = _mla_mid_backward(s["a"], w["q_norm"], w["kv_norm"], tables, dcq, dckv, dkr)
    g["w_a"] = _matmul(s["hn"], da, ta=True, out_dtype=BF, name="mla_a_dw")
    dhn = _matmul(da, w["w_a"], tb=True, name="mla_a_dx")
    return dhn, g


def _fox_forward(hn, w):
    qkv = _matmul(hn, w["w_qkv"], out_dtype=BF, name="fox_qkv")
    f_raw = _matmul(hn, w["w_f"], name="fox_f")
    cum = _forget_forward(f_raw, w["b_f"])
    cum_heads = cum[:, :HEADS].T
    cum_col, cum_row = cum_heads[:, :, None], cum_heads[:, None, :]
    o, lse = _attn_forward(qkv, qkv, HEADS, None, cum_col, cum_row, HEAD_DIM ** -0.5, FOX_GROUP,
                           "fox_attention_forward")
    y = _matmul(o, w["w_o"], name="attn_out")
    return y, dict(hn=hn, qkv=qkv, f_raw=f_raw, cum_col=cum_col, cum_row=cum_row, o=o, lse=lse)


def _fox_backward(dy, w, s):
    g = {}
    g["w_o"] = _matmul(s["o"], dy, ta=True, out_dtype=BF, name="attn_out_dw")
    do = _matmul(dy, w["w_o"], tb=True, out_dtype=BF, name="attn_out_dx")
    dq, dkv, dck, dcq = _attn_backward(s["qkv"], s["qkv"], HEADS, None, s["cum_col"], s["cum_row"], s["o"], do,
                                       s["lse"], HEAD_DIM ** -0.5, FOX_GROUP, "fox_attention_backward")
    dcum = jnp.pad((dck[:, 0, :] + dcq[:, :, 0]).T, ((0, 0), (0, LANE - HEADS)))
    df, g["b_f"] = _forget_backward(s["f_raw"], w["b_f"], dcum)
    dqkv = jnp.concatenate([dq, dkv], axis=1)
    g["w_qkv"] = _matmul(s["hn"], dqkv, ta=True, out_dtype=BF, name="fox_qkv_dw")
    g["w_f"] = _matmul(s["hn"], df, ta=True, out_dtype=BF, name="fox_f_dw")
    dhn = _matmul(dqkv, w["w_qkv"], tb=True, name="fox_qkv_dx")
    dhn = _matmul(df, w["w_f"], tb=True, add=dhn, name="fox_f_dx")
    return dhn, g


def _dil_mixer_forward(hn, w, buckets):
    qkv = _matmul(hn, w["w_qkv"], b_chunks=True, out_dtype=BF, name="dil_qkv")
    views = [_dil_view(qkv, grp, dilation) for grp, (_, dilation) in enumerate(DIL_PATTERNS)]
    outs, lses = [], []
    for grp, (_, dilation) in enumerate(DIL_PATTERNS):
        o_g, lse_g = _dil_forward(views[grp], grp, dilation, w["rel_bias"], buckets[grp])
        outs.append(o_g)
        lses.append(lse_g)
    o = _dil_merge_forward(outs, lses)
    y = _matmul(o, w["w_o"], name="dil_out")
    return y, dict(hn=hn, views=views, outs=outs, lses=lses, o=o)


def _dil_mixer_backward(dy, w, s, buckets):
    g = {}
    g["w_o"] = _matmul(s["o"], dy, ta=True, out_dtype=BF, name="dil_out_dw")
    do = _matmul(dy, w["w_o"], tb=True, name="dil_out_dx")
    do_gs, dlts = _dil_merge_backward(s["outs"], s["lses"], do)
    parts, dbs = [], []
    for grp, (_, dilation) in enumerate(DIL_PATTERNS):
        dq, dk, dv, db = _dil_backward(s["views"][grp], grp, dilation, w["rel_bias"], buckets[grp], do_gs[grp],
                                       s["lses"][grp], dlts[grp])
        parts += [dq, dk, dv]
        dbs.append(db)
    dqkv = jnp.concatenate(parts, axis=1)
    g["rel_bias"] = _rel_bias_grad(dbs, buckets)
    g["w_qkv"] = _matmul(s["hn"], dqkv, ta=True, out_chunks=True, out_dtype=BF, name="dil_qkv_dw")
    dhn = _matmul(dqkv, w["w_qkv"], tb=True, b_chunks=True, name="dil_qkv_dx")
    return dhn, g


def _mixer_weights(i, lw, small):
    mixer, j = i % N_MIXERS, i // N_MIXERS
    if mixer == 0:
        return dict(lw["mixer"], q_norm=small["mla_q_norm"][j][None, :], kv_norm=small["mla_kv_norm"][j][None, :])
    if mixer == 1:
        return dict(lw["mixer"], rel_bias=small["rel_bias"])
    return dict(lw["mixer"], b_f=jnp.pad(small["fox_b_f"][j][None, :], ((0, 0), (0, LANE - HEADS))))


MIXER_PART, COMMON_PART = 0, 1


def _run_layers(x, p, positions, target, get_part, get_small, put_part):
    tables = _rope_tables(positions)
    buckets = [_dil_buckets(d) for _, d in DIL_PATTERNS]
    layers, saved = [], []
    h = x
    first = get_part(0, MIXER_PART, positions)
    small = get_small()

    def gain(i, k):
        return small["norm_g"][i, k][None, :]

    hn = _prenorm(h, gain(0, 0))
    sq = dh = None
    for i in range(DEPTH):
        mixer = i % N_MIXERS
        lw = dict(first if i == 0 else get_part(i, MIXER_PART, h))
        mw = _mixer_weights(i, lw, small)
        if mixer == 0:
            y, ms = _mla_forward(hn, mw, tables)
        elif mixer == 1:
            y, ms = _dil_mixer_forward(hn, mw, buckets)
        else:
            y, ms = _fox_forward(hn, mw)
        if "ffn_w_in" not in lw:
            lw.update(get_part(i, COMMON_PART, y))
        layers.append(lw)
        h1, hn2 = _post_residual(h, y, gain(i, 1), gain(i, 2))
        gu = _matmul(hn2, lw["ffn_w_in"], b_chunks=True, out_dtype=BF, name="ffn_in")
        act = _swiglu_forward(gu)
        f = _matmul(act, lw["ffn_w_out"], name="ffn_out")
        h2, h2b = _post_residual(h1, f, gain(i, 3), None)
        pp = _matmul(p[i], lw["ple_w_proj"], b_chunks=True, name="ple_proj")
        z = _matmul(h2b, lw["ple_w_gate"], name="ple_gate")
        saved.append(dict(h=h, y=y, ms=ms, h1=h1, hn2=hn2, gu=gu, act=act, f=f, h2b=h2b, pp=pp, z=z))
        if i + 1 < DEPTH:
            h, hn = _ple_forward(h2, pp, z, gain(i + 1, 0))
        else:
            dh, sq = _ple_loss(h2, pp, z, target)

    norm_rows = [[None] * 4 for _ in range(DEPTH)]
    sg = dict(mla_q_norm={}, mla_kv_norm={}, rel_bias=None, fox_b_f={})
    for i in reversed(range(DEPTH)):
        s, lw = saved[i], layers[i]
        mixer, j = i % N_MIXERS, i // N_MIXERS
        mw = _mixer_weights(i, lw, small)
        lg = {}
        dpp, dz = _ple_backward(dh, s["pp"], s["z"])
        lg["ple_w_proj"] = _matmul(p[i], dpp, ta=True, out_chunks=True, out_dtype=BF, name="ple_proj_dw")
        lg["ple_w_gate"] = _matmul(s["h2b"], dz, ta=True, out_dtype=BF, name="ple_gate_dw")
        dh2 = _matmul(dz, lw["ple_w_gate"], tb=True, add=dh, name="ple_gate_dx")
        df, norm_rows[i][3] = _rms_backward(s["f"], gain(i, 3), dh2, None, BF)
        lg["ffn_w_out"] = _matmul(s["act"], df, ta=True, out_dtype=BF, name="ffn_out_dw")
        dact = _matmul(df, lw["ffn_w_out"], tb=True, out_dtype=BF, name="ffn_out_dx")
        dgu = _swiglu_backward(s["gu"], dact)
        lg["ffn_w_in"] = _matmul(s["hn2"], dgu, ta=True, out_chunks=True, out_dtype=BF, name="ffn_in_dw")
        split = i in SPLIT_LAYERS
        zero = put_part(i, COMMON_PART, lg)[0:1, 0:1] if split else 0.0
        dhn2 = _matmul(dgu, lw["ffn_w_in"], tb=True, b_chunks=True, name="ffn_in_dx")
        dh1, norm_rows[i][2] = _rms_backward(s["h1"], gain(i, 2), dhn2, dh2, F32)
        dy, norm_rows[i][1] = _rms_backward(s["y"], gain(i, 1) + zero, dh1, None, BF)
        if mixer == 0:
            dhn, mg = _mla_backward(dy, mw, s["ms"], tables)
            sg["mla_q_norm"][j] = mg.pop("q_norm")
            sg["mla_kv_norm"][j] = mg.pop("kv_norm")
        elif mixer == 1:
            dhn, mg = _dil_mixer_backward(dy, mw, s["ms"], buckets)
            rel = mg.pop("rel_bias")[:, :3 * HEADS]
            sg["rel_bias"] = rel if sg["rel_bias"] is None else sg["rel_bias"] + rel
        else:
            dhn, mg = _fox_backward(dy, mw, s["ms"])
            sg["fox_b_f"][j] = mg.pop("b_f")[:, :HEADS]
        token = put_part(i, MIXER_PART, dict(mixer=mg) if split else dict(lg, mixer=mg))
        dh, norm_rows[i][0] = _rms_backward(s["h"], gain(i, 0) + token[0:1, 0:1], dhn, dh1, F32)
    small_grads = dict(norm_g=jnp.stack([jnp.concatenate(row, axis=0) for row in norm_rows]),
                       rel_bias=sg["rel_bias"])
    for k in ("mla_q_norm", "mla_kv_norm", "fox_b_f"):
        small_grads[k] = jnp.concatenate([sg[k][j] for j in sorted(sg[k])], axis=0)
    return sq, dh, small_grads


BIG = ("ffn_w_in", "ffn_w_out", "ple_w_proj", "ple_w_gate", "mla_w_a", "mla_w_uq", "mla_w_ukv", "mla_w_o",
       "dil_w_qkv", "dil_w_o", "fox_w_qkvf", "fox_w_o")
SMALL_SHARDED = ("norm_g", "mla_q_norm", "mla_kv_norm")
SMALL_REPLICATED = ("rel_bias", "fox_b_f")
WEIGHTS = ("norm_g", "ffn_w_in", "ffn_w_out", "ple_w_proj", "ple_w_gate", "rel_bias", "mla_w_a", "mla_q_norm",
           "mla_kv_norm", "mla_w_uq", "mla_w_ukv", "mla_w_o", "dil_w_qkv", "dil_w_o", "fox_w_qkvf", "fox_b_f", "fox_w_o")


TRANSPOSED = "fox_w_qkvf"
SPLIT_LAYERS = (0, 1, 2, 3)
LAYER_COMMON = ("ffn_w_in", "ffn_w_out", "ple_w_proj", "ple_w_gate")
MIXER_WEIGHTS = (("mla_w_a", "mla_w_uq", "mla_w_ukv", "mla_w_o"), ("dil_w_qkv", "dil_w_o"), ("fox_w_qkvf", "fox_w_o"))


def _part_names(i, part):
    if i in SPLIT_LAYERS:
        return MIXER_WEIGHTS[i % N_MIXERS] if part == MIXER_PART else LAYER_COMMON
    return MIXER_WEIGHTS[i % N_MIXERS] + LAYER_COMMON if part == MIXER_PART else ()


def _layer_slot(name, i):
    return i if name in LAYER_COMMON else i // N_MIXERS


def _merge_rows(chunks):
    n, r, c = chunks.shape
    return chunks.reshape(n * r, c)


def _merge_cols(chunks):
    n, r, c = chunks.shape
    return chunks.transpose(1, 0, 2).reshape(r, n * c)


def _pad_heads_out(wo):
    w3 = wo.reshape(HEADS, HEAD_DIM, D_MODEL)
    return jnp.pad(w3, ((0, 0), (HEAD_DIM, 0), (0, 0))).reshape(HEADS * LANE, D_MODEL)


def _part_to_compute(i, part, ch):
    lw = {}
    if "ffn_w_in" in ch:
        lw.update(ffn_w_in=ch["ffn_w_in"], ffn_w_out=_merge_rows(ch["ffn_w_out"]), ple_w_proj=ch["ple_w_proj"],
                  ple_w_gate=_merge_rows(ch["ple_w_gate"]))
    if part == COMMON_PART:
        return lw
    mixer = i % N_MIXERS
    if mixer == 0:
        wa = _merge_rows(ch["mla_w_a"])
        rank = MLA_Q_RANK + MLA_KV_RANK
        wa_p = jnp.concatenate([wa[:, :rank], jnp.zeros((wa.shape[0], 64), wa.dtype), wa[:, rank:],
                                jnp.zeros((wa.shape[0], 32), wa.dtype)], axis=1)
        wuq = _merge_cols(ch["mla_w_uq"]).reshape(MLA_Q_RANK, HEADS, HEAD_DIM + MLA_ROPE)
        wuq_p = jnp.pad(wuq, ((0, 0), (0, 0), (0, LANE - HEAD_DIM - MLA_ROPE))).reshape(MLA_Q_RANK, HEADS * LANE)
        lw["mixer"] = dict(w_a=wa_p, w_uq=wuq_p, w_ukv=ch["mla_w_ukv"], w_o=_pad_heads_out(_merge_rows(ch["mla_w_o"])))
    elif mixer == 1:
        lw["mixer"] = dict(w_qkv=ch["dil_w_qkv"], w_o=_merge_rows(ch["dil_w_o"]))
    else:
        wf = _merge_rows(ch["fox_w_qkvf"]).T
        inner = HEADS * HEAD_DIM
        q3 = wf[:, :inner].reshape(D_MODEL, HEADS, HEAD_DIM)
        k3 = wf[:, inner:2 * inner].reshape(D_MODEL, HEADS, HEAD_DIM)
        v3 = wf[:, 2 * inner:3 * inner].reshape(D_MODEL, HEADS, HEAD_DIM)
        q_p = jnp.pad(q3, ((0, 0), (0, 0), (0, HEAD_DIM))).reshape(D_MODEL, HEADS * LANE)
        kv_p = jnp.concatenate([k3, v3], axis=2).reshape(D_MODEL, HEADS * LANE)
        f_p = jnp.pad(wf[:, 3 * inner:], ((0, 0), (0, LANE - HEADS)))
        lw["mixer"] = dict(w_qkv=jnp.concatenate([q_p, kv_p], axis=1), w_f=f_p,
                           w_o=_pad_heads_out(_merge_rows(ch["fox_w_o"])))
    return lw


def _part_contributions(i, part, lg, chunk_shapes):
    spec = {k: jax.ShapeDtypeStruct(s, BF) for k, s in chunk_shapes.items()}
    (contrib,) = jax.linear_transpose(functools.partial(_part_to_compute, i, part), spec)(lg)
    return contrib


def _chip_peers():
    x, y, c = lax.axis_index("x"), lax.axis_index("y"), lax.axis_index("c")
    peers = [(1 - x, y), (x, 1 - y), (1 - x, 1 - y)]
    return x, y, c, peers


SEM_SPEC = pl.BlockSpec(memory_space=pltpu.SEMAPHORE)
ANY_SPEC = pl.BlockSpec(memory_space=pl.ANY)
SPLIT_EFFECT = pltpu.SideEffectType.DATAFLOW_SIDE_EFFECTING


def _own_slot(shard):
    me = 2 * lax.axis_index("x") + lax.axis_index("y")
    return lax.dynamic_update_index_in_dim(lax.empty((N_CHIPS,) + shard.shape, shard.dtype), shard[None], me, 0)


def _spread_copy(src, land, k, peer, c, send_sems, recv_sems, index, src_slot, slot):
    px, py = peer
    return pltpu.make_async_remote_copy(
        src_ref=src.at[src_slot], dst_ref=land.at[slot],
        send_sem=send_sems.at[3 * index + k], recv_sem=recv_sems.at[3 * index + k],
        device_id=(px, py, c), device_id_type=MESH)


def _spread_start(bufs, srcs, after, name):
    n = len(bufs)
    exchange = srcs is not None
    arrays = (list(srcs) if exchange else []) + list(bufs)
    na = len(arrays)

    def body(*refs):
        src, land = refs[:n], refs[na - n:na]
        send_sems, recv_sems = refs[na + 1], refs[na + 2]
        token = refs[-1]
        x, y, c, peers = _chip_peers()
        me = 2 * x + y
        for w in range(n):
            for k, peer in enumerate(peers):
                src_slot = 2 * peer[0] + peer[1] if exchange else me
                _spread_copy(src[w], land[w], k, peer, c, send_sems, recv_sems, w, src_slot, me).start()
        token[...] = jnp.zeros_like(token)

    hbm = [pltpu.with_memory_space_constraint(a, pltpu.HBM) for a in arrays]
    out = pl.pallas_call(
        body, name=name,
        out_shape=(pltpu.SemaphoreType.DMA((3 * n,)), pltpu.SemaphoreType.DMA((3 * n,)),
                   *[pltpu.HBM(a.shape, a.dtype) for a in hbm], jax.ShapeDtypeStruct((8, LANE), F32)),
        in_specs=[HBM_SPEC] * na + [ANY_SPEC],
        out_specs=(SEM_SPEC, SEM_SPEC, *[HBM_SPEC] * na, pl.BlockSpec(memory_space=pltpu.VMEM)),
        input_output_aliases={w: 2 + w for w in range(na)},
        compiler_params=pltpu.CompilerParams(has_side_effects=SPLIT_EFFECT))(*hbm, after)
    return dict(send=out[0], recv=out[1], arrays=out[2:2 + na], n=n, token=out[-1], exchange=exchange)


def _spread_wait(handle, after, name):
    n, exchange = handle["n"], handle["exchange"]
    arrays = list(handle["arrays"])
    na = len(arrays)

    def body(*refs):
        src, land = refs[:n], refs[na - n:na]
        send_sems, recv_sems = refs[na], refs[na + 1]
        x, y, c, peers = _chip_peers()
        me = 2 * x + y
        for w in range(n):
            for k, peer in enumerate(peers):
                there = 2 * peer[0] + peer[1]
                cp = _spread_copy(src[w], land[w], k, peer, c, send_sems, recv_sems, w, there if exchange else me, there)
                cp.wait_send()
                cp.wait_recv()

    out = pl.pallas_call(
        body, name=name, out_shape=tuple(pltpu.HBM(a.shape, a.dtype) for a in arrays),
        in_specs=[HBM_SPEC] * na + [SEM_SPEC, SEM_SPEC, ANY_SPEC], out_specs=tuple([HBM_SPEC] * na),
        input_output_aliases={w: w for w in range(na)},
        compiler_params=pltpu.CompilerParams(has_side_effects=SPLIT_EFFECT))(*arrays, handle["send"], handle["recv"], after)
    return (list(out[n:]), list(out[:n])) if exchange else list(out)


def _sibling_copy(received, sent, land, k, me, peers, sibling, send_sems, recv_sems, index):
    slot = me if k == 3 else 2 * peers[k][0] + peers[k][1]
    src = sent if k == 3 else received
    return pltpu.make_async_remote_copy(
        src_ref=src.at[slot], dst_ref=land.at[slot], send_sem=send_sems.at[4 * index + k],
        recv_sem=recv_sems.at[4 * index + k], device_id=sibling, device_id_type=MESH)


def _sibling_start(received, sent, after, name):
    n = len(received)
    lands = [lax.empty(a.shape, a.dtype) for a in received]
    arrays = list(received) + list(sent) + lands

    def body(*refs):
        rec, snt, land = refs[:n], refs[n:2 * n], refs[2 * n:3 * n]
        send_sems, recv_sems = refs[3 * n + 1], refs[3 * n + 2]
        token = refs[-1]
        x, y, c, peers = _chip_peers()
        for w in range(n):
            for k in range(4):
                _sibling_copy(rec[w], snt[w], land[w], k, 2 * x + y, peers, (x, y, 1 - c), send_sems, recv_sems, w).start()
        token[...] = jnp.zeros_like(token)

    hbm = [pltpu.with_memory_space_constraint(a, pltpu.HBM) for a in arrays]
    out = pl.pallas_call(
        body, name=name,
        out_shape=(pltpu.SemaphoreType.DMA((4 * n,)), pltpu.SemaphoreType.DMA((4 * n,)),
                   *[pltpu.HBM(a.shape, a.dtype) for a in hbm], jax.ShapeDtypeStruct((8, LANE), F32)),
        in_specs=[HBM_SPEC] * (3 * n) + [ANY_SPEC],
        out_specs=(SEM_SPEC, SEM_SPEC, *[HBM_SPEC] * (3 * n), pl.BlockSpec(memory_space=pltpu.VMEM)),
        input_output_aliases={w: 2 + w for w in range(3 * n)},
        compiler_params=pltpu.CompilerParams(has_side_effects=SPLIT_EFFECT))(*hbm, after)
    return dict(send=out[0], recv=out[1], arrays=out[2:2 + 3 * n], n=n, token=out[-1])


def _sibling_wait(handle, after, name):
    n = handle["n"]
    arrays = list(handle["arrays"])

    def body(*refs):
        rec, snt, land = refs[:n], refs[n:2 * n], refs[2 * n:3 * n]
        send_sems, recv_sems = refs[3 * n], refs[3 * n + 1]
        x, y, c, peers = _chip_peers()
        for w in range(n):
            for k in range(4):
                cp = _sibling_copy(rec[w], snt[w], land[w], k, 2 * x + y, peers, (x, y, 1 - c), send_sems, recv_sems, w)
                cp.wait_send()
                cp.wait_recv()

    out = pl.pallas_call(
        body, name=name, out_shape=tuple(pltpu.HBM(a.shape, a.dtype) for a in arrays),
        in_specs=[HBM_SPEC] * (3 * n) + [SEM_SPEC, SEM_SPEC, ANY_SPEC], out_specs=tuple([HBM_SPEC] * (3 * n)),
        input_output_aliases={w: w for w in range(3 * n)},
        compiler_params=pltpu.CompilerParams(has_side_effects=SPLIT_EFFECT))(*arrays, handle["send"], handle["recv"], after)
    return list(out[:n]), list(out[n:2 * n]), list(out[2 * n:])


def _all_reduce_small(v):
    rows = v.shape[0]

    def body(v_ref, sum_ref, slots, send_sems, recv_sems):
        x, y, c = lax.axis_index("x"), lax.axis_index("y"), lax.axis_index("c")
        me = 4 * x + 2 * y + c
        slots[me] = v_ref[...]
        sends = []
        for k in range(1, N_DEV):
            bx, by, bc = (k >> 2) & 1, (k >> 1) & 1, k & 1
            peer = (x ^ bx, y ^ by, c ^ bc)
            rc = pltpu.make_async_remote_copy(src_ref=v_ref, dst_ref=slots.at[me], send_sem=send_sems.at[k],
                                              recv_sem=recv_sems.at[k], device_id=peer, device_id_type=MESH)
            rc.start()
            sends.append(rc)
        for k in range(1, N_DEV):
            bx, by, bc = (k >> 2) & 1, (k >> 1) & 1, k & 1
            src = 4 * (x ^ bx) + 2 * (y ^ by) + (c ^ bc)
            pltpu.make_async_remote_copy(src_ref=v_ref, dst_ref=slots.at[src], send_sem=send_sems.at[k],
                                         recv_sem=recv_sems.at[k], device_id=(x ^ bx, y ^ by, c ^ bc),
                                         device_id_type=MESH).wait_recv()
        for rc in sends:
            rc.wait_send()
        total = slots[0]
        for k in range(1, N_DEV):
            total = total + slots[k]
        sum_ref[...] = total

    vm = pl.BlockSpec(memory_space=pltpu.VMEM)
    return pl.pallas_call(
        body, out_shape=jax.ShapeDtypeStruct((rows, LANE), F32), in_specs=[vm], out_specs=vm,
        scratch_shapes=[pltpu.VMEM((N_DEV, rows, LANE), F32), pltpu.SemaphoreType.DMA((N_DEV,)),
                        pltpu.SemaphoreType.DMA((N_DEV,))], name="all_reduce_small")(v)


def _as_2d(a):
    return a.reshape(-1, a.shape[-1])


def _row_tile(rows, cols):
    for t in (512, 256, 128, 64, 32, 16):
        if rows % t == 0 and t * cols * 4 <= (1 << 20):
            return t
    return rows


def _adamw_weight(w, m, v, received, sent, sibling):
    layers = len(received)
    _, rows, cols = received[0].shape
    tr = _row_tile(rows, cols)
    by_columns = rows % tr != 0 or tr == rows and rows * cols * 4 > (2 << 20)
    if by_columns:
        assert layers == 1 and cols % (2 * LANE) == 0, (w.shape, received[0].shape)
        tr, tc, steps = rows, cols // 2, 2
        index = lambda i: (0, i)
    else:
        tc, steps = cols, rows // tr
        index = lambda i: (i, 0)
    where = (2 * lax.axis_index("x") + lax.axis_index("y")).astype(jnp.int32).reshape(1)

    def body(where_ref, w_ref, m_ref, v_ref, *rest):
        per_layer, (g_ref, d_ref, nm_ref, nv_ref) = rest[:3 * layers], rest[3 * layers:]
        me = where_ref[0]
        for layer in range(layers):
            r_ref, own_ref, s_ref = per_layer[3 * layer:3 * layer + 3]

            @pl.when(pl.program_id(0) == layer)
            def _():
                mine = theirs = None
                for k in range(N_CHIPS):
                    a = jnp.where(me == k, own_ref[...], r_ref[k]).astype(F32)
                    b = s_ref[k].astype(F32)
                    mine = a if mine is None else mine + a
                    theirs = b if theirs is None else theirs + b
                g = mine + theirs
                delta, nm, nv = _adamw_math(w_ref[...], g, m_ref[...], v_ref[...])
                g_ref[...] = g
                d_ref[...] = delta
                nm_ref[...] = nm
                nv_ref[...] = nv

    def held(layer, now, i):
        return jnp.where(now < layer, 0, jnp.where(now > layer, steps - 1, i))

    if by_columns:
        stacked = pl.BlockSpec((tr, tc), lambda now, i, where_ref: index(i))
    else:
        stacked = pl.BlockSpec((tr, tc), lambda now, i, where_ref: (now * steps + i, 0))
    in_specs = [stacked, stacked, stacked]
    args = [where, w, m, v]
    for layer in range(layers):
        four = pl.BlockSpec((N_CHIPS, tr, tc), lambda now, i, where_ref, layer=layer: (0,) + index(held(layer, now, i)))
        own = pl.BlockSpec((None, tr, tc),
                           lambda now, i, where_ref, layer=layer: (where_ref[0],) + index(held(layer, now, i)))
        in_specs += [four, own, four]
        args += [received[layer], sent[layer], sibling[layer]]
    grid_spec = pltpu.PrefetchScalarGridSpec(num_scalar_prefetch=1, grid=(layers, steps), in_specs=in_specs,
                                             out_specs=[stacked] * 4)
    return pl.pallas_call(body, out_shape=[jax.ShapeDtypeStruct(w.shape, F32)] * 4, grid_spec=grid_spec,
                          name="adamw_weight", compiler_params=_params(("arbitrary", "arbitrary")))(*args)


def _adamw_math(w, g, m, v):
    m = ADAM_B1 * m + (1.0 - ADAM_B1) * g
    v = ADAM_B2 * v + (1.0 - ADAM_B2) * (g * g)
    m_hat = m * (1.0 / (1.0 - ADAM_B1 ** ADAM_STEP))
    v_hat = v * (1.0 / (1.0 - ADAM_B2 ** ADAM_STEP))
    denom = jnp.sqrt(v_hat) + ADAM_EPS
    inv = pl.reciprocal(denom, approx=True)
    inv = inv * (2.0 - denom * inv)
    delta = -ADAM_LR * (m_hat * inv + ADAM_WD * w)
    return delta, m, v


def _adamw(w, m, v, g_mine, g_sibling):
    rows, cols = w.shape
    tr = _row_tile(rows, cols)
    two = g_sibling is not None

    def body(*refs):
        if two:
            w_ref, m_ref, v_ref, ga_ref, gb_ref, g_ref, d_ref, nm_ref, nv_ref = refs
            g = ga_ref[...] + gb_ref[...]
        else:
            w_ref, m_ref, v_ref, ga_ref, g_ref, d_ref, nm_ref, nv_ref = refs
            g = ga_ref[...]
        delta, nm, nv = _adamw_math(w_ref[...], g, m_ref[...], v_ref[...])
        g_ref[...] = g
        d_ref[...] = delta
        nm_ref[...] = nm
        nv_ref[...] = nv

    blk = pl.BlockSpec((tr, cols), lambda i: (i, 0))
    args = [w, m, v, g_mine] + ([g_sibling] if two else [])
    return pl.pallas_call(body, out_shape=[jax.ShapeDtypeStruct((rows, cols), F32)] * 4, grid=(rows // tr,),
                          in_specs=[blk] * len(args), out_specs=[blk] * 4, name="adamw",
                          compiler_params=_params(("parallel",)))(*args)


def _pack_rows(arrays):
    flat = jnp.concatenate([a.reshape(-1) for a in arrays])
    rows = -(-flat.shape[0] // (8 * LANE)) * 8
    return jnp.pad(flat, (0, rows * LANE - flat.shape[0])).reshape(rows, LANE)


def _unpack_rows(packed, shapes):
    flat = packed.reshape(-1)
    out, at = [], 0
    for s in shapes:
        size = math.prod(s)
        out.append(flat[at:at + size].reshape(s))
        at += size
    return out


def kernel(x, p, positions, norm_g, ffn_w_in, ffn_w_out, ple_w_proj, ple_w_gate, rel_bias, mla_w_a, mla_q_norm, mla_kv_norm, mla_w_uq, mla_w_ukv, mla_w_o, dil_w_qkv, dil_w_o, fox_w_qkvf, fox_b_f, fox_w_o, loss_target, m_norm_g, m_ffn_w_in, m_ffn_w_out, m_ple_w_proj, m_ple_w_gate, m_rel_bias, m_mla_w_a, m_mla_q_norm, m_mla_kv_norm, m_mla_w_uq, m_mla_w_ukv, m_mla_w_o, m_dil_w_qkv, m_dil_w_o, m_fox_w_qkvf, m_fox_b_f, m_fox_w_o, v_norm_g, v_ffn_w_in, v_ffn_w_out, v_ple_w_proj, v_ple_w_gate, v_rel_bias, v_mla_w_a, v_mla_q_norm, v_mla_kv_norm, v_mla_w_uq, v_mla_w_ukv, v_mla_w_o, v_dil_w_qkv, v_dil_w_o, v_fox_w_qkvf, v_fox_b_f, v_fox_w_o):
    w = dict(norm_g=norm_g, ffn_w_in=ffn_w_in, ffn_w_out=ffn_w_out, ple_w_proj=ple_w_proj, ple_w_gate=ple_w_gate,
             rel_bias=rel_bias, mla_w_a=mla_w_a, mla_q_norm=mla_q_norm, mla_kv_norm=mla_kv_norm, mla_w_uq=mla_w_uq,
             mla_w_ukv=mla_w_ukv, mla_w_o=mla_w_o, dil_w_qkv=dil_w_qkv, dil_w_o=dil_w_o, fox_w_qkvf=fox_w_qkvf,
             fox_b_f=fox_b_f, fox_w_o=fox_w_o)
    m = dict(norm_g=m_norm_g, ffn_w_in=m_ffn_w_in, ffn_w_out=m_ffn_w_out, ple_w_proj=m_ple_w_proj,
             ple_w_gate=m_ple_w_gate, rel_bias=m_rel_bias, mla_w_a=m_mla_w_a, mla_q_norm=m_mla_q_norm,
             mla_kv_norm=m_mla_kv_norm, mla_w_uq=m_mla_w_uq, mla_w_ukv=m_mla_w_ukv, mla_w_o=m_mla_w_o,
             dil_w_qkv=m_dil_w_qkv, dil_w_o=m_dil_w_o, fox_w_qkvf=m_fox_w_qkvf, fox_b_f=m_fox_b_f, fox_w_o=m_fox_w_o)
    v = dict(norm_g=v_norm_g, ffn_w_in=v_ffn_w_in, ffn_w_out=v_ffn_w_out, ple_w_proj=v_ple_w_proj,
             ple_w_gate=v_ple_w_gate, rel_bias=v_rel_bias, mla_w_a=v_mla_w_a, mla_q_norm=v_mla_q_norm,
             mla_kv_norm=v_mla_kv_norm, mla_w_uq=v_mla_w_uq, mla_w_ukv=v_mla_w_ukv, mla_w_o=v_mla_w_o,
             dil_w_qkv=v_dil_w_qkv, dil_w_o=v_dil_w_o, fox_w_qkvf=v_fox_w_qkvf, fox_b_f=v_fox_b_f, fox_w_o=v_fox_w_o)
    chip = 2 * lax.axis_index("x") + lax.axis_index("y")
    for tree in (w, m, v):
        tree[TRANSPOSED] = jnp.swapaxes(tree[TRANSPOSED], 1, 2)

    small_shapes = [w[k].shape for k in SMALL_SHARDED]
    order = [(i, part) for i in range(DEPTH) for part in (MIXER_PART, COMMON_PART) if _part_names(i, part)]
    gathers = {}
    after = positions
    zero = 0.0
    for i, part in order:
        bufs = [_own_slot((w[k][_layer_slot(k, i)] + zero).astype(BF)) for k in _part_names(i, part)]
        if (i, part) == order[0]:
            bufs.append(_own_slot(_pack_rows([w[k] for k in SMALL_SHARDED])))
        gathers[i, part] = _spread_start(bufs, None, after, f"gather_start_{i}_{part}")
        after = gathers[i, part]["token"]
        if (i, part) == order[0]:
            zero = after[0, 0]
    all_started = after
    state = {}

    def get_part(i, part, after_array):
        is_first = (i, part) == order[0]
        lands = _spread_wait(gathers[i, part], all_started if is_first else after_array, f"gather_wait_{i}_{part}")
        if is_first:
            pieces = [_unpack_rows(lands[-1][k], small_shapes) for k in range(N_CHIPS)]
            small = {name: jnp.concatenate([pieces[k][idx] for k in range(N_CHIPS)], axis=-1)
                     for idx, name in enumerate(SMALL_SHARDED)}
            state["small"] = dict(small, rel_bias=rel_bias, fox_b_f=fox_b_f)
        chunks = dict(zip(_part_names(i, part), lands))
        state[i, part] = {k: a.shape for k, a in chunks.items()}
        return _part_to_compute(i, part, chunks)

    started, forwards = [], {}

    def forward_oldest(after_array):
        i, part, handle = started.pop(0)
        received, sent = _spread_wait(handle, after_array, f"exchange_wait_{i}_{part}")
        forwards[i, part] = _sibling_start(received, sent, after_array, f"sibling_start_{i}_{part}")
        return forwards[i, part]["token"]

    def put_part(i, part, lg):
        contrib = _part_contributions(i, part, lg, state[i, part])
        srcs = [contrib[k] for k in _part_names(i, part)]
        handle = _spread_start([lax.empty(s.shape, s.dtype) for s in srcs], srcs, positions,
                               f"exchange_start_{i}_{part}")
        token = handle["token"]
        if started:
            token = token + forward_oldest(token)
        started.append((i, part, handle))
        return token

    sq, grad_x, sg = _run_layers(x[0], p[:, 0], positions[0], loss_target[0], get_part, lambda: state["small"],
                                 put_part)
    loss = lax.psum(0.5 / D_MODEL * jnp.sum(sq), ("x", "y", "c"))
    forward_oldest(grad_x)

    held = {k: {} for k in BIG}
    for i, part in sorted(forwards, reverse=True):
        received, sent, sibling = _sibling_wait(forwards[i, part], grad_x, f"sibling_wait_{i}_{part}")
        for k, r, s, t in zip(_part_names(i, part), received, sent, sibling):
            held[k][_layer_slot(k, i)] = (r, s, t)
    results = {}
    for k in BIG:
        per_layer = [held[k][slot] for slot in sorted(held[k])]
        outs = _adamw_weight(_as_2d(w[k]), _as_2d(m[k]), _as_2d(v[k]), *[list(col) for col in zip(*per_layer)])
        results[k] = [o.reshape(w[k].shape) for o in outs]
    results[TRANSPOSED] = [jnp.swapaxes(o, 1, 2) for o in results[TRANSPOSED]]

    small_all = SMALL_SHARDED + SMALL_REPLICATED
    full_shapes = [sg[k].shape for k in small_all]
    reduced = dict(zip(small_all, _unpack_rows(_all_reduce_small(_pack_rows([sg[k] for k in small_all])), full_shapes)))
    local_g = []
    for k in small_all:
        g = reduced[k]
        if k in SMALL_SHARDED:
            width = w[k].shape[-1]
            g = lax.dynamic_slice_in_dim(g, chip * width, width, axis=g.ndim - 1)
        local_g.append(g)
    local_shapes = [w[k].shape for k in small_all]
    outs = _adamw(_pack_rows([w[k] for k in small_all]), _pack_rows([m[k] for k in small_all]),
                  _pack_rows([v[k] for k in small_all]), _pack_rows(local_g), None)
    unpacked = [_unpack_rows(o, local_shapes) for o in outs]
    for idx, k in enumerate(small_all):
        results[k] = [u[idx] for u in unpacked]

    return (loss, grad_x[None], *[results[k][0] for k in WEIGHTS], *[results[k][1] for k in WEIGHTS],
            *[results[k][2] for k in WEIGHTS], *[results[k][3] for k in WEIGHTS])
```

```python
import functools
import math

import jax
import jax.numpy as jnp
from jax import lax
from jax.experimental import pallas as pl
from jax.experimental.pallas import tpu as pltpu

F32 = jnp.float32
BF = jnp.bfloat16
MESH = pl.DeviceIdType.MESH
HBM_SPEC = pl.BlockSpec(memory_space=pltpu.HBM)

D_MODEL = 1024
DEPTH = 4
N_MIXERS = 3
D_FF = 2816
NORM_EPS = 1e-6
NEG_INF = -1e30
LANE = 128
HEADS = 16
HEAD_DIM = 64
MLA_Q_RANK = 384
MLA_KV_RANK = 256
MLA_ROPE = 32
MLA_A_PAD = 768
ROPE_THETA = 10000.0
DIL_PATTERNS = ((128, 1), (512, 4), (2048, 16))
Q_BLOCK = 128
DIL_PAIRS = {1: 2, 4: 4, 16: 4}
REL_BUCKETS = 32
REL_MAX_DIST = 2048
N_CHIPS = 4
N_DEV = 8

ADAM_LR = 0.001
ADAM_B1 = 0.9
ADAM_B2 = 0.999
ADAM_EPS = 1e-08
ADAM_WD = 0.01
ADAM_STEP = 10

VMEM_LIMIT = 56 * 1024 * 1024
MATMUL_VMEM_BUDGET = 36 * 1024 * 1024
ROW_TILE = 512
ATTN_TILE = 256
ATTN_Q_TILE = 512
ATTN_FORWARD_KEY_TILE = 512
ATTN_BACKWARD_Q_TILE = 512
FOX_BACKWARD_KEY_TILE = 512
MLA_GROUP = 4
FOX_GROUP = 2


def _params(sem=None):
    return pltpu.CompilerParams(dimension_semantics=sem, vmem_limit_bytes=VMEM_LIMIT)


def _divisor_tiles(dim):
    tiles = [t for t in range(LANE, dim + 1, LANE) if dim % t == 0]
    return tiles or [dim]


def _matmul_tiles(m, n, k, a_bytes, b_bytes, out_bytes, has_add, n_unit=None, k_unit=None):
    best = None
    for tm in _divisor_tiles(m):
        for tn in _divisor_tiles(n_unit or n):
            for tk in _divisor_tiles(k_unit or k):
                if max(tm, tn, tk) > 2048:
                    continue
                vmem = 2 * (tm * tk * a_bytes + tk * tn * b_bytes + tm * tn * out_bytes) + tm * tn * 4
                if has_add:
                    vmem += 2 * tm * tn * 4
                if vmem > MATMUL_VMEM_BUDGET:
                    continue
                steps = (m // tm) * (n // tn) * (k // tk)
                traffic = m * k * a_bytes * (n // tn) + k * n * b_bytes * (m // tm) + m * n * out_bytes
                cost = traffic / 3.0e12 + steps * 0.4e-6
                if best is None or cost < best[0]:
                    best = (cost, tm, tn, tk)
    return best[1:]


def _matmul(a, b, *, ta=False, tb=False, b_chunks=False, out_chunks=False, add=None, out_dtype=F32, name):
    k, m = a.shape if ta else a.shape[::-1]
    n_unit = k_unit = None
    if b_chunks:
        chunks, rows_w, c = b.shape
        if tb:
            kb, n, k_unit = chunks * c, rows_w, c
        else:
            kb, n, n_unit = rows_w, chunks * c, c
    else:
        kb, n = b.shape[::-1] if tb else b.shape
    if out_chunks:
        assert n % N_CHIPS == 0 and add is None
        n_unit = n // N_CHIPS
    assert k == kb, (a.shape, b.shape, ta, tb)
    tm, tn, tk = _matmul_tiles(m, n, k, a.dtype.itemsize, b.dtype.itemsize, jnp.dtype(out_dtype).itemsize,
                               add is not None, n_unit, k_unit)
    nk = k // tk
    dims = (((0 if ta else 1,), (1 if tb else 0,)), ((), ()))

    def body(*refs):
        if add is None:
            a_ref, b_ref, o_ref, acc_ref = refs
            add_ref = None
        else:
            a_ref, b_ref, add_ref, o_ref, acc_ref = refs
        kk = pl.program_id(2)

        @pl.when(kk == 0)
        def _():
            acc_ref[...] = jnp.zeros_like(acc_ref)

        acc_ref[...] += lax.dot_general(a_ref[...].astype(BF), b_ref[...].astype(BF), dims,
                                        preferred_element_type=F32)

        @pl.when(kk == nk - 1)
        def _():
            r = acc_ref[...]
            if add_ref is not None:
                r = r + add_ref[...].astype(F32)
            o_ref[...] = r.astype(out_dtype)

    a_spec = pl.BlockSpec((tk, tm), lambda i, j, q: (q, i)) if ta else pl.BlockSpec((tm, tk), lambda i, j, q: (i, q))
    if b_chunks and tb:
        per_k = k_unit // tk
        b_spec = pl.BlockSpec((None, tn, tk), lambda i, j, q: (q // per_k, j, q % per_k))
    elif b_chunks:
        per_n = n_unit // tn
        b_spec = pl.BlockSpec((None, tk, tn), lambda i, j, q: (j // per_n, q, j % per_n))
    elif tb:
        b_spec = pl.BlockSpec((tn, tk), lambda i, j, q: (j, q))
    else:
        b_spec = pl.BlockSpec((tk, tn), lambda i, j, q: (q, j))
    if out_chunks:
        per_o = n_unit // tn
        o_spec = pl.BlockSpec((None, tm, tn), lambda i, j, q: (j // per_o, i, j % per_o))
        out_shape = jax.ShapeDtypeStruct((N_CHIPS, m, n_unit), out_dtype)
    else:
        o_spec = pl.BlockSpec((tm, tn), lambda i, j, q: (i, j))
        out_shape = jax.ShapeDtypeStruct((m, n), out_dtype)
    in_specs = [a_spec, b_spec]
    args = [a, b]
    if add is not None:
        in_specs.append(o_spec)
        args.append(add)
    return pl.pallas_call(
        body, out_shape=out_shape, grid=(m // tm, n // tn, nk),
        in_specs=in_specs, out_specs=o_spec, scratch_shapes=[pltpu.VMEM((tm, tn), F32)], name=name,
        compiler_params=_params(("parallel", "parallel", "arbitrary")))(*args)


def _rowwise(body, name, rows, ins, outs, tr=ROW_TILE):
    def row_spec(cols):
        return pl.BlockSpec((tr, cols), lambda i: (i, 0))

    def full_spec(shape):
        zeros = (0,) * len(shape)
        return pl.BlockSpec(shape, lambda i: zeros)

    in_specs = [row_spec(a.shape[1]) if kind == "row" else full_spec(a.shape) for a, kind in ins]
    out_specs = [row_spec(shape[1]) if kind == "row" else full_spec(shape) for shape, _, kind in outs]
    out_shape = [jax.ShapeDtypeStruct(shape, dtype) for shape, dtype, _ in outs]
    return pl.pallas_call(body, out_shape=out_shape, grid=(rows // tr,), in_specs=in_specs, out_specs=out_specs,
                          name=name, compiler_params=_params(("arbitrary",)))(*[a for a, _ in ins])


def _rstd(x):
    return lax.rsqrt(jnp.mean(x * x, axis=-1, keepdims=True) + NORM_EPS)


def _rms_bwd_math(x, g, dy):
    r = _rstd(x)
    gd = dy * g
    dx = r * gd - x * (r * r * r) * jnp.mean(gd * x, axis=-1, keepdims=True)
    dg = jnp.sum(dy * x * r, axis=0, keepdims=True)
    return dx, dg


def _sigmoid(x):
    return 0.5 * jnp.tanh(0.5 * x) + 0.5


def _init_acc(*refs):
    @pl.when(pl.program_id(0) == 0)
    def _():
        for r in refs:
            r[...] = jnp.zeros_like(r)


def _prenorm(h, g):
    rows, cols = h.shape

    def body(h_ref, g_ref, o_ref):
        x = h_ref[...]
        o_ref[...] = (x * _rstd(x) * g_ref[...]).astype(BF)

    return _rowwise(body, "prenorm", rows, [(h, "row"), (g, "full")], [((rows, cols), BF, "row")])[0]


def _post_residual(h, y, g_post, g_pre):
    rows, cols = h.shape
    with_pre = g_pre is not None

    def body(*refs):
        if with_pre:
            h_ref, y_ref, gp_ref, gq_ref, hn_ref, hb_ref = refs
        else:
            h_ref, y_ref, gp_ref, hn_ref, hb_ref = refs
        yv = y_ref[...]
        hn = h_ref[...] + yv * _rstd(yv) * gp_ref[...]
        hn_ref[...] = hn
        hb_ref[...] = (hn * _rstd(hn) * gq_ref[...] if with_pre else hn).astype(BF)

    ins = [(h, "row"), (y, "row"), (g_post, "full")] + ([(g_pre, "full")] if with_pre else [])
    return _rowwise(body, "post_residual_pre" if with_pre else "post_residual", rows, ins,
                    [((rows, cols), F32, "row"), ((rows, cols), BF, "row")])


def _ple_forward(h2, pp, z, g_pre):
    rows, cols = h2.shape

    def body(h_ref, p_ref, z_ref, g_ref, h3_ref, hb_ref):
        h3 = h_ref[...] + p_ref[...] * _sigmoid(z_ref[...])
        h3_ref[...] = h3
        hb_ref[...] = (h3 * _rstd(h3) * g_ref[...]).astype(BF)

    return _rowwise(body, "ple_forward", rows, [(h2, "row"), (pp, "row"), (z, "row"), (g_pre, "full")],
                    [((rows, cols), F32, "row"), ((rows, cols), BF, "row")])


def _ple_loss(h2, pp, z, target):
    rows, cols = h2.shape

    def body(h_ref, p_ref, z_ref, t_ref, dh_ref, sq_ref):
        _init_acc(sq_ref)
        err = h_ref[...] + p_ref[...] * _sigmoid(z_ref[...]) - t_ref[...]
        dh_ref[...] = err * (1.0 / cols)
        sq_ref[...] += jnp.sum(err * err, axis=0, keepdims=True)

    return _rowwise(body, "ple_loss", rows, [(h2, "row"), (pp, "row"), (z, "row"), (target, "row")],
                    [((rows, cols), F32, "row"), ((1, cols), F32, "acc")])


def _ple_backward(dh3, pp, z):
    rows, cols = dh3.shape

    def body(d_ref, p_ref, z_ref, dpp_ref, dz_ref):
        d = d_ref[...]
        s = _sigmoid(z_ref[...])
        dpp_ref[...] = (d * s).astype(BF)
        dz_ref[...] = (d * p_ref[...] * s * (1.0 - s)).astype(BF)

    return _rowwise(body, "ple_backward", rows, [(dh3, "row"), (pp, "row"), (z, "row")],
                    [((rows, cols), BF, "row"), ((rows, cols), BF, "row")])


def _rms_backward(x, g, dy, add, out_dtype):
    rows, cols = x.shape
    with_add = add is not None

    def body(*refs):
        if with_add:
            x_ref, g_ref, dy_ref, add_ref, dx_ref, dg_ref = refs
        else:
            x_ref, g_ref, dy_ref, dx_ref, dg_ref = refs
        _init_acc(dg_ref)
        dx, dg = _rms_bwd_math(x_ref[...], g_ref[...], dy_ref[...].astype(F32))
        if with_add:
            dx = dx + add_ref[...]
        dx_ref[...] = dx.astype(out_dtype)
        dg_ref[...] += dg

    ins = [(x, "row"), (g, "full"), (dy, "row")] + ([(add, "row")] if with_add else [])
    return _rowwise(body, "rms_backward_add" if with_add else "rms_backward", rows, ins,
                    [((rows, cols), out_dtype, "row"), ((1, cols), F32, "acc")])


def _swiglu_forward(gu):
    rows = gu.shape[0]

    def body(gu_ref, o_ref):
        g = gu_ref[:, :D_FF].astype(F32)
        o_ref[...] = (g * _sigmoid(g) * gu_ref[:, D_FF:].astype(F32)).astype(BF)

    return _rowwise(body, "swiglu_forward", rows, [(gu, "row")], [((rows, D_FF), BF, "row")])[0]


def _swiglu_backward(gu, dact):
    rows = gu.shape[0]

    def body(gu_ref, d_ref, o_ref):
        g = gu_ref[:, :D_FF].astype(F32)
        u = gu_ref[:, D_FF:].astype(F32)
        d = d_ref[...].astype(F32)
        s = _sigmoid(g)
        gs = g * s
        o_ref[:, :D_FF] = (d * u * (s + gs * (1.0 - s))).astype(BF)
        o_ref[:, D_FF:] = (d * gs).astype(BF)

    return _rowwise(body, "swiglu_backward", rows, [(gu, "row"), (dact, "row")], [((rows, 2 * D_FF), BF, "row")])[0]


def _rope_tables(positions):
    half = MLA_ROPE // 2
    inv = ROPE_THETA ** (-jnp.arange(half, dtype=F32) / half)
    ang = positions.astype(F32)[:, None] * inv
    cos, sin = jnp.cos(ang), jnp.sin(ang)
    rows = positions.shape[0]
    c = jnp.ones((rows, LANE), F32).at[:, 64:80].set(cos).at[:, 80:96].set(cos)
    sa = jnp.zeros((rows, LANE), F32).at[:, 64:80].set(-sin)
    sb = jnp.zeros((rows, LANE), F32).at[:, 80:96].set(sin)
    return c, sa, sb


def _rope_apply(x, c, sa, sb):
    return x * c + pltpu.roll(x, LANE - 16, 1) * sa + pltpu.roll(x, 16, 1) * sb


def _rope_apply_t(dy, c, sa, sb):
    return dy * c + pltpu.roll(dy * sa, 16, 1) + pltpu.roll(dy * sb, LANE - 16, 1)


def _rope_heads(x, tables, transpose, name):
    rows, cols = x.shape

    def body(x_ref, c_ref, sa_ref, sb_ref, o_ref):
        fn = _rope_apply_t if transpose else _rope_apply
        c, sa, sb = c_ref[...], sa_ref[...], sb_ref[...]
        for head in range(cols // LANE):
            lanes = slice(head * LANE, (head + 1) * LANE)
            o_ref[:, lanes] = fn(x_ref[:, lanes].astype(F32), c, sa, sb).astype(BF)

    blk = pl.BlockSpec((ROW_TILE, cols), lambda i: (i, 0))
    tbl = pl.BlockSpec((ROW_TILE, LANE), lambda i: (i, 0))
    return pl.pallas_call(body, out_shape=jax.ShapeDtypeStruct((rows, cols), BF), grid=(rows // ROW_TILE,),
                          in_specs=[blk, tbl, tbl, tbl], out_specs=blk, name=name,
                          compiler_params=_params(("parallel",)))(x, *tables)


def _mla_mid_forward(a, q_norm, kv_norm, tables):
    rows = a.shape[0]
    qr, kvr = MLA_Q_RANK, MLA_KV_RANK

    def body(a_ref, qn_ref, kn_ref, c_ref, sa_ref, sb_ref, cq_ref, ckv_ref, kr_ref):
        aq = a_ref[:, 0:qr]
        akv = a_ref[:, qr:qr + kvr]
        cq_ref[...] = (aq * _rstd(aq) * qn_ref[...]).astype(BF)
        ckv_ref[...] = (akv * _rstd(akv) * kn_ref[...]).astype(BF)
        kr_ref[...] = _rope_apply(a_ref[:, qr + kvr:], c_ref[...], sa_ref[...], sb_ref[...]).astype(BF)

    ins = [(a, "row"), (q_norm, "full"), (kv_norm, "full")] + [(t, "row") for t in tables]
    return _rowwise(body, "mla_mid_forward", rows, ins,
                    [((rows, qr), BF, "row"), ((rows, kvr), BF, "row"), ((rows, LANE), BF, "row")])


def _mla_mid_backward(a, q_norm, kv_norm, tables, dcq, dckv, dkr):
    rows = a.shape[0]
    qr, kvr = MLA_Q_RANK, MLA_KV_RANK

    def body(a_ref, qn_ref, kn_ref, c_ref, sa_ref, sb_ref, dcq_ref, dckv_ref, dkr_ref, da_ref, dqn_ref, dkn_ref):
        _init_acc(dqn_ref, dkn_ref)
        dxq, dgq = _rms_bwd_math(a_ref[:, 0:qr], qn_ref[...], dcq_ref[...])
        dxk, dgk = _rms_bwd_math(a_ref[:, qr:qr + kvr], kn_ref[...], dckv_ref[...])
        da_ref[:, 0:qr] = dxq.astype(BF)
        da_ref[:, qr:qr + kvr] = dxk.astype(BF)
        da_ref[:, qr + kvr:] = _rope_apply_t(dkr_ref[...], c_ref[...], sa_ref[...], sb_ref[...]).astype(BF)
        dqn_ref[...] += dgq
        dkn_ref[...] += dgk

    ins = ([(a, "row"), (q_norm, "full"), (kv_norm, "full")] + [(t, "row") for t in tables]
           + [(dcq, "row"), (dckv, "row"), (dkr, "row")])
    return _rowwise(body, "mla_mid_backward", rows, ins,
                    [((rows, MLA_A_PAD), BF, "row"), ((1, qr), F32, "acc"), ((1, kvr), F32, "acc")])


def _attn_specs(rows, kv_off, g):
    head = pl.BlockSpec((rows, g * LANE), lambda h: (0, h))
    kv_head = pl.BlockSpec((rows, g * LANE), lambda h: (0, h + kv_off // g))
    shared = pl.BlockSpec((rows, LANE), lambda h: (0, 0))
    col_vec = pl.BlockSpec((g, rows, 1), lambda h: (h, 0, 0))
    row_vec = pl.BlockSpec((g, 1, rows), lambda h: (h, 0, 0))
    return head, kv_head, shared, col_vec, row_vec


def _attn_forward(q, kv, kv_off, kr, cum_col, cum_row, scale, group_size, name):
    rows = q.shape[0]
    heads = HEADS
    t = ATTN_FORWARD_KEY_TILE
    tq = ATTN_Q_TILE
    per = tq // t
    has_kr = kr is not None
    has_f = cum_col is not None
    group = range(group_size)

    def body(*refs):
        it = iter(refs)
        q_ref, kv_ref = next(it), next(it)
        kr_ref = next(it) if has_kr else None
        cc_ref = next(it) if has_f else None
        cr_ref = next(it) if has_f else None
        o_ref, lse_ref = next(it), next(it)
        lo = lax.broadcasted_iota(jnp.int32, (1, LANE), 1) < HEAD_DIM
        row = lax.broadcasted_iota(jnp.int32, (tq, t), 0)
        col = lax.broadcasted_iota(jnp.int32, (tq, t), 1)
        lanes = [slice(g * LANE, (g + 1) * LANE) for g in group]

        def q_block(i, _):
            qs = pl.ds(pl.multiple_of(i * tq, tq), tq)
            qbs = [q_ref[qs, lanes[g]] for g in group]
            cqs = [cc_ref[g, qs, :] if has_f else None for g in group]

            def step(j, carry, diag):
                ks = pl.ds(pl.multiple_of(j * t, t), t)
                skip = diag * t if diag and has_f else 0
                other = kr_ref[ks, :] if has_kr else jnp.zeros((t, LANE), BF)
                kvbs = [kv_ref[ks, lanes[g]] for g in group]
                kks = [jnp.where(lo, kvbs[g], other) for g in group]
                middle = skip + (tq - skip) // 2
                chains = [(g, r0, r1) for g in group for r0, r1 in ((skip, middle), (middle, tq))]

                def logit(chain):
                    g, r0, r1 = chain
                    return lax.dot_general(qbs[g][r0:r1], kks[g], (((1,), (1,)), ((), ())), preferred_element_type=F32)

                logits = {n: logit(c) for n, c in enumerate(chains) if has_f or n == 0}
                pieces = {g: [] for g in group}
                for n, (g, r0, r1) in enumerate(chains):
                    m, l, acc = (a[r0:r1] for a in carry[g])
                    if not has_f and n + 1 < len(chains):
                        logits[n + 1] = logit(chains[n + 1])
                    s = logits[n] * scale
                    if has_f:
                        s = s + (cqs[g][r0:r1] - cr_ref[g, :, ks])
                    if diag is not None:
                        s = jnp.where(col[r0:r1] + diag * t <= row[r0:r1], s, NEG_INF)
                    mn = jnp.maximum(m, jnp.max(s, axis=1, keepdims=True))
                    alpha = jnp.exp(m - mn)
                    p = jnp.exp(s - mn)
                    l = alpha * l + jnp.sum(p, axis=1, keepdims=True)
                    acc = alpha * acc + jnp.dot(p.astype(BF), kvbs[g], preferred_element_type=F32)
                    pieces[g].append((mn, l, acc))
                out = []
                for g in group:
                    parts = ([tuple(a[:skip] for a in carry[g])] if skip else []) + pieces[g]
                    out.append(tuple(jnp.concatenate(column, axis=0) for column in zip(*parts)))
                return tuple(out)

            init = tuple((jnp.full((tq, 1), NEG_INF, F32), jnp.zeros((tq, 1), F32), jnp.zeros((tq, LANE), F32))
                         for _ in group)
            carry = lax.fori_loop(0, i * per, lambda j, c: step(j, c, None), init)
            for d in range(per):
                carry = step(i * per + d, carry, d)
            for g, (m, l, acc) in enumerate(carry):
                o_ref[qs, lanes[g]] = jnp.where(lo, 0.0, acc * (1.0 / l)).astype(BF)
                lse_ref[g, qs, :] = m + jnp.log(l)
            return 0

        lax.fori_loop(0, rows // tq, q_block, 0)

    head, kv_head, shared, col_vec, row_vec = _attn_specs(rows, kv_off, group_size)
    in_specs, args = [head, kv_head], [q, kv]
    if has_kr:
        in_specs.append(shared)
        args.append(kr)
    if has_f:
        in_specs += [col_vec, row_vec]
        args += [cum_col, cum_row]
    return pl.pallas_call(
        body, out_shape=[jax.ShapeDtypeStruct((rows, heads * LANE), BF), jax.ShapeDtypeStruct((heads, rows, 1), F32)],
        grid=(heads // group_size,), in_specs=in_specs, out_specs=[head, col_vec], name=name,
        compiler_params=_params(("arbitrary",)))(*args)


def _attn_backward(q, kv, kv_off, kr, cum_col, cum_row, o, do, lse, scale, group_size, name):
    rows = q.shape[0]
    heads = HEADS
    has_kr = kr is not None
    has_f = cum_col is not None
    t = FOX_BACKWARD_KEY_TILE if has_f else ATTN_TILE
    tq = ATTN_BACKWARD_Q_TILE
    group = range(group_size)

    def body(*refs):
        it = iter(refs)
        q_ref, kv_ref = next(it), next(it)
        kr_ref = next(it) if has_kr else None
        cc_ref = next(it) if has_f else None
        cr_ref = next(it) if has_f else None
        o_ref, do_ref, lse_ref = next(it), next(it), next(it)
        dq_ref, dkv_ref = next(it), next(it)
        dkr_ref = next(it) if has_kr else None
        dck_ref = next(it) if has_f else None
        dcq_ref = next(it) if has_f else None
        dq_acc = next(it)
        lo = lax.broadcasted_iota(jnp.int32, (1, LANE), 1) < HEAD_DIM
        row = lax.broadcasted_iota(jnp.int32, (tq, t), 0)
        col = lax.broadcasted_iota(jnp.int32, (tq, t), 1)
        lanes = [slice(g * LANE, (g + 1) * LANE) for g in group]

        dq_acc[...] = jnp.zeros_like(dq_acc)
        if has_kr:
            _init_acc(dkr_ref)
        if has_f:
            dcq_ref[...] = jnp.zeros_like(dcq_ref)

        def kv_block(first_q, within):
            j = first_q * (tq // t) + within
            skip = within * t
            ks = pl.ds(pl.multiple_of(j * t, t), t)
            other = kr_ref[ks, :] if has_kr else jnp.zeros((t, LANE), BF)
            kvbs = [kv_ref[ks, lanes[g]] for g in group]
            kks = [jnp.where(lo, kvbs[g], other) for g in group]
            cks = [cr_ref[g, :, ks] if has_f else None for g in group]
            causal = col[:tq - skip] <= row[:tq - skip]

            def pair(i, carry, diag):
                start = pl.multiple_of(i * tq, tq)
                qs = pl.ds(start + skip, tq - skip) if diag else pl.ds(start, tq)
                nt = (((1,), (1,)), ((), ()))

                def first_stage(g):
                    qb = q_ref[qs, lanes[g]]
                    dob = do_ref[qs, lanes[g]]
                    return (qb, dob, lax.dot_general(qb, kks[g], nt, preferred_element_type=F32),
                            lax.dot_general(dob, kvbs[g], nt, preferred_element_type=F32))

                first = {g: first_stage(g) for g in group}
                out = []
                for g in group:
                    dkk, dvv, dcs = carry[g]
                    qb, dob, logit, dp = first[g]
                    s = logit * scale
                    if has_f:
                        s = s + (cc_ref[g, qs, :] - cks[g])
                    if diag:
                        s = jnp.where(causal, s, NEG_INF)
                    p = jnp.exp(s - lse_ref[g, qs, :])
                    delta = jnp.sum(dob.astype(F32) * o_ref[qs, lanes[g]].astype(F32), axis=1, keepdims=True)
                    ds = p * (dp - delta)
                    dsb = ds.astype(BF)
                    dvv = dvv + lax.dot_general(p.astype(BF), dob, (((0,), (0,)), ((), ())), preferred_element_type=F32)
                    dkk = dkk + lax.dot_general(dsb, qb, (((0,), (0,)), ((), ())), preferred_element_type=F32)
                    dq_acc[qs, lanes[g]] += jnp.dot(dsb, kks[g], preferred_element_type=F32)
                    if has_f:
                        dcs = dcs + jnp.sum(ds, axis=0, keepdims=True)
                        dcq_ref[g, qs, :] += jnp.sum(ds, axis=1, keepdims=True)
                    out.append((dkk, dvv, dcs))
                return tuple(out)

            init = tuple((jnp.zeros((t, LANE), F32), jnp.zeros((t, LANE), F32), jnp.zeros((1, t), F32)) for _ in group)
            carry = pair(first_q, init, True)
            carry = lax.fori_loop(first_q + 1, rows // tq, lambda i, c: pair(i, c, False), carry)
            for g, (dkk, dvv, dcs) in enumerate(carry):
                dkk = dkk * scale
                dkv_ref[ks, lanes[g]] = jnp.where(lo, dkk, dvv).astype(BF)
                if has_kr:
                    dkr_ref[ks, :] += jnp.where(lo, 0.0, dkk)
                if has_f:
                    dck_ref[g, :, ks] = -dcs

        def q_diagonal(first_q, _):
            for within in range(tq // t):
                kv_block(first_q, within)
            return 0

        lax.fori_loop(0, rows // tq, q_diagonal, 0)
        dq_ref[...] = (dq_acc[...] * scale).astype(BF)

    head, kv_head, shared, col_vec, row_vec = _attn_specs(rows, kv_off, group_size)
    in_specs, args = [head, kv_head], [q, kv]
    if has_kr:
        in_specs.append(shared)
        args.append(kr)
    if has_f:
        in_specs += [col_vec, row_vec]
        args += [cum_col, cum_row]
    in_specs += [head, head, col_vec]
    args += [o, do, lse]
    out_shape = [jax.ShapeDtypeStruct((rows, heads * LANE), BF), jax.ShapeDtypeStruct((rows, heads * LANE), BF)]
    out_specs = [head, head]
    if has_kr:
        out_shape.append(jax.ShapeDtypeStruct((rows, LANE), F32))
        out_specs.append(shared)
    if has_f:
        out_shape += [jax.ShapeDtypeStruct((heads, 1, rows), F32), jax.ShapeDtypeStruct((heads, rows, 1), F32)]
        out_specs += [row_vec, col_vec]
    return pl.pallas_call(
        body, out_shape=out_shape, grid=(heads // group_size,), in_specs=in_specs, out_specs=out_specs,
        scratch_shapes=[pltpu.VMEM((rows, group_size * LANE), F32)], name=name,
        compiler_params=_params(("arbitrary",)))(*args)


def _tri_dot(tri, x):
    return jnp.dot(tri, x, preferred_element_type=F32, precision=lax.Precision.HIGHEST)


def _forget_forward(f_raw, b_f):
    rows = f_raw.shape[0]
    t = ATTN_TILE

    def body(f_ref, b_ref, cum_ref):
        tri = (lax.broadcasted_iota(jnp.int32, (t, t), 1) <= lax.broadcasted_iota(jnp.int32, (t, t), 0)).astype(F32)

        def blk(i, carry):
            sl = pl.ds(pl.multiple_of(i * t, t), t)
            xv = f_ref[sl, :] + b_ref[...]
            log_f = jnp.minimum(xv, 0.0) - jnp.log(1.0 + jnp.exp(-jnp.abs(xv)))
            cum_ref[sl, :] = _tri_dot(tri, log_f) + carry
            return carry + jnp.sum(log_f, axis=0, keepdims=True)

        lax.fori_loop(0, rows // t, blk, jnp.zeros((1, LANE), F32))

    return pl.pallas_call(body, out_shape=jax.ShapeDtypeStruct((rows, LANE), F32), name="forget_forward",
                          compiler_params=_params())(f_raw, b_f)


def _forget_backward(f_raw, b_f, dcum):
    rows = f_raw.shape[0]
    t = ATTN_TILE
    nb = rows // t

    def body(f_ref, b_ref, dc_ref, df_ref, db_ref):
        tri = (lax.broadcasted_iota(jnp.int32, (t, t), 1) >= lax.broadcasted_iota(jnp.int32, (t, t), 0)).astype(F32)

        def blk(i, carry):
            later, db = carry
            sl = pl.ds(pl.multiple_of((nb - 1 - i) * t, t), t)
            dc = dc_ref[sl, :]
            dlog = _tri_dot(tri, dc) + later
            xv = f_ref[sl, :] + b_ref[...]
            df = dlog / (1.0 + jnp.exp(xv))
            df_ref[sl, :] = df.astype(BF)
            return later + jnp.sum(dc, axis=0, keepdims=True), db + jnp.sum(df, axis=0, keepdims=True)

        _, db = lax.fori_loop(0, nb, blk, (jnp.zeros((1, LANE), F32), jnp.zeros((1, LANE), F32)))
        db_ref[...] = db

    return pl.pallas_call(body, out_shape=[jax.ShapeDtypeStruct((rows, LANE), BF), jax.ShapeDtypeStruct((1, LANE), F32)],
                          name="forget_backward", compiler_params=_params())(f_raw, b_f, dcum)


def _t5_bucket(dist):
    max_exact = REL_BUCKETS // 2
    n = jnp.maximum(dist.astype(F32), 1.0)
    large = max_exact + (jnp.log(n / max_exact) / math.log(REL_MAX_DIST / max_exact)
                         * (REL_BUCKETS - max_exact)).astype(jnp.int32)
    large = jnp.minimum(large, REL_BUCKETS - 1)
    return jnp.where(dist < max_exact, dist, large)


def _dil_buckets(dilation):
    i = jnp.arange(Q_BLOCK)[:, None]
    j = jnp.arange(Q_BLOCK)[None, :]
    cur = _t5_bucket(jnp.clip(i - j, 0) * dilation).astype(jnp.int32)
    prev = _t5_bucket(jnp.clip(Q_BLOCK + i - j, 0) * dilation).astype(jnp.int32)
    return cur, prev


def _dil_bias_tiles(tbl_ref, bc_ref, bp_ref, bias_ref, group, hp, pairs):
    ii = lax.broadcasted_iota(jnp.int32, (Q_BLOCK, Q_BLOCK), 0)
    jj = lax.broadcasted_iota(jnp.int32, (Q_BLOCK, Q_BLOCK), 1)
    for hh in range(2 * pairs):
        col = group * HEADS + 2 * pairs * hp + hh
        acc_c = jnp.zeros((Q_BLOCK, Q_BLOCK), F32)
        acc_p = jnp.zeros((Q_BLOCK, Q_BLOCK), F32)
        for b in range(REL_BUCKETS):
            val = tbl_ref[b, col]
            acc_c = jnp.where(bc_ref[...] == b, val, acc_c)
            acc_p = jnp.where(bp_ref[...] == b, val, acc_p)
        bias_ref[2 * hh] = jnp.where(jj <= ii, acc_c, NEG_INF)
        bias_ref[2 * hh + 1] = jnp.where(jj >= ii, acc_p, NEG_INF)


def _dil_view(qkv, group, dilation):
    if dilation == 1:
        return qkv
    width = 3 * HEADS * HEAD_DIM
    return qkv[:, group * width:(group + 1) * width].reshape(qkv.shape[0] // dilation, dilation * width)


def _dil_specs(group, dilation, length):
    width = DIL_PAIRS[dilation] * LANE
    per = 8 // DIL_PAIRS[dilation]

    def col(kind):
        if dilation == 1:
            return pl.BlockSpec((length, width), lambda hp, r: (0, (group * 3 + kind) * per + hp))
        return pl.BlockSpec((length, width), lambda hp, r: (0, (r * 3 + kind) * per + hp))

    out = pl.BlockSpec((length, width), lambda hp, r: (0, r * per + hp))
    tile = pl.BlockSpec((Q_BLOCK, Q_BLOCK), lambda hp, r: (0, 0))
    table = pl.BlockSpec(memory_space=pltpu.SMEM)
    return col, out, tile, table


def _dil_forward(view, group, dilation, table, buckets):
    length = view.shape[0]
    rows = length * dilation
    pairs = DIL_PAIRS[dilation]
    nb = length // Q_BLOCK
    scale = HEAD_DIM ** -0.5
    qb = Q_BLOCK

    def body(tbl_ref, bc_ref, bp_ref, q_ref, k_ref, v_ref, o_ref, lse_ref, bias_ref):
        hp = pl.program_id(0)

        @pl.when(pl.program_id(1) == 0)
        def _():
            _dil_bias_tiles(tbl_ref, bc_ref, bp_ref, bias_ref, group, hp, pairs)

        lo = lax.broadcasted_iota(jnp.int32, (1, LANE), 1) < HEAD_DIM
        nt = (((1,), (1,)), ((), ()))

        def blk(n, first):
            cur = pl.ds(0, qb) if first else pl.ds(pl.multiple_of(n * qb, qb), qb)
            prev = None if first else pl.ds(pl.multiple_of((n - 1) * qb, qb), qb)
            logits = []
            for pair in range(pairs):
                lanes = slice(pair * LANE, (pair + 1) * LANE)
                qn = q_ref[cur, lanes] * scale
                for hh in range(2):
                    qm = jnp.where(lo if hh == 0 else ~lo, qn, jnp.zeros_like(qn))
                    s_c = lax.dot_general(qm, k_ref[cur, lanes], nt, preferred_element_type=F32)
                    s_p = None if first else lax.dot_general(qm, k_ref[prev, lanes], nt, preferred_element_type=F32)
                    logits.append((s_c, s_p))
            for pair in range(pairs):
                lanes = slice(pair * LANE, (pair + 1) * LANE)
                outs, lses = [], []
                for hh in range(2):
                    bias = 4 * pair + 2 * hh
                    s_c, s_p = logits[2 * pair + hh]
                    s_c = s_c + bias_ref[bias]
                    m = jnp.max(s_c, axis=1, keepdims=True)
                    if not first:
                        s_p = s_p + bias_ref[bias + 1]
                        m = jnp.maximum(m, jnp.max(s_p, axis=1, keepdims=True))
                    e_c = jnp.exp(s_c - m)
                    l = jnp.sum(e_c, axis=1, keepdims=True)
                    acc = jnp.dot(e_c.astype(BF), v_ref[cur, lanes], preferred_element_type=F32)
                    if not first:
                        e_p = jnp.exp(s_p - m)
                        l = l + jnp.sum(e_p, axis=1, keepdims=True)
                        acc = acc + jnp.dot(e_p.astype(BF), v_ref[prev, lanes], preferred_element_type=F32)
                    outs.append(acc * (1.0 / l))
                    lses.append(m + jnp.log(l))
                o_ref[cur, lanes] = jnp.where(lo, outs[0], outs[1])
                lse_ref[cur, lanes] = jnp.where(lo, lses[0], lses[1])
            return 0

        blk(0, True)
        if nb > 1:
            lax.fori_loop(1, nb, lambda n, _: blk(n, False), 0)

    col, out, tile, tbl = _dil_specs(group, dilation, length)
    bc, bp = buckets
    o, lse = pl.pallas_call(
        body, out_shape=[jax.ShapeDtypeStruct((length, dilation * D_MODEL), F32)] * 2,
        grid=(8 // pairs, dilation), in_specs=[tbl, tile, tile, col(0), col(1), col(2)], out_specs=[out, out],
        scratch_shapes=[pltpu.VMEM((4 * pairs, qb, qb), F32)], name=f"dilated_forward_{dilation}",
        compiler_params=_params(("arbitrary", "arbitrary")))(
            table, bc, bp, view, view, view)
    return o.reshape(rows, D_MODEL), lse.reshape(rows, D_MODEL)


def _dil_backward(view, group, dilation, table, buckets, do_g, lse, dlt):
    length = view.shape[0]
    rows = length * dilation
    pairs = DIL_PAIRS[dilation]
    nb = length // Q_BLOCK
    scale = HEAD_DIM ** -0.5
    qb = Q_BLOCK

    def body(tbl_ref, bc_ref, bp_ref, q_ref, k_ref, v_ref, do_ref, lse_ref, dlt_ref,
             dq_ref, dk_ref, dv_ref, db_ref, bias_ref, dk_acc, dv_acc):
        hp = pl.program_id(0)

        @pl.when(pl.program_id(1) == 0)
        def _():
            _dil_bias_tiles(tbl_ref, bc_ref, bp_ref, bias_ref, group, hp, pairs)
            db_ref[...] = jnp.zeros_like(db_ref)

        dk_acc[...] = jnp.zeros_like(dk_acc)
        dv_acc[...] = jnp.zeros_like(dv_acc)
        lo = lax.broadcasted_iota(jnp.int32, (1, LANE), 1) < HEAD_DIM
        tn = (((0,), (0,)), ((), ()))
        nt = (((1,), (1,)), ((), ()))

        def blk(n, first):
            cur = pl.ds(0, qb) if first else pl.ds(pl.multiple_of(n * qb, qb), qb)
            prev = None if first else pl.ds(pl.multiple_of((n - 1) * qb, qb), qb)
            inputs = []
            for pair in range(pairs):
                lanes = slice(pair * LANE, (pair + 1) * LANE)
                qn = q_ref[cur, lanes] * scale
                don = do_ref[cur, lanes]
                for hh in range(2):
                    mask = lo if hh == 0 else ~lo
                    qm = jnp.where(mask, qn, jnp.zeros_like(qn))
                    dom = jnp.where(mask, don, jnp.zeros_like(don))
                    stage = [qm, dom, lax.dot_general(qm, k_ref[cur, lanes], nt, preferred_element_type=F32),
                             lax.dot_general(dom, v_ref[cur, lanes], nt, preferred_element_type=F32)]
                    if not first:
                        stage += [lax.dot_general(qm, k_ref[prev, lanes], nt, preferred_element_type=F32),
                                  lax.dot_general(dom, v_ref[prev, lanes], nt, preferred_element_type=F32)]
                    inputs.append(stage)
            for pair in range(pairs):
                lanes = slice(pair * LANE, (pair + 1) * LANE)
                kc = k_ref[cur, lanes]
                if not first:
                    kp = k_ref[prev, lanes]
                lse_n = lse_ref[cur, lanes]
                dlt_n = dlt_ref[cur, lanes]
                dqs = []
                dkc = jnp.zeros((qb, LANE), F32)
                dkp = jnp.zeros((qb, LANE), F32)
                dvc = jnp.zeros((qb, LANE), F32)
                dvp = jnp.zeros((qb, LANE), F32)
                for hh in range(2):
                    bias = 4 * pair + 2 * hh
                    mask = lo if hh == 0 else ~lo
                    qm, dom, s_c, dp_c = inputs[2 * pair + hh][:4]
                    lse_h = jnp.max(jnp.where(mask, lse_n, -3e38), axis=1, keepdims=True)
                    dlt_h = jnp.max(jnp.where(mask, dlt_n, -3e38), axis=1, keepdims=True)
                    p_c = jnp.exp(s_c + bias_ref[bias] - lse_h)
                    ds_c = p_c * (dp_c - dlt_h)
                    db_ref[pair, 2 * hh] += ds_c
                    dsc_b = ds_c.astype(BF)
                    dq = jnp.dot(dsc_b, kc, preferred_element_type=F32)
                    dkc = dkc + lax.dot_general(dsc_b, qm, tn, preferred_element_type=F32)
                    dvc = dvc + lax.dot_general(p_c.astype(BF), dom, tn, preferred_element_type=F32)
                    if not first:
                        s_p, dp_p = inputs[2 * pair + hh][4:]
                        p_p = jnp.exp(s_p + bias_ref[bias + 1] - lse_h)
                        ds_p = p_p * (dp_p - dlt_h)
                        db_ref[pair, 2 * hh + 1] += ds_p
                        dsp_b = ds_p.astype(BF)
                        dq = dq + jnp.dot(dsp_b, kp, preferred_element_type=F32)
                        dkp = dkp + lax.dot_general(dsp_b, qm, tn, preferred_element_type=F32)
                        dvp = dvp + lax.dot_general(p_p.astype(BF), dom, tn, preferred_element_type=F32)
                    dqs.append(dq)
                dq_ref[cur, lanes] = (jnp.where(lo, dqs[0], dqs[1]) * scale).astype(BF)
                dk_acc[cur, lanes] += dkc
                dv_acc[cur, lanes] += dvc
                if not first:
                    dk_acc[prev, lanes] += dkp
                    dv_acc[prev, lanes] += dvp
            return 0

        blk(0, True)
        if nb > 1:
            lax.fori_loop(1, nb, lambda n, _: blk(n, False), 0)
        dk_ref[...] = dk_acc[...].astype(BF)
        dv_ref[...] = dv_acc[...].astype(BF)

    col, out, tile, tbl = _dil_specs(group, dilation, length)
    bc, bp = buckets
    wide = (length, dilation * D_MODEL)
    dq, dk, dv, db = pl.pallas_call(
        body, out_shape=[jax.ShapeDtypeStruct(wide, BF)] * 3 + [jax.ShapeDtypeStruct((8, 4, qb, qb), F32)],
        grid=(8 // pairs, dilation), in_specs=[tbl, tile, tile, col(0), col(1), col(2), out, out, out],
        out_specs=[out, out, out, pl.BlockSpec((pairs, 4, qb, qb), lambda hp, r: (hp, 0, 0, 0))],
        scratch_shapes=[pltpu.VMEM((4 * pairs, qb, qb), F32), pltpu.VMEM((length, pairs * LANE), F32),
                        pltpu.VMEM((length, pairs * LANE), F32)],
        name=f"dilated_backward_{dilation}", compiler_params=_params(("arbitrary", "arbitrary")))(
            table, bc, bp, view, view, view,
            do_g.reshape(wide), lse.reshape(wide), dlt.reshape(wide))
    return dq.reshape(rows, D_MODEL), dk.reshape(rows, D_MODEL), dv.reshape(rows, D_MODEL), db


def _head_sums(x, lo):
    s0 = jnp.sum(jnp.where(lo, x, 0.0), axis=1, keepdims=True)
    s1 = jnp.sum(jnp.where(lo, 0.0, x), axis=1, keepdims=True)
    return jnp.where(lo, s0, s1)


def _dil_merge_forward(outs, lses):
    rows = outs[0].shape[0]

    def body(o0, o1, o2, l0, l1, l2, o_ref):
        ls = [l0[...], l1[...], l2[...]]
        m = jnp.maximum(jnp.maximum(ls[0], ls[1]), ls[2])
        es = [jnp.exp(v - m) for v in ls]
        tot = es[0] + es[1] + es[2]
        o_ref[...] = ((es[0] * o0[...] + es[1] * o1[...] + es[2] * o2[...]) / tot).astype(BF)

    blk = pl.BlockSpec((ROW_TILE, LANE), lambda i, j: (i, j))
    return pl.pallas_call(body, out_shape=jax.ShapeDtypeStruct((rows, D_MODEL), BF), grid=(rows // ROW_TILE, 8),
                          in_specs=[blk] * 6, out_specs=blk, name="dilated_merge_forward",
                          compiler_params=_params(("parallel", "parallel")))(*outs, *lses)


def _dil_merge_backward(outs, lses, do):
    rows = outs[0].shape[0]

    def body(o0, o1, o2, l0, l1, l2, do_ref, d0, d1, d2, t0, t1, t2):
        lo = lax.broadcasted_iota(jnp.int32, (1, LANE), 1) < HEAD_DIM
        ls = [l0[...], l1[...], l2[...]]
        os_ = [o0[...], o1[...], o2[...]]
        m = jnp.maximum(jnp.maximum(ls[0], ls[1]), ls[2])
        es = [jnp.exp(v - m) for v in ls]
        inv = 1.0 / (es[0] + es[1] + es[2])
        alphas = [e * inv for e in es]
        dov = do_ref[...]
        merged = alphas[0] * os_[0] + alphas[1] * os_[1] + alphas[2] * os_[2]
        dot = _head_sums(dov * merged, lo)
        for a, d_ref, t_ref in zip(alphas, (d0, d1, d2), (t0, t1, t2)):
            d_ref[...] = (a * dov).astype(BF)
            t_ref[...] = a * dot

    blk = pl.BlockSpec((ROW_TILE, LANE), lambda i, j: (i, j))
    res = pl.pallas_call(
        body, out_shape=[jax.ShapeDtypeStruct((rows, D_MODEL), BF)] * 3 + [jax.ShapeDtypeStruct((rows, D_MODEL), F32)] * 3,
        grid=(rows // ROW_TILE, 8), in_specs=[blk] * 7, out_specs=[blk] * 6, name="dilated_merge_backward",
        compiler_params=_params(("parallel", "parallel")))(*outs, *lses, do)
    return res[:3], res[3:]


def _rel_bias_grad(dbs, buckets):
    def body(db_ref, bc_ref, bp_ref, o_ref):
        g = pl.program_id(0)
        hp = pl.program_id(1)

        @pl.when((g == 0) & (hp == 0))
        def _():
            o_ref[...] = jnp.zeros_like(o_ref)

        rr = lax.broadcasted_iota(jnp.int32, (REL_BUCKETS, LANE), 0)
        cc = lax.broadcasted_iota(jnp.int32, (REL_BUCKETS, LANE), 1)
        bc = bc_ref[0]
        bp = bp_ref[0]
        acc = jnp.zeros((REL_BUCKETS, LANE), F32)
        for hh in range(2):
            col = g * HEADS + 2 * hp + hh
            d_c = db_ref[0, 0, 2 * hh]
            d_p = db_ref[0, 0, 2 * hh + 1]
            for b in range(REL_BUCKETS):
                val = (jnp.sum(jnp.where(bc == b, d_c, 0.0), keepdims=True)
                       + jnp.sum(jnp.where(bp == b, d_p, 0.0), keepdims=True))
                acc = jnp.where((rr == b) & (cc == col), val, acc)
        o_ref[...] += acc

    db_all = jnp.stack(dbs)
    bc_all = jnp.stack([b[0] for b in buckets])
    bp_all = jnp.stack([b[1] for b in buckets])
    tile = pl.BlockSpec((1, Q_BLOCK, Q_BLOCK), lambda g, hp: (g, 0, 0))
    return pl.pallas_call(
        body, out_shape=jax.ShapeDtypeStruct((REL_BUCKETS, LANE), F32), grid=(3, 8),
        in_specs=[pl.BlockSpec((1, 1, 4, Q_BLOCK, Q_BLOCK), lambda g, hp: (g, hp, 0, 0, 0)), tile, tile],
        out_specs=pl.BlockSpec((REL_BUCKETS, LANE), lambda g, hp: (0, 0)), name="rel_bias_grad",
        compiler_params=_params(("arbitrary", "arbitrary")))(db_all, bc_all, bp_all)


def _mla_forward(hn, w, tables):
    a = _matmul(hn, w["w_a"], name="mla_a")
    cq, ckv, kr = _mla_mid_forward(a, w["q_norm"], w["kv_norm"], tables)
    q_raw = _matmul(cq, w["w_uq"], name="mla_uq")
    q = _rope_heads(q_raw, tables, False, "rope_forward")
    kv = _matmul(ckv, w["w_ukv"], b_chunks=True, out_dtype=BF, name="mla_ukv")
    scale = (HEAD_DIM + MLA_ROPE) ** -0.5
    o, lse = _attn_forward(q, kv, 0, kr, None, None, scale, MLA_GROUP, "mla_attention_forward")
    y = _matmul(o, w["w_o"], name="attn_out")
    return y, dict(hn=hn, a=a, cq=cq, ckv=ckv, kr=kr, q=q, kv=kv, o=o, lse=lse)


def _mla_backward(dy, w, s, tables):
    scale = (HEAD_DIM + MLA_ROPE) ** -0.5
    g = {}
    g["w_o"] = _matmul(s["o"], dy, ta=True, out_dtype=BF, name="attn_out_dw")
    do = _matmul(dy, w["w_o"], tb=True, out_dtype=BF, name="attn_out_dx")
    dq, dkv, dkr = _attn_backward(s["q"], s["kv"], 0, s["kr"], None, None, s["o"], do, s["lse"], scale,
                                  MLA_GROUP, "mla_attention_backward")
    dq_raw = _rope_heads(dq, tables, True, "rope_backward")
    g["w_uq"] = _matmul(s["cq"], dq_raw, ta=True, out_dtype=BF, name="mla_uq_dw")
    dcq = _matmul(dq_raw, w["w_uq"], tb=True, name="mla_uq_dx")
    g["w_ukv"] = _matmul(s["ckv"], dkv, ta=True, out_chunks=True, out_dtype=BF, name="mla_ukv_dw")
    dckv = _matmul(dkv, w["w_ukv"], tb=True, b_chunks=True, name="mla_ukv_dx")
    da, g["q_norm"], g["kv_norm"] = _mla_mid_backward(s["a"], w["q_norm"], w["kv_norm"], tables, dcq, dckv, dkr)
    g["w_a"] = _matmul(s["hn"], da, ta=True, out_dtype=BF, name="mla_a_dw")
    dhn = _matmul(da, w["w_a"], tb=True, name="mla_a_dx")
    return dhn, g


def _fox_forward(hn, w):
    qkv = _matmul(hn, w["w_qkv"], out_dtype=BF, name="fox_qkv")
    f_raw = _matmul(hn, w["w_f"], name="fox_f")
    cum = _forget_forward(f_raw, w["b_f"])
    cum_heads = cum[:, :HEADS].T
    cum_col, cum_row = cum_heads[:, :, None], cum_heads[:, None, :]
    o, lse = _attn_forward(qkv, qkv, HEADS, None, cum_col, cum_row, HEAD_DIM ** -0.5, FOX_GROUP,
                           "fox_attention_forward")
    y = _matmul(o, w["w_o"], name="attn_out")
    return y, dict(hn=hn, qkv=qkv, f_raw=f_raw, cum_col=cum_col, cum_row=cum_row, o=o, lse=lse)


def _fox_backward(dy, w, s):
    g = {}
    g["w_o"] = _matmul(s["o"], dy, ta=True, out_dtype=BF, name="attn_out_dw")
    do = _matmul(dy, w["w_o"], tb=True, out_dtype=BF, name="attn_out_dx")
    dq, dkv, dck, dcq = _attn_backward(s["qkv"], s["qkv"], HEADS, None, s["cum_col"], s["cum_row"], s["o"], do,
                                       s["lse"], HEAD_DIM ** -0.5, FOX_GROUP, "fox_attention_backward")
    dcum = jnp.pad((dck[:, 0, :] + dcq[:, :, 0]).T, ((0, 0), (0, LANE - HEADS)))
    df, g["b_f"] = _forget_backward(s["f_raw"], w["b_f"], dcum)
    dqkv = jnp.concatenate([dq, dkv], axis=1)
    g["w_qkv"] = _matmul(s["hn"], dqkv, ta=True, out_dtype=BF, name="fox_qkv_dw")
    g["w_f"] = _matmul(s["hn"], df, ta=True, out_dtype=BF, name="fox_f_dw")
    dhn = _matmul(dqkv, w["w_qkv"], tb=True, name="fox_qkv_dx")
    dhn = _matmul(df, w["w_f"], tb=True, add=dhn, name="fox_f_dx")
    return dhn, g


def _dil_mixer_forward(hn, w, buckets):
    qkv = _matmul(hn, w["w_qkv"], b_chunks=True, out_dtype=BF, name="dil_qkv")
    views = [_dil_view(qkv, grp, dilation) for grp, (_, dilation) in enumerate(DIL_PATTERNS)]
    outs, lses = [], []
    for grp, (_, dilation) in enumerate(DIL_PATTERNS):
        o_g, lse_g = _dil_forward(views[grp], grp, dilation, w["rel_bias"], buckets[grp])
        outs.append(o_g)
        lses.append(lse_g)
    o = _dil_merge_forward(outs, lses)
    y = _matmul(o, w["w_o"], name="dil_out")
    return y, dict(hn=hn, views=views, outs=outs, lses=lses, o=o)


def _dil_mixer_backward(dy, w, s, buckets):
    g = {}
    g["w_o"] = _matmul(s["o"], dy, ta=True, out_dtype=BF, name="dil_out_dw")
    do = _matmul(dy, w["w_o"], tb=True, name="dil_out_dx")
    do_gs, dlts = _dil_merge_backward(s["outs"], s["lses"], do)
    parts, dbs = [], []
    for grp, (_, dilation) in enumerate(DIL_PATTERNS):
        dq, dk, dv, db = _dil_backward(s["views"][grp], grp, dilation, w["rel_bias"], buckets[grp], do_gs[grp],
                                       s["lses"][grp], dlts[grp])
        parts += [dq, dk, dv]
        dbs.append(db)
    dqkv = jnp.concatenate(parts, axis=1)
    g["rel_bias"] = _rel_bias_grad(dbs, buckets)
    g["w_qkv"] = _matmul(s["hn"], dqkv, ta=True, out_chunks=True, out_dtype=BF, name="dil_qkv_dw")
    dhn = _matmul(dqkv, w["w_qkv"], tb=True, b_chunks=True, name="dil_qkv_dx")
    return dhn, g


def _mixer_weights(i, lw, small):
    mixer, j = i % N_MIXERS, i // N_MIXERS
    if mixer == 0:
        return dict(lw["mixer"], q_norm=small["mla_q_norm"][j][None, :], kv_norm=small["mla_kv_norm"][j][None, :])
    if mixer == 1:
        return dict(lw["mixer"], rel_bias=small["rel_bias"])
    return dict(lw["mixer"], b_f=jnp.pad(small["fox_b_f"][j][None, :], ((0, 0), (0, LANE - HEADS))))


MIXER_PART, COMMON_PART = 0, 1


def _run_layers(x, p, positions, target, get_part, get_small, put_part):
    tables = _rope_tables(positions)
    buckets = [_dil_buckets(d) for _, d in DIL_PATTERNS]
    layers, saved = [], []
    h = x
    first = get_part(0, MIXER_PART, positions)
    small = get_small()

    def gain(i, k):
        return small["norm_g"][i, k][None, :]

    hn = _prenorm(h, gain(0, 0))
    sq = dh = None
    for i in range(DEPTH):
        mixer = i % N_MIXERS
        lw = dict(first if i == 0 else get_part(i, MIXER_PART, h))
        mw = _mixer_weights(i, lw, small)
        if mixer == 0:
            y, ms = _mla_forward(hn, mw, tables)
        elif mixer == 1:
            y, ms = _dil_mixer_forward(hn, mw, buckets)
        else:
            y, ms = _fox_forward(hn, mw)
        if "ffn_w_in" not in lw:
            lw.update(get_part(i, COMMON_PART, y))
        layers.append(lw)
        h1, hn2 = _post_residual(h, y, gain(i, 1), gain(i, 2))
        gu = _matmul(hn2, lw["ffn_w_in"], b_chunks=True, out_dtype=BF, name="ffn_in")
        act = _swiglu_forward(gu)
        f = _matmul(act, lw["ffn_w_out"], name="ffn_out")
        h2, h2b = _post_residual(h1, f, gain(i, 3), None)
        pp = _matmul(p[i], lw["ple_w_proj"], b_chunks=True, name="ple_proj")
        z = _matmul(h2b, lw["ple_w_gate"], name="ple_gate")
        saved.append(dict(h=h, y=y, ms=ms, h1=h1, hn2=hn2, gu=gu, act=act, f=f, h2b=h2b, pp=pp, z=z))
        if i + 1 < DEPTH:
            h, hn = _ple_forward(h2, pp, z, gain(i + 1, 0))
        else:
            dh, sq = _ple_loss(h2, pp, z, target)

    norm_rows = [[None] * 4 for _ in range(DEPTH)]
    sg = dict(mla_q_norm={}, mla_kv_norm={}, rel_bias=None, fox_b_f={})
    for i in reversed(range(DEPTH)):
        s, lw = saved[i], layers[i]
        mixer, j = i % N_MIXERS, i // N_MIXERS
        mw = _mixer_weights(i, lw, small)
        lg = {}
        dpp, dz = _ple_backward(dh, s["pp"], s["z"])
        lg["ple_w_proj"] = _matmul(p[i], dpp, ta=True, out_chunks=True, out_dtype=BF, name="ple_proj_dw")
        lg["ple_w_gate"] = _matmul(s["h2b"], dz, ta=True, out_dtype=BF, name="ple_gate_dw")
        dh2 = _matmul(dz, lw["ple_w_gate"], tb=True, add=dh, name="ple_gate_dx")
        df, norm_rows[i][3] = _rms_backward(s["f"], gain(i, 3), dh2, None, BF)
        lg["ffn_w_out"] = _matmul(s["act"], df, ta=True, out_dtype=BF, name="ffn_out_dw")
        dact = _matmul(df, lw["ffn_w_out"], tb=True, out_dtype=BF, name="ffn_out_dx")
        dgu = _swiglu_backward(s["gu"], dact)
        lg["ffn_w_in"] = _matmul(s["hn2"], dgu, ta=True, out_chunks=True, out_dtype=BF, name="ffn_in_dw")
        split = i in SPLIT_LAYERS
        zero = put_part(i, COMMON_PART, lg)[0:1, 0:1] if split else 0.0
        dhn2 = _matmul(dgu, lw["ffn_w_in"], tb=True, b_chunks=True, name="ffn_in_dx")
        dh1, norm_rows[i][2] = _rms_backward(s["h1"], gain(i, 2), dhn2, dh2, F32)
        dy, norm_rows[i][1] = _rms_backward(s["y"], gain(i, 1) + zero, dh1, None, BF)
        if mixer == 0:
            dhn, mg = _mla_backward(dy, mw, s["ms"], tables)
            sg["mla_q_norm"][j] = mg.pop("q_norm")
            sg["mla_kv_norm"][j] = mg.pop("kv_norm")
        elif mixer == 1:
            dhn, mg = _dil_mixer_backward(dy, mw, s["ms"], buckets)
            rel = mg.pop("rel_bias")[:, :3 * HEADS]
            sg["rel_bias"] = rel if sg["rel_bias"] is None else sg["rel_bias"] + rel
        else:
            dhn, mg = _fox_backward(dy, mw, s["ms"])
            sg["fox_b_f"][j] = mg.pop("b_f")[:, :HEADS]
        token = put_part(i, MIXER_PART, dict(mixer=mg) if split else dict(lg, mixer=mg))
        dh, norm_rows[i][0] = _rms_backward(s["h"], gain(i, 0) + token[0:1, 0:1], dhn, dh1, F32)
    small_grads = dict(norm_g=jnp.stack([jnp.concatenate(row, axis=0) for row in norm_rows]),
                       rel_bias=sg["rel_bias"])
    for k in ("mla_q_norm", "mla_kv_norm", "fox_b_f"):
        small_grads[k] = jnp.concatenate([sg[k][j] for j in sorted(sg[k])], axis=0)
    return sq, dh, small_grads


BIG = ("ffn_w_in", "ffn_w_out", "ple_w_proj", "ple_w_gate", "mla_w_a", "mla_w_uq", "mla_w_ukv", "mla_w_o",
       "dil_w_qkv", "dil_w_o", "fox_w_qkvf", "fox_w_o")
SMALL_SHARDED = ("norm_g", "mla_q_norm", "mla_kv_norm")
SMALL_REPLICATED = ("rel_bias", "fox_b_f")
WEIGHTS = ("norm_g", "ffn_w_in", "ffn_w_out", "ple_w_proj", "ple_w_gate", "rel_bias", "mla_w_a", "mla_q_norm",
           "mla_kv_norm", "mla_w_uq", "mla_w_ukv", "mla_w_o", "dil_w_qkv", "dil_w_o", "fox_w_qkvf", "fox_b_f", "fox_w_o")


TRANSPOSED = "fox_w_qkvf"
SPLIT_LAYERS = (0, 1, 2, 3)
LAYER_COMMON = ("ffn_w_in", "ffn_w_out", "ple_w_proj", "ple_w_gate")
MIXER_WEIGHTS = (("mla_w_a", "mla_w_uq", "mla_w_ukv", "mla_w_o"), ("dil_w_qkv", "dil_w_o"), ("fox_w_qkvf", "fox_w_o"))


def _part_names(i, part):
    if i in SPLIT_LAYERS:
        return MIXER_WEIGHTS[i % N_MIXERS] if part == MIXER_PART else LAYER_COMMON
    return MIXER_WEIGHTS[i % N_MIXERS] + LAYER_COMMON if part == MIXER_PART else ()


def _layer_slot(name, i):
    return i if name in LAYER_COMMON else i // N_MIXERS


def _merge_rows(chunks):
    n, r, c = chunks.shape
    return chunks.reshape(n * r, c)


def _merge_cols(chunks):
    n, r, c = chunks.shape
    return chunks.transpose(1, 0, 2).reshape(r, n * c)


def _pad_heads_out(wo):
    w3 = wo.reshape(HEADS, HEAD_DIM, D_MODEL)
    return jnp.pad(w3, ((0, 0), (HEAD_DIM, 0), (0, 0))).reshape(HEADS * LANE, D_MODEL)


def _part_to_compute(i, part, ch):
    lw = {}
    if "ffn_w_in" in ch:
        lw.update(ffn_w_in=ch["ffn_w_in"], ffn_w_out=_merge_rows(ch["ffn_w_out"]), ple_w_proj=ch["ple_w_proj"],
                  ple_w_gate=_merge_rows(ch["ple_w_gate"]))
    if part == COMMON_PART:
        return lw
    mixer = i % N_MIXERS
    if mixer == 0:
        wa = _merge_rows(ch["mla_w_a"])
        rank = MLA_Q_RANK + MLA_KV_RANK
        wa_p = jnp.concatenate([wa[:, :rank], jnp.zeros((wa.shape[0], 64), wa.dtype), wa[:, rank:],
                                jnp.zeros((wa.shape[0], 32), wa.dtype)], axis=1)
        wuq = _merge_cols(ch["mla_w_uq"]).reshape(MLA_Q_RANK, HEADS, HEAD_DIM + MLA_ROPE)
        wuq_p = jnp.pad(wuq, ((0, 0), (0, 0), (0, LANE - HEAD_DIM - MLA_ROPE))).reshape(MLA_Q_RANK, HEADS * LANE)
        lw["mixer"] = dict(w_a=wa_p, w_uq=wuq_p, w_ukv=ch["mla_w_ukv"], w_o=_pad_heads_out(_merge_rows(ch["mla_w_o"])))
    elif mixer == 1:
        lw["mixer"] = dict(w_qkv=ch["dil_w_qkv"], w_o=_merge_rows(ch["dil_w_o"]))
    else:
        wf = _merge_rows(ch["fox_w_qkvf"]).T
        inner = HEADS * HEAD_DIM
        q3 = wf[:, :inner].reshape(D_MODEL, HEADS, HEAD_DIM)
        k3 = wf[:, inner:2 * inner].reshape(D_MODEL, HEADS, HEAD_DIM)
        v3 = wf[:, 2 * inner:3 * inner].reshape(D_MODEL, HEADS, HEAD_DIM)
        q_p = jnp.pad(q3, ((0, 0), (0, 0), (0, HEAD_DIM))).reshape(D_MODEL, HEADS * LANE)
        kv_p = jnp.concatenate([k3, v3], axis=2).reshape(D_MODEL, HEADS * LANE)
        f_p = jnp.pad(wf[:, 3 * inner:], ((0, 0), (0, LANE - HEADS)))
        lw["mixer"] = dict(w_qkv=jnp.concatenate([q_p, kv_p], axis=1), w_f=f_p,
                           w_o=_pad_heads_out(_merge_rows(ch["fox_w_o"])))
    return lw


def _part_contributions(i, part, lg, chunk_shapes):
    spec = {k: jax.ShapeDtypeStruct(s, BF) for k, s in chunk_shapes.items()}
    (contrib,) = jax.linear_transpose(functools.partial(_part_to_compute, i, part), spec)(lg)
    return contrib


def _chip_peers():
    x, y, c = lax.axis_index("x"), lax.axis_index("y"), lax.axis_index("c")
    peers = [(1 - x, y), (x, 1 - y), (1 - x, 1 - y)]
    return x, y, c, peers


SEM_SPEC = pl.BlockSpec(memory_space=pltpu.SEMAPHORE)
ANY_SPEC = pl.BlockSpec(memory_space=pl.ANY)
SPLIT_EFFECT = pltpu.SideEffectType.DATAFLOW_SIDE_EFFECTING


def _own_slot(shard):
    me = 2 * lax.axis_index("x") + lax.axis_index("y")
    return lax.dynamic_update_index_in_dim(lax.empty((N_CHIPS,) + shard.shape, shard.dtype), shard[None], me, 0)


def _spread_copy(src, land, k, peer, c, send_sems, recv_sems, index, src_slot, slot):
    px, py = peer
    return pltpu.make_async_remote_copy(
        src_ref=src.at[src_slot], dst_ref=land.at[slot],
        send_sem=send_sems.at[3 * index + k], recv_sem=recv_sems.at[3 * index + k],
        device_id=(px, py, c), device_id_type=MESH)


def _spread_start(bufs, srcs, after, name):
    n = len(bufs)
    exchange = srcs is not None
    arrays = (list(srcs) if exchange else []) + list(bufs)
    na = len(arrays)

    def body(*refs):
        src, land = refs[:n], refs[na - n:na]
        send_sems, recv_sems = refs[na + 1], refs[na + 2]
        token = refs[-1]
        x, y, c, peers = _chip_peers()
        me = 2 * x + y
        for w in range(n):
            for k, peer in enumerate(peers):
                src_slot = 2 * peer[0] + peer[1] if exchange else me
                _spread_copy(src[w], land[w], k, peer, c, send_sems, recv_sems, w, src_slot, me).start()
        token[...] = jnp.zeros_like(token)

    hbm = [pltpu.with_memory_space_constraint(a, pltpu.HBM) for a in arrays]
    out = pl.pallas_call(
        body, name=name,
        out_shape=(pltpu.SemaphoreType.DMA((3 * n,)), pltpu.SemaphoreType.DMA((3 * n,)),
                   *[pltpu.HBM(a.shape, a.dtype) for a in hbm], jax.ShapeDtypeStruct((8, LANE), F32)),
        in_specs=[HBM_SPEC] * na + [ANY_SPEC],
        out_specs=(SEM_SPEC, SEM_SPEC, *[HBM_SPEC] * na, pl.BlockSpec(memory_space=pltpu.VMEM)),
        input_output_aliases={w: 2 + w for w in range(na)},
        compiler_params=pltpu.CompilerParams(has_side_effects=SPLIT_EFFECT))(*hbm, after)
    return dict(send=out[0], recv=out[1], arrays=out[2:2 + na], n=n, token=out[-1], exchange=exchange)


def _spread_wait(handle, after, name):
    n, exchange = handle["n"], handle["exchange"]
    arrays = list(handle["arrays"])
    na = len(arrays)

    def body(*refs):
        src, land = refs[:n], refs[na - n:na]
        send_sems, recv_sems = refs[na], refs[na + 1]
        x, y, c, peers = _chip_peers()
        me = 2 * x + y
        for w in range(n):
            for k, peer in enumerate(peers):
                there = 2 * peer[0] + peer[1]
                cp = _spread_copy(src[w], land[w], k, peer, c, send_sems, recv_sems, w, there if exchange else me, there)
                cp.wait_send()
                cp.wait_recv()

    out = pl.pallas_call(
        body, name=name, out_shape=tuple(pltpu.HBM(a.shape, a.dtype) for a in arrays),
        in_specs=[HBM_SPEC] * na + [SEM_SPEC, SEM_SPEC, ANY_SPEC], out_specs=tuple([HBM_SPEC] * na),
        input_output_aliases={w: w for w in range(na)},
        compiler_params=pltpu.CompilerParams(has_side_effects=SPLIT_EFFECT))(*arrays, handle["send"], handle["recv"], after)
    return (list(out[n:]), list(out[:n])) if exchange else list(out)


def _sibling_copy(received, sent, land, k, me, peers, sibling, send_sems, recv_sems, index):
    slot = me if k == 3 else 2 * peers[k][0] + peers[k][1]
    src = sent if k == 3 else received
    return pltpu.make_async_remote_copy(
        src_ref=src.at[slot], dst_ref=land.at[slot], send_sem=send_sems.at[4 * index + k],
        recv_sem=recv_sems.at[4 * index + k], device_id=sibling, device_id_type=MESH)


def _sibling_start(received, sent, after, name):
    n = len(received)
    lands = [lax.empty(a.shape, a.dtype) for a in received]
    arrays = list(received) + list(sent) + lands

    def body(*refs):
        rec, snt, land = refs[:n], refs[n:2 * n], refs[2 * n:3 * n]
        send_sems, recv_sems = refs[3 * n + 1], refs[3 * n + 2]
        token = refs[-1]
        x, y, c, peers = _chip_peers()
        for w in range(n):
            for k in range(4):
                _sibling_copy(rec[w], snt[w], land[w], k, 2 * x + y, peers, (x, y, 1 - c), send_sems, recv_sems, w).start()
        token[...] = jnp.zeros_like(token)

    hbm = [pltpu.with_memory_space_constraint(a, pltpu.HBM) for a in arrays]
    out = pl.pallas_call(
        body, name=name,
        out_shape=(pltpu.SemaphoreType.DMA((4 * n,)), pltpu.SemaphoreType.DMA((4 * n,)),
                   *[pltpu.HBM(a.shape, a.dtype) for a in hbm], jax.ShapeDtypeStruct((8, LANE), F32)),
        in_specs=[HBM_SPEC] * (3 * n) + [ANY_SPEC],
        out_specs=(SEM_SPEC, SEM_SPEC, *[HBM_SPEC] * (3 * n), pl.BlockSpec(memory_space=pltpu.VMEM)),
        input_output_aliases={w: 2 + w for w in range(3 * n)},
        compiler_params=pltpu.CompilerParams(has_side_effects=SPLIT_EFFECT))(*hbm, after)
    return dict(send=out[0], recv=out[1], arrays=out[2:2 + 3 * n], n=n, token=out[-1])


def _sibling_wait(handle, after, name):
    n = handle["n"]
    arrays = list(handle["arrays"])

    def body(*refs):
        rec, snt, land = refs[:n], refs[n:2 * n], refs[2 * n:3 * n]
        send_sems, recv_sems = refs[3 * n], refs[3 * n + 1]
        x, y, c, peers = _chip_peers()
        for w in range(n):
            for k in range(4):
                cp = _sibling_copy(rec[w], snt[w], land[w], k, 2 * x + y, peers, (x, y, 1 - c), send_sems, recv_sems, w)
                cp.wait_send()
                cp.wait_recv()

    out = pl.pallas_call(
        body, name=name, out_shape=tuple(pltpu.HBM(a.shape, a.dtype) for a in arrays),
        in_specs=[HBM_SPEC] * (3 * n) + [SEM_SPEC, SEM_SPEC, ANY_SPEC], out_specs=tuple([HBM_SPEC] * (3 * n)),
        input_output_aliases={w: w for w in range(3 * n)},
        compiler_params=pltpu.CompilerParams(has_side_effects=SPLIT_EFFECT))(*arrays, handle["send"], handle["recv"], after)
    return list(out[:n]), list(out[n:2 * n]), list(out[2 * n:])


def _all_reduce_small(v):
    rows = v.shape[0]

    def body(v_ref, sum_ref, slots, send_sems, recv_sems):
        x, y, c = lax.axis_index("x"), lax.axis_index("y"), lax.axis_index("c")
        me = 4 * x + 2 * y + c
        slots[me] = v_ref[...]
        sends = []
        for k in range(1, N_DEV):
            bx, by, bc = (k >> 2) & 1, (k >> 1) & 1, k & 1
            peer = (x ^ bx, y ^ by, c ^ bc)
            rc = pltpu.make_async_remote_copy(src_ref=v_ref, dst_ref=slots.at[me], send_sem=send_sems.at[k],
                                              recv_sem=recv_sems.at[k], device_id=peer, device_id_type=MESH)
            rc.start()
            sends.append(rc)
        for k in range(1, N_DEV):
            bx, by, bc = (k >> 2) & 1, (k >> 1) & 1, k & 1
            src = 4 * (x ^ bx) + 2 * (y ^ by) + (c ^ bc)
            pltpu.make_async_remote_copy(src_ref=v_ref, dst_ref=slots.at[src], send_sem=send_sems.at[k],
                                         recv_sem=recv_sems.at[k], device_id=(x ^ bx, y ^ by, c ^ bc),
                                         device_id_type=MESH).wait_recv()
        for rc in sends:
            rc.wait_send()
        total = slots[0]
        for k in range(1, N_DEV):
            total = total + slots[k]
        sum_ref[...] = total

    vm = pl.BlockSpec(memory_space=pltpu.VMEM)
    return pl.pallas_call(
        body, out_shape=jax.ShapeDtypeStruct((rows, LANE), F32), in_specs=[vm], out_specs=vm,
        scratch_shapes=[pltpu.VMEM((N_DEV, rows, LANE), F32), pltpu.SemaphoreType.DMA((N_DEV,)),
                        pltpu.SemaphoreType.DMA((N_DEV,))], name="all_reduce_small")(v)


def _as_2d(a):
    return a.reshape(-1, a.shape[-1])


def _row_tile(rows, cols):
    for t in (512, 256, 128, 64, 32, 16):
        if rows % t == 0 and t * cols * 4 <= (1 << 20):
            return t
    return rows


def _adamw_weight(w, m, v, received, sent, sibling):
    layers = len(received)
    _, rows, cols = received[0].shape
    tr = _row_tile(rows, cols)
    by_columns = rows % tr != 0 or tr == rows and rows * cols * 4 > (2 << 20)
    if by_columns:
        assert layers == 1 and cols % (2 * LANE) == 0, (w.shape, received[0].shape)
        tr, tc, steps = rows, cols // 2, 2
        index = lambda i: (0, i)
    else:
        tc, steps = cols, rows // tr
        index = lambda i: (i, 0)
    where = (2 * lax.axis_index("x") + lax.axis_index("y")).astype(jnp.int32).reshape(1)

    def body(where_ref, w_ref, m_ref, v_ref, *rest):
        per_layer, (g_ref, d_ref, nm_ref, nv_ref) = rest[:3 * layers], rest[3 * layers:]
        me = where_ref[0]
        for layer in range(layers):
            r_ref, own_ref, s_ref = per_layer[3 * layer:3 * layer + 3]

            @pl.when(pl.program_id(0) == layer)
            def _():
                mine = theirs = None
                for k in range(N_CHIPS):
                    a = jnp.where(me == k, own_ref[...], r_ref[k]).astype(F32)
                    b = s_ref[k].astype(F32)
                    mine = a if mine is None else mine + a
                    theirs = b if theirs is None else theirs + b
                g = mine + theirs
                delta, nm, nv = _adamw_math(w_ref[...], g, m_ref[...], v_ref[...])
                g_ref[...] = g
                d_ref[...] = delta
                nm_ref[...] = nm
                nv_ref[...] = nv

    def held(layer, now, i):
        return jnp.where(now < layer, 0, jnp.where(now > layer, steps - 1, i))

    if by_columns:
        stacked = pl.BlockSpec((tr, tc), lambda now, i, where_ref: index(i))
    else:
        stacked = pl.BlockSpec((tr, tc), lambda now, i, where_ref: (now * steps + i, 0))
    in_specs = [stacked, stacked, stacked]
    args = [where, w, m, v]
    for layer in range(layers):
        four = pl.BlockSpec((N_CHIPS, tr, tc), lambda now, i, where_ref, layer=layer: (0,) + index(held(layer, now, i)))
        own = pl.BlockSpec((None, tr, tc),
                           lambda now, i, where_ref, layer=layer: (where_ref[0],) + index(held(layer, now, i)))
        in_specs += [four, own, four]
        args += [received[layer], sent[layer], sibling[layer]]
    grid_spec = pltpu.PrefetchScalarGridSpec(num_scalar_prefetch=1, grid=(layers, steps), in_specs=in_specs,
                                             out_specs=[stacked] * 4)
    return pl.pallas_call(body, out_shape=[jax.ShapeDtypeStruct(w.shape, F32)] * 4, grid_spec=grid_spec,
                          name="adamw_weight", compiler_params=_params(("arbitrary", "arbitrary")))(*args)


def _adamw_math(w, g, m, v):
    m = ADAM_B1 * m + (1.0 - ADAM_B1) * g
    v = ADAM_B2 * v + (1.0 - ADAM_B2) * (g * g)
    m_hat = m * (1.0 / (1.0 - ADAM_B1 ** ADAM_STEP))
    v_hat = v * (1.0 / (1.0 - ADAM_B2 ** ADAM_STEP))
    denom = jnp.sqrt(v_hat) + ADAM_EPS
    inv = pl.reciprocal(denom, approx=True)
    inv = inv * (2.0 - denom * inv)
    delta = -ADAM_LR * (m_hat * inv + ADAM_WD * w)
    return delta, m, v


def _adamw(w, m, v, g_mine, g_sibling):
    rows, cols = w.shape
    tr = _row_tile(rows, cols)
    two = g_sibling is not None

    def body(*refs):
        if two:
            w_ref, m_ref, v_ref, ga_ref, gb_ref, g_ref, d_ref, nm_ref, nv_ref = refs
            g = ga_ref[...] + gb_ref[...]
        else:
            w_ref, m_ref, v_ref, ga_ref, g_ref, d_ref, nm_ref, nv_ref = refs
            g = ga_ref[...]
        delta, nm, nv = _adamw_math(w_ref[...], g, m_ref[...], v_ref[...])
        g_ref[...] = g
        d_ref[...] = delta
        nm_ref[...] = nm
        nv_ref[...] = nv

    blk = pl.BlockSpec((tr, cols), lambda i: (i, 0))
    args = [w, m, v, g_mine] + ([g_sibling] if two else [])
    return pl.pallas_call(body, out_shape=[jax.ShapeDtypeStruct((rows, cols), F32)] * 4, grid=(rows // tr,),
                          in_specs=[blk] * len(args), out_specs=[blk] * 4, name="adamw",
                          compiler_params=_params(("parallel",)))(*args)


def _pack_rows(arrays):
    flat = jnp.concatenate([a.reshape(-1) for a in arrays])
    rows = -(-flat.shape[0] // (8 * LANE)) * 8
    return jnp.pad(flat, (0, rows * LANE - flat.shape[0])).reshape(rows, LANE)


def _unpack_rows(packed, shapes):
    flat = packed.reshape(-1)
    out, at = [], 0
    for s in shapes:
        size = math.prod(s)
        out.append(flat[at:at + size].reshape(s))
        at += size
    return out


def kernel(x, p, positions, norm_g, ffn_w_in, ffn_w_out, ple_w_proj, ple_w_gate, rel_bias, mla_w_a, mla_q_norm, mla_kv_norm, mla_w_uq, mla_w_ukv, mla_w_o, dil_w_qkv, dil_w_o, fox_w_qkvf, fox_b_f, fox_w_o, loss_target, m_norm_g, m_ffn_w_in, m_ffn_w_out, m_ple_w_proj, m_ple_w_gate, m_rel_bias, m_mla_w_a, m_mla_q_norm, m_mla_kv_norm, m_mla_w_uq, m_mla_w_ukv, m_mla_w_o, m_dil_w_qkv, m_dil_w_o, m_fox_w_qkvf, m_fox_b_f, m_fox_w_o, v_norm_g, v_ffn_w_in, v_ffn_w_out, v_ple_w_proj, v_ple_w_gate, v_rel_bias, v_mla_w_a, v_mla_q_norm, v_mla_kv_norm, v_mla_w_uq, v_mla_w_ukv, v_mla_w_o, v_dil_w_qkv, v_dil_w_o, v_fox_w_qkvf, v_fox_b_f, v_fox_w_o):
    w = dict(norm_g=norm_g, ffn_w_in=ffn_w_in, ffn_w_out=ffn_w_out, ple_w_proj=ple_w_proj, ple_w_gate=ple_w_gate,
             rel_bias=rel_bias, mla_w_a=mla_w_a, mla_q_norm=mla_q_norm, mla_kv_norm=mla_kv_norm, mla_w_uq=mla_w_uq,
             mla_w_ukv=mla_w_ukv, mla_w_o=mla_w_o, dil_w_qkv=dil_w_qkv, dil_w_o=dil_w_o, fox_w_qkvf=fox_w_qkvf,
             fox_b_f=fox_b_f, fox_w_o=fox_w_o)
    m = dict(norm_g=m_norm_g, ffn_w_in=m_ffn_w_in, ffn_w_out=m_ffn_w_out, ple_w_proj=m_ple_w_proj,
             ple_w_gate=m_ple_w_gate, rel_bias=m_rel_bias, mla_w_a=m_mla_w_a, mla_q_norm=m_mla_q_norm,
             mla_kv_norm=m_mla_kv_norm, mla_w_uq=m_mla_w_uq, mla_w_ukv=m_mla_w_ukv, mla_w_o=m_mla_w_o,
             dil_w_qkv=m_dil_w_qkv, dil_w_o=m_dil_w_o, fox_w_qkvf=m_fox_w_qkvf, fox_b_f=m_fox_b_f, fox_w_o=m_fox_w_o)
    v = dict(norm_g=v_norm_g, ffn_w_in=v_ffn_w_in, ffn_w_out=v_ffn_w_out, ple_w_proj=v_ple_w_proj,
             ple_w_gate=v_ple_w_gate, rel_bias=v_rel_bias, mla_w_a=v_mla_w_a, mla_q_norm=v_mla_q_norm,
             mla_kv_norm=v_mla_kv_norm, mla_w_uq=v_mla_w_uq, mla_w_ukv=v_mla_w_ukv, mla_w_o=v_mla_w_o,
             dil_w_qkv=v_dil_w_qkv, dil_w_o=v_dil_w_o, fox_w_qkvf=v_fox_w_qkvf, fox_b_f=v_fox_b_f, fox_w_o=v_fox_w_o)
    chip = 2 * lax.axis_index("x") + lax.axis_index("y")
    for tree in (w, m, v):
        tree[TRANSPOSED] = jnp.swapaxes(tree[TRANSPOSED], 1, 2)

    small_shapes = [w[k].shape for k in SMALL_SHARDED]
    order = [(i, part) for i in range(DEPTH) for part in (MIXER_PART, COMMON_PART) if _part_names(i, part)]
    gathers = {}
    after = positions
    zero = 0.0
    for i, part in order:
        bufs = [_own_slot((w[k][_layer_slot(k, i)] + zero).astype(BF)) for k in _part_names(i, part)]
        if (i, part) == order[0]:
            bufs.append(_own_slot(_pack_rows([w[k] for k in SMALL_SHARDED])))
        gathers[i, part] = _spread_start(bufs, None, after, f"gather_start_{i}_{part}")
        after = gathers[i, part]["token"]
        if (i, part) == order[0]:
            zero = after[0, 0]
    all_started = after
    state = {}

    def get_part(i, part, after_array):
        is_first = (i, part) == order[0]
        lands = _spread_wait(gathers[i, part], all_started if is_first else after_array, f"gather_wait_{i}_{part}")
        if is_first:
            pieces = [_unpack_rows(lands[-1][k], small_shapes) for k in range(N_CHIPS)]
            small = {name: jnp.concatenate([pieces[k][idx] for k in range(N_CHIPS)], axis=-1)
                     for idx, name in enumerate(SMALL_SHARDED)}
            state["small"] = dict(small, rel_bias=rel_bias, fox_b_f=fox_b_f)
        chunks = dict(zip(_part_names(i, part), lands))
        state[i, part] = {k: a.shape for k, a in chunks.items()}
        return _part_to_compute(i, part, chunks)

    started, forwards = [], {}

    def forward_oldest(after_array):
        i, part, handle = started.pop(0)
        received, sent = _spread_wait(handle, after_array, f"exchange_wait_{i}_{part}")
        forwards[i, part] = _sibling_start(received, sent, after_array, f"sibling_start_{i}_{part}")
        return forwards[i, part]["token"]

    def put_part(i, part, lg):
        contrib = _part_contributions(i, part, lg, state[i, part])
        srcs = [contrib[k] for k in _part_names(i, part)]
        handle = _spread_start([lax.empty(s.shape, s.dtype) for s in srcs], srcs, positions,
                               f"exchange_start_{i}_{part}")
        token = handle["token"]
        if started:
            token = token + forward_oldest(token)
        started.append((i, part, handle))
        return token

    sq, grad_x, sg = _run_layers(x[0], p[:, 0], positions[0], loss_target[0], get_part, lambda: state["small"],
                                 put_part)
    loss = lax.psum(0.5 / D_MODEL * jnp.sum(sq), ("x", "y", "c"))
    forward_oldest(grad_x)

    held = {k: {} for k in BIG}
    for i, part in sorted(forwards, reverse=True):
        received, sent, sibling = _sibling_wait(forwards[i, part], grad_x, f"sibling_wait_{i}_{part}")
        for k, r, s, t in zip(_part_names(i, part), received, sent, sibling):
            held[k][_layer_slot(k, i)] = (r, s, t)
    results = {}
    for k in BIG:
        per_layer = [held[k][slot] for slot in sorted(held[k])]
        outs = _adamw_weight(_as_2d(w[k]), _as_2d(m[k]), _as_2d(v[k]), *[list(col) for col in zip(*per_layer)])
        results[k] = [o.reshape(w[k].shape) for o in outs]
    results[TRANSPOSED] = [jnp.swapaxes(o, 1, 2) for o in results[TRANSPOSED]]

    small_all = SMALL_SHARDED + SMALL_REPLICATED
    full_shapes = [sg[k].shape for k in small_all]
    reduced = dict(zip(small_all, _unpack_rows(_all_reduce_small(_pack_rows([sg[k] for k in small_all])), full_shapes)))
    local_g = []
    for k in small_all:
        g = reduced[k]
        if k in SMALL_SHARDED:
            width = w[k].shape[-1]
            g = lax.dynamic_slice_in_dim(g, chip * width, width, axis=g.ndim - 1)
        local_g.append(g)
    local_shapes = [w[k].shape for k in small_all]
    outs = _adamw(_pack_rows([w[k] for k in small_all]), _pack_rows([m[k] for k in small_all]),
                  _pack_rows([v[k] for k in small_all]), _pack_rows(local_g), None)
    unpacked = [_unpack_rows(o, local_shapes) for o in outs]
    for idx, k in enumerate(small_all):
        results[k] = [u[idx] for u in unpacked]

    return (loss, grad_x[None], *[results[k][0] for k in WEIGHTS], *[results[k][1] for k in WEIGHTS],
            *[results[k][2] for k in WEIGHTS], *[results[k][3] for k in WEIGHTS])
```
